```python
import math
import jax
import jax.numpy as jnp
from jax import lax
import numpy as np

D_MODEL = 1024
BATCH = 8
SEQ = 8192
DEPTH = 2

SSM_HEADS = 16
SSM_HEAD_DIM = 64
SSM_INNER = SSM_HEADS * SSM_HEAD_DIM
SSM_GROUPS = 2
SSM_STATE = 128
SSM_CONV = 4
SSM_CHUNK = 128
SSM_CONV_DIM = SSM_INNER + 2 * SSM_GROUPS * SSM_STATE

GMLP_GROUPS = 8
GMLP_GROUP_DIM = 128
GMLP_INNER = GMLP_GROUPS * GMLP_GROUP_DIM
GMLP_CHUNK = 128

MIX_WIDTH = SSM_INNER + GMLP_INNER
IN_EVEN = SSM_INNER + SSM_CONV_DIM + SSM_HEADS + 2 * GMLP_INNER

ATTN_HEADS = 16
ATTN_KV_HEADS = 2
ATTN_HEAD_DIM = 64
ATTN_Q_PER_KV = ATTN_HEADS // ATTN_KV_HEADS
WINDOW = 128
ATTN_BLOCK = 128
QKV_DIM = (ATTN_HEADS + 2 * ATTN_KV_HEADS) * ATTN_HEAD_DIM
REL_BUCKETS = 32
REL_MAX_DIST = 128

FFN_HIDDEN = -(-8 * D_MODEL // (3 * 256)) * 256

N_EVEN = (DEPTH + 1) // 2
N_ODD = DEPTH // 2
EPS = 1e-6
NEG_INF = -1e30

kernel_name = "hybrid_ssd_gmlp_swa_adaln_trunk"


def rms_norm(x, w):
    xf = x.astype(jnp.float32)
    y = xf * lax.rsqrt(jnp.mean(xf * xf, axis=-1, keepdims=True) + EPS)
    return (y * w.astype(jnp.float32)).astype(x.dtype)


def layer_norm(x, w, b):
    xf = x.astype(jnp.float32)
    mu = jnp.mean(xf, axis=-1, keepdims=True)
    var = jnp.mean(jnp.square(xf - mu), axis=-1, keepdims=True)
    y = (xf - mu) * lax.rsqrt(var + EPS)
    return (y * w.astype(jnp.float32) + b.astype(jnp.float32)).astype(x.dtype)


def modulate(h, shift, scale):
    return h * (1 + scale[:, None, :]) + shift[:, None, :]


def causal_dwconv(x, w, b):
    out = lax.conv_general_dilated(
        x, w[:, None, :].astype(x.dtype), window_strides=(1,),
        padding=[(SSM_CONV - 1, 0)], dimension_numbers=('NWC', 'WIO', 'NWC'),
        feature_group_count=x.shape[-1])
    return out + b


def ssd_chunked(x, dt, a, bmat, cmat):
    f32 = jnp.float32
    bsz, seq, nh, hd = x.shape
    ng, ns = bmat.shape[-2:]
    hpg = nh // ng
    nc = seq // SSM_CHUNK
    L = SSM_CHUNK
    xd = (x.astype(f32) * dt[..., None]).reshape(bsz, nc, L, ng, hpg, hd)
    la = jnp.moveaxis((dt * a).reshape(bsz, nc, L, ng, hpg), 2, -1)
    la_cum = jnp.cumsum(la, axis=-1)
    bc = bmat.astype(f32).reshape(bsz, nc, L, ng, ns)
    cc = cmat.astype(f32).reshape(bsz, nc, L, ng, ns)
    causal = jnp.tril(jnp.ones((L, L), dtype=bool))
    seg = la_cum[..., :, None] - la_cum[..., None, :]
    decay = jnp.where(causal, jnp.exp(jnp.where(causal, seg, 0.0)), 0.0)
    cb = jnp.einsum('bclgn,bcsgn->bcgls', cc, bc)
    y_diag = jnp.einsum('bcgjls,bcsgjp->bclgjp', cb[:, :, :, None] * decay, xd)
    decay_to_end = jnp.exp(la_cum[..., -1:] - la_cum)
    states = jnp.einsum('bcsgn,bcgjs,bcsgjp->bcgjpn', bc, decay_to_end, xd)
    chunk_decay = jnp.exp(la_cum[..., -1])

    def step(carry, inp):
        st, dec = inp
        return carry * dec[..., None, None] + st, carry

    init = jnp.zeros((bsz, ng, hpg, hd, ns), f32)
    _, prev = lax.scan(step, init, (jnp.moveaxis(states, 1, 0), jnp.moveaxis(chunk_decay, 1, 0)))
    prev = jnp.moveaxis(prev, 0, 1)
    y_off = jnp.einsum('bclgn,bcgjpn,bcgjl->bclgjp', cc, prev, jnp.exp(la_cum))
    return (y_diag + y_off).reshape(bsz, seq, nh, hd)


def ssd_branch(z, xbc, dt_raw, conv_w, conv_b, dt_bias, a_log, d_skip, norm_w):
    f32 = jnp.float32
    bsz, seq, _ = z.shape
    xbc = jax.nn.silu(causal_dwconv(xbc, conv_w, conv_b))
    xs, bm, cm = jnp.split(xbc, [SSM_INNER, SSM_INNER + SSM_GROUPS * SSM_STATE], axis=-1)
    xs = xs.reshape(bsz, seq, SSM_HEADS, SSM_HEAD_DIM)
    bm = bm.reshape(bsz, seq, SSM_GROUPS, SSM_STATE)
    cm = cm.reshape(bsz, seq, SSM_GROUPS, SSM_STATE)
    dt = jax.nn.softplus(dt_raw.astype(f32) + dt_bias.astype(f32))
    a = -jnp.exp(a_log.astype(f32))
    y = ssd_chunked(xs, dt, a, bm, cm) + d_skip.astype(f32)[:, None] * xs.astype(f32)
    y = y.reshape(bsz, seq, SSM_INNER) * jax.nn.silu(z.astype(f32))
    y = y.reshape(bsz, seq, SSM_GROUPS, SSM_INNER // SSM_GROUPS)
    y = y * lax.rsqrt(jnp.mean(y * y, axis=-1, keepdims=True) + EPS)
    return (y.reshape(bsz, seq, SSM_INNER) * norm_w.astype(f32)).astype(z.dtype)


def spatial_gating_branch(u, v, ln_w, ln_b, w_s, b_s):
    bsz, seq, _ = u.shape
    nc = seq // GMLP_CHUNK
    u = jax.nn.gelu(u, approximate=False)
    v = layer_norm(jax.nn.gelu(v, approximate=False), ln_w, ln_b)
    v = v.reshape(bsz, nc, GMLP_CHUNK, GMLP_GROUPS, GMLP_GROUP_DIM)
    w = w_s * jnp.tril(jnp.ones((GMLP_CHUNK, GMLP_CHUNK), w_s.dtype))
    sv = jnp.einsum('gts,bcsgd->bctgd', w, v) + b_s.T[None, None, :, :, None]
    return u * sv.reshape(bsz, seq, GMLP_INNER)


def even_mixer(h, in_w, conv_w, conv_b, dt_bias, a_log, d_skip, ssm_norm_w,
               ln_w, ln_b, w_s, b_s, out_w):
    proj = h @ in_w
    o1 = SSM_INNER
    o2 = o1 + SSM_CONV_DIM
    o3 = o2 + SSM_HEADS
    o4 = o3 + GMLP_INNER
    z, xbc, dt_raw, u, v = jnp.split(proj, [o1, o2, o3, o4], axis=-1)
    ya = ssd_branch(z, xbc, dt_raw, conv_w, conv_b, dt_bias, a_log, d_skip, ssm_norm_w)
    yb = spatial_gating_branch(u, v, ln_w, ln_b, w_s, b_s)
    return jnp.concatenate([ya, yb], axis=-1) @ out_w


def t5_relative_bias(table):
    qi = jnp.arange(ATTN_BLOCK)[:, None]
    sj = jnp.arange(2 * ATTN_BLOCK)[None, :]
    dist = jnp.maximum(qi + ATTN_BLOCK - sj, 0)
    max_exact = REL_BUCKETS // 2
    log_ratio = (jnp.log(jnp.maximum(dist, 1).astype(jnp.float32) / max_exact)
                 / math.log(REL_MAX_DIST / max_exact))
    large = max_exact + (log_ratio * (REL_BUCKETS - max_exact)).astype(jnp.int32)
    bucket = jnp.where(dist < max_exact, dist, jnp.minimum(large, REL_BUCKETS - 1))
    bias = table[bucket]
    return jnp.transpose(bias, (2, 0, 1)).reshape(
        ATTN_KV_HEADS, ATTN_Q_PER_KV, ATTN_BLOCK, 2 * ATTN_BLOCK)


def window_attention(h, w_qkv, b_qkv, w_o, b_o, sinks, rel_table):
    f32 = jnp.float32
    bsz, seq, _ = h.shape
    nb = seq // ATTN_BLOCK
    qkv = h @ w_qkv + b_qkv
    q, k, v = jnp.split(qkv, [ATTN_HEADS * ATTN_HEAD_DIM,
                              (ATTN_HEADS + ATTN_KV_HEADS) * ATTN_HEAD_DIM], axis=-1)
    q = q.reshape(bsz, nb, ATTN_BLOCK, ATTN_KV_HEADS, ATTN_Q_PER_KV, ATTN_HEAD_DIM)

    def band(t):
        t = t.reshape(bsz, nb, ATTN_BLOCK, ATTN_KV_HEADS, ATTN_HEAD_DIM)
        prev = jnp.pad(t, ((0, 0), (1, 0), (0, 0), (0, 0), (0, 0)))[:, :-1]
        return jnp.concatenate([prev, t], axis=2)

    kb, vb = band(k), band(v)
    scale = ATTN_HEAD_DIM ** -0.5
    logits = jnp.einsum('bnqkgd,bnskd->bnkgqs', q.astype(f32), kb.astype(f32)) * scale
    logits = logits + t5_relative_bias(rel_table).astype(f32)
    qi = jnp.arange(ATTN_BLOCK)[:, None]
    sj = jnp.arange(2 * ATTN_BLOCK)[None, :]
    rel = qi + ATTN_BLOCK - sj
    in_window = (rel >= 0) & (rel < WINDOW)
    key_pos = jnp.arange(nb)[:, None, None] * ATTN_BLOCK - ATTN_BLOCK + sj[None]
    mask = in_window[None] & (key_pos >= 0)
    logits = jnp.where(mask[None, :, None, None], logits, NEG_INF)
    sink = jnp.broadcast_to(
        sinks.astype(f32).reshape(ATTN_KV_HEADS, ATTN_Q_PER_KV)[None, None, :, :, None, None],
        logits.shape[:-1] + (1,))
    probs = jax.nn.softmax(jnp.concatenate([logits, sink], axis=-1), axis=-1)[..., :-1]
    out = jnp.einsum('bnkgqs,bnskd->bnqkgd', probs.astype(vb.dtype), vb)
    return out.reshape(bsz, seq, ATTN_HEADS * ATTN_HEAD_DIM) @ w_o + b_o


def swiglu(h, w_gate, w_up, w_down):
    return (jax.nn.silu(h @ w_gate) * (h @ w_up)) @ w_down


def _fwd_setup_inputs(seed: int = 0) -> dict:
    key = jax.random.key(seed)
    ks = iter(jax.random.split(key, 40))
    nrm = lambda shape, s: jax.random.normal(next(ks), shape, jnp.float32) * s
    ones_n = lambda shape: 1.0 + nrm(shape, 0.02)
    D = D_MODEL
    dt0 = jnp.exp(jax.random.uniform(next(ks), (N_EVEN, SSM_HEADS), jnp.float32,
                                     math.log(1e-3), math.log(1e-1)))
    return {
        "x": nrm((BATCH, SEQ, D), 1.0),
        "c": nrm((BATCH, D), 1.0),
        "ada_w": nrm((DEPTH, D, 6 * D), 0.5 * D ** -0.5),
        "ada_b": nrm((DEPTH, 6 * D), 0.02),
        "norm_mix_w": ones_n((DEPTH, D)),
        "norm_ffn_w": ones_n((DEPTH, D)),
        "in_w_even": nrm((N_EVEN, D, IN_EVEN), D ** -0.5),
        "conv_w": nrm((N_EVEN, SSM_CONV, SSM_CONV_DIM), SSM_CONV ** -0.5),
        "conv_b": nrm((N_EVEN, SSM_CONV_DIM), 0.02),
        "dt_bias": dt0 + jnp.log(-jnp.expm1(-dt0)),
        "a_log": jnp.log(jax.random.uniform(next(ks), (N_EVEN, SSM_HEADS), jnp.float32, 1.0, 16.0)),
        "d_skip": 1.0 + nrm((N_EVEN, SSM_HEADS), 0.1),
        "ssm_norm_w": ones_n((N_EVEN, SSM_INNER)),
        "gmlp_ln_w": ones_n((N_EVEN, GMLP_INNER)),
        "gmlp_ln_b": nrm((N_EVEN, GMLP_INNER), 0.02),
        "gmlp_ws": nrm((N_EVEN, GMLP_GROUPS, GMLP_CHUNK, GMLP_CHUNK), GMLP_CHUNK ** -0.5),
        "gmlp_bs": 1.0 + nrm((N_EVEN, GMLP_GROUPS, GMLP_CHUNK), 0.02),
        "out_w_even": nrm((N_EVEN, MIX_WIDTH, D), MIX_WIDTH ** -0.5),
        "qkv_w": nrm((N_ODD, D, QKV_DIM), D ** -0.5),
        "qkv_b": nrm((N_ODD, QKV_DIM), 0.02),
        "o_w": nrm((N_ODD, ATTN_HEADS * ATTN_HEAD_DIM, D), (ATTN_HEADS * ATTN_HEAD_DIM) ** -0.5),
        "o_b": nrm((N_ODD, D), 0.02),
        "sinks": nrm((N_ODD, ATTN_HEADS), 1.0),
        "rel_table": nrm((REL_BUCKETS, ATTN_HEADS), 0.5),
        "ffn_gate_w": nrm((DEPTH, D, FFN_HIDDEN), D ** -0.5),
        "ffn_up_w": nrm((DEPTH, D, FFN_HIDDEN), D ** -0.5),
        "ffn_down_w": nrm((DEPTH, FFN_HIDDEN, D), FFN_HIDDEN ** -0.5),
        "final_norm_w": ones_n((D,)),
    }


def _fwd_reference(x, c, ada_w, ada_b, norm_mix_w, norm_ffn_w, in_w_even, conv_w, conv_b,
              dt_bias, a_log, d_skip, ssm_norm_w, gmlp_ln_w, gmlp_ln_b, gmlp_ws, gmlp_bs,
              out_w_even, qkv_w, qkv_b, o_w, o_b, sinks, rel_table,
              ffn_gate_w, ffn_up_w, ffn_down_w, final_norm_w):
    cond = jax.nn.silu(c)
    for layer in range(DEPTH):
        mod = cond @ ada_w[layer] + ada_b[layer]
        sh1, sc1, g1, sh2, sc2, g2 = jnp.split(mod, 6, axis=-1)
        h = modulate(rms_norm(x, norm_mix_w[layer]), sh1, sc1)
        i = layer // 2
        if layer % 2 == 0:
            mix = even_mixer(h, in_w_even[i], conv_w[i], conv_b[i], dt_bias[i], a_log[i],
                             d_skip[i], ssm_norm_w[i], gmlp_ln_w[i], gmlp_ln_b[i],
                             gmlp_ws[i], gmlp_bs[i], out_w_even[i])
        else:
            mix = window_attention(h, qkv_w[i], qkv_b[i], o_w[i], o_b[i], sinks[i], rel_table)
        x = x + g1[:, None, :] * mix
        h = modulate(rms_norm(x, norm_ffn_w[layer]), sh2, sc2)
        x = x + g2[:, None, :] * swiglu(h, ffn_gate_w[layer], ffn_up_w[layer], ffn_down_w[layer])
    return rms_norm(x, final_norm_w)


import jax as _jax
import jax.numpy as _jnp

TWIN_FORMAT = 'train_step'
FWD_PARAMS = ['x', 'c', 'ada_w', 'ada_b', 'norm_mix_w', 'norm_ffn_w', 'in_w_even', 'conv_w', 'conv_b', 'dt_bias', 'a_log', 'd_skip', 'ssm_norm_w', 'gmlp_ln_w', 'gmlp_ln_b', 'gmlp_ws', 'gmlp_bs', 'out_w_even', 'qkv_w', 'qkv_b', 'o_w', 'o_b', 'sinks', 'rel_table', 'ffn_gate_w', 'ffn_up_w', 'ffn_down_w', 'final_norm_w']
TWIN_WEIGHTS = ['ada_w', 'ada_b', 'norm_mix_w', 'norm_ffn_w', 'in_w_even', 'conv_w', 'conv_b', 'dt_bias', 'a_log', 'd_skip', 'ssm_norm_w', 'gmlp_ln_w', 'gmlp_ln_b', 'gmlp_ws', 'gmlp_bs', 'out_w_even', 'qkv_w', 'qkv_b', 'o_w', 'o_b', 'sinks', 'rel_table', 'ffn_gate_w', 'ffn_up_w', 'ffn_down_w', 'final_norm_w']
TWIN_DIFF_INPUT = 'x'
TWIN_INPUTS = ['x', 'c', 'ada_w', 'ada_b', 'norm_mix_w', 'norm_ffn_w', 'in_w_even', 'conv_w', 'conv_b', 'dt_bias', 'a_log', 'd_skip', 'ssm_norm_w', 'gmlp_ln_w', 'gmlp_ln_b', 'gmlp_ws', 'gmlp_bs', 'out_w_even', 'qkv_w', 'qkv_b', 'o_w', 'o_b', 'sinks', 'rel_table', 'ffn_gate_w', 'ffn_up_w', 'ffn_down_w', 'final_norm_w', 'loss_target', 'm_ada_w', 'm_ada_b', 'm_norm_mix_w', 'm_norm_ffn_w', 'm_in_w_even', 'm_conv_w', 'm_conv_b', 'm_dt_bias', 'm_a_log', 'm_d_skip', 'm_ssm_norm_w', 'm_gmlp_ln_w', 'm_gmlp_ln_b', 'm_gmlp_ws', 'm_gmlp_bs', 'm_out_w_even', 'm_qkv_w', 'm_qkv_b', 'm_o_w', 'm_o_b', 'm_sinks', 'm_rel_table', 'm_ffn_gate_w', 'm_ffn_up_w', 'm_ffn_down_w', 'm_final_norm_w', 'v_ada_w', 'v_ada_b', 'v_norm_mix_w', 'v_norm_ffn_w', 'v_in_w_even', 'v_conv_w', 'v_conv_b', 'v_dt_bias', 'v_a_log', 'v_d_skip', 'v_ssm_norm_w', 'v_gmlp_ln_w', 'v_gmlp_ln_b', 'v_gmlp_ws', 'v_gmlp_bs', 'v_out_w_even', 'v_qkv_w', 'v_qkv_b', 'v_o_w', 'v_o_b', 'v_sinks', 'v_rel_table', 'v_ffn_gate_w', 'v_ffn_up_w', 'v_ffn_down_w', 'v_final_norm_w']
TWIN_OUTPUTS = ['loss', 'grad_x', 'grad_ada_w', 'grad_ada_b', 'grad_norm_mix_w', 'grad_norm_ffn_w', 'grad_in_w_even', 'grad_conv_w', 'grad_conv_b', 'grad_dt_bias', 'grad_a_log', 'grad_d_skip', 'grad_ssm_norm_w', 'grad_gmlp_ln_w', 'grad_gmlp_ln_b', 'grad_gmlp_ws', 'grad_gmlp_bs', 'grad_out_w_even', 'grad_qkv_w', 'grad_qkv_b', 'grad_o_w', 'grad_o_b', 'grad_sinks', 'grad_rel_table', 'grad_ffn_gate_w', 'grad_ffn_up_w', 'grad_ffn_down_w', 'grad_final_norm_w', 'delta_ada_w', 'delta_ada_b', 'delta_norm_mix_w', 'delta_norm_ffn_w', 'delta_in_w_even', 'delta_conv_w', 'delta_conv_b', 'delta_dt_bias', 'delta_a_log', 'delta_d_skip', 'delta_ssm_norm_w', 'delta_gmlp_ln_w', 'delta_gmlp_ln_b', 'delta_gmlp_ws', 'delta_gmlp_bs', 'delta_out_w_even', 'delta_qkv_w', 'delta_qkv_b', 'delta_o_w', 'delta_o_b', 'delta_sinks', 'delta_rel_table', 'delta_ffn_gate_w', 'delta_ffn_up_w', 'delta_ffn_down_w', 'delta_final_norm_w', 'new_m_ada_w', 'new_m_ada_b', 'new_m_norm_mix_w', 'new_m_norm_ffn_w', 'new_m_in_w_even', 'new_m_conv_w', 'new_m_conv_b', 'new_m_dt_bias', 'new_m_a_log', 'new_m_d_skip', 'new_m_ssm_norm_w', 'new_m_gmlp_ln_w', 'new_m_gmlp_ln_b', 'new_m_gmlp_ws', 'new_m_gmlp_bs', 'new_m_out_w_even', 'new_m_qkv_w', 'new_m_qkv_b', 'new_m_o_w', 'new_m_o_b', 'new_m_sinks', 'new_m_rel_table', 'new_m_ffn_gate_w', 'new_m_ffn_up_w', 'new_m_ffn_down_w', 'new_m_final_norm_w', 'new_v_ada_w', 'new_v_ada_b', 'new_v_norm_mix_w', 'new_v_norm_ffn_w', 'new_v_in_w_even', 'new_v_conv_w', 'new_v_conv_b', 'new_v_dt_bias', 'new_v_a_log', 'new_v_d_skip', 'new_v_ssm_norm_w', 'new_v_gmlp_ln_w', 'new_v_gmlp_ln_b', 'new_v_gmlp_ws', 'new_v_gmlp_bs', 'new_v_out_w_even', 'new_v_qkv_w', 'new_v_qkv_b', 'new_v_o_w', 'new_v_o_b', 'new_v_sinks', 'new_v_rel_table', 'new_v_ffn_gate_w', 'new_v_ffn_up_w', 'new_v_ffn_down_w', 'new_v_final_norm_w']
TWIN_LEAF_KINDS = {'loss': 'loss', 'grad_x': 'grad_x', 'grad_ada_w': 'grad_w', 'grad_ada_b': 'grad_w', 'grad_norm_mix_w': 'grad_w', 'grad_norm_ffn_w': 'grad_w', 'grad_in_w_even': 'grad_w', 'grad_conv_w': 'grad_w', 'grad_conv_b': 'grad_w', 'grad_dt_bias': 'grad_w', 'grad_a_log': 'grad_w', 'grad_d_skip': 'grad_w', 'grad_ssm_norm_w': 'grad_w', 'grad_gmlp_ln_w': 'grad_w', 'grad_gmlp_ln_b': 'grad_w', 'grad_gmlp_ws': 'grad_w', 'grad_gmlp_bs': 'grad_w', 'grad_out_w_even': 'grad_w', 'grad_qkv_w': 'grad_w', 'grad_qkv_b': 'grad_w', 'grad_o_w': 'grad_w', 'grad_o_b': 'grad_w', 'grad_sinks': 'grad_w', 'grad_rel_table': 'grad_w', 'grad_ffn_gate_w': 'grad_w', 'grad_ffn_up_w': 'grad_w', 'grad_ffn_down_w': 'grad_w', 'grad_final_norm_w': 'grad_w', 'delta_ada_w': 'delta_w', 'delta_ada_b': 'delta_w', 'delta_norm_mix_w': 'delta_w', 'delta_norm_ffn_w': 'delta_w', 'delta_in_w_even': 'delta_w', 'delta_conv_w': 'delta_w', 'delta_conv_b': 'delta_w', 'delta_dt_bias': 'delta_w', 'delta_a_log': 'delta_w', 'delta_d_skip': 'delta_w', 'delta_ssm_norm_w': 'delta_w', 'delta_gmlp_ln_w': 'delta_w', 'delta_gmlp_ln_b': 'delta_w', 'delta_gmlp_ws': 'delta_w', 'delta_gmlp_bs': 'delta_w', 'delta_out_w_even': 'delta_w', 'delta_qkv_w': 'delta_w', 'delta_qkv_b': 'delta_w', 'delta_o_w': 'delta_w', 'delta_o_b': 'delta_w', 'delta_sinks': 'delta_w', 'delta_rel_table': 'delta_w', 'delta_ffn_gate_w': 'delta_w', 'delta_ffn_up_w': 'delta_w', 'delta_ffn_down_w': 'delta_w', 'delta_final_norm_w': 'delta_w', 'new_m_ada_w': 'new_m', 'new_m_ada_b': 'new_m', 'new_m_norm_mix_w': 'new_m', 'new_m_norm_ffn_w': 'new_m', 'new_m_in_w_even': 'new_m', 'new_m_conv_w': 'new_m', 'new_m_conv_b': 'new_m', 'new_m_dt_bias': 'new_m', 'new_m_a_log': 'new_m', 'new_m_d_skip': 'new_m', 'new_m_ssm_norm_w': 'new_m', 'new_m_gmlp_ln_w': 'new_m', 'new_m_gmlp_ln_b': 'new_m', 'new_m_gmlp_ws': 'new_m', 'new_m_gmlp_bs': 'new_m', 'new_m_out_w_even': 'new_m', 'new_m_qkv_w': 'new_m', 'new_m_qkv_b': 'new_m', 'new_m_o_w': 'new_m', 'new_m_o_b': 'new_m', 'new_m_sinks': 'new_m', 'new_m_rel_table': 'new_m', 'new_m_ffn_gate_w': 'new_m', 'new_m_ffn_up_w': 'new_m', 'new_m_ffn_down_w': 'new_m', 'new_m_final_norm_w': 'new_m', 'new_v_ada_w': 'new_v', 'new_v_ada_b': 'new_v', 'new_v_norm_mix_w': 'new_v', 'new_v_norm_ffn_w': 'new_v', 'new_v_in_w_even': 'new_v', 'new_v_conv_w': 'new_v', 'new_v_conv_b': 'new_v', 'new_v_dt_bias': 'new_v', 'new_v_a_log': 'new_v', 'new_v_d_skip': 'new_v', 'new_v_ssm_norm_w': 'new_v', 'new_v_gmlp_ln_w': 'new_v', 'new_v_gmlp_ln_b': 'new_v', 'new_v_gmlp_ws': 'new_v', 'new_v_gmlp_bs': 'new_v', 'new_v_out_w_even': 'new_v', 'new_v_qkv_w': 'new_v', 'new_v_qkv_b': 'new_v', 'new_v_o_w': 'new_v', 'new_v_o_b': 'new_v', 'new_v_sinks': 'new_v', 'new_v_rel_table': 'new_v', 'new_v_ffn_gate_w': 'new_v', 'new_v_ffn_up_w': 'new_v', 'new_v_ffn_down_w': 'new_v', 'new_v_final_norm_w': 'new_v'}


def _forward(args):
    return _fwd_reference(*[args[k] for k in FWD_PARAMS])


def _output_shape():
    def fwd():
        inp = _fwd_setup_inputs(0)
        return _fwd_reference(*[inp[k] for k in FWD_PARAMS])
    out = _jax.eval_shape(fwd)
    return out.shape, out.dtype

N_MICROBATCH = 1
ADAM_LR = 0.001
ADAM_B1 = 0.9
ADAM_B2 = 0.999
ADAM_EPS = 1e-08
ADAM_WD = 0.01
ADAM_STEP = 10
PER_EXAMPLE_BATCH_AXIS = {'x': 0, 'c': 0, 'loss_target': 0}
SHARED_INPUTS = []
_WEIGHT_DTYPES = {'ada_w': _jnp.float32, 'ada_b': _jnp.float32, 'norm_mix_w': _jnp.float32, 'norm_ffn_w': _jnp.float32, 'in_w_even': _jnp.float32, 'conv_w': _jnp.float32, 'conv_b': _jnp.float32, 'dt_bias': _jnp.float32, 'a_log': _jnp.float32, 'd_skip': _jnp.float32, 'ssm_norm_w': _jnp.float32, 'gmlp_ln_w': _jnp.float32, 'gmlp_ln_b': _jnp.float32, 'gmlp_ws': _jnp.float32, 'gmlp_bs': _jnp.float32, 'out_w_even': _jnp.float32, 'qkv_w': _jnp.float32, 'qkv_b': _jnp.float32, 'o_w': _jnp.float32, 'o_b': _jnp.float32, 'sinks': _jnp.float32, 'rel_table': _jnp.float32, 'ffn_gate_w': _jnp.float32, 'ffn_up_w': _jnp.float32, 'ffn_down_w': _jnp.float32, 'final_norm_w': _jnp.float32}
MOMENT_SCALE = {'ada_w': 7.197753e-02, 'ada_b': 1.236266e-01, 'norm_mix_w': 6.916837e-02, 'norm_ffn_w': 6.980327e-02, 'in_w_even': 4.629515e-02, 'conv_w': 4.895041e-02, 'conv_b': 6.006475e-02, 'dt_bias': 1.143833e-01, 'a_log': 1.450531e-01, 'd_skip': 3.322515e-01, 'ssm_norm_w': 5.395079e-02, 'gmlp_ln_w': 2.784694e-02, 'gmlp_ln_b': 2.985333e-02, 'gmlp_ws': 2.686203e-02, 'gmlp_bs': 3.817151e-02, 'out_w_even': 7.094424e-02, 'qkv_w': 3.226173e-02, 'qkv_b': 8.195065e-02, 'o_w': 2.763079e-02, 'o_b': 7.128872e-02, 'sinks': 1.240172e-02, 'rel_table': 2.009051e-02, 'ffn_gate_w': 3.114739e-02, 'ffn_up_w': 3.019386e-02, 'ffn_down_w': 4.995029e-02, 'final_norm_w': 6.407269e+01}


def _to_microbatches(a, axis):
    t = _jnp.moveaxis(a, axis, 0)
    t = t.reshape((N_MICROBATCH, t.shape[0] // N_MICROBATCH) + t.shape[1:])
    return _jnp.moveaxis(t, 1, axis + 1)


def setup_inputs(seed: int = 0) -> dict:
    inp = _fwd_setup_inputs(seed)
    key = _jax.random.fold_in(_jax.random.key(seed), 7919)
    shape, _ = _output_shape()
    out = dict(inp)
    out["loss_target"] = _jax.random.normal(_jax.random.fold_in(key, 0), shape, _jnp.float32)
    for i, name in enumerate(TWIN_WEIGHTS):
        w = inp[name].astype(_jnp.float32)
        if MOMENT_SCALE is None:
            s = _jnp.sqrt(_jnp.mean(_jnp.square(w)) + 1e-30)
        else:
            s = MOMENT_SCALE[name]
        km, kv = _jax.random.split(_jax.random.fold_in(key, i + 1))
        out[name] = w
        out["m_" + name] = s * _jax.random.normal(km, w.shape, _jnp.float32)
        out["v_" + name] = (s * s) * _jax.random.uniform(kv, w.shape, _jnp.float32, 0.5, 1.5)
    if N_MICROBATCH > 1:
        for name, axis in PER_EXAMPLE_BATCH_AXIS.items():
            out[name] = _to_microbatches(out[name], axis)
    return {'x': out['x'], 'c': out['c'], 'ada_w': out['ada_w'], 'ada_b': out['ada_b'], 'norm_mix_w': out['norm_mix_w'], 'norm_ffn_w': out['norm_ffn_w'], 'in_w_even': out['in_w_even'], 'conv_w': out['conv_w'], 'conv_b': out['conv_b'], 'dt_bias': out['dt_bias'], 'a_log': out['a_log'], 'd_skip': out['d_skip'], 'ssm_norm_w': out['ssm_norm_w'], 'gmlp_ln_w': out['gmlp_ln_w'], 'gmlp_ln_b': out['gmlp_ln_b'], 'gmlp_ws': out['gmlp_ws'], 'gmlp_bs': out['gmlp_bs'], 'out_w_even': out['out_w_even'], 'qkv_w': out['qkv_w'], 'qkv_b': out['qkv_b'], 'o_w': out['o_w'], 'o_b': out['o_b'], 'sinks': out['sinks'], 'rel_table': out['rel_table'], 'ffn_gate_w': out['ffn_gate_w'], 'ffn_up_w': out['ffn_up_w'], 'ffn_down_w': out['ffn_down_w'], 'final_norm_w': out['final_norm_w'], 'loss_target': out['loss_target'], 'm_ada_w': out['m_ada_w'], 'm_ada_b': out['m_ada_b'], 'm_norm_mix_w': out['m_norm_mix_w'], 'm_norm_ffn_w': out['m_norm_ffn_w'], 'm_in_w_even': out['m_in_w_even'], 'm_conv_w': out['m_conv_w'], 'm_conv_b': out['m_conv_b'], 'm_dt_bias': out['m_dt_bias'], 'm_a_log': out['m_a_log'], 'm_d_skip': out['m_d_skip'], 'm_ssm_norm_w': out['m_ssm_norm_w'], 'm_gmlp_ln_w': out['m_gmlp_ln_w'], 'm_gmlp_ln_b': out['m_gmlp_ln_b'], 'm_gmlp_ws': out['m_gmlp_ws'], 'm_gmlp_bs': out['m_gmlp_bs'], 'm_out_w_even': out['m_out_w_even'], 'm_qkv_w': out['m_qkv_w'], 'm_qkv_b': out['m_qkv_b'], 'm_o_w': out['m_o_w'], 'm_o_b': out['m_o_b'], 'm_sinks': out['m_sinks'], 'm_rel_table': out['m_rel_table'], 'm_ffn_gate_w': out['m_ffn_gate_w'], 'm_ffn_up_w': out['m_ffn_up_w'], 'm_ffn_down_w': out['m_ffn_down_w'], 'm_final_norm_w': out['m_final_norm_w'], 'v_ada_w': out['v_ada_w'], 'v_ada_b': out['v_ada_b'], 'v_norm_mix_w': out['v_norm_mix_w'], 'v_norm_ffn_w': out['v_norm_ffn_w'], 'v_in_w_even': out['v_in_w_even'], 'v_conv_w': out['v_conv_w'], 'v_conv_b': out['v_conv_b'], 'v_dt_bias': out['v_dt_bias'], 'v_a_log': out['v_a_log'], 'v_d_skip': out['v_d_skip'], 'v_ssm_norm_w': out['v_ssm_norm_w'], 'v_gmlp_ln_w': out['v_gmlp_ln_w'], 'v_gmlp_ln_b': out['v_gmlp_ln_b'], 'v_gmlp_ws': out['v_gmlp_ws'], 'v_gmlp_bs': out['v_gmlp_bs'], 'v_out_w_even': out['v_out_w_even'], 'v_qkv_w': out['v_qkv_w'], 'v_qkv_b': out['v_qkv_b'], 'v_o_w': out['v_o_w'], 'v_o_b': out['v_o_b'], 'v_sinks': out['v_sinks'], 'v_rel_table': out['v_rel_table'], 'v_ffn_gate_w': out['v_ffn_gate_w'], 'v_ffn_up_w': out['v_ffn_up_w'], 'v_ffn_down_w': out['v_ffn_down_w'], 'v_final_norm_w': out['v_final_norm_w']}


def _loss(weights, diff, rest, loss_target):
    with _jax.named_scope("forward"):
        args = {**rest, TWIN_DIFF_INPUT: diff, **{k: w.astype(_WEIGHT_DTYPES[k]) for k, w in weights.items()}}
        y = _forward(args)
    with _jax.named_scope("loss_head"):
        err = _jnp.square(y.astype(_jnp.float32) - loss_target)
        return 0.5 * _jnp.sum(_jnp.mean(err, axis=-1)) if err.ndim else 0.5 * err


def _adamw(w, g, m, v):
    m = ADAM_B1 * m + (1.0 - ADAM_B1) * g
    v = ADAM_B2 * v + (1.0 - ADAM_B2) * _jnp.square(g)
    m_hat = m / (1.0 - ADAM_B1 ** ADAM_STEP)
    v_hat = v / (1.0 - ADAM_B2 ** ADAM_STEP)
    delta = -ADAM_LR * (m_hat / (_jnp.sqrt(v_hat) + ADAM_EPS) + ADAM_WD * w)
    return delta, m, v


def reference(x, c, ada_w, ada_b, norm_mix_w, norm_ffn_w, in_w_even, conv_w, conv_b, dt_bias, a_log, d_skip, ssm_norm_w, gmlp_ln_w, gmlp_ln_b, gmlp_ws, gmlp_bs, out_w_even, qkv_w, qkv_b, o_w, o_b, sinks, rel_table, ffn_gate_w, ffn_up_w, ffn_down_w, final_norm_w, loss_target, m_ada_w, m_ada_b, m_norm_mix_w, m_norm_ffn_w, m_in_w_even, m_conv_w, m_conv_b, m_dt_bias, m_a_log, m_d_skip, m_ssm_norm_w, m_gmlp_ln_w, m_gmlp_ln_b, m_gmlp_ws, m_gmlp_bs, m_out_w_even, m_qkv_w, m_qkv_b, m_o_w, m_o_b, m_sinks, m_rel_table, m_ffn_gate_w, m_ffn_up_w, m_ffn_down_w, m_final_norm_w, v_ada_w, v_ada_b, v_norm_mix_w, v_norm_ffn_w, v_in_w_even, v_conv_w, v_conv_b, v_dt_bias, v_a_log, v_d_skip, v_ssm_norm_w, v_gmlp_ln_w, v_gmlp_ln_b, v_gmlp_ws, v_gmlp_bs, v_out_w_even, v_qkv_w, v_qkv_b, v_o_w, v_o_b, v_sinks, v_rel_table, v_ffn_gate_w, v_ffn_up_w, v_ffn_down_w, v_final_norm_w):
    given = dict(x=x, c=c, ada_w=ada_w, ada_b=ada_b, norm_mix_w=norm_mix_w, norm_ffn_w=norm_ffn_w, in_w_even=in_w_even, conv_w=conv_w, conv_b=conv_b, dt_bias=dt_bias, a_log=a_log, d_skip=d_skip, ssm_norm_w=ssm_norm_w, gmlp_ln_w=gmlp_ln_w, gmlp_ln_b=gmlp_ln_b, gmlp_ws=gmlp_ws, gmlp_bs=gmlp_bs, out_w_even=out_w_even, qkv_w=qkv_w, qkv_b=qkv_b, o_w=o_w, o_b=o_b, sinks=sinks, rel_table=rel_table, ffn_gate_w=ffn_gate_w, ffn_up_w=ffn_up_w, ffn_down_w=ffn_down_w, final_norm_w=final_norm_w, loss_target=loss_target, m_ada_w=m_ada_w, m_ada_b=m_ada_b, m_norm_mix_w=m_norm_mix_w, m_norm_ffn_w=m_norm_ffn_w, m_in_w_even=m_in_w_even, m_conv_w=m_conv_w, m_conv_b=m_conv_b, m_dt_bias=m_dt_bias, m_a_log=m_a_log, m_d_skip=m_d_skip, m_ssm_norm_w=m_ssm_norm_w, m_gmlp_ln_w=m_gmlp_ln_w, m_gmlp_ln_b=m_gmlp_ln_b, m_gmlp_ws=m_gmlp_ws, m_gmlp_bs=m_gmlp_bs, m_out_w_even=m_out_w_even, m_qkv_w=m_qkv_w, m_qkv_b=m_qkv_b, m_o_w=m_o_w, m_o_b=m_o_b, m_sinks=m_sinks, m_rel_table=m_rel_table, m_ffn_gate_w=m_ffn_gate_w, m_ffn_up_w=m_ffn_up_w, m_ffn_down_w=m_ffn_down_w, m_final_norm_w=m_final_norm_w, v_ada_w=v_ada_w, v_ada_b=v_ada_b, v_norm_mix_w=v_norm_mix_w, v_norm_ffn_w=v_norm_ffn_w, v_in_w_even=v_in_w_even, v_conv_w=v_conv_w, v_conv_b=v_conv_b, v_dt_bias=v_dt_bias, v_a_log=v_a_log, v_d_skip=v_d_skip, v_ssm_norm_w=v_ssm_norm_w, v_gmlp_ln_w=v_gmlp_ln_w, v_gmlp_ln_b=v_gmlp_ln_b, v_gmlp_ws=v_gmlp_ws, v_gmlp_bs=v_gmlp_bs, v_out_w_even=v_out_w_even, v_qkv_w=v_qkv_w, v_qkv_b=v_qkv_b, v_o_w=v_o_w, v_o_b=v_o_b, v_sinks=v_sinks, v_rel_table=v_rel_table, v_ffn_gate_w=v_ffn_gate_w, v_ffn_up_w=v_ffn_up_w, v_ffn_down_w=v_ffn_down_w, v_final_norm_w=v_final_norm_w)
    weights = {n: given[n] for n in TWIN_WEIGHTS}
    shared = {n: given[n] for n in SHARED_INPUTS}
    per_example = {n: given[n] for n in ['x', 'c']}
    grad_fn = _jax.value_and_grad(_loss, argnums=(0, 1))

    def one_microbatch(ex, loss_target):
        ex = dict(ex)
        diff = ex.pop(TWIN_DIFF_INPUT)
        return grad_fn(weights, diff, {**shared, **ex}, loss_target)

    if N_MICROBATCH == 1:
        loss, (grad_w, grad_x) = one_microbatch(per_example, given["loss_target"])
    else:
        def body(carry, xs):
            loss_sum, grad_sum = carry
            l_k, (gw_k, gx_k) = one_microbatch(xs[0], xs[1])
            with _jax.named_scope("update"):
                return (loss_sum + l_k, _jax.tree.map(_jnp.add, grad_sum, gw_k)), gx_k

        init = (_jnp.zeros((), _jnp.float32), _jax.tree.map(_jnp.zeros_like, weights))
        (loss, grad_w), grad_x = _jax.lax.scan(body, init, (per_example, given["loss_target"]))
    with _jax.named_scope("update"):
        delta_w, new_m, new_v = {}, {}, {}
        for n in TWIN_WEIGHTS:
            delta_w[n], new_m[n], new_v[n] = _adamw(weights[n], grad_w[n], given["m_" + n], given["v_" + n])
    return (loss, grad_x, *[grad_w[n] for n in TWIN_WEIGHTS], *[delta_w[n] for n in TWIN_WEIGHTS],
            *[new_m[n] for n in TWIN_WEIGHTS], *[new_v[n] for n in TWIN_WEIGHTS])
```

```python
import functools
import math

import numpy as np
import jax
import jax.numpy as jnp
from jax import lax
from jax.experimental import pallas as pl
from jax.experimental.pallas import tpu as pltpu

F32 = jnp.float32
BF16 = jnp.bfloat16
_MXU = jnp.bfloat16
_VMEM_LIMIT = 56 * 1024 * 1024
D = 1024
L = 128
NSTATE = 128
EPS = 1e-6
NEG_INF = -1e30
FFN = 2816
ADAM_LR, ADAM_B1, ADAM_B2, ADAM_EPS, ADAM_WD, ADAM_STEP = 0.001, 0.9, 0.999, 1e-08, 0.01, 10
PACK_ROWS = 6656
MESH = pl.DeviceIdType.MESH
ANY = pl.BlockSpec(memory_space=pl.ANY)

NN = (((1,), (0,)), ((), ()))
NT = (((1,), (1,)), ((), ()))
TN = (((0,), (0,)), ((), ()))


def _dot(a, b, dn=NN):
    return lax.dot_general(a.astype(_MXU), b.astype(_MXU), dn, preferred_element_type=F32)


def _params(sem=None):
    return pltpu.CompilerParams(dimension_semantics=sem, vmem_limit_bytes=_VMEM_LIMIT)


def _sigmoid(x):
    return 1.0 / (1.0 + jnp.exp(-x))


def _softplus(x):
    return jnp.maximum(x, 0.0) + jnp.log(1.0 + jnp.exp(-jnp.abs(x)))


def _gelu(x):
    return 0.5 * x * (1.0 + lax.erf(x * (2.0 ** -0.5)))


def _gelu_grad(x):
    return 0.5 * (1.0 + lax.erf(x * (2.0 ** -0.5))) + x * jnp.exp(-0.5 * x * x) * (1.0 / math.sqrt(2.0 * math.pi))


def _silu_grad(a):
    sg = _sigmoid(a)
    return sg * (1.0 + a * (1.0 - sg))


def _rowwise(name, fn, rows, vecs, out_rows, out_accs=(), tr=512):
    S = rows[0].shape[0]
    tr = min(tr, S)
    assert S % tr == 0
    nr, nv, no, na = len(rows), len(vecs), len(out_rows), len(out_accs)

    def body(*refs):
        ins, outs = refs[:nr + nv], refs[nr + nv:]
        res = fn(*[r[...] for r in ins])
        if not isinstance(res, (tuple, list)):
            res = (res,)
        for k in range(no):
            outs[k][...] = res[k].astype(outs[k].dtype)
        if na:
            @pl.when(pl.program_id(0) == 0)
            def _():
                for k in range(na):
                    outs[no + k][...] = jnp.zeros_like(outs[no + k])
            for k in range(na):
                outs[no + k][...] += res[no + k]

    in_specs = [pl.BlockSpec((tr, a.shape[1]), lambda i: (i, 0)) for a in rows]
    in_specs += [pl.BlockSpec(v.shape, lambda i: (0, 0)) for v in vecs]
    out_specs = [pl.BlockSpec((tr, c), lambda i: (i, 0)) for c, _ in out_rows]
    out_specs += [pl.BlockSpec(s, lambda i: (0, 0)) for s in out_accs]
    out_shape = [jax.ShapeDtypeStruct((S, c), dt) for c, dt in out_rows]
    out_shape += [jax.ShapeDtypeStruct(s, F32) for s in out_accs]
    return pl.pallas_call(body, name=name, grid=(S // tr,), in_specs=in_specs, out_specs=out_specs,
                          out_shape=out_shape, compiler_params=_params(("arbitrary",)))(*rows, *vecs)


def _col_tile(n, cap):
    if n <= cap or n % 128:
        return n
    best = 128
    for t in range(128, cap + 1, 128):
        if n % t == 0:
            best = t
    return best


def _mm(name, As, Bs, mode, outs, epi=None, groups=None, extras=(), vecs=(), tm=512, tn_cap=1536):
    M = As[0].shape[0]
    N = Bs[0].shape[1] if mode == "nn" else Bs[0].shape[0]
    tm = min(tm, M)
    tn = _col_tile(N, tn_cap)
    assert M % tm == 0 and N % tn == 0
    npair = len(As)
    groups = groups or [0] * npair
    ng = max(groups) + 1
    nx, nv = len(extras), len(vecs)
    dn = NN if mode == "nn" else NT

    def body(*refs):
        a_refs, b_refs = refs[:npair], refs[npair:2 * npair]
        x_refs = refs[2 * npair:2 * npair + nx]
        v_refs = refs[2 * npair + nx:2 * npair + nx + nv]
        o_refs = refs[2 * npair + nx + nv:]
        accs = [None] * ng
        for k in range(npair):
            d = _dot(a_refs[k][...], b_refs[k][...], dn)
            accs[groups[k]] = d if accs[groups[k]] is None else accs[groups[k]] + d
        args = accs + [x[...] for x in x_refs] + [v[...] for v in v_refs]
        res = epi(*args) if epi is not None else tuple(accs)
        if not isinstance(res, (tuple, list)):
            res = (res,)
        for o, r in zip(o_refs, res):
            o[...] = r.astype(o.dtype)

    in_specs = [pl.BlockSpec((tm, a.shape[1]), lambda i, j: (i, 0)) for a in As]
    if mode == "nn":
        in_specs += [pl.BlockSpec((b.shape[0], tn), lambda i, j: (0, j)) for b in Bs]
    else:
        in_specs += [pl.BlockSpec((tn, b.shape[1]), lambda i, j: (j, 0)) for b in Bs]
    in_specs += [pl.BlockSpec((tm, tn), lambda i, j: (i, j)) for _ in extras]
    in_specs += [pl.BlockSpec((1, tn), lambda i, j: (0, j)) for _ in vecs]
    out_specs = [pl.BlockSpec((tm, tn), lambda i, j: (i, j)) for _ in outs]
    out_shape = [jax.ShapeDtypeStruct((M, N), dt) for dt in outs]
    return pl.pallas_call(body, name=name, grid=(M // tm, N // tn), in_specs=in_specs, out_specs=out_specs,
                          out_shape=out_shape, compiler_params=_params(("parallel", "parallel")))(
                              *As, *Bs, *extras, *vecs)


def _mm_tn(name, A, B, tk=512, t2_cap=1536):
    S, K1 = A.shape
    N2 = B.shape[1]
    tk = min(tk, S)
    t2 = _col_tile(N2, t2_cap)
    assert S % tk == 0 and N2 % t2 == 0

    def body(a_ref, b_ref, o_ref):
        @pl.when(pl.program_id(1) == 0)
        def _():
            o_ref[...] = jnp.zeros_like(o_ref)
        o_ref[...] += _dot(a_ref[...], b_ref[...], TN)

    return pl.pallas_call(
        body, name=name, grid=(N2 // t2, S // tk),
        in_specs=[pl.BlockSpec((tk, K1), lambda j, k: (k, 0)), pl.BlockSpec((tk, t2), lambda j, k: (k, j))],
        out_specs=pl.BlockSpec((K1, t2), lambda j, k: (0, j)),
        out_shape=jax.ShapeDtypeStruct((K1, N2), F32),
        compiler_params=_params(("parallel", "arbitrary")))(A, B)


def _norm_mod_fwd(name, x, nw, sc, sh):
    def fn(x, nw, sc, sh):
        rstd = lax.rsqrt(jnp.mean(x * x, axis=-1, keepdims=True) + EPS)
        return (x * rstd * nw) * (1.0 + sc) + sh
    return _rowwise(name, fn, [x], [nw, sc, sh], [(D, BF16)])[0]


def _norm_mod_bwd(name, x, dh, dres, nw, sc):
    def fn(x, dh, dres, nw, sc):
        rstd = lax.rsqrt(jnp.mean(x * x, axis=-1, keepdims=True) + EPS)
        xh = x * rstd
        dn = dh * (1.0 + sc)
        dxh = dn * nw
        dx = rstd * (dxh - xh * jnp.mean(dxh * xh, axis=-1, keepdims=True))
        return (dres + dx, jnp.sum(dh, axis=0, keepdims=True), jnp.sum(dh * (xh * nw), axis=0, keepdims=True),
                jnp.sum(dn * xh, axis=0, keepdims=True))
    return _rowwise(name, fn, [x, dh, dres], [nw, sc], [(D, F32)], [(1, D)] * 3)


def _gate_bwd(name, dx, y, g):
    def fn(dx, y, g):
        dy = dx * g
        return dy, jnp.sum(dx * y, axis=0, keepdims=True), jnp.sum(dy, axis=0, keepdims=True)
    return _rowwise(name, fn, [dx, y], [g], [(D, BF16)], [(1, D)] * 2)


def _loss_head(x, tgt, fw):
    def fn(x, tgt, fw):
        rstd = lax.rsqrt(jnp.mean(x * x, axis=-1, keepdims=True) + EPS)
        xh = x * rstd
        err = xh * fw - tgt
        dout = err * (1.0 / D)
        dxh = dout * fw
        dx = rstd * (dxh - xh * jnp.mean(dxh * xh, axis=-1, keepdims=True))
        sq = jnp.sum(jnp.sum(err * err, axis=1, keepdims=True), axis=0, keepdims=True)
        return dx, sq, jnp.sum(dout * xh, axis=0, keepdims=True)
    return _rowwise("loss_head", fn, [x, tgt], [fw], [(D, F32)], [(1, 1), (1, D)])


def _ffn_fwd(tag, h, wg, wu, wd, x, g2):
    def act(a, b):
        return a, b, a * _sigmoid(a) * b
    a, b, f = _mm(f"ffn_up_{tag}", [h, h], [wg, wu], "nn", [F32, F32, BF16], epi=act, groups=[0, 1], tn_cap=1408)

    def res(y, x, g):
        return y, x + g * y
    y, xo = _mm(f"ffn_down_{tag}", [f], [wd], "nn", [F32, F32], epi=res, extras=[x], vecs=[g2])
    return a, b, f, y, xo


def _ffn_bwd(tag, dx, h, a, b, f, y, wg, wu, wd, g2):
    dy, dg2, _ = _gate_bwd(f"ffn_gate_bwd_{tag}", dx, y, g2)

    def act_bwd(df, a, b):
        return df * b * _silu_grad(a), df * (a * _sigmoid(a))
    da, db = _mm(f"ffn_dact_{tag}", [dy], [wd], "nt", [BF16, BF16], epi=act_bwd, extras=[a, b], tn_cap=1408)
    dwd = _mm_tn(f"ffn_dwd_{tag}", f, dy)
    dwg = _mm_tn(f"ffn_dwg_{tag}", h, da, t2_cap=1408)
    dwu = _mm_tn(f"ffn_dwu_{tag}", h, db, t2_cap=1408)
    dh = _mm(f"ffn_dh_{tag}", [da, db], [wg, wu], "nt", [F32])[0]
    return dh, dg2, dwg, dwu, dwd


def _conv_fwd(xr, w, b, tb=512):
    S, C = xr.shape
    tb = min(tb, S)

    def body(x_ref, halo_ref, w_ref, b_ref, pre_ref, out_ref):
        i = pl.program_id(0)
        halo = jnp.where(i > 0, halo_ref[...], 0.0)
        xe = jnp.concatenate([halo, x_ref[...]], axis=0)
        pre = w_ref[3:4, :] * x_ref[...] + b_ref[...]
        for j in (1, 2, 3):
            pre = pre + w_ref[3 - j:4 - j, :] * pltpu.roll(xe, j, axis=0)[8:, :]
        pre_ref[...] = pre
        out_ref[...] = pre * _sigmoid(pre)

    return pl.pallas_call(
        body, name="conv_fwd", grid=(S // tb,),
        in_specs=[pl.BlockSpec((tb, C), lambda i: (i, 0)),
                  pl.BlockSpec((8, C), lambda i: (jnp.maximum(i * (tb // 8) - 1, 0), 0)),
                  pl.BlockSpec((4, C), lambda i: (0, 0)), pl.BlockSpec((1, C), lambda i: (0, 0))],
        out_specs=[pl.BlockSpec((tb, C), lambda i: (i, 0))] * 2,
        out_shape=[jax.ShapeDtypeStruct((S, C), F32)] * 2,
        compiler_params=_params(("parallel",)))(xr, xr, w, b)


def _conv_bwd(dxc, pre, xr, w, tb=512):
    S, C = xr.shape
    tb = min(tb, S)
    nblk = S // tb

    def body(d_ref, p_ref, dn_ref, pn_ref, x_ref, xh_ref, w_ref, dx_ref, dw_ref, db_ref):
        i = pl.program_id(0)

        @pl.when(i == 0)
        def _():
            dw_ref[...] = jnp.zeros_like(dw_ref)
            db_ref[...] = jnp.zeros_like(db_ref)

        dpre = d_ref[...] * _silu_grad(p_ref[...])
        dnext = jnp.where(i < nblk - 1, dn_ref[...] * _silu_grad(pn_ref[...]), 0.0)
        pe = jnp.concatenate([dpre, dnext], axis=0)
        dx = w_ref[3:4, :] * dpre
        for j in (1, 2, 3):
            dx = dx + w_ref[3 - j:4 - j, :] * pltpu.roll(pe, tb + 8 - j, axis=0)[:tb, :]
        dx_ref[...] = dx.astype(dx_ref.dtype)
        halo = jnp.where(i > 0, xh_ref[...], 0.0)
        xe = jnp.concatenate([halo, x_ref[...]], axis=0)
        for k in range(3):
            dw_ref[k:k + 1, :] += jnp.sum(dpre * pltpu.roll(xe, 3 - k, axis=0)[8:, :], axis=0, keepdims=True)
        dw_ref[3:4, :] += jnp.sum(dpre * x_ref[...], axis=0, keepdims=True)
        db_ref[...] += jnp.sum(dpre, axis=0, keepdims=True)

    blk = pl.BlockSpec((tb, C), lambda i: (i, 0))
    nxt = pl.BlockSpec((8, C), lambda i: (jnp.minimum((i + 1) * (tb // 8), S // 8 - 1), 0))
    prv = pl.BlockSpec((8, C), lambda i: (jnp.maximum(i * (tb // 8) - 1, 0), 0))
    return pl.pallas_call(
        body, name="conv_bwd", grid=(nblk,),
        in_specs=[blk, blk, nxt, nxt, blk, prv, pl.BlockSpec((4, C), lambda i: (0, 0))],
        out_specs=[blk, pl.BlockSpec((4, C), lambda i: (0, 0)), pl.BlockSpec((1, C), lambda i: (0, 0))],
        out_shape=[jax.ShapeDtypeStruct((S, C), BF16), jax.ShapeDtypeStruct((4, C), F32),
                   jax.ShapeDtypeStruct((1, C), F32)],
        compiler_params=_params(("arbitrary",)))(dxc, pre, dxc, pre, xr, xr, w)


def _iota(shape, dim):
    return lax.broadcasted_iota(jnp.int32, shape, dim)


def _colsel(m, lane, h):
    return jnp.sum(jnp.where(lane == h, m, 0.0), axis=1, keepdims=True)


def _cumsum_rows(v):
    r = _iota(v.shape, 0)
    k = 1
    while k < v.shape[0]:
        v = v + jnp.where(r >= k, pltpu.roll(v, k, axis=0), 0.0)
        k *= 2
    return v


def _suffix_sum_rows(v):
    n = v.shape[0]
    r = _iota(v.shape, 0)
    k = 1
    while k < n:
        v = v + jnp.where(r < n - k, pltpu.roll(v, n - k, axis=0), 0.0)
        k *= 2
    return v


def _ssd_fwd(xc, dtr, z, dtb, alog, dskl, nw):
    S = xc.shape[0]
    nc = S // L

    def body(xc_ref, dtr_ref, z_ref, dtb_ref, alog_ref, dsk_ref, nw_ref, ya_ref, y_ref, prev_ref,
             st_ref, cum_ref, cumT_ref):
        i = pl.program_id(0)

        @pl.when(i == 0)
        def _():
            st_ref[...] = jnp.zeros_like(st_ref)

        lane = _iota((L, 128), 1)
        lane1 = _iota((1, 128), 1)
        lo = lane < 64
        lo1 = lane1 < 64
        tril = _iota((L, L), 0) >= _iota((L, L), 1)
        dt = _softplus(dtr_ref[...] + dtb_ref[...])
        a_neg = -jnp.exp(alog_ref[...])
        cum = _cumsum_rows(dt * a_neg)
        cum_ref[...] = cum
        cumT_ref[...] = cum.T
        last_all = cum_ref[L - 1:L, :]
        prev_t = st_ref[...]
        prev_ref[0] = prev_t
        for g in range(2):
            bg = xc_ref[:, 1024 + g * 128:1152 + g * 128]
            cg = xc_ref[:, 1280 + g * 128:1408 + g * 128]
            gmat = _dot(cg, bg, NT)
            yoff = _dot(cg, prev_t[:, g * 512:(g + 1) * 512])
            bg_t = bg.T
            for jp in range(4):
                j = g * 4 + jp
                sl = slice(j * 128, (j + 1) * 128)
                xp = xc_ref[:, sl]
                cc = [_colsel(cum, lane, 2 * j), _colsel(cum, lane, 2 * j + 1)]
                cum_l = jnp.where(lo, cc[0], cc[1])
                dt_l = jnp.where(lo, _colsel(dt, lane, 2 * j), _colsel(dt, lane, 2 * j + 1))
                last_l = jnp.where(lo1, _colsel(last_all, lane1, 2 * j), _colsel(last_all, lane1, 2 * j + 1))
                xd = xp * dt_l
                ys = []
                for hh in range(2):
                    seg = cc[hh] - cumT_ref[2 * j + hh:2 * j + hh + 1, :]
                    dm = jnp.where(tril, jnp.exp(jnp.where(tril, seg, 0.0)), 0.0)
                    ys.append(_dot(gmat * dm, xd))
                y_ref[:, sl] = (jnp.where(lo, ys[0], ys[1]) + jnp.exp(cum_l) * yoff[:, jp * 128:(jp + 1) * 128]
                                + dsk_ref[:, sl] * xp)
                st_ref[:, sl] = prev_t[:, sl] * jnp.exp(last_l) + _dot(bg_t, xd * jnp.exp(last_l - cum_l))
        for g in range(2):
            sl = slice(g * 512, (g + 1) * 512)
            zz = z_ref[:, sl]
            yg = y_ref[:, sl] * (zz * _sigmoid(zz))
            rstd = lax.rsqrt(jnp.mean(yg * yg, axis=-1, keepdims=True) + EPS)
            ya_ref[:, sl] = (yg * rstd * nw_ref[:, sl]).astype(ya_ref.dtype)

    blk = lambda c: pl.BlockSpec((L, c), lambda i: (i, 0))
    vec = lambda c: pl.BlockSpec((1, c), lambda i: (0, 0))
    return pl.pallas_call(
        body, name="ssd_fwd", grid=(nc,),
        in_specs=[blk(1536), blk(128), blk(1024), vec(128), vec(128), vec(1024), vec(1024)],
        out_specs=[blk(1024), blk(1024), pl.BlockSpec((1, NSTATE, 1024), lambda i: (i, 0, 0))],
        out_shape=[jax.ShapeDtypeStruct((S, 1024), BF16), jax.ShapeDtypeStruct((S, 1024), F32),
                   jax.ShapeDtypeStruct((nc, NSTATE, 1024), F32)],
        scratch_shapes=[pltpu.VMEM((NSTATE, 1024), F32), pltpu.VMEM((L, 128), F32), pltpu.VMEM((L, 128), F32)],
        compiler_params=_params(("arbitrary",)))(xc, dtr, z, dtb, alog, dskl, nw)


def _ssd_bwd(dya, y, z, xc, dtr, prev, dtb, alog, dskl, nw):
    S = xc.shape[0]
    nc = S // L

    def body(dya_ref, y_ref, z_ref, xc_ref, dtr_ref, prev_ref, dtb_ref, alog_ref, dsk_ref, nw_ref,
             dz_ref, dxc_ref, ddtr_ref, dnw_ref, ddsk_ref, dalog_ref, ddtb_ref,
             dst_ref, cum_ref, cumT_ref, dy_ref, dskacc_ref):
        i = pl.program_id(0)

        @pl.when(i == 0)
        def _():
            dst_ref[...] = jnp.zeros_like(dst_ref)
            dskacc_ref[...] = jnp.zeros_like(dskacc_ref)
            dnw_ref[...] = jnp.zeros_like(dnw_ref)
            dalog_ref[...] = jnp.zeros_like(dalog_ref)
            ddtb_ref[...] = jnp.zeros_like(ddtb_ref)

        lane = _iota((L, 128), 1)
        lane1 = _iota((1, 128), 1)
        lo = lane < 64
        lo1 = lane1 < 64
        r2, c2 = _iota((L, L), 0), _iota((L, L), 1)
        tril = r2 >= c2
        triu = r2 <= c2
        is_last = _iota((L, 1), 0) == L - 1

        for g in range(2):
            sl = slice(g * 512, (g + 1) * 512)
            zz = z_ref[:, sl]
            sg = _sigmoid(zz)
            zg = zz * sg
            yv = y_ref[:, sl]
            yg = yv * zg
            rstd = lax.rsqrt(jnp.mean(yg * yg, axis=-1, keepdims=True) + EPS)
            xh = yg * rstd
            d_out = dya_ref[:, sl]
            dnw_ref[:, sl] += jnp.sum(d_out * xh, axis=0, keepdims=True)
            dyn = d_out * nw_ref[:, sl]
            dyg = rstd * (dyn - xh * jnp.mean(dyn * xh, axis=-1, keepdims=True))
            dy_ref[:, sl] = dyg * zg
            dz_ref[:, sl] = (dyg * yv * (sg * (1.0 + zz * (1.0 - sg)))).astype(dz_ref.dtype)

        dtin = dtr_ref[...] + dtb_ref[...]
        dt = _softplus(dtin)
        a_neg = -jnp.exp(alog_ref[...])
        cum = _cumsum_rows(dt * a_neg)
        cum_ref[...] = cum
        cumT_ref[...] = cum.T
        last_all = cum_ref[L - 1:L, :]
        prev_t = prev_ref[0]
        dn_t = dst_ref[...]
        dcum = jnp.zeros((L, 128), F32)
        ddt = jnp.zeros((L, 128), F32)
        for g in range(2):
            gsl = slice(g * 512, (g + 1) * 512)
            bg = xc_ref[:, 1024 + g * 128:1152 + g * 128]
            cg = xc_ref[:, 1280 + g * 128:1408 + g * 128]
            gmat = _dot(cg, bg, NT)
            gmat_t = _dot(bg, cg, NT)
            pg = prev_t[:, gsl]
            zmat = _dot(cg, pg)
            dgm = jnp.zeros((L, L), F32)
            dgm_t = jnp.zeros((L, L), F32)
            db_acc = jnp.zeros((L, NSTATE), F32)
            dz_parts, cd_parts = [], []
            for jp in range(4):
                j = g * 4 + jp
                sl = slice(j * 128, (j + 1) * 128)
                xp = xc_ref[:, sl]
                dyp = dy_ref[:, sl]
                cc = [_colsel(cum, lane, 2 * j), _colsel(cum, lane, 2 * j + 1)]
                lc = [_colsel(last_all, lane1, 2 * j), _colsel(last_all, lane1, 2 * j + 1)]
                cum_l = jnp.where(lo, cc[0], cc[1])
                dt_l = jnp.where(lo, _colsel(dt, lane, 2 * j), _colsel(dt, lane, 2 * j + 1))
                last_l = jnp.where(lo1, lc[0], lc[1])
                e_l = jnp.exp(cum_l)
                dte_l = jnp.exp(last_l - cum_l)
                cd_l = jnp.exp(last_l)
                cd_parts.append(cd_l)
                xd = xp * dt_l
                dskacc_ref[:, sl] += jnp.sum(dyp * xp, axis=0, keepdims=True)
                dxp = dsk_ref[:, sl] * dyp
                t = dyp * (e_l * zmat[:, jp * 128:(jp + 1) * 128])
                dcc = [jnp.sum(jnp.where(lo, t, 0.0), axis=1, keepdims=True),
                       jnp.sum(jnp.where(lo, 0.0, t), axis=1, keepdims=True)]
                dz_parts.append(e_l * dyp)
                dnp_ = dn_t[:, sl]
                t2 = jnp.sum(dnp_ * prev_t[:, sl], axis=0, keepdims=True)
                dcd = [jnp.sum(jnp.where(lo1, t2, 0.0), axis=1, keepdims=True),
                       jnp.sum(jnp.where(lo1, 0.0, t2), axis=1, keepdims=True)]
                wm = _dot(bg, dnp_)
                dxd = wm * dte_l
                t3 = wm * xd
                ddte = [jnp.sum(jnp.where(lo, t3, 0.0), axis=1, keepdims=True),
                        jnp.sum(jnp.where(lo, 0.0, t3), axis=1, keepdims=True)]
                db_acc = db_acc + _dot(xd * dte_l, dnp_, NT)
                for hh in range(2):
                    h = 2 * j + hh
                    half = lo if hh == 0 else jnp.logical_not(lo)
                    row = cumT_ref[h:h + 1, :]
                    dm = jnp.where(tril, jnp.exp(jnp.where(tril, cc[hh] - row, 0.0)), 0.0)
                    dm_t = jnp.where(triu, jnp.exp(jnp.where(triu, row - cc[hh], 0.0)), 0.0)
                    m = gmat * dm
                    m_t = gmat_t * dm_t
                    dym = jnp.where(half, dyp, 0.0)
                    d_m = _dot(dym, xd, NT)
                    d_mt = _dot(xd, dym, NT)
                    dxd = dxd + _dot(m_t, dym)
                    dcc[hh] = dcc[hh] + jnp.sum(d_m * m, axis=1, keepdims=True) - jnp.sum(d_mt * m_t, axis=1, keepdims=True)
                    dgm = dgm + d_m * dm
                    dgm_t = dgm_t + d_mt * dm_t
                    dte_c = jnp.exp(lc[hh] - cc[hh])
                    dcc[hh] = dcc[hh] - ddte[hh] * dte_c
                    endc = dcd[hh] * jnp.exp(lc[hh]) + jnp.sum(ddte[hh] * dte_c, axis=0, keepdims=True)
                    dcc[hh] = dcc[hh] + jnp.where(is_last, endc, 0.0)
                    dcum = jnp.where(lane == h, dcc[hh], dcum)
                dxc_ref[:, sl] = dxp + dxd * dt_l
                t4 = dxd * xp
                ddt = jnp.where(lane == 2 * j, jnp.sum(jnp.where(lo, t4, 0.0), axis=1, keepdims=True), ddt)
                ddt = jnp.where(lane == 2 * j + 1, jnp.sum(jnp.where(lo, 0.0, t4), axis=1, keepdims=True), ddt)
            dzg = jnp.concatenate(dz_parts, axis=1)
            dst_ref[:, gsl] = dn_t[:, gsl] * jnp.concatenate(cd_parts, axis=1) + _dot(cg.T, dzg)
            dxc_ref[:, 1280 + g * 128:1408 + g * 128] = _dot(dgm, bg) + _dot(dzg, pg, NT)
            dxc_ref[:, 1024 + g * 128:1152 + g * 128] = _dot(dgm_t, cg) + db_acc
        dla = _suffix_sum_rows(dcum)
        ddt = ddt + dla * a_neg
        dalog_ref[...] += jnp.sum(dla * dt, axis=0, keepdims=True) * a_neg
        ddtr = jnp.where(lane < 16, ddt * _sigmoid(dtin), 0.0)
        ddtr_ref[...] = ddtr.astype(ddtr_ref.dtype)
        ddtb_ref[...] += jnp.sum(ddtr, axis=0, keepdims=True)

        @pl.when(i == nc - 1)
        def _():
            seg = (_iota((1024, 128), 0) // 64 == _iota((1024, 128), 1)).astype(F32)
            acc8 = jnp.broadcast_to(dskacc_ref[...], (8, 1024))
            ddsk_ref[...] = lax.dot_general(acc8, seg, NN, precision=lax.Precision.HIGHEST,
                                            preferred_element_type=F32)

    rev = lambda c: pl.BlockSpec((L, c), lambda i: (nc - 1 - i, 0))
    vec = lambda c: pl.BlockSpec((1, c), lambda i: (0, 0))
    return pl.pallas_call(
        body, name="ssd_bwd", grid=(nc,),
        in_specs=[rev(1024), rev(1024), rev(1024), rev(1536), rev(128),
                  pl.BlockSpec((1, NSTATE, 1024), lambda i: (nc - 1 - i, 0, 0)),
                  vec(128), vec(128), vec(1024), vec(1024)],
        out_specs=[rev(1024), rev(1536), rev(128), vec(1024), pl.BlockSpec((8, 128), lambda i: (0, 0)),
                   vec(128), vec(128)],
        out_shape=[jax.ShapeDtypeStruct((S, 1024), BF16), jax.ShapeDtypeStruct((S, 1536), F32),
                   jax.ShapeDtypeStruct((S, 128), BF16), jax.ShapeDtypeStruct((1, 1024), F32),
                   jax.ShapeDtypeStruct((8, 128), F32), jax.ShapeDtypeStruct((1, 128), F32),
                   jax.ShapeDtypeStruct((1, 128), F32)],
        scratch_shapes=[pltpu.VMEM((NSTATE, 1024), F32), pltpu.VMEM((L, 128), F32), pltpu.VMEM((L, 128), F32),
                        pltpu.VMEM((L, 1024), F32), pltpu.VMEM((1, 1024), F32)],
        compiler_params=_params(("arbitrary",)))(dya, y, z, xc, dtr, prev, dtb, alog, dskl, nw)


def _layer_norm_parts(vg):
    mu = jnp.mean(vg, axis=-1, keepdims=True)
    vc = vg - mu
    rstd = lax.rsqrt(jnp.mean(vc * vc, axis=-1, keepdims=True) + EPS)
    return vc * rstd, rstd


def _gmlp_fwd(u, v, lnw, lnb, ws, bse, tb=512):
    S = u.shape[0]
    tb = min(tb, S)

    def body(u_ref, v_ref, lnw_ref, lnb_ref, ws_ref, bse_ref, o_ref, vn_ref):
        tril = _iota((L, L), 0) >= _iota((L, L), 1)
        xh, _ = _layer_norm_parts(_gelu(v_ref[...]))
        vn_ref[...] = xh * lnw_ref[...] + lnb_ref[...]
        for g in range(8):
            w = jnp.where(tril, ws_ref[g], 0.0)
            gs = slice(g * 128, (g + 1) * 128)
            for ch in range(tb // L):
                rs = slice(ch * L, (ch + 1) * L)
                sv = _dot(w, vn_ref[rs, gs]) + bse_ref[g]
                o_ref[rs, gs] = (_gelu(u_ref[rs, gs]) * sv).astype(o_ref.dtype)

    blk = pl.BlockSpec((tb, 1024), lambda i: (i, 0))
    vec = pl.BlockSpec((1, 1024), lambda i: (0, 0))
    cube = pl.BlockSpec((8, L, 128), lambda i: (0, 0, 0))
    return pl.pallas_call(
        body, name="gmlp_fwd", grid=(S // tb,), in_specs=[blk, blk, vec, vec, cube, cube], out_specs=blk,
        out_shape=jax.ShapeDtypeStruct((S, 1024), BF16), scratch_shapes=[pltpu.VMEM((tb, 1024), F32)],
        compiler_params=_params(("parallel",)))(u, v, lnw, lnb, ws, bse)


def _gmlp_bwd(dyb, u, v, lnw, lnb, ws, bse, tb=512):
    S = u.shape[0]
    tb = min(tb, S)

    def body(d_ref, u_ref, v_ref, lnw_ref, lnb_ref, ws_ref, bse_ref,
             du_ref, dv_ref, dws_ref, dbse_ref, dlnw_ref, dlnb_ref, vn_ref, dvn_ref):
        @pl.when(pl.program_id(0) == 0)
        def _():
            dws_ref[...] = jnp.zeros_like(dws_ref)
            dbse_ref[...] = jnp.zeros_like(dbse_ref)
            dlnw_ref[...] = jnp.zeros_like(dlnw_ref)
            dlnb_ref[...] = jnp.zeros_like(dlnb_ref)

        tril = _iota((L, L), 0) >= _iota((L, L), 1)
        vv = v_ref[...]
        xh, rstd = _layer_norm_parts(_gelu(vv))
        vn_ref[...] = xh * lnw_ref[...] + lnb_ref[...]
        for g in range(8):
            w = jnp.where(tril, ws_ref[g], 0.0)
            w_t = w.T
            gs = slice(g * 128, (g + 1) * 128)
            dw = jnp.zeros((L, L), F32)
            dbs = jnp.zeros((L, 128), F32)
            for ch in range(tb // L):
                rs = slice(ch * L, (ch + 1) * L)
                vn = vn_ref[rs, gs]
                sv = _dot(w, vn) + bse_ref[g]
                uu = u_ref[rs, gs]
                dd = d_ref[rs, gs]
                du_ref[rs, gs] = (dd * sv * _gelu_grad(uu)).astype(du_ref.dtype)
                dsv = dd * _gelu(uu)
                dw = dw + _dot(dsv, vn, NT)
                dbs = dbs + dsv
                dvn_ref[rs, gs] = _dot(w_t, dsv)
            dws_ref[g] += jnp.where(tril, dw, 0.0)
            dbse_ref[g] += dbs
        dvn = dvn_ref[...]
        dlnw_ref[...] += jnp.sum(dvn * xh, axis=0, keepdims=True)
        dlnb_ref[...] += jnp.sum(dvn, axis=0, keepdims=True)
        dxh = dvn * lnw_ref[...]
        dvg = rstd * (dxh - jnp.mean(dxh, axis=-1, keepdims=True) - xh * jnp.mean(dxh * xh, axis=-1, keepdims=True))
        dv_ref[...] = (dvg * _gelu_grad(vv)).astype(dv_ref.dtype)

    blk = pl.BlockSpec((tb, 1024), lambda i: (i, 0))
    vec = pl.BlockSpec((1, 1024), lambda i: (0, 0))
    cube = pl.BlockSpec((8, L, 128), lambda i: (0, 0, 0))
    return pl.pallas_call(
        body, name="gmlp_bwd", grid=(S // tb,), in_specs=[blk, blk, blk, vec, vec, cube, cube],
        out_specs=[blk, blk, cube, cube, vec, vec],
        out_shape=[jax.ShapeDtypeStruct((S, 1024), BF16), jax.ShapeDtypeStruct((S, 1024), BF16),
                   jax.ShapeDtypeStruct((8, L, 128), F32), jax.ShapeDtypeStruct((8, L, 128), F32),
                   jax.ShapeDtypeStruct((1, 1024), F32), jax.ShapeDtypeStruct((1, 1024), F32)],
        scratch_shapes=[pltpu.VMEM((tb, 1024), F32), pltpu.VMEM((tb, 1024), F32)],
        compiler_params=_params(("arbitrary",)))(dyb, u, v, lnw, lnb, ws, bse)


def _lane_sum(name, a):
    def body(a_ref, o_ref):
        o_ref[...] = jnp.sum(a_ref[...], axis=1, keepdims=True)
    return pl.pallas_call(body, name=name, out_shape=jax.ShapeDtypeStruct((a.shape[0], 1), F32))(a)


def _bucket_onehot_t():
    qi = np.arange(L)[:, None]
    sj = np.arange(2 * L)[None, :]
    dist = np.maximum(qi + L - sj, 0)
    log_ratio = (np.log(np.maximum(dist, 1).astype(np.float32) / np.float32(16)) / np.float32(math.log(128 / 16)))
    large = 16 + (log_ratio.astype(np.float32) * np.float32(16)).astype(np.int32)
    bucket = np.where(dist < 16, dist, np.minimum(large, 31)).reshape(-1)
    return (np.arange(32)[:, None] == bucket[None, :]).astype(np.float32)


def _rel_bias(table_t, onehot_t):
    def body(t_ref, oh_ref, o_ref):
        o_ref[...] = lax.dot_general(t_ref[...], oh_ref[...], NN, precision=lax.Precision.HIGHEST,
                                     preferred_element_type=F32)
    return pl.pallas_call(body, name="rel_bias", out_shape=jax.ShapeDtypeStruct((16, L * 2 * L), F32),
                          compiler_params=_params())(table_t, onehot_t)


def _rel_bias_bwd(dbias, onehot_t):
    def body(d_ref, oh_ref, o_ref):
        o_ref[...] = lax.dot_general(d_ref[...], oh_ref[...], NT, precision=lax.Precision.HIGHEST,
                                     preferred_element_type=F32)
    return pl.pallas_call(body, name="rel_bias_bwd", out_shape=jax.ShapeDtypeStruct((16, 32), F32),
                          compiler_params=_params())(dbias, onehot_t)


def _band(kp, kc, lo):
    kk = jnp.concatenate([kp, kc], axis=0)
    kr = pltpu.roll(kk, 64, axis=1)
    return [jnp.where(lo, kk, kr), jnp.where(lo, kr, kk)]


def _attn_mask(i):
    qi, sj = _iota((L, 2 * L), 0), _iota((L, 2 * L), 1)
    rel = qi + L - sj
    return (rel >= 0) & (rel < L) & ((sj >= L) | (i > 0))


SMEM = pl.BlockSpec(memory_space=pltpu.SMEM)


def _attn_fwd(qkv, bias, sinks):
    S = qkv.shape[0]
    nb = S // L
    scale = 64 ** -0.5

    def body(sink_ref, q_ref, kc_ref, vc_ref, kp_ref, vp_ref, bias_ref, o_ref, lse_ref):
        i = pl.program_id(0)
        lane = _iota((L, 128), 1)
        lo = lane < 64
        lo2 = _iota((2 * L, 128), 1) < 64
        mask = _attn_mask(i)
        kd = _band(kp_ref[...], kc_ref[...], lo2)
        vd = _band(vp_ref[...], vc_ref[...], lo2)
        lse = jnp.zeros((L, 128), F32)
        for pr in range(8):
            sl = slice(pr * 128, (pr + 1) * 128)
            qp = q_ref[:, sl]
            j = pr // 4
            outs = []
            for hh in range(2):
                h = 2 * pr + hh
                qm = jnp.where(lo if hh == 0 else jnp.logical_not(lo), qp, 0.0)
                lg = jnp.where(mask, _dot(qm, kd[j], NT) * scale + bias_ref[h], NEG_INF)
                s = sink_ref[h]
                m = jnp.maximum(jnp.max(lg, axis=1, keepdims=True), s)
                p = jnp.where(mask, jnp.exp(lg - m), 0.0)
                den = jnp.sum(p, axis=1, keepdims=True) + jnp.exp(s - m)
                outs.append(_dot(p / den, vd[j]))
                lse = jnp.where(lane == h, m + jnp.log(den), lse)
            o_ref[:, sl] = jnp.where(lo, outs[0], outs[1]).astype(o_ref.dtype)
        lse_ref[...] = lse

    prev = lambda col: pl.BlockSpec((L, 128), lambda i: (jnp.maximum(i - 1, 0), col))
    cur = lambda col: pl.BlockSpec((L, 128), lambda i: (i, col))
    return pl.pallas_call(
        body, name="attn_fwd", grid=(nb,),
        in_specs=[SMEM, pl.BlockSpec((L, 1024), lambda i: (i, 0)), cur(8), cur(9), prev(8), prev(9),
                  pl.BlockSpec((16, L, 2 * L), lambda i: (0, 0, 0))],
        out_specs=[pl.BlockSpec((L, 1024), lambda i: (i, 0)), pl.BlockSpec((L, 128), lambda i: (i, 0))],
        out_shape=[jax.ShapeDtypeStruct((S, 1024), BF16), jax.ShapeDtypeStruct((S, 128), F32)],
        compiler_params=_params(("parallel",)))(sinks, qkv, qkv, qkv, qkv, qkv, bias)


def _attn_bwd(qkv, d_o, lse, bias, sinks):
    S = qkv.shape[0]
    nb = S // L
    scale = 64 ** -0.5

    def body(sink_ref, q_ref, kc_ref, vc_ref, kp_ref, vp_ref, do_ref, lse_ref, bias_ref,
             dq_ref, dkv_ref, dbias_ref, dsink_ref, dbq_ref, dbkv_ref, carry_ref):
        i = pl.program_id(0)

        @pl.when(i == 0)
        def _():
            dbias_ref[...] = jnp.zeros_like(dbias_ref)
            dsink_ref[...] = jnp.zeros_like(dsink_ref)
            dbq_ref[...] = jnp.zeros_like(dbq_ref)
            dbkv_ref[...] = jnp.zeros_like(dbkv_ref)
            carry_ref[...] = jnp.zeros_like(carry_ref)

        @pl.when(i < nb)
        def _():
            lane = _iota((L, 128), 1)
            lane1 = _iota((1, 128), 1)
            lo = lane < 64
            lo2 = _iota((2 * L, 128), 1) < 64
            mask = _attn_mask(i)
            kd = _band(kp_ref[...], kc_ref[...], lo2)
            vd = _band(vp_ref[...], vc_ref[...], lo2)
            lse_all = lse_ref[...]
            acc_k = [jnp.zeros((2 * L, 128), F32), jnp.zeros((2 * L, 128), F32)]
            acc_v = [jnp.zeros((2 * L, 128), F32), jnp.zeros((2 * L, 128), F32)]
            dsink = jnp.zeros((1, 128), F32)
            for pr in range(8):
                sl = slice(pr * 128, (pr + 1) * 128)
                qp = q_ref[:, sl]
                dop = do_ref[:, sl]
                j = pr // 4
                dqs = []
                for hh in range(2):
                    h = 2 * pr + hh
                    half = lo if hh == 0 else jnp.logical_not(lo)
                    qm = jnp.where(half, qp, 0.0)
                    dom = jnp.where(half, dop, 0.0)
                    lse_h = _colsel(lse_all, lane, h)
                    lg = _dot(qm, kd[j], NT) * scale + bias_ref[h]
                    p = jnp.where(mask, jnp.exp(jnp.where(mask, lg, NEG_INF) - lse_h), 0.0)
                    dp = _dot(dom, vd[j], NT)
                    delta = jnp.sum(p * dp, axis=1, keepdims=True)
                    ds = p * (dp - delta)
                    dbias_ref[h] += ds
                    ds_sink = jnp.sum(-jnp.exp(sink_ref[h] - lse_h) * delta, axis=0, keepdims=True)
                    dsink = dsink + jnp.where(lane1 == h, ds_sink, 0.0)
                    dss = ds * scale
                    dqs.append(_dot(dss, kd[j]))
                    acc_k[j] = acc_k[j] + _dot(dss, qm, TN)
                    acc_v[j] = acc_v[j] + _dot(p, dom, TN)
                dq = jnp.where(lo, dqs[0], dqs[1])
                dq_ref[:, sl] = dq.astype(dq_ref.dtype)
                dbq_ref[:, sl] += jnp.sum(dq, axis=0, keepdims=True)
            dsink_ref[...] += dsink
            tot_k = [a + pltpu.roll(a, 64, axis=1) for a in acc_k]
            tot_v = [a + pltpu.roll(a, 64, axis=1) for a in acc_v]
            dkv = jnp.concatenate([jnp.where(lo2, tot_k[0], tot_k[1]), jnp.where(lo2, tot_v[0], tot_v[1])], axis=1)
            dbkv_ref[...] += jnp.sum(dkv, axis=0, keepdims=True)
            dkv_ref[...] = (carry_ref[...] + dkv[:L, :]).astype(dkv_ref.dtype)
            carry_ref[...] = dkv[L:, :]

        @pl.when(i == nb)
        def _():
            dkv_ref[...] = carry_ref[...].astype(dkv_ref.dtype)

    c = lambda i: jnp.minimum(i, nb - 1)
    prev = lambda col: pl.BlockSpec((L, 128), lambda i: (jnp.maximum(c(i) - 1, 0), col))
    cur = lambda col: pl.BlockSpec((L, 128), lambda i: (c(i), col))
    row = lambda w: pl.BlockSpec((L, w), lambda i: (c(i), 0))
    cube = pl.BlockSpec((16, L, 2 * L), lambda i: (0, 0, 0))
    vec = lambda w: pl.BlockSpec((1, w), lambda i: (0, 0))
    return pl.pallas_call(
        body, name="attn_bwd", grid=(nb + 1,),
        in_specs=[SMEM, row(1024), cur(8), cur(9), prev(8), prev(9), row(1024), row(128), cube],
        out_specs=[row(1024), pl.BlockSpec((L, 256), lambda i: (jnp.maximum(i - 1, 0), 0)), cube,
                   vec(128), vec(1024), vec(256)],
        out_shape=[jax.ShapeDtypeStruct((S, 1024), BF16), jax.ShapeDtypeStruct((S, 256), BF16),
                   jax.ShapeDtypeStruct((16, L, 2 * L), F32), jax.ShapeDtypeStruct((1, 128), F32),
                   jax.ShapeDtypeStruct((1, 1024), F32), jax.ShapeDtypeStruct((1, 256), F32)],
        scratch_shapes=[pltpu.VMEM((L, 256), F32)],
        compiler_params=_params(("arbitrary",)))(sinks, qkv, qkv, qkv, qkv, qkv, d_o, lse, bias)


def _pad_lanes(a, n=128):
    return jnp.pad(a, ((0, 0), (0, n - a.shape[1])))


def _local_step(x, tgt, mod, W, P):
    md = [[mod[l:l + 1, k * D:(k + 1) * D] for k in range(6)] for l in range(2)]
    G, g = {}, {}

    sh1, sc1, g1, sh2, sc2, g2 = md[0]
    nmw0, nfw0 = P["norm_mix_w"][0:1], P["norm_ffn_w"][0:1]
    h0 = _norm_mod_fwd("norm_mix_0", x, nmw0, sc1, sh1)
    w_in = W["in_w"]
    segs = {"z": w_in[:, 0:1024], "xbc": w_in[:, 1024:2560], "dt": _pad_lanes(w_in[:, 2560:2576]),
            "u": w_in[:, 2576:3600], "v": w_in[:, 3600:4624]}
    proj = {k: _mm(f"in_proj_{k}", [h0], [w], "nn", [F32])[0] for k, w in segs.items()}
    conv_w, conv_b = P["conv_w"][0], P["conv_b"]
    pre, xc = _conv_fwd(proj["xbc"], conv_w, conv_b)
    dtb, alog = _pad_lanes(P["dt_bias"]), _pad_lanes(P["a_log"])
    dskl = jnp.repeat(P["d_skip"], 64, axis=1)
    ya, y_ssd, prev = _ssd_fwd(xc, proj["dt"], proj["z"], dtb, alog, dskl, P["ssm_norm_w"])
    ws = P["gmlp_ws"][0]
    bse = jnp.broadcast_to(P["gmlp_bs"][0][:, :, None], (8, L, 128))
    yb = _gmlp_fwd(proj["u"], proj["v"], P["gmlp_ln_w"], P["gmlp_ln_b"], ws, bse)
    w_oa, w_ob = W["out_w"][:1024], W["out_w"][1024:]

    def res(y, x, gate):
        return y, x + gate * y
    mix0, x1 = _mm("out_proj_0", [ya, yb], [w_oa, w_ob], "nn", [F32, F32], epi=res, extras=[x], vecs=[g1])
    h0f = _norm_mod_fwd("norm_ffn_0", x1, nfw0, sc2, sh2)
    a0, b0, f0, y0, x2 = _ffn_fwd("0", h0f, W["gate_w"][0], W["up_w"][0], W["down_w"][0], x1, g2)

    sh1b, sc1b, g1b, sh2b, sc2b, g2b = md[1]
    nmw1, nfw1 = P["norm_mix_w"][1:2], P["norm_ffn_w"][1:2]
    h1 = _norm_mod_fwd("norm_mix_1", x2, nmw1, sc1b, sh1b)
    qkv = _mm("qkv_proj", [h1], [W["qkv_w"]], "nn", [F32], epi=lambda acc, b: acc + b, vecs=[P["qkv_b"]])[0]
    onehot_t = jnp.asarray(_bucket_onehot_t())
    bias = _rel_bias(P["rel_table"].T, onehot_t).reshape(16, L, 2 * L)
    sinks = P["sinks"].reshape(16)
    att, lse = _attn_fwd(qkv, bias, sinks)

    def res_b(y, x, gate, b):
        y = y + b
        return y, x + gate * y
    mix1, x3 = _mm("o_proj", [att], [W["o_w"]], "nn", [F32, F32], epi=res_b, extras=[x2], vecs=[g1b, P["o_b"]])
    h1f = _norm_mod_fwd("norm_ffn_1", x3, nfw1, sc2b, sh2b)
    a1, b1, f1, y1, x4 = _ffn_fwd("1", h1f, W["gate_w"][1], W["up_w"][1], W["down_w"][1], x3, g2b)

    dx, sq, g["final_norm_w"] = _loss_head(x4, tgt, P["final_norm_w"])

    dh, dg2b, dwg1, dwu1, dwd1 = _ffn_bwd("1", dx, h1f, a1, b1, f1, y1, W["gate_w"][1], W["up_w"][1],
                                          W["down_w"][1], g2b)
    dx, dsh2b, dsc2b, dnfw1 = _norm_mod_bwd("norm_ffn_bwd_1", x3, dh, dx, nfw1, sc2b)
    dmix, dg1b, g["o_b"] = _gate_bwd("mix_gate_bwd_1", dx, mix1, g1b)
    G["o_w"] = _mm_tn("o_dw", att, dmix)
    d_att = _mm("o_dx", [dmix], [W["o_w"]], "nt", [F32])[0]
    dq, dkv, dbias, dsinks, dbq, dbkv = _attn_bwd(qkv, d_att, lse, bias, sinks)
    g["rel_table"] = _rel_bias_bwd(dbias.reshape(16, L * 2 * L), onehot_t).T
    g["sinks"] = dsinks[:, :16]
    g["qkv_b"] = jnp.concatenate([dbq, dbkv], axis=1)
    w_q, w_kv = W["qkv_w"][:, :1024], W["qkv_w"][:, 1024:]
    G["qkv_w"] = jnp.concatenate([_mm_tn("qkv_dwq", h1, dq), _mm_tn("qkv_dwkv", h1, dkv)], axis=1)
    dh = _mm("qkv_dx", [dq, dkv], [w_q, w_kv], "nt", [F32])[0]
    dx, dsh1b, dsc1b, dnmw1 = _norm_mod_bwd("norm_mix_bwd_1", x2, dh, dx, nmw1, sc1b)

    dh, dg2, dwg0, dwu0, dwd0 = _ffn_bwd("0", dx, h0f, a0, b0, f0, y0, W["gate_w"][0], W["up_w"][0],
                                         W["down_w"][0], g2)
    dx, dsh2, dsc2, dnfw0 = _norm_mod_bwd("norm_ffn_bwd_0", x1, dh, dx, nfw0, sc2)
    dmix, dg1, _ = _gate_bwd("mix_gate_bwd_0", dx, mix0, g1)
    G["out_w"] = jnp.concatenate([_mm_tn("out_dwa", ya, dmix), _mm_tn("out_dwb", yb, dmix)], axis=0)
    dya = _mm("out_dxa", [dmix], [w_oa], "nt", [F32])[0]
    dyb = _mm("out_dxb", [dmix], [w_ob], "nt", [F32])[0]
    du, dv, dws, dbse, g["gmlp_ln_w"], g["gmlp_ln_b"] = _gmlp_bwd(dyb, proj["u"], proj["v"], P["gmlp_ln_w"],
                                                                 P["gmlp_ln_b"], ws, bse)
    g["gmlp_ws"] = dws[None]
    g["gmlp_bs"] = _lane_sum("gmlp_dbs", dbse.reshape(8 * L, 128)).reshape(1, 8, L)
    dz, dxc, ddt, g["ssm_norm_w"], ddsk, dalog, ddtb = _ssd_bwd(dya, y_ssd, proj["z"], xc, proj["dt"], prev,
                                                                dtb, alog, dskl, P["ssm_norm_w"])
    g["d_skip"], g["a_log"], g["dt_bias"] = ddsk[0:1, :16], dalog[:, :16], ddtb[:, :16]
    dxr, dconv_w, g["conv_b"] = _conv_bwd(dxc, pre, proj["xbc"], conv_w)
    g["conv_w"] = dconv_w[None]
    dsegs = {"z": dz, "xbc": dxr, "dt": ddt, "u": du, "v": dv}
    dws_in = {k: _mm_tn(f"in_dw_{k}", h0, d) for k, d in dsegs.items()}
    G["in_w"] = jnp.concatenate([dws_in["z"], dws_in["xbc"], dws_in["dt"][:, :16], dws_in["u"], dws_in["v"]], axis=1)
    keys = ["z", "xbc", "dt", "u", "v"]
    dh = _mm("in_dx", [dsegs[k] for k in keys], [segs[k] for k in keys], "nt", [F32])[0]
    dx, dsh1, dsc1, dnmw0 = _norm_mod_bwd("norm_mix_bwd_0", x, dh, dx, nmw0, sc1)

    G["gate_w"] = jnp.stack([dwg0, dwg1])
    G["up_w"] = jnp.stack([dwu0, dwu1])
    G["down_w"] = jnp.stack([dwd0, dwd1])
    g["norm_mix_w"] = jnp.concatenate([dnmw0, dnmw1], axis=0)
    g["norm_ffn_w"] = jnp.concatenate([dnfw0, dnfw1], axis=0)
    dmod = jnp.concatenate([jnp.concatenate([dsh1, dsc1, dg1, dsh2, dsc2, dg2], axis=1),
                            jnp.concatenate([dsh1b, dsc1b, dg1b, dsh2b, dsc2b, dg2b], axis=1)], axis=0)
    return sq, dx, dmod, G, g


def _ada_fwd(c_all, ada_w, ada_b):
    n = ada_w.shape[2]
    tn = _col_tile(n, 512)

    def body(c_ref, w_ref, b_ref, o_ref):
        cc = c_ref[...]
        o_ref[...] = lax.dot_general(cc * _sigmoid(cc), w_ref[...], NN, precision=lax.Precision.HIGHEST,
                                     preferred_element_type=F32) + b_ref[...]

    return pl.pallas_call(
        body, name="ada_fwd", grid=(2, n // tn),
        in_specs=[pl.BlockSpec((8, D), lambda l, j: (0, 0)), pl.BlockSpec((None, D, tn), lambda l, j: (l, 0, j)),
                  pl.BlockSpec((None, 1, tn), lambda l, j: (l, 0, j))],
        out_specs=pl.BlockSpec((None, 8, tn), lambda l, j: (l, 0, j)),
        out_shape=jax.ShapeDtypeStruct((2, 8, n), F32), compiler_params=_params(("parallel", "parallel")))(
            c_all, ada_w, ada_b)


def _ada_bwd(c_all, dmod_cols, dmod_all):
    n = dmod_cols.shape[2]
    tn = _col_tile(n, 512)

    def body(c_ref, d_ref, o_ref):
        cc = c_ref[...]
        o_ref[...] = lax.dot_general(cc * _sigmoid(cc), d_ref[...], TN, precision=lax.Precision.HIGHEST,
                                     preferred_element_type=F32)

    dw = pl.pallas_call(
        body, name="ada_dw", grid=(2, n // tn),
        in_specs=[pl.BlockSpec((8, D), lambda l, j: (0, 0)), pl.BlockSpec((None, 8, tn), lambda l, j: (l, 0, j))],
        out_specs=pl.BlockSpec((None, D, tn), lambda l, j: (l, 0, j)),
        out_shape=jax.ShapeDtypeStruct((2, D, n), F32), compiler_params=_params(("parallel", "parallel")))(
            c_all, dmod_cols)

    def sum_body(d_ref, o_ref):
        o_ref[...] = jnp.sum(d_ref[...], axis=0, keepdims=True)

    db = pl.pallas_call(
        sum_body, name="ada_db", grid=(2,),
        in_specs=[pl.BlockSpec((None, 8, 6 * D), lambda l: (l, 0, 0))],
        out_specs=pl.BlockSpec((None, 1, 6 * D), lambda l: (l, 0, 0)),
        out_shape=jax.ShapeDtypeStruct((2, 1, 6 * D), F32), compiler_params=_params(("parallel",)))(dmod_all)
    return dw, db


def _row_tile(rows, cap=512):
    best = rows
    for t in range(8, min(rows, cap) + 1, 8):
        if rows % t == 0:
            best = t
    return best


def _adamw(name, w, g, m, v):
    def fn(w, g, m, v):
        m = ADAM_B1 * m + (1.0 - ADAM_B1) * g
        v = ADAM_B2 * v + (1.0 - ADAM_B2) * (g * g)
        m_hat = m / (1.0 - ADAM_B1 ** ADAM_STEP)
        v_hat = v / (1.0 - ADAM_B2 ** ADAM_STEP)
        return -ADAM_LR * (m_hat / (jnp.sqrt(v_hat) + ADAM_EPS) + ADAM_WD * w), m, v
    cols = w.shape[1]
    return _rowwise(name, fn, [w, g, m, v], [], [(cols, F32)] * 3, tr=_row_tile(w.shape[0]))


def _place():
    return lax.axis_index("x"), lax.axis_index("y"), lax.axis_index("c")


VMEM_SPEC = pl.BlockSpec(memory_space=pltpu.VMEM)


def _allreduce_small(name, buf):
    rows = buf.shape[0]

    def body(x_ref, o_ref, stage, send_sems, recv_sems):
        x, y, c = _place()
        me = 4 * x + 2 * y + c
        stage[me] = x_ref[...]
        copies = []
        for k in range(1, 8):
            peer = (1 - x if k & 4 else x, 1 - y if k & 2 else y, 1 - c if k & 1 else c)
            cp = pltpu.make_async_remote_copy(src_ref=x_ref, dst_ref=stage.at[me], send_sem=send_sems.at[k - 1],
                                              recv_sem=recv_sems.at[k - 1], device_id=peer, device_id_type=MESH)
            cp.start()
            copies.append(cp)
        for cp in copies:
            cp.wait()
        acc = stage[0]
        for d in range(1, 8):
            acc = acc + stage[d]
        o_ref[...] = acc

    return pl.pallas_call(
        body, name=name, in_specs=[VMEM_SPEC], out_specs=VMEM_SPEC,
        out_shape=jax.ShapeDtypeStruct((rows, 128), F32),
        scratch_shapes=[pltpu.VMEM((8, rows, 128), F32), pltpu.SemaphoreType.DMA((7,)), pltpu.SemaphoreType.DMA((7,))],
        compiler_params=pltpu.CompilerParams(vmem_limit_bytes=_VMEM_LIMIT))(buf)


OTHER_CHIPS = ((1, 0), (0, 1), (1, 1))


def _allgather_big(wp):
    rows = wp.shape[0]
    half = rows // 2

    def body(w_ref, o_ref, send_sems, recv_sems, local_sem):
        x, y, c = _place()
        k = 2 * x + y
        mine = pl.ds(pl.multiple_of(c * half, 8), half)
        theirs = pl.ds(pl.multiple_of((1 - c) * half, 8), half)
        local = pltpu.make_async_copy(w_ref, o_ref.at[k], local_sem)
        local.start()
        chips = [(1 - x if fx else x, 1 - y if fy else y) for fx, fy in OTHER_CHIPS]
        idx = [2 * px + py for px, py in chips]
        first = []
        for j, (px, py) in enumerate(chips):
            cp = pltpu.make_async_remote_copy(src_ref=w_ref.at[mine], dst_ref=o_ref.at[k, mine],
                                              send_sem=send_sems.at[j], recv_sem=recv_sems.at[j],
                                              device_id=(px, py, c), device_id_type=MESH)
            cp.start()
            first.append(cp)
        passed = []
        for j in range(3):
            blk = o_ref.at[idx[j], mine]
            pltpu.make_async_remote_copy(src_ref=blk, dst_ref=blk, send_sem=send_sems.at[j], recv_sem=recv_sems.at[j],
                                         device_id=(x, y, c), device_id_type=MESH).wait_recv()
            cp = pltpu.make_async_remote_copy(src_ref=blk, dst_ref=blk, send_sem=send_sems.at[3 + j],
                                              recv_sem=recv_sems.at[3 + j], device_id=(x, y, 1 - c),
                                              device_id_type=MESH)
            cp.start()
            passed.append(cp)
        for j in range(3):
            blk = o_ref.at[idx[j], theirs]
            pltpu.make_async_remote_copy(src_ref=blk, dst_ref=blk, send_sem=send_sems.at[3 + j],
                                         recv_sem=recv_sems.at[3 + j], device_id=(x, y, c),
                                         device_id_type=MESH).wait_recv()
        for cp in first + passed:
            cp.wait_send()
        local.wait()

    return pl.pallas_call(
        body, name="allgather_weights", in_specs=[ANY], out_specs=ANY,
        out_shape=jax.ShapeDtypeStruct((4, rows, 1024), wp.dtype),
        scratch_shapes=[pltpu.SemaphoreType.DMA((6,)), pltpu.SemaphoreType.DMA((6,)), pltpu.SemaphoreType.DMA])(wp)


def _sibling_swap(name, src, halves):
    half = src.shape[-2] // 2
    out_shape = (src.shape[0], half, 1024) if halves else src.shape

    def body(s_ref, o_ref, send_sem, recv_sem):
        x, y, c = _place()
        part = s_ref.at[:, pl.ds(pl.multiple_of((1 - c) * half, 8), half)] if halves else s_ref
        cp = pltpu.make_async_remote_copy(src_ref=part, dst_ref=o_ref, send_sem=send_sem, recv_sem=recv_sem,
                                          device_id=(x, y, 1 - c), device_id_type=MESH)
        cp.start()
        cp.wait()

    return pl.pallas_call(
        body, name=name, in_specs=[ANY], out_specs=ANY, out_shape=jax.ShapeDtypeStruct(out_shape, src.dtype),
        scratch_shapes=[pltpu.SemaphoreType.DMA, pltpu.SemaphoreType.DMA])(src)


def _chip_exchange(p):
    def body(p_ref, q_ref, send_sems, recv_sems, local_sem):
        x, y, c = _place()
        k = 2 * x + y
        local = pltpu.make_async_copy(p_ref.at[k], q_ref.at[k], local_sem)
        local.start()
        copies = []
        for j, (fx, fy) in enumerate(OTHER_CHIPS):
            px, py = (1 - x if fx else x), (1 - y if fy else y)
            cp = pltpu.make_async_remote_copy(src_ref=p_ref.at[2 * px + py], dst_ref=q_ref.at[k],
                                              send_sem=send_sems.at[j], recv_sem=recv_sems.at[j],
                                              device_id=(px, py, c), device_id_type=MESH)
            cp.start()
            copies.append(cp)
        for cp in copies:
            cp.wait()
        local.wait()

    return pl.pallas_call(
        body, name="grad_chip_exchange", in_specs=[ANY], out_specs=ANY,
        out_shape=jax.ShapeDtypeStruct(p.shape, p.dtype),
        scratch_shapes=[pltpu.SemaphoreType.DMA((3,)), pltpu.SemaphoreType.DMA((3,)), pltpu.SemaphoreType.DMA])(p)


def _pair_sum(g, r1, c):
    rows = g.shape[1]
    half = rows // 2
    th = _row_tile(half, 256)
    nblk = half // th

    def body(c_ref, g_ref, r_ref, o_ref):
        o_ref[...] = g_ref[...] + r_ref[...]

    spec = pl.BlockSpec((None, th, 1024), lambda k, i, c_ref: (k, i, 0))
    grid_spec = pltpu.PrefetchScalarGridSpec(
        num_scalar_prefetch=1, grid=(4, nblk),
        in_specs=[pl.BlockSpec((None, th, 1024), lambda k, i, c_ref: (k, c_ref[0] * nblk + i, 0)), spec],
        out_specs=spec)
    return pl.pallas_call(body, name="grad_pair_sum", grid_spec=grid_spec,
                          out_shape=jax.ShapeDtypeStruct((4, half, 1024), F32),
                          compiler_params=_params(("parallel", "parallel")))(c, g, r1)


def _chip_sum(q):
    half = q.shape[1]
    th = _row_tile(half, 256)

    def body(a, b, c, d, o_ref):
        o_ref[...] = ((a[...] + b[...]) + c[...]) + d[...]

    specs = [pl.BlockSpec((None, th, 1024), functools.partial(lambda i, k: (k, i, 0), k=k)) for k in range(4)]
    return pl.pallas_call(body, name="grad_chip_sum", grid=(half // th,), in_specs=specs,
                          out_specs=pl.BlockSpec((th, 1024), lambda i: (i, 0)),
                          out_shape=jax.ShapeDtypeStruct((half, 1024), F32),
                          compiler_params=_params(("parallel",)))(q, q, q, q)


def _join_halves(f, r, c):
    half = f.shape[0]
    th = _row_tile(half, 256)
    nblk = half // th

    def body(c_ref, f_ref, r_ref, o_ref):
        mine = (pl.program_id(0) == c_ref[0])
        o_ref[...] = jnp.where(mine, f_ref[...], r_ref[...])

    spec = pl.BlockSpec((th, 1024), lambda h, i, c_ref: (i, 0))
    grid_spec = pltpu.PrefetchScalarGridSpec(
        num_scalar_prefetch=1, grid=(2, nblk), in_specs=[spec, spec],
        out_specs=pl.BlockSpec((th, 1024), lambda h, i, c_ref: (h * nblk + i, 0)))
    return pl.pallas_call(body, name="grad_join_halves", grid_spec=grid_spec,
                          out_shape=jax.ShapeDtypeStruct((2 * half, 1024), F32),
                          compiler_params=_params(("parallel", "parallel")))(c, f, r)


BIG = ("in_w", "out_w", "qkv_w", "o_w", "gate_w", "up_w", "down_w")
BIG_ARG = {"in_w": "in_w_even", "out_w": "out_w_even", "qkv_w": "qkv_w", "o_w": "o_w", "gate_w": "ffn_gate_w",
           "up_w": "ffn_up_w", "down_w": "ffn_down_w"}
BIG_FULL = {"in_w": ((1024, 4624), 1), "out_w": ((2048, 1024), 0), "qkv_w": ((1024, 1280), 1), "o_w": ((1024, 1024), 0),
            "gate_w": ((2, 1024, FFN), 2), "up_w": ((2, 1024, FFN), 2), "down_w": ((2, FFN, 1024), 1)}


def _shard_shape(name):
    shape, ax = BIG_FULL[name]
    return tuple(s // 4 if i == ax else s for i, s in enumerate(shape))


def _to_slab(parts, rows, width):
    flat = jnp.concatenate(parts, axis=-1)
    pad = rows * width - flat.shape[-1]
    flat = jnp.pad(flat, [(0, 0)] * (flat.ndim - 1) + [(0, pad)])
    return flat.reshape(flat.shape[:-1] + (rows, width))


def _pack_shards(shards, dtype):
    return _to_slab([shards[n].reshape(-1).astype(dtype) for n in BIG], PACK_ROWS, 1024)


def _unpack_shards(slab):
    flat, out, off = slab.reshape(-1), {}, 0
    for n in BIG:
        shp = _shard_shape(n)
        size = math.prod(shp)
        out[n] = flat[off:off + size].reshape(shp)
        off += size
    return out


def _split_for_chips(name, full):
    shape, ax = BIG_FULL[name]
    split = full.reshape(shape[:ax] + (4, shape[ax] // 4) + shape[ax + 1:])
    return jnp.moveaxis(split, ax, 0).reshape(4, -1)


def _join_from_chips(name, stacked):
    shape, ax = BIG_FULL[name]
    shp = _shard_shape(name)
    return jnp.moveaxis(stacked.reshape((4,) + shp), 0, ax).reshape(shape)


def _pack_full(grads):
    return _to_slab([_split_for_chips(n, grads[n]) for n in BIG], PACK_ROWS, 1024)


def _unpack_gathered(slabs):
    flat, out, off = slabs.reshape(4, -1), {}, 0
    for n in BIG:
        size = math.prod(_shard_shape(n))
        out[n] = _join_from_chips(n, flat[:, off:off + size])
        off += size
    return out


def _pack_small(parts):
    padded = []
    for p in parts:
        p = p.reshape(-1).astype(F32)
        padded.append(jnp.pad(p, (0, (-p.shape[0]) % 1024)))
    return jnp.concatenate(padded).reshape(-1, 128)


def _unpack_small(slab, shapes):
    flat, out, off = slab.reshape(-1), [], 0
    for shp in shapes:
        size = math.prod(shp)
        out.append(flat[off:off + size].reshape(shp))
        off += size + (-size) % 1024
    return out


SMALL = ("ada_b", "norm_mix_w", "norm_ffn_w", "conv_w", "conv_b", "dt_bias", "a_log", "d_skip", "ssm_norm_w",
         "gmlp_ln_w", "gmlp_ln_b", "gmlp_ws", "gmlp_bs", "qkv_b", "o_b", "sinks", "rel_table", "final_norm_w")
SMALL_SPLIT = {"conv_w": 1536, "qkv_b": 1280, "o_b": 1024}
WEIGHTS = ("ada_w", "ada_b", "norm_mix_w", "norm_ffn_w", "in_w_even", "conv_w", "conv_b", "dt_bias", "a_log", "d_skip",
           "ssm_norm_w", "gmlp_ln_w", "gmlp_ln_b", "gmlp_ws", "gmlp_bs", "out_w_even", "qkv_w", "qkv_b", "o_w", "o_b",
           "sinks", "rel_table", "ffn_gate_w", "ffn_up_w", "ffn_down_w", "final_norm_w")


def kernel(x, c, ada_w, ada_b, norm_mix_w, norm_ffn_w, in_w_even, conv_w, conv_b, dt_bias, a_log, d_skip, ssm_norm_w, gmlp_ln_w, gmlp_ln_b, gmlp_ws, gmlp_bs, out_w_even, qkv_w, qkv_b, o_w, o_b, sinks, rel_table, ffn_gate_w, ffn_up_w, ffn_down_w, final_norm_w, loss_target, m_ada_w, m_ada_b, m_norm_mix_w, m_norm_ffn_w, m_in_w_even, m_conv_w, m_conv_b, m_dt_bias, m_a_log, m_d_skip, m_ssm_norm_w, m_gmlp_ln_w, m_gmlp_ln_b, m_gmlp_ws, m_gmlp_bs, m_out_w_even, m_qkv_w, m_qkv_b, m_o_w, m_o_b, m_sinks, m_rel_table, m_ffn_gate_w, m_ffn_up_w, m_ffn_down_w, m_final_norm_w, v_ada_w, v_ada_b, v_norm_mix_w, v_norm_ffn_w, v_in_w_even, v_conv_w, v_conv_b, v_dt_bias, v_a_log, v_d_skip, v_ssm_norm_w, v_gmlp_ln_w, v_gmlp_ln_b, v_gmlp_ws, v_gmlp_bs, v_out_w_even, v_qkv_w, v_qkv_b, v_o_w, v_o_b, v_sinks, v_rel_table, v_ffn_gate_w, v_ffn_up_w, v_ffn_down_w, v_final_norm_w):
    args = dict(locals())
    w = {n: args[n] for n in WEIGHTS}
    m = {n: args["m_" + n] for n in WEIGHTS}
    v = {n: args["v_" + n] for n in WEIGHTS}
    ax, ay, ac = _place()
    me = 4 * ax + 2 * ay + ac
    chip = 2 * ax + ay
    south = (ac == 0).astype(F32)
    c_arr = jnp.reshape(ac, (1,)).astype(jnp.int32)

    shards = {n: w[BIG_ARG[n]] for n in BIG}
    shards["in_w"], shards["out_w"], shards["qkv_w"], shards["o_w"] = (
        shards["in_w"][0], shards["out_w"][0], shards["qkv_w"][0], shards["o_w"][0])
    W = _unpack_gathered(_allgather_big(_pack_shards(shards, _MXU)))

    c_all = _allreduce_small("gather_cond", lax.dynamic_update_slice(jnp.zeros((8, D), F32), c, (me, 0)).reshape(64, 128))
    c_all = c_all.reshape(8, D)
    n_ada = ada_w.shape[2]
    mod_cols = _ada_fwd(c_all, ada_w, lax.dynamic_slice(ada_b, (0, chip * n_ada), (2, n_ada)).reshape(2, 1, n_ada))
    pieces = [lax.dynamic_update_slice(jnp.zeros((2, 8, 6 * D), F32), mod_cols, (0, 0, chip * n_ada))]
    split_names = list(SMALL_SPLIT)
    for n in split_names:
        full = SMALL_SPLIT[n]
        local = w[n]
        idx = (0,) * (local.ndim - 1) + (chip * local.shape[-1],)
        pieces.append(lax.dynamic_update_slice(jnp.zeros(local.shape[:-1] + (full,), F32), local, idx))
    shapes = [p.shape for p in pieces]
    gathered = _unpack_small(_allreduce_small("gather_mod", _pack_small(pieces) * south), shapes)
    mod = lax.dynamic_slice(gathered[0], (0, me, 0), (2, 1, 6 * D)).reshape(2, 6 * D)
    P = {n: w[n] for n in SMALL if n not in SMALL_SPLIT and n != "ada_b"}
    for n, full in zip(split_names, gathered[1:]):
        P[n] = full
    P["final_norm_w"] = final_norm_w.reshape(1, D)

    sq, grad_x, dmod, G, g = _local_step(x[0], loss_target[0], mod, W, P)
    loss = lax.psum(0.5 * sq[0, 0] / D, ("x", "y", "c"))

    g["final_norm_w"] = g["final_norm_w"].reshape(D)
    small_names = [n for n in SMALL if n != "ada_b"]
    pieces = [lax.dynamic_update_slice(jnp.zeros((2, 8, 6 * D), F32), dmod.reshape(2, 1, 6 * D), (0, me, 0))]
    pieces += [g[n] for n in small_names]
    shapes = [p.shape for p in pieces]
    reduced = _unpack_small(_allreduce_small("allreduce_small_grads", _pack_small(pieces)), shapes)
    dmod_all = reduced[0]
    grads = dict(zip(small_names, reduced[1:]))
    for n in split_names:
        full = grads[n]
        size = w[n].shape[-1]
        grads[n] = lax.dynamic_slice(full, (0,) * (full.ndim - 1) + (chip * size,), full.shape[:-1] + (size,))
    grads = {n: grads[n].reshape(w[n].shape) for n in small_names}
    dw_ada, db_ada = _ada_bwd(c_all, lax.dynamic_slice(dmod_all, (0, 0, chip * n_ada), (2, 8, n_ada)), dmod_all)
    grads["ada_w"], grads["ada_b"] = dw_ada, db_ada.reshape(2, 6 * D)

    gp = _pack_full(G)
    pair = _pair_sum(gp, _sibling_swap("grad_pair_exchange", gp, True), c_arr)
    fin = _chip_sum(_chip_exchange(pair))
    total = _join_halves(fin, _sibling_swap("grad_final_exchange", fin, False), c_arr)
    big = _unpack_shards(total)
    for n in BIG:
        grads[BIG_ARG[n]] = big[n].reshape(w[BIG_ARG[n]].shape)

    delta, new_m, new_v = {}, {}, {}
    for n in ("ada_w",) + tuple(BIG_ARG[k] for k in BIG):
        cols = w[n].shape[-1]
        d_, m_, v_ = _adamw("adamw_" + n, w[n].reshape(-1, cols), grads[n].reshape(-1, cols), m[n].reshape(-1, cols),
                            v[n].reshape(-1, cols))
        delta[n], new_m[n], new_v[n] = d_.reshape(w[n].shape), m_.reshape(w[n].shape), v_.reshape(w[n].shape)
    shapes = [w[n].shape for n in SMALL]
    packed = [_pack_small([t[n] for n in SMALL]) for t in (w, grads, m, v)]
    outs = _adamw("adamw_small", *packed)
    for dst, slab in zip((delta, new_m, new_v), outs):
        for n, t in zip(SMALL, _unpack_small(slab, shapes)):
            dst[n] = t
    return (loss, grad_x[None], *[grads[n] for n in WEIGHTS], *[delta[n] for n in WEIGHTS],
            *[new_m[n] for n in WEIGHTS], *[new_v[n] for n in WEIGHTS])
```

```python
import functools
import math

import numpy as np
import jax
import jax.numpy as jnp
from jax import lax
from jax.experimental import pallas as pl
from jax.experimental.pallas import tpu as pltpu

F32 = jnp.float32
BF16 = jnp.bfloat16
_MXU = jnp.bfloat16
_VMEM_LIMIT = 56 * 1024 * 1024
D = 1024
L = 128
NSTATE = 128
EPS = 1e-6
NEG_INF = -1e30
FFN = 2816
ADAM_LR, ADAM_B1, ADAM_B2, ADAM_EPS, ADAM_WD, ADAM_STEP = 0.001, 0.9, 0.999, 1e-08, 0.01, 10
MESH = pl.DeviceIdType.MESH
ANY = pl.BlockSpec(memory_space=pl.ANY)

NN = (((1,), (0,)), ((), ()))
NT = (((1,), (1,)), ((), ()))
TN = (((0,), (0,)), ((), ()))


def _dot(a, b, dn=NN):
    return lax.dot_general(a.astype(_MXU), b.astype(_MXU), dn, preferred_element_type=F32)


def _params(sem=None):
    return pltpu.CompilerParams(dimension_semantics=sem, vmem_limit_bytes=_VMEM_LIMIT)


def _sigmoid(x):
    return 1.0 / (1.0 + jnp.exp(-x))


def _softplus(x):
    return jnp.maximum(x, 0.0) + jnp.log(1.0 + jnp.exp(-jnp.abs(x)))


def _gelu(x):
    return 0.5 * x * (1.0 + lax.erf(x * (2.0 ** -0.5)))


def _gelu_grad(x):
    return 0.5 * (1.0 + lax.erf(x * (2.0 ** -0.5))) + x * jnp.exp(-0.5 * x * x) * (1.0 / math.sqrt(2.0 * math.pi))


def _silu_grad(a):
    sg = _sigmoid(a)
    return sg * (1.0 + a * (1.0 - sg))


def _rowwise(name, fn, rows, vecs, out_rows, out_accs=(), tr=512):
    S = rows[0].shape[0]
    tr = min(tr, S)
    assert S % tr == 0
    nr, nv, no, na = len(rows), len(vecs), len(out_rows), len(out_accs)

    def body(*refs):
        ins, outs = refs[:nr + nv], refs[nr + nv:]
        res = fn(*[r[...] for r in ins])
        if not isinstance(res, (tuple, list)):
            res = (res,)
        for k in range(no):
            outs[k][...] = res[k].astype(outs[k].dtype)
        if na:
            @pl.when(pl.program_id(0) == 0)
            def _():
                for k in range(na):
                    outs[no + k][...] = jnp.zeros_like(outs[no + k])
            for k in range(na):
                outs[no + k][...] += res[no + k]

    in_specs = [pl.BlockSpec((tr, a.shape[1]), lambda i: (i, 0)) for a in rows]
    in_specs += [pl.BlockSpec(v.shape, lambda i: (0, 0)) for v in vecs]
    out_specs = [pl.BlockSpec((tr, c), lambda i: (i, 0)) for c, _ in out_rows]
    out_specs += [pl.BlockSpec(s, lambda i: (0, 0)) for s in out_accs]
    out_shape = [jax.ShapeDtypeStruct((S, c), dt) for c, dt in out_rows]
    out_shape += [jax.ShapeDtypeStruct(s, F32) for s in out_accs]
    return pl.pallas_call(body, name=name, grid=(S // tr,), in_specs=in_specs, out_specs=out_specs,
                          out_shape=out_shape, compiler_params=_params(("arbitrary",)))(*rows, *vecs)


def _col_tile(n, cap):
    if n <= cap or n % 128:
        return n
    best = 128
    for t in range(128, cap + 1, 128):
        if n % t == 0:
            best = t
    return best


def _mm(name, As, Bs, mode, outs, epi=None, groups=None, extras=(), vecs=(), tm=512, tn_cap=1536):
    M = As[0].shape[0]
    N = Bs[0].shape[1] if mode == "nn" else Bs[0].shape[0]
    tm = min(tm, M)
    tn = _col_tile(N, tn_cap)
    assert M % tm == 0 and N % tn == 0
    npair = len(As)
    groups = groups or [0] * npair
    ng = max(groups) + 1
    nx, nv = len(extras), len(vecs)
    dn = NN if mode == "nn" else NT

    def body(*refs):
        a_refs, b_refs = refs[:npair], refs[npair:2 * npair]
        x_refs = refs[2 * npair:2 * npair + nx]
        v_refs = refs[2 * npair + nx:2 * npair + nx + nv]
        o_refs = refs[2 * npair + nx + nv:]
        accs = [None] * ng
        for k in range(npair):
            d = _dot(a_refs[k][...], b_refs[k][...], dn)
            accs[groups[k]] = d if accs[groups[k]] is None else accs[groups[k]] + d
        args = accs + [x[...] for x in x_refs] + [v[...] for v in v_refs]
        res = epi(*args) if epi is not None else tuple(accs)
        if not isinstance(res, (tuple, list)):
            res = (res,)
        for o, r in zip(o_refs, res):
            o[...] = r.astype(o.dtype)

    in_specs = [pl.BlockSpec((tm, a.shape[1]), lambda i, j: (i, 0)) for a in As]
    if mode == "nn":
        in_specs += [pl.BlockSpec((b.shape[0], tn), lambda i, j: (0, j)) for b in Bs]
    else:
        in_specs += [pl.BlockSpec((tn, b.shape[1]), lambda i, j: (j, 0)) for b in Bs]
    in_specs += [pl.BlockSpec((tm, tn), lambda i, j: (i, j)) for _ in extras]
    in_specs += [pl.BlockSpec((1, tn), lambda i, j: (0, j)) for _ in vecs]
    out_specs = [pl.BlockSpec((tm, tn), lambda i, j: (i, j)) for _ in outs]
    out_shape = [jax.ShapeDtypeStruct((M, N), dt) for dt in outs]
    return pl.pallas_call(body, name=name, grid=(M // tm, N // tn), in_specs=in_specs, out_specs=out_specs,
                          out_shape=out_shape, compiler_params=_params(("parallel", "parallel")))(
                              *As, *Bs, *extras, *vecs)


def _mm_tn(name, A, B, tk=512, t2_cap=1536):
    S, K1 = A.shape
    N2 = B.shape[1]
    tk = min(tk, S)
    t2 = _col_tile(N2, t2_cap)
    assert S % tk == 0 and N2 % t2 == 0

    def body(a_ref, b_ref, o_ref):
        @pl.when(pl.program_id(1) == 0)
        def _():
            o_ref[...] = jnp.zeros_like(o_ref)
        o_ref[...] += _dot(a_ref[...], b_ref[...], TN)

    return pl.pallas_call(
        body, name=name, grid=(N2 // t2, S // tk),
        in_specs=[pl.BlockSpec((tk, K1), lambda j, k: (k, 0)), pl.BlockSpec((tk, t2), lambda j, k: (k, j))],
        out_specs=pl.BlockSpec((K1, t2), lambda j, k: (0, j)),
        out_shape=jax.ShapeDtypeStruct((K1, N2), F32),
        compiler_params=_params(("parallel", "arbitrary")))(A, B)


def _norm_mod_fwd(name, x, nw, sc, sh):
    def fn(x, nw, sc, sh):
        rstd = lax.rsqrt(jnp.mean(x * x, axis=-1, keepdims=True) + EPS)
        return (x * rstd * nw) * (1.0 + sc) + sh
    return _rowwise(name, fn, [x], [nw, sc, sh], [(D, BF16)])[0]


def _norm_mod_bwd(name, x, dh, dres, nw, sc):
    def fn(x, dh, dres, nw, sc):
        rstd = lax.rsqrt(jnp.mean(x * x, axis=-1, keepdims=True) + EPS)
        xh = x * rstd
        dn = dh * (1.0 + sc)
        dxh = dn * nw
        dx = rstd * (dxh - xh * jnp.mean(dxh * xh, axis=-1, keepdims=True))
        return (dres + dx, jnp.sum(dh, axis=0, keepdims=True), jnp.sum(dh * (xh * nw), axis=0, keepdims=True),
                jnp.sum(dn * xh, axis=0, keepdims=True))
    return _rowwise(name, fn, [x, dh, dres], [nw, sc], [(D, F32)], [(1, D)] * 3)


def _gate_bwd(name, dx, y, g):
    def fn(dx, y, g):
        dy = dx * g
        return dy, jnp.sum(dx * y, axis=0, keepdims=True), jnp.sum(dy, axis=0, keepdims=True)
    return _rowwise(name, fn, [dx, y], [g], [(D, BF16)], [(1, D)] * 2)


def _loss_head(x, tgt, fw):
    def fn(x, tgt, fw):
        rstd = lax.rsqrt(jnp.mean(x * x, axis=-1, keepdims=True) + EPS)
        xh = x * rstd
        err = xh * fw - tgt
        dout = err * (1.0 / D)
        dxh = dout * fw
        dx = rstd * (dxh - xh * jnp.mean(dxh * xh, axis=-1, keepdims=True))
        sq = jnp.sum(jnp.sum(err * err, axis=1, keepdims=True), axis=0, keepdims=True)
        return dx, sq, jnp.sum(dout * xh, axis=0, keepdims=True)
    return _rowwise("loss_head", fn, [x, tgt], [fw], [(D, F32)], [(1, 1), (1, D)])


def _ffn_fwd(tag, h, wg, wu, wd, x, g2):
    def act(a, b):
        return a, b, a * _sigmoid(a) * b
    a, b, f = _mm(f"ffn_up_{tag}", [h, h], [wg, wu], "nt", [F32, F32, BF16], epi=act, groups=[0, 1], tn_cap=1408)

    def res(y, x, g):
        return y, x + g * y
    y, xo = _mm(f"ffn_down_{tag}", [f], [wd], "nn", [F32, F32], epi=res, extras=[x], vecs=[g2])
    return a, b, f, y, xo


def _ffn_bwd(tag, dx, h, a, b, f, y, wg, wu, wd, g2):
    dy, dg2, _ = _gate_bwd(f"ffn_gate_bwd_{tag}", dx, y, g2)

    def act_bwd(df, a, b):
        return df * b * _silu_grad(a), df * (a * _sigmoid(a))
    da, db = _mm(f"ffn_dact_{tag}", [dy], [wd], "nt", [BF16, BF16], epi=act_bwd, extras=[a, b], tn_cap=1408)
    dwd = _mm_tn(f"ffn_dwd_{tag}", f, dy)
    dwg = _mm_tn(f"ffn_dwg_{tag}", da, h)
    dwu = _mm_tn(f"ffn_dwu_{tag}", db, h)
    dh = _mm(f"ffn_dh_{tag}", [da, db], [wg, wu], "nn", [F32])[0]
    return dh, dg2, dwg, dwu, dwd


def _conv_fwd(xr, w, b, tb=512):
    S, C = xr.shape
    tb = min(tb, S)

    def body(x_ref, halo_ref, w_ref, b_ref, pre_ref, out_ref):
        i = pl.program_id(0)
        halo = jnp.where(i > 0, halo_ref[...], 0.0)
        xe = jnp.concatenate([halo, x_ref[...]], axis=0)
        pre = w_ref[3:4, :] * x_ref[...] + b_ref[...]
        for j in (1, 2, 3):
            pre = pre + w_ref[3 - j:4 - j, :] * pltpu.roll(xe, j, axis=0)[8:, :]
        pre_ref[...] = pre
        out_ref[...] = pre * _sigmoid(pre)

    return pl.pallas_call(
        body, name="conv_fwd", grid=(S // tb,),
        in_specs=[pl.BlockSpec((tb, C), lambda i: (i, 0)),
                  pl.BlockSpec((8, C), lambda i: (jnp.maximum(i * (tb // 8) - 1, 0), 0)),
                  pl.BlockSpec((4, C), lambda i: (0, 0)), pl.BlockSpec((1, C), lambda i: (0, 0))],
        out_specs=[pl.BlockSpec((tb, C), lambda i: (i, 0))] * 2,
        out_shape=[jax.ShapeDtypeStruct((S, C), F32)] * 2,
        compiler_params=_params(("parallel",)))(xr, xr, w, b)


def _conv_bwd(dxc, pre, xr, w, tb=512):
    S, C = xr.shape
    tb = min(tb, S)
    nblk = S // tb

    def body(d_ref, p_ref, dn_ref, pn_ref, x_ref, xh_ref, w_ref, dx_ref, dw_ref, db_ref):
        i = pl.program_id(0)

        @pl.when(i == 0)
        def _():
            dw_ref[...] = jnp.zeros_like(dw_ref)
            db_ref[...] = jnp.zeros_like(db_ref)

        dpre = d_ref[...] * _silu_grad(p_ref[...])
        dnext = jnp.where(i < nblk - 1, dn_ref[...] * _silu_grad(pn_ref[...]), 0.0)
        pe = jnp.concatenate([dpre, dnext], axis=0)
        dx = w_ref[3:4, :] * dpre
        for j in (1, 2, 3):
            dx = dx + w_ref[3 - j:4 - j, :] * pltpu.roll(pe, tb + 8 - j, axis=0)[:tb, :]
        dx_ref[...] = dx.astype(dx_ref.dtype)
        halo = jnp.where(i > 0, xh_ref[...], 0.0)
        xe = jnp.concatenate([halo, x_ref[...]], axis=0)
        for k in range(3):
            dw_ref[k:k + 1, :] += jnp.sum(dpre * pltpu.roll(xe, 3 - k, axis=0)[8:, :], axis=0, keepdims=True)
        dw_ref[3:4, :] += jnp.sum(dpre * x_ref[...], axis=0, keepdims=True)
        db_ref[...] += jnp.sum(dpre, axis=0, keepdims=True)

    blk = pl.BlockSpec((tb, C), lambda i: (i, 0))
    nxt = pl.BlockSpec((8, C), lambda i: (jnp.minimum((i + 1) * (tb // 8), S // 8 - 1), 0))
    prv = pl.BlockSpec((8, C), lambda i: (jnp.maximum(i * (tb // 8) - 1, 0), 0))
    return pl.pallas_call(
        body, name="conv_bwd", grid=(nblk,),
        in_specs=[blk, blk, nxt, nxt, blk, prv, pl.BlockSpec((4, C), lambda i: (0, 0))],
        out_specs=[blk, pl.BlockSpec((4, C), lambda i: (0, 0)), pl.BlockSpec((1, C), lambda i: (0, 0))],
        out_shape=[jax.ShapeDtypeStruct((S, C), BF16), jax.ShapeDtypeStruct((4, C), F32),
                   jax.ShapeDtypeStruct((1, C), F32)],
        compiler_params=_params(("arbitrary",)))(dxc, pre, dxc, pre, xr, xr, w)


def _iota(shape, dim):
    return lax.broadcasted_iota(jnp.int32, shape, dim)


def _colsel(m, lane, h):
    return jnp.sum(jnp.where(lane == h, m, 0.0), axis=1, keepdims=True)


def _cumsum_rows(v):
    r = _iota(v.shape, 0)
    k = 1
    while k < v.shape[0]:
        v = v + jnp.where(r >= k, pltpu.roll(v, k, axis=0), 0.0)
        k *= 2
    return v


def _suffix_sum_rows(v):
    n = v.shape[0]
    r = _iota(v.shape, 0)
    k = 1
    while k < n:
        v = v + jnp.where(r < n - k, pltpu.roll(v, n - k, axis=0), 0.0)
        k *= 2
    return v


def _ssd_fwd(xc, dtr, z, dtb, alog, dskl, nw):
    S = xc.shape[0]
    nc = S // L

    def body(xc_ref, dtr_ref, z_ref, dtb_ref, alog_ref, dsk_ref, nw_ref, ya_ref, y_ref, prev_ref,
             st_ref, cum_ref, cumT_ref):
        i = pl.program_id(0)

        @pl.when(i == 0)
        def _():
            st_ref[...] = jnp.zeros_like(st_ref)

        lane = _iota((L, 128), 1)
        lane1 = _iota((1, 128), 1)
        lo = lane < 64
        lo1 = lane1 < 64
        tril = _iota((L, L), 0) >= _iota((L, L), 1)
        dt = _softplus(dtr_ref[...] + dtb_ref[...])
        a_neg = -jnp.exp(alog_ref[...])
        cum = _cumsum_rows(dt * a_neg)
        cum_ref[...] = cum
        cumT_ref[...] = cum.T
        last_all = cum_ref[L - 1:L, :]
        prev_t = st_ref[...]
        prev_ref[0] = prev_t
        for g in range(2):
            bg = xc_ref[:, 1024 + g * 128:1152 + g * 128]
            cg = xc_ref[:, 1280 + g * 128:1408 + g * 128]
            gmat = _dot(cg, bg, NT)
            yoff = _dot(cg, prev_t[:, g * 512:(g + 1) * 512])
            bg_t = bg.T
            for jp in range(4):
                j = g * 4 + jp
                sl = slice(j * 128, (j + 1) * 128)
                xp = xc_ref[:, sl]
                cc = [_colsel(cum, lane, 2 * j), _colsel(cum, lane, 2 * j + 1)]
                cum_l = jnp.where(lo, cc[0], cc[1])
                dt_l = jnp.where(lo, _colsel(dt, lane, 2 * j), _colsel(dt, lane, 2 * j + 1))
                last_l = jnp.where(lo1, _colsel(last_all, lane1, 2 * j), _colsel(last_all, lane1, 2 * j + 1))
                xd = xp * dt_l
                ys = []
                for hh in range(2):
                    seg = cc[hh] - cumT_ref[2 * j + hh:2 * j + hh + 1, :]
                    dm = jnp.where(tril, jnp.exp(jnp.where(tril, seg, 0.0)), 0.0)
                    ys.append(_dot(gmat * dm, xd))
                y_ref[:, sl] = (jnp.where(lo, ys[0], ys[1]) + jnp.exp(cum_l) * yoff[:, jp * 128:(jp + 1) * 128]
                                + dsk_ref[:, sl] * xp)
                st_ref[:, sl] = prev_t[:, sl] * jnp.exp(last_l) + _dot(bg_t, xd * jnp.exp(last_l - cum_l))
        for g in range(2):
            sl = slice(g * 512, (g + 1) * 512)
            zz = z_ref[:, sl]
            yg = y_ref[:, sl] * (zz * _sigmoid(zz))
            rstd = lax.rsqrt(jnp.mean(yg * yg, axis=-1, keepdims=True) + EPS)
            ya_ref[:, sl] = (yg * rstd * nw_ref[:, sl]).astype(ya_ref.dtype)

    blk = lambda c: pl.BlockSpec((L, c), lambda i: (i, 0))
    vec = lambda c: pl.BlockSpec((1, c), lambda i: (0, 0))
    return pl.pallas_call(
        body, name="ssd_fwd", grid=(nc,),
        in_specs=[blk(1536), blk(128), blk(1024), vec(128), vec(128), vec(1024), vec(1024)],
        out_specs=[blk(1024), blk(1024), pl.BlockSpec((1, NSTATE, 1024), lambda i: (i, 0, 0))],
        out_shape=[jax.ShapeDtypeStruct((S, 1024), BF16), jax.ShapeDtypeStruct((S, 1024), F32),
                   jax.ShapeDtypeStruct((nc, NSTATE, 1024), F32)],
        scratch_shapes=[pltpu.VMEM((NSTATE, 1024), F32), pltpu.VMEM((L, 128), F32), pltpu.VMEM((L, 128), F32)],
        compiler_params=_params(("arbitrary",)))(xc, dtr, z, dtb, alog, dskl, nw)


def _ssd_bwd(dya, y, z, xc, dtr, prev, dtb, alog, dskl, nw):
    S = xc.shape[0]
    nc = S // L

    def body(dya_ref, y_ref, z_ref, xc_ref, dtr_ref, prev_ref, dtb_ref, alog_ref, dsk_ref, nw_ref,
             dz_ref, dxc_ref, ddtr_ref, dnw_ref, ddsk_ref, dalog_ref, ddtb_ref,
             dst_ref, cum_ref, cumT_ref, dy_ref, dskacc_ref):
        i = pl.program_id(0)

        @pl.when(i == 0)
        def _():
            dst_ref[...] = jnp.zeros_like(dst_ref)
            dskacc_ref[...] = jnp.zeros_like(dskacc_ref)
            dnw_ref[...] = jnp.zeros_like(dnw_ref)
            dalog_ref[...] = jnp.zeros_like(dalog_ref)
            ddtb_ref[...] = jnp.zeros_like(ddtb_ref)

        lane = _iota((L, 128), 1)
        lane1 = _iota((1, 128), 1)
        lo = lane < 64
        lo1 = lane1 < 64
        r2, c2 = _iota((L, L), 0), _iota((L, L), 1)
        tril = r2 >= c2
        triu = r2 <= c2
        is_last = _iota((L, 1), 0) == L - 1

        for g in range(2):
            sl = slice(g * 512, (g + 1) * 512)
            zz = z_ref[:, sl]
            sg = _sigmoid(zz)
            zg = zz * sg
            yv = y_ref[:, sl]
            yg = yv * zg
            rstd = lax.rsqrt(jnp.mean(yg * yg, axis=-1, keepdims=True) + EPS)
            xh = yg * rstd
            d_out = dya_ref[:, sl]
            dnw_ref[:, sl] += jnp.sum(d_out * xh, axis=0, keepdims=True)
            dyn = d_out * nw_ref[:, sl]
            dyg = rstd * (dyn - xh * jnp.mean(dyn * xh, axis=-1, keepdims=True))
            dy_ref[:, sl] = dyg * zg
            dz_ref[:, sl] = (dyg * yv * (sg * (1.0 + zz * (1.0 - sg)))).astype(dz_ref.dtype)

        dtin = dtr_ref[...] + dtb_ref[...]
        dt = _softplus(dtin)
        a_neg = -jnp.exp(alog_ref[...])
        cum = _cumsum_rows(dt * a_neg)
        cum_ref[...] = cum
        cumT_ref[...] = cum.T
        last_all = cum_ref[L - 1:L, :]
        prev_t = prev_ref[0]
        dn_t = dst_ref[...]
        dcum = jnp.zeros((L, 128), F32)
        ddt = jnp.zeros((L, 128), F32)
        for g in range(2):
            gsl = slice(g * 512, (g + 1) * 512)
            bg = xc_ref[:, 1024 + g * 128:1152 + g * 128]
            cg = xc_ref[:, 1280 + g * 128:1408 + g * 128]
            gmat = _dot(cg, bg, NT)
            gmat_t = _dot(bg, cg, NT)
            pg = prev_t[:, gsl]
            zmat = _dot(cg, pg)
            dgm = jnp.zeros((L, L), F32)
            dgm_t = jnp.zeros((L, L), F32)
            db_acc = jnp.zeros((L, NSTATE), F32)
            dz_parts, cd_parts = [], []
            for jp in range(4):
                j = g * 4 + jp
                sl = slice(j * 128, (j + 1) * 128)
                xp = xc_ref[:, sl]
                dyp = dy_ref[:, sl]
                cc = [_colsel(cum, lane, 2 * j), _colsel(cum, lane, 2 * j + 1)]
                lc = [_colsel(last_all, lane1, 2 * j), _colsel(last_all, lane1, 2 * j + 1)]
                cum_l = jnp.where(lo, cc[0], cc[1])
                dt_l = jnp.where(lo, _colsel(dt, lane, 2 * j), _colsel(dt, lane, 2 * j + 1))
                last_l = jnp.where(lo1, lc[0], lc[1])
                e_l = jnp.exp(cum_l)
                dte_l = jnp.exp(last_l - cum_l)
                cd_l = jnp.exp(last_l)
                cd_parts.append(cd_l)
                xd = xp * dt_l
                dskacc_ref[:, sl] += jnp.sum(dyp * xp, axis=0, keepdims=True)
                dxp = dsk_ref[:, sl] * dyp
                t = dyp * (e_l * zmat[:, jp * 128:(jp + 1) * 128])
                dcc = [jnp.sum(jnp.where(lo, t, 0.0), axis=1, keepdims=True),
                       jnp.sum(jnp.where(lo, 0.0, t), axis=1, keepdims=True)]
                dz_parts.append(e_l * dyp)
                dnp_ = dn_t[:, sl]
                t2 = jnp.sum(dnp_ * prev_t[:, sl], axis=0, keepdims=True)
                dcd = [jnp.sum(jnp.where(lo1, t2, 0.0), axis=1, keepdims=True),
                       jnp.sum(jnp.where(lo1, 0.0, t2), axis=1, keepdims=True)]
                wm = _dot(bg, dnp_)
                dxd = wm * dte_l
                t3 = wm * xd
                ddte = [jnp.sum(jnp.where(lo, t3, 0.0), axis=1, keepdims=True),
                        jnp.sum(jnp.where(lo, 0.0, t3), axis=1, keepdims=True)]
                db_acc = db_acc + _dot(xd * dte_l, dnp_, NT)
                for hh in range(2):
                    h = 2 * j + hh
                    half = lo if hh == 0 else jnp.logical_not(lo)
                    row = cumT_ref[h:h + 1, :]
                    dm = jnp.where(tril, jnp.exp(jnp.where(tril, cc[hh] - row, 0.0)), 0.0)
                    dm_t = jnp.where(triu, jnp.exp(jnp.where(triu, row - cc[hh], 0.0)), 0.0)
                    m = gmat * dm
                    m_t = gmat_t * dm_t
                    dym = jnp.where(half, dyp, 0.0)
                    d_m = _dot(dym, xd, NT)
                    d_mt = _dot(xd, dym, NT)
                    dxd = dxd + _dot(m_t, dym)
                    dcc[hh] = dcc[hh] + jnp.sum(d_m * m, axis=1, keepdims=True) - jnp.sum(d_mt * m_t, axis=1, keepdims=True)
                    dgm = dgm + d_m * dm
                    dgm_t = dgm_t + d_mt * dm_t
                    dte_c = jnp.exp(lc[hh] - cc[hh])
                    dcc[hh] = dcc[hh] - ddte[hh] * dte_c
                    endc = dcd[hh] * jnp.exp(lc[hh]) + jnp.sum(ddte[hh] * dte_c, axis=0, keepdims=True)
                    dcc[hh] = dcc[hh] + jnp.where(is_last, endc, 0.0)
                    dcum = jnp.where(lane == h, dcc[hh], dcum)
                dxc_ref[:, sl] = dxp + dxd * dt_l
                t4 = dxd * xp
                ddt = jnp.where(lane == 2 * j, jnp.sum(jnp.where(lo, t4, 0.0), axis=1, keepdims=True), ddt)
                ddt = jnp.where(lane == 2 * j + 1, jnp.sum(jnp.where(lo, 0.0, t4), axis=1, keepdims=True), ddt)
            dzg = jnp.concatenate(dz_parts, axis=1)
            dst_ref[:, gsl] = dn_t[:, gsl] * jnp.concatenate(cd_parts, axis=1) + _dot(cg.T, dzg)
            dxc_ref[:, 1280 + g * 128:1408 + g * 128] = _dot(dgm, bg) + _dot(dzg, pg, NT)
            dxc_ref[:, 1024 + g * 128:1152 + g * 128] = _dot(dgm_t, cg) + db_acc
        dla = _suffix_sum_rows(dcum)
        ddt = ddt + dla * a_neg
        dalog_ref[...] += jnp.sum(dla * dt, axis=0, keepdims=True) * a_neg
        ddtr = jnp.where(lane < 16, ddt * _sigmoid(dtin), 0.0)
        ddtr_ref[...] = ddtr.astype(ddtr_ref.dtype)
        ddtb_ref[...] += jnp.sum(ddtr, axis=0, keepdims=True)

        @pl.when(i == nc - 1)
        def _():
            seg = (_iota((1024, 128), 0) // 64 == _iota((1024, 128), 1)).astype(F32)
            acc8 = jnp.broadcast_to(dskacc_ref[...], (8, 1024))
            ddsk_ref[...] = lax.dot_general(acc8, seg, NN, precision=lax.Precision.HIGHEST,
                                            preferred_element_type=F32)

    rev = lambda c: pl.BlockSpec((L, c), lambda i: (nc - 1 - i, 0))
    vec = lambda c: pl.BlockSpec((1, c), lambda i: (0, 0))
    return pl.pallas_call(
        body, name="ssd_bwd", grid=(nc,),
        in_specs=[rev(1024), rev(1024), rev(1024), rev(1536), rev(128),
                  pl.BlockSpec((1, NSTATE, 1024), lambda i: (nc - 1 - i, 0, 0)),
                  vec(128), vec(128), vec(1024), vec(1024)],
        out_specs=[rev(1024), rev(1536), rev(128), vec(1024), pl.BlockSpec((8, 128), lambda i: (0, 0)),
                   vec(128), vec(128)],
        out_shape=[jax.ShapeDtypeStruct((S, 1024), BF16), jax.ShapeDtypeStruct((S, 1536), F32),
                   jax.ShapeDtypeStruct((S, 128), BF16), jax.ShapeDtypeStruct((1, 1024), F32),
                   jax.ShapeDtypeStruct((8, 128), F32), jax.ShapeDtypeStruct((1, 128), F32),
                   jax.ShapeDtypeStruct((1, 128), F32)],
        scratch_shapes=[pltpu.VMEM((NSTATE, 1024), F32), pltpu.VMEM((L, 128), F32), pltpu.VMEM((L, 128), F32),
                        pltpu.VMEM((L, 1024), F32), pltpu.VMEM((1, 1024), F32)],
        compiler_params=_params(("arbitrary",)))(dya, y, z, xc, dtr, prev, dtb, alog, dskl, nw)


def _layer_norm_parts(vg):
    mu = jnp.mean(vg, axis=-1, keepdims=True)
    vc = vg - mu
    rstd = lax.rsqrt(jnp.mean(vc * vc, axis=-1, keepdims=True) + EPS)
    return vc * rstd, rstd


def _gmlp_fwd(u, v, lnw, lnb, ws, bse, tb=512):
    S = u.shape[0]
    tb = min(tb, S)

    def body(u_ref, v_ref, lnw_ref, lnb_ref, ws_ref, bse_ref, o_ref, vn_ref):
        tril = _iota((L, L), 0) >= _iota((L, L), 1)
        xh, _ = _layer_norm_parts(_gelu(v_ref[...]))
        vn_ref[...] = xh * lnw_ref[...] + lnb_ref[...]
        for g in range(8):
            w = jnp.where(tril, ws_ref[g], 0.0)
            gs = slice(g * 128, (g + 1) * 128)
            for ch in range(tb // L):
                rs = slice(ch * L, (ch + 1) * L)
                sv = _dot(w, vn_ref[rs, gs]) + bse_ref[g]
                o_ref[rs, gs] = (_gelu(u_ref[rs, gs]) * sv).astype(o_ref.dtype)

    blk = pl.BlockSpec((tb, 1024), lambda i: (i, 0))
    vec = pl.BlockSpec((1, 1024), lambda i: (0, 0))
    cube = pl.BlockSpec((8, L, 128), lambda i: (0, 0, 0))
    return pl.pallas_call(
        body, name="gmlp_fwd", grid=(S // tb,), in_specs=[blk, blk, vec, vec, cube, cube], out_specs=blk,
        out_shape=jax.ShapeDtypeStruct((S, 1024), BF16), scratch_shapes=[pltpu.VMEM((tb, 1024), F32)],
        compiler_params=_params(("parallel",)))(u, v, lnw, lnb, ws, bse)


def _gmlp_bwd(dyb, u, v, lnw, lnb, ws, bse, tb=512):
    S = u.shape[0]
    tb = min(tb, S)

    def body(d_ref, u_ref, v_ref, lnw_ref, lnb_ref, ws_ref, bse_ref,
             du_ref, dv_ref, dws_ref, dbse_ref, dlnw_ref, dlnb_ref, vn_ref, dvn_ref):
        @pl.when(pl.program_id(0) == 0)
        def _():
            dws_ref[...] = jnp.zeros_like(dws_ref)
            dbse_ref[...] = jnp.zeros_like(dbse_ref)
            dlnw_ref[...] = jnp.zeros_like(dlnw_ref)
            dlnb_ref[...] = jnp.zeros_like(dlnb_ref)

        tril = _iota((L, L), 0) >= _iota((L, L), 1)
        vv = v_ref[...]
        xh, rstd = _layer_norm_parts(_gelu(vv))
        vn_ref[...] = xh * lnw_ref[...] + lnb_ref[...]
        for g in range(8):
            w = jnp.where(tril, ws_ref[g], 0.0)
            w_t = w.T
            gs = slice(g * 128, (g + 1) * 128)
            dw = jnp.zeros((L, L), F32)
            dbs = jnp.zeros((L, 128), F32)
            for ch in range(tb // L):
                rs = slice(ch * L, (ch + 1) * L)
                vn = vn_ref[rs, gs]
                sv = _dot(w, vn) + bse_ref[g]
                uu = u_ref[rs, gs]
                dd = d_ref[rs, gs]
                du_ref[rs, gs] = (dd * sv * _gelu_grad(uu)).astype(du_ref.dtype)
                dsv = dd * _gelu(uu)
                dw = dw + _dot(dsv, vn, NT)
                dbs = dbs + dsv
                dvn_ref[rs, gs] = _dot(w_t, dsv)
            dws_ref[g] += jnp.where(tril, dw, 0.0)
            dbse_ref[g] += dbs
        dvn = dvn_ref[...]
        dlnw_ref[...] += jnp.sum(dvn * xh, axis=0, keepdims=True)
        dlnb_ref[...] += jnp.sum(dvn, axis=0, keepdims=True)
        dxh = dvn * lnw_ref[...]
        dvg = rstd * (dxh - jnp.mean(dxh, axis=-1, keepdims=True) - xh * jnp.mean(dxh * xh, axis=-1, keepdims=True))
        dv_ref[...] = (dvg * _gelu_grad(vv)).astype(dv_ref.dtype)

    blk = pl.BlockSpec((tb, 1024), lambda i: (i, 0))
    vec = pl.BlockSpec((1, 1024), lambda i: (0, 0))
    cube = pl.BlockSpec((8, L, 128), lambda i: (0, 0, 0))
    return pl.pallas_call(
        body, name="gmlp_bwd", grid=(S // tb,), in_specs=[blk, blk, blk, vec, vec, cube, cube],
        out_specs=[blk, blk, cube, cube, vec, vec],
        out_shape=[jax.ShapeDtypeStruct((S, 1024), BF16), jax.ShapeDtypeStruct((S, 1024), BF16),
                   jax.ShapeDtypeStruct((8, L, 128), F32), jax.ShapeDtypeStruct((8, L, 128), F32),
                   jax.ShapeDtypeStruct((1, 1024), F32), jax.ShapeDtypeStruct((1, 1024), F32)],
        scratch_shapes=[pltpu.VMEM((tb, 1024), F32), pltpu.VMEM((tb, 1024), F32)],
        compiler_params=_params(("arbitrary",)))(dyb, u, v, lnw, lnb, ws, bse)


def _lane_sum(name, a):
    def body(a_ref, o_ref):
        o_ref[...] = jnp.sum(a_ref[...], axis=1, keepdims=True)
    return pl.pallas_call(body, name=name, out_shape=jax.ShapeDtypeStruct((a.shape[0], 1), F32))(a)


def _bucket_onehot_t():
    qi = np.arange(L)[:, None]
    sj = np.arange(2 * L)[None, :]
    dist = np.maximum(qi + L - sj, 0)
    log_ratio = (np.log(np.maximum(dist, 1).astype(np.float32) / np.float32(16)) / np.float32(math.log(128 / 16)))
    large = 16 + (log_ratio.astype(np.float32) * np.float32(16)).astype(np.int32)
    bucket = np.where(dist < 16, dist, np.minimum(large, 31)).reshape(-1)
    return (np.arange(32)[:, None] == bucket[None, :]).astype(np.float32)


def _rel_bias(table_t, onehot_t):
    def body(t_ref, oh_ref, o_ref):
        o_ref[...] = lax.dot_general(t_ref[...], oh_ref[...], NN, precision=lax.Precision.HIGHEST,
                                     preferred_element_type=F32)
    return pl.pallas_call(body, name="rel_bias", out_shape=jax.ShapeDtypeStruct((16, L * 2 * L), F32),
                          compiler_params=_params())(table_t, onehot_t)


def _rel_bias_bwd(dbias, onehot_t):
    def body(d_ref, oh_ref, o_ref):
        o_ref[...] = lax.dot_general(d_ref[...], oh_ref[...], NT, precision=lax.Precision.HIGHEST,
                                     preferred_element_type=F32)
    return pl.pallas_call(body, name="rel_bias_bwd", out_shape=jax.ShapeDtypeStruct((16, 32), F32),
                          compiler_params=_params())(dbias, onehot_t)


def _band(kp, kc, lo):
    kk = jnp.concatenate([kp, kc], axis=0)
    kr = pltpu.roll(kk, 64, axis=1)
    return [jnp.where(lo, kk, kr), jnp.where(lo, kr, kk)]


def _attn_mask(i):
    qi, sj = _iota((L, 2 * L), 0), _iota((L, 2 * L), 1)
    rel = qi + L - sj
    return (rel >= 0) & (rel < L) & ((sj >= L) | (i > 0))


SMEM = pl.BlockSpec(memory_space=pltpu.SMEM)


def _attn_fwd(qkv, bias, sinks):
    S = qkv.shape[0]
    nb = S // L
    scale = 64 ** -0.5

    def body(sink_ref, q_ref, kc_ref, vc_ref, kp_ref, vp_ref, bias_ref, o_ref, lse_ref):
        i = pl.program_id(0)
        lane = _iota((L, 128), 1)
        lo = lane < 64
        lo2 = _iota((2 * L, 128), 1) < 64
        mask = _attn_mask(i)
        kd = _band(kp_ref[...], kc_ref[...], lo2)
        vd = _band(vp_ref[...], vc_ref[...], lo2)
        lse = jnp.zeros((L, 128), F32)
        for pr in range(8):
            sl = slice(pr * 128, (pr + 1) * 128)
            qp = q_ref[:, sl]
            j = pr // 4
            outs = []
            for hh in range(2):
                h = 2 * pr + hh
                qm = jnp.where(lo if hh == 0 else jnp.logical_not(lo), qp, 0.0)
                lg = jnp.where(mask, _dot(qm, kd[j], NT) * scale + bias_ref[h], NEG_INF)
                s = sink_ref[h]
                m = jnp.maximum(jnp.max(lg, axis=1, keepdims=True), s)
                p = jnp.where(mask, jnp.exp(lg - m), 0.0)
                den = jnp.sum(p, axis=1, keepdims=True) + jnp.exp(s - m)
                outs.append(_dot(p / den, vd[j]))
                lse = jnp.where(lane == h, m + jnp.log(den), lse)
            o_ref[:, sl] = jnp.where(lo, outs[0], outs[1]).astype(o_ref.dtype)
        lse_ref[...] = lse

    prev = lambda col: pl.BlockSpec((L, 128), lambda i: (jnp.maximum(i - 1, 0), col))
    cur = lambda col: pl.BlockSpec((L, 128), lambda i: (i, col))
    return pl.pallas_call(
        body, name="attn_fwd", grid=(nb,),
        in_specs=[SMEM, pl.BlockSpec((L, 1024), lambda i: (i, 0)), cur(8), cur(9), prev(8), prev(9),
                  pl.BlockSpec((16, L, 2 * L), lambda i: (0, 0, 0))],
        out_specs=[pl.BlockSpec((L, 1024), lambda i: (i, 0)), pl.BlockSpec((L, 128), lambda i: (i, 0))],
        out_shape=[jax.ShapeDtypeStruct((S, 1024), BF16), jax.ShapeDtypeStruct((S, 128), F32)],
        compiler_params=_params(("parallel",)))(sinks, qkv, qkv, qkv, qkv, qkv, bias)


def _attn_bwd(qkv, d_o, lse, bias, sinks):
    S = qkv.shape[0]
    nb = S // L
    scale = 64 ** -0.5

    def body(sink_ref, q_ref, kc_ref, vc_ref, kp_ref, vp_ref, do_ref, lse_ref, bias_ref,
             dq_ref, dkv_ref, dbias_ref, dsink_ref, dbq_ref, dbkv_ref, carry_ref):
        i = pl.program_id(0)

        @pl.when(i == 0)
        def _():
            dbias_ref[...] = jnp.zeros_like(dbias_ref)
            dsink_ref[...] = jnp.zeros_like(dsink_ref)
            dbq_ref[...] = jnp.zeros_like(dbq_ref)
            dbkv_ref[...] = jnp.zeros_like(dbkv_ref)
            carry_ref[...] = jnp.zeros_like(carry_ref)

        @pl.when(i < nb)
        def _():
            lane = _iota((L, 128), 1)
            lane1 = _iota((1, 128), 1)
            lo = lane < 64
            lo2 = _iota((2 * L, 128), 1) < 64
            mask = _attn_mask(i)
            kd = _band(kp_ref[...], kc_ref[...], lo2)
            vd = _band(vp_ref[...], vc_ref[...], lo2)
            lse_all = lse_ref[...]
            acc_k = [jnp.zeros((2 * L, 128), F32), jnp.zeros((2 * L, 128), F32)]
            acc_v = [jnp.zeros((2 * L, 128), F32), jnp.zeros((2 * L, 128), F32)]
            dsink = jnp.zeros((1, 128), F32)
            for pr in range(8):
                sl = slice(pr * 128, (pr + 1) * 128)
                qp = q_ref[:, sl]
                dop = do_ref[:, sl]
                j = pr // 4
                dqs = []
                for hh in range(2):
                    h = 2 * pr + hh
                    half = lo if hh == 0 else jnp.logical_not(lo)
                    qm = jnp.where(half, qp, 0.0)
                    dom = jnp.where(half, dop, 0.0)
                    lse_h = _colsel(lse_all, lane, h)
                    lg = _dot(qm, kd[j], NT) * scale + bias_ref[h]
                    p = jnp.where(mask, jnp.exp(jnp.where(mask, lg, NEG_INF) - lse_h), 0.0)
                    dp = _dot(dom, vd[j], NT)
                    delta = jnp.sum(p * dp, axis=1, keepdims=True)
                    ds = p * (dp - delta)
                    dbias_ref[h] += ds
                    ds_sink = jnp.sum(-jnp.exp(sink_ref[h] - lse_h) * delta, axis=0, keepdims=True)
                    dsink = dsink + jnp.where(lane1 == h, ds_sink, 0.0)
                    dss = ds * scale
                    dqs.append(_dot(dss, kd[j]))
                    acc_k[j] = acc_k[j] + _dot(dss, qm, TN)
                    acc_v[j] = acc_v[j] + _dot(p, dom, TN)
                dq = jnp.where(lo, dqs[0], dqs[1])
                dq_ref[:, sl] = dq.astype(dq_ref.dtype)
                dbq_ref[:, sl] += jnp.sum(dq, axis=0, keepdims=True)
            dsink_ref[...] += dsink
            tot_k = [a + pltpu.roll(a, 64, axis=1) for a in acc_k]
            tot_v = [a + pltpu.roll(a, 64, axis=1) for a in acc_v]
            dkv = jnp.concatenate([jnp.where(lo2, tot_k[0], tot_k[1]), jnp.where(lo2, tot_v[0], tot_v[1])], axis=1)
            dbkv_ref[...] += jnp.sum(dkv, axis=0, keepdims=True)
            dkv_ref[...] = (carry_ref[...] + dkv[:L, :]).astype(dkv_ref.dtype)
            carry_ref[...] = dkv[L:, :]

        @pl.when(i == nb)
        def _():
            dkv_ref[...] = carry_ref[...].astype(dkv_ref.dtype)

    c = lambda i: jnp.minimum(i, nb - 1)
    prev = lambda col: pl.BlockSpec((L, 128), lambda i: (jnp.maximum(c(i) - 1, 0), col))
    cur = lambda col: pl.BlockSpec((L, 128), lambda i: (c(i), col))
    row = lambda w: pl.BlockSpec((L, w), lambda i: (c(i), 0))
    cube = pl.BlockSpec((16, L, 2 * L), lambda i: (0, 0, 0))
    vec = lambda w: pl.BlockSpec((1, w), lambda i: (0, 0))
    return pl.pallas_call(
        body, name="attn_bwd", grid=(nb + 1,),
        in_specs=[SMEM, row(1024), cur(8), cur(9), prev(8), prev(9), row(1024), row(128), cube],
        out_specs=[row(1024), pl.BlockSpec((L, 256), lambda i: (jnp.maximum(i - 1, 0), 0)), cube,
                   vec(128), vec(1024), vec(256)],
        out_shape=[jax.ShapeDtypeStruct((S, 1024), BF16), jax.ShapeDtypeStruct((S, 256), BF16),
                   jax.ShapeDtypeStruct((16, L, 2 * L), F32), jax.ShapeDtypeStruct((1, 128), F32),
                   jax.ShapeDtypeStruct((1, 1024), F32), jax.ShapeDtypeStruct((1, 256), F32)],
        scratch_shapes=[pltpu.VMEM((L, 256), F32)],
        compiler_params=_params(("arbitrary",)))(sinks, qkv, qkv, qkv, qkv, qkv, d_o, lse, bias)


def _pad_lanes(a, n=128):
    return jnp.pad(a, ((0, 0), (0, n - a.shape[1])))


def _local_step(x, tgt, mod, W, P):
    md = [[mod[l:l + 1, k * D:(k + 1) * D] for k in range(6)] for l in range(2)]
    G, g = {}, {}

    sh1, sc1, g1, sh2, sc2, g2 = md[0]
    nmw0, nfw0 = P["norm_mix_w"][0:1], P["norm_ffn_w"][0:1]
    h0 = _norm_mod_fwd("norm_mix_0", x, nmw0, sc1, sh1)
    w_in = W["in_wt"]
    segs = {"z": w_in[0:1024], "xbc": w_in[1024:2560], "dt": jnp.pad(w_in[2560:2576], ((0, 112), (0, 0))),
            "u": w_in[2576:3600], "v": w_in[3600:4624]}
    proj = {k: _mm(f"in_proj_{k}", [h0], [w], "nt", [F32])[0] for k, w in segs.items()}
    conv_w, conv_b = P["conv_w"][0], P["conv_b"]
    pre, xc = _conv_fwd(proj["xbc"], conv_w, conv_b)
    dtb, alog = _pad_lanes(P["dt_bias"]), _pad_lanes(P["a_log"])
    dskl = jnp.repeat(P["d_skip"], 64, axis=1)
    ya, y_ssd, prev = _ssd_fwd(xc, proj["dt"], proj["z"], dtb, alog, dskl, P["ssm_norm_w"])
    ws = P["gmlp_ws"][0]
    bse = jnp.broadcast_to(P["gmlp_bs"][0][:, :, None], (8, L, 128))
    yb = _gmlp_fwd(proj["u"], proj["v"], P["gmlp_ln_w"], P["gmlp_ln_b"], ws, bse)
    w_oa, w_ob = W["out_w"][:1024], W["out_w"][1024:]

    def res(y, x, gate):
        return y, x + gate * y
    mix0, x1 = _mm("out_proj_0", [ya, yb], [w_oa, w_ob], "nn", [F32, F32], epi=res, extras=[x], vecs=[g1])
    h0f = _norm_mod_fwd("norm_ffn_0", x1, nfw0, sc2, sh2)
    a0, b0, f0, y0, x2 = _ffn_fwd("0", h0f, W["gate_wt"][0], W["up_wt"][0], W["down_w"][0], x1, g2)

    sh1b, sc1b, g1b, sh2b, sc2b, g2b = md[1]
    nmw1, nfw1 = P["norm_mix_w"][1:2], P["norm_ffn_w"][1:2]
    h1 = _norm_mod_fwd("norm_mix_1", x2, nmw1, sc1b, sh1b)
    qkv = _mm("qkv_proj", [h1], [W["qkv_wt"]], "nt", [F32], epi=lambda acc, b: acc + b, vecs=[P["qkv_b"]])[0]
    onehot_t = jnp.asarray(_bucket_onehot_t())
    bias = _rel_bias(P["rel_table"].T, onehot_t).reshape(16, L, 2 * L)
    sinks = P["sinks"].reshape(16)
    att, lse = _attn_fwd(qkv, bias, sinks)

    def res_b(y, x, gate, b):
        y = y + b
        return y, x + gate * y
    mix1, x3 = _mm("o_proj", [att], [W["o_w"]], "nn", [F32, F32], epi=res_b, extras=[x2], vecs=[g1b, P["o_b"]])
    h1f = _norm_mod_fwd("norm_ffn_1", x3, nfw1, sc2b, sh2b)
    a1, b1, f1, y1, x4 = _ffn_fwd("1", h1f, W["gate_wt"][1], W["up_wt"][1], W["down_w"][1], x3, g2b)

    dx, sq, g["final_norm_w"] = _loss_head(x4, tgt, P["final_norm_w"])

    dh, dg2b, dwg1, dwu1, dwd1 = _ffn_bwd("1", dx, h1f, a1, b1, f1, y1, W["gate_wt"][1], W["up_wt"][1],
                                          W["down_w"][1], g2b)
    dx, dsh2b, dsc2b, dnfw1 = _norm_mod_bwd("norm_ffn_bwd_1", x3, dh, dx, nfw1, sc2b)
    dmix, dg1b, g["o_b"] = _gate_bwd("mix_gate_bwd_1", dx, mix1, g1b)
    G["o_w"] = _mm_tn("o_dw", att, dmix)
    d_att = _mm("o_dx", [dmix], [W["o_w"]], "nt", [F32])[0]
    dq, dkv, dbias, dsinks, dbq, dbkv = _attn_bwd(qkv, d_att, lse, bias, sinks)
    g["rel_table"] = _rel_bias_bwd(dbias.reshape(16, L * 2 * L), onehot_t).T
    g["sinks"] = dsinks[:, :16]
    g["qkv_b"] = jnp.concatenate([dbq, dbkv], axis=1)
    w_q, w_kv = W["qkv_wt"][:1024], W["qkv_wt"][1024:]
    G["qkv_wt"] = jnp.concatenate([_mm_tn("qkv_dwq", dq, h1), _mm_tn("qkv_dwkv", dkv, h1)], axis=0)
    dh = _mm("qkv_dx", [dq, dkv], [w_q, w_kv], "nn", [F32])[0]
    dx, dsh1b, dsc1b, dnmw1 = _norm_mod_bwd("norm_mix_bwd_1", x2, dh, dx, nmw1, sc1b)

    dh, dg2, dwg0, dwu0, dwd0 = _ffn_bwd("0", dx, h0f, a0, b0, f0, y0, W["gate_wt"][0], W["up_wt"][0],
                                         W["down_w"][0], g2)
    dx, dsh2, dsc2, dnfw0 = _norm_mod_bwd("norm_ffn_bwd_0", x1, dh, dx, nfw0, sc2)
    dmix, dg1, _ = _gate_bwd("mix_gate_bwd_0", dx, mix0, g1)
    G["out_w"] = jnp.concatenate([_mm_tn("out_dwa", ya, dmix), _mm_tn("out_dwb", yb, dmix)], axis=0)
    dya = _mm("out_dxa", [dmix], [w_oa], "nt", [F32])[0]
    dyb = _mm("out_dxb", [dmix], [w_ob], "nt", [F32])[0]
    du, dv, dws, dbse, g["gmlp_ln_w"], g["gmlp_ln_b"] = _gmlp_bwd(dyb, proj["u"], proj["v"], P["gmlp_ln_w"],
                                                                 P["gmlp_ln_b"], ws, bse)
    g["gmlp_ws"] = dws[None]
    g["gmlp_bs"] = _lane_sum("gmlp_dbs", dbse.reshape(8 * L, 128)).reshape(1, 8, L)
    dz, dxc, ddt, g["ssm_norm_w"], ddsk, dalog, ddtb = _ssd_bwd(dya, y_ssd, proj["z"], xc, proj["dt"], prev,
                                                                dtb, alog, dskl, P["ssm_norm_w"])
    g["d_skip"], g["a_log"], g["dt_bias"] = ddsk[0:1, :16], dalog[:, :16], ddtb[:, :16]
    dxr, dconv_w, g["conv_b"] = _conv_bwd(dxc, pre, proj["xbc"], conv_w)
    g["conv_w"] = dconv_w[None]
    dsegs = {"z": dz, "xbc": dxr, "dt": ddt, "u": du, "v": dv}
    dws_in = {k: _mm_tn(f"in_dw_{k}", d, h0) for k, d in dsegs.items()}
    G["in_wt"] = jnp.concatenate([dws_in["z"], dws_in["xbc"], dws_in["dt"][:16], dws_in["u"], dws_in["v"]], axis=0)
    keys = ["z", "xbc", "dt", "u", "v"]
    dh = _mm("in_dx", [dsegs[k] for k in keys], [segs[k] for k in keys], "nn", [F32])[0]
    dx, dsh1, dsc1, dnmw0 = _norm_mod_bwd("norm_mix_bwd_0", x, dh, dx, nmw0, sc1)

    G["gate_wt"], G["up_wt"], G["down_w"] = [dwg0, dwg1], [dwu0, dwu1], [dwd0, dwd1]
    g["norm_mix_w"] = jnp.concatenate([dnmw0, dnmw1], axis=0)
    g["norm_ffn_w"] = jnp.concatenate([dnfw0, dnfw1], axis=0)
    dmod = jnp.concatenate([jnp.concatenate([dsh1, dsc1, dg1, dsh2, dsc2, dg2], axis=1),
                            jnp.concatenate([dsh1b, dsc1b, dg1b, dsh2b, dsc2b, dg2b], axis=1)], axis=0)
    return sq, dx, dmod, G, g


def _ada_fwd(c_all, ada_w, ada_b):
    n = ada_w.shape[2]
    tn = _col_tile(n, 512)

    def body(c_ref, w_ref, b_ref, o_ref):
        cc = c_ref[...]
        o_ref[...] = lax.dot_general(cc * _sigmoid(cc), w_ref[...], NN, precision=lax.Precision.HIGHEST,
                                     preferred_element_type=F32) + b_ref[...]

    return pl.pallas_call(
        body, name="ada_fwd", grid=(2, n // tn),
        in_specs=[pl.BlockSpec((8, D), lambda l, j: (0, 0)), pl.BlockSpec((None, D, tn), lambda l, j: (l, 0, j)),
                  pl.BlockSpec((None, 1, tn), lambda l, j: (l, 0, j))],
        out_specs=pl.BlockSpec((None, 8, tn), lambda l, j: (l, 0, j)),
        out_shape=jax.ShapeDtypeStruct((2, 8, n), F32), compiler_params=_params(("parallel", "parallel")))(
            c_all, ada_w, ada_b)


def _ada_bwd(c_all, dmod_cols, dmod_all):
    n = dmod_cols.shape[2]
    tn = _col_tile(n, 512)

    def body(c_ref, d_ref, o_ref):
        cc = c_ref[...]
        o_ref[...] = lax.dot_general(cc * _sigmoid(cc), d_ref[...], TN, precision=lax.Precision.HIGHEST,
                                     preferred_element_type=F32)

    dw = pl.pallas_call(
        body, name="ada_dw", grid=(2, n // tn),
        in_specs=[pl.BlockSpec((8, D), lambda l, j: (0, 0)), pl.BlockSpec((None, 8, tn), lambda l, j: (l, 0, j))],
        out_specs=pl.BlockSpec((None, D, tn), lambda l, j: (l, 0, j)),
        out_shape=jax.ShapeDtypeStruct((2, D, n), F32), compiler_params=_params(("parallel", "parallel")))(
            c_all, dmod_cols)

    def sum_body(d_ref, o_ref):
        o_ref[...] = jnp.sum(d_ref[...], axis=0, keepdims=True)

    db = pl.pallas_call(
        sum_body, name="ada_db", grid=(2,),
        in_specs=[pl.BlockSpec((None, 8, 6 * D), lambda l: (l, 0, 0))],
        out_specs=pl.BlockSpec((None, 1, 6 * D), lambda l: (l, 0, 0)),
        out_shape=jax.ShapeDtypeStruct((2, 1, 6 * D), F32), compiler_params=_params(("parallel",)))(dmod_all)
    return dw, db


def _row_tile(rows, cap=512):
    best = rows
    for t in range(8, min(rows, cap) + 1, 8):
        if rows % t == 0:
            best = t
    return best


def _adamw(name, w, g, m, v):
    def fn(w, g, m, v):
        m = ADAM_B1 * m + (1.0 - ADAM_B1) * g
        v = ADAM_B2 * v + (1.0 - ADAM_B2) * (g * g)
        m_hat = m / (1.0 - ADAM_B1 ** ADAM_STEP)
        v_hat = v / (1.0 - ADAM_B2 ** ADAM_STEP)
        return -ADAM_LR * (m_hat / (jnp.sqrt(v_hat) + ADAM_EPS) + ADAM_WD * w), m, v
    cols = w.shape[1]
    return _rowwise(name, fn, [w, g, m, v], [], [(cols, F32)] * 3, tr=_row_tile(w.shape[0]))


def _place():
    return lax.axis_index("x"), lax.axis_index("y"), lax.axis_index("c")


VMEM_SPEC = pl.BlockSpec(memory_space=pltpu.VMEM)


def _allreduce_small(name, buf):
    rows = buf.shape[0]

    def body(x_ref, o_ref, stage, send_sems, recv_sems):
        x, y, c = _place()
        me = 4 * x + 2 * y + c
        stage[me] = x_ref[...]
        copies = []
        for k in range(1, 8):
            peer = (1 - x if k & 4 else x, 1 - y if k & 2 else y, 1 - c if k & 1 else c)
            cp = pltpu.make_async_remote_copy(src_ref=x_ref, dst_ref=stage.at[me], send_sem=send_sems.at[k - 1],
                                              recv_sem=recv_sems.at[k - 1], device_id=peer, device_id_type=MESH)
            cp.start()
            copies.append(cp)
        for cp in copies:
            cp.wait()
        acc = stage[0]
        for d in range(1, 8):
            acc = acc + stage[d]
        o_ref[...] = acc

    return pl.pallas_call(
        body, name=name, in_specs=[VMEM_SPEC], out_specs=VMEM_SPEC,
        out_shape=jax.ShapeDtypeStruct((rows, 128), F32),
        scratch_shapes=[pltpu.VMEM((8, rows, 128), F32), pltpu.SemaphoreType.DMA((7,)), pltpu.SemaphoreType.DMA((7,))],
        compiler_params=pltpu.CompilerParams(vmem_limit_bytes=_VMEM_LIMIT))(buf)


OTHER_CHIPS = ((1, 0), (0, 1), (1, 1))


def _allgather_big(wp):
    rows = wp.shape[0]
    half = rows // 2

    def body(w_ref, o_ref, send_sems, recv_sems, local_sem):
        x, y, c = _place()
        k = 2 * x + y
        mine = pl.ds(pl.multiple_of(c * half, 8), half)
        theirs = pl.ds(pl.multiple_of((1 - c) * half, 8), half)
        local = pltpu.make_async_copy(w_ref, o_ref.at[k], local_sem)
        local.start()
        chips = [(1 - x if fx else x, 1 - y if fy else y) for fx, fy in OTHER_CHIPS]
        idx = [2 * px + py for px, py in chips]
        first = []
        for j, (px, py) in enumerate(chips):
            cp = pltpu.make_async_remote_copy(src_ref=w_ref.at[mine], dst_ref=o_ref.at[k, mine],
                                              send_sem=send_sems.at[j], recv_sem=recv_sems.at[j],
                                              device_id=(px, py, c), device_id_type=MESH)
            cp.start()
            first.append(cp)
        passed = []
        for j in range(3):
            blk = o_ref.at[idx[j], mine]
            pltpu.make_async_remote_copy(src_ref=blk, dst_ref=blk, send_sem=send_sems.at[j], recv_sem=recv_sems.at[j],
                                         device_id=(x, y, c), device_id_type=MESH).wait_recv()
            cp = pltpu.make_async_remote_copy(src_ref=blk, dst_ref=blk, send_sem=send_sems.at[3 + j],
                                              recv_sem=recv_sems.at[3 + j], device_id=(x, y, 1 - c),
                                              device_id_type=MESH)
            cp.start()
            passed.append(cp)
        for j in range(3):
            blk = o_ref.at[idx[j], theirs]
            pltpu.make_async_remote_copy(src_ref=blk, dst_ref=blk, send_sem=send_sems.at[3 + j],
                                         recv_sem=recv_sems.at[3 + j], device_id=(x, y, c),
                                         device_id_type=MESH).wait_recv()
        for cp in first + passed:
            cp.wait_send()
        local.wait()

    return pl.pallas_call(
        body, name="allgather_weights", in_specs=[ANY], out_specs=ANY,
        out_shape=jax.ShapeDtypeStruct((4, rows, 1024), wp.dtype),
        scratch_shapes=[pltpu.SemaphoreType.DMA((6,)), pltpu.SemaphoreType.DMA((6,)), pltpu.SemaphoreType.DMA])(wp)


def _sibling_swap(name, src, halves):
    half = src.shape[-2] // 2
    out_shape = (src.shape[0], half, 1024) if halves else src.shape

    def body(s_ref, o_ref, send_sem, recv_sem):
        x, y, c = _place()
        part = s_ref.at[:, pl.ds(pl.multiple_of((1 - c) * half, 8), half)] if halves else s_ref
        cp = pltpu.make_async_remote_copy(src_ref=part, dst_ref=o_ref, send_sem=send_sem, recv_sem=recv_sem,
                                          device_id=(x, y, 1 - c), device_id_type=MESH)
        cp.start()
        cp.wait()

    return pl.pallas_call(
        body, name=name, in_specs=[ANY], out_specs=ANY, out_shape=jax.ShapeDtypeStruct(out_shape, src.dtype),
        scratch_shapes=[pltpu.SemaphoreType.DMA, pltpu.SemaphoreType.DMA])(src)


def _chip_exchange(p):
    def body(p_ref, q_ref, send_sems, recv_sems, local_sem):
        x, y, c = _place()
        k = 2 * x + y
        local = pltpu.make_async_copy(p_ref.at[k], q_ref.at[k], local_sem)
        local.start()
        copies = []
        for j, (fx, fy) in enumerate(OTHER_CHIPS):
            px, py = (1 - x if fx else x), (1 - y if fy else y)
            cp = pltpu.make_async_remote_copy(src_ref=p_ref.at[2 * px + py], dst_ref=q_ref.at[k],
                                              send_sem=send_sems.at[j], recv_sem=recv_sems.at[j],
                                              device_id=(px, py, c), device_id_type=MESH)
            cp.start()
            copies.append(cp)
        for cp in copies:
            cp.wait()
        local.wait()

    return pl.pallas_call(
        body, name="grad_chip_exchange", in_specs=[ANY], out_specs=ANY,
        out_shape=jax.ShapeDtypeStruct(p.shape, p.dtype),
        scratch_shapes=[pltpu.SemaphoreType.DMA((3,)), pltpu.SemaphoreType.DMA((3,)), pltpu.SemaphoreType.DMA])(p)


def _pair_sum(g, r1, c):
    rows = g.shape[1]
    half = rows // 2
    th = _row_tile(half, 256)
    nblk = half // th

    def body(c_ref, g_ref, r_ref, o_ref):
        o_ref[...] = g_ref[...] + r_ref[...]

    spec = pl.BlockSpec((None, th, 1024), lambda k, i, c_ref: (k, i, 0))
    grid_spec = pltpu.PrefetchScalarGridSpec(
        num_scalar_prefetch=1, grid=(4, nblk),
        in_specs=[pl.BlockSpec((None, th, 1024), lambda k, i, c_ref: (k, c_ref[0] * nblk + i, 0)), spec],
        out_specs=spec)
    return pl.pallas_call(body, name="grad_pair_sum", grid_spec=grid_spec,
                          out_shape=jax.ShapeDtypeStruct((4, half, 1024), F32),
                          compiler_params=_params(("parallel", "parallel")))(c, g, r1)


def _chip_sum(q):
    half = q.shape[1]
    th = _row_tile(half, 256)

    def body(a, b, c, d, o_ref):
        o_ref[...] = ((a[...] + b[...]) + c[...]) + d[...]

    specs = [pl.BlockSpec((None, th, 1024), functools.partial(lambda i, k: (k, i, 0), k=k)) for k in range(4)]
    return pl.pallas_call(body, name="grad_chip_sum", grid=(half // th,), in_specs=specs,
                          out_specs=pl.BlockSpec((th, 1024), lambda i: (i, 0)),
                          out_shape=jax.ShapeDtypeStruct((half, 1024), F32),
                          compiler_params=_params(("parallel",)))(q, q, q, q)


def _join_halves(f, r, c):
    half = f.shape[0]
    th = _row_tile(half, 256)
    nblk = half // th

    def body(c_ref, f_ref, r_ref, o_ref):
        mine = (pl.program_id(0) == c_ref[0])
        o_ref[...] = jnp.where(mine, f_ref[...], r_ref[...])

    spec = pl.BlockSpec((th, 1024), lambda h, i, c_ref: (i, 0))
    grid_spec = pltpu.PrefetchScalarGridSpec(
        num_scalar_prefetch=1, grid=(2, nblk), in_specs=[spec, spec],
        out_specs=pl.BlockSpec((th, 1024), lambda h, i, c_ref: (h * nblk + i, 0)))
    return pl.pallas_call(body, name="grad_join_halves", grid_spec=grid_spec,
                          out_shape=jax.ShapeDtypeStruct((2 * half, 1024), F32),
                          compiler_params=_params(("parallel", "parallel")))(c, f, r)


BIG_ARGS = ("in_w_even", "out_w_even", "qkv_w", "o_w", "ffn_gate_w", "ffn_up_w", "ffn_down_w")
SLAB = (("in_wt", 1156, 1168), ("out_w", 512, 512), ("qkv_wt", 320, 320), ("o_w", 256, 256),
        ("gate_wt0", 704, 704), ("gate_wt1", 704, 704), ("up_wt0", 704, 704), ("up_wt1", 704, 704),
        ("down_w0", 704, 704), ("down_w1", 704, 704))
SLAB_ROWS = 6528


def _slab(pieces):
    parts, used = [], 0
    for name, rows, room in SLAB:
        p = pieces[name]
        parts.append(jnp.pad(p, [(0, 0)] * (p.ndim - 2) + [(0, room - rows), (0, 0)]) if room > rows else p)
        used += room
    parts.append(jnp.zeros(parts[0].shape[:-2] + (SLAB_ROWS - used, D), parts[0].dtype))
    return jnp.concatenate(parts, axis=-2)


def _unslab(slab):
    out, off = {}, 0
    for name, rows, room in SLAB:
        out[name] = slab[..., off:off + rows, :]
        off += room
    return out


def _share_pieces(w):
    return {"in_wt": w["in_w_even"][0].T, "out_w": w["out_w_even"][0], "qkv_wt": w["qkv_w"][0].T, "o_w": w["o_w"][0],
            "gate_wt0": w["ffn_gate_w"][0].T, "gate_wt1": w["ffn_gate_w"][1].T,
            "up_wt0": w["ffn_up_w"][0].T, "up_wt1": w["ffn_up_w"][1].T,
            "down_w0": w["ffn_down_w"][0], "down_w1": w["ffn_down_w"][1]}


def _pieces_to_shares(p):
    return {"in_w_even": p["in_wt"].T[None], "out_w_even": p["out_w"][None], "qkv_w": p["qkv_wt"].T[None],
            "o_w": p["o_w"][None], "ffn_gate_w": jnp.stack([p["gate_wt0"].T, p["gate_wt1"].T]),
            "ffn_up_w": jnp.stack([p["up_wt0"].T, p["up_wt1"].T]),
            "ffn_down_w": jnp.stack([p["down_w0"], p["down_w1"]])}


def _full_from_chips(p):
    whole = {k: v.reshape(-1, D) for k, v in p.items()}
    return {"in_wt": whole["in_wt"], "out_w": whole["out_w"], "qkv_wt": whole["qkv_wt"], "o_w": whole["o_w"],
            "gate_wt": [whole["gate_wt0"], whole["gate_wt1"]], "up_wt": [whole["up_wt0"], whole["up_wt1"]],
            "down_w": [whole["down_w0"], whole["down_w1"]]}


def _chips_from_full(G):
    flat = {"in_wt": G["in_wt"], "out_w": G["out_w"], "qkv_wt": G["qkv_wt"], "o_w": G["o_w"],
            "gate_wt0": G["gate_wt"][0], "gate_wt1": G["gate_wt"][1], "up_wt0": G["up_wt"][0], "up_wt1": G["up_wt"][1],
            "down_w0": G["down_w"][0], "down_w1": G["down_w"][1]}
    return {k: v.reshape(4, -1, D) for k, v in flat.items()}


def _pack_small(parts):
    padded = []
    for p in parts:
        p = p.reshape(-1).astype(F32)
        padded.append(jnp.pad(p, (0, (-p.shape[0]) % 1024)))
    return jnp.concatenate(padded).reshape(-1, 128)


def _unpack_small(slab, shapes):
    flat, out, off = slab.reshape(-1), [], 0
    for shp in shapes:
        size = math.prod(shp)
        out.append(flat[off:off + size].reshape(shp))
        off += size + (-size) % 1024
    return out


SMALL = ("ada_b", "norm_mix_w", "norm_ffn_w", "conv_w", "conv_b", "dt_bias", "a_log", "d_skip", "ssm_norm_w",
         "gmlp_ln_w", "gmlp_ln_b", "gmlp_ws", "gmlp_bs", "qkv_b", "o_b", "sinks", "rel_table", "final_norm_w")
SMALL_SPLIT = {"conv_w": 1536, "qkv_b": 1280, "o_b": 1024}
WEIGHTS = ("ada_w", "ada_b", "norm_mix_w", "norm_ffn_w", "in_w_even", "conv_w", "conv_b", "dt_bias", "a_log", "d_skip",
           "ssm_norm_w", "gmlp_ln_w", "gmlp_ln_b", "gmlp_ws", "gmlp_bs", "out_w_even", "qkv_w", "qkv_b", "o_w", "o_b",
           "sinks", "rel_table", "ffn_gate_w", "ffn_up_w", "ffn_down_w", "final_norm_w")


def kernel(x, c, ada_w, ada_b, norm_mix_w, norm_ffn_w, in_w_even, conv_w, conv_b, dt_bias, a_log, d_skip, ssm_norm_w, gmlp_ln_w, gmlp_ln_b, gmlp_ws, gmlp_bs, out_w_even, qkv_w, qkv_b, o_w, o_b, sinks, rel_table, ffn_gate_w, ffn_up_w, ffn_down_w, final_norm_w, loss_target, m_ada_w, m_ada_b, m_norm_mix_w, m_norm_ffn_w, m_in_w_even, m_conv_w, m_conv_b, m_dt_bias, m_a_log, m_d_skip, m_ssm_norm_w, m_gmlp_ln_w, m_gmlp_ln_b, m_gmlp_ws, m_gmlp_bs, m_out_w_even, m_qkv_w, m_qkv_b, m_o_w, m_o_b, m_sinks, m_rel_table, m_ffn_gate_w, m_ffn_up_w, m_ffn_down_w, m_final_norm_w, v_ada_w, v_ada_b, v_norm_mix_w, v_norm_ffn_w, v_in_w_even, v_conv_w, v_conv_b, v_dt_bias, v_a_log, v_d_skip, v_ssm_norm_w, v_gmlp_ln_w, v_gmlp_ln_b, v_gmlp_ws, v_gmlp_bs, v_out_w_even, v_qkv_w, v_qkv_b, v_o_w, v_o_b, v_sinks, v_rel_table, v_ffn_gate_w, v_ffn_up_w, v_ffn_down_w, v_final_norm_w):
    args = dict(locals())
    w = {n: args[n] for n in WEIGHTS}
    m = {n: args["m_" + n] for n in WEIGHTS}
    v = {n: args["v_" + n] for n in WEIGHTS}
    ax, ay, ac = _place()
    me = 4 * ax + 2 * ay + ac
    chip = 2 * ax + ay
    south = (ac == 0).astype(F32)
    c_arr = jnp.reshape(ac, (1,)).astype(jnp.int32)

    share = _slab({k: p.astype(_MXU) for k, p in _share_pieces(w).items()})
    W = _full_from_chips(_unslab(_allgather_big(share)))

    c_all = _allreduce_small("gather_cond", lax.dynamic_update_slice(jnp.zeros((8, D), F32), c, (me, 0)).reshape(64, 128))
    c_all = c_all.reshape(8, D)
    n_ada = ada_w.shape[2]
    mod_cols = _ada_fwd(c_all, ada_w, lax.dynamic_slice(ada_b, (0, chip * n_ada), (2, n_ada)).reshape(2, 1, n_ada))
    pieces = [lax.dynamic_update_slice(jnp.zeros((2, 8, 6 * D), F32), mod_cols, (0, 0, chip * n_ada))]
    split_names = list(SMALL_SPLIT)
    for n in split_names:
        full = SMALL_SPLIT[n]
        local = w[n]
        idx = (0,) * (local.ndim - 1) + (chip * local.shape[-1],)
        pieces.append(lax.dynamic_update_slice(jnp.zeros(local.shape[:-1] + (full,), F32), local, idx))
    shapes = [p.shape for p in pieces]
    gathered = _unpack_small(_allreduce_small("gather_mod", _pack_small(pieces) * south), shapes)
    mod = lax.dynamic_slice(gathered[0], (0, me, 0), (2, 1, 6 * D)).reshape(2, 6 * D)
    P = {n: w[n] for n in SMALL if n not in SMALL_SPLIT and n != "ada_b"}
    for n, full in zip(split_names, gathered[1:]):
        P[n] = full
    P["final_norm_w"] = final_norm_w.reshape(1, D)

    sq, grad_x, dmod, G, g = _local_step(x[0], loss_target[0], mod, W, P)
    loss = lax.psum(0.5 * sq[0, 0] / D, ("x", "y", "c"))

    g["final_norm_w"] = g["final_norm_w"].reshape(D)
    small_names = [n for n in SMALL if n != "ada_b"]
    pieces = [lax.dynamic_update_slice(jnp.zeros((2, 8, 6 * D), F32), dmod.reshape(2, 1, 6 * D), (0, me, 0))]
    pieces += [g[n] for n in small_names]
    shapes = [p.shape for p in pieces]
    reduced = _unpack_small(_allreduce_small("allreduce_small_grads", _pack_small(pieces)), shapes)
    dmod_all = reduced[0]
    grads = dict(zip(small_names, reduced[1:]))
    for n in split_names:
        full = grads[n]
        size = w[n].shape[-1]
        grads[n] = lax.dynamic_slice(full, (0,) * (full.ndim - 1) + (chip * size,), full.shape[:-1] + (size,))
    grads = {n: grads[n].reshape(w[n].shape) for n in small_names}
    dw_ada, db_ada = _ada_bwd(c_all, lax.dynamic_slice(dmod_all, (0, 0, chip * n_ada), (2, 8, n_ada)), dmod_all)
    grads["ada_w"], grads["ada_b"] = dw_ada, db_ada.reshape(2, 6 * D)

    gp = _slab(_chips_from_full(G))
    pair = _pair_sum(gp, _sibling_swap("grad_pair_exchange", gp, True), c_arr)
    fin = _chip_sum(_chip_exchange(pair))
    total = _join_halves(fin, _sibling_swap("grad_final_exchange", fin, False), c_arr)
    grads.update(_pieces_to_shares(_unslab(total)))

    delta, new_m, new_v = {}, {}, {}
    for n in ("ada_w",) + BIG_ARGS:
        cols = w[n].shape[-1]
        d_, m_, v_ = _adamw("adamw_" + n, w[n].reshape(-1, cols), grads[n].reshape(-1, cols), m[n].reshape(-1, cols),
                            v[n].reshape(-1, cols))
        delta[n], new_m[n], new_v[n] = d_.reshape(w[n].shape), m_.reshape(w[n].shape), v_.reshape(w[n].shape)
    shapes = [w[n].shape for n in SMALL]
    packed = [_pack_small([t[n] for n in SMALL]) for t in (w, grads, m, v)]
    outs = _adamw("adamw_small", *packed)
    for dst, slab in zip((delta, new_m, new_v), outs):
        for n, t in zip(SMALL, _unpack_small(slab, shapes)):
            dst[n] = t
    return (loss, grad_x[None], *[grads[n] for n in WEIGHTS], *[delta[n] for n in WEIGHTS],
            *[new_m[n] for n in WEIGHTS], *[new_v[n] for n in WEIGHTS])
```

```python
import functools
import math

import numpy as np
import jax
import jax.numpy as jnp
from jax import lax
from jax.experimental import pallas as pl
from jax.experimental.pallas import tpu as pltpu

F32 = jnp.float32
BF16 = jnp.bfloat16
_MXU = jnp.bfloat16
_VMEM_LIMIT = 56 * 1024 * 1024
D = 1024
L = 128
NSTATE = 128
EPS = 1e-6
NEG_INF = -1e30
FFN = 2816
ADAM_LR, ADAM_B1, ADAM_B2, ADAM_EPS, ADAM_WD, ADAM_STEP = 0.001, 0.9, 0.999, 1e-08, 0.01, 10
MESH = pl.DeviceIdType.MESH
ANY = pl.BlockSpec(memory_space=pl.ANY)

NN = (((1,), (0,)), ((), ()))
NT = (((1,), (1,)), ((), ()))
TN = (((0,), (0,)), ((), ()))


def _dot(a, b, dn=NN):
    return lax.dot_general(a.astype(_MXU), b.astype(_MXU), dn, preferred_element_type=F32)


def _params(sem=None):
    return pltpu.CompilerParams(dimension_semantics=sem, vmem_limit_bytes=_VMEM_LIMIT)


def _sigmoid(x):
    return 1.0 / (1.0 + jnp.exp(-x))


def _softplus(x):
    return jnp.maximum(x, 0.0) + jnp.log(1.0 + jnp.exp(-jnp.abs(x)))


def _gelu(x):
    return 0.5 * x * (1.0 + lax.erf(x * (2.0 ** -0.5)))


def _gelu_grad(x):
    return 0.5 * (1.0 + lax.erf(x * (2.0 ** -0.5))) + x * jnp.exp(-0.5 * x * x) * (1.0 / math.sqrt(2.0 * math.pi))


def _silu_grad(a):
    sg = _sigmoid(a)
    return sg * (1.0 + a * (1.0 - sg))


def _rowwise(name, fn, rows, vecs, out_rows, out_accs=(), tr=512):
    S = rows[0].shape[0]
    tr = min(tr, S)
    assert S % tr == 0
    nr, nv, no, na = len(rows), len(vecs), len(out_rows), len(out_accs)

    def body(*refs):
        ins, outs = refs[:nr + nv], refs[nr + nv:]
        res = fn(*[r[...] for r in ins])
        if not isinstance(res, (tuple, list)):
            res = (res,)
        for k in range(no):
            outs[k][...] = res[k].astype(outs[k].dtype)
        if na:
            @pl.when(pl.program_id(0) == 0)
            def _():
                for k in range(na):
                    outs[no + k][...] = jnp.zeros_like(outs[no + k])
            for k in range(na):
                outs[no + k][...] += res[no + k]

    in_specs = [pl.BlockSpec((tr, a.shape[1]), lambda i: (i, 0)) for a in rows]
    in_specs += [pl.BlockSpec(v.shape, lambda i: (0, 0)) for v in vecs]
    out_specs = [pl.BlockSpec((tr, c), lambda i: (i, 0)) for c, _ in out_rows]
    out_specs += [pl.BlockSpec(s, lambda i: (0, 0)) for s in out_accs]
    out_shape = [jax.ShapeDtypeStruct((S, c), dt) for c, dt in out_rows]
    out_shape += [jax.ShapeDtypeStruct(s, F32) for s in out_accs]
    return pl.pallas_call(body, name=name, grid=(S // tr,), in_specs=in_specs, out_specs=out_specs,
                          out_shape=out_shape, compiler_params=_params(("arbitrary",)))(*rows, *vecs)


def _col_tile(n, cap):
    if n <= cap or n % 128:
        return n
    best = 128
    for t in range(128, cap + 1, 128):
        if n % t == 0:
            best = t
    return best


def _mm(name, As, Bs, mode, outs, epi=None, groups=None, extras=(), vecs=(), tm=512, tn_cap=1536):
    M = As[0].shape[0]
    N = Bs[0].shape[1] if mode == "nn" else Bs[0].shape[0]
    tm = min(tm, M)
    tn = _col_tile(N, tn_cap)
    assert M % tm == 0 and N % tn == 0
    npair = len(As)
    groups = groups or [0] * npair
    ng = max(groups) + 1
    nx, nv = len(extras), len(vecs)
    dn = NN if mode == "nn" else NT

    def body(*refs):
        a_refs, b_refs = refs[:npair], refs[npair:2 * npair]
        x_refs = refs[2 * npair:2 * npair + nx]
        v_refs = refs[2 * npair + nx:2 * npair + nx + nv]
        o_refs = refs[2 * npair + nx + nv:]
        accs = [None] * ng
        for k in range(npair):
            d = _dot(a_refs[k][...], b_refs[k][...], dn)
            accs[groups[k]] = d if accs[groups[k]] is None else accs[groups[k]] + d
        args = accs + [x[...] for x in x_refs] + [v[...] for v in v_refs]
        res = epi(*args) if epi is not None else tuple(accs)
        if not isinstance(res, (tuple, list)):
            res = (res,)
        for o, r in zip(o_refs, res):
            o[...] = r.astype(o.dtype)

    in_specs = [pl.BlockSpec((tm, a.shape[1]), lambda i, j: (i, 0)) for a in As]
    if mode == "nn":
        in_specs += [pl.BlockSpec((b.shape[0], tn), lambda i, j: (0, j)) for b in Bs]
    else:
        in_specs += [pl.BlockSpec((tn, b.shape[1]), lambda i, j: (j, 0)) for b in Bs]
    in_specs += [pl.BlockSpec((tm, tn), lambda i, j: (i, j)) for _ in extras]
    in_specs += [pl.BlockSpec((1, tn), lambda i, j: (0, j)) for _ in vecs]
    out_specs = [pl.BlockSpec((tm, tn), lambda i, j: (i, j)) for _ in outs]
    out_shape = [jax.ShapeDtypeStruct((M, N), dt) for dt in outs]
    return pl.pallas_call(body, name=name, grid=(M // tm, N // tn), in_specs=in_specs, out_specs=out_specs,
                          out_shape=out_shape, compiler_params=_params(("parallel", "parallel")))(
                              *As, *Bs, *extras, *vecs)


def _mm_tn(name, A, B, tk=512, t2_cap=1536):
    S, K1 = A.shape
    N2 = B.shape[1]
    tk = min(tk, S)
    t2 = _col_tile(N2, t2_cap)
    assert S % tk == 0 and N2 % t2 == 0

    def body(a_ref, b_ref, o_ref):
        @pl.when(pl.program_id(1) == 0)
        def _():
            o_ref[...] = jnp.zeros_like(o_ref)
        o_ref[...] += _dot(a_ref[...], b_ref[...], TN)

    return pl.pallas_call(
        body, name=name, grid=(N2 // t2, S // tk),
        in_specs=[pl.BlockSpec((tk, K1), lambda j, k: (k, 0)), pl.BlockSpec((tk, t2), lambda j, k: (k, j))],
        out_specs=pl.BlockSpec((K1, t2), lambda j, k: (0, j)),
        out_shape=jax.ShapeDtypeStruct((K1, N2), F32),
        compiler_params=_params(("parallel", "arbitrary")))(A, B)


def _norm_mod_fwd(name, x, nw, sc, sh):
    def fn(x, nw, sc, sh):
        rstd = lax.rsqrt(jnp.mean(x * x, axis=-1, keepdims=True) + EPS)
        return (x * rstd * nw) * (1.0 + sc) + sh
    return _rowwise(name, fn, [x], [nw, sc, sh], [(D, BF16)])[0]


def _norm_mod_bwd(name, x, dh, dres, nw, sc):
    def fn(x, dh, dres, nw, sc):
        rstd = lax.rsqrt(jnp.mean(x * x, axis=-1, keepdims=True) + EPS)
        xh = x * rstd
        dn = dh * (1.0 + sc)
        dxh = dn * nw
        dx = rstd * (dxh - xh * jnp.mean(dxh * xh, axis=-1, keepdims=True))
        return (dres + dx, jnp.sum(dh, axis=0, keepdims=True), jnp.sum(dh * (xh * nw), axis=0, keepdims=True),
                jnp.sum(dn * xh, axis=0, keepdims=True))
    return _rowwise(name, fn, [x, dh, dres], [nw, sc], [(D, F32)], [(1, D)] * 3)


def _gate_bwd(name, dx, y, g):
    def fn(dx, y, g):
        dy = dx * g
        return dy, jnp.sum(dx * y, axis=0, keepdims=True), jnp.sum(dy, axis=0, keepdims=True)
    return _rowwise(name, fn, [dx, y], [g], [(D, BF16)], [(1, D)] * 2)


def _loss_head(x, tgt, fw):
    def fn(x, tgt, fw):
        rstd = lax.rsqrt(jnp.mean(x * x, axis=-1, keepdims=True) + EPS)
        xh = x * rstd
        err = xh * fw - tgt
        dout = err * (1.0 / D)
        dxh = dout * fw
        dx = rstd * (dxh - xh * jnp.mean(dxh * xh, axis=-1, keepdims=True))
        sq = jnp.sum(jnp.sum(err * err, axis=1, keepdims=True), axis=0, keepdims=True)
        return dx, sq, jnp.sum(dout * xh, axis=0, keepdims=True)
    return _rowwise("loss_head", fn, [x, tgt], [fw], [(D, F32)], [(1, 1), (1, D)])


def _ffn_fwd(tag, h, wg, wu, wd, x, g2):
    def act(a, b):
        return a, b, a * _sigmoid(a) * b
    a, b, f = _mm(f"ffn_up_{tag}", [h, h], [wg, wu], "nt", [F32, F32, BF16], epi=act, groups=[0, 1], tn_cap=1408)

    def res(y, x, g):
        return y, x + g * y
    y, xo = _mm(f"ffn_down_{tag}", [f], [wd], "nn", [F32, F32], epi=res, extras=[x], vecs=[g2])
    return a, b, f, y, xo


def _ffn_bwd(tag, dx, h, a, b, f, y, wg, wu, wd, g2):
    dy, dg2, _ = _gate_bwd(f"ffn_gate_bwd_{tag}", dx, y, g2)

    def act_bwd(df, a, b):
        return df * b * _silu_grad(a), df * (a * _sigmoid(a))
    da, db = _mm(f"ffn_dact_{tag}", [dy], [wd], "nt", [BF16, BF16], epi=act_bwd, extras=[a, b], tn_cap=1408)
    dwd = _mm_tn(f"ffn_dwd_{tag}", f, dy)
    dwg = _mm_tn(f"ffn_dwg_{tag}", da, h)
    dwu = _mm_tn(f"ffn_dwu_{tag}", db, h)
    dh = _mm(f"ffn_dh_{tag}", [da, db], [wg, wu], "nn", [F32])[0]
    return dh, dg2, dwg, dwu, dwd


def _conv_fwd(xr, w, b, tb=512):
    S, C = xr.shape
    tb = min(tb, S)

    def body(x_ref, halo_ref, w_ref, b_ref, pre_ref, out_ref):
        i = pl.program_id(0)
        halo = jnp.where(i > 0, halo_ref[...], 0.0)
        xe = jnp.concatenate([halo, x_ref[...]], axis=0)
        pre = w_ref[3:4, :] * x_ref[...] + b_ref[...]
        for j in (1, 2, 3):
            pre = pre + w_ref[3 - j:4 - j, :] * pltpu.roll(xe, j, axis=0)[8:, :]
        pre_ref[...] = pre
        out_ref[...] = pre * _sigmoid(pre)

    return pl.pallas_call(
        body, name="conv_fwd", grid=(S // tb,),
        in_specs=[pl.BlockSpec((tb, C), lambda i: (i, 0)),
                  pl.BlockSpec((8, C), lambda i: (jnp.maximum(i * (tb // 8) - 1, 0), 0)),
                  pl.BlockSpec((4, C), lambda i: (0, 0)), pl.BlockSpec((1, C), lambda i: (0, 0))],
        out_specs=[pl.BlockSpec((tb, C), lambda i: (i, 0))] * 2,
        out_shape=[jax.ShapeDtypeStruct((S, C), F32)] * 2,
        compiler_params=_params(("parallel",)))(xr, xr, w, b)


def _conv_bwd(dxc, pre, xr, w, tb=512):
    S, C = xr.shape
    tb = min(tb, S)
    nblk = S // tb

    def body(d_ref, p_ref, dn_ref, pn_ref, x_ref, xh_ref, w_ref, dx_ref, dw_ref, db_ref):
        i = pl.program_id(0)

        @pl.when(i == 0)
        def _():
            dw_ref[...] = jnp.zeros_like(dw_ref)
            db_ref[...] = jnp.zeros_like(db_ref)

        dpre = d_ref[...] * _silu_grad(p_ref[...])
        dnext = jnp.where(i < nblk - 1, dn_ref[...] * _silu_grad(pn_ref[...]), 0.0)
        pe = jnp.concatenate([dpre, dnext], axis=0)
        dx = w_ref[3:4, :] * dpre
        for j in (1, 2, 3):
            dx = dx + w_ref[3 - j:4 - j, :] * pltpu.roll(pe, tb + 8 - j, axis=0)[:tb, :]
        dx_ref[...] = dx.astype(dx_ref.dtype)
        halo = jnp.where(i > 0, xh_ref[...], 0.0)
        xe = jnp.concatenate([halo, x_ref[...]], axis=0)
        for k in range(3):
            dw_ref[k:k + 1, :] += jnp.sum(dpre * pltpu.roll(xe, 3 - k, axis=0)[8:, :], axis=0, keepdims=True)
        dw_ref[3:4, :] += jnp.sum(dpre * x_ref[...], axis=0, keepdims=True)
        db_ref[...] += jnp.sum(dpre, axis=0, keepdims=True)

    blk = pl.BlockSpec((tb, C), lambda i: (i, 0))
    nxt = pl.BlockSpec((8, C), lambda i: (jnp.minimum((i + 1) * (tb // 8), S // 8 - 1), 0))
    prv = pl.BlockSpec((8, C), lambda i: (jnp.maximum(i * (tb // 8) - 1, 0), 0))
    return pl.pallas_call(
        body, name="conv_bwd", grid=(nblk,),
        in_specs=[blk, blk, nxt, nxt, blk, prv, pl.BlockSpec((4, C), lambda i: (0, 0))],
        out_specs=[blk, pl.BlockSpec((4, C), lambda i: (0, 0)), pl.BlockSpec((1, C), lambda i: (0, 0))],
        out_shape=[jax.ShapeDtypeStruct((S, C), BF16), jax.ShapeDtypeStruct((4, C), F32),
                   jax.ShapeDtypeStruct((1, C), F32)],
        compiler_params=_params(("arbitrary",)))(dxc, pre, dxc, pre, xr, xr, w)


def _iota(shape, dim):
    return lax.broadcasted_iota(jnp.int32, shape, dim)


def _colsel(m, lane, h):
    return jnp.sum(jnp.where(lane == h, m, 0.0), axis=1, keepdims=True)


def _cumsum_rows(v):
    r = _iota(v.shape, 0)
    k = 1
    while k < v.shape[0]:
        v = v + jnp.where(r >= k, pltpu.roll(v, k, axis=0), 0.0)
        k *= 2
    return v


def _suffix_sum_rows(v):
    n = v.shape[0]
    r = _iota(v.shape, 0)
    k = 1
    while k < n:
        v = v + jnp.where(r < n - k, pltpu.roll(v, n - k, axis=0), 0.0)
        k *= 2
    return v


def _ssd_fwd(xc, dtr, z, dtb, alog, dskl, nw):
    S = xc.shape[0]
    nc = S // L

    def body(xc_ref, dtr_ref, z_ref, dtb_ref, alog_ref, dsk_ref, nw_ref, ya_ref, y_ref, prev_ref,
             st_ref, cum_ref, cumT_ref):
        i = pl.program_id(0)

        @pl.when(i == 0)
        def _():
            st_ref[...] = jnp.zeros_like(st_ref)

        lane = _iota((L, 128), 1)
        lane1 = _iota((1, 128), 1)
        lo = lane < 64
        lo1 = lane1 < 64
        tril = _iota((L, L), 0) >= _iota((L, L), 1)
        dt = _softplus(dtr_ref[...] + dtb_ref[...])
        a_neg = -jnp.exp(alog_ref[...])
        cum = _cumsum_rows(dt * a_neg)
        cum_ref[...] = cum
        cumT_ref[...] = cum.T
        last_all = cum_ref[L - 1:L, :]
        prev_t = st_ref[...]
        prev_ref[0] = prev_t
        for g in range(2):
            bg = xc_ref[:, 1024 + g * 128:1152 + g * 128]
            cg = xc_ref[:, 1280 + g * 128:1408 + g * 128]
            gmat = _dot(cg, bg, NT)
            yoff = _dot(cg, prev_t[:, g * 512:(g + 1) * 512])
            bg_t = bg.T
            for jp in range(4):
                j = g * 4 + jp
                sl = slice(j * 128, (j + 1) * 128)
                xp = xc_ref[:, sl]
                cc = [_colsel(cum, lane, 2 * j), _colsel(cum, lane, 2 * j + 1)]
                cum_l = jnp.where(lo, cc[0], cc[1])
                dt_l = jnp.where(lo, _colsel(dt, lane, 2 * j), _colsel(dt, lane, 2 * j + 1))
                last_l = jnp.where(lo1, _colsel(last_all, lane1, 2 * j), _colsel(last_all, lane1, 2 * j + 1))
                xd = xp * dt_l
                ys = []
                for hh in range(2):
                    seg = cc[hh] - cumT_ref[2 * j + hh:2 * j + hh + 1, :]
                    dm = jnp.where(tril, jnp.exp(jnp.where(tril, seg, 0.0)), 0.0)
                    ys.append(_dot(gmat * dm, xd))
                y_ref[:, sl] = (jnp.where(lo, ys[0], ys[1]) + jnp.exp(cum_l) * yoff[:, jp * 128:(jp + 1) * 128]
                                + dsk_ref[:, sl] * xp)
                st_ref[:, sl] = prev_t[:, sl] * jnp.exp(last_l) + _dot(bg_t, xd * jnp.exp(last_l - cum_l))
        for g in range(2):
            sl = slice(g * 512, (g + 1) * 512)
            zz = z_ref[:, sl]
            yg = y_ref[:, sl] * (zz * _sigmoid(zz))
            rstd = lax.rsqrt(jnp.mean(yg * yg, axis=-1, keepdims=True) + EPS)
            ya_ref[:, sl] = (yg * rstd * nw_ref[:, sl]).astype(ya_ref.dtype)

    blk = lambda c: pl.BlockSpec((L, c), lambda i: (i, 0))
    vec = lambda c: pl.BlockSpec((1, c), lambda i: (0, 0))
    return pl.pallas_call(
        body, name="ssd_fwd", grid=(nc,),
        in_specs=[blk(1536), blk(128), blk(1024), vec(128), vec(128), vec(1024), vec(1024)],
        out_specs=[blk(1024), blk(1024), pl.BlockSpec((1, NSTATE, 1024), lambda i: (i, 0, 0))],
        out_shape=[jax.ShapeDtypeStruct((S, 1024), BF16), jax.ShapeDtypeStruct((S, 1024), F32),
                   jax.ShapeDtypeStruct((nc, NSTATE, 1024), F32)],
        scratch_shapes=[pltpu.VMEM((NSTATE, 1024), F32), pltpu.VMEM((L, 128), F32), pltpu.VMEM((L, 128), F32)],
        compiler_params=_params(("arbitrary",)))(xc, dtr, z, dtb, alog, dskl, nw)


def _ssd_bwd(dya, y, z, xc, dtr, prev, dtb, alog, dskl, nw):
    S = xc.shape[0]
    nc = S // L

    def body(dya_ref, y_ref, z_ref, xc_ref, dtr_ref, prev_ref, dtb_ref, alog_ref, dsk_ref, nw_ref,
             dz_ref, dxc_ref, ddtr_ref, dnw_ref, ddsk_ref, dalog_ref, ddtb_ref,
             dst_ref, cum_ref, cumT_ref, dy_ref, dskacc_ref):
        i = pl.program_id(0)

        @pl.when(i == 0)
        def _():
            dst_ref[...] = jnp.zeros_like(dst_ref)
            dskacc_ref[...] = jnp.zeros_like(dskacc_ref)
            dnw_ref[...] = jnp.zeros_like(dnw_ref)
            dalog_ref[...] = jnp.zeros_like(dalog_ref)
            ddtb_ref[...] = jnp.zeros_like(ddtb_ref)

        lane = _iota((L, 128), 1)
        lane1 = _iota((1, 128), 1)
        lo = lane < 64
        lo1 = lane1 < 64
        r2, c2 = _iota((L, L), 0), _iota((L, L), 1)
        tril = r2 >= c2
        triu = r2 <= c2
        is_last = _iota((L, 1), 0) == L - 1

        for g in range(2):
            sl = slice(g * 512, (g + 1) * 512)
            zz = z_ref[:, sl]
            sg = _sigmoid(zz)
            zg = zz * sg
            yv = y_ref[:, sl]
            yg = yv * zg
            rstd = lax.rsqrt(jnp.mean(yg * yg, axis=-1, keepdims=True) + EPS)
            xh = yg * rstd
            d_out = dya_ref[:, sl]
            dnw_ref[:, sl] += jnp.sum(d_out * xh, axis=0, keepdims=True)
            dyn = d_out * nw_ref[:, sl]
            dyg = rstd * (dyn - xh * jnp.mean(dyn * xh, axis=-1, keepdims=True))
            dy_ref[:, sl] = dyg * zg
            dz_ref[:, sl] = (dyg * yv * (sg * (1.0 + zz * (1.0 - sg)))).astype(dz_ref.dtype)

        dtin = dtr_ref[...] + dtb_ref[...]
        dt = _softplus(dtin)
        a_neg = -jnp.exp(alog_ref[...])
        cum = _cumsum_rows(dt * a_neg)
        cum_ref[...] = cum
        cumT_ref[...] = cum.T
        last_all = cum_ref[L - 1:L, :]
        prev_t = prev_ref[0]
        dn_t = dst_ref[...]
        dcum = jnp.zeros((L, 128), F32)
        ddt = jnp.zeros((L, 128), F32)
        for g in range(2):
            gsl = slice(g * 512, (g + 1) * 512)
            bg = xc_ref[:, 1024 + g * 128:1152 + g * 128]
            cg = xc_ref[:, 1280 + g * 128:1408 + g * 128]
            gmat = _dot(cg, bg, NT)
            gmat_t = _dot(bg, cg, NT)
            pg = prev_t[:, gsl]
            zmat = _dot(cg, pg)
            dgm = jnp.zeros((L, L), F32)
            dgm_t = jnp.zeros((L, L), F32)
            db_acc = jnp.zeros((L, NSTATE), F32)
            dz_parts, cd_parts = [], []
            for jp in range(4):
                j = g * 4 + jp
                sl = slice(j * 128, (j + 1) * 128)
                xp = xc_ref[:, sl]
                dyp = dy_ref[:, sl]
                cc = [_colsel(cum, lane, 2 * j), _colsel(cum, lane, 2 * j + 1)]
                lc = [_colsel(last_all, lane1, 2 * j), _colsel(last_all, lane1, 2 * j + 1)]
                cum_l = jnp.where(lo, cc[0], cc[1])
                dt_l = jnp.where(lo, _colsel(dt, lane, 2 * j), _colsel(dt, lane, 2 * j + 1))
                last_l = jnp.where(lo1, lc[0], lc[1])
                e_l = jnp.exp(cum_l)
                dte_l = jnp.exp(last_l - cum_l)
                cd_l = jnp.exp(last_l)
                cd_parts.append(cd_l)
                xd = xp * dt_l
                dskacc_ref[:, sl] += jnp.sum(dyp * xp, axis=0, keepdims=True)
                dxp = dsk_ref[:, sl] * dyp
                t = dyp * (e_l * zmat[:, jp * 128:(jp + 1) * 128])
                dcc = [jnp.sum(jnp.where(lo, t, 0.0), axis=1, keepdims=True),
                       jnp.sum(jnp.where(lo, 0.0, t), axis=1, keepdims=True)]
                dz_parts.append(e_l * dyp)
                dnp_ = dn_t[:, sl]
                t2 = jnp.sum(dnp_ * prev_t[:, sl], axis=0, keepdims=True)
                dcd = [jnp.sum(jnp.where(lo1, t2, 0.0), axis=1, keepdims=True),
                       jnp.sum(jnp.where(lo1, 0.0, t2), axis=1, keepdims=True)]
                wm = _dot(bg, dnp_)
                dxd = wm * dte_l
                t3 = wm * xd
                ddte = [jnp.sum(jnp.where(lo, t3, 0.0), axis=1, keepdims=True),
                        jnp.sum(jnp.where(lo, 0.0, t3), axis=1, keepdims=True)]
                db_acc = db_acc + _dot(xd * dte_l, dnp_, NT)
                for hh in range(2):
                    h = 2 * j + hh
                    half = lo if hh == 0 else jnp.logical_not(lo)
                    row = cumT_ref[h:h + 1, :]
                    dm = jnp.where(tril, jnp.exp(jnp.where(tril, cc[hh] - row, 0.0)), 0.0)
                    dm_t = jnp.where(triu, jnp.exp(jnp.where(triu, row - cc[hh], 0.0)), 0.0)
                    m = gmat * dm
                    m_t = gmat_t * dm_t
                    dym = jnp.where(half, dyp, 0.0)
                    d_m = _dot(dym, xd, NT)
                    d_mt = _dot(xd, dym, NT)
                    dxd = dxd + _dot(m_t, dym)
                    dcc[hh] = dcc[hh] + jnp.sum(d_m * m, axis=1, keepdims=True) - jnp.sum(d_mt * m_t, axis=1, keepdims=True)
                    dgm = dgm + d_m * dm
                    dgm_t = dgm_t + d_mt * dm_t
                    dte_c = jnp.exp(lc[hh] - cc[hh])
                    dcc[hh] = dcc[hh] - ddte[hh] * dte_c
                    endc = dcd[hh] * jnp.exp(lc[hh]) + jnp.sum(ddte[hh] * dte_c, axis=0, keepdims=True)
                    dcc[hh] = dcc[hh] + jnp.where(is_last, endc, 0.0)
                    dcum = jnp.where(lane == h, dcc[hh], dcum)
                dxc_ref[:, sl] = dxp + dxd * dt_l
                t4 = dxd * xp
                ddt = jnp.where(lane == 2 * j, jnp.sum(jnp.where(lo, t4, 0.0), axis=1, keepdims=True), ddt)
                ddt = jnp.where(lane == 2 * j + 1, jnp.sum(jnp.where(lo, 0.0, t4), axis=1, keepdims=True), ddt)
            dzg = jnp.concatenate(dz_parts, axis=1)
            dst_ref[:, gsl] = dn_t[:, gsl] * jnp.concatenate(cd_parts, axis=1) + _dot(cg.T, dzg)
            dxc_ref[:, 1280 + g * 128:1408 + g * 128] = _dot(dgm, bg) + _dot(dzg, pg, NT)
            dxc_ref[:, 1024 + g * 128:1152 + g * 128] = _dot(dgm_t, cg) + db_acc
        dla = _suffix_sum_rows(dcum)
        ddt = ddt + dla * a_neg
        dalog_ref[...] += jnp.sum(dla * dt, axis=0, keepdims=True) * a_neg
        ddtr = jnp.where(lane < 16, ddt * _sigmoid(dtin), 0.0)
        ddtr_ref[...] = ddtr.astype(ddtr_ref.dtype)
        ddtb_ref[...] += jnp.sum(ddtr, axis=0, keepdims=True)

        @pl.when(i == nc - 1)
        def _():
            seg = (_iota((1024, 128), 0) // 64 == _iota((1024, 128), 1)).astype(F32)
            acc8 = jnp.broadcast_to(dskacc_ref[...], (8, 1024))
            ddsk_ref[...] = lax.dot_general(acc8, seg, NN, precision=lax.Precision.HIGHEST,
                                            preferred_element_type=F32)

    rev = lambda c: pl.BlockSpec((L, c), lambda i: (nc - 1 - i, 0))
    vec = lambda c: pl.BlockSpec((1, c), lambda i: (0, 0))
    return pl.pallas_call(
        body, name="ssd_bwd", grid=(nc,),
        in_specs=[rev(1024), rev(1024), rev(1024), rev(1536), rev(128),
                  pl.BlockSpec((1, NSTATE, 1024), lambda i: (nc - 1 - i, 0, 0)),
                  vec(128), vec(128), vec(1024), vec(1024)],
        out_specs=[rev(1024), rev(1536), rev(128), vec(1024), pl.BlockSpec((8, 128), lambda i: (0, 0)),
                   vec(128), vec(128)],
        out_shape=[jax.ShapeDtypeStruct((S, 1024), BF16), jax.ShapeDtypeStruct((S, 1536), F32),
                   jax.ShapeDtypeStruct((S, 128), BF16), jax.ShapeDtypeStruct((1, 1024), F32),
                   jax.ShapeDtypeStruct((8, 128), F32), jax.ShapeDtypeStruct((1, 128), F32),
                   jax.ShapeDtypeStruct((1, 128), F32)],
        scratch_shapes=[pltpu.VMEM((NSTATE, 1024), F32), pltpu.VMEM((L, 128), F32), pltpu.VMEM((L, 128), F32),
                        pltpu.VMEM((L, 1024), F32), pltpu.VMEM((1, 1024), F32)],
        compiler_params=_params(("arbitrary",)))(dya, y, z, xc, dtr, prev, dtb, alog, dskl, nw)


def _layer_norm_parts(vg):
    mu = jnp.mean(vg, axis=-1, keepdims=True)
    vc = vg - mu
    rstd = lax.rsqrt(jnp.mean(vc * vc, axis=-1, keepdims=True) + EPS)
    return vc * rstd, rstd


def _gmlp_fwd(u, v, lnw, lnb, ws, bse, tb=512):
    S = u.shape[0]
    tb = min(tb, S)

    def body(u_ref, v_ref, lnw_ref, lnb_ref, ws_ref, bse_ref, o_ref, vn_ref):
        tril = _iota((L, L), 0) >= _iota((L, L), 1)
        xh, _ = _layer_norm_parts(_gelu(v_ref[...]))
        vn_ref[...] = xh * lnw_ref[...] + lnb_ref[...]
        for g in range(8):
            w = jnp.where(tril, ws_ref[g], 0.0)
            gs = slice(g * 128, (g + 1) * 128)
            for ch in range(tb // L):
                rs = slice(ch * L, (ch + 1) * L)
                sv = _dot(w, vn_ref[rs, gs]) + bse_ref[g]
                o_ref[rs, gs] = (_gelu(u_ref[rs, gs]) * sv).astype(o_ref.dtype)

    blk = pl.BlockSpec((tb, 1024), lambda i: (i, 0))
    vec = pl.BlockSpec((1, 1024), lambda i: (0, 0))
    cube = pl.BlockSpec((8, L, 128), lambda i: (0, 0, 0))
    return pl.pallas_call(
        body, name="gmlp_fwd", grid=(S // tb,), in_specs=[blk, blk, vec, vec, cube, cube], out_specs=blk,
        out_shape=jax.ShapeDtypeStruct((S, 1024), BF16), scratch_shapes=[pltpu.VMEM((tb, 1024), F32)],
        compiler_params=_params(("parallel",)))(u, v, lnw, lnb, ws, bse)


def _gmlp_bwd(dyb, u, v, lnw, lnb, ws, bse, tb=512):
    S = u.shape[0]
    tb = min(tb, S)

    def body(d_ref, u_ref, v_ref, lnw_ref, lnb_ref, ws_ref, bse_ref,
             du_ref, dv_ref, dws_ref, dbse_ref, dlnw_ref, dlnb_ref, vn_ref, dvn_ref):
        @pl.when(pl.program_id(0) == 0)
        def _():
            dws_ref[...] = jnp.zeros_like(dws_ref)
            dbse_ref[...] = jnp.zeros_like(dbse_ref)
            dlnw_ref[...] = jnp.zeros_like(dlnw_ref)
            dlnb_ref[...] = jnp.zeros_like(dlnb_ref)

        tril = _iota((L, L), 0) >= _iota((L, L), 1)
        vv = v_ref[...]
        xh, rstd = _layer_norm_parts(_gelu(vv))
        vn_ref[...] = xh * lnw_ref[...] + lnb_ref[...]
        for g in range(8):
            w = jnp.where(tril, ws_ref[g], 0.0)
            w_t = w.T
            gs = slice(g * 128, (g + 1) * 128)
            dw = jnp.zeros((L, L), F32)
            dbs = jnp.zeros((L, 128), F32)
            for ch in range(tb // L):
                rs = slice(ch * L, (ch + 1) * L)
                vn = vn_ref[rs, gs]
                sv = _dot(w, vn) + bse_ref[g]
                uu = u_ref[rs, gs]
                dd = d_ref[rs, gs]
                du_ref[rs, gs] = (dd * sv * _gelu_grad(uu)).astype(du_ref.dtype)
                dsv = dd * _gelu(uu)
                dw = dw + _dot(dsv, vn, NT)
                dbs = dbs + dsv
                dvn_ref[rs, gs] = _dot(w_t, dsv)
            dws_ref[g] += jnp.where(tril, dw, 0.0)
            dbse_ref[g] += dbs
        dvn = dvn_ref[...]
        dlnw_ref[...] += jnp.sum(dvn * xh, axis=0, keepdims=True)
        dlnb_ref[...] += jnp.sum(dvn, axis=0, keepdims=True)
        dxh = dvn * lnw_ref[...]
        dvg = rstd * (dxh - jnp.mean(dxh, axis=-1, keepdims=True) - xh * jnp.mean(dxh * xh, axis=-1, keepdims=True))
        dv_ref[...] = (dvg * _gelu_grad(vv)).astype(dv_ref.dtype)

    blk = pl.BlockSpec((tb, 1024), lambda i: (i, 0))
    vec = pl.BlockSpec((1, 1024), lambda i: (0, 0))
    cube = pl.BlockSpec((8, L, 128), lambda i: (0, 0, 0))
    return pl.pallas_call(
        body, name="gmlp_bwd", grid=(S // tb,), in_specs=[blk, blk, blk, vec, vec, cube, cube],
        out_specs=[blk, blk, cube, cube, vec, vec],
        out_shape=[jax.ShapeDtypeStruct((S, 1024), BF16), jax.ShapeDtypeStruct((S, 1024), BF16),
                   jax.ShapeDtypeStruct((8, L, 128), F32), jax.ShapeDtypeStruct((8, L, 128), F32),
                   jax.ShapeDtypeStruct((1, 1024), F32), jax.ShapeDtypeStruct((1, 1024), F32)],
        scratch_shapes=[pltpu.VMEM((tb, 1024), F32), pltpu.VMEM((tb, 1024), F32)],
        compiler_params=_params(("arbitrary",)))(dyb, u, v, lnw, lnb, ws, bse)


def _lane_sum(name, a):
    def body(a_ref, o_ref):
        o_ref[...] = jnp.sum(a_ref[...], axis=1, keepdims=True)
    return pl.pallas_call(body, name=name, out_shape=jax.ShapeDtypeStruct((a.shape[0], 1), F32))(a)


def _bucket_onehot_t():
    qi = np.arange(L)[:, None]
    sj = np.arange(2 * L)[None, :]
    dist = np.maximum(qi + L - sj, 0)
    log_ratio = (np.log(np.maximum(dist, 1).astype(np.float32) / np.float32(16)) / np.float32(math.log(128 / 16)))
    large = 16 + (log_ratio.astype(np.float32) * np.float32(16)).astype(np.int32)
    bucket = np.where(dist < 16, dist, np.minimum(large, 31)).reshape(-1)
    return (np.arange(32)[:, None] == bucket[None, :]).astype(np.float32)


def _rel_bias(table_t, onehot_t):
    def body(t_ref, oh_ref, o_ref):
        o_ref[...] = lax.dot_general(t_ref[...], oh_ref[...], NN, precision=lax.Precision.HIGHEST,
                                     preferred_element_type=F32)
    return pl.pallas_call(body, name="rel_bias", out_shape=jax.ShapeDtypeStruct((16, L * 2 * L), F32),
                          compiler_params=_params())(table_t, onehot_t)


def _rel_bias_bwd(dbias, onehot_t):
    def body(d_ref, oh_ref, o_ref):
        o_ref[...] = lax.dot_general(d_ref[...], oh_ref[...], NT, precision=lax.Precision.HIGHEST,
                                     preferred_element_type=F32)
    return pl.pallas_call(body, name="rel_bias_bwd", out_shape=jax.ShapeDtypeStruct((16, 32), F32),
                          compiler_params=_params())(dbias, onehot_t)


def _band(kp, kc, lo):
    kk = jnp.concatenate([kp, kc], axis=0)
    kr = pltpu.roll(kk, 64, axis=1)
    return [jnp.where(lo, kk, kr), jnp.where(lo, kr, kk)]


def _attn_mask(i):
    qi, sj = _iota((L, 2 * L), 0), _iota((L, 2 * L), 1)
    rel = qi + L - sj
    return (rel >= 0) & (rel < L) & ((sj >= L) | (i > 0))


SMEM = pl.BlockSpec(memory_space=pltpu.SMEM)


def _attn_fwd(qkv, bias, sinks):
    S = qkv.shape[0]
    nb = S // L
    scale = 64 ** -0.5

    def body(sink_ref, q_ref, kc_ref, vc_ref, kp_ref, vp_ref, bias_ref, o_ref, lse_ref):
        i = pl.program_id(0)
        lane = _iota((L, 128), 1)
        lo = lane < 64
        lo2 = _iota((2 * L, 128), 1) < 64
        mask = _attn_mask(i)
        kd = _band(kp_ref[...], kc_ref[...], lo2)
        vd = _band(vp_ref[...], vc_ref[...], lo2)
        lse = jnp.zeros((L, 128), F32)
        for pr in range(8):
            sl = slice(pr * 128, (pr + 1) * 128)
            qp = q_ref[:, sl]
            j = pr // 4
            outs = []
            for hh in range(2):
                h = 2 * pr + hh
                qm = jnp.where(lo if hh == 0 else jnp.logical_not(lo), qp, 0.0)
                lg = jnp.where(mask, _dot(qm, kd[j], NT) * scale + bias_ref[h], NEG_INF)
                s = sink_ref[h]
                m = jnp.maximum(jnp.max(lg, axis=1, keepdims=True), s)
                p = jnp.where(mask, jnp.exp(lg - m), 0.0)
                den = jnp.sum(p, axis=1, keepdims=True) + jnp.exp(s - m)
                outs.append(_dot(p / den, vd[j]))
                lse = jnp.where(lane == h, m + jnp.log(den), lse)
            o_ref[:, sl] = jnp.where(lo, outs[0], outs[1]).astype(o_ref.dtype)
        lse_ref[...] = lse

    prev = lambda col: pl.BlockSpec((L, 128), lambda i: (jnp.maximum(i - 1, 0), col))
    cur = lambda col: pl.BlockSpec((L, 128), lambda i: (i, col))
    return pl.pallas_call(
        body, name="attn_fwd", grid=(nb,),
        in_specs=[SMEM, pl.BlockSpec((L, 1024), lambda i: (i, 0)), cur(8), cur(9), prev(8), prev(9),
                  pl.BlockSpec((16, L, 2 * L), lambda i: (0, 0, 0))],
        out_specs=[pl.BlockSpec((L, 1024), lambda i: (i, 0)), pl.BlockSpec((L, 128), lambda i: (i, 0))],
        out_shape=[jax.ShapeDtypeStruct((S, 1024), BF16), jax.ShapeDtypeStruct((S, 128), F32)],
        compiler_params=_params(("parallel",)))(sinks, qkv, qkv, qkv, qkv, qkv, bias)


def _attn_bwd(qkv, d_o, lse, bias, sinks):
    S = qkv.shape[0]
    nb = S // L
    scale = 64 ** -0.5

    def body(sink_ref, q_ref, kc_ref, vc_ref, kp_ref, vp_ref, do_ref, lse_ref, bias_ref,
             dq_ref, dkv_ref, dbias_ref, dsink_ref, dbq_ref, dbkv_ref, carry_ref):
        i = pl.program_id(0)

        @pl.when(i == 0)
        def _():
            dbias_ref[...] = jnp.zeros_like(dbias_ref)
            dsink_ref[...] = jnp.zeros_like(dsink_ref)
            dbq_ref[...] = jnp.zeros_like(dbq_ref)
            dbkv_ref[...] = jnp.zeros_like(dbkv_ref)
            carry_ref[...] = jnp.zeros_like(carry_ref)

        @pl.when(i < nb)
        def _():
            lane = _iota((L, 128), 1)
            lane1 = _iota((1, 128), 1)
            lo = lane < 64
            lo2 = _iota((2 * L, 128), 1) < 64
            mask = _attn_mask(i)
            kd = _band(kp_ref[...], kc_ref[...], lo2)
            vd = _band(vp_ref[...], vc_ref[...], lo2)
            lse_all = lse_ref[...]
            acc_k = [jnp.zeros((2 * L, 128), F32), jnp.zeros((2 * L, 128), F32)]
            acc_v = [jnp.zeros((2 * L, 128), F32), jnp.zeros((2 * L, 128), F32)]
            dsink = jnp.zeros((1, 128), F32)
            for pr in range(8):
                sl = slice(pr * 128, (pr + 1) * 128)
                qp = q_ref[:, sl]
                dop = do_ref[:, sl]
                j = pr // 4
                dqs = []
                for hh in range(2):
                    h = 2 * pr + hh
                    half = lo if hh == 0 else jnp.logical_not(lo)
                    qm = jnp.where(half, qp, 0.0)
                    dom = jnp.where(half, dop, 0.0)
                    lse_h = _colsel(lse_all, lane, h)
                    lg = _dot(qm, kd[j], NT) * scale + bias_ref[h]
                    p = jnp.where(mask, jnp.exp(jnp.where(mask, lg, NEG_INF) - lse_h), 0.0)
                    dp = _dot(dom, vd[j], NT)
                    delta = jnp.sum(p * dp, axis=1, keepdims=True)
                    ds = p * (dp - delta)
                    dbias_ref[h] += ds
                    ds_sink = jnp.sum(-jnp.exp(sink_ref[h] - lse_h) * delta, axis=0, keepdims=True)
                    dsink = dsink + jnp.where(lane1 == h, ds_sink, 0.0)
                    dss = ds * scale
                    dqs.append(_dot(dss, kd[j]))
                    acc_k[j] = acc_k[j] + _dot(dss, qm, TN)
                    acc_v[j] = acc_v[j] + _dot(p, dom, TN)
                dq = jnp.where(lo, dqs[0], dqs[1])
                dq_ref[:, sl] = dq.astype(dq_ref.dtype)
                dbq_ref[:, sl] += jnp.sum(dq, axis=0, keepdims=True)
            dsink_ref[...] += dsink
            tot_k = [a + pltpu.roll(a, 64, axis=1) for a in acc_k]
            tot_v = [a + pltpu.roll(a, 64, axis=1) for a in acc_v]
            dkv = jnp.concatenate([jnp.where(lo2, tot_k[0], tot_k[1]), jnp.where(lo2, tot_v[0], tot_v[1])], axis=1)
            dbkv_ref[...] += jnp.sum(dkv, axis=0, keepdims=True)
            dkv_ref[...] = (carry_ref[...] + dkv[:L, :]).astype(dkv_ref.dtype)
            carry_ref[...] = dkv[L:, :]

        @pl.when(i == nb)
        def _():
            dkv_ref[...] = carry_ref[...].astype(dkv_ref.dtype)

    c = lambda i: jnp.minimum(i, nb - 1)
    prev = lambda col: pl.BlockSpec((L, 128), lambda i: (jnp.maximum(c(i) - 1, 0), col))
    cur = lambda col: pl.BlockSpec((L, 128), lambda i: (c(i), col))
    row = lambda w: pl.BlockSpec((L, w), lambda i: (c(i), 0))
    cube = pl.BlockSpec((16, L, 2 * L), lambda i: (0, 0, 0))
    vec = lambda w: pl.BlockSpec((1, w), lambda i: (0, 0))
    return pl.pallas_call(
        body, name="attn_bwd", grid=(nb + 1,),
        in_specs=[SMEM, row(1024), cur(8), cur(9), prev(8), prev(9), row(1024), row(128), cube],
        out_specs=[row(1024), pl.BlockSpec((L, 256), lambda i: (jnp.maximum(i - 1, 0), 0)), cube,
                   vec(128), vec(1024), vec(256)],
        out_shape=[jax.ShapeDtypeStruct((S, 1024), BF16), jax.ShapeDtypeStruct((S, 256), BF16),
                   jax.ShapeDtypeStruct((16, L, 2 * L), F32), jax.ShapeDtypeStruct((1, 128), F32),
                   jax.ShapeDtypeStruct((1, 1024), F32), jax.ShapeDtypeStruct((1, 256), F32)],
        scratch_shapes=[pltpu.VMEM((L, 256), F32)],
        compiler_params=_params(("arbitrary",)))(sinks, qkv, qkv, qkv, qkv, qkv, d_o, lse, bias)


def _pad_lanes(a, n=128):
    return jnp.pad(a, ((0, 0), (0, n - a.shape[1])))


def _local_step(x, tgt, mod, W, P):
    md = [[mod[l:l + 1, k * D:(k + 1) * D] for k in range(6)] for l in range(2)]
    G, g = {}, {}

    sh1, sc1, g1, sh2, sc2, g2 = md[0]
    nmw0, nfw0 = P["norm_mix_w"][0:1], P["norm_ffn_w"][0:1]
    h0 = _norm_mod_fwd("norm_mix_0", x, nmw0, sc1, sh1)
    w_in = W["in_wt"]
    segs = {"z": w_in[0:1024], "xbc": w_in[1024:2560], "dt": jnp.pad(w_in[2560:2576], ((0, 112), (0, 0))),
            "u": w_in[2576:3600], "v": w_in[3600:4624]}
    proj = {k: _mm(f"in_proj_{k}", [h0], [w], "nt", [F32])[0] for k, w in segs.items()}
    conv_w, conv_b = P["conv_w"][0], P["conv_b"]
    pre, xc = _conv_fwd(proj["xbc"], conv_w, conv_b)
    dtb, alog = _pad_lanes(P["dt_bias"]), _pad_lanes(P["a_log"])
    dskl = jnp.repeat(P["d_skip"], 64, axis=1)
    ya, y_ssd, prev = _ssd_fwd(xc, proj["dt"], proj["z"], dtb, alog, dskl, P["ssm_norm_w"])
    ws = P["gmlp_ws"][0]
    bse = jnp.broadcast_to(P["gmlp_bs"][0][:, :, None], (8, L, 128))
    yb = _gmlp_fwd(proj["u"], proj["v"], P["gmlp_ln_w"], P["gmlp_ln_b"], ws, bse)
    w_oa, w_ob = W["out_w"][:1024], W["out_w"][1024:]

    def res(y, x, gate):
        return y, x + gate * y
    mix0, x1 = _mm("out_proj_0", [ya, yb], [w_oa, w_ob], "nn", [F32, F32], epi=res, extras=[x], vecs=[g1])
    h0f = _norm_mod_fwd("norm_ffn_0", x1, nfw0, sc2, sh2)
    a0, b0, f0, y0, x2 = _ffn_fwd("0", h0f, W["gate_wt"][0], W["up_wt"][0], W["down_w"][0], x1, g2)

    sh1b, sc1b, g1b, sh2b, sc2b, g2b = md[1]
    nmw1, nfw1 = P["norm_mix_w"][1:2], P["norm_ffn_w"][1:2]
    h1 = _norm_mod_fwd("norm_mix_1", x2, nmw1, sc1b, sh1b)
    qkv = _mm("qkv_proj", [h1], [W["qkv_wt"]], "nt", [F32], epi=lambda acc, b: acc + b, vecs=[P["qkv_b"]])[0]
    onehot_t = jnp.asarray(_bucket_onehot_t())
    bias = _rel_bias(P["rel_table"].T, onehot_t).reshape(16, L, 2 * L)
    sinks = P["sinks"].reshape(16)
    att, lse = _attn_fwd(qkv, bias, sinks)

    def res_b(y, x, gate, b):
        y = y + b
        return y, x + gate * y
    mix1, x3 = _mm("o_proj", [att], [W["o_w"]], "nn", [F32, F32], epi=res_b, extras=[x2], vecs=[g1b, P["o_b"]])
    h1f = _norm_mod_fwd("norm_ffn_1", x3, nfw1, sc2b, sh2b)
    a1, b1, f1, y1, x4 = _ffn_fwd("1", h1f, W["gate_wt"][1], W["up_wt"][1], W["down_w"][1], x3, g2b)

    dx, sq, g["final_norm_w"] = _loss_head(x4, tgt, P["final_norm_w"])

    dh, dg2b, dwg1, dwu1, dwd1 = _ffn_bwd("1", dx, h1f, a1, b1, f1, y1, W["gate_wt"][1], W["up_wt"][1],
                                          W["down_w"][1], g2b)
    dx, dsh2b, dsc2b, dnfw1 = _norm_mod_bwd("norm_ffn_bwd_1", x3, dh, dx, nfw1, sc2b)
    dmix, dg1b, g["o_b"] = _gate_bwd("mix_gate_bwd_1", dx, mix1, g1b)
    G["o_w"] = _mm_tn("o_dw", att, dmix)
    d_att = _mm("o_dx", [dmix], [W["o_w"]], "nt", [F32])[0]
    dq, dkv, dbias, dsinks, dbq, dbkv = _attn_bwd(qkv, d_att, lse, bias, sinks)
    g["rel_table"] = _rel_bias_bwd(dbias.reshape(16, L * 2 * L), onehot_t).T
    g["sinks"] = dsinks[:, :16]
    g["qkv_b"] = jnp.concatenate([dbq, dbkv], axis=1)
    w_q, w_kv = W["qkv_wt"][:1024], W["qkv_wt"][1024:]
    G["qkv_wt"] = jnp.concatenate([_mm_tn("qkv_dwq", dq, h1), _mm_tn("qkv_dwkv", dkv, h1)], axis=0)
    dh = _mm("qkv_dx", [dq, dkv], [w_q, w_kv], "nn", [F32])[0]
    dx, dsh1b, dsc1b, dnmw1 = _norm_mod_bwd("norm_mix_bwd_1", x2, dh, dx, nmw1, sc1b)

    dh, dg2, dwg0, dwu0, dwd0 = _ffn_bwd("0", dx, h0f, a0, b0, f0, y0, W["gate_wt"][0], W["up_wt"][0],
                                         W["down_w"][0], g2)
    dx, dsh2, dsc2, dnfw0 = _norm_mod_bwd("norm_ffn_bwd_0", x1, dh, dx, nfw0, sc2)
    dmix, dg1, _ = _gate_bwd("mix_gate_bwd_0", dx, mix0, g1)
    G["out_w"] = jnp.concatenate([_mm_tn("out_dwa", ya, dmix), _mm_tn("out_dwb", yb, dmix)], axis=0)
    dya = _mm("out_dxa", [dmix], [w_oa], "nt", [F32])[0]
    dyb = _mm("out_dxb", [dmix], [w_ob], "nt", [F32])[0]
    du, dv, dws, dbse, g["gmlp_ln_w"], g["gmlp_ln_b"] = _gmlp_bwd(dyb, proj["u"], proj["v"], P["gmlp_ln_w"],
                                                                 P["gmlp_ln_b"], ws, bse)
    g["gmlp_ws"] = dws[None]
    g["gmlp_bs"] = _lane_sum("gmlp_dbs", dbse.reshape(8 * L, 128)).reshape(1, 8, L)
    dz, dxc, ddt, g["ssm_norm_w"], ddsk, dalog, ddtb = _ssd_bwd(dya, y_ssd, proj["z"], xc, proj["dt"], prev,
                                                                dtb, alog, dskl, P["ssm_norm_w"])
    g["d_skip"], g["a_log"], g["dt_bias"] = ddsk[0:1, :16], dalog[:, :16], ddtb[:, :16]
    dxr, dconv_w, g["conv_b"] = _conv_bwd(dxc, pre, proj["xbc"], conv_w)
    g["conv_w"] = dconv_w[None]
    dsegs = {"z": dz, "xbc": dxr, "dt": ddt, "u": du, "v": dv}
    dws_in = {k: _mm_tn(f"in_dw_{k}", d, h0) for k, d in dsegs.items()}
    G["in_wt"] = jnp.concatenate([dws_in["z"], dws_in["xbc"], dws_in["dt"][:16], dws_in["u"], dws_in["v"]], axis=0)
    keys = ["z", "xbc", "dt", "u", "v"]
    dh = _mm("in_dx", [dsegs[k] for k in keys], [segs[k] for k in keys], "nn", [F32])[0]
    dx, dsh1, dsc1, dnmw0 = _norm_mod_bwd("norm_mix_bwd_0", x, dh, dx, nmw0, sc1)

    G["gate_wt"], G["up_wt"], G["down_w"] = [dwg0, dwg1], [dwu0, dwu1], [dwd0, dwd1]
    g["norm_mix_w"] = jnp.concatenate([dnmw0, dnmw1], axis=0)
    g["norm_ffn_w"] = jnp.concatenate([dnfw0, dnfw1], axis=0)
    dmod = jnp.concatenate([jnp.concatenate([dsh1, dsc1, dg1, dsh2, dsc2, dg2], axis=1),
                            jnp.concatenate([dsh1b, dsc1b, dg1b, dsh2b, dsc2b, dg2b], axis=1)], axis=0)
    return sq, dx, dmod, G, g


def _ada_fwd(c_all, ada_w, ada_b):
    n = ada_w.shape[2]
    tn = _col_tile(n, 512)

    def body(c_ref, w_ref, b_ref, o_ref):
        cc = c_ref[...]
        o_ref[...] = lax.dot_general(cc * _sigmoid(cc), w_ref[...], NN, precision=lax.Precision.HIGHEST,
                                     preferred_element_type=F32) + b_ref[...]

    return pl.pallas_call(
        body, name="ada_fwd", grid=(2, n // tn),
        in_specs=[pl.BlockSpec((8, D), lambda l, j: (0, 0)), pl.BlockSpec((None, D, tn), lambda l, j: (l, 0, j)),
                  pl.BlockSpec((None, 1, tn), lambda l, j: (l, 0, j))],
        out_specs=pl.BlockSpec((None, 8, tn), lambda l, j: (l, 0, j)),
        out_shape=jax.ShapeDtypeStruct((2, 8, n), F32), compiler_params=_params(("parallel", "parallel")))(
            c_all, ada_w, ada_b)


def _ada_bwd(c_all, dmod_cols, dmod_all):
    n = dmod_cols.shape[2]
    tn = _col_tile(n, 512)

    def body(c_ref, d_ref, o_ref):
        cc = c_ref[...]
        o_ref[...] = lax.dot_general(cc * _sigmoid(cc), d_ref[...], TN, precision=lax.Precision.HIGHEST,
                                     preferred_element_type=F32)

    dw = pl.pallas_call(
        body, name="ada_dw", grid=(2, n // tn),
        in_specs=[pl.BlockSpec((8, D), lambda l, j: (0, 0)), pl.BlockSpec((None, 8, tn), lambda l, j: (l, 0, j))],
        out_specs=pl.BlockSpec((None, D, tn), lambda l, j: (l, 0, j)),
        out_shape=jax.ShapeDtypeStruct((2, D, n), F32), compiler_params=_params(("parallel", "parallel")))(
            c_all, dmod_cols)

    def sum_body(d_ref, o_ref):
        o_ref[...] = jnp.sum(d_ref[...], axis=0, keepdims=True)

    db = pl.pallas_call(
        sum_body, name="ada_db", grid=(2,),
        in_specs=[pl.BlockSpec((None, 8, 6 * D), lambda l: (l, 0, 0))],
        out_specs=pl.BlockSpec((None, 1, 6 * D), lambda l: (l, 0, 0)),
        out_shape=jax.ShapeDtypeStruct((2, 1, 6 * D), F32), compiler_params=_params(("parallel",)))(dmod_all)
    return dw, db


def _row_tile(rows, cap=512):
    best = rows
    for t in range(8, min(rows, cap) + 1, 8):
        if rows % t == 0:
            best = t
    return best


def _adamw(name, w, g, m, v):
    def fn(w, g, m, v):
        m = ADAM_B1 * m + (1.0 - ADAM_B1) * g
        v = ADAM_B2 * v + (1.0 - ADAM_B2) * (g * g)
        m_hat = m / (1.0 - ADAM_B1 ** ADAM_STEP)
        v_hat = v / (1.0 - ADAM_B2 ** ADAM_STEP)
        return -ADAM_LR * (m_hat / (jnp.sqrt(v_hat) + ADAM_EPS) + ADAM_WD * w), m, v
    cols = w.shape[1]
    return _rowwise(name, fn, [w, g, m, v], [], [(cols, F32)] * 3, tr=_row_tile(w.shape[0]))


def _place():
    return lax.axis_index("x"), lax.axis_index("y"), lax.axis_index("c")


VMEM_SPEC = pl.BlockSpec(memory_space=pltpu.VMEM)


def _allreduce_small(name, buf):
    rows = buf.shape[0]

    def body(x_ref, o_ref, stage, send_sems, recv_sems):
        x, y, c = _place()
        me = 4 * x + 2 * y + c
        stage[me] = x_ref[...]
        copies = []
        for k in range(1, 8):
            peer = (1 - x if k & 4 else x, 1 - y if k & 2 else y, 1 - c if k & 1 else c)
            cp = pltpu.make_async_remote_copy(src_ref=x_ref, dst_ref=stage.at[me], send_sem=send_sems.at[k - 1],
                                              recv_sem=recv_sems.at[k - 1], device_id=peer, device_id_type=MESH)
            cp.start()
            copies.append(cp)
        for cp in copies:
            cp.wait()
        acc = stage[0]
        for d in range(1, 8):
            acc = acc + stage[d]
        o_ref[...] = acc

    return pl.pallas_call(
        body, name=name, in_specs=[VMEM_SPEC], out_specs=VMEM_SPEC,
        out_shape=jax.ShapeDtypeStruct((rows, 128), F32),
        scratch_shapes=[pltpu.VMEM((8, rows, 128), F32), pltpu.SemaphoreType.DMA((7,)), pltpu.SemaphoreType.DMA((7,))],
        compiler_params=pltpu.CompilerParams(vmem_limit_bytes=_VMEM_LIMIT))(buf)


OTHER_CHIPS = ((1, 0), (0, 1), (1, 1))


def _allgather_big(wp):
    rows = wp.shape[0]
    half = rows // 2

    def body(w_ref, o_ref, send_sems, recv_sems, local_sem):
        x, y, c = _place()
        k = 2 * x + y
        mine = pl.ds(pl.multiple_of(c * half, 8), half)
        theirs = pl.ds(pl.multiple_of((1 - c) * half, 8), half)
        local = pltpu.make_async_copy(w_ref, o_ref.at[k], local_sem)
        local.start()
        chips = [(1 - x if fx else x, 1 - y if fy else y) for fx, fy in OTHER_CHIPS]
        idx = [2 * px + py for px, py in chips]
        first = []
        for j, (px, py) in enumerate(chips):
            cp = pltpu.make_async_remote_copy(src_ref=w_ref.at[mine], dst_ref=o_ref.at[k, mine],
                                              send_sem=send_sems.at[j], recv_sem=recv_sems.at[j],
                                              device_id=(px, py, c), device_id_type=MESH)
            cp.start()
            first.append(cp)
        passed = []
        for j in range(3):
            blk = o_ref.at[idx[j], mine]
            pltpu.make_async_remote_copy(src_ref=blk, dst_ref=blk, send_sem=send_sems.at[j], recv_sem=recv_sems.at[j],
                                         device_id=(x, y, c), device_id_type=MESH).wait_recv()
            cp = pltpu.make_async_remote_copy(src_ref=blk, dst_ref=blk, send_sem=send_sems.at[3 + j],
                                              recv_sem=recv_sems.at[3 + j], device_id=(x, y, 1 - c),
                                              device_id_type=MESH)
            cp.start()
            passed.append(cp)
        for j in range(3):
            blk = o_ref.at[idx[j], theirs]
            pltpu.make_async_remote_copy(src_ref=blk, dst_ref=blk, send_sem=send_sems.at[3 + j],
                                         recv_sem=recv_sems.at[3 + j], device_id=(x, y, c),
                                         device_id_type=MESH).wait_recv()
        for cp in first + passed:
            cp.wait_send()
        local.wait()

    return pl.pallas_call(
        body, name="allgather_weights", in_specs=[ANY], out_specs=ANY,
        out_shape=jax.ShapeDtypeStruct((4, rows, 1024), wp.dtype),
        scratch_shapes=[pltpu.SemaphoreType.DMA((6,)), pltpu.SemaphoreType.DMA((6,)), pltpu.SemaphoreType.DMA])(wp)


def _sibling_swap(name, src, halves):
    half = src.shape[-2] // 2
    out_shape = (src.shape[0], half, 1024) if halves else src.shape

    def body(s_ref, o_ref, send_sem, recv_sem):
        x, y, c = _place()
        part = s_ref.at[:, pl.ds(pl.multiple_of((1 - c) * half, 8), half)] if halves else s_ref
        cp = pltpu.make_async_remote_copy(src_ref=part, dst_ref=o_ref, send_sem=send_sem, recv_sem=recv_sem,
                                          device_id=(x, y, 1 - c), device_id_type=MESH)
        cp.start()
        cp.wait()

    return pl.pallas_call(
        body, name=name, in_specs=[ANY], out_specs=ANY, out_shape=jax.ShapeDtypeStruct(out_shape, src.dtype),
        scratch_shapes=[pltpu.SemaphoreType.DMA, pltpu.SemaphoreType.DMA])(src)


def _chip_exchange(p):
    def body(p_ref, q_ref, send_sems, recv_sems, local_sem):
        x, y, c = _place()
        k = 2 * x + y
        local = pltpu.make_async_copy(p_ref.at[k], q_ref.at[k], local_sem)
        local.start()
        copies = []
        for j, (fx, fy) in enumerate(OTHER_CHIPS):
            px, py = (1 - x if fx else x), (1 - y if fy else y)
            cp = pltpu.make_async_remote_copy(src_ref=p_ref.at[2 * px + py], dst_ref=q_ref.at[k],
                                              send_sem=send_sems.at[j], recv_sem=recv_sems.at[j],
                                              device_id=(px, py, c), device_id_type=MESH)
            cp.start()
            copies.append(cp)
        for cp in copies:
            cp.wait()
        local.wait()

    return pl.pallas_call(
        body, name="grad_chip_exchange", in_specs=[ANY], out_specs=ANY,
        out_shape=jax.ShapeDtypeStruct(p.shape, p.dtype),
        scratch_shapes=[pltpu.SemaphoreType.DMA((3,)), pltpu.SemaphoreType.DMA((3,)), pltpu.SemaphoreType.DMA])(p)


def _pair_sum(g, r1, c):
    rows = g.shape[1]
    half = rows // 2
    th = _row_tile(half, 256)
    nblk = half // th

    def body(c_ref, g_ref, r_ref, o_ref):
        o_ref[...] = (g_ref[...] + r_ref[...]).astype(o_ref.dtype)

    spec = pl.BlockSpec((None, th, 1024), lambda k, i, c_ref: (k, i, 0))
    grid_spec = pltpu.PrefetchScalarGridSpec(
        num_scalar_prefetch=1, grid=(4, nblk),
        in_specs=[pl.BlockSpec((None, th, 1024), lambda k, i, c_ref: (k, c_ref[0] * nblk + i, 0)), spec],
        out_specs=spec)
    return pl.pallas_call(body, name="grad_pair_sum", grid_spec=grid_spec,
                          out_shape=jax.ShapeDtypeStruct((4, half, 1024), BF16),
                          compiler_params=_params(("parallel", "parallel")))(c, g, r1)


def _chip_sum(q):
    half = q.shape[1]
    th = _row_tile(half, 256)

    def body(a, b, c, d, o_ref):
        o_ref[...] = ((a[...].astype(F32) + b[...].astype(F32)) + c[...].astype(F32)) + d[...].astype(F32)

    specs = [pl.BlockSpec((None, th, 1024), functools.partial(lambda i, k: (k, i, 0), k=k)) for k in range(4)]
    return pl.pallas_call(body, name="grad_chip_sum", grid=(half // th,), in_specs=specs,
                          out_specs=pl.BlockSpec((th, 1024), lambda i: (i, 0)),
                          out_shape=jax.ShapeDtypeStruct((half, 1024), F32),
                          compiler_params=_params(("parallel",)))(q, q, q, q)


def _join_halves(f, r, c):
    half = f.shape[0]
    th = _row_tile(half, 256)
    nblk = half // th

    def body(c_ref, f_ref, r_ref, o_ref):
        mine = (pl.program_id(0) == c_ref[0])
        o_ref[...] = jnp.where(mine, f_ref[...], r_ref[...])

    spec = pl.BlockSpec((th, 1024), lambda h, i, c_ref: (i, 0))
    grid_spec = pltpu.PrefetchScalarGridSpec(
        num_scalar_prefetch=1, grid=(2, nblk), in_specs=[spec, spec],
        out_specs=pl.BlockSpec((th, 1024), lambda h, i, c_ref: (h * nblk + i, 0)))
    return pl.pallas_call(body, name="grad_join_halves", grid_spec=grid_spec,
                          out_shape=jax.ShapeDtypeStruct((2 * half, 1024), F32),
                          compiler_params=_params(("parallel", "parallel")))(c, f, r)


BIG_ARGS = ("in_w_even", "out_w_even", "qkv_w", "o_w", "ffn_gate_w", "ffn_up_w", "ffn_down_w")
SLAB = (("in_wt", 1156, 1168), ("out_w", 512, 512), ("qkv_wt", 320, 320), ("o_w", 256, 256),
        ("gate_wt0", 704, 704), ("gate_wt1", 704, 704), ("up_wt0", 704, 704), ("up_wt1", 704, 704),
        ("down_w0", 704, 704), ("down_w1", 704, 704))
SLAB_ROWS = 6528


def _slab(pieces):
    parts, used = [], 0
    for name, rows, room in SLAB:
        p = pieces[name]
        parts.append(jnp.pad(p, [(0, 0)] * (p.ndim - 2) + [(0, room - rows), (0, 0)]) if room > rows else p)
        used += room
    parts.append(jnp.zeros(parts[0].shape[:-2] + (SLAB_ROWS - used, D), parts[0].dtype))
    return jnp.concatenate(parts, axis=-2)


def _unslab(slab):
    out, off = {}, 0
    for name, rows, room in SLAB:
        out[name] = slab[..., off:off + rows, :]
        off += room
    return out


def _share_pieces(w):
    return {"in_wt": w["in_w_even"][0].T, "out_w": w["out_w_even"][0], "qkv_wt": w["qkv_w"][0].T, "o_w": w["o_w"][0],
            "gate_wt0": w["ffn_gate_w"][0].T, "gate_wt1": w["ffn_gate_w"][1].T,
            "up_wt0": w["ffn_up_w"][0].T, "up_wt1": w["ffn_up_w"][1].T,
            "down_w0": w["ffn_down_w"][0], "down_w1": w["ffn_down_w"][1]}


def _pieces_to_shares(p):
    return {"in_w_even": p["in_wt"].T[None], "out_w_even": p["out_w"][None], "qkv_w": p["qkv_wt"].T[None],
            "o_w": p["o_w"][None], "ffn_gate_w": jnp.stack([p["gate_wt0"].T, p["gate_wt1"].T]),
            "ffn_up_w": jnp.stack([p["up_wt0"].T, p["up_wt1"].T]),
            "ffn_down_w": jnp.stack([p["down_w0"], p["down_w1"]])}


def _full_from_chips(p):
    whole = {k: v.reshape(-1, D) for k, v in p.items()}
    return {"in_wt": whole["in_wt"], "out_w": whole["out_w"], "qkv_wt": whole["qkv_wt"], "o_w": whole["o_w"],
            "gate_wt": [whole["gate_wt0"], whole["gate_wt1"]], "up_wt": [whole["up_wt0"], whole["up_wt1"]],
            "down_w": [whole["down_w0"], whole["down_w1"]]}


def _chips_from_full(G):
    flat = {"in_wt": G["in_wt"], "out_w": G["out_w"], "qkv_wt": G["qkv_wt"], "o_w": G["o_w"],
            "gate_wt0": G["gate_wt"][0], "gate_wt1": G["gate_wt"][1], "up_wt0": G["up_wt"][0], "up_wt1": G["up_wt"][1],
            "down_w0": G["down_w"][0], "down_w1": G["down_w"][1]}
    return {k: v.reshape(4, -1, D) for k, v in flat.items()}


def _pack_small(parts):
    padded = []
    for p in parts:
        p = p.reshape(-1).astype(F32)
        padded.append(jnp.pad(p, (0, (-p.shape[0]) % 1024)))
    return jnp.concatenate(padded).reshape(-1, 128)


def _unpack_small(slab, shapes):
    flat, out, off = slab.reshape(-1), [], 0
    for shp in shapes:
        size = math.prod(shp)
        out.append(flat[off:off + size].reshape(shp))
        off += size + (-size) % 1024
    return out


SMALL = ("ada_b", "norm_mix_w", "norm_ffn_w", "conv_w", "conv_b", "dt_bias", "a_log", "d_skip", "ssm_norm_w",
         "gmlp_ln_w", "gmlp_ln_b", "gmlp_ws", "gmlp_bs", "qkv_b", "o_b", "sinks", "rel_table", "final_norm_w")
SMALL_SPLIT = {"conv_w": 1536, "qkv_b": 1280, "o_b": 1024}
WEIGHTS = ("ada_w", "ada_b", "norm_mix_w", "norm_ffn_w", "in_w_even", "conv_w", "conv_b", "dt_bias", "a_log", "d_skip",
           "ssm_norm_w", "gmlp_ln_w", "gmlp_ln_b", "gmlp_ws", "gmlp_bs", "out_w_even", "qkv_w", "qkv_b", "o_w", "o_b",
           "sinks", "rel_table", "ffn_gate_w", "ffn_up_w", "ffn_down_w", "final_norm_w")


def kernel(x, c, ada_w, ada_b, norm_mix_w, norm_ffn_w, in_w_even, conv_w, conv_b, dt_bias, a_log, d_skip, ssm_norm_w, gmlp_ln_w, gmlp_ln_b, gmlp_ws, gmlp_bs, out_w_even, qkv_w, qkv_b, o_w, o_b, sinks, rel_table, ffn_gate_w, ffn_up_w, ffn_down_w, final_norm_w, loss_target, m_ada_w, m_ada_b, m_norm_mix_w, m_norm_ffn_w, m_in_w_even, m_conv_w, m_conv_b, m_dt_bias, m_a_log, m_d_skip, m_ssm_norm_w, m_gmlp_ln_w, m_gmlp_ln_b, m_gmlp_ws, m_gmlp_bs, m_out_w_even, m_qkv_w, m_qkv_b, m_o_w, m_o_b, m_sinks, m_rel_table, m_ffn_gate_w, m_ffn_up_w, m_ffn_down_w, m_final_norm_w, v_ada_w, v_ada_b, v_norm_mix_w, v_norm_ffn_w, v_in_w_even, v_conv_w, v_conv_b, v_dt_bias, v_a_log, v_d_skip, v_ssm_norm_w, v_gmlp_ln_w, v_gmlp_ln_b, v_gmlp_ws, v_gmlp_bs, v_out_w_even, v_qkv_w, v_qkv_b, v_o_w, v_o_b, v_sinks, v_rel_table, v_ffn_gate_w, v_ffn_up_w, v_ffn_down_w, v_final_norm_w):
    args = dict(locals())
    w = {n: args[n] for n in WEIGHTS}
    m = {n: args["m_" + n] for n in WEIGHTS}
    v = {n: args["v_" + n] for n in WEIGHTS}
    ax, ay, ac = _place()
    me = 4 * ax + 2 * ay + ac
    chip = 2 * ax + ay
    south = (ac == 0).astype(F32)
    c_arr = jnp.reshape(ac, (1,)).astype(jnp.int32)

    share = _slab({k: p.astype(_MXU) for k, p in _share_pieces(w).items()})
    W = _full_from_chips(_unslab(_allgather_big(share)))

    c_all = _allreduce_small("gather_cond", lax.dynamic_update_slice(jnp.zeros((8, D), F32), c, (me, 0)).reshape(64, 128))
    c_all = c_all.reshape(8, D)
    n_ada = ada_w.shape[2]
    mod_cols = _ada_fwd(c_all, ada_w, lax.dynamic_slice(ada_b, (0, chip * n_ada), (2, n_ada)).reshape(2, 1, n_ada))
    pieces = [lax.dynamic_update_slice(jnp.zeros((2, 8, 6 * D), F32), mod_cols, (0, 0, chip * n_ada))]
    split_names = list(SMALL_SPLIT)
    for n in split_names:
        full = SMALL_SPLIT[n]
        local = w[n]
        idx = (0,) * (local.ndim - 1) + (chip * local.shape[-1],)
        pieces.append(lax.dynamic_update_slice(jnp.zeros(local.shape[:-1] + (full,), F32), local, idx))
    shapes = [p.shape for p in pieces]
    gathered = _unpack_small(_allreduce_small("gather_mod", _pack_small(pieces) * south), shapes)
    mod = lax.dynamic_slice(gathered[0], (0, me, 0), (2, 1, 6 * D)).reshape(2, 6 * D)
    P = {n: w[n] for n in SMALL if n not in SMALL_SPLIT and n != "ada_b"}
    for n, full in zip(split_names, gathered[1:]):
        P[n] = full
    P["final_norm_w"] = final_norm_w.reshape(1, D)

    sq, grad_x, dmod, G, g = _local_step(x[0], loss_target[0], mod, W, P)
    loss = lax.psum(0.5 * sq[0, 0] / D, ("x", "y", "c"))

    g["final_norm_w"] = g["final_norm_w"].reshape(D)
    small_names = [n for n in SMALL if n != "ada_b"]
    pieces = [lax.dynamic_update_slice(jnp.zeros((2, 8, 6 * D), F32), dmod.reshape(2, 1, 6 * D), (0, me, 0))]
    pieces += [g[n] for n in small_names]
    shapes = [p.shape for p in pieces]
    reduced = _unpack_small(_allreduce_small("allreduce_small_grads", _pack_small(pieces)), shapes)
    dmod_all = reduced[0]
    grads = dict(zip(small_names, reduced[1:]))
    for n in split_names:
        full = grads[n]
        size = w[n].shape[-1]
        grads[n] = lax.dynamic_slice(full, (0,) * (full.ndim - 1) + (chip * size,), full.shape[:-1] + (size,))
    grads = {n: grads[n].reshape(w[n].shape) for n in small_names}
    dw_ada, db_ada = _ada_bwd(c_all, lax.dynamic_slice(dmod_all, (0, 0, chip * n_ada), (2, 8, n_ada)), dmod_all)
    grads["ada_w"], grads["ada_b"] = dw_ada, db_ada.reshape(2, 6 * D)

    gp = _slab(_chips_from_full(G))
    pair = _pair_sum(gp, _sibling_swap("grad_pair_exchange", gp, True), c_arr)
    fin = _chip_sum(_chip_exchange(pair))
    total = _join_halves(fin, _sibling_swap("grad_final_exchange", fin, False), c_arr)
    grads.update(_pieces_to_shares(_unslab(total)))

    delta, new_m, new_v = {}, {}, {}
    for n in ("ada_w",) + BIG_ARGS:
        cols = w[n].shape[-1]
        d_, m_, v_ = _adamw("adamw_" + n, w[n].reshape(-1, cols), grads[n].reshape(-1, cols), m[n].reshape(-1, cols),
                            v[n].reshape(-1, cols))
        delta[n], new_m[n], new_v[n] = d_.reshape(w[n].shape), m_.reshape(w[n].shape), v_.reshape(w[n].shape)
    shapes = [w[n].shape for n in SMALL]
    packed = [_pack_small([t[n] for n in SMALL]) for t in (w, grads, m, v)]
    outs = _adamw("adamw_small", *packed)
    for dst, slab in zip((delta, new_m, new_v), outs):
        for n, t in zip(SMALL, _unpack_small(slab, shapes)):
            dst[n] = t
    return (loss, grad_x[None], *[grads[n] for n in WEIGHTS], *[delta[n] for n in WEIGHTS],
            *[new_m[n] for n in WEIGHTS], *[new_v[n] for n in WEIGHTS])
```

```python
import functools
import math

import numpy as np
import jax
import jax.numpy as jnp
from jax import lax
from jax.experimental import pallas as pl
from jax.experimental.pallas import tpu as pltpu

F32 = jnp.float32
BF16 = jnp.bfloat16
_MXU = jnp.bfloat16
_VMEM_LIMIT = 56 * 1024 * 1024
D = 1024
L = 128
NSTATE = 128
EPS = 1e-6
NEG_INF = -1e30
FFN = 2816
ADAM_LR, ADAM_B1, ADAM_B2, ADAM_EPS, ADAM_WD, ADAM_STEP = 0.001, 0.9, 0.999, 1e-08, 0.01, 10
MESH = pl.DeviceIdType.MESH
ANY = pl.BlockSpec(memory_space=pl.ANY)

NN = (((1,), (0,)), ((), ()))
NT = (((1,), (1,)), ((), ()))
TN = (((0,), (0,)), ((), ()))


def _dot(a, b, dn=NN):
    return lax.dot_general(a.astype(_MXU), b.astype(_MXU), dn, preferred_element_type=F32)


def _params(sem=None):
    return pltpu.CompilerParams(dimension_semantics=sem, vmem_limit_bytes=_VMEM_LIMIT)


def _sigmoid(x):
    return 1.0 / (1.0 + jnp.exp(-x))


def _softplus(x):
    return jnp.maximum(x, 0.0) + jnp.log(1.0 + jnp.exp(-jnp.abs(x)))


def _gelu(x):
    return 0.5 * x * (1.0 + lax.erf(x * (2.0 ** -0.5)))


def _gelu_grad(x):
    return 0.5 * (1.0 + lax.erf(x * (2.0 ** -0.5))) + x * jnp.exp(-0.5 * x * x) * (1.0 / math.sqrt(2.0 * math.pi))


def _silu_grad(a):
    sg = _sigmoid(a)
    return sg * (1.0 + a * (1.0 - sg))


def _rowwise(name, fn, rows, vecs, out_rows, out_accs=(), tr=512, after=None):
    S = rows[0].shape[0]
    tr = min(tr, S)
    assert S % tr == 0
    nr, nv, no, na = len(rows), len(vecs), len(out_rows), len(out_accs)
    deps = [] if after is None else [after]

    def body(*refs):
        ins, outs = refs[:nr + nv], refs[nr + nv + len(deps):]
        res = fn(*[r[...] for r in ins])
        if not isinstance(res, (tuple, list)):
            res = (res,)
        for k in range(no):
            outs[k][...] = res[k].astype(outs[k].dtype)
        if na:
            @pl.when(pl.program_id(0) == 0)
            def _():
                for k in range(na):
                    outs[no + k][...] = jnp.zeros_like(outs[no + k])
            for k in range(na):
                outs[no + k][...] += res[no + k]

    in_specs = [pl.BlockSpec((tr, a.shape[1]), lambda i: (i, 0)) for a in rows]
    in_specs += [pl.BlockSpec(v.shape, lambda i: (0, 0)) for v in vecs] + [ANY for _ in deps]
    out_specs = [pl.BlockSpec((tr, c), lambda i: (i, 0)) for c, _ in out_rows]
    out_specs += [pl.BlockSpec(s, lambda i: (0, 0)) for s in out_accs]
    out_shape = [jax.ShapeDtypeStruct((S, c), dt) for c, dt in out_rows]
    out_shape += [jax.ShapeDtypeStruct(s, F32) for s in out_accs]
    return pl.pallas_call(body, name=name, grid=(S // tr,), in_specs=in_specs, out_specs=out_specs,
                          out_shape=out_shape, compiler_params=_params(("arbitrary",)))(*rows, *vecs, *deps)


def _col_tile(n, cap):
    if n <= cap or n % 128:
        return n
    best = 128
    for t in range(128, cap + 1, 128):
        if n % t == 0:
            best = t
    return best


def _mm(name, As, Bs, mode, outs, epi=None, groups=None, extras=(), vecs=(), tm=512, tn_cap=1536):
    M = As[0].shape[0]
    N = Bs[0].shape[1] if mode == "nn" else Bs[0].shape[0]
    tm = min(tm, M)
    tn = _col_tile(N, tn_cap)
    assert M % tm == 0 and N % tn == 0
    npair = len(As)
    groups = groups or [0] * npair
    ng = max(groups) + 1
    nx, nv = len(extras), len(vecs)
    dn = NN if mode == "nn" else NT

    def body(*refs):
        a_refs, b_refs = refs[:npair], refs[npair:2 * npair]
        x_refs = refs[2 * npair:2 * npair + nx]
        v_refs = refs[2 * npair + nx:2 * npair + nx + nv]
        o_refs = refs[2 * npair + nx + nv:]
        accs = [None] * ng
        for k in range(npair):
            d = _dot(a_refs[k][...], b_refs[k][...], dn)
            accs[groups[k]] = d if accs[groups[k]] is None else accs[groups[k]] + d
        args = accs + [x[...] for x in x_refs] + [v[...] for v in v_refs]
        res = epi(*args) if epi is not None else tuple(accs)
        if not isinstance(res, (tuple, list)):
            res = (res,)
        for o, r in zip(o_refs, res):
            o[...] = r.astype(o.dtype)

    in_specs = [pl.BlockSpec((tm, a.shape[1]), lambda i, j: (i, 0)) for a in As]
    if mode == "nn":
        in_specs += [pl.BlockSpec((b.shape[0], tn), lambda i, j: (0, j)) for b in Bs]
    else:
        in_specs += [pl.BlockSpec((tn, b.shape[1]), lambda i, j: (j, 0)) for b in Bs]
    in_specs += [pl.BlockSpec((tm, tn), lambda i, j: (i, j)) for _ in extras]
    in_specs += [pl.BlockSpec((1, tn), lambda i, j: (0, j)) for _ in vecs]
    out_specs = [pl.BlockSpec((tm, tn), lambda i, j: (i, j)) for _ in outs]
    out_shape = [jax.ShapeDtypeStruct((M, N), dt) for dt in outs]
    return pl.pallas_call(body, name=name, grid=(M // tm, N // tn), in_specs=in_specs, out_specs=out_specs,
                          out_shape=out_shape, compiler_params=_params(("parallel", "parallel")))(
                              *As, *Bs, *extras, *vecs)


def _mm_tn(name, A, B, tk=512, t2_cap=1536):
    S, K1 = A.shape
    N2 = B.shape[1]
    tk = min(tk, S)
    t2 = _col_tile(N2, t2_cap)
    assert S % tk == 0 and N2 % t2 == 0

    def body(a_ref, b_ref, o_ref):
        @pl.when(pl.program_id(1) == 0)
        def _():
            o_ref[...] = jnp.zeros_like(o_ref)
        o_ref[...] += _dot(a_ref[...], b_ref[...], TN)

    return pl.pallas_call(
        body, name=name, grid=(N2 // t2, S // tk),
        in_specs=[pl.BlockSpec((tk, K1), lambda j, k: (k, 0)), pl.BlockSpec((tk, t2), lambda j, k: (k, j))],
        out_specs=pl.BlockSpec((K1, t2), lambda j, k: (0, j)),
        out_shape=jax.ShapeDtypeStruct((K1, N2), F32),
        compiler_params=_params(("parallel", "arbitrary")))(A, B)


def _norm_mod_fwd(name, x, nw, sc, sh):
    def fn(x, nw, sc, sh):
        rstd = lax.rsqrt(jnp.mean(x * x, axis=-1, keepdims=True) + EPS)
        return (x * rstd * nw) * (1.0 + sc) + sh
    return _rowwise(name, fn, [x], [nw, sc, sh], [(D, BF16)])[0]


def _norm_mod_bwd(name, x, dh, dres, nw, sc):
    def fn(x, dh, dres, nw, sc):
        rstd = lax.rsqrt(jnp.mean(x * x, axis=-1, keepdims=True) + EPS)
        xh = x * rstd
        dn = dh * (1.0 + sc)
        dxh = dn * nw
        dx = rstd * (dxh - xh * jnp.mean(dxh * xh, axis=-1, keepdims=True))
        return (dres + dx, jnp.sum(dh, axis=0, keepdims=True), jnp.sum(dh * (xh * nw), axis=0, keepdims=True),
                jnp.sum(dn * xh, axis=0, keepdims=True))
    return _rowwise(name, fn, [x, dh, dres], [nw, sc], [(D, F32)], [(1, D)] * 3)


def _gate_bwd(name, dx, y, g, after=None):
    def fn(dx, y, g):
        dy = dx * g
        return dy, jnp.sum(dx * y, axis=0, keepdims=True), jnp.sum(dy, axis=0, keepdims=True)
    return _rowwise(name, fn, [dx, y], [g], [(D, BF16)], [(1, D)] * 2, after=after)


def _loss_head(x, tgt, fw):
    def fn(x, tgt, fw):
        rstd = lax.rsqrt(jnp.mean(x * x, axis=-1, keepdims=True) + EPS)
        xh = x * rstd
        err = xh * fw - tgt
        dout = err * (1.0 / D)
        dxh = dout * fw
        dx = rstd * (dxh - xh * jnp.mean(dxh * xh, axis=-1, keepdims=True))
        sq = jnp.sum(jnp.sum(err * err, axis=1, keepdims=True), axis=0, keepdims=True)
        return dx, sq, jnp.sum(dout * xh, axis=0, keepdims=True)
    return _rowwise("loss_head", fn, [x, tgt], [fw], [(D, F32)], [(1, 1), (1, D)])


def _ffn_fwd(tag, h, wg, wu, wd, x, g2):
    def act(a, b):
        return a, b, a * _sigmoid(a) * b
    a, b, f = _mm(f"ffn_up_{tag}", [h, h], [wg, wu], "nt", [F32, F32, BF16], epi=act, groups=[0, 1], tn_cap=1408)

    def res(y, x, g):
        return y, x + g * y
    y, xo = _mm(f"ffn_down_{tag}", [f], [wd], "nn", [F32, F32], epi=res, extras=[x], vecs=[g2])
    return a, b, f, y, xo


def _ffn_bwd(tag, dx, h, a, b, f, y, wg, wu, wd, g2, after=None):
    dy, dg2, _ = _gate_bwd(f"ffn_gate_bwd_{tag}", dx, y, g2, after=after)

    def act_bwd(df, a, b):
        return df * b * _silu_grad(a), df * (a * _sigmoid(a))
    da, db = _mm(f"ffn_dact_{tag}", [dy], [wd], "nt", [BF16, BF16], epi=act_bwd, extras=[a, b], tn_cap=1408)
    dwd = _mm_tn(f"ffn_dwd_{tag}", f, dy)
    dwg = _mm_tn(f"ffn_dwg_{tag}", da, h)
    dwu = _mm_tn(f"ffn_dwu_{tag}", db, h)
    dh = _mm(f"ffn_dh_{tag}", [da, db], [wg, wu], "nn", [F32])[0]
    return dh, dg2, dwg, dwu, dwd


def _conv_fwd(xr, w, b, tb=512):
    S, C = xr.shape
    tb = min(tb, S)

    def body(x_ref, halo_ref, w_ref, b_ref, pre_ref, out_ref):
        i = pl.program_id(0)
        halo = jnp.where(i > 0, halo_ref[...], 0.0)
        xe = jnp.concatenate([halo, x_ref[...]], axis=0)
        pre = w_ref[3:4, :] * x_ref[...] + b_ref[...]
        for j in (1, 2, 3):
            pre = pre + w_ref[3 - j:4 - j, :] * pltpu.roll(xe, j, axis=0)[8:, :]
        pre_ref[...] = pre
        out_ref[...] = pre * _sigmoid(pre)

    return pl.pallas_call(
        body, name="conv_fwd", grid=(S // tb,),
        in_specs=[pl.BlockSpec((tb, C), lambda i: (i, 0)),
                  pl.BlockSpec((8, C), lambda i: (jnp.maximum(i * (tb // 8) - 1, 0), 0)),
                  pl.BlockSpec((4, C), lambda i: (0, 0)), pl.BlockSpec((1, C), lambda i: (0, 0))],
        out_specs=[pl.BlockSpec((tb, C), lambda i: (i, 0))] * 2,
        out_shape=[jax.ShapeDtypeStruct((S, C), F32)] * 2,
        compiler_params=_params(("parallel",)))(xr, xr, w, b)


def _conv_bwd(dxc, pre, xr, w, tb=512):
    S, C = xr.shape
    tb = min(tb, S)
    nblk = S // tb

    def body(d_ref, p_ref, dn_ref, pn_ref, x_ref, xh_ref, w_ref, dx_ref, dw_ref, db_ref):
        i = pl.program_id(0)

        @pl.when(i == 0)
        def _():
            dw_ref[...] = jnp.zeros_like(dw_ref)
            db_ref[...] = jnp.zeros_like(db_ref)

        dpre = d_ref[...] * _silu_grad(p_ref[...])
        dnext = jnp.where(i < nblk - 1, dn_ref[...] * _silu_grad(pn_ref[...]), 0.0)
        pe = jnp.concatenate([dpre, dnext], axis=0)
        dx = w_ref[3:4, :] * dpre
        for j in (1, 2, 3):
            dx = dx + w_ref[3 - j:4 - j, :] * pltpu.roll(pe, tb + 8 - j, axis=0)[:tb, :]
        dx_ref[...] = dx.astype(dx_ref.dtype)
        halo = jnp.where(i > 0, xh_ref[...], 0.0)
        xe = jnp.concatenate([halo, x_ref[...]], axis=0)
        for k in range(3):
            dw_ref[k:k + 1, :] += jnp.sum(dpre * pltpu.roll(xe, 3 - k, axis=0)[8:, :], axis=0, keepdims=True)
        dw_ref[3:4, :] += jnp.sum(dpre * x_ref[...], axis=0, keepdims=True)
        db_ref[...] += jnp.sum(dpre, axis=0, keepdims=True)

    blk = pl.BlockSpec((tb, C), lambda i: (i, 0))
    nxt = pl.BlockSpec((8, C), lambda i: (jnp.minimum((i + 1) * (tb // 8), S // 8 - 1), 0))
    prv = pl.BlockSpec((8, C), lambda i: (jnp.maximum(i * (tb // 8) - 1, 0), 0))
    return pl.pallas_call(
        body, name="conv_bwd", grid=(nblk,),
        in_specs=[blk, blk, nxt, nxt, blk, prv, pl.BlockSpec((4, C), lambda i: (0, 0))],
        out_specs=[blk, pl.BlockSpec((4, C), lambda i: (0, 0)), pl.BlockSpec((1, C), lambda i: (0, 0))],
        out_shape=[jax.ShapeDtypeStruct((S, C), BF16), jax.ShapeDtypeStruct((4, C), F32),
                   jax.ShapeDtypeStruct((1, C), F32)],
        compiler_params=_params(("arbitrary",)))(dxc, pre, dxc, pre, xr, xr, w)


def _iota(shape, dim):
    return lax.broadcasted_iota(jnp.int32, shape, dim)


def _colsel(m, lane, h):
    return jnp.sum(jnp.where(lane == h, m, 0.0), axis=1, keepdims=True)


def _cumsum_rows(v):
    r = _iota(v.shape, 0)
    k = 1
    while k < v.shape[0]:
        v = v + jnp.where(r >= k, pltpu.roll(v, k, axis=0), 0.0)
        k *= 2
    return v


def _suffix_sum_rows(v):
    n = v.shape[0]
    r = _iota(v.shape, 0)
    k = 1
    while k < n:
        v = v + jnp.where(r < n - k, pltpu.roll(v, n - k, axis=0), 0.0)
        k *= 2
    return v


def _ssd_fwd(xc, dtr, z, dtb, alog, dskl, nw):
    S = xc.shape[0]
    nc = S // L

    def body(xc_ref, dtr_ref, z_ref, dtb_ref, alog_ref, dsk_ref, nw_ref, ya_ref, y_ref, prev_ref,
             st_ref, cum_ref, cumT_ref):
        i = pl.program_id(0)

        @pl.when(i == 0)
        def _():
            st_ref[...] = jnp.zeros_like(st_ref)

        lane = _iota((L, 128), 1)
        lane1 = _iota((1, 128), 1)
        lo = lane < 64
        lo1 = lane1 < 64
        tril = _iota((L, L), 0) >= _iota((L, L), 1)
        dt = _softplus(dtr_ref[...] + dtb_ref[...])
        a_neg = -jnp.exp(alog_ref[...])
        cum = _cumsum_rows(dt * a_neg)
        cum_ref[...] = cum
        cumT_ref[...] = cum.T
        last_all = cum_ref[L - 1:L, :]
        prev_t = st_ref[...]
        prev_ref[0] = prev_t
        for g in range(2):
            bg = xc_ref[:, 1024 + g * 128:1152 + g * 128]
            cg = xc_ref[:, 1280 + g * 128:1408 + g * 128]
            gmat = _dot(cg, bg, NT)
            yoff = _dot(cg, prev_t[:, g * 512:(g + 1) * 512])
            bg_t = bg.T
            for jp in range(4):
                j = g * 4 + jp
                sl = slice(j * 128, (j + 1) * 128)
                xp = xc_ref[:, sl]
                cc = [_colsel(cum, lane, 2 * j), _colsel(cum, lane, 2 * j + 1)]
                cum_l = jnp.where(lo, cc[0], cc[1])
                dt_l = jnp.where(lo, _colsel(dt, lane, 2 * j), _colsel(dt, lane, 2 * j + 1))
                last_l = jnp.where(lo1, _colsel(last_all, lane1, 2 * j), _colsel(last_all, lane1, 2 * j + 1))
                xd = xp * dt_l
                ys = []
                for hh in range(2):
                    seg = cc[hh] - cumT_ref[2 * j + hh:2 * j + hh + 1, :]
                    dm = jnp.where(tril, jnp.exp(jnp.where(tril, seg, 0.0)), 0.0)
                    ys.append(_dot(gmat * dm, xd))
                y_ref[:, sl] = (jnp.where(lo, ys[0], ys[1]) + jnp.exp(cum_l) * yoff[:, jp * 128:(jp + 1) * 128]
                                + dsk_ref[:, sl] * xp)
                st_ref[:, sl] = prev_t[:, sl] * jnp.exp(last_l) + _dot(bg_t, xd * jnp.exp(last_l - cum_l))
        for g in range(2):
            sl = slice(g * 512, (g + 1) * 512)
            zz = z_ref[:, sl]
            yg = y_ref[:, sl] * (zz * _sigmoid(zz))
            rstd = lax.rsqrt(jnp.mean(yg * yg, axis=-1, keepdims=True) + EPS)
            ya_ref[:, sl] = (yg * rstd * nw_ref[:, sl]).astype(ya_ref.dtype)

    blk = lambda c: pl.BlockSpec((L, c), lambda i: (i, 0))
    vec = lambda c: pl.BlockSpec((1, c), lambda i: (0, 0))
    return pl.pallas_call(
        body, name="ssd_fwd", grid=(nc,),
        in_specs=[blk(1536), blk(128), blk(1024), vec(128), vec(128), vec(1024), vec(1024)],
        out_specs=[blk(1024), blk(1024), pl.BlockSpec((1, NSTATE, 1024), lambda i: (i, 0, 0))],
        out_shape=[jax.ShapeDtypeStruct((S, 1024), BF16), jax.ShapeDtypeStruct((S, 1024), F32),
                   jax.ShapeDtypeStruct((nc, NSTATE, 1024), F32)],
        scratch_shapes=[pltpu.VMEM((NSTATE, 1024), F32), pltpu.VMEM((L, 128), F32), pltpu.VMEM((L, 128), F32)],
        compiler_params=_params(("arbitrary",)))(xc, dtr, z, dtb, alog, dskl, nw)


def _ssd_bwd(dya, y, z, xc, dtr, prev, dtb, alog, dskl, nw):
    S = xc.shape[0]
    nc = S // L

    def body(dya_ref, y_ref, z_ref, xc_ref, dtr_ref, prev_ref, dtb_ref, alog_ref, dsk_ref, nw_ref,
             dz_ref, dxc_ref, ddtr_ref, dnw_ref, ddsk_ref, dalog_ref, ddtb_ref,
             dst_ref, cum_ref, cumT_ref, dy_ref, dskacc_ref):
        i = pl.program_id(0)

        @pl.when(i == 0)
        def _():
            dst_ref[...] = jnp.zeros_like(dst_ref)
            dskacc_ref[...] = jnp.zeros_like(dskacc_ref)
            dnw_ref[...] = jnp.zeros_like(dnw_ref)
            dalog_ref[...] = jnp.zeros_like(dalog_ref)
            ddtb_ref[...] = jnp.zeros_like(ddtb_ref)

        lane = _iota((L, 128), 1)
        lane1 = _iota((1, 128), 1)
        lo = lane < 64
        lo1 = lane1 < 64
        r2, c2 = _iota((L, L), 0), _iota((L, L), 1)
        tril = r2 >= c2
        triu = r2 <= c2
        is_last = _iota((L, 1), 0) == L - 1

        for g in range(2):
            sl = slice(g * 512, (g + 1) * 512)
            zz = z_ref[:, sl]
            sg = _sigmoid(zz)
            zg = zz * sg
            yv = y_ref[:, sl]
            yg = yv * zg
            rstd = lax.rsqrt(jnp.mean(yg * yg, axis=-1, keepdims=True) + EPS)
            xh = yg * rstd
            d_out = dya_ref[:, sl]
            dnw_ref[:, sl] += jnp.sum(d_out * xh, axis=0, keepdims=True)
            dyn = d_out * nw_ref[:, sl]
            dyg = rstd * (dyn - xh * jnp.mean(dyn * xh, axis=-1, keepdims=True))
            dy_ref[:, sl] = dyg * zg
            dz_ref[:, sl] = (dyg * yv * (sg * (1.0 + zz * (1.0 - sg)))).astype(dz_ref.dtype)

        dtin = dtr_ref[...] + dtb_ref[...]
        dt = _softplus(dtin)
        a_neg = -jnp.exp(alog_ref[...])
        cum = _cumsum_rows(dt * a_neg)
        cum_ref[...] = cum
        cumT_ref[...] = cum.T
        last_all = cum_ref[L - 1:L, :]
        prev_t = prev_ref[0]
        dn_t = dst_ref[...]
        dcum = jnp.zeros((L, 128), F32)
        ddt = jnp.zeros((L, 128), F32)
        for g in range(2):
            gsl = slice(g * 512, (g + 1) * 512)
            bg = xc_ref[:, 1024 + g * 128:1152 + g * 128]
            cg = xc_ref[:, 1280 + g * 128:1408 + g * 128]
            gmat = _dot(cg, bg, NT)
            gmat_t = _dot(bg, cg, NT)
            pg = prev_t[:, gsl]
            zmat = _dot(cg, pg)
            dgm = jnp.zeros((L, L), F32)
            dgm_t = jnp.zeros((L, L), F32)
            db_acc = jnp.zeros((L, NSTATE), F32)
            dz_parts, cd_parts = [], []
            for jp in range(4):
                j = g * 4 + jp
                sl = slice(j * 128, (j + 1) * 128)
                xp = xc_ref[:, sl]
                dyp = dy_ref[:, sl]
                cc = [_colsel(cum, lane, 2 * j), _colsel(cum, lane, 2 * j + 1)]
                lc = [_colsel(last_all, lane1, 2 * j), _colsel(last_all, lane1, 2 * j + 1)]
                cum_l = jnp.where(lo, cc[0], cc[1])
                dt_l = jnp.where(lo, _colsel(dt, lane, 2 * j), _colsel(dt, lane, 2 * j + 1))
                last_l = jnp.where(lo1, lc[0], lc[1])
                e_l = jnp.exp(cum_l)
                dte_l = jnp.exp(last_l - cum_l)
                cd_l = jnp.exp(last_l)
                cd_parts.append(cd_l)
                xd = xp * dt_l
                dskacc_ref[:, sl] += jnp.sum(dyp * xp, axis=0, keepdims=True)
                dxp = dsk_ref[:, sl] * dyp
                t = dyp * (e_l * zmat[:, jp * 128:(jp + 1) * 128])
                dcc = [jnp.sum(jnp.where(lo, t, 0.0), axis=1, keepdims=True),
                       jnp.sum(jnp.where(lo, 0.0, t), axis=1, keepdims=True)]
                dz_parts.append(e_l * dyp)
                dnp_ = dn_t[:, sl]
                t2 = jnp.sum(dnp_ * prev_t[:, sl], axis=0, keepdims=True)
                dcd = [jnp.sum(jnp.where(lo1, t2, 0.0), axis=1, keepdims=True),
                       jnp.sum(jnp.where(lo1, 0.0, t2), axis=1, keepdims=True)]
                wm = _dot(bg, dnp_)
                dxd = wm * dte_l
                t3 = wm * xd
                ddte = [jnp.sum(jnp.where(lo, t3, 0.0), axis=1, keepdims=True),
                        jnp.sum(jnp.where(lo, 0.0, t3), axis=1, keepdims=True)]
                db_acc = db_acc + _dot(xd * dte_l, dnp_, NT)
                for hh in range(2):
                    h = 2 * j + hh
                    half = lo if hh == 0 else jnp.logical_not(lo)
                    row = cumT_ref[h:h + 1, :]
                    dm = jnp.where(tril, jnp.exp(jnp.where(tril, cc[hh] - row, 0.0)), 0.0)
                    dm_t = jnp.where(triu, jnp.exp(jnp.where(triu, row - cc[hh], 0.0)), 0.0)
                    m = gmat * dm
                    m_t = gmat_t * dm_t
                    dym = jnp.where(half, dyp, 0.0)
                    d_m = _dot(dym, xd, NT)
                    d_mt = _dot(xd, dym, NT)
                    dxd = dxd + _dot(m_t, dym)
                    dcc[hh] = dcc[hh] + jnp.sum(d_m * m, axis=1, keepdims=True) - jnp.sum(d_mt * m_t, axis=1, keepdims=True)
                    dgm = dgm + d_m * dm
                    dgm_t = dgm_t + d_mt * dm_t
                    dte_c = jnp.exp(lc[hh] - cc[hh])
                    dcc[hh] = dcc[hh] - ddte[hh] * dte_c
                    endc = dcd[hh] * jnp.exp(lc[hh]) + jnp.sum(ddte[hh] * dte_c, axis=0, keepdims=True)
                    dcc[hh] = dcc[hh] + jnp.where(is_last, endc, 0.0)
                    dcum = jnp.where(lane == h, dcc[hh], dcum)
                dxc_ref[:, sl] = dxp + dxd * dt_l
                t4 = dxd * xp
                ddt = jnp.where(lane == 2 * j, jnp.sum(jnp.where(lo, t4, 0.0), axis=1, keepdims=True), ddt)
                ddt = jnp.where(lane == 2 * j + 1, jnp.sum(jnp.where(lo, 0.0, t4), axis=1, keepdims=True), ddt)
            dzg = jnp.concatenate(dz_parts, axis=1)
            dst_ref[:, gsl] = dn_t[:, gsl] * jnp.concatenate(cd_parts, axis=1) + _dot(cg.T, dzg)
            dxc_ref[:, 1280 + g * 128:1408 + g * 128] = _dot(dgm, bg) + _dot(dzg, pg, NT)
            dxc_ref[:, 1024 + g * 128:1152 + g * 128] = _dot(dgm_t, cg) + db_acc
        dla = _suffix_sum_rows(dcum)
        ddt = ddt + dla * a_neg
        dalog_ref[...] += jnp.sum(dla * dt, axis=0, keepdims=True) * a_neg
        ddtr = jnp.where(lane < 16, ddt * _sigmoid(dtin), 0.0)
        ddtr_ref[...] = ddtr.astype(ddtr_ref.dtype)
        ddtb_ref[...] += jnp.sum(ddtr, axis=0, keepdims=True)

        @pl.when(i == nc - 1)
        def _():
            seg = (_iota((1024, 128), 0) // 64 == _iota((1024, 128), 1)).astype(F32)
            acc8 = jnp.broadcast_to(dskacc_ref[...], (8, 1024))
            ddsk_ref[...] = lax.dot_general(acc8, seg, NN, precision=lax.Precision.HIGHEST,
                                            preferred_element_type=F32)

    rev = lambda c: pl.BlockSpec((L, c), lambda i: (nc - 1 - i, 0))
    vec = lambda c: pl.BlockSpec((1, c), lambda i: (0, 0))
    return pl.pallas_call(
        body, name="ssd_bwd", grid=(nc,),
        in_specs=[rev(1024), rev(1024), rev(1024), rev(1536), rev(128),
                  pl.BlockSpec((1, NSTATE, 1024), lambda i: (nc - 1 - i, 0, 0)),
                  vec(128), vec(128), vec(1024), vec(1024)],
        out_specs=[rev(1024), rev(1536), rev(128), vec(1024), pl.BlockSpec((8, 128), lambda i: (0, 0)),
                   vec(128), vec(128)],
        out_shape=[jax.ShapeDtypeStruct((S, 1024), BF16), jax.ShapeDtypeStruct((S, 1536), F32),
                   jax.ShapeDtypeStruct((S, 128), BF16), jax.ShapeDtypeStruct((1, 1024), F32),
                   jax.ShapeDtypeStruct((8, 128), F32), jax.ShapeDtypeStruct((1, 128), F32),
                   jax.ShapeDtypeStruct((1, 128), F32)],
        scratch_shapes=[pltpu.VMEM((NSTATE, 1024), F32), pltpu.VMEM((L, 128), F32), pltpu.VMEM((L, 128), F32),
                        pltpu.VMEM((L, 1024), F32), pltpu.VMEM((1, 1024), F32)],
        compiler_params=_params(("arbitrary",)))(dya, y, z, xc, dtr, prev, dtb, alog, dskl, nw)


def _layer_norm_parts(vg):
    mu = jnp.mean(vg, axis=-1, keepdims=True)
    vc = vg - mu
    rstd = lax.rsqrt(jnp.mean(vc * vc, axis=-1, keepdims=True) + EPS)
    return vc * rstd, rstd


def _gmlp_fwd(u, v, lnw, lnb, ws, bse, tb=512):
    S = u.shape[0]
    tb = min(tb, S)

    def body(u_ref, v_ref, lnw_ref, lnb_ref, ws_ref, bse_ref, o_ref, vn_ref):
        tril = _iota((L, L), 0) >= _iota((L, L), 1)
        xh, _ = _layer_norm_parts(_gelu(v_ref[...]))
        vn_ref[...] = xh * lnw_ref[...] + lnb_ref[...]
        for g in range(8):
            w = jnp.where(tril, ws_ref[g], 0.0)
            gs = slice(g * 128, (g + 1) * 128)
            for ch in range(tb // L):
                rs = slice(ch * L, (ch + 1) * L)
                sv = _dot(w, vn_ref[rs, gs]) + bse_ref[g]
                o_ref[rs, gs] = (_gelu(u_ref[rs, gs]) * sv).astype(o_ref.dtype)

    blk = pl.BlockSpec((tb, 1024), lambda i: (i, 0))
    vec = pl.BlockSpec((1, 1024), lambda i: (0, 0))
    cube = pl.BlockSpec((8, L, 128), lambda i: (0, 0, 0))
    return pl.pallas_call(
        body, name="gmlp_fwd", grid=(S // tb,), in_specs=[blk, blk, vec, vec, cube, cube], out_specs=blk,
        out_shape=jax.ShapeDtypeStruct((S, 1024), BF16), scratch_shapes=[pltpu.VMEM((tb, 1024), F32)],
        compiler_params=_params(("parallel",)))(u, v, lnw, lnb, ws, bse)


def _gmlp_bwd(dyb, u, v, lnw, lnb, ws, bse, tb=512):
    S = u.shape[0]
    tb = min(tb, S)

    def body(d_ref, u_ref, v_ref, lnw_ref, lnb_ref, ws_ref, bse_ref,
             du_ref, dv_ref, dws_ref, dbse_ref, dlnw_ref, dlnb_ref, vn_ref, dvn_ref):
        @pl.when(pl.program_id(0) == 0)
        def _():
            dws_ref[...] = jnp.zeros_like(dws_ref)
            dbse_ref[...] = jnp.zeros_like(dbse_ref)
            dlnw_ref[...] = jnp.zeros_like(dlnw_ref)
            dlnb_ref[...] = jnp.zeros_like(dlnb_ref)

        tril = _iota((L, L), 0) >= _iota((L, L), 1)
        vv = v_ref[...]
        xh, rstd = _layer_norm_parts(_gelu(vv))
        vn_ref[...] = xh * lnw_ref[...] + lnb_ref[...]
        for g in range(8):
            w = jnp.where(tril, ws_ref[g], 0.0)
            w_t = w.T
            gs = slice(g * 128, (g + 1) * 128)
            dw = jnp.zeros((L, L), F32)
            dbs = jnp.zeros((L, 128), F32)
            for ch in range(tb // L):
                rs = slice(ch * L, (ch + 1) * L)
                vn = vn_ref[rs, gs]
                sv = _dot(w, vn) + bse_ref[g]
                uu = u_ref[rs, gs]
                dd = d_ref[rs, gs]
                du_ref[rs, gs] = (dd * sv * _gelu_grad(uu)).astype(du_ref.dtype)
                dsv = dd * _gelu(uu)
                dw = dw + _dot(dsv, vn, NT)
                dbs = dbs + dsv
                dvn_ref[rs, gs] = _dot(w_t, dsv)
            dws_ref[g] += jnp.where(tril, dw, 0.0)
            dbse_ref[g] += dbs
        dvn = dvn_ref[...]
        dlnw_ref[...] += jnp.sum(dvn * xh, axis=0, keepdims=True)
        dlnb_ref[...] += jnp.sum(dvn, axis=0, keepdims=True)
        dxh = dvn * lnw_ref[...]
        dvg = rstd * (dxh - jnp.mean(dxh, axis=-1, keepdims=True) - xh * jnp.mean(dxh * xh, axis=-1, keepdims=True))
        dv_ref[...] = (dvg * _gelu_grad(vv)).astype(dv_ref.dtype)

    blk = pl.BlockSpec((tb, 1024), lambda i: (i, 0))
    vec = pl.BlockSpec((1, 1024), lambda i: (0, 0))
    cube = pl.BlockSpec((8, L, 128), lambda i: (0, 0, 0))
    return pl.pallas_call(
        body, name="gmlp_bwd", grid=(S // tb,), in_specs=[blk, blk, blk, vec, vec, cube, cube],
        out_specs=[blk, blk, cube, cube, vec, vec],
        out_shape=[jax.ShapeDtypeStruct((S, 1024), BF16), jax.ShapeDtypeStruct((S, 1024), BF16),
                   jax.ShapeDtypeStruct((8, L, 128), F32), jax.ShapeDtypeStruct((8, L, 128), F32),
                   jax.ShapeDtypeStruct((1, 1024), F32), jax.ShapeDtypeStruct((1, 1024), F32)],
        scratch_shapes=[pltpu.VMEM((tb, 1024), F32), pltpu.VMEM((tb, 1024), F32)],
        compiler_params=_params(("arbitrary",)))(dyb, u, v, lnw, lnb, ws, bse)


def _lane_sum(name, a):
    def body(a_ref, o_ref):
        o_ref[...] = jnp.sum(a_ref[...], axis=1, keepdims=True)
    return pl.pallas_call(body, name=name, out_shape=jax.ShapeDtypeStruct((a.shape[0], 1), F32))(a)


def _bucket_onehot_t():
    qi = np.arange(L)[:, None]
    sj = np.arange(2 * L)[None, :]
    dist = np.maximum(qi + L - sj, 0)
    log_ratio = (np.log(np.maximum(dist, 1).astype(np.float32) / np.float32(16)) / np.float32(math.log(128 / 16)))
    large = 16 + (log_ratio.astype(np.float32) * np.float32(16)).astype(np.int32)
    bucket = np.where(dist < 16, dist, np.minimum(large, 31)).reshape(-1)
    return (np.arange(32)[:, None] == bucket[None, :]).astype(np.float32)


def _rel_bias(table_t, onehot_t):
    def body(t_ref, oh_ref, o_ref):
        o_ref[...] = lax.dot_general(t_ref[...], oh_ref[...], NN, precision=lax.Precision.HIGHEST,
                                     preferred_element_type=F32)
    return pl.pallas_call(body, name="rel_bias", out_shape=jax.ShapeDtypeStruct((16, L * 2 * L), F32),
                          compiler_params=_params())(table_t, onehot_t)


def _rel_bias_bwd(dbias, onehot_t):
    def body(d_ref, oh_ref, o_ref):
        o_ref[...] = lax.dot_general(d_ref[...], oh_ref[...], NT, precision=lax.Precision.HIGHEST,
                                     preferred_element_type=F32)
    return pl.pallas_call(body, name="rel_bias_bwd", out_shape=jax.ShapeDtypeStruct((16, 32), F32),
                          compiler_params=_params())(dbias, onehot_t)


def _band(kp, kc, lo):
    kk = jnp.concatenate([kp, kc], axis=0)
    kr = pltpu.roll(kk, 64, axis=1)
    return [jnp.where(lo, kk, kr), jnp.where(lo, kr, kk)]


def _attn_mask(i):
    qi, sj = _iota((L, 2 * L), 0), _iota((L, 2 * L), 1)
    rel = qi + L - sj
    return (rel >= 0) & (rel < L) & ((sj >= L) | (i > 0))


SMEM = pl.BlockSpec(memory_space=pltpu.SMEM)


def _attn_fwd(qkv, bias, sinks):
    S = qkv.shape[0]
    nb = S // L
    scale = 64 ** -0.5

    def body(sink_ref, q_ref, kc_ref, vc_ref, kp_ref, vp_ref, bias_ref, o_ref, lse_ref):
        i = pl.program_id(0)
        lane = _iota((L, 128), 1)
        lo = lane < 64
        lo2 = _iota((2 * L, 128), 1) < 64
        mask = _attn_mask(i)
        kd = _band(kp_ref[...], kc_ref[...], lo2)
        vd = _band(vp_ref[...], vc_ref[...], lo2)
        lse = jnp.zeros((L, 128), F32)
        for pr in range(8):
            sl = slice(pr * 128, (pr + 1) * 128)
            qp = q_ref[:, sl]
            j = pr // 4
            outs = []
            for hh in range(2):
                h = 2 * pr + hh
                qm = jnp.where(lo if hh == 0 else jnp.logical_not(lo), qp, 0.0)
                lg = jnp.where(mask, _dot(qm, kd[j], NT) * scale + bias_ref[h], NEG_INF)
                s = sink_ref[h]
                m = jnp.maximum(jnp.max(lg, axis=1, keepdims=True), s)
                p = jnp.where(mask, jnp.exp(lg - m), 0.0)
                den = jnp.sum(p, axis=1, keepdims=True) + jnp.exp(s - m)
                outs.append(_dot(p / den, vd[j]))
                lse = jnp.where(lane == h, m + jnp.log(den), lse)
            o_ref[:, sl] = jnp.where(lo, outs[0], outs[1]).astype(o_ref.dtype)
        lse_ref[...] = lse

    prev = lambda col: pl.BlockSpec((L, 128), lambda i: (jnp.maximum(i - 1, 0), col))
    cur = lambda col: pl.BlockSpec((L, 128), lambda i: (i, col))
    return pl.pallas_call(
        body, name="attn_fwd", grid=(nb,),
        in_specs=[SMEM, pl.BlockSpec((L, 1024), lambda i: (i, 0)), cur(8), cur(9), prev(8), prev(9),
                  pl.BlockSpec((16, L, 2 * L), lambda i: (0, 0, 0))],
        out_specs=[pl.BlockSpec((L, 1024), lambda i: (i, 0)), pl.BlockSpec((L, 128), lambda i: (i, 0))],
        out_shape=[jax.ShapeDtypeStruct((S, 1024), BF16), jax.ShapeDtypeStruct((S, 128), F32)],
        compiler_params=_params(("parallel",)))(sinks, qkv, qkv, qkv, qkv, qkv, bias)


def _attn_bwd(qkv, d_o, lse, bias, sinks):
    S = qkv.shape[0]
    nb = S // L
    scale = 64 ** -0.5

    def body(sink_ref, q_ref, kc_ref, vc_ref, kp_ref, vp_ref, do_ref, lse_ref, bias_ref,
             dq_ref, dkv_ref, dbias_ref, dsink_ref, dbq_ref, dbkv_ref, carry_ref):
        i = pl.program_id(0)

        @pl.when(i == 0)
        def _():
            dbias_ref[...] = jnp.zeros_like(dbias_ref)
            dsink_ref[...] = jnp.zeros_like(dsink_ref)
            dbq_ref[...] = jnp.zeros_like(dbq_ref)
            dbkv_ref[...] = jnp.zeros_like(dbkv_ref)
            carry_ref[...] = jnp.zeros_like(carry_ref)

        @pl.when(i < nb)
        def _():
            lane = _iota((L, 128), 1)
            lane1 = _iota((1, 128), 1)
            lo = lane < 64
            lo2 = _iota((2 * L, 128), 1) < 64
            mask = _attn_mask(i)
            kd = _band(kp_ref[...], kc_ref[...], lo2)
            vd = _band(vp_ref[...], vc_ref[...], lo2)
            lse_all = lse_ref[...]
            acc_k = [jnp.zeros((2 * L, 128), F32), jnp.zeros((2 * L, 128), F32)]
            acc_v = [jnp.zeros((2 * L, 128), F32), jnp.zeros((2 * L, 128), F32)]
            dsink = jnp.zeros((1, 128), F32)
            for pr in range(8):
                sl = slice(pr * 128, (pr + 1) * 128)
                qp = q_ref[:, sl]
                dop = do_ref[:, sl]
                j = pr // 4
                dqs = []
                for hh in range(2):
                    h = 2 * pr + hh
                    half = lo if hh == 0 else jnp.logical_not(lo)
                    qm = jnp.where(half, qp, 0.0)
                    dom = jnp.where(half, dop, 0.0)
                    lse_h = _colsel(lse_all, lane, h)
                    lg = _dot(qm, kd[j], NT) * scale + bias_ref[h]
                    p = jnp.where(mask, jnp.exp(jnp.where(mask, lg, NEG_INF) - lse_h), 0.0)
                    dp = _dot(dom, vd[j], NT)
                    delta = jnp.sum(p * dp, axis=1, keepdims=True)
                    ds = p * (dp - delta)
                    dbias_ref[h] += ds
                    ds_sink = jnp.sum(-jnp.exp(sink_ref[h] - lse_h) * delta, axis=0, keepdims=True)
                    dsink = dsink + jnp.where(lane1 == h, ds_sink, 0.0)
                    dss = ds * scale
                    dqs.append(_dot(dss, kd[j]))
                    acc_k[j] = acc_k[j] + _dot(dss, qm, TN)
                    acc_v[j] = acc_v[j] + _dot(p, dom, TN)
                dq = jnp.where(lo, dqs[0], dqs[1])
                dq_ref[:, sl] = dq.astype(dq_ref.dtype)
                dbq_ref[:, sl] += jnp.sum(dq, axis=0, keepdims=True)
            dsink_ref[...] += dsink
            tot_k = [a + pltpu.roll(a, 64, axis=1) for a in acc_k]
            tot_v = [a + pltpu.roll(a, 64, axis=1) for a in acc_v]
            dkv = jnp.concatenate([jnp.where(lo2, tot_k[0], tot_k[1]), jnp.where(lo2, tot_v[0], tot_v[1])], axis=1)
            dbkv_ref[...] += jnp.sum(dkv, axis=0, keepdims=True)
            dkv_ref[...] = (carry_ref[...] + dkv[:L, :]).astype(dkv_ref.dtype)
            carry_ref[...] = dkv[L:, :]

        @pl.when(i == nb)
        def _():
            dkv_ref[...] = carry_ref[...].astype(dkv_ref.dtype)

    c = lambda i: jnp.minimum(i, nb - 1)
    prev = lambda col: pl.BlockSpec((L, 128), lambda i: (jnp.maximum(c(i) - 1, 0), col))
    cur = lambda col: pl.BlockSpec((L, 128), lambda i: (c(i), col))
    row = lambda w: pl.BlockSpec((L, w), lambda i: (c(i), 0))
    cube = pl.BlockSpec((16, L, 2 * L), lambda i: (0, 0, 0))
    vec = lambda w: pl.BlockSpec((1, w), lambda i: (0, 0))
    return pl.pallas_call(
        body, name="attn_bwd", grid=(nb + 1,),
        in_specs=[SMEM, row(1024), cur(8), cur(9), prev(8), prev(9), row(1024), row(128), cube],
        out_specs=[row(1024), pl.BlockSpec((L, 256), lambda i: (jnp.maximum(i - 1, 0), 0)), cube,
                   vec(128), vec(1024), vec(256)],
        out_shape=[jax.ShapeDtypeStruct((S, 1024), BF16), jax.ShapeDtypeStruct((S, 256), BF16),
                   jax.ShapeDtypeStruct((16, L, 2 * L), F32), jax.ShapeDtypeStruct((1, 128), F32),
                   jax.ShapeDtypeStruct((1, 1024), F32), jax.ShapeDtypeStruct((1, 256), F32)],
        scratch_shapes=[pltpu.VMEM((L, 256), F32)],
        compiler_params=_params(("arbitrary",)))(sinks, qkv, qkv, qkv, qkv, qkv, d_o, lse, bias)


def _pad_lanes(a, n=128):
    return jnp.pad(a, ((0, 0), (0, n - a.shape[1])))


def _local_step(x, tgt, mod, w_in, rest_of_weights, P, layer1_grads_ready):
    md = [[mod[l:l + 1, k * D:(k + 1) * D] for k in range(6)] for l in range(2)]
    G, g = {}, {}

    sh1, sc1, g1, sh2, sc2, g2 = md[0]
    nmw0, nfw0 = P["norm_mix_w"][0:1], P["norm_ffn_w"][0:1]
    h0 = _norm_mod_fwd("norm_mix_0", x, nmw0, sc1, sh1)
    segs = {"z": w_in[0:1024], "xbc": w_in[1024:2560], "dt": jnp.pad(w_in[2560:2576], ((0, 112), (0, 0))),
            "u": w_in[2576:3600], "v": w_in[3600:4624]}
    proj = {k: _mm(f"in_proj_{k}", [h0], [w], "nt", [F32])[0] for k, w in segs.items()}
    conv_w, conv_b = P["conv_w"][0], P["conv_b"]
    pre, xc = _conv_fwd(proj["xbc"], conv_w, conv_b)
    dtb, alog = _pad_lanes(P["dt_bias"]), _pad_lanes(P["a_log"])
    dskl = jnp.repeat(P["d_skip"], 64, axis=1)
    ya, y_ssd, prev = _ssd_fwd(xc, proj["dt"], proj["z"], dtb, alog, dskl, P["ssm_norm_w"])
    ws = P["gmlp_ws"][0]
    bse = jnp.broadcast_to(P["gmlp_bs"][0][:, :, None], (8, L, 128))
    yb = _gmlp_fwd(proj["u"], proj["v"], P["gmlp_ln_w"], P["gmlp_ln_b"], ws, bse)
    W = rest_of_weights(yb)
    w_oa, w_ob = W["out_w"][:1024], W["out_w"][1024:]

    def res(y, x, gate):
        return y, x + gate * y
    mix0, x1 = _mm("out_proj_0", [ya, yb], [w_oa, w_ob], "nn", [F32, F32], epi=res, extras=[x], vecs=[g1])
    h0f = _norm_mod_fwd("norm_ffn_0", x1, nfw0, sc2, sh2)
    a0, b0, f0, y0, x2 = _ffn_fwd("0", h0f, W["gate_wt"][0], W["up_wt"][0], W["down_w"][0], x1, g2)

    sh1b, sc1b, g1b, sh2b, sc2b, g2b = md[1]
    nmw1, nfw1 = P["norm_mix_w"][1:2], P["norm_ffn_w"][1:2]
    h1 = _norm_mod_fwd("norm_mix_1", x2, nmw1, sc1b, sh1b)
    qkv = _mm("qkv_proj", [h1], [W["qkv_wt"]], "nt", [F32], epi=lambda acc, b: acc + b, vecs=[P["qkv_b"]])[0]
    onehot_t = jnp.asarray(_bucket_onehot_t())
    bias = _rel_bias(P["rel_table"].T, onehot_t).reshape(16, L, 2 * L)
    sinks = P["sinks"].reshape(16)
    att, lse = _attn_fwd(qkv, bias, sinks)

    def res_b(y, x, gate, b):
        y = y + b
        return y, x + gate * y
    mix1, x3 = _mm("o_proj", [att], [W["o_w"]], "nn", [F32, F32], epi=res_b, extras=[x2], vecs=[g1b, P["o_b"]])
    h1f = _norm_mod_fwd("norm_ffn_1", x3, nfw1, sc2b, sh2b)
    a1, b1, f1, y1, x4 = _ffn_fwd("1", h1f, W["gate_wt"][1], W["up_wt"][1], W["down_w"][1], x3, g2b)

    dx, sq, g["final_norm_w"] = _loss_head(x4, tgt, P["final_norm_w"])

    dh, dg2b, dwg1, dwu1, dwd1 = _ffn_bwd("1", dx, h1f, a1, b1, f1, y1, W["gate_wt"][1], W["up_wt"][1],
                                          W["down_w"][1], g2b)
    dx, dsh2b, dsc2b, dnfw1 = _norm_mod_bwd("norm_ffn_bwd_1", x3, dh, dx, nfw1, sc2b)
    dmix, dg1b, g["o_b"] = _gate_bwd("mix_gate_bwd_1", dx, mix1, g1b)
    G["o_w"] = _mm_tn("o_dw", att, dmix)
    d_att = _mm("o_dx", [dmix], [W["o_w"]], "nt", [F32])[0]
    dq, dkv, dbias, dsinks, dbq, dbkv = _attn_bwd(qkv, d_att, lse, bias, sinks)
    g["rel_table"] = _rel_bias_bwd(dbias.reshape(16, L * 2 * L), onehot_t).T
    g["sinks"] = dsinks[:, :16]
    g["qkv_b"] = jnp.concatenate([dbq, dbkv], axis=1)
    w_q, w_kv = W["qkv_wt"][:1024], W["qkv_wt"][1024:]
    G["qkv_wt"] = jnp.concatenate([_mm_tn("qkv_dwq", dq, h1), _mm_tn("qkv_dwkv", dkv, h1)], axis=0)
    dh = _mm("qkv_dx", [dq, dkv], [w_q, w_kv], "nn", [F32])[0]
    dx, dsh1b, dsc1b, dnmw1 = _norm_mod_bwd("norm_mix_bwd_1", x2, dh, dx, nmw1, sc1b)
    behind = layer1_grads_ready({"qkv_wt": G.pop("qkv_wt"), "o_w": G.pop("o_w"), "gate_wt1": dwg1, "up_wt1": dwu1,
                                 "down_w1": dwd1})

    dh, dg2, dwg0, dwu0, dwd0 = _ffn_bwd("0", dx, h0f, a0, b0, f0, y0, W["gate_wt"][0], W["up_wt"][0],
                                         W["down_w"][0], g2, after=behind)
    dx, dsh2, dsc2, dnfw0 = _norm_mod_bwd("norm_ffn_bwd_0", x1, dh, dx, nfw0, sc2)
    dmix, dg1, _ = _gate_bwd("mix_gate_bwd_0", dx, mix0, g1)
    G["out_w"] = jnp.concatenate([_mm_tn("out_dwa", ya, dmix), _mm_tn("out_dwb", yb, dmix)], axis=0)
    dya = _mm("out_dxa", [dmix], [w_oa], "nt", [F32])[0]
    dyb = _mm("out_dxb", [dmix], [w_ob], "nt", [F32])[0]
    du, dv, dws, dbse, g["gmlp_ln_w"], g["gmlp_ln_b"] = _gmlp_bwd(dyb, proj["u"], proj["v"], P["gmlp_ln_w"],
                                                                 P["gmlp_ln_b"], ws, bse)
    g["gmlp_ws"] = dws[None]
    g["gmlp_bs"] = _lane_sum("gmlp_dbs", dbse.reshape(8 * L, 128)).reshape(1, 8, L)
    dz, dxc, ddt, g["ssm_norm_w"], ddsk, dalog, ddtb = _ssd_bwd(dya, y_ssd, proj["z"], xc, proj["dt"], prev,
                                                                dtb, alog, dskl, P["ssm_norm_w"])
    g["d_skip"], g["a_log"], g["dt_bias"] = ddsk[0:1, :16], dalog[:, :16], ddtb[:, :16]
    dxr, dconv_w, g["conv_b"] = _conv_bwd(dxc, pre, proj["xbc"], conv_w)
    g["conv_w"] = dconv_w[None]
    dsegs = {"z": dz, "xbc": dxr, "dt": ddt, "u": du, "v": dv}
    dws_in = {k: _mm_tn(f"in_dw_{k}", d, h0) for k, d in dsegs.items()}
    G["in_wt"] = jnp.concatenate([dws_in["z"], dws_in["xbc"], dws_in["dt"][:16], dws_in["u"], dws_in["v"]], axis=0)
    keys = ["z", "xbc", "dt", "u", "v"]
    dh = _mm("in_dx", [dsegs[k] for k in keys], [segs[k] for k in keys], "nn", [F32])[0]
    dx, dsh1, dsc1, dnmw0 = _norm_mod_bwd("norm_mix_bwd_0", x, dh, dx, nmw0, sc1)

    G["gate_wt0"], G["up_wt0"], G["down_w0"] = dwg0, dwu0, dwd0
    g["norm_mix_w"] = jnp.concatenate([dnmw0, dnmw1], axis=0)
    g["norm_ffn_w"] = jnp.concatenate([dnfw0, dnfw1], axis=0)
    dmod = jnp.concatenate([jnp.concatenate([dsh1, dsc1, dg1, dsh2, dsc2, dg2], axis=1),
                            jnp.concatenate([dsh1b, dsc1b, dg1b, dsh2b, dsc2b, dg2b], axis=1)], axis=0)
    return sq, dx, dmod, G, g


def _ada_fwd(c_all, ada_w, ada_b):
    n = ada_w.shape[2]
    tn = _col_tile(n, 512)

    def body(c_ref, w_ref, b_ref, o_ref):
        cc = c_ref[...]
        o_ref[...] = lax.dot_general(cc * _sigmoid(cc), w_ref[...], NN, precision=lax.Precision.HIGHEST,
                                     preferred_element_type=F32) + b_ref[...]

    return pl.pallas_call(
        body, name="ada_fwd", grid=(2, n // tn),
        in_specs=[pl.BlockSpec((8, D), lambda l, j: (0, 0)), pl.BlockSpec((None, D, tn), lambda l, j: (l, 0, j)),
                  pl.BlockSpec((None, 1, tn), lambda l, j: (l, 0, j))],
        out_specs=pl.BlockSpec((None, 8, tn), lambda l, j: (l, 0, j)),
        out_shape=jax.ShapeDtypeStruct((2, 8, n), F32), compiler_params=_params(("parallel", "parallel")))(
            c_all, ada_w, ada_b)


def _ada_bwd(c_all, dmod_cols, dmod_all):
    n = dmod_cols.shape[2]
    tn = _col_tile(n, 512)

    def body(c_ref, d_ref, o_ref):
        cc = c_ref[...]
        o_ref[...] = lax.dot_general(cc * _sigmoid(cc), d_ref[...], TN, precision=lax.Precision.HIGHEST,
                                     preferred_element_type=F32)

    dw = pl.pallas_call(
        body, name="ada_dw", grid=(2, n // tn),
        in_specs=[pl.BlockSpec((8, D), lambda l, j: (0, 0)), pl.BlockSpec((None, 8, tn), lambda l, j: (l, 0, j))],
        out_specs=pl.BlockSpec((None, D, tn), lambda l, j: (l, 0, j)),
        out_shape=jax.ShapeDtypeStruct((2, D, n), F32), compiler_params=_params(("parallel", "parallel")))(
            c_all, dmod_cols)

    def sum_body(d_ref, o_ref):
        o_ref[...] = jnp.sum(d_ref[...], axis=0, keepdims=True)

    db = pl.pallas_call(
        sum_body, name="ada_db", grid=(2,),
        in_specs=[pl.BlockSpec((None, 8, 6 * D), lambda l: (l, 0, 0))],
        out_specs=pl.BlockSpec((None, 1, 6 * D), lambda l: (l, 0, 0)),
        out_shape=jax.ShapeDtypeStruct((2, 1, 6 * D), F32), compiler_params=_params(("parallel",)))(dmod_all)
    return dw, db


def _row_tile(rows, cap=512, mult=8):
    best = rows
    for t in range(mult, min(rows, cap) + 1, mult):
        if rows % t == 0:
            best = t
    return best


def _adamw(name, w, g, m, v):
    def fn(w, g, m, v):
        m = ADAM_B1 * m + (1.0 - ADAM_B1) * g
        v = ADAM_B2 * v + (1.0 - ADAM_B2) * (g * g)
        m_hat = m / (1.0 - ADAM_B1 ** ADAM_STEP)
        v_hat = v / (1.0 - ADAM_B2 ** ADAM_STEP)
        return -ADAM_LR * (m_hat / (jnp.sqrt(v_hat) + ADAM_EPS) + ADAM_WD * w), m, v
    cols = w.shape[1]
    return _rowwise(name, fn, [w, g, m, v], [], [(cols, F32)] * 3, tr=_row_tile(w.shape[0]))


def _place():
    return lax.axis_index("x"), lax.axis_index("y"), lax.axis_index("c")


VMEM_SPEC = pl.BlockSpec(memory_space=pltpu.VMEM)


def _allreduce_small(name, buf, after=None):
    rows = buf.shape[0]
    deps = [] if after is None else [after]

    def body(x_ref, *rest):
        o_ref, stage, send_sems, recv_sems = rest[len(deps):]
        x, y, c = _place()
        me = 4 * x + 2 * y + c
        stage[me] = x_ref[...]
        copies = []
        for k in range(1, 8):
            peer = (1 - x if k & 4 else x, 1 - y if k & 2 else y, 1 - c if k & 1 else c)
            cp = pltpu.make_async_remote_copy(src_ref=x_ref, dst_ref=stage.at[me], send_sem=send_sems.at[k - 1],
                                              recv_sem=recv_sems.at[k - 1], device_id=peer, device_id_type=MESH)
            cp.start()
            copies.append(cp)
        for cp in copies:
            cp.wait()
        acc = stage[0]
        for d in range(1, 8):
            acc = acc + stage[d]
        o_ref[...] = acc

    return pl.pallas_call(
        body, name=name, in_specs=[VMEM_SPEC] + [ANY for _ in deps], out_specs=VMEM_SPEC,
        out_shape=jax.ShapeDtypeStruct((rows, 128), F32),
        scratch_shapes=[pltpu.VMEM((8, rows, 128), F32), pltpu.SemaphoreType.DMA((7,)), pltpu.SemaphoreType.DMA((7,))],
        compiler_params=pltpu.CompilerParams(vmem_limit_bytes=_VMEM_LIMIT))(buf, *deps)


OTHER_CHIPS = ((1, 0), (0, 1), (1, 1))


def _allgather_big(wp):
    rows = wp.shape[0]
    half = rows // 2

    def body(w_ref, o_ref, send_sems, recv_sems, local_sem):
        x, y, c = _place()
        k = 2 * x + y
        mine = pl.ds(pl.multiple_of(c * half, 8), half)
        theirs = pl.ds(pl.multiple_of((1 - c) * half, 8), half)
        local = pltpu.make_async_copy(w_ref, o_ref.at[k], local_sem)
        local.start()
        chips = [(1 - x if fx else x, 1 - y if fy else y) for fx, fy in OTHER_CHIPS]
        idx = [2 * px + py for px, py in chips]
        first = []
        for j, (px, py) in enumerate(chips):
            cp = pltpu.make_async_remote_copy(src_ref=w_ref.at[mine], dst_ref=o_ref.at[k, mine],
                                              send_sem=send_sems.at[j], recv_sem=recv_sems.at[j],
                                              device_id=(px, py, c), device_id_type=MESH)
            cp.start()
            first.append(cp)
        passed = []
        for j in range(3):
            blk = o_ref.at[idx[j], mine]
            pltpu.make_async_remote_copy(src_ref=blk, dst_ref=blk, send_sem=send_sems.at[j], recv_sem=recv_sems.at[j],
                                         device_id=(x, y, c), device_id_type=MESH).wait_recv()
            cp = pltpu.make_async_remote_copy(src_ref=blk, dst_ref=blk, send_sem=send_sems.at[3 + j],
                                              recv_sem=recv_sems.at[3 + j], device_id=(x, y, 1 - c),
                                              device_id_type=MESH)
            cp.start()
            passed.append(cp)
        for j in range(3):
            blk = o_ref.at[idx[j], theirs]
            pltpu.make_async_remote_copy(src_ref=blk, dst_ref=blk, send_sem=send_sems.at[3 + j],
                                         recv_sem=recv_sems.at[3 + j], device_id=(x, y, c),
                                         device_id_type=MESH).wait_recv()
        for cp in first + passed:
            cp.wait_send()
        local.wait()

    return pl.pallas_call(
        body, name="allgather_weights", in_specs=[ANY], out_specs=ANY,
        out_shape=jax.ShapeDtypeStruct((4, rows, 1024), wp.dtype),
        scratch_shapes=[pltpu.SemaphoreType.DMA((6,)), pltpu.SemaphoreType.DMA((6,)), pltpu.SemaphoreType.DMA])(wp)


def _sibling_swap(name, src, halves):
    half = src.shape[-2] // 2
    out_shape = (src.shape[0], half, 1024) if halves else src.shape

    def body(s_ref, o_ref, send_sem, recv_sem):
        x, y, c = _place()
        part = s_ref.at[:, pl.ds(pl.multiple_of((1 - c) * half, 8), half)] if halves else s_ref
        cp = pltpu.make_async_remote_copy(src_ref=part, dst_ref=o_ref, send_sem=send_sem, recv_sem=recv_sem,
                                          device_id=(x, y, 1 - c), device_id_type=MESH)
        cp.start()
        cp.wait()

    return pl.pallas_call(
        body, name=name, in_specs=[ANY], out_specs=ANY, out_shape=jax.ShapeDtypeStruct(out_shape, src.dtype),
        scratch_shapes=[pltpu.SemaphoreType.DMA, pltpu.SemaphoreType.DMA])(src)


HBM = pl.BlockSpec(memory_space=pltpu.HBM)
SEM = pl.BlockSpec(memory_space=pltpu.SEMAPHORE)


def _chip_copies(mode, src_ref, land_ref, send_sems, recv_sems):
    x, y, c = _place()
    k = 2 * x + y
    copies = []
    for j, (fx, fy) in enumerate(OTHER_CHIPS):
        px, py = (1 - x if fx else x), (1 - y if fy else y)
        if mode == "gather":
            half = src_ref.shape[0] // 2
            mine = pl.ds(pl.multiple_of(c * half, 16), half)
            src, dst = src_ref.at[mine], land_ref.at[k, mine]
        else:
            src, dst = src_ref.at[2 * px + py], land_ref.at[k]
        copies.append(pltpu.make_async_remote_copy(src_ref=src, dst_ref=dst, send_sem=send_sems.at[j],
                                                   recv_sem=recv_sems.at[j], device_id=(px, py, c),
                                                   device_id_type=MESH))
    return copies


def _exchange_start(name, collective_id, mode, src, land):
    def body(s_ref, l_ref, send_sems, recv_sems, s_thru, l_thru, token):
        x, y, c = _place()
        barrier = pltpu.get_barrier_semaphore()
        for fx, fy in OTHER_CHIPS:
            pl.semaphore_signal(barrier, inc=1, device_id=(1 - x if fx else x, 1 - y if fy else y, c),
                                device_id_type=MESH)
        pl.semaphore_wait(barrier, 3)
        for cp in _chip_copies(mode, s_ref, l_ref, send_sems, recv_sems):
            cp.start()
        token[...] = jnp.zeros_like(token)

    return pl.pallas_call(
        body, name=name,
        out_shape=(pltpu.SemaphoreType.DMA((3,)), pltpu.SemaphoreType.DMA((3,)), pltpu.HBM(src.shape, src.dtype),
                   pltpu.HBM(land.shape, land.dtype), jax.ShapeDtypeStruct((8, 128), F32)),
        in_specs=(HBM, HBM), out_specs=(SEM, SEM, HBM, HBM, VMEM_SPEC), input_output_aliases={0: 2, 1: 3},
        compiler_params=pltpu.CompilerParams(has_side_effects=pltpu.SideEffectType.DATAFLOW_SIDE_EFFECTING,
                                             collective_id=collective_id))(
            pltpu.with_memory_space_constraint(src, pltpu.HBM), pltpu.with_memory_space_constraint(land, pltpu.HBM))


def _exchange_wait(name, mode, started, after):
    send_sems, recv_sems, s_thru, l_thru, _ = started

    def body(s_ref, l_ref, send_sems, recv_sems, after_ref, s_out, l_out):
        for cp in _chip_copies(mode, s_ref, l_ref, send_sems, recv_sems):
            cp.wait_send()
            cp.wait_recv()

    return pl.pallas_call(
        body, name=name, out_shape=(pltpu.HBM(s_thru.shape, s_thru.dtype), pltpu.HBM(l_thru.shape, l_thru.dtype)),
        in_specs=(HBM, HBM, SEM, SEM, ANY), out_specs=(HBM, HBM), input_output_aliases={0: 0, 1: 1},
        compiler_params=pltpu.CompilerParams(has_side_effects=pltpu.SideEffectType.DATAFLOW_SIDE_EFFECTING))(
            s_thru, l_thru, send_sems, recv_sems, after)


def _allgather_finish(share, land):
    rows = share.shape[0]
    half = rows // 2

    def body(w_ref, l_ref, o_ref, send_sems, recv_sems, local_sem):
        x, y, c = _place()
        k = 2 * x + y
        mine = pl.ds(pl.multiple_of(c * half, 16), half)
        theirs = pl.ds(pl.multiple_of((1 - c) * half, 16), half)
        local = pltpu.make_async_copy(w_ref, o_ref.at[k], local_sem)
        local.start()
        passed, landed = [], []
        for j, (fx, fy) in enumerate(OTHER_CHIPS):
            idx = 2 * (1 - x if fx else x) + (1 - y if fy else y)
            passed.append(pltpu.make_async_remote_copy(
                src_ref=o_ref.at[idx, mine], dst_ref=o_ref.at[idx, mine], send_sem=send_sems.at[j],
                recv_sem=recv_sems.at[j], device_id=(x, y, 1 - c), device_id_type=MESH))
            landed.append(pltpu.make_async_remote_copy(
                src_ref=o_ref.at[idx, theirs], dst_ref=o_ref.at[idx, theirs], send_sem=send_sems.at[j],
                recv_sem=recv_sems.at[j], device_id=(x, y, 1 - c), device_id_type=MESH))
        for cp in passed:
            cp.start()
        for cp in landed:
            cp.wait_recv()
        for cp in passed:
            cp.wait_send()
        local.wait()

    return pl.pallas_call(
        body, name="allgather_finish", in_specs=[ANY, ANY], out_specs=ANY, input_output_aliases={1: 0},
        out_shape=jax.ShapeDtypeStruct(land.shape, land.dtype),
        scratch_shapes=[pltpu.SemaphoreType.DMA((3,)), pltpu.SemaphoreType.DMA((3,)), pltpu.SemaphoreType.DMA])(
            share, land)


def _pair_sum(tag, g, r1, c):
    rows = g.shape[1]
    half = rows // 2
    th = _row_tile(half, 256, 16)
    nblk = half // th

    def body(c_ref, g_ref, r_ref, o_ref, o2_ref):
        o_ref[...] = (g_ref[...] + r_ref[...]).astype(o_ref.dtype)
        o2_ref[...] = o_ref[...]

    spec = pl.BlockSpec((None, th, 1024), lambda k, i, c_ref: (k, i, 0))
    grid_spec = pltpu.PrefetchScalarGridSpec(
        num_scalar_prefetch=1, grid=(4, nblk),
        in_specs=[pl.BlockSpec((None, th, 1024), lambda k, i, c_ref: (k, c_ref[0] * nblk + i, 0)), spec],
        out_specs=[spec, spec])
    return pl.pallas_call(body, name="grad_pair_sum_" + tag, grid_spec=grid_spec,
                          out_shape=[jax.ShapeDtypeStruct((4, half, 1024), BF16)] * 2,
                          compiler_params=_params(("parallel", "parallel")))(c, g, r1)


def _chip_sum(tag, q, after=None):
    half = q.shape[1]
    th = _row_tile(half, 256, 16)
    deps = [] if after is None else [after]

    def body(a, b, c, d, *rest):
        rest[-1][...] = ((a[...].astype(F32) + b[...].astype(F32)) + c[...].astype(F32)) + d[...].astype(F32)

    specs = [pl.BlockSpec((None, th, 1024), functools.partial(lambda i, k: (k, i, 0), k=k)) for k in range(4)]
    return pl.pallas_call(body, name="grad_chip_sum_" + tag, grid=(half // th,), in_specs=specs + [ANY for _ in deps],
                          out_specs=pl.BlockSpec((th, 1024), lambda i: (i, 0)),
                          out_shape=jax.ShapeDtypeStruct((half, 1024), F32),
                          compiler_params=_params(("parallel",)))(q, q, q, q, *deps)


def _join_halves(tag, f, r, c):
    half = f.shape[0]
    th = _row_tile(half, 256)
    nblk = half // th

    def body(c_ref, f_ref, r_ref, o_ref):
        mine = (pl.program_id(0) == c_ref[0])
        o_ref[...] = jnp.where(mine, f_ref[...], r_ref[...])

    spec = pl.BlockSpec((th, 1024), lambda h, i, c_ref: (i, 0))
    grid_spec = pltpu.PrefetchScalarGridSpec(
        num_scalar_prefetch=1, grid=(2, nblk), in_specs=[spec, spec],
        out_specs=pl.BlockSpec((th, 1024), lambda h, i, c_ref: (h * nblk + i, 0)))
    return pl.pallas_call(body, name="grad_join_halves_" + tag, grid_spec=grid_spec,
                          out_shape=jax.ShapeDtypeStruct((2 * half, 1024), F32),
                          compiler_params=_params(("parallel", "parallel")))(c, f, r)


BIG_ARGS = ("in_w_even", "out_w_even", "qkv_w", "o_w", "ffn_gate_w", "ffn_up_w", "ffn_down_w")
def _ffn_pieces(layer):
    return tuple((f"{n}{layer}", 704, 704) for n in ("gate_wt", "up_wt", "down_w"))


IN_SLAB = (("in_wt", 1156, 1184),)
REST_SLAB = (("out_w", 512, 512), ("qkv_wt", 320, 320), ("o_w", 256, 256)) + _ffn_pieces(0) + _ffn_pieces(1)
LAYER1_SLAB = (("qkv_wt", 320, 320), ("o_w", 256, 256)) + _ffn_pieces(1)
LAYER0_SLAB = (("in_wt", 1156, 1184), ("out_w", 512, 512)) + _ffn_pieces(0)


def _slab(pieces, spec):
    parts = []
    for name, rows, room in spec:
        p = pieces[name]
        parts.append(jnp.pad(p, [(0, 0)] * (p.ndim - 2) + [(0, room - rows), (0, 0)]) if room > rows else p)
    return jnp.concatenate(parts, axis=-2) if len(parts) > 1 else parts[0]


def _unslab(slab, spec):
    out, off = {}, 0
    for name, rows, room in spec:
        out[name] = slab[..., off:off + rows, :]
        off += room
    return out


def _share_pieces(w):
    return {"in_wt": w["in_w_even"][0].T, "out_w": w["out_w_even"][0], "qkv_wt": w["qkv_w"][0].T, "o_w": w["o_w"][0],
            "gate_wt0": w["ffn_gate_w"][0].T, "gate_wt1": w["ffn_gate_w"][1].T,
            "up_wt0": w["ffn_up_w"][0].T, "up_wt1": w["ffn_up_w"][1].T,
            "down_w0": w["ffn_down_w"][0], "down_w1": w["ffn_down_w"][1]}


def _pieces_to_shares(p):
    return {"in_w_even": p["in_wt"].T[None], "out_w_even": p["out_w"][None], "qkv_w": p["qkv_wt"].T[None],
            "o_w": p["o_w"][None], "ffn_gate_w": jnp.stack([p["gate_wt0"].T, p["gate_wt1"].T]),
            "ffn_up_w": jnp.stack([p["up_wt0"].T, p["up_wt1"].T]),
            "ffn_down_w": jnp.stack([p["down_w0"], p["down_w1"]])}


def _rest_from_chips(p):
    whole = {k: v.reshape(-1, D) for k, v in p.items()}
    return {"out_w": whole["out_w"], "qkv_wt": whole["qkv_wt"], "o_w": whole["o_w"],
            "gate_wt": [whole["gate_wt0"], whole["gate_wt1"]], "up_wt": [whole["up_wt0"], whole["up_wt1"]],
            "down_w": [whole["down_w0"], whole["down_w1"]]}


def _chips_from_full(G, spec):
    return _slab({k: v.reshape(4, -1, D) for k, v in G.items()}, spec)


def _pack_small(parts):
    padded = []
    for p in parts:
        p = p.reshape(-1).astype(F32)
        padded.append(jnp.pad(p, (0, (-p.shape[0]) % 1024)))
    return jnp.concatenate(padded).reshape(-1, 128)


def _unpack_small(slab, shapes):
    flat, out, off = slab.reshape(-1), [], 0
    for shp in shapes:
        size = math.prod(shp)
        out.append(flat[off:off + size].reshape(shp))
        off += size + (-size) % 1024
    return out


SMALL = ("ada_b", "norm_mix_w", "norm_ffn_w", "conv_w", "conv_b", "dt_bias", "a_log", "d_skip", "ssm_norm_w",
         "gmlp_ln_w", "gmlp_ln_b", "gmlp_ws", "gmlp_bs", "qkv_b", "o_b", "sinks", "rel_table", "final_norm_w")
SMALL_SPLIT = {"conv_w": 1536, "qkv_b": 1280, "o_b": 1024}
WEIGHTS = ("ada_w", "ada_b", "norm_mix_w", "norm_ffn_w", "in_w_even", "conv_w", "conv_b", "dt_bias", "a_log", "d_skip",
           "ssm_norm_w", "gmlp_ln_w", "gmlp_ln_b", "gmlp_ws", "gmlp_bs", "out_w_even", "qkv_w", "qkv_b", "o_w", "o_b",
           "sinks", "rel_table", "ffn_gate_w", "ffn_up_w", "ffn_down_w", "final_norm_w")


def kernel(x, c, ada_w, ada_b, norm_mix_w, norm_ffn_w, in_w_even, conv_w, conv_b, dt_bias, a_log, d_skip, ssm_norm_w, gmlp_ln_w, gmlp_ln_b, gmlp_ws, gmlp_bs, out_w_even, qkv_w, qkv_b, o_w, o_b, sinks, rel_table, ffn_gate_w, ffn_up_w, ffn_down_w, final_norm_w, loss_target, m_ada_w, m_ada_b, m_norm_mix_w, m_norm_ffn_w, m_in_w_even, m_conv_w, m_conv_b, m_dt_bias, m_a_log, m_d_skip, m_ssm_norm_w, m_gmlp_ln_w, m_gmlp_ln_b, m_gmlp_ws, m_gmlp_bs, m_out_w_even, m_qkv_w, m_qkv_b, m_o_w, m_o_b, m_sinks, m_rel_table, m_ffn_gate_w, m_ffn_up_w, m_ffn_down_w, m_final_norm_w, v_ada_w, v_ada_b, v_norm_mix_w, v_norm_ffn_w, v_in_w_even, v_conv_w, v_conv_b, v_dt_bias, v_a_log, v_d_skip, v_ssm_norm_w, v_gmlp_ln_w, v_gmlp_ln_b, v_gmlp_ws, v_gmlp_bs, v_out_w_even, v_qkv_w, v_qkv_b, v_o_w, v_o_b, v_sinks, v_rel_table, v_ffn_gate_w, v_ffn_up_w, v_ffn_down_w, v_final_norm_w):
    args = dict(locals())
    w = {n: args[n] for n in WEIGHTS}
    m = {n: args["m_" + n] for n in WEIGHTS}
    v = {n: args["v_" + n] for n in WEIGHTS}
    ax, ay, ac = _place()
    me = 4 * ax + 2 * ay + ac
    chip = 2 * ax + ay
    south = (ac == 0).astype(F32)
    c_arr = jnp.reshape(ac, (1,)).astype(jnp.int32)

    pieces = {k: p.astype(_MXU) for k, p in _share_pieces(w).items()}
    w_in = _unslab(_allgather_big(_slab(pieces, IN_SLAB)), IN_SLAB)["in_wt"].reshape(4 * 1156, D)
    rest_share = _slab(pieces, REST_SLAB)
    rest_started = _exchange_start("allgather_rest_start", 1, "gather", rest_share,
                                   lax.empty((4,) + rest_share.shape, rest_share.dtype))

    def rest_of_weights(after):
        share, land = _exchange_wait("allgather_rest_wait", "gather", rest_started, after)
        return _rest_from_chips(_unslab(_allgather_finish(share, land), REST_SLAB))

    c_all = _allreduce_small("gather_cond", lax.dynamic_update_slice(jnp.zeros((8, D), F32), c, (me, 0)).reshape(64, 128),
                             after=rest_started[4])
    c_all = c_all.reshape(8, D)
    n_ada = ada_w.shape[2]
    mod_cols = _ada_fwd(c_all, ada_w, lax.dynamic_slice(ada_b, (0, chip * n_ada), (2, n_ada)).reshape(2, 1, n_ada))
    pieces = [lax.dynamic_update_slice(jnp.zeros((2, 8, 6 * D), F32), mod_cols, (0, 0, chip * n_ada))]
    split_names = list(SMALL_SPLIT)
    for n in split_names:
        full = SMALL_SPLIT[n]
        local = w[n]
        idx = (0,) * (local.ndim - 1) + (chip * local.shape[-1],)
        pieces.append(lax.dynamic_update_slice(jnp.zeros(local.shape[:-1] + (full,), F32), local, idx))
    shapes = [p.shape for p in pieces]
    gathered = _unpack_small(_allreduce_small("gather_mod", _pack_small(pieces) * south), shapes)
    mod = lax.dynamic_slice(gathered[0], (0, me, 0), (2, 1, 6 * D)).reshape(2, 6 * D)
    P = {n: w[n] for n in SMALL if n not in SMALL_SPLIT and n != "ada_b"}
    for n, full in zip(split_names, gathered[1:]):
        P[n] = full
    P["final_norm_w"] = final_norm_w.reshape(1, D)

    def start_reduce(tag, collective_id, G, spec):
        gp = _chips_from_full(G, spec)
        p, q = _pair_sum(tag, gp, _sibling_swap("grad_pair_exchange_" + tag, gp, True), c_arr)
        return _exchange_start("grad_exchange_start_" + tag, collective_id, "scatter", p, q)

    def finish_reduce(tag, started, spec, after, behind=None):
        q = _exchange_wait("grad_exchange_wait_" + tag, "scatter", started, after)[1]
        fin = _chip_sum(tag, q, after=behind)
        total = _join_halves(tag, fin, _sibling_swap("grad_final_exchange_" + tag, fin, False), c_arr)
        return _unslab(total, spec)

    layer1 = {}

    def layer1_grads_ready(G1):
        layer1["started"] = start_reduce("1", 2, G1, LAYER1_SLAB)
        return layer1["started"][4]

    sq, grad_x, dmod, G0, g = _local_step(x[0], loss_target[0], mod, w_in, rest_of_weights, P, layer1_grads_ready)
    loss = lax.psum(0.5 * sq[0, 0] / D, ("x", "y", "c"))
    layer0_started = start_reduce("0", 3, G0, LAYER0_SLAB)
    shares = finish_reduce("1", layer1["started"], LAYER1_SLAB, grad_x, behind=layer0_started[4])

    g["final_norm_w"] = g["final_norm_w"].reshape(D)
    small_names = [n for n in SMALL if n != "ada_b"]
    pieces = [lax.dynamic_update_slice(jnp.zeros((2, 8, 6 * D), F32), dmod.reshape(2, 1, 6 * D), (0, me, 0))]
    pieces += [g[n] for n in small_names]
    shapes = [p.shape for p in pieces]
    reduced = _unpack_small(_allreduce_small("allreduce_small_grads", _pack_small(pieces), after=layer0_started[4]),
                            shapes)
    dmod_all = reduced[0]
    grads = dict(zip(small_names, reduced[1:]))
    for n in split_names:
        full = grads[n]
        size = w[n].shape[-1]
        grads[n] = lax.dynamic_slice(full, (0,) * (full.ndim - 1) + (chip * size,), full.shape[:-1] + (size,))
    grads = {n: grads[n].reshape(w[n].shape) for n in small_names}
    dw_ada, db_ada = _ada_bwd(c_all, lax.dynamic_slice(dmod_all, (0, 0, chip * n_ada), (2, 8, n_ada)), dmod_all)
    grads["ada_w"], grads["ada_b"] = dw_ada, db_ada.reshape(2, 6 * D)

    delta, new_m, new_v = {}, {}, {}

    def update(n):
        cols = w[n].shape[-1]
        d_, m_, v_ = _adamw("adamw_" + n, w[n].reshape(-1, cols), grads[n].reshape(-1, cols), m[n].reshape(-1, cols),
                            v[n].reshape(-1, cols))
        delta[n], new_m[n], new_v[n] = d_.reshape(w[n].shape), m_.reshape(w[n].shape), v_.reshape(w[n].shape)

    update("ada_w")
    shapes = [w[n].shape for n in SMALL]
    packed = [_pack_small([t[n] for n in SMALL]) for t in (w, grads, m, v)]
    outs = _adamw("adamw_small", *packed)
    for dst, slab in zip((delta, new_m, new_v), outs):
        for n, t in zip(SMALL, _unpack_small(slab, shapes)):
            dst[n] = t
    shares.update(finish_reduce("0", layer0_started, LAYER0_SLAB, outs[0]))
    grads.update(_pieces_to_shares(shares))
    for n in BIG_ARGS:
        update(n)
    return (loss, grad_x[None], *[grads[n] for n in WEIGHTS], *[delta[n] for n in WEIGHTS],
            *[new_m[n] for n in WEIGHTS], *[new_v[n] for n in WEIGHTS])
```

```python
import functools
import math

import numpy as np
import jax
import jax.numpy as jnp
from jax import lax
from jax.experimental import pallas as pl
from jax.experimental.pallas import tpu as pltpu

F32 = jnp.float32
BF16 = jnp.bfloat16
_MXU = jnp.bfloat16
_VMEM_LIMIT = 56 * 1024 * 1024
D = 1024
L = 128
NSTATE = 128
EPS = 1e-6
NEG_INF = -1e30
FFN = 2816
ADAM_LR, ADAM_B1, ADAM_B2, ADAM_EPS, ADAM_WD, ADAM_STEP = 0.001, 0.9, 0.999, 1e-08, 0.01, 10
MESH = pl.DeviceIdType.MESH
ANY = pl.BlockSpec(memory_space=pl.ANY)

NN = (((1,), (0,)), ((), ()))
NT = (((1,), (1,)), ((), ()))
TN = (((0,), (0,)), ((), ()))


def _dot(a, b, dn=NN):
    return lax.dot_general(a.astype(_MXU), b.astype(_MXU), dn, preferred_element_type=F32)


def _params(sem=None):
    return pltpu.CompilerParams(dimension_semantics=sem, vmem_limit_bytes=_VMEM_LIMIT)


def _sigmoid(x):
    return 1.0 / (1.0 + jnp.exp(-x))


def _softplus(x):
    return jnp.maximum(x, 0.0) + jnp.log(1.0 + jnp.exp(-jnp.abs(x)))


def _gelu(x):
    return 0.5 * x * (1.0 + lax.erf(x * (2.0 ** -0.5)))


def _gelu_grad(x):
    return 0.5 * (1.0 + lax.erf(x * (2.0 ** -0.5))) + x * jnp.exp(-0.5 * x * x) * (1.0 / math.sqrt(2.0 * math.pi))


def _silu_grad(a):
    sg = _sigmoid(a)
    return sg * (1.0 + a * (1.0 - sg))


def _rowwise(name, fn, rows, vecs, out_rows, out_accs=(), tr=512, after=None):
    S = rows[0].shape[0]
    tr = min(tr, S)
    assert S % tr == 0
    nr, nv, no, na = len(rows), len(vecs), len(out_rows), len(out_accs)
    deps = [] if after is None else [after]

    def body(*refs):
        ins, outs = refs[:nr + nv], refs[nr + nv + len(deps):]
        res = fn(*[r[...] for r in ins])
        if not isinstance(res, (tuple, list)):
            res = (res,)
        for k in range(no):
            outs[k][...] = res[k].astype(outs[k].dtype)
        if na:
            @pl.when(pl.program_id(0) == 0)
            def _():
                for k in range(na):
                    outs[no + k][...] = jnp.zeros_like(outs[no + k])
            for k in range(na):
                outs[no + k][...] += res[no + k]

    in_specs = [pl.BlockSpec((tr, a.shape[1]), lambda i: (i, 0)) for a in rows]
    in_specs += [pl.BlockSpec(v.shape, lambda i: (0, 0)) for v in vecs] + [ANY for _ in deps]
    out_specs = [pl.BlockSpec((tr, c), lambda i: (i, 0)) for c, _ in out_rows]
    out_specs += [pl.BlockSpec(s, lambda i: (0, 0)) for s in out_accs]
    out_shape = [jax.ShapeDtypeStruct((S, c), dt) for c, dt in out_rows]
    out_shape += [jax.ShapeDtypeStruct(s, F32) for s in out_accs]
    return pl.pallas_call(body, name=name, grid=(S // tr,), in_specs=in_specs, out_specs=out_specs,
                          out_shape=out_shape, compiler_params=_params(("arbitrary",)))(*rows, *vecs, *deps)


def _col_tile(n, cap):
    if n <= cap or n % 128:
        return n
    best = 128
    for t in range(128, cap + 1, 128):
        if n % t == 0:
            best = t
    return best


def _mm(name, As, Bs, mode, outs, epi=None, groups=None, extras=(), vecs=(), tm=512, tn_cap=1536):
    M = As[0].shape[0]
    N = Bs[0].shape[1] if mode == "nn" else Bs[0].shape[0]
    tm = min(tm, M)
    tn = _col_tile(N, tn_cap)
    assert M % tm == 0 and N % tn == 0
    npair = len(As)
    groups = groups or [0] * npair
    ng = max(groups) + 1
    nx, nv = len(extras), len(vecs)
    dn = NN if mode == "nn" else NT

    def body(*refs):
        a_refs, b_refs = refs[:npair], refs[npair:2 * npair]
        x_refs = refs[2 * npair:2 * npair + nx]
        v_refs = refs[2 * npair + nx:2 * npair + nx + nv]
        o_refs = refs[2 * npair + nx + nv:]
        accs = [None] * ng
        for k in range(npair):
            d = _dot(a_refs[k][...], b_refs[k][...], dn)
            accs[groups[k]] = d if accs[groups[k]] is None else accs[groups[k]] + d
        args = accs + [x[...] for x in x_refs] + [v[...] for v in v_refs]
        res = epi(*args) if epi is not None else tuple(accs)
        if not isinstance(res, (tuple, list)):
            res = (res,)
        for o, r in zip(o_refs, res):
            o[...] = r.astype(o.dtype)

    in_specs = [pl.BlockSpec((tm, a.shape[1]), lambda i, j: (i, 0)) for a in As]
    if mode == "nn":
        in_specs += [pl.BlockSpec((b.shape[0], tn), lambda i, j: (0, j)) for b in Bs]
    else:
        in_specs += [pl.BlockSpec((tn, b.shape[1]), lambda i, j: (j, 0)) for b in Bs]
    in_specs += [pl.BlockSpec((tm, tn), lambda i, j: (i, j)) for _ in extras]
    in_specs += [pl.BlockSpec((1, tn), lambda i, j: (0, j)) for _ in vecs]
    out_specs = [pl.BlockSpec((tm, tn), lambda i, j: (i, j)) for _ in outs]
    out_shape = [jax.ShapeDtypeStruct((M, N), dt) for dt in outs]
    return pl.pallas_call(body, name=name, grid=(M // tm, N // tn), in_specs=in_specs, out_specs=out_specs,
                          out_shape=out_shape, compiler_params=_params(("parallel", "parallel")))(
                              *As, *Bs, *extras, *vecs)


def _mm_tn(name, A, B, tk=512, t2_cap=1536):
    S, K1 = A.shape
    N2 = B.shape[1]
    tk = min(tk, S)
    t2 = _col_tile(N2, t2_cap)
    assert S % tk == 0 and N2 % t2 == 0

    def body(a_ref, b_ref, o_ref):
        @pl.when(pl.program_id(1) == 0)
        def _():
            o_ref[...] = jnp.zeros_like(o_ref)
        o_ref[...] += _dot(a_ref[...], b_ref[...], TN)

    return pl.pallas_call(
        body, name=name, grid=(N2 // t2, S // tk),
        in_specs=[pl.BlockSpec((tk, K1), lambda j, k: (k, 0)), pl.BlockSpec((tk, t2), lambda j, k: (k, j))],
        out_specs=pl.BlockSpec((K1, t2), lambda j, k: (0, j)),
        out_shape=jax.ShapeDtypeStruct((K1, N2), F32),
        compiler_params=_params(("parallel", "arbitrary")))(A, B)


def _norm_mod_fwd(name, x, nw, sc, sh, after=None):
    def fn(x, nw, sc, sh):
        rstd = lax.rsqrt(jnp.mean(x * x, axis=-1, keepdims=True) + EPS)
        return (x * rstd * nw) * (1.0 + sc) + sh
    return _rowwise(name, fn, [x], [nw, sc, sh], [(D, BF16)], after=after)[0]


def _norm_mod_bwd(name, x, dh, dres, nw, sc, after=None):
    def fn(x, dh, dres, nw, sc):
        rstd = lax.rsqrt(jnp.mean(x * x, axis=-1, keepdims=True) + EPS)
        xh = x * rstd
        dn = dh * (1.0 + sc)
        dxh = dn * nw
        dx = rstd * (dxh - xh * jnp.mean(dxh * xh, axis=-1, keepdims=True))
        return (dres + dx, jnp.sum(dh, axis=0, keepdims=True), jnp.sum(dh * (xh * nw), axis=0, keepdims=True),
                jnp.sum(dn * xh, axis=0, keepdims=True))
    return _rowwise(name, fn, [x, dh, dres], [nw, sc], [(D, F32)], [(1, D)] * 3, after=after)


def _gate_bwd(name, dx, y, g, after=None):
    def fn(dx, y, g):
        dy = dx * g
        return dy, jnp.sum(dx * y, axis=0, keepdims=True), jnp.sum(dy, axis=0, keepdims=True)
    return _rowwise(name, fn, [dx, y], [g], [(D, BF16)], [(1, D)] * 2, after=after)


def _loss_head(x, tgt, fw):
    def fn(x, tgt, fw):
        rstd = lax.rsqrt(jnp.mean(x * x, axis=-1, keepdims=True) + EPS)
        xh = x * rstd
        err = xh * fw - tgt
        dout = err * (1.0 / D)
        dxh = dout * fw
        dx = rstd * (dxh - xh * jnp.mean(dxh * xh, axis=-1, keepdims=True))
        sq = jnp.sum(jnp.sum(err * err, axis=1, keepdims=True), axis=0, keepdims=True)
        return dx, sq, jnp.sum(dout * xh, axis=0, keepdims=True)
    return _rowwise("loss_head", fn, [x, tgt], [fw], [(D, F32)], [(1, 1), (1, D)])


def _ffn_fwd(tag, h, wg, wu, wd, x, g2):
    def act(a, b):
        return a, b, a * _sigmoid(a) * b
    a, b, f = _mm(f"ffn_up_{tag}", [h, h], [wg, wu], "nt", [F32, F32, BF16], epi=act, groups=[0, 1], tn_cap=1408)

    def res(y, x, g):
        return y, x + g * y
    y, xo = _mm(f"ffn_down_{tag}", [f], [wd], "nn", [F32, F32], epi=res, extras=[x], vecs=[g2])
    return a, b, f, y, xo


def _ffn_bwd(tag, dx, h, a, b, f, y, wg, wu, wd, g2, after=None):
    dy, dg2, _ = _gate_bwd(f"ffn_gate_bwd_{tag}", dx, y, g2, after=after)

    def act_bwd(df, a, b):
        return df * b * _silu_grad(a), df * (a * _sigmoid(a))
    da, db = _mm(f"ffn_dact_{tag}", [dy], [wd], "nt", [BF16, BF16], epi=act_bwd, extras=[a, b], tn_cap=1408)
    dwd = _mm_tn(f"ffn_dwd_{tag}", f, dy)
    dwg = _mm_tn(f"ffn_dwg_{tag}", da, h)
    dwu = _mm_tn(f"ffn_dwu_{tag}", db, h)
    dh = _mm(f"ffn_dh_{tag}", [da, db], [wg, wu], "nn", [F32])[0]
    return dh, dg2, dwg, dwu, dwd


def _conv_fwd(xr, w, b, tb=512):
    S, C = xr.shape
    tb = min(tb, S)

    def body(x_ref, halo_ref, w_ref, b_ref, pre_ref, out_ref):
        i = pl.program_id(0)
        halo = jnp.where(i > 0, halo_ref[...], 0.0)
        xe = jnp.concatenate([halo, x_ref[...]], axis=0)
        pre = w_ref[3:4, :] * x_ref[...] + b_ref[...]
        for j in (1, 2, 3):
            pre = pre + w_ref[3 - j:4 - j, :] * pltpu.roll(xe, j, axis=0)[8:, :]
        pre_ref[...] = pre
        out_ref[...] = pre * _sigmoid(pre)

    return pl.pallas_call(
        body, name="conv_fwd", grid=(S // tb,),
        in_specs=[pl.BlockSpec((tb, C), lambda i: (i, 0)),
                  pl.BlockSpec((8, C), lambda i: (jnp.maximum(i * (tb // 8) - 1, 0), 0)),
                  pl.BlockSpec((4, C), lambda i: (0, 0)), pl.BlockSpec((1, C), lambda i: (0, 0))],
        out_specs=[pl.BlockSpec((tb, C), lambda i: (i, 0))] * 2,
        out_shape=[jax.ShapeDtypeStruct((S, C), F32)] * 2,
        compiler_params=_params(("parallel",)))(xr, xr, w, b)


def _conv_bwd(dxc, pre, xr, w, tb=512):
    S, C = xr.shape
    tb = min(tb, S)
    nblk = S // tb

    def body(d_ref, p_ref, dn_ref, pn_ref, x_ref, xh_ref, w_ref, dx_ref, dw_ref, db_ref):
        i = pl.program_id(0)

        @pl.when(i == 0)
        def _():
            dw_ref[...] = jnp.zeros_like(dw_ref)
            db_ref[...] = jnp.zeros_like(db_ref)

        dpre = d_ref[...] * _silu_grad(p_ref[...])
        dnext = jnp.where(i < nblk - 1, dn_ref[...] * _silu_grad(pn_ref[...]), 0.0)
        pe = jnp.concatenate([dpre, dnext], axis=0)
        dx = w_ref[3:4, :] * dpre
        for j in (1, 2, 3):
            dx = dx + w_ref[3 - j:4 - j, :] * pltpu.roll(pe, tb + 8 - j, axis=0)[:tb, :]
        dx_ref[...] = dx.astype(dx_ref.dtype)
        halo = jnp.where(i > 0, xh_ref[...], 0.0)
        xe = jnp.concatenate([halo, x_ref[...]], axis=0)
        for k in range(3):
            dw_ref[k:k + 1, :] += jnp.sum(dpre * pltpu.roll(xe, 3 - k, axis=0)[8:, :], axis=0, keepdims=True)
        dw_ref[3:4, :] += jnp.sum(dpre * x_ref[...], axis=0, keepdims=True)
        db_ref[...] += jnp.sum(dpre, axis=0, keepdims=True)

    blk = pl.BlockSpec((tb, C), lambda i: (i, 0))
    nxt = pl.BlockSpec((8, C), lambda i: (jnp.minimum((i + 1) * (tb // 8), S // 8 - 1), 0))
    prv = pl.BlockSpec((8, C), lambda i: (jnp.maximum(i * (tb // 8) - 1, 0), 0))
    return pl.pallas_call(
        body, name="conv_bwd", grid=(nblk,),
        in_specs=[blk, blk, nxt, nxt, blk, prv, pl.BlockSpec((4, C), lambda i: (0, 0))],
        out_specs=[blk, pl.BlockSpec((4, C), lambda i: (0, 0)), pl.BlockSpec((1, C), lambda i: (0, 0))],
        out_shape=[jax.ShapeDtypeStruct((S, C), BF16), jax.ShapeDtypeStruct((4, C), F32),
                   jax.ShapeDtypeStruct((1, C), F32)],
        compiler_params=_params(("arbitrary",)))(dxc, pre, dxc, pre, xr, xr, w)


def _iota(shape, dim):
    return lax.broadcasted_iota(jnp.int32, shape, dim)


def _colsel(m, lane, h):
    return jnp.sum(jnp.where(lane == h, m, 0.0), axis=1, keepdims=True)


def _cumsum_rows(v):
    r = _iota(v.shape, 0)
    k = 1
    while k < v.shape[0]:
        v = v + jnp.where(r >= k, pltpu.roll(v, k, axis=0), 0.0)
        k *= 2
    return v


def _suffix_sum_rows(v):
    n = v.shape[0]
    r = _iota(v.shape, 0)
    k = 1
    while k < n:
        v = v + jnp.where(r < n - k, pltpu.roll(v, n - k, axis=0), 0.0)
        k *= 2
    return v


def _ssd_fwd(xc, dtr, z, dtb, alog, dskl, nw):
    S = xc.shape[0]
    nc = S // L

    def body(xc_ref, dtr_ref, z_ref, dtb_ref, alog_ref, dsk_ref, nw_ref, ya_ref, y_ref, prev_ref,
             st_ref, cum_ref, cumT_ref):
        i = pl.program_id(0)

        @pl.when(i == 0)
        def _():
            st_ref[...] = jnp.zeros_like(st_ref)

        lane = _iota((L, 128), 1)
        lane1 = _iota((1, 128), 1)
        lo = lane < 64
        lo1 = lane1 < 64
        tril = _iota((L, L), 0) >= _iota((L, L), 1)
        dt = _softplus(dtr_ref[...] + dtb_ref[...])
        a_neg = -jnp.exp(alog_ref[...])
        cum = _cumsum_rows(dt * a_neg)
        cum_ref[...] = cum
        cumT_ref[...] = cum.T
        last_all = cum_ref[L - 1:L, :]
        prev_t = st_ref[...]
        prev_ref[0] = prev_t
        for g in range(2):
            bg = xc_ref[:, 1024 + g * 128:1152 + g * 128]
            cg = xc_ref[:, 1280 + g * 128:1408 + g * 128]
            gmat = _dot(cg, bg, NT)
            yoff = _dot(cg, prev_t[:, g * 512:(g + 1) * 512])
            bg_t = bg.T
            for jp in range(4):
                j = g * 4 + jp
                sl = slice(j * 128, (j + 1) * 128)
                xp = xc_ref[:, sl]
                cc = [_colsel(cum, lane, 2 * j), _colsel(cum, lane, 2 * j + 1)]
                cum_l = jnp.where(lo, cc[0], cc[1])
                dt_l = jnp.where(lo, _colsel(dt, lane, 2 * j), _colsel(dt, lane, 2 * j + 1))
                last_l = jnp.where(lo1, _colsel(last_all, lane1, 2 * j), _colsel(last_all, lane1, 2 * j + 1))
                xd = xp * dt_l
                ys = []
                for hh in range(2):
                    seg = cc[hh] - cumT_ref[2 * j + hh:2 * j + hh + 1, :]
                    dm = jnp.where(tril, jnp.exp(jnp.where(tril, seg, 0.0)), 0.0)
                    ys.append(_dot(gmat * dm, xd))
                y_ref[:, sl] = (jnp.where(lo, ys[0], ys[1]) + jnp.exp(cum_l) * yoff[:, jp * 128:(jp + 1) * 128]
                                + dsk_ref[:, sl] * xp)
                st_ref[:, sl] = prev_t[:, sl] * jnp.exp(last_l) + _dot(bg_t, xd * jnp.exp(last_l - cum_l))
        for g in range(2):
            sl = slice(g * 512, (g + 1) * 512)
            zz = z_ref[:, sl]
            yg = y_ref[:, sl] * (zz * _sigmoid(zz))
            rstd = lax.rsqrt(jnp.mean(yg * yg, axis=-1, keepdims=True) + EPS)
            ya_ref[:, sl] = (yg * rstd * nw_ref[:, sl]).astype(ya_ref.dtype)

    blk = lambda c: pl.BlockSpec((L, c), lambda i: (i, 0))
    vec = lambda c: pl.BlockSpec((1, c), lambda i: (0, 0))
    return pl.pallas_call(
        body, name="ssd_fwd", grid=(nc,),
        in_specs=[blk(1536), blk(128), blk(1024), vec(128), vec(128), vec(1024), vec(1024)],
        out_specs=[blk(1024), blk(1024), pl.BlockSpec((1, NSTATE, 1024), lambda i: (i, 0, 0))],
        out_shape=[jax.ShapeDtypeStruct((S, 1024), BF16), jax.ShapeDtypeStruct((S, 1024), F32),
                   jax.ShapeDtypeStruct((nc, NSTATE, 1024), F32)],
        scratch_shapes=[pltpu.VMEM((NSTATE, 1024), F32), pltpu.VMEM((L, 128), F32), pltpu.VMEM((L, 128), F32)],
        compiler_params=_params(("arbitrary",)))(xc, dtr, z, dtb, alog, dskl, nw)


def _ssd_bwd(dya, y, z, xc, dtr, prev, dtb, alog, dskl, nw):
    S = xc.shape[0]
    nc = S // L

    def body(dya_ref, y_ref, z_ref, xc_ref, dtr_ref, prev_ref, dtb_ref, alog_ref, dsk_ref, nw_ref,
             dz_ref, dxc_ref, ddtr_ref, dnw_ref, ddsk_ref, dalog_ref, ddtb_ref,
             dst_ref, cum_ref, cumT_ref, dy_ref, dskacc_ref):
        i = pl.program_id(0)

        @pl.when(i == 0)
        def _():
            dst_ref[...] = jnp.zeros_like(dst_ref)
            dskacc_ref[...] = jnp.zeros_like(dskacc_ref)
            dnw_ref[...] = jnp.zeros_like(dnw_ref)
            dalog_ref[...] = jnp.zeros_like(dalog_ref)
            ddtb_ref[...] = jnp.zeros_like(ddtb_ref)

        lane = _iota((L, 128), 1)
        lane1 = _iota((1, 128), 1)
        lo = lane < 64
        lo1 = lane1 < 64
        r2, c2 = _iota((L, L), 0), _iota((L, L), 1)
        tril = r2 >= c2
        triu = r2 <= c2
        is_last = _iota((L, 1), 0) == L - 1

        for g in range(2):
            sl = slice(g * 512, (g + 1) * 512)
            zz = z_ref[:, sl]
            sg = _sigmoid(zz)
            zg = zz * sg
            yv = y_ref[:, sl]
            yg = yv * zg
            rstd = lax.rsqrt(jnp.mean(yg * yg, axis=-1, keepdims=True) + EPS)
            xh = yg * rstd
            d_out = dya_ref[:, sl]
            dnw_ref[:, sl] += jnp.sum(d_out * xh, axis=0, keepdims=True)
            dyn = d_out * nw_ref[:, sl]
            dyg = rstd * (dyn - xh * jnp.mean(dyn * xh, axis=-1, keepdims=True))
            dy_ref[:, sl] = dyg * zg
            dz_ref[:, sl] = (dyg * yv * (sg * (1.0 + zz * (1.0 - sg)))).astype(dz_ref.dtype)

        dtin = dtr_ref[...] + dtb_ref[...]
        dt = _softplus(dtin)
        a_neg = -jnp.exp(alog_ref[...])
        cum = _cumsum_rows(dt * a_neg)
        cum_ref[...] = cum
        cumT_ref[...] = cum.T
        last_all = cum_ref[L - 1:L, :]
        prev_t = prev_ref[0]
        dn_t = dst_ref[...]
        dcum = jnp.zeros((L, 128), F32)
        ddt = jnp.zeros((L, 128), F32)
        for g in range(2):
            gsl = slice(g * 512, (g + 1) * 512)
            bg = xc_ref[:, 1024 + g * 128:1152 + g * 128]
            cg = xc_ref[:, 1280 + g * 128:1408 + g * 128]
            gmat = _dot(cg, bg, NT)
            gmat_t = _dot(bg, cg, NT)
            pg = prev_t[:, gsl]
            zmat = _dot(cg, pg)
            dgm = jnp.zeros((L, L), F32)
            dgm_t = jnp.zeros((L, L), F32)
            db_acc = jnp.zeros((L, NSTATE), F32)
            dz_parts, cd_parts = [], []
            for jp in range(4):
                j = g * 4 + jp
                sl = slice(j * 128, (j + 1) * 128)
                xp = xc_ref[:, sl]
                dyp = dy_ref[:, sl]
                cc = [_colsel(cum, lane, 2 * j), _colsel(cum, lane, 2 * j + 1)]
                lc = [_colsel(last_all, lane1, 2 * j), _colsel(last_all, lane1, 2 * j + 1)]
                cum_l = jnp.where(lo, cc[0], cc[1])
                dt_l = jnp.where(lo, _colsel(dt, lane, 2 * j), _colsel(dt, lane, 2 * j + 1))
                last_l = jnp.where(lo1, lc[0], lc[1])
                e_l = jnp.exp(cum_l)
                dte_l = jnp.exp(last_l - cum_l)
                cd_l = jnp.exp(last_l)
                cd_parts.append(cd_l)
                xd = xp * dt_l
                dskacc_ref[:, sl] += jnp.sum(dyp * xp, axis=0, keepdims=True)
                dxp = dsk_ref[:, sl] * dyp
                t = dyp * (e_l * zmat[:, jp * 128:(jp + 1) * 128])
                dcc = [jnp.sum(jnp.where(lo, t, 0.0), axis=1, keepdims=True),
                       jnp.sum(jnp.where(lo, 0.0, t), axis=1, keepdims=True)]
                dz_parts.append(e_l * dyp)
                dnp_ = dn_t[:, sl]
                t2 = jnp.sum(dnp_ * prev_t[:, sl], axis=0, keepdims=True)
                dcd = [jnp.sum(jnp.where(lo1, t2, 0.0), axis=1, keepdims=True),
                       jnp.sum(jnp.where(lo1, 0.0, t2), axis=1, keepdims=True)]
                wm = _dot(bg, dnp_)
                dxd = wm * dte_l
                t3 = wm * xd
                ddte = [jnp.sum(jnp.where(lo, t3, 0.0), axis=1, keepdims=True),
                        jnp.sum(jnp.where(lo, 0.0, t3), axis=1, keepdims=True)]
                db_acc = db_acc + _dot(xd * dte_l, dnp_, NT)
                for hh in range(2):
                    h = 2 * j + hh
                    half = lo if hh == 0 else jnp.logical_not(lo)
                    row = cumT_ref[h:h + 1, :]
                    dm = jnp.where(tril, jnp.exp(jnp.where(tril, cc[hh] - row, 0.0)), 0.0)
                    dm_t = jnp.where(triu, jnp.exp(jnp.where(triu, row - cc[hh], 0.0)), 0.0)
                    m = gmat * dm
                    m_t = gmat_t * dm_t
                    dym = jnp.where(half, dyp, 0.0)
                    d_m = _dot(dym, xd, NT)
                    d_mt = _dot(xd, dym, NT)
                    dxd = dxd + _dot(m_t, dym)
                    dcc[hh] = dcc[hh] + jnp.sum(d_m * m, axis=1, keepdims=True) - jnp.sum(d_mt * m_t, axis=1, keepdims=True)
                    dgm = dgm + d_m * dm
                    dgm_t = dgm_t + d_mt * dm_t
                    dte_c = jnp.exp(lc[hh] - cc[hh])
                    dcc[hh] = dcc[hh] - ddte[hh] * dte_c
                    endc = dcd[hh] * jnp.exp(lc[hh]) + jnp.sum(ddte[hh] * dte_c, axis=0, keepdims=True)
                    dcc[hh] = dcc[hh] + jnp.where(is_last, endc, 0.0)
                    dcum = jnp.where(lane == h, dcc[hh], dcum)
                dxc_ref[:, sl] = dxp + dxd * dt_l
                t4 = dxd * xp
                ddt = jnp.where(lane == 2 * j, jnp.sum(jnp.where(lo, t4, 0.0), axis=1, keepdims=True), ddt)
                ddt = jnp.where(lane == 2 * j + 1, jnp.sum(jnp.where(lo, 0.0, t4), axis=1, keepdims=True), ddt)
            dzg = jnp.concatenate(dz_parts, axis=1)
            dst_ref[:, gsl] = dn_t[:, gsl] * jnp.concatenate(cd_parts, axis=1) + _dot(cg.T, dzg)
            dxc_ref[:, 1280 + g * 128:1408 + g * 128] = _dot(dgm, bg) + _dot(dzg, pg, NT)
            dxc_ref[:, 1024 + g * 128:1152 + g * 128] = _dot(dgm_t, cg) + db_acc
        dla = _suffix_sum_rows(dcum)
        ddt = ddt + dla * a_neg
        dalog_ref[...] += jnp.sum(dla * dt, axis=0, keepdims=True) * a_neg
        ddtr = jnp.where(lane < 16, ddt * _sigmoid(dtin), 0.0)
        ddtr_ref[...] = ddtr.astype(ddtr_ref.dtype)
        ddtb_ref[...] += jnp.sum(ddtr, axis=0, keepdims=True)

        @pl.when(i == nc - 1)
        def _():
            seg = (_iota((1024, 128), 0) // 64 == _iota((1024, 128), 1)).astype(F32)
            acc8 = jnp.broadcast_to(dskacc_ref[...], (8, 1024))
            ddsk_ref[...] = lax.dot_general(acc8, seg, NN, precision=lax.Precision.HIGHEST,
                                            preferred_element_type=F32)

    rev = lambda c: pl.BlockSpec((L, c), lambda i: (nc - 1 - i, 0))
    vec = lambda c: pl.BlockSpec((1, c), lambda i: (0, 0))
    return pl.pallas_call(
        body, name="ssd_bwd", grid=(nc,),
        in_specs=[rev(1024), rev(1024), rev(1024), rev(1536), rev(128),
                  pl.BlockSpec((1, NSTATE, 1024), lambda i: (nc - 1 - i, 0, 0)),
                  vec(128), vec(128), vec(1024), vec(1024)],
        out_specs=[rev(1024), rev(1536), rev(128), vec(1024), pl.BlockSpec((8, 128), lambda i: (0, 0)),
                   vec(128), vec(128)],
        out_shape=[jax.ShapeDtypeStruct((S, 1024), BF16), jax.ShapeDtypeStruct((S, 1536), F32),
                   jax.ShapeDtypeStruct((S, 128), BF16), jax.ShapeDtypeStruct((1, 1024), F32),
                   jax.ShapeDtypeStruct((8, 128), F32), jax.ShapeDtypeStruct((1, 128), F32),
                   jax.ShapeDtypeStruct((1, 128), F32)],
        scratch_shapes=[pltpu.VMEM((NSTATE, 1024), F32), pltpu.VMEM((L, 128), F32), pltpu.VMEM((L, 128), F32),
                        pltpu.VMEM((L, 1024), F32), pltpu.VMEM((1, 1024), F32)],
        compiler_params=_params(("arbitrary",)))(dya, y, z, xc, dtr, prev, dtb, alog, dskl, nw)


def _layer_norm_parts(vg):
    mu = jnp.mean(vg, axis=-1, keepdims=True)
    vc = vg - mu
    rstd = lax.rsqrt(jnp.mean(vc * vc, axis=-1, keepdims=True) + EPS)
    return vc * rstd, rstd


def _gmlp_fwd(u, v, lnw, lnb, ws, bse, tb=512):
    S = u.shape[0]
    tb = min(tb, S)

    def body(u_ref, v_ref, lnw_ref, lnb_ref, ws_ref, bse_ref, o_ref, vn_ref):
        tril = _iota((L, L), 0) >= _iota((L, L), 1)
        xh, _ = _layer_norm_parts(_gelu(v_ref[...]))
        vn_ref[...] = xh * lnw_ref[...] + lnb_ref[...]
        for g in range(8):
            w = jnp.where(tril, ws_ref[g], 0.0)
            gs = slice(g * 128, (g + 1) * 128)
            for ch in range(tb // L):
                rs = slice(ch * L, (ch + 1) * L)
                sv = _dot(w, vn_ref[rs, gs]) + bse_ref[g]
                o_ref[rs, gs] = (_gelu(u_ref[rs, gs]) * sv).astype(o_ref.dtype)

    blk = pl.BlockSpec((tb, 1024), lambda i: (i, 0))
    vec = pl.BlockSpec((1, 1024), lambda i: (0, 0))
    cube = pl.BlockSpec((8, L, 128), lambda i: (0, 0, 0))
    return pl.pallas_call(
        body, name="gmlp_fwd", grid=(S // tb,), in_specs=[blk, blk, vec, vec, cube, cube], out_specs=blk,
        out_shape=jax.ShapeDtypeStruct((S, 1024), BF16), scratch_shapes=[pltpu.VMEM((tb, 1024), F32)],
        compiler_params=_params(("parallel",)))(u, v, lnw, lnb, ws, bse)


def _gmlp_bwd(dyb, u, v, lnw, lnb, ws, bse, tb=512):
    S = u.shape[0]
    tb = min(tb, S)

    def body(d_ref, u_ref, v_ref, lnw_ref, lnb_ref, ws_ref, bse_ref,
             du_ref, dv_ref, dws_ref, dbse_ref, dlnw_ref, dlnb_ref, vn_ref, dvn_ref):
        @pl.when(pl.program_id(0) == 0)
        def _():
            dws_ref[...] = jnp.zeros_like(dws_ref)
            dbse_ref[...] = jnp.zeros_like(dbse_ref)
            dlnw_ref[...] = jnp.zeros_like(dlnw_ref)
            dlnb_ref[...] = jnp.zeros_like(dlnb_ref)

        tril = _iota((L, L), 0) >= _iota((L, L), 1)
        vv = v_ref[...]
        xh, rstd = _layer_norm_parts(_gelu(vv))
        vn_ref[...] = xh * lnw_ref[...] + lnb_ref[...]
        for g in range(8):
            w = jnp.where(tril, ws_ref[g], 0.0)
            w_t = w.T
            gs = slice(g * 128, (g + 1) * 128)
            dw = jnp.zeros((L, L), F32)
            dbs = jnp.zeros((L, 128), F32)
            for ch in range(tb // L):
                rs = slice(ch * L, (ch + 1) * L)
                vn = vn_ref[rs, gs]
                sv = _dot(w, vn) + bse_ref[g]
                uu = u_ref[rs, gs]
                dd = d_ref[rs, gs]
                du_ref[rs, gs] = (dd * sv * _gelu_grad(uu)).astype(du_ref.dtype)
                dsv = dd * _gelu(uu)
                dw = dw + _dot(dsv, vn, NT)
                dbs = dbs + dsv
                dvn_ref[rs, gs] = _dot(w_t, dsv)
            dws_ref[g] += jnp.where(tril, dw, 0.0)
            dbse_ref[g] += dbs
        dvn = dvn_ref[...]
        dlnw_ref[...] += jnp.sum(dvn * xh, axis=0, keepdims=True)
        dlnb_ref[...] += jnp.sum(dvn, axis=0, keepdims=True)
        dxh = dvn * lnw_ref[...]
        dvg = rstd * (dxh - jnp.mean(dxh, axis=-1, keepdims=True) - xh * jnp.mean(dxh * xh, axis=-1, keepdims=True))
        dv_ref[...] = (dvg * _gelu_grad(vv)).astype(dv_ref.dtype)

    blk = pl.BlockSpec((tb, 1024), lambda i: (i, 0))
    vec = pl.BlockSpec((1, 1024), lambda i: (0, 0))
    cube = pl.BlockSpec((8, L, 128), lambda i: (0, 0, 0))
    return pl.pallas_call(
        body, name="gmlp_bwd", grid=(S // tb,), in_specs=[blk, blk, blk, vec, vec, cube, cube],
        out_specs=[blk, blk, cube, cube, vec, vec],
        out_shape=[jax.ShapeDtypeStruct((S, 1024), BF16), jax.ShapeDtypeStruct((S, 1024), BF16),
                   jax.ShapeDtypeStruct((8, L, 128), F32), jax.ShapeDtypeStruct((8, L, 128), F32),
                   jax.ShapeDtypeStruct((1, 1024), F32), jax.ShapeDtypeStruct((1, 1024), F32)],
        scratch_shapes=[pltpu.VMEM((tb, 1024), F32), pltpu.VMEM((tb, 1024), F32)],
        compiler_params=_params(("arbitrary",)))(dyb, u, v, lnw, lnb, ws, bse)


def _lane_sum(name, a):
    def body(a_ref, o_ref):
        o_ref[...] = jnp.sum(a_ref[...], axis=1, keepdims=True)
    return pl.pallas_call(body, name=name, out_shape=jax.ShapeDtypeStruct((a.shape[0], 1), F32))(a)


def _bucket_onehot_t():
    qi = np.arange(L)[:, None]
    sj = np.arange(2 * L)[None, :]
    dist = np.maximum(qi + L - sj, 0)
    log_ratio = (np.log(np.maximum(dist, 1).astype(np.float32) / np.float32(16)) / np.float32(math.log(128 / 16)))
    large = 16 + (log_ratio.astype(np.float32) * np.float32(16)).astype(np.int32)
    bucket = np.where(dist < 16, dist, np.minimum(large, 31)).reshape(-1)
    return (np.arange(32)[:, None] == bucket[None, :]).astype(np.float32)


def _rel_bias(table_t, onehot_t):
    def body(t_ref, oh_ref, o_ref):
        o_ref[...] = lax.dot_general(t_ref[...], oh_ref[...], NN, precision=lax.Precision.HIGHEST,
                                     preferred_element_type=F32)
    return pl.pallas_call(body, name="rel_bias", out_shape=jax.ShapeDtypeStruct((16, L * 2 * L), F32),
                          compiler_params=_params())(table_t, onehot_t)


def _rel_bias_bwd(dbias, onehot_t):
    def body(d_ref, oh_ref, o_ref):
        o_ref[...] = lax.dot_general(d_ref[...], oh_ref[...], NT, precision=lax.Precision.HIGHEST,
                                     preferred_element_type=F32)
    return pl.pallas_call(body, name="rel_bias_bwd", out_shape=jax.ShapeDtypeStruct((16, 32), F32),
                          compiler_params=_params())(dbias, onehot_t)


def _band(kp, kc, lo):
    kk = jnp.concatenate([kp, kc], axis=0)
    kr = pltpu.roll(kk, 64, axis=1)
    return [jnp.where(lo, kk, kr), jnp.where(lo, kr, kk)]


def _attn_mask(i):
    qi, sj = _iota((L, 2 * L), 0), _iota((L, 2 * L), 1)
    rel = qi + L - sj
    return (rel >= 0) & (rel < L) & ((sj >= L) | (i > 0))


SMEM = pl.BlockSpec(memory_space=pltpu.SMEM)


def _attn_fwd(qkv, bias, sinks):
    S = qkv.shape[0]
    nb = S // L
    scale = 64 ** -0.5

    def body(sink_ref, q_ref, kc_ref, vc_ref, kp_ref, vp_ref, bias_ref, o_ref, lse_ref):
        i = pl.program_id(0)
        lane = _iota((L, 128), 1)
        lo = lane < 64
        lo2 = _iota((2 * L, 128), 1) < 64
        mask = _attn_mask(i)
        kd = _band(kp_ref[...], kc_ref[...], lo2)
        vd = _band(vp_ref[...], vc_ref[...], lo2)
        lse = jnp.zeros((L, 128), F32)
        for pr in range(8):
            sl = slice(pr * 128, (pr + 1) * 128)
            qp = q_ref[:, sl]
            j = pr // 4
            outs = []
            for hh in range(2):
                h = 2 * pr + hh
                qm = jnp.where(lo if hh == 0 else jnp.logical_not(lo), qp, 0.0)
                lg = jnp.where(mask, _dot(qm, kd[j], NT) * scale + bias_ref[h], NEG_INF)
                s = sink_ref[h]
                m = jnp.maximum(jnp.max(lg, axis=1, keepdims=True), s)
                p = jnp.where(mask, jnp.exp(lg - m), 0.0)
                den = jnp.sum(p, axis=1, keepdims=True) + jnp.exp(s - m)
                outs.append(_dot(p / den, vd[j]))
                lse = jnp.where(lane == h, m + jnp.log(den), lse)
            o_ref[:, sl] = jnp.where(lo, outs[0], outs[1]).astype(o_ref.dtype)
        lse_ref[...] = lse

    prev = lambda col: pl.BlockSpec((L, 128), lambda i: (jnp.maximum(i - 1, 0), col))
    cur = lambda col: pl.BlockSpec((L, 128), lambda i: (i, col))
    return pl.pallas_call(
        body, name="attn_fwd", grid=(nb,),
        in_specs=[SMEM, pl.BlockSpec((L, 1024), lambda i: (i, 0)), cur(8), cur(9), prev(8), prev(9),
                  pl.BlockSpec((16, L, 2 * L), lambda i: (0, 0, 0))],
        out_specs=[pl.BlockSpec((L, 1024), lambda i: (i, 0)), pl.BlockSpec((L, 128), lambda i: (i, 0))],
        out_shape=[jax.ShapeDtypeStruct((S, 1024), BF16), jax.ShapeDtypeStruct((S, 128), F32)],
        compiler_params=_params(("parallel",)))(sinks, qkv, qkv, qkv, qkv, qkv, bias)


def _attn_bwd(qkv, d_o, lse, bias, sinks):
    S = qkv.shape[0]
    nb = S // L
    scale = 64 ** -0.5

    def body(sink_ref, q_ref, kc_ref, vc_ref, kp_ref, vp_ref, do_ref, lse_ref, bias_ref,
             dq_ref, dkv_ref, dbias_ref, dsink_ref, dbq_ref, dbkv_ref, carry_ref):
        i = pl.program_id(0)

        @pl.when(i == 0)
        def _():
            dbias_ref[...] = jnp.zeros_like(dbias_ref)
            dsink_ref[...] = jnp.zeros_like(dsink_ref)
            dbq_ref[...] = jnp.zeros_like(dbq_ref)
            dbkv_ref[...] = jnp.zeros_like(dbkv_ref)
            carry_ref[...] = jnp.zeros_like(carry_ref)

        @pl.when(i < nb)
        def _():
            lane = _iota((L, 128), 1)
            lane1 = _iota((1, 128), 1)
            lo = lane < 64
            lo2 = _iota((2 * L, 128), 1) < 64
            mask = _attn_mask(i)
            kd = _band(kp_ref[...], kc_ref[...], lo2)
            vd = _band(vp_ref[...], vc_ref[...], lo2)
            lse_all = lse_ref[...]
            acc_k = [jnp.zeros((2 * L, 128), F32), jnp.zeros((2 * L, 128), F32)]
            acc_v = [jnp.zeros((2 * L, 128), F32), jnp.zeros((2 * L, 128), F32)]
            dsink = jnp.zeros((1, 128), F32)
            for pr in range(8):
                sl = slice(pr * 128, (pr + 1) * 128)
                qp = q_ref[:, sl]
                dop = do_ref[:, sl]
                j = pr // 4
                dqs = []
                for hh in range(2):
                    h = 2 * pr + hh
                    half = lo if hh == 0 else jnp.logical_not(lo)
                    qm = jnp.where(half, qp, 0.0)
                    dom = jnp.where(half, dop, 0.0)
                    lse_h = _colsel(lse_all, lane, h)
                    lg = _dot(qm, kd[j], NT) * scale + bias_ref[h]
                    p = jnp.where(mask, jnp.exp(jnp.where(mask, lg, NEG_INF) - lse_h), 0.0)
                    dp = _dot(dom, vd[j], NT)
                    delta = jnp.sum(p * dp, axis=1, keepdims=True)
                    ds = p * (dp - delta)
                    dbias_ref[h] += ds
                    ds_sink = jnp.sum(-jnp.exp(sink_ref[h] - lse_h) * delta, axis=0, keepdims=True)
                    dsink = dsink + jnp.where(lane1 == h, ds_sink, 0.0)
                    dss = ds * scale
                    dqs.append(_dot(dss, kd[j]))
                    acc_k[j] = acc_k[j] + _dot(dss, qm, TN)
                    acc_v[j] = acc_v[j] + _dot(p, dom, TN)
                dq = jnp.where(lo, dqs[0], dqs[1])
                dq_ref[:, sl] = dq.astype(dq_ref.dtype)
                dbq_ref[:, sl] += jnp.sum(dq, axis=0, keepdims=True)
            dsink_ref[...] += dsink
            tot_k = [a + pltpu.roll(a, 64, axis=1) for a in acc_k]
            tot_v = [a + pltpu.roll(a, 64, axis=1) for a in acc_v]
            dkv = jnp.concatenate([jnp.where(lo2, tot_k[0], tot_k[1]), jnp.where(lo2, tot_v[0], tot_v[1])], axis=1)
            dbkv_ref[...] += jnp.sum(dkv, axis=0, keepdims=True)
            dkv_ref[...] = (carry_ref[...] + dkv[:L, :]).astype(dkv_ref.dtype)
            carry_ref[...] = dkv[L:, :]

        @pl.when(i == nb)
        def _():
            dkv_ref[...] = carry_ref[...].astype(dkv_ref.dtype)

    c = lambda i: jnp.minimum(i, nb - 1)
    prev = lambda col: pl.BlockSpec((L, 128), lambda i: (jnp.maximum(c(i) - 1, 0), col))
    cur = lambda col: pl.BlockSpec((L, 128), lambda i: (c(i), col))
    row = lambda w: pl.BlockSpec((L, w), lambda i: (c(i), 0))
    cube = pl.BlockSpec((16, L, 2 * L), lambda i: (0, 0, 0))
    vec = lambda w: pl.BlockSpec((1, w), lambda i: (0, 0))
    return pl.pallas_call(
        body, name="attn_bwd", grid=(nb + 1,),
        in_specs=[SMEM, row(1024), cur(8), cur(9), prev(8), prev(9), row(1024), row(128), cube],
        out_specs=[row(1024), pl.BlockSpec((L, 256), lambda i: (jnp.maximum(i - 1, 0), 0)), cube,
                   vec(128), vec(1024), vec(256)],
        out_shape=[jax.ShapeDtypeStruct((S, 1024), BF16), jax.ShapeDtypeStruct((S, 256), BF16),
                   jax.ShapeDtypeStruct((16, L, 2 * L), F32), jax.ShapeDtypeStruct((1, 128), F32),
                   jax.ShapeDtypeStruct((1, 1024), F32), jax.ShapeDtypeStruct((1, 256), F32)],
        scratch_shapes=[pltpu.VMEM((L, 256), F32)],
        compiler_params=_params(("arbitrary",)))(sinks, qkv, qkv, qkv, qkv, qkv, d_o, lse, bias)


def _pad_lanes(a, n=128):
    return jnp.pad(a, ((0, 0), (0, n - a.shape[1])))


def _local_step(x, tgt, mod, w_in, P, io):
    md = [[mod[l:l + 1, k * D:(k + 1) * D] for k in range(6)] for l in range(2)]
    G, g = {}, {}

    sh1, sc1, g1, sh2, sc2, g2 = md[0]
    nmw0, nfw0 = P["norm_mix_w"][0:1], P["norm_ffn_w"][0:1]
    h0 = _norm_mod_fwd("norm_mix_0", x, nmw0, sc1, sh1, after=io["start"])
    segs = {"z": w_in[0:1024], "xbc": w_in[1024:2560], "dt": jnp.pad(w_in[2560:2576], ((0, 112), (0, 0))),
            "u": w_in[2576:3600], "v": w_in[3600:4624]}
    proj = {k: _mm(f"in_proj_{k}", [h0], [w], "nt", [F32])[0] for k, w in segs.items()}
    conv_w, conv_b = P["conv_w"][0], P["conv_b"]
    pre, xc = _conv_fwd(proj["xbc"], conv_w, conv_b)
    dtb, alog = _pad_lanes(P["dt_bias"]), _pad_lanes(P["a_log"])
    dskl = jnp.repeat(P["d_skip"], 64, axis=1)
    ya, y_ssd, prev = _ssd_fwd(xc, proj["dt"], proj["z"], dtb, alog, dskl, P["ssm_norm_w"])
    ws = P["gmlp_ws"][0]
    bse = jnp.broadcast_to(P["gmlp_bs"][0][:, :, None], (8, L, 128))
    yb = _gmlp_fwd(proj["u"], proj["v"], P["gmlp_ln_w"], P["gmlp_ln_b"], ws, bse)
    W = dict(io["weights0"](yb))
    w_oa, w_ob = W["out_w"][:1024], W["out_w"][1024:]

    def res(y, x, gate):
        return y, x + gate * y
    mix0, x1 = _mm("out_proj_0", [ya, yb], [w_oa, w_ob], "nn", [F32, F32], epi=res, extras=[x], vecs=[g1])
    h0f = _norm_mod_fwd("norm_ffn_0", x1, nfw0, sc2, sh2)
    a0, b0, f0, y0, x2 = _ffn_fwd("0", h0f, W["gate_wt0"], W["up_wt0"], W["down_w0"], x1, g2)

    sh1b, sc1b, g1b, sh2b, sc2b, g2b = md[1]
    nmw1, nfw1 = P["norm_mix_w"][1:2], P["norm_ffn_w"][1:2]
    W.update(io["weights1"](x2))
    h1 = _norm_mod_fwd("norm_mix_1", x2, nmw1, sc1b, sh1b)
    qkv = _mm("qkv_proj", [h1], [W["qkv_wt"]], "nt", [F32], epi=lambda acc, b: acc + b, vecs=[P["qkv_b"]])[0]
    onehot_t = jnp.asarray(_bucket_onehot_t())
    bias = _rel_bias(P["rel_table"].T, onehot_t).reshape(16, L, 2 * L)
    sinks = P["sinks"].reshape(16)
    att, lse = _attn_fwd(qkv, bias, sinks)

    def res_b(y, x, gate, b):
        y = y + b
        return y, x + gate * y
    mix1, x3 = _mm("o_proj", [att], [W["o_w"]], "nn", [F32, F32], epi=res_b, extras=[x2], vecs=[g1b, P["o_b"]])
    h1f = _norm_mod_fwd("norm_ffn_1", x3, nfw1, sc2b, sh2b)
    a1, b1, f1, y1, x4 = _ffn_fwd("1", h1f, W["gate_wt1"], W["up_wt1"], W["down_w1"], x3, g2b)

    dx, sq, g["final_norm_w"] = _loss_head(x4, tgt, P["final_norm_w"])

    dh, dg2b, dwg1, dwu1, dwd1 = _ffn_bwd("1", dx, h1f, a1, b1, f1, y1, W["gate_wt1"], W["up_wt1"],
                                          W["down_w1"], g2b)
    dx, dsh2b, dsc2b, dnfw1 = _norm_mod_bwd("norm_ffn_bwd_1", x3, dh, dx, nfw1, sc2b)
    dmix, dg1b, g["o_b"] = _gate_bwd("mix_gate_bwd_1", dx, mix1, g1b)
    G["o_w"] = _mm_tn("o_dw", att, dmix)
    d_att = _mm("o_dx", [dmix], [W["o_w"]], "nt", [F32])[0]
    dq, dkv, dbias, dsinks, dbq, dbkv = _attn_bwd(qkv, d_att, lse, bias, sinks)
    g["rel_table"] = _rel_bias_bwd(dbias.reshape(16, L * 2 * L), onehot_t).T
    g["sinks"] = dsinks[:, :16]
    g["qkv_b"] = jnp.concatenate([dbq, dbkv], axis=1)
    w_q, w_kv = W["qkv_wt"][:1024], W["qkv_wt"][1024:]
    G["qkv_wt"] = jnp.concatenate([_mm_tn("qkv_dwq", dq, h1), _mm_tn("qkv_dwkv", dkv, h1)], axis=0)
    dh = _mm("qkv_dx", [dq, dkv], [w_q, w_kv], "nn", [F32])[0]
    dx, dsh1b, dsc1b, dnmw1 = _norm_mod_bwd("norm_mix_bwd_1", x2, dh, dx, nmw1, sc1b)
    behind = io["grads1"]({"qkv_wt": G.pop("qkv_wt"), "o_w": G.pop("o_w"), "gate_wt1": dwg1, "up_wt1": dwu1,
                           "down_w1": dwd1})

    dh, dg2, dwg0, dwu0, dwd0 = _ffn_bwd("0", dx, h0f, a0, b0, f0, y0, W["gate_wt0"], W["up_wt0"],
                                         W["down_w0"], g2, after=behind)
    behind = io["grads_ffn0"]({"gate_wt0": dwg0, "up_wt0": dwu0, "down_w0": dwd0})
    dx, dsh2, dsc2, dnfw0 = _norm_mod_bwd("norm_ffn_bwd_0", x1, dh, dx, nfw0, sc2, after=behind)
    dmix, dg1, _ = _gate_bwd("mix_gate_bwd_0", dx, mix0, g1)
    G["out_w"] = jnp.concatenate([_mm_tn("out_dwa", ya, dmix), _mm_tn("out_dwb", yb, dmix)], axis=0)
    dya = _mm("out_dxa", [dmix], [w_oa], "nt", [F32])[0]
    dyb = _mm("out_dxb", [dmix], [w_ob], "nt", [F32])[0]
    du, dv, dws, dbse, g["gmlp_ln_w"], g["gmlp_ln_b"] = _gmlp_bwd(dyb, proj["u"], proj["v"], P["gmlp_ln_w"],
                                                                 P["gmlp_ln_b"], ws, bse)
    g["gmlp_ws"] = dws[None]
    g["gmlp_bs"] = _lane_sum("gmlp_dbs", dbse.reshape(8 * L, 128)).reshape(1, 8, L)
    dz, dxc, ddt, g["ssm_norm_w"], ddsk, dalog, ddtb = _ssd_bwd(dya, y_ssd, proj["z"], xc, proj["dt"], prev,
                                                                dtb, alog, dskl, P["ssm_norm_w"])
    g["d_skip"], g["a_log"], g["dt_bias"] = ddsk[0:1, :16], dalog[:, :16], ddtb[:, :16]
    dxr, dconv_w, g["conv_b"] = _conv_bwd(dxc, pre, proj["xbc"], conv_w)
    g["conv_w"] = dconv_w[None]
    dsegs = {"z": dz, "xbc": dxr, "dt": ddt, "u": du, "v": dv}
    dws_in = {k: _mm_tn(f"in_dw_{k}", d, h0) for k, d in dsegs.items()}
    G["in_wt"] = jnp.concatenate([dws_in["z"], dws_in["xbc"], dws_in["dt"][:16], dws_in["u"], dws_in["v"]], axis=0)
    keys = ["z", "xbc", "dt", "u", "v"]
    dh = _mm("in_dx", [dsegs[k] for k in keys], [segs[k] for k in keys], "nn", [F32])[0]
    dx, dsh1, dsc1, dnmw0 = _norm_mod_bwd("norm_mix_bwd_0", x, dh, dx, nmw0, sc1)

    g["norm_mix_w"] = jnp.concatenate([dnmw0, dnmw1], axis=0)
    g["norm_ffn_w"] = jnp.concatenate([dnfw0, dnfw1], axis=0)
    dmod = jnp.concatenate([jnp.concatenate([dsh1, dsc1, dg1, dsh2, dsc2, dg2], axis=1),
                            jnp.concatenate([dsh1b, dsc1b, dg1b, dsh2b, dsc2b, dg2b], axis=1)], axis=0)
    return sq, dx, dmod, G, g


def _ada_fwd(c_all, ada_w, ada_b):
    n = ada_w.shape[2]
    tn = _col_tile(n, 512)

    def body(c_ref, w_ref, b_ref, o_ref):
        cc = c_ref[...]
        o_ref[...] = lax.dot_general(cc * _sigmoid(cc), w_ref[...], NN, precision=lax.Precision.HIGHEST,
                                     preferred_element_type=F32) + b_ref[...]

    return pl.pallas_call(
        body, name="ada_fwd", grid=(2, n // tn),
        in_specs=[pl.BlockSpec((8, D), lambda l, j: (0, 0)), pl.BlockSpec((None, D, tn), lambda l, j: (l, 0, j)),
                  pl.BlockSpec((None, 1, tn), lambda l, j: (l, 0, j))],
        out_specs=pl.BlockSpec((None, 8, tn), lambda l, j: (l, 0, j)),
        out_shape=jax.ShapeDtypeStruct((2, 8, n), F32), compiler_params=_params(("parallel", "parallel")))(
            c_all, ada_w, ada_b)


def _ada_bwd(c_all, dmod_cols, dmod_all):
    n = dmod_cols.shape[2]
    tn = _col_tile(n, 512)

    def body(c_ref, d_ref, o_ref):
        cc = c_ref[...]
        o_ref[...] = lax.dot_general(cc * _sigmoid(cc), d_ref[...], TN, precision=lax.Precision.HIGHEST,
                                     preferred_element_type=F32)

    dw = pl.pallas_call(
        body, name="ada_dw", grid=(2, n // tn),
        in_specs=[pl.BlockSpec((8, D), lambda l, j: (0, 0)), pl.BlockSpec((None, 8, tn), lambda l, j: (l, 0, j))],
        out_specs=pl.BlockSpec((None, D, tn), lambda l, j: (l, 0, j)),
        out_shape=jax.ShapeDtypeStruct((2, D, n), F32), compiler_params=_params(("parallel", "parallel")))(
            c_all, dmod_cols)

    def sum_body(d_ref, o_ref):
        o_ref[...] = jnp.sum(d_ref[...], axis=0, keepdims=True)

    db = pl.pallas_call(
        sum_body, name="ada_db", grid=(2,),
        in_specs=[pl.BlockSpec((None, 8, 6 * D), lambda l: (l, 0, 0))],
        out_specs=pl.BlockSpec((None, 1, 6 * D), lambda l: (l, 0, 0)),
        out_shape=jax.ShapeDtypeStruct((2, 1, 6 * D), F32), compiler_params=_params(("parallel",)))(dmod_all)
    return dw, db


def _row_tile(rows, cap=512, mult=8):
    best = rows
    for t in range(mult, min(rows, cap) + 1, mult):
        if rows % t == 0:
            best = t
    return best


def _adamw(name, w, g, m, v):
    def fn(w, g, m, v):
        m = ADAM_B1 * m + (1.0 - ADAM_B1) * g
        v = ADAM_B2 * v + (1.0 - ADAM_B2) * (g * g)
        m_hat = m / (1.0 - ADAM_B1 ** ADAM_STEP)
        v_hat = v / (1.0 - ADAM_B2 ** ADAM_STEP)
        return -ADAM_LR * (m_hat / (jnp.sqrt(v_hat) + ADAM_EPS) + ADAM_WD * w), m, v
    cols = w.shape[1]
    return _rowwise(name, fn, [w, g, m, v], [], [(cols, F32)] * 3, tr=_row_tile(w.shape[0]))


def _place():
    return lax.axis_index("x"), lax.axis_index("y"), lax.axis_index("c")


VMEM_SPEC = pl.BlockSpec(memory_space=pltpu.VMEM)


def _allreduce_small(name, buf, after=None):
    rows = buf.shape[0]
    deps = [] if after is None else [after]

    def body(x_ref, *rest):
        o_ref, stage, send_sems, recv_sems = rest[len(deps):]
        x, y, c = _place()
        me = 4 * x + 2 * y + c
        stage[me] = x_ref[...]
        copies = []
        for k in range(1, 8):
            peer = (1 - x if k & 4 else x, 1 - y if k & 2 else y, 1 - c if k & 1 else c)
            cp = pltpu.make_async_remote_copy(src_ref=x_ref, dst_ref=stage.at[me], send_sem=send_sems.at[k - 1],
                                              recv_sem=recv_sems.at[k - 1], device_id=peer, device_id_type=MESH)
            cp.start()
            copies.append(cp)
        for cp in copies:
            cp.wait()
        acc = stage[0]
        for d in range(1, 8):
            acc = acc + stage[d]
        o_ref[...] = acc

    return pl.pallas_call(
        body, name=name, in_specs=[VMEM_SPEC] + [ANY for _ in deps], out_specs=VMEM_SPEC,
        out_shape=jax.ShapeDtypeStruct((rows, 128), F32),
        scratch_shapes=[pltpu.VMEM((8, rows, 128), F32), pltpu.SemaphoreType.DMA((7,)), pltpu.SemaphoreType.DMA((7,))],
        compiler_params=pltpu.CompilerParams(vmem_limit_bytes=_VMEM_LIMIT))(buf, *deps)


OTHER_CHIPS = ((1, 0), (0, 1), (1, 1))


def _allgather_big(wp, after=None):
    rows = wp.shape[0]
    half = rows // 2
    deps = [] if after is None else [after]

    def body(w_ref, *rest):
        o_ref, send_sems, recv_sems, local_sem = rest[len(deps):]
        x, y, c = _place()
        k = 2 * x + y
        mine = pl.ds(pl.multiple_of(c * half, 8), half)
        theirs = pl.ds(pl.multiple_of((1 - c) * half, 8), half)
        local = pltpu.make_async_copy(w_ref, o_ref.at[k], local_sem)
        local.start()
        chips = [(1 - x if fx else x, 1 - y if fy else y) for fx, fy in OTHER_CHIPS]
        idx = [2 * px + py for px, py in chips]
        first = []
        for j, (px, py) in enumerate(chips):
            cp = pltpu.make_async_remote_copy(src_ref=w_ref.at[mine], dst_ref=o_ref.at[k, mine],
                                              send_sem=send_sems.at[j], recv_sem=recv_sems.at[j],
                                              device_id=(px, py, c), device_id_type=MESH)
            cp.start()
            first.append(cp)
        passed = []
        for j in range(3):
            blk = o_ref.at[idx[j], mine]
            pltpu.make_async_remote_copy(src_ref=blk, dst_ref=blk, send_sem=send_sems.at[j], recv_sem=recv_sems.at[j],
                                         device_id=(x, y, c), device_id_type=MESH).wait_recv()
            cp = pltpu.make_async_remote_copy(src_ref=blk, dst_ref=blk, send_sem=send_sems.at[3 + j],
                                              recv_sem=recv_sems.at[3 + j], device_id=(x, y, 1 - c),
                                              device_id_type=MESH)
            cp.start()
            passed.append(cp)
        for j in range(3):
            blk = o_ref.at[idx[j], theirs]
            pltpu.make_async_remote_copy(src_ref=blk, dst_ref=blk, send_sem=send_sems.at[3 + j],
                                         recv_sem=recv_sems.at[3 + j], device_id=(x, y, c),
                                         device_id_type=MESH).wait_recv()
        for cp in first + passed:
            cp.wait_send()
        local.wait()

    return pl.pallas_call(
        body, name="allgather_weights", in_specs=[ANY] + [ANY for _ in deps], out_specs=ANY,
        out_shape=jax.ShapeDtypeStruct((4, rows, 1024), wp.dtype),
        scratch_shapes=[pltpu.SemaphoreType.DMA((6,)), pltpu.SemaphoreType.DMA((6,)), pltpu.SemaphoreType.DMA])(
            wp, *deps)


SIBLING_COLLECTIVE_ID = 6


def _sibling_handshake():
    x, y, c = _place()
    barrier = pltpu.get_barrier_semaphore()
    pl.semaphore_signal(barrier, inc=1, device_id=(x, y, 1 - c), device_id_type=MESH)
    pl.semaphore_wait(barrier, 1)


def _sibling_swap(name, src, halves):
    half = src.shape[-2] // 2
    out_shape = (src.shape[0], half, 1024) if halves else src.shape

    def body(s_ref, o_ref, send_sem, recv_sem):
        x, y, c = _place()
        _sibling_handshake()
        part = s_ref.at[:, pl.ds(pl.multiple_of((1 - c) * half, 8), half)] if halves else s_ref
        cp = pltpu.make_async_remote_copy(src_ref=part, dst_ref=o_ref, send_sem=send_sem, recv_sem=recv_sem,
                                          device_id=(x, y, 1 - c), device_id_type=MESH)
        cp.start()
        cp.wait()

    return pl.pallas_call(
        body, name=name, in_specs=[ANY], out_specs=ANY, out_shape=jax.ShapeDtypeStruct(out_shape, src.dtype),
        scratch_shapes=[pltpu.SemaphoreType.DMA, pltpu.SemaphoreType.DMA],
        compiler_params=pltpu.CompilerParams(collective_id=SIBLING_COLLECTIVE_ID))(src)


HBM = pl.BlockSpec(memory_space=pltpu.HBM)
SEM = pl.BlockSpec(memory_space=pltpu.SEMAPHORE)


def _chip_copies(mode, src_ref, land_ref, send_sems, recv_sems):
    x, y, c = _place()
    k = 2 * x + y
    copies = []
    for j, (fx, fy) in enumerate(OTHER_CHIPS):
        px, py = (1 - x if fx else x), (1 - y if fy else y)
        if mode == "gather":
            half = src_ref.shape[0] // 2
            mine = pl.ds(pl.multiple_of(c * half, 16), half)
            src, dst = src_ref.at[mine], land_ref.at[k, mine]
        else:
            src, dst = src_ref.at[2 * px + py], land_ref.at[k]
        copies.append(pltpu.make_async_remote_copy(src_ref=src, dst_ref=dst, send_sem=send_sems.at[j],
                                                   recv_sem=recv_sems.at[j], device_id=(px, py, c),
                                                   device_id_type=MESH))
    return copies


def _exchange_start(name, collective_id, mode, src, land, after=None):
    deps = [] if after is None else [after]

    def body(s_ref, l_ref, *rest):
        send_sems, recv_sems, s_thru, l_thru, token = rest[len(deps):]
        x, y, c = _place()
        barrier = pltpu.get_barrier_semaphore()
        for fx, fy in OTHER_CHIPS:
            pl.semaphore_signal(barrier, inc=1, device_id=(1 - x if fx else x, 1 - y if fy else y, c),
                                device_id_type=MESH)
        pl.semaphore_wait(barrier, 3)
        for cp in _chip_copies(mode, s_ref, l_ref, send_sems, recv_sems):
            cp.start()
        token[...] = jnp.zeros_like(token)

    return pl.pallas_call(
        body, name=name,
        out_shape=(pltpu.SemaphoreType.DMA((3,)), pltpu.SemaphoreType.DMA((3,)), pltpu.HBM(src.shape, src.dtype),
                   pltpu.HBM(land.shape, land.dtype), jax.ShapeDtypeStruct((8, 128), F32)),
        in_specs=(HBM, HBM) + tuple(ANY for _ in deps), out_specs=(SEM, SEM, HBM, HBM, VMEM_SPEC),
        input_output_aliases={0: 2, 1: 3},
        compiler_params=pltpu.CompilerParams(has_side_effects=pltpu.SideEffectType.DATAFLOW_SIDE_EFFECTING,
                                             collective_id=collective_id))(
            pltpu.with_memory_space_constraint(src, pltpu.HBM), pltpu.with_memory_space_constraint(land, pltpu.HBM),
            *deps)


def _exchange_wait(name, mode, started, after):
    send_sems, recv_sems, s_thru, l_thru, _ = started

    def body(s_ref, l_ref, send_sems, recv_sems, after_ref, s_out, l_out):
        for cp in _chip_copies(mode, s_ref, l_ref, send_sems, recv_sems):
            cp.wait_send()
            cp.wait_recv()

    return pl.pallas_call(
        body, name=name, out_shape=(pltpu.HBM(s_thru.shape, s_thru.dtype), pltpu.HBM(l_thru.shape, l_thru.dtype)),
        in_specs=(HBM, HBM, SEM, SEM, ANY), out_specs=(HBM, HBM), input_output_aliases={0: 0, 1: 1},
        compiler_params=pltpu.CompilerParams(has_side_effects=pltpu.SideEffectType.DATAFLOW_SIDE_EFFECTING))(
            s_thru, l_thru, send_sems, recv_sems, after)


def _allgather_finish(tag, share, land):
    rows = share.shape[0]
    half = rows // 2

    def body(w_ref, l_ref, o_ref, send_sems, recv_sems, local_sem):
        x, y, c = _place()
        k = 2 * x + y
        _sibling_handshake()
        mine = pl.ds(pl.multiple_of(c * half, 16), half)
        theirs = pl.ds(pl.multiple_of((1 - c) * half, 16), half)
        local = pltpu.make_async_copy(w_ref, o_ref.at[k], local_sem)
        local.start()
        passed, landed = [], []
        for j, (fx, fy) in enumerate(OTHER_CHIPS):
            idx = 2 * (1 - x if fx else x) + (1 - y if fy else y)
            passed.append(pltpu.make_async_remote_copy(
                src_ref=o_ref.at[idx, mine], dst_ref=o_ref.at[idx, mine], send_sem=send_sems.at[j],
                recv_sem=recv_sems.at[j], device_id=(x, y, 1 - c), device_id_type=MESH))
            landed.append(pltpu.make_async_remote_copy(
                src_ref=o_ref.at[idx, theirs], dst_ref=o_ref.at[idx, theirs], send_sem=send_sems.at[j],
                recv_sem=recv_sems.at[j], device_id=(x, y, 1 - c), device_id_type=MESH))
        for cp in passed:
            cp.start()
        for cp in landed:
            cp.wait_recv()
        for cp in passed:
            cp.wait_send()
        local.wait()

    return pl.pallas_call(
        body, name="allgather_finish_" + tag, in_specs=[ANY, ANY], out_specs=ANY, input_output_aliases={1: 0},
        out_shape=jax.ShapeDtypeStruct(land.shape, land.dtype),
        scratch_shapes=[pltpu.SemaphoreType.DMA((3,)), pltpu.SemaphoreType.DMA((3,)), pltpu.SemaphoreType.DMA],
        compiler_params=pltpu.CompilerParams(collective_id=SIBLING_COLLECTIVE_ID))(share, land)


def _pair_sum(tag, g, r1, c):
    rows = g.shape[1]
    half = rows // 2
    th = _row_tile(half, 256, 16)
    nblk = half // th

    def body(c_ref, g_ref, r_ref, o_ref, o2_ref):
        o_ref[...] = (g_ref[...] + r_ref[...]).astype(o_ref.dtype)
        o2_ref[...] = o_ref[...]

    spec = pl.BlockSpec((None, th, 1024), lambda k, i, c_ref: (k, i, 0))
    grid_spec = pltpu.PrefetchScalarGridSpec(
        num_scalar_prefetch=1, grid=(4, nblk),
        in_specs=[pl.BlockSpec((None, th, 1024), lambda k, i, c_ref: (k, c_ref[0] * nblk + i, 0)), spec],
        out_specs=[spec, spec])
    return pl.pallas_call(body, name="grad_pair_sum_" + tag, grid_spec=grid_spec,
                          out_shape=[jax.ShapeDtypeStruct((4, half, 1024), BF16)] * 2,
                          compiler_params=_params(("parallel", "parallel")))(c, g, r1)


def _chip_sum(tag, q, after=None):
    half = q.shape[1]
    th = _row_tile(half, 256, 16)
    deps = [] if after is None else [after]

    def body(a, b, c, d, *rest):
        rest[-1][...] = ((a[...].astype(F32) + b[...].astype(F32)) + c[...].astype(F32)) + d[...].astype(F32)

    specs = [pl.BlockSpec((None, th, 1024), functools.partial(lambda i, k: (k, i, 0), k=k)) for k in range(4)]
    return pl.pallas_call(body, name="grad_chip_sum_" + tag, grid=(half // th,), in_specs=specs + [ANY for _ in deps],
                          out_specs=pl.BlockSpec((th, 1024), lambda i: (i, 0)),
                          out_shape=jax.ShapeDtypeStruct((half, 1024), F32),
                          compiler_params=_params(("parallel",)))(q, q, q, q, *deps)


def _join_halves(tag, f, r, c):
    half = f.shape[0]
    th = _row_tile(half, 256)
    nblk = half // th

    def body(c_ref, f_ref, r_ref, o_ref):
        mine = (pl.program_id(0) == c_ref[0])
        o_ref[...] = jnp.where(mine, f_ref[...], r_ref[...])

    spec = pl.BlockSpec((th, 1024), lambda h, i, c_ref: (i, 0))
    grid_spec = pltpu.PrefetchScalarGridSpec(
        num_scalar_prefetch=1, grid=(2, nblk), in_specs=[spec, spec],
        out_specs=pl.BlockSpec((th, 1024), lambda h, i, c_ref: (h * nblk + i, 0)))
    return pl.pallas_call(body, name="grad_join_halves_" + tag, grid_spec=grid_spec,
                          out_shape=jax.ShapeDtypeStruct((2 * half, 1024), F32),
                          compiler_params=_params(("parallel", "parallel")))(c, f, r)


BIG_ARGS = ("in_w_even", "out_w_even", "qkv_w", "o_w", "ffn_gate_w", "ffn_up_w", "ffn_down_w")
def _ffn_pieces(layer):
    return tuple((f"{n}{layer}", 704, 704) for n in ("gate_wt", "up_wt", "down_w"))


IN_SLAB = (("in_wt", 1156, 1184),)
LAYER0_REST_SLAB = (("out_w", 512, 512),) + _ffn_pieces(0)
LAYER1_SLAB = (("qkv_wt", 320, 320), ("o_w", 256, 256)) + _ffn_pieces(1)
FFN0_SLAB = _ffn_pieces(0)
MIXER0_SLAB = (("in_wt", 1156, 1280), ("out_w", 512, 512))


def _slab(pieces, spec):
    parts = []
    for name, rows, room in spec:
        p = pieces[name]
        parts.append(jnp.pad(p, [(0, 0)] * (p.ndim - 2) + [(0, room - rows), (0, 0)]) if room > rows else p)
    return jnp.concatenate(parts, axis=-2) if len(parts) > 1 else parts[0]


def _unslab(slab, spec):
    out, off = {}, 0
    for name, rows, room in spec:
        out[name] = slab[..., off:off + rows, :]
        off += room
    return out


def _share_pieces(w):
    return {"in_wt": w["in_w_even"][0].T, "out_w": w["out_w_even"][0], "qkv_wt": w["qkv_w"][0].T, "o_w": w["o_w"][0],
            "gate_wt0": w["ffn_gate_w"][0].T, "gate_wt1": w["ffn_gate_w"][1].T,
            "up_wt0": w["ffn_up_w"][0].T, "up_wt1": w["ffn_up_w"][1].T,
            "down_w0": w["ffn_down_w"][0], "down_w1": w["ffn_down_w"][1]}


def _pieces_to_shares(p):
    return {"in_w_even": p["in_wt"].T[None], "out_w_even": p["out_w"][None], "qkv_w": p["qkv_wt"].T[None],
            "o_w": p["o_w"][None], "ffn_gate_w": jnp.stack([p["gate_wt0"].T, p["gate_wt1"].T]),
            "ffn_up_w": jnp.stack([p["up_wt0"].T, p["up_wt1"].T]),
            "ffn_down_w": jnp.stack([p["down_w0"], p["down_w1"]])}


def _whole_from_chips(p):
    return {k: v.reshape(-1, D) for k, v in p.items()}


def _chips_from_full(G, spec):
    return _slab({k: v.reshape(4, -1, D) for k, v in G.items()}, spec)


def _pack_small(parts):
    padded = []
    for p in parts:
        p = p.reshape(-1).astype(F32)
        padded.append(jnp.pad(p, (0, (-p.shape[0]) % 1024)))
    return jnp.concatenate(padded).reshape(-1, 128)


def _unpack_small(slab, shapes):
    flat, out, off = slab.reshape(-1), [], 0
    for shp in shapes:
        size = math.prod(shp)
        out.append(flat[off:off + size].reshape(shp))
        off += size + (-size) % 1024
    return out


SMALL = ("ada_b", "norm_mix_w", "norm_ffn_w", "conv_w", "conv_b", "dt_bias", "a_log", "d_skip", "ssm_norm_w",
         "gmlp_ln_w", "gmlp_ln_b", "gmlp_ws", "gmlp_bs", "qkv_b", "o_b", "sinks", "rel_table", "final_norm_w")
SMALL_SPLIT = {"conv_w": 1536, "qkv_b": 1280, "o_b": 1024}
WEIGHTS = ("ada_w", "ada_b", "norm_mix_w", "norm_ffn_w", "in_w_even", "conv_w", "conv_b", "dt_bias", "a_log", "d_skip",
           "ssm_norm_w", "gmlp_ln_w", "gmlp_ln_b", "gmlp_ws", "gmlp_bs", "out_w_even", "qkv_w", "qkv_b", "o_w", "o_b",
           "sinks", "rel_table", "ffn_gate_w", "ffn_up_w", "ffn_down_w", "final_norm_w")


def kernel(x, c, ada_w, ada_b, norm_mix_w, norm_ffn_w, in_w_even, conv_w, conv_b, dt_bias, a_log, d_skip, ssm_norm_w, gmlp_ln_w, gmlp_ln_b, gmlp_ws, gmlp_bs, out_w_even, qkv_w, qkv_b, o_w, o_b, sinks, rel_table, ffn_gate_w, ffn_up_w, ffn_down_w, final_norm_w, loss_target, m_ada_w, m_ada_b, m_norm_mix_w, m_norm_ffn_w, m_in_w_even, m_conv_w, m_conv_b, m_dt_bias, m_a_log, m_d_skip, m_ssm_norm_w, m_gmlp_ln_w, m_gmlp_ln_b, m_gmlp_ws, m_gmlp_bs, m_out_w_even, m_qkv_w, m_qkv_b, m_o_w, m_o_b, m_sinks, m_rel_table, m_ffn_gate_w, m_ffn_up_w, m_ffn_down_w, m_final_norm_w, v_ada_w, v_ada_b, v_norm_mix_w, v_norm_ffn_w, v_in_w_even, v_conv_w, v_conv_b, v_dt_bias, v_a_log, v_d_skip, v_ssm_norm_w, v_gmlp_ln_w, v_gmlp_ln_b, v_gmlp_ws, v_gmlp_bs, v_out_w_even, v_qkv_w, v_qkv_b, v_o_w, v_o_b, v_sinks, v_rel_table, v_ffn_gate_w, v_ffn_up_w, v_ffn_down_w, v_final_norm_w):
    args = dict(locals())
    w = {n: args[n] for n in WEIGHTS}
    m = {n: args["m_" + n] for n in WEIGHTS}
    v = {n: args["v_" + n] for n in WEIGHTS}
    ax, ay, ac = _place()
    me = 4 * ax + 2 * ay + ac
    chip = 2 * ax + ay
    south = (ac == 0).astype(F32)
    c_arr = jnp.reshape(ac, (1,)).astype(jnp.int32)

    c_all = _allreduce_small("gather_cond", lax.dynamic_update_slice(jnp.zeros((8, D), F32), c, (me, 0)).reshape(64, 128))
    c_all = c_all.reshape(8, D)
    n_ada = ada_w.shape[2]
    mod_cols = _ada_fwd(c_all, ada_w, lax.dynamic_slice(ada_b, (0, chip * n_ada), (2, n_ada)).reshape(2, 1, n_ada))
    pieces = [lax.dynamic_update_slice(jnp.zeros((2, 8, 6 * D), F32), mod_cols, (0, 0, chip * n_ada))]
    split_names = list(SMALL_SPLIT)
    for n in split_names:
        full = SMALL_SPLIT[n]
        local = w[n]
        idx = (0,) * (local.ndim - 1) + (chip * local.shape[-1],)
        pieces.append(lax.dynamic_update_slice(jnp.zeros(local.shape[:-1] + (full,), F32), local, idx))
    shapes = [p.shape for p in pieces]
    mod_slab = _allreduce_small("gather_mod", _pack_small(pieces) * south)
    gathered = _unpack_small(mod_slab, shapes)
    mod = lax.dynamic_slice(gathered[0], (0, me, 0), (2, 1, 6 * D)).reshape(2, 6 * D)
    P = {n: w[n] for n in SMALL if n not in SMALL_SPLIT and n != "ada_b"}
    for n, full in zip(split_names, gathered[1:]):
        P[n] = full
    P["final_norm_w"] = final_norm_w.reshape(1, D)

    cast = {k: p.astype(_MXU) for k, p in _share_pieces(w).items()}
    in_slab = _allgather_big(_slab(cast, IN_SLAB), after=mod_slab)
    w_in = _unslab(in_slab, IN_SLAB)["in_wt"].reshape(4 * 1156, D)

    def start_gather(tag, collective_id, spec, after):
        share = _slab(cast, spec)
        return _exchange_start("allgather_start_" + tag, collective_id, "gather", share,
                               lax.empty((4,) + share.shape, share.dtype), after=after)

    def finish_gather(tag, started, spec, after):
        share, land = _exchange_wait("allgather_wait_" + tag, "gather", started, after)
        return _whole_from_chips(_unslab(_allgather_finish(tag, share, land), spec))

    gather0 = start_gather("0", 1, LAYER0_REST_SLAB, in_slab)
    gather1 = start_gather("1", 2, LAYER1_SLAB, gather0[4])

    def start_reduce(tag, collective_id, G, spec, after=None):
        gp = _chips_from_full(G, spec)
        p, q = _pair_sum(tag, gp, _sibling_swap("grad_pair_exchange_" + tag, gp, True), c_arr)
        return _exchange_start("grad_exchange_start_" + tag, collective_id, "scatter", p, q, after=after)

    def finish_reduce(tag, started, spec, after, behind=None):
        q = _exchange_wait("grad_exchange_wait_" + tag, "scatter", started, after)[1]
        fin = _chip_sum(tag, q, after=behind)
        total = _join_halves(tag, fin, _sibling_swap("grad_final_exchange_" + tag, fin, False), c_arr)
        return _unslab(total, spec)

    reduces = {}

    def grads1(G1):
        reduces["1"] = start_reduce("1", 3, G1, LAYER1_SLAB)
        return reduces["1"][4]

    def grads_ffn0(G):
        reduces["f"] = start_reduce("f", 4, G, FFN0_SLAB)
        return reduces["f"][4]

    io = {"start": gather1[4],
          "weights0": lambda after: finish_gather("0", gather0, LAYER0_REST_SLAB, after),
          "weights1": lambda after: finish_gather("1", gather1, LAYER1_SLAB, after),
          "grads1": grads1, "grads_ffn0": grads_ffn0}
    sq, grad_x, dmod, G0, g = _local_step(x[0], loss_target[0], mod, w_in, P, io)
    loss = lax.psum(0.5 * sq[0, 0] / D, ("x", "y", "c"))

    g["final_norm_w"] = g["final_norm_w"].reshape(D)
    small_names = [n for n in SMALL if n != "ada_b"]
    pieces = [lax.dynamic_update_slice(jnp.zeros((2, 8, 6 * D), F32), dmod.reshape(2, 1, 6 * D), (0, me, 0))]
    pieces += [g[n] for n in small_names]
    shapes = [p.shape for p in pieces]
    small_slab = _allreduce_small("allreduce_small_grads", _pack_small(pieces))
    reduces["m"] = start_reduce("m", 5, G0, MIXER0_SLAB, after=small_slab)
    shares = finish_reduce("1", reduces["1"], LAYER1_SLAB, grad_x, behind=reduces["m"][4])
    shares.update(finish_reduce("f", reduces["f"], FFN0_SLAB, grad_x, behind=reduces["m"][4]))
    reduced = _unpack_small(small_slab, shapes)
    dmod_all = reduced[0]
    grads = dict(zip(small_names, reduced[1:]))
    for n in split_names:
        full = grads[n]
        size = w[n].shape[-1]
        grads[n] = lax.dynamic_slice(full, (0,) * (full.ndim - 1) + (chip * size,), full.shape[:-1] + (size,))
    grads = {n: grads[n].reshape(w[n].shape) for n in small_names}
    dw_ada, db_ada = _ada_bwd(c_all, lax.dynamic_slice(dmod_all, (0, 0, chip * n_ada), (2, 8, n_ada)), dmod_all)
    grads["ada_w"], grads["ada_b"] = dw_ada, db_ada.reshape(2, 6 * D)

    delta, new_m, new_v = {}, {}, {}

    def update(n):
        cols = w[n].shape[-1]
        d_, m_, v_ = _adamw("adamw_" + n, w[n].reshape(-1, cols), grads[n].reshape(-1, cols), m[n].reshape(-1, cols),
                            v[n].reshape(-1, cols))
        delta[n], new_m[n], new_v[n] = d_.reshape(w[n].shape), m_.reshape(w[n].shape), v_.reshape(w[n].shape)

    update("ada_w")
    shapes = [w[n].shape for n in SMALL]
    packed = [_pack_small([t[n] for n in SMALL]) for t in (w, grads, m, v)]
    outs = _adamw("adamw_small", *packed)
    for dst, slab in zip((delta, new_m, new_v), outs):
        for n, t in zip(SMALL, _unpack_small(slab, shapes)):
            dst[n] = t
    shares.update(finish_reduce("m", reduces["m"], MIXER0_SLAB, outs[0]))
    grads.update(_pieces_to_shares(shares))
    for n in BIG_ARGS:
        update(n)
    return (loss, grad_x[None], *[grads[n] for n in WEIGHTS], *[delta[n] for n in WEIGHTS],
            *[new_m[n] for n in WEIGHTS], *[new_v[n] for n in WEIGHTS])
```

```python
import functools
import math

import numpy as np
import jax
import jax.numpy as jnp
from jax import lax
from jax.experimental import pallas as pl
from jax.experimental.pallas import tpu as pltpu

F32 = jnp.float32
BF16 = jnp.bfloat16
_MXU = jnp.bfloat16
_VMEM_LIMIT = 56 * 1024 * 1024
D = 1024
L = 128
NSTATE = 128
EPS = 1e-6
NEG_INF = -1e30
FFN = 2816
ADAM_LR, ADAM_B1, ADAM_B2, ADAM_EPS, ADAM_WD, ADAM_STEP = 0.001, 0.9, 0.999, 1e-08, 0.01, 10
MESH = pl.DeviceIdType.MESH
ANY = pl.BlockSpec(memory_space=pl.ANY)

NN = (((1,), (0,)), ((), ()))
NT = (((1,), (1,)), ((), ()))
TN = (((0,), (0,)), ((), ()))


def _dot(a, b, dn=NN):
    return lax.dot_general(a.astype(_MXU), b.astype(_MXU), dn, preferred_element_type=F32)


def _params(sem=None):
    return pltpu.CompilerParams(dimension_semantics=sem, vmem_limit_bytes=_VMEM_LIMIT)


def _sigmoid(x):
    return 1.0 / (1.0 + jnp.exp(-x))


def _softplus(x):
    return jnp.maximum(x, 0.0) + jnp.log(1.0 + jnp.exp(-jnp.abs(x)))


def _gelu(x):
    return 0.5 * x * (1.0 + lax.erf(x * (2.0 ** -0.5)))


def _gelu_grad(x):
    return 0.5 * (1.0 + lax.erf(x * (2.0 ** -0.5))) + x * jnp.exp(-0.5 * x * x) * (1.0 / math.sqrt(2.0 * math.pi))


def _silu_grad(a):
    sg = _sigmoid(a)
    return sg * (1.0 + a * (1.0 - sg))


def _rowwise(name, fn, rows, vecs, out_rows, out_accs=(), tr=512, after=None):
    S = rows[0].shape[0]
    tr = min(tr, S)
    assert S % tr == 0
    nr, nv, no, na = len(rows), len(vecs), len(out_rows), len(out_accs)
    deps = [] if after is None else [after]

    def body(*refs):
        ins, outs = refs[:nr + nv], refs[nr + nv + len(deps):]
        res = fn(*[r[...] for r in ins])
        if not isinstance(res, (tuple, list)):
            res = (res,)
        for k in range(no):
            outs[k][...] = res[k].astype(outs[k].dtype)
        if na:
            @pl.when(pl.program_id(0) == 0)
            def _():
                for k in range(na):
                    outs[no + k][...] = jnp.zeros_like(outs[no + k])
            for k in range(na):
                outs[no + k][...] += res[no + k]

    in_specs = [pl.BlockSpec((tr, a.shape[1]), lambda i: (i, 0)) for a in rows]
    in_specs += [pl.BlockSpec(v.shape, lambda i: (0, 0)) for v in vecs] + [ANY for _ in deps]
    out_specs = [pl.BlockSpec((tr, c), lambda i: (i, 0)) for c, _ in out_rows]
    out_specs += [pl.BlockSpec(s, lambda i: (0, 0)) for s in out_accs]
    out_shape = [jax.ShapeDtypeStruct((S, c), dt) for c, dt in out_rows]
    out_shape += [jax.ShapeDtypeStruct(s, F32) for s in out_accs]
    return pl.pallas_call(body, name=name, grid=(S // tr,), in_specs=in_specs, out_specs=out_specs,
                          out_shape=out_shape, compiler_params=_params(("arbitrary",)))(*rows, *vecs, *deps)


def _col_tile(n, cap):
    if n <= cap or n % 128:
        return n
    best = 128
    for t in range(128, cap + 1, 128):
        if n % t == 0:
            best = t
    return best


def _mm(name, As, Bs, mode, outs, epi=None, groups=None, extras=(), vecs=(), tm=512, tn_cap=1536):
    M = As[0].shape[0]
    N = Bs[0].shape[1] if mode == "nn" else Bs[0].shape[0]
    tm = min(tm, M)
    tn = _col_tile(N, tn_cap)
    assert M % tm == 0 and N % tn == 0
    npair = len(As)
    groups = groups or [0] * npair
    ng = max(groups) + 1
    nx, nv = len(extras), len(vecs)
    dn = NN if mode == "nn" else NT

    def body(*refs):
        a_refs, b_refs = refs[:npair], refs[npair:2 * npair]
        x_refs = refs[2 * npair:2 * npair + nx]
        v_refs = refs[2 * npair + nx:2 * npair + nx + nv]
        o_refs = refs[2 * npair + nx + nv:]
        accs = [None] * ng
        for k in range(npair):
            d = _dot(a_refs[k][...], b_refs[k][...], dn)
            accs[groups[k]] = d if accs[groups[k]] is None else accs[groups[k]] + d
        args = accs + [x[...] for x in x_refs] + [v[...] for v in v_refs]
        res = epi(*args) if epi is not None else tuple(accs)
        if not isinstance(res, (tuple, list)):
            res = (res,)
        for o, r in zip(o_refs, res):
            o[...] = r.astype(o.dtype)

    in_specs = [pl.BlockSpec((tm, a.shape[1]), lambda i, j: (i, 0)) for a in As]
    if mode == "nn":
        in_specs += [pl.BlockSpec((b.shape[0], tn), lambda i, j: (0, j)) for b in Bs]
    else:
        in_specs += [pl.BlockSpec((tn, b.shape[1]), lambda i, j: (j, 0)) for b in Bs]
    in_specs += [pl.BlockSpec((tm, tn), lambda i, j: (i, j)) for _ in extras]
    in_specs += [pl.BlockSpec((1, tn), lambda i, j: (0, j)) for _ in vecs]
    out_specs = [pl.BlockSpec((tm, tn), lambda i, j: (i, j)) for _ in outs]
    out_shape = [jax.ShapeDtypeStruct((M, N), dt) for dt in outs]
    return pl.pallas_call(body, name=name, grid=(M // tm, N // tn), in_specs=in_specs, out_specs=out_specs,
                          out_shape=out_shape, compiler_params=_params(("parallel", "parallel")))(
                              *As, *Bs, *extras, *vecs)


def _mm_tn(name, A, B, tk=512, t2_cap=1536):
    S, K1 = A.shape
    N2 = B.shape[1]
    tk = min(tk, S)
    t2 = _col_tile(N2, t2_cap)
    assert S % tk == 0 and N2 % t2 == 0

    def body(a_ref, b_ref, o_ref):
        @pl.when(pl.program_id(1) == 0)
        def _():
            o_ref[...] = jnp.zeros_like(o_ref)
        o_ref[...] += _dot(a_ref[...], b_ref[...], TN)

    return pl.pallas_call(
        body, name=name, grid=(N2 // t2, S // tk),
        in_specs=[pl.BlockSpec((tk, K1), lambda j, k: (k, 0)), pl.BlockSpec((tk, t2), lambda j, k: (k, j))],
        out_specs=pl.BlockSpec((K1, t2), lambda j, k: (0, j)),
        out_shape=jax.ShapeDtypeStruct((K1, N2), F32),
        compiler_params=_params(("parallel", "arbitrary")))(A, B)


def _norm_mod_fwd(name, x, nw, sc, sh, after=None):
    def fn(x, nw, sc, sh):
        rstd = lax.rsqrt(jnp.mean(x * x, axis=-1, keepdims=True) + EPS)
        return (x * rstd * nw) * (1.0 + sc) + sh
    return _rowwise(name, fn, [x], [nw, sc, sh], [(D, BF16)], after=after)[0]


def _norm_mod_bwd(name, x, dh, dres, nw, sc, after=None):
    def fn(x, dh, dres, nw, sc):
        rstd = lax.rsqrt(jnp.mean(x * x, axis=-1, keepdims=True) + EPS)
        xh = x * rstd
        dn = dh * (1.0 + sc)
        dxh = dn * nw
        dx = rstd * (dxh - xh * jnp.mean(dxh * xh, axis=-1, keepdims=True))
        return (dres + dx, jnp.sum(dh, axis=0, keepdims=True), jnp.sum(dh * (xh * nw), axis=0, keepdims=True),
                jnp.sum(dn * xh, axis=0, keepdims=True))
    return _rowwise(name, fn, [x, dh, dres], [nw, sc], [(D, F32)], [(1, D)] * 3, after=after)


def _gate_bwd(name, dx, y, g, after=None):
    def fn(dx, y, g):
        dy = dx * g
        return dy, jnp.sum(dx * y, axis=0, keepdims=True), jnp.sum(dy, axis=0, keepdims=True)
    return _rowwise(name, fn, [dx, y], [g], [(D, BF16)], [(1, D)] * 2, after=after)


def _loss_head(x, tgt, fw):
    def fn(x, tgt, fw):
        rstd = lax.rsqrt(jnp.mean(x * x, axis=-1, keepdims=True) + EPS)
        xh = x * rstd
        err = xh * fw - tgt
        dout = err * (1.0 / D)
        dxh = dout * fw
        dx = rstd * (dxh - xh * jnp.mean(dxh * xh, axis=-1, keepdims=True))
        sq = jnp.sum(jnp.sum(err * err, axis=1, keepdims=True), axis=0, keepdims=True)
        return dx, sq, jnp.sum(dout * xh, axis=0, keepdims=True)
    return _rowwise("loss_head", fn, [x, tgt], [fw], [(D, F32)], [(1, 1), (1, D)])


def _ffn_fwd(tag, h, wg, wu, wd, x, g2):
    def act(a, b):
        return a, b, a * _sigmoid(a) * b
    a, b, f = _mm(f"ffn_up_{tag}", [h, h], [wg, wu], "nt", [F32, F32, BF16], epi=act, groups=[0, 1], tn_cap=1408)

    def res(y, x, g):
        return y, x + g * y
    y, xo = _mm(f"ffn_down_{tag}", [f], [wd], "nn", [F32, F32], epi=res, extras=[x], vecs=[g2])
    return a, b, f, y, xo


def _ffn_bwd(tag, dx, h, a, b, f, y, wg, wu, wd, g2, after=None):
    dy, dg2, _ = _gate_bwd(f"ffn_gate_bwd_{tag}", dx, y, g2, after=after)

    def act_bwd(df, a, b):
        return df * b * _silu_grad(a), df * (a * _sigmoid(a))
    da, db = _mm(f"ffn_dact_{tag}", [dy], [wd], "nt", [BF16, BF16], epi=act_bwd, extras=[a, b], tn_cap=1408)
    dwd = _mm_tn(f"ffn_dwd_{tag}", f, dy)
    dwg = _mm_tn(f"ffn_dwg_{tag}", da, h)
    dwu = _mm_tn(f"ffn_dwu_{tag}", db, h)
    dh = _mm(f"ffn_dh_{tag}", [da, db], [wg, wu], "nn", [F32])[0]
    return dh, dg2, dwg, dwu, dwd


def _conv_fwd(xr, w, b, tb=512):
    S, C = xr.shape
    tb = min(tb, S)

    def body(x_ref, halo_ref, w_ref, b_ref, pre_ref, out_ref):
        i = pl.program_id(0)
        halo = jnp.where(i > 0, halo_ref[...], 0.0)
        xe = jnp.concatenate([halo, x_ref[...]], axis=0)
        pre = w_ref[3:4, :] * x_ref[...] + b_ref[...]
        for j in (1, 2, 3):
            pre = pre + w_ref[3 - j:4 - j, :] * pltpu.roll(xe, j, axis=0)[8:, :]
        pre_ref[...] = pre
        out_ref[...] = pre * _sigmoid(pre)

    return pl.pallas_call(
        body, name="conv_fwd", grid=(S // tb,),
        in_specs=[pl.BlockSpec((tb, C), lambda i: (i, 0)),
                  pl.BlockSpec((8, C), lambda i: (jnp.maximum(i * (tb // 8) - 1, 0), 0)),
                  pl.BlockSpec((4, C), lambda i: (0, 0)), pl.BlockSpec((1, C), lambda i: (0, 0))],
        out_specs=[pl.BlockSpec((tb, C), lambda i: (i, 0))] * 2,
        out_shape=[jax.ShapeDtypeStruct((S, C), F32)] * 2,
        compiler_params=_params(("parallel",)))(xr, xr, w, b)


def _conv_bwd(dxc, pre, xr, w, tb=512):
    S, C = xr.shape
    tb = min(tb, S)
    nblk = S // tb

    def body(d_ref, p_ref, dn_ref, pn_ref, x_ref, xh_ref, w_ref, dx_ref, dw_ref, db_ref):
        i = pl.program_id(0)

        @pl.when(i == 0)
        def _():
            dw_ref[...] = jnp.zeros_like(dw_ref)
            db_ref[...] = jnp.zeros_like(db_ref)

        dpre = d_ref[...] * _silu_grad(p_ref[...])
        dnext = jnp.where(i < nblk - 1, dn_ref[...] * _silu_grad(pn_ref[...]), 0.0)
        pe = jnp.concatenate([dpre, dnext], axis=0)
        dx = w_ref[3:4, :] * dpre
        for j in (1, 2, 3):
            dx = dx + w_ref[3 - j:4 - j, :] * pltpu.roll(pe, tb + 8 - j, axis=0)[:tb, :]
        dx_ref[...] = dx.astype(dx_ref.dtype)
        halo = jnp.where(i > 0, xh_ref[...], 0.0)
        xe = jnp.concatenate([halo, x_ref[...]], axis=0)
        for k in range(3):
            dw_ref[k:k + 1, :] += jnp.sum(dpre * pltpu.roll(xe, 3 - k, axis=0)[8:, :], axis=0, keepdims=True)
        dw_ref[3:4, :] += jnp.sum(dpre * x_ref[...], axis=0, keepdims=True)
        db_ref[...] += jnp.sum(dpre, axis=0, keepdims=True)

    blk = pl.BlockSpec((tb, C), lambda i: (i, 0))
    nxt = pl.BlockSpec((8, C), lambda i: (jnp.minimum((i + 1) * (tb // 8), S // 8 - 1), 0))
    prv = pl.BlockSpec((8, C), lambda i: (jnp.maximum(i * (tb // 8) - 1, 0), 0))
    return pl.pallas_call(
        body, name="conv_bwd", grid=(nblk,),
        in_specs=[blk, blk, nxt, nxt, blk, prv, pl.BlockSpec((4, C), lambda i: (0, 0))],
        out_specs=[blk, pl.BlockSpec((4, C), lambda i: (0, 0)), pl.BlockSpec((1, C), lambda i: (0, 0))],
        out_shape=[jax.ShapeDtypeStruct((S, C), BF16), jax.ShapeDtypeStruct((4, C), F32),
                   jax.ShapeDtypeStruct((1, C), F32)],
        compiler_params=_params(("arbitrary",)))(dxc, pre, dxc, pre, xr, xr, w)


def _iota(shape, dim):
    return lax.broadcasted_iota(jnp.int32, shape, dim)


def _colsel(m, lane, h):
    return jnp.sum(jnp.where(lane == h, m, 0.0), axis=1, keepdims=True)


def _cumsum_rows(v):
    r = _iota(v.shape, 0)
    k = 1
    while k < v.shape[0]:
        v = v + jnp.where(r >= k, pltpu.roll(v, k, axis=0), 0.0)
        k *= 2
    return v


def _suffix_sum_rows(v):
    n = v.shape[0]
    r = _iota(v.shape, 0)
    k = 1
    while k < n:
        v = v + jnp.where(r < n - k, pltpu.roll(v, n - k, axis=0), 0.0)
        k *= 2
    return v


def _ssd_fwd(xc, dtr, z, dtb, alog, dskl, nw):
    S = xc.shape[0]
    nc = S // L

    def body(xc_ref, dtr_ref, z_ref, dtb_ref, alog_ref, dsk_ref, nw_ref, ya_ref, y_ref, prev_ref,
             st_ref, cum_ref, cumT_ref):
        i = pl.program_id(0)

        @pl.when(i == 0)
        def _():
            st_ref[...] = jnp.zeros_like(st_ref)

        lane = _iota((L, 128), 1)
        lane1 = _iota((1, 128), 1)
        lo = lane < 64
        lo1 = lane1 < 64
        tril = _iota((L, L), 0) >= _iota((L, L), 1)
        dt = _softplus(dtr_ref[...] + dtb_ref[...])
        a_neg = -jnp.exp(alog_ref[...])
        cum = _cumsum_rows(dt * a_neg)
        cum_ref[...] = cum
        cumT_ref[...] = cum.T
        last_all = cum_ref[L - 1:L, :]
        prev_t = st_ref[...]
        prev_ref[0] = prev_t
        for g in range(2):
            bg = xc_ref[:, 1024 + g * 128:1152 + g * 128]
            cg = xc_ref[:, 1280 + g * 128:1408 + g * 128]
            gmat = _dot(cg, bg, NT)
            yoff = _dot(cg, prev_t[:, g * 512:(g + 1) * 512])
            bg_t = bg.T
            for jp in range(4):
                j = g * 4 + jp
                sl = slice(j * 128, (j + 1) * 128)
                xp = xc_ref[:, sl]
                cc = [_colsel(cum, lane, 2 * j), _colsel(cum, lane, 2 * j + 1)]
                cum_l = jnp.where(lo, cc[0], cc[1])
                dt_l = jnp.where(lo, _colsel(dt, lane, 2 * j), _colsel(dt, lane, 2 * j + 1))
                last_l = jnp.where(lo1, _colsel(last_all, lane1, 2 * j), _colsel(last_all, lane1, 2 * j + 1))
                xd = xp * dt_l
                ys = []
                for hh in range(2):
                    seg = cc[hh] - cumT_ref[2 * j + hh:2 * j + hh + 1, :]
                    dm = jnp.where(tril, jnp.exp(jnp.where(tril, seg, 0.0)), 0.0)
                    ys.append(_dot(gmat * dm, xd))
                y_ref[:, sl] = (jnp.where(lo, ys[0], ys[1]) + jnp.exp(cum_l) * yoff[:, jp * 128:(jp + 1) * 128]
                                + dsk_ref[:, sl] * xp)
                st_ref[:, sl] = prev_t[:, sl] * jnp.exp(last_l) + _dot(bg_t, xd * jnp.exp(last_l - cum_l))
        for g in range(2):
            sl = slice(g * 512, (g + 1) * 512)
            zz = z_ref[:, sl]
            yg = y_ref[:, sl] * (zz * _sigmoid(zz))
            rstd = lax.rsqrt(jnp.mean(yg * yg, axis=-1, keepdims=True) + EPS)
            ya_ref[:, sl] = (yg * rstd * nw_ref[:, sl]).astype(ya_ref.dtype)

    blk = lambda c: pl.BlockSpec((L, c), lambda i: (i, 0))
    vec = lambda c: pl.BlockSpec((1, c), lambda i: (0, 0))
    return pl.pallas_call(
        body, name="ssd_fwd", grid=(nc,),
        in_specs=[blk(1536), blk(128), blk(1024), vec(128), vec(128), vec(1024), vec(1024)],
        out_specs=[blk(1024), blk(1024), pl.BlockSpec((1, NSTATE, 1024), lambda i: (i, 0, 0))],
        out_shape=[jax.ShapeDtypeStruct((S, 1024), BF16), jax.ShapeDtypeStruct((S, 1024), F32),
                   jax.ShapeDtypeStruct((nc, NSTATE, 1024), F32)],
        scratch_shapes=[pltpu.VMEM((NSTATE, 1024), F32), pltpu.VMEM((L, 128), F32), pltpu.VMEM((L, 128), F32)],
        compiler_params=_params(("arbitrary",)))(xc, dtr, z, dtb, alog, dskl, nw)


def _ssd_bwd(dya, y, z, xc, dtr, prev, dtb, alog, dskl, nw):
    S = xc.shape[0]
    nc = S // L

    def body(dya_ref, y_ref, z_ref, xc_ref, dtr_ref, prev_ref, dtb_ref, alog_ref, dsk_ref, nw_ref,
             dz_ref, dxc_ref, ddtr_ref, dnw_ref, ddsk_ref, dalog_ref, ddtb_ref,
             dst_ref, cum_ref, cumT_ref, dy_ref, dskacc_ref):
        i = pl.program_id(0)

        @pl.when(i == 0)
        def _():
            dst_ref[...] = jnp.zeros_like(dst_ref)
            dskacc_ref[...] = jnp.zeros_like(dskacc_ref)
            dnw_ref[...] = jnp.zeros_like(dnw_ref)
            dalog_ref[...] = jnp.zeros_like(dalog_ref)
            ddtb_ref[...] = jnp.zeros_like(ddtb_ref)

        lane = _iota((L, 128), 1)
        lane1 = _iota((1, 128), 1)
        lo = lane < 64
        lo1 = lane1 < 64
        r2, c2 = _iota((L, L), 0), _iota((L, L), 1)
        tril = r2 >= c2
        triu = r2 <= c2
        is_last = _iota((L, 1), 0) == L - 1

        for g in range(2):
            sl = slice(g * 512, (g + 1) * 512)
            zz = z_ref[:, sl]
            sg = _sigmoid(zz)
            zg = zz * sg
            yv = y_ref[:, sl]
            yg = yv * zg
            rstd = lax.rsqrt(jnp.mean(yg * yg, axis=-1, keepdims=True) + EPS)
            xh = yg * rstd
            d_out = dya_ref[:, sl]
            dnw_ref[:, sl] += jnp.sum(d_out * xh, axis=0, keepdims=True)
            dyn = d_out * nw_ref[:, sl]
            dyg = rstd * (dyn - xh * jnp.mean(dyn * xh, axis=-1, keepdims=True))
            dy_ref[:, sl] = dyg * zg
            dz_ref[:, sl] = (dyg * yv * (sg * (1.0 + zz * (1.0 - sg)))).astype(dz_ref.dtype)

        dtin = dtr_ref[...] + dtb_ref[...]
        dt = _softplus(dtin)
        a_neg = -jnp.exp(alog_ref[...])
        cum = _cumsum_rows(dt * a_neg)
        cum_ref[...] = cum
        cumT_ref[...] = cum.T
        last_all = cum_ref[L - 1:L, :]
        prev_t = prev_ref[0]
        dn_t = dst_ref[...]
        dcum = jnp.zeros((L, 128), F32)
        ddt = jnp.zeros((L, 128), F32)
        for g in range(2):
            gsl = slice(g * 512, (g + 1) * 512)
            bg = xc_ref[:, 1024 + g * 128:1152 + g * 128]
            cg = xc_ref[:, 1280 + g * 128:1408 + g * 128]
            gmat = _dot(cg, bg, NT)
            gmat_t = _dot(bg, cg, NT)
            pg = prev_t[:, gsl]
            zmat = _dot(cg, pg)
            dgm = jnp.zeros((L, L), F32)
            dgm_t = jnp.zeros((L, L), F32)
            db_acc = jnp.zeros((L, NSTATE), F32)
            dz_parts, cd_parts = [], []
            for jp in range(4):
                j = g * 4 + jp
                sl = slice(j * 128, (j + 1) * 128)
                xp = xc_ref[:, sl]
                dyp = dy_ref[:, sl]
                cc = [_colsel(cum, lane, 2 * j), _colsel(cum, lane, 2 * j + 1)]
                lc = [_colsel(last_all, lane1, 2 * j), _colsel(last_all, lane1, 2 * j + 1)]
                cum_l = jnp.where(lo, cc[0], cc[1])
                dt_l = jnp.where(lo, _colsel(dt, lane, 2 * j), _colsel(dt, lane, 2 * j + 1))
                last_l = jnp.where(lo1, lc[0], lc[1])
                e_l = jnp.exp(cum_l)
                dte_l = jnp.exp(last_l - cum_l)
                cd_l = jnp.exp(last_l)
                cd_parts.append(cd_l)
                xd = xp * dt_l
                dskacc_ref[:, sl] += jnp.sum(dyp * xp, axis=0, keepdims=True)
                dxp = dsk_ref[:, sl] * dyp
                t = dyp * (e_l * zmat[:, jp * 128:(jp + 1) * 128])
                dcc = [jnp.sum(jnp.where(lo, t, 0.0), axis=1, keepdims=True),
                       jnp.sum(jnp.where(lo, 0.0, t), axis=1, keepdims=True)]
                dz_parts.append(e_l * dyp)
                dnp_ = dn_t[:, sl]
                t2 = jnp.sum(dnp_ * prev_t[:, sl], axis=0, keepdims=True)
                dcd = [jnp.sum(jnp.where(lo1, t2, 0.0), axis=1, keepdims=True),
                       jnp.sum(jnp.where(lo1, 0.0, t2), axis=1, keepdims=True)]
                wm = _dot(bg, dnp_)
                dxd = wm * dte_l
                t3 = wm * xd
                ddte = [jnp.sum(jnp.where(lo, t3, 0.0), axis=1, keepdims=True),
                        jnp.sum(jnp.where(lo, 0.0, t3), axis=1, keepdims=True)]
                db_acc = db_acc + _dot(xd * dte_l, dnp_, NT)
                for hh in range(2):
                    h = 2 * j + hh
                    half = lo if hh == 0 else jnp.logical_not(lo)
                    row = cumT_ref[h:h + 1, :]
                    dm = jnp.where(tril, jnp.exp(jnp.where(tril, cc[hh] - row, 0.0)), 0.0)
                    dm_t = jnp.where(triu, jnp.exp(jnp.where(triu, row - cc[hh], 0.0)), 0.0)
                    m = gmat * dm
                    m_t = gmat_t * dm_t
                    dym = jnp.where(half, dyp, 0.0)
                    d_m = _dot(dym, xd, NT)
                    d_mt = _dot(xd, dym, NT)
                    dxd = dxd + _dot(m_t, dym)
                    dcc[hh] = dcc[hh] + jnp.sum(d_m * m, axis=1, keepdims=True) - jnp.sum(d_mt * m_t, axis=1, keepdims=True)
                    dgm = dgm + d_m * dm
                    dgm_t = dgm_t + d_mt * dm_t
                    dte_c = jnp.exp(lc[hh] - cc[hh])
                    dcc[hh] = dcc[hh] - ddte[hh] * dte_c
                    endc = dcd[hh] * jnp.exp(lc[hh]) + jnp.sum(ddte[hh] * dte_c, axis=0, keepdims=True)
                    dcc[hh] = dcc[hh] + jnp.where(is_last, endc, 0.0)
                    dcum = jnp.where(lane == h, dcc[hh], dcum)
                dxc_ref[:, sl] = dxp + dxd * dt_l
                t4 = dxd * xp
                ddt = jnp.where(lane == 2 * j, jnp.sum(jnp.where(lo, t4, 0.0), axis=1, keepdims=True), ddt)
                ddt = jnp.where(lane == 2 * j + 1, jnp.sum(jnp.where(lo, 0.0, t4), axis=1, keepdims=True), ddt)
            dzg = jnp.concatenate(dz_parts, axis=1)
            dst_ref[:, gsl] = dn_t[:, gsl] * jnp.concatenate(cd_parts, axis=1) + _dot(cg.T, dzg)
            dxc_ref[:, 1280 + g * 128:1408 + g * 128] = _dot(dgm, bg) + _dot(dzg, pg, NT)
            dxc_ref[:, 1024 + g * 128:1152 + g * 128] = _dot(dgm_t, cg) + db_acc
        dla = _suffix_sum_rows(dcum)
        ddt = ddt + dla * a_neg
        dalog_ref[...] += jnp.sum(dla * dt, axis=0, keepdims=True) * a_neg
        ddtr = jnp.where(lane < 16, ddt * _sigmoid(dtin), 0.0)
        ddtr_ref[...] = ddtr.astype(ddtr_ref.dtype)
        ddtb_ref[...] += jnp.sum(ddtr, axis=0, keepdims=True)

        @pl.when(i == nc - 1)
        def _():
            seg = (_iota((1024, 128), 0) // 64 == _iota((1024, 128), 1)).astype(F32)
            acc8 = jnp.broadcast_to(dskacc_ref[...], (8, 1024))
            ddsk_ref[...] = lax.dot_general(acc8, seg, NN, precision=lax.Precision.HIGHEST,
                                            preferred_element_type=F32)

    rev = lambda c: pl.BlockSpec((L, c), lambda i: (nc - 1 - i, 0))
    vec = lambda c: pl.BlockSpec((1, c), lambda i: (0, 0))
    return pl.pallas_call(
        body, name="ssd_bwd", grid=(nc,),
        in_specs=[rev(1024), rev(1024), rev(1024), rev(1536), rev(128),
                  pl.BlockSpec((1, NSTATE, 1024), lambda i: (nc - 1 - i, 0, 0)),
                  vec(128), vec(128), vec(1024), vec(1024)],
        out_specs=[rev(1024), rev(1536), rev(128), vec(1024), pl.BlockSpec((8, 128), lambda i: (0, 0)),
                   vec(128), vec(128)],
        out_shape=[jax.ShapeDtypeStruct((S, 1024), BF16), jax.ShapeDtypeStruct((S, 1536), F32),
                   jax.ShapeDtypeStruct((S, 128), BF16), jax.ShapeDtypeStruct((1, 1024), F32),
                   jax.ShapeDtypeStruct((8, 128), F32), jax.ShapeDtypeStruct((1, 128), F32),
                   jax.ShapeDtypeStruct((1, 128), F32)],
        scratch_shapes=[pltpu.VMEM((NSTATE, 1024), F32), pltpu.VMEM((L, 128), F32), pltpu.VMEM((L, 128), F32),
                        pltpu.VMEM((L, 1024), F32), pltpu.VMEM((1, 1024), F32)],
        compiler_params=_params(("arbitrary",)))(dya, y, z, xc, dtr, prev, dtb, alog, dskl, nw)


def _layer_norm_parts(vg):
    mu = jnp.mean(vg, axis=-1, keepdims=True)
    vc = vg - mu
    rstd = lax.rsqrt(jnp.mean(vc * vc, axis=-1, keepdims=True) + EPS)
    return vc * rstd, rstd


def _gmlp_fwd(u, v, lnw, lnb, ws, bse, tb=512):
    S = u.shape[0]
    tb = min(tb, S)

    def body(u_ref, v_ref, lnw_ref, lnb_ref, ws_ref, bse_ref, o_ref, vn_ref):
        tril = _iota((L, L), 0) >= _iota((L, L), 1)
        xh, _ = _layer_norm_parts(_gelu(v_ref[...]))
        vn_ref[...] = xh * lnw_ref[...] + lnb_ref[...]
        for g in range(8):
            w = jnp.where(tril, ws_ref[g], 0.0)
            gs = slice(g * 128, (g + 1) * 128)
            for ch in range(tb // L):
                rs = slice(ch * L, (ch + 1) * L)
                sv = _dot(w, vn_ref[rs, gs]) + bse_ref[g]
                o_ref[rs, gs] = (_gelu(u_ref[rs, gs]) * sv).astype(o_ref.dtype)

    blk = pl.BlockSpec((tb, 1024), lambda i: (i, 0))
    vec = pl.BlockSpec((1, 1024), lambda i: (0, 0))
    cube = pl.BlockSpec((8, L, 128), lambda i: (0, 0, 0))
    return pl.pallas_call(
        body, name="gmlp_fwd", grid=(S // tb,), in_specs=[blk, blk, vec, vec, cube, cube], out_specs=blk,
        out_shape=jax.ShapeDtypeStruct((S, 1024), BF16), scratch_shapes=[pltpu.VMEM((tb, 1024), F32)],
        compiler_params=_params(("parallel",)))(u, v, lnw, lnb, ws, bse)


def _gmlp_bwd(dyb, u, v, lnw, lnb, ws, bse, tb=512):
    S = u.shape[0]
    tb = min(tb, S)

    def body(d_ref, u_ref, v_ref, lnw_ref, lnb_ref, ws_ref, bse_ref,
             du_ref, dv_ref, dws_ref, dbse_ref, dlnw_ref, dlnb_ref, vn_ref, dvn_ref):
        @pl.when(pl.program_id(0) == 0)
        def _():
            dws_ref[...] = jnp.zeros_like(dws_ref)
            dbse_ref[...] = jnp.zeros_like(dbse_ref)
            dlnw_ref[...] = jnp.zeros_like(dlnw_ref)
            dlnb_ref[...] = jnp.zeros_like(dlnb_ref)

        tril = _iota((L, L), 0) >= _iota((L, L), 1)
        vv = v_ref[...]
        xh, rstd = _layer_norm_parts(_gelu(vv))
        vn_ref[...] = xh * lnw_ref[...] + lnb_ref[...]
        for g in range(8):
            w = jnp.where(tril, ws_ref[g], 0.0)
            w_t = w.T
            gs = slice(g * 128, (g + 1) * 128)
            dw = jnp.zeros((L, L), F32)
            dbs = jnp.zeros((L, 128), F32)
            for ch in range(tb // L):
                rs = slice(ch * L, (ch + 1) * L)
                vn = vn_ref[rs, gs]
                sv = _dot(w, vn) + bse_ref[g]
                uu = u_ref[rs, gs]
                dd = d_ref[rs, gs]
                du_ref[rs, gs] = (dd * sv * _gelu_grad(uu)).astype(du_ref.dtype)
                dsv = dd * _gelu(uu)
                dw = dw + _dot(dsv, vn, NT)
                dbs = dbs + dsv
                dvn_ref[rs, gs] = _dot(w_t, dsv)
            dws_ref[g] += jnp.where(tril, dw, 0.0)
            dbse_ref[g] += dbs
        dvn = dvn_ref[...]
        dlnw_ref[...] += jnp.sum(dvn * xh, axis=0, keepdims=True)
        dlnb_ref[...] += jnp.sum(dvn, axis=0, keepdims=True)
        dxh = dvn * lnw_ref[...]
        dvg = rstd * (dxh - jnp.mean(dxh, axis=-1, keepdims=True) - xh * jnp.mean(dxh * xh, axis=-1, keepdims=True))
        dv_ref[...] = (dvg * _gelu_grad(vv)).astype(dv_ref.dtype)

    blk = pl.BlockSpec((tb, 1024), lambda i: (i, 0))
    vec = pl.BlockSpec((1, 1024), lambda i: (0, 0))
    cube = pl.BlockSpec((8, L, 128), lambda i: (0, 0, 0))
    return pl.pallas_call(
        body, name="gmlp_bwd", grid=(S // tb,), in_specs=[blk, blk, blk, vec, vec, cube, cube],
        out_specs=[blk, blk, cube, cube, vec, vec],
        out_shape=[jax.ShapeDtypeStruct((S, 1024), BF16), jax.ShapeDtypeStruct((S, 1024), BF16),
                   jax.ShapeDtypeStruct((8, L, 128), F32), jax.ShapeDtypeStruct((8, L, 128), F32),
                   jax.ShapeDtypeStruct((1, 1024), F32), jax.ShapeDtypeStruct((1, 1024), F32)],
        scratch_shapes=[pltpu.VMEM((tb, 1024), F32), pltpu.VMEM((tb, 1024), F32)],
        compiler_params=_params(("arbitrary",)))(dyb, u, v, lnw, lnb, ws, bse)


def _lane_sum(name, a):
    def body(a_ref, o_ref):
        o_ref[...] = jnp.sum(a_ref[...], axis=1, keepdims=True)
    return pl.pallas_call(body, name=name, out_shape=jax.ShapeDtypeStruct((a.shape[0], 1), F32))(a)


def _bucket_onehot_t():
    qi = np.arange(L)[:, None]
    sj = np.arange(2 * L)[None, :]
    dist = np.maximum(qi + L - sj, 0)
    log_ratio = (np.log(np.maximum(dist, 1).astype(np.float32) / np.float32(16)) / np.float32(math.log(128 / 16)))
    large = 16 + (log_ratio.astype(np.float32) * np.float32(16)).astype(np.int32)
    bucket = np.where(dist < 16, dist, np.minimum(large, 31)).reshape(-1)
    return (np.arange(32)[:, None] == bucket[None, :]).astype(np.float32)


def _rel_bias(table_t, onehot_t):
    def body(t_ref, oh_ref, o_ref):
        o_ref[...] = lax.dot_general(t_ref[...], oh_ref[...], NN, precision=lax.Precision.HIGHEST,
                                     preferred_element_type=F32)
    return pl.pallas_call(body, name="rel_bias", out_shape=jax.ShapeDtypeStruct((16, L * 2 * L), F32),
                          compiler_params=_params())(table_t, onehot_t)


def _rel_bias_bwd(dbias, onehot_t):
    def body(d_ref, oh_ref, o_ref):
        o_ref[...] = lax.dot_general(d_ref[...], oh_ref[...], NT, precision=lax.Precision.HIGHEST,
                                     preferred_element_type=F32)
    return pl.pallas_call(body, name="rel_bias_bwd", out_shape=jax.ShapeDtypeStruct((16, 32), F32),
                          compiler_params=_params())(dbias, onehot_t)


def _band(kp, kc, lo):
    kk = jnp.concatenate([kp, kc], axis=0)
    kr = pltpu.roll(kk, 64, axis=1)
    return [jnp.where(lo, kk, kr), jnp.where(lo, kr, kk)]


def _attn_mask(i):
    qi, sj = _iota((L, 2 * L), 0), _iota((L, 2 * L), 1)
    rel = qi + L - sj
    return (rel >= 0) & (rel < L) & ((sj >= L) | (i > 0))


SMEM = pl.BlockSpec(memory_space=pltpu.SMEM)


def _attn_fwd(qkv, bias, sinks):
    S = qkv.shape[0]
    nb = S // L
    scale = 64 ** -0.5

    def body(sink_ref, q_ref, kc_ref, vc_ref, kp_ref, vp_ref, bias_ref, o_ref, lse_ref):
        i = pl.program_id(0)
        lane = _iota((L, 128), 1)
        lo = lane < 64
        lo2 = _iota((2 * L, 128), 1) < 64
        mask = _attn_mask(i)
        kd = _band(kp_ref[...], kc_ref[...], lo2)
        vd = _band(vp_ref[...], vc_ref[...], lo2)
        lse = jnp.zeros((L, 128), F32)
        for pr in range(8):
            sl = slice(pr * 128, (pr + 1) * 128)
            qp = q_ref[:, sl]
            j = pr // 4
            outs = []
            for hh in range(2):
                h = 2 * pr + hh
                qm = jnp.where(lo if hh == 0 else jnp.logical_not(lo), qp, 0.0)
                lg = jnp.where(mask, _dot(qm, kd[j], NT) * scale + bias_ref[h], NEG_INF)
                s = sink_ref[h]
                m = jnp.maximum(jnp.max(lg, axis=1, keepdims=True), s)
                p = jnp.where(mask, jnp.exp(lg - m), 0.0)
                den = jnp.sum(p, axis=1, keepdims=True) + jnp.exp(s - m)
                outs.append(_dot(p / den, vd[j]))
                lse = jnp.where(lane == h, m + jnp.log(den), lse)
            o_ref[:, sl] = jnp.where(lo, outs[0], outs[1]).astype(o_ref.dtype)
        lse_ref[...] = lse

    prev = lambda col: pl.BlockSpec((L, 128), lambda i: (jnp.maximum(i - 1, 0), col))
    cur = lambda col: pl.BlockSpec((L, 128), lambda i: (i, col))
    return pl.pallas_call(
        body, name="attn_fwd", grid=(nb,),
        in_specs=[SMEM, pl.BlockSpec((L, 1024), lambda i: (i, 0)), cur(8), cur(9), prev(8), prev(9),
                  pl.BlockSpec((16, L, 2 * L), lambda i: (0, 0, 0))],
        out_specs=[pl.BlockSpec((L, 1024), lambda i: (i, 0)), pl.BlockSpec((L, 128), lambda i: (i, 0))],
        out_shape=[jax.ShapeDtypeStruct((S, 1024), BF16), jax.ShapeDtypeStruct((S, 128), F32)],
        compiler_params=_params(("parallel",)))(sinks, qkv, qkv, qkv, qkv, qkv, bias)


def _attn_bwd(qkv, d_o, lse, bias, sinks):
    S = qkv.shape[0]
    nb = S // L
    scale = 64 ** -0.5

    def body(sink_ref, q_ref, kc_ref, vc_ref, kp_ref, vp_ref, do_ref, lse_ref, bias_ref,
             dq_ref, dkv_ref, dbias_ref, dsink_ref, dbq_ref, dbkv_ref, carry_ref):
        i = pl.program_id(0)

        @pl.when(i == 0)
        def _():
            dbias_ref[...] = jnp.zeros_like(dbias_ref)
            dsink_ref[...] = jnp.zeros_like(dsink_ref)
            dbq_ref[...] = jnp.zeros_like(dbq_ref)
            dbkv_ref[...] = jnp.zeros_like(dbkv_ref)
            carry_ref[...] = jnp.zeros_like(carry_ref)

        @pl.when(i < nb)
        def _():
            lane = _iota((L, 128), 1)
            lane1 = _iota((1, 128), 1)
            lo = lane < 64
            lo2 = _iota((2 * L, 128), 1) < 64
            mask = _attn_mask(i)
            kd = _band(kp_ref[...], kc_ref[...], lo2)
            vd = _band(vp_ref[...], vc_ref[...], lo2)
            lse_all = lse_ref[...]
            acc_k = [jnp.zeros((2 * L, 128), F32), jnp.zeros((2 * L, 128), F32)]
            acc_v = [jnp.zeros((2 * L, 128), F32), jnp.zeros((2 * L, 128), F32)]
            dsink = jnp.zeros((1, 128), F32)
            for pr in range(8):
                sl = slice(pr * 128, (pr + 1) * 128)
                qp = q_ref[:, sl]
                dop = do_ref[:, sl]
                j = pr // 4
                dqs = []
                for hh in range(2):
                    h = 2 * pr + hh
                    half = lo if hh == 0 else jnp.logical_not(lo)
                    qm = jnp.where(half, qp, 0.0)
                    dom = jnp.where(half, dop, 0.0)
                    lse_h = _colsel(lse_all, lane, h)
                    lg = _dot(qm, kd[j], NT) * scale + bias_ref[h]
                    p = jnp.where(mask, jnp.exp(jnp.where(mask, lg, NEG_INF) - lse_h), 0.0)
                    dp = _dot(dom, vd[j], NT)
                    delta = jnp.sum(p * dp, axis=1, keepdims=True)
                    ds = p * (dp - delta)
                    dbias_ref[h] += ds
                    ds_sink = jnp.sum(-jnp.exp(sink_ref[h] - lse_h) * delta, axis=0, keepdims=True)
                    dsink = dsink + jnp.where(lane1 == h, ds_sink, 0.0)
                    dss = ds * scale
                    dqs.append(_dot(dss, kd[j]))
                    acc_k[j] = acc_k[j] + _dot(dss, qm, TN)
                    acc_v[j] = acc_v[j] + _dot(p, dom, TN)
                dq = jnp.where(lo, dqs[0], dqs[1])
                dq_ref[:, sl] = dq.astype(dq_ref.dtype)
                dbq_ref[:, sl] += jnp.sum(dq, axis=0, keepdims=True)
            dsink_ref[...] += dsink
            tot_k = [a + pltpu.roll(a, 64, axis=1) for a in acc_k]
            tot_v = [a + pltpu.roll(a, 64, axis=1) for a in acc_v]
            dkv = jnp.concatenate([jnp.where(lo2, tot_k[0], tot_k[1]), jnp.where(lo2, tot_v[0], tot_v[1])], axis=1)
            dbkv_ref[...] += jnp.sum(dkv, axis=0, keepdims=True)
            dkv_ref[...] = (carry_ref[...] + dkv[:L, :]).astype(dkv_ref.dtype)
            carry_ref[...] = dkv[L:, :]

        @pl.when(i == nb)
        def _():
            dkv_ref[...] = carry_ref[...].astype(dkv_ref.dtype)

    c = lambda i: jnp.minimum(i, nb - 1)
    prev = lambda col: pl.BlockSpec((L, 128), lambda i: (jnp.maximum(c(i) - 1, 0), col))
    cur = lambda col: pl.BlockSpec((L, 128), lambda i: (c(i), col))
    row = lambda w: pl.BlockSpec((L, w), lambda i: (c(i), 0))
    cube = pl.BlockSpec((16, L, 2 * L), lambda i: (0, 0, 0))
    vec = lambda w: pl.BlockSpec((1, w), lambda i: (0, 0))
    return pl.pallas_call(
        body, name="attn_bwd", grid=(nb + 1,),
        in_specs=[SMEM, row(1024), cur(8), cur(9), prev(8), prev(9), row(1024), row(128), cube],
        out_specs=[row(1024), pl.BlockSpec((L, 256), lambda i: (jnp.maximum(i - 1, 0), 0)), cube,
                   vec(128), vec(1024), vec(256)],
        out_shape=[jax.ShapeDtypeStruct((S, 1024), BF16), jax.ShapeDtypeStruct((S, 256), BF16),
                   jax.ShapeDtypeStruct((16, L, 2 * L), F32), jax.ShapeDtypeStruct((1, 128), F32),
                   jax.ShapeDtypeStruct((1, 1024), F32), jax.ShapeDtypeStruct((1, 256), F32)],
        scratch_shapes=[pltpu.VMEM((L, 256), F32)],
        compiler_params=_params(("arbitrary",)))(sinks, qkv, qkv, qkv, qkv, qkv, d_o, lse, bias)


def _pad_lanes(a, n=128):
    return jnp.pad(a, ((0, 0), (0, n - a.shape[1])))


def _local_step(x, tgt, mod, w_in, P, io):
    md = [[mod[l:l + 1, k * D:(k + 1) * D] for k in range(6)] for l in range(2)]
    G, g = {}, {}

    sh1, sc1, g1, sh2, sc2, g2 = md[0]
    nmw0, nfw0 = P["norm_mix_w"][0:1], P["norm_ffn_w"][0:1]
    h0 = _norm_mod_fwd("norm_mix_0", x, nmw0, sc1, sh1, after=io["start"])
    segs = {"z": w_in[0:1024], "xbc": w_in[1024:2560], "dt": jnp.pad(w_in[2560:2576], ((0, 112), (0, 0))),
            "u": w_in[2576:3600], "v": w_in[3600:4624]}
    proj = {k: _mm(f"in_proj_{k}", [h0], [w], "nt", [F32])[0] for k, w in segs.items()}
    conv_w, conv_b = P["conv_w"][0], P["conv_b"]
    pre, xc = _conv_fwd(proj["xbc"], conv_w, conv_b)
    dtb, alog = _pad_lanes(P["dt_bias"]), _pad_lanes(P["a_log"])
    dskl = jnp.repeat(P["d_skip"], 64, axis=1)
    ya, y_ssd, prev = _ssd_fwd(xc, proj["dt"], proj["z"], dtb, alog, dskl, P["ssm_norm_w"])
    ws = P["gmlp_ws"][0]
    bse = jnp.broadcast_to(P["gmlp_bs"][0][:, :, None], (8, L, 128))
    yb = _gmlp_fwd(proj["u"], proj["v"], P["gmlp_ln_w"], P["gmlp_ln_b"], ws, bse)
    W = dict(io["weights0"]((ya, yb)))
    w_oa, w_ob = W["out_w"][:1024], W["out_w"][1024:]

    def res(y, x, gate):
        return y, x + gate * y
    mix0, x1 = _mm("out_proj_0", [ya, yb], [w_oa, w_ob], "nn", [F32, F32], epi=res, extras=[x], vecs=[g1])
    h0f = _norm_mod_fwd("norm_ffn_0", x1, nfw0, sc2, sh2)
    a0, b0, f0, y0, x2 = _ffn_fwd("0", h0f, W["gate_wt0"], W["up_wt0"], W["down_w0"], x1, g2)

    sh1b, sc1b, g1b, sh2b, sc2b, g2b = md[1]
    nmw1, nfw1 = P["norm_mix_w"][1:2], P["norm_ffn_w"][1:2]
    W.update(io["weights1"](x2))
    h1 = _norm_mod_fwd("norm_mix_1", x2, nmw1, sc1b, sh1b)
    qkv = _mm("qkv_proj", [h1], [W["qkv_wt"]], "nt", [F32], epi=lambda acc, b: acc + b, vecs=[P["qkv_b"]])[0]
    onehot_t = jnp.asarray(_bucket_onehot_t())
    bias = _rel_bias(P["rel_table"].T, onehot_t).reshape(16, L, 2 * L)
    sinks = P["sinks"].reshape(16)
    att, lse = _attn_fwd(qkv, bias, sinks)

    def res_b(y, x, gate, b):
        y = y + b
        return y, x + gate * y
    mix1, x3 = _mm("o_proj", [att], [W["o_w"]], "nn", [F32, F32], epi=res_b, extras=[x2], vecs=[g1b, P["o_b"]])
    h1f = _norm_mod_fwd("norm_ffn_1", x3, nfw1, sc2b, sh2b)
    a1, b1, f1, y1, x4 = _ffn_fwd("1", h1f, W["gate_wt1"], W["up_wt1"], W["down_w1"], x3, g2b)

    dx, sq, g["final_norm_w"] = _loss_head(x4, tgt, P["final_norm_w"])

    dh, dg2b, dwg1, dwu1, dwd1 = _ffn_bwd("1", dx, h1f, a1, b1, f1, y1, W["gate_wt1"], W["up_wt1"],
                                          W["down_w1"], g2b)
    dx, dsh2b, dsc2b, dnfw1 = _norm_mod_bwd("norm_ffn_bwd_1", x3, dh, dx, nfw1, sc2b)
    dmix, dg1b, g["o_b"] = _gate_bwd("mix_gate_bwd_1", dx, mix1, g1b)
    G["o_w"] = _mm_tn("o_dw", att, dmix)
    d_att = _mm("o_dx", [dmix], [W["o_w"]], "nt", [F32])[0]
    dq, dkv, dbias, dsinks, dbq, dbkv = _attn_bwd(qkv, d_att, lse, bias, sinks)
    g["rel_table"] = _rel_bias_bwd(dbias.reshape(16, L * 2 * L), onehot_t).T
    g["sinks"] = dsinks[:, :16]
    g["qkv_b"] = jnp.concatenate([dbq, dbkv], axis=1)
    w_q, w_kv = W["qkv_wt"][:1024], W["qkv_wt"][1024:]
    G["qkv_wt"] = jnp.concatenate([_mm_tn("qkv_dwq", dq, h1), _mm_tn("qkv_dwkv", dkv, h1)], axis=0)
    dh = _mm("qkv_dx", [dq, dkv], [w_q, w_kv], "nn", [F32])[0]
    dx, dsh1b, dsc1b, dnmw1 = _norm_mod_bwd("norm_mix_bwd_1", x2, dh, dx, nmw1, sc1b)
    behind = io["grads1"]({"qkv_wt": G.pop("qkv_wt"), "o_w": G.pop("o_w"), "gate_wt1": dwg1, "up_wt1": dwu1,
                           "down_w1": dwd1})

    dh, dg2, dwg0, dwu0, dwd0 = _ffn_bwd("0", dx, h0f, a0, b0, f0, y0, W["gate_wt0"], W["up_wt0"],
                                         W["down_w0"], g2, after=behind)
    behind = io["grads_ffn0"]({"gate_wt0": dwg0, "up_wt0": dwu0, "down_w0": dwd0})
    dx, dsh2, dsc2, dnfw0 = _norm_mod_bwd("norm_ffn_bwd_0", x1, dh, dx, nfw0, sc2, after=behind)
    dmix, dg1, _ = _gate_bwd("mix_gate_bwd_0", dx, mix0, g1)
    G["out_w"] = jnp.concatenate([_mm_tn("out_dwa", ya, dmix), _mm_tn("out_dwb", yb, dmix)], axis=0)
    dya = _mm("out_dxa", [dmix], [w_oa], "nt", [F32])[0]
    dyb = _mm("out_dxb", [dmix], [w_ob], "nt", [F32])[0]
    du, dv, dws, dbse, g["gmlp_ln_w"], g["gmlp_ln_b"] = _gmlp_bwd(dyb, proj["u"], proj["v"], P["gmlp_ln_w"],
                                                                 P["gmlp_ln_b"], ws, bse)
    g["gmlp_ws"] = dws[None]
    g["gmlp_bs"] = _lane_sum("gmlp_dbs", dbse.reshape(8 * L, 128)).reshape(1, 8, L)
    dz, dxc, ddt, g["ssm_norm_w"], ddsk, dalog, ddtb = _ssd_bwd(dya, y_ssd, proj["z"], xc, proj["dt"], prev,
                                                                dtb, alog, dskl, P["ssm_norm_w"])
    g["d_skip"], g["a_log"], g["dt_bias"] = ddsk[0:1, :16], dalog[:, :16], ddtb[:, :16]
    dxr, dconv_w, g["conv_b"] = _conv_bwd(dxc, pre, proj["xbc"], conv_w)
    g["conv_w"] = dconv_w[None]
    dsegs = {"z": dz, "xbc": dxr, "dt": ddt, "u": du, "v": dv}
    dws_in = {k: _mm_tn(f"in_dw_{k}", d, h0) for k, d in dsegs.items()}
    G["in_wt"] = jnp.concatenate([dws_in["z"], dws_in["xbc"], dws_in["dt"][:16], dws_in["u"], dws_in["v"]], axis=0)
    keys = ["z", "xbc", "dt", "u", "v"]
    dh = _mm("in_dx", [dsegs[k] for k in keys], [segs[k] for k in keys], "nn", [F32])[0]
    dx, dsh1, dsc1, dnmw0 = _norm_mod_bwd("norm_mix_bwd_0", x, dh, dx, nmw0, sc1)

    g["norm_mix_w"] = jnp.concatenate([dnmw0, dnmw1], axis=0)
    g["norm_ffn_w"] = jnp.concatenate([dnfw0, dnfw1], axis=0)
    dmod = jnp.concatenate([jnp.concatenate([dsh1, dsc1, dg1, dsh2, dsc2, dg2], axis=1),
                            jnp.concatenate([dsh1b, dsc1b, dg1b, dsh2b, dsc2b, dg2b], axis=1)], axis=0)
    return sq, dx, dmod, G, g


def _ada_fwd(c_all, ada_w, ada_b):
    n = ada_w.shape[2]
    tn = _col_tile(n, 512)

    def body(c_ref, w_ref, b_ref, o_ref):
        cc = c_ref[...]
        o_ref[...] = lax.dot_general(cc * _sigmoid(cc), w_ref[...], NN, precision=lax.Precision.HIGHEST,
                                     preferred_element_type=F32) + b_ref[...]

    return pl.pallas_call(
        body, name="ada_fwd", grid=(2, n // tn),
        in_specs=[pl.BlockSpec((8, D), lambda l, j: (0, 0)), pl.BlockSpec((None, D, tn), lambda l, j: (l, 0, j)),
                  pl.BlockSpec((None, 1, tn), lambda l, j: (l, 0, j))],
        out_specs=pl.BlockSpec((None, 8, tn), lambda l, j: (l, 0, j)),
        out_shape=jax.ShapeDtypeStruct((2, 8, n), F32), compiler_params=_params(("parallel", "parallel")))(
            c_all, ada_w, ada_b)


def _ada_bwd(c_all, dmod_cols, dmod_all):
    n = dmod_cols.shape[2]
    tn = _col_tile(n, 512)

    def body(c_ref, d_ref, o_ref):
        cc = c_ref[...]
        o_ref[...] = lax.dot_general(cc * _sigmoid(cc), d_ref[...], TN, precision=lax.Precision.HIGHEST,
                                     preferred_element_type=F32)

    dw = pl.pallas_call(
        body, name="ada_dw", grid=(2, n // tn),
        in_specs=[pl.BlockSpec((8, D), lambda l, j: (0, 0)), pl.BlockSpec((None, 8, tn), lambda l, j: (l, 0, j))],
        out_specs=pl.BlockSpec((None, D, tn), lambda l, j: (l, 0, j)),
        out_shape=jax.ShapeDtypeStruct((2, D, n), F32), compiler_params=_params(("parallel", "parallel")))(
            c_all, dmod_cols)

    def sum_body(d_ref, o_ref):
        o_ref[...] = jnp.sum(d_ref[...], axis=0, keepdims=True)

    db = pl.pallas_call(
        sum_body, name="ada_db", grid=(2,),
        in_specs=[pl.BlockSpec((None, 8, 6 * D), lambda l: (l, 0, 0))],
        out_specs=pl.BlockSpec((None, 1, 6 * D), lambda l: (l, 0, 0)),
        out_shape=jax.ShapeDtypeStruct((2, 1, 6 * D), F32), compiler_params=_params(("parallel",)))(dmod_all)
    return dw, db


def _row_tile(rows, cap=512, mult=8):
    best = rows
    for t in range(mult, min(rows, cap) + 1, mult):
        if rows % t == 0:
            best = t
    return best


def _adamw(name, w, g, m, v):
    def fn(w, g, m, v):
        m = ADAM_B1 * m + (1.0 - ADAM_B1) * g
        v = ADAM_B2 * v + (1.0 - ADAM_B2) * (g * g)
        m_hat = m / (1.0 - ADAM_B1 ** ADAM_STEP)
        v_hat = v / (1.0 - ADAM_B2 ** ADAM_STEP)
        return -ADAM_LR * (m_hat / (jnp.sqrt(v_hat) + ADAM_EPS) + ADAM_WD * w), m, v
    cols = w.shape[1]
    return _rowwise(name, fn, [w, g, m, v], [], [(cols, F32)] * 3, tr=_row_tile(w.shape[0]))


def _place():
    return lax.axis_index("x"), lax.axis_index("y"), lax.axis_index("c")


VMEM_SPEC = pl.BlockSpec(memory_space=pltpu.VMEM)


def _allreduce_small(name, buf, after=None):
    rows = buf.shape[0]
    deps = [] if after is None else [after]

    def body(x_ref, *rest):
        o_ref, stage, send_sems, recv_sems = rest[len(deps):]
        x, y, c = _place()
        me = 4 * x + 2 * y + c
        stage[me] = x_ref[...]
        copies = []
        for k in range(1, 8):
            peer = (1 - x if k & 4 else x, 1 - y if k & 2 else y, 1 - c if k & 1 else c)
            cp = pltpu.make_async_remote_copy(src_ref=x_ref, dst_ref=stage.at[me], send_sem=send_sems.at[k - 1],
                                              recv_sem=recv_sems.at[k - 1], device_id=peer, device_id_type=MESH)
            cp.start()
            copies.append(cp)
        for cp in copies:
            cp.wait()
        acc = stage[0]
        for d in range(1, 8):
            acc = acc + stage[d]
        o_ref[...] = acc

    return pl.pallas_call(
        body, name=name, in_specs=[VMEM_SPEC] + [ANY for _ in deps], out_specs=VMEM_SPEC,
        out_shape=jax.ShapeDtypeStruct((rows, 128), F32),
        scratch_shapes=[pltpu.VMEM((8, rows, 128), F32), pltpu.SemaphoreType.DMA((7,)), pltpu.SemaphoreType.DMA((7,))],
        compiler_params=pltpu.CompilerParams(vmem_limit_bytes=_VMEM_LIMIT))(buf, *deps)


OTHER_CHIPS = ((1, 0), (0, 1), (1, 1))


def _allgather_big(wp, after=None):
    rows = wp.shape[0]
    half = rows // 2
    deps = [] if after is None else [after]

    def body(w_ref, *rest):
        o_ref, send_sems, recv_sems, local_sem = rest[len(deps):]
        x, y, c = _place()
        k = 2 * x + y
        mine = pl.ds(pl.multiple_of(c * half, 8), half)
        theirs = pl.ds(pl.multiple_of((1 - c) * half, 8), half)
        local = pltpu.make_async_copy(w_ref.at[mine], o_ref.at[k, mine], local_sem)
        local.start()
        first = []
        for j, (fx, fy) in enumerate(OTHER_CHIPS):
            cp = pltpu.make_async_remote_copy(src_ref=w_ref.at[mine], dst_ref=o_ref.at[k, mine],
                                              send_sem=send_sems.at[j], recv_sem=recv_sems.at[j],
                                              device_id=(1 - x if fx else x, 1 - y if fy else y, c),
                                              device_id_type=MESH)
            cp.start()
            first.append(cp)
        for cp in first:
            cp.wait_recv()
        local.wait()
        swap = pltpu.make_async_remote_copy(src_ref=o_ref.at[:, mine], dst_ref=o_ref.at[:, mine],
                                            send_sem=send_sems.at[3], recv_sem=recv_sems.at[3],
                                            device_id=(x, y, 1 - c), device_id_type=MESH)
        swap.start()
        swap.wait()
        for cp in first:
            cp.wait_send()

    return pl.pallas_call(
        body, name="allgather_weights", in_specs=[ANY] + [ANY for _ in deps], out_specs=ANY,
        out_shape=jax.ShapeDtypeStruct((4, rows, 1024), wp.dtype),
        scratch_shapes=[pltpu.SemaphoreType.DMA((4,)), pltpu.SemaphoreType.DMA((4,)), pltpu.SemaphoreType.DMA])(
            wp, *deps)


SIBLING_COLLECTIVE_ID = 6


def _sibling_handshake():
    x, y, c = _place()
    barrier = pltpu.get_barrier_semaphore()
    pl.semaphore_signal(barrier, inc=1, device_id=(x, y, 1 - c), device_id_type=MESH)
    pl.semaphore_wait(barrier, 1)


def _sibling_swap(name, src, halves):
    half = src.shape[-2] // 2
    out_shape = (src.shape[0], half, 1024) if halves else src.shape

    def body(s_ref, o_ref, send_sem, recv_sem):
        x, y, c = _place()
        _sibling_handshake()
        part = s_ref.at[:, pl.ds(pl.multiple_of((1 - c) * half, 8), half)] if halves else s_ref
        cp = pltpu.make_async_remote_copy(src_ref=part, dst_ref=o_ref, send_sem=send_sem, recv_sem=recv_sem,
                                          device_id=(x, y, 1 - c), device_id_type=MESH)
        cp.start()
        cp.wait()

    return pl.pallas_call(
        body, name=name, in_specs=[ANY], out_specs=ANY, out_shape=jax.ShapeDtypeStruct(out_shape, src.dtype),
        scratch_shapes=[pltpu.SemaphoreType.DMA, pltpu.SemaphoreType.DMA],
        compiler_params=pltpu.CompilerParams(collective_id=SIBLING_COLLECTIVE_ID))(src)


HBM = pl.BlockSpec(memory_space=pltpu.HBM)
SEM = pl.BlockSpec(memory_space=pltpu.SEMAPHORE)


def _chip_copies(mode, src_ref, land_ref, send_sems, recv_sems):
    x, y, c = _place()
    k = 2 * x + y
    copies = []
    for j, (fx, fy) in enumerate(OTHER_CHIPS):
        px, py = (1 - x if fx else x), (1 - y if fy else y)
        if mode == "gather":
            half = src_ref.shape[0] // 2
            mine = pl.ds(pl.multiple_of(c * half, 16), half)
            src, dst = src_ref.at[mine], land_ref.at[k, mine]
        else:
            src, dst = src_ref.at[2 * px + py], land_ref.at[k]
        copies.append(pltpu.make_async_remote_copy(src_ref=src, dst_ref=dst, send_sem=send_sems.at[j],
                                                   recv_sem=recv_sems.at[j], device_id=(px, py, c),
                                                   device_id_type=MESH))
    return copies


def _exchange_start(name, collective_id, mode, src, land, after=None):
    deps = [] if after is None else [after]

    def body(s_ref, l_ref, *rest):
        send_sems, recv_sems, s_thru, l_thru, token = rest[len(deps):]
        x, y, c = _place()
        barrier = pltpu.get_barrier_semaphore()
        for fx, fy in OTHER_CHIPS:
            pl.semaphore_signal(barrier, inc=1, device_id=(1 - x if fx else x, 1 - y if fy else y, c),
                                device_id_type=MESH)
        pl.semaphore_wait(barrier, 3)
        for cp in _chip_copies(mode, s_ref, l_ref, send_sems, recv_sems):
            cp.start()
        token[...] = jnp.zeros_like(token)

    return pl.pallas_call(
        body, name=name,
        out_shape=(pltpu.SemaphoreType.DMA((3,)), pltpu.SemaphoreType.DMA((3,)), pltpu.HBM(src.shape, src.dtype),
                   pltpu.HBM(land.shape, land.dtype), jax.ShapeDtypeStruct((8, 128), F32)),
        in_specs=(HBM, HBM) + tuple(ANY for _ in deps), out_specs=(SEM, SEM, HBM, HBM, VMEM_SPEC),
        input_output_aliases={0: 2, 1: 3},
        compiler_params=pltpu.CompilerParams(has_side_effects=pltpu.SideEffectType.DATAFLOW_SIDE_EFFECTING,
                                             collective_id=collective_id))(
            pltpu.with_memory_space_constraint(src, pltpu.HBM), pltpu.with_memory_space_constraint(land, pltpu.HBM),
            *deps)


def _exchange_wait(name, mode, started, after):
    send_sems, recv_sems, s_thru, l_thru, _ = started
    deps = list(after) if isinstance(after, (tuple, list)) else [after]

    def body(s_ref, l_ref, send_sems, recv_sems, *rest):
        for cp in _chip_copies(mode, s_ref, l_ref, send_sems, recv_sems):
            cp.wait_send()
            cp.wait_recv()

    return pl.pallas_call(
        body, name=name, out_shape=(pltpu.HBM(s_thru.shape, s_thru.dtype), pltpu.HBM(l_thru.shape, l_thru.dtype)),
        in_specs=(HBM, HBM, SEM, SEM) + tuple(ANY for _ in deps), out_specs=(HBM, HBM),
        input_output_aliases={0: 0, 1: 1},
        compiler_params=pltpu.CompilerParams(has_side_effects=pltpu.SideEffectType.DATAFLOW_SIDE_EFFECTING))(
            s_thru, l_thru, send_sems, recv_sems, *deps)


def _allgather_finish(tag, share, land):
    rows = share.shape[0]
    half = rows // 2

    def body(w_ref, l_ref, o_ref, send_sem, recv_sem, local_sem):
        x, y, c = _place()
        k = 2 * x + y
        _sibling_handshake()
        mine = pl.ds(pl.multiple_of(c * half, 16), half)
        local = pltpu.make_async_copy(w_ref.at[mine], o_ref.at[k, mine], local_sem)
        local.start()
        local.wait()
        swap = pltpu.make_async_remote_copy(src_ref=o_ref.at[:, mine], dst_ref=o_ref.at[:, mine], send_sem=send_sem,
                                            recv_sem=recv_sem, device_id=(x, y, 1 - c), device_id_type=MESH)
        swap.start()
        swap.wait()

    return pl.pallas_call(
        body, name="allgather_finish_" + tag, in_specs=[ANY, ANY], out_specs=ANY, input_output_aliases={1: 0},
        out_shape=jax.ShapeDtypeStruct(land.shape, land.dtype),
        scratch_shapes=[pltpu.SemaphoreType.DMA, pltpu.SemaphoreType.DMA, pltpu.SemaphoreType.DMA],
        compiler_params=pltpu.CompilerParams(collective_id=SIBLING_COLLECTIVE_ID))(share, land)


def _pair_sum(tag, g, r1, c):
    rows = g.shape[1]
    half = rows // 2
    th = _row_tile(half, 256, 16)
    nblk = half // th

    def body(c_ref, g_ref, r_ref, o_ref, o2_ref):
        o_ref[...] = (g_ref[...] + r_ref[...]).astype(o_ref.dtype)
        o2_ref[...] = o_ref[...]

    spec = pl.BlockSpec((None, th, 1024), lambda k, i, c_ref: (k, i, 0))
    grid_spec = pltpu.PrefetchScalarGridSpec(
        num_scalar_prefetch=1, grid=(4, nblk),
        in_specs=[pl.BlockSpec((None, th, 1024), lambda k, i, c_ref: (k, c_ref[0] * nblk + i, 0)), spec],
        out_specs=[spec, spec])
    return pl.pallas_call(body, name="grad_pair_sum_" + tag, grid_spec=grid_spec,
                          out_shape=[jax.ShapeDtypeStruct((4, half, 1024), BF16)] * 2,
                          compiler_params=_params(("parallel", "parallel")))(c, g, r1)


def _chip_sum(tag, q, after=None):
    half = q.shape[1]
    th = _row_tile(half, 256, 16)
    deps = [] if after is None else [after]

    def body(a, b, c, d, *rest):
        rest[-1][...] = ((a[...].astype(F32) + b[...].astype(F32)) + c[...].astype(F32)) + d[...].astype(F32)

    specs = [pl.BlockSpec((None, th, 1024), functools.partial(lambda i, k: (k, i, 0), k=k)) for k in range(4)]
    return pl.pallas_call(body, name="grad_chip_sum_" + tag, grid=(half // th,), in_specs=specs + [ANY for _ in deps],
                          out_specs=pl.BlockSpec((th, 1024), lambda i: (i, 0)),
                          out_shape=jax.ShapeDtypeStruct((half, 1024), F32),
                          compiler_params=_params(("parallel",)))(q, q, q, q, *deps)


def _join_halves(tag, f, r, c):
    half = f.shape[0]
    th = _row_tile(half, 256)
    nblk = half // th

    def body(c_ref, f_ref, r_ref, o_ref):
        mine = (pl.program_id(0) == c_ref[0])
        o_ref[...] = jnp.where(mine, f_ref[...], r_ref[...])

    spec = pl.BlockSpec((th, 1024), lambda h, i, c_ref: (i, 0))
    grid_spec = pltpu.PrefetchScalarGridSpec(
        num_scalar_prefetch=1, grid=(2, nblk), in_specs=[spec, spec],
        out_specs=pl.BlockSpec((th, 1024), lambda h, i, c_ref: (h * nblk + i, 0)))
    return pl.pallas_call(body, name="grad_join_halves_" + tag, grid_spec=grid_spec,
                          out_shape=jax.ShapeDtypeStruct((2 * half, 1024), F32),
                          compiler_params=_params(("parallel", "parallel")))(c, f, r)


BIG_ARGS = ("in_w_even", "out_w_even", "qkv_w", "o_w", "ffn_gate_w", "ffn_up_w", "ffn_down_w")
def _ffn_pieces(layer):
    return tuple((f"{n}{layer}", 704, 704) for n in ("gate_wt", "up_wt", "down_w"))


IN_SLAB = (("in_wt", 1156, 1184),)
LAYER0_REST_SLAB = (("out_w", 512, 512),) + _ffn_pieces(0)
LAYER1_SLAB = (("qkv_wt", 320, 320), ("o_w", 256, 256)) + _ffn_pieces(1)
FFN0_SLAB = _ffn_pieces(0)
MIXER0_SLAB = (("in_wt", 1156, 1280), ("out_w", 512, 512))


def _slab(pieces, spec):
    parts = []
    for name, rows, room in spec:
        p = pieces[name]
        parts.append(jnp.pad(p, [(0, 0)] * (p.ndim - 2) + [(0, room - rows), (0, 0)]) if room > rows else p)
    return jnp.concatenate(parts, axis=-2) if len(parts) > 1 else parts[0]


def _unslab(slab, spec):
    out, off = {}, 0
    for name, rows, room in spec:
        out[name] = slab[..., off:off + rows, :]
        off += room
    return out


def _share_pieces(w):
    return {"in_wt": w["in_w_even"][0].T, "out_w": w["out_w_even"][0], "qkv_wt": w["qkv_w"][0].T, "o_w": w["o_w"][0],
            "gate_wt0": w["ffn_gate_w"][0].T, "gate_wt1": w["ffn_gate_w"][1].T,
            "up_wt0": w["ffn_up_w"][0].T, "up_wt1": w["ffn_up_w"][1].T,
            "down_w0": w["ffn_down_w"][0], "down_w1": w["ffn_down_w"][1]}


def _pieces_to_shares(p):
    return {"in_w_even": p["in_wt"].T[None], "out_w_even": p["out_w"][None], "qkv_w": p["qkv_wt"].T[None],
            "o_w": p["o_w"][None], "ffn_gate_w": jnp.stack([p["gate_wt0"].T, p["gate_wt1"].T]),
            "ffn_up_w": jnp.stack([p["up_wt0"].T, p["up_wt1"].T]),
            "ffn_down_w": jnp.stack([p["down_w0"], p["down_w1"]])}


def _whole_from_chips(p):
    return {k: v.reshape(-1, D) for k, v in p.items()}


def _chips_from_full(G, spec):
    return _slab({k: v.reshape(4, -1, D) for k, v in G.items()}, spec)


def _pack_small(parts):
    padded = []
    for p in parts:
        p = p.reshape(-1).astype(F32)
        padded.append(jnp.pad(p, (0, (-p.shape[0]) % 1024)))
    return jnp.concatenate(padded).reshape(-1, 128)


def _unpack_small(slab, shapes):
    flat, out, off = slab.reshape(-1), [], 0
    for shp in shapes:
        size = math.prod(shp)
        out.append(flat[off:off + size].reshape(shp))
        off += size + (-size) % 1024
    return out


SMALL = ("ada_b", "norm_mix_w", "norm_ffn_w", "conv_w", "conv_b", "dt_bias", "a_log", "d_skip", "ssm_norm_w",
         "gmlp_ln_w", "gmlp_ln_b", "gmlp_ws", "gmlp_bs", "qkv_b", "o_b", "sinks", "rel_table", "final_norm_w")
SMALL_SPLIT = {"conv_w": 1536, "qkv_b": 1280, "o_b": 1024}
WEIGHTS = ("ada_w", "ada_b", "norm_mix_w", "norm_ffn_w", "in_w_even", "conv_w", "conv_b", "dt_bias", "a_log", "d_skip",
           "ssm_norm_w", "gmlp_ln_w", "gmlp_ln_b", "gmlp_ws", "gmlp_bs", "out_w_even", "qkv_w", "qkv_b", "o_w", "o_b",
           "sinks", "rel_table", "ffn_gate_w", "ffn_up_w", "ffn_down_w", "final_norm_w")


def kernel(x, c, ada_w, ada_b, norm_mix_w, norm_ffn_w, in_w_even, conv_w, conv_b, dt_bias, a_log, d_skip, ssm_norm_w, gmlp_ln_w, gmlp_ln_b, gmlp_ws, gmlp_bs, out_w_even, qkv_w, qkv_b, o_w, o_b, sinks, rel_table, ffn_gate_w, ffn_up_w, ffn_down_w, final_norm_w, loss_target, m_ada_w, m_ada_b, m_norm_mix_w, m_norm_ffn_w, m_in_w_even, m_conv_w, m_conv_b, m_dt_bias, m_a_log, m_d_skip, m_ssm_norm_w, m_gmlp_ln_w, m_gmlp_ln_b, m_gmlp_ws, m_gmlp_bs, m_out_w_even, m_qkv_w, m_qkv_b, m_o_w, m_o_b, m_sinks, m_rel_table, m_ffn_gate_w, m_ffn_up_w, m_ffn_down_w, m_final_norm_w, v_ada_w, v_ada_b, v_norm_mix_w, v_norm_ffn_w, v_in_w_even, v_conv_w, v_conv_b, v_dt_bias, v_a_log, v_d_skip, v_ssm_norm_w, v_gmlp_ln_w, v_gmlp_ln_b, v_gmlp_ws, v_gmlp_bs, v_out_w_even, v_qkv_w, v_qkv_b, v_o_w, v_o_b, v_sinks, v_rel_table, v_ffn_gate_w, v_ffn_up_w, v_ffn_down_w, v_final_norm_w):
    args = dict(locals())
    w = {n: args[n] for n in WEIGHTS}
    m = {n: args["m_" + n] for n in WEIGHTS}
    v = {n: args["v_" + n] for n in WEIGHTS}
    ax, ay, ac = _place()
    me = 4 * ax + 2 * ay + ac
    chip = 2 * ax + ay
    south = (ac == 0).astype(F32)
    c_arr = jnp.reshape(ac, (1,)).astype(jnp.int32)

    c_all = _allreduce_small("gather_cond", lax.dynamic_update_slice(jnp.zeros((8, D), F32), c, (me, 0)).reshape(64, 128))
    c_all = c_all.reshape(8, D)
    n_ada = ada_w.shape[2]
    mod_cols = _ada_fwd(c_all, ada_w, lax.dynamic_slice(ada_b, (0, chip * n_ada), (2, n_ada)).reshape(2, 1, n_ada))
    pieces = [lax.dynamic_update_slice(jnp.zeros((2, 8, 6 * D), F32), mod_cols, (0, 0, chip * n_ada))]
    split_names = list(SMALL_SPLIT)
    for n in split_names:
        full = SMALL_SPLIT[n]
        local = w[n]
        idx = (0,) * (local.ndim - 1) + (chip * local.shape[-1],)
        pieces.append(lax.dynamic_update_slice(jnp.zeros(local.shape[:-1] + (full,), F32), local, idx))
    shapes = [p.shape for p in pieces]
    mod_slab = _allreduce_small("gather_mod", _pack_small(pieces) * south)
    gathered = _unpack_small(mod_slab, shapes)
    mod = lax.dynamic_slice(gathered[0], (0, me, 0), (2, 1, 6 * D)).reshape(2, 6 * D)
    P = {n: w[n] for n in SMALL if n not in SMALL_SPLIT and n != "ada_b"}
    for n, full in zip(split_names, gathered[1:]):
        P[n] = full
    P["final_norm_w"] = final_norm_w.reshape(1, D)

    cast = {k: p.astype(_MXU) for k, p in _share_pieces(w).items()}
    in_slab = _allgather_big(_slab(cast, IN_SLAB), after=mod_slab)
    w_in = _unslab(in_slab, IN_SLAB)["in_wt"].reshape(4 * 1156, D)

    def start_gather(tag, collective_id, spec, after):
        share = _slab(cast, spec)
        return _exchange_start("allgather_start_" + tag, collective_id, "gather", share,
                               lax.empty((4,) + share.shape, share.dtype), after=after)

    def finish_gather(tag, started, spec, after):
        share, land = _exchange_wait("allgather_wait_" + tag, "gather", started, after)
        return _whole_from_chips(_unslab(_allgather_finish(tag, share, land), spec))

    gather0 = start_gather("0", 1, LAYER0_REST_SLAB, in_slab)
    gather1 = start_gather("1", 2, LAYER1_SLAB, gather0[4])

    def start_reduce(tag, collective_id, G, spec, after=None):
        gp = _chips_from_full(G, spec)
        p, q = _pair_sum(tag, gp, _sibling_swap("grad_pair_exchange_" + tag, gp, True), c_arr)
        return _exchange_start("grad_exchange_start_" + tag, collective_id, "scatter", p, q, after=after)

    def finish_reduce(tag, started, spec, after, behind=None):
        q = _exchange_wait("grad_exchange_wait_" + tag, "scatter", started, after)[1]
        fin = _chip_sum(tag, q, after=behind)
        total = _join_halves(tag, fin, _sibling_swap("grad_final_exchange_" + tag, fin, False), c_arr)
        return _unslab(total, spec)

    reduces = {}

    def grads1(G1):
        reduces["1"] = start_reduce("1", 3, G1, LAYER1_SLAB)
        return reduces["1"][4]

    def grads_ffn0(G):
        reduces["f"] = start_reduce("f", 4, G, FFN0_SLAB)
        return reduces["f"][4]

    io = {"start": gather1[4],
          "weights0": lambda after: finish_gather("0", gather0, LAYER0_REST_SLAB, after),
          "weights1": lambda after: finish_gather("1", gather1, LAYER1_SLAB, after),
          "grads1": grads1, "grads_ffn0": grads_ffn0}
    sq, grad_x, dmod, G0, g = _local_step(x[0], loss_target[0], mod, w_in, P, io)
    loss = lax.psum(0.5 * sq[0, 0] / D, ("x", "y", "c"))

    g["final_norm_w"] = g["final_norm_w"].reshape(D)
    small_names = [n for n in SMALL if n != "ada_b"]
    pieces = [lax.dynamic_update_slice(jnp.zeros((2, 8, 6 * D), F32), dmod.reshape(2, 1, 6 * D), (0, me, 0))]
    pieces += [g[n] for n in small_names]
    shapes = [p.shape for p in pieces]
    small_slab = _allreduce_small("allreduce_small_grads", _pack_small(pieces))
    reduces["m"] = start_reduce("m", 5, G0, MIXER0_SLAB, after=small_slab)
    shares = finish_reduce("1", reduces["1"], LAYER1_SLAB, grad_x, behind=reduces["m"][4])
    shares.update(finish_reduce("f", reduces["f"], FFN0_SLAB, grad_x, behind=reduces["m"][4]))
    reduced = _unpack_small(small_slab, shapes)
    dmod_all = reduced[0]
    grads = dict(zip(small_names, reduced[1:]))
    for n in split_names:
        full = grads[n]
        size = w[n].shape[-1]
        grads[n] = lax.dynamic_slice(full, (0,) * (full.ndim - 1) + (chip * size,), full.shape[:-1] + (size,))
    grads = {n: grads[n].reshape(w[n].shape) for n in small_names}
    dw_ada, db_ada = _ada_bwd(c_all, lax.dynamic_slice(dmod_all, (0, 0, chip * n_ada), (2, 8, n_ada)), dmod_all)
    grads["ada_w"], grads["ada_b"] = dw_ada, db_ada.reshape(2, 6 * D)

    delta, new_m, new_v = {}, {}, {}

    def update(n):
        cols = w[n].shape[-1]
        d_, m_, v_ = _adamw("adamw_" + n, w[n].reshape(-1, cols), grads[n].reshape(-1, cols), m[n].reshape(-1, cols),
                            v[n].reshape(-1, cols))
        delta[n], new_m[n], new_v[n] = d_.reshape(w[n].shape), m_.reshape(w[n].shape), v_.reshape(w[n].shape)

    update("ada_w")
    shapes = [w[n].shape for n in SMALL]
    packed = [_pack_small([t[n] for n in SMALL]) for t in (w, grads, m, v)]
    outs = _adamw("adamw_small", *packed)
    for dst, slab in zip((delta, new_m, new_v), outs):
        for n, t in zip(SMALL, _unpack_small(slab, shapes)):
            dst[n] = t
    shares.update(finish_reduce("m", reduces["m"], MIXER0_SLAB, outs[0]))
    grads.update(_pieces_to_shares(shares))
    for n in BIG_ARGS:
        update(n)
    return (loss, grad_x[None], *[grads[n] for n in WEIGHTS], *[delta[n] for n in WEIGHTS],
            *[new_m[n] for n in WEIGHTS], *[new_v[n] for n in WEIGHTS])
```

```python
import functools
import math

import numpy as np
import jax
import jax.numpy as jnp
from jax import lax
from jax.experimental import pallas as pl
from jax.experimental.pallas import tpu as pltpu

F32 = jnp.float32
BF16 = jnp.bfloat16
_MXU = jnp.bfloat16
_VMEM_LIMIT = 56 * 1024 * 1024
MXU_COLS = 256
D = 1024
L = 128
NSTATE = 128
EPS = 1e-6
NEG_INF = -1e30
FFN = 2816
ADAM_LR, ADAM_B1, ADAM_B2, ADAM_EPS, ADAM_WD, ADAM_STEP = 0.001, 0.9, 0.999, 1e-08, 0.01, 10
MESH = pl.DeviceIdType.MESH
ANY = pl.BlockSpec(memory_space=pl.ANY)

NN = (((1,), (0,)), ((), ()))
NT = (((1,), (1,)), ((), ()))
TN = (((0,), (0,)), ((), ()))


def _dot(a, b, dn=NN):
    return lax.dot_general(a.astype(_MXU), b.astype(_MXU), dn, preferred_element_type=F32)


def _params(sem=None):
    return pltpu.CompilerParams(dimension_semantics=sem, vmem_limit_bytes=_VMEM_LIMIT)


def _sigmoid(x):
    return 1.0 / (1.0 + jnp.exp(-x))


def _softplus(x):
    return jnp.maximum(x, 0.0) + jnp.log(1.0 + jnp.exp(-jnp.abs(x)))


def _gelu(x):
    return 0.5 * x * (1.0 + lax.erf(x * (2.0 ** -0.5)))


def _gelu_grad(x):
    return 0.5 * (1.0 + lax.erf(x * (2.0 ** -0.5))) + x * jnp.exp(-0.5 * x * x) * (1.0 / math.sqrt(2.0 * math.pi))


def _silu_grad(a):
    sg = _sigmoid(a)
    return sg * (1.0 + a * (1.0 - sg))


def _rowwise(name, fn, rows, vecs, out_rows, out_accs=(), tr=512, after=None):
    S = rows[0].shape[0]
    tr = min(tr, S)
    assert S % tr == 0
    nr, nv, no, na = len(rows), len(vecs), len(out_rows), len(out_accs)
    deps = [] if after is None else [after]

    def body(*refs):
        ins, outs = refs[:nr + nv], refs[nr + nv + len(deps):]
        res = fn(*[r[...] for r in ins])
        if not isinstance(res, (tuple, list)):
            res = (res,)
        for k in range(no):
            outs[k][...] = res[k].astype(outs[k].dtype)
        if na:
            @pl.when(pl.program_id(0) == 0)
            def _():
                for k in range(na):
                    outs[no + k][...] = jnp.zeros_like(outs[no + k])
            for k in range(na):
                outs[no + k][...] += res[no + k]

    in_specs = [pl.BlockSpec((tr, a.shape[1]), lambda i: (i, 0)) for a in rows]
    in_specs += [pl.BlockSpec(v.shape, lambda i: (0, 0)) for v in vecs] + [ANY for _ in deps]
    out_specs = [pl.BlockSpec((tr, c), lambda i: (i, 0)) for c, _ in out_rows]
    out_specs += [pl.BlockSpec(s, lambda i: (0, 0)) for s in out_accs]
    out_shape = [jax.ShapeDtypeStruct((S, c), dt) for c, dt in out_rows]
    out_shape += [jax.ShapeDtypeStruct(s, F32) for s in out_accs]
    return pl.pallas_call(body, name=name, grid=(S // tr,), in_specs=in_specs, out_specs=out_specs,
                          out_shape=out_shape, compiler_params=_params(("arbitrary",)))(*rows, *vecs, *deps)


def _col_tile(n, cap):
    if n <= cap or n % 128:
        return n
    best = 128
    for t in range(128, cap + 1, 128):
        if n % t == 0:
            best = t
    return best


def _mm(name, As, Bs, mode, outs, epi=None, groups=None, extras=(), vecs=(), tm=512, tn_cap=1536):
    M = As[0].shape[0]
    N = Bs[0].shape[1] if mode == "nn" else Bs[0].shape[0]
    tm = min(tm, M)
    tn = _col_tile(N, tn_cap)
    assert M % tm == 0 and N % tn == 0
    npair = len(As)
    groups = groups or [0] * npair
    ng = max(groups) + 1
    nx, nv = len(extras), len(vecs)
    dn = NN if mode == "nn" else NT

    def body(*refs):
        a_refs, b_refs = refs[:npair], refs[npair:2 * npair]
        x_refs = refs[2 * npair:2 * npair + nx]
        v_refs = refs[2 * npair + nx:2 * npair + nx + nv]
        o_refs = refs[2 * npair + nx + nv:]
        step = tn if epi is None else min(tn, MXU_COLS)
        for col in range(0, tn, step):
            sl = slice(col, min(col + step, tn))
            accs = [None] * ng
            for k in range(npair):
                b = b_refs[k][:, sl] if mode == "nn" else b_refs[k][sl, :]
                d = _dot(a_refs[k][...], b, dn)
                accs[groups[k]] = d if accs[groups[k]] is None else accs[groups[k]] + d
            args = accs + [x[:, sl] for x in x_refs] + [v[:, sl] for v in v_refs]
            res = epi(*args) if epi is not None else tuple(accs)
            if not isinstance(res, (tuple, list)):
                res = (res,)
            for o, r in zip(o_refs, res):
                o[:, sl] = r.astype(o.dtype)

    in_specs = [pl.BlockSpec((tm, a.shape[1]), lambda i, j: (i, 0)) for a in As]
    if mode == "nn":
        in_specs += [pl.BlockSpec((b.shape[0], tn), lambda i, j: (0, j)) for b in Bs]
    else:
        in_specs += [pl.BlockSpec((tn, b.shape[1]), lambda i, j: (j, 0)) for b in Bs]
    in_specs += [pl.BlockSpec((tm, tn), lambda i, j: (i, j)) for _ in extras]
    in_specs += [pl.BlockSpec((1, tn), lambda i, j: (0, j)) for _ in vecs]
    out_specs = [pl.BlockSpec((tm, tn), lambda i, j: (i, j)) for _ in outs]
    out_shape = [jax.ShapeDtypeStruct((M, N), dt) for dt in outs]
    return pl.pallas_call(body, name=name, grid=(M // tm, N // tn), in_specs=in_specs, out_specs=out_specs,
                          out_shape=out_shape, compiler_params=_params(("parallel", "parallel")))(
                              *As, *Bs, *extras, *vecs)


def _mm_tn(name, A, B, tk=512, t2_cap=1536):
    S, K1 = A.shape
    N2 = B.shape[1]
    tk = min(tk, S)
    t2 = _col_tile(N2, t2_cap)
    assert S % tk == 0 and N2 % t2 == 0

    def body(a_ref, b_ref, o_ref):
        @pl.when(pl.program_id(1) == 0)
        def _():
            o_ref[...] = jnp.zeros_like(o_ref)
        o_ref[...] += _dot(a_ref[...], b_ref[...], TN)

    return pl.pallas_call(
        body, name=name, grid=(N2 // t2, S // tk),
        in_specs=[pl.BlockSpec((tk, K1), lambda j, k: (k, 0)), pl.BlockSpec((tk, t2), lambda j, k: (k, j))],
        out_specs=pl.BlockSpec((K1, t2), lambda j, k: (0, j)),
        out_shape=jax.ShapeDtypeStruct((K1, N2), F32),
        compiler_params=_params(("parallel", "arbitrary")))(A, B)


def _norm_mod_fwd(name, x, nw, sc, sh, after=None):
    def fn(x, nw, sc, sh):
        rstd = lax.rsqrt(jnp.mean(x * x, axis=-1, keepdims=True) + EPS)
        return (x * rstd * nw) * (1.0 + sc) + sh
    return _rowwise(name, fn, [x], [nw, sc, sh], [(D, BF16)], after=after)[0]


def _norm_mod_bwd(name, x, dh, dres, nw, sc, after=None):
    def fn(x, dh, dres, nw, sc):
        rstd = lax.rsqrt(jnp.mean(x * x, axis=-1, keepdims=True) + EPS)
        xh = x * rstd
        dn = dh * (1.0 + sc)
        dxh = dn * nw
        dx = rstd * (dxh - xh * jnp.mean(dxh * xh, axis=-1, keepdims=True))
        return (dres + dx, jnp.sum(dh, axis=0, keepdims=True), jnp.sum(dh * (xh * nw), axis=0, keepdims=True),
                jnp.sum(dn * xh, axis=0, keepdims=True))
    return _rowwise(name, fn, [x, dh, dres], [nw, sc], [(D, F32)], [(1, D)] * 3, after=after)


def _gate_bwd(name, dx, y, g, after=None):
    def fn(dx, y, g):
        dy = dx * g
        return dy, jnp.sum(dx * y, axis=0, keepdims=True), jnp.sum(dy, axis=0, keepdims=True)
    return _rowwise(name, fn, [dx, y], [g], [(D, BF16)], [(1, D)] * 2, after=after)


def _loss_head(x, tgt, fw):
    def fn(x, tgt, fw):
        rstd = lax.rsqrt(jnp.mean(x * x, axis=-1, keepdims=True) + EPS)
        xh = x * rstd
        err = xh * fw - tgt
        dout = err * (1.0 / D)
        dxh = dout * fw
        dx = rstd * (dxh - xh * jnp.mean(dxh * xh, axis=-1, keepdims=True))
        sq = jnp.sum(jnp.sum(err * err, axis=1, keepdims=True), axis=0, keepdims=True)
        return dx, sq, jnp.sum(dout * xh, axis=0, keepdims=True)
    return _rowwise("loss_head", fn, [x, tgt], [fw], [(D, F32)], [(1, 1), (1, D)])


def _ffn_fwd(tag, h, wg, wu, wd, x, g2):
    def act(a, b):
        return a, b, a * _sigmoid(a) * b
    a, b, f = _mm(f"ffn_up_{tag}", [h, h], [wg, wu], "nt", [F32, F32, BF16], epi=act, groups=[0, 1], tn_cap=1408)

    def res(y, x, g):
        return y, x + g * y
    y, xo = _mm(f"ffn_down_{tag}", [f], [wd], "nn", [F32, F32], epi=res, extras=[x], vecs=[g2])
    return a, b, f, y, xo


def _ffn_bwd(tag, dx, h, a, b, f, y, wg, wu, wd, g2, after=None):
    dy, dg2, _ = _gate_bwd(f"ffn_gate_bwd_{tag}", dx, y, g2, after=after)

    def act_bwd(df, a, b):
        sg = _sigmoid(a)
        return df * b * (sg * (1.0 + a * (1.0 - sg))), df * (a * sg)
    da, db = _mm(f"ffn_dact_{tag}", [dy], [wd], "nt", [BF16, BF16], epi=act_bwd, extras=[a, b], tn_cap=1408)
    dwd = _mm_tn(f"ffn_dwd_{tag}", f, dy)
    dwg = _mm_tn(f"ffn_dwg_{tag}", da, h)
    dwu = _mm_tn(f"ffn_dwu_{tag}", db, h)
    dh = _mm(f"ffn_dh_{tag}", [da, db], [wg, wu], "nn", [F32])[0]
    return dh, dg2, dwg, dwu, dwd


def _conv_fwd(xr, w, b, tb=512):
    S, C = xr.shape
    tb = min(tb, S)

    def body(x_ref, halo_ref, w_ref, b_ref, pre_ref, out_ref):
        i = pl.program_id(0)
        halo = jnp.where(i > 0, halo_ref[...], 0.0)
        xe = jnp.concatenate([halo, x_ref[...]], axis=0)
        pre = w_ref[3:4, :] * x_ref[...] + b_ref[...]
        for j in (1, 2, 3):
            pre = pre + w_ref[3 - j:4 - j, :] * pltpu.roll(xe, j, axis=0)[8:, :]
        pre_ref[...] = pre
        out_ref[...] = pre * _sigmoid(pre)

    return pl.pallas_call(
        body, name="conv_fwd", grid=(S // tb,),
        in_specs=[pl.BlockSpec((tb, C), lambda i: (i, 0)),
                  pl.BlockSpec((8, C), lambda i: (jnp.maximum(i * (tb // 8) - 1, 0), 0)),
                  pl.BlockSpec((4, C), lambda i: (0, 0)), pl.BlockSpec((1, C), lambda i: (0, 0))],
        out_specs=[pl.BlockSpec((tb, C), lambda i: (i, 0))] * 2,
        out_shape=[jax.ShapeDtypeStruct((S, C), F32)] * 2,
        compiler_params=_params(("parallel",)))(xr, xr, w, b)


def _conv_bwd(dxc, pre, xr, w, tb=512):
    S, C = xr.shape
    tb = min(tb, S)
    nblk = S // tb

    def body(d_ref, p_ref, dn_ref, pn_ref, x_ref, xh_ref, w_ref, dx_ref, dw_ref, db_ref):
        i = pl.program_id(0)

        @pl.when(i == 0)
        def _():
            dw_ref[...] = jnp.zeros_like(dw_ref)
            db_ref[...] = jnp.zeros_like(db_ref)

        dpre = d_ref[...] * _silu_grad(p_ref[...])
        dnext = jnp.where(i < nblk - 1, dn_ref[...] * _silu_grad(pn_ref[...]), 0.0)
        pe = jnp.concatenate([dpre, dnext], axis=0)
        dx = w_ref[3:4, :] * dpre
        for j in (1, 2, 3):
            dx = dx + w_ref[3 - j:4 - j, :] * pltpu.roll(pe, tb + 8 - j, axis=0)[:tb, :]
        dx_ref[...] = dx.astype(dx_ref.dtype)
        halo = jnp.where(i > 0, xh_ref[...], 0.0)
        xe = jnp.concatenate([halo, x_ref[...]], axis=0)
        for k in range(3):
            dw_ref[k:k + 1, :] += jnp.sum(dpre * pltpu.roll(xe, 3 - k, axis=0)[8:, :], axis=0, keepdims=True)
        dw_ref[3:4, :] += jnp.sum(dpre * x_ref[...], axis=0, keepdims=True)
        db_ref[...] += jnp.sum(dpre, axis=0, keepdims=True)

    blk = pl.BlockSpec((tb, C), lambda i: (i, 0))
    nxt = pl.BlockSpec((8, C), lambda i: (jnp.minimum((i + 1) * (tb // 8), S // 8 - 1), 0))
    prv = pl.BlockSpec((8, C), lambda i: (jnp.maximum(i * (tb // 8) - 1, 0), 0))
    return pl.pallas_call(
        body, name="conv_bwd", grid=(nblk,),
        in_specs=[blk, blk, nxt, nxt, blk, prv, pl.BlockSpec((4, C), lambda i: (0, 0))],
        out_specs=[blk, pl.BlockSpec((4, C), lambda i: (0, 0)), pl.BlockSpec((1, C), lambda i: (0, 0))],
        out_shape=[jax.ShapeDtypeStruct((S, C), BF16), jax.ShapeDtypeStruct((4, C), F32),
                   jax.ShapeDtypeStruct((1, C), F32)],
        compiler_params=_params(("arbitrary",)))(dxc, pre, dxc, pre, xr, xr, w)


def _iota(shape, dim):
    return lax.broadcasted_iota(jnp.int32, shape, dim)


def _colsel(m, lane, h):
    return jnp.sum(jnp.where(lane == h, m, 0.0), axis=1, keepdims=True)


def _cumsum_rows(v):
    r = _iota(v.shape, 0)
    k = 1
    while k < v.shape[0]:
        v = v + jnp.where(r >= k, pltpu.roll(v, k, axis=0), 0.0)
        k *= 2
    return v


def _suffix_sum_rows(v):
    n = v.shape[0]
    r = _iota(v.shape, 0)
    k = 1
    while k < n:
        v = v + jnp.where(r < n - k, pltpu.roll(v, n - k, axis=0), 0.0)
        k *= 2
    return v


def _ssd_fwd(xc, dtr, z, dtb, alog, dskl, nw):
    S = xc.shape[0]
    nc = S // L

    def body(xc_ref, dtr_ref, z_ref, dtb_ref, alog_ref, dsk_ref, nw_ref, ya_ref, y_ref, prev_ref,
             st_ref, cum_ref, cumT_ref):
        i = pl.program_id(0)

        @pl.when(i == 0)
        def _():
            st_ref[...] = jnp.zeros_like(st_ref)

        lane = _iota((L, 128), 1)
        lane1 = _iota((1, 128), 1)
        lo = lane < 64
        lo1 = lane1 < 64
        tril = _iota((L, L), 0) >= _iota((L, L), 1)
        dt = _softplus(dtr_ref[...] + dtb_ref[...])
        a_neg = -jnp.exp(alog_ref[...])
        cum = _cumsum_rows(dt * a_neg)
        cum_ref[...] = cum
        cumT_ref[...] = cum.T
        last_all = cum_ref[L - 1:L, :]
        prev_t = st_ref[...]
        prev_ref[0] = prev_t
        for g in range(2):
            bg = xc_ref[:, 1024 + g * 128:1152 + g * 128]
            cg = xc_ref[:, 1280 + g * 128:1408 + g * 128]
            gmat = _dot(cg, bg, NT)
            yoff = _dot(cg, prev_t[:, g * 512:(g + 1) * 512])
            bg_t = bg.T
            for jp in range(4):
                j = g * 4 + jp
                sl = slice(j * 128, (j + 1) * 128)
                xp = xc_ref[:, sl]
                cc = [_colsel(cum, lane, 2 * j), _colsel(cum, lane, 2 * j + 1)]
                cum_l = jnp.where(lo, cc[0], cc[1])
                dt_l = jnp.where(lo, _colsel(dt, lane, 2 * j), _colsel(dt, lane, 2 * j + 1))
                last_l = jnp.where(lo1, _colsel(last_all, lane1, 2 * j), _colsel(last_all, lane1, 2 * j + 1))
                xd = xp * dt_l
                ys = []
                for hh in range(2):
                    seg = cc[hh] - cumT_ref[2 * j + hh:2 * j + hh + 1, :]
                    dm = jnp.where(tril, jnp.exp(jnp.where(tril, seg, 0.0)), 0.0)
                    ys.append(_dot(gmat * dm, xd))
                y_ref[:, sl] = (jnp.where(lo, ys[0], ys[1]) + jnp.exp(cum_l) * yoff[:, jp * 128:(jp + 1) * 128]
                                + dsk_ref[:, sl] * xp)
                st_ref[:, sl] = prev_t[:, sl] * jnp.exp(last_l) + _dot(bg_t, xd * jnp.exp(last_l - cum_l))
        for g in range(2):
            sl = slice(g * 512, (g + 1) * 512)
            zz = z_ref[:, sl]
            yg = y_ref[:, sl] * (zz * _sigmoid(zz))
            rstd = lax.rsqrt(jnp.mean(yg * yg, axis=-1, keepdims=True) + EPS)
            ya_ref[:, sl] = (yg * rstd * nw_ref[:, sl]).astype(ya_ref.dtype)

    blk = lambda c: pl.BlockSpec((L, c), lambda i: (i, 0))
    vec = lambda c: pl.BlockSpec((1, c), lambda i: (0, 0))
    return pl.pallas_call(
        body, name="ssd_fwd", grid=(nc,),
        in_specs=[blk(1536), blk(128), blk(1024), vec(128), vec(128), vec(1024), vec(1024)],
        out_specs=[blk(1024), blk(1024), pl.BlockSpec((1, NSTATE, 1024), lambda i: (i, 0, 0))],
        out_shape=[jax.ShapeDtypeStruct((S, 1024), BF16), jax.ShapeDtypeStruct((S, 1024), F32),
                   jax.ShapeDtypeStruct((nc, NSTATE, 1024), F32)],
        scratch_shapes=[pltpu.VMEM((NSTATE, 1024), F32), pltpu.VMEM((L, 128), F32), pltpu.VMEM((L, 128), F32)],
        compiler_params=_params(("arbitrary",)))(xc, dtr, z, dtb, alog, dskl, nw)


def _ssd_bwd(dya, y, z, xc, dtr, prev, dtb, alog, dskl, nw):
    S = xc.shape[0]
    nc = S // L

    def body(dya_ref, y_ref, z_ref, xc_ref, dtr_ref, prev_ref, dtb_ref, alog_ref, dsk_ref, nw_ref,
             dz_ref, dxc_ref, ddtr_ref, dnw_ref, ddsk_ref, dalog_ref, ddtb_ref,
             dst_ref, cum_ref, cumT_ref, dy_ref, dskacc_ref):
        i = pl.program_id(0)

        @pl.when(i == 0)
        def _():
            dst_ref[...] = jnp.zeros_like(dst_ref)
            dskacc_ref[...] = jnp.zeros_like(dskacc_ref)
            dnw_ref[...] = jnp.zeros_like(dnw_ref)
            dalog_ref[...] = jnp.zeros_like(dalog_ref)
            ddtb_ref[...] = jnp.zeros_like(ddtb_ref)

        lane = _iota((L, 128), 1)
        lane1 = _iota((1, 128), 1)
        lo = lane < 64
        lo1 = lane1 < 64
        r2, c2 = _iota((L, L), 0), _iota((L, L), 1)
        tril = r2 >= c2
        triu = r2 <= c2
        is_last = _iota((L, 1), 0) == L - 1

        for g in range(2):
            sl = slice(g * 512, (g + 1) * 512)
            zz = z_ref[:, sl]
            sg = _sigmoid(zz)
            zg = zz * sg
            yv = y_ref[:, sl]
            yg = yv * zg
            rstd = lax.rsqrt(jnp.mean(yg * yg, axis=-1, keepdims=True) + EPS)
            xh = yg * rstd
            d_out = dya_ref[:, sl]
            dnw_ref[:, sl] += jnp.sum(d_out * xh, axis=0, keepdims=True)
            dyn = d_out * nw_ref[:, sl]
            dyg = rstd * (dyn - xh * jnp.mean(dyn * xh, axis=-1, keepdims=True))
            dy_ref[:, sl] = dyg * zg
            dz_ref[:, sl] = (dyg * yv * (sg * (1.0 + zz * (1.0 - sg)))).astype(dz_ref.dtype)

        dtin = dtr_ref[...] + dtb_ref[...]
        dt = _softplus(dtin)
        a_neg = -jnp.exp(alog_ref[...])
        cum = _cumsum_rows(dt * a_neg)
        cum_ref[...] = cum
        cumT_ref[...] = cum.T
        last_all = cum_ref[L - 1:L, :]
        prev_t = prev_ref[0]
        dn_t = dst_ref[...]
        dcum = jnp.zeros((L, 128), F32)
        ddt = jnp.zeros((L, 128), F32)
        for g in range(2):
            gsl = slice(g * 512, (g + 1) * 512)
            bg = xc_ref[:, 1024 + g * 128:1152 + g * 128]
            cg = xc_ref[:, 1280 + g * 128:1408 + g * 128]
            gmat = _dot(cg, bg, NT)
            gmat_t = _dot(bg, cg, NT)
            pg = prev_t[:, gsl]
            zmat = _dot(cg, pg)
            dgm = jnp.zeros((L, L), F32)
            dgm_t = jnp.zeros((L, L), F32)
            db_acc = jnp.zeros((L, NSTATE), F32)
            dz_parts, cd_parts = [], []
            for jp in range(4):
                j = g * 4 + jp
                sl = slice(j * 128, (j + 1) * 128)
                xp = xc_ref[:, sl]
                dyp = dy_ref[:, sl]
                cc = [_colsel(cum, lane, 2 * j), _colsel(cum, lane, 2 * j + 1)]
                lc = [_colsel(last_all, lane1, 2 * j), _colsel(last_all, lane1, 2 * j + 1)]
                cum_l = jnp.where(lo, cc[0], cc[1])
                dt_l = jnp.where(lo, _colsel(dt, lane, 2 * j), _colsel(dt, lane, 2 * j + 1))
                last_l = jnp.where(lo1, lc[0], lc[1])
                e_l = jnp.exp(cum_l)
                dte_l = jnp.exp(last_l - cum_l)
                cd_l = jnp.exp(last_l)
                cd_parts.append(cd_l)
                xd = xp * dt_l
                dskacc_ref[:, sl] += jnp.sum(dyp * xp, axis=0, keepdims=True)
                dxp = dsk_ref[:, sl] * dyp
                t = dyp * (e_l * zmat[:, jp * 128:(jp + 1) * 128])
                dcc = [jnp.sum(jnp.where(lo, t, 0.0), axis=1, keepdims=True),
                       jnp.sum(jnp.where(lo, 0.0, t), axis=1, keepdims=True)]
                dz_parts.append(e_l * dyp)
                dnp_ = dn_t[:, sl]
                t2 = jnp.sum(dnp_ * prev_t[:, sl], axis=0, keepdims=True)
                dcd = [jnp.sum(jnp.where(lo1, t2, 0.0), axis=1, keepdims=True),
                       jnp.sum(jnp.where(lo1, 0.0, t2), axis=1, keepdims=True)]
                wm = _dot(bg, dnp_)
                dxd = wm * dte_l
                t3 = wm * xd
                ddte = [jnp.sum(jnp.where(lo, t3, 0.0), axis=1, keepdims=True),
                        jnp.sum(jnp.where(lo, 0.0, t3), axis=1, keepdims=True)]
                db_acc = db_acc + _dot(xd * dte_l, dnp_, NT)
                for hh in range(2):
                    h = 2 * j + hh
                    half = lo if hh == 0 else jnp.logical_not(lo)
                    row = cumT_ref[h:h + 1, :]
                    dm = jnp.where(tril, jnp.exp(jnp.where(tril, cc[hh] - row, 0.0)), 0.0)
                    dm_t = jnp.where(triu, jnp.exp(jnp.where(triu, row - cc[hh], 0.0)), 0.0)
                    m = gmat * dm
                    m_t = gmat_t * dm_t
                    dym = jnp.where(half, dyp, 0.0)
                    d_m = _dot(dym, xd, NT)
                    d_mt = _dot(xd, dym, NT)
                    dxd = dxd + _dot(m_t, dym)
                    dcc[hh] = dcc[hh] + jnp.sum(d_m * m, axis=1, keepdims=True) - jnp.sum(d_mt * m_t, axis=1, keepdims=True)
                    dgm = dgm + d_m * dm
                    dgm_t = dgm_t + d_mt * dm_t
                    dte_c = jnp.exp(lc[hh] - cc[hh])
                    dcc[hh] = dcc[hh] - ddte[hh] * dte_c
                    endc = dcd[hh] * jnp.exp(lc[hh]) + jnp.sum(ddte[hh] * dte_c, axis=0, keepdims=True)
                    dcc[hh] = dcc[hh] + jnp.where(is_last, endc, 0.0)
                    dcum = jnp.where(lane == h, dcc[hh], dcum)
                dxc_ref[:, sl] = dxp + dxd * dt_l
                t4 = dxd * xp
                ddt = jnp.where(lane == 2 * j, jnp.sum(jnp.where(lo, t4, 0.0), axis=1, keepdims=True), ddt)
                ddt = jnp.where(lane == 2 * j + 1, jnp.sum(jnp.where(lo, 0.0, t4), axis=1, keepdims=True), ddt)
            dzg = jnp.concatenate(dz_parts, axis=1)
            dst_ref[:, gsl] = dn_t[:, gsl] * jnp.concatenate(cd_parts, axis=1) + _dot(cg.T, dzg)
            dxc_ref[:, 1280 + g * 128:1408 + g * 128] = _dot(dgm, bg) + _dot(dzg, pg, NT)
            dxc_ref[:, 1024 + g * 128:1152 + g * 128] = _dot(dgm_t, cg) + db_acc
        dla = _suffix_sum_rows(dcum)
        ddt = ddt + dla * a_neg
        dalog_ref[...] += jnp.sum(dla * dt, axis=0, keepdims=True) * a_neg
        ddtr = jnp.where(lane < 16, ddt * _sigmoid(dtin), 0.0)
        ddtr_ref[...] = ddtr.astype(ddtr_ref.dtype)
        ddtb_ref[...] += jnp.sum(ddtr, axis=0, keepdims=True)

        @pl.when(i == nc - 1)
        def _():
            seg = (_iota((1024, 128), 0) // 64 == _iota((1024, 128), 1)).astype(F32)
            acc8 = jnp.broadcast_to(dskacc_ref[...], (8, 1024))
            ddsk_ref[...] = lax.dot_general(acc8, seg, NN, precision=lax.Precision.HIGHEST,
                                            preferred_element_type=F32)

    rev = lambda c: pl.BlockSpec((L, c), lambda i: (nc - 1 - i, 0))
    vec = lambda c: pl.BlockSpec((1, c), lambda i: (0, 0))
    return pl.pallas_call(
        body, name="ssd_bwd", grid=(nc,),
        in_specs=[rev(1024), rev(1024), rev(1024), rev(1536), rev(128),
                  pl.BlockSpec((1, NSTATE, 1024), lambda i: (nc - 1 - i, 0, 0)),
                  vec(128), vec(128), vec(1024), vec(1024)],
        out_specs=[rev(1024), rev(1536), rev(128), vec(1024), pl.BlockSpec((8, 128), lambda i: (0, 0)),
                   vec(128), vec(128)],
        out_shape=[jax.ShapeDtypeStruct((S, 1024), BF16), jax.ShapeDtypeStruct((S, 1536), F32),
                   jax.ShapeDtypeStruct((S, 128), BF16), jax.ShapeDtypeStruct((1, 1024), F32),
                   jax.ShapeDtypeStruct((8, 128), F32), jax.ShapeDtypeStruct((1, 128), F32),
                   jax.ShapeDtypeStruct((1, 128), F32)],
        scratch_shapes=[pltpu.VMEM((NSTATE, 1024), F32), pltpu.VMEM((L, 128), F32), pltpu.VMEM((L, 128), F32),
                        pltpu.VMEM((L, 1024), F32), pltpu.VMEM((1, 1024), F32)],
        compiler_params=_params(("arbitrary",)))(dya, y, z, xc, dtr, prev, dtb, alog, dskl, nw)


def _layer_norm_parts(vg):
    mu = jnp.mean(vg, axis=-1, keepdims=True)
    vc = vg - mu
    rstd = lax.rsqrt(jnp.mean(vc * vc, axis=-1, keepdims=True) + EPS)
    return vc * rstd, rstd


def _gmlp_fwd(u, v, lnw, lnb, ws, bse, tb=512):
    S = u.shape[0]
    tb = min(tb, S)

    def body(u_ref, v_ref, lnw_ref, lnb_ref, ws_ref, bse_ref, o_ref, vn_ref):
        tril = _iota((L, L), 0) >= _iota((L, L), 1)
        xh, _ = _layer_norm_parts(_gelu(v_ref[...]))
        vn_ref[...] = xh * lnw_ref[...] + lnb_ref[...]
        for g in range(8):
            w = jnp.where(tril, ws_ref[g], 0.0)
            gs = slice(g * 128, (g + 1) * 128)
            for ch in range(tb // L):
                rs = slice(ch * L, (ch + 1) * L)
                sv = _dot(w, vn_ref[rs, gs]) + bse_ref[g]
                o_ref[rs, gs] = (_gelu(u_ref[rs, gs]) * sv).astype(o_ref.dtype)

    blk = pl.BlockSpec((tb, 1024), lambda i: (i, 0))
    vec = pl.BlockSpec((1, 1024), lambda i: (0, 0))
    cube = pl.BlockSpec((8, L, 128), lambda i: (0, 0, 0))
    return pl.pallas_call(
        body, name="gmlp_fwd", grid=(S // tb,), in_specs=[blk, blk, vec, vec, cube, cube], out_specs=blk,
        out_shape=jax.ShapeDtypeStruct((S, 1024), BF16), scratch_shapes=[pltpu.VMEM((tb, 1024), F32)],
        compiler_params=_params(("parallel",)))(u, v, lnw, lnb, ws, bse)


def _gmlp_bwd(dyb, u, v, lnw, lnb, ws, bse, tb=512):
    S = u.shape[0]
    tb = min(tb, S)

    def body(d_ref, u_ref, v_ref, lnw_ref, lnb_ref, ws_ref, bse_ref,
             du_ref, dv_ref, dws_ref, dbse_ref, dlnw_ref, dlnb_ref, vn_ref, dvn_ref):
        @pl.when(pl.program_id(0) == 0)
        def _():
            dws_ref[...] = jnp.zeros_like(dws_ref)
            dbse_ref[...] = jnp.zeros_like(dbse_ref)
            dlnw_ref[...] = jnp.zeros_like(dlnw_ref)
            dlnb_ref[...] = jnp.zeros_like(dlnb_ref)

        tril = _iota((L, L), 0) >= _iota((L, L), 1)
        vv = v_ref[...]
        xh, rstd = _layer_norm_parts(_gelu(vv))
        vn_ref[...] = xh * lnw_ref[...] + lnb_ref[...]
        for g in range(8):
            w = jnp.where(tril, ws_ref[g], 0.0)
            w_t = w.T
            gs = slice(g * 128, (g + 1) * 128)
            dw = jnp.zeros((L, L), F32)
            dbs = jnp.zeros((L, 128), F32)
            for ch in range(tb // L):
                rs = slice(ch * L, (ch + 1) * L)
                vn = vn_ref[rs, gs]
                sv = _dot(w, vn) + bse_ref[g]
                uu = u_ref[rs, gs]
                dd = d_ref[rs, gs]
                du_ref[rs, gs] = (dd * sv * _gelu_grad(uu)).astype(du_ref.dtype)
                dsv = dd * _gelu(uu)
                dw = dw + _dot(dsv, vn, NT)
                dbs = dbs + dsv
                dvn_ref[rs, gs] = _dot(w_t, dsv)
            dws_ref[g] += jnp.where(tril, dw, 0.0)
            dbse_ref[g] += dbs
        dvn = dvn_ref[...]
        dlnw_ref[...] += jnp.sum(dvn * xh, axis=0, keepdims=True)
        dlnb_ref[...] += jnp.sum(dvn, axis=0, keepdims=True)
        dxh = dvn * lnw_ref[...]
        dvg = rstd * (dxh - jnp.mean(dxh, axis=-1, keepdims=True) - xh * jnp.mean(dxh * xh, axis=-1, keepdims=True))
        dv_ref[...] = (dvg * _gelu_grad(vv)).astype(dv_ref.dtype)

    blk = pl.BlockSpec((tb, 1024), lambda i: (i, 0))
    vec = pl.BlockSpec((1, 1024), lambda i: (0, 0))
    cube = pl.BlockSpec((8, L, 128), lambda i: (0, 0, 0))
    return pl.pallas_call(
        body, name="gmlp_bwd", grid=(S // tb,), in_specs=[blk, blk, blk, vec, vec, cube, cube],
        out_specs=[blk, blk, cube, cube, vec, vec],
        out_shape=[jax.ShapeDtypeStruct((S, 1024), BF16), jax.ShapeDtypeStruct((S, 1024), BF16),
                   jax.ShapeDtypeStruct((8, L, 128), F32), jax.ShapeDtypeStruct((8, L, 128), F32),
                   jax.ShapeDtypeStruct((1, 1024), F32), jax.ShapeDtypeStruct((1, 1024), F32)],
        scratch_shapes=[pltpu.VMEM((tb, 1024), F32), pltpu.VMEM((tb, 1024), F32)],
        compiler_params=_params(("arbitrary",)))(dyb, u, v, lnw, lnb, ws, bse)


def _lane_sum(name, a):
    def body(a_ref, o_ref):
        o_ref[...] = jnp.sum(a_ref[...], axis=1, keepdims=True)
    return pl.pallas_call(body, name=name, out_shape=jax.ShapeDtypeStruct((a.shape[0], 1), F32))(a)


def _bucket_onehot_t():
    qi = np.arange(L)[:, None]
    sj = np.arange(2 * L)[None, :]
    dist = np.maximum(qi + L - sj, 0)
    log_ratio = (np.log(np.maximum(dist, 1).astype(np.float32) / np.float32(16)) / np.float32(math.log(128 / 16)))
    large = 16 + (log_ratio.astype(np.float32) * np.float32(16)).astype(np.int32)
    bucket = np.where(dist < 16, dist, np.minimum(large, 31)).reshape(-1)
    return (np.arange(32)[:, None] == bucket[None, :]).astype(np.float32)


def _rel_bias(table_t, onehot_t):
    def body(t_ref, oh_ref, o_ref):
        o_ref[...] = lax.dot_general(t_ref[...], oh_ref[...], NN, precision=lax.Precision.HIGHEST,
                                     preferred_element_type=F32)
    return pl.pallas_call(body, name="rel_bias", out_shape=jax.ShapeDtypeStruct((16, L * 2 * L), F32),
                          compiler_params=_params())(table_t, onehot_t)


def _rel_bias_bwd(dbias, onehot_t):
    def body(d_ref, oh_ref, o_ref):
        o_ref[...] = lax.dot_general(d_ref[...], oh_ref[...], NT, precision=lax.Precision.HIGHEST,
                                     preferred_element_type=F32)
    return pl.pallas_call(body, name="rel_bias_bwd", out_shape=jax.ShapeDtypeStruct((16, 32), F32),
                          compiler_params=_params())(dbias, onehot_t)


def _band(kp, kc, lo):
    kk = jnp.concatenate([kp, kc], axis=0)
    kr = pltpu.roll(kk, 64, axis=1)
    return [jnp.where(lo, kk, kr), jnp.where(lo, kr, kk)]


def _attn_mask(i):
    qi, sj = _iota((L, 2 * L), 0), _iota((L, 2 * L), 1)
    rel = qi + L - sj
    return (rel >= 0) & (rel < L) & ((sj >= L) | (i > 0))


SMEM = pl.BlockSpec(memory_space=pltpu.SMEM)


def _attn_fwd(qkv, bias, sinks):
    S = qkv.shape[0]
    nb = S // L
    scale = 64 ** -0.5

    def body(sink_ref, q_ref, kc_ref, vc_ref, kp_ref, vp_ref, bias_ref, o_ref, lse_ref):
        i = pl.program_id(0)
        lane = _iota((L, 128), 1)
        lo = lane < 64
        lo2 = _iota((2 * L, 128), 1) < 64
        mask = _attn_mask(i)
        kd = _band(kp_ref[...], kc_ref[...], lo2)
        vd = _band(vp_ref[...], vc_ref[...], lo2)
        lse = jnp.zeros((L, 128), F32)
        for pr in range(8):
            sl = slice(pr * 128, (pr + 1) * 128)
            qp = q_ref[:, sl]
            j = pr // 4
            outs = []
            for hh in range(2):
                h = 2 * pr + hh
                qm = jnp.where(lo if hh == 0 else jnp.logical_not(lo), qp, 0.0)
                lg = jnp.where(mask, _dot(qm, kd[j], NT) * scale + bias_ref[h], NEG_INF)
                s = sink_ref[h]
                m = jnp.maximum(jnp.max(lg, axis=1, keepdims=True), s)
                p = jnp.where(mask, jnp.exp(lg - m), 0.0)
                den = jnp.sum(p, axis=1, keepdims=True) + jnp.exp(s - m)
                outs.append(_dot(p / den, vd[j]))
                lse = jnp.where(lane == h, m + jnp.log(den), lse)
            o_ref[:, sl] = jnp.where(lo, outs[0], outs[1]).astype(o_ref.dtype)
        lse_ref[...] = lse

    prev = lambda col: pl.BlockSpec((L, 128), lambda i: (jnp.maximum(i - 1, 0), col))
    cur = lambda col: pl.BlockSpec((L, 128), lambda i: (i, col))
    return pl.pallas_call(
        body, name="attn_fwd", grid=(nb,),
        in_specs=[SMEM, pl.BlockSpec((L, 1024), lambda i: (i, 0)), cur(8), cur(9), prev(8), prev(9),
                  pl.BlockSpec((16, L, 2 * L), lambda i: (0, 0, 0))],
        out_specs=[pl.BlockSpec((L, 1024), lambda i: (i, 0)), pl.BlockSpec((L, 128), lambda i: (i, 0))],
        out_shape=[jax.ShapeDtypeStruct((S, 1024), BF16), jax.ShapeDtypeStruct((S, 128), F32)],
        compiler_params=_params(("parallel",)))(sinks, qkv, qkv, qkv, qkv, qkv, bias)


def _attn_bwd(qkv, d_o, lse, bias, sinks):
    S = qkv.shape[0]
    nb = S // L
    scale = 64 ** -0.5

    def body(sink_ref, q_ref, kc_ref, vc_ref, kp_ref, vp_ref, do_ref, lse_ref, bias_ref,
             dq_ref, dkv_ref, dbias_ref, dsink_ref, dbq_ref, dbkv_ref, carry_ref):
        i = pl.program_id(0)

        @pl.when(i == 0)
        def _():
            dbias_ref[...] = jnp.zeros_like(dbias_ref)
            dsink_ref[...] = jnp.zeros_like(dsink_ref)
            dbq_ref[...] = jnp.zeros_like(dbq_ref)
            dbkv_ref[...] = jnp.zeros_like(dbkv_ref)
            carry_ref[...] = jnp.zeros_like(carry_ref)

        @pl.when(i < nb)
        def _():
            lane = _iota((L, 128), 1)
            lane1 = _iota((1, 128), 1)
            lo = lane < 64
            lo2 = _iota((2 * L, 128), 1) < 64
            mask = _attn_mask(i)
            kd = _band(kp_ref[...], kc_ref[...], lo2)
            vd = _band(vp_ref[...], vc_ref[...], lo2)
            lse_all = lse_ref[...]
            acc_k = [jnp.zeros((2 * L, 128), F32), jnp.zeros((2 * L, 128), F32)]
            acc_v = [jnp.zeros((2 * L, 128), F32), jnp.zeros((2 * L, 128), F32)]
            dsink = jnp.zeros((1, 128), F32)
            for pr in range(8):
                sl = slice(pr * 128, (pr + 1) * 128)
                qp = q_ref[:, sl]
                dop = do_ref[:, sl]
                j = pr // 4
                dqs = []
                for hh in range(2):
                    h = 2 * pr + hh
                    half = lo if hh == 0 else jnp.logical_not(lo)
                    qm = jnp.where(half, qp, 0.0)
                    dom = jnp.where(half, dop, 0.0)
                    lse_h = _colsel(lse_all, lane, h)
                    lg = _dot(qm, kd[j], NT) * scale + bias_ref[h]
                    p = jnp.where(mask, jnp.exp(jnp.where(mask, lg, NEG_INF) - lse_h), 0.0)
                    dp = _dot(dom, vd[j], NT)
                    delta = jnp.sum(p * dp, axis=1, keepdims=True)
                    ds = p * (dp - delta)
                    dbias_ref[h] += ds
                    ds_sink = jnp.sum(-jnp.exp(sink_ref[h] - lse_h) * delta, axis=0, keepdims=True)
                    dsink = dsink + jnp.where(lane1 == h, ds_sink, 0.0)
                    dss = ds * scale
                    dqs.append(_dot(dss, kd[j]))
                    acc_k[j] = acc_k[j] + _dot(dss, qm, TN)
                    acc_v[j] = acc_v[j] + _dot(p, dom, TN)
                dq = jnp.where(lo, dqs[0], dqs[1])
                dq_ref[:, sl] = dq.astype(dq_ref.dtype)
                dbq_ref[:, sl] += jnp.sum(dq, axis=0, keepdims=True)
            dsink_ref[...] += dsink
            tot_k = [a + pltpu.roll(a, 64, axis=1) for a in acc_k]
            tot_v = [a + pltpu.roll(a, 64, axis=1) for a in acc_v]
            dkv = jnp.concatenate([jnp.where(lo2, tot_k[0], tot_k[1]), jnp.where(lo2, tot_v[0], tot_v[1])], axis=1)
            dbkv_ref[...] += jnp.sum(dkv, axis=0, keepdims=True)
            dkv_ref[...] = (carry_ref[...] + dkv[:L, :]).astype(dkv_ref.dtype)
            carry_ref[...] = dkv[L:, :]

        @pl.when(i == nb)
        def _():
            dkv_ref[...] = carry_ref[...].astype(dkv_ref.dtype)

    c = lambda i: jnp.minimum(i, nb - 1)
    prev = lambda col: pl.BlockSpec((L, 128), lambda i: (jnp.maximum(c(i) - 1, 0), col))
    cur = lambda col: pl.BlockSpec((L, 128), lambda i: (c(i), col))
    row = lambda w: pl.BlockSpec((L, w), lambda i: (c(i), 0))
    cube = pl.BlockSpec((16, L, 2 * L), lambda i: (0, 0, 0))
    vec = lambda w: pl.BlockSpec((1, w), lambda i: (0, 0))
    return pl.pallas_call(
        body, name="attn_bwd", grid=(nb + 1,),
        in_specs=[SMEM, row(1024), cur(8), cur(9), prev(8), prev(9), row(1024), row(128), cube],
        out_specs=[row(1024), pl.BlockSpec((L, 256), lambda i: (jnp.maximum(i - 1, 0), 0)), cube,
                   vec(128), vec(1024), vec(256)],
        out_shape=[jax.ShapeDtypeStruct((S, 1024), BF16), jax.ShapeDtypeStruct((S, 256), BF16),
                   jax.ShapeDtypeStruct((16, L, 2 * L), F32), jax.ShapeDtypeStruct((1, 128), F32),
                   jax.ShapeDtypeStruct((1, 1024), F32), jax.ShapeDtypeStruct((1, 256), F32)],
        scratch_shapes=[pltpu.VMEM((L, 256), F32)],
        compiler_params=_params(("arbitrary",)))(sinks, qkv, qkv, qkv, qkv, qkv, d_o, lse, bias)


def _pad_lanes(a, n=128):
    return jnp.pad(a, ((0, 0), (0, n - a.shape[1])))


def _local_step(x, tgt, mod, w_in, P, io):
    md = [[mod[l:l + 1, k * D:(k + 1) * D] for k in range(6)] for l in range(2)]
    G, g = {}, {}

    sh1, sc1, g1, sh2, sc2, g2 = md[0]
    nmw0, nfw0 = P["norm_mix_w"][0:1], P["norm_ffn_w"][0:1]
    h0 = _norm_mod_fwd("norm_mix_0", x, nmw0, sc1, sh1, after=io["start"])
    segs = {"z": w_in[0:1024], "xbc": w_in[1024:2560], "dt": jnp.pad(w_in[2560:2576], ((0, 112), (0, 0))),
            "u": w_in[2576:3600], "v": w_in[3600:4624]}
    proj = {k: _mm(f"in_proj_{k}", [h0], [w], "nt", [F32])[0] for k, w in segs.items()}
    conv_w, conv_b = P["conv_w"][0], P["conv_b"]
    pre, xc = _conv_fwd(proj["xbc"], conv_w, conv_b)
    dtb, alog = _pad_lanes(P["dt_bias"]), _pad_lanes(P["a_log"])
    dskl = jnp.repeat(P["d_skip"], 64, axis=1)
    ya, y_ssd, prev = _ssd_fwd(xc, proj["dt"], proj["z"], dtb, alog, dskl, P["ssm_norm_w"])
    ws = P["gmlp_ws"][0]
    bse = jnp.broadcast_to(P["gmlp_bs"][0][:, :, None], (8, L, 128))
    yb = _gmlp_fwd(proj["u"], proj["v"], P["gmlp_ln_w"], P["gmlp_ln_b"], ws, bse)
    W = dict(io["weights0"]((ya, yb)))
    w_oa, w_ob = W["out_w"][:1024], W["out_w"][1024:]

    def res(y, x, gate):
        return y, x + gate * y
    mix0, x1 = _mm("out_proj_0", [ya, yb], [w_oa, w_ob], "nn", [F32, F32], epi=res, extras=[x], vecs=[g1])
    h0f = _norm_mod_fwd("norm_ffn_0", x1, nfw0, sc2, sh2)
    a0, b0, f0, y0, x2 = _ffn_fwd("0", h0f, W["gate_wt0"], W["up_wt0"], W["down_w0"], x1, g2)

    sh1b, sc1b, g1b, sh2b, sc2b, g2b = md[1]
    nmw1, nfw1 = P["norm_mix_w"][1:2], P["norm_ffn_w"][1:2]
    W.update(io["weights1"](x2))
    h1 = _norm_mod_fwd("norm_mix_1", x2, nmw1, sc1b, sh1b)
    qkv = _mm("qkv_proj", [h1], [W["qkv_wt"]], "nt", [F32], epi=lambda acc, b: acc + b, vecs=[P["qkv_b"]])[0]
    onehot_t = jnp.asarray(_bucket_onehot_t())
    bias = _rel_bias(P["rel_table"].T, onehot_t).reshape(16, L, 2 * L)
    sinks = P["sinks"].reshape(16)
    att, lse = _attn_fwd(qkv, bias, sinks)

    def res_b(y, x, gate, b):
        y = y + b
        return y, x + gate * y
    mix1, x3 = _mm("o_proj", [att], [W["o_w"]], "nn", [F32, F32], epi=res_b, extras=[x2], vecs=[g1b, P["o_b"]])
    h1f = _norm_mod_fwd("norm_ffn_1", x3, nfw1, sc2b, sh2b)
    a1, b1, f1, y1, x4 = _ffn_fwd("1", h1f, W["gate_wt1"], W["up_wt1"], W["down_w1"], x3, g2b)

    dx, sq, g["final_norm_w"] = _loss_head(x4, tgt, P["final_norm_w"])

    dh, dg2b, dwg1, dwu1, dwd1 = _ffn_bwd("1", dx, h1f, a1, b1, f1, y1, W["gate_wt1"], W["up_wt1"],
                                          W["down_w1"], g2b)
    dx, dsh2b, dsc2b, dnfw1 = _norm_mod_bwd("norm_ffn_bwd_1", x3, dh, dx, nfw1, sc2b)
    dmix, dg1b, g["o_b"] = _gate_bwd("mix_gate_bwd_1", dx, mix1, g1b)
    G["o_w"] = _mm_tn("o_dw", att, dmix)
    d_att = _mm("o_dx", [dmix], [W["o_w"]], "nt", [F32])[0]
    dq, dkv, dbias, dsinks, dbq, dbkv = _attn_bwd(qkv, d_att, lse, bias, sinks)
    g["rel_table"] = _rel_bias_bwd(dbias.reshape(16, L * 2 * L), onehot_t).T
    g["sinks"] = dsinks[:, :16]
    g["qkv_b"] = jnp.concatenate([dbq, dbkv], axis=1)
    w_q, w_kv = W["qkv_wt"][:1024], W["qkv_wt"][1024:]
    G["qkv_wt"] = jnp.concatenate([_mm_tn("qkv_dwq", dq, h1), _mm_tn("qkv_dwkv", dkv, h1)], axis=0)
    dh = _mm("qkv_dx", [dq, dkv], [w_q, w_kv], "nn", [F32])[0]
    dx, dsh1b, dsc1b, dnmw1 = _norm_mod_bwd("norm_mix_bwd_1", x2, dh, dx, nmw1, sc1b)
    behind = io["grads1"]({"qkv_wt": G.pop("qkv_wt"), "o_w": G.pop("o_w"), "gate_wt1": dwg1, "up_wt1": dwu1,
                           "down_w1": dwd1})

    dh, dg2, dwg0, dwu0, dwd0 = _ffn_bwd("0", dx, h0f, a0, b0, f0, y0, W["gate_wt0"], W["up_wt0"],
                                         W["down_w0"], g2, after=behind)
    behind = io["grads_ffn0"]({"gate_wt0": dwg0, "up_wt0": dwu0, "down_w0": dwd0})
    dx, dsh2, dsc2, dnfw0 = _norm_mod_bwd("norm_ffn_bwd_0", x1, dh, dx, nfw0, sc2, after=behind)
    dmix, dg1, _ = _gate_bwd("mix_gate_bwd_0", dx, mix0, g1)
    G["out_w"] = jnp.concatenate([_mm_tn("out_dwa", ya, dmix), _mm_tn("out_dwb", yb, dmix)], axis=0)
    dya = _mm("out_dxa", [dmix], [w_oa], "nt", [F32])[0]
    dyb = _mm("out_dxb", [dmix], [w_ob], "nt", [F32])[0]
    du, dv, dws, dbse, g["gmlp_ln_w"], g["gmlp_ln_b"] = _gmlp_bwd(dyb, proj["u"], proj["v"], P["gmlp_ln_w"],
                                                                 P["gmlp_ln_b"], ws, bse)
    g["gmlp_ws"] = dws[None]
    g["gmlp_bs"] = _lane_sum("gmlp_dbs", dbse.reshape(8 * L, 128)).reshape(1, 8, L)
    dz, dxc, ddt, g["ssm_norm_w"], ddsk, dalog, ddtb = _ssd_bwd(dya, y_ssd, proj["z"], xc, proj["dt"], prev,
                                                                dtb, alog, dskl, P["ssm_norm_w"])
    g["d_skip"], g["a_log"], g["dt_bias"] = ddsk[0:1, :16], dalog[:, :16], ddtb[:, :16]
    dxr, dconv_w, g["conv_b"] = _conv_bwd(dxc, pre, proj["xbc"], conv_w)
    g["conv_w"] = dconv_w[None]
    dsegs = {"z": dz, "xbc": dxr, "dt": ddt, "u": du, "v": dv}
    dws_in = {k: _mm_tn(f"in_dw_{k}", d, h0) for k, d in dsegs.items()}
    G["in_wt"] = jnp.concatenate([dws_in["z"], dws_in["xbc"], dws_in["dt"][:16], dws_in["u"], dws_in["v"]], axis=0)
    keys = ["z", "xbc", "dt", "u", "v"]
    dh = _mm("in_dx", [dsegs[k] for k in keys], [segs[k] for k in keys], "nn", [F32])[0]
    dx, dsh1, dsc1, dnmw0 = _norm_mod_bwd("norm_mix_bwd_0", x, dh, dx, nmw0, sc1)

    g["norm_mix_w"] = jnp.concatenate([dnmw0, dnmw1], axis=0)
    g["norm_ffn_w"] = jnp.concatenate([dnfw0, dnfw1], axis=0)
    dmod = jnp.concatenate([jnp.concatenate([dsh1, dsc1, dg1, dsh2, dsc2, dg2], axis=1),
                            jnp.concatenate([dsh1b, dsc1b, dg1b, dsh2b, dsc2b, dg2b], axis=1)], axis=0)
    return sq, dx, dmod, G, g


def _ada_fwd(c_all, ada_w, ada_b):
    n = ada_w.shape[2]
    tn = _col_tile(n, 512)

    def body(c_ref, w_ref, b_ref, o_ref):
        cc = c_ref[...]
        o_ref[...] = lax.dot_general(cc * _sigmoid(cc), w_ref[...], NN, precision=lax.Precision.HIGHEST,
                                     preferred_element_type=F32) + b_ref[...]

    return pl.pallas_call(
        body, name="ada_fwd", grid=(2, n // tn),
        in_specs=[pl.BlockSpec((8, D), lambda l, j: (0, 0)), pl.BlockSpec((None, D, tn), lambda l, j: (l, 0, j)),
                  pl.BlockSpec((None, 1, tn), lambda l, j: (l, 0, j))],
        out_specs=pl.BlockSpec((None, 8, tn), lambda l, j: (l, 0, j)),
        out_shape=jax.ShapeDtypeStruct((2, 8, n), F32), compiler_params=_params(("parallel", "parallel")))(
            c_all, ada_w, ada_b)


def _ada_bwd(c_all, dmod_cols, dmod_all):
    n = dmod_cols.shape[2]
    tn = _col_tile(n, 512)

    def body(c_ref, d_ref, o_ref):
        cc = c_ref[...]
        o_ref[...] = lax.dot_general(cc * _sigmoid(cc), d_ref[...], TN, precision=lax.Precision.HIGHEST,
                                     preferred_element_type=F32)

    dw = pl.pallas_call(
        body, name="ada_dw", grid=(2, n // tn),
        in_specs=[pl.BlockSpec((8, D), lambda l, j: (0, 0)), pl.BlockSpec((None, 8, tn), lambda l, j: (l, 0, j))],
        out_specs=pl.BlockSpec((None, D, tn), lambda l, j: (l, 0, j)),
        out_shape=jax.ShapeDtypeStruct((2, D, n), F32), compiler_params=_params(("parallel", "parallel")))(
            c_all, dmod_cols)

    def sum_body(d_ref, o_ref):
        o_ref[...] = jnp.sum(d_ref[...], axis=0, keepdims=True)

    db = pl.pallas_call(
        sum_body, name="ada_db", grid=(2,),
        in_specs=[pl.BlockSpec((None, 8, 6 * D), lambda l: (l, 0, 0))],
        out_specs=pl.BlockSpec((None, 1, 6 * D), lambda l: (l, 0, 0)),
        out_shape=jax.ShapeDtypeStruct((2, 1, 6 * D), F32), compiler_params=_params(("parallel",)))(dmod_all)
    return dw, db


def _row_tile(rows, cap=512, mult=8):
    best = rows
    for t in range(mult, min(rows, cap) + 1, mult):
        if rows % t == 0:
            best = t
    return best


def _adamw(name, w, g, m, v):
    def fn(w, g, m, v):
        m = ADAM_B1 * m + (1.0 - ADAM_B1) * g
        v = ADAM_B2 * v + (1.0 - ADAM_B2) * (g * g)
        m_hat = m / (1.0 - ADAM_B1 ** ADAM_STEP)
        v_hat = v / (1.0 - ADAM_B2 ** ADAM_STEP)
        return -ADAM_LR * (m_hat / (jnp.sqrt(v_hat) + ADAM_EPS) + ADAM_WD * w), m, v
    cols = w.shape[1]
    return _rowwise(name, fn, [w, g, m, v], [], [(cols, F32)] * 3, tr=_row_tile(w.shape[0]))


def _place():
    return lax.axis_index("x"), lax.axis_index("y"), lax.axis_index("c")


VMEM_SPEC = pl.BlockSpec(memory_space=pltpu.VMEM)


def _allreduce_small(name, buf, after=None):
    rows = buf.shape[0]
    deps = [] if after is None else [after]

    def body(x_ref, *rest):
        o_ref, stage, send_sems, recv_sems = rest[len(deps):]
        x, y, c = _place()
        me = 4 * x + 2 * y + c
        stage[me] = x_ref[...]
        copies = []
        for k in range(1, 8):
            peer = (1 - x if k & 4 else x, 1 - y if k & 2 else y, 1 - c if k & 1 else c)
            cp = pltpu.make_async_remote_copy(src_ref=x_ref, dst_ref=stage.at[me], send_sem=send_sems.at[k - 1],
                                              recv_sem=recv_sems.at[k - 1], device_id=peer, device_id_type=MESH)
            cp.start()
            copies.append(cp)
        for cp in copies:
            cp.wait()
        acc = stage[0]
        for d in range(1, 8):
            acc = acc + stage[d]
        o_ref[...] = acc

    return pl.pallas_call(
        body, name=name, in_specs=[VMEM_SPEC] + [ANY for _ in deps], out_specs=VMEM_SPEC,
        out_shape=jax.ShapeDtypeStruct((rows, 128), F32),
        scratch_shapes=[pltpu.VMEM((8, rows, 128), F32), pltpu.SemaphoreType.DMA((7,)), pltpu.SemaphoreType.DMA((7,))],
        compiler_params=pltpu.CompilerParams(vmem_limit_bytes=_VMEM_LIMIT))(buf, *deps)


OTHER_CHIPS = ((1, 0), (0, 1), (1, 1))


def _allgather_big(wp, after=None):
    rows = wp.shape[0]
    half = rows // 2
    deps = [] if after is None else [after]

    def body(w_ref, *rest):
        o_ref, send_sems, recv_sems = rest[len(deps):]
        x, y, c = _place()
        k = 2 * x + y
        mine = pl.ds(pl.multiple_of(c * half, 8), half)
        first = []
        for j, (fx, fy) in enumerate(OTHER_CHIPS):
            cp = pltpu.make_async_remote_copy(src_ref=w_ref.at[mine], dst_ref=o_ref.at[k, mine],
                                              send_sem=send_sems.at[j], recv_sem=recv_sems.at[j],
                                              device_id=(1 - x if fx else x, 1 - y if fy else y, c),
                                              device_id_type=MESH)
            cp.start()
            first.append(cp)
        for cp in first:
            cp.wait_recv()
        swap = pltpu.make_async_remote_copy(src_ref=o_ref.at[:, mine], dst_ref=o_ref.at[:, mine],
                                            send_sem=send_sems.at[3], recv_sem=recv_sems.at[3],
                                            device_id=(x, y, 1 - c), device_id_type=MESH)
        swap.start()
        swap.wait()
        for cp in first:
            cp.wait_send()

    gathered = pl.pallas_call(
        body, name="allgather_weights", in_specs=[ANY] + [ANY for _ in deps], out_specs=ANY,
        out_shape=jax.ShapeDtypeStruct((4, rows, 1024), wp.dtype),
        scratch_shapes=[pltpu.SemaphoreType.DMA((4,)), pltpu.SemaphoreType.DMA((4,))])(wp, *deps)
    return _with_own_share(gathered, wp)


SIBLING_COLLECTIVE_ID = 6


def _sibling_handshake():
    x, y, c = _place()
    barrier = pltpu.get_barrier_semaphore()
    pl.semaphore_signal(barrier, inc=1, device_id=(x, y, 1 - c), device_id_type=MESH)
    pl.semaphore_wait(barrier, 1)


def _sibling_swap(name, src, halves):
    half = src.shape[-2] // 2
    out_shape = (src.shape[0], half, 1024) if halves else src.shape

    def body(s_ref, o_ref, send_sem, recv_sem):
        x, y, c = _place()
        _sibling_handshake()
        part = s_ref.at[:, pl.ds(pl.multiple_of((1 - c) * half, 8), half)] if halves else s_ref
        cp = pltpu.make_async_remote_copy(src_ref=part, dst_ref=o_ref, send_sem=send_sem, recv_sem=recv_sem,
                                          device_id=(x, y, 1 - c), device_id_type=MESH)
        cp.start()
        cp.wait()

    return pl.pallas_call(
        body, name=name, in_specs=[ANY], out_specs=ANY, out_shape=jax.ShapeDtypeStruct(out_shape, src.dtype),
        scratch_shapes=[pltpu.SemaphoreType.DMA, pltpu.SemaphoreType.DMA],
        compiler_params=pltpu.CompilerParams(collective_id=SIBLING_COLLECTIVE_ID))(src)


HBM = pl.BlockSpec(memory_space=pltpu.HBM)
SEM = pl.BlockSpec(memory_space=pltpu.SEMAPHORE)


def _chip_copies(mode, src_ref, land_ref, send_sems, recv_sems):
    x, y, c = _place()
    k = 2 * x + y
    copies = []
    for j, (fx, fy) in enumerate(OTHER_CHIPS):
        px, py = (1 - x if fx else x), (1 - y if fy else y)
        if mode == "gather":
            half = src_ref.shape[0] // 2
            mine = pl.ds(pl.multiple_of(c * half, 16), half)
            src, dst = src_ref.at[mine], land_ref.at[k, mine]
        else:
            src, dst = src_ref.at[2 * px + py], land_ref.at[k]
        copies.append(pltpu.make_async_remote_copy(src_ref=src, dst_ref=dst, send_sem=send_sems.at[j],
                                                   recv_sem=recv_sems.at[j], device_id=(px, py, c),
                                                   device_id_type=MESH))
    return copies


def _exchange_start(name, collective_id, mode, src, land, after=None):
    deps = [] if after is None else [after]

    def body(s_ref, l_ref, *rest):
        send_sems, recv_sems, s_thru, l_thru, token = rest[len(deps):]
        x, y, c = _place()
        barrier = pltpu.get_barrier_semaphore()
        for fx, fy in OTHER_CHIPS:
            pl.semaphore_signal(barrier, inc=1, device_id=(1 - x if fx else x, 1 - y if fy else y, c),
                                device_id_type=MESH)
        pl.semaphore_wait(barrier, 3)
        for cp in _chip_copies(mode, s_ref, l_ref, send_sems, recv_sems):
            cp.start()
        token[...] = jnp.zeros_like(token)

    return pl.pallas_call(
        body, name=name,
        out_shape=(pltpu.SemaphoreType.DMA((3,)), pltpu.SemaphoreType.DMA((3,)), pltpu.HBM(src.shape, src.dtype),
                   pltpu.HBM(land.shape, land.dtype), jax.ShapeDtypeStruct((8, 128), F32)),
        in_specs=(HBM, HBM) + tuple(ANY for _ in deps), out_specs=(SEM, SEM, HBM, HBM, VMEM_SPEC),
        input_output_aliases={0: 2, 1: 3},
        compiler_params=pltpu.CompilerParams(has_side_effects=pltpu.SideEffectType.DATAFLOW_SIDE_EFFECTING,
                                             collective_id=collective_id))(
            pltpu.with_memory_space_constraint(src, pltpu.HBM), pltpu.with_memory_space_constraint(land, pltpu.HBM),
            *deps)


def _exchange_wait(name, mode, started, after):
    send_sems, recv_sems, s_thru, l_thru, _ = started
    deps = list(after) if isinstance(after, (tuple, list)) else [after]

    def body(s_ref, l_ref, send_sems, recv_sems, *rest):
        for cp in _chip_copies(mode, s_ref, l_ref, send_sems, recv_sems):
            cp.wait_send()
            cp.wait_recv()

    return pl.pallas_call(
        body, name=name, out_shape=(pltpu.HBM(s_thru.shape, s_thru.dtype), pltpu.HBM(l_thru.shape, l_thru.dtype)),
        in_specs=(HBM, HBM, SEM, SEM) + tuple(ANY for _ in deps), out_specs=(HBM, HBM),
        input_output_aliases={0: 0, 1: 1},
        compiler_params=pltpu.CompilerParams(has_side_effects=pltpu.SideEffectType.DATAFLOW_SIDE_EFFECTING))(
            s_thru, l_thru, send_sems, recv_sems, *deps)


def _with_own_share(gathered, share):
    slot = lax.broadcasted_iota(jnp.int32, (4, 1, 1), 0)
    return jnp.where(slot == 2 * lax.axis_index("x") + lax.axis_index("y"), share[None], gathered)


def _allgather_finish(tag, share, land):
    rows = share.shape[0]
    half = rows // 2

    def body(l_ref, o_ref, send_sem, recv_sem):
        x, y, c = _place()
        _sibling_handshake()
        mine = pl.ds(pl.multiple_of(c * half, 16), half)
        swap = pltpu.make_async_remote_copy(src_ref=o_ref.at[:, mine], dst_ref=o_ref.at[:, mine], send_sem=send_sem,
                                            recv_sem=recv_sem, device_id=(x, y, 1 - c), device_id_type=MESH)
        swap.start()
        swap.wait()

    swapped = pl.pallas_call(
        body, name="allgather_finish_" + tag, in_specs=[ANY], out_specs=ANY, input_output_aliases={0: 0},
        out_shape=jax.ShapeDtypeStruct(land.shape, land.dtype),
        scratch_shapes=[pltpu.SemaphoreType.DMA, pltpu.SemaphoreType.DMA],
        compiler_params=pltpu.CompilerParams(collective_id=SIBLING_COLLECTIVE_ID))(land)
    return _with_own_share(swapped, share)


def _pair_sum(tag, g, r1, c):
    rows = g.shape[1]
    half = rows // 2
    th = _row_tile(half, 256, 16)
    nblk = half // th

    def body(c_ref, g_ref, r_ref, o_ref, o2_ref):
        o_ref[...] = (g_ref[...] + r_ref[...]).astype(o_ref.dtype)
        o2_ref[...] = o_ref[...]

    spec = pl.BlockSpec((None, th, 1024), lambda k, i, c_ref: (k, i, 0))
    grid_spec = pltpu.PrefetchScalarGridSpec(
        num_scalar_prefetch=1, grid=(4, nblk),
        in_specs=[pl.BlockSpec((None, th, 1024), lambda k, i, c_ref: (k, c_ref[0] * nblk + i, 0)), spec],
        out_specs=[spec, spec])
    return pl.pallas_call(body, name="grad_pair_sum_" + tag, grid_spec=grid_spec,
                          out_shape=[jax.ShapeDtypeStruct((4, half, 1024), BF16)] * 2,
                          compiler_params=_params(("parallel", "parallel")))(c, g, r1)


def _chip_sum(tag, q, after=None):
    half = q.shape[1]
    th = _row_tile(half, 256, 16)
    deps = [] if after is None else [after]

    def body(a, b, c, d, *rest):
        rest[-1][...] = ((a[...].astype(F32) + b[...].astype(F32)) + c[...].astype(F32)) + d[...].astype(F32)

    specs = [pl.BlockSpec((None, th, 1024), functools.partial(lambda i, k: (k, i, 0), k=k)) for k in range(4)]
    return pl.pallas_call(body, name="grad_chip_sum_" + tag, grid=(half // th,), in_specs=specs + [ANY for _ in deps],
                          out_specs=pl.BlockSpec((th, 1024), lambda i: (i, 0)),
                          out_shape=jax.ShapeDtypeStruct((half, 1024), F32),
                          compiler_params=_params(("parallel",)))(q, q, q, q, *deps)


def _join_halves(tag, f, r, c):
    half = f.shape[0]
    th = _row_tile(half, 256)
    nblk = half // th

    def body(c_ref, f_ref, r_ref, o_ref):
        mine = (pl.program_id(0) == c_ref[0])
        o_ref[...] = jnp.where(mine, f_ref[...], r_ref[...])

    spec = pl.BlockSpec((th, 1024), lambda h, i, c_ref: (i, 0))
    grid_spec = pltpu.PrefetchScalarGridSpec(
        num_scalar_prefetch=1, grid=(2, nblk), in_specs=[spec, spec],
        out_specs=pl.BlockSpec((th, 1024), lambda h, i, c_ref: (h * nblk + i, 0)))
    return pl.pallas_call(body, name="grad_join_halves_" + tag, grid_spec=grid_spec,
                          out_shape=jax.ShapeDtypeStruct((2 * half, 1024), F32),
                          compiler_params=_params(("parallel", "parallel")))(c, f, r)


BIG_ARGS = ("in_w_even", "out_w_even", "qkv_w", "o_w", "ffn_gate_w", "ffn_up_w", "ffn_down_w")
def _ffn_pieces(layer):
    return tuple((f"{n}{layer}", 704, 704) for n in ("gate_wt", "up_wt", "down_w"))


IN_SLAB = (("in_wt", 1156, 1184),)
LAYER0_REST_SLAB = (("out_w", 512, 512),) + _ffn_pieces(0)
LAYER1_SLAB = (("qkv_wt", 320, 320), ("o_w", 256, 256)) + _ffn_pieces(1)
FFN0_SLAB = _ffn_pieces(0)
MIXER0_SLAB = (("in_wt", 1156, 1280), ("out_w", 512, 512))


def _slab(pieces, spec):
    parts = []
    for name, rows, room in spec:
        p = pieces[name]
        parts.append(jnp.pad(p, [(0, 0)] * (p.ndim - 2) + [(0, room - rows), (0, 0)]) if room > rows else p)
    return jnp.concatenate(parts, axis=-2) if len(parts) > 1 else parts[0]


def _unslab(slab, spec):
    out, off = {}, 0
    for name, rows, room in spec:
        out[name] = slab[..., off:off + rows, :]
        off += room
    return out


def _share_pieces(w):
    return {"in_wt": w["in_w_even"][0].T, "out_w": w["out_w_even"][0], "qkv_wt": w["qkv_w"][0].T, "o_w": w["o_w"][0],
            "gate_wt0": w["ffn_gate_w"][0].T, "gate_wt1": w["ffn_gate_w"][1].T,
            "up_wt0": w["ffn_up_w"][0].T, "up_wt1": w["ffn_up_w"][1].T,
            "down_w0": w["ffn_down_w"][0], "down_w1": w["ffn_down_w"][1]}


def _pieces_to_shares(p):
    return {"in_w_even": p["in_wt"].T[None], "out_w_even": p["out_w"][None], "qkv_w": p["qkv_wt"].T[None],
            "o_w": p["o_w"][None], "ffn_gate_w": jnp.stack([p["gate_wt0"].T, p["gate_wt1"].T]),
            "ffn_up_w": jnp.stack([p["up_wt0"].T, p["up_wt1"].T]),
            "ffn_down_w": jnp.stack([p["down_w0"], p["down_w1"]])}


def _whole_from_chips(p):
    return {k: v.reshape(-1, D) for k, v in p.items()}


def _chips_from_full(G, spec):
    return _slab({k: v.reshape(4, -1, D) for k, v in G.items()}, spec)


def _pack_small(parts):
    padded = []
    for p in parts:
        p = p.reshape(-1).astype(F32)
        padded.append(jnp.pad(p, (0, (-p.shape[0]) % 1024)))
    return jnp.concatenate(padded).reshape(-1, 128)


def _unpack_small(slab, shapes):
    flat, out, off = slab.reshape(-1), [], 0
    for shp in shapes:
        size = math.prod(shp)
        out.append(flat[off:off + size].reshape(shp))
        off += size + (-size) % 1024
    return out


SMALL = ("ada_b", "norm_mix_w", "norm_ffn_w", "conv_w", "conv_b", "dt_bias", "a_log", "d_skip", "ssm_norm_w",
         "gmlp_ln_w", "gmlp_ln_b", "gmlp_ws", "gmlp_bs", "qkv_b", "o_b", "sinks", "rel_table", "final_norm_w")
SMALL_SPLIT = {"conv_w": 1536, "qkv_b": 1280, "o_b": 1024}
WEIGHTS = ("ada_w", "ada_b", "norm_mix_w", "norm_ffn_w", "in_w_even", "conv_w", "conv_b", "dt_bias", "a_log", "d_skip",
           "ssm_norm_w", "gmlp_ln_w", "gmlp_ln_b", "gmlp_ws", "gmlp_bs", "out_w_even", "qkv_w", "qkv_b", "o_w", "o_b",
           "sinks", "rel_table", "ffn_gate_w", "ffn_up_w", "ffn_down_w", "final_norm_w")


def kernel(x, c, ada_w, ada_b, norm_mix_w, norm_ffn_w, in_w_even, conv_w, conv_b, dt_bias, a_log, d_skip, ssm_norm_w, gmlp_ln_w, gmlp_ln_b, gmlp_ws, gmlp_bs, out_w_even, qkv_w, qkv_b, o_w, o_b, sinks, rel_table, ffn_gate_w, ffn_up_w, ffn_down_w, final_norm_w, loss_target, m_ada_w, m_ada_b, m_norm_mix_w, m_norm_ffn_w, m_in_w_even, m_conv_w, m_conv_b, m_dt_bias, m_a_log, m_d_skip, m_ssm_norm_w, m_gmlp_ln_w, m_gmlp_ln_b, m_gmlp_ws, m_gmlp_bs, m_out_w_even, m_qkv_w, m_qkv_b, m_o_w, m_o_b, m_sinks, m_rel_table, m_ffn_gate_w, m_ffn_up_w, m_ffn_down_w, m_final_norm_w, v_ada_w, v_ada_b, v_norm_mix_w, v_norm_ffn_w, v_in_w_even, v_conv_w, v_conv_b, v_dt_bias, v_a_log, v_d_skip, v_ssm_norm_w, v_gmlp_ln_w, v_gmlp_ln_b, v_gmlp_ws, v_gmlp_bs, v_out_w_even, v_qkv_w, v_qkv_b, v_o_w, v_o_b, v_sinks, v_rel_table, v_ffn_gate_w, v_ffn_up_w, v_ffn_down_w, v_final_norm_w):
    args = dict(locals())
    w = {n: args[n] for n in WEIGHTS}
    m = {n: args["m_" + n] for n in WEIGHTS}
    v = {n: args["v_" + n] for n in WEIGHTS}
    ax, ay, ac = _place()
    me = 4 * ax + 2 * ay + ac
    chip = 2 * ax + ay
    south = (ac == 0).astype(F32)
    c_arr = jnp.reshape(ac, (1,)).astype(jnp.int32)

    c_all = _allreduce_small("gather_cond", lax.dynamic_update_slice(jnp.zeros((8, D), F32), c, (me, 0)).reshape(64, 128))
    c_all = c_all.reshape(8, D)
    n_ada = ada_w.shape[2]
    mod_cols = _ada_fwd(c_all, ada_w, lax.dynamic_slice(ada_b, (0, chip * n_ada), (2, n_ada)).reshape(2, 1, n_ada))
    pieces = [lax.dynamic_update_slice(jnp.zeros((2, 8, 6 * D), F32), mod_cols, (0, 0, chip * n_ada))]
    split_names = list(SMALL_SPLIT)
    for n in split_names:
        full = SMALL_SPLIT[n]
        local = w[n]
        idx = (0,) * (local.ndim - 1) + (chip * local.shape[-1],)
        pieces.append(lax.dynamic_update_slice(jnp.zeros(local.shape[:-1] + (full,), F32), local, idx))
    shapes = [p.shape for p in pieces]
    mod_slab = _allreduce_small("gather_mod", _pack_small(pieces) * south)
    gathered = _unpack_small(mod_slab, shapes)
    mod = lax.dynamic_slice(gathered[0], (0, me, 0), (2, 1, 6 * D)).reshape(2, 6 * D)
    P = {n: w[n] for n in SMALL if n not in SMALL_SPLIT and n != "ada_b"}
    for n, full in zip(split_names, gathered[1:]):
        P[n] = full
    P["final_norm_w"] = final_norm_w.reshape(1, D)

    cast = {k: p.astype(_MXU) for k, p in _share_pieces(w).items()}
    in_slab = _allgather_big(_slab(cast, IN_SLAB), after=mod_slab)
    w_in = _unslab(in_slab, IN_SLAB)["in_wt"].reshape(4 * 1156, D)

    def start_gather(tag, collective_id, spec, after):
        share = _slab(cast, spec)
        return _exchange_start("allgather_start_" + tag, collective_id, "gather", share,
                               lax.empty((4,) + share.shape, share.dtype), after=after)

    def finish_gather(tag, started, spec, after):
        share, land = _exchange_wait("allgather_wait_" + tag, "gather", started, after)
        return _whole_from_chips(_unslab(_allgather_finish(tag, share, land), spec))

    gather0 = start_gather("0", 1, LAYER0_REST_SLAB, in_slab)
    gather1 = start_gather("1", 2, LAYER1_SLAB, gather0[4])

    def start_reduce(tag, collective_id, G, spec, after=None):
        gp = _chips_from_full(G, spec)
        p, q = _pair_sum(tag, gp, _sibling_swap("grad_pair_exchange_" + tag, gp, True), c_arr)
        return _exchange_start("grad_exchange_start_" + tag, collective_id, "scatter", p, q, after=after)

    def finish_reduce(tag, started, spec, after, behind=None):
        q = _exchange_wait("grad_exchange_wait_" + tag, "scatter", started, after)[1]
        fin = _chip_sum(tag, q, after=behind)
        total = _join_halves(tag, fin, _sibling_swap("grad_final_exchange_" + tag, fin, False), c_arr)
        return _unslab(total, spec)

    reduces = {}

    def grads1(G1):
        reduces["1"] = start_reduce("1", 3, G1, LAYER1_SLAB)
        return reduces["1"][4]

    def grads_ffn0(G):
        reduces["f"] = start_reduce("f", 4, G, FFN0_SLAB)
        return reduces["f"][4]

    io = {"start": gather1[4],
          "weights0": lambda after: finish_gather("0", gather0, LAYER0_REST_SLAB, after),
          "weights1": lambda after: finish_gather("1", gather1, LAYER1_SLAB, after),
          "grads1": grads1, "grads_ffn0": grads_ffn0}
    sq, grad_x, dmod, G0, g = _local_step(x[0], loss_target[0], mod, w_in, P, io)
    loss = lax.psum(0.5 * sq[0, 0] / D, ("x", "y", "c"))

    g["final_norm_w"] = g["final_norm_w"].reshape(D)
    small_names = [n for n in SMALL if n != "ada_b"]
    pieces = [lax.dynamic_update_slice(jnp.zeros((2, 8, 6 * D), F32), dmod.reshape(2, 1, 6 * D), (0, me, 0))]
    pieces += [g[n] for n in small_names]
    shapes = [p.shape for p in pieces]
    small_slab = _allreduce_small("allreduce_small_grads", _pack_small(pieces))
    reduces["m"] = start_reduce("m", 5, G0, MIXER0_SLAB, after=small_slab)
    shares = finish_reduce("1", reduces["1"], LAYER1_SLAB, grad_x, behind=reduces["m"][4])
    shares.update(finish_reduce("f", reduces["f"], FFN0_SLAB, grad_x, behind=reduces["m"][4]))
    reduced = _unpack_small(small_slab, shapes)
    dmod_all = reduced[0]
    grads = dict(zip(small_names, reduced[1:]))
    for n in split_names:
        full = grads[n]
        size = w[n].shape[-1]
        grads[n] = lax.dynamic_slice(full, (0,) * (full.ndim - 1) + (chip * size,), full.shape[:-1] + (size,))
    grads = {n: grads[n].reshape(w[n].shape) for n in small_names}
    dw_ada, db_ada = _ada_bwd(c_all, lax.dynamic_slice(dmod_all, (0, 0, chip * n_ada), (2, 8, n_ada)), dmod_all)
    grads["ada_w"], grads["ada_b"] = dw_ada, db_ada.reshape(2, 6 * D)

    delta, new_m, new_v = {}, {}, {}

    def update(n):
        cols = w[n].shape[-1]
        d_, m_, v_ = _adamw("adamw_" + n, w[n].reshape(-1, cols), grads[n].reshape(-1, cols), m[n].reshape(-1, cols),
                            v[n].reshape(-1, cols))
        delta[n], new_m[n], new_v[n] = d_.reshape(w[n].shape), m_.reshape(w[n].shape), v_.reshape(w[n].shape)

    update("ada_w")
    shapes = [w[n].shape for n in SMALL]
    packed = [_pack_small([t[n] for n in SMALL]) for t in (w, grads, m, v)]
    outs = _adamw("adamw_small", *packed)
    for dst, slab in zip((delta, new_m, new_v), outs):
        for n, t in zip(SMALL, _unpack_small(slab, shapes)):
            dst[n] = t
    shares.update(finish_reduce("m", reduces["m"], MIXER0_SLAB, outs[0]))
    grads.update(_pieces_to_shares(shares))
    for n in BIG_ARGS:
        update(n)
    return (loss, grad_x[None], *[grads[n] for n in WEIGHTS], *[delta[n] for n in WEIGHTS],
            *[new_m[n] for n in WEIGHTS], *[new_v[n] for n in WEIGHTS])
```

```python
import functools
import math

import numpy as np
import jax
import jax.numpy as jnp
from jax import lax
from jax.experimental import pallas as pl
from jax.experimental.pallas import tpu as pltpu

F32 = jnp.float32
BF16 = jnp.bfloat16
_MXU = jnp.bfloat16
_VMEM_LIMIT = 56 * 1024 * 1024
MXU_COLS = 256
D = 1024
L = 128
NSTATE = 128
EPS = 1e-6
NEG_INF = -1e30
FFN = 2816
ADAM_LR, ADAM_B1, ADAM_B2, ADAM_EPS, ADAM_WD, ADAM_STEP = 0.001, 0.9, 0.999, 1e-08, 0.01, 10
MESH = pl.DeviceIdType.MESH
ANY = pl.BlockSpec(memory_space=pl.ANY)

NN = (((1,), (0,)), ((), ()))
NT = (((1,), (1,)), ((), ()))
TN = (((0,), (0,)), ((), ()))


def _dot(a, b, dn=NN):
    return lax.dot_general(a.astype(_MXU), b.astype(_MXU), dn, preferred_element_type=F32)


def _params(sem=None):
    return pltpu.CompilerParams(dimension_semantics=sem, vmem_limit_bytes=_VMEM_LIMIT)


def _sigmoid(x):
    return 1.0 / (1.0 + jnp.exp(-x))


def _softplus(x):
    return jnp.maximum(x, 0.0) + jnp.log(1.0 + jnp.exp(-jnp.abs(x)))


def _gelu(x):
    return 0.5 * x * (1.0 + lax.erf(x * (2.0 ** -0.5)))


def _gelu_grad(x):
    return 0.5 * (1.0 + lax.erf(x * (2.0 ** -0.5))) + x * jnp.exp(-0.5 * x * x) * (1.0 / math.sqrt(2.0 * math.pi))


def _silu_grad(a):
    sg = _sigmoid(a)
    return sg * (1.0 + a * (1.0 - sg))


def _rowwise(name, fn, rows, vecs, out_rows, out_accs=(), tr=512, after=None):
    S = rows[0].shape[0]
    tr = min(tr, S)
    assert S % tr == 0
    nr, nv, no, na = len(rows), len(vecs), len(out_rows), len(out_accs)
    deps = [] if after is None else [after]

    def body(*refs):
        ins, outs = refs[:nr + nv], refs[nr + nv + len(deps):]
        res = fn(*[r[...] for r in ins])
        if not isinstance(res, (tuple, list)):
            res = (res,)
        for k in range(no):
            outs[k][...] = res[k].astype(outs[k].dtype)
        if na:
            @pl.when(pl.program_id(0) == 0)
            def _():
                for k in range(na):
                    outs[no + k][...] = jnp.zeros_like(outs[no + k])
            for k in range(na):
                outs[no + k][...] += res[no + k]

    in_specs = [pl.BlockSpec((tr, a.shape[1]), lambda i: (i, 0)) for a in rows]
    in_specs += [pl.BlockSpec(v.shape, lambda i: (0, 0)) for v in vecs] + [ANY for _ in deps]
    out_specs = [pl.BlockSpec((tr, c), lambda i: (i, 0)) for c, _ in out_rows]
    out_specs += [pl.BlockSpec(s, lambda i: (0, 0)) for s in out_accs]
    out_shape = [jax.ShapeDtypeStruct((S, c), dt) for c, dt in out_rows]
    out_shape += [jax.ShapeDtypeStruct(s, F32) for s in out_accs]
    return pl.pallas_call(body, name=name, grid=(S // tr,), in_specs=in_specs, out_specs=out_specs,
                          out_shape=out_shape, compiler_params=_params(("arbitrary",)))(*rows, *vecs, *deps)


def _col_tile(n, cap):
    if n <= cap or n % 128:
        return n
    best = 128
    for t in range(128, cap + 1, 128):
        if n % t == 0:
            best = t
    return best


def _mm(name, As, Bs, mode, outs, epi=None, groups=None, extras=(), vecs=(), tm=512, tn_cap=1536):
    M = As[0].shape[0]
    N = Bs[0].shape[1] if mode == "nn" else Bs[0].shape[0]
    tm = min(tm, M)
    tn = _col_tile(N, tn_cap)
    assert M % tm == 0 and N % tn == 0
    npair = len(As)
    groups = groups or [0] * npair
    ng = max(groups) + 1
    nx, nv = len(extras), len(vecs)
    dn = NN if mode == "nn" else NT

    def body(*refs):
        a_refs, b_refs = refs[:npair], refs[npair:2 * npair]
        x_refs = refs[2 * npair:2 * npair + nx]
        v_refs = refs[2 * npair + nx:2 * npair + nx + nv]
        o_refs = refs[2 * npair + nx + nv:]
        step = tn if epi is None else min(tn, MXU_COLS)
        for col in range(0, tn, step):
            sl = slice(col, min(col + step, tn))
            accs = [None] * ng
            for k in range(npair):
                b = b_refs[k][:, sl] if mode == "nn" else b_refs[k][sl, :]
                d = _dot(a_refs[k][...], b, dn)
                accs[groups[k]] = d if accs[groups[k]] is None else accs[groups[k]] + d
            args = accs + [x[:, sl] for x in x_refs] + [v[:, sl] for v in v_refs]
            res = epi(*args) if epi is not None else tuple(accs)
            if not isinstance(res, (tuple, list)):
                res = (res,)
            for o, r in zip(o_refs, res):
                o[:, sl] = r.astype(o.dtype)

    in_specs = [pl.BlockSpec((tm, a.shape[1]), lambda i, j: (i, 0)) for a in As]
    if mode == "nn":
        in_specs += [pl.BlockSpec((b.shape[0], tn), lambda i, j: (0, j)) for b in Bs]
    else:
        in_specs += [pl.BlockSpec((tn, b.shape[1]), lambda i, j: (j, 0)) for b in Bs]
    in_specs += [pl.BlockSpec((tm, tn), lambda i, j: (i, j)) for _ in extras]
    in_specs += [pl.BlockSpec((1, tn), lambda i, j: (0, j)) for _ in vecs]
    out_specs = [pl.BlockSpec((tm, tn), lambda i, j: (i, j)) for _ in outs]
    out_shape = [jax.ShapeDtypeStruct((M, N), dt) for dt in outs]
    return pl.pallas_call(body, name=name, grid=(M // tm, N // tn), in_specs=in_specs, out_specs=out_specs,
                          out_shape=out_shape, compiler_params=_params(("parallel", "parallel")))(
                              *As, *Bs, *extras, *vecs)


def _mm_tn(name, A, B, tk=512, t2_cap=1536):
    S, K1 = A.shape
    N2 = B.shape[1]
    tk = min(tk, S)
    t2 = _col_tile(N2, t2_cap)
    assert S % tk == 0 and N2 % t2 == 0

    def body(a_ref, b_ref, o_ref):
        @pl.when(pl.program_id(1) == 0)
        def _():
            o_ref[...] = jnp.zeros_like(o_ref)
        o_ref[...] += _dot(a_ref[...], b_ref[...], TN)

    return pl.pallas_call(
        body, name=name, grid=(N2 // t2, S // tk),
        in_specs=[pl.BlockSpec((tk, K1), lambda j, k: (k, 0)), pl.BlockSpec((tk, t2), lambda j, k: (k, j))],
        out_specs=pl.BlockSpec((K1, t2), lambda j, k: (0, j)),
        out_shape=jax.ShapeDtypeStruct((K1, N2), F32),
        compiler_params=_params(("parallel", "arbitrary")))(A, B)


def _norm_mod_fwd(name, x, nw, sc, sh, after=None):
    def fn(x, nw, sc, sh):
        rstd = lax.rsqrt(jnp.mean(x * x, axis=-1, keepdims=True) + EPS)
        return (x * rstd * nw) * (1.0 + sc) + sh
    return _rowwise(name, fn, [x], [nw, sc, sh], [(D, BF16)], after=after)[0]


def _norm_mod_bwd(name, x, dh, dres, nw, sc, after=None):
    def fn(x, dh, dres, nw, sc):
        rstd = lax.rsqrt(jnp.mean(x * x, axis=-1, keepdims=True) + EPS)
        xh = x * rstd
        dn = dh * (1.0 + sc)
        dxh = dn * nw
        dx = rstd * (dxh - xh * jnp.mean(dxh * xh, axis=-1, keepdims=True))
        return (dres + dx, jnp.sum(dh, axis=0, keepdims=True), jnp.sum(dh * (xh * nw), axis=0, keepdims=True),
                jnp.sum(dn * xh, axis=0, keepdims=True))
    return _rowwise(name, fn, [x, dh, dres], [nw, sc], [(D, F32)], [(1, D)] * 3, after=after)


def _gate_bwd(name, dx, y, g, after=None):
    def fn(dx, y, g):
        dy = dx * g
        return dy, jnp.sum(dx * y, axis=0, keepdims=True), jnp.sum(dy, axis=0, keepdims=True)
    return _rowwise(name, fn, [dx, y], [g], [(D, BF16)], [(1, D)] * 2, after=after)


def _loss_head(x, tgt, fw):
    def fn(x, tgt, fw):
        rstd = lax.rsqrt(jnp.mean(x * x, axis=-1, keepdims=True) + EPS)
        xh = x * rstd
        err = xh * fw - tgt
        dout = err * (1.0 / D)
        dxh = dout * fw
        dx = rstd * (dxh - xh * jnp.mean(dxh * xh, axis=-1, keepdims=True))
        sq = jnp.sum(jnp.sum(err * err, axis=1, keepdims=True), axis=0, keepdims=True)
        return dx, sq, jnp.sum(dout * xh, axis=0, keepdims=True)
    return _rowwise("loss_head", fn, [x, tgt], [fw], [(D, F32)], [(1, 1), (1, D)])


def _ffn_fwd(tag, h, wg, wu, wd, x, g2):
    def act(a, b):
        return a, b, a * _sigmoid(a) * b
    a, b, f = _mm(f"ffn_up_{tag}", [h, h], [wg, wu], "nt", [BF16, BF16, BF16], epi=act, groups=[0, 1], tn_cap=1408,
                  tm=1024)

    def res(y, x, g):
        return y, x + g * y
    y, xo = _mm(f"ffn_down_{tag}", [f], [wd], "nn", [F32, F32], epi=res, extras=[x], vecs=[g2])
    return a, b, f, y, xo


def _ffn_bwd(tag, dx, h, a, b, f, y, wg, wu, wd, g2, after=None):
    dy, dg2, _ = _gate_bwd(f"ffn_gate_bwd_{tag}", dx, y, g2, after=after)

    def act_bwd(df, a, b):
        a, b = a.astype(F32), b.astype(F32)
        sg = _sigmoid(a)
        return df * b * (sg * (1.0 + a * (1.0 - sg))), df * (a * sg)
    da, db = _mm(f"ffn_dact_{tag}", [dy], [wd], "nt", [BF16, BF16], epi=act_bwd, extras=[a, b], tn_cap=1408, tm=1024)
    dwd = _mm_tn(f"ffn_dwd_{tag}", f, dy)
    dwg = _mm_tn(f"ffn_dwg_{tag}", da, h)
    dwu = _mm_tn(f"ffn_dwu_{tag}", db, h)
    dh = _mm(f"ffn_dh_{tag}", [da, db], [wg, wu], "nn", [F32])[0]
    return dh, dg2, dwg, dwu, dwd


def _conv_fwd(xr, w, b, tb=512):
    S, C = xr.shape
    tb = min(tb, S)

    def body(x_ref, halo_ref, w_ref, b_ref, pre_ref, out_ref):
        i = pl.program_id(0)
        halo = jnp.where(i > 0, halo_ref[...], 0.0)
        xe = jnp.concatenate([halo, x_ref[...]], axis=0)
        pre = w_ref[3:4, :] * x_ref[...] + b_ref[...]
        for j in (1, 2, 3):
            pre = pre + w_ref[3 - j:4 - j, :] * pltpu.roll(xe, j, axis=0)[8:, :]
        pre_ref[...] = pre
        out_ref[...] = pre * _sigmoid(pre)

    return pl.pallas_call(
        body, name="conv_fwd", grid=(S // tb,),
        in_specs=[pl.BlockSpec((tb, C), lambda i: (i, 0)),
                  pl.BlockSpec((8, C), lambda i: (jnp.maximum(i * (tb // 8) - 1, 0), 0)),
                  pl.BlockSpec((4, C), lambda i: (0, 0)), pl.BlockSpec((1, C), lambda i: (0, 0))],
        out_specs=[pl.BlockSpec((tb, C), lambda i: (i, 0))] * 2,
        out_shape=[jax.ShapeDtypeStruct((S, C), F32)] * 2,
        compiler_params=_params(("parallel",)))(xr, xr, w, b)


def _conv_bwd(dxc, pre, xr, w, tb=512):
    S, C = xr.shape
    tb = min(tb, S)
    nblk = S // tb

    def body(d_ref, p_ref, dn_ref, pn_ref, x_ref, xh_ref, w_ref, dx_ref, dw_ref, db_ref):
        i = pl.program_id(0)

        @pl.when(i == 0)
        def _():
            dw_ref[...] = jnp.zeros_like(dw_ref)
            db_ref[...] = jnp.zeros_like(db_ref)

        dpre = d_ref[...] * _silu_grad(p_ref[...])
        dnext = jnp.where(i < nblk - 1, dn_ref[...] * _silu_grad(pn_ref[...]), 0.0)
        pe = jnp.concatenate([dpre, dnext], axis=0)
        dx = w_ref[3:4, :] * dpre
        for j in (1, 2, 3):
            dx = dx + w_ref[3 - j:4 - j, :] * pltpu.roll(pe, tb + 8 - j, axis=0)[:tb, :]
        dx_ref[...] = dx.astype(dx_ref.dtype)
        halo = jnp.where(i > 0, xh_ref[...], 0.0)
        xe = jnp.concatenate([halo, x_ref[...]], axis=0)
        for k in range(3):
            dw_ref[k:k + 1, :] += jnp.sum(dpre * pltpu.roll(xe, 3 - k, axis=0)[8:, :], axis=0, keepdims=True)
        dw_ref[3:4, :] += jnp.sum(dpre * x_ref[...], axis=0, keepdims=True)
        db_ref[...] += jnp.sum(dpre, axis=0, keepdims=True)

    blk = pl.BlockSpec((tb, C), lambda i: (i, 0))
    nxt = pl.BlockSpec((8, C), lambda i: (jnp.minimum((i + 1) * (tb // 8), S // 8 - 1), 0))
    prv = pl.BlockSpec((8, C), lambda i: (jnp.maximum(i * (tb // 8) - 1, 0), 0))
    return pl.pallas_call(
        body, name="conv_bwd", grid=(nblk,),
        in_specs=[blk, blk, nxt, nxt, blk, prv, pl.BlockSpec((4, C), lambda i: (0, 0))],
        out_specs=[blk, pl.BlockSpec((4, C), lambda i: (0, 0)), pl.BlockSpec((1, C), lambda i: (0, 0))],
        out_shape=[jax.ShapeDtypeStruct((S, C), BF16), jax.ShapeDtypeStruct((4, C), F32),
                   jax.ShapeDtypeStruct((1, C), F32)],
        compiler_params=_params(("arbitrary",)))(dxc, pre, dxc, pre, xr, xr, w)


def _iota(shape, dim):
    return lax.broadcasted_iota(jnp.int32, shape, dim)


def _colsel(m, lane, h):
    return jnp.sum(jnp.where(lane == h, m, 0.0), axis=1, keepdims=True)


def _cumsum_rows(v):
    r = _iota(v.shape, 0)
    k = 1
    while k < v.shape[0]:
        v = v + jnp.where(r >= k, pltpu.roll(v, k, axis=0), 0.0)
        k *= 2
    return v


def _suffix_sum_rows(v):
    n = v.shape[0]
    r = _iota(v.shape, 0)
    k = 1
    while k < n:
        v = v + jnp.where(r < n - k, pltpu.roll(v, n - k, axis=0), 0.0)
        k *= 2
    return v


def _ssd_fwd(xc, dtr, z, dtb, alog, dskl, nw):
    S = xc.shape[0]
    nc = S // L

    def body(xc_ref, dtr_ref, z_ref, dtb_ref, alog_ref, dsk_ref, nw_ref, ya_ref, y_ref, prev_ref,
             st_ref, cum_ref, cumT_ref):
        i = pl.program_id(0)

        @pl.when(i == 0)
        def _():
            st_ref[...] = jnp.zeros_like(st_ref)

        lane = _iota((L, 128), 1)
        lane1 = _iota((1, 128), 1)
        lo = lane < 64
        lo1 = lane1 < 64
        tril = _iota((L, L), 0) >= _iota((L, L), 1)
        dt = _softplus(dtr_ref[...] + dtb_ref[...])
        a_neg = -jnp.exp(alog_ref[...])
        cum = _cumsum_rows(dt * a_neg)
        cum_ref[...] = cum
        cumT_ref[...] = cum.T
        last_all = cum_ref[L - 1:L, :]
        prev_t = st_ref[...]
        prev_ref[0] = prev_t
        for g in range(2):
            bg = xc_ref[:, 1024 + g * 128:1152 + g * 128]
            cg = xc_ref[:, 1280 + g * 128:1408 + g * 128]
            gmat = _dot(cg, bg, NT)
            yoff = _dot(cg, prev_t[:, g * 512:(g + 1) * 512])
            bg_t = bg.T
            for jp in range(4):
                j = g * 4 + jp
                sl = slice(j * 128, (j + 1) * 128)
                xp = xc_ref[:, sl]
                cc = [_colsel(cum, lane, 2 * j), _colsel(cum, lane, 2 * j + 1)]
                cum_l = jnp.where(lo, cc[0], cc[1])
                dt_l = jnp.where(lo, _colsel(dt, lane, 2 * j), _colsel(dt, lane, 2 * j + 1))
                last_l = jnp.where(lo1, _colsel(last_all, lane1, 2 * j), _colsel(last_all, lane1, 2 * j + 1))
                xd = xp * dt_l
                ys = []
                for hh in range(2):
                    seg = cc[hh] - cumT_ref[2 * j + hh:2 * j + hh + 1, :]
                    dm = jnp.where(tril, jnp.exp(jnp.where(tril, seg, 0.0)), 0.0)
                    ys.append(_dot(gmat * dm, xd))
                y_ref[:, sl] = (jnp.where(lo, ys[0], ys[1]) + jnp.exp(cum_l) * yoff[:, jp * 128:(jp + 1) * 128]
                                + dsk_ref[:, sl] * xp)
                st_ref[:, sl] = prev_t[:, sl] * jnp.exp(last_l) + _dot(bg_t, xd * jnp.exp(last_l - cum_l))
        for g in range(2):
            sl = slice(g * 512, (g + 1) * 512)
            zz = z_ref[:, sl]
            yg = y_ref[:, sl] * (zz * _sigmoid(zz))
            rstd = lax.rsqrt(jnp.mean(yg * yg, axis=-1, keepdims=True) + EPS)
            ya_ref[:, sl] = (yg * rstd * nw_ref[:, sl]).astype(ya_ref.dtype)

    blk = lambda c: pl.BlockSpec((L, c), lambda i: (i, 0))
    vec = lambda c: pl.BlockSpec((1, c), lambda i: (0, 0))
    return pl.pallas_call(
        body, name="ssd_fwd", grid=(nc,),
        in_specs=[blk(1536), blk(128), blk(1024), vec(128), vec(128), vec(1024), vec(1024)],
        out_specs=[blk(1024), blk(1024), pl.BlockSpec((1, NSTATE, 1024), lambda i: (i, 0, 0))],
        out_shape=[jax.ShapeDtypeStruct((S, 1024), BF16), jax.ShapeDtypeStruct((S, 1024), F32),
                   jax.ShapeDtypeStruct((nc, NSTATE, 1024), F32)],
        scratch_shapes=[pltpu.VMEM((NSTATE, 1024), F32), pltpu.VMEM((L, 128), F32), pltpu.VMEM((L, 128), F32)],
        compiler_params=_params(("arbitrary",)))(xc, dtr, z, dtb, alog, dskl, nw)


def _ssd_bwd(dya, y, z, xc, dtr, prev, dtb, alog, dskl, nw):
    S = xc.shape[0]
    nc = S // L

    def body(dya_ref, y_ref, z_ref, xc_ref, dtr_ref, prev_ref, dtb_ref, alog_ref, dsk_ref, nw_ref,
             dz_ref, dxc_ref, ddtr_ref, dnw_ref, ddsk_ref, dalog_ref, ddtb_ref,
             dst_ref, cum_ref, cumT_ref, dy_ref, dskacc_ref):
        i = pl.program_id(0)

        @pl.when(i == 0)
        def _():
            dst_ref[...] = jnp.zeros_like(dst_ref)
            dskacc_ref[...] = jnp.zeros_like(dskacc_ref)
            dnw_ref[...] = jnp.zeros_like(dnw_ref)
            dalog_ref[...] = jnp.zeros_like(dalog_ref)
            ddtb_ref[...] = jnp.zeros_like(ddtb_ref)

        lane = _iota((L, 128), 1)
        lane1 = _iota((1, 128), 1)
        lo = lane < 64
        lo1 = lane1 < 64
        r2, c2 = _iota((L, L), 0), _iota((L, L), 1)
        tril = r2 >= c2
        triu = r2 <= c2
        is_last = _iota((L, 1), 0) == L - 1

        for g in range(2):
            sl = slice(g * 512, (g + 1) * 512)
            zz = z_ref[:, sl]
            sg = _sigmoid(zz)
            zg = zz * sg
            yv = y_ref[:, sl]
            yg = yv * zg
            rstd = lax.rsqrt(jnp.mean(yg * yg, axis=-1, keepdims=True) + EPS)
            xh = yg * rstd
            d_out = dya_ref[:, sl]
            dnw_ref[:, sl] += jnp.sum(d_out * xh, axis=0, keepdims=True)
            dyn = d_out * nw_ref[:, sl]
            dyg = rstd * (dyn - xh * jnp.mean(dyn * xh, axis=-1, keepdims=True))
            dy_ref[:, sl] = dyg * zg
            dz_ref[:, sl] = (dyg * yv * (sg * (1.0 + zz * (1.0 - sg)))).astype(dz_ref.dtype)

        dtin = dtr_ref[...] + dtb_ref[...]
        dt = _softplus(dtin)
        a_neg = -jnp.exp(alog_ref[...])
        cum = _cumsum_rows(dt * a_neg)
        cum_ref[...] = cum
        cumT_ref[...] = cum.T
        last_all = cum_ref[L - 1:L, :]
        prev_t = prev_ref[0]
        dn_t = dst_ref[...]
        dcum = jnp.zeros((L, 128), F32)
        ddt = jnp.zeros((L, 128), F32)
        for g in range(2):
            gsl = slice(g * 512, (g + 1) * 512)
            bg = xc_ref[:, 1024 + g * 128:1152 + g * 128]
            cg = xc_ref[:, 1280 + g * 128:1408 + g * 128]
            gmat = _dot(cg, bg, NT)
            gmat_t = _dot(bg, cg, NT)
            pg = prev_t[:, gsl]
            zmat = _dot(cg, pg)
            dgm = jnp.zeros((L, L), F32)
            dgm_t = jnp.zeros((L, L), F32)
            db_acc = jnp.zeros((L, NSTATE), F32)
            dz_parts, cd_parts = [], []
            for jp in range(4):
                j = g * 4 + jp
                sl = slice(j * 128, (j + 1) * 128)
                xp = xc_ref[:, sl]
                dyp = dy_ref[:, sl]
                cc = [_colsel(cum, lane, 2 * j), _colsel(cum, lane, 2 * j + 1)]
                lc = [_colsel(last_all, lane1, 2 * j), _colsel(last_all, lane1, 2 * j + 1)]
                cum_l = jnp.where(lo, cc[0], cc[1])
                dt_l = jnp.where(lo, _colsel(dt, lane, 2 * j), _colsel(dt, lane, 2 * j + 1))
                last_l = jnp.where(lo1, lc[0], lc[1])
                e_l = jnp.exp(cum_l)
                dte_l = jnp.exp(last_l - cum_l)
                cd_l = jnp.exp(last_l)
                cd_parts.append(cd_l)
                xd = xp * dt_l
                dskacc_ref[:, sl] += jnp.sum(dyp * xp, axis=0, keepdims=True)
                dxp = dsk_ref[:, sl] * dyp
                t = dyp * (e_l * zmat[:, jp * 128:(jp + 1) * 128])
                dcc = [jnp.sum(jnp.where(lo, t, 0.0), axis=1, keepdims=True),
                       jnp.sum(jnp.where(lo, 0.0, t), axis=1, keepdims=True)]
                dz_parts.append(e_l * dyp)
                dnp_ = dn_t[:, sl]
                t2 = jnp.sum(dnp_ * prev_t[:, sl], axis=0, keepdims=True)
                dcd = [jnp.sum(jnp.where(lo1, t2, 0.0), axis=1, keepdims=True),
                       jnp.sum(jnp.where(lo1, 0.0, t2), axis=1, keepdims=True)]
                wm = _dot(bg, dnp_)
                dxd = wm * dte_l
                t3 = wm * xd
                ddte = [jnp.sum(jnp.where(lo, t3, 0.0), axis=1, keepdims=True),
                        jnp.sum(jnp.where(lo, 0.0, t3), axis=1, keepdims=True)]
                db_acc = db_acc + _dot(xd * dte_l, dnp_, NT)
                for hh in range(2):
                    h = 2 * j + hh
                    half = lo if hh == 0 else jnp.logical_not(lo)
                    row = cumT_ref[h:h + 1, :]
                    dm = jnp.where(tril, jnp.exp(jnp.where(tril, cc[hh] - row, 0.0)), 0.0)
                    dm_t = jnp.where(triu, jnp.exp(jnp.where(triu, row - cc[hh], 0.0)), 0.0)
                    m = gmat * dm
                    m_t = gmat_t * dm_t
                    dym = jnp.where(half, dyp, 0.0)
                    d_m = _dot(dym, xd, NT)
                    d_mt = _dot(xd, dym, NT)
                    dxd = dxd + _dot(m_t, dym)
                    dcc[hh] = dcc[hh] + jnp.sum(d_m * m, axis=1, keepdims=True) - jnp.sum(d_mt * m_t, axis=1, keepdims=True)
                    dgm = dgm + d_m * dm
                    dgm_t = dgm_t + d_mt * dm_t
                    dte_c = jnp.exp(lc[hh] - cc[hh])
                    dcc[hh] = dcc[hh] - ddte[hh] * dte_c
                    endc = dcd[hh] * jnp.exp(lc[hh]) + jnp.sum(ddte[hh] * dte_c, axis=0, keepdims=True)
                    dcc[hh] = dcc[hh] + jnp.where(is_last, endc, 0.0)
                    dcum = jnp.where(lane == h, dcc[hh], dcum)
                dxc_ref[:, sl] = dxp + dxd * dt_l
                t4 = dxd * xp
                ddt = jnp.where(lane == 2 * j, jnp.sum(jnp.where(lo, t4, 0.0), axis=1, keepdims=True), ddt)
                ddt = jnp.where(lane == 2 * j + 1, jnp.sum(jnp.where(lo, 0.0, t4), axis=1, keepdims=True), ddt)
            dzg = jnp.concatenate(dz_parts, axis=1)
            dst_ref[:, gsl] = dn_t[:, gsl] * jnp.concatenate(cd_parts, axis=1) + _dot(cg.T, dzg)
            dxc_ref[:, 1280 + g * 128:1408 + g * 128] = _dot(dgm, bg) + _dot(dzg, pg, NT)
            dxc_ref[:, 1024 + g * 128:1152 + g * 128] = _dot(dgm_t, cg) + db_acc
        dla = _suffix_sum_rows(dcum)
        ddt = ddt + dla * a_neg
        dalog_ref[...] += jnp.sum(dla * dt, axis=0, keepdims=True) * a_neg
        ddtr = jnp.where(lane < 16, ddt * _sigmoid(dtin), 0.0)
        ddtr_ref[...] = ddtr.astype(ddtr_ref.dtype)
        ddtb_ref[...] += jnp.sum(ddtr, axis=0, keepdims=True)

        @pl.when(i == nc - 1)
        def _():
            seg = (_iota((1024, 128), 0) // 64 == _iota((1024, 128), 1)).astype(F32)
            acc8 = jnp.broadcast_to(dskacc_ref[...], (8, 1024))
            ddsk_ref[...] = lax.dot_general(acc8, seg, NN, precision=lax.Precision.HIGHEST,
                                            preferred_element_type=F32)

    rev = lambda c: pl.BlockSpec((L, c), lambda i: (nc - 1 - i, 0))
    vec = lambda c: pl.BlockSpec((1, c), lambda i: (0, 0))
    return pl.pallas_call(
        body, name="ssd_bwd", grid=(nc,),
        in_specs=[rev(1024), rev(1024), rev(1024), rev(1536), rev(128),
                  pl.BlockSpec((1, NSTATE, 1024), lambda i: (nc - 1 - i, 0, 0)),
                  vec(128), vec(128), vec(1024), vec(1024)],
        out_specs=[rev(1024), rev(1536), rev(128), vec(1024), pl.BlockSpec((8, 128), lambda i: (0, 0)),
                   vec(128), vec(128)],
        out_shape=[jax.ShapeDtypeStruct((S, 1024), BF16), jax.ShapeDtypeStruct((S, 1536), F32),
                   jax.ShapeDtypeStruct((S, 128), BF16), jax.ShapeDtypeStruct((1, 1024), F32),
                   jax.ShapeDtypeStruct((8, 128), F32), jax.ShapeDtypeStruct((1, 128), F32),
                   jax.ShapeDtypeStruct((1, 128), F32)],
        scratch_shapes=[pltpu.VMEM((NSTATE, 1024), F32), pltpu.VMEM((L, 128), F32), pltpu.VMEM((L, 128), F32),
                        pltpu.VMEM((L, 1024), F32), pltpu.VMEM((1, 1024), F32)],
        compiler_params=_params(("arbitrary",)))(dya, y, z, xc, dtr, prev, dtb, alog, dskl, nw)


def _layer_norm_parts(vg):
    mu = jnp.mean(vg, axis=-1, keepdims=True)
    vc = vg - mu
    rstd = lax.rsqrt(jnp.mean(vc * vc, axis=-1, keepdims=True) + EPS)
    return vc * rstd, rstd


def _gmlp_fwd(u, v, lnw, lnb, ws, bse, tb=512):
    S = u.shape[0]
    tb = min(tb, S)

    def body(u_ref, v_ref, lnw_ref, lnb_ref, ws_ref, bse_ref, o_ref, vn_ref):
        tril = _iota((L, L), 0) >= _iota((L, L), 1)
        xh, _ = _layer_norm_parts(_gelu(v_ref[...]))
        vn_ref[...] = xh * lnw_ref[...] + lnb_ref[...]
        for g in range(8):
            w = jnp.where(tril, ws_ref[g], 0.0)
            gs = slice(g * 128, (g + 1) * 128)
            for ch in range(tb // L):
                rs = slice(ch * L, (ch + 1) * L)
                sv = _dot(w, vn_ref[rs, gs]) + bse_ref[g]
                o_ref[rs, gs] = (_gelu(u_ref[rs, gs]) * sv).astype(o_ref.dtype)

    blk = pl.BlockSpec((tb, 1024), lambda i: (i, 0))
    vec = pl.BlockSpec((1, 1024), lambda i: (0, 0))
    cube = pl.BlockSpec((8, L, 128), lambda i: (0, 0, 0))
    return pl.pallas_call(
        body, name="gmlp_fwd", grid=(S // tb,), in_specs=[blk, blk, vec, vec, cube, cube], out_specs=blk,
        out_shape=jax.ShapeDtypeStruct((S, 1024), BF16), scratch_shapes=[pltpu.VMEM((tb, 1024), F32)],
        compiler_params=_params(("parallel",)))(u, v, lnw, lnb, ws, bse)


def _gmlp_bwd(dyb, u, v, lnw, lnb, ws, bse, tb=512):
    S = u.shape[0]
    tb = min(tb, S)

    def body(d_ref, u_ref, v_ref, lnw_ref, lnb_ref, ws_ref, bse_ref,
             du_ref, dv_ref, dws_ref, dbse_ref, dlnw_ref, dlnb_ref, vn_ref, dvn_ref):
        @pl.when(pl.program_id(0) == 0)
        def _():
            dws_ref[...] = jnp.zeros_like(dws_ref)
            dbse_ref[...] = jnp.zeros_like(dbse_ref)
            dlnw_ref[...] = jnp.zeros_like(dlnw_ref)
            dlnb_ref[...] = jnp.zeros_like(dlnb_ref)

        tril = _iota((L, L), 0) >= _iota((L, L), 1)
        vv = v_ref[...]
        xh, rstd = _layer_norm_parts(_gelu(vv))
        vn_ref[...] = xh * lnw_ref[...] + lnb_ref[...]
        for g in range(8):
            w = jnp.where(tril, ws_ref[g], 0.0)
            w_t = w.T
            gs = slice(g * 128, (g + 1) * 128)
            dw = jnp.zeros((L, L), F32)
            dbs = jnp.zeros((L, 128), F32)
            for ch in range(tb // L):
                rs = slice(ch * L, (ch + 1) * L)
                vn = vn_ref[rs, gs]
                sv = _dot(w, vn) + bse_ref[g]
                uu = u_ref[rs, gs]
                dd = d_ref[rs, gs]
                du_ref[rs, gs] = (dd * sv * _gelu_grad(uu)).astype(du_ref.dtype)
                dsv = dd * _gelu(uu)
                dw = dw + _dot(dsv, vn, NT)
                dbs = dbs + dsv
                dvn_ref[rs, gs] = _dot(w_t, dsv)
            dws_ref[g] += jnp.where(tril, dw, 0.0)
            dbse_ref[g] += dbs
        dvn = dvn_ref[...]
        dlnw_ref[...] += jnp.sum(dvn * xh, axis=0, keepdims=True)
        dlnb_ref[...] += jnp.sum(dvn, axis=0, keepdims=True)
        dxh = dvn * lnw_ref[...]
        dvg = rstd * (dxh - jnp.mean(dxh, axis=-1, keepdims=True) - xh * jnp.mean(dxh * xh, axis=-1, keepdims=True))
        dv_ref[...] = (dvg * _gelu_grad(vv)).astype(dv_ref.dtype)

    blk = pl.BlockSpec((tb, 1024), lambda i: (i, 0))
    vec = pl.BlockSpec((1, 1024), lambda i: (0, 0))
    cube = pl.BlockSpec((8, L, 128), lambda i: (0, 0, 0))
    return pl.pallas_call(
        body, name="gmlp_bwd", grid=(S // tb,), in_specs=[blk, blk, blk, vec, vec, cube, cube],
        out_specs=[blk, blk, cube, cube, vec, vec],
        out_shape=[jax.ShapeDtypeStruct((S, 1024), BF16), jax.ShapeDtypeStruct((S, 1024), BF16),
                   jax.ShapeDtypeStruct((8, L, 128), F32), jax.ShapeDtypeStruct((8, L, 128), F32),
                   jax.ShapeDtypeStruct((1, 1024), F32), jax.ShapeDtypeStruct((1, 1024), F32)],
        scratch_shapes=[pltpu.VMEM((tb, 1024), F32), pltpu.VMEM((tb, 1024), F32)],
        compiler_params=_params(("arbitrary",)))(dyb, u, v, lnw, lnb, ws, bse)


def _lane_sum(name, a):
    def body(a_ref, o_ref):
        o_ref[...] = jnp.sum(a_ref[...], axis=1, keepdims=True)
    return pl.pallas_call(body, name=name, out_shape=jax.ShapeDtypeStruct((a.shape[0], 1), F32))(a)


def _bucket_onehot_t():
    qi = np.arange(L)[:, None]
    sj = np.arange(2 * L)[None, :]
    dist = np.maximum(qi + L - sj, 0)
    log_ratio = (np.log(np.maximum(dist, 1).astype(np.float32) / np.float32(16)) / np.float32(math.log(128 / 16)))
    large = 16 + (log_ratio.astype(np.float32) * np.float32(16)).astype(np.int32)
    bucket = np.where(dist < 16, dist, np.minimum(large, 31)).reshape(-1)
    return (np.arange(32)[:, None] == bucket[None, :]).astype(np.float32)


def _rel_bias(table_t, onehot_t):
    def body(t_ref, oh_ref, o_ref):
        o_ref[...] = lax.dot_general(t_ref[...], oh_ref[...], NN, precision=lax.Precision.HIGHEST,
                                     preferred_element_type=F32)
    return pl.pallas_call(body, name="rel_bias", out_shape=jax.ShapeDtypeStruct((16, L * 2 * L), F32),
                          compiler_params=_params())(table_t, onehot_t)


def _rel_bias_bwd(dbias, onehot_t):
    def body(d_ref, oh_ref, o_ref):
        o_ref[...] = lax.dot_general(d_ref[...], oh_ref[...], NT, precision=lax.Precision.HIGHEST,
                                     preferred_element_type=F32)
    return pl.pallas_call(body, name="rel_bias_bwd", out_shape=jax.ShapeDtypeStruct((16, 32), F32),
                          compiler_params=_params())(dbias, onehot_t)


def _band(kp, kc, lo):
    kk = jnp.concatenate([kp, kc], axis=0)
    kr = pltpu.roll(kk, 64, axis=1)
    return [jnp.where(lo, kk, kr), jnp.where(lo, kr, kk)]


def _attn_mask(i):
    qi, sj = _iota((L, 2 * L), 0), _iota((L, 2 * L), 1)
    rel = qi + L - sj
    return (rel >= 0) & (rel < L) & ((sj >= L) | (i > 0))


SMEM = pl.BlockSpec(memory_space=pltpu.SMEM)


def _attn_fwd(qkv, bias, sinks):
    S = qkv.shape[0]
    nb = S // L
    scale = 64 ** -0.5

    def body(sink_ref, q_ref, kc_ref, vc_ref, kp_ref, vp_ref, bias_ref, o_ref, lse_ref):
        i = pl.program_id(0)
        lane = _iota((L, 128), 1)
        lo = lane < 64
        lo2 = _iota((2 * L, 128), 1) < 64
        mask = _attn_mask(i)
        kd = _band(kp_ref[...], kc_ref[...], lo2)
        vd = _band(vp_ref[...], vc_ref[...], lo2)
        lse = jnp.zeros((L, 128), F32)
        for pr in range(8):
            sl = slice(pr * 128, (pr + 1) * 128)
            qp = q_ref[:, sl]
            j = pr // 4
            outs = []
            for hh in range(2):
                h = 2 * pr + hh
                qm = jnp.where(lo if hh == 0 else jnp.logical_not(lo), qp, 0.0)
                lg = jnp.where(mask, _dot(qm, kd[j], NT) * scale + bias_ref[h], NEG_INF)
                s = sink_ref[h]
                m = jnp.maximum(jnp.max(lg, axis=1, keepdims=True), s)
                p = jnp.where(mask, jnp.exp(lg - m), 0.0)
                den = jnp.sum(p, axis=1, keepdims=True) + jnp.exp(s - m)
                outs.append(_dot(p / den, vd[j]))
                lse = jnp.where(lane == h, m + jnp.log(den), lse)
            o_ref[:, sl] = jnp.where(lo, outs[0], outs[1]).astype(o_ref.dtype)
        lse_ref[...] = lse

    prev = lambda col: pl.BlockSpec((L, 128), lambda i: (jnp.maximum(i - 1, 0), col))
    cur = lambda col: pl.BlockSpec((L, 128), lambda i: (i, col))
    return pl.pallas_call(
        body, name="attn_fwd", grid=(nb,),
        in_specs=[SMEM, pl.BlockSpec((L, 1024), lambda i: (i, 0)), cur(8), cur(9), prev(8), prev(9),
                  pl.BlockSpec((16, L, 2 * L), lambda i: (0, 0, 0))],
        out_specs=[pl.BlockSpec((L, 1024), lambda i: (i, 0)), pl.BlockSpec((L, 128), lambda i: (i, 0))],
        out_shape=[jax.ShapeDtypeStruct((S, 1024), BF16), jax.ShapeDtypeStruct((S, 128), F32)],
        compiler_params=_params(("parallel",)))(sinks, qkv, qkv, qkv, qkv, qkv, bias)


def _attn_bwd(qkv, d_o, lse, bias, sinks):
    S = qkv.shape[0]
    nb = S // L
    scale = 64 ** -0.5

    def body(sink_ref, q_ref, kc_ref, vc_ref, kp_ref, vp_ref, do_ref, lse_ref, bias_ref,
             dq_ref, dkv_ref, dbias_ref, dsink_ref, dbq_ref, dbkv_ref, carry_ref):
        i = pl.program_id(0)

        @pl.when(i == 0)
        def _():
            dbias_ref[...] = jnp.zeros_like(dbias_ref)
            dsink_ref[...] = jnp.zeros_like(dsink_ref)
            dbq_ref[...] = jnp.zeros_like(dbq_ref)
            dbkv_ref[...] = jnp.zeros_like(dbkv_ref)
            carry_ref[...] = jnp.zeros_like(carry_ref)

        @pl.when(i < nb)
        def _():
            lane = _iota((L, 128), 1)
            lane1 = _iota((1, 128), 1)
            lo = lane < 64
            lo2 = _iota((2 * L, 128), 1) < 64
            mask = _attn_mask(i)
            kd = _band(kp_ref[...], kc_ref[...], lo2)
            vd = _band(vp_ref[...], vc_ref[...], lo2)
            lse_all = lse_ref[...]
            acc_k = [jnp.zeros((2 * L, 128), F32), jnp.zeros((2 * L, 128), F32)]
            acc_v = [jnp.zeros((2 * L, 128), F32), jnp.zeros((2 * L, 128), F32)]
            dsink = jnp.zeros((1, 128), F32)
            for pr in range(8):
                sl = slice(pr * 128, (pr + 1) * 128)
                qp = q_ref[:, sl]
                dop = do_ref[:, sl]
                j = pr // 4
                dqs = []
                for hh in range(2):
                    h = 2 * pr + hh
                    half = lo if hh == 0 else jnp.logical_not(lo)
                    qm = jnp.where(half, qp, 0.0)
                    dom = jnp.where(half, dop, 0.0)
                    lse_h = _colsel(lse_all, lane, h)
                    lg = _dot(qm, kd[j], NT) * scale + bias_ref[h]
                    p = jnp.where(mask, jnp.exp(jnp.where(mask, lg, NEG_INF) - lse_h), 0.0)
                    dp = _dot(dom, vd[j], NT)
                    delta = jnp.sum(p * dp, axis=1, keepdims=True)
                    ds = p * (dp - delta)
                    dbias_ref[h] += ds
                    ds_sink = jnp.sum(-jnp.exp(sink_ref[h] - lse_h) * delta, axis=0, keepdims=True)
                    dsink = dsink + jnp.where(lane1 == h, ds_sink, 0.0)
                    dss = ds * scale
                    dqs.append(_dot(dss, kd[j]))
                    acc_k[j] = acc_k[j] + _dot(dss, qm, TN)
                    acc_v[j] = acc_v[j] + _dot(p, dom, TN)
                dq = jnp.where(lo, dqs[0], dqs[1])
                dq_ref[:, sl] = dq.astype(dq_ref.dtype)
                dbq_ref[:, sl] += jnp.sum(dq, axis=0, keepdims=True)
            dsink_ref[...] += dsink
            tot_k = [a + pltpu.roll(a, 64, axis=1) for a in acc_k]
            tot_v = [a + pltpu.roll(a, 64, axis=1) for a in acc_v]
            dkv = jnp.concatenate([jnp.where(lo2, tot_k[0], tot_k[1]), jnp.where(lo2, tot_v[0], tot_v[1])], axis=1)
            dbkv_ref[...] += jnp.sum(dkv, axis=0, keepdims=True)
            dkv_ref[...] = (carry_ref[...] + dkv[:L, :]).astype(dkv_ref.dtype)
            carry_ref[...] = dkv[L:, :]

        @pl.when(i == nb)
        def _():
            dkv_ref[...] = carry_ref[...].astype(dkv_ref.dtype)

    c = lambda i: jnp.minimum(i, nb - 1)
    prev = lambda col: pl.BlockSpec((L, 128), lambda i: (jnp.maximum(c(i) - 1, 0), col))
    cur = lambda col: pl.BlockSpec((L, 128), lambda i: (c(i), col))
    row = lambda w: pl.BlockSpec((L, w), lambda i: (c(i), 0))
    cube = pl.BlockSpec((16, L, 2 * L), lambda i: (0, 0, 0))
    vec = lambda w: pl.BlockSpec((1, w), lambda i: (0, 0))
    return pl.pallas_call(
        body, name="attn_bwd", grid=(nb + 1,),
        in_specs=[SMEM, row(1024), cur(8), cur(9), prev(8), prev(9), row(1024), row(128), cube],
        out_specs=[row(1024), pl.BlockSpec((L, 256), lambda i: (jnp.maximum(i - 1, 0), 0)), cube,
                   vec(128), vec(1024), vec(256)],
        out_shape=[jax.ShapeDtypeStruct((S, 1024), BF16), jax.ShapeDtypeStruct((S, 256), BF16),
                   jax.ShapeDtypeStruct((16, L, 2 * L), F32), jax.ShapeDtypeStruct((1, 128), F32),
                   jax.ShapeDtypeStruct((1, 1024), F32), jax.ShapeDtypeStruct((1, 256), F32)],
        scratch_shapes=[pltpu.VMEM((L, 256), F32)],
        compiler_params=_params(("arbitrary",)))(sinks, qkv, qkv, qkv, qkv, qkv, d_o, lse, bias)


def _pad_lanes(a, n=128):
    return jnp.pad(a, ((0, 0), (0, n - a.shape[1])))


def _local_step(x, tgt, mod, w_in, P, io):
    md = [[mod[l:l + 1, k * D:(k + 1) * D] for k in range(6)] for l in range(2)]
    G, g = {}, {}

    sh1, sc1, g1, sh2, sc2, g2 = md[0]
    nmw0, nfw0 = P["norm_mix_w"][0:1], P["norm_ffn_w"][0:1]
    h0 = _norm_mod_fwd("norm_mix_0", x, nmw0, sc1, sh1, after=io["start"])
    segs = {"z": w_in[0:1024], "xbc": w_in[1024:2560], "dt": jnp.pad(w_in[2560:2576], ((0, 112), (0, 0))),
            "u": w_in[2576:3600], "v": w_in[3600:4624]}
    proj = {k: _mm(f"in_proj_{k}", [h0], [w], "nt", [F32])[0] for k, w in segs.items()}
    conv_w, conv_b = P["conv_w"][0], P["conv_b"]
    pre, xc = _conv_fwd(proj["xbc"], conv_w, conv_b)
    dtb, alog = _pad_lanes(P["dt_bias"]), _pad_lanes(P["a_log"])
    dskl = jnp.repeat(P["d_skip"], 64, axis=1)
    ya, y_ssd, prev = _ssd_fwd(xc, proj["dt"], proj["z"], dtb, alog, dskl, P["ssm_norm_w"])
    ws = P["gmlp_ws"][0]
    bse = jnp.broadcast_to(P["gmlp_bs"][0][:, :, None], (8, L, 128))
    yb = _gmlp_fwd(proj["u"], proj["v"], P["gmlp_ln_w"], P["gmlp_ln_b"], ws, bse)
    W = dict(io["weights0"]((ya, yb)))
    w_oa, w_ob = W["out_w"][:1024], W["out_w"][1024:]

    def res(y, x, gate):
        return y, x + gate * y
    mix0, x1 = _mm("out_proj_0", [ya, yb], [w_oa, w_ob], "nn", [F32, F32], epi=res, extras=[x], vecs=[g1])
    h0f = _norm_mod_fwd("norm_ffn_0", x1, nfw0, sc2, sh2)
    a0, b0, f0, y0, x2 = _ffn_fwd("0", h0f, W["gate_wt0"], W["up_wt0"], W["down_w0"], x1, g2)

    sh1b, sc1b, g1b, sh2b, sc2b, g2b = md[1]
    nmw1, nfw1 = P["norm_mix_w"][1:2], P["norm_ffn_w"][1:2]
    W.update(io["weights1"](x2))
    h1 = _norm_mod_fwd("norm_mix_1", x2, nmw1, sc1b, sh1b)
    qkv = _mm("qkv_proj", [h1], [W["qkv_wt"]], "nt", [F32], epi=lambda acc, b: acc + b, vecs=[P["qkv_b"]])[0]
    onehot_t = jnp.asarray(_bucket_onehot_t())
    bias = _rel_bias(P["rel_table"].T, onehot_t).reshape(16, L, 2 * L)
    sinks = P["sinks"].reshape(16)
    att, lse = _attn_fwd(qkv, bias, sinks)

    def res_b(y, x, gate, b):
        y = y + b
        return y, x + gate * y
    mix1, x3 = _mm("o_proj", [att], [W["o_w"]], "nn", [F32, F32], epi=res_b, extras=[x2], vecs=[g1b, P["o_b"]])
    h1f = _norm_mod_fwd("norm_ffn_1", x3, nfw1, sc2b, sh2b)
    a1, b1, f1, y1, x4 = _ffn_fwd("1", h1f, W["gate_wt1"], W["up_wt1"], W["down_w1"], x3, g2b)

    dx, sq, g["final_norm_w"] = _loss_head(x4, tgt, P["final_norm_w"])

    dh, dg2b, dwg1, dwu1, dwd1 = _ffn_bwd("1", dx, h1f, a1, b1, f1, y1, W["gate_wt1"], W["up_wt1"],
                                          W["down_w1"], g2b)
    dx, dsh2b, dsc2b, dnfw1 = _norm_mod_bwd("norm_ffn_bwd_1", x3, dh, dx, nfw1, sc2b)
    dmix, dg1b, g["o_b"] = _gate_bwd("mix_gate_bwd_1", dx, mix1, g1b)
    G["o_w"] = _mm_tn("o_dw", att, dmix)
    d_att = _mm("o_dx", [dmix], [W["o_w"]], "nt", [F32])[0]
    dq, dkv, dbias, dsinks, dbq, dbkv = _attn_bwd(qkv, d_att, lse, bias, sinks)
    g["rel_table"] = _rel_bias_bwd(dbias.reshape(16, L * 2 * L), onehot_t).T
    g["sinks"] = dsinks[:, :16]
    g["qkv_b"] = jnp.concatenate([dbq, dbkv], axis=1)
    w_q, w_kv = W["qkv_wt"][:1024], W["qkv_wt"][1024:]
    G["qkv_wt"] = jnp.concatenate([_mm_tn("qkv_dwq", dq, h1), _mm_tn("qkv_dwkv", dkv, h1)], axis=0)
    dh = _mm("qkv_dx", [dq, dkv], [w_q, w_kv], "nn", [F32])[0]
    dx, dsh1b, dsc1b, dnmw1 = _norm_mod_bwd("norm_mix_bwd_1", x2, dh, dx, nmw1, sc1b)
    behind = io["grads1"]({"qkv_wt": G.pop("qkv_wt"), "o_w": G.pop("o_w"), "gate_wt1": dwg1, "up_wt1": dwu1,
                           "down_w1": dwd1})

    dh, dg2, dwg0, dwu0, dwd0 = _ffn_bwd("0", dx, h0f, a0, b0, f0, y0, W["gate_wt0"], W["up_wt0"],
                                         W["down_w0"], g2, after=behind)
    behind = io["grads_ffn0"]({"gate_wt0": dwg0, "up_wt0": dwu0, "down_w0": dwd0})
    dx, dsh2, dsc2, dnfw0 = _norm_mod_bwd("norm_ffn_bwd_0", x1, dh, dx, nfw0, sc2, after=behind)
    dmix, dg1, _ = _gate_bwd("mix_gate_bwd_0", dx, mix0, g1)
    G["out_w"] = jnp.concatenate([_mm_tn("out_dwa", ya, dmix), _mm_tn("out_dwb", yb, dmix)], axis=0)
    dya = _mm("out_dxa", [dmix], [w_oa], "nt", [F32])[0]
    dyb = _mm("out_dxb", [dmix], [w_ob], "nt", [F32])[0]
    du, dv, dws, dbse, g["gmlp_ln_w"], g["gmlp_ln_b"] = _gmlp_bwd(dyb, proj["u"], proj["v"], P["gmlp_ln_w"],
                                                                 P["gmlp_ln_b"], ws, bse)
    g["gmlp_ws"] = dws[None]
    g["gmlp_bs"] = _lane_sum("gmlp_dbs", dbse.reshape(8 * L, 128)).reshape(1, 8, L)
    dz, dxc, ddt, g["ssm_norm_w"], ddsk, dalog, ddtb = _ssd_bwd(dya, y_ssd, proj["z"], xc, proj["dt"], prev,
                                                                dtb, alog, dskl, P["ssm_norm_w"])
    g["d_skip"], g["a_log"], g["dt_bias"] = ddsk[0:1, :16], dalog[:, :16], ddtb[:, :16]
    dxr, dconv_w, g["conv_b"] = _conv_bwd(dxc, pre, proj["xbc"], conv_w)
    g["conv_w"] = dconv_w[None]
    dsegs = {"z": dz, "xbc": dxr, "dt": ddt, "u": du, "v": dv}
    dws_in = {k: _mm_tn(f"in_dw_{k}", d, h0) for k, d in dsegs.items()}
    G["in_wt"] = jnp.concatenate([dws_in["z"], dws_in["xbc"], dws_in["dt"][:16], dws_in["u"], dws_in["v"]], axis=0)
    keys = ["z", "xbc", "dt", "u", "v"]
    dh = _mm("in_dx", [dsegs[k] for k in keys], [segs[k] for k in keys], "nn", [F32])[0]
    dx, dsh1, dsc1, dnmw0 = _norm_mod_bwd("norm_mix_bwd_0", x, dh, dx, nmw0, sc1)

    g["norm_mix_w"] = jnp.concatenate([dnmw0, dnmw1], axis=0)
    g["norm_ffn_w"] = jnp.concatenate([dnfw0, dnfw1], axis=0)
    dmod = jnp.concatenate([jnp.concatenate([dsh1, dsc1, dg1, dsh2, dsc2, dg2], axis=1),
                            jnp.concatenate([dsh1b, dsc1b, dg1b, dsh2b, dsc2b, dg2b], axis=1)], axis=0)
    return sq, dx, dmod, G, g


def _ada_fwd(c_all, ada_w, ada_b):
    n = ada_w.shape[2]
    tn = _col_tile(n, 512)

    def body(c_ref, w_ref, b_ref, o_ref):
        cc = c_ref[...]
        o_ref[...] = lax.dot_general(cc * _sigmoid(cc), w_ref[...], NN, precision=lax.Precision.HIGHEST,
                                     preferred_element_type=F32) + b_ref[...]

    return pl.pallas_call(
        body, name="ada_fwd", grid=(2, n // tn),
        in_specs=[pl.BlockSpec((8, D), lambda l, j: (0, 0)), pl.BlockSpec((None, D, tn), lambda l, j: (l, 0, j)),
                  pl.BlockSpec((None, 1, tn), lambda l, j: (l, 0, j))],
        out_specs=pl.BlockSpec((None, 8, tn), lambda l, j: (l, 0, j)),
        out_shape=jax.ShapeDtypeStruct((2, 8, n), F32), compiler_params=_params(("parallel", "parallel")))(
            c_all, ada_w, ada_b)


def _ada_bwd(c_all, dmod_cols, dmod_all):
    n = dmod_cols.shape[2]
    tn = _col_tile(n, 512)

    def body(c_ref, d_ref, o_ref):
        cc = c_ref[...]
        o_ref[...] = lax.dot_general(cc * _sigmoid(cc), d_ref[...], TN, precision=lax.Precision.HIGHEST,
                                     preferred_element_type=F32)

    dw = pl.pallas_call(
        body, name="ada_dw", grid=(2, n // tn),
        in_specs=[pl.BlockSpec((8, D), lambda l, j: (0, 0)), pl.BlockSpec((None, 8, tn), lambda l, j: (l, 0, j))],
        out_specs=pl.BlockSpec((None, D, tn), lambda l, j: (l, 0, j)),
        out_shape=jax.ShapeDtypeStruct((2, D, n), F32), compiler_params=_params(("parallel", "parallel")))(
            c_all, dmod_cols)

    def sum_body(d_ref, o_ref):
        o_ref[...] = jnp.sum(d_ref[...], axis=0, keepdims=True)

    db = pl.pallas_call(
        sum_body, name="ada_db", grid=(2,),
        in_specs=[pl.BlockSpec((None, 8, 6 * D), lambda l: (l, 0, 0))],
        out_specs=pl.BlockSpec((None, 1, 6 * D), lambda l: (l, 0, 0)),
        out_shape=jax.ShapeDtypeStruct((2, 1, 6 * D), F32), compiler_params=_params(("parallel",)))(dmod_all)
    return dw, db


def _row_tile(rows, cap=512, mult=8):
    best = rows
    for t in range(mult, min(rows, cap) + 1, mult):
        if rows % t == 0:
            best = t
    return best


def _adamw(name, w, g, m, v):
    def fn(w, g, m, v):
        m = ADAM_B1 * m + (1.0 - ADAM_B1) * g
        v = ADAM_B2 * v + (1.0 - ADAM_B2) * (g * g)
        m_hat = m / (1.0 - ADAM_B1 ** ADAM_STEP)
        v_hat = v / (1.0 - ADAM_B2 ** ADAM_STEP)
        return -ADAM_LR * (m_hat / (jnp.sqrt(v_hat) + ADAM_EPS) + ADAM_WD * w), m, v
    cols = w.shape[1]
    return _rowwise(name, fn, [w, g, m, v], [], [(cols, F32)] * 3, tr=_row_tile(w.shape[0]))


def _place():
    return lax.axis_index("x"), lax.axis_index("y"), lax.axis_index("c")


VMEM_SPEC = pl.BlockSpec(memory_space=pltpu.VMEM)


def _allreduce_small(name, buf, after=None):
    rows = buf.shape[0]
    deps = [] if after is None else [after]

    def body(x_ref, *rest):
        o_ref, stage, send_sems, recv_sems = rest[len(deps):]
        x, y, c = _place()
        me = 4 * x + 2 * y + c
        stage[me] = x_ref[...]
        copies = []
        for k in range(1, 8):
            peer = (1 - x if k & 4 else x, 1 - y if k & 2 else y, 1 - c if k & 1 else c)
            cp = pltpu.make_async_remote_copy(src_ref=x_ref, dst_ref=stage.at[me], send_sem=send_sems.at[k - 1],
                                              recv_sem=recv_sems.at[k - 1], device_id=peer, device_id_type=MESH)
            cp.start()
            copies.append(cp)
        for cp in copies:
            cp.wait()
        acc = stage[0]
        for d in range(1, 8):
            acc = acc + stage[d]
        o_ref[...] = acc

    return pl.pallas_call(
        body, name=name, in_specs=[VMEM_SPEC] + [ANY for _ in deps], out_specs=VMEM_SPEC,
        out_shape=jax.ShapeDtypeStruct((rows, 128), F32),
        scratch_shapes=[pltpu.VMEM((8, rows, 128), F32), pltpu.SemaphoreType.DMA((7,)), pltpu.SemaphoreType.DMA((7,))],
        compiler_params=pltpu.CompilerParams(vmem_limit_bytes=_VMEM_LIMIT))(buf, *deps)


OTHER_CHIPS = ((1, 0), (0, 1), (1, 1))


def _allgather_big(wp, after=None):
    rows = wp.shape[0]
    half = rows // 2
    deps = [] if after is None else [after]

    def body(w_ref, *rest):
        o_ref, send_sems, recv_sems = rest[len(deps):]
        x, y, c = _place()
        k = 2 * x + y
        mine = pl.ds(pl.multiple_of(c * half, 8), half)
        first = []
        for j, (fx, fy) in enumerate(OTHER_CHIPS):
            cp = pltpu.make_async_remote_copy(src_ref=w_ref.at[mine], dst_ref=o_ref.at[k, mine],
                                              send_sem=send_sems.at[j], recv_sem=recv_sems.at[j],
                                              device_id=(1 - x if fx else x, 1 - y if fy else y, c),
                                              device_id_type=MESH)
            cp.start()
            first.append(cp)
        for cp in first:
            cp.wait_recv()
        swap = pltpu.make_async_remote_copy(src_ref=o_ref.at[:, mine], dst_ref=o_ref.at[:, mine],
                                            send_sem=send_sems.at[3], recv_sem=recv_sems.at[3],
                                            device_id=(x, y, 1 - c), device_id_type=MESH)
        swap.start()
        swap.wait()
        for cp in first:
            cp.wait_send()

    gathered = pl.pallas_call(
        body, name="allgather_weights", in_specs=[ANY] + [ANY for _ in deps], out_specs=ANY,
        out_shape=jax.ShapeDtypeStruct((4, rows, 1024), wp.dtype),
        scratch_shapes=[pltpu.SemaphoreType.DMA((4,)), pltpu.SemaphoreType.DMA((4,))])(wp, *deps)
    return _with_own_share(gathered, wp)


SIBLING_COLLECTIVE_ID = 6


def _sibling_handshake():
    x, y, c = _place()
    barrier = pltpu.get_barrier_semaphore()
    pl.semaphore_signal(barrier, inc=1, device_id=(x, y, 1 - c), device_id_type=MESH)
    pl.semaphore_wait(barrier, 1)


def _sibling_swap(name, src, halves):
    half = src.shape[-2] // 2
    out_shape = (src.shape[0], half, 1024) if halves else src.shape

    def body(s_ref, o_ref, send_sem, recv_sem):
        x, y, c = _place()
        _sibling_handshake()
        part = s_ref.at[:, pl.ds(pl.multiple_of((1 - c) * half, 8), half)] if halves else s_ref
        cp = pltpu.make_async_remote_copy(src_ref=part, dst_ref=o_ref, send_sem=send_sem, recv_sem=recv_sem,
                                          device_id=(x, y, 1 - c), device_id_type=MESH)
        cp.start()
        cp.wait()

    return pl.pallas_call(
        body, name=name, in_specs=[ANY], out_specs=ANY, out_shape=jax.ShapeDtypeStruct(out_shape, src.dtype),
        scratch_shapes=[pltpu.SemaphoreType.DMA, pltpu.SemaphoreType.DMA],
        compiler_params=pltpu.CompilerParams(collective_id=SIBLING_COLLECTIVE_ID))(src)


HBM = pl.BlockSpec(memory_space=pltpu.HBM)
SEM = pl.BlockSpec(memory_space=pltpu.SEMAPHORE)


def _chip_copies(mode, src_ref, land_ref, send_sems, recv_sems):
    x, y, c = _place()
    k = 2 * x + y
    copies = []
    for j, (fx, fy) in enumerate(OTHER_CHIPS):
        px, py = (1 - x if fx else x), (1 - y if fy else y)
        if mode == "gather":
            half = src_ref.shape[0] // 2
            mine = pl.ds(pl.multiple_of(c * half, 16), half)
            src, dst = src_ref.at[mine], land_ref.at[k, mine]
        else:
            src, dst = src_ref.at[2 * px + py], land_ref.at[k]
        copies.append(pltpu.make_async_remote_copy(src_ref=src, dst_ref=dst, send_sem=send_sems.at[j],
                                                   recv_sem=recv_sems.at[j], device_id=(px, py, c),
                                                   device_id_type=MESH))
    return copies


def _exchange_start(name, collective_id, mode, src, land, after=None):
    deps = [] if after is None else [after]

    def body(s_ref, l_ref, *rest):
        send_sems, recv_sems, s_thru, l_thru, token = rest[len(deps):]
        x, y, c = _place()
        barrier = pltpu.get_barrier_semaphore()
        for fx, fy in OTHER_CHIPS:
            pl.semaphore_signal(barrier, inc=1, device_id=(1 - x if fx else x, 1 - y if fy else y, c),
                                device_id_type=MESH)
        pl.semaphore_wait(barrier, 3)
        for cp in _chip_copies(mode, s_ref, l_ref, send_sems, recv_sems):
            cp.start()
        token[...] = jnp.zeros_like(token)

    return pl.pallas_call(
        body, name=name,
        out_shape=(pltpu.SemaphoreType.DMA((3,)), pltpu.SemaphoreType.DMA((3,)), pltpu.HBM(src.shape, src.dtype),
                   pltpu.HBM(land.shape, land.dtype), jax.ShapeDtypeStruct((8, 128), F32)),
        in_specs=(HBM, HBM) + tuple(ANY for _ in deps), out_specs=(SEM, SEM, HBM, HBM, VMEM_SPEC),
        input_output_aliases={0: 2, 1: 3},
        compiler_params=pltpu.CompilerParams(has_side_effects=pltpu.SideEffectType.DATAFLOW_SIDE_EFFECTING,
                                             collective_id=collective_id))(
            pltpu.with_memory_space_constraint(src, pltpu.HBM), pltpu.with_memory_space_constraint(land, pltpu.HBM),
            *deps)


def _exchange_wait(name, mode, started, after):
    send_sems, recv_sems, s_thru, l_thru, _ = started
    deps = list(after) if isinstance(after, (tuple, list)) else [after]

    def body(s_ref, l_ref, send_sems, recv_sems, *rest):
        for cp in _chip_copies(mode, s_ref, l_ref, send_sems, recv_sems):
            cp.wait_send()
            cp.wait_recv()

    return pl.pallas_call(
        body, name=name, out_shape=(pltpu.HBM(s_thru.shape, s_thru.dtype), pltpu.HBM(l_thru.shape, l_thru.dtype)),
        in_specs=(HBM, HBM, SEM, SEM) + tuple(ANY for _ in deps), out_specs=(HBM, HBM),
        input_output_aliases={0: 0, 1: 1},
        compiler_params=pltpu.CompilerParams(has_side_effects=pltpu.SideEffectType.DATAFLOW_SIDE_EFFECTING))(
            s_thru, l_thru, send_sems, recv_sems, *deps)


def _with_own_share(gathered, share):
    slot = lax.broadcasted_iota(jnp.int32, (4, 1, 1), 0)
    return jnp.where(slot == 2 * lax.axis_index("x") + lax.axis_index("y"), share[None], gathered)


def _allgather_finish(tag, share, land):
    rows = share.shape[0]
    half = rows // 2

    def body(l_ref, o_ref, send_sem, recv_sem):
        x, y, c = _place()
        _sibling_handshake()
        mine = pl.ds(pl.multiple_of(c * half, 16), half)
        swap = pltpu.make_async_remote_copy(src_ref=o_ref.at[:, mine], dst_ref=o_ref.at[:, mine], send_sem=send_sem,
                                            recv_sem=recv_sem, device_id=(x, y, 1 - c), device_id_type=MESH)
        swap.start()
        swap.wait()

    swapped = pl.pallas_call(
        body, name="allgather_finish_" + tag, in_specs=[ANY], out_specs=ANY, input_output_aliases={0: 0},
        out_shape=jax.ShapeDtypeStruct(land.shape, land.dtype),
        scratch_shapes=[pltpu.SemaphoreType.DMA, pltpu.SemaphoreType.DMA],
        compiler_params=pltpu.CompilerParams(collective_id=SIBLING_COLLECTIVE_ID))(land)
    return _with_own_share(swapped, share)


def _pair_sum(tag, g, r1, c):
    rows = g.shape[1]
    half = rows // 2
    th = _row_tile(half, 256, 16)
    nblk = half // th

    def body(c_ref, g_ref, r_ref, o_ref, o2_ref):
        o_ref[...] = (g_ref[...] + r_ref[...]).astype(o_ref.dtype)
        o2_ref[...] = o_ref[...]

    spec = pl.BlockSpec((None, th, 1024), lambda k, i, c_ref: (k, i, 0))
    grid_spec = pltpu.PrefetchScalarGridSpec(
        num_scalar_prefetch=1, grid=(4, nblk),
        in_specs=[pl.BlockSpec((None, th, 1024), lambda k, i, c_ref: (k, c_ref[0] * nblk + i, 0)), spec],
        out_specs=[spec, spec])
    return pl.pallas_call(body, name="grad_pair_sum_" + tag, grid_spec=grid_spec,
                          out_shape=[jax.ShapeDtypeStruct((4, half, 1024), BF16)] * 2,
                          compiler_params=_params(("parallel", "parallel")))(c, g, r1)


def _chip_sum(tag, q, after=None):
    half = q.shape[1]
    th = _row_tile(half, 256, 16)
    deps = [] if after is None else [after]

    def body(a, b, c, d, *rest):
        rest[-1][...] = ((a[...].astype(F32) + b[...].astype(F32)) + c[...].astype(F32)) + d[...].astype(F32)

    specs = [pl.BlockSpec((None, th, 1024), functools.partial(lambda i, k: (k, i, 0), k=k)) for k in range(4)]
    return pl.pallas_call(body, name="grad_chip_sum_" + tag, grid=(half // th,), in_specs=specs + [ANY for _ in deps],
                          out_specs=pl.BlockSpec((th, 1024), lambda i: (i, 0)),
                          out_shape=jax.ShapeDtypeStruct((half, 1024), F32),
                          compiler_params=_params(("parallel",)))(q, q, q, q, *deps)


def _join_halves(tag, f, r, c):
    half = f.shape[0]
    th = _row_tile(half, 256)
    nblk = half // th

    def body(c_ref, f_ref, r_ref, o_ref):
        mine = (pl.program_id(0) == c_ref[0])
        o_ref[...] = jnp.where(mine, f_ref[...], r_ref[...])

    spec = pl.BlockSpec((th, 1024), lambda h, i, c_ref: (i, 0))
    grid_spec = pltpu.PrefetchScalarGridSpec(
        num_scalar_prefetch=1, grid=(2, nblk), in_specs=[spec, spec],
        out_specs=pl.BlockSpec((th, 1024), lambda h, i, c_ref: (h * nblk + i, 0)))
    return pl.pallas_call(body, name="grad_join_halves_" + tag, grid_spec=grid_spec,
                          out_shape=jax.ShapeDtypeStruct((2 * half, 1024), F32),
                          compiler_params=_params(("parallel", "parallel")))(c, f, r)


BIG_ARGS = ("in_w_even", "out_w_even", "qkv_w", "o_w", "ffn_gate_w", "ffn_up_w", "ffn_down_w")
def _ffn_pieces(layer):
    return tuple((f"{n}{layer}", 704, 704) for n in ("gate_wt", "up_wt", "down_w"))


IN_SLAB = (("in_wt", 1156, 1184),)
LAYER0_REST_SLAB = (("out_w", 512, 512),) + _ffn_pieces(0)
LAYER1_SLAB = (("qkv_wt", 320, 320), ("o_w", 256, 256)) + _ffn_pieces(1)
FFN0_SLAB = _ffn_pieces(0)
MIXER0_SLAB = (("in_wt", 1156, 1280), ("out_w", 512, 512))


def _slab(pieces, spec):
    parts = []
    for name, rows, room in spec:
        p = pieces[name]
        parts.append(jnp.pad(p, [(0, 0)] * (p.ndim - 2) + [(0, room - rows), (0, 0)]) if room > rows else p)
    return jnp.concatenate(parts, axis=-2) if len(parts) > 1 else parts[0]


def _unslab(slab, spec):
    out, off = {}, 0
    for name, rows, room in spec:
        out[name] = slab[..., off:off + rows, :]
        off += room
    return out


def _share_pieces(w):
    return {"in_wt": w["in_w_even"][0].T, "out_w": w["out_w_even"][0], "qkv_wt": w["qkv_w"][0].T, "o_w": w["o_w"][0],
            "gate_wt0": w["ffn_gate_w"][0].T, "gate_wt1": w["ffn_gate_w"][1].T,
            "up_wt0": w["ffn_up_w"][0].T, "up_wt1": w["ffn_up_w"][1].T,
            "down_w0": w["ffn_down_w"][0], "down_w1": w["ffn_down_w"][1]}


def _pieces_to_shares(p):
    return {"in_w_even": p["in_wt"].T[None], "out_w_even": p["out_w"][None], "qkv_w": p["qkv_wt"].T[None],
            "o_w": p["o_w"][None], "ffn_gate_w": jnp.stack([p["gate_wt0"].T, p["gate_wt1"].T]),
            "ffn_up_w": jnp.stack([p["up_wt0"].T, p["up_wt1"].T]),
            "ffn_down_w": jnp.stack([p["down_w0"], p["down_w1"]])}


def _whole_from_chips(p):
    return {k: v.reshape(-1, D) for k, v in p.items()}


def _chips_from_full(G, spec):
    return _slab({k: v.reshape(4, -1, D) for k, v in G.items()}, spec)


def _pack_small(parts):
    padded = []
    for p in parts:
        p = p.reshape(-1).astype(F32)
        padded.append(jnp.pad(p, (0, (-p.shape[0]) % 1024)))
    return jnp.concatenate(padded).reshape(-1, 128)


def _unpack_small(slab, shapes):
    flat, out, off = slab.reshape(-1), [], 0
    for shp in shapes:
        size = math.prod(shp)
        out.append(flat[off:off + size].reshape(shp))
        off += size + (-size) % 1024
    return out


SMALL = ("ada_b", "norm_mix_w", "norm_ffn_w", "conv_w", "conv_b", "dt_bias", "a_log", "d_skip", "ssm_norm_w",
         "gmlp_ln_w", "gmlp_ln_b", "gmlp_ws", "gmlp_bs", "qkv_b", "o_b", "sinks", "rel_table", "final_norm_w")
SMALL_SPLIT = {"conv_w": 1536, "qkv_b": 1280, "o_b": 1024}
WEIGHTS = ("ada_w", "ada_b", "norm_mix_w", "norm_ffn_w", "in_w_even", "conv_w", "conv_b", "dt_bias", "a_log", "d_skip",
           "ssm_norm_w", "gmlp_ln_w", "gmlp_ln_b", "gmlp_ws", "gmlp_bs", "out_w_even", "qkv_w", "qkv_b", "o_w", "o_b",
           "sinks", "rel_table", "ffn_gate_w", "ffn_up_w", "ffn_down_w", "final_norm_w")


def kernel(x, c, ada_w, ada_b, norm_mix_w, norm_ffn_w, in_w_even, conv_w, conv_b, dt_bias, a_log, d_skip, ssm_norm_w, gmlp_ln_w, gmlp_ln_b, gmlp_ws, gmlp_bs, out_w_even, qkv_w, qkv_b, o_w, o_b, sinks, rel_table, ffn_gate_w, ffn_up_w, ffn_down_w, final_norm_w, loss_target, m_ada_w, m_ada_b, m_norm_mix_w, m_norm_ffn_w, m_in_w_even, m_conv_w, m_conv_b, m_dt_bias, m_a_log, m_d_skip, m_ssm_norm_w, m_gmlp_ln_w, m_gmlp_ln_b, m_gmlp_ws, m_gmlp_bs, m_out_w_even, m_qkv_w, m_qkv_b, m_o_w, m_o_b, m_sinks, m_rel_table, m_ffn_gate_w, m_ffn_up_w, m_ffn_down_w, m_final_norm_w, v_ada_w, v_ada_b, v_norm_mix_w, v_norm_ffn_w, v_in_w_even, v_conv_w, v_conv_b, v_dt_bias, v_a_log, v_d_skip, v_ssm_norm_w, v_gmlp_ln_w, v_gmlp_ln_b, v_gmlp_ws, v_gmlp_bs, v_out_w_even, v_qkv_w, v_qkv_b, v_o_w, v_o_b, v_sinks, v_rel_table, v_ffn_gate_w, v_ffn_up_w, v_ffn_down_w, v_final_norm_w):
    args = dict(locals())
    w = {n: args[n] for n in WEIGHTS}
    m = {n: args["m_" + n] for n in WEIGHTS}
    v = {n: args["v_" + n] for n in WEIGHTS}
    ax, ay, ac = _place()
    me = 4 * ax + 2 * ay + ac
    chip = 2 * ax + ay
    south = (ac == 0).astype(F32)
    c_arr = jnp.reshape(ac, (1,)).astype(jnp.int32)

    c_all = _allreduce_small("gather_cond", lax.dynamic_update_slice(jnp.zeros((8, D), F32), c, (me, 0)).reshape(64, 128))
    c_all = c_all.reshape(8, D)
    n_ada = ada_w.shape[2]
    mod_cols = _ada_fwd(c_all, ada_w, lax.dynamic_slice(ada_b, (0, chip * n_ada), (2, n_ada)).reshape(2, 1, n_ada))
    pieces = [lax.dynamic_update_slice(jnp.zeros((2, 8, 6 * D), F32), mod_cols, (0, 0, chip * n_ada))]
    split_names = list(SMALL_SPLIT)
    for n in split_names:
        full = SMALL_SPLIT[n]
        local = w[n]
        idx = (0,) * (local.ndim - 1) + (chip * local.shape[-1],)
        pieces.append(lax.dynamic_update_slice(jnp.zeros(local.shape[:-1] + (full,), F32), local, idx))
    shapes = [p.shape for p in pieces]
    mod_slab = _allreduce_small("gather_mod", _pack_small(pieces) * south)
    gathered = _unpack_small(mod_slab, shapes)
    mod = lax.dynamic_slice(gathered[0], (0, me, 0), (2, 1, 6 * D)).reshape(2, 6 * D)
    P = {n: w[n] for n in SMALL if n not in SMALL_SPLIT and n != "ada_b"}
    for n, full in zip(split_names, gathered[1:]):
        P[n] = full
    P["final_norm_w"] = final_norm_w.reshape(1, D)

    cast = {k: p.astype(_MXU) for k, p in _share_pieces(w).items()}
    in_slab = _allgather_big(_slab(cast, IN_SLAB), after=mod_slab)
    w_in = _unslab(in_slab, IN_SLAB)["in_wt"].reshape(4 * 1156, D)

    def start_gather(tag, collective_id, spec, after):
        share = _slab(cast, spec)
        return _exchange_start("allgather_start_" + tag, collective_id, "gather", share,
                               lax.empty((4,) + share.shape, share.dtype), after=after)

    def finish_gather(tag, started, spec, after):
        share, land = _exchange_wait("allgather_wait_" + tag, "gather", started, after)
        return _whole_from_chips(_unslab(_allgather_finish(tag, share, land), spec))

    gather0 = start_gather("0", 1, LAYER0_REST_SLAB, in_slab)
    gather1 = start_gather("1", 2, LAYER1_SLAB, gather0[4])

    def start_reduce(tag, collective_id, G, spec, after=None):
        gp = _chips_from_full(G, spec)
        p, q = _pair_sum(tag, gp, _sibling_swap("grad_pair_exchange_" + tag, gp, True), c_arr)
        return _exchange_start("grad_exchange_start_" + tag, collective_id, "scatter", p, q, after=after)

    def finish_reduce(tag, started, spec, after, behind=None):
        q = _exchange_wait("grad_exchange_wait_" + tag, "scatter", started, after)[1]
        fin = _chip_sum(tag, q, after=behind)
        total = _join_halves(tag, fin, _sibling_swap("grad_final_exchange_" + tag, fin, False), c_arr)
        return _unslab(total, spec)

    reduces = {}

    def grads1(G1):
        reduces["1"] = start_reduce("1", 3, G1, LAYER1_SLAB)
        return reduces["1"][4]

    def grads_ffn0(G):
        reduces["f"] = start_reduce("f", 4, G, FFN0_SLAB)
        return reduces["f"][4]

    io = {"start": gather1[4],
          "weights0": lambda after: finish_gather("0", gather0, LAYER0_REST_SLAB, after),
          "weights1": lambda after: finish_gather("1", gather1, LAYER1_SLAB, after),
          "grads1": grads1, "grads_ffn0": grads_ffn0}
    sq, grad_x, dmod, G0, g = _local_step(x[0], loss_target[0], mod, w_in, P, io)
    loss = lax.psum(0.5 * sq[0, 0] / D, ("x", "y", "c"))

    g["final_norm_w"] = g["final_norm_w"].reshape(D)
    small_names = [n for n in SMALL if n != "ada_b"]
    pieces = [lax.dynamic_update_slice(jnp.zeros((2, 8, 6 * D), F32), dmod.reshape(2, 1, 6 * D), (0, me, 0))]
    pieces += [g[n] for n in small_names]
    shapes = [p.shape for p in pieces]
    small_slab = _allreduce_small("allreduce_small_grads", _pack_small(pieces))
    reduces["m"] = start_reduce("m", 5, G0, MIXER0_SLAB, after=small_slab)
    shares = finish_reduce("1", reduces["1"], LAYER1_SLAB, grad_x, behind=reduces["m"][4])
    shares.update(finish_reduce("f", reduces["f"], FFN0_SLAB, grad_x, behind=reduces["m"][4]))
    reduced = _unpack_small(small_slab, shapes)
    dmod_all = reduced[0]
    grads = dict(zip(small_names, reduced[1:]))
    for n in split_names:
        full = grads[n]
        size = w[n].shape[-1]
        grads[n] = lax.dynamic_slice(full, (0,) * (full.ndim - 1) + (chip * size,), full.shape[:-1] + (size,))
    grads = {n: grads[n].reshape(w[n].shape) for n in small_names}
    dw_ada, db_ada = _ada_bwd(c_all, lax.dynamic_slice(dmod_all, (0, 0, chip * n_ada), (2, 8, n_ada)), dmod_all)
    grads["ada_w"], grads["ada_b"] = dw_ada, db_ada.reshape(2, 6 * D)

    delta, new_m, new_v = {}, {}, {}

    def update(n):
        cols = w[n].shape[-1]
        d_, m_, v_ = _adamw("adamw_" + n, w[n].reshape(-1, cols), grads[n].reshape(-1, cols), m[n].reshape(-1, cols),
                            v[n].reshape(-1, cols))
        delta[n], new_m[n], new_v[n] = d_.reshape(w[n].shape), m_.reshape(w[n].shape), v_.reshape(w[n].shape)

    update("ada_w")
    shapes = [w[n].shape for n in SMALL]
    packed = [_pack_small([t[n] for n in SMALL]) for t in (w, grads, m, v)]
    outs = _adamw("adamw_small", *packed)
    for dst, slab in zip((delta, new_m, new_v), outs):
        for n, t in zip(SMALL, _unpack_small(slab, shapes)):
            dst[n] = t
    shares.update(finish_reduce("m", reduces["m"], MIXER0_SLAB, outs[0]))
    grads.update(_pieces_to_shares(shares))
    for n in BIG_ARGS:
        update(n)
    return (loss, grad_x[None], *[grads[n] for n in WEIGHTS], *[delta[n] for n in WEIGHTS],
            *[new_m[n] for n in WEIGHTS], *[new_v[n] for n in WEIGHTS])
```

```python
import functools
import math

import numpy as np
import jax
import jax.numpy as jnp
from jax import lax
from jax.experimental import pallas as pl
from jax.experimental.pallas import tpu as pltpu

F32 = jnp.float32
BF16 = jnp.bfloat16
_MXU = jnp.bfloat16
_VMEM_LIMIT = 56 * 1024 * 1024
MXU_COLS = 256
D = 1024
L = 128
NSTATE = 128
EPS = 1e-6
NEG_INF = -1e30
FFN = 2816
ADAM_LR, ADAM_B1, ADAM_B2, ADAM_EPS, ADAM_WD, ADAM_STEP = 0.001, 0.9, 0.999, 1e-08, 0.01, 10
MESH = pl.DeviceIdType.MESH
ANY = pl.BlockSpec(memory_space=pl.ANY)

NN = (((1,), (0,)), ((), ()))
NT = (((1,), (1,)), ((), ()))
TN = (((0,), (0,)), ((), ()))


def _dot(a, b, dn=NN):
    return lax.dot_general(a.astype(_MXU), b.astype(_MXU), dn, preferred_element_type=F32)


def _params(sem=None):
    return pltpu.CompilerParams(dimension_semantics=sem, vmem_limit_bytes=_VMEM_LIMIT)


def _sigmoid(x):
    return 1.0 / (1.0 + jnp.exp(-x))


def _softplus(x):
    return jnp.maximum(x, 0.0) + jnp.log(1.0 + jnp.exp(-jnp.abs(x)))


def _gelu(x):
    return 0.5 * x * (1.0 + lax.erf(x * (2.0 ** -0.5)))


def _gelu_grad(x):
    return 0.5 * (1.0 + lax.erf(x * (2.0 ** -0.5))) + x * jnp.exp(-0.5 * x * x) * (1.0 / math.sqrt(2.0 * math.pi))


def _silu_grad(a):
    sg = _sigmoid(a)
    return sg * (1.0 + a * (1.0 - sg))


def _rowwise(name, fn, rows, vecs, out_rows, out_accs=(), tr=512, after=None):
    S = rows[0].shape[0]
    tr = min(tr, S)
    assert S % tr == 0
    nr, nv, no, na = len(rows), len(vecs), len(out_rows), len(out_accs)
    deps = [] if after is None else [after]

    def body(*refs):
        ins, outs = refs[:nr + nv], refs[nr + nv + len(deps):]
        res = fn(*[r[...] for r in ins])
        if not isinstance(res, (tuple, list)):
            res = (res,)
        for k in range(no):
            outs[k][...] = res[k].astype(outs[k].dtype)
        if na:
            @pl.when(pl.program_id(0) == 0)
            def _():
                for k in range(na):
                    outs[no + k][...] = jnp.zeros_like(outs[no + k])
            for k in range(na):
                outs[no + k][...] += res[no + k]

    in_specs = [pl.BlockSpec((tr, a.shape[1]), lambda i: (i, 0)) for a in rows]
    in_specs += [pl.BlockSpec(v.shape, lambda i: (0, 0)) for v in vecs] + [ANY for _ in deps]
    out_specs = [pl.BlockSpec((tr, c), lambda i: (i, 0)) for c, _ in out_rows]
    out_specs += [pl.BlockSpec(s, lambda i: (0, 0)) for s in out_accs]
    out_shape = [jax.ShapeDtypeStruct((S, c), dt) for c, dt in out_rows]
    out_shape += [jax.ShapeDtypeStruct(s, F32) for s in out_accs]
    return pl.pallas_call(body, name=name, grid=(S // tr,), in_specs=in_specs, out_specs=out_specs,
                          out_shape=out_shape, compiler_params=_params(("arbitrary",)))(*rows, *vecs, *deps)


def _col_tile(n, cap):
    if n <= cap or n % 128:
        return n
    best = 128
    for t in range(128, cap + 1, 128):
        if n % t == 0:
            best = t
    return best


def _mm(name, As, Bs, mode, outs, epi=None, groups=None, extras=(), vecs=(), tm=512, tn_cap=1536):
    M = As[0].shape[0]
    N = Bs[0].shape[1] if mode == "nn" else Bs[0].shape[0]
    tm = min(tm, M)
    tn = _col_tile(N, tn_cap)
    assert M % tm == 0 and N % tn == 0
    npair = len(As)
    groups = groups or [0] * npair
    ng = max(groups) + 1
    nx, nv = len(extras), len(vecs)
    dn = NN if mode == "nn" else NT

    def body(*refs):
        a_refs, b_refs = refs[:npair], refs[npair:2 * npair]
        x_refs = refs[2 * npair:2 * npair + nx]
        v_refs = refs[2 * npair + nx:2 * npair + nx + nv]
        o_refs = refs[2 * npair + nx + nv:]
        step = tn if epi is None else min(tn, MXU_COLS)
        for col in range(0, tn, step):
            sl = slice(col, min(col + step, tn))
            accs = [None] * ng
            for k in range(npair):
                b = b_refs[k][:, sl] if mode == "nn" else b_refs[k][sl, :]
                d = _dot(a_refs[k][...], b, dn)
                accs[groups[k]] = d if accs[groups[k]] is None else accs[groups[k]] + d
            args = accs + [x[:, sl] for x in x_refs] + [v[:, sl] for v in v_refs]
            res = epi(*args) if epi is not None else tuple(accs)
            if not isinstance(res, (tuple, list)):
                res = (res,)
            for o, r in zip(o_refs, res):
                o[:, sl] = r.astype(o.dtype)

    in_specs = [pl.BlockSpec((tm, a.shape[1]), lambda i, j: (i, 0)) for a in As]
    if mode == "nn":
        in_specs += [pl.BlockSpec((b.shape[0], tn), lambda i, j: (0, j)) for b in Bs]
    else:
        in_specs += [pl.BlockSpec((tn, b.shape[1]), lambda i, j: (j, 0)) for b in Bs]
    in_specs += [pl.BlockSpec((tm, tn), lambda i, j: (i, j)) for _ in extras]
    in_specs += [pl.BlockSpec((1, tn), lambda i, j: (0, j)) for _ in vecs]
    out_specs = [pl.BlockSpec((tm, tn), lambda i, j: (i, j)) for _ in outs]
    out_shape = [jax.ShapeDtypeStruct((M, N), dt) for dt in outs]
    return pl.pallas_call(body, name=name, grid=(M // tm, N // tn), in_specs=in_specs, out_specs=out_specs,
                          out_shape=out_shape, compiler_params=_params(("parallel", "parallel")))(
                              *As, *Bs, *extras, *vecs)


def _mm_tn(name, A, B, tk=512, t2_cap=1536):
    S, K1 = A.shape
    N2 = B.shape[1]
    tk = min(tk, S)
    t2 = _col_tile(N2, t2_cap)
    assert S % tk == 0 and N2 % t2 == 0

    def body(a_ref, b_ref, o_ref):
        @pl.when(pl.program_id(1) == 0)
        def _():
            o_ref[...] = jnp.zeros_like(o_ref)
        o_ref[...] += _dot(a_ref[...], b_ref[...], TN)

    return pl.pallas_call(
        body, name=name, grid=(N2 // t2, S // tk),
        in_specs=[pl.BlockSpec((tk, K1), lambda j, k: (k, 0)), pl.BlockSpec((tk, t2), lambda j, k: (k, j))],
        out_specs=pl.BlockSpec((K1, t2), lambda j, k: (0, j)),
        out_shape=jax.ShapeDtypeStruct((K1, N2), F32),
        compiler_params=_params(("parallel", "arbitrary")))(A, B)


def _norm_mod_fwd(name, x, nw, sc, sh, after=None):
    def fn(x, nw, sc, sh):
        rstd = lax.rsqrt(jnp.mean(x * x, axis=-1, keepdims=True) + EPS)
        return (x * rstd * nw) * (1.0 + sc) + sh
    return _rowwise(name, fn, [x], [nw, sc, sh], [(D, BF16)], after=after)[0]


def _norm_mod_bwd(name, x, dh, dres, nw, sc, after=None):
    def fn(x, dh, dres, nw, sc):
        rstd = lax.rsqrt(jnp.mean(x * x, axis=-1, keepdims=True) + EPS)
        xh = x * rstd
        dn = dh * (1.0 + sc)
        dxh = dn * nw
        dx = rstd * (dxh - xh * jnp.mean(dxh * xh, axis=-1, keepdims=True))
        return (dres + dx, jnp.sum(dh, axis=0, keepdims=True), jnp.sum(dh * (xh * nw), axis=0, keepdims=True),
                jnp.sum(dn * xh, axis=0, keepdims=True))
    return _rowwise(name, fn, [x, dh, dres], [nw, sc], [(D, F32)], [(1, D)] * 3, after=after)


def _gate_bwd(name, dx, y, g, after=None):
    def fn(dx, y, g):
        dy = dx * g
        return dy, jnp.sum(dx * y, axis=0, keepdims=True), jnp.sum(dy, axis=0, keepdims=True)
    return _rowwise(name, fn, [dx, y], [g], [(D, BF16)], [(1, D)] * 2, after=after)


def _loss_head(x, tgt, fw):
    def fn(x, tgt, fw):
        rstd = lax.rsqrt(jnp.mean(x * x, axis=-1, keepdims=True) + EPS)
        xh = x * rstd
        err = xh * fw - tgt
        dout = err * (1.0 / D)
        dxh = dout * fw
        dx = rstd * (dxh - xh * jnp.mean(dxh * xh, axis=-1, keepdims=True))
        sq = jnp.sum(jnp.sum(err * err, axis=1, keepdims=True), axis=0, keepdims=True)
        return dx, sq, jnp.sum(dout * xh, axis=0, keepdims=True)
    return _rowwise("loss_head", fn, [x, tgt], [fw], [(D, F32)], [(1, 1), (1, D)])


def _ffn_fwd(tag, h, wg, wu, wd, x, g2):
    def act(a, b):
        return a, b, a * _sigmoid(a) * b
    a, b, f = _mm(f"ffn_up_{tag}", [h, h], [wg, wu], "nt", [BF16, BF16, BF16], epi=act, groups=[0, 1], tn_cap=1408,
                  tm=1024)

    def res(y, x, g):
        return y, x + g * y
    y, xo = _mm(f"ffn_down_{tag}", [f], [wd], "nn", [F32, F32], epi=res, extras=[x], vecs=[g2])
    return a, b, f, y, xo


def _ffn_bwd(tag, dx, h, a, b, f, y, wg, wu, wd, g2, after=None):
    dy, dg2, _ = _gate_bwd(f"ffn_gate_bwd_{tag}", dx, y, g2, after=after)

    def act_bwd(df, a, b):
        a, b = a.astype(F32), b.astype(F32)
        sg = _sigmoid(a)
        return df * b * (sg * (1.0 + a * (1.0 - sg))), df * (a * sg)
    da, db = _mm(f"ffn_dact_{tag}", [dy], [wd], "nt", [BF16, BF16], epi=act_bwd, extras=[a, b], tn_cap=1408, tm=1024)
    dwd = _mm_tn(f"ffn_dwd_{tag}", f, dy)
    dwg = _mm_tn(f"ffn_dwg_{tag}", da, h)
    dwu = _mm_tn(f"ffn_dwu_{tag}", db, h)
    dh = _mm(f"ffn_dh_{tag}", [da, db], [wg, wu], "nn", [F32])[0]
    return dh, dg2, dwg, dwu, dwd


def _conv_fwd(xr, w, b, tb=512):
    S, C = xr.shape
    tb = min(tb, S)

    def body(x_ref, halo_ref, w_ref, b_ref, pre_ref, out_ref):
        i = pl.program_id(0)
        halo = jnp.where(i > 0, halo_ref[...], 0.0)
        xe = jnp.concatenate([halo, x_ref[...]], axis=0)
        pre = w_ref[3:4, :] * x_ref[...] + b_ref[...]
        for j in (1, 2, 3):
            pre = pre + w_ref[3 - j:4 - j, :] * pltpu.roll(xe, j, axis=0)[8:, :]
        pre_ref[...] = pre
        out_ref[...] = pre * _sigmoid(pre)

    return pl.pallas_call(
        body, name="conv_fwd", grid=(S // tb,),
        in_specs=[pl.BlockSpec((tb, C), lambda i: (i, 0)),
                  pl.BlockSpec((8, C), lambda i: (jnp.maximum(i * (tb // 8) - 1, 0), 0)),
                  pl.BlockSpec((4, C), lambda i: (0, 0)), pl.BlockSpec((1, C), lambda i: (0, 0))],
        out_specs=[pl.BlockSpec((tb, C), lambda i: (i, 0))] * 2,
        out_shape=[jax.ShapeDtypeStruct((S, C), F32)] * 2,
        compiler_params=_params(("parallel",)))(xr, xr, w, b)


def _conv_bwd(dxc, pre, xr, w, tb=512):
    S, C = xr.shape
    tb = min(tb, S)
    nblk = S // tb

    def body(d_ref, p_ref, dn_ref, pn_ref, x_ref, xh_ref, w_ref, dx_ref, dw_ref, db_ref):
        i = pl.program_id(0)

        @pl.when(i == 0)
        def _():
            dw_ref[...] = jnp.zeros_like(dw_ref)
            db_ref[...] = jnp.zeros_like(db_ref)

        dpre = d_ref[...] * _silu_grad(p_ref[...])
        dnext = jnp.where(i < nblk - 1, dn_ref[...] * _silu_grad(pn_ref[...]), 0.0)
        pe = jnp.concatenate([dpre, dnext], axis=0)
        dx = w_ref[3:4, :] * dpre
        for j in (1, 2, 3):
            dx = dx + w_ref[3 - j:4 - j, :] * pltpu.roll(pe, tb + 8 - j, axis=0)[:tb, :]
        dx_ref[...] = dx.astype(dx_ref.dtype)
        halo = jnp.where(i > 0, xh_ref[...], 0.0)
        xe = jnp.concatenate([halo, x_ref[...]], axis=0)
        for k in range(3):
            dw_ref[k:k + 1, :] += jnp.sum(dpre * pltpu.roll(xe, 3 - k, axis=0)[8:, :], axis=0, keepdims=True)
        dw_ref[3:4, :] += jnp.sum(dpre * x_ref[...], axis=0, keepdims=True)
        db_ref[...] += jnp.sum(dpre, axis=0, keepdims=True)

    blk = pl.BlockSpec((tb, C), lambda i: (i, 0))
    nxt = pl.BlockSpec((8, C), lambda i: (jnp.minimum((i + 1) * (tb // 8), S // 8 - 1), 0))
    prv = pl.BlockSpec((8, C), lambda i: (jnp.maximum(i * (tb // 8) - 1, 0), 0))
    return pl.pallas_call(
        body, name="conv_bwd", grid=(nblk,),
        in_specs=[blk, blk, nxt, nxt, blk, prv, pl.BlockSpec((4, C), lambda i: (0, 0))],
        out_specs=[blk, pl.BlockSpec((4, C), lambda i: (0, 0)), pl.BlockSpec((1, C), lambda i: (0, 0))],
        out_shape=[jax.ShapeDtypeStruct((S, C), BF16), jax.ShapeDtypeStruct((4, C), F32),
                   jax.ShapeDtypeStruct((1, C), F32)],
        compiler_params=_params(("arbitrary",)))(dxc, pre, dxc, pre, xr, xr, w)


def _iota(shape, dim):
    return lax.broadcasted_iota(jnp.int32, shape, dim)


def _colsel(m, lane, h):
    return jnp.sum(jnp.where(lane == h, m, 0.0), axis=1, keepdims=True)


def _cumsum_rows(v):
    r = _iota(v.shape, 0)
    k = 1
    while k < v.shape[0]:
        v = v + jnp.where(r >= k, pltpu.roll(v, k, axis=0), 0.0)
        k *= 2
    return v


def _suffix_sum_rows(v):
    n = v.shape[0]
    r = _iota(v.shape, 0)
    k = 1
    while k < n:
        v = v + jnp.where(r < n - k, pltpu.roll(v, n - k, axis=0), 0.0)
        k *= 2
    return v


def _ssd_fwd(xc, dtr, z, dtb, alog, dskl, nw):
    S = xc.shape[0]
    nc = S // L

    def body(xc_ref, dtr_ref, z_ref, dtb_ref, alog_ref, dsk_ref, nw_ref, ya_ref, y_ref, prev_ref,
             st_ref, cum_ref, cumT_ref):
        i = pl.program_id(0)

        @pl.when(i == 0)
        def _():
            st_ref[...] = jnp.zeros_like(st_ref)

        lane = _iota((L, 128), 1)
        lane1 = _iota((1, 128), 1)
        lo = lane < 64
        lo1 = lane1 < 64
        tril = _iota((L, L), 0) >= _iota((L, L), 1)
        dt = _softplus(dtr_ref[...] + dtb_ref[...])
        a_neg = -jnp.exp(alog_ref[...])
        cum = _cumsum_rows(dt * a_neg)
        cum_ref[...] = cum
        cumT_ref[...] = cum.T
        last_all = cum_ref[L - 1:L, :]
        prev_t = st_ref[...]
        prev_ref[0] = prev_t
        for g in range(2):
            bg = xc_ref[:, 1024 + g * 128:1152 + g * 128]
            cg = xc_ref[:, 1280 + g * 128:1408 + g * 128]
            gmat = _dot(cg, bg, NT)
            yoff = _dot(cg, prev_t[:, g * 512:(g + 1) * 512])
            bg_t = bg.T
            for jp in range(4):
                j = g * 4 + jp
                sl = slice(j * 128, (j + 1) * 128)
                xp = xc_ref[:, sl]
                cc = [_colsel(cum, lane, 2 * j), _colsel(cum, lane, 2 * j + 1)]
                cum_l = jnp.where(lo, cc[0], cc[1])
                dt_l = jnp.where(lo, _colsel(dt, lane, 2 * j), _colsel(dt, lane, 2 * j + 1))
                last_l = jnp.where(lo1, _colsel(last_all, lane1, 2 * j), _colsel(last_all, lane1, 2 * j + 1))
                xd = xp * dt_l
                ys = []
                for hh in range(2):
                    seg = cc[hh] - cumT_ref[2 * j + hh:2 * j + hh + 1, :]
                    dm = jnp.where(tril, jnp.exp(jnp.where(tril, seg, 0.0)), 0.0)
                    ys.append(_dot(gmat * dm, xd))
                y_ref[:, sl] = (jnp.where(lo, ys[0], ys[1]) + jnp.exp(cum_l) * yoff[:, jp * 128:(jp + 1) * 128]
                                + dsk_ref[:, sl] * xp)
                st_ref[:, sl] = prev_t[:, sl] * jnp.exp(last_l) + _dot(bg_t, xd * jnp.exp(last_l - cum_l))
        for g in range(2):
            sl = slice(g * 512, (g + 1) * 512)
            zz = z_ref[:, sl]
            yg = y_ref[:, sl] * (zz * _sigmoid(zz))
            rstd = lax.rsqrt(jnp.mean(yg * yg, axis=-1, keepdims=True) + EPS)
            ya_ref[:, sl] = (yg * rstd * nw_ref[:, sl]).astype(ya_ref.dtype)

    blk = lambda c: pl.BlockSpec((L, c), lambda i: (i, 0))
    vec = lambda c: pl.BlockSpec((1, c), lambda i: (0, 0))
    return pl.pallas_call(
        body, name="ssd_fwd", grid=(nc,),
        in_specs=[blk(1536), blk(128), blk(1024), vec(128), vec(128), vec(1024), vec(1024)],
        out_specs=[blk(1024), blk(1024), pl.BlockSpec((1, NSTATE, 1024), lambda i: (i, 0, 0))],
        out_shape=[jax.ShapeDtypeStruct((S, 1024), BF16), jax.ShapeDtypeStruct((S, 1024), F32),
                   jax.ShapeDtypeStruct((nc, NSTATE, 1024), F32)],
        scratch_shapes=[pltpu.VMEM((NSTATE, 1024), F32), pltpu.VMEM((L, 128), F32), pltpu.VMEM((L, 128), F32)],
        compiler_params=_params(("arbitrary",)))(xc, dtr, z, dtb, alog, dskl, nw)


def _ssd_bwd(dya, y, z, xc, dtr, prev, dtb, alog, dskl, nw):
    S = xc.shape[0]
    nc = S // L

    def body(dya_ref, y_ref, z_ref, xc_ref, dtr_ref, prev_ref, dtb_ref, alog_ref, dsk_ref, nw_ref,
             dz_ref, dxc_ref, ddtr_ref, dnw_ref, ddsk_ref, dalog_ref, ddtb_ref,
             dst_ref, cum_ref, cumT_ref, dy_ref, dskacc_ref):
        i = pl.program_id(0)

        @pl.when(i == 0)
        def _():
            dst_ref[...] = jnp.zeros_like(dst_ref)
            dskacc_ref[...] = jnp.zeros_like(dskacc_ref)
            dnw_ref[...] = jnp.zeros_like(dnw_ref)
            dalog_ref[...] = jnp.zeros_like(dalog_ref)
            ddtb_ref[...] = jnp.zeros_like(ddtb_ref)

        lane = _iota((L, 128), 1)
        lane1 = _iota((1, 128), 1)
        lo = lane < 64
        lo1 = lane1 < 64
        r2, c2 = _iota((L, L), 0), _iota((L, L), 1)
        tril = r2 >= c2
        triu = r2 <= c2
        is_last = _iota((L, 1), 0) == L - 1

        for g in range(2):
            sl = slice(g * 512, (g + 1) * 512)
            zz = z_ref[:, sl]
            sg = _sigmoid(zz)
            zg = zz * sg
            yv = y_ref[:, sl]
            yg = yv * zg
            rstd = lax.rsqrt(jnp.mean(yg * yg, axis=-1, keepdims=True) + EPS)
            xh = yg * rstd
            d_out = dya_ref[:, sl]
            dnw_ref[:, sl] += jnp.sum(d_out * xh, axis=0, keepdims=True)
            dyn = d_out * nw_ref[:, sl]
            dyg = rstd * (dyn - xh * jnp.mean(dyn * xh, axis=-1, keepdims=True))
            dy_ref[:, sl] = dyg * zg
            dz_ref[:, sl] = (dyg * yv * (sg * (1.0 + zz * (1.0 - sg)))).astype(dz_ref.dtype)

        dtin = dtr_ref[...] + dtb_ref[...]
        dt = _softplus(dtin)
        a_neg = -jnp.exp(alog_ref[...])
        cum = _cumsum_rows(dt * a_neg)
        cum_ref[...] = cum
        cumT_ref[...] = cum.T
        last_all = cum_ref[L - 1:L, :]
        prev_t = prev_ref[0]
        dn_t = dst_ref[...]
        dcum = jnp.zeros((L, 128), F32)
        ddt = jnp.zeros((L, 128), F32)
        col_sums = jnp.zeros((128, L), F32)
        head_row = _iota((128, L), 0)
        for g in range(2):
            gsl = slice(g * 512, (g + 1) * 512)
            bg = xc_ref[:, 1024 + g * 128:1152 + g * 128]
            cg = xc_ref[:, 1280 + g * 128:1408 + g * 128]
            gmat = _dot(cg, bg, NT)
            pg = prev_t[:, gsl]
            zmat = _dot(cg, pg)
            dgm = jnp.zeros((L, L), F32)
            db_acc = jnp.zeros((L, NSTATE), F32)
            dz_parts, cd_parts = [], []
            for jp in range(4):
                j = g * 4 + jp
                sl = slice(j * 128, (j + 1) * 128)
                xp = xc_ref[:, sl]
                dyp = dy_ref[:, sl]
                cc = [_colsel(cum, lane, 2 * j), _colsel(cum, lane, 2 * j + 1)]
                lc = [_colsel(last_all, lane1, 2 * j), _colsel(last_all, lane1, 2 * j + 1)]
                cum_l = jnp.where(lo, cc[0], cc[1])
                dt_l = jnp.where(lo, _colsel(dt, lane, 2 * j), _colsel(dt, lane, 2 * j + 1))
                last_l = jnp.where(lo1, lc[0], lc[1])
                e_l = jnp.exp(cum_l)
                dte_l = jnp.exp(last_l - cum_l)
                cd_l = jnp.exp(last_l)
                cd_parts.append(cd_l)
                xd = xp * dt_l
                dskacc_ref[:, sl] += jnp.sum(dyp * xp, axis=0, keepdims=True)
                dxp = dsk_ref[:, sl] * dyp
                t = dyp * (e_l * zmat[:, jp * 128:(jp + 1) * 128])
                dcc = [jnp.sum(jnp.where(lo, t, 0.0), axis=1, keepdims=True),
                       jnp.sum(jnp.where(lo, 0.0, t), axis=1, keepdims=True)]
                dz_parts.append(e_l * dyp)
                dnp_ = dn_t[:, sl]
                t2 = jnp.sum(dnp_ * prev_t[:, sl], axis=0, keepdims=True)
                dcd = [jnp.sum(jnp.where(lo1, t2, 0.0), axis=1, keepdims=True),
                       jnp.sum(jnp.where(lo1, 0.0, t2), axis=1, keepdims=True)]
                wm = _dot(bg, dnp_)
                dxd = wm * dte_l
                t3 = wm * xd
                ddte = [jnp.sum(jnp.where(lo, t3, 0.0), axis=1, keepdims=True),
                        jnp.sum(jnp.where(lo, 0.0, t3), axis=1, keepdims=True)]
                db_acc = db_acc + _dot(xd * dte_l, dnp_, NT)
                for hh in range(2):
                    h = 2 * j + hh
                    half = lo if hh == 0 else jnp.logical_not(lo)
                    row = cumT_ref[h:h + 1, :]
                    dm = jnp.where(tril, jnp.exp(jnp.where(tril, cc[hh] - row, 0.0)), 0.0)
                    m = gmat * dm
                    dym = jnp.where(half, dyp, 0.0)
                    d_m = _dot(dym, xd, NT)
                    dxd = dxd + _dot(m, dym, TN)
                    t_mat = d_m * m
                    dcc[hh] = dcc[hh] + jnp.sum(t_mat, axis=1, keepdims=True)
                    col_sums = jnp.where(head_row == h, jnp.sum(t_mat, axis=0, keepdims=True), col_sums)
                    dgm = dgm + d_m * dm
                    dte_c = jnp.exp(lc[hh] - cc[hh])
                    dcc[hh] = dcc[hh] - ddte[hh] * dte_c
                    endc = dcd[hh] * jnp.exp(lc[hh]) + jnp.sum(ddte[hh] * dte_c, axis=0, keepdims=True)
                    dcc[hh] = dcc[hh] + jnp.where(is_last, endc, 0.0)
                    dcum = jnp.where(lane == h, dcc[hh], dcum)
                dxc_ref[:, sl] = dxp + dxd * dt_l
                t4 = dxd * xp
                ddt = jnp.where(lane == 2 * j, jnp.sum(jnp.where(lo, t4, 0.0), axis=1, keepdims=True), ddt)
                ddt = jnp.where(lane == 2 * j + 1, jnp.sum(jnp.where(lo, 0.0, t4), axis=1, keepdims=True), ddt)
            dzg = jnp.concatenate(dz_parts, axis=1)
            dst_ref[:, gsl] = dn_t[:, gsl] * jnp.concatenate(cd_parts, axis=1) + _dot(cg.T, dzg)
            dxc_ref[:, 1280 + g * 128:1408 + g * 128] = _dot(dgm, bg) + _dot(dzg, pg, NT)
            dxc_ref[:, 1024 + g * 128:1152 + g * 128] = _dot(dgm, cg, TN) + db_acc
        dla = _suffix_sum_rows(dcum - col_sums.T)
        ddt = ddt + dla * a_neg
        dalog_ref[...] += jnp.sum(dla * dt, axis=0, keepdims=True) * a_neg
        ddtr = jnp.where(lane < 16, ddt * _sigmoid(dtin), 0.0)
        ddtr_ref[...] = ddtr.astype(ddtr_ref.dtype)
        ddtb_ref[...] += jnp.sum(ddtr, axis=0, keepdims=True)

        @pl.when(i == nc - 1)
        def _():
            seg = (_iota((1024, 128), 0) // 64 == _iota((1024, 128), 1)).astype(F32)
            acc8 = jnp.broadcast_to(dskacc_ref[...], (8, 1024))
            ddsk_ref[...] = lax.dot_general(acc8, seg, NN, precision=lax.Precision.HIGHEST,
                                            preferred_element_type=F32)

    rev = lambda c: pl.BlockSpec((L, c), lambda i: (nc - 1 - i, 0))
    vec = lambda c: pl.BlockSpec((1, c), lambda i: (0, 0))
    return pl.pallas_call(
        body, name="ssd_bwd", grid=(nc,),
        in_specs=[rev(1024), rev(1024), rev(1024), rev(1536), rev(128),
                  pl.BlockSpec((1, NSTATE, 1024), lambda i: (nc - 1 - i, 0, 0)),
                  vec(128), vec(128), vec(1024), vec(1024)],
        out_specs=[rev(1024), rev(1536), rev(128), vec(1024), pl.BlockSpec((8, 128), lambda i: (0, 0)),
                   vec(128), vec(128)],
        out_shape=[jax.ShapeDtypeStruct((S, 1024), BF16), jax.ShapeDtypeStruct((S, 1536), F32),
                   jax.ShapeDtypeStruct((S, 128), BF16), jax.ShapeDtypeStruct((1, 1024), F32),
                   jax.ShapeDtypeStruct((8, 128), F32), jax.ShapeDtypeStruct((1, 128), F32),
                   jax.ShapeDtypeStruct((1, 128), F32)],
        scratch_shapes=[pltpu.VMEM((NSTATE, 1024), F32), pltpu.VMEM((L, 128), F32), pltpu.VMEM((L, 128), F32),
                        pltpu.VMEM((L, 1024), F32), pltpu.VMEM((1, 1024), F32)],
        compiler_params=_params(("arbitrary",)))(dya, y, z, xc, dtr, prev, dtb, alog, dskl, nw)


def _layer_norm_parts(vg):
    mu = jnp.mean(vg, axis=-1, keepdims=True)
    vc = vg - mu
    rstd = lax.rsqrt(jnp.mean(vc * vc, axis=-1, keepdims=True) + EPS)
    return vc * rstd, rstd


def _gmlp_fwd(u, v, lnw, lnb, ws, bse, tb=512):
    S = u.shape[0]
    tb = min(tb, S)

    def body(u_ref, v_ref, lnw_ref, lnb_ref, ws_ref, bse_ref, o_ref, vn_ref):
        tril = _iota((L, L), 0) >= _iota((L, L), 1)
        xh, _ = _layer_norm_parts(_gelu(v_ref[...]))
        vn_ref[...] = xh * lnw_ref[...] + lnb_ref[...]
        for g in range(8):
            w = jnp.where(tril, ws_ref[g], 0.0)
            gs = slice(g * 128, (g + 1) * 128)
            for ch in range(tb // L):
                rs = slice(ch * L, (ch + 1) * L)
                sv = _dot(w, vn_ref[rs, gs]) + bse_ref[g]
                o_ref[rs, gs] = (_gelu(u_ref[rs, gs]) * sv).astype(o_ref.dtype)

    blk = pl.BlockSpec((tb, 1024), lambda i: (i, 0))
    vec = pl.BlockSpec((1, 1024), lambda i: (0, 0))
    cube = pl.BlockSpec((8, L, 128), lambda i: (0, 0, 0))
    return pl.pallas_call(
        body, name="gmlp_fwd", grid=(S // tb,), in_specs=[blk, blk, vec, vec, cube, cube], out_specs=blk,
        out_shape=jax.ShapeDtypeStruct((S, 1024), BF16), scratch_shapes=[pltpu.VMEM((tb, 1024), F32)],
        compiler_params=_params(("parallel",)))(u, v, lnw, lnb, ws, bse)


def _gmlp_bwd(dyb, u, v, lnw, lnb, ws, bse, tb=512):
    S = u.shape[0]
    tb = min(tb, S)

    def body(d_ref, u_ref, v_ref, lnw_ref, lnb_ref, ws_ref, bse_ref,
             du_ref, dv_ref, dws_ref, dbse_ref, dlnw_ref, dlnb_ref, vn_ref, dvn_ref):
        @pl.when(pl.program_id(0) == 0)
        def _():
            dws_ref[...] = jnp.zeros_like(dws_ref)
            dbse_ref[...] = jnp.zeros_like(dbse_ref)
            dlnw_ref[...] = jnp.zeros_like(dlnw_ref)
            dlnb_ref[...] = jnp.zeros_like(dlnb_ref)

        tril = _iota((L, L), 0) >= _iota((L, L), 1)
        vv = v_ref[...]
        xh, rstd = _layer_norm_parts(_gelu(vv))
        vn_ref[...] = xh * lnw_ref[...] + lnb_ref[...]
        for g in range(8):
            w = jnp.where(tril, ws_ref[g], 0.0)
            w_t = w.T
            gs = slice(g * 128, (g + 1) * 128)
            dw = jnp.zeros((L, L), F32)
            dbs = jnp.zeros((L, 128), F32)
            for ch in range(tb // L):
                rs = slice(ch * L, (ch + 1) * L)
                vn = vn_ref[rs, gs]
                sv = _dot(w, vn) + bse_ref[g]
                uu = u_ref[rs, gs]
                dd = d_ref[rs, gs]
                du_ref[rs, gs] = (dd * sv * _gelu_grad(uu)).astype(du_ref.dtype)
                dsv = dd * _gelu(uu)
                dw = dw + _dot(dsv, vn, NT)
                dbs = dbs + dsv
                dvn_ref[rs, gs] = _dot(w_t, dsv)
            dws_ref[g] += jnp.where(tril, dw, 0.0)
            dbse_ref[g] += dbs
        dvn = dvn_ref[...]
        dlnw_ref[...] += jnp.sum(dvn * xh, axis=0, keepdims=True)
        dlnb_ref[...] += jnp.sum(dvn, axis=0, keepdims=True)
        dxh = dvn * lnw_ref[...]
        dvg = rstd * (dxh - jnp.mean(dxh, axis=-1, keepdims=True) - xh * jnp.mean(dxh * xh, axis=-1, keepdims=True))
        dv_ref[...] = (dvg * _gelu_grad(vv)).astype(dv_ref.dtype)

    blk = pl.BlockSpec((tb, 1024), lambda i: (i, 0))
    vec = pl.BlockSpec((1, 1024), lambda i: (0, 0))
    cube = pl.BlockSpec((8, L, 128), lambda i: (0, 0, 0))
    return pl.pallas_call(
        body, name="gmlp_bwd", grid=(S // tb,), in_specs=[blk, blk, blk, vec, vec, cube, cube],
        out_specs=[blk, blk, cube, cube, vec, vec],
        out_shape=[jax.ShapeDtypeStruct((S, 1024), BF16), jax.ShapeDtypeStruct((S, 1024), BF16),
                   jax.ShapeDtypeStruct((8, L, 128), F32), jax.ShapeDtypeStruct((8, L, 128), F32),
                   jax.ShapeDtypeStruct((1, 1024), F32), jax.ShapeDtypeStruct((1, 1024), F32)],
        scratch_shapes=[pltpu.VMEM((tb, 1024), F32), pltpu.VMEM((tb, 1024), F32)],
        compiler_params=_params(("arbitrary",)))(dyb, u, v, lnw, lnb, ws, bse)


def _lane_sum(name, a):
    def body(a_ref, o_ref):
        o_ref[...] = jnp.sum(a_ref[...], axis=1, keepdims=True)
    return pl.pallas_call(body, name=name, out_shape=jax.ShapeDtypeStruct((a.shape[0], 1), F32))(a)


def _bucket_onehot_t():
    qi = np.arange(L)[:, None]
    sj = np.arange(2 * L)[None, :]
    dist = np.maximum(qi + L - sj, 0)
    log_ratio = (np.log(np.maximum(dist, 1).astype(np.float32) / np.float32(16)) / np.float32(math.log(128 / 16)))
    large = 16 + (log_ratio.astype(np.float32) * np.float32(16)).astype(np.int32)
    bucket = np.where(dist < 16, dist, np.minimum(large, 31)).reshape(-1)
    return (np.arange(32)[:, None] == bucket[None, :]).astype(np.float32)


def _rel_bias(table_t, onehot_t):
    def body(t_ref, oh_ref, o_ref):
        o_ref[...] = lax.dot_general(t_ref[...], oh_ref[...], NN, precision=lax.Precision.HIGHEST,
                                     preferred_element_type=F32)
    return pl.pallas_call(body, name="rel_bias", out_shape=jax.ShapeDtypeStruct((16, L * 2 * L), F32),
                          compiler_params=_params())(table_t, onehot_t)


def _rel_bias_bwd(dbias, onehot_t):
    def body(d_ref, oh_ref, o_ref):
        o_ref[...] = lax.dot_general(d_ref[...], oh_ref[...], NT, precision=lax.Precision.HIGHEST,
                                     preferred_element_type=F32)
    return pl.pallas_call(body, name="rel_bias_bwd", out_shape=jax.ShapeDtypeStruct((16, 32), F32),
                          compiler_params=_params())(dbias, onehot_t)


def _band(kp, kc, lo):
    kk = jnp.concatenate([kp, kc], axis=0)
    kr = pltpu.roll(kk, 64, axis=1)
    return [jnp.where(lo, kk, kr), jnp.where(lo, kr, kk)]


def _attn_rows(ref, j, lo):
    parts = []
    for t in range(8):
        pair = ref[:, (4 * j + t // 2) * 128:(4 * j + t // 2 + 1) * 128]
        parts.append(jnp.where(lo if t % 2 == 0 else jnp.logical_not(lo), pair, 0.0))
    return jnp.concatenate(parts, axis=0)


def _attn_mask(i, rows):
    qi, sj = _iota((rows, 2 * L), 0) & (L - 1), _iota((rows, 2 * L), 1)
    rel = qi + L - sj
    return (rel >= 0) & (rel < L) & ((sj >= L) | (i > 0))


def _per_head_col(vals):
    return jnp.concatenate([jnp.broadcast_to(v, (L, 1)) for v in vals], axis=0)


SMEM = pl.BlockSpec(memory_space=pltpu.SMEM)


def _attn_fwd(qkv, bias, sinks):
    S = qkv.shape[0]
    nb = S // L
    scale = 64 ** -0.5

    def body(sink_ref, q_ref, kc_ref, vc_ref, kp_ref, vp_ref, bias_ref, o_ref, lse_ref):
        i = pl.program_id(0)
        lane = _iota((L, 128), 1)
        lo = lane < 64
        lo2 = _iota((2 * L, 128), 1) < 64
        mask = _attn_mask(i, 8 * L)
        kd = _band(kp_ref[...], kc_ref[...], lo2)
        vd = _band(vp_ref[...], vc_ref[...], lo2)
        lse = jnp.zeros((L, 128), F32)
        for j in range(2):
            q_all = _attn_rows(q_ref, j, lo)
            lg = _dot(q_all, kd[j], NT) * scale + bias_ref[8 * j:8 * j + 8].reshape(8 * L, 2 * L)
            lg = jnp.where(mask, lg, NEG_INF)
            s = _per_head_col([sink_ref[8 * j + t] for t in range(8)])
            m = jnp.maximum(jnp.max(lg, axis=1, keepdims=True), s)
            p = jnp.where(mask, jnp.exp(lg - m), 0.0)
            den = jnp.sum(p, axis=1, keepdims=True) + jnp.exp(s - m)
            out = _dot(p / den, vd[j])
            lse_col = m + jnp.log(den)
            for t in range(0, 8, 2):
                sl = slice((4 * j + t // 2) * 128, (4 * j + t // 2 + 1) * 128)
                o_ref[:, sl] = jnp.where(lo, out[t * L:(t + 1) * L], out[(t + 1) * L:(t + 2) * L]).astype(o_ref.dtype)
            for t in range(8):
                lse = jnp.where(lane == 8 * j + t, lse_col[t * L:(t + 1) * L], lse)
        lse_ref[...] = lse

    prev = lambda col: pl.BlockSpec((L, 128), lambda i: (jnp.maximum(i - 1, 0), col))
    cur = lambda col: pl.BlockSpec((L, 128), lambda i: (i, col))
    return pl.pallas_call(
        body, name="attn_fwd", grid=(nb,),
        in_specs=[SMEM, pl.BlockSpec((L, 1024), lambda i: (i, 0)), cur(8), cur(9), prev(8), prev(9),
                  pl.BlockSpec((16, L, 2 * L), lambda i: (0, 0, 0))],
        out_specs=[pl.BlockSpec((L, 1024), lambda i: (i, 0)), pl.BlockSpec((L, 128), lambda i: (i, 0))],
        out_shape=[jax.ShapeDtypeStruct((S, 1024), BF16), jax.ShapeDtypeStruct((S, 128), F32)],
        compiler_params=_params(("parallel",)))(sinks, qkv, qkv, qkv, qkv, qkv, bias)


def _attn_bwd(qkv, d_o, lse, bias, sinks):
    S = qkv.shape[0]
    nb = S // L
    scale = 64 ** -0.5

    def body(sink_ref, q_ref, kc_ref, vc_ref, kp_ref, vp_ref, do_ref, lse_ref, bias_ref,
             dq_ref, dkv_ref, dbias_ref, dsink_ref, dbq_ref, dbkv_ref, carry_ref):
        i = pl.program_id(0)

        @pl.when(i == 0)
        def _():
            dbias_ref[...] = jnp.zeros_like(dbias_ref)
            dsink_ref[...] = jnp.zeros_like(dsink_ref)
            dbq_ref[...] = jnp.zeros_like(dbq_ref)
            dbkv_ref[...] = jnp.zeros_like(dbkv_ref)
            carry_ref[...] = jnp.zeros_like(carry_ref)

        @pl.when(i < nb)
        def _():
            lane = _iota((L, 128), 1)
            lane1 = _iota((1, 128), 1)
            lo = lane < 64
            lo2 = _iota((2 * L, 128), 1) < 64
            mask = _attn_mask(i, 8 * L)
            kd = _band(kp_ref[...], kc_ref[...], lo2)
            vd = _band(vp_ref[...], vc_ref[...], lo2)
            lse_all = lse_ref[...]
            dsink = jnp.zeros((1, 128), F32)
            tot_k, tot_v = [], []
            for j in range(2):
                q_all = _attn_rows(q_ref, j, lo)
                do_all = _attn_rows(do_ref, j, lo)
                lse_col = _per_head_col([_colsel(lse_all, lane, 8 * j + t) for t in range(8)])
                lg = _dot(q_all, kd[j], NT) * scale + bias_ref[8 * j:8 * j + 8].reshape(8 * L, 2 * L)
                p = jnp.where(mask, jnp.exp(jnp.where(mask, lg, NEG_INF) - lse_col), 0.0)
                dp = _dot(do_all, vd[j], NT)
                delta = jnp.sum(p * dp, axis=1, keepdims=True)
                ds = p * (dp - delta)
                dbias_ref[8 * j:8 * j + 8] += ds.reshape(8, L, 2 * L)
                s = _per_head_col([sink_ref[8 * j + t] for t in range(8)])
                sink_part = -jnp.exp(s - lse_col) * delta
                for t in range(8):
                    dsink = dsink + jnp.where(lane1 == 8 * j + t,
                                              jnp.sum(sink_part[t * L:(t + 1) * L], axis=0, keepdims=True), 0.0)
                dss = ds * scale
                dq_all = _dot(dss, kd[j])
                for t in range(0, 8, 2):
                    sl = slice((4 * j + t // 2) * 128, (4 * j + t // 2 + 1) * 128)
                    dq = jnp.where(lo, dq_all[t * L:(t + 1) * L], dq_all[(t + 1) * L:(t + 2) * L])
                    dq_ref[:, sl] = dq.astype(dq_ref.dtype)
                    dbq_ref[:, sl] += jnp.sum(dq, axis=0, keepdims=True)
                acc_k = _dot(dss, q_all, TN)
                acc_v = _dot(p, do_all, TN)
                tot_k.append(acc_k + pltpu.roll(acc_k, 64, axis=1))
                tot_v.append(acc_v + pltpu.roll(acc_v, 64, axis=1))
            dsink_ref[...] += dsink
            dkv = jnp.concatenate([jnp.where(lo2, tot_k[0], tot_k[1]), jnp.where(lo2, tot_v[0], tot_v[1])], axis=1)
            dbkv_ref[...] += jnp.sum(dkv, axis=0, keepdims=True)
            dkv_ref[...] = (carry_ref[...] + dkv[:L, :]).astype(dkv_ref.dtype)
            carry_ref[...] = dkv[L:, :]

        @pl.when(i == nb)
        def _():
            dkv_ref[...] = carry_ref[...].astype(dkv_ref.dtype)

    c = lambda i: jnp.minimum(i, nb - 1)
    prev = lambda col: pl.BlockSpec((L, 128), lambda i: (jnp.maximum(c(i) - 1, 0), col))
    cur = lambda col: pl.BlockSpec((L, 128), lambda i: (c(i), col))
    row = lambda w: pl.BlockSpec((L, w), lambda i: (c(i), 0))
    cube = pl.BlockSpec((16, L, 2 * L), lambda i: (0, 0, 0))
    vec = lambda w: pl.BlockSpec((1, w), lambda i: (0, 0))
    return pl.pallas_call(
        body, name="attn_bwd", grid=(nb + 1,),
        in_specs=[SMEM, row(1024), cur(8), cur(9), prev(8), prev(9), row(1024), row(128), cube],
        out_specs=[row(1024), pl.BlockSpec((L, 256), lambda i: (jnp.maximum(i - 1, 0), 0)), cube,
                   vec(128), vec(1024), vec(256)],
        out_shape=[jax.ShapeDtypeStruct((S, 1024), BF16), jax.ShapeDtypeStruct((S, 256), BF16),
                   jax.ShapeDtypeStruct((16, L, 2 * L), F32), jax.ShapeDtypeStruct((1, 128), F32),
                   jax.ShapeDtypeStruct((1, 1024), F32), jax.ShapeDtypeStruct((1, 256), F32)],
        scratch_shapes=[pltpu.VMEM((L, 256), F32)],
        compiler_params=_params(("arbitrary",)))(sinks, qkv, qkv, qkv, qkv, qkv, d_o, lse, bias)


def _pad_lanes(a, n=128):
    return jnp.pad(a, ((0, 0), (0, n - a.shape[1])))


def _local_step(x, tgt, mod, w_in, P, io):
    md = [[mod[l:l + 1, k * D:(k + 1) * D] for k in range(6)] for l in range(2)]
    G, g = {}, {}

    sh1, sc1, g1, sh2, sc2, g2 = md[0]
    nmw0, nfw0 = P["norm_mix_w"][0:1], P["norm_ffn_w"][0:1]
    h0 = _norm_mod_fwd("norm_mix_0", x, nmw0, sc1, sh1, after=io["start"])
    segs = {"z": w_in[0:1024], "xbc": w_in[1024:2560], "dt": jnp.pad(w_in[2560:2576], ((0, 112), (0, 0))),
            "u": w_in[2576:3600], "v": w_in[3600:4624]}
    proj = {k: _mm(f"in_proj_{k}", [h0], [w], "nt", [F32])[0] for k, w in segs.items()}
    conv_w, conv_b = P["conv_w"][0], P["conv_b"]
    pre, xc = _conv_fwd(proj["xbc"], conv_w, conv_b)
    dtb, alog = _pad_lanes(P["dt_bias"]), _pad_lanes(P["a_log"])
    dskl = jnp.repeat(P["d_skip"], 64, axis=1)
    ya, y_ssd, prev = _ssd_fwd(xc, proj["dt"], proj["z"], dtb, alog, dskl, P["ssm_norm_w"])
    ws = P["gmlp_ws"][0]
    bse = jnp.broadcast_to(P["gmlp_bs"][0][:, :, None], (8, L, 128))
    yb = _gmlp_fwd(proj["u"], proj["v"], P["gmlp_ln_w"], P["gmlp_ln_b"], ws, bse)
    W = dict(io["weights0"]((ya, yb)))
    w_oa, w_ob = W["out_w"][:1024], W["out_w"][1024:]

    def res(y, x, gate):
        return y, x + gate * y
    mix0, x1 = _mm("out_proj_0", [ya, yb], [w_oa, w_ob], "nn", [F32, F32], epi=res, extras=[x], vecs=[g1])
    h0f = _norm_mod_fwd("norm_ffn_0", x1, nfw0, sc2, sh2)
    a0, b0, f0, y0, x2 = _ffn_fwd("0", h0f, W["gate_wt0"], W["up_wt0"], W["down_w0"], x1, g2)

    sh1b, sc1b, g1b, sh2b, sc2b, g2b = md[1]
    nmw1, nfw1 = P["norm_mix_w"][1:2], P["norm_ffn_w"][1:2]
    W.update(io["weights1"](x2))
    h1 = _norm_mod_fwd("norm_mix_1", x2, nmw1, sc1b, sh1b)
    qkv = _mm("qkv_proj", [h1], [W["qkv_wt"]], "nt", [F32], epi=lambda acc, b: acc + b, vecs=[P["qkv_b"]])[0]
    onehot_t = jnp.asarray(_bucket_onehot_t())
    bias = _rel_bias(P["rel_table"].T, onehot_t).reshape(16, L, 2 * L)
    sinks = P["sinks"].reshape(16)
    att, lse = _attn_fwd(qkv, bias, sinks)

    def res_b(y, x, gate, b):
        y = y + b
        return y, x + gate * y
    mix1, x3 = _mm("o_proj", [att], [W["o_w"]], "nn", [F32, F32], epi=res_b, extras=[x2], vecs=[g1b, P["o_b"]])
    h1f = _norm_mod_fwd("norm_ffn_1", x3, nfw1, sc2b, sh2b)
    a1, b1, f1, y1, x4 = _ffn_fwd("1", h1f, W["gate_wt1"], W["up_wt1"], W["down_w1"], x3, g2b)

    dx, sq, g["final_norm_w"] = _loss_head(x4, tgt, P["final_norm_w"])

    dh, dg2b, dwg1, dwu1, dwd1 = _ffn_bwd("1", dx, h1f, a1, b1, f1, y1, W["gate_wt1"], W["up_wt1"],
                                          W["down_w1"], g2b)
    dx, dsh2b, dsc2b, dnfw1 = _norm_mod_bwd("norm_ffn_bwd_1", x3, dh, dx, nfw1, sc2b)
    dmix, dg1b, g["o_b"] = _gate_bwd("mix_gate_bwd_1", dx, mix1, g1b)
    G["o_w"] = _mm_tn("o_dw", att, dmix)
    d_att = _mm("o_dx", [dmix], [W["o_w"]], "nt", [F32])[0]
    dq, dkv, dbias, dsinks, dbq, dbkv = _attn_bwd(qkv, d_att, lse, bias, sinks)
    g["rel_table"] = _rel_bias_bwd(dbias.reshape(16, L * 2 * L), onehot_t).T
    g["sinks"] = dsinks[:, :16]
    g["qkv_b"] = jnp.concatenate([dbq, dbkv], axis=1)
    w_q, w_kv = W["qkv_wt"][:1024], W["qkv_wt"][1024:]
    G["qkv_wt"] = jnp.concatenate([_mm_tn("qkv_dwq", dq, h1), _mm_tn("qkv_dwkv", dkv, h1)], axis=0)
    dh = _mm("qkv_dx", [dq, dkv], [w_q, w_kv], "nn", [F32])[0]
    dx, dsh1b, dsc1b, dnmw1 = _norm_mod_bwd("norm_mix_bwd_1", x2, dh, dx, nmw1, sc1b)
    behind = io["grads1"]({"qkv_wt": G.pop("qkv_wt"), "o_w": G.pop("o_w"), "gate_wt1": dwg1, "up_wt1": dwu1,
                           "down_w1": dwd1})

    dh, dg2, dwg0, dwu0, dwd0 = _ffn_bwd("0", dx, h0f, a0, b0, f0, y0, W["gate_wt0"], W["up_wt0"],
                                         W["down_w0"], g2, after=behind)
    behind = io["grads_ffn0"]({"gate_wt0": dwg0, "up_wt0": dwu0, "down_w0": dwd0})
    dx, dsh2, dsc2, dnfw0 = _norm_mod_bwd("norm_ffn_bwd_0", x1, dh, dx, nfw0, sc2, after=behind)
    dmix, dg1, _ = _gate_bwd("mix_gate_bwd_0", dx, mix0, g1)
    G["out_w"] = jnp.concatenate([_mm_tn("out_dwa", ya, dmix), _mm_tn("out_dwb", yb, dmix)], axis=0)
    dya = _mm("out_dxa", [dmix], [w_oa], "nt", [F32])[0]
    dyb = _mm("out_dxb", [dmix], [w_ob], "nt", [F32])[0]
    du, dv, dws, dbse, g["gmlp_ln_w"], g["gmlp_ln_b"] = _gmlp_bwd(dyb, proj["u"], proj["v"], P["gmlp_ln_w"],
                                                                 P["gmlp_ln_b"], ws, bse)
    g["gmlp_ws"] = dws[None]
    g["gmlp_bs"] = _lane_sum("gmlp_dbs", dbse.reshape(8 * L, 128)).reshape(1, 8, L)
    dz, dxc, ddt, g["ssm_norm_w"], ddsk, dalog, ddtb = _ssd_bwd(dya, y_ssd, proj["z"], xc, proj["dt"], prev,
                                                                dtb, alog, dskl, P["ssm_norm_w"])
    g["d_skip"], g["a_log"], g["dt_bias"] = ddsk[0:1, :16], dalog[:, :16], ddtb[:, :16]
    dxr, dconv_w, g["conv_b"] = _conv_bwd(dxc, pre, proj["xbc"], conv_w)
    g["conv_w"] = dconv_w[None]
    dsegs = {"z": dz, "xbc": dxr, "dt": ddt, "u": du, "v": dv}
    dws_in = {k: _mm_tn(f"in_dw_{k}", d, h0) for k, d in dsegs.items()}
    G["in_wt"] = jnp.concatenate([dws_in["z"], dws_in["xbc"], dws_in["dt"][:16], dws_in["u"], dws_in["v"]], axis=0)
    keys = ["z", "xbc", "dt", "u", "v"]
    dh = _mm("in_dx", [dsegs[k] for k in keys], [segs[k] for k in keys], "nn", [F32])[0]
    dx, dsh1, dsc1, dnmw0 = _norm_mod_bwd("norm_mix_bwd_0", x, dh, dx, nmw0, sc1)

    g["norm_mix_w"] = jnp.concatenate([dnmw0, dnmw1], axis=0)
    g["norm_ffn_w"] = jnp.concatenate([dnfw0, dnfw1], axis=0)
    dmod = jnp.concatenate([jnp.concatenate([dsh1, dsc1, dg1, dsh2, dsc2, dg2], axis=1),
                            jnp.concatenate([dsh1b, dsc1b, dg1b, dsh2b, dsc2b, dg2b], axis=1)], axis=0)
    return sq, dx, dmod, G, g


def _ada_fwd(c_all, ada_w, ada_b):
    n = ada_w.shape[2]
    tn = _col_tile(n, 512)

    def body(c_ref, w_ref, b_ref, o_ref):
        cc = c_ref[...]
        o_ref[...] = lax.dot_general(cc * _sigmoid(cc), w_ref[...], NN, precision=lax.Precision.HIGHEST,
                                     preferred_element_type=F32) + b_ref[...]

    return pl.pallas_call(
        body, name="ada_fwd", grid=(2, n // tn),
        in_specs=[pl.BlockSpec((8, D), lambda l, j: (0, 0)), pl.BlockSpec((None, D, tn), lambda l, j: (l, 0, j)),
                  pl.BlockSpec((None, 1, tn), lambda l, j: (l, 0, j))],
        out_specs=pl.BlockSpec((None, 8, tn), lambda l, j: (l, 0, j)),
        out_shape=jax.ShapeDtypeStruct((2, 8, n), F32), compiler_params=_params(("parallel", "parallel")))(
            c_all, ada_w, ada_b)


def _ada_bwd(c_all, dmod_cols, dmod_all):
    n = dmod_cols.shape[2]
    tn = _col_tile(n, 512)

    def body(c_ref, d_ref, o_ref):
        cc = c_ref[...]
        o_ref[...] = lax.dot_general(cc * _sigmoid(cc), d_ref[...], TN, precision=lax.Precision.HIGHEST,
                                     preferred_element_type=F32)

    dw = pl.pallas_call(
        body, name="ada_dw", grid=(2, n // tn),
        in_specs=[pl.BlockSpec((8, D), lambda l, j: (0, 0)), pl.BlockSpec((None, 8, tn), lambda l, j: (l, 0, j))],
        out_specs=pl.BlockSpec((None, D, tn), lambda l, j: (l, 0, j)),
        out_shape=jax.ShapeDtypeStruct((2, D, n), F32), compiler_params=_params(("parallel", "parallel")))(
            c_all, dmod_cols)

    def sum_body(d_ref, o_ref):
        o_ref[...] = jnp.sum(d_ref[...], axis=0, keepdims=True)

    db = pl.pallas_call(
        sum_body, name="ada_db", grid=(2,),
        in_specs=[pl.BlockSpec((None, 8, 6 * D), lambda l: (l, 0, 0))],
        out_specs=pl.BlockSpec((None, 1, 6 * D), lambda l: (l, 0, 0)),
        out_shape=jax.ShapeDtypeStruct((2, 1, 6 * D), F32), compiler_params=_params(("parallel",)))(dmod_all)
    return dw, db


def _row_tile(rows, cap=512, mult=8):
    best = rows
    for t in range(mult, min(rows, cap) + 1, mult):
        if rows % t == 0:
            best = t
    return best


def _adamw(name, w, g, m, v):
    def fn(w, g, m, v):
        m = ADAM_B1 * m + (1.0 - ADAM_B1) * g
        v = ADAM_B2 * v + (1.0 - ADAM_B2) * (g * g)
        m_hat = m / (1.0 - ADAM_B1 ** ADAM_STEP)
        v_hat = v / (1.0 - ADAM_B2 ** ADAM_STEP)
        return -ADAM_LR * (m_hat / (jnp.sqrt(v_hat) + ADAM_EPS) + ADAM_WD * w), m, v
    cols = w.shape[1]
    return _rowwise(name, fn, [w, g, m, v], [], [(cols, F32)] * 3, tr=_row_tile(w.shape[0]))


def _place():
    return lax.axis_index("x"), lax.axis_index("y"), lax.axis_index("c")


VMEM_SPEC = pl.BlockSpec(memory_space=pltpu.VMEM)


def _allreduce_small(name, buf, after=None):
    rows = buf.shape[0]
    deps = [] if after is None else [after]

    def body(x_ref, *rest):
        o_ref, stage, send_sems, recv_sems = rest[len(deps):]
        x, y, c = _place()
        me = 4 * x + 2 * y + c
        stage[me] = x_ref[...]
        copies = []
        for k in range(1, 8):
            peer = (1 - x if k & 4 else x, 1 - y if k & 2 else y, 1 - c if k & 1 else c)
            cp = pltpu.make_async_remote_copy(src_ref=x_ref, dst_ref=stage.at[me], send_sem=send_sems.at[k - 1],
                                              recv_sem=recv_sems.at[k - 1], device_id=peer, device_id_type=MESH)
            cp.start()
            copies.append(cp)
        for cp in copies:
            cp.wait()
        acc = stage[0]
        for d in range(1, 8):
            acc = acc + stage[d]
        o_ref[...] = acc

    return pl.pallas_call(
        body, name=name, in_specs=[VMEM_SPEC] + [ANY for _ in deps], out_specs=VMEM_SPEC,
        out_shape=jax.ShapeDtypeStruct((rows, 128), F32),
        scratch_shapes=[pltpu.VMEM((8, rows, 128), F32), pltpu.SemaphoreType.DMA((7,)), pltpu.SemaphoreType.DMA((7,))],
        compiler_params=pltpu.CompilerParams(vmem_limit_bytes=_VMEM_LIMIT))(buf, *deps)


OTHER_CHIPS = ((1, 0), (0, 1), (1, 1))


def _allgather_big(wp, after=None):
    rows = wp.shape[0]
    half = rows // 2
    deps = [] if after is None else [after]

    def body(w_ref, *rest):
        o_ref, send_sems, recv_sems = rest[len(deps):]
        x, y, c = _place()
        k = 2 * x + y
        mine = pl.ds(pl.multiple_of(c * half, 8), half)
        first = []
        for j, (fx, fy) in enumerate(OTHER_CHIPS):
            cp = pltpu.make_async_remote_copy(src_ref=w_ref.at[mine], dst_ref=o_ref.at[k, mine],
                                              send_sem=send_sems.at[j], recv_sem=recv_sems.at[j],
                                              device_id=(1 - x if fx else x, 1 - y if fy else y, c),
                                              device_id_type=MESH)
            cp.start()
            first.append(cp)
        for cp in first:
            cp.wait_recv()
        swap = pltpu.make_async_remote_copy(src_ref=o_ref.at[:, mine], dst_ref=o_ref.at[:, mine],
                                            send_sem=send_sems.at[3], recv_sem=recv_sems.at[3],
                                            device_id=(x, y, 1 - c), device_id_type=MESH)
        swap.start()
        swap.wait()
        for cp in first:
            cp.wait_send()

    gathered = pl.pallas_call(
        body, name="allgather_weights", in_specs=[ANY] + [ANY for _ in deps], out_specs=ANY,
        out_shape=jax.ShapeDtypeStruct((4, rows, 1024), wp.dtype),
        scratch_shapes=[pltpu.SemaphoreType.DMA((4,)), pltpu.SemaphoreType.DMA((4,))])(wp, *deps)
    return _with_own_share(gathered, wp)


SIBLING_COLLECTIVE_ID = 6


def _sibling_handshake():
    x, y, c = _place()
    barrier = pltpu.get_barrier_semaphore()
    pl.semaphore_signal(barrier, inc=1, device_id=(x, y, 1 - c), device_id_type=MESH)
    pl.semaphore_wait(barrier, 1)


def _sibling_swap(name, src, halves):
    half = src.shape[-2] // 2
    out_shape = (src.shape[0], half, 1024) if halves else src.shape

    def body(s_ref, o_ref, send_sem, recv_sem):
        x, y, c = _place()
        _sibling_handshake()
        part = s_ref.at[:, pl.ds(pl.multiple_of((1 - c) * half, 8), half)] if halves else s_ref
        cp = pltpu.make_async_remote_copy(src_ref=part, dst_ref=o_ref, send_sem=send_sem, recv_sem=recv_sem,
                                          device_id=(x, y, 1 - c), device_id_type=MESH)
        cp.start()
        cp.wait()

    return pl.pallas_call(
        body, name=name, in_specs=[ANY], out_specs=ANY, out_shape=jax.ShapeDtypeStruct(out_shape, src.dtype),
        scratch_shapes=[pltpu.SemaphoreType.DMA, pltpu.SemaphoreType.DMA],
        compiler_params=pltpu.CompilerParams(collective_id=SIBLING_COLLECTIVE_ID))(src)


HBM = pl.BlockSpec(memory_space=pltpu.HBM)
SEM = pl.BlockSpec(memory_space=pltpu.SEMAPHORE)


def _chip_copies(mode, src_ref, land_ref, send_sems, recv_sems):
    x, y, c = _place()
    k = 2 * x + y
    copies = []
    for j, (fx, fy) in enumerate(OTHER_CHIPS):
        px, py = (1 - x if fx else x), (1 - y if fy else y)
        if mode == "gather":
            half = src_ref.shape[0] // 2
            mine = pl.ds(pl.multiple_of(c * half, 16), half)
            src, dst = src_ref.at[mine], land_ref.at[k, mine]
        else:
            src, dst = src_ref.at[2 * px + py], land_ref.at[k]
        copies.append(pltpu.make_async_remote_copy(src_ref=src, dst_ref=dst, send_sem=send_sems.at[j],
                                                   recv_sem=recv_sems.at[j], device_id=(px, py, c),
                                                   device_id_type=MESH))
    return copies


def _exchange_start(name, collective_id, mode, src, land, after=None):
    deps = [] if after is None else [after]

    def body(s_ref, l_ref, *rest):
        send_sems, recv_sems, s_thru, l_thru, token = rest[len(deps):]
        x, y, c = _place()
        barrier = pltpu.get_barrier_semaphore()
        for fx, fy in OTHER_CHIPS:
            pl.semaphore_signal(barrier, inc=1, device_id=(1 - x if fx else x, 1 - y if fy else y, c),
                                device_id_type=MESH)
        pl.semaphore_wait(barrier, 3)
        for cp in _chip_copies(mode, s_ref, l_ref, send_sems, recv_sems):
            cp.start()
        token[...] = jnp.zeros_like(token)

    return pl.pallas_call(
        body, name=name,
        out_shape=(pltpu.SemaphoreType.DMA((3,)), pltpu.SemaphoreType.DMA((3,)), pltpu.HBM(src.shape, src.dtype),
                   pltpu.HBM(land.shape, land.dtype), jax.ShapeDtypeStruct((8, 128), F32)),
        in_specs=(HBM, HBM) + tuple(ANY for _ in deps), out_specs=(SEM, SEM, HBM, HBM, VMEM_SPEC),
        input_output_aliases={0: 2, 1: 3},
        compiler_params=pltpu.CompilerParams(has_side_effects=pltpu.SideEffectType.DATAFLOW_SIDE_EFFECTING,
                                             collective_id=collective_id))(
            pltpu.with_memory_space_constraint(src, pltpu.HBM), pltpu.with_memory_space_constraint(land, pltpu.HBM),
            *deps)


def _exchange_wait(name, mode, started, after):
    send_sems, recv_sems, s_thru, l_thru, _ = started
    deps = list(after) if isinstance(after, (tuple, list)) else [after]

    def body(s_ref, l_ref, send_sems, recv_sems, *rest):
        for cp in _chip_copies(mode, s_ref, l_ref, send_sems, recv_sems):
            cp.wait_send()
            cp.wait_recv()

    return pl.pallas_call(
        body, name=name, out_shape=(pltpu.HBM(s_thru.shape, s_thru.dtype), pltpu.HBM(l_thru.shape, l_thru.dtype)),
        in_specs=(HBM, HBM, SEM, SEM) + tuple(ANY for _ in deps), out_specs=(HBM, HBM),
        input_output_aliases={0: 0, 1: 1},
        compiler_params=pltpu.CompilerParams(has_side_effects=pltpu.SideEffectType.DATAFLOW_SIDE_EFFECTING))(
            s_thru, l_thru, send_sems, recv_sems, *deps)


def _with_own_share(gathered, share):
    slot = lax.broadcasted_iota(jnp.int32, (4, 1, 1), 0)
    return jnp.where(slot == 2 * lax.axis_index("x") + lax.axis_index("y"), share[None], gathered)


def _allgather_finish(tag, share, land):
    rows = share.shape[0]
    half = rows // 2

    def body(l_ref, o_ref, send_sem, recv_sem):
        x, y, c = _place()
        _sibling_handshake()
        mine = pl.ds(pl.multiple_of(c * half, 16), half)
        swap = pltpu.make_async_remote_copy(src_ref=o_ref.at[:, mine], dst_ref=o_ref.at[:, mine], send_sem=send_sem,
                                            recv_sem=recv_sem, device_id=(x, y, 1 - c), device_id_type=MESH)
        swap.start()
        swap.wait()

    swapped = pl.pallas_call(
        body, name="allgather_finish_" + tag, in_specs=[ANY], out_specs=ANY, input_output_aliases={0: 0},
        out_shape=jax.ShapeDtypeStruct(land.shape, land.dtype),
        scratch_shapes=[pltpu.SemaphoreType.DMA, pltpu.SemaphoreType.DMA],
        compiler_params=pltpu.CompilerParams(collective_id=SIBLING_COLLECTIVE_ID))(land)
    return _with_own_share(swapped, share)


def _pair_sum(tag, g, r1, c):
    rows = g.shape[1]
    half = rows // 2
    th = _row_tile(half, 256, 16)
    nblk = half // th

    def body(c_ref, g_ref, r_ref, o_ref, o2_ref):
        o_ref[...] = (g_ref[...] + r_ref[...]).astype(o_ref.dtype)
        o2_ref[...] = o_ref[...]

    spec = pl.BlockSpec((None, th, 1024), lambda k, i, c_ref: (k, i, 0))
    grid_spec = pltpu.PrefetchScalarGridSpec(
        num_scalar_prefetch=1, grid=(4, nblk),
        in_specs=[pl.BlockSpec((None, th, 1024), lambda k, i, c_ref: (k, c_ref[0] * nblk + i, 0)), spec],
        out_specs=[spec, spec])
    return pl.pallas_call(body, name="grad_pair_sum_" + tag, grid_spec=grid_spec,
                          out_shape=[jax.ShapeDtypeStruct((4, half, 1024), BF16)] * 2,
                          compiler_params=_params(("parallel", "parallel")))(c, g, r1)


def _chip_sum(tag, q, after=None):
    half = q.shape[1]
    th = _row_tile(half, 256, 16)
    deps = [] if after is None else [after]

    def body(a, b, c, d, *rest):
        rest[-1][...] = ((a[...].astype(F32) + b[...].astype(F32)) + c[...].astype(F32)) + d[...].astype(F32)

    specs = [pl.BlockSpec((None, th, 1024), functools.partial(lambda i, k: (k, i, 0), k=k)) for k in range(4)]
    return pl.pallas_call(body, name="grad_chip_sum_" + tag, grid=(half // th,), in_specs=specs + [ANY for _ in deps],
                          out_specs=pl.BlockSpec((th, 1024), lambda i: (i, 0)),
                          out_shape=jax.ShapeDtypeStruct((half, 1024), F32),
                          compiler_params=_params(("parallel",)))(q, q, q, q, *deps)


def _join_halves(tag, f, r, c):
    half = f.shape[0]
    th = _row_tile(half, 256)
    nblk = half // th

    def body(c_ref, f_ref, r_ref, o_ref):
        mine = (pl.program_id(0) == c_ref[0])
        o_ref[...] = jnp.where(mine, f_ref[...], r_ref[...])

    spec = pl.BlockSpec((th, 1024), lambda h, i, c_ref: (i, 0))
    grid_spec = pltpu.PrefetchScalarGridSpec(
        num_scalar_prefetch=1, grid=(2, nblk), in_specs=[spec, spec],
        out_specs=pl.BlockSpec((th, 1024), lambda h, i, c_ref: (h * nblk + i, 0)))
    return pl.pallas_call(body, name="grad_join_halves_" + tag, grid_spec=grid_spec,
                          out_shape=jax.ShapeDtypeStruct((2 * half, 1024), F32),
                          compiler_params=_params(("parallel", "parallel")))(c, f, r)


BIG_ARGS = ("in_w_even", "out_w_even", "qkv_w", "o_w", "ffn_gate_w", "ffn_up_w", "ffn_down_w")
def _ffn_pieces(layer):
    return tuple((f"{n}{layer}", 704, 704) for n in ("gate_wt", "up_wt", "down_w"))


IN_SLAB = (("in_wt", 1156, 1184),)
LAYER0_REST_SLAB = (("out_w", 512, 512),) + _ffn_pieces(0)
LAYER1_SLAB = (("qkv_wt", 320, 320), ("o_w", 256, 256)) + _ffn_pieces(1)
FFN0_SLAB = _ffn_pieces(0)
MIXER0_SLAB = (("in_wt", 1156, 1280), ("out_w", 512, 512))


def _slab(pieces, spec):
    parts = []
    for name, rows, room in spec:
        p = pieces[name]
        parts.append(jnp.pad(p, [(0, 0)] * (p.ndim - 2) + [(0, room - rows), (0, 0)]) if room > rows else p)
    return jnp.concatenate(parts, axis=-2) if len(parts) > 1 else parts[0]


def _unslab(slab, spec):
    out, off = {}, 0
    for name, rows, room in spec:
        out[name] = slab[..., off:off + rows, :]
        off += room
    return out


def _share_pieces(w):
    return {"in_wt": w["in_w_even"][0].T, "out_w": w["out_w_even"][0], "qkv_wt": w["qkv_w"][0].T, "o_w": w["o_w"][0],
            "gate_wt0": w["ffn_gate_w"][0].T, "gate_wt1": w["ffn_gate_w"][1].T,
            "up_wt0": w["ffn_up_w"][0].T, "up_wt1": w["ffn_up_w"][1].T,
            "down_w0": w["ffn_down_w"][0], "down_w1": w["ffn_down_w"][1]}


def _pieces_to_shares(p):
    return {"in_w_even": p["in_wt"].T[None], "out_w_even": p["out_w"][None], "qkv_w": p["qkv_wt"].T[None],
            "o_w": p["o_w"][None], "ffn_gate_w": jnp.stack([p["gate_wt0"].T, p["gate_wt1"].T]),
            "ffn_up_w": jnp.stack([p["up_wt0"].T, p["up_wt1"].T]),
            "ffn_down_w": jnp.stack([p["down_w0"], p["down_w1"]])}


def _whole_from_chips(p):
    return {k: v.reshape(-1, D) for k, v in p.items()}


def _chips_from_full(G, spec):
    return _slab({k: v.reshape(4, -1, D) for k, v in G.items()}, spec)


def _pack_small(parts):
    padded = []
    for p in parts:
        p = p.reshape(-1).astype(F32)
        padded.append(jnp.pad(p, (0, (-p.shape[0]) % 1024)))
    return jnp.concatenate(padded).reshape(-1, 128)


def _unpack_small(slab, shapes):
    flat, out, off = slab.reshape(-1), [], 0
    for shp in shapes:
        size = math.prod(shp)
        out.append(flat[off:off + size].reshape(shp))
        off += size + (-size) % 1024
    return out


SMALL = ("ada_b", "norm_mix_w", "norm_ffn_w", "conv_w", "conv_b", "dt_bias", "a_log", "d_skip", "ssm_norm_w",
         "gmlp_ln_w", "gmlp_ln_b", "gmlp_ws", "gmlp_bs", "qkv_b", "o_b", "sinks", "rel_table", "final_norm_w")
SMALL_SPLIT = {"conv_w": 1536, "qkv_b": 1280, "o_b": 1024}
WEIGHTS = ("ada_w", "ada_b", "norm_mix_w", "norm_ffn_w", "in_w_even", "conv_w", "conv_b", "dt_bias", "a_log", "d_skip",
           "ssm_norm_w", "gmlp_ln_w", "gmlp_ln_b", "gmlp_ws", "gmlp_bs", "out_w_even", "qkv_w", "qkv_b", "o_w", "o_b",
           "sinks", "rel_table", "ffn_gate_w", "ffn_up_w", "ffn_down_w", "final_norm_w")


def kernel(x, c, ada_w, ada_b, norm_mix_w, norm_ffn_w, in_w_even, conv_w, conv_b, dt_bias, a_log, d_skip, ssm_norm_w, gmlp_ln_w, gmlp_ln_b, gmlp_ws, gmlp_bs, out_w_even, qkv_w, qkv_b, o_w, o_b, sinks, rel_table, ffn_gate_w, ffn_up_w, ffn_down_w, final_norm_w, loss_target, m_ada_w, m_ada_b, m_norm_mix_w, m_norm_ffn_w, m_in_w_even, m_conv_w, m_conv_b, m_dt_bias, m_a_log, m_d_skip, m_ssm_norm_w, m_gmlp_ln_w, m_gmlp_ln_b, m_gmlp_ws, m_gmlp_bs, m_out_w_even, m_qkv_w, m_qkv_b, m_o_w, m_o_b, m_sinks, m_rel_table, m_ffn_gate_w, m_ffn_up_w, m_ffn_down_w, m_final_norm_w, v_ada_w, v_ada_b, v_norm_mix_w, v_norm_ffn_w, v_in_w_even, v_conv_w, v_conv_b, v_dt_bias, v_a_log, v_d_skip, v_ssm_norm_w, v_gmlp_ln_w, v_gmlp_ln_b, v_gmlp_ws, v_gmlp_bs, v_out_w_even, v_qkv_w, v_qkv_b, v_o_w, v_o_b, v_sinks, v_rel_table, v_ffn_gate_w, v_ffn_up_w, v_ffn_down_w, v_final_norm_w):
    args = dict(locals())
    w = {n: args[n] for n in WEIGHTS}
    m = {n: args["m_" + n] for n in WEIGHTS}
    v = {n: args["v_" + n] for n in WEIGHTS}
    ax, ay, ac = _place()
    me = 4 * ax + 2 * ay + ac
    chip = 2 * ax + ay
    south = (ac == 0).astype(F32)
    c_arr = jnp.reshape(ac, (1,)).astype(jnp.int32)

    c_all = _allreduce_small("gather_cond", lax.dynamic_update_slice(jnp.zeros((8, D), F32), c, (me, 0)).reshape(64, 128))
    c_all = c_all.reshape(8, D)
    n_ada = ada_w.shape[2]
    mod_cols = _ada_fwd(c_all, ada_w, lax.dynamic_slice(ada_b, (0, chip * n_ada), (2, n_ada)).reshape(2, 1, n_ada))
    pieces = [lax.dynamic_update_slice(jnp.zeros((2, 8, 6 * D), F32), mod_cols, (0, 0, chip * n_ada))]
    split_names = list(SMALL_SPLIT)
    for n in split_names:
        full = SMALL_SPLIT[n]
        local = w[n]
        idx = (0,) * (local.ndim - 1) + (chip * local.shape[-1],)
        pieces.append(lax.dynamic_update_slice(jnp.zeros(local.shape[:-1] + (full,), F32), local, idx))
    shapes = [p.shape for p in pieces]
    mod_slab = _allreduce_small("gather_mod", _pack_small(pieces) * south)
    gathered = _unpack_small(mod_slab, shapes)
    mod = lax.dynamic_slice(gathered[0], (0, me, 0), (2, 1, 6 * D)).reshape(2, 6 * D)
    P = {n: w[n] for n in SMALL if n not in SMALL_SPLIT and n != "ada_b"}
    for n, full in zip(split_names, gathered[1:]):
        P[n] = full
    P["final_norm_w"] = final_norm_w.reshape(1, D)

    cast = {k: p.astype(_MXU) for k, p in _share_pieces(w).items()}
    in_slab = _allgather_big(_slab(cast, IN_SLAB), after=mod_slab)
    w_in = _unslab(in_slab, IN_SLAB)["in_wt"].reshape(4 * 1156, D)

    def start_gather(tag, collective_id, spec, after):
        share = _slab(cast, spec)
        return _exchange_start("allgather_start_" + tag, collective_id, "gather", share,
                               lax.empty((4,) + share.shape, share.dtype), after=after)

    def finish_gather(tag, started, spec, after):
        share, land = _exchange_wait("allgather_wait_" + tag, "gather", started, after)
        return _whole_from_chips(_unslab(_allgather_finish(tag, share, land), spec))

    gather0 = start_gather("0", 1, LAYER0_REST_SLAB, in_slab)
    gather1 = start_gather("1", 2, LAYER1_SLAB, gather0[4])

    def start_reduce(tag, collective_id, G, spec, after=None):
        gp = _chips_from_full(G, spec)
        p, q = _pair_sum(tag, gp, _sibling_swap("grad_pair_exchange_" + tag, gp, True), c_arr)
        return _exchange_start("grad_exchange_start_" + tag, collective_id, "scatter", p, q, after=after)

    def finish_reduce(tag, started, spec, after, behind=None):
        q = _exchange_wait("grad_exchange_wait_" + tag, "scatter", started, after)[1]
        fin = _chip_sum(tag, q, after=behind)
        total = _join_halves(tag, fin, _sibling_swap("grad_final_exchange_" + tag, fin, False), c_arr)
        return _unslab(total, spec)

    reduces = {}

    def grads1(G1):
        reduces["1"] = start_reduce("1", 3, G1, LAYER1_SLAB)
        return reduces["1"][4]

    def grads_ffn0(G):
        reduces["f"] = start_reduce("f", 4, G, FFN0_SLAB)
        return reduces["f"][4]

    io = {"start": gather1[4],
          "weights0": lambda after: finish_gather("0", gather0, LAYER0_REST_SLAB, after),
          "weights1": lambda after: finish_gather("1", gather1, LAYER1_SLAB, after),
          "grads1": grads1, "grads_ffn0": grads_ffn0}
    sq, grad_x, dmod, G0, g = _local_step(x[0], loss_target[0], mod, w_in, P, io)
    loss = lax.psum(0.5 * sq[0, 0] / D, ("x", "y", "c"))

    g["final_norm_w"] = g["final_norm_w"].reshape(D)
    small_names = [n for n in SMALL if n != "ada_b"]
    pieces = [lax.dynamic_update_slice(jnp.zeros((2, 8, 6 * D), F32), dmod.reshape(2, 1, 6 * D), (0, me, 0))]
    pieces += [g[n] for n in small_names]
    shapes = [p.shape for p in pieces]
    small_slab = _allreduce_small("allreduce_small_grads", _pack_small(pieces))
    reduces["m"] = start_reduce("m", 5, G0, MIXER0_SLAB, after=small_slab)
    shares = finish_reduce("1", reduces["1"], LAYER1_SLAB, grad_x, behind=reduces["m"][4])
    shares.update(finish_reduce("f", reduces["f"], FFN0_SLAB, grad_x, behind=reduces["m"][4]))
    reduced = _unpack_small(small_slab, shapes)
    dmod_all = reduced[0]
    grads = dict(zip(small_names, reduced[1:]))
    for n in split_names:
        full = grads[n]
        size = w[n].shape[-1]
        grads[n] = lax.dynamic_slice(full, (0,) * (full.ndim - 1) + (chip * size,), full.shape[:-1] + (size,))
    grads = {n: grads[n].reshape(w[n].shape) for n in small_names}
    dw_ada, db_ada = _ada_bwd(c_all, lax.dynamic_slice(dmod_all, (0, 0, chip * n_ada), (2, 8, n_ada)), dmod_all)
    grads["ada_w"], grads["ada_b"] = dw_ada, db_ada.reshape(2, 6 * D)

    delta, new_m, new_v = {}, {}, {}

    def update(n):
        cols = w[n].shape[-1]
        d_, m_, v_ = _adamw("adamw_" + n, w[n].reshape(-1, cols), grads[n].reshape(-1, cols), m[n].reshape(-1, cols),
                            v[n].reshape(-1, cols))
        delta[n], new_m[n], new_v[n] = d_.reshape(w[n].shape), m_.reshape(w[n].shape), v_.reshape(w[n].shape)

    update("ada_w")
    shapes = [w[n].shape for n in SMALL]
    packed = [_pack_small([t[n] for n in SMALL]) for t in (w, grads, m, v)]
    outs = _adamw("adamw_small", *packed)
    for dst, slab in zip((delta, new_m, new_v), outs):
        for n, t in zip(SMALL, _unpack_small(slab, shapes)):
            dst[n] = t
    shares.update(finish_reduce("m", reduces["m"], MIXER0_SLAB, outs[0]))
    grads.update(_pieces_to_shares(shares))
    for n in BIG_ARGS:
        update(n)
    return (loss, grad_x[None], *[grads[n] for n in WEIGHTS], *[delta[n] for n in WEIGHTS],
            *[new_m[n] for n in WEIGHTS], *[new_v[n] for n in WEIGHTS])
```

```python
import functools
import math

import numpy as np
import jax
import jax.numpy as jnp
from jax import lax
from jax.experimental import pallas as pl
from jax.experimental.pallas import tpu as pltpu

F32 = jnp.float32
BF16 = jnp.bfloat16
_MXU = jnp.bfloat16
_VMEM_LIMIT = 56 * 1024 * 1024
MXU_COLS = 256
D = 1024
L = 128
NSTATE = 128
EPS = 1e-6
NEG_INF = -1e30
FFN = 2816
ADAM_LR, ADAM_B1, ADAM_B2, ADAM_EPS, ADAM_WD, ADAM_STEP = 0.001, 0.9, 0.999, 1e-08, 0.01, 10
MESH = pl.DeviceIdType.MESH
ANY = pl.BlockSpec(memory_space=pl.ANY)

NN = (((1,), (0,)), ((), ()))
NT = (((1,), (1,)), ((), ()))
TN = (((0,), (0,)), ((), ()))


def _dot(a, b, dn=NN):
    return lax.dot_general(a.astype(_MXU), b.astype(_MXU), dn, preferred_element_type=F32)


def _params(sem=None):
    return pltpu.CompilerParams(dimension_semantics=sem, vmem_limit_bytes=_VMEM_LIMIT)


def _sigmoid(x):
    return 1.0 / (1.0 + jnp.exp(-x))


def _softplus(x):
    return jnp.maximum(x, 0.0) + jnp.log(1.0 + jnp.exp(-jnp.abs(x)))


def _gelu(x):
    return 0.5 * x * (1.0 + lax.erf(x * (2.0 ** -0.5)))


def _gelu_grad(x):
    return 0.5 * (1.0 + lax.erf(x * (2.0 ** -0.5))) + x * jnp.exp(-0.5 * x * x) * (1.0 / math.sqrt(2.0 * math.pi))


def _silu_grad(a):
    sg = _sigmoid(a)
    return sg * (1.0 + a * (1.0 - sg))


def _rowwise(name, fn, rows, vecs, out_rows, out_accs=(), tr=512, after=None):
    S = rows[0].shape[0]
    tr = min(tr, S)
    assert S % tr == 0
    nr, nv, no, na = len(rows), len(vecs), len(out_rows), len(out_accs)
    deps = [] if after is None else [after]

    def body(*refs):
        ins, outs = refs[:nr + nv], refs[nr + nv + len(deps):]
        res = fn(*[r[...] for r in ins])
        if not isinstance(res, (tuple, list)):
            res = (res,)
        for k in range(no):
            outs[k][...] = res[k].astype(outs[k].dtype)
        if na:
            @pl.when(pl.program_id(0) == 0)
            def _():
                for k in range(na):
                    outs[no + k][...] = jnp.zeros_like(outs[no + k])
            for k in range(na):
                outs[no + k][...] += res[no + k]

    in_specs = [pl.BlockSpec((tr, a.shape[1]), lambda i: (i, 0)) for a in rows]
    in_specs += [pl.BlockSpec(v.shape, lambda i: (0, 0)) for v in vecs] + [ANY for _ in deps]
    out_specs = [pl.BlockSpec((tr, c), lambda i: (i, 0)) for c, _ in out_rows]
    out_specs += [pl.BlockSpec(s, lambda i: (0, 0)) for s in out_accs]
    out_shape = [jax.ShapeDtypeStruct((S, c), dt) for c, dt in out_rows]
    out_shape += [jax.ShapeDtypeStruct(s, F32) for s in out_accs]
    return pl.pallas_call(body, name=name, grid=(S // tr,), in_specs=in_specs, out_specs=out_specs,
                          out_shape=out_shape, compiler_params=_params(("arbitrary",)))(*rows, *vecs, *deps)


def _col_tile(n, cap):
    if n <= cap or n % 128:
        return n
    best = 128
    for t in range(128, cap + 1, 128):
        if n % t == 0:
            best = t
    return best


def _mm(name, As, Bs, mode, outs, epi=None, groups=None, extras=(), vecs=(), tm=512, tn_cap=1536):
    M = As[0].shape[0]
    N = Bs[0].shape[1] if mode == "nn" else Bs[0].shape[0]
    tm = min(tm, M)
    tn = _col_tile(N, tn_cap)
    assert M % tm == 0 and N % tn == 0
    npair = len(As)
    groups = groups or [0] * npair
    ng = max(groups) + 1
    nx, nv = len(extras), len(vecs)
    dn = NN if mode == "nn" else NT

    def body(*refs):
        a_refs, b_refs = refs[:npair], refs[npair:2 * npair]
        x_refs = refs[2 * npair:2 * npair + nx]
        v_refs = refs[2 * npair + nx:2 * npair + nx + nv]
        o_refs = refs[2 * npair + nx + nv:]
        step = tn if epi is None else min(tn, MXU_COLS)
        for col in range(0, tn, step):
            sl = slice(col, min(col + step, tn))
            accs = [None] * ng
            for k in range(npair):
                b = b_refs[k][:, sl] if mode == "nn" else b_refs[k][sl, :]
                d = _dot(a_refs[k][...], b, dn)
                accs[groups[k]] = d if accs[groups[k]] is None else accs[groups[k]] + d
            args = accs + [x[:, sl] for x in x_refs] + [v[:, sl] for v in v_refs]
            res = epi(*args) if epi is not None else tuple(accs)
            if not isinstance(res, (tuple, list)):
                res = (res,)
            for o, r in zip(o_refs, res):
                o[:, sl] = r.astype(o.dtype)

    in_specs = [pl.BlockSpec((tm, a.shape[1]), lambda i, j: (i, 0)) for a in As]
    if mode == "nn":
        in_specs += [pl.BlockSpec((b.shape[0], tn), lambda i, j: (0, j)) for b in Bs]
    else:
        in_specs += [pl.BlockSpec((tn, b.shape[1]), lambda i, j: (j, 0)) for b in Bs]
    in_specs += [pl.BlockSpec((tm, tn), lambda i, j: (i, j)) for _ in extras]
    in_specs += [pl.BlockSpec((1, tn), lambda i, j: (0, j)) for _ in vecs]
    out_specs = [pl.BlockSpec((tm, tn), lambda i, j: (i, j)) for _ in outs]
    out_shape = [jax.ShapeDtypeStruct((M, N), dt) for dt in outs]
    return pl.pallas_call(body, name=name, grid=(M // tm, N // tn), in_specs=in_specs, out_specs=out_specs,
                          out_shape=out_shape, compiler_params=_params(("parallel", "parallel")))(
                              *As, *Bs, *extras, *vecs)


def _mm_tn(name, A, B, tk=512, t2_cap=1536):
    S, K1 = A.shape
    N2 = B.shape[1]
    tk = min(tk, S)
    t2 = _col_tile(N2, t2_cap)
    assert S % tk == 0 and N2 % t2 == 0

    def body(a_ref, b_ref, o_ref):
        @pl.when(pl.program_id(1) == 0)
        def _():
            o_ref[...] = jnp.zeros_like(o_ref)
        o_ref[...] += _dot(a_ref[...], b_ref[...], TN)

    return pl.pallas_call(
        body, name=name, grid=(N2 // t2, S // tk),
        in_specs=[pl.BlockSpec((tk, K1), lambda j, k: (k, 0)), pl.BlockSpec((tk, t2), lambda j, k: (k, j))],
        out_specs=pl.BlockSpec((K1, t2), lambda j, k: (0, j)),
        out_shape=jax.ShapeDtypeStruct((K1, N2), F32),
        compiler_params=_params(("parallel", "arbitrary")))(A, B)


def _norm_mod_fwd(name, x, nw, sc, sh, after=None):
    def fn(x, nw, sc, sh):
        rstd = lax.rsqrt(jnp.mean(x * x, axis=-1, keepdims=True) + EPS)
        return (x * rstd * nw) * (1.0 + sc) + sh
    return _rowwise(name, fn, [x], [nw, sc, sh], [(D, BF16)], after=after)[0]


def _norm_mod_bwd(name, x, dh, dres, nw, sc, after=None):
    def fn(x, dh, dres, nw, sc):
        rstd = lax.rsqrt(jnp.mean(x * x, axis=-1, keepdims=True) + EPS)
        xh = x * rstd
        dn = dh * (1.0 + sc)
        dxh = dn * nw
        dx = rstd * (dxh - xh * jnp.mean(dxh * xh, axis=-1, keepdims=True))
        return (dres + dx, jnp.sum(dh, axis=0, keepdims=True), jnp.sum(dh * (xh * nw), axis=0, keepdims=True),
                jnp.sum(dn * xh, axis=0, keepdims=True))
    return _rowwise(name, fn, [x, dh, dres], [nw, sc], [(D, F32)], [(1, D)] * 3, after=after)


def _gate_bwd(name, dx, y, g, after=None):
    def fn(dx, y, g):
        dy = dx * g
        return dy, jnp.sum(dx * y, axis=0, keepdims=True), jnp.sum(dy, axis=0, keepdims=True)
    return _rowwise(name, fn, [dx, y], [g], [(D, BF16)], [(1, D)] * 2, after=after)


def _loss_head(x, tgt, fw):
    def fn(x, tgt, fw):
        rstd = lax.rsqrt(jnp.mean(x * x, axis=-1, keepdims=True) + EPS)
        xh = x * rstd
        err = xh * fw - tgt
        dout = err * (1.0 / D)
        dxh = dout * fw
        dx = rstd * (dxh - xh * jnp.mean(dxh * xh, axis=-1, keepdims=True))
        sq = jnp.sum(jnp.sum(err * err, axis=1, keepdims=True), axis=0, keepdims=True)
        return dx, sq, jnp.sum(dout * xh, axis=0, keepdims=True)
    return _rowwise("loss_head", fn, [x, tgt], [fw], [(D, F32)], [(1, 1), (1, D)])


def _ffn_fwd(tag, h, wg, wu, wd, x, g2):
    def act(a, b):
        return a, b, a * _sigmoid(a) * b
    a, b, f = _mm(f"ffn_up_{tag}", [h, h], [wg, wu], "nt", [BF16, BF16, BF16], epi=act, groups=[0, 1], tn_cap=1408,
                  tm=1024)

    def res(y, x, g):
        return y, x + g * y
    y, xo = _mm(f"ffn_down_{tag}", [f], [wd], "nn", [F32, F32], epi=res, extras=[x], vecs=[g2])
    return a, b, f, y, xo


def _ffn_bwd(tag, dx, h, a, b, f, y, wg, wu, wd, g2, after=None):
    dy, dg2, _ = _gate_bwd(f"ffn_gate_bwd_{tag}", dx, y, g2, after=after)

    def act_bwd(df, a, b):
        a, b = a.astype(F32), b.astype(F32)
        sg = _sigmoid(a)
        return df * b * (sg * (1.0 + a * (1.0 - sg))), df * (a * sg)
    da, db = _mm(f"ffn_dact_{tag}", [dy], [wd], "nt", [BF16, BF16], epi=act_bwd, extras=[a, b], tn_cap=1408, tm=1024)
    dwd = _mm_tn(f"ffn_dwd_{tag}", f, dy)
    dwg = _mm_tn(f"ffn_dwg_{tag}", da, h)
    dwu = _mm_tn(f"ffn_dwu_{tag}", db, h)
    dh = _mm(f"ffn_dh_{tag}", [da, db], [wg, wu], "nn", [F32])[0]
    return dh, dg2, dwg, dwu, dwd


def _conv_fwd(xr, w, b, tb=512):
    S, C = xr.shape
    tb = min(tb, S)

    def body(x_ref, halo_ref, w_ref, b_ref, pre_ref, out_ref):
        i = pl.program_id(0)
        halo = jnp.where(i > 0, halo_ref[...], 0.0)
        xe = jnp.concatenate([halo, x_ref[...]], axis=0)
        pre = w_ref[3:4, :] * x_ref[...] + b_ref[...]
        for j in (1, 2, 3):
            pre = pre + w_ref[3 - j:4 - j, :] * pltpu.roll(xe, j, axis=0)[8:, :]
        pre_ref[...] = pre
        out_ref[...] = pre * _sigmoid(pre)

    return pl.pallas_call(
        body, name="conv_fwd", grid=(S // tb,),
        in_specs=[pl.BlockSpec((tb, C), lambda i: (i, 0)),
                  pl.BlockSpec((8, C), lambda i: (jnp.maximum(i * (tb // 8) - 1, 0), 0)),
                  pl.BlockSpec((4, C), lambda i: (0, 0)), pl.BlockSpec((1, C), lambda i: (0, 0))],
        out_specs=[pl.BlockSpec((tb, C), lambda i: (i, 0))] * 2,
        out_shape=[jax.ShapeDtypeStruct((S, C), F32)] * 2,
        compiler_params=_params(("parallel",)))(xr, xr, w, b)


def _conv_bwd(dxc, pre, xr, w, tb=512):
    S, C = xr.shape
    tb = min(tb, S)
    nblk = S // tb

    def body(d_ref, p_ref, dn_ref, pn_ref, x_ref, w_ref, dx_ref, dw_ref, db_ref):
        i = pl.program_id(0)

        @pl.when(i == 0)
        def _():
            dw_ref[...] = jnp.zeros_like(dw_ref)
            db_ref[...] = jnp.zeros_like(db_ref)

        dpre = d_ref[...] * _silu_grad(p_ref[...])
        dnext = jnp.where(i < nblk - 1, dn_ref[...] * _silu_grad(pn_ref[...]), 0.0)
        pe = jnp.concatenate([dpre, dnext], axis=0)
        xx = x_ref[...]
        dx = w_ref[3:4, :] * dpre
        dw_ref[3:4, :] += jnp.sum(dpre * xx, axis=0, keepdims=True)
        for j in (1, 2, 3):
            ahead = pltpu.roll(pe, tb + 8 - j, axis=0)[:tb, :]
            dx = dx + w_ref[3 - j:4 - j, :] * ahead
            dw_ref[3 - j:4 - j, :] += jnp.sum(ahead * xx, axis=0, keepdims=True)
        dx_ref[...] = dx.astype(dx_ref.dtype)
        db_ref[...] += jnp.sum(dpre, axis=0, keepdims=True)

    blk = pl.BlockSpec((tb, C), lambda i: (i, 0))
    nxt = pl.BlockSpec((8, C), lambda i: (jnp.minimum((i + 1) * (tb // 8), S // 8 - 1), 0))
    return pl.pallas_call(
        body, name="conv_bwd", grid=(nblk,),
        in_specs=[blk, blk, nxt, nxt, blk, pl.BlockSpec((4, C), lambda i: (0, 0))],
        out_specs=[blk, pl.BlockSpec((4, C), lambda i: (0, 0)), pl.BlockSpec((1, C), lambda i: (0, 0))],
        out_shape=[jax.ShapeDtypeStruct((S, C), BF16), jax.ShapeDtypeStruct((4, C), F32),
                   jax.ShapeDtypeStruct((1, C), F32)],
        compiler_params=_params(("arbitrary",)))(dxc, pre, dxc, pre, xr, w)


def _iota(shape, dim):
    return lax.broadcasted_iota(jnp.int32, shape, dim)


def _colsel(m, lane, h):
    return jnp.sum(jnp.where(lane == h, m, 0.0), axis=1, keepdims=True)


def _cumsum_rows(v):
    r = _iota(v.shape, 0)
    k = 1
    while k < v.shape[0]:
        v = v + jnp.where(r >= k, pltpu.roll(v, k, axis=0), 0.0)
        k *= 2
    return v


def _suffix_sum_rows(v):
    n = v.shape[0]
    r = _iota(v.shape, 0)
    k = 1
    while k < n:
        v = v + jnp.where(r < n - k, pltpu.roll(v, n - k, axis=0), 0.0)
        k *= 2
    return v


def _ssd_fwd(xc, dtr, z, dtb, alog, dskl, nw):
    S = xc.shape[0]
    nc = S // L

    def body(xc_ref, dtr_ref, z_ref, dtb_ref, alog_ref, dsk_ref, nw_ref, ya_ref, y_ref, prev_ref,
             st_ref, cum_ref, cumT_ref):
        i = pl.program_id(0)

        @pl.when(i == 0)
        def _():
            st_ref[...] = jnp.zeros_like(st_ref)

        lane = _iota((L, 128), 1)
        lane1 = _iota((1, 128), 1)
        lo = lane < 64
        lo1 = lane1 < 64
        tril = _iota((L, L), 0) >= _iota((L, L), 1)
        dt = _softplus(dtr_ref[...] + dtb_ref[...])
        a_neg = -jnp.exp(alog_ref[...])
        cum = _cumsum_rows(dt * a_neg)
        cum_ref[...] = cum
        cumT_ref[...] = cum.T
        last_all = cum_ref[L - 1:L, :]
        prev_t = st_ref[...]
        prev_ref[0] = prev_t
        for g in range(2):
            bg = xc_ref[:, 1024 + g * 128:1152 + g * 128]
            cg = xc_ref[:, 1280 + g * 128:1408 + g * 128]
            gmat = _dot(cg, bg, NT)
            yoff = _dot(cg, prev_t[:, g * 512:(g + 1) * 512])
            bg_t = bg.T
            for jp in range(4):
                j = g * 4 + jp
                sl = slice(j * 128, (j + 1) * 128)
                xp = xc_ref[:, sl]
                cc = [_colsel(cum, lane, 2 * j), _colsel(cum, lane, 2 * j + 1)]
                cum_l = jnp.where(lo, cc[0], cc[1])
                dt_l = jnp.where(lo, _colsel(dt, lane, 2 * j), _colsel(dt, lane, 2 * j + 1))
                last_l = jnp.where(lo1, _colsel(last_all, lane1, 2 * j), _colsel(last_all, lane1, 2 * j + 1))
                xd = xp * dt_l
                ys = []
                for hh in range(2):
                    seg = cc[hh] - cumT_ref[2 * j + hh:2 * j + hh + 1, :]
                    dm = jnp.where(tril, jnp.exp(jnp.where(tril, seg, 0.0)), 0.0)
                    ys.append(_dot(gmat * dm, xd))
                y_ref[:, sl] = (jnp.where(lo, ys[0], ys[1]) + jnp.exp(cum_l) * yoff[:, jp * 128:(jp + 1) * 128]
                                + dsk_ref[:, sl] * xp)
                st_ref[:, sl] = prev_t[:, sl] * jnp.exp(last_l) + _dot(bg_t, xd * jnp.exp(last_l - cum_l))
        for g in range(2):
            sl = slice(g * 512, (g + 1) * 512)
            zz = z_ref[:, sl]
            yg = y_ref[:, sl] * (zz * _sigmoid(zz))
            rstd = lax.rsqrt(jnp.mean(yg * yg, axis=-1, keepdims=True) + EPS)
            ya_ref[:, sl] = (yg * rstd * nw_ref[:, sl]).astype(ya_ref.dtype)

    blk = lambda c: pl.BlockSpec((L, c), lambda i: (i, 0))
    vec = lambda c: pl.BlockSpec((1, c), lambda i: (0, 0))
    return pl.pallas_call(
        body, name="ssd_fwd", grid=(nc,),
        in_specs=[blk(1536), blk(128), blk(1024), vec(128), vec(128), vec(1024), vec(1024)],
        out_specs=[blk(1024), blk(1024), pl.BlockSpec((1, NSTATE, 1024), lambda i: (i, 0, 0))],
        out_shape=[jax.ShapeDtypeStruct((S, 1024), BF16), jax.ShapeDtypeStruct((S, 1024), F32),
                   jax.ShapeDtypeStruct((nc, NSTATE, 1024), F32)],
        scratch_shapes=[pltpu.VMEM((NSTATE, 1024), F32), pltpu.VMEM((L, 128), F32), pltpu.VMEM((L, 128), F32)],
        compiler_params=_params(("arbitrary",)))(xc, dtr, z, dtb, alog, dskl, nw)


def _ssd_bwd(dya, y, z, xc, dtr, prev, dtb, alog, dskl, nw):
    S = xc.shape[0]
    nc = S // L

    def body(dya_ref, y_ref, z_ref, xc_ref, dtr_ref, prev_ref, dtb_ref, alog_ref, dsk_ref, nw_ref,
             dz_ref, dxc_ref, ddtr_ref, dnw_ref, ddsk_ref, dalog_ref, ddtb_ref,
             dst_ref, cum_ref, cumT_ref, dy_ref, dskacc_ref):
        i = pl.program_id(0)

        @pl.when(i == 0)
        def _():
            dst_ref[...] = jnp.zeros_like(dst_ref)
            dskacc_ref[...] = jnp.zeros_like(dskacc_ref)
            dnw_ref[...] = jnp.zeros_like(dnw_ref)
            dalog_ref[...] = jnp.zeros_like(dalog_ref)
            ddtb_ref[...] = jnp.zeros_like(ddtb_ref)

        lane = _iota((L, 128), 1)
        lane1 = _iota((1, 128), 1)
        lo = lane < 64
        lo1 = lane1 < 64
        r2, c2 = _iota((L, L), 0), _iota((L, L), 1)
        tril = r2 >= c2
        triu = r2 <= c2
        is_last = _iota((L, 1), 0) == L - 1

        for g in range(2):
            sl = slice(g * 512, (g + 1) * 512)
            zz = z_ref[:, sl]
            sg = _sigmoid(zz)
            zg = zz * sg
            yv = y_ref[:, sl]
            yg = yv * zg
            rstd = lax.rsqrt(jnp.mean(yg * yg, axis=-1, keepdims=True) + EPS)
            xh = yg * rstd
            d_out = dya_ref[:, sl]
            dnw_ref[:, sl] += jnp.sum(d_out * xh, axis=0, keepdims=True)
            dyn = d_out * nw_ref[:, sl]
            dyg = rstd * (dyn - xh * jnp.mean(dyn * xh, axis=-1, keepdims=True))
            dy_ref[:, sl] = dyg * zg
            dz_ref[:, sl] = (dyg * yv * (sg * (1.0 + zz * (1.0 - sg)))).astype(dz_ref.dtype)

        dtin = dtr_ref[...] + dtb_ref[...]
        dt = _softplus(dtin)
        a_neg = -jnp.exp(alog_ref[...])
        cum = _cumsum_rows(dt * a_neg)
        cum_ref[...] = cum
        cumT_ref[...] = cum.T
        last_all = cum_ref[L - 1:L, :]
        prev_t = prev_ref[0]
        dn_t = dst_ref[...]
        dcum = jnp.zeros((L, 128), F32)
        ddt = jnp.zeros((L, 128), F32)
        for g in range(2):
            gsl = slice(g * 512, (g + 1) * 512)
            bg = xc_ref[:, 1024 + g * 128:1152 + g * 128]
            cg = xc_ref[:, 1280 + g * 128:1408 + g * 128]
            gmat = _dot(cg, bg, NT)
            gmat_t = _dot(bg, cg, NT)
            pg = prev_t[:, gsl]
            zmat = _dot(cg, pg)
            dgm = jnp.zeros((L, L), F32)
            dgm_t = jnp.zeros((L, L), F32)
            db_acc = jnp.zeros((L, NSTATE), F32)
            dz_parts, cd_parts = [], []
            for jp in range(4):
                j = g * 4 + jp
                sl = slice(j * 128, (j + 1) * 128)
                xp = xc_ref[:, sl]
                dyp = dy_ref[:, sl]
                cc = [_colsel(cum, lane, 2 * j), _colsel(cum, lane, 2 * j + 1)]
                lc = [_colsel(last_all, lane1, 2 * j), _colsel(last_all, lane1, 2 * j + 1)]
                cum_l = jnp.where(lo, cc[0], cc[1])
                dt_l = jnp.where(lo, _colsel(dt, lane, 2 * j), _colsel(dt, lane, 2 * j + 1))
                last_l = jnp.where(lo1, lc[0], lc[1])
                e_l = jnp.exp(cum_l)
                dte_l = jnp.exp(last_l - cum_l)
                cd_l = jnp.exp(last_l)
                cd_parts.append(cd_l)
                xd = xp * dt_l
                dskacc_ref[:, sl] += jnp.sum(dyp * xp, axis=0, keepdims=True)
                dxp = dsk_ref[:, sl] * dyp
                t = dyp * (e_l * zmat[:, jp * 128:(jp + 1) * 128])
                dcc = [jnp.sum(jnp.where(lo, t, 0.0), axis=1, keepdims=True),
                       jnp.sum(jnp.where(lo, 0.0, t), axis=1, keepdims=True)]
                dz_parts.append(e_l * dyp)
                dnp_ = dn_t[:, sl]
                t2 = jnp.sum(dnp_ * prev_t[:, sl], axis=0, keepdims=True)
                dcd = [jnp.sum(jnp.where(lo1, t2, 0.0), axis=1, keepdims=True),
                       jnp.sum(jnp.where(lo1, 0.0, t2), axis=1, keepdims=True)]
                wm = _dot(bg, dnp_)
                dxd = wm * dte_l
                t3 = wm * xd
                ddte = [jnp.sum(jnp.where(lo, t3, 0.0), axis=1, keepdims=True),
                        jnp.sum(jnp.where(lo, 0.0, t3), axis=1, keepdims=True)]
                db_acc = db_acc + _dot(xd * dte_l, dnp_, NT)
                for hh in range(2):
                    h = 2 * j + hh
                    half = lo if hh == 0 else jnp.logical_not(lo)
                    row = cumT_ref[h:h + 1, :]
                    dm = jnp.where(tril, jnp.exp(jnp.where(tril, cc[hh] - row, 0.0)), 0.0)
                    dm_t = jnp.where(triu, jnp.exp(jnp.where(triu, row - cc[hh], 0.0)), 0.0)
                    m = gmat * dm
                    m_t = gmat_t * dm_t
                    dym = jnp.where(half, dyp, 0.0)
                    d_m = _dot(dym, xd, NT)
                    d_mt = _dot(xd, dym, NT)
                    dxd = dxd + _dot(m_t, dym)
                    dcc[hh] = dcc[hh] + jnp.sum(d_m * m, axis=1, keepdims=True) - jnp.sum(d_mt * m_t, axis=1, keepdims=True)
                    dgm = dgm + d_m * dm
                    dgm_t = dgm_t + d_mt * dm_t
                    dte_c = jnp.exp(lc[hh] - cc[hh])
                    dcc[hh] = dcc[hh] - ddte[hh] * dte_c
                    endc = dcd[hh] * jnp.exp(lc[hh]) + jnp.sum(ddte[hh] * dte_c, axis=0, keepdims=True)
                    dcc[hh] = dcc[hh] + jnp.where(is_last, endc, 0.0)
                    dcum = jnp.where(lane == h, dcc[hh], dcum)
                dxc_ref[:, sl] = dxp + dxd * dt_l
                t4 = dxd * xp
                ddt = jnp.where(lane == 2 * j, jnp.sum(jnp.where(lo, t4, 0.0), axis=1, keepdims=True), ddt)
                ddt = jnp.where(lane == 2 * j + 1, jnp.sum(jnp.where(lo, 0.0, t4), axis=1, keepdims=True), ddt)
            dzg = jnp.concatenate(dz_parts, axis=1)
            dst_ref[:, gsl] = dn_t[:, gsl] * jnp.concatenate(cd_parts, axis=1) + _dot(cg.T, dzg)
            dxc_ref[:, 1280 + g * 128:1408 + g * 128] = _dot(dgm, bg) + _dot(dzg, pg, NT)
            dxc_ref[:, 1024 + g * 128:1152 + g * 128] = _dot(dgm_t, cg) + db_acc
        dla = _suffix_sum_rows(dcum)
        ddt = ddt + dla * a_neg
        dalog_ref[...] += jnp.sum(dla * dt, axis=0, keepdims=True) * a_neg
        ddtr = jnp.where(lane < 16, ddt * _sigmoid(dtin), 0.0)
        ddtr_ref[...] = ddtr.astype(ddtr_ref.dtype)
        ddtb_ref[...] += jnp.sum(ddtr, axis=0, keepdims=True)

        @pl.when(i == nc - 1)
        def _():
            seg = (_iota((1024, 128), 0) // 64 == _iota((1024, 128), 1)).astype(F32)
            acc8 = jnp.broadcast_to(dskacc_ref[...], (8, 1024))
            ddsk_ref[...] = lax.dot_general(acc8, seg, NN, precision=lax.Precision.HIGHEST,
                                            preferred_element_type=F32)

    rev = lambda c: pl.BlockSpec((L, c), lambda i: (nc - 1 - i, 0))
    vec = lambda c: pl.BlockSpec((1, c), lambda i: (0, 0))
    return pl.pallas_call(
        body, name="ssd_bwd", grid=(nc,),
        in_specs=[rev(1024), rev(1024), rev(1024), rev(1536), rev(128),
                  pl.BlockSpec((1, NSTATE, 1024), lambda i: (nc - 1 - i, 0, 0)),
                  vec(128), vec(128), vec(1024), vec(1024)],
        out_specs=[rev(1024), rev(1536), rev(128), vec(1024), pl.BlockSpec((8, 128), lambda i: (0, 0)),
                   vec(128), vec(128)],
        out_shape=[jax.ShapeDtypeStruct((S, 1024), BF16), jax.ShapeDtypeStruct((S, 1536), F32),
                   jax.ShapeDtypeStruct((S, 128), BF16), jax.ShapeDtypeStruct((1, 1024), F32),
                   jax.ShapeDtypeStruct((8, 128), F32), jax.ShapeDtypeStruct((1, 128), F32),
                   jax.ShapeDtypeStruct((1, 128), F32)],
        scratch_shapes=[pltpu.VMEM((NSTATE, 1024), F32), pltpu.VMEM((L, 128), F32), pltpu.VMEM((L, 128), F32),
                        pltpu.VMEM((L, 1024), F32), pltpu.VMEM((1, 1024), F32)],
        compiler_params=_params(("arbitrary",)))(dya, y, z, xc, dtr, prev, dtb, alog, dskl, nw)


def _layer_norm_parts(vg):
    mu = jnp.mean(vg, axis=-1, keepdims=True)
    vc = vg - mu
    rstd = lax.rsqrt(jnp.mean(vc * vc, axis=-1, keepdims=True) + EPS)
    return vc * rstd, rstd


def _gmlp_fwd(u, v, lnw, lnb, ws, bse, tb=512):
    S = u.shape[0]
    tb = min(tb, S)

    def body(u_ref, v_ref, lnw_ref, lnb_ref, ws_ref, bse_ref, o_ref, vn_ref):
        tril = _iota((L, L), 0) >= _iota((L, L), 1)
        xh, _ = _layer_norm_parts(_gelu(v_ref[...]))
        vn_ref[...] = xh * lnw_ref[...] + lnb_ref[...]
        for g in range(8):
            w = jnp.where(tril, ws_ref[g], 0.0)
            gs = slice(g * 128, (g + 1) * 128)
            for ch in range(tb // L):
                rs = slice(ch * L, (ch + 1) * L)
                sv = _dot(w, vn_ref[rs, gs]) + bse_ref[g]
                o_ref[rs, gs] = (_gelu(u_ref[rs, gs]) * sv).astype(o_ref.dtype)

    blk = pl.BlockSpec((tb, 1024), lambda i: (i, 0))
    vec = pl.BlockSpec((1, 1024), lambda i: (0, 0))
    cube = pl.BlockSpec((8, L, 128), lambda i: (0, 0, 0))
    return pl.pallas_call(
        body, name="gmlp_fwd", grid=(S // tb,), in_specs=[blk, blk, vec, vec, cube, cube], out_specs=blk,
        out_shape=jax.ShapeDtypeStruct((S, 1024), BF16), scratch_shapes=[pltpu.VMEM((tb, 1024), F32)],
        compiler_params=_params(("parallel",)))(u, v, lnw, lnb, ws, bse)


def _gmlp_bwd(dyb, u, v, lnw, lnb, ws, bse, tb=512):
    S = u.shape[0]
    tb = min(tb, S)

    def body(d_ref, u_ref, v_ref, lnw_ref, lnb_ref, ws_ref, bse_ref,
             du_ref, dv_ref, dws_ref, dbse_ref, dlnw_ref, dlnb_ref, vn_ref, dvn_ref):
        @pl.when(pl.program_id(0) == 0)
        def _():
            dws_ref[...] = jnp.zeros_like(dws_ref)
            dbse_ref[...] = jnp.zeros_like(dbse_ref)
            dlnw_ref[...] = jnp.zeros_like(dlnw_ref)
            dlnb_ref[...] = jnp.zeros_like(dlnb_ref)

        tril = _iota((L, L), 0) >= _iota((L, L), 1)
        vv = v_ref[...]
        xh, rstd = _layer_norm_parts(_gelu(vv))
        vn_ref[...] = xh * lnw_ref[...] + lnb_ref[...]
        for g in range(8):
            w = jnp.where(tril, ws_ref[g], 0.0)
            w_t = w.T
            gs = slice(g * 128, (g + 1) * 128)
            dw = jnp.zeros((L, L), F32)
            dbs = jnp.zeros((L, 128), F32)
            for ch in range(tb // L):
                rs = slice(ch * L, (ch + 1) * L)
                vn = vn_ref[rs, gs]
                sv = _dot(w, vn) + bse_ref[g]
                uu = u_ref[rs, gs]
                dd = d_ref[rs, gs]
                du_ref[rs, gs] = (dd * sv * _gelu_grad(uu)).astype(du_ref.dtype)
                dsv = dd * _gelu(uu)
                dw = dw + _dot(dsv, vn, NT)
                dbs = dbs + dsv
                dvn_ref[rs, gs] = _dot(w_t, dsv)
            dws_ref[g] += jnp.where(tril, dw, 0.0)
            dbse_ref[g] += dbs
        dvn = dvn_ref[...]
        dlnw_ref[...] += jnp.sum(dvn * xh, axis=0, keepdims=True)
        dlnb_ref[...] += jnp.sum(dvn, axis=0, keepdims=True)
        dxh = dvn * lnw_ref[...]
        dvg = rstd * (dxh - jnp.mean(dxh, axis=-1, keepdims=True) - xh * jnp.mean(dxh * xh, axis=-1, keepdims=True))
        dv_ref[...] = (dvg * _gelu_grad(vv)).astype(dv_ref.dtype)

    blk = pl.BlockSpec((tb, 1024), lambda i: (i, 0))
    vec = pl.BlockSpec((1, 1024), lambda i: (0, 0))
    cube = pl.BlockSpec((8, L, 128), lambda i: (0, 0, 0))
    return pl.pallas_call(
        body, name="gmlp_bwd", grid=(S // tb,), in_specs=[blk, blk, blk, vec, vec, cube, cube],
        out_specs=[blk, blk, cube, cube, vec, vec],
        out_shape=[jax.ShapeDtypeStruct((S, 1024), BF16), jax.ShapeDtypeStruct((S, 1024), BF16),
                   jax.ShapeDtypeStruct((8, L, 128), F32), jax.ShapeDtypeStruct((8, L, 128), F32),
                   jax.ShapeDtypeStruct((1, 1024), F32), jax.ShapeDtypeStruct((1, 1024), F32)],
        scratch_shapes=[pltpu.VMEM((tb, 1024), F32), pltpu.VMEM((tb, 1024), F32)],
        compiler_params=_params(("arbitrary",)))(dyb, u, v, lnw, lnb, ws, bse)


def _lane_sum(name, a):
    def body(a_ref, o_ref):
        o_ref[...] = jnp.sum(a_ref[...], axis=1, keepdims=True)
    return pl.pallas_call(body, name=name, out_shape=jax.ShapeDtypeStruct((a.shape[0], 1), F32))(a)


def _bucket_onehot_t():
    qi = np.arange(L)[:, None]
    sj = np.arange(2 * L)[None, :]
    dist = np.maximum(qi + L - sj, 0)
    log_ratio = (np.log(np.maximum(dist, 1).astype(np.float32) / np.float32(16)) / np.float32(math.log(128 / 16)))
    large = 16 + (log_ratio.astype(np.float32) * np.float32(16)).astype(np.int32)
    bucket = np.where(dist < 16, dist, np.minimum(large, 31)).reshape(-1)
    return (np.arange(32)[:, None] == bucket[None, :]).astype(np.float32)


def _rel_bias(table_t, onehot_t):
    def body(t_ref, oh_ref, o_ref):
        o_ref[...] = lax.dot_general(t_ref[...], oh_ref[...], NN, precision=lax.Precision.HIGHEST,
                                     preferred_element_type=F32)
    return pl.pallas_call(body, name="rel_bias", out_shape=jax.ShapeDtypeStruct((16, L * 2 * L), F32),
                          compiler_params=_params())(table_t, onehot_t)


def _rel_bias_bwd(dbias, onehot_t):
    def body(d_ref, oh_ref, o_ref):
        o_ref[...] = lax.dot_general(d_ref[...], oh_ref[...], NT, precision=lax.Precision.HIGHEST,
                                     preferred_element_type=F32)
    return pl.pallas_call(body, name="rel_bias_bwd", out_shape=jax.ShapeDtypeStruct((16, 32), F32),
                          compiler_params=_params())(dbias, onehot_t)


def _band(kp, kc, lo):
    kk = jnp.concatenate([kp, kc], axis=0)
    kr = pltpu.roll(kk, 64, axis=1)
    return [jnp.where(lo, kk, kr), jnp.where(lo, kr, kk)]


def _attn_rows(ref, j, lo):
    parts = []
    for t in range(8):
        pair = ref[:, (4 * j + t // 2) * 128:(4 * j + t // 2 + 1) * 128]
        parts.append(jnp.where(lo if t % 2 == 0 else jnp.logical_not(lo), pair, 0.0))
    return jnp.concatenate(parts, axis=0)


def _attn_mask(i, rows):
    qi, sj = _iota((rows, 2 * L), 0) & (L - 1), _iota((rows, 2 * L), 1)
    rel = qi + L - sj
    return (rel >= 0) & (rel < L) & ((sj >= L) | (i > 0))


def _per_head_col(vals):
    return jnp.concatenate([jnp.broadcast_to(v, (L, 1)) for v in vals], axis=0)


SMEM = pl.BlockSpec(memory_space=pltpu.SMEM)


def _attn_fwd(qkv, bias, sinks):
    S = qkv.shape[0]
    nb = S // L
    scale = 64 ** -0.5

    def body(sink_ref, q_ref, kc_ref, vc_ref, kp_ref, vp_ref, bias_ref, o_ref, lse_ref):
        i = pl.program_id(0)
        lane = _iota((L, 128), 1)
        lo = lane < 64
        lo2 = _iota((2 * L, 128), 1) < 64
        mask = _attn_mask(i, L)
        kd = _band(kp_ref[...], kc_ref[...], lo2)
        vd = _band(vp_ref[...], vc_ref[...], lo2)
        lse = jnp.zeros((L, 128), F32)
        for pr in range(8):
            sl = slice(pr * 128, (pr + 1) * 128)
            qp = q_ref[:, sl]
            j = pr // 4
            outs = []
            for hh in range(2):
                h = 2 * pr + hh
                qm = jnp.where(lo if hh == 0 else jnp.logical_not(lo), qp, 0.0)
                lg = jnp.where(mask, _dot(qm, kd[j], NT) * scale + bias_ref[h], NEG_INF)
                s = sink_ref[h]
                m = jnp.maximum(jnp.max(lg, axis=1, keepdims=True), s)
                p = jnp.where(mask, jnp.exp(lg - m), 0.0)
                den = jnp.sum(p, axis=1, keepdims=True) + jnp.exp(s - m)
                outs.append(_dot(p / den, vd[j]))
                lse = jnp.where(lane == h, m + jnp.log(den), lse)
            o_ref[:, sl] = jnp.where(lo, outs[0], outs[1]).astype(o_ref.dtype)
        lse_ref[...] = lse

    prev = lambda col: pl.BlockSpec((L, 128), lambda i: (jnp.maximum(i - 1, 0), col))
    cur = lambda col: pl.BlockSpec((L, 128), lambda i: (i, col))
    return pl.pallas_call(
        body, name="attn_fwd", grid=(nb,),
        in_specs=[SMEM, pl.BlockSpec((L, 1024), lambda i: (i, 0)), cur(8), cur(9), prev(8), prev(9),
                  pl.BlockSpec((16, L, 2 * L), lambda i: (0, 0, 0))],
        out_specs=[pl.BlockSpec((L, 1024), lambda i: (i, 0)), pl.BlockSpec((L, 128), lambda i: (i, 0))],
        out_shape=[jax.ShapeDtypeStruct((S, 1024), BF16), jax.ShapeDtypeStruct((S, 128), F32)],
        compiler_params=_params(("parallel",)))(sinks, qkv, qkv, qkv, qkv, qkv, bias)


def _attn_bwd(qkv, d_o, lse, bias, sinks):
    S = qkv.shape[0]
    nb = S // L
    scale = 64 ** -0.5

    def body(sink_ref, q_ref, kc_ref, vc_ref, kp_ref, vp_ref, do_ref, lse_ref, bias_ref,
             dq_ref, dkv_ref, dbias_ref, dsink_ref, dbq_ref, dbkv_ref, carry_ref):
        i = pl.program_id(0)

        @pl.when(i == 0)
        def _():
            dbias_ref[...] = jnp.zeros_like(dbias_ref)
            dsink_ref[...] = jnp.zeros_like(dsink_ref)
            dbq_ref[...] = jnp.zeros_like(dbq_ref)
            dbkv_ref[...] = jnp.zeros_like(dbkv_ref)
            carry_ref[...] = jnp.zeros_like(carry_ref)

        @pl.when(i < nb)
        def _():
            lane = _iota((L, 128), 1)
            lane1 = _iota((1, 128), 1)
            lo = lane < 64
            lo2 = _iota((2 * L, 128), 1) < 64
            mask = _attn_mask(i, 8 * L)
            kd = _band(kp_ref[...], kc_ref[...], lo2)
            vd = _band(vp_ref[...], vc_ref[...], lo2)
            lse_all = lse_ref[...]
            dsink = jnp.zeros((1, 128), F32)
            tot_k, tot_v = [], []
            for j in range(2):
                q_all = _attn_rows(q_ref, j, lo)
                do_all = _attn_rows(do_ref, j, lo)
                lse_col = _per_head_col([_colsel(lse_all, lane, 8 * j + t) for t in range(8)])
                lg = _dot(q_all, kd[j], NT) * scale + bias_ref[8 * j:8 * j + 8].reshape(8 * L, 2 * L)
                p = jnp.where(mask, jnp.exp(jnp.where(mask, lg, NEG_INF) - lse_col), 0.0)
                dp = _dot(do_all, vd[j], NT)
                delta = jnp.sum(p * dp, axis=1, keepdims=True)
                ds = p * (dp - delta)
                dbias_ref[8 * j:8 * j + 8] += ds.reshape(8, L, 2 * L)
                s = _per_head_col([sink_ref[8 * j + t] for t in range(8)])
                sink_part = -jnp.exp(s - lse_col) * delta
                for t in range(8):
                    dsink = dsink + jnp.where(lane1 == 8 * j + t,
                                              jnp.sum(sink_part[t * L:(t + 1) * L], axis=0, keepdims=True), 0.0)
                dss = ds * scale
                dq_all = _dot(dss, kd[j])
                for t in range(0, 8, 2):
                    sl = slice((4 * j + t // 2) * 128, (4 * j + t // 2 + 1) * 128)
                    dq = jnp.where(lo, dq_all[t * L:(t + 1) * L], dq_all[(t + 1) * L:(t + 2) * L])
                    dq_ref[:, sl] = dq.astype(dq_ref.dtype)
                    dbq_ref[:, sl] += jnp.sum(dq, axis=0, keepdims=True)
                acc_k = _dot(dss, q_all, TN)
                acc_v = _dot(p, do_all, TN)
                tot_k.append(acc_k + pltpu.roll(acc_k, 64, axis=1))
                tot_v.append(acc_v + pltpu.roll(acc_v, 64, axis=1))
            dsink_ref[...] += dsink
            dkv = jnp.concatenate([jnp.where(lo2, tot_k[0], tot_k[1]), jnp.where(lo2, tot_v[0], tot_v[1])], axis=1)
            dbkv_ref[...] += jnp.sum(dkv, axis=0, keepdims=True)
            dkv_ref[...] = (carry_ref[...] + dkv[:L, :]).astype(dkv_ref.dtype)
            carry_ref[...] = dkv[L:, :]

        @pl.when(i == nb)
        def _():
            dkv_ref[...] = carry_ref[...].astype(dkv_ref.dtype)

    c = lambda i: jnp.minimum(i, nb - 1)
    prev = lambda col: pl.BlockSpec((L, 128), lambda i: (jnp.maximum(c(i) - 1, 0), col))
    cur = lambda col: pl.BlockSpec((L, 128), lambda i: (c(i), col))
    row = lambda w: pl.BlockSpec((L, w), lambda i: (c(i), 0))
    cube = pl.BlockSpec((16, L, 2 * L), lambda i: (0, 0, 0))
    vec = lambda w: pl.BlockSpec((1, w), lambda i: (0, 0))
    return pl.pallas_call(
        body, name="attn_bwd", grid=(nb + 1,),
        in_specs=[SMEM, row(1024), cur(8), cur(9), prev(8), prev(9), row(1024), row(128), cube],
        out_specs=[row(1024), pl.BlockSpec((L, 256), lambda i: (jnp.maximum(i - 1, 0), 0)), cube,
                   vec(128), vec(1024), vec(256)],
        out_shape=[jax.ShapeDtypeStruct((S, 1024), BF16), jax.ShapeDtypeStruct((S, 256), BF16),
                   jax.ShapeDtypeStruct((16, L, 2 * L), F32), jax.ShapeDtypeStruct((1, 128), F32),
                   jax.ShapeDtypeStruct((1, 1024), F32), jax.ShapeDtypeStruct((1, 256), F32)],
        scratch_shapes=[pltpu.VMEM((L, 256), F32)],
        compiler_params=_params(("arbitrary",)))(sinks, qkv, qkv, qkv, qkv, qkv, d_o, lse, bias)


def _pad_lanes(a, n=128):
    return jnp.pad(a, ((0, 0), (0, n - a.shape[1])))


def _local_step(x, tgt, mod, w_in, P, io):
    md = [[mod[l:l + 1, k * D:(k + 1) * D] for k in range(6)] for l in range(2)]
    G, g = {}, {}

    sh1, sc1, g1, sh2, sc2, g2 = md[0]
    nmw0, nfw0 = P["norm_mix_w"][0:1], P["norm_ffn_w"][0:1]
    h0 = _norm_mod_fwd("norm_mix_0", x, nmw0, sc1, sh1, after=io["start"])
    segs = {"z": w_in[0:1024], "xbc": w_in[1024:2560], "dt": jnp.pad(w_in[2560:2576], ((0, 112), (0, 0))),
            "u": w_in[2576:3600], "v": w_in[3600:4624]}
    proj = {k: _mm(f"in_proj_{k}", [h0], [w], "nt", [F32])[0] for k, w in segs.items()}
    conv_w, conv_b = P["conv_w"][0], P["conv_b"]
    pre, xc = _conv_fwd(proj["xbc"], conv_w, conv_b)
    dtb, alog = _pad_lanes(P["dt_bias"]), _pad_lanes(P["a_log"])
    dskl = jnp.repeat(P["d_skip"], 64, axis=1)
    ya, y_ssd, prev = _ssd_fwd(xc, proj["dt"], proj["z"], dtb, alog, dskl, P["ssm_norm_w"])
    ws = P["gmlp_ws"][0]
    bse = jnp.broadcast_to(P["gmlp_bs"][0][:, :, None], (8, L, 128))
    yb = _gmlp_fwd(proj["u"], proj["v"], P["gmlp_ln_w"], P["gmlp_ln_b"], ws, bse)
    W = dict(io["weights0"]((ya, yb)))
    w_oa, w_ob = W["out_w"][:1024], W["out_w"][1024:]

    def res(y, x, gate):
        return y, x + gate * y
    mix0, x1 = _mm("out_proj_0", [ya, yb], [w_oa, w_ob], "nn", [F32, F32], epi=res, extras=[x], vecs=[g1])
    h0f = _norm_mod_fwd("norm_ffn_0", x1, nfw0, sc2, sh2)
    a0, b0, f0, y0, x2 = _ffn_fwd("0", h0f, W["gate_wt0"], W["up_wt0"], W["down_w0"], x1, g2)

    sh1b, sc1b, g1b, sh2b, sc2b, g2b = md[1]
    nmw1, nfw1 = P["norm_mix_w"][1:2], P["norm_ffn_w"][1:2]
    W.update(io["weights1"](x2))
    h1 = _norm_mod_fwd("norm_mix_1", x2, nmw1, sc1b, sh1b)
    qkv = _mm("qkv_proj", [h1], [W["qkv_wt"]], "nt", [F32], epi=lambda acc, b: acc + b, vecs=[P["qkv_b"]])[0]
    onehot_t = jnp.asarray(_bucket_onehot_t())
    bias = _rel_bias(P["rel_table"].T, onehot_t).reshape(16, L, 2 * L)
    sinks = P["sinks"].reshape(16)
    att, lse = _attn_fwd(qkv, bias, sinks)

    def res_b(y, x, gate, b):
        y = y + b
        return y, x + gate * y
    mix1, x3 = _mm("o_proj", [att], [W["o_w"]], "nn", [F32, F32], epi=res_b, extras=[x2], vecs=[g1b, P["o_b"]])
    h1f = _norm_mod_fwd("norm_ffn_1", x3, nfw1, sc2b, sh2b)
    a1, b1, f1, y1, x4 = _ffn_fwd("1", h1f, W["gate_wt1"], W["up_wt1"], W["down_w1"], x3, g2b)

    dx, sq, g["final_norm_w"] = _loss_head(x4, tgt, P["final_norm_w"])

    dh, dg2b, dwg1, dwu1, dwd1 = _ffn_bwd("1", dx, h1f, a1, b1, f1, y1, W["gate_wt1"], W["up_wt1"],
                                          W["down_w1"], g2b)
    dx, dsh2b, dsc2b, dnfw1 = _norm_mod_bwd("norm_ffn_bwd_1", x3, dh, dx, nfw1, sc2b)
    dmix, dg1b, g["o_b"] = _gate_bwd("mix_gate_bwd_1", dx, mix1, g1b)
    G["o_w"] = _mm_tn("o_dw", att, dmix)
    d_att = _mm("o_dx", [dmix], [W["o_w"]], "nt", [F32])[0]
    dq, dkv, dbias, dsinks, dbq, dbkv = _attn_bwd(qkv, d_att, lse, bias, sinks)
    g["rel_table"] = _rel_bias_bwd(dbias.reshape(16, L * 2 * L), onehot_t).T
    g["sinks"] = dsinks[:, :16]
    g["qkv_b"] = jnp.concatenate([dbq, dbkv], axis=1)
    w_q, w_kv = W["qkv_wt"][:1024], W["qkv_wt"][1024:]
    G["qkv_wt"] = jnp.concatenate([_mm_tn("qkv_dwq", dq, h1), _mm_tn("qkv_dwkv", dkv, h1)], axis=0)
    dh = _mm("qkv_dx", [dq, dkv], [w_q, w_kv], "nn", [F32])[0]
    dx, dsh1b, dsc1b, dnmw1 = _norm_mod_bwd("norm_mix_bwd_1", x2, dh, dx, nmw1, sc1b)
    behind = io["grads1"]({"qkv_wt": G.pop("qkv_wt"), "o_w": G.pop("o_w"), "gate_wt1": dwg1, "up_wt1": dwu1,
                           "down_w1": dwd1})

    dh, dg2, dwg0, dwu0, dwd0 = _ffn_bwd("0", dx, h0f, a0, b0, f0, y0, W["gate_wt0"], W["up_wt0"],
                                         W["down_w0"], g2, after=behind)
    behind = io["grads_ffn0"]({"gate_wt0": dwg0, "up_wt0": dwu0, "down_w0": dwd0})
    dx, dsh2, dsc2, dnfw0 = _norm_mod_bwd("norm_ffn_bwd_0", x1, dh, dx, nfw0, sc2, after=behind)
    dmix, dg1, _ = _gate_bwd("mix_gate_bwd_0", dx, mix0, g1)
    G["out_w"] = jnp.concatenate([_mm_tn("out_dwa", ya, dmix), _mm_tn("out_dwb", yb, dmix)], axis=0)
    dya = _mm("out_dxa", [dmix], [w_oa], "nt", [F32])[0]
    dyb = _mm("out_dxb", [dmix], [w_ob], "nt", [F32])[0]
    du, dv, dws, dbse, g["gmlp_ln_w"], g["gmlp_ln_b"] = _gmlp_bwd(dyb, proj["u"], proj["v"], P["gmlp_ln_w"],
                                                                 P["gmlp_ln_b"], ws, bse)
    g["gmlp_ws"] = dws[None]
    g["gmlp_bs"] = _lane_sum("gmlp_dbs", dbse.reshape(8 * L, 128)).reshape(1, 8, L)
    dz, dxc, ddt, g["ssm_norm_w"], ddsk, dalog, ddtb = _ssd_bwd(dya, y_ssd, proj["z"], xc, proj["dt"], prev,
                                                                dtb, alog, dskl, P["ssm_norm_w"])
    g["d_skip"], g["a_log"], g["dt_bias"] = ddsk[0:1, :16], dalog[:, :16], ddtb[:, :16]
    dxr, dconv_w, g["conv_b"] = _conv_bwd(dxc, pre, proj["xbc"], conv_w)
    g["conv_w"] = dconv_w[None]
    dsegs = {"z": dz, "xbc": dxr, "dt": ddt, "u": du, "v": dv}
    dws_in = {k: _mm_tn(f"in_dw_{k}", d, h0) for k, d in dsegs.items()}
    G["in_wt"] = jnp.concatenate([dws_in["z"], dws_in["xbc"], dws_in["dt"][:16], dws_in["u"], dws_in["v"]], axis=0)
    keys = ["z", "xbc", "dt", "u", "v"]
    dh = _mm("in_dx", [dsegs[k] for k in keys], [segs[k] for k in keys], "nn", [F32])[0]
    dx, dsh1, dsc1, dnmw0 = _norm_mod_bwd("norm_mix_bwd_0", x, dh, dx, nmw0, sc1)

    g["norm_mix_w"] = jnp.concatenate([dnmw0, dnmw1], axis=0)
    g["norm_ffn_w"] = jnp.concatenate([dnfw0, dnfw1], axis=0)
    dmod = jnp.concatenate([jnp.concatenate([dsh1, dsc1, dg1, dsh2, dsc2, dg2], axis=1),
                            jnp.concatenate([dsh1b, dsc1b, dg1b, dsh2b, dsc2b, dg2b], axis=1)], axis=0)
    return sq, dx, dmod, G, g


def _ada_fwd(c_all, ada_w, ada_b):
    n = ada_w.shape[2]
    tn = _col_tile(n, 512)

    def body(c_ref, w_ref, b_ref, o_ref):
        cc = c_ref[...]
        o_ref[...] = lax.dot_general(cc * _sigmoid(cc), w_ref[...], NN, precision=lax.Precision.HIGHEST,
                                     preferred_element_type=F32) + b_ref[...]

    return pl.pallas_call(
        body, name="ada_fwd", grid=(2, n // tn),
        in_specs=[pl.BlockSpec((8, D), lambda l, j: (0, 0)), pl.BlockSpec((None, D, tn), lambda l, j: (l, 0, j)),
                  pl.BlockSpec((None, 1, tn), lambda l, j: (l, 0, j))],
        out_specs=pl.BlockSpec((None, 8, tn), lambda l, j: (l, 0, j)),
        out_shape=jax.ShapeDtypeStruct((2, 8, n), F32), compiler_params=_params(("parallel", "parallel")))(
            c_all, ada_w, ada_b)


def _ada_bwd(c_all, dmod_cols, dmod_all):
    n = dmod_cols.shape[2]
    tn = _col_tile(n, 512)

    def body(c_ref, d_ref, o_ref):
        cc = c_ref[...]
        o_ref[...] = lax.dot_general(cc * _sigmoid(cc), d_ref[...], TN, precision=lax.Precision.HIGHEST,
                                     preferred_element_type=F32)

    dw = pl.pallas_call(
        body, name="ada_dw", grid=(2, n // tn),
        in_specs=[pl.BlockSpec((8, D), lambda l, j: (0, 0)), pl.BlockSpec((None, 8, tn), lambda l, j: (l, 0, j))],
        out_specs=pl.BlockSpec((None, D, tn), lambda l, j: (l, 0, j)),
        out_shape=jax.ShapeDtypeStruct((2, D, n), F32), compiler_params=_params(("parallel", "parallel")))(
            c_all, dmod_cols)

    def sum_body(d_ref, o_ref):
        o_ref[...] = jnp.sum(d_ref[...], axis=0, keepdims=True)

    db = pl.pallas_call(
        sum_body, name="ada_db", grid=(2,),
        in_specs=[pl.BlockSpec((None, 8, 6 * D), lambda l: (l, 0, 0))],
        out_specs=pl.BlockSpec((None, 1, 6 * D), lambda l: (l, 0, 0)),
        out_shape=jax.ShapeDtypeStruct((2, 1, 6 * D), F32), compiler_params=_params(("parallel",)))(dmod_all)
    return dw, db


def _row_tile(rows, cap=512, mult=8):
    best = rows
    for t in range(mult, min(rows, cap) + 1, mult):
        if rows % t == 0:
            best = t
    return best


def _adamw(name, w, g, m, v):
    def fn(w, g, m, v):
        m = ADAM_B1 * m + (1.0 - ADAM_B1) * g
        v = ADAM_B2 * v + (1.0 - ADAM_B2) * (g * g)
        m_hat = m / (1.0 - ADAM_B1 ** ADAM_STEP)
        v_hat = v / (1.0 - ADAM_B2 ** ADAM_STEP)
        return -ADAM_LR * (m_hat / (jnp.sqrt(v_hat) + ADAM_EPS) + ADAM_WD * w), m, v
    cols = w.shape[1]
    return _rowwise(name, fn, [w, g, m, v], [], [(cols, F32)] * 3, tr=_row_tile(w.shape[0]))


def _place():
    return lax.axis_index("x"), lax.axis_index("y"), lax.axis_index("c")


VMEM_SPEC = pl.BlockSpec(memory_space=pltpu.VMEM)


def _allreduce_small(name, buf, after=None):
    rows = buf.shape[0]
    deps = [] if after is None else [after]

    def body(x_ref, *rest):
        o_ref, stage, send_sems, recv_sems = rest[len(deps):]
        x, y, c = _place()
        me = 4 * x + 2 * y + c
        stage[me] = x_ref[...]
        copies = []
        for k in range(1, 8):
            peer = (1 - x if k & 4 else x, 1 - y if k & 2 else y, 1 - c if k & 1 else c)
            cp = pltpu.make_async_remote_copy(src_ref=x_ref, dst_ref=stage.at[me], send_sem=send_sems.at[k - 1],
                                              recv_sem=recv_sems.at[k - 1], device_id=peer, device_id_type=MESH)
            cp.start()
            copies.append(cp)
        for cp in copies:
            cp.wait()
        acc = stage[0]
        for d in range(1, 8):
            acc = acc + stage[d]
        o_ref[...] = acc

    return pl.pallas_call(
        body, name=name, in_specs=[VMEM_SPEC] + [ANY for _ in deps], out_specs=VMEM_SPEC,
        out_shape=jax.ShapeDtypeStruct((rows, 128), F32),
        scratch_shapes=[pltpu.VMEM((8, rows, 128), F32), pltpu.SemaphoreType.DMA((7,)), pltpu.SemaphoreType.DMA((7,))],
        compiler_params=pltpu.CompilerParams(vmem_limit_bytes=_VMEM_LIMIT))(buf, *deps)


OTHER_CHIPS = ((1, 0), (0, 1), (1, 1))


def _allgather_big(wp, after=None):
    rows = wp.shape[0]
    half = rows // 2
    deps = [] if after is None else [after]

    def body(w_ref, *rest):
        o_ref, send_sems, recv_sems = rest[len(deps):]
        x, y, c = _place()
        k = 2 * x + y
        mine = pl.ds(pl.multiple_of(c * half, 8), half)
        first = []
        for j, (fx, fy) in enumerate(OTHER_CHIPS):
            cp = pltpu.make_async_remote_copy(src_ref=w_ref.at[mine], dst_ref=o_ref.at[k, mine],
                                              send_sem=send_sems.at[j], recv_sem=recv_sems.at[j],
                                              device_id=(1 - x if fx else x, 1 - y if fy else y, c),
                                              device_id_type=MESH)
            cp.start()
            first.append(cp)
        for cp in first:
            cp.wait_recv()
        swap = pltpu.make_async_remote_copy(src_ref=o_ref.at[:, mine], dst_ref=o_ref.at[:, mine],
                                            send_sem=send_sems.at[3], recv_sem=recv_sems.at[3],
                                            device_id=(x, y, 1 - c), device_id_type=MESH)
        swap.start()
        swap.wait()
        for cp in first:
            cp.wait_send()

    return pl.pallas_call(
        body, name="allgather_weights", in_specs=[ANY] + [ANY for _ in deps], out_specs=ANY,
        out_shape=jax.ShapeDtypeStruct((4, rows, 1024), wp.dtype),
        scratch_shapes=[pltpu.SemaphoreType.DMA((4,)), pltpu.SemaphoreType.DMA((4,))])(wp, *deps)


SIBLING_COLLECTIVE_ID = 6


def _sibling_handshake():
    x, y, c = _place()
    barrier = pltpu.get_barrier_semaphore()
    pl.semaphore_signal(barrier, inc=1, device_id=(x, y, 1 - c), device_id_type=MESH)
    pl.semaphore_wait(barrier, 1)


def _sibling_swap(name, src, halves):
    half = src.shape[-2] // 2
    out_shape = (src.shape[0], half, 1024) if halves else src.shape

    def body(s_ref, o_ref, send_sem, recv_sem):
        x, y, c = _place()
        _sibling_handshake()
        part = s_ref.at[:, pl.ds(pl.multiple_of((1 - c) * half, 8), half)] if halves else s_ref
        cp = pltpu.make_async_remote_copy(src_ref=part, dst_ref=o_ref, send_sem=send_sem, recv_sem=recv_sem,
                                          device_id=(x, y, 1 - c), device_id_type=MESH)
        cp.start()
        cp.wait()

    return pl.pallas_call(
        body, name=name, in_specs=[ANY], out_specs=ANY, out_shape=jax.ShapeDtypeStruct(out_shape, src.dtype),
        scratch_shapes=[pltpu.SemaphoreType.DMA, pltpu.SemaphoreType.DMA],
        compiler_params=pltpu.CompilerParams(collective_id=SIBLING_COLLECTIVE_ID))(src)


HBM = pl.BlockSpec(memory_space=pltpu.HBM)
SEM = pl.BlockSpec(memory_space=pltpu.SEMAPHORE)


def _chip_copies(mode, src_ref, land_ref, send_sems, recv_sems):
    x, y, c = _place()
    k = 2 * x + y
    copies = []
    for j, (fx, fy) in enumerate(OTHER_CHIPS):
        px, py = (1 - x if fx else x), (1 - y if fy else y)
        if mode == "gather":
            half = src_ref.shape[0] // 2
            mine = pl.ds(pl.multiple_of(c * half, 16), half)
            src, dst = src_ref.at[mine], land_ref.at[k, mine]
        else:
            src, dst = src_ref.at[2 * px + py], land_ref.at[k]
        copies.append(pltpu.make_async_remote_copy(src_ref=src, dst_ref=dst, send_sem=send_sems.at[j],
                                                   recv_sem=recv_sems.at[j], device_id=(px, py, c),
                                                   device_id_type=MESH))
    return copies


def _exchange_start(name, collective_id, mode, src, land, after=None):
    deps = [] if after is None else [after]

    def body(s_ref, l_ref, *rest):
        send_sems, recv_sems, s_thru, l_thru, token = rest[len(deps):]
        x, y, c = _place()
        barrier = pltpu.get_barrier_semaphore()
        for fx, fy in OTHER_CHIPS:
            pl.semaphore_signal(barrier, inc=1, device_id=(1 - x if fx else x, 1 - y if fy else y, c),
                                device_id_type=MESH)
        pl.semaphore_wait(barrier, 3)
        for cp in _chip_copies(mode, s_ref, l_ref, send_sems, recv_sems):
            cp.start()
        token[...] = jnp.zeros_like(token)

    return pl.pallas_call(
        body, name=name,
        out_shape=(pltpu.SemaphoreType.DMA((3,)), pltpu.SemaphoreType.DMA((3,)), pltpu.HBM(src.shape, src.dtype),
                   pltpu.HBM(land.shape, land.dtype), jax.ShapeDtypeStruct((8, 128), F32)),
        in_specs=(HBM, HBM) + tuple(ANY for _ in deps), out_specs=(SEM, SEM, HBM, HBM, VMEM_SPEC),
        input_output_aliases={0: 2, 1: 3},
        compiler_params=pltpu.CompilerParams(has_side_effects=pltpu.SideEffectType.DATAFLOW_SIDE_EFFECTING,
                                             collective_id=collective_id))(
            pltpu.with_memory_space_constraint(src, pltpu.HBM), pltpu.with_memory_space_constraint(land, pltpu.HBM),
            *deps)


def _exchange_wait(name, mode, started, after):
    send_sems, recv_sems, s_thru, l_thru, _ = started
    deps = list(after) if isinstance(after, (tuple, list)) else [after]

    def body(s_ref, l_ref, send_sems, recv_sems, *rest):
        for cp in _chip_copies(mode, s_ref, l_ref, send_sems, recv_sems):
            cp.wait_send()
            cp.wait_recv()

    return pl.pallas_call(
        body, name=name, out_shape=(pltpu.HBM(s_thru.shape, s_thru.dtype), pltpu.HBM(l_thru.shape, l_thru.dtype)),
        in_specs=(HBM, HBM, SEM, SEM) + tuple(ANY for _ in deps), out_specs=(HBM, HBM),
        input_output_aliases={0: 0, 1: 1},
        compiler_params=pltpu.CompilerParams(has_side_effects=pltpu.SideEffectType.DATAFLOW_SIDE_EFFECTING))(
            s_thru, l_thru, send_sems, recv_sems, *deps)


def _allgather_finish(tag, land):
    half = land.shape[1] // 2

    def body(l_ref, o_ref, send_sem, recv_sem):
        x, y, c = _place()
        _sibling_handshake()
        mine = pl.ds(pl.multiple_of(c * half, 16), half)
        swap = pltpu.make_async_remote_copy(src_ref=o_ref.at[:, mine], dst_ref=o_ref.at[:, mine], send_sem=send_sem,
                                            recv_sem=recv_sem, device_id=(x, y, 1 - c), device_id_type=MESH)
        swap.start()
        swap.wait()

    return pl.pallas_call(
        body, name="allgather_finish_" + tag, in_specs=[ANY], out_specs=ANY, input_output_aliases={0: 0},
        out_shape=jax.ShapeDtypeStruct(land.shape, land.dtype),
        scratch_shapes=[pltpu.SemaphoreType.DMA, pltpu.SemaphoreType.DMA],
        compiler_params=pltpu.CompilerParams(collective_id=SIBLING_COLLECTIVE_ID))(land)


def _pair_sum(tag, g, r1, c):
    rows = g.shape[1]
    half = rows // 2
    th = _row_tile(half, 256, 16)
    nblk = half // th

    def body(c_ref, g_ref, r_ref, o_ref, o2_ref):
        o_ref[...] = (g_ref[...] + r_ref[...]).astype(o_ref.dtype)
        o2_ref[...] = o_ref[...]

    spec = pl.BlockSpec((None, th, 1024), lambda k, i, c_ref: (k, i, 0))
    grid_spec = pltpu.PrefetchScalarGridSpec(
        num_scalar_prefetch=1, grid=(4, nblk),
        in_specs=[pl.BlockSpec((None, th, 1024), lambda k, i, c_ref: (k, c_ref[0] * nblk + i, 0)), spec],
        out_specs=[spec, spec])
    return pl.pallas_call(body, name="grad_pair_sum_" + tag, grid_spec=grid_spec,
                          out_shape=[jax.ShapeDtypeStruct((4, half, 1024), BF16)] * 2,
                          compiler_params=_params(("parallel", "parallel")))(c, g, r1)


def _chip_sum(tag, q, after=None):
    half = q.shape[1]
    th = _row_tile(half, 256, 16)
    deps = [] if after is None else [after]

    def body(a, b, c, d, *rest):
        rest[-1][...] = ((a[...].astype(F32) + b[...].astype(F32)) + c[...].astype(F32)) + d[...].astype(F32)

    specs = [pl.BlockSpec((None, th, 1024), functools.partial(lambda i, k: (k, i, 0), k=k)) for k in range(4)]
    return pl.pallas_call(body, name="grad_chip_sum_" + tag, grid=(half // th,), in_specs=specs + [ANY for _ in deps],
                          out_specs=pl.BlockSpec((th, 1024), lambda i: (i, 0)),
                          out_shape=jax.ShapeDtypeStruct((half, 1024), F32),
                          compiler_params=_params(("parallel",)))(q, q, q, q, *deps)


def _join_halves(tag, f, r, c):
    half = f.shape[0]
    th = _row_tile(half, 256)
    nblk = half // th

    def body(c_ref, f_ref, r_ref, o_ref):
        mine = (pl.program_id(0) == c_ref[0])
        o_ref[...] = jnp.where(mine, f_ref[...], r_ref[...])

    spec = pl.BlockSpec((th, 1024), lambda h, i, c_ref: (i, 0))
    grid_spec = pltpu.PrefetchScalarGridSpec(
        num_scalar_prefetch=1, grid=(2, nblk), in_specs=[spec, spec],
        out_specs=pl.BlockSpec((th, 1024), lambda h, i, c_ref: (h * nblk + i, 0)))
    return pl.pallas_call(body, name="grad_join_halves_" + tag, grid_spec=grid_spec,
                          out_shape=jax.ShapeDtypeStruct((2 * half, 1024), F32),
                          compiler_params=_params(("parallel", "parallel")))(c, f, r)


BIG_ARGS = ("in_w_even", "out_w_even", "qkv_w", "o_w", "ffn_gate_w", "ffn_up_w", "ffn_down_w")
def _ffn_pieces(layer):
    return tuple((f"{n}{layer}", 704, 704) for n in ("gate_wt", "up_wt", "down_w"))


IN_SLAB = (("in_wt", 1156, 1184),)
LAYER0_REST_SLAB = (("out_w", 512, 512),) + _ffn_pieces(0)
LAYER1_SLAB = (("qkv_wt", 320, 320), ("o_w", 256, 256)) + _ffn_pieces(1)
FFN0_SLAB = _ffn_pieces(0)
MIXER0_SLAB = (("in_wt", 1156, 1280), ("out_w", 512, 512))


def _slab(pieces, spec):
    parts = []
    for name, rows, room in spec:
        p = pieces[name]
        parts.append(jnp.pad(p, [(0, 0)] * (p.ndim - 2) + [(0, room - rows), (0, 0)]) if room > rows else p)
    return jnp.concatenate(parts, axis=-2) if len(parts) > 1 else parts[0]


def _unslab(slab, spec):
    out, off = {}, 0
    for name, rows, room in spec:
        out[name] = slab[..., off:off + rows, :]
        off += room
    return out


def _share_pieces(w):
    return {"in_wt": w["in_w_even"][0].T, "out_w": w["out_w_even"][0], "qkv_wt": w["qkv_w"][0].T, "o_w": w["o_w"][0],
            "gate_wt0": w["ffn_gate_w"][0].T, "gate_wt1": w["ffn_gate_w"][1].T,
            "up_wt0": w["ffn_up_w"][0].T, "up_wt1": w["ffn_up_w"][1].T,
            "down_w0": w["ffn_down_w"][0], "down_w1": w["ffn_down_w"][1]}


def _pieces_to_shares(p):
    return {"in_w_even": p["in_wt"].T[None], "out_w_even": p["out_w"][None], "qkv_w": p["qkv_wt"].T[None],
            "o_w": p["o_w"][None], "ffn_gate_w": jnp.stack([p["gate_wt0"].T, p["gate_wt1"].T]),
            "ffn_up_w": jnp.stack([p["up_wt0"].T, p["up_wt1"].T]),
            "ffn_down_w": jnp.stack([p["down_w0"], p["down_w1"]])}


def _whole_from_chips(p):
    return {k: v.reshape(-1, D) for k, v in p.items()}


def _chips_from_full(G, spec):
    return _slab({k: v.reshape(4, -1, D) for k, v in G.items()}, spec)


def _pack_small(parts):
    padded = []
    for p in parts:
        p = p.reshape(-1).astype(F32)
        padded.append(jnp.pad(p, (0, (-p.shape[0]) % 1024)))
    return jnp.concatenate(padded).reshape(-1, 128)


def _unpack_small(slab, shapes):
    flat, out, off = slab.reshape(-1), [], 0
    for shp in shapes:
        size = math.prod(shp)
        out.append(flat[off:off + size].reshape(shp))
        off += size + (-size) % 1024
    return out


SMALL = ("ada_b", "norm_mix_w", "norm_ffn_w", "conv_w", "conv_b", "dt_bias", "a_log", "d_skip", "ssm_norm_w",
         "gmlp_ln_w", "gmlp_ln_b", "gmlp_ws", "gmlp_bs", "qkv_b", "o_b", "sinks", "rel_table", "final_norm_w")
SMALL_SPLIT = {"conv_w": 1536, "qkv_b": 1280, "o_b": 1024}
WEIGHTS = ("ada_w", "ada_b", "norm_mix_w", "norm_ffn_w", "in_w_even", "conv_w", "conv_b", "dt_bias", "a_log", "d_skip",
           "ssm_norm_w", "gmlp_ln_w", "gmlp_ln_b", "gmlp_ws", "gmlp_bs", "out_w_even", "qkv_w", "qkv_b", "o_w", "o_b",
           "sinks", "rel_table", "ffn_gate_w", "ffn_up_w", "ffn_down_w", "final_norm_w")


def kernel(x, c, ada_w, ada_b, norm_mix_w, norm_ffn_w, in_w_even, conv_w, conv_b, dt_bias, a_log, d_skip, ssm_norm_w, gmlp_ln_w, gmlp_ln_b, gmlp_ws, gmlp_bs, out_w_even, qkv_w, qkv_b, o_w, o_b, sinks, rel_table, ffn_gate_w, ffn_up_w, ffn_down_w, final_norm_w, loss_target, m_ada_w, m_ada_b, m_norm_mix_w, m_norm_ffn_w, m_in_w_even, m_conv_w, m_conv_b, m_dt_bias, m_a_log, m_d_skip, m_ssm_norm_w, m_gmlp_ln_w, m_gmlp_ln_b, m_gmlp_ws, m_gmlp_bs, m_out_w_even, m_qkv_w, m_qkv_b, m_o_w, m_o_b, m_sinks, m_rel_table, m_ffn_gate_w, m_ffn_up_w, m_ffn_down_w, m_final_norm_w, v_ada_w, v_ada_b, v_norm_mix_w, v_norm_ffn_w, v_in_w_even, v_conv_w, v_conv_b, v_dt_bias, v_a_log, v_d_skip, v_ssm_norm_w, v_gmlp_ln_w, v_gmlp_ln_b, v_gmlp_ws, v_gmlp_bs, v_out_w_even, v_qkv_w, v_qkv_b, v_o_w, v_o_b, v_sinks, v_rel_table, v_ffn_gate_w, v_ffn_up_w, v_ffn_down_w, v_final_norm_w):
    args = dict(locals())
    w = {n: args[n] for n in WEIGHTS}
    m = {n: args["m_" + n] for n in WEIGHTS}
    v = {n: args["v_" + n] for n in WEIGHTS}
    ax, ay, ac = _place()
    me = 4 * ax + 2 * ay + ac
    chip = 2 * ax + ay
    south = (ac == 0).astype(F32)
    c_arr = jnp.reshape(ac, (1,)).astype(jnp.int32)

    c_all = _allreduce_small("gather_cond", lax.dynamic_update_slice(jnp.zeros((8, D), F32), c, (me, 0)).reshape(64, 128))
    c_all = c_all.reshape(8, D)
    n_ada = ada_w.shape[2]
    mod_cols = _ada_fwd(c_all, ada_w, lax.dynamic_slice(ada_b, (0, chip * n_ada), (2, n_ada)).reshape(2, 1, n_ada))
    pieces = [lax.dynamic_update_slice(jnp.zeros((2, 8, 6 * D), F32), mod_cols, (0, 0, chip * n_ada))]
    split_names = list(SMALL_SPLIT)
    for n in split_names:
        full = SMALL_SPLIT[n]
        local = w[n]
        idx = (0,) * (local.ndim - 1) + (chip * local.shape[-1],)
        pieces.append(lax.dynamic_update_slice(jnp.zeros(local.shape[:-1] + (full,), F32), local, idx))
    shapes = [p.shape for p in pieces]
    mod_slab = _allreduce_small("gather_mod", _pack_small(pieces) * south)
    gathered = _unpack_small(mod_slab, shapes)
    mod = lax.dynamic_slice(gathered[0], (0, me, 0), (2, 1, 6 * D)).reshape(2, 6 * D)
    P = {n: w[n] for n in SMALL if n not in SMALL_SPLIT and n != "ada_b"}
    for n, full in zip(split_names, gathered[1:]):
        P[n] = full
    P["final_norm_w"] = final_norm_w.reshape(1, D)

    cast = {k: p.astype(_MXU) for k, p in _share_pieces(w).items()}
    def whole(gathered, spec):
        out = {}
        for name, piece in _unslab(gathered, spec).items():
            out[name] = lax.dynamic_update_slice(piece.reshape(-1, D), cast[name], (chip * piece.shape[1], 0))
        return out

    in_slab = _allgather_big(_slab(cast, IN_SLAB), after=mod_slab)
    w_in = whole(in_slab, IN_SLAB)["in_wt"]

    def start_gather(tag, collective_id, spec, after):
        share = _slab(cast, spec)
        return _exchange_start("allgather_start_" + tag, collective_id, "gather", share,
                               lax.empty((4,) + share.shape, share.dtype), after=after)

    def finish_gather(tag, started, spec, after):
        land = _exchange_wait("allgather_wait_" + tag, "gather", started, after)[1]
        return whole(_allgather_finish(tag, land), spec)

    gather0 = start_gather("0", 1, LAYER0_REST_SLAB, in_slab)
    gather1 = start_gather("1", 2, LAYER1_SLAB, gather0[4])

    def start_reduce(tag, collective_id, G, spec, after=None):
        gp = _chips_from_full(G, spec)
        p, q = _pair_sum(tag, gp, _sibling_swap("grad_pair_exchange_" + tag, gp, True), c_arr)
        return _exchange_start("grad_exchange_start_" + tag, collective_id, "scatter", p, q, after=after)

    def finish_reduce(tag, started, spec, after, behind=None):
        q = _exchange_wait("grad_exchange_wait_" + tag, "scatter", started, after)[1]
        fin = _chip_sum(tag, q, after=behind)
        total = _join_halves(tag, fin, _sibling_swap("grad_final_exchange_" + tag, fin, False), c_arr)
        return _unslab(total, spec)

    reduces = {}

    def grads1(G1):
        reduces["1"] = start_reduce("1", 3, G1, LAYER1_SLAB)
        return reduces["1"][4]

    def grads_ffn0(G):
        reduces["f"] = start_reduce("f", 4, G, FFN0_SLAB)
        return reduces["f"][4]

    io = {"start": gather1[4],
          "weights0": lambda after: finish_gather("0", gather0, LAYER0_REST_SLAB, after),
          "weights1": lambda after: finish_gather("1", gather1, LAYER1_SLAB, after),
          "grads1": grads1, "grads_ffn0": grads_ffn0}
    sq, grad_x, dmod, G0, g = _local_step(x[0], loss_target[0], mod, w_in, P, io)
    loss = lax.psum(0.5 * sq[0, 0] / D, ("x", "y", "c"))

    g["final_norm_w"] = g["final_norm_w"].reshape(D)
    small_names = [n for n in SMALL if n != "ada_b"]
    pieces = [lax.dynamic_update_slice(jnp.zeros((2, 8, 6 * D), F32), dmod.reshape(2, 1, 6 * D), (0, me, 0))]
    pieces += [g[n] for n in small_names]
    shapes = [p.shape for p in pieces]
    small_slab = _allreduce_small("allreduce_small_grads", _pack_small(pieces))
    reduces["m"] = start_reduce("m", 5, G0, MIXER0_SLAB, after=small_slab)
    shares = finish_reduce("1", reduces["1"], LAYER1_SLAB, grad_x, behind=reduces["m"][4])
    shares.update(finish_reduce("f", reduces["f"], FFN0_SLAB, grad_x, behind=reduces["m"][4]))
    reduced = _unpack_small(small_slab, shapes)
    dmod_all = reduced[0]
    grads = dict(zip(small_names, reduced[1:]))
    for n in split_names:
        full = grads[n]
        size = w[n].shape[-1]
        grads[n] = lax.dynamic_slice(full, (0,) * (full.ndim - 1) + (chip * size,), full.shape[:-1] + (size,))
    grads = {n: grads[n].reshape(w[n].shape) for n in small_names}
    dw_ada, db_ada = _ada_bwd(c_all, lax.dynamic_slice(dmod_all, (0, 0, chip * n_ada), (2, 8, n_ada)), dmod_all)
    grads["ada_w"], grads["ada_b"] = dw_ada, db_ada.reshape(2, 6 * D)

    delta, new_m, new_v = {}, {}, {}

    def update(n):
        cols = w[n].shape[-1]
        d_, m_, v_ = _adamw("adamw_" + n, w[n].reshape(-1, cols), grads[n].reshape(-1, cols), m[n].reshape(-1, cols),
                            v[n].reshape(-1, cols))
        delta[n], new_m[n], new_v[n] = d_.reshape(w[n].shape), m_.reshape(w[n].shape), v_.reshape(w[n].shape)

    update("ada_w")
    shapes = [w[n].shape for n in SMALL]
    packed = [_pack_small([t[n] for n in SMALL]) for t in (w, grads, m, v)]
    outs = _adamw("adamw_small", *packed)
    for dst, slab in zip((delta, new_m, new_v), outs):
        for n, t in zip(SMALL, _unpack_small(slab, shapes)):
            dst[n] = t
    shares.update(finish_reduce("m", reduces["m"], MIXER0_SLAB, outs[0]))
    grads.update(_pieces_to_shares(shares))
    for n in BIG_ARGS:
        update(n)
    return (loss, grad_x[None], *[grads[n] for n in WEIGHTS], *[delta[n] for n in WEIGHTS],
            *[new_m[n] for n in WEIGHTS], *[new_v[n] for n in WEIGHTS])
```

```python
import functools
import math

import numpy as np
import jax
import jax.numpy as jnp
from jax import lax
from jax.experimental import pallas as pl
from jax.experimental.pallas import tpu as pltpu

F32 = jnp.float32
BF16 = jnp.bfloat16
_MXU = jnp.bfloat16
_VMEM_LIMIT = 56 * 1024 * 1024
MXU_COLS = 256
D = 1024
L = 128
NSTATE = 128
EPS = 1e-6
NEG_INF = -1e30
FFN = 2816
ADAM_LR, ADAM_B1, ADAM_B2, ADAM_EPS, ADAM_WD, ADAM_STEP = 0.001, 0.9, 0.999, 1e-08, 0.01, 10
MESH = pl.DeviceIdType.MESH
ANY = pl.BlockSpec(memory_space=pl.ANY)

NN = (((1,), (0,)), ((), ()))
NT = (((1,), (1,)), ((), ()))
TN = (((0,), (0,)), ((), ()))


def _dot(a, b, dn=NN):
    return lax.dot_general(a.astype(_MXU), b.astype(_MXU), dn, preferred_element_type=F32)


def _params(sem=None):
    return pltpu.CompilerParams(dimension_semantics=sem, vmem_limit_bytes=_VMEM_LIMIT)


def _sigmoid(x):
    return 1.0 / (1.0 + jnp.exp(-x))


def _softplus(x):
    return jnp.maximum(x, 0.0) + jnp.log(1.0 + jnp.exp(-jnp.abs(x)))


def _gelu(x):
    return 0.5 * x * (1.0 + lax.erf(x * (2.0 ** -0.5)))


def _gelu_grad(x):
    return 0.5 * (1.0 + lax.erf(x * (2.0 ** -0.5))) + x * jnp.exp(-0.5 * x * x) * (1.0 / math.sqrt(2.0 * math.pi))


def _silu_grad(a):
    sg = _sigmoid(a)
    return sg * (1.0 + a * (1.0 - sg))


def _rowwise(name, fn, rows, vecs, out_rows, out_accs=(), tr=512, after=None):
    S = rows[0].shape[0]
    tr = min(tr, S)
    assert S % tr == 0
    nr, nv, no, na = len(rows), len(vecs), len(out_rows), len(out_accs)
    deps = [] if after is None else [after]

    def body(*refs):
        ins, outs = refs[:nr + nv], refs[nr + nv + len(deps):]
        res = fn(*[r[...] for r in ins])
        if not isinstance(res, (tuple, list)):
            res = (res,)
        for k in range(no):
            outs[k][...] = res[k].astype(outs[k].dtype)
        if na:
            @pl.when(pl.program_id(0) == 0)
            def _():
                for k in range(na):
                    outs[no + k][...] = jnp.zeros_like(outs[no + k])
            for k in range(na):
                outs[no + k][...] += res[no + k]

    in_specs = [pl.BlockSpec((tr, a.shape[1]), lambda i: (i, 0)) for a in rows]
    in_specs += [pl.BlockSpec(v.shape, lambda i: (0, 0)) for v in vecs] + [ANY for _ in deps]
    out_specs = [pl.BlockSpec((tr, c), lambda i: (i, 0)) for c, _ in out_rows]
    out_specs += [pl.BlockSpec(s, lambda i: (0, 0)) for s in out_accs]
    out_shape = [jax.ShapeDtypeStruct((S, c), dt) for c, dt in out_rows]
    out_shape += [jax.ShapeDtypeStruct(s, F32) for s in out_accs]
    return pl.pallas_call(body, name=name, grid=(S // tr,), in_specs=in_specs, out_specs=out_specs,
                          out_shape=out_shape, compiler_params=_params(("arbitrary",)))(*rows, *vecs, *deps)


def _col_tile(n, cap):
    if n <= cap or n % 128:
        return n
    best = 128
    for t in range(128, cap + 1, 128):
        if n % t == 0:
            best = t
    return best


def _mm(name, As, Bs, mode, outs, epi=None, groups=None, extras=(), vecs=(), tm=512, tn_cap=1536):
    M = As[0].shape[0]
    N = Bs[0].shape[1] if mode == "nn" else Bs[0].shape[0]
    tm = min(tm, M)
    tn = _col_tile(N, tn_cap)
    assert M % tm == 0 and N % tn == 0
    npair = len(As)
    groups = groups or [0] * npair
    ng = max(groups) + 1
    nx, nv = len(extras), len(vecs)
    dn = NN if mode == "nn" else NT

    def body(*refs):
        a_refs, b_refs = refs[:npair], refs[npair:2 * npair]
        x_refs = refs[2 * npair:2 * npair + nx]
        v_refs = refs[2 * npair + nx:2 * npair + nx + nv]
        o_refs = refs[2 * npair + nx + nv:]
        step = tn if epi is None else min(tn, MXU_COLS)
        for col in range(0, tn, step):
            sl = slice(col, min(col + step, tn))
            accs = [None] * ng
            for k in range(npair):
                b = b_refs[k][:, sl] if mode == "nn" else b_refs[k][sl, :]
                d = _dot(a_refs[k][...], b, dn)
                accs[groups[k]] = d if accs[groups[k]] is None else accs[groups[k]] + d
            args = accs + [x[:, sl] for x in x_refs] + [v[:, sl] for v in v_refs]
            res = epi(*args) if epi is not None else tuple(accs)
            if not isinstance(res, (tuple, list)):
                res = (res,)
            for o, r in zip(o_refs, res):
                o[:, sl] = r.astype(o.dtype)

    in_specs = [pl.BlockSpec((tm, a.shape[1]), lambda i, j: (i, 0)) for a in As]
    if mode == "nn":
        in_specs += [pl.BlockSpec((b.shape[0], tn), lambda i, j: (0, j)) for b in Bs]
    else:
        in_specs += [pl.BlockSpec((tn, b.shape[1]), lambda i, j: (j, 0)) for b in Bs]
    in_specs += [pl.BlockSpec((tm, tn), lambda i, j: (i, j)) for _ in extras]
    in_specs += [pl.BlockSpec((1, tn), lambda i, j: (0, j)) for _ in vecs]
    out_specs = [pl.BlockSpec((tm, tn), lambda i, j: (i, j)) for _ in outs]
    out_shape = [jax.ShapeDtypeStruct((M, N), dt) for dt in outs]
    return pl.pallas_call(body, name=name, grid=(M // tm, N // tn), in_specs=in_specs, out_specs=out_specs,
                          out_shape=out_shape, compiler_params=_params(("parallel", "parallel")))(
                              *As, *Bs, *extras, *vecs)


def _mm_shared_lhs(name, A, Bs, tm=512):
    M, K = A.shape
    tm = min(tm, M)
    n = len(Bs)

    def body(a_ref, *refs):
        a = a_ref[...]
        for b_ref, o_ref in zip(refs[:n], refs[n:]):
            o_ref[...] = _dot(a, b_ref[...], NT)

    return pl.pallas_call(
        body, name=name, grid=(M // tm,),
        in_specs=[pl.BlockSpec((tm, K), lambda i: (i, 0))] + [pl.BlockSpec(b.shape, lambda i: (0, 0)) for b in Bs],
        out_specs=[pl.BlockSpec((tm, b.shape[0]), lambda i: (i, 0)) for b in Bs],
        out_shape=[jax.ShapeDtypeStruct((M, b.shape[0]), F32) for b in Bs],
        compiler_params=_params(("parallel",)))(A, *Bs)


def _mm_tn(name, A, B, tk=512, t2_cap=1536):
    S, K1 = A.shape
    N2 = B.shape[1]
    tk = min(tk, S)
    t2 = _col_tile(N2, t2_cap)
    assert S % tk == 0 and N2 % t2 == 0

    def body(a_ref, b_ref, o_ref):
        @pl.when(pl.program_id(1) == 0)
        def _():
            o_ref[...] = jnp.zeros_like(o_ref)
        o_ref[...] += _dot(a_ref[...], b_ref[...], TN)

    return pl.pallas_call(
        body, name=name, grid=(N2 // t2, S // tk),
        in_specs=[pl.BlockSpec((tk, K1), lambda j, k: (k, 0)), pl.BlockSpec((tk, t2), lambda j, k: (k, j))],
        out_specs=pl.BlockSpec((K1, t2), lambda j, k: (0, j)),
        out_shape=jax.ShapeDtypeStruct((K1, N2), F32),
        compiler_params=_params(("parallel", "arbitrary")))(A, B)


def _norm_mod_fwd(name, x, nw, sc, sh, after=None):
    def fn(x, nw, sc, sh):
        rstd = lax.rsqrt(jnp.mean(x * x, axis=-1, keepdims=True) + EPS)
        return (x * rstd * nw) * (1.0 + sc) + sh
    return _rowwise(name, fn, [x], [nw, sc, sh], [(D, BF16)], after=after)[0]


def _norm_mod_bwd(name, x, dh, dres, nw, sc, after=None):
    def fn(x, dh, dres, nw, sc):
        rstd = lax.rsqrt(jnp.mean(x * x, axis=-1, keepdims=True) + EPS)
        xh = x * rstd
        dn = dh * (1.0 + sc)
        dxh = dn * nw
        dx = rstd * (dxh - xh * jnp.mean(dxh * xh, axis=-1, keepdims=True))
        return (dres + dx, jnp.sum(dh, axis=0, keepdims=True), jnp.sum(dh * (xh * nw), axis=0, keepdims=True),
                jnp.sum(dn * xh, axis=0, keepdims=True))
    return _rowwise(name, fn, [x, dh, dres], [nw, sc], [(D, F32)], [(1, D)] * 3, after=after)


def _gate_bwd(name, dx, y, g, after=None):
    def fn(dx, y, g):
        dy = dx * g
        return dy, jnp.sum(dx * y, axis=0, keepdims=True), jnp.sum(dy, axis=0, keepdims=True)
    return _rowwise(name, fn, [dx, y], [g], [(D, BF16)], [(1, D)] * 2, after=after)


def _loss_head(x, tgt, fw):
    def fn(x, tgt, fw):
        rstd = lax.rsqrt(jnp.mean(x * x, axis=-1, keepdims=True) + EPS)
        xh = x * rstd
        err = xh * fw - tgt
        dout = err * (1.0 / D)
        dxh = dout * fw
        dx = rstd * (dxh - xh * jnp.mean(dxh * xh, axis=-1, keepdims=True))
        sq = jnp.sum(jnp.sum(err * err, axis=1, keepdims=True), axis=0, keepdims=True)
        return dx, sq, jnp.sum(dout * xh, axis=0, keepdims=True)
    return _rowwise("loss_head", fn, [x, tgt], [fw], [(D, F32)], [(1, 1), (1, D)])


def _ffn_fwd(tag, h, wg, wu, wd, x, g2):
    def act(a, b):
        return a, b, a * _sigmoid(a) * b
    a, b, f = _mm(f"ffn_up_{tag}", [h, h], [wg, wu], "nt", [BF16, BF16, BF16], epi=act, groups=[0, 1], tn_cap=1408,
                  tm=1024)

    def res(y, x, g):
        return y, x + g * y
    y, xo = _mm(f"ffn_down_{tag}", [f], [wd], "nn", [F32, F32], epi=res, extras=[x], vecs=[g2])
    return a, b, f, y, xo


def _ffn_bwd(tag, dx, h, a, b, f, y, wg, wu, wd, g2, after=None):
    dy, dg2, _ = _gate_bwd(f"ffn_gate_bwd_{tag}", dx, y, g2, after=after)

    def act_bwd(df, a, b):
        a, b = a.astype(F32), b.astype(F32)
        sg = _sigmoid(a)
        return df * b * (sg * (1.0 + a * (1.0 - sg))), df * (a * sg)
    da, db = _mm(f"ffn_dact_{tag}", [dy], [wd], "nt", [BF16, BF16], epi=act_bwd, extras=[a, b], tn_cap=1408, tm=1024)
    dwd = _mm_tn(f"ffn_dwd_{tag}", f, dy)
    dwg = _mm_tn(f"ffn_dwg_{tag}", da, h)
    dwu = _mm_tn(f"ffn_dwu_{tag}", db, h)
    dh = _mm(f"ffn_dh_{tag}", [da, db], [wg, wu], "nn", [F32])[0]
    return dh, dg2, dwg, dwu, dwd


def _conv_fwd(xr, w, b, tb=512):
    S, C = xr.shape
    tb = min(tb, S)

    def body(x_ref, halo_ref, w_ref, b_ref, pre_ref, out_ref):
        i = pl.program_id(0)
        halo = jnp.where(i > 0, halo_ref[...], 0.0)
        xe = jnp.concatenate([halo, x_ref[...]], axis=0)
        pre = w_ref[3:4, :] * x_ref[...] + b_ref[...]
        for j in (1, 2, 3):
            pre = pre + w_ref[3 - j:4 - j, :] * pltpu.roll(xe, j, axis=0)[8:, :]
        pre_ref[...] = pre
        out_ref[...] = pre * _sigmoid(pre)

    return pl.pallas_call(
        body, name="conv_fwd", grid=(S // tb,),
        in_specs=[pl.BlockSpec((tb, C), lambda i: (i, 0)),
                  pl.BlockSpec((8, C), lambda i: (jnp.maximum(i * (tb // 8) - 1, 0), 0)),
                  pl.BlockSpec((4, C), lambda i: (0, 0)), pl.BlockSpec((1, C), lambda i: (0, 0))],
        out_specs=[pl.BlockSpec((tb, C), lambda i: (i, 0))] * 2,
        out_shape=[jax.ShapeDtypeStruct((S, C), F32)] * 2,
        compiler_params=_params(("parallel",)))(xr, xr, w, b)


def _conv_bwd(dxc, pre, xr, w, tb=512):
    S, C = xr.shape
    tb = min(tb, S)
    nblk = S // tb

    def body(d_ref, p_ref, dn_ref, pn_ref, x_ref, w_ref, dx_ref, dw_ref, db_ref):
        i = pl.program_id(0)

        @pl.when(i == 0)
        def _():
            dw_ref[...] = jnp.zeros_like(dw_ref)
            db_ref[...] = jnp.zeros_like(db_ref)

        dpre = d_ref[...] * _silu_grad(p_ref[...])
        dnext = jnp.where(i < nblk - 1, dn_ref[...] * _silu_grad(pn_ref[...]), 0.0)
        pe = jnp.concatenate([dpre, dnext], axis=0)
        xx = x_ref[...]
        dx = w_ref[3:4, :] * dpre
        dw_ref[3:4, :] += jnp.sum(dpre * xx, axis=0, keepdims=True)
        for j in (1, 2, 3):
            ahead = pltpu.roll(pe, tb + 8 - j, axis=0)[:tb, :]
            dx = dx + w_ref[3 - j:4 - j, :] * ahead
            dw_ref[3 - j:4 - j, :] += jnp.sum(ahead * xx, axis=0, keepdims=True)
        dx_ref[...] = dx.astype(dx_ref.dtype)
        db_ref[...] += jnp.sum(dpre, axis=0, keepdims=True)

    blk = pl.BlockSpec((tb, C), lambda i: (i, 0))
    nxt = pl.BlockSpec((8, C), lambda i: (jnp.minimum((i + 1) * (tb // 8), S // 8 - 1), 0))
    return pl.pallas_call(
        body, name="conv_bwd", grid=(nblk,),
        in_specs=[blk, blk, nxt, nxt, blk, pl.BlockSpec((4, C), lambda i: (0, 0))],
        out_specs=[blk, pl.BlockSpec((4, C), lambda i: (0, 0)), pl.BlockSpec((1, C), lambda i: (0, 0))],
        out_shape=[jax.ShapeDtypeStruct((S, C), BF16), jax.ShapeDtypeStruct((4, C), F32),
                   jax.ShapeDtypeStruct((1, C), F32)],
        compiler_params=_params(("arbitrary",)))(dxc, pre, dxc, pre, xr, w)


def _iota(shape, dim):
    return lax.broadcasted_iota(jnp.int32, shape, dim)


def _colsel(m, lane, h):
    return jnp.sum(jnp.where(lane == h, m, 0.0), axis=1, keepdims=True)


def _cumsum_rows(v):
    r = _iota(v.shape, 0)
    k = 1
    while k < v.shape[0]:
        v = v + jnp.where(r >= k, pltpu.roll(v, k, axis=0), 0.0)
        k *= 2
    return v


def _suffix_sum_rows(v):
    n = v.shape[0]
    r = _iota(v.shape, 0)
    k = 1
    while k < n:
        v = v + jnp.where(r < n - k, pltpu.roll(v, n - k, axis=0), 0.0)
        k *= 2
    return v


def _ssd_fwd(xc, dtr, z, dtb, alog, dskl, nw):
    S = xc.shape[0]
    nc = S // L

    def body(xc_ref, dtr_ref, z_ref, dtb_ref, alog_ref, dsk_ref, nw_ref, ya_ref, y_ref, prev_ref,
             st_ref, cum_ref, cumT_ref):
        i = pl.program_id(0)

        @pl.when(i == 0)
        def _():
            st_ref[...] = jnp.zeros_like(st_ref)

        lane = _iota((L, 128), 1)
        lane1 = _iota((1, 128), 1)
        lo = lane < 64
        lo1 = lane1 < 64
        tril = _iota((L, L), 0) >= _iota((L, L), 1)
        dt = _softplus(dtr_ref[...] + dtb_ref[...])
        a_neg = -jnp.exp(alog_ref[...])
        cum = _cumsum_rows(dt * a_neg)
        cum_ref[...] = cum
        cumT_ref[...] = cum.T
        last_all = cum_ref[L - 1:L, :]
        prev_t = st_ref[...]
        prev_ref[0] = prev_t
        for g in range(2):
            bg = xc_ref[:, 1024 + g * 128:1152 + g * 128]
            cg = xc_ref[:, 1280 + g * 128:1408 + g * 128]
            gmat = _dot(cg, bg, NT)
            yoff = _dot(cg, prev_t[:, g * 512:(g + 1) * 512])
            bg_t = bg.T
            for jp in range(4):
                j = g * 4 + jp
                sl = slice(j * 128, (j + 1) * 128)
                xp = xc_ref[:, sl]
                cc = [_colsel(cum, lane, 2 * j), _colsel(cum, lane, 2 * j + 1)]
                cum_l = jnp.where(lo, cc[0], cc[1])
                dt_l = jnp.where(lo, _colsel(dt, lane, 2 * j), _colsel(dt, lane, 2 * j + 1))
                last_l = jnp.where(lo1, _colsel(last_all, lane1, 2 * j), _colsel(last_all, lane1, 2 * j + 1))
                xd = xp * dt_l
                ys = []
                for hh in range(2):
                    seg = cc[hh] - cumT_ref[2 * j + hh:2 * j + hh + 1, :]
                    dm = jnp.where(tril, jnp.exp(jnp.where(tril, seg, 0.0)), 0.0)
                    ys.append(_dot(gmat * dm, xd))
                y_ref[:, sl] = (jnp.where(lo, ys[0], ys[1]) + jnp.exp(cum_l) * yoff[:, jp * 128:(jp + 1) * 128]
                                + dsk_ref[:, sl] * xp)
                st_ref[:, sl] = prev_t[:, sl] * jnp.exp(last_l) + _dot(bg_t, xd * jnp.exp(last_l - cum_l))
        for g in range(2):
            sl = slice(g * 512, (g + 1) * 512)
            zz = z_ref[:, sl]
            yg = y_ref[:, sl] * (zz * _sigmoid(zz))
            rstd = lax.rsqrt(jnp.mean(yg * yg, axis=-1, keepdims=True) + EPS)
            ya_ref[:, sl] = (yg * rstd * nw_ref[:, sl]).astype(ya_ref.dtype)

    blk = lambda c: pl.BlockSpec((L, c), lambda i: (i, 0))
    vec = lambda c: pl.BlockSpec((1, c), lambda i: (0, 0))
    return pl.pallas_call(
        body, name="ssd_fwd", grid=(nc,),
        in_specs=[blk(1536), blk(128), blk(1024), vec(128), vec(128), vec(1024), vec(1024)],
        out_specs=[blk(1024), blk(1024), pl.BlockSpec((1, NSTATE, 1024), lambda i: (i, 0, 0))],
        out_shape=[jax.ShapeDtypeStruct((S, 1024), BF16), jax.ShapeDtypeStruct((S, 1024), F32),
                   jax.ShapeDtypeStruct((nc, NSTATE, 1024), F32)],
        scratch_shapes=[pltpu.VMEM((NSTATE, 1024), F32), pltpu.VMEM((L, 128), F32), pltpu.VMEM((L, 128), F32)],
        compiler_params=_params(("arbitrary",)))(xc, dtr, z, dtb, alog, dskl, nw)


def _ssd_bwd(dya, y, z, xc, dtr, prev, dtb, alog, dskl, nw):
    S = xc.shape[0]
    nc = S // L

    def body(dya_ref, y_ref, z_ref, xc_ref, dtr_ref, prev_ref, dtb_ref, alog_ref, dsk_ref, nw_ref,
             dz_ref, dxc_ref, ddtr_ref, dnw_ref, ddsk_ref, dalog_ref, ddtb_ref,
             dst_ref, cum_ref, cumT_ref, dy_ref, dskacc_ref):
        i = pl.program_id(0)

        @pl.when(i == 0)
        def _():
            dst_ref[...] = jnp.zeros_like(dst_ref)
            dskacc_ref[...] = jnp.zeros_like(dskacc_ref)
            dnw_ref[...] = jnp.zeros_like(dnw_ref)
            dalog_ref[...] = jnp.zeros_like(dalog_ref)
            ddtb_ref[...] = jnp.zeros_like(ddtb_ref)

        lane = _iota((L, 128), 1)
        lane1 = _iota((1, 128), 1)
        lo = lane < 64
        lo1 = lane1 < 64
        r2, c2 = _iota((L, L), 0), _iota((L, L), 1)
        tril = r2 >= c2
        triu = r2 <= c2
        is_last = _iota((L, 1), 0) == L - 1

        for g in range(2):
            sl = slice(g * 512, (g + 1) * 512)
            zz = z_ref[:, sl]
            sg = _sigmoid(zz)
            zg = zz * sg
            yv = y_ref[:, sl]
            yg = yv * zg
            rstd = lax.rsqrt(jnp.mean(yg * yg, axis=-1, keepdims=True) + EPS)
            xh = yg * rstd
            d_out = dya_ref[:, sl]
            dnw_ref[:, sl] += jnp.sum(d_out * xh, axis=0, keepdims=True)
            dyn = d_out * nw_ref[:, sl]
            dyg = rstd * (dyn - xh * jnp.mean(dyn * xh, axis=-1, keepdims=True))
            dy_ref[:, sl] = dyg * zg
            dz_ref[:, sl] = (dyg * yv * (sg * (1.0 + zz * (1.0 - sg)))).astype(dz_ref.dtype)

        dtin = dtr_ref[...] + dtb_ref[...]
        dt = _softplus(dtin)
        a_neg = -jnp.exp(alog_ref[...])
        cum = _cumsum_rows(dt * a_neg)
        cum_ref[...] = cum
        cumT_ref[...] = cum.T
        last_all = cum_ref[L - 1:L, :]
        prev_t = prev_ref[0]
        dn_t = dst_ref[...]
        dcum = jnp.zeros((L, 128), F32)
        ddt = jnp.zeros((L, 128), F32)
        for g in range(2):
            gsl = slice(g * 512, (g + 1) * 512)
            bg = xc_ref[:, 1024 + g * 128:1152 + g * 128]
            cg = xc_ref[:, 1280 + g * 128:1408 + g * 128]
            gmat = _dot(cg, bg, NT)
            gmat_t = _dot(bg, cg, NT)
            pg = prev_t[:, gsl]
            zmat = _dot(cg, pg)
            dgm = jnp.zeros((L, L), F32)
            dgm_t = jnp.zeros((L, L), F32)
            db_acc = jnp.zeros((L, NSTATE), F32)
            dz_parts, cd_parts = [], []
            for jp in range(4):
                j = g * 4 + jp
                sl = slice(j * 128, (j + 1) * 128)
                xp = xc_ref[:, sl]
                dyp = dy_ref[:, sl]
                cc = [_colsel(cum, lane, 2 * j), _colsel(cum, lane, 2 * j + 1)]
                lc = [_colsel(last_all, lane1, 2 * j), _colsel(last_all, lane1, 2 * j + 1)]
                cum_l = jnp.where(lo, cc[0], cc[1])
                dt_l = jnp.where(lo, _colsel(dt, lane, 2 * j), _colsel(dt, lane, 2 * j + 1))
                last_l = jnp.where(lo1, lc[0], lc[1])
                e_l = jnp.exp(cum_l)
                dte_l = jnp.exp(last_l - cum_l)
                cd_l = jnp.exp(last_l)
                cd_parts.append(cd_l)
                xd = xp * dt_l
                dskacc_ref[:, sl] += jnp.sum(dyp * xp, axis=0, keepdims=True)
                dxp = dsk_ref[:, sl] * dyp
                t = dyp * (e_l * zmat[:, jp * 128:(jp + 1) * 128])
                dcc = [jnp.sum(jnp.where(lo, t, 0.0), axis=1, keepdims=True),
                       jnp.sum(jnp.where(lo, 0.0, t), axis=1, keepdims=True)]
                dz_parts.append(e_l * dyp)
                dnp_ = dn_t[:, sl]
                t2 = jnp.sum(dnp_ * prev_t[:, sl], axis=0, keepdims=True)
                dcd = [jnp.sum(jnp.where(lo1, t2, 0.0), axis=1, keepdims=True),
                       jnp.sum(jnp.where(lo1, 0.0, t2), axis=1, keepdims=True)]
                wm = _dot(bg, dnp_)
                dxd = wm * dte_l
                t3 = wm * xd
                ddte = [jnp.sum(jnp.where(lo, t3, 0.0), axis=1, keepdims=True),
                        jnp.sum(jnp.where(lo, 0.0, t3), axis=1, keepdims=True)]
                db_acc = db_acc + _dot(xd * dte_l, dnp_, NT)
                for hh in range(2):
                    h = 2 * j + hh
                    half = lo if hh == 0 else jnp.logical_not(lo)
                    row = cumT_ref[h:h + 1, :]
                    dm = jnp.where(tril, jnp.exp(jnp.where(tril, cc[hh] - row, 0.0)), 0.0)
                    dm_t = jnp.where(triu, jnp.exp(jnp.where(triu, row - cc[hh], 0.0)), 0.0)
                    m = gmat * dm
                    m_t = gmat_t * dm_t
                    dym = jnp.where(half, dyp, 0.0)
                    d_m = _dot(dym, xd, NT)
                    d_mt = _dot(xd, dym, NT)
                    dxd = dxd + _dot(m_t, dym)
                    dcc[hh] = dcc[hh] + jnp.sum(d_m * m, axis=1, keepdims=True) - jnp.sum(d_mt * m_t, axis=1, keepdims=True)
                    dgm = dgm + d_m * dm
                    dgm_t = dgm_t + d_mt * dm_t
                    dte_c = jnp.exp(lc[hh] - cc[hh])
                    dcc[hh] = dcc[hh] - ddte[hh] * dte_c
                    endc = dcd[hh] * jnp.exp(lc[hh]) + jnp.sum(ddte[hh] * dte_c, axis=0, keepdims=True)
                    dcc[hh] = dcc[hh] + jnp.where(is_last, endc, 0.0)
                    dcum = jnp.where(lane == h, dcc[hh], dcum)
                dxc_ref[:, sl] = dxp + dxd * dt_l
                t4 = dxd * xp
                ddt = jnp.where(lane == 2 * j, jnp.sum(jnp.where(lo, t4, 0.0), axis=1, keepdims=True), ddt)
                ddt = jnp.where(lane == 2 * j + 1, jnp.sum(jnp.where(lo, 0.0, t4), axis=1, keepdims=True), ddt)
            dzg = jnp.concatenate(dz_parts, axis=1)
            dst_ref[:, gsl] = dn_t[:, gsl] * jnp.concatenate(cd_parts, axis=1) + _dot(cg.T, dzg)
            dxc_ref[:, 1280 + g * 128:1408 + g * 128] = _dot(dgm, bg) + _dot(dzg, pg, NT)
            dxc_ref[:, 1024 + g * 128:1152 + g * 128] = _dot(dgm_t, cg) + db_acc
        dla = _suffix_sum_rows(dcum)
        ddt = ddt + dla * a_neg
        dalog_ref[...] += jnp.sum(dla * dt, axis=0, keepdims=True) * a_neg
        ddtr = jnp.where(lane < 16, ddt * _sigmoid(dtin), 0.0)
        ddtr_ref[...] = ddtr.astype(ddtr_ref.dtype)
        ddtb_ref[...] += jnp.sum(ddtr, axis=0, keepdims=True)

        @pl.when(i == nc - 1)
        def _():
            seg = (_iota((1024, 128), 0) // 64 == _iota((1024, 128), 1)).astype(F32)
            acc8 = jnp.broadcast_to(dskacc_ref[...], (8, 1024))
            ddsk_ref[...] = lax.dot_general(acc8, seg, NN, precision=lax.Precision.HIGHEST,
                                            preferred_element_type=F32)

    rev = lambda c: pl.BlockSpec((L, c), lambda i: (nc - 1 - i, 0))
    vec = lambda c: pl.BlockSpec((1, c), lambda i: (0, 0))
    return pl.pallas_call(
        body, name="ssd_bwd", grid=(nc,),
        in_specs=[rev(1024), rev(1024), rev(1024), rev(1536), rev(128),
                  pl.BlockSpec((1, NSTATE, 1024), lambda i: (nc - 1 - i, 0, 0)),
                  vec(128), vec(128), vec(1024), vec(1024)],
        out_specs=[rev(1024), rev(1536), rev(128), vec(1024), pl.BlockSpec((8, 128), lambda i: (0, 0)),
                   vec(128), vec(128)],
        out_shape=[jax.ShapeDtypeStruct((S, 1024), BF16), jax.ShapeDtypeStruct((S, 1536), F32),
                   jax.ShapeDtypeStruct((S, 128), BF16), jax.ShapeDtypeStruct((1, 1024), F32),
                   jax.ShapeDtypeStruct((8, 128), F32), jax.ShapeDtypeStruct((1, 128), F32),
                   jax.ShapeDtypeStruct((1, 128), F32)],
        scratch_shapes=[pltpu.VMEM((NSTATE, 1024), F32), pltpu.VMEM((L, 128), F32), pltpu.VMEM((L, 128), F32),
                        pltpu.VMEM((L, 1024), F32), pltpu.VMEM((1, 1024), F32)],
        compiler_params=_params(("arbitrary",)))(dya, y, z, xc, dtr, prev, dtb, alog, dskl, nw)


def _layer_norm_parts(vg):
    mu = jnp.mean(vg, axis=-1, keepdims=True)
    vc = vg - mu
    rstd = lax.rsqrt(jnp.mean(vc * vc, axis=-1, keepdims=True) + EPS)
    return vc * rstd, rstd


def _gmlp_fwd(u, v, lnw, lnb, ws, bse, tb=512):
    S = u.shape[0]
    tb = min(tb, S)

    def body(u_ref, v_ref, lnw_ref, lnb_ref, ws_ref, bse_ref, o_ref, vn_ref):
        tril = _iota((L, L), 0) >= _iota((L, L), 1)
        xh, _ = _layer_norm_parts(_gelu(v_ref[...]))
        vn_ref[...] = xh * lnw_ref[...] + lnb_ref[...]
        for g in range(8):
            w = jnp.where(tril, ws_ref[g], 0.0)
            gs = slice(g * 128, (g + 1) * 128)
            for ch in range(tb // L):
                rs = slice(ch * L, (ch + 1) * L)
                sv = _dot(w, vn_ref[rs, gs]) + bse_ref[g]
                o_ref[rs, gs] = (_gelu(u_ref[rs, gs]) * sv).astype(o_ref.dtype)

    blk = pl.BlockSpec((tb, 1024), lambda i: (i, 0))
    vec = pl.BlockSpec((1, 1024), lambda i: (0, 0))
    cube = pl.BlockSpec((8, L, 128), lambda i: (0, 0, 0))
    return pl.pallas_call(
        body, name="gmlp_fwd", grid=(S // tb,), in_specs=[blk, blk, vec, vec, cube, cube], out_specs=blk,
        out_shape=jax.ShapeDtypeStruct((S, 1024), BF16), scratch_shapes=[pltpu.VMEM((tb, 1024), F32)],
        compiler_params=_params(("parallel",)))(u, v, lnw, lnb, ws, bse)


def _gmlp_bwd(dyb, u, v, lnw, lnb, ws, bse, tb=512):
    S = u.shape[0]
    tb = min(tb, S)

    def body(d_ref, u_ref, v_ref, lnw_ref, lnb_ref, ws_ref, bse_ref,
             du_ref, dv_ref, dws_ref, dbse_ref, dlnw_ref, dlnb_ref, vn_ref, dvn_ref):
        @pl.when(pl.program_id(0) == 0)
        def _():
            dws_ref[...] = jnp.zeros_like(dws_ref)
            dbse_ref[...] = jnp.zeros_like(dbse_ref)
            dlnw_ref[...] = jnp.zeros_like(dlnw_ref)
            dlnb_ref[...] = jnp.zeros_like(dlnb_ref)

        tril = _iota((L, L), 0) >= _iota((L, L), 1)
        vv = v_ref[...]
        xh, rstd = _layer_norm_parts(_gelu(vv))
        vn_ref[...] = xh * lnw_ref[...] + lnb_ref[...]
        for g in range(8):
            w = jnp.where(tril, ws_ref[g], 0.0)
            w_t = w.T
            gs = slice(g * 128, (g + 1) * 128)
            dw = jnp.zeros((L, L), F32)
            dbs = jnp.zeros((L, 128), F32)
            for ch in range(tb // L):
                rs = slice(ch * L, (ch + 1) * L)
                vn = vn_ref[rs, gs]
                sv = _dot(w, vn) + bse_ref[g]
                uu = u_ref[rs, gs]
                dd = d_ref[rs, gs]
                du_ref[rs, gs] = (dd * sv * _gelu_grad(uu)).astype(du_ref.dtype)
                dsv = dd * _gelu(uu)
                dw = dw + _dot(dsv, vn, NT)
                dbs = dbs + dsv
                dvn_ref[rs, gs] = _dot(w_t, dsv)
            dws_ref[g] += jnp.where(tril, dw, 0.0)
            dbse_ref[g] += dbs
        dvn = dvn_ref[...]
        dlnw_ref[...] += jnp.sum(dvn * xh, axis=0, keepdims=True)
        dlnb_ref[...] += jnp.sum(dvn, axis=0, keepdims=True)
        dxh = dvn * lnw_ref[...]
        dvg = rstd * (dxh - jnp.mean(dxh, axis=-1, keepdims=True) - xh * jnp.mean(dxh * xh, axis=-1, keepdims=True))
        dv_ref[...] = (dvg * _gelu_grad(vv)).astype(dv_ref.dtype)

    blk = pl.BlockSpec((tb, 1024), lambda i: (i, 0))
    vec = pl.BlockSpec((1, 1024), lambda i: (0, 0))
    cube = pl.BlockSpec((8, L, 128), lambda i: (0, 0, 0))
    return pl.pallas_call(
        body, name="gmlp_bwd", grid=(S // tb,), in_specs=[blk, blk, blk, vec, vec, cube, cube],
        out_specs=[blk, blk, cube, cube, vec, vec],
        out_shape=[jax.ShapeDtypeStruct((S, 1024), BF16), jax.ShapeDtypeStruct((S, 1024), BF16),
                   jax.ShapeDtypeStruct((8, L, 128), F32), jax.ShapeDtypeStruct((8, L, 128), F32),
                   jax.ShapeDtypeStruct((1, 1024), F32), jax.ShapeDtypeStruct((1, 1024), F32)],
        scratch_shapes=[pltpu.VMEM((tb, 1024), F32), pltpu.VMEM((tb, 1024), F32)],
        compiler_params=_params(("arbitrary",)))(dyb, u, v, lnw, lnb, ws, bse)


def _lane_sum(name, a):
    def body(a_ref, o_ref):
        o_ref[...] = jnp.sum(a_ref[...], axis=1, keepdims=True)
    return pl.pallas_call(body, name=name, out_shape=jax.ShapeDtypeStruct((a.shape[0], 1), F32))(a)


def _bucket_onehot_t():
    qi = np.arange(L)[:, None]
    sj = np.arange(2 * L)[None, :]
    dist = np.maximum(qi + L - sj, 0)
    log_ratio = (np.log(np.maximum(dist, 1).astype(np.float32) / np.float32(16)) / np.float32(math.log(128 / 16)))
    large = 16 + (log_ratio.astype(np.float32) * np.float32(16)).astype(np.int32)
    bucket = np.where(dist < 16, dist, np.minimum(large, 31)).reshape(-1)
    return (np.arange(32)[:, None] == bucket[None, :]).astype(np.float32)


def _rel_bias(table_t, onehot_t):
    def body(t_ref, oh_ref, o_ref):
        o_ref[...] = lax.dot_general(t_ref[...], oh_ref[...], NN, precision=lax.Precision.HIGHEST,
                                     preferred_element_type=F32)
    return pl.pallas_call(body, name="rel_bias", out_shape=jax.ShapeDtypeStruct((16, L * 2 * L), F32),
                          compiler_params=_params())(table_t, onehot_t)


def _rel_bias_bwd(dbias, onehot_t):
    def body(d_ref, oh_ref, o_ref):
        o_ref[...] = lax.dot_general(d_ref[...], oh_ref[...], NT, precision=lax.Precision.HIGHEST,
                                     preferred_element_type=F32)
    return pl.pallas_call(body, name="rel_bias_bwd", out_shape=jax.ShapeDtypeStruct((16, 32), F32),
                          compiler_params=_params())(dbias, onehot_t)


def _band(kp, kc, lo):
    kk = jnp.concatenate([kp, kc], axis=0)
    kr = pltpu.roll(kk, 64, axis=1)
    return [jnp.where(lo, kk, kr), jnp.where(lo, kr, kk)]


def _attn_rows(ref, j, lo):
    parts = []
    for t in range(8):
        pair = ref[:, (4 * j + t // 2) * 128:(4 * j + t // 2 + 1) * 128]
        parts.append(jnp.where(lo if t % 2 == 0 else jnp.logical_not(lo), pair, 0.0))
    return jnp.concatenate(parts, axis=0)


def _attn_mask(i, rows):
    qi, sj = _iota((rows, 2 * L), 0) & (L - 1), _iota((rows, 2 * L), 1)
    rel = qi + L - sj
    return (rel >= 0) & (rel < L) & ((sj >= L) | (i > 0))


def _per_head_col(vals):
    return jnp.concatenate([jnp.broadcast_to(v, (L, 1)) for v in vals], axis=0)


SMEM = pl.BlockSpec(memory_space=pltpu.SMEM)


def _attn_fwd(qkv, bias, sinks):
    S = qkv.shape[0]
    nb = S // L
    scale = 64 ** -0.5

    def body(sink_ref, q_ref, kc_ref, vc_ref, kp_ref, vp_ref, bias_ref, o_ref, lse_ref):
        i = pl.program_id(0)
        lane = _iota((L, 128), 1)
        lo = lane < 64
        lo2 = _iota((2 * L, 128), 1) < 64
        mask = _attn_mask(i, L)
        kd = _band(kp_ref[...], kc_ref[...], lo2)
        vd = _band(vp_ref[...], vc_ref[...], lo2)
        lse = jnp.zeros((L, 128), F32)
        for pr in range(8):
            sl = slice(pr * 128, (pr + 1) * 128)
            qp = q_ref[:, sl]
            j = pr // 4
            outs = []
            for hh in range(2):
                h = 2 * pr + hh
                qm = jnp.where(lo if hh == 0 else jnp.logical_not(lo), qp, 0.0)
                lg = jnp.where(mask, _dot(qm, kd[j], NT) * scale + bias_ref[h], NEG_INF)
                s = sink_ref[h]
                m = jnp.maximum(jnp.max(lg, axis=1, keepdims=True), s)
                p = jnp.where(mask, jnp.exp(lg - m), 0.0)
                den = jnp.sum(p, axis=1, keepdims=True) + jnp.exp(s - m)
                outs.append(_dot(p / den, vd[j]))
                lse = jnp.where(lane == h, m + jnp.log(den), lse)
            o_ref[:, sl] = jnp.where(lo, outs[0], outs[1]).astype(o_ref.dtype)
        lse_ref[...] = lse

    prev = lambda col: pl.BlockSpec((L, 128), lambda i: (jnp.maximum(i - 1, 0), col))
    cur = lambda col: pl.BlockSpec((L, 128), lambda i: (i, col))
    return pl.pallas_call(
        body, name="attn_fwd", grid=(nb,),
        in_specs=[SMEM, pl.BlockSpec((L, 1024), lambda i: (i, 0)), cur(8), cur(9), prev(8), prev(9),
                  pl.BlockSpec((16, L, 2 * L), lambda i: (0, 0, 0))],
        out_specs=[pl.BlockSpec((L, 1024), lambda i: (i, 0)), pl.BlockSpec((L, 128), lambda i: (i, 0))],
        out_shape=[jax.ShapeDtypeStruct((S, 1024), BF16), jax.ShapeDtypeStruct((S, 128), F32)],
        compiler_params=_params(("parallel",)))(sinks, qkv, qkv, qkv, qkv, qkv, bias)


def _attn_bwd(qkv, d_o, lse, bias, sinks):
    S = qkv.shape[0]
    nb = S // L
    scale = 64 ** -0.5

    def body(sink_ref, q_ref, kc_ref, vc_ref, kp_ref, vp_ref, do_ref, lse_ref, bias_ref,
             dq_ref, dkv_ref, dbias_ref, dsink_ref, dbq_ref, dbkv_ref, carry_ref):
        i = pl.program_id(0)

        @pl.when(i == 0)
        def _():
            dbias_ref[...] = jnp.zeros_like(dbias_ref)
            dsink_ref[...] = jnp.zeros_like(dsink_ref)
            dbq_ref[...] = jnp.zeros_like(dbq_ref)
            dbkv_ref[...] = jnp.zeros_like(dbkv_ref)
            carry_ref[...] = jnp.zeros_like(carry_ref)

        @pl.when(i < nb)
        def _():
            lane = _iota((L, 128), 1)
            lane1 = _iota((1, 128), 1)
            lo = lane < 64
            lo2 = _iota((2 * L, 128), 1) < 64
            mask = _attn_mask(i, 8 * L)
            kd = _band(kp_ref[...], kc_ref[...], lo2)
            vd = _band(vp_ref[...], vc_ref[...], lo2)
            lse_all = lse_ref[...]
            dsink = jnp.zeros((1, 128), F32)
            tot_k, tot_v = [], []
            for j in range(2):
                q_all = _attn_rows(q_ref, j, lo)
                do_all = _attn_rows(do_ref, j, lo)
                lse_col = _per_head_col([_colsel(lse_all, lane, 8 * j + t) for t in range(8)])
                lg = _dot(q_all, kd[j], NT) * scale + bias_ref[8 * j:8 * j + 8].reshape(8 * L, 2 * L)
                p = jnp.where(mask, jnp.exp(jnp.where(mask, lg, NEG_INF) - lse_col), 0.0)
                dp = _dot(do_all, vd[j], NT)
                delta = jnp.sum(p * dp, axis=1, keepdims=True)
                ds = p * (dp - delta)
                dbias_ref[8 * j:8 * j + 8] += ds.reshape(8, L, 2 * L)
                s = _per_head_col([sink_ref[8 * j + t] for t in range(8)])
                sink_part = -jnp.exp(s - lse_col) * delta
                for t in range(8):
                    dsink = dsink + jnp.where(lane1 == 8 * j + t,
                                              jnp.sum(sink_part[t * L:(t + 1) * L], axis=0, keepdims=True), 0.0)
                dss = ds * scale
                dq_all = _dot(dss, kd[j])
                for t in range(0, 8, 2):
                    sl = slice((4 * j + t // 2) * 128, (4 * j + t // 2 + 1) * 128)
                    dq = jnp.where(lo, dq_all[t * L:(t + 1) * L], dq_all[(t + 1) * L:(t + 2) * L])
                    dq_ref[:, sl] = dq.astype(dq_ref.dtype)
                    dbq_ref[:, sl] += jnp.sum(dq, axis=0, keepdims=True)
                acc_k = _dot(dss, q_all, TN)
                acc_v = _dot(p, do_all, TN)
                tot_k.append(acc_k + pltpu.roll(acc_k, 64, axis=1))
                tot_v.append(acc_v + pltpu.roll(acc_v, 64, axis=1))
            dsink_ref[...] += dsink
            dkv = jnp.concatenate([jnp.where(lo2, tot_k[0], tot_k[1]), jnp.where(lo2, tot_v[0], tot_v[1])], axis=1)
            dbkv_ref[...] += jnp.sum(dkv, axis=0, keepdims=True)
            dkv_ref[...] = (carry_ref[...] + dkv[:L, :]).astype(dkv_ref.dtype)
            carry_ref[...] = dkv[L:, :]

        @pl.when(i == nb)
        def _():
            dkv_ref[...] = carry_ref[...].astype(dkv_ref.dtype)

    c = lambda i: jnp.minimum(i, nb - 1)
    prev = lambda col: pl.BlockSpec((L, 128), lambda i: (jnp.maximum(c(i) - 1, 0), col))
    cur = lambda col: pl.BlockSpec((L, 128), lambda i: (c(i), col))
    row = lambda w: pl.BlockSpec((L, w), lambda i: (c(i), 0))
    cube = pl.BlockSpec((16, L, 2 * L), lambda i: (0, 0, 0))
    vec = lambda w: pl.BlockSpec((1, w), lambda i: (0, 0))
    return pl.pallas_call(
        body, name="attn_bwd", grid=(nb + 1,),
        in_specs=[SMEM, row(1024), cur(8), cur(9), prev(8), prev(9), row(1024), row(128), cube],
        out_specs=[row(1024), pl.BlockSpec((L, 256), lambda i: (jnp.maximum(i - 1, 0), 0)), cube,
                   vec(128), vec(1024), vec(256)],
        out_shape=[jax.ShapeDtypeStruct((S, 1024), BF16), jax.ShapeDtypeStruct((S, 256), BF16),
                   jax.ShapeDtypeStruct((16, L, 2 * L), F32), jax.ShapeDtypeStruct((1, 128), F32),
                   jax.ShapeDtypeStruct((1, 1024), F32), jax.ShapeDtypeStruct((1, 256), F32)],
        scratch_shapes=[pltpu.VMEM((L, 256), F32)],
        compiler_params=_params(("arbitrary",)))(sinks, qkv, qkv, qkv, qkv, qkv, d_o, lse, bias)


def _pad_lanes(a, n=128):
    return jnp.pad(a, ((0, 0), (0, n - a.shape[1])))


def _local_step(x, tgt, mod, w_in, P, io):
    md = [[mod[l:l + 1, k * D:(k + 1) * D] for k in range(6)] for l in range(2)]
    G, g = {}, {}

    sh1, sc1, g1, sh2, sc2, g2 = md[0]
    nmw0, nfw0 = P["norm_mix_w"][0:1], P["norm_ffn_w"][0:1]
    h0 = _norm_mod_fwd("norm_mix_0", x, nmw0, sc1, sh1, after=io["start"])
    segs = {"z": w_in[0:1024], "xbc": w_in[1024:2560], "dt": jnp.pad(w_in[2560:2576], ((0, 112), (0, 0))),
            "u": w_in[2576:3600], "v": w_in[3600:4624]}
    proj = dict(zip(segs, _mm_shared_lhs("in_proj", h0, list(segs.values()))))
    conv_w, conv_b = P["conv_w"][0], P["conv_b"]
    pre, xc = _conv_fwd(proj["xbc"], conv_w, conv_b)
    dtb, alog = _pad_lanes(P["dt_bias"]), _pad_lanes(P["a_log"])
    dskl = jnp.repeat(P["d_skip"], 64, axis=1)
    ya, y_ssd, prev = _ssd_fwd(xc, proj["dt"], proj["z"], dtb, alog, dskl, P["ssm_norm_w"])
    ws = P["gmlp_ws"][0]
    bse = jnp.broadcast_to(P["gmlp_bs"][0][:, :, None], (8, L, 128))
    yb = _gmlp_fwd(proj["u"], proj["v"], P["gmlp_ln_w"], P["gmlp_ln_b"], ws, bse)
    W = dict(io["weights0"]((ya, yb)))
    w_oa, w_ob = W["out_w"][:1024], W["out_w"][1024:]

    def res(y, x, gate):
        return y, x + gate * y
    mix0, x1 = _mm("out_proj_0", [ya, yb], [w_oa, w_ob], "nn", [F32, F32], epi=res, extras=[x], vecs=[g1])
    h0f = _norm_mod_fwd("norm_ffn_0", x1, nfw0, sc2, sh2)
    a0, b0, f0, y0, x2 = _ffn_fwd("0", h0f, W["gate_wt0"], W["up_wt0"], W["down_w0"], x1, g2)

    sh1b, sc1b, g1b, sh2b, sc2b, g2b = md[1]
    nmw1, nfw1 = P["norm_mix_w"][1:2], P["norm_ffn_w"][1:2]
    W.update(io["weights1"](x2))
    h1 = _norm_mod_fwd("norm_mix_1", x2, nmw1, sc1b, sh1b)
    qkv = _mm("qkv_proj", [h1], [W["qkv_wt"]], "nt", [F32], epi=lambda acc, b: acc + b, vecs=[P["qkv_b"]])[0]
    onehot_t = jnp.asarray(_bucket_onehot_t())
    bias = _rel_bias(P["rel_table"].T, onehot_t).reshape(16, L, 2 * L)
    sinks = P["sinks"].reshape(16)
    att, lse = _attn_fwd(qkv, bias, sinks)

    def res_b(y, x, gate, b):
        y = y + b
        return y, x + gate * y
    mix1, x3 = _mm("o_proj", [att], [W["o_w"]], "nn", [F32, F32], epi=res_b, extras=[x2], vecs=[g1b, P["o_b"]])
    h1f = _norm_mod_fwd("norm_ffn_1", x3, nfw1, sc2b, sh2b)
    a1, b1, f1, y1, x4 = _ffn_fwd("1", h1f, W["gate_wt1"], W["up_wt1"], W["down_w1"], x3, g2b)

    dx, sq, g["final_norm_w"] = _loss_head(x4, tgt, P["final_norm_w"])

    dh, dg2b, dwg1, dwu1, dwd1 = _ffn_bwd("1", dx, h1f, a1, b1, f1, y1, W["gate_wt1"], W["up_wt1"],
                                          W["down_w1"], g2b)
    dx, dsh2b, dsc2b, dnfw1 = _norm_mod_bwd("norm_ffn_bwd_1", x3, dh, dx, nfw1, sc2b)
    dmix, dg1b, g["o_b"] = _gate_bwd("mix_gate_bwd_1", dx, mix1, g1b)
    G["o_w"] = _mm_tn("o_dw", att, dmix)
    d_att = _mm("o_dx", [dmix], [W["o_w"]], "nt", [F32])[0]
    dq, dkv, dbias, dsinks, dbq, dbkv = _attn_bwd(qkv, d_att, lse, bias, sinks)
    g["rel_table"] = _rel_bias_bwd(dbias.reshape(16, L * 2 * L), onehot_t).T
    g["sinks"] = dsinks[:, :16]
    g["qkv_b"] = jnp.concatenate([dbq, dbkv], axis=1)
    w_q, w_kv = W["qkv_wt"][:1024], W["qkv_wt"][1024:]
    G["qkv_wt"] = jnp.concatenate([_mm_tn("qkv_dwq", dq, h1), _mm_tn("qkv_dwkv", dkv, h1)], axis=0)
    dh = _mm("qkv_dx", [dq, dkv], [w_q, w_kv], "nn", [F32])[0]
    dx, dsh1b, dsc1b, dnmw1 = _norm_mod_bwd("norm_mix_bwd_1", x2, dh, dx, nmw1, sc1b)
    behind = io["grads1"]({"qkv_wt": G.pop("qkv_wt"), "o_w": G.pop("o_w"), "gate_wt1": dwg1, "up_wt1": dwu1,
                           "down_w1": dwd1})

    dh, dg2, dwg0, dwu0, dwd0 = _ffn_bwd("0", dx, h0f, a0, b0, f0, y0, W["gate_wt0"], W["up_wt0"],
                                         W["down_w0"], g2, after=behind)
    behind = io["grads_ffn0"]({"gate_wt0": dwg0, "up_wt0": dwu0, "down_w0": dwd0})
    dx, dsh2, dsc2, dnfw0 = _norm_mod_bwd("norm_ffn_bwd_0", x1, dh, dx, nfw0, sc2, after=behind)
    dmix, dg1, _ = _gate_bwd("mix_gate_bwd_0", dx, mix0, g1)
    G["out_w"] = jnp.concatenate([_mm_tn("out_dwa", ya, dmix), _mm_tn("out_dwb", yb, dmix)], axis=0)
    dya = _mm("out_dxa", [dmix], [w_oa], "nt", [F32])[0]
    dyb = _mm("out_dxb", [dmix], [w_ob], "nt", [F32])[0]
    du, dv, dws, dbse, g["gmlp_ln_w"], g["gmlp_ln_b"] = _gmlp_bwd(dyb, proj["u"], proj["v"], P["gmlp_ln_w"],
                                                                 P["gmlp_ln_b"], ws, bse)
    g["gmlp_ws"] = dws[None]
    g["gmlp_bs"] = _lane_sum("gmlp_dbs", dbse.reshape(8 * L, 128)).reshape(1, 8, L)
    dz, dxc, ddt, g["ssm_norm_w"], ddsk, dalog, ddtb = _ssd_bwd(dya, y_ssd, proj["z"], xc, proj["dt"], prev,
                                                                dtb, alog, dskl, P["ssm_norm_w"])
    g["d_skip"], g["a_log"], g["dt_bias"] = ddsk[0:1, :16], dalog[:, :16], ddtb[:, :16]
    dxr, dconv_w, g["conv_b"] = _conv_bwd(dxc, pre, proj["xbc"], conv_w)
    g["conv_w"] = dconv_w[None]
    dsegs = {"z": dz, "xbc": dxr, "dt": ddt, "u": du, "v": dv}
    dws_in = {k: _mm_tn(f"in_dw_{k}", d, h0) for k, d in dsegs.items()}
    G["in_wt"] = jnp.concatenate([dws_in["z"], dws_in["xbc"], dws_in["dt"][:16], dws_in["u"], dws_in["v"]], axis=0)
    keys = ["z", "xbc", "dt", "u", "v"]
    dh = _mm("in_dx", [dsegs[k] for k in keys], [segs[k] for k in keys], "nn", [F32])[0]
    dx, dsh1, dsc1, dnmw0 = _norm_mod_bwd("norm_mix_bwd_0", x, dh, dx, nmw0, sc1)

    g["norm_mix_w"] = jnp.concatenate([dnmw0, dnmw1], axis=0)
    g["norm_ffn_w"] = jnp.concatenate([dnfw0, dnfw1], axis=0)
    dmod = jnp.concatenate([jnp.concatenate([dsh1, dsc1, dg1, dsh2, dsc2, dg2], axis=1),
                            jnp.concatenate([dsh1b, dsc1b, dg1b, dsh2b, dsc2b, dg2b], axis=1)], axis=0)
    return sq, dx, dmod, G, g


def _ada_fwd(c_all, ada_w, ada_b):
    n = ada_w.shape[2]
    tn = _col_tile(n, 512)

    def body(c_ref, w_ref, b_ref, o_ref):
        cc = c_ref[...]
        o_ref[...] = lax.dot_general(cc * _sigmoid(cc), w_ref[...], NN, precision=lax.Precision.HIGHEST,
                                     preferred_element_type=F32) + b_ref[...]

    return pl.pallas_call(
        body, name="ada_fwd", grid=(2, n // tn),
        in_specs=[pl.BlockSpec((8, D), lambda l, j: (0, 0)), pl.BlockSpec((None, D, tn), lambda l, j: (l, 0, j)),
                  pl.BlockSpec((None, 1, tn), lambda l, j: (l, 0, j))],
        out_specs=pl.BlockSpec((None, 8, tn), lambda l, j: (l, 0, j)),
        out_shape=jax.ShapeDtypeStruct((2, 8, n), F32), compiler_params=_params(("parallel", "parallel")))(
            c_all, ada_w, ada_b)


def _ada_bwd(c_all, dmod_cols, dmod_all):
    n = dmod_cols.shape[2]
    tn = _col_tile(n, 512)

    def body(c_ref, d_ref, o_ref):
        cc = c_ref[...]
        o_ref[...] = lax.dot_general(cc * _sigmoid(cc), d_ref[...], TN, precision=lax.Precision.HIGHEST,
                                     preferred_element_type=F32)

    dw = pl.pallas_call(
        body, name="ada_dw", grid=(2, n // tn),
        in_specs=[pl.BlockSpec((8, D), lambda l, j: (0, 0)), pl.BlockSpec((None, 8, tn), lambda l, j: (l, 0, j))],
        out_specs=pl.BlockSpec((None, D, tn), lambda l, j: (l, 0, j)),
        out_shape=jax.ShapeDtypeStruct((2, D, n), F32), compiler_params=_params(("parallel", "parallel")))(
            c_all, dmod_cols)

    def sum_body(d_ref, o_ref):
        o_ref[...] = jnp.sum(d_ref[...], axis=0, keepdims=True)

    db = pl.pallas_call(
        sum_body, name="ada_db", grid=(2,),
        in_specs=[pl.BlockSpec((None, 8, 6 * D), lambda l: (l, 0, 0))],
        out_specs=pl.BlockSpec((None, 1, 6 * D), lambda l: (l, 0, 0)),
        out_shape=jax.ShapeDtypeStruct((2, 1, 6 * D), F32), compiler_params=_params(("parallel",)))(dmod_all)
    return dw, db


def _row_tile(rows, cap=512, mult=8):
    best = rows
    for t in range(mult, min(rows, cap) + 1, mult):
        if rows % t == 0:
            best = t
    return best


def _adamw(name, w, g, m, v):
    def fn(w, g, m, v):
        m = ADAM_B1 * m + (1.0 - ADAM_B1) * g
        v = ADAM_B2 * v + (1.0 - ADAM_B2) * (g * g)
        m_hat = m / (1.0 - ADAM_B1 ** ADAM_STEP)
        v_hat = v / (1.0 - ADAM_B2 ** ADAM_STEP)
        return -ADAM_LR * (m_hat / (jnp.sqrt(v_hat) + ADAM_EPS) + ADAM_WD * w), m, v
    cols = w.shape[1]
    return _rowwise(name, fn, [w, g, m, v], [], [(cols, F32)] * 3, tr=_row_tile(w.shape[0]))


def _place():
    return lax.axis_index("x"), lax.axis_index("y"), lax.axis_index("c")


VMEM_SPEC = pl.BlockSpec(memory_space=pltpu.VMEM)


def _allreduce_small(name, buf, after=None):
    rows = buf.shape[0]
    deps = [] if after is None else [after]

    def body(x_ref, *rest):
        o_ref, stage, send_sems, recv_sems = rest[len(deps):]
        x, y, c = _place()
        me = 4 * x + 2 * y + c
        stage[me] = x_ref[...]
        copies = []
        for k in range(1, 8):
            peer = (1 - x if k & 4 else x, 1 - y if k & 2 else y, 1 - c if k & 1 else c)
            cp = pltpu.make_async_remote_copy(src_ref=x_ref, dst_ref=stage.at[me], send_sem=send_sems.at[k - 1],
                                              recv_sem=recv_sems.at[k - 1], device_id=peer, device_id_type=MESH)
            cp.start()
            copies.append(cp)
        for cp in copies:
            cp.wait()
        acc = stage[0]
        for d in range(1, 8):
            acc = acc + stage[d]
        o_ref[...] = acc

    return pl.pallas_call(
        body, name=name, in_specs=[VMEM_SPEC] + [ANY for _ in deps], out_specs=VMEM_SPEC,
        out_shape=jax.ShapeDtypeStruct((rows, 128), F32),
        scratch_shapes=[pltpu.VMEM((8, rows, 128), F32), pltpu.SemaphoreType.DMA((7,)), pltpu.SemaphoreType.DMA((7,))],
        compiler_params=pltpu.CompilerParams(vmem_limit_bytes=_VMEM_LIMIT))(buf, *deps)


def _sum_slots(name, own, land):
    def body(own_ref, land_ref, o_ref):
        x, y, c = _place()
        me = 4 * x + 2 * y + c
        acc = None
        for d in range(8):
            v = jnp.where(me == d, own_ref[...], land_ref[d])
            acc = v if acc is None else acc + v
        o_ref[...] = acc

    return pl.pallas_call(body, name=name, in_specs=[VMEM_SPEC, VMEM_SPEC], out_specs=VMEM_SPEC,
                          out_shape=jax.ShapeDtypeStruct(own.shape, F32),
                          compiler_params=pltpu.CompilerParams(vmem_limit_bytes=_VMEM_LIMIT))(own, land)


OTHER_CHIPS = ((1, 0), (0, 1), (1, 1))


SIBLING_COLLECTIVE_ID = 6


def _sibling_handshake():
    x, y, c = _place()
    barrier = pltpu.get_barrier_semaphore()
    pl.semaphore_signal(barrier, inc=1, device_id=(x, y, 1 - c), device_id_type=MESH)
    pl.semaphore_wait(barrier, 1)


def _sibling_swap(name, src, halves):
    half = src.shape[-2] // 2
    out_shape = (src.shape[0], half, 1024) if halves else src.shape

    def body(s_ref, o_ref, send_sem, recv_sem):
        x, y, c = _place()
        _sibling_handshake()
        part = s_ref.at[:, pl.ds(pl.multiple_of((1 - c) * half, 8), half)] if halves else s_ref
        cp = pltpu.make_async_remote_copy(src_ref=part, dst_ref=o_ref, send_sem=send_sem, recv_sem=recv_sem,
                                          device_id=(x, y, 1 - c), device_id_type=MESH)
        cp.start()
        cp.wait()

    return pl.pallas_call(
        body, name=name, in_specs=[ANY], out_specs=ANY, out_shape=jax.ShapeDtypeStruct(out_shape, src.dtype),
        scratch_shapes=[pltpu.SemaphoreType.DMA, pltpu.SemaphoreType.DMA],
        compiler_params=pltpu.CompilerParams(collective_id=SIBLING_COLLECTIVE_ID))(src)


HBM = pl.BlockSpec(memory_space=pltpu.HBM)
SEM = pl.BlockSpec(memory_space=pltpu.SEMAPHORE)


def _exchange_peers(mode):
    x, y, c = _place()
    if mode == "all":
        return [(1 - x if k & 4 else x, 1 - y if k & 2 else y, 1 - c if k & 1 else c) for k in range(1, 8)]
    return [(1 - x if fx else x, 1 - y if fy else y, c) for fx, fy in OTHER_CHIPS]


def _chip_copies(mode, src_ref, land_ref, send_sems, recv_sems):
    x, y, c = _place()
    k = 2 * x + y
    copies = []
    for j, peer in enumerate(_exchange_peers(mode)):
        if mode == "gather":
            half = src_ref.shape[0] // 2
            mine = pl.ds(pl.multiple_of(c * half, 16), half)
            src, dst = src_ref.at[mine], land_ref.at[k, mine]
        elif mode == "scatter":
            src, dst = src_ref.at[2 * peer[0] + peer[1]], land_ref.at[k]
        else:
            src, dst = src_ref, land_ref.at[4 * x + 2 * y + c]
        copies.append(pltpu.make_async_remote_copy(src_ref=src, dst_ref=dst, send_sem=send_sems.at[j],
                                                   recv_sem=recv_sems.at[j], device_id=peer, device_id_type=MESH))
    return copies


def _exchange_start(name, collective_id, mode, src, land, after=None):
    deps = [] if after is None else [after]
    npeers = 7 if mode == "all" else 3

    def body(s_ref, l_ref, *rest):
        send_sems, recv_sems, s_thru, l_thru, token = rest[len(deps):]
        barrier = pltpu.get_barrier_semaphore()
        for peer in _exchange_peers(mode):
            pl.semaphore_signal(barrier, inc=1, device_id=peer, device_id_type=MESH)
        pl.semaphore_wait(barrier, npeers)
        for cp in _chip_copies(mode, s_ref, l_ref, send_sems, recv_sems):
            cp.start()
        token[...] = jnp.zeros_like(token)

    return pl.pallas_call(
        body, name=name,
        out_shape=(pltpu.SemaphoreType.DMA((npeers,)), pltpu.SemaphoreType.DMA((npeers,)),
                   pltpu.HBM(src.shape, src.dtype),
                   pltpu.HBM(land.shape, land.dtype), jax.ShapeDtypeStruct((8, 128), F32)),
        in_specs=(HBM, HBM) + tuple(ANY for _ in deps), out_specs=(SEM, SEM, HBM, HBM, VMEM_SPEC),
        input_output_aliases={0: 2, 1: 3},
        compiler_params=pltpu.CompilerParams(has_side_effects=pltpu.SideEffectType.DATAFLOW_SIDE_EFFECTING,
                                             collective_id=collective_id))(
            pltpu.with_memory_space_constraint(src, pltpu.HBM), pltpu.with_memory_space_constraint(land, pltpu.HBM),
            *deps)


def _exchange_wait(name, mode, started, after):
    send_sems, recv_sems, s_thru, l_thru, _ = started
    deps = list(after) if isinstance(after, (tuple, list)) else [after]

    def body(s_ref, l_ref, send_sems, recv_sems, *rest):
        for cp in _chip_copies(mode, s_ref, l_ref, send_sems, recv_sems):
            cp.wait_send()
            cp.wait_recv()

    return pl.pallas_call(
        body, name=name, out_shape=(pltpu.HBM(s_thru.shape, s_thru.dtype), pltpu.HBM(l_thru.shape, l_thru.dtype)),
        in_specs=(HBM, HBM, SEM, SEM) + tuple(ANY for _ in deps), out_specs=(HBM, HBM),
        input_output_aliases={0: 0, 1: 1},
        compiler_params=pltpu.CompilerParams(has_side_effects=pltpu.SideEffectType.DATAFLOW_SIDE_EFFECTING))(
            s_thru, l_thru, send_sems, recv_sems, *deps)


def _allgather_finish(tag, land):
    half = land.shape[1] // 2

    def body(l_ref, o_ref, send_sem, recv_sem):
        x, y, c = _place()
        _sibling_handshake()
        mine = pl.ds(pl.multiple_of(c * half, 16), half)
        swap = pltpu.make_async_remote_copy(src_ref=o_ref.at[:, mine], dst_ref=o_ref.at[:, mine], send_sem=send_sem,
                                            recv_sem=recv_sem, device_id=(x, y, 1 - c), device_id_type=MESH)
        swap.start()
        swap.wait()

    return pl.pallas_call(
        body, name="allgather_finish_" + tag, in_specs=[ANY], out_specs=ANY, input_output_aliases={0: 0},
        out_shape=jax.ShapeDtypeStruct(land.shape, land.dtype),
        scratch_shapes=[pltpu.SemaphoreType.DMA, pltpu.SemaphoreType.DMA],
        compiler_params=pltpu.CompilerParams(collective_id=SIBLING_COLLECTIVE_ID))(land)


def _pair_sum(tag, g, r1, c):
    rows = g.shape[1]
    half = rows // 2
    th = _row_tile(half, 256, 16)
    nblk = half // th

    def body(c_ref, g_ref, r_ref, o_ref, o2_ref):
        o_ref[...] = (g_ref[...] + r_ref[...]).astype(o_ref.dtype)
        o2_ref[...] = o_ref[...]

    spec = pl.BlockSpec((None, th, 1024), lambda k, i, c_ref: (k, i, 0))
    grid_spec = pltpu.PrefetchScalarGridSpec(
        num_scalar_prefetch=1, grid=(4, nblk),
        in_specs=[pl.BlockSpec((None, th, 1024), lambda k, i, c_ref: (k, c_ref[0] * nblk + i, 0)), spec],
        out_specs=[spec, spec])
    return pl.pallas_call(body, name="grad_pair_sum_" + tag, grid_spec=grid_spec,
                          out_shape=[jax.ShapeDtypeStruct((4, half, 1024), BF16)] * 2,
                          compiler_params=_params(("parallel", "parallel")))(c, g, r1)


def _chip_sum(tag, q, after=None):
    half = q.shape[1]
    th = _row_tile(half, 256, 16)
    deps = [] if after is None else [after]

    def body(a, b, c, d, *rest):
        rest[-1][...] = ((a[...].astype(F32) + b[...].astype(F32)) + c[...].astype(F32)) + d[...].astype(F32)

    specs = [pl.BlockSpec((None, th, 1024), functools.partial(lambda i, k: (k, i, 0), k=k)) for k in range(4)]
    return pl.pallas_call(body, name="grad_chip_sum_" + tag, grid=(half // th,), in_specs=specs + [ANY for _ in deps],
                          out_specs=pl.BlockSpec((th, 1024), lambda i: (i, 0)),
                          out_shape=jax.ShapeDtypeStruct((half, 1024), F32),
                          compiler_params=_params(("parallel",)))(q, q, q, q, *deps)


def _join_halves(tag, f, r, c):
    half = f.shape[0]
    th = _row_tile(half, 256)
    nblk = half // th

    def body(c_ref, f_ref, r_ref, o_ref):
        mine = (pl.program_id(0) == c_ref[0])
        o_ref[...] = jnp.where(mine, f_ref[...], r_ref[...])

    spec = pl.BlockSpec((th, 1024), lambda h, i, c_ref: (i, 0))
    grid_spec = pltpu.PrefetchScalarGridSpec(
        num_scalar_prefetch=1, grid=(2, nblk), in_specs=[spec, spec],
        out_specs=pl.BlockSpec((th, 1024), lambda h, i, c_ref: (h * nblk + i, 0)))
    return pl.pallas_call(body, name="grad_join_halves_" + tag, grid_spec=grid_spec,
                          out_shape=jax.ShapeDtypeStruct((2 * half, 1024), F32),
                          compiler_params=_params(("parallel", "parallel")))(c, f, r)


BIG_ARGS = ("in_w_even", "out_w_even", "qkv_w", "o_w", "ffn_gate_w", "ffn_up_w", "ffn_down_w")
def _ffn_pieces(layer):
    return tuple((f"{n}{layer}", 704, 704) for n in ("gate_wt", "up_wt", "down_w"))


IN_SLAB = (("in_wt", 1156, 1184),)
LAYER0_REST_SLAB = (("out_w", 512, 512),) + _ffn_pieces(0)
LAYER1_SLAB = (("qkv_wt", 320, 320), ("o_w", 256, 256)) + _ffn_pieces(1)
FFN0_SLAB = _ffn_pieces(0)
MIXER0_SLAB = (("in_wt", 1156, 1280), ("out_w", 512, 512))


def _slab(pieces, spec):
    parts = []
    for name, rows, room in spec:
        p = pieces[name]
        parts.append(jnp.pad(p, [(0, 0)] * (p.ndim - 2) + [(0, room - rows), (0, 0)]) if room > rows else p)
    return jnp.concatenate(parts, axis=-2) if len(parts) > 1 else parts[0]


def _unslab(slab, spec):
    out, off = {}, 0
    for name, rows, room in spec:
        out[name] = slab[..., off:off + rows, :]
        off += room
    return out


def _share_pieces(w):
    return {"in_wt": w["in_w_even"][0].T, "out_w": w["out_w_even"][0], "qkv_wt": w["qkv_w"][0].T, "o_w": w["o_w"][0],
            "gate_wt0": w["ffn_gate_w"][0].T, "gate_wt1": w["ffn_gate_w"][1].T,
            "up_wt0": w["ffn_up_w"][0].T, "up_wt1": w["ffn_up_w"][1].T,
            "down_w0": w["ffn_down_w"][0], "down_w1": w["ffn_down_w"][1]}


def _pieces_to_shares(p):
    return {"in_w_even": p["in_wt"].T[None], "out_w_even": p["out_w"][None], "qkv_w": p["qkv_wt"].T[None],
            "o_w": p["o_w"][None], "ffn_gate_w": jnp.stack([p["gate_wt0"].T, p["gate_wt1"].T]),
            "ffn_up_w": jnp.stack([p["up_wt0"].T, p["up_wt1"].T]),
            "ffn_down_w": jnp.stack([p["down_w0"], p["down_w1"]])}


def _chips_from_full(G, spec):
    return _slab({k: v.reshape(4, -1, D) for k, v in G.items()}, spec)


def _pack_small(parts):
    padded = []
    for p in parts:
        p = p.reshape(-1).astype(F32)
        padded.append(jnp.pad(p, (0, (-p.shape[0]) % 1024)))
    return jnp.concatenate(padded).reshape(-1, 128)


def _unpack_small(slab, shapes):
    flat, out, off = slab.reshape(-1), [], 0
    for shp in shapes:
        size = math.prod(shp)
        out.append(flat[off:off + size].reshape(shp))
        off += size + (-size) % 1024
    return out


SMALL = ("ada_b", "norm_mix_w", "norm_ffn_w", "conv_w", "conv_b", "dt_bias", "a_log", "d_skip", "ssm_norm_w",
         "gmlp_ln_w", "gmlp_ln_b", "gmlp_ws", "gmlp_bs", "qkv_b", "o_b", "sinks", "rel_table", "final_norm_w")
SMALL_SPLIT = {"conv_w": 1536, "qkv_b": 1280, "o_b": 1024}
WEIGHTS = ("ada_w", "ada_b", "norm_mix_w", "norm_ffn_w", "in_w_even", "conv_w", "conv_b", "dt_bias", "a_log", "d_skip",
           "ssm_norm_w", "gmlp_ln_w", "gmlp_ln_b", "gmlp_ws", "gmlp_bs", "out_w_even", "qkv_w", "qkv_b", "o_w", "o_b",
           "sinks", "rel_table", "ffn_gate_w", "ffn_up_w", "ffn_down_w", "final_norm_w")


def kernel(x, c, ada_w, ada_b, norm_mix_w, norm_ffn_w, in_w_even, conv_w, conv_b, dt_bias, a_log, d_skip, ssm_norm_w, gmlp_ln_w, gmlp_ln_b, gmlp_ws, gmlp_bs, out_w_even, qkv_w, qkv_b, o_w, o_b, sinks, rel_table, ffn_gate_w, ffn_up_w, ffn_down_w, final_norm_w, loss_target, m_ada_w, m_ada_b, m_norm_mix_w, m_norm_ffn_w, m_in_w_even, m_conv_w, m_conv_b, m_dt_bias, m_a_log, m_d_skip, m_ssm_norm_w, m_gmlp_ln_w, m_gmlp_ln_b, m_gmlp_ws, m_gmlp_bs, m_out_w_even, m_qkv_w, m_qkv_b, m_o_w, m_o_b, m_sinks, m_rel_table, m_ffn_gate_w, m_ffn_up_w, m_ffn_down_w, m_final_norm_w, v_ada_w, v_ada_b, v_norm_mix_w, v_norm_ffn_w, v_in_w_even, v_conv_w, v_conv_b, v_dt_bias, v_a_log, v_d_skip, v_ssm_norm_w, v_gmlp_ln_w, v_gmlp_ln_b, v_gmlp_ws, v_gmlp_bs, v_out_w_even, v_qkv_w, v_qkv_b, v_o_w, v_o_b, v_sinks, v_rel_table, v_ffn_gate_w, v_ffn_up_w, v_ffn_down_w, v_final_norm_w):
    args = dict(locals())
    w = {n: args[n] for n in WEIGHTS}
    m = {n: args["m_" + n] for n in WEIGHTS}
    v = {n: args["v_" + n] for n in WEIGHTS}
    ax, ay, ac = _place()
    me = 4 * ax + 2 * ay + ac
    chip = 2 * ax + ay
    south = (ac == 0).astype(F32)
    c_arr = jnp.reshape(ac, (1,)).astype(jnp.int32)

    c_all = _allreduce_small("gather_cond", lax.dynamic_update_slice(jnp.zeros((8, D), F32), c, (me, 0)).reshape(64, 128))
    c_all = c_all.reshape(8, D)
    n_ada = ada_w.shape[2]
    mod_cols = _ada_fwd(c_all, ada_w, lax.dynamic_slice(ada_b, (0, chip * n_ada), (2, n_ada)).reshape(2, 1, n_ada))
    pieces = [lax.dynamic_update_slice(jnp.zeros((2, 8, 6 * D), F32), mod_cols, (0, 0, chip * n_ada))]
    split_names = list(SMALL_SPLIT)
    for n in split_names:
        full = SMALL_SPLIT[n]
        local = w[n]
        idx = (0,) * (local.ndim - 1) + (chip * local.shape[-1],)
        pieces.append(lax.dynamic_update_slice(jnp.zeros(local.shape[:-1] + (full,), F32), local, idx))
    shapes = [p.shape for p in pieces]
    mod_slab = _allreduce_small("gather_mod", _pack_small(pieces) * south)
    gathered = _unpack_small(mod_slab, shapes)
    mod = lax.dynamic_slice(gathered[0], (0, me, 0), (2, 1, 6 * D)).reshape(2, 6 * D)
    P = {n: w[n] for n in SMALL if n not in SMALL_SPLIT and n != "ada_b"}
    for n, full in zip(split_names, gathered[1:]):
        P[n] = full
    P["final_norm_w"] = final_norm_w.reshape(1, D)

    pieces = _share_pieces(w)
    cast = {"in_wt": pieces["in_wt"].astype(_MXU)}

    def start_gather(tag, collective_id, share, after):
        return _exchange_start("allgather_start_" + tag, collective_id, "gather", share,
                               lax.empty((4,) + share.shape, share.dtype), after=after)

    def finish_gather(tag, started, spec, after):
        land = _exchange_wait("allgather_wait_" + tag, "gather", started, after)[1]
        out = {}
        for name, piece in _unslab(_allgather_finish(tag, land), spec).items():
            out[name] = lax.dynamic_update_slice(piece.reshape(-1, D), cast[name], (chip * piece.shape[1], 0))
        return out

    gather_in = start_gather("in", 7, _slab(cast, IN_SLAB), mod_slab)
    zero = gather_in[4][0, 0]
    cast.update({k: (p + zero).astype(_MXU) for k, p in pieces.items() if k != "in_wt"})
    share0, share1 = _slab(cast, LAYER0_REST_SLAB), _slab(cast, LAYER1_SLAB)
    w_in = finish_gather("in", gather_in, IN_SLAB, (share0, share1))["in_wt"]
    gather0 = start_gather("0", 1, share0, w_in)
    gather1 = start_gather("1", 2, share1, gather0[4])

    def start_reduce(tag, collective_id, G, spec, after=None):
        gp = _chips_from_full(G, spec)
        p, q = _pair_sum(tag, gp, _sibling_swap("grad_pair_exchange_" + tag, gp, True), c_arr)
        return _exchange_start("grad_exchange_start_" + tag, collective_id, "scatter", p, q, after=after)

    def finish_reduce(tag, started, spec, after, behind=None):
        q = _exchange_wait("grad_exchange_wait_" + tag, "scatter", started, after)[1]
        fin = _chip_sum(tag, q, after=behind)
        total = _join_halves(tag, fin, _sibling_swap("grad_final_exchange_" + tag, fin, False), c_arr)
        return _unslab(total, spec)

    reduces = {}

    def grads1(G1):
        reduces["1"] = start_reduce("1", 3, G1, LAYER1_SLAB)
        return reduces["1"][4]

    def grads_ffn0(G):
        reduces["f"] = start_reduce("f", 4, G, FFN0_SLAB)
        return reduces["f"][4]

    io = {"start": gather1[4],
          "weights0": lambda after: finish_gather("0", gather0, LAYER0_REST_SLAB, after),
          "weights1": lambda after: finish_gather("1", gather1, LAYER1_SLAB, after),
          "grads1": grads1, "grads_ffn0": grads_ffn0}
    sq, grad_x, dmod, G0, g = _local_step(x[0], loss_target[0], mod, w_in, P, io)
    loss = lax.psum(0.5 * sq[0, 0] / D, ("x", "y", "c"))

    g["final_norm_w"] = g["final_norm_w"].reshape(D)
    small_names = [n for n in SMALL if n != "ada_b"]
    pieces = [lax.dynamic_update_slice(jnp.zeros((2, 8, 6 * D), F32), dmod.reshape(2, 1, 6 * D), (0, me, 0))]
    pieces += [g[n] for n in small_names]
    shapes = [p.shape for p in pieces]
    small_own = _pack_small(pieces)
    small_started = _exchange_start("small_grads_start", 8, "all", small_own, lax.empty((8,) + small_own.shape, F32))
    reduces["m"] = start_reduce("m", 5, G0, MIXER0_SLAB, after=small_started[4])
    shares = finish_reduce("1", reduces["1"], LAYER1_SLAB, grad_x, behind=reduces["m"][4])
    shares.update(finish_reduce("f", reduces["f"], FFN0_SLAB, grad_x, behind=reduces["m"][4]))
    small_own, small_land = _exchange_wait("small_grads_wait", "all", small_started, shares["down_w0"])
    reduced = _unpack_small(_sum_slots("small_grads_sum", small_own, small_land), shapes)
    dmod_all = reduced[0]
    grads = dict(zip(small_names, reduced[1:]))
    for n in split_names:
        full = grads[n]
        size = w[n].shape[-1]
        grads[n] = lax.dynamic_slice(full, (0,) * (full.ndim - 1) + (chip * size,), full.shape[:-1] + (size,))
    grads = {n: grads[n].reshape(w[n].shape) for n in small_names}
    dw_ada, db_ada = _ada_bwd(c_all, lax.dynamic_slice(dmod_all, (0, 0, chip * n_ada), (2, 8, n_ada)), dmod_all)
    grads["ada_w"], grads["ada_b"] = dw_ada, db_ada.reshape(2, 6 * D)

    delta, new_m, new_v = {}, {}, {}

    def update(n):
        cols = w[n].shape[-1]
        d_, m_, v_ = _adamw("adamw_" + n, w[n].reshape(-1, cols), grads[n].reshape(-1, cols), m[n].reshape(-1, cols),
                            v[n].reshape(-1, cols))
        delta[n], new_m[n], new_v[n] = d_.reshape(w[n].shape), m_.reshape(w[n].shape), v_.reshape(w[n].shape)

    update("ada_w")
    shapes = [w[n].shape for n in SMALL]
    packed = [_pack_small([t[n] for n in SMALL]) for t in (w, grads, m, v)]
    outs = _adamw("adamw_small", *packed)
    for dst, slab in zip((delta, new_m, new_v), outs):
        for n, t in zip(SMALL, _unpack_small(slab, shapes)):
            dst[n] = t
    shares.update(finish_reduce("m", reduces["m"], MIXER0_SLAB, outs[0]))
    grads.update(_pieces_to_shares(shares))
    for n in BIG_ARGS:
        update(n)
    return (loss, grad_x[None], *[grads[n] for n in WEIGHTS], *[delta[n] for n in WEIGHTS],
            *[new_m[n] for n in WEIGHTS], *[new_v[n] for n in WEIGHTS])
```

```python
import functools
import math

import numpy as np
import jax
import jax.numpy as jnp
from jax import lax
from jax.experimental import pallas as pl
from jax.experimental.pallas import tpu as pltpu

F32 = jnp.float32
BF16 = jnp.bfloat16
_MXU = jnp.bfloat16
_VMEM_LIMIT = 56 * 1024 * 1024
MXU_COLS = 256
D = 1024
L = 128
NSTATE = 128
EPS = 1e-6
NEG_INF = -1e30
FFN = 2816
ADAM_LR, ADAM_B1, ADAM_B2, ADAM_EPS, ADAM_WD, ADAM_STEP = 0.001, 0.9, 0.999, 1e-08, 0.01, 10
MESH = pl.DeviceIdType.MESH
ANY = pl.BlockSpec(memory_space=pl.ANY)

NN = (((1,), (0,)), ((), ()))
NT = (((1,), (1,)), ((), ()))
TN = (((0,), (0,)), ((), ()))


def _dot(a, b, dn=NN):
    return lax.dot_general(a.astype(_MXU), b.astype(_MXU), dn, preferred_element_type=F32)


def _params(sem=None):
    return pltpu.CompilerParams(dimension_semantics=sem, vmem_limit_bytes=_VMEM_LIMIT)


def _sigmoid(x):
    return 1.0 / (1.0 + jnp.exp(-x))


def _softplus(x):
    return jnp.maximum(x, 0.0) + jnp.log(1.0 + jnp.exp(-jnp.abs(x)))


def _gelu(x):
    return 0.5 * x * (1.0 + lax.erf(x * (2.0 ** -0.5)))


def _gelu_grad(x):
    return 0.5 * (1.0 + lax.erf(x * (2.0 ** -0.5))) + x * jnp.exp(-0.5 * x * x) * (1.0 / math.sqrt(2.0 * math.pi))


def _silu_grad(a):
    sg = _sigmoid(a)
    return sg * (1.0 + a * (1.0 - sg))


def _rowwise(name, fn, rows, vecs, out_rows, out_accs=(), tr=512, after=None):
    S = rows[0].shape[0]
    tr = min(tr, S)
    assert S % tr == 0
    nr, nv, no, na = len(rows), len(vecs), len(out_rows), len(out_accs)
    deps = [] if after is None else [after]

    def body(*refs):
        ins, outs = refs[:nr + nv], refs[nr + nv + len(deps):]
        res = fn(*[r[...] for r in ins])
        if not isinstance(res, (tuple, list)):
            res = (res,)
        for k in range(no):
            outs[k][...] = res[k].astype(outs[k].dtype)
        if na:
            @pl.when(pl.program_id(0) == 0)
            def _():
                for k in range(na):
                    outs[no + k][...] = jnp.zeros_like(outs[no + k])
            for k in range(na):
                outs[no + k][...] += res[no + k]

    in_specs = [pl.BlockSpec((tr, a.shape[1]), lambda i: (i, 0)) for a in rows]
    in_specs += [pl.BlockSpec(v.shape, lambda i: (0, 0)) for v in vecs] + [ANY for _ in deps]
    out_specs = [pl.BlockSpec((tr, c), lambda i: (i, 0)) for c, _ in out_rows]
    out_specs += [pl.BlockSpec(s, lambda i: (0, 0)) for s in out_accs]
    out_shape = [jax.ShapeDtypeStruct((S, c), dt) for c, dt in out_rows]
    out_shape += [jax.ShapeDtypeStruct(s, F32) for s in out_accs]
    return pl.pallas_call(body, name=name, grid=(S // tr,), in_specs=in_specs, out_specs=out_specs,
                          out_shape=out_shape, compiler_params=_params(("arbitrary",)))(*rows, *vecs, *deps)


def _col_tile(n, cap):
    if n <= cap or n % 128:
        return n
    best = 128
    for t in range(128, cap + 1, 128):
        if n % t == 0:
            best = t
    return best


def _mm(name, As, Bs, mode, outs, epi=None, groups=None, extras=(), vecs=(), tm=512, tn_cap=1536):
    M = As[0].shape[0]
    N = Bs[0].shape[1] if mode == "nn" else Bs[0].shape[0]
    tm = min(tm, M)
    tn = _col_tile(N, tn_cap)
    assert M % tm == 0 and N % tn == 0
    npair = len(As)
    groups = groups or [0] * npair
    ng = max(groups) + 1
    nx, nv = len(extras), len(vecs)
    dn = NN if mode == "nn" else NT

    def body(*refs):
        a_refs, b_refs = refs[:npair], refs[npair:2 * npair]
        x_refs = refs[2 * npair:2 * npair + nx]
        v_refs = refs[2 * npair + nx:2 * npair + nx + nv]
        o_refs = refs[2 * npair + nx + nv:]
        step = tn if epi is None else min(tn, MXU_COLS)
        for col in range(0, tn, step):
            sl = slice(col, min(col + step, tn))
            accs = [None] * ng
            for k in range(npair):
                b = b_refs[k][:, sl] if mode == "nn" else b_refs[k][sl, :]
                d = _dot(a_refs[k][...], b, dn)
                accs[groups[k]] = d if accs[groups[k]] is None else accs[groups[k]] + d
            args = accs + [x[:, sl] for x in x_refs] + [v[:, sl] for v in v_refs]
            res = epi(*args) if epi is not None else tuple(accs)
            if not isinstance(res, (tuple, list)):
                res = (res,)
            for o, r in zip(o_refs, res):
                o[:, sl] = r.astype(o.dtype)

    in_specs = [pl.BlockSpec((tm, a.shape[1]), lambda i, j: (i, 0)) for a in As]
    if mode == "nn":
        in_specs += [pl.BlockSpec((b.shape[0], tn), lambda i, j: (0, j)) for b in Bs]
    else:
        in_specs += [pl.BlockSpec((tn, b.shape[1]), lambda i, j: (j, 0)) for b in Bs]
    in_specs += [pl.BlockSpec((tm, tn), lambda i, j: (i, j)) for _ in extras]
    in_specs += [pl.BlockSpec((1, tn), lambda i, j: (0, j)) for _ in vecs]
    out_specs = [pl.BlockSpec((tm, tn), lambda i, j: (i, j)) for _ in outs]
    out_shape = [jax.ShapeDtypeStruct((M, N), dt) for dt in outs]
    return pl.pallas_call(body, name=name, grid=(M // tm, N // tn), in_specs=in_specs, out_specs=out_specs,
                          out_shape=out_shape, compiler_params=_params(("parallel", "parallel")))(
                              *As, *Bs, *extras, *vecs)


def _mm_shared_lhs(name, A, Bs, tm=512):
    M, K = A.shape
    tm = min(tm, M)
    assert M % tm == 0
    n = len(Bs)

    def body(a_ref, *refs):
        a = a_ref[...]
        for b_ref, o_ref in zip(refs[:n], refs[n:]):
            o_ref[...] = _dot(a, b_ref[...], NT)

    return pl.pallas_call(
        body, name=name, grid=(M // tm,),
        in_specs=[pl.BlockSpec((tm, K), lambda i: (i, 0))] + [pl.BlockSpec(b.shape, lambda i: (0, 0)) for b in Bs],
        out_specs=[pl.BlockSpec((tm, b.shape[0]), lambda i: (i, 0)) for b in Bs],
        out_shape=[jax.ShapeDtypeStruct((M, b.shape[0]), F32) for b in Bs],
        compiler_params=_params(("parallel",)))(A, *Bs)


def _mm_tn_shared_rhs(name, As, B, tk=256):
    S, N = B.shape
    tk = min(tk, S)
    assert S % tk == 0
    n = len(As)

    def body(*refs):
        a_refs, b_ref, o_refs = refs[:n], refs[n], refs[n + 1:]

        @pl.when(pl.program_id(0) == 0)
        def _():
            for o_ref in o_refs:
                o_ref[...] = jnp.zeros_like(o_ref)
        b = b_ref[...]
        for a_ref, o_ref in zip(a_refs, o_refs):
            o_ref[...] += _dot(a_ref[...], b, TN)

    return pl.pallas_call(
        body, name=name, grid=(S // tk,),
        in_specs=[pl.BlockSpec((tk, a.shape[1]), lambda k: (k, 0)) for a in As] + [pl.BlockSpec((tk, N), lambda k: (k, 0))],
        out_specs=[pl.BlockSpec((a.shape[1], N), lambda k: (0, 0)) for a in As],
        out_shape=[jax.ShapeDtypeStruct((a.shape[1], N), F32) for a in As],
        compiler_params=_params(("arbitrary",)))(*As, B)


def _mm_tn(name, A, B, tk=512, t2_cap=1536):
    S, K1 = A.shape
    N2 = B.shape[1]
    tk = min(tk, S)
    t2 = _col_tile(N2, t2_cap)
    assert S % tk == 0 and N2 % t2 == 0

    def body(a_ref, b_ref, o_ref):
        @pl.when(pl.program_id(1) == 0)
        def _():
            o_ref[...] = jnp.zeros_like(o_ref)
        o_ref[...] += _dot(a_ref[...], b_ref[...], TN)

    return pl.pallas_call(
        body, name=name, grid=(N2 // t2, S // tk),
        in_specs=[pl.BlockSpec((tk, K1), lambda j, k: (k, 0)), pl.BlockSpec((tk, t2), lambda j, k: (k, j))],
        out_specs=pl.BlockSpec((K1, t2), lambda j, k: (0, j)),
        out_shape=jax.ShapeDtypeStruct((K1, N2), F32),
        compiler_params=_params(("parallel", "arbitrary")))(A, B)


def _norm_mod_fwd(name, x, nw, sc, sh, after=None):
    def fn(x, nw, sc, sh):
        rstd = lax.rsqrt(jnp.mean(x * x, axis=-1, keepdims=True) + EPS)
        return (x * rstd * nw) * (1.0 + sc) + sh
    return _rowwise(name, fn, [x], [nw, sc, sh], [(D, BF16)], after=after)[0]


def _norm_mod_bwd(name, x, dh, dres, nw, sc, after=None):
    def fn(x, dh, dres, nw, sc):
        rstd = lax.rsqrt(jnp.mean(x * x, axis=-1, keepdims=True) + EPS)
        xh = x * rstd
        dn = dh * (1.0 + sc)
        dxh = dn * nw
        dx = rstd * (dxh - xh * jnp.mean(dxh * xh, axis=-1, keepdims=True))
        return (dres + dx, jnp.sum(dh, axis=0, keepdims=True), jnp.sum(dh * (xh * nw), axis=0, keepdims=True),
                jnp.sum(dn * xh, axis=0, keepdims=True))
    return _rowwise(name, fn, [x, dh, dres], [nw, sc], [(D, F32)], [(1, D)] * 3, after=after)


def _gate_bwd(name, dx, y, g, after=None):
    def fn(dx, y, g):
        dy = dx * g
        return dy, jnp.sum(dx * y, axis=0, keepdims=True), jnp.sum(dy, axis=0, keepdims=True)
    return _rowwise(name, fn, [dx, y], [g], [(D, BF16)], [(1, D)] * 2, after=after)


def _loss_head(x, tgt, fw):
    def fn(x, tgt, fw):
        rstd = lax.rsqrt(jnp.mean(x * x, axis=-1, keepdims=True) + EPS)
        xh = x * rstd
        err = xh * fw - tgt
        dout = err * (1.0 / D)
        dxh = dout * fw
        dx = rstd * (dxh - xh * jnp.mean(dxh * xh, axis=-1, keepdims=True))
        sq = jnp.sum(jnp.sum(err * err, axis=1, keepdims=True), axis=0, keepdims=True)
        return dx, sq, jnp.sum(dout * xh, axis=0, keepdims=True)
    return _rowwise("loss_head", fn, [x, tgt], [fw], [(D, F32)], [(1, 1), (1, D)])


def _ffn_fwd(tag, h, wg, wu, wd, x, g2):
    def act(a, b):
        return a, b, a * _sigmoid(a) * b
    a, b, f = _mm(f"ffn_up_{tag}", [h, h], [wg, wu], "nt", [BF16, BF16, BF16], epi=act, groups=[0, 1], tn_cap=1408,
                  tm=1024)

    def res(y, x, g):
        return y, x + g * y
    y, xo = _mm(f"ffn_down_{tag}", [f], [wd], "nn", [F32, F32], epi=res, extras=[x], vecs=[g2])
    return a, b, f, y, xo


def _ffn_bwd(tag, dx, h, a, b, f, y, wg, wu, wd, g2, after=None):
    dy, dg2, _ = _gate_bwd(f"ffn_gate_bwd_{tag}", dx, y, g2, after=after)

    def act_bwd(df, a, b):
        a, b = a.astype(F32), b.astype(F32)
        sg = _sigmoid(a)
        return df * b * (sg * (1.0 + a * (1.0 - sg))), df * (a * sg)
    da, db = _mm(f"ffn_dact_{tag}", [dy], [wd], "nt", [BF16, BF16], epi=act_bwd, extras=[a, b], tn_cap=1408, tm=1024)
    dwd = _mm_tn(f"ffn_dwd_{tag}", f, dy)
    dwg = _mm_tn(f"ffn_dwg_{tag}", da, h)
    dwu = _mm_tn(f"ffn_dwu_{tag}", db, h)
    dh = _mm(f"ffn_dh_{tag}", [da, db], [wg, wu], "nn", [F32])[0]
    return dh, dg2, dwg, dwu, dwd


def _conv_fwd(xr, w, b, tb=512):
    S, C = xr.shape
    tb = min(tb, S)

    def body(x_ref, halo_ref, w_ref, b_ref, pre_ref, out_ref):
        i = pl.program_id(0)
        halo = jnp.where(i > 0, halo_ref[...], 0.0)
        xe = jnp.concatenate([halo, x_ref[...]], axis=0)
        pre = w_ref[3:4, :] * x_ref[...] + b_ref[...]
        for j in (1, 2, 3):
            pre = pre + w_ref[3 - j:4 - j, :] * pltpu.roll(xe, j, axis=0)[8:, :]
        pre_ref[...] = pre
        out_ref[...] = pre * _sigmoid(pre)

    return pl.pallas_call(
        body, name="conv_fwd", grid=(S // tb,),
        in_specs=[pl.BlockSpec((tb, C), lambda i: (i, 0)),
                  pl.BlockSpec((8, C), lambda i: (jnp.maximum(i * (tb // 8) - 1, 0), 0)),
                  pl.BlockSpec((4, C), lambda i: (0, 0)), pl.BlockSpec((1, C), lambda i: (0, 0))],
        out_specs=[pl.BlockSpec((tb, C), lambda i: (i, 0))] * 2,
        out_shape=[jax.ShapeDtypeStruct((S, C), F32)] * 2,
        compiler_params=_params(("parallel",)))(xr, xr, w, b)


def _conv_bwd(dxc, pre, xr, w, tb=512):
    S, C = xr.shape
    tb = min(tb, S)
    nblk = S // tb

    def body(d_ref, p_ref, dn_ref, pn_ref, x_ref, w_ref, dx_ref, dw_ref, db_ref):
        i = pl.program_id(0)

        @pl.when(i == 0)
        def _():
            dw_ref[...] = jnp.zeros_like(dw_ref)
            db_ref[...] = jnp.zeros_like(db_ref)

        dpre = d_ref[...] * _silu_grad(p_ref[...])
        dnext = jnp.where(i < nblk - 1, dn_ref[...] * _silu_grad(pn_ref[...]), 0.0)
        pe = jnp.concatenate([dpre, dnext], axis=0)
        xx = x_ref[...]
        dx = w_ref[3:4, :] * dpre
        dw_ref[3:4, :] += jnp.sum(dpre * xx, axis=0, keepdims=True)
        for j in (1, 2, 3):
            ahead = pltpu.roll(pe, tb + 8 - j, axis=0)[:tb, :]
            dx = dx + w_ref[3 - j:4 - j, :] * ahead
            dw_ref[3 - j:4 - j, :] += jnp.sum(ahead * xx, axis=0, keepdims=True)
        dx_ref[...] = dx.astype(dx_ref.dtype)
        db_ref[...] += jnp.sum(dpre, axis=0, keepdims=True)

    blk = pl.BlockSpec((tb, C), lambda i: (i, 0))
    nxt = pl.BlockSpec((8, C), lambda i: (jnp.minimum((i + 1) * (tb // 8), S // 8 - 1), 0))
    return pl.pallas_call(
        body, name="conv_bwd", grid=(nblk,),
        in_specs=[blk, blk, nxt, nxt, blk, pl.BlockSpec((4, C), lambda i: (0, 0))],
        out_specs=[blk, pl.BlockSpec((4, C), lambda i: (0, 0)), pl.BlockSpec((1, C), lambda i: (0, 0))],
        out_shape=[jax.ShapeDtypeStruct((S, C), BF16), jax.ShapeDtypeStruct((4, C), F32),
                   jax.ShapeDtypeStruct((1, C), F32)],
        compiler_params=_params(("arbitrary",)))(dxc, pre, dxc, pre, xr, w)


def _iota(shape, dim):
    return lax.broadcasted_iota(jnp.int32, shape, dim)


def _colsel(m, lane, h):
    return jnp.sum(jnp.where(lane == h, m, 0.0), axis=1, keepdims=True)


def _cumsum_rows(v):
    r = _iota(v.shape, 0)
    k = 1
    while k < v.shape[0]:
        v = v + jnp.where(r >= k, pltpu.roll(v, k, axis=0), 0.0)
        k *= 2
    return v


def _suffix_sum_rows(v):
    n = v.shape[0]
    r = _iota(v.shape, 0)
    k = 1
    while k < n:
        v = v + jnp.where(r < n - k, pltpu.roll(v, n - k, axis=0), 0.0)
        k *= 2
    return v


def _ssd_fwd(xc, dtr, z, dtb, alog, dskl, nw):
    S = xc.shape[0]
    nc = S // L

    def body(xc_ref, dtr_ref, z_ref, dtb_ref, alog_ref, dsk_ref, nw_ref, ya_ref, y_ref, prev_ref,
             st_ref, cum_ref, cumT_ref):
        i = pl.program_id(0)

        @pl.when(i == 0)
        def _():
            st_ref[...] = jnp.zeros_like(st_ref)

        lane = _iota((L, 128), 1)
        lane1 = _iota((1, 128), 1)
        lo = lane < 64
        lo1 = lane1 < 64
        tril = _iota((L, L), 0) >= _iota((L, L), 1)
        dt = _softplus(dtr_ref[...] + dtb_ref[...])
        a_neg = -jnp.exp(alog_ref[...])
        cum = _cumsum_rows(dt * a_neg)
        cum_ref[...] = cum
        cumT_ref[...] = cum.T
        last_all = cum_ref[L - 1:L, :]
        prev_t = st_ref[...]
        prev_ref[0] = prev_t
        for g in range(2):
            bg = xc_ref[:, 1024 + g * 128:1152 + g * 128]
            cg = xc_ref[:, 1280 + g * 128:1408 + g * 128]
            gmat = _dot(cg, bg, NT)
            yoff = _dot(cg, prev_t[:, g * 512:(g + 1) * 512])
            bg_t = bg.T
            for jp in range(4):
                j = g * 4 + jp
                sl = slice(j * 128, (j + 1) * 128)
                xp = xc_ref[:, sl]
                cc = [_colsel(cum, lane, 2 * j), _colsel(cum, lane, 2 * j + 1)]
                cum_l = jnp.where(lo, cc[0], cc[1])
                dt_l = jnp.where(lo, _colsel(dt, lane, 2 * j), _colsel(dt, lane, 2 * j + 1))
                last_l = jnp.where(lo1, _colsel(last_all, lane1, 2 * j), _colsel(last_all, lane1, 2 * j + 1))
                xd = xp * dt_l
                ys = []
                for hh in range(2):
                    seg = cc[hh] - cumT_ref[2 * j + hh:2 * j + hh + 1, :]
                    dm = jnp.where(tril, jnp.exp(seg), 0.0)
                    ys.append(_dot(gmat * dm, xd))
                y_ref[:, sl] = (jnp.where(lo, ys[0], ys[1]) + jnp.exp(cum_l) * yoff[:, jp * 128:(jp + 1) * 128]
                                + dsk_ref[:, sl] * xp)
                st_ref[:, sl] = prev_t[:, sl] * jnp.exp(last_l) + _dot(bg_t, xd * jnp.exp(last_l - cum_l))
        for g in range(2):
            sl = slice(g * 512, (g + 1) * 512)
            zz = z_ref[:, sl]
            yg = y_ref[:, sl] * (zz * _sigmoid(zz))
            rstd = lax.rsqrt(jnp.mean(yg * yg, axis=-1, keepdims=True) + EPS)
            ya_ref[:, sl] = (yg * rstd * nw_ref[:, sl]).astype(ya_ref.dtype)

    blk = lambda c: pl.BlockSpec((L, c), lambda i: (i, 0))
    vec = lambda c: pl.BlockSpec((1, c), lambda i: (0, 0))
    return pl.pallas_call(
        body, name="ssd_fwd", grid=(nc,),
        in_specs=[blk(1536), blk(128), blk(1024), vec(128), vec(128), vec(1024), vec(1024)],
        out_specs=[blk(1024), blk(1024), pl.BlockSpec((1, NSTATE, 1024), lambda i: (i, 0, 0))],
        out_shape=[jax.ShapeDtypeStruct((S, 1024), BF16), jax.ShapeDtypeStruct((S, 1024), F32),
                   jax.ShapeDtypeStruct((nc, NSTATE, 1024), F32)],
        scratch_shapes=[pltpu.VMEM((NSTATE, 1024), F32), pltpu.VMEM((L, 128), F32), pltpu.VMEM((L, 128), F32)],
        compiler_params=_params(("arbitrary",)))(xc, dtr, z, dtb, alog, dskl, nw)


def _ssd_bwd(dya, y, z, xc, dtr, prev, dtb, alog, dskl, nw):
    S = xc.shape[0]
    nc = S // L

    def body(dya_ref, y_ref, z_ref, xc_ref, dtr_ref, prev_ref, dtb_ref, alog_ref, dsk_ref, nw_ref,
             dz_ref, dxc_ref, ddtr_ref, dnw_ref, ddsk_ref, dalog_ref, ddtb_ref,
             dst_ref, cum_ref, cumT_ref, dy_ref, dskacc_ref):
        i = pl.program_id(0)

        @pl.when(i == 0)
        def _():
            dst_ref[...] = jnp.zeros_like(dst_ref)
            dskacc_ref[...] = jnp.zeros_like(dskacc_ref)
            dnw_ref[...] = jnp.zeros_like(dnw_ref)
            dalog_ref[...] = jnp.zeros_like(dalog_ref)
            ddtb_ref[...] = jnp.zeros_like(ddtb_ref)

        lane = _iota((L, 128), 1)
        lane1 = _iota((1, 128), 1)
        lo = lane < 64
        lo1 = lane1 < 64
        r2, c2 = _iota((L, L), 0), _iota((L, L), 1)
        tril = r2 >= c2
        triu = r2 <= c2
        is_last = _iota((L, 1), 0) == L - 1

        for g in range(2):
            sl = slice(g * 512, (g + 1) * 512)
            zz = z_ref[:, sl]
            sg = _sigmoid(zz)
            zg = zz * sg
            yv = y_ref[:, sl]
            yg = yv * zg
            rstd = lax.rsqrt(jnp.mean(yg * yg, axis=-1, keepdims=True) + EPS)
            xh = yg * rstd
            d_out = dya_ref[:, sl]
            dnw_ref[:, sl] += jnp.sum(d_out * xh, axis=0, keepdims=True)
            dyn = d_out * nw_ref[:, sl]
            dyg = rstd * (dyn - xh * jnp.mean(dyn * xh, axis=-1, keepdims=True))
            dy_ref[:, sl] = dyg * zg
            dz_ref[:, sl] = (dyg * yv * (sg * (1.0 + zz * (1.0 - sg)))).astype(dz_ref.dtype)

        dtin = dtr_ref[...] + dtb_ref[...]
        dt = _softplus(dtin)
        a_neg = -jnp.exp(alog_ref[...])
        cum = _cumsum_rows(dt * a_neg)
        cum_ref[...] = cum
        cumT_ref[...] = cum.T
        last_all = cum_ref[L - 1:L, :]
        prev_t = prev_ref[0]
        dn_t = dst_ref[...]
        dcum = jnp.zeros((L, 128), F32)
        ddt = jnp.zeros((L, 128), F32)
        for g in range(2):
            gsl = slice(g * 512, (g + 1) * 512)
            bg = xc_ref[:, 1024 + g * 128:1152 + g * 128]
            cg = xc_ref[:, 1280 + g * 128:1408 + g * 128]
            gmat = _dot(cg, bg, NT)
            gmat_t = _dot(bg, cg, NT)
            pg = prev_t[:, gsl]
            zmat = _dot(cg, pg)
            dgm = jnp.zeros((L, L), F32)
            dgm_t = jnp.zeros((L, L), F32)
            db_acc = jnp.zeros((L, NSTATE), F32)
            dz_parts, cd_parts = [], []
            for jp in range(4):
                j = g * 4 + jp
                sl = slice(j * 128, (j + 1) * 128)
                xp = xc_ref[:, sl]
                dyp = dy_ref[:, sl]
                cc = [_colsel(cum, lane, 2 * j), _colsel(cum, lane, 2 * j + 1)]
                lc = [_colsel(last_all, lane1, 2 * j), _colsel(last_all, lane1, 2 * j + 1)]
                cum_l = jnp.where(lo, cc[0], cc[1])
                dt_l = jnp.where(lo, _colsel(dt, lane, 2 * j), _colsel(dt, lane, 2 * j + 1))
                last_l = jnp.where(lo1, lc[0], lc[1])
                e_l = jnp.exp(cum_l)
                dte_l = jnp.exp(last_l - cum_l)
                cd_l = jnp.exp(last_l)
                cd_parts.append(cd_l)
                xd = xp * dt_l
                dskacc_ref[:, sl] += jnp.sum(dyp * xp, axis=0, keepdims=True)
                dxp = dsk_ref[:, sl] * dyp
                t = dyp * (e_l * zmat[:, jp * 128:(jp + 1) * 128])
                dcc = [jnp.sum(jnp.where(lo, t, 0.0), axis=1, keepdims=True),
                       jnp.sum(jnp.where(lo, 0.0, t), axis=1, keepdims=True)]
                dz_parts.append(e_l * dyp)
                dnp_ = dn_t[:, sl]
                t2 = jnp.sum(dnp_ * prev_t[:, sl], axis=0, keepdims=True)
                dcd = [jnp.sum(jnp.where(lo1, t2, 0.0), axis=1, keepdims=True),
                       jnp.sum(jnp.where(lo1, 0.0, t2), axis=1, keepdims=True)]
                wm = _dot(bg, dnp_)
                dxd = wm * dte_l
                t3 = wm * xd
                ddte = [jnp.sum(jnp.where(lo, t3, 0.0), axis=1, keepdims=True),
                        jnp.sum(jnp.where(lo, 0.0, t3), axis=1, keepdims=True)]
                db_acc = db_acc + _dot(xd * dte_l, dnp_, NT)
                for hh in range(2):
                    h = 2 * j + hh
                    half = lo if hh == 0 else jnp.logical_not(lo)
                    row = cumT_ref[h:h + 1, :]
                    dm = jnp.where(tril, jnp.exp(cc[hh] - row), 0.0)
                    dm_t = jnp.where(triu, jnp.exp(row - cc[hh]), 0.0)
                    dym = jnp.where(half, dyp, 0.0)
                    u = _dot(dym, xd, NT) * dm
                    u_t = _dot(xd, dym, NT) * dm_t
                    dxd = dxd + _dot(gmat_t * dm_t, dym)
                    dcc[hh] = dcc[hh] + jnp.sum(u * gmat, axis=1, keepdims=True) - jnp.sum(u_t * gmat_t, axis=1, keepdims=True)
                    dgm = dgm + u
                    dgm_t = dgm_t + u_t
                    dte_c = jnp.exp(lc[hh] - cc[hh])
                    dcc[hh] = dcc[hh] - ddte[hh] * dte_c
                    endc = dcd[hh] * jnp.exp(lc[hh]) + jnp.sum(ddte[hh] * dte_c, axis=0, keepdims=True)
                    dcc[hh] = dcc[hh] + jnp.where(is_last, endc, 0.0)
                    dcum = jnp.where(lane == h, dcc[hh], dcum)
                dxc_ref[:, sl] = dxp + dxd * dt_l
                t4 = dxd * xp
                ddt = jnp.where(lane == 2 * j, jnp.sum(jnp.where(lo, t4, 0.0), axis=1, keepdims=True), ddt)
                ddt = jnp.where(lane == 2 * j + 1, jnp.sum(jnp.where(lo, 0.0, t4), axis=1, keepdims=True), ddt)
            dzg = jnp.concatenate(dz_parts, axis=1)
            dst_ref[:, gsl] = dn_t[:, gsl] * jnp.concatenate(cd_parts, axis=1) + _dot(cg.T, dzg)
            dxc_ref[:, 1280 + g * 128:1408 + g * 128] = _dot(dgm, bg) + _dot(dzg, pg, NT)
            dxc_ref[:, 1024 + g * 128:1152 + g * 128] = _dot(dgm_t, cg) + db_acc
        dla = _suffix_sum_rows(dcum)
        ddt = ddt + dla * a_neg
        dalog_ref[...] += jnp.sum(dla * dt, axis=0, keepdims=True) * a_neg
        ddtr = jnp.where(lane < 16, ddt * _sigmoid(dtin), 0.0)
        ddtr_ref[...] = ddtr.astype(ddtr_ref.dtype)
        ddtb_ref[...] += jnp.sum(ddtr, axis=0, keepdims=True)

        @pl.when(i == nc - 1)
        def _():
            seg = (_iota((1024, 128), 0) // 64 == _iota((1024, 128), 1)).astype(F32)
            acc8 = jnp.broadcast_to(dskacc_ref[...], (8, 1024))
            ddsk_ref[...] = lax.dot_general(acc8, seg, NN, precision=lax.Precision.HIGHEST,
                                            preferred_element_type=F32)

    rev = lambda c: pl.BlockSpec((L, c), lambda i: (nc - 1 - i, 0))
    vec = lambda c: pl.BlockSpec((1, c), lambda i: (0, 0))
    return pl.pallas_call(
        body, name="ssd_bwd", grid=(nc,),
        in_specs=[rev(1024), rev(1024), rev(1024), rev(1536), rev(128),
                  pl.BlockSpec((1, NSTATE, 1024), lambda i: (nc - 1 - i, 0, 0)),
                  vec(128), vec(128), vec(1024), vec(1024)],
        out_specs=[rev(1024), rev(1536), rev(128), vec(1024), pl.BlockSpec((8, 128), lambda i: (0, 0)),
                   vec(128), vec(128)],
        out_shape=[jax.ShapeDtypeStruct((S, 1024), BF16), jax.ShapeDtypeStruct((S, 1536), F32),
                   jax.ShapeDtypeStruct((S, 128), BF16), jax.ShapeDtypeStruct((1, 1024), F32),
                   jax.ShapeDtypeStruct((8, 128), F32), jax.ShapeDtypeStruct((1, 128), F32),
                   jax.ShapeDtypeStruct((1, 128), F32)],
        scratch_shapes=[pltpu.VMEM((NSTATE, 1024), F32), pltpu.VMEM((L, 128), F32), pltpu.VMEM((L, 128), F32),
                        pltpu.VMEM((L, 1024), F32), pltpu.VMEM((1, 1024), F32)],
        compiler_params=_params(("arbitrary",)))(dya, y, z, xc, dtr, prev, dtb, alog, dskl, nw)


def _layer_norm_parts(vg):
    mu = jnp.mean(vg, axis=-1, keepdims=True)
    vc = vg - mu
    rstd = lax.rsqrt(jnp.mean(vc * vc, axis=-1, keepdims=True) + EPS)
    return vc * rstd, rstd


def _gmlp_fwd(u, v, lnw, lnb, ws, bse, tb=512):
    S = u.shape[0]
    tb = min(tb, S)

    def body(u_ref, v_ref, lnw_ref, lnb_ref, ws_ref, bse_ref, o_ref, vn_ref):
        tril = _iota((L, L), 0) >= _iota((L, L), 1)
        xh, _ = _layer_norm_parts(_gelu(v_ref[...]))
        vn_ref[...] = xh * lnw_ref[...] + lnb_ref[...]
        for g in range(8):
            w = jnp.where(tril, ws_ref[g], 0.0)
            gs = slice(g * 128, (g + 1) * 128)
            for ch in range(tb // L):
                rs = slice(ch * L, (ch + 1) * L)
                sv = _dot(w, vn_ref[rs, gs]) + bse_ref[g]
                o_ref[rs, gs] = (_gelu(u_ref[rs, gs]) * sv).astype(o_ref.dtype)

    blk = pl.BlockSpec((tb, 1024), lambda i: (i, 0))
    vec = pl.BlockSpec((1, 1024), lambda i: (0, 0))
    cube = pl.BlockSpec((8, L, 128), lambda i: (0, 0, 0))
    return pl.pallas_call(
        body, name="gmlp_fwd", grid=(S // tb,), in_specs=[blk, blk, vec, vec, cube, cube], out_specs=blk,
        out_shape=jax.ShapeDtypeStruct((S, 1024), BF16), scratch_shapes=[pltpu.VMEM((tb, 1024), F32)],
        compiler_params=_params(("parallel",)))(u, v, lnw, lnb, ws, bse)


def _gmlp_bwd(dyb, u, v, lnw, lnb, ws, bse, tb=512):
    S = u.shape[0]
    tb = min(tb, S)

    def body(d_ref, u_ref, v_ref, lnw_ref, lnb_ref, ws_ref, bse_ref,
             du_ref, dv_ref, dws_ref, dbse_ref, dlnw_ref, dlnb_ref, vn_ref, dvn_ref):
        @pl.when(pl.program_id(0) == 0)
        def _():
            dws_ref[...] = jnp.zeros_like(dws_ref)
            dbse_ref[...] = jnp.zeros_like(dbse_ref)
            dlnw_ref[...] = jnp.zeros_like(dlnw_ref)
            dlnb_ref[...] = jnp.zeros_like(dlnb_ref)

        tril = _iota((L, L), 0) >= _iota((L, L), 1)
        vv = v_ref[...]
        xh, rstd = _layer_norm_parts(_gelu(vv))
        vn_ref[...] = xh * lnw_ref[...] + lnb_ref[...]
        for g in range(8):
            w = jnp.where(tril, ws_ref[g], 0.0)
            w_t = w.T
            gs = slice(g * 128, (g + 1) * 128)
            dw = jnp.zeros((L, L), F32)
            dbs = jnp.zeros((L, 128), F32)
            for ch in range(tb // L):
                rs = slice(ch * L, (ch + 1) * L)
                vn = vn_ref[rs, gs]
                sv = _dot(w, vn) + bse_ref[g]
                uu = u_ref[rs, gs]
                dd = d_ref[rs, gs]
                du_ref[rs, gs] = (dd * sv * _gelu_grad(uu)).astype(du_ref.dtype)
                dsv = dd * _gelu(uu)
                dw = dw + _dot(dsv, vn, NT)
                dbs = dbs + dsv
                dvn_ref[rs, gs] = _dot(w_t, dsv)
            dws_ref[g] += jnp.where(tril, dw, 0.0)
            dbse_ref[g] += dbs
        dvn = dvn_ref[...]
        dlnw_ref[...] += jnp.sum(dvn * xh, axis=0, keepdims=True)
        dlnb_ref[...] += jnp.sum(dvn, axis=0, keepdims=True)
        dxh = dvn * lnw_ref[...]
        dvg = rstd * (dxh - jnp.mean(dxh, axis=-1, keepdims=True) - xh * jnp.mean(dxh * xh, axis=-1, keepdims=True))
        dv_ref[...] = (dvg * _gelu_grad(vv)).astype(dv_ref.dtype)

    blk = pl.BlockSpec((tb, 1024), lambda i: (i, 0))
    vec = pl.BlockSpec((1, 1024), lambda i: (0, 0))
    cube = pl.BlockSpec((8, L, 128), lambda i: (0, 0, 0))
    return pl.pallas_call(
        body, name="gmlp_bwd", grid=(S // tb,), in_specs=[blk, blk, blk, vec, vec, cube, cube],
        out_specs=[blk, blk, cube, cube, vec, vec],
        out_shape=[jax.ShapeDtypeStruct((S, 1024), BF16), jax.ShapeDtypeStruct((S, 1024), BF16),
                   jax.ShapeDtypeStruct((8, L, 128), F32), jax.ShapeDtypeStruct((8, L, 128), F32),
                   jax.ShapeDtypeStruct((1, 1024), F32), jax.ShapeDtypeStruct((1, 1024), F32)],
        scratch_shapes=[pltpu.VMEM((tb, 1024), F32), pltpu.VMEM((tb, 1024), F32)],
        compiler_params=_params(("arbitrary",)))(dyb, u, v, lnw, lnb, ws, bse)


def _lane_sum(name, a):
    def body(a_ref, o_ref):
        o_ref[...] = jnp.sum(a_ref[...], axis=1, keepdims=True)
    return pl.pallas_call(body, name=name, out_shape=jax.ShapeDtypeStruct((a.shape[0], 1), F32))(a)


def _bucket_onehot_t():
    qi = np.arange(L)[:, None]
    sj = np.arange(2 * L)[None, :]
    dist = np.maximum(qi + L - sj, 0)
    log_ratio = (np.log(np.maximum(dist, 1).astype(np.float32) / np.float32(16)) / np.float32(math.log(128 / 16)))
    large = 16 + (log_ratio.astype(np.float32) * np.float32(16)).astype(np.int32)
    bucket = np.where(dist < 16, dist, np.minimum(large, 31)).reshape(-1)
    return (np.arange(32)[:, None] == bucket[None, :]).astype(np.float32)


def _rel_bias(table_t, onehot_t):
    def body(t_ref, oh_ref, o_ref):
        o_ref[...] = lax.dot_general(t_ref[...], oh_ref[...], NN, precision=lax.Precision.HIGHEST,
                                     preferred_element_type=F32)
    return pl.pallas_call(body, name="rel_bias", out_shape=jax.ShapeDtypeStruct((16, L * 2 * L), F32),
                          compiler_params=_params())(table_t, onehot_t)


def _rel_bias_bwd(dbias, onehot_t):
    def body(d_ref, oh_ref, o_ref):
        o_ref[...] = lax.dot_general(d_ref[...], oh_ref[...], NT, precision=lax.Precision.HIGHEST,
                                     preferred_element_type=F32)
    return pl.pallas_call(body, name="rel_bias_bwd", out_shape=jax.ShapeDtypeStruct((16, 32), F32),
                          compiler_params=_params())(dbias, onehot_t)


def _band(kp, kc, lo):
    kk = jnp.concatenate([kp, kc], axis=0)
    kr = pltpu.roll(kk, 64, axis=1)
    return [jnp.where(lo, kk, kr), jnp.where(lo, kr, kk)]


def _attn_rows(ref, j, lo):
    parts = []
    for t in range(8):
        pair = ref[:, (4 * j + t // 2) * 128:(4 * j + t // 2 + 1) * 128]
        parts.append(jnp.where(lo if t % 2 == 0 else jnp.logical_not(lo), pair, 0.0))
    return jnp.concatenate(parts, axis=0)


def _attn_mask(i, rows):
    qi, sj = _iota((rows, 2 * L), 0) & (L - 1), _iota((rows, 2 * L), 1)
    rel = qi + L - sj
    return (rel >= 0) & (rel < L) & ((sj >= L) | (i > 0))


def _per_head_col(vals):
    return jnp.concatenate([jnp.broadcast_to(v, (L, 1)) for v in vals], axis=0)


SMEM = pl.BlockSpec(memory_space=pltpu.SMEM)


def _attn_fwd(qkv, bias, sinks):
    S = qkv.shape[0]
    nb = S // L
    scale = 64 ** -0.5

    def body(sink_ref, q_ref, kc_ref, vc_ref, kp_ref, vp_ref, bias_ref, o_ref, lse_ref):
        i = pl.program_id(0)
        lane = _iota((L, 128), 1)
        lo = lane < 64
        lo2 = _iota((2 * L, 128), 1) < 64
        mask = _attn_mask(i, L)
        kd = _band(kp_ref[...], kc_ref[...], lo2)
        vd = _band(vp_ref[...], vc_ref[...], lo2)
        lse = jnp.zeros((L, 128), F32)
        for pr in range(8):
            sl = slice(pr * 128, (pr + 1) * 128)
            qp = q_ref[:, sl]
            j = pr // 4
            outs = []
            for hh in range(2):
                h = 2 * pr + hh
                qm = jnp.where(lo if hh == 0 else jnp.logical_not(lo), qp, 0.0)
                lg = jnp.where(mask, _dot(qm, kd[j], NT) * scale + bias_ref[h], NEG_INF)
                s = sink_ref[h]
                m = jnp.maximum(jnp.max(lg, axis=1, keepdims=True), s)
                p = jnp.where(mask, jnp.exp(lg - m), 0.0)
                den = jnp.sum(p, axis=1, keepdims=True) + jnp.exp(s - m)
                outs.append(_dot(p / den, vd[j]))
                lse = jnp.where(lane == h, m + jnp.log(den), lse)
            o_ref[:, sl] = jnp.where(lo, outs[0], outs[1]).astype(o_ref.dtype)
        lse_ref[...] = lse

    prev = lambda col: pl.BlockSpec((L, 128), lambda i: (jnp.maximum(i - 1, 0), col))
    cur = lambda col: pl.BlockSpec((L, 128), lambda i: (i, col))
    return pl.pallas_call(
        body, name="attn_fwd", grid=(nb,),
        in_specs=[SMEM, pl.BlockSpec((L, 1024), lambda i: (i, 0)), cur(8), cur(9), prev(8), prev(9),
                  pl.BlockSpec((16, L, 2 * L), lambda i: (0, 0, 0))],
        out_specs=[pl.BlockSpec((L, 1024), lambda i: (i, 0)), pl.BlockSpec((L, 128), lambda i: (i, 0))],
        out_shape=[jax.ShapeDtypeStruct((S, 1024), BF16), jax.ShapeDtypeStruct((S, 128), F32)],
        compiler_params=_params(("parallel",)))(sinks, qkv, qkv, qkv, qkv, qkv, bias)


def _attn_bwd(qkv, d_o, lse, bias, sinks):
    S = qkv.shape[0]
    nb = S // L
    scale = 64 ** -0.5

    def body(sink_ref, q_ref, kc_ref, vc_ref, kp_ref, vp_ref, do_ref, lse_ref, bias_ref,
             dq_ref, dkv_ref, dbias_ref, dsink_ref, dbq_ref, dbkv_ref, carry_ref):
        i = pl.program_id(0)

        @pl.when(i == 0)
        def _():
            dbias_ref[...] = jnp.zeros_like(dbias_ref)
            dsink_ref[...] = jnp.zeros_like(dsink_ref)
            dbq_ref[...] = jnp.zeros_like(dbq_ref)
            dbkv_ref[...] = jnp.zeros_like(dbkv_ref)
            carry_ref[...] = jnp.zeros_like(carry_ref)

        @pl.when(i < nb)
        def _():
            lane = _iota((L, 128), 1)
            lane1 = _iota((1, 128), 1)
            lo = lane < 64
            lo2 = _iota((2 * L, 128), 1) < 64
            mask = _attn_mask(i, 8 * L)
            kd = _band(kp_ref[...], kc_ref[...], lo2)
            vd = _band(vp_ref[...], vc_ref[...], lo2)
            lse_all = lse_ref[...]
            dsink = jnp.zeros((1, 128), F32)
            tot_k, tot_v = [], []
            for j in range(2):
                q_all = _attn_rows(q_ref, j, lo)
                do_all = _attn_rows(do_ref, j, lo)
                lse_col = _per_head_col([_colsel(lse_all, lane, 8 * j + t) for t in range(8)])
                lg = _dot(q_all, kd[j], NT) * scale + bias_ref[8 * j:8 * j + 8].reshape(8 * L, 2 * L)
                p = jnp.where(mask, jnp.exp(jnp.where(mask, lg, NEG_INF) - lse_col), 0.0)
                dp = _dot(do_all, vd[j], NT)
                delta = jnp.sum(p * dp, axis=1, keepdims=True)
                ds = p * (dp - delta)
                dbias_ref[8 * j:8 * j + 8] += ds.reshape(8, L, 2 * L)
                s = _per_head_col([sink_ref[8 * j + t] for t in range(8)])
                sink_part = -jnp.exp(s - lse_col) * delta
                for t in range(8):
                    dsink = dsink + jnp.where(lane1 == 8 * j + t,
                                              jnp.sum(sink_part[t * L:(t + 1) * L], axis=0, keepdims=True), 0.0)
                dss = ds * scale
                dq_all = _dot(dss, kd[j])
                for t in range(0, 8, 2):
                    sl = slice((4 * j + t // 2) * 128, (4 * j + t // 2 + 1) * 128)
                    dq = jnp.where(lo, dq_all[t * L:(t + 1) * L], dq_all[(t + 1) * L:(t + 2) * L])
                    dq_ref[:, sl] = dq.astype(dq_ref.dtype)
                    dbq_ref[:, sl] += jnp.sum(dq, axis=0, keepdims=True)
                acc_k = _dot(dss, q_all, TN)
                acc_v = _dot(p, do_all, TN)
                tot_k.append(acc_k + pltpu.roll(acc_k, 64, axis=1))
                tot_v.append(acc_v + pltpu.roll(acc_v, 64, axis=1))
            dsink_ref[...] += dsink
            dkv = jnp.concatenate([jnp.where(lo2, tot_k[0], tot_k[1]), jnp.where(lo2, tot_v[0], tot_v[1])], axis=1)
            dbkv_ref[...] += jnp.sum(dkv, axis=0, keepdims=True)
            dkv_ref[...] = (carry_ref[...] + dkv[:L, :]).astype(dkv_ref.dtype)
            carry_ref[...] = dkv[L:, :]

        @pl.when(i == nb)
        def _():
            dkv_ref[...] = carry_ref[...].astype(dkv_ref.dtype)

    c = lambda i: jnp.minimum(i, nb - 1)
    prev = lambda col: pl.BlockSpec((L, 128), lambda i: (jnp.maximum(c(i) - 1, 0), col))
    cur = lambda col: pl.BlockSpec((L, 128), lambda i: (c(i), col))
    row = lambda w: pl.BlockSpec((L, w), lambda i: (c(i), 0))
    cube = pl.BlockSpec((16, L, 2 * L), lambda i: (0, 0, 0))
    vec = lambda w: pl.BlockSpec((1, w), lambda i: (0, 0))
    return pl.pallas_call(
        body, name="attn_bwd", grid=(nb + 1,),
        in_specs=[SMEM, row(1024), cur(8), cur(9), prev(8), prev(9), row(1024), row(128), cube],
        out_specs=[row(1024), pl.BlockSpec((L, 256), lambda i: (jnp.maximum(i - 1, 0), 0)), cube,
                   vec(128), vec(1024), vec(256)],
        out_shape=[jax.ShapeDtypeStruct((S, 1024), BF16), jax.ShapeDtypeStruct((S, 256), BF16),
                   jax.ShapeDtypeStruct((16, L, 2 * L), F32), jax.ShapeDtypeStruct((1, 128), F32),
                   jax.ShapeDtypeStruct((1, 1024), F32), jax.ShapeDtypeStruct((1, 256), F32)],
        scratch_shapes=[pltpu.VMEM((L, 256), F32)],
        compiler_params=_params(("arbitrary",)))(sinks, qkv, qkv, qkv, qkv, qkv, d_o, lse, bias)


def _pad_lanes(a, n=128):
    return jnp.pad(a, ((0, 0), (0, n - a.shape[1])))


def _local_step(x, tgt, mod, w_in, P, io):
    md = [[mod[l:l + 1, k * D:(k + 1) * D] for k in range(6)] for l in range(2)]
    G, g = {}, {}

    sh1, sc1, g1, sh2, sc2, g2 = md[0]
    nmw0, nfw0 = P["norm_mix_w"][0:1], P["norm_ffn_w"][0:1]
    h0 = _norm_mod_fwd("norm_mix_0", x, nmw0, sc1, sh1, after=io["start"])
    segs = {"z": w_in[0:1024], "xbc": w_in[1024:2560], "dt": jnp.pad(w_in[2560:2576], ((0, 112), (0, 0))),
            "u": w_in[2576:3600], "v": w_in[3600:4624]}
    proj = dict(zip(segs, _mm_shared_lhs("in_proj", h0, list(segs.values()))))
    conv_w, conv_b = P["conv_w"][0], P["conv_b"]
    pre, xc = _conv_fwd(proj["xbc"], conv_w, conv_b)
    dtb, alog = _pad_lanes(P["dt_bias"]), _pad_lanes(P["a_log"])
    dskl = jnp.repeat(P["d_skip"], 64, axis=1)
    ya, y_ssd, prev = _ssd_fwd(xc, proj["dt"], proj["z"], dtb, alog, dskl, P["ssm_norm_w"])
    ws = P["gmlp_ws"][0]
    bse = jnp.broadcast_to(P["gmlp_bs"][0][:, :, None], (8, L, 128))
    yb = _gmlp_fwd(proj["u"], proj["v"], P["gmlp_ln_w"], P["gmlp_ln_b"], ws, bse)
    W = dict(io["weights0"]((ya, yb)))
    w_oa, w_ob = W["out_w"][:1024], W["out_w"][1024:]

    def res(y, x, gate):
        return y, x + gate * y
    mix0, x1 = _mm("out_proj_0", [ya, yb], [w_oa, w_ob], "nn", [F32, F32], epi=res, extras=[x], vecs=[g1])
    h0f = _norm_mod_fwd("norm_ffn_0", x1, nfw0, sc2, sh2)
    a0, b0, f0, y0, x2 = _ffn_fwd("0", h0f, W["gate_wt0"], W["up_wt0"], W["down_w0"], x1, g2)

    sh1b, sc1b, g1b, sh2b, sc2b, g2b = md[1]
    nmw1, nfw1 = P["norm_mix_w"][1:2], P["norm_ffn_w"][1:2]
    W.update(io["weights1"](x2))
    h1 = _norm_mod_fwd("norm_mix_1", x2, nmw1, sc1b, sh1b)
    qkv = _mm("qkv_proj", [h1], [W["qkv_wt"]], "nt", [F32], epi=lambda acc, b: acc + b, vecs=[P["qkv_b"]])[0]
    onehot_t = jnp.asarray(_bucket_onehot_t())
    bias = _rel_bias(P["rel_table"].T, onehot_t).reshape(16, L, 2 * L)
    sinks = P["sinks"].reshape(16)
    att, lse = _attn_fwd(qkv, bias, sinks)

    def res_b(y, x, gate, b):
        y = y + b
        return y, x + gate * y
    mix1, x3 = _mm("o_proj", [att], [W["o_w"]], "nn", [F32, F32], epi=res_b, extras=[x2], vecs=[g1b, P["o_b"]])
    h1f = _norm_mod_fwd("norm_ffn_1", x3, nfw1, sc2b, sh2b)
    a1, b1, f1, y1, x4 = _ffn_fwd("1", h1f, W["gate_wt1"], W["up_wt1"], W["down_w1"], x3, g2b)

    dx, sq, g["final_norm_w"] = _loss_head(x4, tgt, P["final_norm_w"])

    dh, dg2b, dwg1, dwu1, dwd1 = _ffn_bwd("1", dx, h1f, a1, b1, f1, y1, W["gate_wt1"], W["up_wt1"],
                                          W["down_w1"], g2b)
    dx, dsh2b, dsc2b, dnfw1 = _norm_mod_bwd("norm_ffn_bwd_1", x3, dh, dx, nfw1, sc2b)
    dmix, dg1b, g["o_b"] = _gate_bwd("mix_gate_bwd_1", dx, mix1, g1b)
    G["o_w"] = _mm_tn("o_dw", att, dmix)
    d_att = _mm("o_dx", [dmix], [W["o_w"]], "nt", [F32])[0]
    dq, dkv, dbias, dsinks, dbq, dbkv = _attn_bwd(qkv, d_att, lse, bias, sinks)
    g["rel_table"] = _rel_bias_bwd(dbias.reshape(16, L * 2 * L), onehot_t).T
    g["sinks"] = dsinks[:, :16]
    g["qkv_b"] = jnp.concatenate([dbq, dbkv], axis=1)
    w_q, w_kv = W["qkv_wt"][:1024], W["qkv_wt"][1024:]
    G["qkv_wt"] = jnp.concatenate([_mm_tn("qkv_dwq", dq, h1), _mm_tn("qkv_dwkv", dkv, h1)], axis=0)
    dh = _mm("qkv_dx", [dq, dkv], [w_q, w_kv], "nn", [F32])[0]
    dx, dsh1b, dsc1b, dnmw1 = _norm_mod_bwd("norm_mix_bwd_1", x2, dh, dx, nmw1, sc1b)
    behind = io["grads1"]({"qkv_wt": G.pop("qkv_wt"), "o_w": G.pop("o_w"), "gate_wt1": dwg1, "up_wt1": dwu1,
                           "down_w1": dwd1})

    dh, dg2, dwg0, dwu0, dwd0 = _ffn_bwd("0", dx, h0f, a0, b0, f0, y0, W["gate_wt0"], W["up_wt0"],
                                         W["down_w0"], g2, after=behind)
    behind = io["grads_ffn0"]({"gate_wt0": dwg0, "up_wt0": dwu0, "down_w0": dwd0})
    dx, dsh2, dsc2, dnfw0 = _norm_mod_bwd("norm_ffn_bwd_0", x1, dh, dx, nfw0, sc2, after=behind)
    dmix, dg1, _ = _gate_bwd("mix_gate_bwd_0", dx, mix0, g1)
    G["out_w"] = jnp.concatenate([_mm_tn("out_dwa", ya, dmix), _mm_tn("out_dwb", yb, dmix)], axis=0)
    dya = _mm("out_dxa", [dmix], [w_oa], "nt", [F32])[0]
    dyb = _mm("out_dxb", [dmix], [w_ob], "nt", [F32])[0]
    du, dv, dws, dbse, g["gmlp_ln_w"], g["gmlp_ln_b"] = _gmlp_bwd(dyb, proj["u"], proj["v"], P["gmlp_ln_w"],
                                                                 P["gmlp_ln_b"], ws, bse)
    g["gmlp_ws"] = dws[None]
    g["gmlp_bs"] = _lane_sum("gmlp_dbs", dbse.reshape(8 * L, 128)).reshape(1, 8, L)
    dz, dxc, ddt, g["ssm_norm_w"], ddsk, dalog, ddtb = _ssd_bwd(dya, y_ssd, proj["z"], xc, proj["dt"], prev,
                                                                dtb, alog, dskl, P["ssm_norm_w"])
    g["d_skip"], g["a_log"], g["dt_bias"] = ddsk[0:1, :16], dalog[:, :16], ddtb[:, :16]
    dxr, dconv_w, g["conv_b"] = _conv_bwd(dxc, pre, proj["xbc"], conv_w)
    g["conv_w"] = dconv_w[None]
    dsegs = {"z": dz, "xbc": dxr, "dt": ddt, "u": du, "v": dv}
    dws_in = dict(zip(dsegs, _mm_tn_shared_rhs("in_dw", list(dsegs.values()), h0)))
    G["in_wt"] = jnp.concatenate([dws_in["z"], dws_in["xbc"], dws_in["dt"][:16], dws_in["u"], dws_in["v"]], axis=0)
    keys = ["z", "xbc", "dt", "u", "v"]
    dh = _mm("in_dx", [dsegs[k] for k in keys], [segs[k] for k in keys], "nn", [F32])[0]
    dx, dsh1, dsc1, dnmw0 = _norm_mod_bwd("norm_mix_bwd_0", x, dh, dx, nmw0, sc1)

    g["norm_mix_w"] = jnp.concatenate([dnmw0, dnmw1], axis=0)
    g["norm_ffn_w"] = jnp.concatenate([dnfw0, dnfw1], axis=0)
    dmod = jnp.concatenate([jnp.concatenate([dsh1, dsc1, dg1, dsh2, dsc2, dg2], axis=1),
                            jnp.concatenate([dsh1b, dsc1b, dg1b, dsh2b, dsc2b, dg2b], axis=1)], axis=0)
    return sq, dx, dmod, G, g


def _ada_fwd(c_all, ada_w, ada_b):
    n = ada_w.shape[2]
    tn = _col_tile(n, 512)

    def body(c_ref, w_ref, b_ref, o_ref):
        cc = c_ref[...]
        o_ref[...] = lax.dot_general(cc * _sigmoid(cc), w_ref[...], NN, precision=lax.Precision.HIGHEST,
                                     preferred_element_type=F32) + b_ref[...]

    return pl.pallas_call(
        body, name="ada_fwd", grid=(2, n // tn),
        in_specs=[pl.BlockSpec((8, D), lambda l, j: (0, 0)), pl.BlockSpec((None, D, tn), lambda l, j: (l, 0, j)),
                  pl.BlockSpec((None, 1, tn), lambda l, j: (l, 0, j))],
        out_specs=pl.BlockSpec((None, 8, tn), lambda l, j: (l, 0, j)),
        out_shape=jax.ShapeDtypeStruct((2, 8, n), F32), compiler_params=_params(("parallel", "parallel")))(
            c_all, ada_w, ada_b)


def _ada_bwd(c_all, dmod_cols, dmod_all):
    n = dmod_cols.shape[2]
    tn = _col_tile(n, 512)

    def body(c_ref, d_ref, o_ref):
        cc = c_ref[...]
        o_ref[...] = lax.dot_general(cc * _sigmoid(cc), d_ref[...], TN, precision=lax.Precision.HIGHEST,
                                     preferred_element_type=F32)

    dw = pl.pallas_call(
        body, name="ada_dw", grid=(2, n // tn),
        in_specs=[pl.BlockSpec((8, D), lambda l, j: (0, 0)), pl.BlockSpec((None, 8, tn), lambda l, j: (l, 0, j))],
        out_specs=pl.BlockSpec((None, D, tn), lambda l, j: (l, 0, j)),
        out_shape=jax.ShapeDtypeStruct((2, D, n), F32), compiler_params=_params(("parallel", "parallel")))(
            c_all, dmod_cols)

    def sum_body(d_ref, o_ref):
        o_ref[...] = jnp.sum(d_ref[...], axis=0, keepdims=True)

    db = pl.pallas_call(
        sum_body, name="ada_db", grid=(2,),
        in_specs=[pl.BlockSpec((None, 8, 6 * D), lambda l: (l, 0, 0))],
        out_specs=pl.BlockSpec((None, 1, 6 * D), lambda l: (l, 0, 0)),
        out_shape=jax.ShapeDtypeStruct((2, 1, 6 * D), F32), compiler_params=_params(("parallel",)))(dmod_all)
    return dw, db


def _row_tile(rows, cap=512, mult=8):
    best = rows
    for t in range(mult, min(rows, cap) + 1, mult):
        if rows % t == 0:
            best = t
    return best


def _adamw(name, w, g, m, v):
    def fn(w, g, m, v):
        m = ADAM_B1 * m + (1.0 - ADAM_B1) * g
        v = ADAM_B2 * v + (1.0 - ADAM_B2) * (g * g)
        m_hat = m / (1.0 - ADAM_B1 ** ADAM_STEP)
        v_hat = v / (1.0 - ADAM_B2 ** ADAM_STEP)
        return -ADAM_LR * (m_hat / (jnp.sqrt(v_hat) + ADAM_EPS) + ADAM_WD * w), m, v
    cols = w.shape[1]
    return _rowwise(name, fn, [w, g, m, v], [], [(cols, F32)] * 3, tr=_row_tile(w.shape[0]))


def _place():
    return lax.axis_index("x"), lax.axis_index("y"), lax.axis_index("c")


VMEM_SPEC = pl.BlockSpec(memory_space=pltpu.VMEM)


def _allreduce_small(name, buf, after=None):
    rows = buf.shape[0]
    deps = [] if after is None else [after]

    def body(x_ref, *rest):
        o_ref, stage, send_sems, recv_sems = rest[len(deps):]
        x, y, c = _place()
        me = 4 * x + 2 * y + c
        stage[me] = x_ref[...]
        copies = []
        for k in range(1, 8):
            peer = (1 - x if k & 4 else x, 1 - y if k & 2 else y, 1 - c if k & 1 else c)
            cp = pltpu.make_async_remote_copy(src_ref=x_ref, dst_ref=stage.at[me], send_sem=send_sems.at[k - 1],
                                              recv_sem=recv_sems.at[k - 1], device_id=peer, device_id_type=MESH)
            cp.start()
            copies.append(cp)
        for cp in copies:
            cp.wait()
        acc = stage[0]
        for d in range(1, 8):
            acc = acc + stage[d]
        o_ref[...] = acc

    return pl.pallas_call(
        body, name=name, in_specs=[VMEM_SPEC] + [ANY for _ in deps], out_specs=VMEM_SPEC,
        out_shape=jax.ShapeDtypeStruct((rows, 128), F32),
        scratch_shapes=[pltpu.VMEM((8, rows, 128), F32), pltpu.SemaphoreType.DMA((7,)), pltpu.SemaphoreType.DMA((7,))],
        compiler_params=pltpu.CompilerParams(vmem_limit_bytes=_VMEM_LIMIT))(buf, *deps)


def _sum_slots(name, own, land):
    def body(own_ref, land_ref, o_ref):
        x, y, c = _place()
        me = 4 * x + 2 * y + c
        acc = None
        for d in range(8):
            v = jnp.where(me == d, own_ref[...], land_ref[d])
            acc = v if acc is None else acc + v
        o_ref[...] = acc

    return pl.pallas_call(body, name=name, in_specs=[VMEM_SPEC, VMEM_SPEC], out_specs=VMEM_SPEC,
                          out_shape=jax.ShapeDtypeStruct(own.shape, F32),
                          compiler_params=pltpu.CompilerParams(vmem_limit_bytes=_VMEM_LIMIT))(own, land)


OTHER_CHIPS = ((1, 0), (0, 1), (1, 1))


SIBLING_COLLECTIVE_ID = 6


def _sibling_handshake():
    x, y, c = _place()
    barrier = pltpu.get_barrier_semaphore()
    pl.semaphore_signal(barrier, inc=1, device_id=(x, y, 1 - c), device_id_type=MESH)
    pl.semaphore_wait(barrier, 1)


def _sibling_swap(name, src, halves):
    half = src.shape[-2] // 2
    out_shape = (src.shape[0], half, 1024) if halves else src.shape

    def body(s_ref, o_ref, send_sem, recv_sem):
        x, y, c = _place()
        _sibling_handshake()
        part = s_ref.at[:, pl.ds(pl.multiple_of((1 - c) * half, 8), half)] if halves else s_ref
        cp = pltpu.make_async_remote_copy(src_ref=part, dst_ref=o_ref, send_sem=send_sem, recv_sem=recv_sem,
                                          device_id=(x, y, 1 - c), device_id_type=MESH)
        cp.start()
        cp.wait()

    return pl.pallas_call(
        body, name=name, in_specs=[ANY], out_specs=ANY, out_shape=jax.ShapeDtypeStruct(out_shape, src.dtype),
        scratch_shapes=[pltpu.SemaphoreType.DMA, pltpu.SemaphoreType.DMA],
        compiler_params=pltpu.CompilerParams(collective_id=SIBLING_COLLECTIVE_ID))(src)


HBM = pl.BlockSpec(memory_space=pltpu.HBM)
SEM = pl.BlockSpec(memory_space=pltpu.SEMAPHORE)


def _exchange_peers(mode):
    x, y, c = _place()
    if mode == "all":
        return [(1 - x if k & 4 else x, 1 - y if k & 2 else y, 1 - c if k & 1 else c) for k in range(1, 8)]
    return [(1 - x if fx else x, 1 - y if fy else y, c) for fx, fy in OTHER_CHIPS]


def _chip_copies(mode, src_ref, land_ref, send_sems, recv_sems):
    x, y, c = _place()
    k = 2 * x + y
    copies = []
    for j, peer in enumerate(_exchange_peers(mode)):
        if mode == "gather":
            half = src_ref.shape[0] // 2
            mine = pl.ds(pl.multiple_of(c * half, 16), half)
            src, dst = src_ref.at[mine], land_ref.at[k, mine]
        elif mode == "scatter":
            src, dst = src_ref.at[2 * peer[0] + peer[1]], land_ref.at[k]
        else:
            src, dst = src_ref, land_ref.at[4 * x + 2 * y + c]
        copies.append(pltpu.make_async_remote_copy(src_ref=src, dst_ref=dst, send_sem=send_sems.at[j],
                                                   recv_sem=recv_sems.at[j], device_id=peer, device_id_type=MESH))
    return copies


def _exchange_start(name, collective_id, mode, src, land, after=None):
    deps = [] if after is None else [after]
    npeers = 7 if mode == "all" else 3

    def body(s_ref, l_ref, *rest):
        send_sems, recv_sems, s_thru, l_thru, token = rest[len(deps):]
        barrier = pltpu.get_barrier_semaphore()
        for peer in _exchange_peers(mode):
            pl.semaphore_signal(barrier, inc=1, device_id=peer, device_id_type=MESH)
        pl.semaphore_wait(barrier, npeers)
        for cp in _chip_copies(mode, s_ref, l_ref, send_sems, recv_sems):
            cp.start()
        token[...] = jnp.zeros_like(token)

    return pl.pallas_call(
        body, name=name,
        out_shape=(pltpu.SemaphoreType.DMA((npeers,)), pltpu.SemaphoreType.DMA((npeers,)),
                   pltpu.HBM(src.shape, src.dtype),
                   pltpu.HBM(land.shape, land.dtype), jax.ShapeDtypeStruct((8, 128), F32)),
        in_specs=(HBM, HBM) + tuple(ANY for _ in deps), out_specs=(SEM, SEM, HBM, HBM, VMEM_SPEC),
        input_output_aliases={0: 2, 1: 3},
        compiler_params=pltpu.CompilerParams(has_side_effects=pltpu.SideEffectType.DATAFLOW_SIDE_EFFECTING,
                                             collective_id=collective_id))(
            pltpu.with_memory_space_constraint(src, pltpu.HBM), pltpu.with_memory_space_constraint(land, pltpu.HBM),
            *deps)


def _exchange_wait(name, mode, started, after):
    send_sems, recv_sems, s_thru, l_thru, _ = started
    deps = list(after) if isinstance(after, (tuple, list)) else [after]

    def body(s_ref, l_ref, send_sems, recv_sems, *rest):
        for cp in _chip_copies(mode, s_ref, l_ref, send_sems, recv_sems):
            cp.wait_send()
            cp.wait_recv()

    return pl.pallas_call(
        body, name=name, out_shape=(pltpu.HBM(s_thru.shape, s_thru.dtype), pltpu.HBM(l_thru.shape, l_thru.dtype)),
        in_specs=(HBM, HBM, SEM, SEM) + tuple(ANY for _ in deps), out_specs=(HBM, HBM),
        input_output_aliases={0: 0, 1: 1},
        compiler_params=pltpu.CompilerParams(has_side_effects=pltpu.SideEffectType.DATAFLOW_SIDE_EFFECTING))(
            s_thru, l_thru, send_sems, recv_sems, *deps)


def _allgather_finish(tag, land):
    half = land.shape[1] // 2

    def body(l_ref, o_ref, send_sem, recv_sem):
        x, y, c = _place()
        _sibling_handshake()
        mine = pl.ds(pl.multiple_of(c * half, 16), half)
        swap = pltpu.make_async_remote_copy(src_ref=o_ref.at[:, mine], dst_ref=o_ref.at[:, mine], send_sem=send_sem,
                                            recv_sem=recv_sem, device_id=(x, y, 1 - c), device_id_type=MESH)
        swap.start()
        swap.wait()

    return pl.pallas_call(
        body, name="allgather_finish_" + tag, in_specs=[ANY], out_specs=ANY, input_output_aliases={0: 0},
        out_shape=jax.ShapeDtypeStruct(land.shape, land.dtype),
        scratch_shapes=[pltpu.SemaphoreType.DMA, pltpu.SemaphoreType.DMA],
        compiler_params=pltpu.CompilerParams(collective_id=SIBLING_COLLECTIVE_ID))(land)


def _pair_sum(tag, g, r1, c):
    rows = g.shape[1]
    half = rows // 2
    th = _row_tile(half, 256, 16)
    nblk = half // th

    def body(c_ref, g_ref, r_ref, o_ref, o2_ref):
        o_ref[...] = (g_ref[...] + r_ref[...]).astype(o_ref.dtype)
        o2_ref[...] = o_ref[...]

    spec = pl.BlockSpec((None, th, 1024), lambda k, i, c_ref: (k, i, 0))
    grid_spec = pltpu.PrefetchScalarGridSpec(
        num_scalar_prefetch=1, grid=(4, nblk),
        in_specs=[pl.BlockSpec((None, th, 1024), lambda k, i, c_ref: (k, c_ref[0] * nblk + i, 0)), spec],
        out_specs=[spec, spec])
    return pl.pallas_call(body, name="grad_pair_sum_" + tag, grid_spec=grid_spec,
                          out_shape=[jax.ShapeDtypeStruct((4, half, 1024), BF16)] * 2,
                          compiler_params=_params(("parallel", "parallel")))(c, g, r1)


def _chip_sum(tag, q, after=None):
    half = q.shape[1]
    th = _row_tile(half, 256, 16)
    deps = [] if after is None else [after]

    def body(a, b, c, d, *rest):
        rest[-1][...] = ((a[...].astype(F32) + b[...].astype(F32)) + c[...].astype(F32)) + d[...].astype(F32)

    specs = [pl.BlockSpec((None, th, 1024), functools.partial(lambda i, k: (k, i, 0), k=k)) for k in range(4)]
    return pl.pallas_call(body, name="grad_chip_sum_" + tag, grid=(half // th,), in_specs=specs + [ANY for _ in deps],
                          out_specs=pl.BlockSpec((th, 1024), lambda i: (i, 0)),
                          out_shape=jax.ShapeDtypeStruct((half, 1024), F32),
                          compiler_params=_params(("parallel",)))(q, q, q, q, *deps)


def _join_halves(tag, f, r, c):
    half = f.shape[0]
    th = _row_tile(half, 256)
    nblk = half // th

    def body(c_ref, f_ref, r_ref, o_ref):
        mine = (pl.program_id(0) == c_ref[0])
        o_ref[...] = jnp.where(mine, f_ref[...], r_ref[...])

    spec = pl.BlockSpec((th, 1024), lambda h, i, c_ref: (i, 0))
    grid_spec = pltpu.PrefetchScalarGridSpec(
        num_scalar_prefetch=1, grid=(2, nblk), in_specs=[spec, spec],
        out_specs=pl.BlockSpec((th, 1024), lambda h, i, c_ref: (h * nblk + i, 0)))
    return pl.pallas_call(body, name="grad_join_halves_" + tag, grid_spec=grid_spec,
                          out_shape=jax.ShapeDtypeStruct((2 * half, 1024), F32),
                          compiler_params=_params(("parallel", "parallel")))(c, f, r)


BIG_ARGS = ("in_w_even", "out_w_even", "qkv_w", "o_w", "ffn_gate_w", "ffn_up_w", "ffn_down_w")
def _ffn_pieces(layer):
    return tuple((f"{n}{layer}", 704, 704) for n in ("gate_wt", "up_wt", "down_w"))


IN_SLAB = (("in_wt", 1156, 1184),)
LAYER0_REST_SLAB = (("out_w", 512, 512),) + _ffn_pieces(0)
LAYER1_SLAB = (("qkv_wt", 320, 320), ("o_w", 256, 256)) + _ffn_pieces(1)
FFN0_SLAB = _ffn_pieces(0)
MIXER0_SLAB = (("in_wt", 1156, 1280), ("out_w", 512, 512))


def _slab(pieces, spec):
    parts = []
    for name, rows, room in spec:
        p = pieces[name]
        parts.append(jnp.pad(p, [(0, 0)] * (p.ndim - 2) + [(0, room - rows), (0, 0)]) if room > rows else p)
    return jnp.concatenate(parts, axis=-2) if len(parts) > 1 else parts[0]


def _unslab(slab, spec):
    out, off = {}, 0
    for name, rows, room in spec:
        out[name] = slab[..., off:off + rows, :]
        off += room
    return out


def _share_pieces(w):
    return {"in_wt": w["in_w_even"][0].T, "out_w": w["out_w_even"][0], "qkv_wt": w["qkv_w"][0].T, "o_w": w["o_w"][0],
            "gate_wt0": w["ffn_gate_w"][0].T, "gate_wt1": w["ffn_gate_w"][1].T,
            "up_wt0": w["ffn_up_w"][0].T, "up_wt1": w["ffn_up_w"][1].T,
            "down_w0": w["ffn_down_w"][0], "down_w1": w["ffn_down_w"][1]}


def _pieces_to_shares(p):
    return {"in_w_even": p["in_wt"].T[None], "out_w_even": p["out_w"][None], "qkv_w": p["qkv_wt"].T[None],
            "o_w": p["o_w"][None], "ffn_gate_w": jnp.stack([p["gate_wt0"].T, p["gate_wt1"].T]),
            "ffn_up_w": jnp.stack([p["up_wt0"].T, p["up_wt1"].T]),
            "ffn_down_w": jnp.stack([p["down_w0"], p["down_w1"]])}


def _chips_from_full(G, spec):
    return _slab({k: v.reshape(4, -1, D) for k, v in G.items()}, spec)


def _pack_small(parts):
    padded = []
    for p in parts:
        p = p.reshape(-1).astype(F32)
        padded.append(jnp.pad(p, (0, (-p.shape[0]) % 1024)))
    return jnp.concatenate(padded).reshape(-1, 128)


def _unpack_small(slab, shapes):
    flat, out, off = slab.reshape(-1), [], 0
    for shp in shapes:
        size = math.prod(shp)
        out.append(flat[off:off + size].reshape(shp))
        off += size + (-size) % 1024
    return out


SMALL = ("ada_b", "norm_mix_w", "norm_ffn_w", "conv_w", "conv_b", "dt_bias", "a_log", "d_skip", "ssm_norm_w",
         "gmlp_ln_w", "gmlp_ln_b", "gmlp_ws", "gmlp_bs", "qkv_b", "o_b", "sinks", "rel_table", "final_norm_w")
SMALL_SPLIT = {"conv_w": 1536, "qkv_b": 1280, "o_b": 1024}
WEIGHTS = ("ada_w", "ada_b", "norm_mix_w", "norm_ffn_w", "in_w_even", "conv_w", "conv_b", "dt_bias", "a_log", "d_skip",
           "ssm_norm_w", "gmlp_ln_w", "gmlp_ln_b", "gmlp_ws", "gmlp_bs", "out_w_even", "qkv_w", "qkv_b", "o_w", "o_b",
           "sinks", "rel_table", "ffn_gate_w", "ffn_up_w", "ffn_down_w", "final_norm_w")


def kernel(x, c, ada_w, ada_b, norm_mix_w, norm_ffn_w, in_w_even, conv_w, conv_b, dt_bias, a_log, d_skip, ssm_norm_w, gmlp_ln_w, gmlp_ln_b, gmlp_ws, gmlp_bs, out_w_even, qkv_w, qkv_b, o_w, o_b, sinks, rel_table, ffn_gate_w, ffn_up_w, ffn_down_w, final_norm_w, loss_target, m_ada_w, m_ada_b, m_norm_mix_w, m_norm_ffn_w, m_in_w_even, m_conv_w, m_conv_b, m_dt_bias, m_a_log, m_d_skip, m_ssm_norm_w, m_gmlp_ln_w, m_gmlp_ln_b, m_gmlp_ws, m_gmlp_bs, m_out_w_even, m_qkv_w, m_qkv_b, m_o_w, m_o_b, m_sinks, m_rel_table, m_ffn_gate_w, m_ffn_up_w, m_ffn_down_w, m_final_norm_w, v_ada_w, v_ada_b, v_norm_mix_w, v_norm_ffn_w, v_in_w_even, v_conv_w, v_conv_b, v_dt_bias, v_a_log, v_d_skip, v_ssm_norm_w, v_gmlp_ln_w, v_gmlp_ln_b, v_gmlp_ws, v_gmlp_bs, v_out_w_even, v_qkv_w, v_qkv_b, v_o_w, v_o_b, v_sinks, v_rel_table, v_ffn_gate_w, v_ffn_up_w, v_ffn_down_w, v_final_norm_w):
    args = dict(locals())
    w = {n: args[n] for n in WEIGHTS}
    m = {n: args["m_" + n] for n in WEIGHTS}
    v = {n: args["v_" + n] for n in WEIGHTS}
    ax, ay, ac = _place()
    me = 4 * ax + 2 * ay + ac
    chip = 2 * ax + ay
    south = (ac == 0).astype(F32)
    c_arr = jnp.reshape(ac, (1,)).astype(jnp.int32)

    c_all = _allreduce_small("gather_cond", lax.dynamic_update_slice(jnp.zeros((8, D), F32), c, (me, 0)).reshape(64, 128))
    c_all = c_all.reshape(8, D)
    n_ada = ada_w.shape[2]
    mod_cols = _ada_fwd(c_all, ada_w, lax.dynamic_slice(ada_b, (0, chip * n_ada), (2, n_ada)).reshape(2, 1, n_ada))
    pieces = [lax.dynamic_update_slice(jnp.zeros((2, 8, 6 * D), F32), mod_cols, (0, 0, chip * n_ada))]
    split_names = list(SMALL_SPLIT)
    for n in split_names:
        full = SMALL_SPLIT[n]
        local = w[n]
        idx = (0,) * (local.ndim - 1) + (chip * local.shape[-1],)
        pieces.append(lax.dynamic_update_slice(jnp.zeros(local.shape[:-1] + (full,), F32), local, idx))
    shapes = [p.shape for p in pieces]
    mod_slab = _allreduce_small("gather_mod", _pack_small(pieces) * south)
    gathered = _unpack_small(mod_slab, shapes)
    mod = lax.dynamic_slice(gathered[0], (0, me, 0), (2, 1, 6 * D)).reshape(2, 6 * D)
    P = {n: w[n] for n in SMALL if n not in SMALL_SPLIT and n != "ada_b"}
    for n, full in zip(split_names, gathered[1:]):
        P[n] = full
    P["final_norm_w"] = final_norm_w.reshape(1, D)

    pieces = _share_pieces(w)
    cast = {"in_wt": pieces["in_wt"].astype(_MXU)}

    def start_gather(tag, collective_id, share, after):
        return _exchange_start("allgather_start_" + tag, collective_id, "gather", share,
                               lax.empty((4,) + share.shape, share.dtype), after=after)

    def finish_gather(tag, started, spec, after):
        land = _exchange_wait("allgather_wait_" + tag, "gather", started, after)[1]
        out = {}
        for name, piece in _unslab(_allgather_finish(tag, land), spec).items():
            out[name] = lax.dynamic_update_slice(piece.reshape(-1, D), cast[name], (chip * piece.shape[1], 0))
        return out

    gather_in = start_gather("in", 7, _slab(cast, IN_SLAB), mod_slab)
    zero = gather_in[4][0, 0]
    cast.update({k: (p + zero).astype(_MXU) for k, p in pieces.items() if k != "in_wt"})
    share0, share1 = _slab(cast, LAYER0_REST_SLAB), _slab(cast, LAYER1_SLAB)
    w_in = finish_gather("in", gather_in, IN_SLAB, (share0, share1))["in_wt"]
    gather0 = start_gather("0", 1, share0, w_in)
    gather1 = start_gather("1", 2, share1, gather0[4])

    def start_reduce(tag, collective_id, G, spec, after=None):
        gp = _chips_from_full(G, spec)
        p, q = _pair_sum(tag, gp, _sibling_swap("grad_pair_exchange_" + tag, gp, True), c_arr)
        return _exchange_start("grad_exchange_start_" + tag, collective_id, "scatter", p, q, after=after)

    def finish_reduce(tag, started, spec, after, behind=None):
        q = _exchange_wait("grad_exchange_wait_" + tag, "scatter", started, after)[1]
        fin = _chip_sum(tag, q, after=behind)
        total = _join_halves(tag, fin, _sibling_swap("grad_final_exchange_" + tag, fin, False), c_arr)
        return _unslab(total, spec)

    reduces = {}

    def grads1(G1):
        reduces["1"] = start_reduce("1", 3, G1, LAYER1_SLAB)
        return reduces["1"][4]

    def grads_ffn0(G):
        reduces["f"] = start_reduce("f", 4, G, FFN0_SLAB)
        return reduces["f"][4]

    io = {"start": gather1[4],
          "weights0": lambda after: finish_gather("0", gather0, LAYER0_REST_SLAB, after),
          "weights1": lambda after: finish_gather("1", gather1, LAYER1_SLAB, after),
          "grads1": grads1, "grads_ffn0": grads_ffn0}
    sq, grad_x, dmod, G0, g = _local_step(x[0], loss_target[0], mod, w_in, P, io)
    loss = lax.psum(0.5 * sq[0, 0] / D, ("x", "y", "c"))

    g["final_norm_w"] = g["final_norm_w"].reshape(D)
    small_names = [n for n in SMALL if n != "ada_b"]
    pieces = [lax.dynamic_update_slice(jnp.zeros((2, 8, 6 * D), F32), dmod.reshape(2, 1, 6 * D), (0, me, 0))]
    pieces += [g[n] for n in small_names]
    shapes = [p.shape for p in pieces]
    small_own = _pack_small(pieces)
    small_started = _exchange_start("small_grads_start", 8, "all", small_own, lax.empty((8,) + small_own.shape, F32))
    reduces["m"] = start_reduce("m", 5, G0, MIXER0_SLAB, after=small_started[4])
    shares = finish_reduce("1", reduces["1"], LAYER1_SLAB, grad_x, behind=reduces["m"][4])
    shares.update(finish_reduce("f", reduces["f"], FFN0_SLAB, grad_x, behind=reduces["m"][4]))
    small_own, small_land = _exchange_wait("small_grads_wait", "all", small_started, shares["down_w0"])
    reduced = _unpack_small(_sum_slots("small_grads_sum", small_own, small_land), shapes)
    dmod_all = reduced[0]
    grads = dict(zip(small_names, reduced[1:]))
    for n in split_names:
        full = grads[n]
        size = w[n].shape[-1]
        grads[n] = lax.dynamic_slice(full, (0,) * (full.ndim - 1) + (chip * size,), full.shape[:-1] + (size,))
    grads = {n: grads[n].reshape(w[n].shape) for n in small_names}
    dw_ada, db_ada = _ada_bwd(c_all, lax.dynamic_slice(dmod_all, (0, 0, chip * n_ada), (2, 8, n_ada)), dmod_all)
    grads["ada_w"], grads["ada_b"] = dw_ada, db_ada.reshape(2, 6 * D)

    delta, new_m, new_v = {}, {}, {}

    def update(n):
        cols = w[n].shape[-1]
        d_, m_, v_ = _adamw("adamw_" + n, w[n].reshape(-1, cols), grads[n].reshape(-1, cols), m[n].reshape(-1, cols),
                            v[n].reshape(-1, cols))
        delta[n], new_m[n], new_v[n] = d_.reshape(w[n].shape), m_.reshape(w[n].shape), v_.reshape(w[n].shape)

    update("ada_w")
    shapes = [w[n].shape for n in SMALL]
    packed = [_pack_small([t[n] for n in SMALL]) for t in (w, grads, m, v)]
    outs = _adamw("adamw_small", *packed)
    for dst, slab in zip((delta, new_m, new_v), outs):
        for n, t in zip(SMALL, _unpack_small(slab, shapes)):
            dst[n] = t
    shares.update(finish_reduce("m", reduces["m"], MIXER0_SLAB, outs[0]))
    grads.update(_pieces_to_shares(shares))
    for n in BIG_ARGS:
        update(n)
    return (loss, grad_x[None], *[grads[n] for n in WEIGHTS], *[delta[n] for n in WEIGHTS],
            *[new_m[n] for n in WEIGHTS], *[new_v[n] for n in WEIGHTS])
```

```python
import functools
import math

import numpy as np
import jax
import jax.numpy as jnp
from jax import lax
from jax.experimental import pallas as pl
from jax.experimental.pallas import tpu as pltpu

F32 = jnp.float32
BF16 = jnp.bfloat16
_MXU = jnp.bfloat16
_VMEM_LIMIT = 56 * 1024 * 1024
MXU_COLS = 256
D = 1024
L = 128
NSTATE = 128
EPS = 1e-6
NEG_INF = -1e30
FFN = 2816
ADAM_LR, ADAM_B1, ADAM_B2, ADAM_EPS, ADAM_WD, ADAM_STEP = 0.001, 0.9, 0.999, 1e-08, 0.01, 10
MESH = pl.DeviceIdType.MESH
ANY = pl.BlockSpec(memory_space=pl.ANY)

NN = (((1,), (0,)), ((), ()))
NT = (((1,), (1,)), ((), ()))
TN = (((0,), (0,)), ((), ()))


def _dot(a, b, dn=NN):
    return lax.dot_general(a.astype(_MXU), b.astype(_MXU), dn, preferred_element_type=F32)


def _params(sem=None):
    return pltpu.CompilerParams(dimension_semantics=sem, vmem_limit_bytes=_VMEM_LIMIT)


def _sigmoid(x):
    return 1.0 / (1.0 + jnp.exp(-x))


def _softplus(x):
    return jnp.maximum(x, 0.0) + jnp.log(1.0 + jnp.exp(-jnp.abs(x)))


def _gelu(x):
    return 0.5 * x * (1.0 + lax.erf(x * (2.0 ** -0.5)))


def _gelu_grad(x):
    return 0.5 * (1.0 + lax.erf(x * (2.0 ** -0.5))) + x * jnp.exp(-0.5 * x * x) * (1.0 / math.sqrt(2.0 * math.pi))


def _silu_grad(a):
    sg = _sigmoid(a)
    return sg * (1.0 + a * (1.0 - sg))


def _rowwise(name, fn, rows, vecs, out_rows, out_accs=(), tr=512, after=None):
    S = rows[0].shape[0]
    tr = min(tr, S)
    assert S % tr == 0
    nr, nv, no, na = len(rows), len(vecs), len(out_rows), len(out_accs)
    deps = [] if after is None else [after]

    def body(*refs):
        ins, outs = refs[:nr + nv], refs[nr + nv + len(deps):]
        res = fn(*[r[...] for r in ins])
        if not isinstance(res, (tuple, list)):
            res = (res,)
        for k in range(no):
            outs[k][...] = res[k].astype(outs[k].dtype)
        if na:
            @pl.when(pl.program_id(0) == 0)
            def _():
                for k in range(na):
                    outs[no + k][...] = jnp.zeros_like(outs[no + k])
            for k in range(na):
                outs[no + k][...] += res[no + k]

    in_specs = [pl.BlockSpec((tr, a.shape[1]), lambda i: (i, 0)) for a in rows]
    in_specs += [pl.BlockSpec(v.shape, lambda i: (0, 0)) for v in vecs] + [ANY for _ in deps]
    out_specs = [pl.BlockSpec((tr, c), lambda i: (i, 0)) for c, _ in out_rows]
    out_specs += [pl.BlockSpec(s, lambda i: (0, 0)) for s in out_accs]
    out_shape = [jax.ShapeDtypeStruct((S, c), dt) for c, dt in out_rows]
    out_shape += [jax.ShapeDtypeStruct(s, F32) for s in out_accs]
    return pl.pallas_call(body, name=name, grid=(S // tr,), in_specs=in_specs, out_specs=out_specs,
                          out_shape=out_shape, compiler_params=_params(("arbitrary",)))(*rows, *vecs, *deps)


def _col_tile(n, cap):
    if n <= cap or n % 128:
        return n
    best = 128
    for t in range(128, cap + 1, 128):
        if n % t == 0:
            best = t
    return best


def _mm(name, As, Bs, mode, outs, epi=None, groups=None, extras=(), vecs=(), tm=512, tn_cap=1536, whole_rows=False):
    M = As[0].shape[0]
    N = Bs[0].shape[1] if mode == "nn" else Bs[0].shape[0]
    tm = min(tm, M)
    tn = _col_tile(N, tn_cap)
    assert M % tm == 0 and N % tn == 0
    npair = len(As)
    groups = groups or [0] * npair
    ng = max(groups) + 1
    nx, nv = len(extras), len(vecs)
    dn = NN if mode == "nn" else NT

    def body(*refs):
        a_refs, b_refs = refs[:npair], refs[npair:2 * npair]
        x_refs = refs[2 * npair:2 * npair + nx]
        v_refs = refs[2 * npair + nx:2 * npair + nx + nv]
        o_refs = refs[2 * npair + nx + nv:]
        step = tn if (epi is None or whole_rows) else min(tn, MXU_COLS)
        for col in range(0, tn, step):
            sl = slice(col, min(col + step, tn))
            accs = [None] * ng
            for k in range(npair):
                b = b_refs[k][:, sl] if mode == "nn" else b_refs[k][sl, :]
                d = _dot(a_refs[k][...], b, dn)
                accs[groups[k]] = d if accs[groups[k]] is None else accs[groups[k]] + d
            args = accs + [x[:, sl] for x in x_refs] + [v[:, sl] for v in v_refs]
            res = epi(*args) if epi is not None else tuple(accs)
            if not isinstance(res, (tuple, list)):
                res = (res,)
            for o, r in zip(o_refs, res):
                o[:, sl] = r.astype(o.dtype)

    in_specs = [pl.BlockSpec((tm, a.shape[1]), lambda i, j: (i, 0)) for a in As]
    if mode == "nn":
        in_specs += [pl.BlockSpec((b.shape[0], tn), lambda i, j: (0, j)) for b in Bs]
    else:
        in_specs += [pl.BlockSpec((tn, b.shape[1]), lambda i, j: (j, 0)) for b in Bs]
    in_specs += [pl.BlockSpec((tm, tn), lambda i, j: (i, j)) for _ in extras]
    in_specs += [pl.BlockSpec((1, tn), lambda i, j: (0, j)) for _ in vecs]
    out_specs = [pl.BlockSpec((tm, tn), lambda i, j: (i, j)) for _ in outs]
    out_shape = [jax.ShapeDtypeStruct((M, N), dt) for dt in outs]
    return pl.pallas_call(body, name=name, grid=(M // tm, N // tn), in_specs=in_specs, out_specs=out_specs,
                          out_shape=out_shape, compiler_params=_params(("parallel", "parallel")))(
                              *As, *Bs, *extras, *vecs)


def _mm_shared_lhs(name, A, Bs, tm=512):
    M, K = A.shape
    tm = min(tm, M)
    assert M % tm == 0
    n = len(Bs)

    def body(a_ref, *refs):
        a = a_ref[...]
        for b_ref, o_ref in zip(refs[:n], refs[n:]):
            o_ref[...] = _dot(a, b_ref[...], NT)

    return pl.pallas_call(
        body, name=name, grid=(M // tm,),
        in_specs=[pl.BlockSpec((tm, K), lambda i: (i, 0))] + [pl.BlockSpec(b.shape, lambda i: (0, 0)) for b in Bs],
        out_specs=[pl.BlockSpec((tm, b.shape[0]), lambda i: (i, 0)) for b in Bs],
        out_shape=[jax.ShapeDtypeStruct((M, b.shape[0]), F32) for b in Bs],
        compiler_params=_params(("parallel",)))(A, *Bs)


def _mm_tn_shared_rhs(name, As, B, tk=256):
    S, N = B.shape
    tk = min(tk, S)
    assert S % tk == 0
    n = len(As)

    def body(*refs):
        a_refs, b_ref, o_refs = refs[:n], refs[n], refs[n + 1:]

        @pl.when(pl.program_id(0) == 0)
        def _():
            for o_ref in o_refs:
                o_ref[...] = jnp.zeros_like(o_ref)
        b = b_ref[...]
        for a_ref, o_ref in zip(a_refs, o_refs):
            o_ref[...] += _dot(a_ref[...], b, TN)

    return pl.pallas_call(
        body, name=name, grid=(S // tk,),
        in_specs=[pl.BlockSpec((tk, a.shape[1]), lambda k: (k, 0)) for a in As] + [pl.BlockSpec((tk, N), lambda k: (k, 0))],
        out_specs=[pl.BlockSpec((a.shape[1], N), lambda k: (0, 0)) for a in As],
        out_shape=[jax.ShapeDtypeStruct((a.shape[1], N), F32) for a in As],
        compiler_params=_params(("arbitrary",)))(*As, B)


def _mm_tn(name, A, B, tk=512, t2_cap=1536):
    S, K1 = A.shape
    N2 = B.shape[1]
    tk = min(tk, S)
    t2 = _col_tile(N2, t2_cap)
    assert S % tk == 0 and N2 % t2 == 0

    def body(a_ref, b_ref, o_ref):
        @pl.when(pl.program_id(1) == 0)
        def _():
            o_ref[...] = jnp.zeros_like(o_ref)
        o_ref[...] += _dot(a_ref[...], b_ref[...], TN)

    return pl.pallas_call(
        body, name=name, grid=(N2 // t2, S // tk),
        in_specs=[pl.BlockSpec((tk, K1), lambda j, k: (k, 0)), pl.BlockSpec((tk, t2), lambda j, k: (k, j))],
        out_specs=pl.BlockSpec((K1, t2), lambda j, k: (0, j)),
        out_shape=jax.ShapeDtypeStruct((K1, N2), F32),
        compiler_params=_params(("parallel", "arbitrary")))(A, B)


def _norm_mod(x, nw, sc, sh):
    rstd = lax.rsqrt(jnp.mean(x * x, axis=-1, keepdims=True) + EPS)
    return (x * rstd * nw) * (1.0 + sc) + sh


def _norm_mod_fwd(name, x, nw, sc, sh, after=None):
    return _rowwise(name, _norm_mod, [x], [nw, sc, sh], [(D, BF16)], after=after)[0]


def _norm_mod_bwd(name, x, dh, dres, nw, sc, gate=None, after=None):
    def fn(x, dh, dres, *rest):
        nw, sc = rest[-3:-1] if gate else rest
        rstd = lax.rsqrt(jnp.mean(x * x, axis=-1, keepdims=True) + EPS)
        xh = x * rstd
        dn = dh * (1.0 + sc)
        dxh = dn * nw
        dx = dres + rstd * (dxh - xh * jnp.mean(dxh * xh, axis=-1, keepdims=True))
        sums = [jnp.sum(dh, axis=0, keepdims=True), jnp.sum(dh * (xh * nw), axis=0, keepdims=True),
                jnp.sum(dn * xh, axis=0, keepdims=True)]
        if not gate:
            return (dx, *sums)
        dy = dx * rest[-1]
        return (dx, dy, *sums, jnp.sum(dx * rest[0], axis=0, keepdims=True), jnp.sum(dy, axis=0, keepdims=True))
    if not gate:
        return _rowwise(name, fn, [x, dh, dres], [nw, sc], [(D, F32)], [(1, D)] * 3, after=after)
    return _rowwise(name, fn, [x, dh, dres, gate[0]], [nw, sc, gate[1]], [(D, F32), (D, BF16)], [(1, D)] * 5,
                    after=after)


def _loss_head(x, tgt, fw, y, g):
    def fn(x, tgt, y, fw, g):
        rstd = lax.rsqrt(jnp.mean(x * x, axis=-1, keepdims=True) + EPS)
        xh = x * rstd
        err = xh * fw - tgt
        dout = err * (1.0 / D)
        dxh = dout * fw
        dx = rstd * (dxh - xh * jnp.mean(dxh * xh, axis=-1, keepdims=True))
        sq = jnp.sum(jnp.sum(err * err, axis=1, keepdims=True), axis=0, keepdims=True)
        return (dx, dx * g, sq, jnp.sum(dout * xh, axis=0, keepdims=True), jnp.sum(dx * y, axis=0, keepdims=True))
    return _rowwise("loss_head", fn, [x, tgt, y], [fw, g], [(D, F32), (D, BF16)], [(1, 1), (1, D), (1, D)])


def _ffn_fwd(tag, h, wg, wu, wd, x, g2, next_norm=None):
    def act(a, b):
        return a, b, a * _sigmoid(a) * b
    a, b, f = _mm(f"ffn_up_{tag}", [h, h], [wg, wu], "nt", [BF16, BF16, BF16], epi=act, groups=[0, 1], tn_cap=1408,
                  tm=1024)

    if next_norm is None:
        def res(y, x, g):
            return y, x + g * y
        y, xo = _mm(f"ffn_down_{tag}", [f], [wd], "nn", [F32, F32], epi=res, extras=[x], vecs=[g2])
        return a, b, f, y, xo, None

    def res_norm(y, x, g, nw, sc, sh):
        xo = x + g * y
        return y, xo, _norm_mod(xo, nw, sc, sh)
    assert wd.shape[1] == D
    y, xo, h_next = _mm(f"ffn_down_{tag}", [f], [wd], "nn", [F32, F32, BF16], epi=res_norm, extras=[x],
                        vecs=[g2, *next_norm], whole_rows=True)
    return a, b, f, y, xo, h_next


def _ffn_bwd(tag, dy, h, a, b, f, wg, wu, wd):
    def act_bwd(df, a, b):
        a, b = a.astype(F32), b.astype(F32)
        sg = _sigmoid(a)
        return df * b * (sg * (1.0 + a * (1.0 - sg))), df * (a * sg)
    da, db = _mm(f"ffn_dact_{tag}", [dy], [wd], "nt", [BF16, BF16], epi=act_bwd, extras=[a, b], tn_cap=1408, tm=1024)
    dwd = _mm_tn(f"ffn_dwd_{tag}", f, dy)
    dwg = _mm_tn(f"ffn_dwg_{tag}", da, h)
    dwu = _mm_tn(f"ffn_dwu_{tag}", db, h)
    dh = _mm(f"ffn_dh_{tag}", [da, db], [wg, wu], "nn", [F32])[0]
    return dh, dwg, dwu, dwd


def _conv_fwd(xr, w, b, tb=512):
    S, C = xr.shape
    tb = min(tb, S)

    def body(x_ref, halo_ref, w_ref, b_ref, pre_ref, out_ref):
        i = pl.program_id(0)
        halo = jnp.where(i > 0, halo_ref[...], 0.0)
        xe = jnp.concatenate([halo, x_ref[...]], axis=0)
        pre = w_ref[3:4, :] * x_ref[...] + b_ref[...]
        for j in (1, 2, 3):
            pre = pre + w_ref[3 - j:4 - j, :] * pltpu.roll(xe, j, axis=0)[8:, :]
        pre_ref[...] = pre
        out_ref[...] = pre * _sigmoid(pre)

    return pl.pallas_call(
        body, name="conv_fwd", grid=(S // tb,),
        in_specs=[pl.BlockSpec((tb, C), lambda i: (i, 0)),
                  pl.BlockSpec((8, C), lambda i: (jnp.maximum(i * (tb // 8) - 1, 0), 0)),
                  pl.BlockSpec((4, C), lambda i: (0, 0)), pl.BlockSpec((1, C), lambda i: (0, 0))],
        out_specs=[pl.BlockSpec((tb, C), lambda i: (i, 0))] * 2,
        out_shape=[jax.ShapeDtypeStruct((S, C), F32)] * 2,
        compiler_params=_params(("parallel",)))(xr, xr, w, b)


def _conv_bwd(dxc, pre, xr, w, tb=512):
    S, C = xr.shape
    tb = min(tb, S)
    nblk = S // tb

    def body(d_ref, p_ref, dn_ref, pn_ref, x_ref, w_ref, dx_ref, dw_ref, db_ref):
        i = pl.program_id(0)

        @pl.when(i == 0)
        def _():
            dw_ref[...] = jnp.zeros_like(dw_ref)
            db_ref[...] = jnp.zeros_like(db_ref)

        dpre = d_ref[...] * _silu_grad(p_ref[...])
        dnext = jnp.where(i < nblk - 1, dn_ref[...] * _silu_grad(pn_ref[...]), 0.0)
        pe = jnp.concatenate([dpre, dnext], axis=0)
        xx = x_ref[...]
        dx = w_ref[3:4, :] * dpre
        dw_ref[3:4, :] += jnp.sum(dpre * xx, axis=0, keepdims=True)
        for j in (1, 2, 3):
            ahead = pltpu.roll(pe, tb + 8 - j, axis=0)[:tb, :]
            dx = dx + w_ref[3 - j:4 - j, :] * ahead
            dw_ref[3 - j:4 - j, :] += jnp.sum(ahead * xx, axis=0, keepdims=True)
        dx_ref[...] = dx.astype(dx_ref.dtype)
        db_ref[...] += jnp.sum(dpre, axis=0, keepdims=True)

    blk = pl.BlockSpec((tb, C), lambda i: (i, 0))
    nxt = pl.BlockSpec((8, C), lambda i: (jnp.minimum((i + 1) * (tb // 8), S // 8 - 1), 0))
    return pl.pallas_call(
        body, name="conv_bwd", grid=(nblk,),
        in_specs=[blk, blk, nxt, nxt, blk, pl.BlockSpec((4, C), lambda i: (0, 0))],
        out_specs=[blk, pl.BlockSpec((4, C), lambda i: (0, 0)), pl.BlockSpec((1, C), lambda i: (0, 0))],
        out_shape=[jax.ShapeDtypeStruct((S, C), BF16), jax.ShapeDtypeStruct((4, C), F32),
                   jax.ShapeDtypeStruct((1, C), F32)],
        compiler_params=_params(("arbitrary",)))(dxc, pre, dxc, pre, xr, w)


def _iota(shape, dim):
    return lax.broadcasted_iota(jnp.int32, shape, dim)


def _colsel(m, lane, h):
    return jnp.sum(jnp.where(lane == h, m, 0.0), axis=1, keepdims=True)


def _cumsum_rows(v):
    r = _iota(v.shape, 0)
    k = 1
    while k < v.shape[0]:
        v = v + jnp.where(r >= k, pltpu.roll(v, k, axis=0), 0.0)
        k *= 2
    return v


def _suffix_sum_rows(v):
    n = v.shape[0]
    r = _iota(v.shape, 0)
    k = 1
    while k < n:
        v = v + jnp.where(r < n - k, pltpu.roll(v, n - k, axis=0), 0.0)
        k *= 2
    return v


def _ssd_fwd(xc, dtr, z, dtb, alog, dskl, nw):
    S = xc.shape[0]
    nc = S // L

    def body(xc_ref, dtr_ref, z_ref, dtb_ref, alog_ref, dsk_ref, nw_ref, ya_ref, y_ref, prev_ref,
             st_ref, cum_ref, cumT_ref):
        i = pl.program_id(0)

        @pl.when(i == 0)
        def _():
            st_ref[...] = jnp.zeros_like(st_ref)

        lane = _iota((L, 128), 1)
        lane1 = _iota((1, 128), 1)
        lo = lane < 64
        lo1 = lane1 < 64
        tril = _iota((L, L), 0) >= _iota((L, L), 1)
        dt = _softplus(dtr_ref[...] + dtb_ref[...])
        a_neg = -jnp.exp(alog_ref[...])
        cum = _cumsum_rows(dt * a_neg)
        cum_ref[...] = cum
        cumT_ref[...] = cum.T
        last_all = cum_ref[L - 1:L, :]
        prev_t = st_ref[...]
        prev_ref[0] = prev_t
        for g in range(2):
            bg = xc_ref[:, 1024 + g * 128:1152 + g * 128]
            cg = xc_ref[:, 1280 + g * 128:1408 + g * 128]
            gmat = _dot(cg, bg, NT)
            yoff = _dot(cg, prev_t[:, g * 512:(g + 1) * 512])
            bg_t = bg.T
            for jp in range(4):
                j = g * 4 + jp
                sl = slice(j * 128, (j + 1) * 128)
                xp = xc_ref[:, sl]
                cc = [_colsel(cum, lane, 2 * j), _colsel(cum, lane, 2 * j + 1)]
                cum_l = jnp.where(lo, cc[0], cc[1])
                dt_l = jnp.where(lo, _colsel(dt, lane, 2 * j), _colsel(dt, lane, 2 * j + 1))
                last_l = jnp.where(lo1, _colsel(last_all, lane1, 2 * j), _colsel(last_all, lane1, 2 * j + 1))
                xd = xp * dt_l
                ys = []
                for hh in range(2):
                    seg = cc[hh] - cumT_ref[2 * j + hh:2 * j + hh + 1, :]
                    dm = jnp.where(tril, jnp.exp(seg), 0.0)
                    ys.append(_dot(gmat * dm, xd))
                y_ref[:, sl] = (jnp.where(lo, ys[0], ys[1]) + jnp.exp(cum_l) * yoff[:, jp * 128:(jp + 1) * 128]
                                + dsk_ref[:, sl] * xp)
                st_ref[:, sl] = prev_t[:, sl] * jnp.exp(last_l) + _dot(bg_t, xd * jnp.exp(last_l - cum_l))
        for g in range(2):
            sl = slice(g * 512, (g + 1) * 512)
            zz = z_ref[:, sl]
            yg = y_ref[:, sl] * (zz * _sigmoid(zz))
            rstd = lax.rsqrt(jnp.mean(yg * yg, axis=-1, keepdims=True) + EPS)
            ya_ref[:, sl] = (yg * rstd * nw_ref[:, sl]).astype(ya_ref.dtype)

    blk = lambda c: pl.BlockSpec((L, c), lambda i: (i, 0))
    vec = lambda c: pl.BlockSpec((1, c), lambda i: (0, 0))
    return pl.pallas_call(
        body, name="ssd_fwd", grid=(nc,),
        in_specs=[blk(1536), blk(128), blk(1024), vec(128), vec(128), vec(1024), vec(1024)],
        out_specs=[blk(1024), blk(1024), pl.BlockSpec((1, NSTATE, 1024), lambda i: (i, 0, 0))],
        out_shape=[jax.ShapeDtypeStruct((S, 1024), BF16), jax.ShapeDtypeStruct((S, 1024), F32),
                   jax.ShapeDtypeStruct((nc, NSTATE, 1024), F32)],
        scratch_shapes=[pltpu.VMEM((NSTATE, 1024), F32), pltpu.VMEM((L, 128), F32), pltpu.VMEM((L, 128), F32)],
        compiler_params=_params(("arbitrary",)))(xc, dtr, z, dtb, alog, dskl, nw)


def _ssd_bwd(dya, y, z, xc, dtr, prev, dtb, alog, dskl, nw):
    S = xc.shape[0]
    nc = S // L

    def body(dya_ref, y_ref, z_ref, xc_ref, dtr_ref, prev_ref, dtb_ref, alog_ref, dsk_ref, nw_ref,
             dz_ref, dxc_ref, ddtr_ref, dnw_ref, ddsk_ref, dalog_ref, ddtb_ref,
             dst_ref, cum_ref, cumT_ref, dy_ref, dskacc_ref):
        i = pl.program_id(0)

        @pl.when(i == 0)
        def _():
            dst_ref[...] = jnp.zeros_like(dst_ref)
            dskacc_ref[...] = jnp.zeros_like(dskacc_ref)
            dnw_ref[...] = jnp.zeros_like(dnw_ref)
            dalog_ref[...] = jnp.zeros_like(dalog_ref)
            ddtb_ref[...] = jnp.zeros_like(ddtb_ref)

        lane = _iota((L, 128), 1)
        lane1 = _iota((1, 128), 1)
        lo = lane < 64
        lo1 = lane1 < 64
        r2, c2 = _iota((L, L), 0), _iota((L, L), 1)
        tril = r2 >= c2
        triu = r2 <= c2
        is_last = _iota((L, 1), 0) == L - 1

        for g in range(2):
            sl = slice(g * 512, (g + 1) * 512)
            zz = z_ref[:, sl]
            sg = _sigmoid(zz)
            zg = zz * sg
            yv = y_ref[:, sl]
            yg = yv * zg
            rstd = lax.rsqrt(jnp.mean(yg * yg, axis=-1, keepdims=True) + EPS)
            xh = yg * rstd
            d_out = dya_ref[:, sl]
            dnw_ref[:, sl] += jnp.sum(d_out * xh, axis=0, keepdims=True)
            dyn = d_out * nw_ref[:, sl]
            dyg = rstd * (dyn - xh * jnp.mean(dyn * xh, axis=-1, keepdims=True))
            dy_ref[:, sl] = dyg * zg
            dz_ref[:, sl] = (dyg * yv * (sg * (1.0 + zz * (1.0 - sg)))).astype(dz_ref.dtype)

        dtin = dtr_ref[...] + dtb_ref[...]
        dt = _softplus(dtin)
        a_neg = -jnp.exp(alog_ref[...])
        cum = _cumsum_rows(dt * a_neg)
        cum_ref[...] = cum
        cumT_ref[...] = cum.T
        last_all = cum_ref[L - 1:L, :]
        prev_t = prev_ref[0]
        dn_t = dst_ref[...]
        dcum = jnp.zeros((L, 128), F32)
        ddt = jnp.zeros((L, 128), F32)
        for g in range(2):
            gsl = slice(g * 512, (g + 1) * 512)
            bg = xc_ref[:, 1024 + g * 128:1152 + g * 128]
            cg = xc_ref[:, 1280 + g * 128:1408 + g * 128]
            gmat = _dot(cg, bg, NT)
            gmat_t = _dot(bg, cg, NT)
            pg = prev_t[:, gsl]
            zmat = _dot(cg, pg)
            dgm = jnp.zeros((L, L), F32)
            dgm_t = jnp.zeros((L, L), F32)
            db_acc = jnp.zeros((L, NSTATE), F32)
            dz_parts, cd_parts = [], []
            for jp in range(4):
                j = g * 4 + jp
                sl = slice(j * 128, (j + 1) * 128)
                xp = xc_ref[:, sl]
                dyp = dy_ref[:, sl]
                cc = [_colsel(cum, lane, 2 * j), _colsel(cum, lane, 2 * j + 1)]
                lc = [_colsel(last_all, lane1, 2 * j), _colsel(last_all, lane1, 2 * j + 1)]
                cum_l = jnp.where(lo, cc[0], cc[1])
                dt_l = jnp.where(lo, _colsel(dt, lane, 2 * j), _colsel(dt, lane, 2 * j + 1))
                last_l = jnp.where(lo1, lc[0], lc[1])
                e_l = jnp.exp(cum_l)
                dte_l = jnp.exp(last_l - cum_l)
                cd_l = jnp.exp(last_l)
                cd_parts.append(cd_l)
                xd = xp * dt_l
                dskacc_ref[:, sl] += jnp.sum(dyp * xp, axis=0, keepdims=True)
                dxp = dsk_ref[:, sl] * dyp
                t = dyp * (e_l * zmat[:, jp * 128:(jp + 1) * 128])
                dcc = [jnp.sum(jnp.where(lo, t, 0.0), axis=1, keepdims=True),
                       jnp.sum(jnp.where(lo, 0.0, t), axis=1, keepdims=True)]
                dz_parts.append(e_l * dyp)
                dnp_ = dn_t[:, sl]
                t2 = jnp.sum(dnp_ * prev_t[:, sl], axis=0, keepdims=True)
                dcd = [jnp.sum(jnp.where(lo1, t2, 0.0), axis=1, keepdims=True),
                       jnp.sum(jnp.where(lo1, 0.0, t2), axis=1, keepdims=True)]
                wm = _dot(bg, dnp_)
                dxd = wm * dte_l
                t3 = wm * xd
                ddte = [jnp.sum(jnp.where(lo, t3, 0.0), axis=1, keepdims=True),
                        jnp.sum(jnp.where(lo, 0.0, t3), axis=1, keepdims=True)]
                db_acc = db_acc + _dot(xd * dte_l, dnp_, NT)
                for hh in range(2):
                    h = 2 * j + hh
                    half = lo if hh == 0 else jnp.logical_not(lo)
                    row = cumT_ref[h:h + 1, :]
                    dm = jnp.where(tril, jnp.exp(cc[hh] - row), 0.0)
                    dm_t = jnp.where(triu, jnp.exp(row - cc[hh]), 0.0)
                    dym = jnp.where(half, dyp, 0.0)
                    u = _dot(dym, xd, NT) * dm
                    u_t = _dot(xd, dym, NT) * dm_t
                    dxd = dxd + _dot(gmat_t * dm_t, dym)
                    dcc[hh] = dcc[hh] + jnp.sum(u * gmat, axis=1, keepdims=True) - jnp.sum(u_t * gmat_t, axis=1, keepdims=True)
                    dgm = dgm + u
                    dgm_t = dgm_t + u_t
                    dte_c = jnp.exp(lc[hh] - cc[hh])
                    dcc[hh] = dcc[hh] - ddte[hh] * dte_c
                    endc = dcd[hh] * jnp.exp(lc[hh]) + jnp.sum(ddte[hh] * dte_c, axis=0, keepdims=True)
                    dcc[hh] = dcc[hh] + jnp.where(is_last, endc, 0.0)
                    dcum = jnp.where(lane == h, dcc[hh], dcum)
                dxc_ref[:, sl] = dxp + dxd * dt_l
                t4 = dxd * xp
                ddt = jnp.where(lane == 2 * j, jnp.sum(jnp.where(lo, t4, 0.0), axis=1, keepdims=True), ddt)
                ddt = jnp.where(lane == 2 * j + 1, jnp.sum(jnp.where(lo, 0.0, t4), axis=1, keepdims=True), ddt)
            dzg = jnp.concatenate(dz_parts, axis=1)
            dst_ref[:, gsl] = dn_t[:, gsl] * jnp.concatenate(cd_parts, axis=1) + _dot(cg.T, dzg)
            dxc_ref[:, 1280 + g * 128:1408 + g * 128] = _dot(dgm, bg) + _dot(dzg, pg, NT)
            dxc_ref[:, 1024 + g * 128:1152 + g * 128] = _dot(dgm_t, cg) + db_acc
        dla = _suffix_sum_rows(dcum)
        ddt = ddt + dla * a_neg
        dalog_ref[...] += jnp.sum(dla * dt, axis=0, keepdims=True) * a_neg
        ddtr = jnp.where(lane < 16, ddt * _sigmoid(dtin), 0.0)
        ddtr_ref[...] = ddtr.astype(ddtr_ref.dtype)
        ddtb_ref[...] += jnp.sum(ddtr, axis=0, keepdims=True)

        @pl.when(i == nc - 1)
        def _():
            seg = (_iota((1024, 128), 0) // 64 == _iota((1024, 128), 1)).astype(F32)
            acc8 = jnp.broadcast_to(dskacc_ref[...], (8, 1024))
            ddsk_ref[...] = lax.dot_general(acc8, seg, NN, precision=lax.Precision.HIGHEST,
                                            preferred_element_type=F32)

    rev = lambda c: pl.BlockSpec((L, c), lambda i: (nc - 1 - i, 0))
    vec = lambda c: pl.BlockSpec((1, c), lambda i: (0, 0))
    return pl.pallas_call(
        body, name="ssd_bwd", grid=(nc,),
        in_specs=[rev(1024), rev(1024), rev(1024), rev(1536), rev(128),
                  pl.BlockSpec((1, NSTATE, 1024), lambda i: (nc - 1 - i, 0, 0)),
                  vec(128), vec(128), vec(1024), vec(1024)],
        out_specs=[rev(1024), rev(1536), rev(128), vec(1024), pl.BlockSpec((8, 128), lambda i: (0, 0)),
                   vec(128), vec(128)],
        out_shape=[jax.ShapeDtypeStruct((S, 1024), BF16), jax.ShapeDtypeStruct((S, 1536), F32),
                   jax.ShapeDtypeStruct((S, 128), BF16), jax.ShapeDtypeStruct((1, 1024), F32),
                   jax.ShapeDtypeStruct((8, 128), F32), jax.ShapeDtypeStruct((1, 128), F32),
                   jax.ShapeDtypeStruct((1, 128), F32)],
        scratch_shapes=[pltpu.VMEM((NSTATE, 1024), F32), pltpu.VMEM((L, 128), F32), pltpu.VMEM((L, 128), F32),
                        pltpu.VMEM((L, 1024), F32), pltpu.VMEM((1, 1024), F32)],
        compiler_params=_params(("arbitrary",)))(dya, y, z, xc, dtr, prev, dtb, alog, dskl, nw)


def _layer_norm_parts(vg):
    mu = jnp.mean(vg, axis=-1, keepdims=True)
    vc = vg - mu
    rstd = lax.rsqrt(jnp.mean(vc * vc, axis=-1, keepdims=True) + EPS)
    return vc * rstd, rstd


def _gmlp_fwd(u, v, lnw, lnb, ws, bse, tb=512):
    S = u.shape[0]
    tb = min(tb, S)

    def body(u_ref, v_ref, lnw_ref, lnb_ref, ws_ref, bse_ref, o_ref, vn_ref):
        tril = _iota((L, L), 0) >= _iota((L, L), 1)
        xh, _ = _layer_norm_parts(_gelu(v_ref[...]))
        vn_ref[...] = xh * lnw_ref[...] + lnb_ref[...]
        for g in range(8):
            w = jnp.where(tril, ws_ref[g], 0.0)
            gs = slice(g * 128, (g + 1) * 128)
            for ch in range(tb // L):
                rs = slice(ch * L, (ch + 1) * L)
                sv = _dot(w, vn_ref[rs, gs]) + bse_ref[g]
                o_ref[rs, gs] = (_gelu(u_ref[rs, gs]) * sv).astype(o_ref.dtype)

    blk = pl.BlockSpec((tb, 1024), lambda i: (i, 0))
    vec = pl.BlockSpec((1, 1024), lambda i: (0, 0))
    cube = pl.BlockSpec((8, L, 128), lambda i: (0, 0, 0))
    return pl.pallas_call(
        body, name="gmlp_fwd", grid=(S // tb,), in_specs=[blk, blk, vec, vec, cube, cube], out_specs=blk,
        out_shape=jax.ShapeDtypeStruct((S, 1024), BF16), scratch_shapes=[pltpu.VMEM((tb, 1024), F32)],
        compiler_params=_params(("parallel",)))(u, v, lnw, lnb, ws, bse)


def _gmlp_bwd(dyb, u, v, lnw, lnb, ws, bse, tb=512):
    S = u.shape[0]
    tb = min(tb, S)

    def body(d_ref, u_ref, v_ref, lnw_ref, lnb_ref, ws_ref, bse_ref,
             du_ref, dv_ref, dws_ref, dbse_ref, dlnw_ref, dlnb_ref, vn_ref, dvn_ref):
        @pl.when(pl.program_id(0) == 0)
        def _():
            dws_ref[...] = jnp.zeros_like(dws_ref)
            dbse_ref[...] = jnp.zeros_like(dbse_ref)
            dlnw_ref[...] = jnp.zeros_like(dlnw_ref)
            dlnb_ref[...] = jnp.zeros_like(dlnb_ref)

        tril = _iota((L, L), 0) >= _iota((L, L), 1)
        vv = v_ref[...]
        xh, rstd = _layer_norm_parts(_gelu(vv))
        vn_ref[...] = xh * lnw_ref[...] + lnb_ref[...]
        for g in range(8):
            w = jnp.where(tril, ws_ref[g], 0.0)
            w_t = w.T
            gs = slice(g * 128, (g + 1) * 128)
            dw = jnp.zeros((L, L), F32)
            dbs = jnp.zeros((L, 128), F32)
            for ch in range(tb // L):
                rs = slice(ch * L, (ch + 1) * L)
                vn = vn_ref[rs, gs]
                sv = _dot(w, vn) + bse_ref[g]
                uu = u_ref[rs, gs]
                dd = d_ref[rs, gs]
                du_ref[rs, gs] = (dd * sv * _gelu_grad(uu)).astype(du_ref.dtype)
                dsv = dd * _gelu(uu)
                dw = dw + _dot(dsv, vn, NT)
                dbs = dbs + dsv
                dvn_ref[rs, gs] = _dot(w_t, dsv)
            dws_ref[g] += jnp.where(tril, dw, 0.0)
            dbse_ref[g] += dbs
        dvn = dvn_ref[...]
        dlnw_ref[...] += jnp.sum(dvn * xh, axis=0, keepdims=True)
        dlnb_ref[...] += jnp.sum(dvn, axis=0, keepdims=True)
        dxh = dvn * lnw_ref[...]
        dvg = rstd * (dxh - jnp.mean(dxh, axis=-1, keepdims=True) - xh * jnp.mean(dxh * xh, axis=-1, keepdims=True))
        dv_ref[...] = (dvg * _gelu_grad(vv)).astype(dv_ref.dtype)

    blk = pl.BlockSpec((tb, 1024), lambda i: (i, 0))
    vec = pl.BlockSpec((1, 1024), lambda i: (0, 0))
    cube = pl.BlockSpec((8, L, 128), lambda i: (0, 0, 0))
    return pl.pallas_call(
        body, name="gmlp_bwd", grid=(S // tb,), in_specs=[blk, blk, blk, vec, vec, cube, cube],
        out_specs=[blk, blk, cube, cube, vec, vec],
        out_shape=[jax.ShapeDtypeStruct((S, 1024), BF16), jax.ShapeDtypeStruct((S, 1024), BF16),
                   jax.ShapeDtypeStruct((8, L, 128), F32), jax.ShapeDtypeStruct((8, L, 128), F32),
                   jax.ShapeDtypeStruct((1, 1024), F32), jax.ShapeDtypeStruct((1, 1024), F32)],
        scratch_shapes=[pltpu.VMEM((tb, 1024), F32), pltpu.VMEM((tb, 1024), F32)],
        compiler_params=_params(("arbitrary",)))(dyb, u, v, lnw, lnb, ws, bse)


def _lane_sum(name, a):
    def body(a_ref, o_ref):
        o_ref[...] = jnp.sum(a_ref[...], axis=1, keepdims=True)
    return pl.pallas_call(body, name=name, out_shape=jax.ShapeDtypeStruct((a.shape[0], 1), F32))(a)


def _bucket_onehot_t():
    qi = np.arange(L)[:, None]
    sj = np.arange(2 * L)[None, :]
    dist = np.maximum(qi + L - sj, 0)
    log_ratio = (np.log(np.maximum(dist, 1).astype(np.float32) / np.float32(16)) / np.float32(math.log(128 / 16)))
    large = 16 + (log_ratio.astype(np.float32) * np.float32(16)).astype(np.int32)
    bucket = np.where(dist < 16, dist, np.minimum(large, 31)).reshape(-1)
    return (np.arange(32)[:, None] == bucket[None, :]).astype(np.float32)


def _rel_bias(table_t, onehot_t):
    def body(t_ref, oh_ref, o_ref):
        o_ref[...] = lax.dot_general(t_ref[...], oh_ref[...], NN, precision=lax.Precision.HIGHEST,
                                     preferred_element_type=F32)
    return pl.pallas_call(body, name="rel_bias", out_shape=jax.ShapeDtypeStruct((16, L * 2 * L), F32),
                          compiler_params=_params())(table_t, onehot_t)


def _rel_bias_bwd(dbias, onehot_t):
    def body(d_ref, oh_ref, o_ref):
        o_ref[...] = lax.dot_general(d_ref[...], oh_ref[...], NT, precision=lax.Precision.HIGHEST,
                                     preferred_element_type=F32)
    return pl.pallas_call(body, name="rel_bias_bwd", out_shape=jax.ShapeDtypeStruct((16, 32), F32),
                          compiler_params=_params())(dbias, onehot_t)


def _band(kp, kc, lo):
    kk = jnp.concatenate([kp, kc], axis=0)
    kr = pltpu.roll(kk, 64, axis=1)
    return [jnp.where(lo, kk, kr), jnp.where(lo, kr, kk)]


def _attn_rows(ref, j, lo):
    parts = []
    for t in range(8):
        pair = ref[:, (4 * j + t // 2) * 128:(4 * j + t // 2 + 1) * 128]
        parts.append(jnp.where(lo if t % 2 == 0 else jnp.logical_not(lo), pair, 0.0))
    return jnp.concatenate(parts, axis=0)


def _attn_mask(i, rows):
    qi, sj = _iota((rows, 2 * L), 0) & (L - 1), _iota((rows, 2 * L), 1)
    rel = qi + L - sj
    return (rel >= 0) & (rel < L) & ((sj >= L) | (i > 0))


def _per_head_col(vals):
    return jnp.concatenate([jnp.broadcast_to(v, (L, 1)) for v in vals], axis=0)


SMEM = pl.BlockSpec(memory_space=pltpu.SMEM)


def _attn_fwd(qkv, bias, sinks):
    S = qkv.shape[0]
    nb = S // L
    scale = 64 ** -0.5

    def body(sink_ref, q_ref, kc_ref, vc_ref, kp_ref, vp_ref, bias_ref, o_ref, lse_ref):
        i = pl.program_id(0)
        lane = _iota((L, 128), 1)
        lo = lane < 64
        lo2 = _iota((2 * L, 128), 1) < 64
        mask = _attn_mask(i, L)
        kd = _band(kp_ref[...], kc_ref[...], lo2)
        vd = _band(vp_ref[...], vc_ref[...], lo2)
        lse = jnp.zeros((L, 128), F32)
        for pr in range(8):
            sl = slice(pr * 128, (pr + 1) * 128)
            qp = q_ref[:, sl]
            j = pr // 4
            outs = []
            for hh in range(2):
                h = 2 * pr + hh
                qm = jnp.where(lo if hh == 0 else jnp.logical_not(lo), qp, 0.0)
                lg = jnp.where(mask, _dot(qm, kd[j], NT) * scale + bias_ref[h], NEG_INF)
                s = sink_ref[h]
                m = jnp.maximum(jnp.max(lg, axis=1, keepdims=True), s)
                p = jnp.where(mask, jnp.exp(lg - m), 0.0)
                den = jnp.sum(p, axis=1, keepdims=True) + jnp.exp(s - m)
                outs.append(_dot(p / den, vd[j]))
                lse = jnp.where(lane == h, m + jnp.log(den), lse)
            o_ref[:, sl] = jnp.where(lo, outs[0], outs[1]).astype(o_ref.dtype)
        lse_ref[...] = lse

    prev = lambda col: pl.BlockSpec((L, 128), lambda i: (jnp.maximum(i - 1, 0), col))
    cur = lambda col: pl.BlockSpec((L, 128), lambda i: (i, col))
    return pl.pallas_call(
        body, name="attn_fwd", grid=(nb,),
        in_specs=[SMEM, pl.BlockSpec((L, 1024), lambda i: (i, 0)), cur(8), cur(9), prev(8), prev(9),
                  pl.BlockSpec((16, L, 2 * L), lambda i: (0, 0, 0))],
        out_specs=[pl.BlockSpec((L, 1024), lambda i: (i, 0)), pl.BlockSpec((L, 128), lambda i: (i, 0))],
        out_shape=[jax.ShapeDtypeStruct((S, 1024), BF16), jax.ShapeDtypeStruct((S, 128), F32)],
        compiler_params=_params(("parallel",)))(sinks, qkv, qkv, qkv, qkv, qkv, bias)


def _attn_bwd(qkv, d_o, lse, bias, sinks):
    S = qkv.shape[0]
    nb = S // L
    scale = 64 ** -0.5

    def body(sink_ref, q_ref, kc_ref, vc_ref, kp_ref, vp_ref, do_ref, lse_ref, bias_ref,
             dq_ref, dkv_ref, dbias_ref, dsink_ref, dbq_ref, dbkv_ref, carry_ref):
        i = pl.program_id(0)

        @pl.when(i == 0)
        def _():
            dbias_ref[...] = jnp.zeros_like(dbias_ref)
            dsink_ref[...] = jnp.zeros_like(dsink_ref)
            dbq_ref[...] = jnp.zeros_like(dbq_ref)
            dbkv_ref[...] = jnp.zeros_like(dbkv_ref)
            carry_ref[...] = jnp.zeros_like(carry_ref)

        @pl.when(i < nb)
        def _():
            lane = _iota((L, 128), 1)
            lane1 = _iota((1, 128), 1)
            lo = lane < 64
            lo2 = _iota((2 * L, 128), 1) < 64
            mask = _attn_mask(i, 8 * L)
            kd = _band(kp_ref[...], kc_ref[...], lo2)
            vd = _band(vp_ref[...], vc_ref[...], lo2)
            lse_all = lse_ref[...]
            dsink = jnp.zeros((1, 128), F32)
            tot_k, tot_v = [], []
            for j in range(2):
                q_all = _attn_rows(q_ref, j, lo)
                do_all = _attn_rows(do_ref, j, lo)
                lse_col = _per_head_col([_colsel(lse_all, lane, 8 * j + t) for t in range(8)])
                lg = _dot(q_all, kd[j], NT) * scale + bias_ref[8 * j:8 * j + 8].reshape(8 * L, 2 * L)
                p = jnp.where(mask, jnp.exp(jnp.where(mask, lg, NEG_INF) - lse_col), 0.0)
                dp = _dot(do_all, vd[j], NT)
                delta = jnp.sum(p * dp, axis=1, keepdims=True)
                ds = p * (dp - delta)
                dbias_ref[8 * j:8 * j + 8] += ds.reshape(8, L, 2 * L)
                s = _per_head_col([sink_ref[8 * j + t] for t in range(8)])
                sink_part = -jnp.exp(s - lse_col) * delta
                for t in range(8):
                    dsink = dsink + jnp.where(lane1 == 8 * j + t,
                                              jnp.sum(sink_part[t * L:(t + 1) * L], axis=0, keepdims=True), 0.0)
                dss = ds * scale
                dq_all = _dot(dss, kd[j])
                for t in range(0, 8, 2):
                    sl = slice((4 * j + t // 2) * 128, (4 * j + t // 2 + 1) * 128)
                    dq = jnp.where(lo, dq_all[t * L:(t + 1) * L], dq_all[(t + 1) * L:(t + 2) * L])
                    dq_ref[:, sl] = dq.astype(dq_ref.dtype)
                    dbq_ref[:, sl] += jnp.sum(dq, axis=0, keepdims=True)
                acc_k = _dot(dss, q_all, TN)
                acc_v = _dot(p, do_all, TN)
                tot_k.append(acc_k + pltpu.roll(acc_k, 64, axis=1))
                tot_v.append(acc_v + pltpu.roll(acc_v, 64, axis=1))
            dsink_ref[...] += dsink
            dkv = jnp.concatenate([jnp.where(lo2, tot_k[0], tot_k[1]), jnp.where(lo2, tot_v[0], tot_v[1])], axis=1)
            dbkv_ref[...] += jnp.sum(dkv, axis=0, keepdims=True)
            dkv_ref[...] = (carry_ref[...] + dkv[:L, :]).astype(dkv_ref.dtype)
            carry_ref[...] = dkv[L:, :]

        @pl.when(i == nb)
        def _():
            dkv_ref[...] = carry_ref[...].astype(dkv_ref.dtype)

    c = lambda i: jnp.minimum(i, nb - 1)
    prev = lambda col: pl.BlockSpec((L, 128), lambda i: (jnp.maximum(c(i) - 1, 0), col))
    cur = lambda col: pl.BlockSpec((L, 128), lambda i: (c(i), col))
    row = lambda w: pl.BlockSpec((L, w), lambda i: (c(i), 0))
    cube = pl.BlockSpec((16, L, 2 * L), lambda i: (0, 0, 0))
    vec = lambda w: pl.BlockSpec((1, w), lambda i: (0, 0))
    return pl.pallas_call(
        body, name="attn_bwd", grid=(nb + 1,),
        in_specs=[SMEM, row(1024), cur(8), cur(9), prev(8), prev(9), row(1024), row(128), cube],
        out_specs=[row(1024), pl.BlockSpec((L, 256), lambda i: (jnp.maximum(i - 1, 0), 0)), cube,
                   vec(128), vec(1024), vec(256)],
        out_shape=[jax.ShapeDtypeStruct((S, 1024), BF16), jax.ShapeDtypeStruct((S, 256), BF16),
                   jax.ShapeDtypeStruct((16, L, 2 * L), F32), jax.ShapeDtypeStruct((1, 128), F32),
                   jax.ShapeDtypeStruct((1, 1024), F32), jax.ShapeDtypeStruct((1, 256), F32)],
        scratch_shapes=[pltpu.VMEM((L, 256), F32)],
        compiler_params=_params(("arbitrary",)))(sinks, qkv, qkv, qkv, qkv, qkv, d_o, lse, bias)


def _pad_lanes(a, n=128):
    return jnp.pad(a, ((0, 0), (0, n - a.shape[1])))


def _local_step(x, tgt, mod, w_in, P, io):
    md = [[mod[l:l + 1, k * D:(k + 1) * D] for k in range(6)] for l in range(2)]
    G, g = {}, {}

    sh1, sc1, g1, sh2, sc2, g2 = md[0]
    nmw0, nfw0 = P["norm_mix_w"][0:1], P["norm_ffn_w"][0:1]
    h0 = _norm_mod_fwd("norm_mix_0", x, nmw0, sc1, sh1, after=io["start"])
    segs = {"z": w_in[0:1024], "xbc": w_in[1024:2560], "dt": jnp.pad(w_in[2560:2576], ((0, 112), (0, 0))),
            "u": w_in[2576:3600], "v": w_in[3600:4624]}
    proj = dict(zip(segs, _mm_shared_lhs("in_proj", h0, list(segs.values()))))
    conv_w, conv_b = P["conv_w"][0], P["conv_b"]
    pre, xc = _conv_fwd(proj["xbc"], conv_w, conv_b)
    dtb, alog = _pad_lanes(P["dt_bias"]), _pad_lanes(P["a_log"])
    dskl = jnp.repeat(P["d_skip"], 64, axis=1)
    ya, y_ssd, prev = _ssd_fwd(xc, proj["dt"], proj["z"], dtb, alog, dskl, P["ssm_norm_w"])
    ws = P["gmlp_ws"][0]
    bse = jnp.broadcast_to(P["gmlp_bs"][0][:, :, None], (8, L, 128))
    yb = _gmlp_fwd(proj["u"], proj["v"], P["gmlp_ln_w"], P["gmlp_ln_b"], ws, bse)
    W = dict(io["weights0"]((ya, yb)))
    w_oa, w_ob = W["out_w"][:1024], W["out_w"][1024:]

    def res(y, x, gate, nw, sc, sh):
        xo = x + gate * y
        return y, xo, _norm_mod(xo, nw, sc, sh)
    mix0, x1, h0f = _mm("out_proj_0", [ya, yb], [w_oa, w_ob], "nn", [F32, F32, BF16], epi=res, extras=[x],
                        vecs=[g1, nfw0, sc2, sh2], whole_rows=True)
    sh1b, sc1b, g1b, sh2b, sc2b, g2b = md[1]
    nmw1, nfw1 = P["norm_mix_w"][1:2], P["norm_ffn_w"][1:2]
    a0, b0, f0, y0, x2, h1 = _ffn_fwd("0", h0f, W["gate_wt0"], W["up_wt0"], W["down_w0"], x1, g2,
                                      next_norm=(nmw1, sc1b, sh1b))

    W.update(io["weights1"](x2))
    qkv = _mm("qkv_proj", [h1], [W["qkv_wt"]], "nt", [F32], epi=lambda acc, b: acc + b, vecs=[P["qkv_b"]])[0]
    onehot_t = jnp.asarray(_bucket_onehot_t())
    bias = _rel_bias(P["rel_table"].T, onehot_t).reshape(16, L, 2 * L)
    sinks = P["sinks"].reshape(16)
    att, lse = _attn_fwd(qkv, bias, sinks)

    def res_b(y, x, gate, b, nw, sc, sh):
        y = y + b
        xo = x + gate * y
        return y, xo, _norm_mod(xo, nw, sc, sh)
    mix1, x3, h1f = _mm("o_proj", [att], [W["o_w"]], "nn", [F32, F32, BF16], epi=res_b, extras=[x2],
                        vecs=[g1b, P["o_b"], nfw1, sc2b, sh2b], whole_rows=True)
    a1, b1, f1, y1, x4, _ = _ffn_fwd("1", h1f, W["gate_wt1"], W["up_wt1"], W["down_w1"], x3, g2b)

    dx, dy, sq, g["final_norm_w"], dg2b = _loss_head(x4, tgt, P["final_norm_w"], y1, g2b)

    dh, dwg1, dwu1, dwd1 = _ffn_bwd("1", dy, h1f, a1, b1, f1, W["gate_wt1"], W["up_wt1"], W["down_w1"])
    dx, dmix, dsh2b, dsc2b, dnfw1, dg1b, g["o_b"] = _norm_mod_bwd("norm_ffn_bwd_1", x3, dh, dx, nfw1, sc2b,
                                                                 gate=(mix1, g1b))
    G["o_w"] = _mm_tn("o_dw", att, dmix)
    d_att = _mm("o_dx", [dmix], [W["o_w"]], "nt", [F32])[0]
    dq, dkv, dbias, dsinks, dbq, dbkv = _attn_bwd(qkv, d_att, lse, bias, sinks)
    g["rel_table"] = _rel_bias_bwd(dbias.reshape(16, L * 2 * L), onehot_t).T
    g["sinks"] = dsinks[:, :16]
    g["qkv_b"] = jnp.concatenate([dbq, dbkv], axis=1)
    w_q, w_kv = W["qkv_wt"][:1024], W["qkv_wt"][1024:]
    G["qkv_wt"] = jnp.concatenate([_mm_tn("qkv_dwq", dq, h1), _mm_tn("qkv_dwkv", dkv, h1)], axis=0)
    dh = _mm("qkv_dx", [dq, dkv], [w_q, w_kv], "nn", [F32])[0]
    behind = io["grads1"]({"qkv_wt": G.pop("qkv_wt"), "o_w": G.pop("o_w"), "gate_wt1": dwg1, "up_wt1": dwu1,
                           "down_w1": dwd1})
    dx, dy, dsh1b, dsc1b, dnmw1, dg2, _ = _norm_mod_bwd("norm_mix_bwd_1", x2, dh, dx, nmw1, sc1b, gate=(y0, g2),
                                                        after=behind)

    dh, dwg0, dwu0, dwd0 = _ffn_bwd("0", dy, h0f, a0, b0, f0, W["gate_wt0"], W["up_wt0"], W["down_w0"])
    behind = io["grads_ffn0"]({"gate_wt0": dwg0, "up_wt0": dwu0, "down_w0": dwd0})
    dx, dmix, dsh2, dsc2, dnfw0, dg1, _ = _norm_mod_bwd("norm_ffn_bwd_0", x1, dh, dx, nfw0, sc2, gate=(mix0, g1),
                                                        after=behind)
    G["out_w"] = jnp.concatenate([_mm_tn("out_dwa", ya, dmix), _mm_tn("out_dwb", yb, dmix)], axis=0)
    dya = _mm("out_dxa", [dmix], [w_oa], "nt", [F32])[0]
    dyb = _mm("out_dxb", [dmix], [w_ob], "nt", [F32])[0]
    du, dv, dws, dbse, g["gmlp_ln_w"], g["gmlp_ln_b"] = _gmlp_bwd(dyb, proj["u"], proj["v"], P["gmlp_ln_w"],
                                                                 P["gmlp_ln_b"], ws, bse)
    g["gmlp_ws"] = dws[None]
    g["gmlp_bs"] = _lane_sum("gmlp_dbs", dbse.reshape(8 * L, 128)).reshape(1, 8, L)
    dz, dxc, ddt, g["ssm_norm_w"], ddsk, dalog, ddtb = _ssd_bwd(dya, y_ssd, proj["z"], xc, proj["dt"], prev,
                                                                dtb, alog, dskl, P["ssm_norm_w"])
    g["d_skip"], g["a_log"], g["dt_bias"] = ddsk[0:1, :16], dalog[:, :16], ddtb[:, :16]
    dxr, dconv_w, g["conv_b"] = _conv_bwd(dxc, pre, proj["xbc"], conv_w)
    g["conv_w"] = dconv_w[None]
    dsegs = {"z": dz, "xbc": dxr, "dt": ddt, "u": du, "v": dv}
    dws_in = dict(zip(dsegs, _mm_tn_shared_rhs("in_dw", list(dsegs.values()), h0)))
    G["in_wt"] = jnp.concatenate([dws_in["z"], dws_in["xbc"], dws_in["dt"][:16], dws_in["u"], dws_in["v"]], axis=0)
    keys = ["z", "xbc", "dt", "u", "v"]
    dh = _mm("in_dx", [dsegs[k] for k in keys], [segs[k] for k in keys], "nn", [F32])[0]
    dx, dsh1, dsc1, dnmw0 = _norm_mod_bwd("norm_mix_bwd_0", x, dh, dx, nmw0, sc1)

    g["norm_mix_w"] = jnp.concatenate([dnmw0, dnmw1], axis=0)
    g["norm_ffn_w"] = jnp.concatenate([dnfw0, dnfw1], axis=0)
    dmod = jnp.concatenate([jnp.concatenate([dsh1, dsc1, dg1, dsh2, dsc2, dg2], axis=1),
                            jnp.concatenate([dsh1b, dsc1b, dg1b, dsh2b, dsc2b, dg2b], axis=1)], axis=0)
    return sq, dx, dmod, G, g


def _ada_fwd(c_all, ada_w, ada_b):
    n = ada_w.shape[2]
    tn = _col_tile(n, 512)

    def body(c_ref, w_ref, b_ref, o_ref):
        cc = c_ref[...]
        o_ref[...] = lax.dot_general(cc * _sigmoid(cc), w_ref[...], NN, precision=lax.Precision.HIGHEST,
                                     preferred_element_type=F32) + b_ref[...]

    return pl.pallas_call(
        body, name="ada_fwd", grid=(2, n // tn),
        in_specs=[pl.BlockSpec((8, D), lambda l, j: (0, 0)), pl.BlockSpec((None, D, tn), lambda l, j: (l, 0, j)),
                  pl.BlockSpec((None, 1, tn), lambda l, j: (l, 0, j))],
        out_specs=pl.BlockSpec((None, 8, tn), lambda l, j: (l, 0, j)),
        out_shape=jax.ShapeDtypeStruct((2, 8, n), F32), compiler_params=_params(("parallel", "parallel")))(
            c_all, ada_w, ada_b)


def _ada_bwd(c_all, dmod_cols, dmod_all):
    n = dmod_cols.shape[2]
    tn = _col_tile(n, 512)

    def body(c_ref, d_ref, o_ref):
        cc = c_ref[...]
        o_ref[...] = lax.dot_general(cc * _sigmoid(cc), d_ref[...], TN, precision=lax.Precision.HIGHEST,
                                     preferred_element_type=F32)

    dw = pl.pallas_call(
        body, name="ada_dw", grid=(2, n // tn),
        in_specs=[pl.BlockSpec((8, D), lambda l, j: (0, 0)), pl.BlockSpec((None, 8, tn), lambda l, j: (l, 0, j))],
        out_specs=pl.BlockSpec((None, D, tn), lambda l, j: (l, 0, j)),
        out_shape=jax.ShapeDtypeStruct((2, D, n), F32), compiler_params=_params(("parallel", "parallel")))(
            c_all, dmod_cols)

    def sum_body(d_ref, o_ref):
        o_ref[...] = jnp.sum(d_ref[...], axis=0, keepdims=True)

    db = pl.pallas_call(
        sum_body, name="ada_db", grid=(2,),
        in_specs=[pl.BlockSpec((None, 8, 6 * D), lambda l: (l, 0, 0))],
        out_specs=pl.BlockSpec((None, 1, 6 * D), lambda l: (l, 0, 0)),
        out_shape=jax.ShapeDtypeStruct((2, 1, 6 * D), F32), compiler_params=_params(("parallel",)))(dmod_all)
    return dw, db


def _row_tile(rows, cap=512, mult=8):
    best = rows
    for t in range(mult, min(rows, cap) + 1, mult):
        if rows % t == 0:
            best = t
    return best


def _adamw(name, w, g, m, v):
    def fn(w, g, m, v):
        m = ADAM_B1 * m + (1.0 - ADAM_B1) * g
        v = ADAM_B2 * v + (1.0 - ADAM_B2) * (g * g)
        m_hat = m / (1.0 - ADAM_B1 ** ADAM_STEP)
        v_hat = v / (1.0 - ADAM_B2 ** ADAM_STEP)
        return -ADAM_LR * (m_hat / (jnp.sqrt(v_hat) + ADAM_EPS) + ADAM_WD * w), m, v
    cols = w.shape[1]
    return _rowwise(name, fn, [w, g, m, v], [], [(cols, F32)] * 3, tr=_row_tile(w.shape[0]))


def _place():
    return lax.axis_index("x"), lax.axis_index("y"), lax.axis_index("c")


VMEM_SPEC = pl.BlockSpec(memory_space=pltpu.VMEM)


def _allreduce_small(name, buf, after=None):
    rows = buf.shape[0]
    deps = [] if after is None else [after]

    def body(x_ref, *rest):
        o_ref, stage, send_sems, recv_sems = rest[len(deps):]
        x, y, c = _place()
        me = 4 * x + 2 * y + c
        stage[me] = x_ref[...]
        copies = []
        for k in range(1, 8):
            peer = (1 - x if k & 4 else x, 1 - y if k & 2 else y, 1 - c if k & 1 else c)
            cp = pltpu.make_async_remote_copy(src_ref=x_ref, dst_ref=stage.at[me], send_sem=send_sems.at[k - 1],
                                              recv_sem=recv_sems.at[k - 1], device_id=peer, device_id_type=MESH)
            cp.start()
            copies.append(cp)
        for cp in copies:
            cp.wait()
        acc = stage[0]
        for d in range(1, 8):
            acc = acc + stage[d]
        o_ref[...] = acc

    return pl.pallas_call(
        body, name=name, in_specs=[VMEM_SPEC] + [ANY for _ in deps], out_specs=VMEM_SPEC,
        out_shape=jax.ShapeDtypeStruct((rows, 128), F32),
        scratch_shapes=[pltpu.VMEM((8, rows, 128), F32), pltpu.SemaphoreType.DMA((7,)), pltpu.SemaphoreType.DMA((7,))],
        compiler_params=pltpu.CompilerParams(vmem_limit_bytes=_VMEM_LIMIT))(buf, *deps)


def _sum_slots(name, own, land):
    def body(own_ref, land_ref, o_ref):
        x, y, c = _place()
        me = 4 * x + 2 * y + c
        acc = None
        for d in range(8):
            v = jnp.where(me == d, own_ref[...], land_ref[d])
            acc = v if acc is None else acc + v
        o_ref[...] = acc

    return pl.pallas_call(body, name=name, in_specs=[VMEM_SPEC, VMEM_SPEC], out_specs=VMEM_SPEC,
                          out_shape=jax.ShapeDtypeStruct(own.shape, F32),
                          compiler_params=pltpu.CompilerParams(vmem_limit_bytes=_VMEM_LIMIT))(own, land)


OTHER_CHIPS = ((1, 0), (0, 1), (1, 1))


SIBLING_COLLECTIVE_ID = 6


def _sibling_handshake():
    x, y, c = _place()
    barrier = pltpu.get_barrier_semaphore()
    pl.semaphore_signal(barrier, inc=1, device_id=(x, y, 1 - c), device_id_type=MESH)
    pl.semaphore_wait(barrier, 1)


def _sibling_swap(name, src, halves):
    half = src.shape[-2] // 2
    out_shape = (src.shape[0], half, 1024) if halves else src.shape

    def body(s_ref, o_ref, send_sem, recv_sem):
        x, y, c = _place()
        _sibling_handshake()
        part = s_ref.at[:, pl.ds(pl.multiple_of((1 - c) * half, 8), half)] if halves else s_ref
        cp = pltpu.make_async_remote_copy(src_ref=part, dst_ref=o_ref, send_sem=send_sem, recv_sem=recv_sem,
                                          device_id=(x, y, 1 - c), device_id_type=MESH)
        cp.start()
        cp.wait()

    return pl.pallas_call(
        body, name=name, in_specs=[ANY], out_specs=ANY, out_shape=jax.ShapeDtypeStruct(out_shape, src.dtype),
        scratch_shapes=[pltpu.SemaphoreType.DMA, pltpu.SemaphoreType.DMA],
        compiler_params=pltpu.CompilerParams(collective_id=SIBLING_COLLECTIVE_ID))(src)


HBM = pl.BlockSpec(memory_space=pltpu.HBM)
SEM = pl.BlockSpec(memory_space=pltpu.SEMAPHORE)


def _exchange_peers(mode):
    x, y, c = _place()
    if mode == "all":
        return [(1 - x if k & 4 else x, 1 - y if k & 2 else y, 1 - c if k & 1 else c) for k in range(1, 8)]
    return [(1 - x if fx else x, 1 - y if fy else y, c) for fx, fy in OTHER_CHIPS]


def _chip_copies(mode, src_ref, land_ref, send_sems, recv_sems):
    x, y, c = _place()
    k = 2 * x + y
    copies = []
    for j, peer in enumerate(_exchange_peers(mode)):
        if mode == "gather":
            half = src_ref.shape[0] // 2
            mine = pl.ds(pl.multiple_of(c * half, 16), half)
            src, dst = src_ref.at[mine], land_ref.at[k, mine]
        elif mode == "scatter":
            src, dst = src_ref.at[2 * peer[0] + peer[1]], land_ref.at[k]
        else:
            src, dst = src_ref, land_ref.at[4 * x + 2 * y + c]
        copies.append(pltpu.make_async_remote_copy(src_ref=src, dst_ref=dst, send_sem=send_sems.at[j],
                                                   recv_sem=recv_sems.at[j], device_id=peer, device_id_type=MESH))
    return copies


def _exchange_start(name, collective_id, mode, src, land, after=None):
    deps = [] if after is None else [after]
    npeers = 7 if mode == "all" else 3

    def body(s_ref, l_ref, *rest):
        send_sems, recv_sems, s_thru, l_thru, token = rest[len(deps):]
        barrier = pltpu.get_barrier_semaphore()
        for peer in _exchange_peers(mode):
            pl.semaphore_signal(barrier, inc=1, device_id=peer, device_id_type=MESH)
        pl.semaphore_wait(barrier, npeers)
        for cp in _chip_copies(mode, s_ref, l_ref, send_sems, recv_sems):
            cp.start()
        token[...] = jnp.zeros_like(token)

    return pl.pallas_call(
        body, name=name,
        out_shape=(pltpu.SemaphoreType.DMA((npeers,)), pltpu.SemaphoreType.DMA((npeers,)),
                   pltpu.HBM(src.shape, src.dtype),
                   pltpu.HBM(land.shape, land.dtype), jax.ShapeDtypeStruct((8, 128), F32)),
        in_specs=(HBM, HBM) + tuple(ANY for _ in deps), out_specs=(SEM, SEM, HBM, HBM, VMEM_SPEC),
        input_output_aliases={0: 2, 1: 3},
        compiler_params=pltpu.CompilerParams(has_side_effects=pltpu.SideEffectType.DATAFLOW_SIDE_EFFECTING,
                                             collective_id=collective_id))(
            pltpu.with_memory_space_constraint(src, pltpu.HBM), pltpu.with_memory_space_constraint(land, pltpu.HBM),
            *deps)


def _exchange_wait(name, mode, started, after):
    send_sems, recv_sems, s_thru, l_thru, _ = started
    deps = list(after) if isinstance(after, (tuple, list)) else [after]

    def body(s_ref, l_ref, send_sems, recv_sems, *rest):
        for cp in _chip_copies(mode, s_ref, l_ref, send_sems, recv_sems):
            cp.wait_send()
            cp.wait_recv()

    return pl.pallas_call(
        body, name=name, out_shape=(pltpu.HBM(s_thru.shape, s_thru.dtype), pltpu.HBM(l_thru.shape, l_thru.dtype)),
        in_specs=(HBM, HBM, SEM, SEM) + tuple(ANY for _ in deps), out_specs=(HBM, HBM),
        input_output_aliases={0: 0, 1: 1},
        compiler_params=pltpu.CompilerParams(has_side_effects=pltpu.SideEffectType.DATAFLOW_SIDE_EFFECTING))(
            s_thru, l_thru, send_sems, recv_sems, *deps)


def _allgather_finish(tag, land):
    half = land.shape[1] // 2

    def body(l_ref, o_ref, send_sem, recv_sem):
        x, y, c = _place()
        _sibling_handshake()
        mine = pl.ds(pl.multiple_of(c * half, 16), half)
        swap = pltpu.make_async_remote_copy(src_ref=o_ref.at[:, mine], dst_ref=o_ref.at[:, mine], send_sem=send_sem,
                                            recv_sem=recv_sem, device_id=(x, y, 1 - c), device_id_type=MESH)
        swap.start()
        swap.wait()

    return pl.pallas_call(
        body, name="allgather_finish_" + tag, in_specs=[ANY], out_specs=ANY, input_output_aliases={0: 0},
        out_shape=jax.ShapeDtypeStruct(land.shape, land.dtype),
        scratch_shapes=[pltpu.SemaphoreType.DMA, pltpu.SemaphoreType.DMA],
        compiler_params=pltpu.CompilerParams(collective_id=SIBLING_COLLECTIVE_ID))(land)


def _pair_sum(tag, g, r1, c):
    rows = g.shape[1]
    half = rows // 2
    th = _row_tile(half, 256, 16)
    nblk = half // th

    def body(c_ref, g_ref, r_ref, o_ref, o2_ref):
        o_ref[...] = (g_ref[...] + r_ref[...]).astype(o_ref.dtype)
        o2_ref[...] = o_ref[...]

    spec = pl.BlockSpec((None, th, 1024), lambda k, i, c_ref: (k, i, 0))
    grid_spec = pltpu.PrefetchScalarGridSpec(
        num_scalar_prefetch=1, grid=(4, nblk),
        in_specs=[pl.BlockSpec((None, th, 1024), lambda k, i, c_ref: (k, c_ref[0] * nblk + i, 0)), spec],
        out_specs=[spec, spec])
    return pl.pallas_call(body, name="grad_pair_sum_" + tag, grid_spec=grid_spec,
                          out_shape=[jax.ShapeDtypeStruct((4, half, 1024), BF16)] * 2,
                          compiler_params=_params(("parallel", "parallel")))(c, g, r1)


def _chip_sum(tag, q, after=None):
    half = q.shape[1]
    th = _row_tile(half, 256, 16)
    deps = [] if after is None else [after]

    def body(a, b, c, d, *rest):
        rest[-1][...] = ((a[...].astype(F32) + b[...].astype(F32)) + c[...].astype(F32)) + d[...].astype(F32)

    specs = [pl.BlockSpec((None, th, 1024), functools.partial(lambda i, k: (k, i, 0), k=k)) for k in range(4)]
    return pl.pallas_call(body, name="grad_chip_sum_" + tag, grid=(half // th,), in_specs=specs + [ANY for _ in deps],
                          out_specs=pl.BlockSpec((th, 1024), lambda i: (i, 0)),
                          out_shape=jax.ShapeDtypeStruct((half, 1024), F32),
                          compiler_params=_params(("parallel",)))(q, q, q, q, *deps)


def _join_halves(tag, f, r, c):
    half = f.shape[0]
    th = _row_tile(half, 256)
    nblk = half // th

    def body(c_ref, f_ref, r_ref, o_ref):
        mine = (pl.program_id(0) == c_ref[0])
        o_ref[...] = jnp.where(mine, f_ref[...], r_ref[...])

    spec = pl.BlockSpec((th, 1024), lambda h, i, c_ref: (i, 0))
    grid_spec = pltpu.PrefetchScalarGridSpec(
        num_scalar_prefetch=1, grid=(2, nblk), in_specs=[spec, spec],
        out_specs=pl.BlockSpec((th, 1024), lambda h, i, c_ref: (h * nblk + i, 0)))
    return pl.pallas_call(body, name="grad_join_halves_" + tag, grid_spec=grid_spec,
                          out_shape=jax.ShapeDtypeStruct((2 * half, 1024), F32),
                          compiler_params=_params(("parallel", "parallel")))(c, f, r)


BIG_ARGS = ("in_w_even", "out_w_even", "qkv_w", "o_w", "ffn_gate_w", "ffn_up_w", "ffn_down_w")
def _ffn_pieces(layer):
    return tuple((f"{n}{layer}", 704, 704) for n in ("gate_wt", "up_wt", "down_w"))


IN_SLAB = (("in_wt", 1156, 1184),)
LAYER0_REST_SLAB = (("out_w", 512, 512),) + _ffn_pieces(0)
LAYER1_SLAB = (("qkv_wt", 320, 320), ("o_w", 256, 256)) + _ffn_pieces(1)
FFN0_SLAB = _ffn_pieces(0)
MIXER0_SLAB = (("in_wt", 1156, 1280), ("out_w", 512, 512))


def _slab(pieces, spec):
    parts = []
    for name, rows, room in spec:
        p = pieces[name]
        parts.append(jnp.pad(p, [(0, 0)] * (p.ndim - 2) + [(0, room - rows), (0, 0)]) if room > rows else p)
    return jnp.concatenate(parts, axis=-2) if len(parts) > 1 else parts[0]


def _unslab(slab, spec):
    out, off = {}, 0
    for name, rows, room in spec:
        out[name] = slab[..., off:off + rows, :]
        off += room
    return out


def _share_pieces(w):
    return {"in_wt": w["in_w_even"][0].T, "out_w": w["out_w_even"][0], "qkv_wt": w["qkv_w"][0].T, "o_w": w["o_w"][0],
            "gate_wt0": w["ffn_gate_w"][0].T, "gate_wt1": w["ffn_gate_w"][1].T,
            "up_wt0": w["ffn_up_w"][0].T, "up_wt1": w["ffn_up_w"][1].T,
            "down_w0": w["ffn_down_w"][0], "down_w1": w["ffn_down_w"][1]}


def _pieces_to_shares(p):
    return {"in_w_even": p["in_wt"].T[None], "out_w_even": p["out_w"][None], "qkv_w": p["qkv_wt"].T[None],
            "o_w": p["o_w"][None], "ffn_gate_w": jnp.stack([p["gate_wt0"].T, p["gate_wt1"].T]),
            "ffn_up_w": jnp.stack([p["up_wt0"].T, p["up_wt1"].T]),
            "ffn_down_w": jnp.stack([p["down_w0"], p["down_w1"]])}


def _chips_from_full(G, spec):
    return _slab({k: v.reshape(4, -1, D) for k, v in G.items()}, spec)


def _pack_small(parts):
    padded = []
    for p in parts:
        p = p.reshape(-1).astype(F32)
        padded.append(jnp.pad(p, (0, (-p.shape[0]) % 1024)))
    return jnp.concatenate(padded).reshape(-1, 128)


def _unpack_small(slab, shapes):
    flat, out, off = slab.reshape(-1), [], 0
    for shp in shapes:
        size = math.prod(shp)
        out.append(flat[off:off + size].reshape(shp))
        off += size + (-size) % 1024
    return out


SMALL = ("ada_b", "norm_mix_w", "norm_ffn_w", "conv_w", "conv_b", "dt_bias", "a_log", "d_skip", "ssm_norm_w",
         "gmlp_ln_w", "gmlp_ln_b", "gmlp_ws", "gmlp_bs", "qkv_b", "o_b", "sinks", "rel_table", "final_norm_w")
SMALL_SPLIT = {"conv_w": 1536, "qkv_b": 1280, "o_b": 1024}
WEIGHTS = ("ada_w", "ada_b", "norm_mix_w", "norm_ffn_w", "in_w_even", "conv_w", "conv_b", "dt_bias", "a_log", "d_skip",
           "ssm_norm_w", "gmlp_ln_w", "gmlp_ln_b", "gmlp_ws", "gmlp_bs", "out_w_even", "qkv_w", "qkv_b", "o_w", "o_b",
           "sinks", "rel_table", "ffn_gate_w", "ffn_up_w", "ffn_down_w", "final_norm_w")


def kernel(x, c, ada_w, ada_b, norm_mix_w, norm_ffn_w, in_w_even, conv_w, conv_b, dt_bias, a_log, d_skip, ssm_norm_w, gmlp_ln_w, gmlp_ln_b, gmlp_ws, gmlp_bs, out_w_even, qkv_w, qkv_b, o_w, o_b, sinks, rel_table, ffn_gate_w, ffn_up_w, ffn_down_w, final_norm_w, loss_target, m_ada_w, m_ada_b, m_norm_mix_w, m_norm_ffn_w, m_in_w_even, m_conv_w, m_conv_b, m_dt_bias, m_a_log, m_d_skip, m_ssm_norm_w, m_gmlp_ln_w, m_gmlp_ln_b, m_gmlp_ws, m_gmlp_bs, m_out_w_even, m_qkv_w, m_qkv_b, m_o_w, m_o_b, m_sinks, m_rel_table, m_ffn_gate_w, m_ffn_up_w, m_ffn_down_w, m_final_norm_w, v_ada_w, v_ada_b, v_norm_mix_w, v_norm_ffn_w, v_in_w_even, v_conv_w, v_conv_b, v_dt_bias, v_a_log, v_d_skip, v_ssm_norm_w, v_gmlp_ln_w, v_gmlp_ln_b, v_gmlp_ws, v_gmlp_bs, v_out_w_even, v_qkv_w, v_qkv_b, v_o_w, v_o_b, v_sinks, v_rel_table, v_ffn_gate_w, v_ffn_up_w, v_ffn_down_w, v_final_norm_w):
    args = dict(locals())
    w = {n: args[n] for n in WEIGHTS}
    m = {n: args["m_" + n] for n in WEIGHTS}
    v = {n: args["v_" + n] for n in WEIGHTS}
    ax, ay, ac = _place()
    me = 4 * ax + 2 * ay + ac
    chip = 2 * ax + ay
    south = (ac == 0).astype(F32)
    c_arr = jnp.reshape(ac, (1,)).astype(jnp.int32)

    c_all = _allreduce_small("gather_cond", lax.dynamic_update_slice(jnp.zeros((8, D), F32), c, (me, 0)).reshape(64, 128))
    c_all = c_all.reshape(8, D)
    n_ada = ada_w.shape[2]
    mod_cols = _ada_fwd(c_all, ada_w, lax.dynamic_slice(ada_b, (0, chip * n_ada), (2, n_ada)).reshape(2, 1, n_ada))
    pieces = [lax.dynamic_update_slice(jnp.zeros((2, 8, 6 * D), F32), mod_cols, (0, 0, chip * n_ada))]
    split_names = list(SMALL_SPLIT)
    for n in split_names:
        full = SMALL_SPLIT[n]
        local = w[n]
        idx = (0,) * (local.ndim - 1) + (chip * local.shape[-1],)
        pieces.append(lax.dynamic_update_slice(jnp.zeros(local.shape[:-1] + (full,), F32), local, idx))
    shapes = [p.shape for p in pieces]
    mod_slab = _allreduce_small("gather_mod", _pack_small(pieces) * south)
    gathered = _unpack_small(mod_slab, shapes)
    mod = lax.dynamic_slice(gathered[0], (0, me, 0), (2, 1, 6 * D)).reshape(2, 6 * D)
    P = {n: w[n] for n in SMALL if n not in SMALL_SPLIT and n != "ada_b"}
    for n, full in zip(split_names, gathered[1:]):
        P[n] = full
    P["final_norm_w"] = final_norm_w.reshape(1, D)

    pieces = _share_pieces(w)
    cast = {"in_wt": pieces["in_wt"].astype(_MXU)}

    def start_gather(tag, collective_id, share, after):
        return _exchange_start("allgather_start_" + tag, collective_id, "gather", share,
                               lax.empty((4,) + share.shape, share.dtype), after=after)

    def finish_gather(tag, started, spec, after):
        land = _exchange_wait("allgather_wait_" + tag, "gather", started, after)[1]
        out = {}
        for name, piece in _unslab(_allgather_finish(tag, land), spec).items():
            out[name] = lax.dynamic_update_slice(piece.reshape(-1, D), cast[name], (chip * piece.shape[1], 0))
        return out

    gather_in = start_gather("in", 7, _slab(cast, IN_SLAB), mod_slab)
    zero = gather_in[4][0, 0]
    cast.update({k: (p + zero).astype(_MXU) for k, p in pieces.items() if k != "in_wt"})
    share0, share1 = _slab(cast, LAYER0_REST_SLAB), _slab(cast, LAYER1_SLAB)
    w_in = finish_gather("in", gather_in, IN_SLAB, (share0, share1))["in_wt"]
    gather0 = start_gather("0", 1, share0, w_in)
    gather1 = start_gather("1", 2, share1, gather0[4])

    def start_reduce(tag, collective_id, G, spec, after=None):
        gp = _chips_from_full(G, spec)
        p, q = _pair_sum(tag, gp, _sibling_swap("grad_pair_exchange_" + tag, gp, True), c_arr)
        return _exchange_start("grad_exchange_start_" + tag, collective_id, "scatter", p, q, after=after)

    def finish_reduce(tag, started, spec, after, behind=None):
        q = _exchange_wait("grad_exchange_wait_" + tag, "scatter", started, after)[1]
        fin = _chip_sum(tag, q, after=behind)
        total = _join_halves(tag, fin, _sibling_swap("grad_final_exchange_" + tag, fin, False), c_arr)
        return _unslab(total, spec)

    reduces = {}

    def grads1(G1):
        reduces["1"] = start_reduce("1", 3, G1, LAYER1_SLAB)
        return reduces["1"][4]

    def grads_ffn0(G):
        reduces["f"] = start_reduce("f", 4, G, FFN0_SLAB)
        return reduces["f"][4]

    io = {"start": gather1[4],
          "weights0": lambda after: finish_gather("0", gather0, LAYER0_REST_SLAB, after),
          "weights1": lambda after: finish_gather("1", gather1, LAYER1_SLAB, after),
          "grads1": grads1, "grads_ffn0": grads_ffn0}
    sq, grad_x, dmod, G0, g = _local_step(x[0], loss_target[0], mod, w_in, P, io)
    loss = lax.psum(0.5 * sq[0, 0] / D, ("x", "y", "c"))

    g["final_norm_w"] = g["final_norm_w"].reshape(D)
    small_names = [n for n in SMALL if n != "ada_b"]
    pieces = [lax.dynamic_update_slice(jnp.zeros((2, 8, 6 * D), F32), dmod.reshape(2, 1, 6 * D), (0, me, 0))]
    pieces += [g[n] for n in small_names]
    shapes = [p.shape for p in pieces]
    small_own = _pack_small(pieces)
    small_started = _exchange_start("small_grads_start", 8, "all", small_own, lax.empty((8,) + small_own.shape, F32))
    reduces["m"] = start_reduce("m", 5, G0, MIXER0_SLAB, after=small_started[4])
    shares = finish_reduce("1", reduces["1"], LAYER1_SLAB, grad_x, behind=reduces["m"][4])
    shares.update(finish_reduce("f", reduces["f"], FFN0_SLAB, grad_x, behind=reduces["m"][4]))
    small_own, small_land = _exchange_wait("small_grads_wait", "all", small_started, shares["down_w0"])
    reduced = _unpack_small(_sum_slots("small_grads_sum", small_own, small_land), shapes)
    dmod_all = reduced[0]
    grads = dict(zip(small_names, reduced[1:]))
    for n in split_names:
        full = grads[n]
        size = w[n].shape[-1]
        grads[n] = lax.dynamic_slice(full, (0,) * (full.ndim - 1) + (chip * size,), full.shape[:-1] + (size,))
    grads = {n: grads[n].reshape(w[n].shape) for n in small_names}
    dw_ada, db_ada = _ada_bwd(c_all, lax.dynamic_slice(dmod_all, (0, 0, chip * n_ada), (2, 8, n_ada)), dmod_all)
    grads["ada_w"], grads["ada_b"] = dw_ada, db_ada.reshape(2, 6 * D)

    delta, new_m, new_v = {}, {}, {}

    def update(n):
        cols = w[n].shape[-1]
        d_, m_, v_ = _adamw("adamw_" + n, w[n].reshape(-1, cols), grads[n].reshape(-1, cols), m[n].reshape(-1, cols),
                            v[n].reshape(-1, cols))
        delta[n], new_m[n], new_v[n] = d_.reshape(w[n].shape), m_.reshape(w[n].shape), v_.reshape(w[n].shape)

    update("ada_w")
    shapes = [w[n].shape for n in SMALL]
    packed = [_pack_small([t[n] for n in SMALL]) for t in (w, grads, m, v)]
    outs = _adamw("adamw_small", *packed)
    for dst, slab in zip((delta, new_m, new_v), outs):
        for n, t in zip(SMALL, _unpack_small(slab, shapes)):
            dst[n] = t
    shares.update(finish_reduce("m", reduces["m"], MIXER0_SLAB, outs[0]))
    grads.update(_pieces_to_shares(shares))
    for n in BIG_ARGS:
        update(n)
    return (loss, grad_x[None], *[grads[n] for n in WEIGHTS], *[delta[n] for n in WEIGHTS],
            *[new_m[n] for n in WEIGHTS], *[new_v[n] for n in WEIGHTS])
```

```python
import functools
import math

import numpy as np
import jax
import jax.numpy as jnp
from jax import lax
from jax.experimental import pallas as pl
from jax.experimental.pallas import tpu as pltpu

F32 = jnp.float32
BF16 = jnp.bfloat16
_MXU = jnp.bfloat16
_VMEM_LIMIT = 56 * 1024 * 1024
MXU_COLS = 256
D = 1024
L = 128
NSTATE = 128
EPS = 1e-6
NEG_INF = -1e30
FFN = 2816
ADAM_LR, ADAM_B1, ADAM_B2, ADAM_EPS, ADAM_WD, ADAM_STEP = 0.001, 0.9, 0.999, 1e-08, 0.01, 10
MESH = pl.DeviceIdType.MESH
ANY = pl.BlockSpec(memory_space=pl.ANY)

NN = (((1,), (0,)), ((), ()))
NT = (((1,), (1,)), ((), ()))
TN = (((0,), (0,)), ((), ()))


def _dot(a, b, dn=NN):
    return lax.dot_general(a.astype(_MXU), b.astype(_MXU), dn, preferred_element_type=F32)


def _params(sem=None):
    return pltpu.CompilerParams(dimension_semantics=sem, vmem_limit_bytes=_VMEM_LIMIT)


def _sigmoid(x):
    return 1.0 / (1.0 + jnp.exp(-x))


def _softplus(x):
    return jnp.maximum(x, 0.0) + jnp.log(1.0 + jnp.exp(-jnp.abs(x)))


def _gelu(x):
    return 0.5 * x * (1.0 + lax.erf(x * (2.0 ** -0.5)))


def _gelu_grad(x):
    return 0.5 * (1.0 + lax.erf(x * (2.0 ** -0.5))) + x * jnp.exp(-0.5 * x * x) * (1.0 / math.sqrt(2.0 * math.pi))


def _silu_grad(a):
    sg = _sigmoid(a)
    return sg * (1.0 + a * (1.0 - sg))


def _rowwise(name, fn, rows, vecs, out_rows, out_accs=(), tr=512, after=None):
    S = rows[0].shape[0]
    tr = min(tr, S)
    assert S % tr == 0
    nr, nv, no, na = len(rows), len(vecs), len(out_rows), len(out_accs)
    deps = [] if after is None else [after]

    def body(*refs):
        ins, outs = refs[:nr + nv], refs[nr + nv + len(deps):]
        res = fn(*[r[...] for r in ins])
        if not isinstance(res, (tuple, list)):
            res = (res,)
        for k in range(no):
            outs[k][...] = res[k].astype(outs[k].dtype)
        if na:
            @pl.when(pl.program_id(0) == 0)
            def _():
                for k in range(na):
                    outs[no + k][...] = jnp.zeros_like(outs[no + k])
            for k in range(na):
                outs[no + k][...] += res[no + k]

    in_specs = [pl.BlockSpec((tr, a.shape[1]), lambda i: (i, 0)) for a in rows]
    in_specs += [pl.BlockSpec(v.shape, lambda i: (0, 0)) for v in vecs] + [ANY for _ in deps]
    out_specs = [pl.BlockSpec((tr, c), lambda i: (i, 0)) for c, _ in out_rows]
    out_specs += [pl.BlockSpec(s, lambda i: (0, 0)) for s in out_accs]
    out_shape = [jax.ShapeDtypeStruct((S, c), dt) for c, dt in out_rows]
    out_shape += [jax.ShapeDtypeStruct(s, F32) for s in out_accs]
    return pl.pallas_call(body, name=name, grid=(S // tr,), in_specs=in_specs, out_specs=out_specs,
                          out_shape=out_shape, compiler_params=_params(("arbitrary",)))(*rows, *vecs, *deps)


def _col_tile(n, cap):
    if n <= cap or n % 128:
        return n
    best = 128
    for t in range(128, cap + 1, 128):
        if n % t == 0:
            best = t
    return best


def _mm(name, As, Bs, mode, outs, epi=None, groups=None, extras=(), vecs=(), tm=512, tn_cap=1536, whole_rows=False):
    M = As[0].shape[0]
    N = Bs[0].shape[1] if mode == "nn" else Bs[0].shape[0]
    tm = min(tm, M)
    tn = _col_tile(N, tn_cap)
    assert M % tm == 0 and N % tn == 0
    npair = len(As)
    groups = groups or [0] * npair
    ng = max(groups) + 1
    nx, nv = len(extras), len(vecs)
    dn = NN if mode == "nn" else NT

    def body(*refs):
        a_refs, b_refs = refs[:npair], refs[npair:2 * npair]
        x_refs = refs[2 * npair:2 * npair + nx]
        v_refs = refs[2 * npair + nx:2 * npair + nx + nv]
        o_refs = refs[2 * npair + nx + nv:]
        step = tn if (epi is None or whole_rows) else min(tn, MXU_COLS)
        for col in range(0, tn, step):
            sl = slice(col, min(col + step, tn))
            accs = [None] * ng
            for k in range(npair):
                b = b_refs[k][:, sl] if mode == "nn" else b_refs[k][sl, :]
                d = _dot(a_refs[k][...], b, dn)
                accs[groups[k]] = d if accs[groups[k]] is None else accs[groups[k]] + d
            args = accs + [x[:, sl] for x in x_refs] + [v[:, sl] for v in v_refs]
            res = epi(*args) if epi is not None else tuple(accs)
            if not isinstance(res, (tuple, list)):
                res = (res,)
            for o, r in zip(o_refs, res):
                o[:, sl] = r.astype(o.dtype)

    in_specs = [pl.BlockSpec((tm, a.shape[1]), lambda i, j: (i, 0)) for a in As]
    if mode == "nn":
        in_specs += [pl.BlockSpec((b.shape[0], tn), lambda i, j: (0, j)) for b in Bs]
    else:
        in_specs += [pl.BlockSpec((tn, b.shape[1]), lambda i, j: (j, 0)) for b in Bs]
    in_specs += [pl.BlockSpec((tm, tn), lambda i, j: (i, j)) for _ in extras]
    in_specs += [pl.BlockSpec((1, tn), lambda i, j: (0, j)) for _ in vecs]
    out_specs = [pl.BlockSpec((tm, tn), lambda i, j: (i, j)) for _ in outs]
    out_shape = [jax.ShapeDtypeStruct((M, N), dt) for dt in outs]
    return pl.pallas_call(body, name=name, grid=(M // tm, N // tn), in_specs=in_specs, out_specs=out_specs,
                          out_shape=out_shape, compiler_params=_params(("parallel", "parallel")))(
                              *As, *Bs, *extras, *vecs)


def _mm_shared_lhs(name, A, Bs, tm=512):
    M, K = A.shape
    tm = min(tm, M)
    assert M % tm == 0
    n = len(Bs)

    def body(a_ref, *refs):
        a = a_ref[...]
        for b_ref, o_ref in zip(refs[:n], refs[n:]):
            o_ref[...] = _dot(a, b_ref[...], NT)

    return pl.pallas_call(
        body, name=name, grid=(M // tm,),
        in_specs=[pl.BlockSpec((tm, K), lambda i: (i, 0))] + [pl.BlockSpec(b.shape, lambda i: (0, 0)) for b in Bs],
        out_specs=[pl.BlockSpec((tm, b.shape[0]), lambda i: (i, 0)) for b in Bs],
        out_shape=[jax.ShapeDtypeStruct((M, b.shape[0]), F32) for b in Bs],
        compiler_params=_params(("parallel",)))(A, *Bs)


def _mm_tn_shared_rhs(name, As, B, tk=256):
    S, N = B.shape
    tk = min(tk, S)
    assert S % tk == 0
    n = len(As)

    def body(*refs):
        a_refs, b_ref, o_refs = refs[:n], refs[n], refs[n + 1:]

        @pl.when(pl.program_id(0) == 0)
        def _():
            for o_ref in o_refs:
                o_ref[...] = jnp.zeros_like(o_ref)
        b = b_ref[...]
        for a_ref, o_ref in zip(a_refs, o_refs):
            o_ref[...] += _dot(a_ref[...], b, TN)

    return pl.pallas_call(
        body, name=name, grid=(S // tk,),
        in_specs=[pl.BlockSpec((tk, a.shape[1]), lambda k: (k, 0)) for a in As] + [pl.BlockSpec((tk, N), lambda k: (k, 0))],
        out_specs=[pl.BlockSpec((a.shape[1], N), lambda k: (0, 0)) for a in As],
        out_shape=[jax.ShapeDtypeStruct((a.shape[1], N), F32) for a in As],
        compiler_params=_params(("arbitrary",)))(*As, B)


def _mm_tn(name, A, B, tk=512, t2_cap=1536):
    S, K1 = A.shape
    N2 = B.shape[1]
    tk = min(tk, S)
    t2 = _col_tile(N2, t2_cap)
    assert S % tk == 0 and N2 % t2 == 0

    def body(a_ref, b_ref, o_ref):
        @pl.when(pl.program_id(1) == 0)
        def _():
            o_ref[...] = jnp.zeros_like(o_ref)
        o_ref[...] += _dot(a_ref[...], b_ref[...], TN)

    return pl.pallas_call(
        body, name=name, grid=(N2 // t2, S // tk),
        in_specs=[pl.BlockSpec((tk, K1), lambda j, k: (k, 0)), pl.BlockSpec((tk, t2), lambda j, k: (k, j))],
        out_specs=pl.BlockSpec((K1, t2), lambda j, k: (0, j)),
        out_shape=jax.ShapeDtypeStruct((K1, N2), F32),
        compiler_params=_params(("parallel", "arbitrary")))(A, B)


def _norm_mod(x, nw, sc, sh):
    rstd = lax.rsqrt(jnp.mean(x * x, axis=-1, keepdims=True) + EPS)
    return (x * rstd * nw) * (1.0 + sc) + sh


def _norm_mod_fwd(name, x, nw, sc, sh, after=None):
    return _rowwise(name, _norm_mod, [x], [nw, sc, sh], [(D, BF16)], after=after)[0]


def _norm_mod_bwd(name, x, dh, dres, nw, sc, gate=None, after=None):
    def fn(x, dh, dres, *rest):
        nw, sc = rest[-3:-1] if gate else rest
        rstd = lax.rsqrt(jnp.mean(x * x, axis=-1, keepdims=True) + EPS)
        xh = x * rstd
        dn = dh * (1.0 + sc)
        dxh = dn * nw
        dx = dres + rstd * (dxh - xh * jnp.mean(dxh * xh, axis=-1, keepdims=True))
        sums = [jnp.sum(dh, axis=0, keepdims=True), jnp.sum(dh * (xh * nw), axis=0, keepdims=True),
                jnp.sum(dn * xh, axis=0, keepdims=True)]
        if not gate:
            return (dx, *sums)
        dy = dx * rest[-1]
        return (dx, dy, *sums, jnp.sum(dx * rest[0], axis=0, keepdims=True), jnp.sum(dy, axis=0, keepdims=True))
    if not gate:
        return _rowwise(name, fn, [x, dh, dres], [nw, sc], [(D, F32)], [(1, D)] * 3, after=after)
    return _rowwise(name, fn, [x, dh, dres, gate[0]], [nw, sc, gate[1]], [(D, F32), (D, BF16)], [(1, D)] * 5,
                    after=after)


def _loss_head(x, tgt, fw, y, g):
    def fn(x, tgt, y, fw, g):
        rstd = lax.rsqrt(jnp.mean(x * x, axis=-1, keepdims=True) + EPS)
        xh = x * rstd
        err = xh * fw - tgt
        dout = err * (1.0 / D)
        dxh = dout * fw
        dx = rstd * (dxh - xh * jnp.mean(dxh * xh, axis=-1, keepdims=True))
        sq = jnp.sum(jnp.sum(err * err, axis=1, keepdims=True), axis=0, keepdims=True)
        return (dx, dx * g, sq, jnp.sum(dout * xh, axis=0, keepdims=True), jnp.sum(dx * y, axis=0, keepdims=True))
    return _rowwise("loss_head", fn, [x, tgt, y], [fw, g], [(D, F32), (D, BF16)], [(1, 1), (1, D), (1, D)])


def _ffn_fwd(tag, h, wg, wu, wd, x, g2, next_norm=None):
    def act(a, b):
        return a, b, a * _sigmoid(a) * b
    a, b, f = _mm(f"ffn_up_{tag}", [h, h], [wg, wu], "nt", [BF16, BF16, BF16], epi=act, groups=[0, 1], tn_cap=1408,
                  tm=1024)

    if next_norm is None:
        def res(y, x, g):
            return y, x + g * y
        y, xo = _mm(f"ffn_down_{tag}", [f], [wd], "nn", [F32, F32], epi=res, extras=[x], vecs=[g2])
        return a, b, f, y, xo, None

    def res_norm(y, x, g, nw, sc, sh):
        xo = x + g * y
        return y, xo, _norm_mod(xo, nw, sc, sh)
    assert wd.shape[1] == D
    y, xo, h_next = _mm(f"ffn_down_{tag}", [f], [wd], "nn", [F32, F32, BF16], epi=res_norm, extras=[x],
                        vecs=[g2, *next_norm], whole_rows=True)
    return a, b, f, y, xo, h_next


def _ffn_bwd(tag, dy, h, a, b, f, wg, wu, wd):
    def act_bwd(df, a, b):
        a, b = a.astype(F32), b.astype(F32)
        sg = _sigmoid(a)
        return df * b * (sg * (1.0 + a * (1.0 - sg))), df * (a * sg)
    da, db = _mm(f"ffn_dact_{tag}", [dy], [wd], "nt", [BF16, BF16], epi=act_bwd, extras=[a, b], tn_cap=1408, tm=1024)
    dwd = _mm_tn(f"ffn_dwd_{tag}", f, dy)
    dwg = _mm_tn(f"ffn_dwg_{tag}", da, h)
    dwu = _mm_tn(f"ffn_dwu_{tag}", db, h)
    dh = _mm(f"ffn_dh_{tag}", [da, db], [wg, wu], "nn", [F32])[0]
    return dh, dwg, dwu, dwd


def _conv_fwd(xr, w, b, tb=512):
    S, C = xr.shape
    tb = min(tb, S)

    def body(x_ref, halo_ref, w_ref, b_ref, pre_ref, out_ref):
        i = pl.program_id(0)
        halo = jnp.where(i > 0, halo_ref[...], 0.0)
        xe = jnp.concatenate([halo, x_ref[...]], axis=0)
        pre = w_ref[3:4, :] * x_ref[...] + b_ref[...]
        for j in (1, 2, 3):
            pre = pre + w_ref[3 - j:4 - j, :] * pltpu.roll(xe, j, axis=0)[8:, :]
        pre_ref[...] = pre
        out_ref[...] = pre * _sigmoid(pre)

    return pl.pallas_call(
        body, name="conv_fwd", grid=(S // tb,),
        in_specs=[pl.BlockSpec((tb, C), lambda i: (i, 0)),
                  pl.BlockSpec((8, C), lambda i: (jnp.maximum(i * (tb // 8) - 1, 0), 0)),
                  pl.BlockSpec((4, C), lambda i: (0, 0)), pl.BlockSpec((1, C), lambda i: (0, 0))],
        out_specs=[pl.BlockSpec((tb, C), lambda i: (i, 0))] * 2,
        out_shape=[jax.ShapeDtypeStruct((S, C), F32)] * 2,
        compiler_params=_params(("parallel",)))(xr, xr, w, b)


def _conv_bwd(dxc, pre, xr, w, tb=512):
    S, C = xr.shape
    tb = min(tb, S)
    nblk = S // tb

    def body(d_ref, p_ref, dn_ref, pn_ref, x_ref, w_ref, dx_ref, dw_ref, db_ref):
        i = pl.program_id(0)

        @pl.when(i == 0)
        def _():
            dw_ref[...] = jnp.zeros_like(dw_ref)
            db_ref[...] = jnp.zeros_like(db_ref)

        dpre = d_ref[...] * _silu_grad(p_ref[...])
        dnext = jnp.where(i < nblk - 1, dn_ref[...] * _silu_grad(pn_ref[...]), 0.0)
        pe = jnp.concatenate([dpre, dnext], axis=0)
        xx = x_ref[...]
        dx = w_ref[3:4, :] * dpre
        dw_ref[3:4, :] += jnp.sum(dpre * xx, axis=0, keepdims=True)
        for j in (1, 2, 3):
            ahead = pltpu.roll(pe, tb + 8 - j, axis=0)[:tb, :]
            dx = dx + w_ref[3 - j:4 - j, :] * ahead
            dw_ref[3 - j:4 - j, :] += jnp.sum(ahead * xx, axis=0, keepdims=True)
        dx_ref[...] = dx.astype(dx_ref.dtype)
        db_ref[...] += jnp.sum(dpre, axis=0, keepdims=True)

    blk = pl.BlockSpec((tb, C), lambda i: (i, 0))
    nxt = pl.BlockSpec((8, C), lambda i: (jnp.minimum((i + 1) * (tb // 8), S // 8 - 1), 0))
    return pl.pallas_call(
        body, name="conv_bwd", grid=(nblk,),
        in_specs=[blk, blk, nxt, nxt, blk, pl.BlockSpec((4, C), lambda i: (0, 0))],
        out_specs=[blk, pl.BlockSpec((4, C), lambda i: (0, 0)), pl.BlockSpec((1, C), lambda i: (0, 0))],
        out_shape=[jax.ShapeDtypeStruct((S, C), BF16), jax.ShapeDtypeStruct((4, C), F32),
                   jax.ShapeDtypeStruct((1, C), F32)],
        compiler_params=_params(("arbitrary",)))(dxc, pre, dxc, pre, xr, w)


def _iota(shape, dim):
    return lax.broadcasted_iota(jnp.int32, shape, dim)


def _colsel(m, lane, h):
    return jnp.sum(jnp.where(lane == h, m, 0.0), axis=1, keepdims=True)


def _cumsum_rows(v):
    r = _iota(v.shape, 0)
    k = 1
    while k < v.shape[0]:
        v = v + jnp.where(r >= k, pltpu.roll(v, k, axis=0), 0.0)
        k *= 2
    return v


def _suffix_sum_rows(v):
    n = v.shape[0]
    r = _iota(v.shape, 0)
    k = 1
    while k < n:
        v = v + jnp.where(r < n - k, pltpu.roll(v, n - k, axis=0), 0.0)
        k *= 2
    return v


def _ssd_fwd(xc, dtr, z, dtb, alog, dskl, nw):
    S = xc.shape[0]
    nc = S // L

    def body(xc_ref, dtr_ref, z_ref, dtb_ref, alog_ref, dsk_ref, nw_ref, ya_ref, y_ref, prev_ref,
             st_ref, cum_ref, cumT_ref):
        i = pl.program_id(0)

        @pl.when(i == 0)
        def _():
            st_ref[...] = jnp.zeros_like(st_ref)

        lane = _iota((L, 128), 1)
        lane1 = _iota((1, 128), 1)
        lo = lane < 64
        lo1 = lane1 < 64
        tril = _iota((L, L), 0) >= _iota((L, L), 1)
        dt = _softplus(dtr_ref[...] + dtb_ref[...])
        a_neg = -jnp.exp(alog_ref[...])
        cum = _cumsum_rows(dt * a_neg)
        cum_ref[...] = cum
        cumT_ref[...] = cum.T
        last_all = cum_ref[L - 1:L, :]
        prev_t = st_ref[...]
        prev_ref[0] = prev_t
        for g in range(2):
            bg = xc_ref[:, 1024 + g * 128:1152 + g * 128]
            cg = xc_ref[:, 1280 + g * 128:1408 + g * 128]
            gmat = _dot(cg, bg, NT)
            yoff = _dot(cg, prev_t[:, g * 512:(g + 1) * 512])
            bg_t = bg.T
            for jp in range(4):
                j = g * 4 + jp
                sl = slice(j * 128, (j + 1) * 128)
                xp = xc_ref[:, sl]
                cc = [_colsel(cum, lane, 2 * j), _colsel(cum, lane, 2 * j + 1)]
                cum_l = jnp.where(lo, cc[0], cc[1])
                dt_l = jnp.where(lo, _colsel(dt, lane, 2 * j), _colsel(dt, lane, 2 * j + 1))
                last_l = jnp.where(lo1, _colsel(last_all, lane1, 2 * j), _colsel(last_all, lane1, 2 * j + 1))
                xd = xp * dt_l
                ys = []
                for hh in range(2):
                    seg = cc[hh] - cumT_ref[2 * j + hh:2 * j + hh + 1, :]
                    dm = jnp.where(tril, jnp.exp(seg), 0.0)
                    ys.append(_dot(gmat * dm, xd))
                y_ref[:, sl] = (jnp.where(lo, ys[0], ys[1]) + jnp.exp(cum_l) * yoff[:, jp * 128:(jp + 1) * 128]
                                + dsk_ref[:, sl] * xp)
                st_ref[:, sl] = prev_t[:, sl] * jnp.exp(last_l) + _dot(bg_t, xd * jnp.exp(last_l - cum_l))
        for g in range(2):
            sl = slice(g * 512, (g + 1) * 512)
            zz = z_ref[:, sl]
            yg = y_ref[:, sl] * (zz * _sigmoid(zz))
            rstd = lax.rsqrt(jnp.mean(yg * yg, axis=-1, keepdims=True) + EPS)
            ya_ref[:, sl] = (yg * rstd * nw_ref[:, sl]).astype(ya_ref.dtype)

    blk = lambda c: pl.BlockSpec((L, c), lambda i: (i, 0))
    vec = lambda c: pl.BlockSpec((1, c), lambda i: (0, 0))
    return pl.pallas_call(
        body, name="ssd_fwd", grid=(nc,),
        in_specs=[blk(1536), blk(128), blk(1024), vec(128), vec(128), vec(1024), vec(1024)],
        out_specs=[blk(1024), blk(1024), pl.BlockSpec((1, NSTATE, 1024), lambda i: (i, 0, 0))],
        out_shape=[jax.ShapeDtypeStruct((S, 1024), BF16), jax.ShapeDtypeStruct((S, 1024), F32),
                   jax.ShapeDtypeStruct((nc, NSTATE, 1024), F32)],
        scratch_shapes=[pltpu.VMEM((NSTATE, 1024), F32), pltpu.VMEM((L, 128), F32), pltpu.VMEM((L, 128), F32)],
        compiler_params=_params(("arbitrary",)))(xc, dtr, z, dtb, alog, dskl, nw)


def _ssd_bwd(dya, y, z, xc, dtr, prev, dtb, alog, dskl, nw):
    S = xc.shape[0]
    nc = S // L

    def body(dya_ref, y_ref, z_ref, xc_ref, dtr_ref, prev_ref, dtb_ref, alog_ref, dsk_ref, nw_ref,
             dz_ref, dxc_ref, ddtr_ref, dnw_ref, ddsk_ref, dalog_ref, ddtb_ref,
             dst_ref, cum_ref, cumT_ref, dy_ref, dskacc_ref):
        i = pl.program_id(0)

        @pl.when(i == 0)
        def _():
            dst_ref[...] = jnp.zeros_like(dst_ref)
            dskacc_ref[...] = jnp.zeros_like(dskacc_ref)
            dnw_ref[...] = jnp.zeros_like(dnw_ref)
            dalog_ref[...] = jnp.zeros_like(dalog_ref)
            ddtb_ref[...] = jnp.zeros_like(ddtb_ref)

        lane = _iota((L, 128), 1)
        lane1 = _iota((1, 128), 1)
        lo = lane < 64
        lo1 = lane1 < 64
        r2, c2 = _iota((L, L), 0), _iota((L, L), 1)
        tril = r2 >= c2
        triu = r2 <= c2
        is_last = _iota((L, 1), 0) == L - 1

        for g in range(2):
            sl = slice(g * 512, (g + 1) * 512)
            zz = z_ref[:, sl]
            sg = _sigmoid(zz)
            zg = zz * sg
            yv = y_ref[:, sl]
            yg = yv * zg
            rstd = lax.rsqrt(jnp.mean(yg * yg, axis=-1, keepdims=True) + EPS)
            xh = yg * rstd
            d_out = dya_ref[:, sl]
            dnw_ref[:, sl] += jnp.sum(d_out * xh, axis=0, keepdims=True)
            dyn = d_out * nw_ref[:, sl]
            dyg = rstd * (dyn - xh * jnp.mean(dyn * xh, axis=-1, keepdims=True))
            dy_ref[:, sl] = dyg * zg
            dz_ref[:, sl] = (dyg * yv * (sg * (1.0 + zz * (1.0 - sg)))).astype(dz_ref.dtype)

        dtin = dtr_ref[...] + dtb_ref[...]
        dt = _softplus(dtin)
        a_neg = -jnp.exp(alog_ref[...])
        cum = _cumsum_rows(dt * a_neg)
        cum_ref[...] = cum
        cumT_ref[...] = cum.T
        last_all = cum_ref[L - 1:L, :]
        prev_t = prev_ref[0]
        dn_t = dst_ref[...]
        dcum = jnp.zeros((L, 128), F32)
        ddt = jnp.zeros((L, 128), F32)
        for g in range(2):
            gsl = slice(g * 512, (g + 1) * 512)
            bg = xc_ref[:, 1024 + g * 128:1152 + g * 128]
            cg = xc_ref[:, 1280 + g * 128:1408 + g * 128]
            gmat = _dot(cg, bg, NT)
            gmat_t = _dot(bg, cg, NT)
            pg = prev_t[:, gsl]
            zmat = _dot(cg, pg)
            dgm = jnp.zeros((L, L), F32)
            dgm_t = jnp.zeros((L, L), F32)
            db_acc = jnp.zeros((L, NSTATE), F32)
            dz_parts, cd_parts = [], []
            for jp in range(4):
                j = g * 4 + jp
                sl = slice(j * 128, (j + 1) * 128)
                xp = xc_ref[:, sl]
                dyp = dy_ref[:, sl]
                cc = [_colsel(cum, lane, 2 * j), _colsel(cum, lane, 2 * j + 1)]
                lc = [_colsel(last_all, lane1, 2 * j), _colsel(last_all, lane1, 2 * j + 1)]
                cum_l = jnp.where(lo, cc[0], cc[1])
                dt_l = jnp.where(lo, _colsel(dt, lane, 2 * j), _colsel(dt, lane, 2 * j + 1))
                last_l = jnp.where(lo1, lc[0], lc[1])
                e_l = jnp.exp(cum_l)
                dte_l = jnp.exp(last_l - cum_l)
                cd_l = jnp.exp(last_l)
                cd_parts.append(cd_l)
                xd = xp * dt_l
                dskacc_ref[:, sl] += jnp.sum(dyp * xp, axis=0, keepdims=True)
                dxp = dsk_ref[:, sl] * dyp
                t = dyp * (e_l * zmat[:, jp * 128:(jp + 1) * 128])
                dcc = [jnp.sum(jnp.where(lo, t, 0.0), axis=1, keepdims=True),
                       jnp.sum(jnp.where(lo, 0.0, t), axis=1, keepdims=True)]
                dz_parts.append(e_l * dyp)
                dnp_ = dn_t[:, sl]
                t2 = jnp.sum(dnp_ * prev_t[:, sl], axis=0, keepdims=True)
                dcd = [jnp.sum(jnp.where(lo1, t2, 0.0), axis=1, keepdims=True),
                       jnp.sum(jnp.where(lo1, 0.0, t2), axis=1, keepdims=True)]
                wm = _dot(bg, dnp_)
                dxd = wm * dte_l
                t3 = wm * xd
                ddte = [jnp.sum(jnp.where(lo, t3, 0.0), axis=1, keepdims=True),
                        jnp.sum(jnp.where(lo, 0.0, t3), axis=1, keepdims=True)]
                db_acc = db_acc + _dot(xd * dte_l, dnp_, NT)
                for hh in range(2):
                    h = 2 * j + hh
                    half = lo if hh == 0 else jnp.logical_not(lo)
                    row = cumT_ref[h:h + 1, :]
                    dm = jnp.where(tril, jnp.exp(cc[hh] - row), 0.0)
                    dm_t = jnp.where(triu, jnp.exp(row - cc[hh]), 0.0)
                    dym = jnp.where(half, dyp, 0.0)
                    u = _dot(dym, xd, NT) * dm
                    u_t = _dot(xd, dym, NT) * dm_t
                    dxd = dxd + _dot(gmat_t * dm_t, dym)
                    dcc[hh] = dcc[hh] + jnp.sum(u * gmat, axis=1, keepdims=True) - jnp.sum(u_t * gmat_t, axis=1, keepdims=True)
                    dgm = dgm + u
                    dgm_t = dgm_t + u_t
                    dte_c = jnp.exp(lc[hh] - cc[hh])
                    dcc[hh] = dcc[hh] - ddte[hh] * dte_c
                    endc = dcd[hh] * jnp.exp(lc[hh]) + jnp.sum(ddte[hh] * dte_c, axis=0, keepdims=True)
                    dcc[hh] = dcc[hh] + jnp.where(is_last, endc, 0.0)
                    dcum = jnp.where(lane == h, dcc[hh], dcum)
                dxc_ref[:, sl] = dxp + dxd * dt_l
                t4 = dxd * xp
                ddt = jnp.where(lane == 2 * j, jnp.sum(jnp.where(lo, t4, 0.0), axis=1, keepdims=True), ddt)
                ddt = jnp.where(lane == 2 * j + 1, jnp.sum(jnp.where(lo, 0.0, t4), axis=1, keepdims=True), ddt)
            dzg = jnp.concatenate(dz_parts, axis=1)
            dst_ref[:, gsl] = dn_t[:, gsl] * jnp.concatenate(cd_parts, axis=1) + _dot(cg.T, dzg)
            dxc_ref[:, 1280 + g * 128:1408 + g * 128] = _dot(dgm, bg) + _dot(dzg, pg, NT)
            dxc_ref[:, 1024 + g * 128:1152 + g * 128] = _dot(dgm_t, cg) + db_acc
        dla = _suffix_sum_rows(dcum)
        ddt = ddt + dla * a_neg
        dalog_ref[...] += jnp.sum(dla * dt, axis=0, keepdims=True) * a_neg
        ddtr = jnp.where(lane < 16, ddt * _sigmoid(dtin), 0.0)
        ddtr_ref[...] = ddtr.astype(ddtr_ref.dtype)
        ddtb_ref[...] += jnp.sum(ddtr, axis=0, keepdims=True)

        @pl.when(i == nc - 1)
        def _():
            seg = (_iota((1024, 128), 0) // 64 == _iota((1024, 128), 1)).astype(F32)
            acc8 = jnp.broadcast_to(dskacc_ref[...], (8, 1024))
            ddsk_ref[...] = lax.dot_general(acc8, seg, NN, precision=lax.Precision.HIGHEST,
                                            preferred_element_type=F32)

    rev = lambda c: pl.BlockSpec((L, c), lambda i: (nc - 1 - i, 0))
    vec = lambda c: pl.BlockSpec((1, c), lambda i: (0, 0))
    return pl.pallas_call(
        body, name="ssd_bwd", grid=(nc,),
        in_specs=[rev(1024), rev(1024), rev(1024), rev(1536), rev(128),
                  pl.BlockSpec((1, NSTATE, 1024), lambda i: (nc - 1 - i, 0, 0)),
                  vec(128), vec(128), vec(1024), vec(1024)],
        out_specs=[rev(1024), rev(1536), rev(128), vec(1024), pl.BlockSpec((8, 128), lambda i: (0, 0)),
                   vec(128), vec(128)],
        out_shape=[jax.ShapeDtypeStruct((S, 1024), BF16), jax.ShapeDtypeStruct((S, 1536), F32),
                   jax.ShapeDtypeStruct((S, 128), BF16), jax.ShapeDtypeStruct((1, 1024), F32),
                   jax.ShapeDtypeStruct((8, 128), F32), jax.ShapeDtypeStruct((1, 128), F32),
                   jax.ShapeDtypeStruct((1, 128), F32)],
        scratch_shapes=[pltpu.VMEM((NSTATE, 1024), F32), pltpu.VMEM((L, 128), F32), pltpu.VMEM((L, 128), F32),
                        pltpu.VMEM((L, 1024), F32), pltpu.VMEM((1, 1024), F32)],
        compiler_params=_params(("arbitrary",)))(dya, y, z, xc, dtr, prev, dtb, alog, dskl, nw)


def _layer_norm_parts(vg):
    mu = jnp.mean(vg, axis=-1, keepdims=True)
    vc = vg - mu
    rstd = lax.rsqrt(jnp.mean(vc * vc, axis=-1, keepdims=True) + EPS)
    return vc * rstd, rstd


def _gmlp_fwd(u, v, lnw, lnb, ws, bse, tb=512):
    S = u.shape[0]
    tb = min(tb, S)

    def body(u_ref, v_ref, lnw_ref, lnb_ref, ws_ref, bse_ref, o_ref, vn_ref):
        tril = _iota((L, L), 0) >= _iota((L, L), 1)
        xh, _ = _layer_norm_parts(_gelu(v_ref[...]))
        vn_ref[...] = xh * lnw_ref[...] + lnb_ref[...]
        for g in range(8):
            w = jnp.where(tril, ws_ref[g], 0.0)
            gs = slice(g * 128, (g + 1) * 128)
            for ch in range(tb // L):
                rs = slice(ch * L, (ch + 1) * L)
                sv = _dot(w, vn_ref[rs, gs]) + bse_ref[g]
                o_ref[rs, gs] = (_gelu(u_ref[rs, gs]) * sv).astype(o_ref.dtype)

    blk = pl.BlockSpec((tb, 1024), lambda i: (i, 0))
    vec = pl.BlockSpec((1, 1024), lambda i: (0, 0))
    cube = pl.BlockSpec((8, L, 128), lambda i: (0, 0, 0))
    return pl.pallas_call(
        body, name="gmlp_fwd", grid=(S // tb,), in_specs=[blk, blk, vec, vec, cube, cube], out_specs=blk,
        out_shape=jax.ShapeDtypeStruct((S, 1024), BF16), scratch_shapes=[pltpu.VMEM((tb, 1024), F32)],
        compiler_params=_params(("parallel",)))(u, v, lnw, lnb, ws, bse)


def _gmlp_bwd(dyb, u, v, lnw, lnb, ws, bse, tb=512):
    S = u.shape[0]
    tb = min(tb, S)

    def body(d_ref, u_ref, v_ref, lnw_ref, lnb_ref, ws_ref, bse_ref,
             du_ref, dv_ref, dws_ref, dbse_ref, dlnw_ref, dlnb_ref, vn_ref, dvn_ref):
        @pl.when(pl.program_id(0) == 0)
        def _():
            dws_ref[...] = jnp.zeros_like(dws_ref)
            dbse_ref[...] = jnp.zeros_like(dbse_ref)
            dlnw_ref[...] = jnp.zeros_like(dlnw_ref)
            dlnb_ref[...] = jnp.zeros_like(dlnb_ref)

        tril = _iota((L, L), 0) >= _iota((L, L), 1)
        vv = v_ref[...]
        xh, rstd = _layer_norm_parts(_gelu(vv))
        vn_ref[...] = xh * lnw_ref[...] + lnb_ref[...]
        for g in range(8):
            w = jnp.where(tril, ws_ref[g], 0.0)
            w_t = w.T
            gs = slice(g * 128, (g + 1) * 128)
            dw = jnp.zeros((L, L), F32)
            dbs = jnp.zeros((L, 128), F32)
            for ch in range(tb // L):
                rs = slice(ch * L, (ch + 1) * L)
                vn = vn_ref[rs, gs]
                sv = _dot(w, vn) + bse_ref[g]
                uu = u_ref[rs, gs]
                dd = d_ref[rs, gs]
                du_ref[rs, gs] = (dd * sv * _gelu_grad(uu)).astype(du_ref.dtype)
                dsv = dd * _gelu(uu)
                dw = dw + _dot(dsv, vn, NT)
                dbs = dbs + dsv
                dvn_ref[rs, gs] = _dot(w_t, dsv)
            dws_ref[g] += jnp.where(tril, dw, 0.0)
            dbse_ref[g] += dbs
        dvn = dvn_ref[...]
        dlnw_ref[...] += jnp.sum(dvn * xh, axis=0, keepdims=True)
        dlnb_ref[...] += jnp.sum(dvn, axis=0, keepdims=True)
        dxh = dvn * lnw_ref[...]
        dvg = rstd * (dxh - jnp.mean(dxh, axis=-1, keepdims=True) - xh * jnp.mean(dxh * xh, axis=-1, keepdims=True))
        dv_ref[...] = (dvg * _gelu_grad(vv)).astype(dv_ref.dtype)

    blk = pl.BlockSpec((tb, 1024), lambda i: (i, 0))
    vec = pl.BlockSpec((1, 1024), lambda i: (0, 0))
    cube = pl.BlockSpec((8, L, 128), lambda i: (0, 0, 0))
    return pl.pallas_call(
        body, name="gmlp_bwd", grid=(S // tb,), in_specs=[blk, blk, blk, vec, vec, cube, cube],
        out_specs=[blk, blk, cube, cube, vec, vec],
        out_shape=[jax.ShapeDtypeStruct((S, 1024), BF16), jax.ShapeDtypeStruct((S, 1024), BF16),
                   jax.ShapeDtypeStruct((8, L, 128), F32), jax.ShapeDtypeStruct((8, L, 128), F32),
                   jax.ShapeDtypeStruct((1, 1024), F32), jax.ShapeDtypeStruct((1, 1024), F32)],
        scratch_shapes=[pltpu.VMEM((tb, 1024), F32), pltpu.VMEM((tb, 1024), F32)],
        compiler_params=_params(("arbitrary",)))(dyb, u, v, lnw, lnb, ws, bse)


def _lane_sum(name, a):
    def body(a_ref, o_ref):
        o_ref[...] = jnp.sum(a_ref[...], axis=1, keepdims=True)
    return pl.pallas_call(body, name=name, out_shape=jax.ShapeDtypeStruct((a.shape[0], 1), F32))(a)


def _bucket_onehot_t():
    qi = np.arange(L)[:, None]
    sj = np.arange(2 * L)[None, :]
    dist = np.maximum(qi + L - sj, 0)
    log_ratio = (np.log(np.maximum(dist, 1).astype(np.float32) / np.float32(16)) / np.float32(math.log(128 / 16)))
    large = 16 + (log_ratio.astype(np.float32) * np.float32(16)).astype(np.int32)
    bucket = np.where(dist < 16, dist, np.minimum(large, 31)).reshape(-1)
    return (np.arange(32)[:, None] == bucket[None, :]).astype(np.float32)


def _rel_bias(table_t, onehot_t):
    def body(t_ref, oh_ref, o_ref):
        o_ref[...] = lax.dot_general(t_ref[...], oh_ref[...], NN, precision=lax.Precision.HIGHEST,
                                     preferred_element_type=F32)
    return pl.pallas_call(body, name="rel_bias", out_shape=jax.ShapeDtypeStruct((16, L * 2 * L), F32),
                          compiler_params=_params())(table_t, onehot_t)


def _rel_bias_bwd(dbias, onehot_t):
    def body(d_ref, oh_ref, o_ref):
        o_ref[...] = lax.dot_general(d_ref[...], oh_ref[...], NT, precision=lax.Precision.HIGHEST,
                                     preferred_element_type=F32)
    return pl.pallas_call(body, name="rel_bias_bwd", out_shape=jax.ShapeDtypeStruct((16, 32), F32),
                          compiler_params=_params())(dbias, onehot_t)


def _band(kp, kc, lo):
    kk = jnp.concatenate([kp, kc], axis=0)
    kr = pltpu.roll(kk, 64, axis=1)
    return [jnp.where(lo, kk, kr), jnp.where(lo, kr, kk)]


def _attn_rows(ref, j, lo):
    parts = []
    for t in range(8):
        pair = ref[:, (4 * j + t // 2) * 128:(4 * j + t // 2 + 1) * 128]
        parts.append(jnp.where(lo if t % 2 == 0 else jnp.logical_not(lo), pair, 0.0))
    return jnp.concatenate(parts, axis=0)


def _attn_mask(i, rows):
    qi, sj = _iota((rows, 2 * L), 0) & (L - 1), _iota((rows, 2 * L), 1)
    rel = qi + L - sj
    return (rel >= 0) & (rel < L) & ((sj >= L) | (i > 0))


def _per_head_col(vals):
    return jnp.concatenate([jnp.broadcast_to(v, (L, 1)) for v in vals], axis=0)


SMEM = pl.BlockSpec(memory_space=pltpu.SMEM)


def _attn_fwd(qkv, bias, sinks):
    S = qkv.shape[0]
    nb = S // L
    scale = 64 ** -0.5

    def body(sink_ref, q_ref, kc_ref, vc_ref, kp_ref, vp_ref, bias_ref, o_ref, lse_ref):
        i = pl.program_id(0)
        lane = _iota((L, 128), 1)
        lo = lane < 64
        lo2 = _iota((2 * L, 128), 1) < 64
        mask = _attn_mask(i, L)
        kd = _band(kp_ref[...], kc_ref[...], lo2)
        vd = _band(vp_ref[...], vc_ref[...], lo2)
        lse = jnp.zeros((L, 128), F32)
        for pr in range(8):
            sl = slice(pr * 128, (pr + 1) * 128)
            qp = q_ref[:, sl]
            j = pr // 4
            outs = []
            for hh in range(2):
                h = 2 * pr + hh
                qm = jnp.where(lo if hh == 0 else jnp.logical_not(lo), qp, 0.0)
                lg = jnp.where(mask, _dot(qm, kd[j], NT) * scale + bias_ref[h], NEG_INF)
                s = sink_ref[h]
                m = jnp.maximum(jnp.max(lg, axis=1, keepdims=True), s)
                p = jnp.where(mask, jnp.exp(lg - m), 0.0)
                den = jnp.sum(p, axis=1, keepdims=True) + jnp.exp(s - m)
                outs.append(_dot(p / den, vd[j]))
                lse = jnp.where(lane == h, m + jnp.log(den), lse)
            o_ref[:, sl] = jnp.where(lo, outs[0], outs[1]).astype(o_ref.dtype)
        lse_ref[...] = lse

    prev = lambda col: pl.BlockSpec((L, 128), lambda i: (jnp.maximum(i - 1, 0), col))
    cur = lambda col: pl.BlockSpec((L, 128), lambda i: (i, col))
    return pl.pallas_call(
        body, name="attn_fwd", grid=(nb,),
        in_specs=[SMEM, pl.BlockSpec((L, 1024), lambda i: (i, 0)), cur(8), cur(9), prev(8), prev(9),
                  pl.BlockSpec((16, L, 2 * L), lambda i: (0, 0, 0))],
        out_specs=[pl.BlockSpec((L, 1024), lambda i: (i, 0)), pl.BlockSpec((L, 128), lambda i: (i, 0))],
        out_shape=[jax.ShapeDtypeStruct((S, 1024), BF16), jax.ShapeDtypeStruct((S, 128), F32)],
        compiler_params=_params(("parallel",)))(sinks, qkv, qkv, qkv, qkv, qkv, bias)


def _attn_bwd(qkv, d_o, lse, bias, sinks):
    S = qkv.shape[0]
    nb = S // L
    scale = 64 ** -0.5

    def body(sink_ref, q_ref, kc_ref, vc_ref, kp_ref, vp_ref, do_ref, lse_ref, bias_ref,
             dq_ref, dkv_ref, dbias_ref, dsink_ref, dbq_ref, dbkv_ref, carry_ref):
        i = pl.program_id(0)

        @pl.when(i == 0)
        def _():
            dbias_ref[...] = jnp.zeros_like(dbias_ref)
            dsink_ref[...] = jnp.zeros_like(dsink_ref)
            dbq_ref[...] = jnp.zeros_like(dbq_ref)
            dbkv_ref[...] = jnp.zeros_like(dbkv_ref)
            carry_ref[...] = jnp.zeros_like(carry_ref)

        @pl.when(i < nb)
        def _():
            lane = _iota((L, 128), 1)
            lane1 = _iota((1, 128), 1)
            lo = lane < 64
            lo2 = _iota((2 * L, 128), 1) < 64
            mask = _attn_mask(i, 8 * L)
            kd = _band(kp_ref[...], kc_ref[...], lo2)
            vd = _band(vp_ref[...], vc_ref[...], lo2)
            lse_all = lse_ref[...]
            dsink = jnp.zeros((1, 128), F32)
            tot_k, tot_v = [], []
            for j in range(2):
                q_all = _attn_rows(q_ref, j, lo)
                do_all = _attn_rows(do_ref, j, lo)
                lse_col = _per_head_col([_colsel(lse_all, lane, 8 * j + t) for t in range(8)])
                lg = _dot(q_all, kd[j], NT) * scale + bias_ref[8 * j:8 * j + 8].reshape(8 * L, 2 * L)
                p = jnp.where(mask, jnp.exp(jnp.where(mask, lg, NEG_INF) - lse_col), 0.0)
                dp = _dot(do_all, vd[j], NT)
                delta = jnp.sum(p * dp, axis=1, keepdims=True)
                ds = p * (dp - delta)
                dbias_ref[8 * j:8 * j + 8] += ds.reshape(8, L, 2 * L)
                s = _per_head_col([sink_ref[8 * j + t] for t in range(8)])
                sink_part = -jnp.exp(s - lse_col) * delta
                for t in range(8):
                    dsink = dsink + jnp.where(lane1 == 8 * j + t,
                                              jnp.sum(sink_part[t * L:(t + 1) * L], axis=0, keepdims=True), 0.0)
                dss = ds * scale
                dq_all = _dot(dss, kd[j])
                for t in range(0, 8, 2):
                    sl = slice((4 * j + t // 2) * 128, (4 * j + t // 2 + 1) * 128)
                    dq = jnp.where(lo, dq_all[t * L:(t + 1) * L], dq_all[(t + 1) * L:(t + 2) * L])
                    dq_ref[:, sl] = dq.astype(dq_ref.dtype)
                    dbq_ref[:, sl] += jnp.sum(dq, axis=0, keepdims=True)
                acc_k = _dot(dss, q_all, TN)
                acc_v = _dot(p, do_all, TN)
                tot_k.append(acc_k + pltpu.roll(acc_k, 64, axis=1))
                tot_v.append(acc_v + pltpu.roll(acc_v, 64, axis=1))
            dsink_ref[...] += dsink
            dkv = jnp.concatenate([jnp.where(lo2, tot_k[0], tot_k[1]), jnp.where(lo2, tot_v[0], tot_v[1])], axis=1)
            dbkv_ref[...] += jnp.sum(dkv, axis=0, keepdims=True)
            dkv_ref[...] = (carry_ref[...] + dkv[:L, :]).astype(dkv_ref.dtype)
            carry_ref[...] = dkv[L:, :]

        @pl.when(i == nb)
        def _():
            dkv_ref[...] = carry_ref[...].astype(dkv_ref.dtype)

    c = lambda i: jnp.minimum(i, nb - 1)
    prev = lambda col: pl.BlockSpec((L, 128), lambda i: (jnp.maximum(c(i) - 1, 0), col))
    cur = lambda col: pl.BlockSpec((L, 128), lambda i: (c(i), col))
    row = lambda w: pl.BlockSpec((L, w), lambda i: (c(i), 0))
    cube = pl.BlockSpec((16, L, 2 * L), lambda i: (0, 0, 0))
    vec = lambda w: pl.BlockSpec((1, w), lambda i: (0, 0))
    return pl.pallas_call(
        body, name="attn_bwd", grid=(nb + 1,),
        in_specs=[SMEM, row(1024), cur(8), cur(9), prev(8), prev(9), row(1024), row(128), cube],
        out_specs=[row(1024), pl.BlockSpec((L, 256), lambda i: (jnp.maximum(i - 1, 0), 0)), cube,
                   vec(128), vec(1024), vec(256)],
        out_shape=[jax.ShapeDtypeStruct((S, 1024), BF16), jax.ShapeDtypeStruct((S, 256), BF16),
                   jax.ShapeDtypeStruct((16, L, 2 * L), F32), jax.ShapeDtypeStruct((1, 128), F32),
                   jax.ShapeDtypeStruct((1, 1024), F32), jax.ShapeDtypeStruct((1, 256), F32)],
        scratch_shapes=[pltpu.VMEM((L, 256), F32)],
        compiler_params=_params(("arbitrary",)))(sinks, qkv, qkv, qkv, qkv, qkv, d_o, lse, bias)


def _pad_lanes(a, n=128):
    return jnp.pad(a, ((0, 0), (0, n - a.shape[1])))


def _local_step(x, tgt, mod, w_in, P, io):
    md = [[mod[l:l + 1, k * D:(k + 1) * D] for k in range(6)] for l in range(2)]
    G, g = {}, {}

    sh1, sc1, g1, sh2, sc2, g2 = md[0]
    nmw0, nfw0 = P["norm_mix_w"][0:1], P["norm_ffn_w"][0:1]
    h0 = _norm_mod_fwd("norm_mix_0", x, nmw0, sc1, sh1, after=io["start"])
    segs = {"z": w_in[0:1024], "xbc": w_in[1024:2560], "dt": jnp.pad(w_in[2560:2576], ((0, 112), (0, 0))),
            "u": w_in[2576:3600], "v": w_in[3600:4624]}
    proj = dict(zip(segs, _mm_shared_lhs("in_proj", h0, list(segs.values()))))
    conv_w, conv_b = P["conv_w"][0], P["conv_b"]
    pre, xc = _conv_fwd(proj["xbc"], conv_w, conv_b)
    dtb, alog = _pad_lanes(P["dt_bias"]), _pad_lanes(P["a_log"])
    dskl = jnp.repeat(P["d_skip"], 64, axis=1)
    ya, y_ssd, prev = _ssd_fwd(xc, proj["dt"], proj["z"], dtb, alog, dskl, P["ssm_norm_w"])
    ws = P["gmlp_ws"][0]
    bse = jnp.broadcast_to(P["gmlp_bs"][0][:, :, None], (8, L, 128))
    yb = _gmlp_fwd(proj["u"], proj["v"], P["gmlp_ln_w"], P["gmlp_ln_b"], ws, bse)
    W = dict(io["weights0"]((ya, yb)))
    w_oa, w_ob = W["out_w"][:1024], W["out_w"][1024:]

    def res(y, x, gate, nw, sc, sh):
        xo = x + gate * y
        return y, xo, _norm_mod(xo, nw, sc, sh)
    mix0, x1, h0f = _mm("out_proj_0", [ya, yb], [w_oa, w_ob], "nn", [F32, F32, BF16], epi=res, extras=[x],
                        vecs=[g1, nfw0, sc2, sh2], whole_rows=True)
    sh1b, sc1b, g1b, sh2b, sc2b, g2b = md[1]
    nmw1, nfw1 = P["norm_mix_w"][1:2], P["norm_ffn_w"][1:2]
    a0, b0, f0, y0, x2, h1 = _ffn_fwd("0", h0f, W["gate_wt0"], W["up_wt0"], W["down_w0"], x1, g2,
                                      next_norm=(nmw1, sc1b, sh1b))

    W.update(io["weights1"](x2))
    qkv = _mm("qkv_proj", [h1], [W["qkv_wt"]], "nt", [F32], epi=lambda acc, b: acc + b, vecs=[P["qkv_b"]])[0]
    onehot_t = jnp.asarray(_bucket_onehot_t())
    bias = _rel_bias(P["rel_table"].T, onehot_t).reshape(16, L, 2 * L)
    sinks = P["sinks"].reshape(16)
    att, lse = _attn_fwd(qkv, bias, sinks)

    def res_b(y, x, gate, b, nw, sc, sh):
        y = y + b
        xo = x + gate * y
        return y, xo, _norm_mod(xo, nw, sc, sh)
    mix1, x3, h1f = _mm("o_proj", [att], [W["o_w"]], "nn", [F32, F32, BF16], epi=res_b, extras=[x2],
                        vecs=[g1b, P["o_b"], nfw1, sc2b, sh2b], whole_rows=True)
    a1, b1, f1, y1, x4, _ = _ffn_fwd("1", h1f, W["gate_wt1"], W["up_wt1"], W["down_w1"], x3, g2b)

    dx, dy, sq, g["final_norm_w"], dg2b = _loss_head(x4, tgt, P["final_norm_w"], y1, g2b)

    dh, dwg1, dwu1, dwd1 = _ffn_bwd("1", dy, h1f, a1, b1, f1, W["gate_wt1"], W["up_wt1"], W["down_w1"])
    dx, dmix, dsh2b, dsc2b, dnfw1, dg1b, g["o_b"] = _norm_mod_bwd("norm_ffn_bwd_1", x3, dh, dx, nfw1, sc2b,
                                                                 gate=(mix1, g1b))
    G["o_w"] = _mm_tn("o_dw", att, dmix)
    d_att = _mm("o_dx", [dmix], [W["o_w"]], "nt", [F32])[0]
    dq, dkv, dbias, dsinks, dbq, dbkv = _attn_bwd(qkv, d_att, lse, bias, sinks)
    g["rel_table"] = _rel_bias_bwd(dbias.reshape(16, L * 2 * L), onehot_t).T
    g["sinks"] = dsinks[:, :16]
    g["qkv_b"] = jnp.concatenate([dbq, dbkv], axis=1)
    w_q, w_kv = W["qkv_wt"][:1024], W["qkv_wt"][1024:]
    G["qkv_wt"] = jnp.concatenate([_mm_tn("qkv_dwq", dq, h1), _mm_tn("qkv_dwkv", dkv, h1)], axis=0)
    dh = _mm("qkv_dx", [dq, dkv], [w_q, w_kv], "nn", [F32])[0]
    behind = io["grads1"]({"qkv_wt": G.pop("qkv_wt"), "o_w": G.pop("o_w"), "gate_wt1": dwg1, "up_wt1": dwu1,
                           "down_w1": dwd1})
    dx, dy, dsh1b, dsc1b, dnmw1, dg2, _ = _norm_mod_bwd("norm_mix_bwd_1", x2, dh, dx, nmw1, sc1b, gate=(y0, g2),
                                                        after=behind)

    dh, dwg0, dwu0, dwd0 = _ffn_bwd("0", dy, h0f, a0, b0, f0, W["gate_wt0"], W["up_wt0"], W["down_w0"])
    behind = io["grads_ffn0"]({"gate_wt0": dwg0, "up_wt0": dwu0, "down_w0": dwd0})
    dx, dmix, dsh2, dsc2, dnfw0, dg1, _ = _norm_mod_bwd("norm_ffn_bwd_0", x1, dh, dx, nfw0, sc2, gate=(mix0, g1),
                                                        after=behind)
    G["out_w"] = jnp.concatenate([_mm_tn("out_dwa", ya, dmix), _mm_tn("out_dwb", yb, dmix)], axis=0)
    dya = _mm("out_dxa", [dmix], [w_oa], "nt", [F32])[0]
    dyb = _mm("out_dxb", [dmix], [w_ob], "nt", [F32])[0]
    du, dv, dws, dbse, g["gmlp_ln_w"], g["gmlp_ln_b"] = _gmlp_bwd(dyb, proj["u"], proj["v"], P["gmlp_ln_w"],
                                                                 P["gmlp_ln_b"], ws, bse)
    g["gmlp_ws"] = dws[None]
    g["gmlp_bs"] = _lane_sum("gmlp_dbs", dbse.reshape(8 * L, 128)).reshape(1, 8, L)
    dz, dxc, ddt, g["ssm_norm_w"], ddsk, dalog, ddtb = _ssd_bwd(dya, y_ssd, proj["z"], xc, proj["dt"], prev,
                                                                dtb, alog, dskl, P["ssm_norm_w"])
    g["d_skip"], g["a_log"], g["dt_bias"] = ddsk[0:1, :16], dalog[:, :16], ddtb[:, :16]
    dxr, dconv_w, g["conv_b"] = _conv_bwd(dxc, pre, proj["xbc"], conv_w)
    g["conv_w"] = dconv_w[None]
    dsegs = {"z": dz, "xbc": dxr, "dt": ddt, "u": du, "v": dv}
    dws_in = dict(zip(dsegs, _mm_tn_shared_rhs("in_dw", list(dsegs.values()), h0)))
    G["in_wt"] = jnp.concatenate([dws_in["z"], dws_in["xbc"], dws_in["dt"][:16], dws_in["u"], dws_in["v"]], axis=0)
    keys = ["z", "xbc", "dt", "u", "v"]
    dh = _mm("in_dx", [dsegs[k] for k in keys], [segs[k] for k in keys], "nn", [F32])[0]
    dx, dsh1, dsc1, dnmw0 = _norm_mod_bwd("norm_mix_bwd_0", x, dh, dx, nmw0, sc1)

    g["norm_mix_w"] = jnp.concatenate([dnmw0, dnmw1], axis=0)
    g["norm_ffn_w"] = jnp.concatenate([dnfw0, dnfw1], axis=0)
    dmod = jnp.concatenate([jnp.concatenate([dsh1, dsc1, dg1, dsh2, dsc2, dg2], axis=1),
                            jnp.concatenate([dsh1b, dsc1b, dg1b, dsh2b, dsc2b, dg2b], axis=1)], axis=0)
    return sq, dx, dmod, G, g


def _ada_fwd(c_all, ada_w, ada_b):
    n = ada_w.shape[2]
    tn = _col_tile(n, 512)

    def body(c_ref, w_ref, b_ref, o_ref):
        cc = c_ref[...]
        o_ref[...] = lax.dot_general(cc * _sigmoid(cc), w_ref[...], NN, precision=lax.Precision.HIGHEST,
                                     preferred_element_type=F32) + b_ref[...]

    return pl.pallas_call(
        body, name="ada_fwd", grid=(2, n // tn),
        in_specs=[pl.BlockSpec((8, D), lambda l, j: (0, 0)), pl.BlockSpec((None, D, tn), lambda l, j: (l, 0, j)),
                  pl.BlockSpec((None, 1, tn), lambda l, j: (l, 0, j))],
        out_specs=pl.BlockSpec((None, 8, tn), lambda l, j: (l, 0, j)),
        out_shape=jax.ShapeDtypeStruct((2, 8, n), F32), compiler_params=_params(("parallel", "parallel")))(
            c_all, ada_w, ada_b)


def _ada_bwd(c_all, dmod_cols, dmod_all):
    n = dmod_cols.shape[2]
    tn = _col_tile(n, 512)

    def body(c_ref, d_ref, o_ref):
        cc = c_ref[...]
        o_ref[...] = lax.dot_general(cc * _sigmoid(cc), d_ref[...], TN, precision=lax.Precision.HIGHEST,
                                     preferred_element_type=F32)

    dw = pl.pallas_call(
        body, name="ada_dw", grid=(2, n // tn),
        in_specs=[pl.BlockSpec((8, D), lambda l, j: (0, 0)), pl.BlockSpec((None, 8, tn), lambda l, j: (l, 0, j))],
        out_specs=pl.BlockSpec((None, D, tn), lambda l, j: (l, 0, j)),
        out_shape=jax.ShapeDtypeStruct((2, D, n), F32), compiler_params=_params(("parallel", "parallel")))(
            c_all, dmod_cols)

    def sum_body(d_ref, o_ref):
        o_ref[...] = jnp.sum(d_ref[...], axis=0, keepdims=True)

    db = pl.pallas_call(
        sum_body, name="ada_db", grid=(2,),
        in_specs=[pl.BlockSpec((None, 8, 6 * D), lambda l: (l, 0, 0))],
        out_specs=pl.BlockSpec((None, 1, 6 * D), lambda l: (l, 0, 0)),
        out_shape=jax.ShapeDtypeStruct((2, 1, 6 * D), F32), compiler_params=_params(("parallel",)))(dmod_all)
    return dw, db


def _row_tile(rows, cap=512, mult=8):
    best = rows
    for t in range(mult, min(rows, cap) + 1, mult):
        if rows % t == 0:
            best = t
    return best


def _adamw(name, w, g, m, v):
    def fn(w, g, m, v):
        m = ADAM_B1 * m + (1.0 - ADAM_B1) * g
        v = ADAM_B2 * v + (1.0 - ADAM_B2) * (g * g)
        m_hat = m / (1.0 - ADAM_B1 ** ADAM_STEP)
        v_hat = v / (1.0 - ADAM_B2 ** ADAM_STEP)
        return -ADAM_LR * (m_hat / (jnp.sqrt(v_hat) + ADAM_EPS) + ADAM_WD * w), m, v
    cols = w.shape[1]
    return _rowwise(name, fn, [w, g, m, v], [], [(cols, F32)] * 3, tr=_row_tile(w.shape[0]))


def _place():
    return lax.axis_index("x"), lax.axis_index("y"), lax.axis_index("c")


VMEM_SPEC = pl.BlockSpec(memory_space=pltpu.VMEM)


def _allreduce_small(name, buf, after=None):
    rows = buf.shape[0]
    deps = [] if after is None else [after]

    def body(x_ref, *rest):
        o_ref, stage, send_sems, recv_sems = rest[len(deps):]
        x, y, c = _place()
        me = 4 * x + 2 * y + c
        stage[me] = x_ref[...]
        copies = []
        for k in range(1, 8):
            peer = (1 - x if k & 4 else x, 1 - y if k & 2 else y, 1 - c if k & 1 else c)
            cp = pltpu.make_async_remote_copy(src_ref=x_ref, dst_ref=stage.at[me], send_sem=send_sems.at[k - 1],
                                              recv_sem=recv_sems.at[k - 1], device_id=peer, device_id_type=MESH)
            cp.start()
            copies.append(cp)
        for cp in copies:
            cp.wait()
        acc = stage[0]
        for d in range(1, 8):
            acc = acc + stage[d]
        o_ref[...] = acc

    return pl.pallas_call(
        body, name=name, in_specs=[VMEM_SPEC] + [ANY for _ in deps], out_specs=VMEM_SPEC,
        out_shape=jax.ShapeDtypeStruct((rows, 128), F32),
        scratch_shapes=[pltpu.VMEM((8, rows, 128), F32), pltpu.SemaphoreType.DMA((7,)), pltpu.SemaphoreType.DMA((7,))],
        compiler_params=pltpu.CompilerParams(vmem_limit_bytes=_VMEM_LIMIT))(buf, *deps)


def _sum_slots(name, own, land):
    def body(own_ref, land_ref, o_ref):
        x, y, c = _place()
        me = 4 * x + 2 * y + c
        acc = None
        for d in range(8):
            v = jnp.where(me == d, own_ref[...], land_ref[d])
            acc = v if acc is None else acc + v
        o_ref[...] = acc

    return pl.pallas_call(body, name=name, in_specs=[VMEM_SPEC, VMEM_SPEC], out_specs=VMEM_SPEC,
                          out_shape=jax.ShapeDtypeStruct(own.shape, F32),
                          compiler_params=pltpu.CompilerParams(vmem_limit_bytes=_VMEM_LIMIT))(own, land)


OTHER_CHIPS = ((1, 0), (0, 1), (1, 1))


SIBLING_COLLECTIVE_ID = 6


def _sibling_handshake():
    x, y, c = _place()
    barrier = pltpu.get_barrier_semaphore()
    pl.semaphore_signal(barrier, inc=1, device_id=(x, y, 1 - c), device_id_type=MESH)
    pl.semaphore_wait(barrier, 1)


def _sibling_swap(name, src, halves):
    half = src.shape[-2] // 2
    out_shape = (src.shape[0], half, 1024) if halves else src.shape

    def body(s_ref, o_ref, send_sem, recv_sem):
        x, y, c = _place()
        _sibling_handshake()
        part = s_ref.at[:, pl.ds(pl.multiple_of((1 - c) * half, 8), half)] if halves else s_ref
        cp = pltpu.make_async_remote_copy(src_ref=part, dst_ref=o_ref, send_sem=send_sem, recv_sem=recv_sem,
                                          device_id=(x, y, 1 - c), device_id_type=MESH)
        cp.start()
        cp.wait()

    return pl.pallas_call(
        body, name=name, in_specs=[ANY], out_specs=ANY, out_shape=jax.ShapeDtypeStruct(out_shape, src.dtype),
        scratch_shapes=[pltpu.SemaphoreType.DMA, pltpu.SemaphoreType.DMA],
        compiler_params=pltpu.CompilerParams(collective_id=SIBLING_COLLECTIVE_ID))(src)


HBM = pl.BlockSpec(memory_space=pltpu.HBM)
SEM = pl.BlockSpec(memory_space=pltpu.SEMAPHORE)


def _exchange_peers(mode):
    x, y, c = _place()
    if mode == "all":
        return [(1 - x if k & 4 else x, 1 - y if k & 2 else y, 1 - c if k & 1 else c) for k in range(1, 8)]
    return [(1 - x if fx else x, 1 - y if fy else y, c) for fx, fy in OTHER_CHIPS]


def _chip_copies(mode, src_ref, land_ref, send_sems, recv_sems):
    x, y, c = _place()
    k = 2 * x + y
    copies = []
    for j, peer in enumerate(_exchange_peers(mode)):
        if mode == "gather":
            half = src_ref.shape[0] // 2
            mine = pl.ds(pl.multiple_of(c * half, 16), half)
            src, dst = src_ref.at[mine], land_ref.at[k, mine]
        elif mode == "scatter":
            src, dst = src_ref.at[2 * peer[0] + peer[1]], land_ref.at[k]
        else:
            src, dst = src_ref, land_ref.at[4 * x + 2 * y + c]
        copies.append(pltpu.make_async_remote_copy(src_ref=src, dst_ref=dst, send_sem=send_sems.at[j],
                                                   recv_sem=recv_sems.at[j], device_id=peer, device_id_type=MESH))
    return copies


def _exchange_start(name, collective_id, mode, src, land, after=None):
    deps = [] if after is None else [after]
    npeers = 7 if mode == "all" else 3

    def body(s_ref, l_ref, *rest):
        send_sems, recv_sems, s_thru, l_thru, token = rest[len(deps):]
        barrier = pltpu.get_barrier_semaphore()
        for peer in _exchange_peers(mode):
            pl.semaphore_signal(barrier, inc=1, device_id=peer, device_id_type=MESH)
        pl.semaphore_wait(barrier, npeers)
        for cp in _chip_copies(mode, s_ref, l_ref, send_sems, recv_sems):
            cp.start()
        token[...] = jnp.zeros_like(token)

    return pl.pallas_call(
        body, name=name,
        out_shape=(pltpu.SemaphoreType.DMA((npeers,)), pltpu.SemaphoreType.DMA((npeers,)),
                   pltpu.HBM(src.shape, src.dtype),
                   pltpu.HBM(land.shape, land.dtype), jax.ShapeDtypeStruct((8, 128), F32)),
        in_specs=(HBM, HBM) + tuple(ANY for _ in deps), out_specs=(SEM, SEM, HBM, HBM, VMEM_SPEC),
        input_output_aliases={0: 2, 1: 3},
        compiler_params=pltpu.CompilerParams(has_side_effects=pltpu.SideEffectType.DATAFLOW_SIDE_EFFECTING,
                                             collective_id=collective_id))(
            pltpu.with_memory_space_constraint(src, pltpu.HBM), pltpu.with_memory_space_constraint(land, pltpu.HBM),
            *deps)


def _exchange_wait(name, mode, started, after):
    send_sems, recv_sems, s_thru, l_thru, _ = started
    deps = list(after) if isinstance(after, (tuple, list)) else [after]

    def body(s_ref, l_ref, send_sems, recv_sems, *rest):
        for cp in _chip_copies(mode, s_ref, l_ref, send_sems, recv_sems):
            cp.wait_send()
            cp.wait_recv()

    return pl.pallas_call(
        body, name=name, out_shape=(pltpu.HBM(s_thru.shape, s_thru.dtype), pltpu.HBM(l_thru.shape, l_thru.dtype)),
        in_specs=(HBM, HBM, SEM, SEM) + tuple(ANY for _ in deps), out_specs=(HBM, HBM),
        input_output_aliases={0: 0, 1: 1},
        compiler_params=pltpu.CompilerParams(has_side_effects=pltpu.SideEffectType.DATAFLOW_SIDE_EFFECTING))(
            s_thru, l_thru, send_sems, recv_sems, *deps)


def _allgather_finish(tag, land):
    half = land.shape[1] // 2

    def body(l_ref, o_ref, send_sem, recv_sem):
        x, y, c = _place()
        _sibling_handshake()
        mine = pl.ds(pl.multiple_of(c * half, 16), half)
        swap = pltpu.make_async_remote_copy(src_ref=o_ref.at[:, mine], dst_ref=o_ref.at[:, mine], send_sem=send_sem,
                                            recv_sem=recv_sem, device_id=(x, y, 1 - c), device_id_type=MESH)
        swap.start()
        swap.wait()

    return pl.pallas_call(
        body, name="allgather_finish_" + tag, in_specs=[ANY], out_specs=ANY, input_output_aliases={0: 0},
        out_shape=jax.ShapeDtypeStruct(land.shape, land.dtype),
        scratch_shapes=[pltpu.SemaphoreType.DMA, pltpu.SemaphoreType.DMA],
        compiler_params=pltpu.CompilerParams(collective_id=SIBLING_COLLECTIVE_ID))(land)


def _pair_sum(tag, g, r1, c):
    rows = g.shape[1]
    half = rows // 2
    th = _row_tile(half, 256, 16)
    nblk = half // th

    def body(c_ref, g_ref, r_ref, o_ref, o2_ref):
        o_ref[...] = (g_ref[...] + r_ref[...]).astype(o_ref.dtype)
        o2_ref[...] = o_ref[...]

    spec = pl.BlockSpec((None, th, 1024), lambda k, i, c_ref: (k, i, 0))
    grid_spec = pltpu.PrefetchScalarGridSpec(
        num_scalar_prefetch=1, grid=(4, nblk),
        in_specs=[pl.BlockSpec((None, th, 1024), lambda k, i, c_ref: (k, c_ref[0] * nblk + i, 0)), spec],
        out_specs=[spec, spec])
    return pl.pallas_call(body, name="grad_pair_sum_" + tag, grid_spec=grid_spec,
                          out_shape=[jax.ShapeDtypeStruct((4, half, 1024), BF16)] * 2,
                          compiler_params=_params(("parallel", "parallel")))(c, g, r1)


def _chip_sum(tag, q, after=None):
    half = q.shape[1]
    th = _row_tile(half, 256, 16)
    deps = [] if after is None else [after]

    def body(a, b, c, d, *rest):
        rest[-1][...] = ((a[...].astype(F32) + b[...].astype(F32)) + c[...].astype(F32)) + d[...].astype(F32)

    specs = [pl.BlockSpec((None, th, 1024), functools.partial(lambda i, k: (k, i, 0), k=k)) for k in range(4)]
    return pl.pallas_call(body, name="grad_chip_sum_" + tag, grid=(half // th,), in_specs=specs + [ANY for _ in deps],
                          out_specs=pl.BlockSpec((th, 1024), lambda i: (i, 0)),
                          out_shape=jax.ShapeDtypeStruct((half, 1024), F32),
                          compiler_params=_params(("parallel",)))(q, q, q, q, *deps)


def _join_halves(tag, f, r, c):
    half = f.shape[0]
    th = _row_tile(half, 256)
    nblk = half // th

    def body(c_ref, f_ref, r_ref, o_ref):
        mine = (pl.program_id(0) == c_ref[0])
        o_ref[...] = jnp.where(mine, f_ref[...], r_ref[...])

    spec = pl.BlockSpec((th, 1024), lambda h, i, c_ref: (i, 0))
    grid_spec = pltpu.PrefetchScalarGridSpec(
        num_scalar_prefetch=1, grid=(2, nblk), in_specs=[spec, spec],
        out_specs=pl.BlockSpec((th, 1024), lambda h, i, c_ref: (h * nblk + i, 0)))
    return pl.pallas_call(body, name="grad_join_halves_" + tag, grid_spec=grid_spec,
                          out_shape=jax.ShapeDtypeStruct((2 * half, 1024), F32),
                          compiler_params=_params(("parallel", "parallel")))(c, f, r)


BIG_ARGS = ("in_w_even", "out_w_even", "qkv_w", "o_w", "ffn_gate_w", "ffn_up_w", "ffn_down_w")
def _ffn_pieces(layer):
    return tuple((f"{n}{layer}", 704, 704) for n in ("gate_wt", "up_wt", "down_w"))


IN_SLAB = (("in_wt", 1156, 1184),)
LAYER0_REST_SLAB = (("out_w", 512, 512),) + _ffn_pieces(0)
LAYER1_SLAB = (("qkv_wt", 320, 320), ("o_w", 256, 256)) + _ffn_pieces(1)
FFN0_SLAB = _ffn_pieces(0)
MIXER0_SLAB = (("in_wt", 1156, 1280), ("out_w", 512, 512))


def _slab(pieces, spec):
    parts = []
    for name, rows, room in spec:
        p = pieces[name]
        parts.append(jnp.pad(p, [(0, 0)] * (p.ndim - 2) + [(0, room - rows), (0, 0)]) if room > rows else p)
    return jnp.concatenate(parts, axis=-2) if len(parts) > 1 else parts[0]


def _unslab(slab, spec):
    out, off = {}, 0
    for name, rows, room in spec:
        out[name] = slab[..., off:off + rows, :]
        off += room
    return out


def _share_pieces(w):
    return {"in_wt": w["in_w_even"][0].T, "out_w": w["out_w_even"][0], "qkv_wt": w["qkv_w"][0].T, "o_w": w["o_w"][0],
            "gate_wt0": w["ffn_gate_w"][0].T, "gate_wt1": w["ffn_gate_w"][1].T,
            "up_wt0": w["ffn_up_w"][0].T, "up_wt1": w["ffn_up_w"][1].T,
            "down_w0": w["ffn_down_w"][0], "down_w1": w["ffn_down_w"][1]}


def _pieces_to_shares(p):
    return {"in_w_even": p["in_wt"].T[None], "out_w_even": p["out_w"][None], "qkv_w": p["qkv_wt"].T[None],
            "o_w": p["o_w"][None], "ffn_gate_w": jnp.stack([p["gate_wt0"].T, p["gate_wt1"].T]),
            "ffn_up_w": jnp.stack([p["up_wt0"].T, p["up_wt1"].T]),
            "ffn_down_w": jnp.stack([p["down_w0"], p["down_w1"]])}


def _chips_from_full(G, spec):
    return _slab({k: v.reshape(4, -1, D) for k, v in G.items()}, spec)


def _pack_small(parts):
    padded = []
    for p in parts:
        p = p.reshape(-1).astype(F32)
        padded.append(jnp.pad(p, (0, (-p.shape[0]) % 1024)))
    return jnp.concatenate(padded).reshape(-1, 128)


def _unpack_small(slab, shapes):
    flat, out, off = slab.reshape(-1), [], 0
    for shp in shapes:
        size = math.prod(shp)
        out.append(flat[off:off + size].reshape(shp))
        off += size + (-size) % 1024
    return out


SMALL = ("ada_b", "norm_mix_w", "norm_ffn_w", "conv_w", "conv_b", "dt_bias", "a_log", "d_skip", "ssm_norm_w",
         "gmlp_ln_w", "gmlp_ln_b", "gmlp_ws", "gmlp_bs", "qkv_b", "o_b", "sinks", "rel_table", "final_norm_w")
SMALL_SPLIT = {"conv_w": 1536, "qkv_b": 1280, "o_b": 1024}
WEIGHTS = ("ada_w", "ada_b", "norm_mix_w", "norm_ffn_w", "in_w_even", "conv_w", "conv_b", "dt_bias", "a_log", "d_skip",
           "ssm_norm_w", "gmlp_ln_w", "gmlp_ln_b", "gmlp_ws", "gmlp_bs", "out_w_even", "qkv_w", "qkv_b", "o_w", "o_b",
           "sinks", "rel_table", "ffn_gate_w", "ffn_up_w", "ffn_down_w", "final_norm_w")


def kernel(x, c, ada_w, ada_b, norm_mix_w, norm_ffn_w, in_w_even, conv_w, conv_b, dt_bias, a_log, d_skip, ssm_norm_w, gmlp_ln_w, gmlp_ln_b, gmlp_ws, gmlp_bs, out_w_even, qkv_w, qkv_b, o_w, o_b, sinks, rel_table, ffn_gate_w, ffn_up_w, ffn_down_w, final_norm_w, loss_target, m_ada_w, m_ada_b, m_norm_mix_w, m_norm_ffn_w, m_in_w_even, m_conv_w, m_conv_b, m_dt_bias, m_a_log, m_d_skip, m_ssm_norm_w, m_gmlp_ln_w, m_gmlp_ln_b, m_gmlp_ws, m_gmlp_bs, m_out_w_even, m_qkv_w, m_qkv_b, m_o_w, m_o_b, m_sinks, m_rel_table, m_ffn_gate_w, m_ffn_up_w, m_ffn_down_w, m_final_norm_w, v_ada_w, v_ada_b, v_norm_mix_w, v_norm_ffn_w, v_in_w_even, v_conv_w, v_conv_b, v_dt_bias, v_a_log, v_d_skip, v_ssm_norm_w, v_gmlp_ln_w, v_gmlp_ln_b, v_gmlp_ws, v_gmlp_bs, v_out_w_even, v_qkv_w, v_qkv_b, v_o_w, v_o_b, v_sinks, v_rel_table, v_ffn_gate_w, v_ffn_up_w, v_ffn_down_w, v_final_norm_w):
    args = dict(locals())
    w = {n: args[n] for n in WEIGHTS}
    m = {n: args["m_" + n] for n in WEIGHTS}
    v = {n: args["v_" + n] for n in WEIGHTS}
    ax, ay, ac = _place()
    me = 4 * ax + 2 * ay + ac
    chip = 2 * ax + ay
    south = (ac == 0).astype(F32)
    c_arr = jnp.reshape(ac, (1,)).astype(jnp.int32)

    c_all = _allreduce_small("gather_cond", lax.dynamic_update_slice(jnp.zeros((8, D), F32), c, (me, 0)).reshape(64, 128))
    c_all = c_all.reshape(8, D)
    n_ada = ada_w.shape[2]
    mod_cols = _ada_fwd(c_all, ada_w, lax.dynamic_slice(ada_b, (0, chip * n_ada), (2, n_ada)).reshape(2, 1, n_ada))
    pieces = [lax.dynamic_update_slice(jnp.zeros((2, 8, 6 * D), F32), mod_cols, (0, 0, chip * n_ada))]
    split_names = list(SMALL_SPLIT)
    for n in split_names:
        full = SMALL_SPLIT[n]
        local = w[n]
        idx = (0,) * (local.ndim - 1) + (chip * local.shape[-1],)
        pieces.append(lax.dynamic_update_slice(jnp.zeros(local.shape[:-1] + (full,), F32), local, idx))
    shapes = [p.shape for p in pieces]
    mod_own = _pack_small(pieces) * south
    mod_started = _exchange_start("gather_mod_start", 9, "all", mod_own, lax.empty((8,) + mod_own.shape, F32))

    pieces = _share_pieces(w)
    cast = {"in_wt": pieces["in_wt"].astype(_MXU)}

    def start_gather(tag, collective_id, share, after):
        return _exchange_start("allgather_start_" + tag, collective_id, "gather", share,
                               lax.empty((4,) + share.shape, share.dtype), after=after)

    def finish_gather(tag, started, spec, after):
        land = _exchange_wait("allgather_wait_" + tag, "gather", started, after)[1]
        out = {}
        for name, piece in _unslab(_allgather_finish(tag, land), spec).items():
            out[name] = lax.dynamic_update_slice(piece.reshape(-1, D), cast[name], (chip * piece.shape[1], 0))
        return out

    gather_in = start_gather("in", 7, _slab(cast, IN_SLAB), mod_started[4])
    zero = gather_in[4][0, 0]
    cast.update({k: (p + zero).astype(_MXU) for k, p in pieces.items() if k != "in_wt"})
    share0, share1 = _slab(cast, LAYER0_REST_SLAB), _slab(cast, LAYER1_SLAB)
    mod_own, mod_land = _exchange_wait("gather_mod_wait", "all", mod_started, (share0, share1))
    mod_slab = _sum_slots("gather_mod_sum", mod_own, mod_land)
    gathered = _unpack_small(mod_slab, shapes)
    mod = lax.dynamic_slice(gathered[0], (0, me, 0), (2, 1, 6 * D)).reshape(2, 6 * D)
    P = {n: w[n] for n in SMALL if n not in SMALL_SPLIT and n != "ada_b"}
    for n, full in zip(split_names, gathered[1:]):
        P[n] = full
    P["final_norm_w"] = final_norm_w.reshape(1, D)
    w_in = finish_gather("in", gather_in, IN_SLAB, mod_slab)["in_wt"]
    gather0 = start_gather("0", 1, share0, w_in)
    gather1 = start_gather("1", 2, share1, gather0[4])

    def start_reduce(tag, collective_id, G, spec, after=None):
        gp = _chips_from_full(G, spec)
        p, q = _pair_sum(tag, gp, _sibling_swap("grad_pair_exchange_" + tag, gp, True), c_arr)
        return _exchange_start("grad_exchange_start_" + tag, collective_id, "scatter", p, q, after=after)

    def finish_reduce(tag, started, spec, after, behind=None):
        q = _exchange_wait("grad_exchange_wait_" + tag, "scatter", started, after)[1]
        fin = _chip_sum(tag, q, after=behind)
        total = _join_halves(tag, fin, _sibling_swap("grad_final_exchange_" + tag, fin, False), c_arr)
        return _unslab(total, spec)

    reduces = {}

    def grads1(G1):
        reduces["1"] = start_reduce("1", 3, G1, LAYER1_SLAB)
        return reduces["1"][4]

    def grads_ffn0(G):
        reduces["f"] = start_reduce("f", 4, G, FFN0_SLAB)
        return reduces["f"][4]

    io = {"start": gather1[4],
          "weights0": lambda after: finish_gather("0", gather0, LAYER0_REST_SLAB, after),
          "weights1": lambda after: finish_gather("1", gather1, LAYER1_SLAB, after),
          "grads1": grads1, "grads_ffn0": grads_ffn0}
    sq, grad_x, dmod, G0, g = _local_step(x[0], loss_target[0], mod, w_in, P, io)
    loss = lax.psum(0.5 * sq[0, 0] / D, ("x", "y", "c"))

    g["final_norm_w"] = g["final_norm_w"].reshape(D)
    small_names = [n for n in SMALL if n != "ada_b"]
    pieces = [lax.dynamic_update_slice(jnp.zeros((2, 8, 6 * D), F32), dmod.reshape(2, 1, 6 * D), (0, me, 0))]
    pieces += [g[n] for n in small_names]
    shapes = [p.shape for p in pieces]
    small_own = _pack_small(pieces)
    small_started = _exchange_start("small_grads_start", 8, "all", small_own, lax.empty((8,) + small_own.shape, F32))
    reduces["m"] = start_reduce("m", 5, G0, MIXER0_SLAB, after=small_started[4])
    shares = finish_reduce("1", reduces["1"], LAYER1_SLAB, grad_x, behind=reduces["m"][4])
    shares.update(finish_reduce("f", reduces["f"], FFN0_SLAB, grad_x, behind=reduces["m"][4]))
    small_own, small_land = _exchange_wait("small_grads_wait", "all", small_started, shares["down_w0"])
    reduced = _unpack_small(_sum_slots("small_grads_sum", small_own, small_land), shapes)
    dmod_all = reduced[0]
    grads = dict(zip(small_names, reduced[1:]))
    for n in split_names:
        full = grads[n]
        size = w[n].shape[-1]
        grads[n] = lax.dynamic_slice(full, (0,) * (full.ndim - 1) + (chip * size,), full.shape[:-1] + (size,))
    grads = {n: grads[n].reshape(w[n].shape) for n in small_names}
    dw_ada, db_ada = _ada_bwd(c_all, lax.dynamic_slice(dmod_all, (0, 0, chip * n_ada), (2, 8, n_ada)), dmod_all)
    grads["ada_w"], grads["ada_b"] = dw_ada, db_ada.reshape(2, 6 * D)

    delta, new_m, new_v = {}, {}, {}

    def update(n):
        cols = w[n].shape[-1]
        d_, m_, v_ = _adamw("adamw_" + n, w[n].reshape(-1, cols), grads[n].reshape(-1, cols), m[n].reshape(-1, cols),
                            v[n].reshape(-1, cols))
        delta[n], new_m[n], new_v[n] = d_.reshape(w[n].shape), m_.reshape(w[n].shape), v_.reshape(w[n].shape)

    update("ada_w")
    shapes = [w[n].shape for n in SMALL]
    packed = [_pack_small([t[n] for n in SMALL]) for t in (w, grads, m, v)]
    outs = _adamw("adamw_small", *packed)
    for dst, slab in zip((delta, new_m, new_v), outs):
        for n, t in zip(SMALL, _unpack_small(slab, shapes)):
            dst[n] = t
    shares.update(finish_reduce("m", reduces["m"], MIXER0_SLAB, outs[0]))
    grads.update(_pieces_to_shares(shares))
    for n in BIG_ARGS:
        update(n)
    return (loss, grad_x[None], *[grads[n] for n in WEIGHTS], *[delta[n] for n in WEIGHTS],
            *[new_m[n] for n in WEIGHTS], *[new_v[n] for n in WEIGHTS])
```

```python
import functools
import math

import numpy as np
import jax
import jax.numpy as jnp
from jax import lax
from jax.experimental import pallas as pl
from jax.experimental.pallas import tpu as pltpu

F32 = jnp.float32
BF16 = jnp.bfloat16
_MXU = jnp.bfloat16
_VMEM_LIMIT = 56 * 1024 * 1024
MXU_COLS = 256
D = 1024
L = 128
NSTATE = 128
EPS = 1e-6
NEG_INF = -1e30
FFN = 2816
ADAM_LR, ADAM_B1, ADAM_B2, ADAM_EPS, ADAM_WD, ADAM_STEP = 0.001, 0.9, 0.999, 1e-08, 0.01, 10
MESH = pl.DeviceIdType.MESH
ANY = pl.BlockSpec(memory_space=pl.ANY)

NN = (((1,), (0,)), ((), ()))
NT = (((1,), (1,)), ((), ()))
TN = (((0,), (0,)), ((), ()))


def _dot(a, b, dn=NN):
    return lax.dot_general(a.astype(_MXU), b.astype(_MXU), dn, preferred_element_type=F32)


def _params(sem=None):
    return pltpu.CompilerParams(dimension_semantics=sem, vmem_limit_bytes=_VMEM_LIMIT)


def _sigmoid(x):
    return 1.0 / (1.0 + jnp.exp(-x))


def _softplus(x):
    return jnp.maximum(x, 0.0) + jnp.log(1.0 + jnp.exp(-jnp.abs(x)))


def _gelu(x):
    return 0.5 * x * (1.0 + lax.erf(x * (2.0 ** -0.5)))


def _gelu_grad(x):
    return 0.5 * (1.0 + lax.erf(x * (2.0 ** -0.5))) + x * jnp.exp(-0.5 * x * x) * (1.0 / math.sqrt(2.0 * math.pi))


def _silu_grad(a):
    sg = _sigmoid(a)
    return sg * (1.0 + a * (1.0 - sg))


def _rowwise(name, fn, rows, vecs, out_rows, out_accs=(), tr=512, after=None):
    S = rows[0].shape[0]
    tr = min(tr, S)
    assert S % tr == 0
    nr, nv, no, na = len(rows), len(vecs), len(out_rows), len(out_accs)
    deps = [] if after is None else [after]

    def body(*refs):
        ins, outs = refs[:nr + nv], refs[nr + nv + len(deps):]
        res = fn(*[r[...] for r in ins])
        if not isinstance(res, (tuple, list)):
            res = (res,)
        for k in range(no):
            outs[k][...] = res[k].astype(outs[k].dtype)
        if na:
            @pl.when(pl.program_id(0) == 0)
            def _():
                for k in range(na):
                    outs[no + k][...] = jnp.zeros_like(outs[no + k])
            for k in range(na):
                outs[no + k][...] += res[no + k]

    in_specs = [pl.BlockSpec((tr, a.shape[1]), lambda i: (i, 0)) for a in rows]
    in_specs += [pl.BlockSpec(v.shape, lambda i: (0, 0)) for v in vecs] + [ANY for _ in deps]
    out_specs = [pl.BlockSpec((tr, c), lambda i: (i, 0)) for c, _ in out_rows]
    out_specs += [pl.BlockSpec(s, lambda i: (0, 0)) for s in out_accs]
    out_shape = [jax.ShapeDtypeStruct((S, c), dt) for c, dt in out_rows]
    out_shape += [jax.ShapeDtypeStruct(s, F32) for s in out_accs]
    return pl.pallas_call(body, name=name, grid=(S // tr,), in_specs=in_specs, out_specs=out_specs,
                          out_shape=out_shape, compiler_params=_params(("arbitrary",)))(*rows, *vecs, *deps)


def _col_tile(n, cap):
    if n <= cap or n % 128:
        return n
    best = 128
    for t in range(128, cap + 1, 128):
        if n % t == 0:
            best = t
    return best


def _mm(name, As, Bs, mode, outs, epi=None, groups=None, extras=(), vecs=(), tm=512, tn_cap=1536, whole_rows=False):
    M = As[0].shape[0]
    N = Bs[0].shape[1] if mode == "nn" else Bs[0].shape[0]
    tm = min(tm, M)
    tn = _col_tile(N, tn_cap)
    assert M % tm == 0 and N % tn == 0
    npair = len(As)
    groups = groups or [0] * npair
    ng = max(groups) + 1
    nx, nv = len(extras), len(vecs)
    dn = NN if mode == "nn" else NT

    def body(*refs):
        a_refs, b_refs = refs[:npair], refs[npair:2 * npair]
        x_refs = refs[2 * npair:2 * npair + nx]
        v_refs = refs[2 * npair + nx:2 * npair + nx + nv]
        o_refs = refs[2 * npair + nx + nv:]
        step = tn if (epi is None or whole_rows) else min(tn, MXU_COLS)
        for col in range(0, tn, step):
            sl = slice(col, min(col + step, tn))
            accs = [None] * ng
            for k in range(npair):
                b = b_refs[k][:, sl] if mode == "nn" else b_refs[k][sl, :]
                d = _dot(a_refs[k][...], b, dn)
                accs[groups[k]] = d if accs[groups[k]] is None else accs[groups[k]] + d
            args = accs + [x[:, sl] for x in x_refs] + [v[:, sl] for v in v_refs]
            res = epi(*args) if epi is not None else tuple(accs)
            if not isinstance(res, (tuple, list)):
                res = (res,)
            for o, r in zip(o_refs, res):
                o[:, sl] = r.astype(o.dtype)

    in_specs = [pl.BlockSpec((tm, a.shape[1]), lambda i, j: (i, 0)) for a in As]
    if mode == "nn":
        in_specs += [pl.BlockSpec((b.shape[0], tn), lambda i, j: (0, j)) for b in Bs]
    else:
        in_specs += [pl.BlockSpec((tn, b.shape[1]), lambda i, j: (j, 0)) for b in Bs]
    in_specs += [pl.BlockSpec((tm, tn), lambda i, j: (i, j)) for _ in extras]
    in_specs += [pl.BlockSpec((1, tn), lambda i, j: (0, j)) for _ in vecs]
    out_specs = [pl.BlockSpec((tm, tn), lambda i, j: (i, j)) for _ in outs]
    out_shape = [jax.ShapeDtypeStruct((M, N), dt) for dt in outs]
    return pl.pallas_call(body, name=name, grid=(M // tm, N // tn), in_specs=in_specs, out_specs=out_specs,
                          out_shape=out_shape, compiler_params=_params(("parallel", "parallel")))(
                              *As, *Bs, *extras, *vecs)


def _mm_shared_lhs(name, A, Bs, tm=512):
    M, K = A.shape
    tm = min(tm, M)
    assert M % tm == 0
    n = len(Bs)

    def body(a_ref, *refs):
        a = a_ref[...]
        for b_ref, o_ref in zip(refs[:n], refs[n:]):
            o_ref[...] = _dot(a, b_ref[...], NT)

    return pl.pallas_call(
        body, name=name, grid=(M // tm,),
        in_specs=[pl.BlockSpec((tm, K), lambda i: (i, 0))] + [pl.BlockSpec(b.shape, lambda i: (0, 0)) for b in Bs],
        out_specs=[pl.BlockSpec((tm, b.shape[0]), lambda i: (i, 0)) for b in Bs],
        out_shape=[jax.ShapeDtypeStruct((M, b.shape[0]), F32) for b in Bs],
        compiler_params=_params(("parallel",)))(A, *Bs)


def _mm_tn_shared_rhs(name, As, B, tk=256):
    S, N = B.shape
    tk = min(tk, S)
    assert S % tk == 0
    n = len(As)

    def body(*refs):
        a_refs, b_ref, o_refs = refs[:n], refs[n], refs[n + 1:]

        @pl.when(pl.program_id(0) == 0)
        def _():
            for o_ref in o_refs:
                o_ref[...] = jnp.zeros_like(o_ref)
        b = b_ref[...]
        for a_ref, o_ref in zip(a_refs, o_refs):
            o_ref[...] += _dot(a_ref[...], b, TN)

    return pl.pallas_call(
        body, name=name, grid=(S // tk,),
        in_specs=[pl.BlockSpec((tk, a.shape[1]), lambda k: (k, 0)) for a in As] + [pl.BlockSpec((tk, N), lambda k: (k, 0))],
        out_specs=[pl.BlockSpec((a.shape[1], N), lambda k: (0, 0)) for a in As],
        out_shape=[jax.ShapeDtypeStruct((a.shape[1], N), F32) for a in As],
        compiler_params=_params(("arbitrary",)))(*As, B)


def _mm_tn(name, A, B, tk=512, t2_cap=1536):
    S, K1 = A.shape
    N2 = B.shape[1]
    tk = min(tk, S)
    t2 = _col_tile(N2, t2_cap)
    assert S % tk == 0 and N2 % t2 == 0

    def body(a_ref, b_ref, o_ref):
        @pl.when(pl.program_id(1) == 0)
        def _():
            o_ref[...] = jnp.zeros_like(o_ref)
        o_ref[...] += _dot(a_ref[...], b_ref[...], TN)

    return pl.pallas_call(
        body, name=name, grid=(N2 // t2, S // tk),
        in_specs=[pl.BlockSpec((tk, K1), lambda j, k: (k, 0)), pl.BlockSpec((tk, t2), lambda j, k: (k, j))],
        out_specs=pl.BlockSpec((K1, t2), lambda j, k: (0, j)),
        out_shape=jax.ShapeDtypeStruct((K1, N2), F32),
        compiler_params=_params(("parallel", "arbitrary")))(A, B)


def _norm_mod(x, nw, sc, sh):
    rstd = lax.rsqrt(jnp.mean(x * x, axis=-1, keepdims=True) + EPS)
    return (x * rstd * nw) * (1.0 + sc) + sh


def _norm_mod_fwd(name, x, nw, sc, sh, after=None):
    return _rowwise(name, _norm_mod, [x], [nw, sc, sh], [(D, BF16)], after=after)[0]


def _norm_mod_bwd(name, x, dh, dres, nw, sc, gate=None, after=None):
    def fn(x, dh, dres, *rest):
        nw, sc = rest[-3:-1] if gate else rest
        rstd = lax.rsqrt(jnp.mean(x * x, axis=-1, keepdims=True) + EPS)
        xh = x * rstd
        dn = dh * (1.0 + sc)
        dxh = dn * nw
        dx = dres + rstd * (dxh - xh * jnp.mean(dxh * xh, axis=-1, keepdims=True))
        sums = [jnp.sum(dh, axis=0, keepdims=True), jnp.sum(dh * (xh * nw), axis=0, keepdims=True),
                jnp.sum(dn * xh, axis=0, keepdims=True)]
        if not gate:
            return (dx, *sums)
        dy = dx * rest[-1]
        return (dx, dy, *sums, jnp.sum(dx * rest[0], axis=0, keepdims=True), jnp.sum(dy, axis=0, keepdims=True))
    if not gate:
        return _rowwise(name, fn, [x, dh, dres], [nw, sc], [(D, F32)], [(1, D)] * 3, after=after)
    return _rowwise(name, fn, [x, dh, dres, gate[0]], [nw, sc, gate[1]], [(D, F32), (D, BF16)], [(1, D)] * 5,
                    after=after)


def _loss_head(x, tgt, fw, y, g):
    def fn(x, tgt, y, fw, g):
        rstd = lax.rsqrt(jnp.mean(x * x, axis=-1, keepdims=True) + EPS)
        xh = x * rstd
        err = xh * fw - tgt
        dout = err * (1.0 / D)
        dxh = dout * fw
        dx = rstd * (dxh - xh * jnp.mean(dxh * xh, axis=-1, keepdims=True))
        sq = jnp.sum(jnp.sum(err * err, axis=1, keepdims=True), axis=0, keepdims=True)
        return (dx, dx * g, sq, jnp.sum(dout * xh, axis=0, keepdims=True), jnp.sum(dx * y, axis=0, keepdims=True))
    return _rowwise("loss_head", fn, [x, tgt, y], [fw, g], [(D, F32), (D, BF16)], [(1, 1), (1, D), (1, D)])


def _ffn_fwd(tag, h, wg, wu, wd, x, g2, next_norm=None):
    def act(a, b):
        return a, b, a * _sigmoid(a) * b
    a, b, f = _mm(f"ffn_up_{tag}", [h, h], [wg, wu], "nt", [BF16, BF16, BF16], epi=act, groups=[0, 1], tn_cap=1408,
                  tm=1024)

    if next_norm is None:
        def res(y, x, g):
            return y, x + g * y
        y, xo = _mm(f"ffn_down_{tag}", [f], [wd], "nn", [F32, F32], epi=res, extras=[x], vecs=[g2])
        return a, b, f, y, xo, None

    def res_norm(y, x, g, nw, sc, sh):
        xo = x + g * y
        return y, xo, _norm_mod(xo, nw, sc, sh)
    assert wd.shape[1] == D
    y, xo, h_next = _mm(f"ffn_down_{tag}", [f], [wd], "nn", [F32, F32, BF16], epi=res_norm, extras=[x],
                        vecs=[g2, *next_norm], whole_rows=True)
    return a, b, f, y, xo, h_next


def _ffn_bwd(tag, dy, h, a, b, f, wg, wu, wd):
    def act_bwd(df, a, b):
        a, b = a.astype(F32), b.astype(F32)
        sg = _sigmoid(a)
        return df * b * (sg * (1.0 + a * (1.0 - sg))), df * (a * sg)
    da, db = _mm(f"ffn_dact_{tag}", [dy], [wd], "nt", [BF16, BF16], epi=act_bwd, extras=[a, b], tn_cap=1408, tm=1024)
    dwd = _mm_tn(f"ffn_dwd_{tag}", f, dy)
    dwg = _mm_tn(f"ffn_dwg_{tag}", da, h)
    dwu = _mm_tn(f"ffn_dwu_{tag}", db, h)
    dh = _mm(f"ffn_dh_{tag}", [da, db], [wg, wu], "nn", [F32])[0]
    return dh, dwg, dwu, dwd


def _conv_fwd(xr, w, b, tb=512):
    S, C = xr.shape
    tb = min(tb, S)

    def body(x_ref, halo_ref, w_ref, b_ref, pre_ref, out_ref):
        i = pl.program_id(0)
        halo = jnp.where(i > 0, halo_ref[...], 0.0)
        xe = jnp.concatenate([halo, x_ref[...]], axis=0)
        pre = w_ref[3:4, :] * x_ref[...] + b_ref[...]
        for j in (1, 2, 3):
            pre = pre + w_ref[3 - j:4 - j, :] * pltpu.roll(xe, j, axis=0)[8:, :]
        pre_ref[...] = pre
        out_ref[...] = pre * _sigmoid(pre)

    return pl.pallas_call(
        body, name="conv_fwd", grid=(S // tb,),
        in_specs=[pl.BlockSpec((tb, C), lambda i: (i, 0)),
                  pl.BlockSpec((8, C), lambda i: (jnp.maximum(i * (tb // 8) - 1, 0), 0)),
                  pl.BlockSpec((4, C), lambda i: (0, 0)), pl.BlockSpec((1, C), lambda i: (0, 0))],
        out_specs=[pl.BlockSpec((tb, C), lambda i: (i, 0))] * 2,
        out_shape=[jax.ShapeDtypeStruct((S, C), F32)] * 2,
        compiler_params=_params(("parallel",)))(xr, xr, w, b)


def _conv_bwd(dxc, pre, xr, w, tb=512):
    S, C = xr.shape
    tb = min(tb, S)
    nblk = S // tb

    def body(d_ref, p_ref, dn_ref, pn_ref, x_ref, w_ref, dx_ref, dw_ref, db_ref):
        i = pl.program_id(0)

        @pl.when(i == 0)
        def _():
            dw_ref[...] = jnp.zeros_like(dw_ref)
            db_ref[...] = jnp.zeros_like(db_ref)

        dpre = d_ref[...] * _silu_grad(p_ref[...])
        dnext = jnp.where(i < nblk - 1, dn_ref[...] * _silu_grad(pn_ref[...]), 0.0)
        pe = jnp.concatenate([dpre, dnext], axis=0)
        xx = x_ref[...]
        dx = w_ref[3:4, :] * dpre
        dw_ref[3:4, :] += jnp.sum(dpre * xx, axis=0, keepdims=True)
        for j in (1, 2, 3):
            ahead = pltpu.roll(pe, tb + 8 - j, axis=0)[:tb, :]
            dx = dx + w_ref[3 - j:4 - j, :] * ahead
            dw_ref[3 - j:4 - j, :] += jnp.sum(ahead * xx, axis=0, keepdims=True)
        dx_ref[...] = dx.astype(dx_ref.dtype)
        db_ref[...] += jnp.sum(dpre, axis=0, keepdims=True)

    blk = pl.BlockSpec((tb, C), lambda i: (i, 0))
    nxt = pl.BlockSpec((8, C), lambda i: (jnp.minimum((i + 1) * (tb // 8), S // 8 - 1), 0))
    return pl.pallas_call(
        body, name="conv_bwd", grid=(nblk,),
        in_specs=[blk, blk, nxt, nxt, blk, pl.BlockSpec((4, C), lambda i: (0, 0))],
        out_specs=[blk, pl.BlockSpec((4, C), lambda i: (0, 0)), pl.BlockSpec((1, C), lambda i: (0, 0))],
        out_shape=[jax.ShapeDtypeStruct((S, C), BF16), jax.ShapeDtypeStruct((4, C), F32),
                   jax.ShapeDtypeStruct((1, C), F32)],
        compiler_params=_params(("arbitrary",)))(dxc, pre, dxc, pre, xr, w)


def _iota(shape, dim):
    return lax.broadcasted_iota(jnp.int32, shape, dim)


def _colsel(m, lane, h):
    return jnp.sum(jnp.where(lane == h, m, 0.0), axis=1, keepdims=True)


def _cumsum_rows(v):
    r = _iota(v.shape, 0)
    k = 1
    while k < v.shape[0]:
        v = v + jnp.where(r >= k, pltpu.roll(v, k, axis=0), 0.0)
        k *= 2
    return v


def _suffix_sum_rows(v):
    n = v.shape[0]
    r = _iota(v.shape, 0)
    k = 1
    while k < n:
        v = v + jnp.where(r < n - k, pltpu.roll(v, n - k, axis=0), 0.0)
        k *= 2
    return v


def _ssd_fwd(xc, dtr, z, dtb, alog, dskl, nw):
    S = xc.shape[0]
    nc = S // L

    def body(xc_ref, dtr_ref, z_ref, dtb_ref, alog_ref, dsk_ref, nw_ref, ya_ref, y_ref, prev_ref,
             st_ref, cum_ref, cumT_ref):
        i = pl.program_id(0)

        @pl.when(i == 0)
        def _():
            st_ref[...] = jnp.zeros_like(st_ref)

        lane = _iota((L, 128), 1)
        lane1 = _iota((1, 128), 1)
        lo = lane < 64
        lo1 = lane1 < 64
        tril = _iota((L, L), 0) >= _iota((L, L), 1)
        dt = _softplus(dtr_ref[...] + dtb_ref[...])
        a_neg = -jnp.exp(alog_ref[...])
        cum = _cumsum_rows(dt * a_neg)
        cum_ref[...] = cum
        cumT_ref[...] = cum.T
        last_all = cum_ref[L - 1:L, :]
        prev_t = st_ref[...]
        prev_ref[0] = prev_t
        for g in range(2):
            bg = xc_ref[:, 1024 + g * 128:1152 + g * 128]
            cg = xc_ref[:, 1280 + g * 128:1408 + g * 128]
            gmat = _dot(cg, bg, NT)
            yoff = _dot(cg, prev_t[:, g * 512:(g + 1) * 512])
            bg_t = bg.T
            for jp in range(4):
                j = g * 4 + jp
                sl = slice(j * 128, (j + 1) * 128)
                xp = xc_ref[:, sl]
                cc = [_colsel(cum, lane, 2 * j), _colsel(cum, lane, 2 * j + 1)]
                cum_l = jnp.where(lo, cc[0], cc[1])
                dt_l = jnp.where(lo, _colsel(dt, lane, 2 * j), _colsel(dt, lane, 2 * j + 1))
                last_l = jnp.where(lo1, _colsel(last_all, lane1, 2 * j), _colsel(last_all, lane1, 2 * j + 1))
                xd = xp * dt_l
                ys = []
                for hh in range(2):
                    seg = cc[hh] - cumT_ref[2 * j + hh:2 * j + hh + 1, :]
                    dm = jnp.where(tril, jnp.exp(seg), 0.0)
                    ys.append(_dot(gmat * dm, xd))
                y_ref[:, sl] = (jnp.where(lo, ys[0], ys[1]) + jnp.exp(cum_l) * yoff[:, jp * 128:(jp + 1) * 128]
                                + dsk_ref[:, sl] * xp)
                st_ref[:, sl] = prev_t[:, sl] * jnp.exp(last_l) + _dot(bg_t, xd * jnp.exp(last_l - cum_l))
        for g in range(2):
            sl = slice(g * 512, (g + 1) * 512)
            zz = z_ref[:, sl]
            yg = y_ref[:, sl] * (zz * _sigmoid(zz))
            rstd = lax.rsqrt(jnp.mean(yg * yg, axis=-1, keepdims=True) + EPS)
            ya_ref[:, sl] = (yg * rstd * nw_ref[:, sl]).astype(ya_ref.dtype)

    blk = lambda c: pl.BlockSpec((L, c), lambda i: (i, 0))
    vec = lambda c: pl.BlockSpec((1, c), lambda i: (0, 0))
    return pl.pallas_call(
        body, name="ssd_fwd", grid=(nc,),
        in_specs=[blk(1536), blk(128), blk(1024), vec(128), vec(128), vec(1024), vec(1024)],
        out_specs=[blk(1024), blk(1024), pl.BlockSpec((1, NSTATE, 1024), lambda i: (i, 0, 0))],
        out_shape=[jax.ShapeDtypeStruct((S, 1024), BF16), jax.ShapeDtypeStruct((S, 1024), F32),
                   jax.ShapeDtypeStruct((nc, NSTATE, 1024), F32)],
        scratch_shapes=[pltpu.VMEM((NSTATE, 1024), F32), pltpu.VMEM((L, 128), F32), pltpu.VMEM((L, 128), F32)],
        compiler_params=_params(("arbitrary",)))(xc, dtr, z, dtb, alog, dskl, nw)


def _ssd_bwd(dya, y, z, xc, dtr, prev, dtb, alog, dskl, nw):
    S = xc.shape[0]
    nc = S // L

    def body(dya_ref, y_ref, z_ref, xc_ref, dtr_ref, prev_ref, dtb_ref, alog_ref, dsk_ref, nw_ref,
             dz_ref, dxc_ref, ddtr_ref, dnw_ref, ddsk_ref, dalog_ref, ddtb_ref,
             dst_ref, cum_ref, cumT_ref, dy_ref, dskacc_ref):
        i = pl.program_id(0)

        @pl.when(i == 0)
        def _():
            dst_ref[...] = jnp.zeros_like(dst_ref)
            dskacc_ref[...] = jnp.zeros_like(dskacc_ref)
            dnw_ref[...] = jnp.zeros_like(dnw_ref)
            dalog_ref[...] = jnp.zeros_like(dalog_ref)
            ddtb_ref[...] = jnp.zeros_like(ddtb_ref)

        lane = _iota((L, 128), 1)
        lane1 = _iota((1, 128), 1)
        lo = lane < 64
        lo1 = lane1 < 64
        r2, c2 = _iota((L, L), 0), _iota((L, L), 1)
        tril = r2 >= c2
        triu = r2 <= c2
        is_last = _iota((L, 1), 0) == L - 1

        for g in range(2):
            sl = slice(g * 512, (g + 1) * 512)
            zz = z_ref[:, sl]
            sg = _sigmoid(zz)
            zg = zz * sg
            yv = y_ref[:, sl]
            yg = yv * zg
            rstd = lax.rsqrt(jnp.mean(yg * yg, axis=-1, keepdims=True) + EPS)
            xh = yg * rstd
            d_out = dya_ref[:, sl]
            dnw_ref[:, sl] += jnp.sum(d_out * xh, axis=0, keepdims=True)
            dyn = d_out * nw_ref[:, sl]
            dyg = rstd * (dyn - xh * jnp.mean(dyn * xh, axis=-1, keepdims=True))
            dy_ref[:, sl] = dyg * zg
            dz_ref[:, sl] = (dyg * yv * (sg * (1.0 + zz * (1.0 - sg)))).astype(dz_ref.dtype)

        dtin = dtr_ref[...] + dtb_ref[...]
        dt = _softplus(dtin)
        a_neg = -jnp.exp(alog_ref[...])
        cum = _cumsum_rows(dt * a_neg)
        cum_ref[...] = cum
        cumT_ref[...] = cum.T
        last_all = cum_ref[L - 1:L, :]
        prev_t = prev_ref[0]
        dn_t = dst_ref[...]
        dcum = jnp.zeros((L, 128), F32)
        ddt = jnp.zeros((L, 128), F32)
        for g in range(2):
            gsl = slice(g * 512, (g + 1) * 512)
            bg = xc_ref[:, 1024 + g * 128:1152 + g * 128]
            cg = xc_ref[:, 1280 + g * 128:1408 + g * 128]
            gmat = _dot(cg, bg, NT)
            gmat_t = _dot(bg, cg, NT)
            pg = prev_t[:, gsl]
            zmat = _dot(cg, pg)
            dgm = jnp.zeros((L, L), F32)
            dgm_t = jnp.zeros((L, L), F32)
            db_acc = jnp.zeros((L, NSTATE), F32)
            dz_parts, cd_parts = [], []
            for jp in range(4):
                j = g * 4 + jp
                sl = slice(j * 128, (j + 1) * 128)
                xp = xc_ref[:, sl]
                dyp = dy_ref[:, sl]
                cc = [_colsel(cum, lane, 2 * j), _colsel(cum, lane, 2 * j + 1)]
                lc = [_colsel(last_all, lane1, 2 * j), _colsel(last_all, lane1, 2 * j + 1)]
                cum_l = jnp.where(lo, cc[0], cc[1])
                dt_l = jnp.where(lo, _colsel(dt, lane, 2 * j), _colsel(dt, lane, 2 * j + 1))
                last_l = jnp.where(lo1, lc[0], lc[1])
                e_l = jnp.exp(cum_l)
                dte_l = jnp.exp(last_l - cum_l)
                cd_l = jnp.exp(last_l)
                cd_parts.append(cd_l)
                xd = xp * dt_l
                dskacc_ref[:, sl] += jnp.sum(dyp * xp, axis=0, keepdims=True)
                dxp = dsk_ref[:, sl] * dyp
                t = dyp * (e_l * zmat[:, jp * 128:(jp + 1) * 128])
                dcc = [jnp.sum(jnp.where(lo, t, 0.0), axis=1, keepdims=True),
                       jnp.sum(jnp.where(lo, 0.0, t), axis=1, keepdims=True)]
                dz_parts.append(e_l * dyp)
                dnp_ = dn_t[:, sl]
                t2 = jnp.sum(dnp_ * prev_t[:, sl], axis=0, keepdims=True)
                dcd = [jnp.sum(jnp.where(lo1, t2, 0.0), axis=1, keepdims=True),
                       jnp.sum(jnp.where(lo1, 0.0, t2), axis=1, keepdims=True)]
                wm = _dot(bg, dnp_)
                dxd = wm * dte_l
                t3 = wm * xd
                ddte = [jnp.sum(jnp.where(lo, t3, 0.0), axis=1, keepdims=True),
                        jnp.sum(jnp.where(lo, 0.0, t3), axis=1, keepdims=True)]
                db_acc = db_acc + _dot(xd * dte_l, dnp_, NT)
                for hh in range(2):
                    h = 2 * j + hh
                    half = lo if hh == 0 else jnp.logical_not(lo)
                    row = cumT_ref[h:h + 1, :]
                    dm = jnp.where(tril, jnp.exp(cc[hh] - row), 0.0)
                    dm_t = jnp.where(triu, jnp.exp(row - cc[hh]), 0.0)
                    dym = jnp.where(half, dyp, 0.0)
                    u = _dot(dym, xd, NT) * dm
                    u_t = _dot(xd, dym, NT) * dm_t
                    dxd = dxd + _dot(gmat_t * dm_t, dym)
                    dcc[hh] = dcc[hh] + jnp.sum(u * gmat, axis=1, keepdims=True) - jnp.sum(u_t * gmat_t, axis=1, keepdims=True)
                    dgm = dgm + u
                    dgm_t = dgm_t + u_t
                    dte_c = jnp.exp(lc[hh] - cc[hh])
                    dcc[hh] = dcc[hh] - ddte[hh] * dte_c
                    endc = dcd[hh] * jnp.exp(lc[hh]) + jnp.sum(ddte[hh] * dte_c, axis=0, keepdims=True)
                    dcc[hh] = dcc[hh] + jnp.where(is_last, endc, 0.0)
                    dcum = jnp.where(lane == h, dcc[hh], dcum)
                dxc_ref[:, sl] = dxp + dxd * dt_l
                t4 = dxd * xp
                ddt = jnp.where(lane == 2 * j, jnp.sum(jnp.where(lo, t4, 0.0), axis=1, keepdims=True), ddt)
                ddt = jnp.where(lane == 2 * j + 1, jnp.sum(jnp.where(lo, 0.0, t4), axis=1, keepdims=True), ddt)
            dzg = jnp.concatenate(dz_parts, axis=1)
            dst_ref[:, gsl] = dn_t[:, gsl] * jnp.concatenate(cd_parts, axis=1) + _dot(cg.T, dzg)
            dxc_ref[:, 1280 + g * 128:1408 + g * 128] = _dot(dgm, bg) + _dot(dzg, pg, NT)
            dxc_ref[:, 1024 + g * 128:1152 + g * 128] = _dot(dgm_t, cg) + db_acc
        dla = _suffix_sum_rows(dcum)
        ddt = ddt + dla * a_neg
        dalog_ref[...] += jnp.sum(dla * dt, axis=0, keepdims=True) * a_neg
        ddtr = jnp.where(lane < 16, ddt * _sigmoid(dtin), 0.0)
        ddtr_ref[...] = ddtr.astype(ddtr_ref.dtype)
        ddtb_ref[...] += jnp.sum(ddtr, axis=0, keepdims=True)

        @pl.when(i == nc - 1)
        def _():
            seg = (_iota((1024, 128), 0) // 64 == _iota((1024, 128), 1)).astype(F32)
            acc8 = jnp.broadcast_to(dskacc_ref[...], (8, 1024))
            ddsk_ref[...] = lax.dot_general(acc8, seg, NN, precision=lax.Precision.HIGHEST,
                                            preferred_element_type=F32)

    rev = lambda c: pl.BlockSpec((L, c), lambda i: (nc - 1 - i, 0))
    vec = lambda c: pl.BlockSpec((1, c), lambda i: (0, 0))
    return pl.pallas_call(
        body, name="ssd_bwd", grid=(nc,),
        in_specs=[rev(1024), rev(1024), rev(1024), rev(1536), rev(128),
                  pl.BlockSpec((1, NSTATE, 1024), lambda i: (nc - 1 - i, 0, 0)),
                  vec(128), vec(128), vec(1024), vec(1024)],
        out_specs=[rev(1024), rev(1536), rev(128), vec(1024), pl.BlockSpec((8, 128), lambda i: (0, 0)),
                   vec(128), vec(128)],
        out_shape=[jax.ShapeDtypeStruct((S, 1024), BF16), jax.ShapeDtypeStruct((S, 1536), F32),
                   jax.ShapeDtypeStruct((S, 128), BF16), jax.ShapeDtypeStruct((1, 1024), F32),
                   jax.ShapeDtypeStruct((8, 128), F32), jax.ShapeDtypeStruct((1, 128), F32),
                   jax.ShapeDtypeStruct((1, 128), F32)],
        scratch_shapes=[pltpu.VMEM((NSTATE, 1024), F32), pltpu.VMEM((L, 128), F32), pltpu.VMEM((L, 128), F32),
                        pltpu.VMEM((L, 1024), F32), pltpu.VMEM((1, 1024), F32)],
        compiler_params=_params(("arbitrary",)))(dya, y, z, xc, dtr, prev, dtb, alog, dskl, nw)


def _layer_norm_parts(vg):
    mu = jnp.mean(vg, axis=-1, keepdims=True)
    vc = vg - mu
    rstd = lax.rsqrt(jnp.mean(vc * vc, axis=-1, keepdims=True) + EPS)
    return vc * rstd, rstd


def _gmlp_fwd(u, v, lnw, lnb, ws, bse, tb=512):
    S = u.shape[0]
    tb = min(tb, S)

    def body(u_ref, v_ref, lnw_ref, lnb_ref, ws_ref, bse_ref, o_ref, vn_ref):
        tril = _iota((L, L), 0) >= _iota((L, L), 1)
        xh, _ = _layer_norm_parts(_gelu(v_ref[...]))
        vn_ref[...] = xh * lnw_ref[...] + lnb_ref[...]
        for g in range(8):
            w = jnp.where(tril, ws_ref[g], 0.0)
            gs = slice(g * 128, (g + 1) * 128)
            for ch in range(tb // L):
                rs = slice(ch * L, (ch + 1) * L)
                sv = _dot(w, vn_ref[rs, gs]) + bse_ref[g]
                o_ref[rs, gs] = (_gelu(u_ref[rs, gs]) * sv).astype(o_ref.dtype)

    blk = pl.BlockSpec((tb, 1024), lambda i: (i, 0))
    vec = pl.BlockSpec((1, 1024), lambda i: (0, 0))
    cube = pl.BlockSpec((8, L, 128), lambda i: (0, 0, 0))
    return pl.pallas_call(
        body, name="gmlp_fwd", grid=(S // tb,), in_specs=[blk, blk, vec, vec, cube, cube], out_specs=blk,
        out_shape=jax.ShapeDtypeStruct((S, 1024), BF16), scratch_shapes=[pltpu.VMEM((tb, 1024), F32)],
        compiler_params=_params(("parallel",)))(u, v, lnw, lnb, ws, bse)


def _gmlp_bwd(dyb, u, v, lnw, lnb, ws, bse, tb=512):
    S = u.shape[0]
    tb = min(tb, S)

    def body(d_ref, u_ref, v_ref, lnw_ref, lnb_ref, ws_ref, bse_ref,
             du_ref, dv_ref, dws_ref, dbse_ref, dlnw_ref, dlnb_ref, vn_ref, dvn_ref):
        @pl.when(pl.program_id(0) == 0)
        def _():
            dws_ref[...] = jnp.zeros_like(dws_ref)
            dbse_ref[...] = jnp.zeros_like(dbse_ref)
            dlnw_ref[...] = jnp.zeros_like(dlnw_ref)
            dlnb_ref[...] = jnp.zeros_like(dlnb_ref)

        tril = _iota((L, L), 0) >= _iota((L, L), 1)
        vv = v_ref[...]
        xh, rstd = _layer_norm_parts(_gelu(vv))
        vn_ref[...] = xh * lnw_ref[...] + lnb_ref[...]
        for g in range(8):
            w = jnp.where(tril, ws_ref[g], 0.0)
            w_t = w.T
            gs = slice(g * 128, (g + 1) * 128)
            dw = jnp.zeros((L, L), F32)
            dbs = jnp.zeros((L, 128), F32)
            for ch in range(tb // L):
                rs = slice(ch * L, (ch + 1) * L)
                vn = vn_ref[rs, gs]
                sv = _dot(w, vn) + bse_ref[g]
                uu = u_ref[rs, gs]
                dd = d_ref[rs, gs]
                du_ref[rs, gs] = (dd * sv * _gelu_grad(uu)).astype(du_ref.dtype)
                dsv = dd * _gelu(uu)
                dw = dw + _dot(dsv, vn, NT)
                dbs = dbs + dsv
                dvn_ref[rs, gs] = _dot(w_t, dsv)
            dws_ref[g] += jnp.where(tril, dw, 0.0)
            dbse_ref[g] += dbs
        dvn = dvn_ref[...]
        dlnw_ref[...] += jnp.sum(dvn * xh, axis=0, keepdims=True)
        dlnb_ref[...] += jnp.sum(dvn, axis=0, keepdims=True)
        dxh = dvn * lnw_ref[...]
        dvg = rstd * (dxh - jnp.mean(dxh, axis=-1, keepdims=True) - xh * jnp.mean(dxh * xh, axis=-1, keepdims=True))
        dv_ref[...] = (dvg * _gelu_grad(vv)).astype(dv_ref.dtype)

    blk = pl.BlockSpec((tb, 1024), lambda i: (i, 0))
    vec = pl.BlockSpec((1, 1024), lambda i: (0, 0))
    cube = pl.BlockSpec((8, L, 128), lambda i: (0, 0, 0))
    return pl.pallas_call(
        body, name="gmlp_bwd", grid=(S // tb,), in_specs=[blk, blk, blk, vec, vec, cube, cube],
        out_specs=[blk, blk, cube, cube, vec, vec],
        out_shape=[jax.ShapeDtypeStruct((S, 1024), BF16), jax.ShapeDtypeStruct((S, 1024), BF16),
                   jax.ShapeDtypeStruct((8, L, 128), F32), jax.ShapeDtypeStruct((8, L, 128), F32),
                   jax.ShapeDtypeStruct((1, 1024), F32), jax.ShapeDtypeStruct((1, 1024), F32)],
        scratch_shapes=[pltpu.VMEM((tb, 1024), F32), pltpu.VMEM((tb, 1024), F32)],
        compiler_params=_params(("arbitrary",)))(dyb, u, v, lnw, lnb, ws, bse)


def _lane_sum(name, a):
    def body(a_ref, o_ref):
        o_ref[...] = jnp.sum(a_ref[...], axis=1, keepdims=True)
    return pl.pallas_call(body, name=name, out_shape=jax.ShapeDtypeStruct((a.shape[0], 1), F32))(a)


def _bucket_onehot_t():
    qi = np.arange(L)[:, None]
    sj = np.arange(2 * L)[None, :]
    dist = np.maximum(qi + L - sj, 0)
    log_ratio = (np.log(np.maximum(dist, 1).astype(np.float32) / np.float32(16)) / np.float32(math.log(128 / 16)))
    large = 16 + (log_ratio.astype(np.float32) * np.float32(16)).astype(np.int32)
    bucket = np.where(dist < 16, dist, np.minimum(large, 31)).reshape(-1)
    return (np.arange(32)[:, None] == bucket[None, :]).astype(np.float32)


def _rel_bias(table_t, onehot_t):
    def body(t_ref, oh_ref, o_ref):
        o_ref[...] = lax.dot_general(t_ref[...], oh_ref[...], NN, precision=lax.Precision.HIGHEST,
                                     preferred_element_type=F32)
    return pl.pallas_call(body, name="rel_bias", out_shape=jax.ShapeDtypeStruct((16, L * 2 * L), F32),
                          compiler_params=_params())(table_t, onehot_t)


def _rel_bias_bwd(dbias, onehot_t):
    def body(d_ref, oh_ref, o_ref):
        o_ref[...] = lax.dot_general(d_ref[...], oh_ref[...], NT, precision=lax.Precision.HIGHEST,
                                     preferred_element_type=F32)
    return pl.pallas_call(body, name="rel_bias_bwd", out_shape=jax.ShapeDtypeStruct((16, 32), F32),
                          compiler_params=_params())(dbias, onehot_t)


def _band(kp, kc, lo):
    kk = jnp.concatenate([kp, kc], axis=0)
    kr = pltpu.roll(kk, 64, axis=1)
    return [jnp.where(lo, kk, kr), jnp.where(lo, kr, kk)]


def _attn_rows(ref, j, lo):
    parts = []
    for t in range(8):
        pair = ref[:, (4 * j + t // 2) * 128:(4 * j + t // 2 + 1) * 128]
        parts.append(jnp.where(lo if t % 2 == 0 else jnp.logical_not(lo), pair, 0.0))
    return jnp.concatenate(parts, axis=0)


def _attn_mask(i, rows):
    qi, sj = _iota((rows, 2 * L), 0) & (L - 1), _iota((rows, 2 * L), 1)
    rel = qi + L - sj
    return (rel >= 0) & (rel < L) & ((sj >= L) | (i > 0))


def _per_head_col(vals):
    return jnp.concatenate([jnp.broadcast_to(v, (L, 1)) for v in vals], axis=0)


SMEM = pl.BlockSpec(memory_space=pltpu.SMEM)


def _attn_fwd(qkv, bias, sinks):
    S = qkv.shape[0]
    nb = S // L
    scale = 64 ** -0.5

    def body(sink_ref, q_ref, kc_ref, vc_ref, kp_ref, vp_ref, bias_ref, o_ref, lse_ref):
        i = pl.program_id(0)
        lane = _iota((L, 128), 1)
        lo = lane < 64
        lo2 = _iota((2 * L, 128), 1) < 64
        mask = _attn_mask(i, L)
        kd = _band(kp_ref[...], kc_ref[...], lo2)
        vd = _band(vp_ref[...], vc_ref[...], lo2)
        lse = jnp.zeros((L, 128), F32)
        for pr in range(8):
            sl = slice(pr * 128, (pr + 1) * 128)
            qp = q_ref[:, sl]
            j = pr // 4
            outs = []
            for hh in range(2):
                h = 2 * pr + hh
                qm = jnp.where(lo if hh == 0 else jnp.logical_not(lo), qp, 0.0)
                lg = jnp.where(mask, _dot(qm, kd[j], NT) * scale + bias_ref[h], NEG_INF)
                s = sink_ref[h]
                m = jnp.maximum(jnp.max(lg, axis=1, keepdims=True), s)
                p = jnp.where(mask, jnp.exp(lg - m), 0.0)
                den = jnp.sum(p, axis=1, keepdims=True) + jnp.exp(s - m)
                outs.append(_dot(p / den, vd[j]))
                lse = jnp.where(lane == h, m + jnp.log(den), lse)
            o_ref[:, sl] = jnp.where(lo, outs[0], outs[1]).astype(o_ref.dtype)
        lse_ref[...] = lse

    prev = lambda col: pl.BlockSpec((L, 128), lambda i: (jnp.maximum(i - 1, 0), col))
    cur = lambda col: pl.BlockSpec((L, 128), lambda i: (i, col))
    return pl.pallas_call(
        body, name="attn_fwd", grid=(nb,),
        in_specs=[SMEM, pl.BlockSpec((L, 1024), lambda i: (i, 0)), cur(8), cur(9), prev(8), prev(9),
                  pl.BlockSpec((16, L, 2 * L), lambda i: (0, 0, 0))],
        out_specs=[pl.BlockSpec((L, 1024), lambda i: (i, 0)), pl.BlockSpec((L, 128), lambda i: (i, 0))],
        out_shape=[jax.ShapeDtypeStruct((S, 1024), BF16), jax.ShapeDtypeStruct((S, 128), F32)],
        compiler_params=_params(("parallel",)))(sinks, qkv, qkv, qkv, qkv, qkv, bias)


def _attn_bwd(qkv, d_o, lse, bias, sinks):
    S = qkv.shape[0]
    nb = S // L
    scale = 64 ** -0.5

    def body(sink_ref, q_ref, kc_ref, vc_ref, kp_ref, vp_ref, do_ref, lse_ref, bias_ref,
             dq_ref, dkv_ref, dbias_ref, dsink_ref, dbq_ref, dbkv_ref, carry_ref):
        i = pl.program_id(0)

        @pl.when(i == 0)
        def _():
            dbias_ref[...] = jnp.zeros_like(dbias_ref)
            dsink_ref[...] = jnp.zeros_like(dsink_ref)
            dbq_ref[...] = jnp.zeros_like(dbq_ref)
            dbkv_ref[...] = jnp.zeros_like(dbkv_ref)
            carry_ref[...] = jnp.zeros_like(carry_ref)

        @pl.when(i < nb)
        def _():
            lane = _iota((L, 128), 1)
            lane1 = _iota((1, 128), 1)
            lo = lane < 64
            lo2 = _iota((2 * L, 128), 1) < 64
            mask = _attn_mask(i, 8 * L)
            kd = _band(kp_ref[...], kc_ref[...], lo2)
            vd = _band(vp_ref[...], vc_ref[...], lo2)
            lse_all = lse_ref[...]
            dsink = jnp.zeros((1, 128), F32)
            tot_k, tot_v = [], []
            for j in range(2):
                q_all = _attn_rows(q_ref, j, lo)
                do_all = _attn_rows(do_ref, j, lo)
                lse_col = _per_head_col([_colsel(lse_all, lane, 8 * j + t) for t in range(8)])
                lg = _dot(q_all, kd[j], NT) * scale + bias_ref[8 * j:8 * j + 8].reshape(8 * L, 2 * L)
                p = jnp.where(mask, jnp.exp(jnp.where(mask, lg, NEG_INF) - lse_col), 0.0)
                dp = _dot(do_all, vd[j], NT)
                delta = jnp.sum(p * dp, axis=1, keepdims=True)
                ds = p * (dp - delta)
                dbias_ref[8 * j:8 * j + 8] += ds.reshape(8, L, 2 * L)
                s = _per_head_col([sink_ref[8 * j + t] for t in range(8)])
                sink_part = -jnp.exp(s - lse_col) * delta
                for t in range(8):
                    dsink = dsink + jnp.where(lane1 == 8 * j + t,
                                              jnp.sum(sink_part[t * L:(t + 1) * L], axis=0, keepdims=True), 0.0)
                dss = ds * scale
                dq_all = _dot(dss, kd[j])
                for t in range(0, 8, 2):
                    sl = slice((4 * j + t // 2) * 128, (4 * j + t // 2 + 1) * 128)
                    dq = jnp.where(lo, dq_all[t * L:(t + 1) * L], dq_all[(t + 1) * L:(t + 2) * L])
                    dq_ref[:, sl] = dq.astype(dq_ref.dtype)
                    dbq_ref[:, sl] += jnp.sum(dq, axis=0, keepdims=True)
                acc_k = _dot(dss, q_all, TN)
                acc_v = _dot(p, do_all, TN)
                tot_k.append(acc_k + pltpu.roll(acc_k, 64, axis=1))
                tot_v.append(acc_v + pltpu.roll(acc_v, 64, axis=1))
            dsink_ref[...] += dsink
            dkv = jnp.concatenate([jnp.where(lo2, tot_k[0], tot_k[1]), jnp.where(lo2, tot_v[0], tot_v[1])], axis=1)
            dbkv_ref[...] += jnp.sum(dkv, axis=0, keepdims=True)
            dkv_ref[...] = (carry_ref[...] + dkv[:L, :]).astype(dkv_ref.dtype)
            carry_ref[...] = dkv[L:, :]

        @pl.when(i == nb)
        def _():
            dkv_ref[...] = carry_ref[...].astype(dkv_ref.dtype)

    c = lambda i: jnp.minimum(i, nb - 1)
    prev = lambda col: pl.BlockSpec((L, 128), lambda i: (jnp.maximum(c(i) - 1, 0), col))
    cur = lambda col: pl.BlockSpec((L, 128), lambda i: (c(i), col))
    row = lambda w: pl.BlockSpec((L, w), lambda i: (c(i), 0))
    cube = pl.BlockSpec((16, L, 2 * L), lambda i: (0, 0, 0))
    vec = lambda w: pl.BlockSpec((1, w), lambda i: (0, 0))
    return pl.pallas_call(
        body, name="attn_bwd", grid=(nb + 1,),
        in_specs=[SMEM, row(1024), cur(8), cur(9), prev(8), prev(9), row(1024), row(128), cube],
        out_specs=[row(1024), pl.BlockSpec((L, 256), lambda i: (jnp.maximum(i - 1, 0), 0)), cube,
                   vec(128), vec(1024), vec(256)],
        out_shape=[jax.ShapeDtypeStruct((S, 1024), BF16), jax.ShapeDtypeStruct((S, 256), BF16),
                   jax.ShapeDtypeStruct((16, L, 2 * L), F32), jax.ShapeDtypeStruct((1, 128), F32),
                   jax.ShapeDtypeStruct((1, 1024), F32), jax.ShapeDtypeStruct((1, 256), F32)],
        scratch_shapes=[pltpu.VMEM((L, 256), F32)],
        compiler_params=_params(("arbitrary",)))(sinks, qkv, qkv, qkv, qkv, qkv, d_o, lse, bias)


def _pad_lanes(a, n=128):
    return jnp.pad(a, ((0, 0), (0, n - a.shape[1])))


def _local_step(x, tgt, mod, w_in, P, io):
    md = [[mod[l:l + 1, k * D:(k + 1) * D] for k in range(6)] for l in range(2)]
    G, g = {}, {}

    sh1, sc1, g1, sh2, sc2, g2 = md[0]
    nmw0, nfw0 = P["norm_mix_w"][0:1], P["norm_ffn_w"][0:1]
    h0 = _norm_mod_fwd("norm_mix_0", x, nmw0, sc1, sh1, after=io["start"])
    segs = {"z": w_in[0:1024], "xbc": w_in[1024:2560], "dt": jnp.pad(w_in[2560:2576], ((0, 112), (0, 0))),
            "u": w_in[2576:3600], "v": w_in[3600:4624]}
    proj = dict(zip(segs, _mm_shared_lhs("in_proj", h0, list(segs.values()))))
    conv_w, conv_b = P["conv_w"][0], P["conv_b"]
    pre, xc = _conv_fwd(proj["xbc"], conv_w, conv_b)
    dtb, alog = _pad_lanes(P["dt_bias"]), _pad_lanes(P["a_log"])
    dskl = jnp.repeat(P["d_skip"], 64, axis=1)
    ya, y_ssd, prev = _ssd_fwd(xc, proj["dt"], proj["z"], dtb, alog, dskl, P["ssm_norm_w"])
    ws = P["gmlp_ws"][0]
    bse = jnp.broadcast_to(P["gmlp_bs"][0][:, :, None], (8, L, 128))
    yb = _gmlp_fwd(proj["u"], proj["v"], P["gmlp_ln_w"], P["gmlp_ln_b"], ws, bse)
    W = dict(io["weights0"]((ya, yb)))
    w_oa, w_ob = W["out_w"][:1024], W["out_w"][1024:]

    def res(y, x, gate, nw, sc, sh):
        xo = x + gate * y
        return y, xo, _norm_mod(xo, nw, sc, sh)
    mix0, x1, h0f = _mm("out_proj_0", [ya, yb], [w_oa, w_ob], "nn", [F32, F32, BF16], epi=res, extras=[x],
                        vecs=[g1, nfw0, sc2, sh2], whole_rows=True)
    sh1b, sc1b, g1b, sh2b, sc2b, g2b = md[1]
    nmw1, nfw1 = P["norm_mix_w"][1:2], P["norm_ffn_w"][1:2]
    a0, b0, f0, y0, x2, h1 = _ffn_fwd("0", h0f, W["gate_wt0"], W["up_wt0"], W["down_w0"], x1, g2,
                                      next_norm=(nmw1, sc1b, sh1b))

    W.update(io["weights1"](x2))
    qkv = _mm("qkv_proj", [h1], [W["qkv_wt"]], "nt", [F32], epi=lambda acc, b: acc + b, vecs=[P["qkv_b"]])[0]
    onehot_t = jnp.asarray(_bucket_onehot_t())
    bias = _rel_bias(P["rel_table"].T, onehot_t).reshape(16, L, 2 * L)
    sinks = P["sinks"].reshape(16)
    att, lse = _attn_fwd(qkv, bias, sinks)

    def res_b(y, x, gate, b, nw, sc, sh):
        y = y + b
        xo = x + gate * y
        return y, xo, _norm_mod(xo, nw, sc, sh)
    mix1, x3, h1f = _mm("o_proj", [att], [W["o_w"]], "nn", [F32, F32, BF16], epi=res_b, extras=[x2],
                        vecs=[g1b, P["o_b"], nfw1, sc2b, sh2b], whole_rows=True)
    a1, b1, f1, y1, x4, _ = _ffn_fwd("1", h1f, W["gate_wt1"], W["up_wt1"], W["down_w1"], x3, g2b)

    dx, dy, sq, g["final_norm_w"], dg2b = _loss_head(x4, tgt, P["final_norm_w"], y1, g2b)

    dh, dwg1, dwu1, dwd1 = _ffn_bwd("1", dy, h1f, a1, b1, f1, W["gate_wt1"], W["up_wt1"], W["down_w1"])
    dx, dmix, dsh2b, dsc2b, dnfw1, dg1b, g["o_b"] = _norm_mod_bwd("norm_ffn_bwd_1", x3, dh, dx, nfw1, sc2b,
                                                                 gate=(mix1, g1b))
    G["o_w"] = _mm_tn("o_dw", att, dmix)
    d_att = _mm("o_dx", [dmix], [W["o_w"]], "nt", [F32])[0]
    dq, dkv, dbias, dsinks, dbq, dbkv = _attn_bwd(qkv, d_att, lse, bias, sinks)
    g["rel_table"] = _rel_bias_bwd(dbias.reshape(16, L * 2 * L), onehot_t).T
    g["sinks"] = dsinks[:, :16]
    g["qkv_b"] = jnp.concatenate([dbq, dbkv], axis=1)
    w_q, w_kv = W["qkv_wt"][:1024], W["qkv_wt"][1024:]
    G["qkv_wt"] = jnp.concatenate([_mm_tn("qkv_dwq", dq, h1), _mm_tn("qkv_dwkv", dkv, h1)], axis=0)
    dh = _mm("qkv_dx", [dq, dkv], [w_q, w_kv], "nn", [F32])[0]
    behind = io["grads1"]({"qkv_wt": G.pop("qkv_wt"), "o_w": G.pop("o_w"), "gate_wt1": dwg1, "up_wt1": dwu1,
                           "down_w1": dwd1})
    dx, dy, dsh1b, dsc1b, dnmw1, dg2, _ = _norm_mod_bwd("norm_mix_bwd_1", x2, dh, dx, nmw1, sc1b, gate=(y0, g2),
                                                        after=behind)

    dh, dwg0, dwu0, dwd0 = _ffn_bwd("0", dy, h0f, a0, b0, f0, W["gate_wt0"], W["up_wt0"], W["down_w0"])
    behind = io["grads_ffn0"]({"gate_wt0": dwg0, "up_wt0": dwu0, "down_w0": dwd0})
    dx, dmix, dsh2, dsc2, dnfw0, dg1, _ = _norm_mod_bwd("norm_ffn_bwd_0", x1, dh, dx, nfw0, sc2, gate=(mix0, g1),
                                                        after=behind)
    G["out_w"] = jnp.concatenate([_mm_tn("out_dwa", ya, dmix), _mm_tn("out_dwb", yb, dmix)], axis=0)
    dya = _mm("out_dxa", [dmix], [w_oa], "nt", [F32])[0]
    dyb = _mm("out_dxb", [dmix], [w_ob], "nt", [F32])[0]
    du, dv, dws, dbse, g["gmlp_ln_w"], g["gmlp_ln_b"] = _gmlp_bwd(dyb, proj["u"], proj["v"], P["gmlp_ln_w"],
                                                                 P["gmlp_ln_b"], ws, bse)
    g["gmlp_ws"] = dws[None]
    g["gmlp_bs"] = _lane_sum("gmlp_dbs", dbse.reshape(8 * L, 128)).reshape(1, 8, L)
    dz, dxc, ddt, g["ssm_norm_w"], ddsk, dalog, ddtb = _ssd_bwd(dya, y_ssd, proj["z"], xc, proj["dt"], prev,
                                                                dtb, alog, dskl, P["ssm_norm_w"])
    g["d_skip"], g["a_log"], g["dt_bias"] = ddsk[0:1, :16], dalog[:, :16], ddtb[:, :16]
    dxr, dconv_w, g["conv_b"] = _conv_bwd(dxc, pre, proj["xbc"], conv_w)
    g["conv_w"] = dconv_w[None]
    dsegs = {"z": dz, "xbc": dxr, "dt": ddt, "u": du, "v": dv}
    dws_in = dict(zip(dsegs, _mm_tn_shared_rhs("in_dw", list(dsegs.values()), h0)))
    G["in_wt"] = jnp.concatenate([dws_in["z"], dws_in["xbc"], dws_in["dt"][:16], dws_in["u"], dws_in["v"]], axis=0)
    keys = ["z", "xbc", "dt", "u", "v"]
    dh = _mm("in_dx", [dsegs[k] for k in keys], [segs[k] for k in keys], "nn", [F32])[0]
    dx, dsh1, dsc1, dnmw0 = _norm_mod_bwd("norm_mix_bwd_0", x, dh, dx, nmw0, sc1)

    g["norm_mix_w"] = jnp.concatenate([dnmw0, dnmw1], axis=0)
    g["norm_ffn_w"] = jnp.concatenate([dnfw0, dnfw1], axis=0)
    dmod = jnp.concatenate([jnp.concatenate([dsh1, dsc1, dg1, dsh2, dsc2, dg2], axis=1),
                            jnp.concatenate([dsh1b, dsc1b, dg1b, dsh2b, dsc2b, dg2b], axis=1)], axis=0)
    return sq, dx, dmod, G, g


def _ada_fwd(c_all, ada_w, ada_b):
    n = ada_w.shape[2]
    tn = _col_tile(n, 512)

    def body(c_ref, w_ref, b_ref, o_ref):
        cc = c_ref[...]
        o_ref[...] = lax.dot_general(cc * _sigmoid(cc), w_ref[...], NN, precision=lax.Precision.HIGHEST,
                                     preferred_element_type=F32) + b_ref[...]

    return pl.pallas_call(
        body, name="ada_fwd", grid=(2, n // tn),
        in_specs=[pl.BlockSpec((8, D), lambda l, j: (0, 0)), pl.BlockSpec((None, D, tn), lambda l, j: (l, 0, j)),
                  pl.BlockSpec((None, 1, tn), lambda l, j: (l, 0, j))],
        out_specs=pl.BlockSpec((None, 8, tn), lambda l, j: (l, 0, j)),
        out_shape=jax.ShapeDtypeStruct((2, 8, n), F32), compiler_params=_params(("parallel", "parallel")))(
            c_all, ada_w, ada_b)


def _ada_bwd(c_all, dmod_cols, dmod_all):
    n = dmod_cols.shape[2]
    tn = _col_tile(n, 512)

    def body(c_ref, d_ref, o_ref):
        cc = c_ref[...]
        o_ref[...] = lax.dot_general(cc * _sigmoid(cc), d_ref[...], TN, precision=lax.Precision.HIGHEST,
                                     preferred_element_type=F32)

    dw = pl.pallas_call(
        body, name="ada_dw", grid=(2, n // tn),
        in_specs=[pl.BlockSpec((8, D), lambda l, j: (0, 0)), pl.BlockSpec((None, 8, tn), lambda l, j: (l, 0, j))],
        out_specs=pl.BlockSpec((None, D, tn), lambda l, j: (l, 0, j)),
        out_shape=jax.ShapeDtypeStruct((2, D, n), F32), compiler_params=_params(("parallel", "parallel")))(
            c_all, dmod_cols)

    def sum_body(d_ref, o_ref):
        o_ref[...] = jnp.sum(d_ref[...], axis=0, keepdims=True)

    db = pl.pallas_call(
        sum_body, name="ada_db", grid=(2,),
        in_specs=[pl.BlockSpec((None, 8, 6 * D), lambda l: (l, 0, 0))],
        out_specs=pl.BlockSpec((None, 1, 6 * D), lambda l: (l, 0, 0)),
        out_shape=jax.ShapeDtypeStruct((2, 1, 6 * D), F32), compiler_params=_params(("parallel",)))(dmod_all)
    return dw, db


def _row_tile(rows, cap=512, mult=8):
    best = rows
    for t in range(mult, min(rows, cap) + 1, mult):
        if rows % t == 0:
            best = t
    return best


def _adamw(name, w, g, m, v):
    def fn(w, g, m, v):
        m = ADAM_B1 * m + (1.0 - ADAM_B1) * g
        v = ADAM_B2 * v + (1.0 - ADAM_B2) * (g * g)
        m_hat = m / (1.0 - ADAM_B1 ** ADAM_STEP)
        v_hat = v / (1.0 - ADAM_B2 ** ADAM_STEP)
        return -ADAM_LR * (m_hat / (jnp.sqrt(v_hat) + ADAM_EPS) + ADAM_WD * w), m, v
    cols = w.shape[1]
    return _rowwise(name, fn, [w, g, m, v], [], [(cols, F32)] * 3, tr=_row_tile(w.shape[0]))


def _place():
    return lax.axis_index("x"), lax.axis_index("y"), lax.axis_index("c")


VMEM_SPEC = pl.BlockSpec(memory_space=pltpu.VMEM)


def _allreduce_small(name, buf, after=None):
    rows = buf.shape[0]
    deps = [] if after is None else [after]

    def body(x_ref, *rest):
        o_ref, stage, send_sems, recv_sems = rest[len(deps):]
        x, y, c = _place()
        me = 4 * x + 2 * y + c
        stage[me] = x_ref[...]
        copies = []
        for k in range(1, 8):
            peer = (1 - x if k & 4 else x, 1 - y if k & 2 else y, 1 - c if k & 1 else c)
            cp = pltpu.make_async_remote_copy(src_ref=x_ref, dst_ref=stage.at[me], send_sem=send_sems.at[k - 1],
                                              recv_sem=recv_sems.at[k - 1], device_id=peer, device_id_type=MESH)
            cp.start()
            copies.append(cp)
        for cp in copies:
            cp.wait()
        acc = stage[0]
        for d in range(1, 8):
            acc = acc + stage[d]
        o_ref[...] = acc

    return pl.pallas_call(
        body, name=name, in_specs=[VMEM_SPEC] + [ANY for _ in deps], out_specs=VMEM_SPEC,
        out_shape=jax.ShapeDtypeStruct((rows, 128), F32),
        scratch_shapes=[pltpu.VMEM((8, rows, 128), F32), pltpu.SemaphoreType.DMA((7,)), pltpu.SemaphoreType.DMA((7,))],
        compiler_params=pltpu.CompilerParams(vmem_limit_bytes=_VMEM_LIMIT))(buf, *deps)


def _sum_slots(name, own, land):
    def body(own_ref, land_ref, o_ref):
        x, y, c = _place()
        me = 4 * x + 2 * y + c
        acc = None
        for d in range(8):
            v = jnp.where(me == d, own_ref[...], land_ref[d])
            acc = v if acc is None else acc + v
        o_ref[...] = acc

    return pl.pallas_call(body, name=name, in_specs=[VMEM_SPEC, VMEM_SPEC], out_specs=VMEM_SPEC,
                          out_shape=jax.ShapeDtypeStruct(own.shape, F32),
                          compiler_params=pltpu.CompilerParams(vmem_limit_bytes=_VMEM_LIMIT))(own, land)


OTHER_CHIPS = ((1, 0), (0, 1), (1, 1))


SIBLING_COLLECTIVE_ID = 6


def _sibling_handshake():
    x, y, c = _place()
    barrier = pltpu.get_barrier_semaphore()
    pl.semaphore_signal(barrier, inc=1, device_id=(x, y, 1 - c), device_id_type=MESH)
    pl.semaphore_wait(barrier, 1)


def _sibling_swap(name, src, halves):
    half = src.shape[-2] // 2
    out_shape = (src.shape[0], half, 1024) if halves else src.shape

    def body(s_ref, o_ref, send_sem, recv_sem):
        x, y, c = _place()
        _sibling_handshake()
        part = s_ref.at[:, pl.ds(pl.multiple_of((1 - c) * half, 8), half)] if halves else s_ref
        cp = pltpu.make_async_remote_copy(src_ref=part, dst_ref=o_ref, send_sem=send_sem, recv_sem=recv_sem,
                                          device_id=(x, y, 1 - c), device_id_type=MESH)
        cp.start()
        cp.wait()

    return pl.pallas_call(
        body, name=name, in_specs=[ANY], out_specs=ANY, out_shape=jax.ShapeDtypeStruct(out_shape, src.dtype),
        scratch_shapes=[pltpu.SemaphoreType.DMA, pltpu.SemaphoreType.DMA],
        compiler_params=pltpu.CompilerParams(collective_id=SIBLING_COLLECTIVE_ID))(src)


HBM = pl.BlockSpec(memory_space=pltpu.HBM)
SEM = pl.BlockSpec(memory_space=pltpu.SEMAPHORE)


def _exchange_peers(mode):
    x, y, c = _place()
    if mode == "all":
        return [(1 - x if k & 4 else x, 1 - y if k & 2 else y, 1 - c if k & 1 else c) for k in range(1, 8)]
    return [(1 - x if fx else x, 1 - y if fy else y, c) for fx, fy in OTHER_CHIPS]


def _chip_copies(mode, src_ref, land_ref, send_sems, recv_sems):
    x, y, c = _place()
    k = 2 * x + y
    copies = []
    for j, peer in enumerate(_exchange_peers(mode)):
        if mode == "gather":
            half = src_ref.shape[0] // 2
            mine = pl.ds(pl.multiple_of(c * half, 16), half)
            src, dst = src_ref.at[mine], land_ref.at[k, mine]
        elif mode == "scatter":
            src, dst = src_ref.at[2 * peer[0] + peer[1]], land_ref.at[k]
        else:
            src, dst = src_ref, land_ref.at[4 * x + 2 * y + c]
        copies.append(pltpu.make_async_remote_copy(src_ref=src, dst_ref=dst, send_sem=send_sems.at[j],
                                                   recv_sem=recv_sems.at[j], device_id=peer, device_id_type=MESH))
    return copies


def _exchange_start(name, collective_id, mode, src, land, after=None):
    deps = [] if after is None else [after]
    npeers = 7 if mode == "all" else 3

    def body(s_ref, l_ref, *rest):
        send_sems, recv_sems, s_thru, l_thru, token = rest[len(deps):]
        barrier = pltpu.get_barrier_semaphore()
        for peer in _exchange_peers(mode):
            pl.semaphore_signal(barrier, inc=1, device_id=peer, device_id_type=MESH)
        pl.semaphore_wait(barrier, npeers)
        for cp in _chip_copies(mode, s_ref, l_ref, send_sems, recv_sems):
            cp.start()
        token[...] = jnp.zeros_like(token)

    return pl.pallas_call(
        body, name=name,
        out_shape=(pltpu.SemaphoreType.DMA((npeers,)), pltpu.SemaphoreType.DMA((npeers,)),
                   pltpu.HBM(src.shape, src.dtype),
                   pltpu.HBM(land.shape, land.dtype), jax.ShapeDtypeStruct((8, 128), F32)),
        in_specs=(HBM, HBM) + tuple(ANY for _ in deps), out_specs=(SEM, SEM, HBM, HBM, VMEM_SPEC),
        input_output_aliases={0: 2, 1: 3},
        compiler_params=pltpu.CompilerParams(has_side_effects=pltpu.SideEffectType.DATAFLOW_SIDE_EFFECTING,
                                             collective_id=collective_id))(
            pltpu.with_memory_space_constraint(src, pltpu.HBM), pltpu.with_memory_space_constraint(land, pltpu.HBM),
            *deps)


def _exchange_wait(name, mode, started, after):
    send_sems, recv_sems, s_thru, l_thru, _ = started
    deps = list(after) if isinstance(after, (tuple, list)) else [after]

    def body(s_ref, l_ref, send_sems, recv_sems, *rest):
        for cp in _chip_copies(mode, s_ref, l_ref, send_sems, recv_sems):
            cp.wait_send()
            cp.wait_recv()

    return pl.pallas_call(
        body, name=name, out_shape=(pltpu.HBM(s_thru.shape, s_thru.dtype), pltpu.HBM(l_thru.shape, l_thru.dtype)),
        in_specs=(HBM, HBM, SEM, SEM) + tuple(ANY for _ in deps), out_specs=(HBM, HBM),
        input_output_aliases={0: 0, 1: 1},
        compiler_params=pltpu.CompilerParams(has_side_effects=pltpu.SideEffectType.DATAFLOW_SIDE_EFFECTING))(
            s_thru, l_thru, send_sems, recv_sems, *deps)


def _allgather_finish(tag, land):
    half = land.shape[1] // 2

    def body(l_ref, o_ref, send_sem, recv_sem):
        x, y, c = _place()
        _sibling_handshake()
        mine = pl.ds(pl.multiple_of(c * half, 16), half)
        swap = pltpu.make_async_remote_copy(src_ref=o_ref.at[:, mine], dst_ref=o_ref.at[:, mine], send_sem=send_sem,
                                            recv_sem=recv_sem, device_id=(x, y, 1 - c), device_id_type=MESH)
        swap.start()
        swap.wait()

    return pl.pallas_call(
        body, name="allgather_finish_" + tag, in_specs=[ANY], out_specs=ANY, input_output_aliases={0: 0},
        out_shape=jax.ShapeDtypeStruct(land.shape, land.dtype),
        scratch_shapes=[pltpu.SemaphoreType.DMA, pltpu.SemaphoreType.DMA],
        compiler_params=pltpu.CompilerParams(collective_id=SIBLING_COLLECTIVE_ID))(land)


def _pair_sum(tag, g, r1, c):
    rows = g.shape[1]
    half = rows // 2
    th = _row_tile(half, 256, 16)
    nblk = half // th

    def body(c_ref, g_ref, r_ref, o_ref, o2_ref):
        o_ref[...] = (g_ref[...].astype(F32) + r_ref[...].astype(F32)).astype(o_ref.dtype)
        o2_ref[...] = o_ref[...]

    spec = pl.BlockSpec((None, th, 1024), lambda k, i, c_ref: (k, i, 0))
    grid_spec = pltpu.PrefetchScalarGridSpec(
        num_scalar_prefetch=1, grid=(4, nblk),
        in_specs=[pl.BlockSpec((None, th, 1024), lambda k, i, c_ref: (k, c_ref[0] * nblk + i, 0)), spec],
        out_specs=[spec, spec])
    return pl.pallas_call(body, name="grad_pair_sum_" + tag, grid_spec=grid_spec,
                          out_shape=[jax.ShapeDtypeStruct((4, half, 1024), BF16)] * 2,
                          compiler_params=_params(("parallel", "parallel")))(c, g, r1)


def _chip_sum(tag, q, after=None):
    half = q.shape[1]
    th = _row_tile(half, 256, 16)
    deps = [] if after is None else [after]

    def body(a, b, c, d, *rest):
        rest[-1][...] = ((a[...].astype(F32) + b[...].astype(F32)) + c[...].astype(F32)) + d[...].astype(F32)

    specs = [pl.BlockSpec((None, th, 1024), functools.partial(lambda i, k: (k, i, 0), k=k)) for k in range(4)]
    return pl.pallas_call(body, name="grad_chip_sum_" + tag, grid=(half // th,), in_specs=specs + [ANY for _ in deps],
                          out_specs=pl.BlockSpec((th, 1024), lambda i: (i, 0)),
                          out_shape=jax.ShapeDtypeStruct((half, 1024), F32),
                          compiler_params=_params(("parallel",)))(q, q, q, q, *deps)


def _join_halves(tag, f, r, c):
    half = f.shape[0]
    th = _row_tile(half, 256)
    nblk = half // th

    def body(c_ref, f_ref, r_ref, o_ref):
        mine = (pl.program_id(0) == c_ref[0])
        o_ref[...] = jnp.where(mine, f_ref[...], r_ref[...])

    spec = pl.BlockSpec((th, 1024), lambda h, i, c_ref: (i, 0))
    grid_spec = pltpu.PrefetchScalarGridSpec(
        num_scalar_prefetch=1, grid=(2, nblk), in_specs=[spec, spec],
        out_specs=pl.BlockSpec((th, 1024), lambda h, i, c_ref: (h * nblk + i, 0)))
    return pl.pallas_call(body, name="grad_join_halves_" + tag, grid_spec=grid_spec,
                          out_shape=jax.ShapeDtypeStruct((2 * half, 1024), F32),
                          compiler_params=_params(("parallel", "parallel")))(c, f, r)


BIG_ARGS = ("in_w_even", "out_w_even", "qkv_w", "o_w", "ffn_gate_w", "ffn_up_w", "ffn_down_w")
def _ffn_pieces(layer):
    return tuple((f"{n}{layer}", 704, 704) for n in ("gate_wt", "up_wt", "down_w"))


IN_SLAB = (("in_wt", 1156, 1184),)
LAYER0_REST_SLAB = (("out_w", 512, 512),) + _ffn_pieces(0)
LAYER1_SLAB = (("qkv_wt", 320, 320), ("o_w", 256, 256)) + _ffn_pieces(1)
FFN0_SLAB = _ffn_pieces(0)
MIXER0_SLAB = (("in_wt", 1156, 1280), ("out_w", 512, 512))


def _slab(pieces, spec):
    parts = []
    for name, rows, room in spec:
        p = pieces[name]
        parts.append(jnp.pad(p, [(0, 0)] * (p.ndim - 2) + [(0, room - rows), (0, 0)]) if room > rows else p)
    return jnp.concatenate(parts, axis=-2) if len(parts) > 1 else parts[0]


def _unslab(slab, spec):
    out, off = {}, 0
    for name, rows, room in spec:
        out[name] = slab[..., off:off + rows, :]
        off += room
    return out


def _share_pieces(w):
    return {"in_wt": w["in_w_even"][0].T, "out_w": w["out_w_even"][0], "qkv_wt": w["qkv_w"][0].T, "o_w": w["o_w"][0],
            "gate_wt0": w["ffn_gate_w"][0].T, "gate_wt1": w["ffn_gate_w"][1].T,
            "up_wt0": w["ffn_up_w"][0].T, "up_wt1": w["ffn_up_w"][1].T,
            "down_w0": w["ffn_down_w"][0], "down_w1": w["ffn_down_w"][1]}


def _pieces_to_shares(p):
    return {"in_w_even": p["in_wt"].T[None], "out_w_even": p["out_w"][None], "qkv_w": p["qkv_wt"].T[None],
            "o_w": p["o_w"][None], "ffn_gate_w": jnp.stack([p["gate_wt0"].T, p["gate_wt1"].T]),
            "ffn_up_w": jnp.stack([p["up_wt0"].T, p["up_wt1"].T]),
            "ffn_down_w": jnp.stack([p["down_w0"], p["down_w1"]])}


def _chips_from_full(G, spec):
    return _slab({k: v.reshape(4, -1, D) for k, v in G.items()}, spec)


def _pack_small(parts):
    padded = []
    for p in parts:
        p = p.reshape(-1).astype(F32)
        padded.append(jnp.pad(p, (0, (-p.shape[0]) % 1024)))
    return jnp.concatenate(padded).reshape(-1, 128)


def _unpack_small(slab, shapes):
    flat, out, off = slab.reshape(-1), [], 0
    for shp in shapes:
        size = math.prod(shp)
        out.append(flat[off:off + size].reshape(shp))
        off += size + (-size) % 1024
    return out


SMALL = ("ada_b", "norm_mix_w", "norm_ffn_w", "conv_w", "conv_b", "dt_bias", "a_log", "d_skip", "ssm_norm_w",
         "gmlp_ln_w", "gmlp_ln_b", "gmlp_ws", "gmlp_bs", "qkv_b", "o_b", "sinks", "rel_table", "final_norm_w")
SMALL_SPLIT = {"conv_w": 1536, "qkv_b": 1280, "o_b": 1024}
WEIGHTS = ("ada_w", "ada_b", "norm_mix_w", "norm_ffn_w", "in_w_even", "conv_w", "conv_b", "dt_bias", "a_log", "d_skip",
           "ssm_norm_w", "gmlp_ln_w", "gmlp_ln_b", "gmlp_ws", "gmlp_bs", "out_w_even", "qkv_w", "qkv_b", "o_w", "o_b",
           "sinks", "rel_table", "ffn_gate_w", "ffn_up_w", "ffn_down_w", "final_norm_w")


def kernel(x, c, ada_w, ada_b, norm_mix_w, norm_ffn_w, in_w_even, conv_w, conv_b, dt_bias, a_log, d_skip, ssm_norm_w, gmlp_ln_w, gmlp_ln_b, gmlp_ws, gmlp_bs, out_w_even, qkv_w, qkv_b, o_w, o_b, sinks, rel_table, ffn_gate_w, ffn_up_w, ffn_down_w, final_norm_w, loss_target, m_ada_w, m_ada_b, m_norm_mix_w, m_norm_ffn_w, m_in_w_even, m_conv_w, m_conv_b, m_dt_bias, m_a_log, m_d_skip, m_ssm_norm_w, m_gmlp_ln_w, m_gmlp_ln_b, m_gmlp_ws, m_gmlp_bs, m_out_w_even, m_qkv_w, m_qkv_b, m_o_w, m_o_b, m_sinks, m_rel_table, m_ffn_gate_w, m_ffn_up_w, m_ffn_down_w, m_final_norm_w, v_ada_w, v_ada_b, v_norm_mix_w, v_norm_ffn_w, v_in_w_even, v_conv_w, v_conv_b, v_dt_bias, v_a_log, v_d_skip, v_ssm_norm_w, v_gmlp_ln_w, v_gmlp_ln_b, v_gmlp_ws, v_gmlp_bs, v_out_w_even, v_qkv_w, v_qkv_b, v_o_w, v_o_b, v_sinks, v_rel_table, v_ffn_gate_w, v_ffn_up_w, v_ffn_down_w, v_final_norm_w):
    args = dict(locals())
    w = {n: args[n] for n in WEIGHTS}
    m = {n: args["m_" + n] for n in WEIGHTS}
    v = {n: args["v_" + n] for n in WEIGHTS}
    ax, ay, ac = _place()
    me = 4 * ax + 2 * ay + ac
    chip = 2 * ax + ay
    south = (ac == 0).astype(F32)
    c_arr = jnp.reshape(ac, (1,)).astype(jnp.int32)

    c_all = _allreduce_small("gather_cond", lax.dynamic_update_slice(jnp.zeros((8, D), F32), c, (me, 0)).reshape(64, 128))
    c_all = c_all.reshape(8, D)
    n_ada = ada_w.shape[2]
    mod_cols = _ada_fwd(c_all, ada_w, lax.dynamic_slice(ada_b, (0, chip * n_ada), (2, n_ada)).reshape(2, 1, n_ada))
    pieces = [lax.dynamic_update_slice(jnp.zeros((2, 8, 6 * D), F32), mod_cols, (0, 0, chip * n_ada))]
    split_names = list(SMALL_SPLIT)
    for n in split_names:
        full = SMALL_SPLIT[n]
        local = w[n]
        idx = (0,) * (local.ndim - 1) + (chip * local.shape[-1],)
        pieces.append(lax.dynamic_update_slice(jnp.zeros(local.shape[:-1] + (full,), F32), local, idx))
    shapes = [p.shape for p in pieces]
    mod_own = _pack_small(pieces) * south
    mod_started = _exchange_start("gather_mod_start", 9, "all", mod_own, lax.empty((8,) + mod_own.shape, F32))

    pieces = _share_pieces(w)
    cast = {"in_wt": pieces["in_wt"].astype(_MXU)}

    def start_gather(tag, collective_id, share, after):
        return _exchange_start("allgather_start_" + tag, collective_id, "gather", share,
                               lax.empty((4,) + share.shape, share.dtype), after=after)

    def finish_gather(tag, started, spec, after):
        land = _exchange_wait("allgather_wait_" + tag, "gather", started, after)[1]
        out = {}
        for name, piece in _unslab(_allgather_finish(tag, land), spec).items():
            out[name] = lax.dynamic_update_slice(piece.reshape(-1, D), cast[name], (chip * piece.shape[1], 0))
        return out

    gather_in = start_gather("in", 7, _slab(cast, IN_SLAB), mod_started[4])
    zero = gather_in[4][0, 0]
    cast.update({k: (p + zero).astype(_MXU) for k, p in pieces.items() if k != "in_wt"})
    share0, share1 = _slab(cast, LAYER0_REST_SLAB), _slab(cast, LAYER1_SLAB)
    mod_own, mod_land = _exchange_wait("gather_mod_wait", "all", mod_started, (share0, share1))
    mod_slab = _sum_slots("gather_mod_sum", mod_own, mod_land)
    gathered = _unpack_small(mod_slab, shapes)
    mod = lax.dynamic_slice(gathered[0], (0, me, 0), (2, 1, 6 * D)).reshape(2, 6 * D)
    P = {n: w[n] for n in SMALL if n not in SMALL_SPLIT and n != "ada_b"}
    for n, full in zip(split_names, gathered[1:]):
        P[n] = full
    P["final_norm_w"] = final_norm_w.reshape(1, D)
    w_in = finish_gather("in", gather_in, IN_SLAB, mod_slab)["in_wt"]
    gather0 = start_gather("0", 1, share0, w_in)
    gather1 = start_gather("1", 2, share1, gather0[4])

    def start_reduce(tag, collective_id, G, spec, after=None):
        gp = _chips_from_full(G, spec).astype(BF16)
        p, q = _pair_sum(tag, gp, _sibling_swap("grad_pair_exchange_" + tag, gp, True), c_arr)
        return _exchange_start("grad_exchange_start_" + tag, collective_id, "scatter", p, q, after=after)

    def finish_reduce(tag, started, spec, after, behind=None):
        q = _exchange_wait("grad_exchange_wait_" + tag, "scatter", started, after)[1]
        fin = _chip_sum(tag, q, after=behind)
        total = _join_halves(tag, fin, _sibling_swap("grad_final_exchange_" + tag, fin, False), c_arr)
        return _unslab(total, spec)

    reduces = {}

    def grads1(G1):
        reduces["1"] = start_reduce("1", 3, G1, LAYER1_SLAB)
        return reduces["1"][4]

    def grads_ffn0(G):
        reduces["f"] = start_reduce("f", 4, G, FFN0_SLAB)
        return reduces["f"][4]

    io = {"start": gather1[4],
          "weights0": lambda after: finish_gather("0", gather0, LAYER0_REST_SLAB, after),
          "weights1": lambda after: finish_gather("1", gather1, LAYER1_SLAB, after),
          "grads1": grads1, "grads_ffn0": grads_ffn0}
    sq, grad_x, dmod, G0, g = _local_step(x[0], loss_target[0], mod, w_in, P, io)
    loss = lax.psum(0.5 * sq[0, 0] / D, ("x", "y", "c"))

    g["final_norm_w"] = g["final_norm_w"].reshape(D)
    small_names = [n for n in SMALL if n != "ada_b"]
    pieces = [lax.dynamic_update_slice(jnp.zeros((2, 8, 6 * D), F32), dmod.reshape(2, 1, 6 * D), (0, me, 0))]
    pieces += [g[n] for n in small_names]
    shapes = [p.shape for p in pieces]
    small_own = _pack_small(pieces)
    small_started = _exchange_start("small_grads_start", 8, "all", small_own, lax.empty((8,) + small_own.shape, F32))
    reduces["m"] = start_reduce("m", 5, G0, MIXER0_SLAB, after=small_started[4])
    shares = finish_reduce("1", reduces["1"], LAYER1_SLAB, grad_x, behind=reduces["m"][4])
    shares.update(finish_reduce("f", reduces["f"], FFN0_SLAB, grad_x, behind=reduces["m"][4]))
    small_own, small_land = _exchange_wait("small_grads_wait", "all", small_started, shares["down_w0"])
    reduced = _unpack_small(_sum_slots("small_grads_sum", small_own, small_land), shapes)
    dmod_all = reduced[0]
    grads = dict(zip(small_names, reduced[1:]))
    for n in split_names:
        full = grads[n]
        size = w[n].shape[-1]
        grads[n] = lax.dynamic_slice(full, (0,) * (full.ndim - 1) + (chip * size,), full.shape[:-1] + (size,))
    grads = {n: grads[n].reshape(w[n].shape) for n in small_names}
    dw_ada, db_ada = _ada_bwd(c_all, lax.dynamic_slice(dmod_all, (0, 0, chip * n_ada), (2, 8, n_ada)), dmod_all)
    grads["ada_w"], grads["ada_b"] = dw_ada, db_ada.reshape(2, 6 * D)

    delta, new_m, new_v = {}, {}, {}

    def update(n):
        cols = w[n].shape[-1]
        d_, m_, v_ = _adamw("adamw_" + n, w[n].reshape(-1, cols), grads[n].reshape(-1, cols), m[n].reshape(-1, cols),
                            v[n].reshape(-1, cols))
        delta[n], new_m[n], new_v[n] = d_.reshape(w[n].shape), m_.reshape(w[n].shape), v_.reshape(w[n].shape)

    update("ada_w")
    shapes = [w[n].shape for n in SMALL]
    packed = [_pack_small([t[n] for n in SMALL]) for t in (w, grads, m, v)]
    outs = _adamw("adamw_small", *packed)
    for dst, slab in zip((delta, new_m, new_v), outs):
        for n, t in zip(SMALL, _unpack_small(slab, shapes)):
            dst[n] = t
    shares.update(finish_reduce("m", reduces["m"], MIXER0_SLAB, outs[0]))
    grads.update(_pieces_to_shares(shares))
    for n in BIG_ARGS:
        update(n)
    return (loss, grad_x[None], *[grads[n] for n in WEIGHTS], *[delta[n] for n in WEIGHTS],
            *[new_m[n] for n in WEIGHTS], *[new_v[n] for n in WEIGHTS])
```

```python
import functools
import math

import numpy as np
import jax
import jax.numpy as jnp
from jax import lax
from jax.experimental import pallas as pl
from jax.experimental.pallas import tpu as pltpu

F32 = jnp.float32
BF16 = jnp.bfloat16
_MXU = jnp.bfloat16
_VMEM_LIMIT = 56 * 1024 * 1024
MXU_COLS = 256
D = 1024
L = 128
NSTATE = 128
EPS = 1e-6
NEG_INF = -1e30
FFN = 2816
ADAM_LR, ADAM_B1, ADAM_B2, ADAM_EPS, ADAM_WD, ADAM_STEP = 0.001, 0.9, 0.999, 1e-08, 0.01, 10
MESH = pl.DeviceIdType.MESH
ANY = pl.BlockSpec(memory_space=pl.ANY)

NN = (((1,), (0,)), ((), ()))
NT = (((1,), (1,)), ((), ()))
TN = (((0,), (0,)), ((), ()))


def _dot(a, b, dn=NN):
    return lax.dot_general(a.astype(_MXU), b.astype(_MXU), dn, preferred_element_type=F32)


def _params(sem=None):
    return pltpu.CompilerParams(dimension_semantics=sem, vmem_limit_bytes=_VMEM_LIMIT)


def _sigmoid(x):
    return 1.0 / (1.0 + jnp.exp(-x))


def _softplus(x):
    return jnp.maximum(x, 0.0) + jnp.log(1.0 + jnp.exp(-jnp.abs(x)))


def _gelu(x):
    return 0.5 * x * (1.0 + lax.erf(x * (2.0 ** -0.5)))


def _gelu_grad(x):
    return 0.5 * (1.0 + lax.erf(x * (2.0 ** -0.5))) + x * jnp.exp(-0.5 * x * x) * (1.0 / math.sqrt(2.0 * math.pi))


def _silu_grad(a):
    sg = _sigmoid(a)
    return sg * (1.0 + a * (1.0 - sg))


def _rowwise(name, fn, rows, vecs, out_rows, out_accs=(), tr=512, after=None):
    S = rows[0].shape[0]
    tr = min(tr, S)
    assert S % tr == 0
    nr, nv, no, na = len(rows), len(vecs), len(out_rows), len(out_accs)
    deps = [] if after is None else [after]

    def body(*refs):
        ins, outs = refs[:nr + nv], refs[nr + nv + len(deps):]
        res = fn(*[r[...] for r in ins])
        if not isinstance(res, (tuple, list)):
            res = (res,)
        for k in range(no):
            outs[k][...] = res[k].astype(outs[k].dtype)
        if na:
            @pl.when(pl.program_id(0) == 0)
            def _():
                for k in range(na):
                    outs[no + k][...] = jnp.zeros_like(outs[no + k])
            for k in range(na):
                outs[no + k][...] += res[no + k]

    in_specs = [pl.BlockSpec((tr, a.shape[1]), lambda i: (i, 0)) for a in rows]
    in_specs += [pl.BlockSpec(v.shape, lambda i: (0, 0)) for v in vecs] + [ANY for _ in deps]
    out_specs = [pl.BlockSpec((tr, c), lambda i: (i, 0)) for c, _ in out_rows]
    out_specs += [pl.BlockSpec(s, lambda i: (0, 0)) for s in out_accs]
    out_shape = [jax.ShapeDtypeStruct((S, c), dt) for c, dt in out_rows]
    out_shape += [jax.ShapeDtypeStruct(s, F32) for s in out_accs]
    return pl.pallas_call(body, name=name, grid=(S // tr,), in_specs=in_specs, out_specs=out_specs,
                          out_shape=out_shape, compiler_params=_params(("arbitrary",)))(*rows, *vecs, *deps)


def _col_tile(n, cap):
    if n <= cap or n % 128:
        return n
    best = 128
    for t in range(128, cap + 1, 128):
        if n % t == 0:
            best = t
    return best


def _mm(name, As, Bs, mode, outs, epi=None, groups=None, extras=(), vecs=(), tm=512, tn_cap=1536, whole_rows=False):
    M = As[0].shape[0]
    N = Bs[0].shape[1] if mode == "nn" else Bs[0].shape[0]
    tm = min(tm, M)
    tn = _col_tile(N, tn_cap)
    assert M % tm == 0 and N % tn == 0
    npair = len(As)
    groups = groups or [0] * npair
    ng = max(groups) + 1
    nx, nv = len(extras), len(vecs)
    dn = NN if mode == "nn" else NT

    def body(*refs):
        a_refs, b_refs = refs[:npair], refs[npair:2 * npair]
        x_refs = refs[2 * npair:2 * npair + nx]
        v_refs = refs[2 * npair + nx:2 * npair + nx + nv]
        o_refs = refs[2 * npair + nx + nv:]
        step = tn if (epi is None or whole_rows) else min(tn, MXU_COLS)
        for col in range(0, tn, step):
            sl = slice(col, min(col + step, tn))
            accs = [None] * ng
            for k in range(npair):
                b = b_refs[k][:, sl] if mode == "nn" else b_refs[k][sl, :]
                d = _dot(a_refs[k][...], b, dn)
                accs[groups[k]] = d if accs[groups[k]] is None else accs[groups[k]] + d
            args = accs + [x[:, sl] for x in x_refs] + [v[:, sl] for v in v_refs]
            res = epi(*args) if epi is not None else tuple(accs)
            if not isinstance(res, (tuple, list)):
                res = (res,)
            for o, r in zip(o_refs, res):
                o[:, sl] = r.astype(o.dtype)

    in_specs = [pl.BlockSpec((tm, a.shape[1]), lambda i, j: (i, 0)) for a in As]
    if mode == "nn":
        in_specs += [pl.BlockSpec((b.shape[0], tn), lambda i, j: (0, j)) for b in Bs]
    else:
        in_specs += [pl.BlockSpec((tn, b.shape[1]), lambda i, j: (j, 0)) for b in Bs]
    in_specs += [pl.BlockSpec((tm, tn), lambda i, j: (i, j)) for _ in extras]
    in_specs += [pl.BlockSpec((1, tn), lambda i, j: (0, j)) for _ in vecs]
    out_specs = [pl.BlockSpec((tm, tn), lambda i, j: (i, j)) for _ in outs]
    out_shape = [jax.ShapeDtypeStruct((M, N), dt) for dt in outs]
    return pl.pallas_call(body, name=name, grid=(M // tm, N // tn), in_specs=in_specs, out_specs=out_specs,
                          out_shape=out_shape, compiler_params=_params(("parallel", "parallel")))(
                              *As, *Bs, *extras, *vecs)


def _mm_shared_lhs(name, A, Bs, tm=512):
    M, K = A.shape
    tm = min(tm, M)
    assert M % tm == 0
    n = len(Bs)

    def body(a_ref, *refs):
        a = a_ref[...]
        for b_ref, o_ref in zip(refs[:n], refs[n:]):
            o_ref[...] = _dot(a, b_ref[...], NT)

    return pl.pallas_call(
        body, name=name, grid=(M // tm,),
        in_specs=[pl.BlockSpec((tm, K), lambda i: (i, 0))] + [pl.BlockSpec(b.shape, lambda i: (0, 0)) for b in Bs],
        out_specs=[pl.BlockSpec((tm, b.shape[0]), lambda i: (i, 0)) for b in Bs],
        out_shape=[jax.ShapeDtypeStruct((M, b.shape[0]), F32) for b in Bs],
        compiler_params=_params(("parallel",)))(A, *Bs)


def _mm_tn_shared_rhs(name, As, B, tk=256):
    S, N = B.shape
    tk = min(tk, S)
    assert S % tk == 0
    n = len(As)

    def body(*refs):
        a_refs, b_ref, o_refs = refs[:n], refs[n], refs[n + 1:]

        @pl.when(pl.program_id(0) == 0)
        def _():
            for o_ref in o_refs:
                o_ref[...] = jnp.zeros_like(o_ref)
        b = b_ref[...]
        for a_ref, o_ref in zip(a_refs, o_refs):
            o_ref[...] += _dot(a_ref[...], b, TN)

    return pl.pallas_call(
        body, name=name, grid=(S // tk,),
        in_specs=[pl.BlockSpec((tk, a.shape[1]), lambda k: (k, 0)) for a in As] + [pl.BlockSpec((tk, N), lambda k: (k, 0))],
        out_specs=[pl.BlockSpec((a.shape[1], N), lambda k: (0, 0)) for a in As],
        out_shape=[jax.ShapeDtypeStruct((a.shape[1], N), F32) for a in As],
        compiler_params=_params(("arbitrary",)))(*As, B)


def _mm_tn(name, A, B, tk=512, t2_cap=1536):
    S, K1 = A.shape
    N2 = B.shape[1]
    tk = min(tk, S)
    t2 = _col_tile(N2, t2_cap)
    assert S % tk == 0 and N2 % t2 == 0

    def body(a_ref, b_ref, o_ref):
        @pl.when(pl.program_id(1) == 0)
        def _():
            o_ref[...] = jnp.zeros_like(o_ref)
        o_ref[...] += _dot(a_ref[...], b_ref[...], TN)

    return pl.pallas_call(
        body, name=name, grid=(N2 // t2, S // tk),
        in_specs=[pl.BlockSpec((tk, K1), lambda j, k: (k, 0)), pl.BlockSpec((tk, t2), lambda j, k: (k, j))],
        out_specs=pl.BlockSpec((K1, t2), lambda j, k: (0, j)),
        out_shape=jax.ShapeDtypeStruct((K1, N2), F32),
        compiler_params=_params(("parallel", "arbitrary")))(A, B)


def _norm_mod(x, nw, sc, sh):
    rstd = lax.rsqrt(jnp.mean(x * x, axis=-1, keepdims=True) + EPS)
    return (x * rstd * nw) * (1.0 + sc) + sh


def _norm_mod_fwd(name, x, nw, sc, sh, after=None):
    return _rowwise(name, _norm_mod, [x], [nw, sc, sh], [(D, BF16)], after=after)[0]


def _norm_mod_bwd(name, x, dh, dres, nw, sc, gate=None, after=None):
    def fn(x, dh, dres, *rest):
        nw, sc = rest[-3:-1] if gate else rest
        rstd = lax.rsqrt(jnp.mean(x * x, axis=-1, keepdims=True) + EPS)
        xh = x * rstd
        dn = dh * (1.0 + sc)
        dxh = dn * nw
        dx = dres + rstd * (dxh - xh * jnp.mean(dxh * xh, axis=-1, keepdims=True))
        sums = [jnp.sum(dh, axis=0, keepdims=True), jnp.sum(dh * (xh * nw), axis=0, keepdims=True),
                jnp.sum(dn * xh, axis=0, keepdims=True)]
        if not gate:
            return (dx, *sums)
        dy = dx * rest[-1]
        return (dx, dy, *sums, jnp.sum(dx * rest[0], axis=0, keepdims=True), jnp.sum(dy, axis=0, keepdims=True))
    if not gate:
        return _rowwise(name, fn, [x, dh, dres], [nw, sc], [(D, F32)], [(1, D)] * 3, after=after)
    return _rowwise(name, fn, [x, dh, dres, gate[0]], [nw, sc, gate[1]], [(D, F32), (D, BF16)], [(1, D)] * 5,
                    after=after)


def _loss_head(x, tgt, fw, y, g):
    def fn(x, tgt, y, fw, g):
        rstd = lax.rsqrt(jnp.mean(x * x, axis=-1, keepdims=True) + EPS)
        xh = x * rstd
        err = xh * fw - tgt
        dout = err * (1.0 / D)
        dxh = dout * fw
        dx = rstd * (dxh - xh * jnp.mean(dxh * xh, axis=-1, keepdims=True))
        sq = jnp.sum(jnp.sum(err * err, axis=1, keepdims=True), axis=0, keepdims=True)
        return (dx, dx * g, sq, jnp.sum(dout * xh, axis=0, keepdims=True), jnp.sum(dx * y, axis=0, keepdims=True))
    return _rowwise("loss_head", fn, [x, tgt, y], [fw, g], [(D, F32), (D, BF16)], [(1, 1), (1, D), (1, D)])


def _ffn_fwd(tag, h, wg, wu, wd, x, g2, next_norm=None):
    def act(a, b):
        return a, b, a * _sigmoid(a) * b
    a, b, f = _mm(f"ffn_up_{tag}", [h, h], [wg, wu], "nt", [BF16, BF16, BF16], epi=act, groups=[0, 1], tn_cap=1408,
                  tm=1024)

    if next_norm is None:
        def res(y, x, g):
            return y, x + g * y
        y, xo = _mm(f"ffn_down_{tag}", [f], [wd], "nn", [F32, F32], epi=res, extras=[x], vecs=[g2])
        return a, b, f, y, xo, None

    def res_norm(y, x, g, nw, sc, sh):
        xo = x + g * y
        return y, xo, _norm_mod(xo, nw, sc, sh)
    assert wd.shape[1] == D
    y, xo, h_next = _mm(f"ffn_down_{tag}", [f], [wd], "nn", [F32, F32, BF16], epi=res_norm, extras=[x],
                        vecs=[g2, *next_norm], whole_rows=True)
    return a, b, f, y, xo, h_next


def _ffn_bwd(tag, dy, h, a, b, f, wg, wu, wd):
    def act_bwd(df, a, b):
        a, b = a.astype(F32), b.astype(F32)
        sg = _sigmoid(a)
        return df * b * (sg * (1.0 + a * (1.0 - sg))), df * (a * sg)
    da, db = _mm(f"ffn_dact_{tag}", [dy], [wd], "nt", [BF16, BF16], epi=act_bwd, extras=[a, b], tn_cap=1408, tm=1024)
    dwd = _mm_tn(f"ffn_dwd_{tag}", f, dy)
    dwg = _mm_tn(f"ffn_dwg_{tag}", da, h)
    dwu = _mm_tn(f"ffn_dwu_{tag}", db, h)
    dh = _mm(f"ffn_dh_{tag}", [da, db], [wg, wu], "nn", [F32])[0]
    return dh, dwg, dwu, dwd


def _conv_fwd(xr, w, b, tb=512):
    S, C = xr.shape
    tb = min(tb, S)

    def body(x_ref, halo_ref, w_ref, b_ref, pre_ref, out_ref):
        i = pl.program_id(0)
        halo = jnp.where(i > 0, halo_ref[...], 0.0)
        xe = jnp.concatenate([halo, x_ref[...]], axis=0)
        pre = w_ref[3:4, :] * x_ref[...] + b_ref[...]
        for j in (1, 2, 3):
            pre = pre + w_ref[3 - j:4 - j, :] * pltpu.roll(xe, j, axis=0)[8:, :]
        pre_ref[...] = pre
        out_ref[...] = pre * _sigmoid(pre)

    return pl.pallas_call(
        body, name="conv_fwd", grid=(S // tb,),
        in_specs=[pl.BlockSpec((tb, C), lambda i: (i, 0)),
                  pl.BlockSpec((8, C), lambda i: (jnp.maximum(i * (tb // 8) - 1, 0), 0)),
                  pl.BlockSpec((4, C), lambda i: (0, 0)), pl.BlockSpec((1, C), lambda i: (0, 0))],
        out_specs=[pl.BlockSpec((tb, C), lambda i: (i, 0))] * 2,
        out_shape=[jax.ShapeDtypeStruct((S, C), F32)] * 2,
        compiler_params=_params(("parallel",)))(xr, xr, w, b)


def _conv_bwd(dxc, pre, xr, w, tb=512):
    S, C = xr.shape
    tb = min(tb, S)
    nblk = S // tb

    def body(d_ref, p_ref, dn_ref, pn_ref, x_ref, w_ref, dx_ref, dw_ref, db_ref):
        i = pl.program_id(0)

        @pl.when(i == 0)
        def _():
            dw_ref[...] = jnp.zeros_like(dw_ref)
            db_ref[...] = jnp.zeros_like(db_ref)

        dpre = d_ref[...] * _silu_grad(p_ref[...])
        dnext = jnp.where(i < nblk - 1, dn_ref[...] * _silu_grad(pn_ref[...]), 0.0)
        pe = jnp.concatenate([dpre, dnext], axis=0)
        xx = x_ref[...]
        dx = w_ref[3:4, :] * dpre
        dw_ref[3:4, :] += jnp.sum(dpre * xx, axis=0, keepdims=True)
        for j in (1, 2, 3):
            ahead = pltpu.roll(pe, tb + 8 - j, axis=0)[:tb, :]
            dx = dx + w_ref[3 - j:4 - j, :] * ahead
            dw_ref[3 - j:4 - j, :] += jnp.sum(ahead * xx, axis=0, keepdims=True)
        dx_ref[...] = dx.astype(dx_ref.dtype)
        db_ref[...] += jnp.sum(dpre, axis=0, keepdims=True)

    blk = pl.BlockSpec((tb, C), lambda i: (i, 0))
    nxt = pl.BlockSpec((8, C), lambda i: (jnp.minimum((i + 1) * (tb // 8), S // 8 - 1), 0))
    return pl.pallas_call(
        body, name="conv_bwd", grid=(nblk,),
        in_specs=[blk, blk, nxt, nxt, blk, pl.BlockSpec((4, C), lambda i: (0, 0))],
        out_specs=[blk, pl.BlockSpec((4, C), lambda i: (0, 0)), pl.BlockSpec((1, C), lambda i: (0, 0))],
        out_shape=[jax.ShapeDtypeStruct((S, C), BF16), jax.ShapeDtypeStruct((4, C), F32),
                   jax.ShapeDtypeStruct((1, C), F32)],
        compiler_params=_params(("arbitrary",)))(dxc, pre, dxc, pre, xr, w)


def _iota(shape, dim):
    return lax.broadcasted_iota(jnp.int32, shape, dim)


def _colsel(m, lane, h):
    return jnp.sum(jnp.where(lane == h, m, 0.0), axis=1, keepdims=True)


def _cumsum_rows(v):
    r = _iota(v.shape, 0)
    k = 1
    while k < v.shape[0]:
        v = v + jnp.where(r >= k, pltpu.roll(v, k, axis=0), 0.0)
        k *= 2
    return v


def _suffix_sum_rows(v):
    n = v.shape[0]
    r = _iota(v.shape, 0)
    k = 1
    while k < n:
        v = v + jnp.where(r < n - k, pltpu.roll(v, n - k, axis=0), 0.0)
        k *= 2
    return v


def _ssd_fwd(xc, dtr, z, dtb, alog, dskl, nw):
    S = xc.shape[0]
    nc = S // L

    def body(xc_ref, dtr_ref, z_ref, dtb_ref, alog_ref, dsk_ref, nw_ref, ya_ref, y_ref, prev_ref,
             st_ref, cum_ref, cumT_ref):
        i = pl.program_id(0)

        @pl.when(i == 0)
        def _():
            st_ref[...] = jnp.zeros_like(st_ref)

        lane = _iota((L, 128), 1)
        lane1 = _iota((1, 128), 1)
        lo = lane < 64
        lo1 = lane1 < 64
        tril = _iota((L, L), 0) >= _iota((L, L), 1)
        dt = _softplus(dtr_ref[...] + dtb_ref[...])
        a_neg = -jnp.exp(alog_ref[...])
        cum = _cumsum_rows(dt * a_neg)
        cum_ref[...] = cum
        cumT_ref[...] = cum.T
        last_all = cum_ref[L - 1:L, :]
        prev_t = st_ref[...]
        prev_ref[0] = prev_t
        for g in range(2):
            bg = xc_ref[:, 1024 + g * 128:1152 + g * 128]
            cg = xc_ref[:, 1280 + g * 128:1408 + g * 128]
            gmat = _dot(cg, bg, NT)
            yoff = _dot(cg, prev_t[:, g * 512:(g + 1) * 512])
            bg_t = bg.T
            for jp in range(4):
                j = g * 4 + jp
                sl = slice(j * 128, (j + 1) * 128)
                xp = xc_ref[:, sl]
                cc = [_colsel(cum, lane, 2 * j), _colsel(cum, lane, 2 * j + 1)]
                cum_l = jnp.where(lo, cc[0], cc[1])
                dt_l = jnp.where(lo, _colsel(dt, lane, 2 * j), _colsel(dt, lane, 2 * j + 1))
                last_l = jnp.where(lo1, _colsel(last_all, lane1, 2 * j), _colsel(last_all, lane1, 2 * j + 1))
                xd = xp * dt_l
                ys = []
                for hh in range(2):
                    seg = cc[hh] - cumT_ref[2 * j + hh:2 * j + hh + 1, :]
                    dm = jnp.where(tril, jnp.exp(seg), 0.0)
                    ys.append(_dot(gmat * dm, xd))
                y_ref[:, sl] = (jnp.where(lo, ys[0], ys[1]) + jnp.exp(cum_l) * yoff[:, jp * 128:(jp + 1) * 128]
                                + dsk_ref[:, sl] * xp)
                st_ref[:, sl] = prev_t[:, sl] * jnp.exp(last_l) + _dot(bg_t, xd * jnp.exp(last_l - cum_l))
        for g in range(2):
            sl = slice(g * 512, (g + 1) * 512)
            zz = z_ref[:, sl]
            yg = y_ref[:, sl] * (zz * _sigmoid(zz))
            rstd = lax.rsqrt(jnp.mean(yg * yg, axis=-1, keepdims=True) + EPS)
            ya_ref[:, sl] = (yg * rstd * nw_ref[:, sl]).astype(ya_ref.dtype)

    blk = lambda c: pl.BlockSpec((L, c), lambda i: (i, 0))
    vec = lambda c: pl.BlockSpec((1, c), lambda i: (0, 0))
    return pl.pallas_call(
        body, name="ssd_fwd", grid=(nc,),
        in_specs=[blk(1536), blk(128), blk(1024), vec(128), vec(128), vec(1024), vec(1024)],
        out_specs=[blk(1024), blk(1024), pl.BlockSpec((1, NSTATE, 1024), lambda i: (i, 0, 0))],
        out_shape=[jax.ShapeDtypeStruct((S, 1024), BF16), jax.ShapeDtypeStruct((S, 1024), F32),
                   jax.ShapeDtypeStruct((nc, NSTATE, 1024), F32)],
        scratch_shapes=[pltpu.VMEM((NSTATE, 1024), F32), pltpu.VMEM((L, 128), F32), pltpu.VMEM((L, 128), F32)],
        compiler_params=_params(("arbitrary",)))(xc, dtr, z, dtb, alog, dskl, nw)


def _ssd_bwd(dya, y, z, xc, dtr, prev, dtb, alog, dskl, nw):
    S = xc.shape[0]
    nc = S // L

    def body(dya_ref, y_ref, z_ref, xc_ref, dtr_ref, prev_ref, dtb_ref, alog_ref, dsk_ref, nw_ref,
             dz_ref, dxc_ref, ddtr_ref, dnw_ref, ddsk_ref, dalog_ref, ddtb_ref,
             dst_ref, cum_ref, cumT_ref, dy_ref, dskacc_ref):
        i = pl.program_id(0)

        @pl.when(i == 0)
        def _():
            dst_ref[...] = jnp.zeros_like(dst_ref)
            dskacc_ref[...] = jnp.zeros_like(dskacc_ref)
            dnw_ref[...] = jnp.zeros_like(dnw_ref)
            dalog_ref[...] = jnp.zeros_like(dalog_ref)
            ddtb_ref[...] = jnp.zeros_like(ddtb_ref)

        lane = _iota((L, 128), 1)
        lane1 = _iota((1, 128), 1)
        lo = lane < 64
        lo1 = lane1 < 64
        r2, c2 = _iota((L, L), 0), _iota((L, L), 1)
        tril = r2 >= c2
        triu = r2 <= c2
        is_last = _iota((L, 1), 0) == L - 1

        for g in range(2):
            sl = slice(g * 512, (g + 1) * 512)
            zz = z_ref[:, sl]
            sg = _sigmoid(zz)
            zg = zz * sg
            yv = y_ref[:, sl]
            yg = yv * zg
            rstd = lax.rsqrt(jnp.mean(yg * yg, axis=-1, keepdims=True) + EPS)
            xh = yg * rstd
            d_out = dya_ref[:, sl]
            dnw_ref[:, sl] += jnp.sum(d_out * xh, axis=0, keepdims=True)
            dyn = d_out * nw_ref[:, sl]
            dyg = rstd * (dyn - xh * jnp.mean(dyn * xh, axis=-1, keepdims=True))
            dy_ref[:, sl] = dyg * zg
            dz_ref[:, sl] = (dyg * yv * (sg * (1.0 + zz * (1.0 - sg)))).astype(dz_ref.dtype)

        dtin = dtr_ref[...] + dtb_ref[...]
        dt = _softplus(dtin)
        a_neg = -jnp.exp(alog_ref[...])
        cum = _cumsum_rows(dt * a_neg)
        cum_ref[...] = cum
        cumT_ref[...] = cum.T
        last_all = cum_ref[L - 1:L, :]
        prev_t = prev_ref[0]
        dn_t = dst_ref[...]
        dcum = jnp.zeros((L, 128), F32)
        ddt = jnp.zeros((L, 128), F32)
        for g in range(2):
            gsl = slice(g * 512, (g + 1) * 512)
            bg = xc_ref[:, 1024 + g * 128:1152 + g * 128]
            cg = xc_ref[:, 1280 + g * 128:1408 + g * 128]
            gmat = _dot(cg, bg, NT)
            gmat_t = _dot(bg, cg, NT)
            pg = prev_t[:, gsl]
            zmat = _dot(cg, pg)
            dgm = jnp.zeros((L, L), F32)
            dgm_t = jnp.zeros((L, L), F32)
            db_acc = jnp.zeros((L, NSTATE), F32)
            dz_parts, cd_parts = [], []
            for jp in range(4):
                j = g * 4 + jp
                sl = slice(j * 128, (j + 1) * 128)
                xp = xc_ref[:, sl]
                dyp = dy_ref[:, sl]
                cc = [_colsel(cum, lane, 2 * j), _colsel(cum, lane, 2 * j + 1)]
                lc = [_colsel(last_all, lane1, 2 * j), _colsel(last_all, lane1, 2 * j + 1)]
                cum_l = jnp.where(lo, cc[0], cc[1])
                dt_l = jnp.where(lo, _colsel(dt, lane, 2 * j), _colsel(dt, lane, 2 * j + 1))
                last_l = jnp.where(lo1, lc[0], lc[1])
                e_l = jnp.exp(cum_l)
                dte_l = jnp.exp(last_l - cum_l)
                cd_l = jnp.exp(last_l)
                cd_parts.append(cd_l)
                xd = xp * dt_l
                dskacc_ref[:, sl] += jnp.sum(dyp * xp, axis=0, keepdims=True)
                dxp = dsk_ref[:, sl] * dyp
                t = dyp * (e_l * zmat[:, jp * 128:(jp + 1) * 128])
                dcc = [jnp.sum(jnp.where(lo, t, 0.0), axis=1, keepdims=True),
                       jnp.sum(jnp.where(lo, 0.0, t), axis=1, keepdims=True)]
                dz_parts.append(e_l * dyp)
                dnp_ = dn_t[:, sl]
                t2 = jnp.sum(dnp_ * prev_t[:, sl], axis=0, keepdims=True)
                dcd = [jnp.sum(jnp.where(lo1, t2, 0.0), axis=1, keepdims=True),
                       jnp.sum(jnp.where(lo1, 0.0, t2), axis=1, keepdims=True)]
                wm = _dot(bg, dnp_)
                dxd = wm * dte_l
                t3 = wm * xd
                ddte = [jnp.sum(jnp.where(lo, t3, 0.0), axis=1, keepdims=True),
                        jnp.sum(jnp.where(lo, 0.0, t3), axis=1, keepdims=True)]
                db_acc = db_acc + _dot(xd * dte_l, dnp_, NT)
                for hh in range(2):
                    h = 2 * j + hh
                    half = lo if hh == 0 else jnp.logical_not(lo)
                    row = cumT_ref[h:h + 1, :]
                    dm = jnp.where(tril, jnp.exp(cc[hh] - row), 0.0)
                    dm_t = jnp.where(triu, jnp.exp(row - cc[hh]), 0.0)
                    dym = jnp.where(half, dyp, 0.0)
                    u = _dot(dym, xd, NT) * dm
                    u_t = _dot(xd, dym, NT) * dm_t
                    dxd = dxd + _dot(gmat_t * dm_t, dym)
                    dcc[hh] = dcc[hh] + jnp.sum(u * gmat, axis=1, keepdims=True) - jnp.sum(u_t * gmat_t, axis=1, keepdims=True)
                    dgm = dgm + u
                    dgm_t = dgm_t + u_t
                    dte_c = jnp.exp(lc[hh] - cc[hh])
                    dcc[hh] = dcc[hh] - ddte[hh] * dte_c
                    endc = dcd[hh] * jnp.exp(lc[hh]) + jnp.sum(ddte[hh] * dte_c, axis=0, keepdims=True)
                    dcc[hh] = dcc[hh] + jnp.where(is_last, endc, 0.0)
                    dcum = jnp.where(lane == h, dcc[hh], dcum)
                dxc_ref[:, sl] = dxp + dxd * dt_l
                t4 = dxd * xp
                ddt = jnp.where(lane == 2 * j, jnp.sum(jnp.where(lo, t4, 0.0), axis=1, keepdims=True), ddt)
                ddt = jnp.where(lane == 2 * j + 1, jnp.sum(jnp.where(lo, 0.0, t4), axis=1, keepdims=True), ddt)
            dzg = jnp.concatenate(dz_parts, axis=1)
            dst_ref[:, gsl] = dn_t[:, gsl] * jnp.concatenate(cd_parts, axis=1) + _dot(cg.T, dzg)
            dxc_ref[:, 1280 + g * 128:1408 + g * 128] = _dot(dgm, bg) + _dot(dzg, pg, NT)
            dxc_ref[:, 1024 + g * 128:1152 + g * 128] = _dot(dgm_t, cg) + db_acc
        dla = _suffix_sum_rows(dcum)
        ddt = ddt + dla * a_neg
        dalog_ref[...] += jnp.sum(dla * dt, axis=0, keepdims=True) * a_neg
        ddtr = jnp.where(lane < 16, ddt * _sigmoid(dtin), 0.0)
        ddtr_ref[...] = ddtr.astype(ddtr_ref.dtype)
        ddtb_ref[...] += jnp.sum(ddtr, axis=0, keepdims=True)

        @pl.when(i == nc - 1)
        def _():
            seg = (_iota((1024, 128), 0) // 64 == _iota((1024, 128), 1)).astype(F32)
            acc8 = jnp.broadcast_to(dskacc_ref[...], (8, 1024))
            ddsk_ref[...] = lax.dot_general(acc8, seg, NN, precision=lax.Precision.HIGHEST,
                                            preferred_element_type=F32)

    rev = lambda c: pl.BlockSpec((L, c), lambda i: (nc - 1 - i, 0))
    vec = lambda c: pl.BlockSpec((1, c), lambda i: (0, 0))
    return pl.pallas_call(
        body, name="ssd_bwd", grid=(nc,),
        in_specs=[rev(1024), rev(1024), rev(1024), rev(1536), rev(128),
                  pl.BlockSpec((1, NSTATE, 1024), lambda i: (nc - 1 - i, 0, 0)),
                  vec(128), vec(128), vec(1024), vec(1024)],
        out_specs=[rev(1024), rev(1536), rev(128), vec(1024), pl.BlockSpec((8, 128), lambda i: (0, 0)),
                   vec(128), vec(128)],
        out_shape=[jax.ShapeDtypeStruct((S, 1024), BF16), jax.ShapeDtypeStruct((S, 1536), F32),
                   jax.ShapeDtypeStruct((S, 128), BF16), jax.ShapeDtypeStruct((1, 1024), F32),
                   jax.ShapeDtypeStruct((8, 128), F32), jax.ShapeDtypeStruct((1, 128), F32),
                   jax.ShapeDtypeStruct((1, 128), F32)],
        scratch_shapes=[pltpu.VMEM((NSTATE, 1024), F32), pltpu.VMEM((L, 128), F32), pltpu.VMEM((L, 128), F32),
                        pltpu.VMEM((L, 1024), F32), pltpu.VMEM((1, 1024), F32)],
        compiler_params=_params(("arbitrary",)))(dya, y, z, xc, dtr, prev, dtb, alog, dskl, nw)


def _layer_norm_parts(vg):
    mu = jnp.mean(vg, axis=-1, keepdims=True)
    vc = vg - mu
    rstd = lax.rsqrt(jnp.mean(vc * vc, axis=-1, keepdims=True) + EPS)
    return vc * rstd, rstd


def _gmlp_fwd(u, v, lnw, lnb, ws, bse, tb=512):
    S = u.shape[0]
    tb = min(tb, S)

    def body(u_ref, v_ref, lnw_ref, lnb_ref, ws_ref, bse_ref, o_ref, vn_ref):
        tril = _iota((L, L), 0) >= _iota((L, L), 1)
        xh, _ = _layer_norm_parts(_gelu(v_ref[...]))
        vn_ref[...] = xh * lnw_ref[...] + lnb_ref[...]
        for g in range(8):
            w = jnp.where(tril, ws_ref[g], 0.0)
            gs = slice(g * 128, (g + 1) * 128)
            for ch in range(tb // L):
                rs = slice(ch * L, (ch + 1) * L)
                sv = _dot(w, vn_ref[rs, gs]) + bse_ref[g]
                o_ref[rs, gs] = (_gelu(u_ref[rs, gs]) * sv).astype(o_ref.dtype)

    blk = pl.BlockSpec((tb, 1024), lambda i: (i, 0))
    vec = pl.BlockSpec((1, 1024), lambda i: (0, 0))
    cube = pl.BlockSpec((8, L, 128), lambda i: (0, 0, 0))
    return pl.pallas_call(
        body, name="gmlp_fwd", grid=(S // tb,), in_specs=[blk, blk, vec, vec, cube, cube], out_specs=blk,
        out_shape=jax.ShapeDtypeStruct((S, 1024), BF16), scratch_shapes=[pltpu.VMEM((tb, 1024), F32)],
        compiler_params=_params(("parallel",)))(u, v, lnw, lnb, ws, bse)


def _gmlp_bwd(dyb, u, v, lnw, lnb, ws, bse, tb=512):
    S = u.shape[0]
    tb = min(tb, S)

    def body(d_ref, u_ref, v_ref, lnw_ref, lnb_ref, ws_ref, bse_ref,
             du_ref, dv_ref, dws_ref, dbse_ref, dlnw_ref, dlnb_ref, vn_ref, dvn_ref):
        @pl.when(pl.program_id(0) == 0)
        def _():
            dws_ref[...] = jnp.zeros_like(dws_ref)
            dbse_ref[...] = jnp.zeros_like(dbse_ref)
            dlnw_ref[...] = jnp.zeros_like(dlnw_ref)
            dlnb_ref[...] = jnp.zeros_like(dlnb_ref)

        tril = _iota((L, L), 0) >= _iota((L, L), 1)
        vv = v_ref[...]
        xh, rstd = _layer_norm_parts(_gelu(vv))
        vn_ref[...] = xh * lnw_ref[...] + lnb_ref[...]
        for g in range(8):
            w = jnp.where(tril, ws_ref[g], 0.0)
            w_t = w.T
            gs = slice(g * 128, (g + 1) * 128)
            dw = jnp.zeros((L, L), F32)
            dbs = jnp.zeros((L, 128), F32)
            for ch in range(tb // L):
                rs = slice(ch * L, (ch + 1) * L)
                vn = vn_ref[rs, gs]
                sv = _dot(w, vn) + bse_ref[g]
                uu = u_ref[rs, gs]
                dd = d_ref[rs, gs]
                du_ref[rs, gs] = (dd * sv * _gelu_grad(uu)).astype(du_ref.dtype)
                dsv = dd * _gelu(uu)
                dw = dw + _dot(dsv, vn, NT)
                dbs = dbs + dsv
                dvn_ref[rs, gs] = _dot(w_t, dsv)
            dws_ref[g] += jnp.where(tril, dw, 0.0)
            dbse_ref[g] += dbs
        dvn = dvn_ref[...]
        dlnw_ref[...] += jnp.sum(dvn * xh, axis=0, keepdims=True)
        dlnb_ref[...] += jnp.sum(dvn, axis=0, keepdims=True)
        dxh = dvn * lnw_ref[...]
        dvg = rstd * (dxh - jnp.mean(dxh, axis=-1, keepdims=True) - xh * jnp.mean(dxh * xh, axis=-1, keepdims=True))
        dv_ref[...] = (dvg * _gelu_grad(vv)).astype(dv_ref.dtype)

    blk = pl.BlockSpec((tb, 1024), lambda i: (i, 0))
    vec = pl.BlockSpec((1, 1024), lambda i: (0, 0))
    cube = pl.BlockSpec((8, L, 128), lambda i: (0, 0, 0))
    return pl.pallas_call(
        body, name="gmlp_bwd", grid=(S // tb,), in_specs=[blk, blk, blk, vec, vec, cube, cube],
        out_specs=[blk, blk, cube, cube, vec, vec],
        out_shape=[jax.ShapeDtypeStruct((S, 1024), BF16), jax.ShapeDtypeStruct((S, 1024), BF16),
                   jax.ShapeDtypeStruct((8, L, 128), F32), jax.ShapeDtypeStruct((8, L, 128), F32),
                   jax.ShapeDtypeStruct((1, 1024), F32), jax.ShapeDtypeStruct((1, 1024), F32)],
        scratch_shapes=[pltpu.VMEM((tb, 1024), F32), pltpu.VMEM((tb, 1024), F32)],
        compiler_params=_params(("arbitrary",)))(dyb, u, v, lnw, lnb, ws, bse)


def _lane_sum(name, a):
    def body(a_ref, o_ref):
        o_ref[...] = jnp.sum(a_ref[...], axis=1, keepdims=True)
    return pl.pallas_call(body, name=name, out_shape=jax.ShapeDtypeStruct((a.shape[0], 1), F32))(a)


def _bucket_onehot_t():
    qi = np.arange(L)[:, None]
    sj = np.arange(2 * L)[None, :]
    dist = np.maximum(qi + L - sj, 0)
    log_ratio = (np.log(np.maximum(dist, 1).astype(np.float32) / np.float32(16)) / np.float32(math.log(128 / 16)))
    large = 16 + (log_ratio.astype(np.float32) * np.float32(16)).astype(np.int32)
    bucket = np.where(dist < 16, dist, np.minimum(large, 31)).reshape(-1)
    return (np.arange(32)[:, None] == bucket[None, :]).astype(np.float32)


def _rel_bias(table_t, onehot_t):
    def body(t_ref, oh_ref, o_ref):
        o_ref[...] = lax.dot_general(t_ref[...], oh_ref[...], NN, precision=lax.Precision.HIGHEST,
                                     preferred_element_type=F32)
    return pl.pallas_call(body, name="rel_bias", out_shape=jax.ShapeDtypeStruct((16, L * 2 * L), F32),
                          compiler_params=_params())(table_t, onehot_t)


def _rel_bias_bwd(dbias, onehot_t):
    def body(d_ref, oh_ref, o_ref):
        o_ref[...] = lax.dot_general(d_ref[...], oh_ref[...], NT, precision=lax.Precision.HIGHEST,
                                     preferred_element_type=F32)
    return pl.pallas_call(body, name="rel_bias_bwd", out_shape=jax.ShapeDtypeStruct((16, 32), F32),
                          compiler_params=_params())(dbias, onehot_t)


def _band(kp, kc, lo):
    kk = jnp.concatenate([kp, kc], axis=0)
    kr = pltpu.roll(kk, 64, axis=1)
    return [jnp.where(lo, kk, kr), jnp.where(lo, kr, kk)]


def _attn_rows(ref, j, lo):
    parts = []
    for t in range(8):
        pair = ref[:, (4 * j + t // 2) * 128:(4 * j + t // 2 + 1) * 128]
        parts.append(jnp.where(lo if t % 2 == 0 else jnp.logical_not(lo), pair, 0.0))
    return jnp.concatenate(parts, axis=0)


def _attn_mask(i, rows):
    qi, sj = _iota((rows, 2 * L), 0) & (L - 1), _iota((rows, 2 * L), 1)
    rel = qi + L - sj
    return (rel >= 0) & (rel < L) & ((sj >= L) | (i > 0))


def _per_head_col(vals):
    return jnp.concatenate([jnp.broadcast_to(v, (L, 1)) for v in vals], axis=0)


SMEM = pl.BlockSpec(memory_space=pltpu.SMEM)


def _attn_fwd(qkv, bias, sinks):
    S = qkv.shape[0]
    nb = S // L
    scale = 64 ** -0.5

    def body(sink_ref, q_ref, kc_ref, vc_ref, kp_ref, vp_ref, bias_ref, o_ref, lse_ref):
        i = pl.program_id(0)
        lane = _iota((L, 128), 1)
        lo = lane < 64
        lo2 = _iota((2 * L, 128), 1) < 64
        mask = _attn_mask(i, L)
        kd = _band(kp_ref[...], kc_ref[...], lo2)
        vd = _band(vp_ref[...], vc_ref[...], lo2)
        lse = jnp.zeros((L, 128), F32)
        for pr in range(8):
            sl = slice(pr * 128, (pr + 1) * 128)
            qp = q_ref[:, sl]
            j = pr // 4
            outs = []
            for hh in range(2):
                h = 2 * pr + hh
                qm = jnp.where(lo if hh == 0 else jnp.logical_not(lo), qp, 0.0)
                lg = jnp.where(mask, _dot(qm, kd[j], NT) * scale + bias_ref[h], NEG_INF)
                s = sink_ref[h]
                m = jnp.maximum(jnp.max(lg, axis=1, keepdims=True), s)
                p = jnp.where(mask, jnp.exp(lg - m), 0.0)
                den = jnp.sum(p, axis=1, keepdims=True) + jnp.exp(s - m)
                outs.append(_dot(p * (1.0 / den), vd[j]))
                lse = jnp.where(lane == h, m + jnp.log(den), lse)
            o_ref[:, sl] = jnp.where(lo, outs[0], outs[1]).astype(o_ref.dtype)
        lse_ref[...] = lse

    prev = lambda col: pl.BlockSpec((L, 128), lambda i: (jnp.maximum(i - 1, 0), col))
    cur = lambda col: pl.BlockSpec((L, 128), lambda i: (i, col))
    return pl.pallas_call(
        body, name="attn_fwd", grid=(nb,),
        in_specs=[SMEM, pl.BlockSpec((L, 1024), lambda i: (i, 0)), cur(8), cur(9), prev(8), prev(9),
                  pl.BlockSpec((16, L, 2 * L), lambda i: (0, 0, 0))],
        out_specs=[pl.BlockSpec((L, 1024), lambda i: (i, 0)), pl.BlockSpec((L, 128), lambda i: (i, 0))],
        out_shape=[jax.ShapeDtypeStruct((S, 1024), BF16), jax.ShapeDtypeStruct((S, 128), F32)],
        compiler_params=_params(("parallel",)))(sinks, qkv, qkv, qkv, qkv, qkv, bias)


def _attn_bwd(qkv, d_o, lse, bias, sinks):
    S = qkv.shape[0]
    nb = S // L
    scale = 64 ** -0.5

    def body(sink_ref, q_ref, kc_ref, vc_ref, kp_ref, vp_ref, do_ref, lse_ref, bias_ref,
             dq_ref, dkv_ref, dbias_ref, dsink_ref, dbq_ref, dbkv_ref, carry_ref):
        i = pl.program_id(0)

        @pl.when(i == 0)
        def _():
            dbias_ref[...] = jnp.zeros_like(dbias_ref)
            dsink_ref[...] = jnp.zeros_like(dsink_ref)
            dbq_ref[...] = jnp.zeros_like(dbq_ref)
            dbkv_ref[...] = jnp.zeros_like(dbkv_ref)
            carry_ref[...] = jnp.zeros_like(carry_ref)

        @pl.when(i < nb)
        def _():
            lane = _iota((L, 128), 1)
            lane1 = _iota((1, 128), 1)
            lo = lane < 64
            lo2 = _iota((2 * L, 128), 1) < 64
            mask = _attn_mask(i, 8 * L)
            kd = _band(kp_ref[...], kc_ref[...], lo2)
            vd = _band(vp_ref[...], vc_ref[...], lo2)
            lse_all = lse_ref[...]
            dsink = jnp.zeros((1, 128), F32)
            tot_k, tot_v = [], []
            for j in range(2):
                q_all = _attn_rows(q_ref, j, lo)
                do_all = _attn_rows(do_ref, j, lo)
                lse_col = _per_head_col([_colsel(lse_all, lane, 8 * j + t) for t in range(8)])
                lg = _dot(q_all, kd[j], NT) * scale + bias_ref[8 * j:8 * j + 8].reshape(8 * L, 2 * L)
                p = jnp.where(mask, jnp.exp(jnp.where(mask, lg, NEG_INF) - lse_col), 0.0)
                dp = _dot(do_all, vd[j], NT)
                delta = jnp.sum(p * dp, axis=1, keepdims=True)
                ds = p * (dp - delta)
                dbias_ref[8 * j:8 * j + 8] += ds.reshape(8, L, 2 * L)
                s = _per_head_col([sink_ref[8 * j + t] for t in range(8)])
                sink_part = -jnp.exp(s - lse_col) * delta
                for t in range(8):
                    dsink = dsink + jnp.where(lane1 == 8 * j + t,
                                              jnp.sum(sink_part[t * L:(t + 1) * L], axis=0, keepdims=True), 0.0)
                dss = ds * scale
                dq_all = _dot(dss, kd[j])
                for t in range(0, 8, 2):
                    sl = slice((4 * j + t // 2) * 128, (4 * j + t // 2 + 1) * 128)
                    dq = jnp.where(lo, dq_all[t * L:(t + 1) * L], dq_all[(t + 1) * L:(t + 2) * L])
                    dq_ref[:, sl] = dq.astype(dq_ref.dtype)
                    dbq_ref[:, sl] += jnp.sum(dq, axis=0, keepdims=True)
                acc_k = _dot(dss, q_all, TN)
                acc_v = _dot(p, do_all, TN)
                tot_k.append(acc_k + pltpu.roll(acc_k, 64, axis=1))
                tot_v.append(acc_v + pltpu.roll(acc_v, 64, axis=1))
            dsink_ref[...] += dsink
            dkv = jnp.concatenate([jnp.where(lo2, tot_k[0], tot_k[1]), jnp.where(lo2, tot_v[0], tot_v[1])], axis=1)
            dbkv_ref[...] += jnp.sum(dkv, axis=0, keepdims=True)
            dkv_ref[...] = (carry_ref[...] + dkv[:L, :]).astype(dkv_ref.dtype)
            carry_ref[...] = dkv[L:, :]

        @pl.when(i == nb)
        def _():
            dkv_ref[...] = carry_ref[...].astype(dkv_ref.dtype)

    c = lambda i: jnp.minimum(i, nb - 1)
    prev = lambda col: pl.BlockSpec((L, 128), lambda i: (jnp.maximum(c(i) - 1, 0), col))
    cur = lambda col: pl.BlockSpec((L, 128), lambda i: (c(i), col))
    row = lambda w: pl.BlockSpec((L, w), lambda i: (c(i), 0))
    cube = pl.BlockSpec((16, L, 2 * L), lambda i: (0, 0, 0))
    vec = lambda w: pl.BlockSpec((1, w), lambda i: (0, 0))
    return pl.pallas_call(
        body, name="attn_bwd", grid=(nb + 1,),
        in_specs=[SMEM, row(1024), cur(8), cur(9), prev(8), prev(9), row(1024), row(128), cube],
        out_specs=[row(1024), pl.BlockSpec((L, 256), lambda i: (jnp.maximum(i - 1, 0), 0)), cube,
                   vec(128), vec(1024), vec(256)],
        out_shape=[jax.ShapeDtypeStruct((S, 1024), BF16), jax.ShapeDtypeStruct((S, 256), BF16),
                   jax.ShapeDtypeStruct((16, L, 2 * L), F32), jax.ShapeDtypeStruct((1, 128), F32),
                   jax.ShapeDtypeStruct((1, 1024), F32), jax.ShapeDtypeStruct((1, 256), F32)],
        scratch_shapes=[pltpu.VMEM((L, 256), F32)],
        compiler_params=_params(("arbitrary",)))(sinks, qkv, qkv, qkv, qkv, qkv, d_o, lse, bias)


def _pad_lanes(a, n=128):
    return jnp.pad(a, ((0, 0), (0, n - a.shape[1])))


def _local_step(x, tgt, mod, w_in, P, io):
    md = [[mod[l:l + 1, k * D:(k + 1) * D] for k in range(6)] for l in range(2)]
    G, g = {}, {}

    sh1, sc1, g1, sh2, sc2, g2 = md[0]
    nmw0, nfw0 = P["norm_mix_w"][0:1], P["norm_ffn_w"][0:1]
    h0 = _norm_mod_fwd("norm_mix_0", x, nmw0, sc1, sh1, after=io["start"])
    segs = {"z": w_in[0:1024], "xbc": w_in[1024:2560], "dt": jnp.pad(w_in[2560:2576], ((0, 112), (0, 0))),
            "u": w_in[2576:3600], "v": w_in[3600:4624]}
    proj = dict(zip(segs, _mm_shared_lhs("in_proj", h0, list(segs.values()))))
    conv_w, conv_b = P["conv_w"][0], P["conv_b"]
    pre, xc = _conv_fwd(proj["xbc"], conv_w, conv_b)
    dtb, alog = _pad_lanes(P["dt_bias"]), _pad_lanes(P["a_log"])
    dskl = jnp.repeat(P["d_skip"], 64, axis=1)
    ya, y_ssd, prev = _ssd_fwd(xc, proj["dt"], proj["z"], dtb, alog, dskl, P["ssm_norm_w"])
    ws = P["gmlp_ws"][0]
    bse = jnp.broadcast_to(P["gmlp_bs"][0][:, :, None], (8, L, 128))
    yb = _gmlp_fwd(proj["u"], proj["v"], P["gmlp_ln_w"], P["gmlp_ln_b"], ws, bse)
    W = dict(io["weights0"]((ya, yb)))
    w_oa, w_ob = W["out_w"][:1024], W["out_w"][1024:]

    def res(y, x, gate, nw, sc, sh):
        xo = x + gate * y
        return y, xo, _norm_mod(xo, nw, sc, sh)
    mix0, x1, h0f = _mm("out_proj_0", [ya, yb], [w_oa, w_ob], "nn", [F32, F32, BF16], epi=res, extras=[x],
                        vecs=[g1, nfw0, sc2, sh2], whole_rows=True)
    sh1b, sc1b, g1b, sh2b, sc2b, g2b = md[1]
    nmw1, nfw1 = P["norm_mix_w"][1:2], P["norm_ffn_w"][1:2]
    a0, b0, f0, y0, x2, h1 = _ffn_fwd("0", h0f, W["gate_wt0"], W["up_wt0"], W["down_w0"], x1, g2,
                                      next_norm=(nmw1, sc1b, sh1b))

    W.update(io["weights1"](x2))
    qkv = _mm("qkv_proj", [h1], [W["qkv_wt"]], "nt", [F32], epi=lambda acc, b: acc + b, vecs=[P["qkv_b"]])[0]
    onehot_t = jnp.asarray(_bucket_onehot_t())
    bias = _rel_bias(P["rel_table"].T, onehot_t).reshape(16, L, 2 * L)
    sinks = P["sinks"].reshape(16)
    att, lse = _attn_fwd(qkv, bias, sinks)

    def res_b(y, x, gate, b, nw, sc, sh):
        y = y + b
        xo = x + gate * y
        return y, xo, _norm_mod(xo, nw, sc, sh)
    mix1, x3, h1f = _mm("o_proj", [att], [W["o_w"]], "nn", [F32, F32, BF16], epi=res_b, extras=[x2],
                        vecs=[g1b, P["o_b"], nfw1, sc2b, sh2b], whole_rows=True)
    a1, b1, f1, y1, x4, _ = _ffn_fwd("1", h1f, W["gate_wt1"], W["up_wt1"], W["down_w1"], x3, g2b)

    dx, dy, sq, g["final_norm_w"], dg2b = _loss_head(x4, tgt, P["final_norm_w"], y1, g2b)

    dh, dwg1, dwu1, dwd1 = _ffn_bwd("1", dy, h1f, a1, b1, f1, W["gate_wt1"], W["up_wt1"], W["down_w1"])
    dx, dmix, dsh2b, dsc2b, dnfw1, dg1b, g["o_b"] = _norm_mod_bwd("norm_ffn_bwd_1", x3, dh, dx, nfw1, sc2b,
                                                                 gate=(mix1, g1b))
    G["o_w"] = _mm_tn("o_dw", att, dmix)
    d_att = _mm("o_dx", [dmix], [W["o_w"]], "nt", [F32])[0]
    dq, dkv, dbias, dsinks, dbq, dbkv = _attn_bwd(qkv, d_att, lse, bias, sinks)
    g["rel_table"] = _rel_bias_bwd(dbias.reshape(16, L * 2 * L), onehot_t).T
    g["sinks"] = dsinks[:, :16]
    g["qkv_b"] = jnp.concatenate([dbq, dbkv], axis=1)
    w_q, w_kv = W["qkv_wt"][:1024], W["qkv_wt"][1024:]
    G["qkv_wt"] = jnp.concatenate([_mm_tn("qkv_dwq", dq, h1), _mm_tn("qkv_dwkv", dkv, h1)], axis=0)
    dh = _mm("qkv_dx", [dq, dkv], [w_q, w_kv], "nn", [F32])[0]
    behind = io["grads1"]({"qkv_wt": G.pop("qkv_wt"), "o_w": G.pop("o_w"), "gate_wt1": dwg1, "up_wt1": dwu1,
                           "down_w1": dwd1})
    dx, dy, dsh1b, dsc1b, dnmw1, dg2, _ = _norm_mod_bwd("norm_mix_bwd_1", x2, dh, dx, nmw1, sc1b, gate=(y0, g2),
                                                        after=behind)

    dh, dwg0, dwu0, dwd0 = _ffn_bwd("0", dy, h0f, a0, b0, f0, W["gate_wt0"], W["up_wt0"], W["down_w0"])
    behind = io["grads_ffn0"]({"gate_wt0": dwg0, "up_wt0": dwu0, "down_w0": dwd0})
    dx, dmix, dsh2, dsc2, dnfw0, dg1, _ = _norm_mod_bwd("norm_ffn_bwd_0", x1, dh, dx, nfw0, sc2, gate=(mix0, g1),
                                                        after=behind)
    G["out_w"] = jnp.concatenate([_mm_tn("out_dwa", ya, dmix), _mm_tn("out_dwb", yb, dmix)], axis=0)
    dya = _mm("out_dxa", [dmix], [w_oa], "nt", [F32])[0]
    dyb = _mm("out_dxb", [dmix], [w_ob], "nt", [F32])[0]
    du, dv, dws, dbse, g["gmlp_ln_w"], g["gmlp_ln_b"] = _gmlp_bwd(dyb, proj["u"], proj["v"], P["gmlp_ln_w"],
                                                                 P["gmlp_ln_b"], ws, bse)
    g["gmlp_ws"] = dws[None]
    g["gmlp_bs"] = _lane_sum("gmlp_dbs", dbse.reshape(8 * L, 128)).reshape(1, 8, L)
    dz, dxc, ddt, g["ssm_norm_w"], ddsk, dalog, ddtb = _ssd_bwd(dya, y_ssd, proj["z"], xc, proj["dt"], prev,
                                                                dtb, alog, dskl, P["ssm_norm_w"])
    g["d_skip"], g["a_log"], g["dt_bias"] = ddsk[0:1, :16], dalog[:, :16], ddtb[:, :16]
    dxr, dconv_w, g["conv_b"] = _conv_bwd(dxc, pre, proj["xbc"], conv_w)
    g["conv_w"] = dconv_w[None]
    dsegs = {"z": dz, "xbc": dxr, "dt": ddt, "u": du, "v": dv}
    dws_in = dict(zip(dsegs, _mm_tn_shared_rhs("in_dw", list(dsegs.values()), h0)))
    G["in_wt"] = jnp.concatenate([dws_in["z"], dws_in["xbc"], dws_in["dt"][:16], dws_in["u"], dws_in["v"]], axis=0)
    keys = ["z", "xbc", "dt", "u", "v"]
    dh = _mm("in_dx", [dsegs[k] for k in keys], [segs[k] for k in keys], "nn", [F32])[0]
    dx, dsh1, dsc1, dnmw0 = _norm_mod_bwd("norm_mix_bwd_0", x, dh, dx, nmw0, sc1)

    g["norm_mix_w"] = jnp.concatenate([dnmw0, dnmw1], axis=0)
    g["norm_ffn_w"] = jnp.concatenate([dnfw0, dnfw1], axis=0)
    dmod = jnp.concatenate([jnp.concatenate([dsh1, dsc1, dg1, dsh2, dsc2, dg2], axis=1),
                            jnp.concatenate([dsh1b, dsc1b, dg1b, dsh2b, dsc2b, dg2b], axis=1)], axis=0)
    return sq, dx, dmod, G, g


def _ada_fwd(c_all, ada_w, ada_b):
    n = ada_w.shape[2]
    tn = _col_tile(n, 512)

    def body(c_ref, w_ref, b_ref, o_ref):
        cc = c_ref[...]
        o_ref[...] = lax.dot_general(cc * _sigmoid(cc), w_ref[...], NN, precision=lax.Precision.HIGHEST,
                                     preferred_element_type=F32) + b_ref[...]

    return pl.pallas_call(
        body, name="ada_fwd", grid=(2, n // tn),
        in_specs=[pl.BlockSpec((8, D), lambda l, j: (0, 0)), pl.BlockSpec((None, D, tn), lambda l, j: (l, 0, j)),
                  pl.BlockSpec((None, 1, tn), lambda l, j: (l, 0, j))],
        out_specs=pl.BlockSpec((None, 8, tn), lambda l, j: (l, 0, j)),
        out_shape=jax.ShapeDtypeStruct((2, 8, n), F32), compiler_params=_params(("parallel", "parallel")))(
            c_all, ada_w, ada_b)


def _ada_bwd(c_all, dmod_cols, dmod_all):
    n = dmod_cols.shape[2]
    tn = _col_tile(n, 512)

    def body(c_ref, d_ref, o_ref):
        cc = c_ref[...]
        o_ref[...] = lax.dot_general(cc * _sigmoid(cc), d_ref[...], TN, precision=lax.Precision.HIGHEST,
                                     preferred_element_type=F32)

    dw = pl.pallas_call(
        body, name="ada_dw", grid=(2, n // tn),
        in_specs=[pl.BlockSpec((8, D), lambda l, j: (0, 0)), pl.BlockSpec((None, 8, tn), lambda l, j: (l, 0, j))],
        out_specs=pl.BlockSpec((None, D, tn), lambda l, j: (l, 0, j)),
        out_shape=jax.ShapeDtypeStruct((2, D, n), F32), compiler_params=_params(("parallel", "parallel")))(
            c_all, dmod_cols)

    def sum_body(d_ref, o_ref):
        o_ref[...] = jnp.sum(d_ref[...], axis=0, keepdims=True)

    db = pl.pallas_call(
        sum_body, name="ada_db", grid=(2,),
        in_specs=[pl.BlockSpec((None, 8, 6 * D), lambda l: (l, 0, 0))],
        out_specs=pl.BlockSpec((None, 1, 6 * D), lambda l: (l, 0, 0)),
        out_shape=jax.ShapeDtypeStruct((2, 1, 6 * D), F32), compiler_params=_params(("parallel",)))(dmod_all)
    return dw, db


def _row_tile(rows, cap=512, mult=8):
    best = rows
    for t in range(mult, min(rows, cap) + 1, mult):
        if rows % t == 0:
            best = t
    return best


def _adamw(name, w, g, m, v):
    def fn(w, g, m, v):
        m = ADAM_B1 * m + (1.0 - ADAM_B1) * g
        v = ADAM_B2 * v + (1.0 - ADAM_B2) * (g * g)
        m_hat = m / (1.0 - ADAM_B1 ** ADAM_STEP)
        v_hat = v / (1.0 - ADAM_B2 ** ADAM_STEP)
        return -ADAM_LR * (m_hat / (jnp.sqrt(v_hat) + ADAM_EPS) + ADAM_WD * w), m, v
    cols = w.shape[1]
    return _rowwise(name, fn, [w, g, m, v], [], [(cols, F32)] * 3, tr=_row_tile(w.shape[0]))


def _place():
    return lax.axis_index("x"), lax.axis_index("y"), lax.axis_index("c")


VMEM_SPEC = pl.BlockSpec(memory_space=pltpu.VMEM)


def _allreduce_small(name, buf, after=None):
    rows = buf.shape[0]
    deps = [] if after is None else [after]

    def body(x_ref, *rest):
        o_ref, stage, send_sems, recv_sems = rest[len(deps):]
        x, y, c = _place()
        me = 4 * x + 2 * y + c
        stage[me] = x_ref[...]
        copies = []
        for k in range(1, 8):
            peer = (1 - x if k & 4 else x, 1 - y if k & 2 else y, 1 - c if k & 1 else c)
            cp = pltpu.make_async_remote_copy(src_ref=x_ref, dst_ref=stage.at[me], send_sem=send_sems.at[k - 1],
                                              recv_sem=recv_sems.at[k - 1], device_id=peer, device_id_type=MESH)
            cp.start()
            copies.append(cp)
        for cp in copies:
            cp.wait()
        acc = stage[0]
        for d in range(1, 8):
            acc = acc + stage[d]
        o_ref[...] = acc

    return pl.pallas_call(
        body, name=name, in_specs=[VMEM_SPEC] + [ANY for _ in deps], out_specs=VMEM_SPEC,
        out_shape=jax.ShapeDtypeStruct((rows, 128), F32),
        scratch_shapes=[pltpu.VMEM((8, rows, 128), F32), pltpu.SemaphoreType.DMA((7,)), pltpu.SemaphoreType.DMA((7,))],
        compiler_params=pltpu.CompilerParams(vmem_limit_bytes=_VMEM_LIMIT))(buf, *deps)


def _sum_slots(name, own, land):
    def body(own_ref, land_ref, o_ref):
        x, y, c = _place()
        me = 4 * x + 2 * y + c
        acc = None
        for d in range(8):
            v = jnp.where(me == d, own_ref[...], land_ref[d])
            acc = v if acc is None else acc + v
        o_ref[...] = acc

    return pl.pallas_call(body, name=name, in_specs=[VMEM_SPEC, VMEM_SPEC], out_specs=VMEM_SPEC,
                          out_shape=jax.ShapeDtypeStruct(own.shape, F32),
                          compiler_params=pltpu.CompilerParams(vmem_limit_bytes=_VMEM_LIMIT))(own, land)


OTHER_CHIPS = ((1, 0), (0, 1), (1, 1))


SIBLING_COLLECTIVE_ID = 6


def _sibling_handshake():
    x, y, c = _place()
    barrier = pltpu.get_barrier_semaphore()
    pl.semaphore_signal(barrier, inc=1, device_id=(x, y, 1 - c), device_id_type=MESH)
    pl.semaphore_wait(barrier, 1)


def _sibling_swap(name, src, halves):
    half = src.shape[-2] // 2
    out_shape = (src.shape[0], half, 1024) if halves else src.shape

    def body(s_ref, o_ref, send_sem, recv_sem):
        x, y, c = _place()
        _sibling_handshake()
        part = s_ref.at[:, pl.ds(pl.multiple_of((1 - c) * half, 8), half)] if halves else s_ref
        cp = pltpu.make_async_remote_copy(src_ref=part, dst_ref=o_ref, send_sem=send_sem, recv_sem=recv_sem,
                                          device_id=(x, y, 1 - c), device_id_type=MESH)
        cp.start()
        cp.wait()

    return pl.pallas_call(
        body, name=name, in_specs=[ANY], out_specs=ANY, out_shape=jax.ShapeDtypeStruct(out_shape, src.dtype),
        scratch_shapes=[pltpu.SemaphoreType.DMA, pltpu.SemaphoreType.DMA],
        compiler_params=pltpu.CompilerParams(collective_id=SIBLING_COLLECTIVE_ID))(src)


HBM = pl.BlockSpec(memory_space=pltpu.HBM)
SEM = pl.BlockSpec(memory_space=pltpu.SEMAPHORE)


def _exchange_peers(mode):
    x, y, c = _place()
    if mode == "all":
        return [(1 - x if k & 4 else x, 1 - y if k & 2 else y, 1 - c if k & 1 else c) for k in range(1, 8)]
    return [(1 - x if fx else x, 1 - y if fy else y, c) for fx, fy in OTHER_CHIPS]


def _chip_copies(mode, src_ref, land_ref, send_sems, recv_sems):
    x, y, c = _place()
    k = 2 * x + y
    copies = []
    for j, peer in enumerate(_exchange_peers(mode)):
        if mode == "gather":
            half = src_ref.shape[0] // 2
            mine = pl.ds(pl.multiple_of(c * half, 16), half)
            src, dst = src_ref.at[mine], land_ref.at[k, mine]
        elif mode == "scatter":
            src, dst = src_ref.at[2 * peer[0] + peer[1]], land_ref.at[k]
        else:
            src, dst = src_ref, land_ref.at[4 * x + 2 * y + c]
        copies.append(pltpu.make_async_remote_copy(src_ref=src, dst_ref=dst, send_sem=send_sems.at[j],
                                                   recv_sem=recv_sems.at[j], device_id=peer, device_id_type=MESH))
    return copies


def _exchange_start(name, collective_id, mode, src, land, after=None):
    deps = [] if after is None else [after]
    npeers = 7 if mode == "all" else 3

    def body(s_ref, l_ref, *rest):
        send_sems, recv_sems, s_thru, l_thru, token = rest[len(deps):]
        barrier = pltpu.get_barrier_semaphore()
        for peer in _exchange_peers(mode):
            pl.semaphore_signal(barrier, inc=1, device_id=peer, device_id_type=MESH)
        pl.semaphore_wait(barrier, npeers)
        for cp in _chip_copies(mode, s_ref, l_ref, send_sems, recv_sems):
            cp.start()
        token[...] = jnp.zeros_like(token)

    return pl.pallas_call(
        body, name=name,
        out_shape=(pltpu.SemaphoreType.DMA((npeers,)), pltpu.SemaphoreType.DMA((npeers,)),
                   pltpu.HBM(src.shape, src.dtype),
                   pltpu.HBM(land.shape, land.dtype), jax.ShapeDtypeStruct((8, 128), F32)),
        in_specs=(HBM, HBM) + tuple(ANY for _ in deps), out_specs=(SEM, SEM, HBM, HBM, VMEM_SPEC),
        input_output_aliases={0: 2, 1: 3},
        compiler_params=pltpu.CompilerParams(has_side_effects=pltpu.SideEffectType.DATAFLOW_SIDE_EFFECTING,
                                             collective_id=collective_id))(
            pltpu.with_memory_space_constraint(src, pltpu.HBM), pltpu.with_memory_space_constraint(land, pltpu.HBM),
            *deps)


def _exchange_wait(name, mode, started, after):
    send_sems, recv_sems, s_thru, l_thru, _ = started
    deps = list(after) if isinstance(after, (tuple, list)) else [after]

    def body(s_ref, l_ref, send_sems, recv_sems, *rest):
        for cp in _chip_copies(mode, s_ref, l_ref, send_sems, recv_sems):
            cp.wait_send()
            cp.wait_recv()

    return pl.pallas_call(
        body, name=name, out_shape=(pltpu.HBM(s_thru.shape, s_thru.dtype), pltpu.HBM(l_thru.shape, l_thru.dtype)),
        in_specs=(HBM, HBM, SEM, SEM) + tuple(ANY for _ in deps), out_specs=(HBM, HBM),
        input_output_aliases={0: 0, 1: 1},
        compiler_params=pltpu.CompilerParams(has_side_effects=pltpu.SideEffectType.DATAFLOW_SIDE_EFFECTING))(
            s_thru, l_thru, send_sems, recv_sems, *deps)


def _allgather_finish(tag, land):
    half = land.shape[1] // 2

    def body(l_ref, o_ref, send_sem, recv_sem):
        x, y, c = _place()
        _sibling_handshake()
        mine = pl.ds(pl.multiple_of(c * half, 16), half)
        swap = pltpu.make_async_remote_copy(src_ref=o_ref.at[:, mine], dst_ref=o_ref.at[:, mine], send_sem=send_sem,
                                            recv_sem=recv_sem, device_id=(x, y, 1 - c), device_id_type=MESH)
        swap.start()
        swap.wait()

    return pl.pallas_call(
        body, name="allgather_finish_" + tag, in_specs=[ANY], out_specs=ANY, input_output_aliases={0: 0},
        out_shape=jax.ShapeDtypeStruct(land.shape, land.dtype),
        scratch_shapes=[pltpu.SemaphoreType.DMA, pltpu.SemaphoreType.DMA],
        compiler_params=pltpu.CompilerParams(collective_id=SIBLING_COLLECTIVE_ID))(land)


def _pair_sum(tag, g, r1, c):
    rows = g.shape[1]
    half = rows // 2
    th = _row_tile(half, 1408, 16)
    nblk = half // th

    def body(c_ref, g_ref, r_ref, o_ref, o2_ref):
        o_ref[...] = (g_ref[...].astype(F32) + r_ref[...].astype(F32)).astype(o_ref.dtype)
        o2_ref[...] = o_ref[...]

    spec = pl.BlockSpec((None, th, 1024), lambda k, i, c_ref: (k, i, 0))
    grid_spec = pltpu.PrefetchScalarGridSpec(
        num_scalar_prefetch=1, grid=(4, nblk),
        in_specs=[pl.BlockSpec((None, th, 1024), lambda k, i, c_ref: (k, c_ref[0] * nblk + i, 0)), spec],
        out_specs=[spec, spec])
    return pl.pallas_call(body, name="grad_pair_sum_" + tag, grid_spec=grid_spec,
                          out_shape=[jax.ShapeDtypeStruct((4, half, 1024), BF16)] * 2,
                          compiler_params=_params(("parallel", "parallel")))(c, g, r1)


def _chip_sum(tag, q, after=None):
    half = q.shape[1]
    th = _row_tile(half, 704, 16)
    deps = [] if after is None else [after]

    def body(a, b, c, d, *rest):
        rest[-1][...] = ((a[...].astype(F32) + b[...].astype(F32)) + c[...].astype(F32)) + d[...].astype(F32)

    specs = [pl.BlockSpec((None, th, 1024), functools.partial(lambda i, k: (k, i, 0), k=k)) for k in range(4)]
    return pl.pallas_call(body, name="grad_chip_sum_" + tag, grid=(half // th,), in_specs=specs + [ANY for _ in deps],
                          out_specs=pl.BlockSpec((th, 1024), lambda i: (i, 0)),
                          out_shape=jax.ShapeDtypeStruct((half, 1024), F32),
                          compiler_params=_params(("parallel",)))(q, q, q, q, *deps)


def _join_halves(tag, f, r, c):
    half = f.shape[0]
    th = _row_tile(half, 704)
    nblk = half // th

    def body(c_ref, f_ref, r_ref, o_ref):
        mine = (pl.program_id(0) == c_ref[0])
        o_ref[...] = jnp.where(mine, f_ref[...], r_ref[...])

    spec = pl.BlockSpec((th, 1024), lambda h, i, c_ref: (i, 0))
    grid_spec = pltpu.PrefetchScalarGridSpec(
        num_scalar_prefetch=1, grid=(2, nblk), in_specs=[spec, spec],
        out_specs=pl.BlockSpec((th, 1024), lambda h, i, c_ref: (h * nblk + i, 0)))
    return pl.pallas_call(body, name="grad_join_halves_" + tag, grid_spec=grid_spec,
                          out_shape=jax.ShapeDtypeStruct((2 * half, 1024), F32),
                          compiler_params=_params(("parallel", "parallel")))(c, f, r)


BIG_ARGS = ("in_w_even", "out_w_even", "qkv_w", "o_w", "ffn_gate_w", "ffn_up_w", "ffn_down_w")
def _ffn_pieces(layer):
    return tuple((f"{n}{layer}", 704, 704) for n in ("gate_wt", "up_wt", "down_w"))


IN_SLAB = (("in_wt", 1156, 1184),)
LAYER0_REST_SLAB = (("out_w", 512, 512),) + _ffn_pieces(0)
LAYER1_SLAB = (("qkv_wt", 320, 320), ("o_w", 256, 256)) + _ffn_pieces(1)
FFN0_SLAB = _ffn_pieces(0)
MIXER0_SLAB = (("in_wt", 1156, 1280), ("out_w", 512, 512))


def _slab(pieces, spec):
    parts = []
    for name, rows, room in spec:
        p = pieces[name]
        parts.append(jnp.pad(p, [(0, 0)] * (p.ndim - 2) + [(0, room - rows), (0, 0)]) if room > rows else p)
    return jnp.concatenate(parts, axis=-2) if len(parts) > 1 else parts[0]


def _unslab(slab, spec):
    out, off = {}, 0
    for name, rows, room in spec:
        out[name] = slab[..., off:off + rows, :]
        off += room
    return out


def _share_pieces(w):
    return {"in_wt": w["in_w_even"][0].T, "out_w": w["out_w_even"][0], "qkv_wt": w["qkv_w"][0].T, "o_w": w["o_w"][0],
            "gate_wt0": w["ffn_gate_w"][0].T, "gate_wt1": w["ffn_gate_w"][1].T,
            "up_wt0": w["ffn_up_w"][0].T, "up_wt1": w["ffn_up_w"][1].T,
            "down_w0": w["ffn_down_w"][0], "down_w1": w["ffn_down_w"][1]}


def _pieces_to_shares(p):
    return {"in_w_even": p["in_wt"].T[None], "out_w_even": p["out_w"][None], "qkv_w": p["qkv_wt"].T[None],
            "o_w": p["o_w"][None], "ffn_gate_w": jnp.stack([p["gate_wt0"].T, p["gate_wt1"].T]),
            "ffn_up_w": jnp.stack([p["up_wt0"].T, p["up_wt1"].T]),
            "ffn_down_w": jnp.stack([p["down_w0"], p["down_w1"]])}


def _chips_from_full(G, spec):
    return _slab({k: v.reshape(4, -1, D) for k, v in G.items()}, spec)


def _pack_small(parts):
    padded = []
    for p in parts:
        p = p.reshape(-1).astype(F32)
        padded.append(jnp.pad(p, (0, (-p.shape[0]) % 1024)))
    return jnp.concatenate(padded).reshape(-1, 128)


def _unpack_small(slab, shapes):
    flat, out, off = slab.reshape(-1), [], 0
    for shp in shapes:
        size = math.prod(shp)
        out.append(flat[off:off + size].reshape(shp))
        off += size + (-size) % 1024
    return out


SMALL = ("ada_b", "norm_mix_w", "norm_ffn_w", "conv_w", "conv_b", "dt_bias", "a_log", "d_skip", "ssm_norm_w",
         "gmlp_ln_w", "gmlp_ln_b", "gmlp_ws", "gmlp_bs", "qkv_b", "o_b", "sinks", "rel_table", "final_norm_w")
SMALL_SPLIT = {"conv_w": 1536, "qkv_b": 1280, "o_b": 1024}
WEIGHTS = ("ada_w", "ada_b", "norm_mix_w", "norm_ffn_w", "in_w_even", "conv_w", "conv_b", "dt_bias", "a_log", "d_skip",
           "ssm_norm_w", "gmlp_ln_w", "gmlp_ln_b", "gmlp_ws", "gmlp_bs", "out_w_even", "qkv_w", "qkv_b", "o_w", "o_b",
           "sinks", "rel_table", "ffn_gate_w", "ffn_up_w", "ffn_down_w", "final_norm_w")


def kernel(x, c, ada_w, ada_b, norm_mix_w, norm_ffn_w, in_w_even, conv_w, conv_b, dt_bias, a_log, d_skip, ssm_norm_w, gmlp_ln_w, gmlp_ln_b, gmlp_ws, gmlp_bs, out_w_even, qkv_w, qkv_b, o_w, o_b, sinks, rel_table, ffn_gate_w, ffn_up_w, ffn_down_w, final_norm_w, loss_target, m_ada_w, m_ada_b, m_norm_mix_w, m_norm_ffn_w, m_in_w_even, m_conv_w, m_conv_b, m_dt_bias, m_a_log, m_d_skip, m_ssm_norm_w, m_gmlp_ln_w, m_gmlp_ln_b, m_gmlp_ws, m_gmlp_bs, m_out_w_even, m_qkv_w, m_qkv_b, m_o_w, m_o_b, m_sinks, m_rel_table, m_ffn_gate_w, m_ffn_up_w, m_ffn_down_w, m_final_norm_w, v_ada_w, v_ada_b, v_norm_mix_w, v_norm_ffn_w, v_in_w_even, v_conv_w, v_conv_b, v_dt_bias, v_a_log, v_d_skip, v_ssm_norm_w, v_gmlp_ln_w, v_gmlp_ln_b, v_gmlp_ws, v_gmlp_bs, v_out_w_even, v_qkv_w, v_qkv_b, v_o_w, v_o_b, v_sinks, v_rel_table, v_ffn_gate_w, v_ffn_up_w, v_ffn_down_w, v_final_norm_w):
    args = dict(locals())
    w = {n: args[n] for n in WEIGHTS}
    m = {n: args["m_" + n] for n in WEIGHTS}
    v = {n: args["v_" + n] for n in WEIGHTS}
    ax, ay, ac = _place()
    me = 4 * ax + 2 * ay + ac
    chip = 2 * ax + ay
    south = (ac == 0).astype(F32)
    c_arr = jnp.reshape(ac, (1,)).astype(jnp.int32)

    c_all = _allreduce_small("gather_cond", lax.dynamic_update_slice(jnp.zeros((8, D), F32), c, (me, 0)).reshape(64, 128))
    c_all = c_all.reshape(8, D)
    n_ada = ada_w.shape[2]
    mod_cols = _ada_fwd(c_all, ada_w, lax.dynamic_slice(ada_b, (0, chip * n_ada), (2, n_ada)).reshape(2, 1, n_ada))
    pieces = [lax.dynamic_update_slice(jnp.zeros((2, 8, 6 * D), F32), mod_cols, (0, 0, chip * n_ada))]
    split_names = list(SMALL_SPLIT)
    for n in split_names:
        full = SMALL_SPLIT[n]
        local = w[n]
        idx = (0,) * (local.ndim - 1) + (chip * local.shape[-1],)
        pieces.append(lax.dynamic_update_slice(jnp.zeros(local.shape[:-1] + (full,), F32), local, idx))
    shapes = [p.shape for p in pieces]
    mod_own = _pack_small(pieces) * south
    mod_started = _exchange_start("gather_mod_start", 9, "all", mod_own, lax.empty((8,) + mod_own.shape, F32))

    pieces = _share_pieces(w)
    cast = {"in_wt": pieces["in_wt"].astype(_MXU)}

    def start_gather(tag, collective_id, share, after):
        return _exchange_start("allgather_start_" + tag, collective_id, "gather", share,
                               lax.empty((4,) + share.shape, share.dtype), after=after)

    def finish_gather(tag, started, spec, after):
        land = _exchange_wait("allgather_wait_" + tag, "gather", started, after)[1]
        out = {}
        for name, piece in _unslab(_allgather_finish(tag, land), spec).items():
            out[name] = lax.dynamic_update_slice(piece.reshape(-1, D), cast[name], (chip * piece.shape[1], 0))
        return out

    gather_in = start_gather("in", 7, _slab(cast, IN_SLAB), mod_started[4])
    zero = gather_in[4][0, 0]
    cast.update({k: (p + zero).astype(_MXU) for k, p in pieces.items() if k != "in_wt"})
    share0, share1 = _slab(cast, LAYER0_REST_SLAB), _slab(cast, LAYER1_SLAB)
    mod_own, mod_land = _exchange_wait("gather_mod_wait", "all", mod_started, (share0, share1))
    mod_slab = _sum_slots("gather_mod_sum", mod_own, mod_land)
    gathered = _unpack_small(mod_slab, shapes)
    mod = lax.dynamic_slice(gathered[0], (0, me, 0), (2, 1, 6 * D)).reshape(2, 6 * D)
    P = {n: w[n] for n in SMALL if n not in SMALL_SPLIT and n != "ada_b"}
    for n, full in zip(split_names, gathered[1:]):
        P[n] = full
    P["final_norm_w"] = final_norm_w.reshape(1, D)
    w_in = finish_gather("in", gather_in, IN_SLAB, mod_slab)["in_wt"]
    gather0 = start_gather("0", 1, share0, w_in)
    gather1 = start_gather("1", 2, share1, gather0[4])

    def start_reduce(tag, collective_id, G, spec, after=None):
        gp = _chips_from_full(G, spec).astype(BF16)
        p, q = _pair_sum(tag, gp, _sibling_swap("grad_pair_exchange_" + tag, gp, True), c_arr)
        return _exchange_start("grad_exchange_start_" + tag, collective_id, "scatter", p, q, after=after)

    def finish_reduce(tag, started, spec, after, behind=None):
        q = _exchange_wait("grad_exchange_wait_" + tag, "scatter", started, after)[1]
        fin = _chip_sum(tag, q, after=behind)
        total = _join_halves(tag, fin, _sibling_swap("grad_final_exchange_" + tag, fin, False), c_arr)
        return _unslab(total, spec)

    reduces = {}

    def grads1(G1):
        reduces["1"] = start_reduce("1", 3, G1, LAYER1_SLAB)
        return reduces["1"][4]

    def grads_ffn0(G):
        reduces["f"] = start_reduce("f", 4, G, FFN0_SLAB)
        return reduces["f"][4]

    io = {"start": gather1[4],
          "weights0": lambda after: finish_gather("0", gather0, LAYER0_REST_SLAB, after),
          "weights1": lambda after: finish_gather("1", gather1, LAYER1_SLAB, after),
          "grads1": grads1, "grads_ffn0": grads_ffn0}
    sq, grad_x, dmod, G0, g = _local_step(x[0], loss_target[0], mod, w_in, P, io)
    loss = lax.psum(0.5 * sq[0, 0] / D, ("x", "y", "c"))

    g["final_norm_w"] = g["final_norm_w"].reshape(D)
    small_names = [n for n in SMALL if n != "ada_b"]
    pieces = [lax.dynamic_update_slice(jnp.zeros((2, 8, 6 * D), F32), dmod.reshape(2, 1, 6 * D), (0, me, 0))]
    pieces += [g[n] for n in small_names]
    shapes = [p.shape for p in pieces]
    small_own = _pack_small(pieces)
    small_started = _exchange_start("small_grads_start", 8, "all", small_own, lax.empty((8,) + small_own.shape, F32))
    reduces["m"] = start_reduce("m", 5, G0, MIXER0_SLAB, after=small_started[4])
    shares = finish_reduce("1", reduces["1"], LAYER1_SLAB, grad_x, behind=reduces["m"][4])
    shares.update(finish_reduce("f", reduces["f"], FFN0_SLAB, grad_x, behind=reduces["m"][4]))
    small_own, small_land = _exchange_wait("small_grads_wait", "all", small_started, shares["down_w0"])
    reduced = _unpack_small(_sum_slots("small_grads_sum", small_own, small_land), shapes)
    dmod_all = reduced[0]
    grads = dict(zip(small_names, reduced[1:]))
    for n in split_names:
        full = grads[n]
        size = w[n].shape[-1]
        grads[n] = lax.dynamic_slice(full, (0,) * (full.ndim - 1) + (chip * size,), full.shape[:-1] + (size,))
    grads = {n: grads[n].reshape(w[n].shape) for n in small_names}
    dw_ada, db_ada = _ada_bwd(c_all, lax.dynamic_slice(dmod_all, (0, 0, chip * n_ada), (2, 8, n_ada)), dmod_all)
    grads["ada_w"], grads["ada_b"] = dw_ada, db_ada.reshape(2, 6 * D)

    delta, new_m, new_v = {}, {}, {}

    def update(n):
        cols = w[n].shape[-1]
        d_, m_, v_ = _adamw("adamw_" + n, w[n].reshape(-1, cols), grads[n].reshape(-1, cols), m[n].reshape(-1, cols),
                            v[n].reshape(-1, cols))
        delta[n], new_m[n], new_v[n] = d_.reshape(w[n].shape), m_.reshape(w[n].shape), v_.reshape(w[n].shape)

    update("ada_w")
    shapes = [w[n].shape for n in SMALL]
    packed = [_pack_small([t[n] for n in SMALL]) for t in (w, grads, m, v)]
    outs = _adamw("adamw_small", *packed)
    for dst, slab in zip((delta, new_m, new_v), outs):
        for n, t in zip(SMALL, _unpack_small(slab, shapes)):
            dst[n] = t
    shares.update(finish_reduce("m", reduces["m"], MIXER0_SLAB, outs[0]))
    grads.update(_pieces_to_shares(shares))
    for n in BIG_ARGS:
        update(n)
    return (loss, grad_x[None], *[grads[n] for n in WEIGHTS], *[delta[n] for n in WEIGHTS],
            *[new_m[n] for n in WEIGHTS], *[new_v[n] for n in WEIGHTS])
```

```python
import functools
import math

import numpy as np
import jax
import jax.numpy as jnp
from jax import lax
from jax.experimental import pallas as pl
from jax.experimental.pallas import tpu as pltpu

F32 = jnp.float32
BF16 = jnp.bfloat16
_MXU = jnp.bfloat16
_VMEM_LIMIT = 56 * 1024 * 1024
MXU_COLS = 256
D = 1024
L = 128
NSTATE = 128
EPS = 1e-6
NEG_INF = -1e30
FFN = 2816
ADAM_LR, ADAM_B1, ADAM_B2, ADAM_EPS, ADAM_WD, ADAM_STEP = 0.001, 0.9, 0.999, 1e-08, 0.01, 10
MESH = pl.DeviceIdType.MESH
ANY = pl.BlockSpec(memory_space=pl.ANY)

NN = (((1,), (0,)), ((), ()))
NT = (((1,), (1,)), ((), ()))
TN = (((0,), (0,)), ((), ()))


def _dot(a, b, dn=NN):
    return lax.dot_general(a.astype(_MXU), b.astype(_MXU), dn, preferred_element_type=F32)


def _params(sem=None):
    return pltpu.CompilerParams(dimension_semantics=sem, vmem_limit_bytes=_VMEM_LIMIT)


def _sigmoid(x):
    return 1.0 / (1.0 + jnp.exp(-x))


def _softplus(x):
    return jnp.maximum(x, 0.0) + jnp.log(1.0 + jnp.exp(-jnp.abs(x)))


def _gelu(x):
    return 0.5 * x * (1.0 + lax.erf(x * (2.0 ** -0.5)))


def _gelu_grad(x):
    return 0.5 * (1.0 + lax.erf(x * (2.0 ** -0.5))) + x * jnp.exp(-0.5 * x * x) * (1.0 / math.sqrt(2.0 * math.pi))


def _silu_grad(a):
    sg = _sigmoid(a)
    return sg * (1.0 + a * (1.0 - sg))


def _rowwise(name, fn, rows, vecs, out_rows, out_accs=(), tr=512, after=None):
    S = rows[0].shape[0]
    tr = min(tr, S)
    assert S % tr == 0
    nr, nv, no, na = len(rows), len(vecs), len(out_rows), len(out_accs)
    deps = [] if after is None else [after]

    def body(*refs):
        ins, outs = refs[:nr + nv], refs[nr + nv + len(deps):]
        res = fn(*[r[...] for r in ins])
        if not isinstance(res, (tuple, list)):
            res = (res,)
        for k in range(no):
            outs[k][...] = res[k].astype(outs[k].dtype)
        if na:
            @pl.when(pl.program_id(0) == 0)
            def _():
                for k in range(na):
                    outs[no + k][...] = jnp.zeros_like(outs[no + k])
            for k in range(na):
                outs[no + k][...] += res[no + k]

    in_specs = [pl.BlockSpec((tr, a.shape[1]), lambda i: (i, 0)) for a in rows]
    in_specs += [pl.BlockSpec(v.shape, lambda i: (0, 0)) for v in vecs] + [ANY for _ in deps]
    out_specs = [pl.BlockSpec((tr, c), lambda i: (i, 0)) for c, _ in out_rows]
    out_specs += [pl.BlockSpec(s, lambda i: (0, 0)) for s in out_accs]
    out_shape = [jax.ShapeDtypeStruct((S, c), dt) for c, dt in out_rows]
    out_shape += [jax.ShapeDtypeStruct(s, F32) for s in out_accs]
    return pl.pallas_call(body, name=name, grid=(S // tr,), in_specs=in_specs, out_specs=out_specs,
                          out_shape=out_shape, compiler_params=_params(("arbitrary",)))(*rows, *vecs, *deps)


def _col_tile(n, cap):
    if n <= cap or n % 128:
        return n
    best = 128
    for t in range(128, cap + 1, 128):
        if n % t == 0:
            best = t
    return best


def _mm(name, As, Bs, mode, outs, epi=None, groups=None, extras=(), vecs=(), tm=512, tn_cap=1536, whole_rows=False):
    M = As[0].shape[0]
    N = Bs[0].shape[1] if mode == "nn" else Bs[0].shape[0]
    tm = min(tm, M)
    tn = _col_tile(N, tn_cap)
    assert M % tm == 0 and N % tn == 0
    npair = len(As)
    groups = groups or [0] * npair
    ng = max(groups) + 1
    nx, nv = len(extras), len(vecs)
    dn = NN if mode == "nn" else NT

    def body(*refs):
        a_refs, b_refs = refs[:npair], refs[npair:2 * npair]
        x_refs = refs[2 * npair:2 * npair + nx]
        v_refs = refs[2 * npair + nx:2 * npair + nx + nv]
        o_refs = refs[2 * npair + nx + nv:]
        step = tn if (epi is None or whole_rows) else min(tn, MXU_COLS)
        for col in range(0, tn, step):
            sl = slice(col, min(col + step, tn))
            accs = [None] * ng
            for k in range(npair):
                b = b_refs[k][:, sl] if mode == "nn" else b_refs[k][sl, :]
                d = _dot(a_refs[k][...], b, dn)
                accs[groups[k]] = d if accs[groups[k]] is None else accs[groups[k]] + d
            args = accs + [x[:, sl] for x in x_refs] + [v[:, sl] for v in v_refs]
            res = epi(*args) if epi is not None else tuple(accs)
            if not isinstance(res, (tuple, list)):
                res = (res,)
            for o, r in zip(o_refs, res):
                o[:, sl] = r.astype(o.dtype)

    in_specs = [pl.BlockSpec((tm, a.shape[1]), lambda i, j: (i, 0)) for a in As]
    if mode == "nn":
        in_specs += [pl.BlockSpec((b.shape[0], tn), lambda i, j: (0, j)) for b in Bs]
    else:
        in_specs += [pl.BlockSpec((tn, b.shape[1]), lambda i, j: (j, 0)) for b in Bs]
    in_specs += [pl.BlockSpec((tm, tn), lambda i, j: (i, j)) for _ in extras]
    in_specs += [pl.BlockSpec((1, tn), lambda i, j: (0, j)) for _ in vecs]
    out_specs = [pl.BlockSpec((tm, tn), lambda i, j: (i, j)) for _ in outs]
    out_shape = [jax.ShapeDtypeStruct((M, N), dt) for dt in outs]
    return pl.pallas_call(body, name=name, grid=(M // tm, N // tn), in_specs=in_specs, out_specs=out_specs,
                          out_shape=out_shape, compiler_params=_params(("parallel", "parallel")))(
                              *As, *Bs, *extras, *vecs)


def _mm_shared_lhs(name, A, Bs, tm=512):
    M, K = A.shape
    tm = min(tm, M)
    assert M % tm == 0
    n = len(Bs)

    def body(a_ref, *refs):
        a = a_ref[...]
        for b_ref, o_ref in zip(refs[:n], refs[n:]):
            o_ref[...] = _dot(a, b_ref[...], NT)

    return pl.pallas_call(
        body, name=name, grid=(M // tm,),
        in_specs=[pl.BlockSpec((tm, K), lambda i: (i, 0))] + [pl.BlockSpec(b.shape, lambda i: (0, 0)) for b in Bs],
        out_specs=[pl.BlockSpec((tm, b.shape[0]), lambda i: (i, 0)) for b in Bs],
        out_shape=[jax.ShapeDtypeStruct((M, b.shape[0]), F32) for b in Bs],
        compiler_params=_params(("parallel",)))(A, *Bs)


def _mm_tn_shared_rhs(name, As, B, tk=256):
    S, N = B.shape
    tk = min(tk, S)
    assert S % tk == 0
    n = len(As)

    def body(*refs):
        a_refs, b_ref, o_refs = refs[:n], refs[n], refs[n + 1:]

        @pl.when(pl.program_id(0) == 0)
        def _():
            for o_ref in o_refs:
                o_ref[...] = jnp.zeros_like(o_ref)
        b = b_ref[...]
        for a_ref, o_ref in zip(a_refs, o_refs):
            o_ref[...] += _dot(a_ref[...], b, TN)

    return pl.pallas_call(
        body, name=name, grid=(S // tk,),
        in_specs=[pl.BlockSpec((tk, a.shape[1]), lambda k: (k, 0)) for a in As] + [pl.BlockSpec((tk, N), lambda k: (k, 0))],
        out_specs=[pl.BlockSpec((a.shape[1], N), lambda k: (0, 0)) for a in As],
        out_shape=[jax.ShapeDtypeStruct((a.shape[1], N), F32) for a in As],
        compiler_params=_params(("arbitrary",)))(*As, B)


def _mm_tn(name, A, B, tk=512, t2_cap=1536):
    S, K1 = A.shape
    N2 = B.shape[1]
    tk = min(tk, S)
    t2 = _col_tile(N2, t2_cap)
    assert S % tk == 0 and N2 % t2 == 0

    def body(a_ref, b_ref, o_ref):
        @pl.when(pl.program_id(1) == 0)
        def _():
            o_ref[...] = jnp.zeros_like(o_ref)
        o_ref[...] += _dot(a_ref[...], b_ref[...], TN)

    return pl.pallas_call(
        body, name=name, grid=(N2 // t2, S // tk),
        in_specs=[pl.BlockSpec((tk, K1), lambda j, k: (k, 0)), pl.BlockSpec((tk, t2), lambda j, k: (k, j))],
        out_specs=pl.BlockSpec((K1, t2), lambda j, k: (0, j)),
        out_shape=jax.ShapeDtypeStruct((K1, N2), F32),
        compiler_params=_params(("parallel", "arbitrary")))(A, B)


def _norm_mod(x, nw, sc, sh):
    rstd = lax.rsqrt(jnp.mean(x * x, axis=-1, keepdims=True) + EPS)
    return (x * rstd * nw) * (1.0 + sc) + sh


def _norm_mod_fwd(name, x, nw, sc, sh, after=None):
    return _rowwise(name, _norm_mod, [x], [nw, sc, sh], [(D, BF16)], after=after)[0]


def _norm_mod_bwd(name, x, dh, dres, nw, sc, gate=None, after=None):
    def fn(x, dh, dres, *rest):
        nw, sc = rest[-3:-1] if gate else rest
        rstd = lax.rsqrt(jnp.mean(x * x, axis=-1, keepdims=True) + EPS)
        xh = x * rstd
        dn = dh * (1.0 + sc)
        dxh = dn * nw
        dx = dres + rstd * (dxh - xh * jnp.mean(dxh * xh, axis=-1, keepdims=True))
        sums = [jnp.sum(dh, axis=0, keepdims=True), jnp.sum(dh * (xh * nw), axis=0, keepdims=True),
                jnp.sum(dn * xh, axis=0, keepdims=True)]
        if not gate:
            return (dx, *sums)
        dy = dx * rest[-1]
        return (dx, dy, *sums, jnp.sum(dx * rest[0], axis=0, keepdims=True), jnp.sum(dy, axis=0, keepdims=True))
    if not gate:
        return _rowwise(name, fn, [x, dh, dres], [nw, sc], [(D, F32)], [(1, D)] * 3, after=after)
    return _rowwise(name, fn, [x, dh, dres, gate[0]], [nw, sc, gate[1]], [(D, F32), (D, BF16)], [(1, D)] * 5,
                    tr=1024, after=after)


def _loss_head(x, tgt, fw, y, g):
    def fn(x, tgt, y, fw, g):
        rstd = lax.rsqrt(jnp.mean(x * x, axis=-1, keepdims=True) + EPS)
        xh = x * rstd
        err = xh * fw - tgt
        dout = err * (1.0 / D)
        dxh = dout * fw
        dx = rstd * (dxh - xh * jnp.mean(dxh * xh, axis=-1, keepdims=True))
        sq = jnp.sum(jnp.sum(err * err, axis=1, keepdims=True), axis=0, keepdims=True)
        return (dx, dx * g, sq, jnp.sum(dout * xh, axis=0, keepdims=True), jnp.sum(dx * y, axis=0, keepdims=True))
    return _rowwise("loss_head", fn, [x, tgt, y], [fw, g], [(D, F32), (D, BF16)], [(1, 1), (1, D), (1, D)], tr=1024)


def _ffn_fwd(tag, h, wg, wu, wd, x, g2, next_norm=None):
    def act(a, b):
        return a, b, a * _sigmoid(a) * b
    a, b, f = _mm(f"ffn_up_{tag}", [h, h], [wg, wu], "nt", [BF16, BF16, BF16], epi=act, groups=[0, 1], tn_cap=1408,
                  tm=1024)

    if next_norm is None:
        def res(y, x, g):
            return y, x + g * y
        y, xo = _mm(f"ffn_down_{tag}", [f], [wd], "nn", [F32, F32], epi=res, extras=[x], vecs=[g2])
        return a, b, f, y, xo, None

    def res_norm(y, x, g, nw, sc, sh):
        xo = x + g * y
        return y, xo, _norm_mod(xo, nw, sc, sh)
    assert wd.shape[1] == D
    y, xo, h_next = _mm(f"ffn_down_{tag}", [f], [wd], "nn", [F32, F32, BF16], epi=res_norm, extras=[x],
                        vecs=[g2, *next_norm], whole_rows=True)
    return a, b, f, y, xo, h_next


def _ffn_bwd(tag, dy, h, a, b, f, wg, wu, wd):
    def act_bwd(df, a, b):
        a, b = a.astype(F32), b.astype(F32)
        sg = _sigmoid(a)
        return df * b * (sg * (1.0 + a * (1.0 - sg))), df * (a * sg)
    da, db = _mm(f"ffn_dact_{tag}", [dy], [wd], "nt", [BF16, BF16], epi=act_bwd, extras=[a, b], tn_cap=1408, tm=1024)
    dwd = _mm_tn(f"ffn_dwd_{tag}", f, dy)
    dwg = _mm_tn(f"ffn_dwg_{tag}", da, h)
    dwu = _mm_tn(f"ffn_dwu_{tag}", db, h)
    dh = _mm(f"ffn_dh_{tag}", [da, db], [wg, wu], "nn", [F32])[0]
    return dh, dwg, dwu, dwd


def _conv_fwd(xr, w, b, tb=512):
    S, C = xr.shape
    tb = min(tb, S)

    def body(x_ref, halo_ref, w_ref, b_ref, pre_ref, out_ref):
        i = pl.program_id(0)
        halo = jnp.where(i > 0, halo_ref[...], 0.0)
        xe = jnp.concatenate([halo, x_ref[...]], axis=0)
        pre = w_ref[3:4, :] * x_ref[...] + b_ref[...]
        for j in (1, 2, 3):
            pre = pre + w_ref[3 - j:4 - j, :] * pltpu.roll(xe, j, axis=0)[8:, :]
        pre_ref[...] = pre
        out_ref[...] = pre * _sigmoid(pre)

    return pl.pallas_call(
        body, name="conv_fwd", grid=(S // tb,),
        in_specs=[pl.BlockSpec((tb, C), lambda i: (i, 0)),
                  pl.BlockSpec((8, C), lambda i: (jnp.maximum(i * (tb // 8) - 1, 0), 0)),
                  pl.BlockSpec((4, C), lambda i: (0, 0)), pl.BlockSpec((1, C), lambda i: (0, 0))],
        out_specs=[pl.BlockSpec((tb, C), lambda i: (i, 0))] * 2,
        out_shape=[jax.ShapeDtypeStruct((S, C), F32)] * 2,
        compiler_params=_params(("parallel",)))(xr, xr, w, b)


def _conv_bwd(dxc, pre, xr, w, tb=512):
    S, C = xr.shape
    tb = min(tb, S)
    nblk = S // tb

    def body(d_ref, p_ref, dn_ref, pn_ref, x_ref, w_ref, dx_ref, dw_ref, db_ref):
        i = pl.program_id(0)

        @pl.when(i == 0)
        def _():
            dw_ref[...] = jnp.zeros_like(dw_ref)
            db_ref[...] = jnp.zeros_like(db_ref)

        dpre = d_ref[...] * _silu_grad(p_ref[...])
        dnext = jnp.where(i < nblk - 1, dn_ref[...] * _silu_grad(pn_ref[...]), 0.0)
        pe = jnp.concatenate([dpre, dnext], axis=0)
        xx = x_ref[...]
        dx = w_ref[3:4, :] * dpre
        dw_ref[3:4, :] += jnp.sum(dpre * xx, axis=0, keepdims=True)
        for j in (1, 2, 3):
            ahead = pltpu.roll(pe, tb + 8 - j, axis=0)[:tb, :]
            dx = dx + w_ref[3 - j:4 - j, :] * ahead
            dw_ref[3 - j:4 - j, :] += jnp.sum(ahead * xx, axis=0, keepdims=True)
        dx_ref[...] = dx.astype(dx_ref.dtype)
        db_ref[...] += jnp.sum(dpre, axis=0, keepdims=True)

    blk = pl.BlockSpec((tb, C), lambda i: (i, 0))
    nxt = pl.BlockSpec((8, C), lambda i: (jnp.minimum((i + 1) * (tb // 8), S // 8 - 1), 0))
    return pl.pallas_call(
        body, name="conv_bwd", grid=(nblk,),
        in_specs=[blk, blk, nxt, nxt, blk, pl.BlockSpec((4, C), lambda i: (0, 0))],
        out_specs=[blk, pl.BlockSpec((4, C), lambda i: (0, 0)), pl.BlockSpec((1, C), lambda i: (0, 0))],
        out_shape=[jax.ShapeDtypeStruct((S, C), BF16), jax.ShapeDtypeStruct((4, C), F32),
                   jax.ShapeDtypeStruct((1, C), F32)],
        compiler_params=_params(("arbitrary",)))(dxc, pre, dxc, pre, xr, w)


def _iota(shape, dim):
    return lax.broadcasted_iota(jnp.int32, shape, dim)


def _colsel(m, lane, h):
    return jnp.sum(jnp.where(lane == h, m, 0.0), axis=1, keepdims=True)


def _cumsum_rows(v):
    r = _iota(v.shape, 0)
    k = 1
    while k < v.shape[0]:
        v = v + jnp.where(r >= k, pltpu.roll(v, k, axis=0), 0.0)
        k *= 2
    return v


def _suffix_sum_rows(v):
    n = v.shape[0]
    r = _iota(v.shape, 0)
    k = 1
    while k < n:
        v = v + jnp.where(r < n - k, pltpu.roll(v, n - k, axis=0), 0.0)
        k *= 2
    return v


def _ssd_fwd(xc, dtr, z, dtb, alog, dskl, nw):
    S = xc.shape[0]
    nc = S // L

    def body(xc_ref, dtr_ref, z_ref, dtb_ref, alog_ref, dsk_ref, nw_ref, ya_ref, y_ref, prev_ref,
             st_ref, cum_ref, cumT_ref):
        i = pl.program_id(0)

        @pl.when(i == 0)
        def _():
            st_ref[...] = jnp.zeros_like(st_ref)

        lane = _iota((L, 128), 1)
        lane1 = _iota((1, 128), 1)
        lo = lane < 64
        lo1 = lane1 < 64
        tril = _iota((L, L), 0) >= _iota((L, L), 1)
        dt = _softplus(dtr_ref[...] + dtb_ref[...])
        a_neg = -jnp.exp(alog_ref[...])
        cum = _cumsum_rows(dt * a_neg)
        cum_ref[...] = cum
        cumT_ref[...] = cum.T
        last_all = cum_ref[L - 1:L, :]
        prev_t = st_ref[...]
        prev_ref[0] = prev_t
        for g in range(2):
            bg = xc_ref[:, 1024 + g * 128:1152 + g * 128]
            cg = xc_ref[:, 1280 + g * 128:1408 + g * 128]
            gmat = _dot(cg, bg, NT)
            yoff = _dot(cg, prev_t[:, g * 512:(g + 1) * 512])
            bg_t = bg.T
            for jp in range(4):
                j = g * 4 + jp
                sl = slice(j * 128, (j + 1) * 128)
                xp = xc_ref[:, sl]
                cc = [_colsel(cum, lane, 2 * j), _colsel(cum, lane, 2 * j + 1)]
                cum_l = jnp.where(lo, cc[0], cc[1])
                dt_l = jnp.where(lo, _colsel(dt, lane, 2 * j), _colsel(dt, lane, 2 * j + 1))
                last_l = jnp.where(lo1, _colsel(last_all, lane1, 2 * j), _colsel(last_all, lane1, 2 * j + 1))
                xd = xp * dt_l
                ys = []
                for hh in range(2):
                    seg = cc[hh] - cumT_ref[2 * j + hh:2 * j + hh + 1, :]
                    dm = jnp.where(tril, jnp.exp(seg), 0.0)
                    ys.append(_dot(gmat * dm, xd))
                y_ref[:, sl] = (jnp.where(lo, ys[0], ys[1]) + jnp.exp(cum_l) * yoff[:, jp * 128:(jp + 1) * 128]
                                + dsk_ref[:, sl] * xp)
                st_ref[:, sl] = prev_t[:, sl] * jnp.exp(last_l) + _dot(bg_t, xd * jnp.exp(last_l - cum_l))
        for g in range(2):
            sl = slice(g * 512, (g + 1) * 512)
            zz = z_ref[:, sl]
            yg = y_ref[:, sl] * (zz * _sigmoid(zz))
            rstd = lax.rsqrt(jnp.mean(yg * yg, axis=-1, keepdims=True) + EPS)
            ya_ref[:, sl] = (yg * rstd * nw_ref[:, sl]).astype(ya_ref.dtype)

    blk = lambda c: pl.BlockSpec((L, c), lambda i: (i, 0))
    vec = lambda c: pl.BlockSpec((1, c), lambda i: (0, 0))
    return pl.pallas_call(
        body, name="ssd_fwd", grid=(nc,),
        in_specs=[blk(1536), blk(128), blk(1024), vec(128), vec(128), vec(1024), vec(1024)],
        out_specs=[blk(1024), blk(1024), pl.BlockSpec((1, NSTATE, 1024), lambda i: (i, 0, 0))],
        out_shape=[jax.ShapeDtypeStruct((S, 1024), BF16), jax.ShapeDtypeStruct((S, 1024), F32),
                   jax.ShapeDtypeStruct((nc, NSTATE, 1024), F32)],
        scratch_shapes=[pltpu.VMEM((NSTATE, 1024), F32), pltpu.VMEM((L, 128), F32), pltpu.VMEM((L, 128), F32)],
        compiler_params=_params(("arbitrary",)))(xc, dtr, z, dtb, alog, dskl, nw)


def _ssd_bwd(dya, y, z, xc, dtr, prev, dtb, alog, dskl, nw):
    S = xc.shape[0]
    nc = S // L

    def body(dya_ref, y_ref, z_ref, xc_ref, dtr_ref, prev_ref, dtb_ref, alog_ref, dsk_ref, nw_ref,
             dz_ref, dxc_ref, ddtr_ref, dnw_ref, ddsk_ref, dalog_ref, ddtb_ref,
             dst_ref, cum_ref, cumT_ref, dy_ref, dskacc_ref):
        i = pl.program_id(0)

        @pl.when(i == 0)
        def _():
            dst_ref[...] = jnp.zeros_like(dst_ref)
            dskacc_ref[...] = jnp.zeros_like(dskacc_ref)
            dnw_ref[...] = jnp.zeros_like(dnw_ref)
            dalog_ref[...] = jnp.zeros_like(dalog_ref)
            ddtb_ref[...] = jnp.zeros_like(ddtb_ref)

        lane = _iota((L, 128), 1)
        lane1 = _iota((1, 128), 1)
        lo = lane < 64
        lo1 = lane1 < 64
        r2, c2 = _iota((L, L), 0), _iota((L, L), 1)
        tril = r2 >= c2
        triu = r2 <= c2
        is_last = _iota((L, 1), 0) == L - 1

        for g in range(2):
            sl = slice(g * 512, (g + 1) * 512)
            zz = z_ref[:, sl]
            sg = _sigmoid(zz)
            zg = zz * sg
            yv = y_ref[:, sl]
            yg = yv * zg
            rstd = lax.rsqrt(jnp.mean(yg * yg, axis=-1, keepdims=True) + EPS)
            xh = yg * rstd
            d_out = dya_ref[:, sl]
            dnw_ref[:, sl] += jnp.sum(d_out * xh, axis=0, keepdims=True)
            dyn = d_out * nw_ref[:, sl]
            dyg = rstd * (dyn - xh * jnp.mean(dyn * xh, axis=-1, keepdims=True))
            dy_ref[:, sl] = dyg * zg
            dz_ref[:, sl] = (dyg * yv * (sg * (1.0 + zz * (1.0 - sg)))).astype(dz_ref.dtype)

        dtin = dtr_ref[...] + dtb_ref[...]
        dt = _softplus(dtin)
        a_neg = -jnp.exp(alog_ref[...])
        cum = _cumsum_rows(dt * a_neg)
        cum_ref[...] = cum
        cumT_ref[...] = cum.T
        last_all = cum_ref[L - 1:L, :]
        prev_t = prev_ref[0]
        dn_t = dst_ref[...]
        dcum = jnp.zeros((L, 128), F32)
        ddt = jnp.zeros((L, 128), F32)
        for g in range(2):
            gsl = slice(g * 512, (g + 1) * 512)
            bg = xc_ref[:, 1024 + g * 128:1152 + g * 128]
            cg = xc_ref[:, 1280 + g * 128:1408 + g * 128]
            gmat = _dot(cg, bg, NT)
            gmat_t = _dot(bg, cg, NT)
            pg = prev_t[:, gsl]
            zmat = _dot(cg, pg)
            dgm = jnp.zeros((L, L), F32)
            dgm_t = jnp.zeros((L, L), F32)
            db_acc = jnp.zeros((L, NSTATE), F32)
            dz_parts, cd_parts = [], []
            for jp in range(4):
                j = g * 4 + jp
                sl = slice(j * 128, (j + 1) * 128)
                xp = xc_ref[:, sl]
                dyp = dy_ref[:, sl]
                cc = [_colsel(cum, lane, 2 * j), _colsel(cum, lane, 2 * j + 1)]
                lc = [_colsel(last_all, lane1, 2 * j), _colsel(last_all, lane1, 2 * j + 1)]
                cum_l = jnp.where(lo, cc[0], cc[1])
                dt_l = jnp.where(lo, _colsel(dt, lane, 2 * j), _colsel(dt, lane, 2 * j + 1))
                last_l = jnp.where(lo1, lc[0], lc[1])
                e_l = jnp.exp(cum_l)
                dte_l = jnp.exp(last_l - cum_l)
                cd_l = jnp.exp(last_l)
                cd_parts.append(cd_l)
                xd = xp * dt_l
                dskacc_ref[:, sl] += jnp.sum(dyp * xp, axis=0, keepdims=True)
                dxp = dsk_ref[:, sl] * dyp
                t = dyp * (e_l * zmat[:, jp * 128:(jp + 1) * 128])
                dcc = [jnp.sum(jnp.where(lo, t, 0.0), axis=1, keepdims=True),
                       jnp.sum(jnp.where(lo, 0.0, t), axis=1, keepdims=True)]
                dz_parts.append(e_l * dyp)
                dnp_ = dn_t[:, sl]
                t2 = jnp.sum(dnp_ * prev_t[:, sl], axis=0, keepdims=True)
                dcd = [jnp.sum(jnp.where(lo1, t2, 0.0), axis=1, keepdims=True),
                       jnp.sum(jnp.where(lo1, 0.0, t2), axis=1, keepdims=True)]
                wm = _dot(bg, dnp_)
                dxd = wm * dte_l
                t3 = wm * xd
                ddte = [jnp.sum(jnp.where(lo, t3, 0.0), axis=1, keepdims=True),
                        jnp.sum(jnp.where(lo, 0.0, t3), axis=1, keepdims=True)]
                db_acc = db_acc + _dot(xd * dte_l, dnp_, NT)
                for hh in range(2):
                    h = 2 * j + hh
                    half = lo if hh == 0 else jnp.logical_not(lo)
                    row = cumT_ref[h:h + 1, :]
                    dm = jnp.where(tril, jnp.exp(cc[hh] - row), 0.0)
                    dm_t = jnp.where(triu, jnp.exp(row - cc[hh]), 0.0)
                    dym = jnp.where(half, dyp, 0.0)
                    u = _dot(dym, xd, NT) * dm
                    u_t = _dot(xd, dym, NT) * dm_t
                    dxd = dxd + _dot(gmat_t * dm_t, dym)
                    dcc[hh] = dcc[hh] + jnp.sum(u * gmat, axis=1, keepdims=True) - jnp.sum(u_t * gmat_t, axis=1, keepdims=True)
                    dgm = dgm + u
                    dgm_t = dgm_t + u_t
                    dte_c = jnp.exp(lc[hh] - cc[hh])
                    dcc[hh] = dcc[hh] - ddte[hh] * dte_c
                    endc = dcd[hh] * jnp.exp(lc[hh]) + jnp.sum(ddte[hh] * dte_c, axis=0, keepdims=True)
                    dcc[hh] = dcc[hh] + jnp.where(is_last, endc, 0.0)
                    dcum = jnp.where(lane == h, dcc[hh], dcum)
                dxc_ref[:, sl] = dxp + dxd * dt_l
                t4 = dxd * xp
                ddt = jnp.where(lane == 2 * j, jnp.sum(jnp.where(lo, t4, 0.0), axis=1, keepdims=True), ddt)
                ddt = jnp.where(lane == 2 * j + 1, jnp.sum(jnp.where(lo, 0.0, t4), axis=1, keepdims=True), ddt)
            dzg = jnp.concatenate(dz_parts, axis=1)
            dst_ref[:, gsl] = dn_t[:, gsl] * jnp.concatenate(cd_parts, axis=1) + _dot(cg.T, dzg)
            dxc_ref[:, 1280 + g * 128:1408 + g * 128] = _dot(dgm, bg) + _dot(dzg, pg, NT)
            dxc_ref[:, 1024 + g * 128:1152 + g * 128] = _dot(dgm_t, cg) + db_acc
        dla = _suffix_sum_rows(dcum)
        ddt = ddt + dla * a_neg
        dalog_ref[...] += jnp.sum(dla * dt, axis=0, keepdims=True) * a_neg
        ddtr = jnp.where(lane < 16, ddt * _sigmoid(dtin), 0.0)
        ddtr_ref[...] = ddtr.astype(ddtr_ref.dtype)
        ddtb_ref[...] += jnp.sum(ddtr, axis=0, keepdims=True)

        @pl.when(i == nc - 1)
        def _():
            seg = (_iota((1024, 128), 0) // 64 == _iota((1024, 128), 1)).astype(F32)
            acc8 = jnp.broadcast_to(dskacc_ref[...], (8, 1024))
            ddsk_ref[...] = lax.dot_general(acc8, seg, NN, precision=lax.Precision.HIGHEST,
                                            preferred_element_type=F32)

    rev = lambda c: pl.BlockSpec((L, c), lambda i: (nc - 1 - i, 0))
    vec = lambda c: pl.BlockSpec((1, c), lambda i: (0, 0))
    return pl.pallas_call(
        body, name="ssd_bwd", grid=(nc,),
        in_specs=[rev(1024), rev(1024), rev(1024), rev(1536), rev(128),
                  pl.BlockSpec((1, NSTATE, 1024), lambda i: (nc - 1 - i, 0, 0)),
                  vec(128), vec(128), vec(1024), vec(1024)],
        out_specs=[rev(1024), rev(1536), rev(128), vec(1024), pl.BlockSpec((8, 128), lambda i: (0, 0)),
                   vec(128), vec(128)],
        out_shape=[jax.ShapeDtypeStruct((S, 1024), BF16), jax.ShapeDtypeStruct((S, 1536), F32),
                   jax.ShapeDtypeStruct((S, 128), BF16), jax.ShapeDtypeStruct((1, 1024), F32),
                   jax.ShapeDtypeStruct((8, 128), F32), jax.ShapeDtypeStruct((1, 128), F32),
                   jax.ShapeDtypeStruct((1, 128), F32)],
        scratch_shapes=[pltpu.VMEM((NSTATE, 1024), F32), pltpu.VMEM((L, 128), F32), pltpu.VMEM((L, 128), F32),
                        pltpu.VMEM((L, 1024), F32), pltpu.VMEM((1, 1024), F32)],
        compiler_params=_params(("arbitrary",)))(dya, y, z, xc, dtr, prev, dtb, alog, dskl, nw)


def _layer_norm_parts(vg):
    mu = jnp.mean(vg, axis=-1, keepdims=True)
    vc = vg - mu
    rstd = lax.rsqrt(jnp.mean(vc * vc, axis=-1, keepdims=True) + EPS)
    return vc * rstd, rstd


def _gmlp_fwd(u, v, lnw, lnb, ws, bse, tb=512):
    S = u.shape[0]
    tb = min(tb, S)

    def body(u_ref, v_ref, lnw_ref, lnb_ref, ws_ref, bse_ref, o_ref, vn_ref):
        tril = _iota((L, L), 0) >= _iota((L, L), 1)
        xh, _ = _layer_norm_parts(_gelu(v_ref[...]))
        vn_ref[...] = xh * lnw_ref[...] + lnb_ref[...]
        for g in range(8):
            w = jnp.where(tril, ws_ref[g], 0.0)
            gs = slice(g * 128, (g + 1) * 128)
            for ch in range(tb // L):
                rs = slice(ch * L, (ch + 1) * L)
                sv = _dot(w, vn_ref[rs, gs]) + bse_ref[g]
                o_ref[rs, gs] = (_gelu(u_ref[rs, gs]) * sv).astype(o_ref.dtype)

    blk = pl.BlockSpec((tb, 1024), lambda i: (i, 0))
    vec = pl.BlockSpec((1, 1024), lambda i: (0, 0))
    cube = pl.BlockSpec((8, L, 128), lambda i: (0, 0, 0))
    return pl.pallas_call(
        body, name="gmlp_fwd", grid=(S // tb,), in_specs=[blk, blk, vec, vec, cube, cube], out_specs=blk,
        out_shape=jax.ShapeDtypeStruct((S, 1024), BF16), scratch_shapes=[pltpu.VMEM((tb, 1024), F32)],
        compiler_params=_params(("parallel",)))(u, v, lnw, lnb, ws, bse)


def _gmlp_bwd(dyb, u, v, lnw, lnb, ws, bse, tb=512):
    S = u.shape[0]
    tb = min(tb, S)

    def body(d_ref, u_ref, v_ref, lnw_ref, lnb_ref, ws_ref, bse_ref,
             du_ref, dv_ref, dws_ref, dbse_ref, dlnw_ref, dlnb_ref, vn_ref, dvn_ref):
        @pl.when(pl.program_id(0) == 0)
        def _():
            dws_ref[...] = jnp.zeros_like(dws_ref)
            dbse_ref[...] = jnp.zeros_like(dbse_ref)
            dlnw_ref[...] = jnp.zeros_like(dlnw_ref)
            dlnb_ref[...] = jnp.zeros_like(dlnb_ref)

        tril = _iota((L, L), 0) >= _iota((L, L), 1)
        vv = v_ref[...]
        xh, rstd = _layer_norm_parts(_gelu(vv))
        vn_ref[...] = xh * lnw_ref[...] + lnb_ref[...]
        for g in range(8):
            w = jnp.where(tril, ws_ref[g], 0.0)
            w_t = w.T
            gs = slice(g * 128, (g + 1) * 128)
            dw = jnp.zeros((L, L), F32)
            dbs = jnp.zeros((L, 128), F32)
            for ch in range(tb // L):
                rs = slice(ch * L, (ch + 1) * L)
                vn = vn_ref[rs, gs]
                sv = _dot(w, vn) + bse_ref[g]
                uu = u_ref[rs, gs]
                dd = d_ref[rs, gs]
                du_ref[rs, gs] = (dd * sv * _gelu_grad(uu)).astype(du_ref.dtype)
                dsv = dd * _gelu(uu)
                dw = dw + _dot(dsv, vn, NT)
                dbs = dbs + dsv
                dvn_ref[rs, gs] = _dot(w_t, dsv)
            dws_ref[g] += jnp.where(tril, dw, 0.0)
            dbse_ref[g] += dbs
        dvn = dvn_ref[...]
        dlnw_ref[...] += jnp.sum(dvn * xh, axis=0, keepdims=True)
        dlnb_ref[...] += jnp.sum(dvn, axis=0, keepdims=True)
        dxh = dvn * lnw_ref[...]
        dvg = rstd * (dxh - jnp.mean(dxh, axis=-1, keepdims=True) - xh * jnp.mean(dxh * xh, axis=-1, keepdims=True))
        dv_ref[...] = (dvg * _gelu_grad(vv)).astype(dv_ref.dtype)

    blk = pl.BlockSpec((tb, 1024), lambda i: (i, 0))
    vec = pl.BlockSpec((1, 1024), lambda i: (0, 0))
    cube = pl.BlockSpec((8, L, 128), lambda i: (0, 0, 0))
    return pl.pallas_call(
        body, name="gmlp_bwd", grid=(S // tb,), in_specs=[blk, blk, blk, vec, vec, cube, cube],
        out_specs=[blk, blk, cube, cube, vec, vec],
        out_shape=[jax.ShapeDtypeStruct((S, 1024), BF16), jax.ShapeDtypeStruct((S, 1024), BF16),
                   jax.ShapeDtypeStruct((8, L, 128), F32), jax.ShapeDtypeStruct((8, L, 128), F32),
                   jax.ShapeDtypeStruct((1, 1024), F32), jax.ShapeDtypeStruct((1, 1024), F32)],
        scratch_shapes=[pltpu.VMEM((tb, 1024), F32), pltpu.VMEM((tb, 1024), F32)],
        compiler_params=_params(("arbitrary",)))(dyb, u, v, lnw, lnb, ws, bse)


def _lane_sum(name, a):
    def body(a_ref, o_ref):
        o_ref[...] = jnp.sum(a_ref[...], axis=1, keepdims=True)
    return pl.pallas_call(body, name=name, out_shape=jax.ShapeDtypeStruct((a.shape[0], 1), F32))(a)


def _bucket_onehot_t():
    qi = np.arange(L)[:, None]
    sj = np.arange(2 * L)[None, :]
    dist = np.maximum(qi + L - sj, 0)
    log_ratio = (np.log(np.maximum(dist, 1).astype(np.float32) / np.float32(16)) / np.float32(math.log(128 / 16)))
    large = 16 + (log_ratio.astype(np.float32) * np.float32(16)).astype(np.int32)
    bucket = np.where(dist < 16, dist, np.minimum(large, 31)).reshape(-1)
    return (np.arange(32)[:, None] == bucket[None, :]).astype(np.float32)


def _rel_bias(table_t, onehot_t):
    def body(t_ref, oh_ref, o_ref):
        o_ref[...] = lax.dot_general(t_ref[...], oh_ref[...], NN, precision=lax.Precision.HIGHEST,
                                     preferred_element_type=F32)
    return pl.pallas_call(body, name="rel_bias", out_shape=jax.ShapeDtypeStruct((16, L * 2 * L), F32),
                          compiler_params=_params())(table_t, onehot_t)


def _rel_bias_bwd(dbias, onehot_t):
    def body(d_ref, oh_ref, o_ref):
        o_ref[...] = lax.dot_general(d_ref[...], oh_ref[...], NT, precision=lax.Precision.HIGHEST,
                                     preferred_element_type=F32)
    return pl.pallas_call(body, name="rel_bias_bwd", out_shape=jax.ShapeDtypeStruct((16, 32), F32),
                          compiler_params=_params())(dbias, onehot_t)


def _band(kp, kc, lo):
    kk = jnp.concatenate([kp, kc], axis=0)
    kr = pltpu.roll(kk, 64, axis=1)
    return [jnp.where(lo, kk, kr), jnp.where(lo, kr, kk)]


def _attn_rows(ref, j, lo):
    parts = []
    for t in range(8):
        pair = ref[:, (4 * j + t // 2) * 128:(4 * j + t // 2 + 1) * 128]
        parts.append(jnp.where(lo if t % 2 == 0 else jnp.logical_not(lo), pair, 0.0))
    return jnp.concatenate(parts, axis=0)


def _attn_mask(i, rows):
    qi, sj = _iota((rows, 2 * L), 0) & (L - 1), _iota((rows, 2 * L), 1)
    rel = qi + L - sj
    return (rel >= 0) & (rel < L) & ((sj >= L) | (i > 0))


def _per_head_col(vals):
    return jnp.concatenate([jnp.broadcast_to(v, (L, 1)) for v in vals], axis=0)


SMEM = pl.BlockSpec(memory_space=pltpu.SMEM)


def _attn_fwd(qkv, bias, sinks):
    S = qkv.shape[0]
    nb = S // L
    scale = 64 ** -0.5

    def body(sink_ref, q_ref, kc_ref, vc_ref, kp_ref, vp_ref, bias_ref, o_ref, lse_ref):
        i = pl.program_id(0)
        lane = _iota((L, 128), 1)
        lo = lane < 64
        lo2 = _iota((2 * L, 128), 1) < 64
        mask = _attn_mask(i, L)
        kd = _band(kp_ref[...], kc_ref[...], lo2)
        vd = _band(vp_ref[...], vc_ref[...], lo2)
        lse = jnp.zeros((L, 128), F32)
        for pr in range(8):
            sl = slice(pr * 128, (pr + 1) * 128)
            qp = q_ref[:, sl]
            j = pr // 4
            outs = []
            for hh in range(2):
                h = 2 * pr + hh
                qm = jnp.where(lo if hh == 0 else jnp.logical_not(lo), qp, 0.0)
                lg = jnp.where(mask, _dot(qm, kd[j], NT) * scale + bias_ref[h], NEG_INF)
                s = sink_ref[h]
                m = jnp.maximum(jnp.max(lg, axis=1, keepdims=True), s)
                p = jnp.where(mask, jnp.exp(lg - m), 0.0)
                den = jnp.sum(p, axis=1, keepdims=True) + jnp.exp(s - m)
                outs.append(_dot(p * (1.0 / den), vd[j]))
                lse = jnp.where(lane == h, m + jnp.log(den), lse)
            o_ref[:, sl] = jnp.where(lo, outs[0], outs[1]).astype(o_ref.dtype)
        lse_ref[...] = lse

    prev = lambda col: pl.BlockSpec((L, 128), lambda i: (jnp.maximum(i - 1, 0), col))
    cur = lambda col: pl.BlockSpec((L, 128), lambda i: (i, col))
    return pl.pallas_call(
        body, name="attn_fwd", grid=(nb,),
        in_specs=[SMEM, pl.BlockSpec((L, 1024), lambda i: (i, 0)), cur(8), cur(9), prev(8), prev(9),
                  pl.BlockSpec((16, L, 2 * L), lambda i: (0, 0, 0))],
        out_specs=[pl.BlockSpec((L, 1024), lambda i: (i, 0)), pl.BlockSpec((L, 128), lambda i: (i, 0))],
        out_shape=[jax.ShapeDtypeStruct((S, 1024), BF16), jax.ShapeDtypeStruct((S, 128), F32)],
        compiler_params=_params(("parallel",)))(sinks, qkv, qkv, qkv, qkv, qkv, bias)


def _attn_bwd(qkv, d_o, lse, bias, sinks):
    S = qkv.shape[0]
    nb = S // L
    scale = 64 ** -0.5

    def body(sink_ref, q_ref, kc_ref, vc_ref, kp_ref, vp_ref, do_ref, lse_ref, bias_ref,
             dq_ref, dkv_ref, dbias_ref, dsink_ref, dbq_ref, dbkv_ref, carry_ref):
        i = pl.program_id(0)

        @pl.when(i == 0)
        def _():
            dbias_ref[...] = jnp.zeros_like(dbias_ref)
            dsink_ref[...] = jnp.zeros_like(dsink_ref)
            dbq_ref[...] = jnp.zeros_like(dbq_ref)
            dbkv_ref[...] = jnp.zeros_like(dbkv_ref)
            carry_ref[...] = jnp.zeros_like(carry_ref)

        @pl.when(i < nb)
        def _():
            lane = _iota((L, 128), 1)
            lane1 = _iota((1, 128), 1)
            lo = lane < 64
            lo2 = _iota((2 * L, 128), 1) < 64
            mask = _attn_mask(i, 8 * L)
            kd = _band(kp_ref[...], kc_ref[...], lo2)
            vd = _band(vp_ref[...], vc_ref[...], lo2)
            lse_all = lse_ref[...]
            dsink = jnp.zeros((1, 128), F32)
            tot_k, tot_v = [], []
            for j in range(2):
                q_all = _attn_rows(q_ref, j, lo)
                do_all = _attn_rows(do_ref, j, lo)
                lse_col = _per_head_col([_colsel(lse_all, lane, 8 * j + t) for t in range(8)])
                lg = _dot(q_all, kd[j], NT) * scale + bias_ref[8 * j:8 * j + 8].reshape(8 * L, 2 * L)
                p = jnp.where(mask, jnp.exp(jnp.where(mask, lg, NEG_INF) - lse_col), 0.0)
                dp = _dot(do_all, vd[j], NT)
                delta = jnp.sum(p * dp, axis=1, keepdims=True)
                ds = p * (dp - delta)
                dbias_ref[8 * j:8 * j + 8] += ds.reshape(8, L, 2 * L)
                s = _per_head_col([sink_ref[8 * j + t] for t in range(8)])
                sink_part = -jnp.exp(s - lse_col) * delta
                for t in range(8):
                    dsink = dsink + jnp.where(lane1 == 8 * j + t,
                                              jnp.sum(sink_part[t * L:(t + 1) * L], axis=0, keepdims=True), 0.0)
                dss = ds * scale
                dq_all = _dot(dss, kd[j])
                for t in range(0, 8, 2):
                    sl = slice((4 * j + t // 2) * 128, (4 * j + t // 2 + 1) * 128)
                    dq = jnp.where(lo, dq_all[t * L:(t + 1) * L], dq_all[(t + 1) * L:(t + 2) * L])
                    dq_ref[:, sl] = dq.astype(dq_ref.dtype)
                    dbq_ref[:, sl] += jnp.sum(dq, axis=0, keepdims=True)
                acc_k = _dot(dss, q_all, TN)
                acc_v = _dot(p, do_all, TN)
                tot_k.append(acc_k + pltpu.roll(acc_k, 64, axis=1))
                tot_v.append(acc_v + pltpu.roll(acc_v, 64, axis=1))
            dsink_ref[...] += dsink
            dkv = jnp.concatenate([jnp.where(lo2, tot_k[0], tot_k[1]), jnp.where(lo2, tot_v[0], tot_v[1])], axis=1)
            dbkv_ref[...] += jnp.sum(dkv, axis=0, keepdims=True)
            dkv_ref[...] = (carry_ref[...] + dkv[:L, :]).astype(dkv_ref.dtype)
            carry_ref[...] = dkv[L:, :]

        @pl.when(i == nb)
        def _():
            dkv_ref[...] = carry_ref[...].astype(dkv_ref.dtype)

    c = lambda i: jnp.minimum(i, nb - 1)
    prev = lambda col: pl.BlockSpec((L, 128), lambda i: (jnp.maximum(c(i) - 1, 0), col))
    cur = lambda col: pl.BlockSpec((L, 128), lambda i: (c(i), col))
    row = lambda w: pl.BlockSpec((L, w), lambda i: (c(i), 0))
    cube = pl.BlockSpec((16, L, 2 * L), lambda i: (0, 0, 0))
    vec = lambda w: pl.BlockSpec((1, w), lambda i: (0, 0))
    return pl.pallas_call(
        body, name="attn_bwd", grid=(nb + 1,),
        in_specs=[SMEM, row(1024), cur(8), cur(9), prev(8), prev(9), row(1024), row(128), cube],
        out_specs=[row(1024), pl.BlockSpec((L, 256), lambda i: (jnp.maximum(i - 1, 0), 0)), cube,
                   vec(128), vec(1024), vec(256)],
        out_shape=[jax.ShapeDtypeStruct((S, 1024), BF16), jax.ShapeDtypeStruct((S, 256), BF16),
                   jax.ShapeDtypeStruct((16, L, 2 * L), F32), jax.ShapeDtypeStruct((1, 128), F32),
                   jax.ShapeDtypeStruct((1, 1024), F32), jax.ShapeDtypeStruct((1, 256), F32)],
        scratch_shapes=[pltpu.VMEM((L, 256), F32)],
        compiler_params=_params(("arbitrary",)))(sinks, qkv, qkv, qkv, qkv, qkv, d_o, lse, bias)


def _pad_lanes(a, n=128):
    return jnp.pad(a, ((0, 0), (0, n - a.shape[1])))


def _local_step(x, tgt, mod, w_in, P, io):
    md = [[mod[l:l + 1, k * D:(k + 1) * D] for k in range(6)] for l in range(2)]
    G, g = {}, {}

    sh1, sc1, g1, sh2, sc2, g2 = md[0]
    nmw0, nfw0 = P["norm_mix_w"][0:1], P["norm_ffn_w"][0:1]
    h0 = _norm_mod_fwd("norm_mix_0", x, nmw0, sc1, sh1, after=io["start"])
    segs = {"z": w_in[0:1024], "xbc": w_in[1024:2560], "dt": jnp.pad(w_in[2560:2576], ((0, 112), (0, 0))),
            "u": w_in[2576:3600], "v": w_in[3600:4624]}
    proj = dict(zip(segs, _mm_shared_lhs("in_proj", h0, list(segs.values()))))
    conv_w, conv_b = P["conv_w"][0], P["conv_b"]
    pre, xc = _conv_fwd(proj["xbc"], conv_w, conv_b)
    dtb, alog = _pad_lanes(P["dt_bias"]), _pad_lanes(P["a_log"])
    dskl = jnp.repeat(P["d_skip"], 64, axis=1)
    ya, y_ssd, prev = _ssd_fwd(xc, proj["dt"], proj["z"], dtb, alog, dskl, P["ssm_norm_w"])
    ws = P["gmlp_ws"][0]
    bse = jnp.broadcast_to(P["gmlp_bs"][0][:, :, None], (8, L, 128))
    yb = _gmlp_fwd(proj["u"], proj["v"], P["gmlp_ln_w"], P["gmlp_ln_b"], ws, bse)
    W = dict(io["weights0"]((ya, yb)))
    w_oa, w_ob = W["out_w"][:1024], W["out_w"][1024:]

    def res(y, x, gate, nw, sc, sh):
        xo = x + gate * y
        return y, xo, _norm_mod(xo, nw, sc, sh)
    mix0, x1, h0f = _mm("out_proj_0", [ya, yb], [w_oa, w_ob], "nn", [F32, F32, BF16], epi=res, extras=[x],
                        vecs=[g1, nfw0, sc2, sh2], whole_rows=True)
    sh1b, sc1b, g1b, sh2b, sc2b, g2b = md[1]
    nmw1, nfw1 = P["norm_mix_w"][1:2], P["norm_ffn_w"][1:2]
    a0, b0, f0, y0, x2, h1 = _ffn_fwd("0", h0f, W["gate_wt0"], W["up_wt0"], W["down_w0"], x1, g2,
                                      next_norm=(nmw1, sc1b, sh1b))

    W.update(io["weights1"](x2))
    qkv = _mm("qkv_proj", [h1], [W["qkv_wt"]], "nt", [F32], epi=lambda acc, b: acc + b, vecs=[P["qkv_b"]])[0]
    onehot_t = jnp.asarray(_bucket_onehot_t())
    bias = _rel_bias(P["rel_table"].T, onehot_t).reshape(16, L, 2 * L)
    sinks = P["sinks"].reshape(16)
    att, lse = _attn_fwd(qkv, bias, sinks)

    def res_b(y, x, gate, b, nw, sc, sh):
        y = y + b
        xo = x + gate * y
        return y, xo, _norm_mod(xo, nw, sc, sh)
    mix1, x3, h1f = _mm("o_proj", [att], [W["o_w"]], "nn", [F32, F32, BF16], epi=res_b, extras=[x2],
                        vecs=[g1b, P["o_b"], nfw1, sc2b, sh2b], whole_rows=True)
    a1, b1, f1, y1, x4, _ = _ffn_fwd("1", h1f, W["gate_wt1"], W["up_wt1"], W["down_w1"], x3, g2b)

    dx, dy, sq, g["final_norm_w"], dg2b = _loss_head(x4, tgt, P["final_norm_w"], y1, g2b)

    dh, dwg1, dwu1, dwd1 = _ffn_bwd("1", dy, h1f, a1, b1, f1, W["gate_wt1"], W["up_wt1"], W["down_w1"])
    dx, dmix, dsh2b, dsc2b, dnfw1, dg1b, g["o_b"] = _norm_mod_bwd("norm_ffn_bwd_1", x3, dh, dx, nfw1, sc2b,
                                                                 gate=(mix1, g1b))
    G["o_w"] = _mm_tn("o_dw", att, dmix)
    d_att = _mm("o_dx", [dmix], [W["o_w"]], "nt", [F32])[0]
    dq, dkv, dbias, dsinks, dbq, dbkv = _attn_bwd(qkv, d_att, lse, bias, sinks)
    g["rel_table"] = _rel_bias_bwd(dbias.reshape(16, L * 2 * L), onehot_t).T
    g["sinks"] = dsinks[:, :16]
    g["qkv_b"] = jnp.concatenate([dbq, dbkv], axis=1)
    w_q, w_kv = W["qkv_wt"][:1024], W["qkv_wt"][1024:]
    G["qkv_wt"] = jnp.concatenate(_mm_tn_shared_rhs("qkv_dw", [dq, dkv], h1), axis=0)
    dh = _mm("qkv_dx", [dq, dkv], [w_q, w_kv], "nn", [F32])[0]
    behind = io["grads1"]({"qkv_wt": G.pop("qkv_wt"), "o_w": G.pop("o_w"), "gate_wt1": dwg1, "up_wt1": dwu1,
                           "down_w1": dwd1})
    dx, dy, dsh1b, dsc1b, dnmw1, dg2, _ = _norm_mod_bwd("norm_mix_bwd_1", x2, dh, dx, nmw1, sc1b, gate=(y0, g2),
                                                        after=behind)

    dh, dwg0, dwu0, dwd0 = _ffn_bwd("0", dy, h0f, a0, b0, f0, W["gate_wt0"], W["up_wt0"], W["down_w0"])
    behind = io["grads_ffn0"]({"gate_wt0": dwg0, "up_wt0": dwu0, "down_w0": dwd0})
    dx, dmix, dsh2, dsc2, dnfw0, dg1, _ = _norm_mod_bwd("norm_ffn_bwd_0", x1, dh, dx, nfw0, sc2, gate=(mix0, g1),
                                                        after=behind)
    G["out_w"] = jnp.concatenate(_mm_tn_shared_rhs("out_dw", [ya, yb], dmix), axis=0)
    dya, dyb = _mm_shared_lhs("out_dx", dmix, [w_oa, w_ob])
    du, dv, dws, dbse, g["gmlp_ln_w"], g["gmlp_ln_b"] = _gmlp_bwd(dyb, proj["u"], proj["v"], P["gmlp_ln_w"],
                                                                 P["gmlp_ln_b"], ws, bse)
    g["gmlp_ws"] = dws[None]
    g["gmlp_bs"] = _lane_sum("gmlp_dbs", dbse.reshape(8 * L, 128)).reshape(1, 8, L)
    dz, dxc, ddt, g["ssm_norm_w"], ddsk, dalog, ddtb = _ssd_bwd(dya, y_ssd, proj["z"], xc, proj["dt"], prev,
                                                                dtb, alog, dskl, P["ssm_norm_w"])
    g["d_skip"], g["a_log"], g["dt_bias"] = ddsk[0:1, :16], dalog[:, :16], ddtb[:, :16]
    dxr, dconv_w, g["conv_b"] = _conv_bwd(dxc, pre, proj["xbc"], conv_w)
    g["conv_w"] = dconv_w[None]
    dsegs = {"z": dz, "xbc": dxr, "dt": ddt, "u": du, "v": dv}
    dws_in = dict(zip(dsegs, _mm_tn_shared_rhs("in_dw", list(dsegs.values()), h0)))
    G["in_wt"] = jnp.concatenate([dws_in["z"], dws_in["xbc"], dws_in["dt"][:16], dws_in["u"], dws_in["v"]], axis=0)
    keys = ["z", "xbc", "dt", "u", "v"]
    dh = _mm("in_dx", [dsegs[k] for k in keys], [segs[k] for k in keys], "nn", [F32])[0]
    dx, dsh1, dsc1, dnmw0 = _norm_mod_bwd("norm_mix_bwd_0", x, dh, dx, nmw0, sc1)

    g["norm_mix_w"] = jnp.concatenate([dnmw0, dnmw1], axis=0)
    g["norm_ffn_w"] = jnp.concatenate([dnfw0, dnfw1], axis=0)
    dmod = jnp.concatenate([jnp.concatenate([dsh1, dsc1, dg1, dsh2, dsc2, dg2], axis=1),
                            jnp.concatenate([dsh1b, dsc1b, dg1b, dsh2b, dsc2b, dg2b], axis=1)], axis=0)
    return sq, dx, dmod, G, g


def _ada_fwd(c_all, ada_w, ada_b):
    n = ada_w.shape[2]
    tn = _col_tile(n, 512)

    def body(c_ref, w_ref, b_ref, o_ref):
        cc = c_ref[...]
        o_ref[...] = lax.dot_general(cc * _sigmoid(cc), w_ref[...], NN, precision=lax.Precision.HIGHEST,
                                     preferred_element_type=F32) + b_ref[...]

    return pl.pallas_call(
        body, name="ada_fwd", grid=(2, n // tn),
        in_specs=[pl.BlockSpec((8, D), lambda l, j: (0, 0)), pl.BlockSpec((None, D, tn), lambda l, j: (l, 0, j)),
                  pl.BlockSpec((None, 1, tn), lambda l, j: (l, 0, j))],
        out_specs=pl.BlockSpec((None, 8, tn), lambda l, j: (l, 0, j)),
        out_shape=jax.ShapeDtypeStruct((2, 8, n), F32), compiler_params=_params(("parallel", "parallel")))(
            c_all, ada_w, ada_b)


def _ada_bwd(c_all, dmod_cols, dmod_all):
    n = dmod_cols.shape[2]
    tn = _col_tile(n, 512)

    def body(c_ref, d_ref, o_ref):
        cc = c_ref[...]
        o_ref[...] = lax.dot_general(cc * _sigmoid(cc), d_ref[...], TN, precision=lax.Precision.HIGHEST,
                                     preferred_element_type=F32)

    dw = pl.pallas_call(
        body, name="ada_dw", grid=(2, n // tn),
        in_specs=[pl.BlockSpec((8, D), lambda l, j: (0, 0)), pl.BlockSpec((None, 8, tn), lambda l, j: (l, 0, j))],
        out_specs=pl.BlockSpec((None, D, tn), lambda l, j: (l, 0, j)),
        out_shape=jax.ShapeDtypeStruct((2, D, n), F32), compiler_params=_params(("parallel", "parallel")))(
            c_all, dmod_cols)

    def sum_body(d_ref, o_ref):
        o_ref[...] = jnp.sum(d_ref[...], axis=0, keepdims=True)

    db = pl.pallas_call(
        sum_body, name="ada_db", grid=(2,),
        in_specs=[pl.BlockSpec((None, 8, 6 * D), lambda l: (l, 0, 0))],
        out_specs=pl.BlockSpec((None, 1, 6 * D), lambda l: (l, 0, 0)),
        out_shape=jax.ShapeDtypeStruct((2, 1, 6 * D), F32), compiler_params=_params(("parallel",)))(dmod_all)
    return dw, db


def _row_tile(rows, cap=512, mult=8):
    best = rows
    for t in range(mult, min(rows, cap) + 1, mult):
        if rows % t == 0:
            best = t
    return best


def _adamw(name, w, g, m, v):
    def fn(w, g, m, v):
        m = ADAM_B1 * m + (1.0 - ADAM_B1) * g
        v = ADAM_B2 * v + (1.0 - ADAM_B2) * (g * g)
        m_hat = m / (1.0 - ADAM_B1 ** ADAM_STEP)
        v_hat = v / (1.0 - ADAM_B2 ** ADAM_STEP)
        return -ADAM_LR * (m_hat / (jnp.sqrt(v_hat) + ADAM_EPS) + ADAM_WD * w), m, v
    cols = w.shape[1]
    return _rowwise(name, fn, [w, g, m, v], [], [(cols, F32)] * 3, tr=_row_tile(w.shape[0]))


def _place():
    return lax.axis_index("x"), lax.axis_index("y"), lax.axis_index("c")


VMEM_SPEC = pl.BlockSpec(memory_space=pltpu.VMEM)


def _allreduce_small(name, buf, after=None):
    rows = buf.shape[0]
    deps = [] if after is None else [after]

    def body(x_ref, *rest):
        o_ref, stage, send_sems, recv_sems = rest[len(deps):]
        x, y, c = _place()
        me = 4 * x + 2 * y + c
        stage[me] = x_ref[...]
        copies = []
        for k in range(1, 8):
            peer = (1 - x if k & 4 else x, 1 - y if k & 2 else y, 1 - c if k & 1 else c)
            cp = pltpu.make_async_remote_copy(src_ref=x_ref, dst_ref=stage.at[me], send_sem=send_sems.at[k - 1],
                                              recv_sem=recv_sems.at[k - 1], device_id=peer, device_id_type=MESH)
            cp.start()
            copies.append(cp)
        for cp in copies:
            cp.wait()
        acc = stage[0]
        for d in range(1, 8):
            acc = acc + stage[d]
        o_ref[...] = acc

    return pl.pallas_call(
        body, name=name, in_specs=[VMEM_SPEC] + [ANY for _ in deps], out_specs=VMEM_SPEC,
        out_shape=jax.ShapeDtypeStruct((rows, 128), F32),
        scratch_shapes=[pltpu.VMEM((8, rows, 128), F32), pltpu.SemaphoreType.DMA((7,)), pltpu.SemaphoreType.DMA((7,))],
        compiler_params=pltpu.CompilerParams(vmem_limit_bytes=_VMEM_LIMIT))(buf, *deps)


def _sum_slots(name, own, land):
    def body(own_ref, land_ref, o_ref):
        x, y, c = _place()
        me = 4 * x + 2 * y + c
        acc = None
        for d in range(8):
            v = jnp.where(me == d, own_ref[...], land_ref[d])
            acc = v if acc is None else acc + v
        o_ref[...] = acc

    return pl.pallas_call(body, name=name, in_specs=[VMEM_SPEC, VMEM_SPEC], out_specs=VMEM_SPEC,
                          out_shape=jax.ShapeDtypeStruct(own.shape, F32),
                          compiler_params=pltpu.CompilerParams(vmem_limit_bytes=_VMEM_LIMIT))(own, land)


OTHER_CHIPS = ((1, 0), (0, 1), (1, 1))


SIBLING_COLLECTIVE_ID = 6


def _sibling_handshake():
    x, y, c = _place()
    barrier = pltpu.get_barrier_semaphore()
    pl.semaphore_signal(barrier, inc=1, device_id=(x, y, 1 - c), device_id_type=MESH)
    pl.semaphore_wait(barrier, 1)


def _sibling_swap(name, src, halves):
    half = src.shape[-2] // 2
    out_shape = (src.shape[0], half, 1024) if halves else src.shape

    def body(s_ref, o_ref, send_sem, recv_sem):
        x, y, c = _place()
        _sibling_handshake()
        part = s_ref.at[:, pl.ds(pl.multiple_of((1 - c) * half, 8), half)] if halves else s_ref
        cp = pltpu.make_async_remote_copy(src_ref=part, dst_ref=o_ref, send_sem=send_sem, recv_sem=recv_sem,
                                          device_id=(x, y, 1 - c), device_id_type=MESH)
        cp.start()
        cp.wait()

    return pl.pallas_call(
        body, name=name, in_specs=[ANY], out_specs=ANY, out_shape=jax.ShapeDtypeStruct(out_shape, src.dtype),
        scratch_shapes=[pltpu.SemaphoreType.DMA, pltpu.SemaphoreType.DMA],
        compiler_params=pltpu.CompilerParams(collective_id=SIBLING_COLLECTIVE_ID))(src)


HBM = pl.BlockSpec(memory_space=pltpu.HBM)
SEM = pl.BlockSpec(memory_space=pltpu.SEMAPHORE)


def _exchange_peers(mode):
    x, y, c = _place()
    if mode == "all":
        return [(1 - x if k & 4 else x, 1 - y if k & 2 else y, 1 - c if k & 1 else c) for k in range(1, 8)]
    return [(1 - x if fx else x, 1 - y if fy else y, c) for fx, fy in OTHER_CHIPS]


def _chip_copies(mode, src_ref, land_ref, send_sems, recv_sems):
    x, y, c = _place()
    k = 2 * x + y
    copies = []
    for j, peer in enumerate(_exchange_peers(mode)):
        if mode == "gather":
            half = src_ref.shape[0] // 2
            mine = pl.ds(pl.multiple_of(c * half, 16), half)
            src, dst = src_ref.at[mine], land_ref.at[k, mine]
        elif mode == "scatter":
            src, dst = src_ref.at[2 * peer[0] + peer[1]], land_ref.at[k]
        else:
            src, dst = src_ref, land_ref.at[4 * x + 2 * y + c]
        copies.append(pltpu.make_async_remote_copy(src_ref=src, dst_ref=dst, send_sem=send_sems.at[j],
                                                   recv_sem=recv_sems.at[j], device_id=peer, device_id_type=MESH))
    return copies


def _exchange_start(name, collective_id, mode, src, land, after=None):
    deps = [] if after is None else [after]
    npeers = 7 if mode == "all" else 3

    def body(s_ref, l_ref, *rest):
        send_sems, recv_sems, s_thru, l_thru, token = rest[len(deps):]
        barrier = pltpu.get_barrier_semaphore()
        for peer in _exchange_peers(mode):
            pl.semaphore_signal(barrier, inc=1, device_id=peer, device_id_type=MESH)
        pl.semaphore_wait(barrier, npeers)
        for cp in _chip_copies(mode, s_ref, l_ref, send_sems, recv_sems):
            cp.start()
        token[...] = jnp.zeros_like(token)

    return pl.pallas_call(
        body, name=name,
        out_shape=(pltpu.SemaphoreType.DMA((npeers,)), pltpu.SemaphoreType.DMA((npeers,)),
                   pltpu.HBM(src.shape, src.dtype),
                   pltpu.HBM(land.shape, land.dtype), jax.ShapeDtypeStruct((8, 128), F32)),
        in_specs=(HBM, HBM) + tuple(ANY for _ in deps), out_specs=(SEM, SEM, HBM, HBM, VMEM_SPEC),
        input_output_aliases={0: 2, 1: 3},
        compiler_params=pltpu.CompilerParams(has_side_effects=pltpu.SideEffectType.DATAFLOW_SIDE_EFFECTING,
                                             collective_id=collective_id))(
            pltpu.with_memory_space_constraint(src, pltpu.HBM), pltpu.with_memory_space_constraint(land, pltpu.HBM),
            *deps)


def _exchange_wait(name, mode, started, after):
    send_sems, recv_sems, s_thru, l_thru, _ = started
    deps = list(after) if isinstance(after, (tuple, list)) else [after]

    def body(s_ref, l_ref, send_sems, recv_sems, *rest):
        for cp in _chip_copies(mode, s_ref, l_ref, send_sems, recv_sems):
            cp.wait_send()
            cp.wait_recv()

    return pl.pallas_call(
        body, name=name, out_shape=(pltpu.HBM(s_thru.shape, s_thru.dtype), pltpu.HBM(l_thru.shape, l_thru.dtype)),
        in_specs=(HBM, HBM, SEM, SEM) + tuple(ANY for _ in deps), out_specs=(HBM, HBM),
        input_output_aliases={0: 0, 1: 1},
        compiler_params=pltpu.CompilerParams(has_side_effects=pltpu.SideEffectType.DATAFLOW_SIDE_EFFECTING))(
            s_thru, l_thru, send_sems, recv_sems, *deps)


def _allgather_finish(tag, land):
    half = land.shape[1] // 2

    def body(l_ref, o_ref, send_sem, recv_sem):
        x, y, c = _place()
        _sibling_handshake()
        mine = pl.ds(pl.multiple_of(c * half, 16), half)
        swap = pltpu.make_async_remote_copy(src_ref=o_ref.at[:, mine], dst_ref=o_ref.at[:, mine], send_sem=send_sem,
                                            recv_sem=recv_sem, device_id=(x, y, 1 - c), device_id_type=MESH)
        swap.start()
        swap.wait()

    return pl.pallas_call(
        body, name="allgather_finish_" + tag, in_specs=[ANY], out_specs=ANY, input_output_aliases={0: 0},
        out_shape=jax.ShapeDtypeStruct(land.shape, land.dtype),
        scratch_shapes=[pltpu.SemaphoreType.DMA, pltpu.SemaphoreType.DMA],
        compiler_params=pltpu.CompilerParams(collective_id=SIBLING_COLLECTIVE_ID))(land)


def _pair_sum(tag, g, r1, c):
    rows = g.shape[1]
    half = rows // 2
    th = _row_tile(half, 1408, 16)
    nblk = half // th

    def body(c_ref, g_ref, r_ref, o_ref, o2_ref):
        o_ref[...] = (g_ref[...].astype(F32) + r_ref[...].astype(F32)).astype(o_ref.dtype)
        o2_ref[...] = o_ref[...]

    spec = pl.BlockSpec((None, th, 1024), lambda k, i, c_ref: (k, i, 0))
    grid_spec = pltpu.PrefetchScalarGridSpec(
        num_scalar_prefetch=1, grid=(4, nblk),
        in_specs=[pl.BlockSpec((None, th, 1024), lambda k, i, c_ref: (k, c_ref[0] * nblk + i, 0)), spec],
        out_specs=[spec, spec])
    return pl.pallas_call(body, name="grad_pair_sum_" + tag, grid_spec=grid_spec,
                          out_shape=[jax.ShapeDtypeStruct((4, half, 1024), BF16)] * 2,
                          compiler_params=_params(("parallel", "parallel")))(c, g, r1)


def _chip_sum(tag, q, after=None):
    half = q.shape[1]
    th = _row_tile(half, 704, 16)
    deps = [] if after is None else [after]

    def body(a, b, c, d, *rest):
        rest[-1][...] = ((a[...].astype(F32) + b[...].astype(F32)) + c[...].astype(F32)) + d[...].astype(F32)

    specs = [pl.BlockSpec((None, th, 1024), functools.partial(lambda i, k: (k, i, 0), k=k)) for k in range(4)]
    return pl.pallas_call(body, name="grad_chip_sum_" + tag, grid=(half // th,), in_specs=specs + [ANY for _ in deps],
                          out_specs=pl.BlockSpec((th, 1024), lambda i: (i, 0)),
                          out_shape=jax.ShapeDtypeStruct((half, 1024), F32),
                          compiler_params=_params(("parallel",)))(q, q, q, q, *deps)


def _join_halves(tag, f, r, c):
    half = f.shape[0]
    th = _row_tile(half, 704)
    nblk = half // th

    def body(c_ref, f_ref, r_ref, o_ref):
        mine = (pl.program_id(0) == c_ref[0])
        o_ref[...] = jnp.where(mine, f_ref[...], r_ref[...])

    spec = pl.BlockSpec((th, 1024), lambda h, i, c_ref: (i, 0))
    grid_spec = pltpu.PrefetchScalarGridSpec(
        num_scalar_prefetch=1, grid=(2, nblk), in_specs=[spec, spec],
        out_specs=pl.BlockSpec((th, 1024), lambda h, i, c_ref: (h * nblk + i, 0)))
    return pl.pallas_call(body, name="grad_join_halves_" + tag, grid_spec=grid_spec,
                          out_shape=jax.ShapeDtypeStruct((2 * half, 1024), F32),
                          compiler_params=_params(("parallel", "parallel")))(c, f, r)


BIG_ARGS = ("in_w_even", "out_w_even", "qkv_w", "o_w", "ffn_gate_w", "ffn_up_w", "ffn_down_w")
def _ffn_pieces(layer):
    return tuple((f"{n}{layer}", 704, 704) for n in ("gate_wt", "up_wt", "down_w"))


IN_SLAB = (("in_wt", 1156, 1184),)
LAYER0_REST_SLAB = (("out_w", 512, 512),) + _ffn_pieces(0)
LAYER1_SLAB = (("qkv_wt", 320, 320), ("o_w", 256, 256)) + _ffn_pieces(1)
FFN0_SLAB = _ffn_pieces(0)
MIXER0_SLAB = (("in_wt", 1156, 1280), ("out_w", 512, 512))


def _slab(pieces, spec):
    parts = []
    for name, rows, room in spec:
        p = pieces[name]
        parts.append(jnp.pad(p, [(0, 0)] * (p.ndim - 2) + [(0, room - rows), (0, 0)]) if room > rows else p)
    return jnp.concatenate(parts, axis=-2) if len(parts) > 1 else parts[0]


def _unslab(slab, spec):
    out, off = {}, 0
    for name, rows, room in spec:
        out[name] = slab[..., off:off + rows, :]
        off += room
    return out


def _share_pieces(w):
    return {"in_wt": w["in_w_even"][0].T, "out_w": w["out_w_even"][0], "qkv_wt": w["qkv_w"][0].T, "o_w": w["o_w"][0],
            "gate_wt0": w["ffn_gate_w"][0].T, "gate_wt1": w["ffn_gate_w"][1].T,
            "up_wt0": w["ffn_up_w"][0].T, "up_wt1": w["ffn_up_w"][1].T,
            "down_w0": w["ffn_down_w"][0], "down_w1": w["ffn_down_w"][1]}


def _pieces_to_shares(p):
    return {"in_w_even": p["in_wt"].T[None], "out_w_even": p["out_w"][None], "qkv_w": p["qkv_wt"].T[None],
            "o_w": p["o_w"][None], "ffn_gate_w": jnp.stack([p["gate_wt0"].T, p["gate_wt1"].T]),
            "ffn_up_w": jnp.stack([p["up_wt0"].T, p["up_wt1"].T]),
            "ffn_down_w": jnp.stack([p["down_w0"], p["down_w1"]])}


def _chips_from_full(G, spec):
    return _slab({k: v.reshape(4, -1, D) for k, v in G.items()}, spec)


def _pack_small(parts):
    padded = []
    for p in parts:
        p = p.reshape(-1).astype(F32)
        padded.append(jnp.pad(p, (0, (-p.shape[0]) % 1024)))
    return jnp.concatenate(padded).reshape(-1, 128)


def _unpack_small(slab, shapes):
    flat, out, off = slab.reshape(-1), [], 0
    for shp in shapes:
        size = math.prod(shp)
        out.append(flat[off:off + size].reshape(shp))
        off += size + (-size) % 1024
    return out


SMALL = ("ada_b", "norm_mix_w", "norm_ffn_w", "conv_w", "conv_b", "dt_bias", "a_log", "d_skip", "ssm_norm_w",
         "gmlp_ln_w", "gmlp_ln_b", "gmlp_ws", "gmlp_bs", "qkv_b", "o_b", "sinks", "rel_table", "final_norm_w")
SMALL_SPLIT = {"conv_w": 1536, "qkv_b": 1280, "o_b": 1024}
WEIGHTS = ("ada_w", "ada_b", "norm_mix_w", "norm_ffn_w", "in_w_even", "conv_w", "conv_b", "dt_bias", "a_log", "d_skip",
           "ssm_norm_w", "gmlp_ln_w", "gmlp_ln_b", "gmlp_ws", "gmlp_bs", "out_w_even", "qkv_w", "qkv_b", "o_w", "o_b",
           "sinks", "rel_table", "ffn_gate_w", "ffn_up_w", "ffn_down_w", "final_norm_w")


def kernel(x, c, ada_w, ada_b, norm_mix_w, norm_ffn_w, in_w_even, conv_w, conv_b, dt_bias, a_log, d_skip, ssm_norm_w, gmlp_ln_w, gmlp_ln_b, gmlp_ws, gmlp_bs, out_w_even, qkv_w, qkv_b, o_w, o_b, sinks, rel_table, ffn_gate_w, ffn_up_w, ffn_down_w, final_norm_w, loss_target, m_ada_w, m_ada_b, m_norm_mix_w, m_norm_ffn_w, m_in_w_even, m_conv_w, m_conv_b, m_dt_bias, m_a_log, m_d_skip, m_ssm_norm_w, m_gmlp_ln_w, m_gmlp_ln_b, m_gmlp_ws, m_gmlp_bs, m_out_w_even, m_qkv_w, m_qkv_b, m_o_w, m_o_b, m_sinks, m_rel_table, m_ffn_gate_w, m_ffn_up_w, m_ffn_down_w, m_final_norm_w, v_ada_w, v_ada_b, v_norm_mix_w, v_norm_ffn_w, v_in_w_even, v_conv_w, v_conv_b, v_dt_bias, v_a_log, v_d_skip, v_ssm_norm_w, v_gmlp_ln_w, v_gmlp_ln_b, v_gmlp_ws, v_gmlp_bs, v_out_w_even, v_qkv_w, v_qkv_b, v_o_w, v_o_b, v_sinks, v_rel_table, v_ffn_gate_w, v_ffn_up_w, v_ffn_down_w, v_final_norm_w):
    args = dict(locals())
    w = {n: args[n] for n in WEIGHTS}
    m = {n: args["m_" + n] for n in WEIGHTS}
    v = {n: args["v_" + n] for n in WEIGHTS}
    ax, ay, ac = _place()
    me = 4 * ax + 2 * ay + ac
    chip = 2 * ax + ay
    south = (ac == 0).astype(F32)
    c_arr = jnp.reshape(ac, (1,)).astype(jnp.int32)

    c_all = _allreduce_small("gather_cond", lax.dynamic_update_slice(jnp.zeros((8, D), F32), c, (me, 0)).reshape(64, 128))
    c_all = c_all.reshape(8, D)
    n_ada = ada_w.shape[2]
    mod_cols = _ada_fwd(c_all, ada_w, lax.dynamic_slice(ada_b, (0, chip * n_ada), (2, n_ada)).reshape(2, 1, n_ada))
    pieces = [lax.dynamic_update_slice(jnp.zeros((2, 8, 6 * D), F32), mod_cols, (0, 0, chip * n_ada))]
    split_names = list(SMALL_SPLIT)
    for n in split_names:
        full = SMALL_SPLIT[n]
        local = w[n]
        idx = (0,) * (local.ndim - 1) + (chip * local.shape[-1],)
        pieces.append(lax.dynamic_update_slice(jnp.zeros(local.shape[:-1] + (full,), F32), local, idx))
    shapes = [p.shape for p in pieces]
    mod_own = _pack_small(pieces) * south
    mod_started = _exchange_start("gather_mod_start", 9, "all", mod_own, lax.empty((8,) + mod_own.shape, F32))

    pieces = _share_pieces(w)
    cast = {"in_wt": pieces["in_wt"].astype(_MXU)}

    def start_gather(tag, collective_id, share, after):
        return _exchange_start("allgather_start_" + tag, collective_id, "gather", share,
                               lax.empty((4,) + share.shape, share.dtype), after=after)

    def finish_gather(tag, started, spec, after):
        land = _exchange_wait("allgather_wait_" + tag, "gather", started, after)[1]
        out = {}
        for name, piece in _unslab(_allgather_finish(tag, land), spec).items():
            out[name] = lax.dynamic_update_slice(piece.reshape(-1, D), cast[name], (chip * piece.shape[1], 0))
        return out

    gather_in = start_gather("in", 7, _slab(cast, IN_SLAB), mod_started[4])
    zero = gather_in[4][0, 0]
    cast.update({k: (p + zero).astype(_MXU) for k, p in pieces.items() if k != "in_wt"})
    share0, share1 = _slab(cast, LAYER0_REST_SLAB), _slab(cast, LAYER1_SLAB)
    mod_own, mod_land = _exchange_wait("gather_mod_wait", "all", mod_started, (share0, share1))
    mod_slab = _sum_slots("gather_mod_sum", mod_own, mod_land)
    gathered = _unpack_small(mod_slab, shapes)
    mod = lax.dynamic_slice(gathered[0], (0, me, 0), (2, 1, 6 * D)).reshape(2, 6 * D)
    P = {n: w[n] for n in SMALL if n not in SMALL_SPLIT and n != "ada_b"}
    for n, full in zip(split_names, gathered[1:]):
        P[n] = full
    P["final_norm_w"] = final_norm_w.reshape(1, D)
    w_in = finish_gather("in", gather_in, IN_SLAB, mod_slab)["in_wt"]
    gather0 = start_gather("0", 1, share0, w_in)
    gather1 = start_gather("1", 2, share1, gather0[4])

    def start_reduce(tag, collective_id, G, spec, after=None):
        gp = _chips_from_full(G, spec).astype(BF16)
        p, q = _pair_sum(tag, gp, _sibling_swap("grad_pair_exchange_" + tag, gp, True), c_arr)
        return _exchange_start("grad_exchange_start_" + tag, collective_id, "scatter", p, q, after=after)

    def finish_reduce(tag, started, spec, after, behind=None):
        q = _exchange_wait("grad_exchange_wait_" + tag, "scatter", started, after)[1]
        fin = _chip_sum(tag, q, after=behind)
        total = _join_halves(tag, fin, _sibling_swap("grad_final_exchange_" + tag, fin, False), c_arr)
        return _unslab(total, spec)

    reduces = {}

    def grads1(G1):
        reduces["1"] = start_reduce("1", 3, G1, LAYER1_SLAB)
        return reduces["1"][4]

    def grads_ffn0(G):
        reduces["f"] = start_reduce("f", 4, G, FFN0_SLAB)
        return reduces["f"][4]

    io = {"start": gather1[4],
          "weights0": lambda after: finish_gather("0", gather0, LAYER0_REST_SLAB, after),
          "weights1": lambda after: finish_gather("1", gather1, LAYER1_SLAB, after),
          "grads1": grads1, "grads_ffn0": grads_ffn0}
    sq, grad_x, dmod, G0, g = _local_step(x[0], loss_target[0], mod, w_in, P, io)
    loss = lax.psum(0.5 * sq[0, 0] / D, ("x", "y", "c"))

    g["final_norm_w"] = g["final_norm_w"].reshape(D)
    small_names = [n for n in SMALL if n != "ada_b"]
    pieces = [lax.dynamic_update_slice(jnp.zeros((2, 8, 6 * D), F32), dmod.reshape(2, 1, 6 * D), (0, me, 0))]
    pieces += [g[n] for n in small_names]
    shapes = [p.shape for p in pieces]
    small_own = _pack_small(pieces)
    small_started = _exchange_start("small_grads_start", 8, "all", small_own, lax.empty((8,) + small_own.shape, F32))
    reduces["m"] = start_reduce("m", 5, G0, MIXER0_SLAB, after=small_started[4])
    shares = finish_reduce("1", reduces["1"], LAYER1_SLAB, grad_x, behind=reduces["m"][4])
    shares.update(finish_reduce("f", reduces["f"], FFN0_SLAB, grad_x, behind=reduces["m"][4]))
    small_own, small_land = _exchange_wait("small_grads_wait", "all", small_started, shares["down_w0"])
    reduced = _unpack_small(_sum_slots("small_grads_sum", small_own, small_land), shapes)
    dmod_all = reduced[0]
    grads = dict(zip(small_names, reduced[1:]))
    for n in split_names:
        full = grads[n]
        size = w[n].shape[-1]
        grads[n] = lax.dynamic_slice(full, (0,) * (full.ndim - 1) + (chip * size,), full.shape[:-1] + (size,))
    grads = {n: grads[n].reshape(w[n].shape) for n in small_names}
    dw_ada, db_ada = _ada_bwd(c_all, lax.dynamic_slice(dmod_all, (0, 0, chip * n_ada), (2, 8, n_ada)), dmod_all)
    grads["ada_w"], grads["ada_b"] = dw_ada, db_ada.reshape(2, 6 * D)

    delta, new_m, new_v = {}, {}, {}

    def update(n):
        cols = w[n].shape[-1]
        d_, m_, v_ = _adamw("adamw_" + n, w[n].reshape(-1, cols), grads[n].reshape(-1, cols), m[n].reshape(-1, cols),
                            v[n].reshape(-1, cols))
        delta[n], new_m[n], new_v[n] = d_.reshape(w[n].shape), m_.reshape(w[n].shape), v_.reshape(w[n].shape)

    update("ada_w")
    shapes = [w[n].shape for n in SMALL]
    packed = [_pack_small([t[n] for n in SMALL]) for t in (w, grads, m, v)]
    outs = _adamw("adamw_small", *packed)
    for dst, slab in zip((delta, new_m, new_v), outs):
        for n, t in zip(SMALL, _unpack_small(slab, shapes)):
            dst[n] = t
    shares.update(finish_reduce("m", reduces["m"], MIXER0_SLAB, outs[0]))
    grads.update(_pieces_to_shares(shares))
    for n in BIG_ARGS:
        update(n)
    return (loss, grad_x[None], *[grads[n] for n in WEIGHTS], *[delta[n] for n in WEIGHTS],
            *[new_m[n] for n in WEIGHTS], *[new_v[n] for n in WEIGHTS])
```

```python
import functools
import math

import numpy as np
import jax
import jax.numpy as jnp
from jax import lax
from jax.experimental import pallas as pl
from jax.experimental.pallas import tpu as pltpu

F32 = jnp.float32
BF16 = jnp.bfloat16
_MXU = jnp.bfloat16
_VMEM_LIMIT = 56 * 1024 * 1024
MXU_COLS = 256
D = 1024
L = 128
NSTATE = 128
EPS = 1e-6
NEG_INF = -1e30
FFN = 2816
ADAM_LR, ADAM_B1, ADAM_B2, ADAM_EPS, ADAM_WD, ADAM_STEP = 0.001, 0.9, 0.999, 1e-08, 0.01, 10
MESH = pl.DeviceIdType.MESH
ANY = pl.BlockSpec(memory_space=pl.ANY)

NN = (((1,), (0,)), ((), ()))
NT = (((1,), (1,)), ((), ()))
TN = (((0,), (0,)), ((), ()))


def _dot(a, b, dn=NN):
    return lax.dot_general(a.astype(_MXU), b.astype(_MXU), dn, preferred_element_type=F32)


def _params(sem=None):
    return pltpu.CompilerParams(dimension_semantics=sem, vmem_limit_bytes=_VMEM_LIMIT)


def _sigmoid(x):
    return 1.0 / (1.0 + jnp.exp(-x))


def _softplus(x):
    return jnp.maximum(x, 0.0) + jnp.log(1.0 + jnp.exp(-jnp.abs(x)))


def _gelu(x):
    return 0.5 * x * (1.0 + lax.erf(x * (2.0 ** -0.5)))


def _gelu_grad(x):
    return 0.5 * (1.0 + lax.erf(x * (2.0 ** -0.5))) + x * jnp.exp(-0.5 * x * x) * (1.0 / math.sqrt(2.0 * math.pi))


def _silu_grad(a):
    sg = _sigmoid(a)
    return sg * (1.0 + a * (1.0 - sg))


def _rowwise(name, fn, rows, vecs, out_rows, out_accs=(), tr=512, after=None):
    S = rows[0].shape[0]
    tr = min(tr, S)
    assert S % tr == 0
    nr, nv, no, na = len(rows), len(vecs), len(out_rows), len(out_accs)
    deps = [] if after is None else [after]

    def body(*refs):
        ins, outs = refs[:nr + nv], refs[nr + nv + len(deps):]
        res = fn(*[r[...] for r in ins])
        if not isinstance(res, (tuple, list)):
            res = (res,)
        for k in range(no):
            outs[k][...] = res[k].astype(outs[k].dtype)
        if na:
            @pl.when(pl.program_id(0) == 0)
            def _():
                for k in range(na):
                    outs[no + k][...] = jnp.zeros_like(outs[no + k])
            for k in range(na):
                outs[no + k][...] += res[no + k]

    in_specs = [pl.BlockSpec((tr, a.shape[1]), lambda i: (i, 0)) for a in rows]
    in_specs += [pl.BlockSpec(v.shape, lambda i: (0, 0)) for v in vecs] + [ANY for _ in deps]
    out_specs = [pl.BlockSpec((tr, c), lambda i: (i, 0)) for c, _ in out_rows]
    out_specs += [pl.BlockSpec(s, lambda i: (0, 0)) for s in out_accs]
    out_shape = [jax.ShapeDtypeStruct((S, c), dt) for c, dt in out_rows]
    out_shape += [jax.ShapeDtypeStruct(s, F32) for s in out_accs]
    return pl.pallas_call(body, name=name, grid=(S // tr,), in_specs=in_specs, out_specs=out_specs,
                          out_shape=out_shape, compiler_params=_params(("arbitrary",)))(*rows, *vecs, *deps)


def _col_tile(n, cap):
    if n <= cap or n % 128:
        return n
    best = 128
    for t in range(128, cap + 1, 128):
        if n % t == 0:
            best = t
    return best


def _mm(name, As, Bs, mode, outs, epi=None, groups=None, extras=(), vecs=(), tm=512, tn_cap=1536, whole_rows=False):
    M = As[0].shape[0]
    N = Bs[0].shape[1] if mode == "nn" else Bs[0].shape[0]
    tm = min(tm, M)
    tn = _col_tile(N, tn_cap)
    assert M % tm == 0 and N % tn == 0
    npair = len(As)
    groups = groups or [0] * npair
    ng = max(groups) + 1
    nx, nv = len(extras), len(vecs)
    dn = NN if mode == "nn" else NT

    def body(*refs):
        a_refs, b_refs = refs[:npair], refs[npair:2 * npair]
        x_refs = refs[2 * npair:2 * npair + nx]
        v_refs = refs[2 * npair + nx:2 * npair + nx + nv]
        o_refs = refs[2 * npair + nx + nv:]
        step = tn if (epi is None or whole_rows) else min(tn, MXU_COLS)
        for col in range(0, tn, step):
            sl = slice(col, min(col + step, tn))
            accs = [None] * ng
            for k in range(npair):
                b = b_refs[k][:, sl] if mode == "nn" else b_refs[k][sl, :]
                d = _dot(a_refs[k][...], b, dn)
                accs[groups[k]] = d if accs[groups[k]] is None else accs[groups[k]] + d
            args = accs + [x[:, sl] for x in x_refs] + [v[:, sl] for v in v_refs]
            res = epi(*args) if epi is not None else tuple(accs)
            if not isinstance(res, (tuple, list)):
                res = (res,)
            for o, r in zip(o_refs, res):
                o[:, sl] = r.astype(o.dtype)

    in_specs = [pl.BlockSpec((tm, a.shape[1]), lambda i, j: (i, 0)) for a in As]
    if mode == "nn":
        in_specs += [pl.BlockSpec((b.shape[0], tn), lambda i, j: (0, j)) for b in Bs]
    else:
        in_specs += [pl.BlockSpec((tn, b.shape[1]), lambda i, j: (j, 0)) for b in Bs]
    in_specs += [pl.BlockSpec((tm, tn), lambda i, j: (i, j)) for _ in extras]
    in_specs += [pl.BlockSpec((1, tn), lambda i, j: (0, j)) for _ in vecs]
    out_specs = [pl.BlockSpec((tm, tn), lambda i, j: (i, j)) for _ in outs]
    out_shape = [jax.ShapeDtypeStruct((M, N), dt) for dt in outs]
    return pl.pallas_call(body, name=name, grid=(M // tm, N // tn), in_specs=in_specs, out_specs=out_specs,
                          out_shape=out_shape, compiler_params=_params(("parallel", "parallel")))(
                              *As, *Bs, *extras, *vecs)


def _mm_shared_lhs(name, A, Bs, tm=512):
    M, K = A.shape
    tm = min(tm, M)
    assert M % tm == 0
    n = len(Bs)

    def body(a_ref, *refs):
        a = a_ref[...]
        for b_ref, o_ref in zip(refs[:n], refs[n:]):
            o_ref[...] = _dot(a, b_ref[...], NT)

    return pl.pallas_call(
        body, name=name, grid=(M // tm,),
        in_specs=[pl.BlockSpec((tm, K), lambda i: (i, 0))] + [pl.BlockSpec(b.shape, lambda i: (0, 0)) for b in Bs],
        out_specs=[pl.BlockSpec((tm, b.shape[0]), lambda i: (i, 0)) for b in Bs],
        out_shape=[jax.ShapeDtypeStruct((M, b.shape[0]), F32) for b in Bs],
        compiler_params=_params(("parallel",)))(A, *Bs)


def _mm_tn_shared_rhs(name, As, B, tk=256):
    S, N = B.shape
    tk = min(tk, S)
    assert S % tk == 0
    n = len(As)

    def body(*refs):
        a_refs, b_ref, o_refs = refs[:n], refs[n], refs[n + 1:]

        @pl.when(pl.program_id(0) == 0)
        def _():
            for o_ref in o_refs:
                o_ref[...] = jnp.zeros_like(o_ref)
        b = b_ref[...]
        for a_ref, o_ref in zip(a_refs, o_refs):
            o_ref[...] += _dot(a_ref[...], b, TN)

    return pl.pallas_call(
        body, name=name, grid=(S // tk,),
        in_specs=[pl.BlockSpec((tk, a.shape[1]), lambda k: (k, 0)) for a in As] + [pl.BlockSpec((tk, N), lambda k: (k, 0))],
        out_specs=[pl.BlockSpec((a.shape[1], N), lambda k: (0, 0)) for a in As],
        out_shape=[jax.ShapeDtypeStruct((a.shape[1], N), F32) for a in As],
        compiler_params=_params(("arbitrary",)))(*As, B)


def _mm_tn(name, A, B, tk=1024, t2_cap=1536):
    S, K1 = A.shape
    N2 = B.shape[1]
    tk = min(tk, S)
    t2 = _col_tile(N2, t2_cap)
    assert S % tk == 0 and N2 % t2 == 0

    def body(a_ref, b_ref, o_ref):
        @pl.when(pl.program_id(1) == 0)
        def _():
            o_ref[...] = jnp.zeros_like(o_ref)
        o_ref[...] += _dot(a_ref[...], b_ref[...], TN)

    return pl.pallas_call(
        body, name=name, grid=(N2 // t2, S // tk),
        in_specs=[pl.BlockSpec((tk, K1), lambda j, k: (k, 0)), pl.BlockSpec((tk, t2), lambda j, k: (k, j))],
        out_specs=pl.BlockSpec((K1, t2), lambda j, k: (0, j)),
        out_shape=jax.ShapeDtypeStruct((K1, N2), F32),
        compiler_params=_params(("parallel", "arbitrary")))(A, B)


def _norm_mod(x, nw, sc, sh):
    rstd = lax.rsqrt(jnp.mean(x * x, axis=-1, keepdims=True) + EPS)
    return (x * rstd * nw) * (1.0 + sc) + sh


def _norm_mod_fwd(name, x, nw, sc, sh, after=None):
    return _rowwise(name, _norm_mod, [x], [nw, sc, sh], [(D, BF16)], after=after)[0]


def _norm_mod_bwd(name, x, dh, dres, nw, sc, gate=None, after=None):
    def fn(x, dh, dres, *rest):
        nw, sc = rest[-3:-1] if gate else rest
        rstd = lax.rsqrt(jnp.mean(x * x, axis=-1, keepdims=True) + EPS)
        xh = x * rstd
        dn = dh * (1.0 + sc)
        dxh = dn * nw
        dx = dres + rstd * (dxh - xh * jnp.mean(dxh * xh, axis=-1, keepdims=True))
        sums = [jnp.sum(dh, axis=0, keepdims=True), jnp.sum(dh * (xh * nw), axis=0, keepdims=True),
                jnp.sum(dn * xh, axis=0, keepdims=True)]
        if not gate:
            return (dx, *sums)
        dy = dx * rest[-1]
        return (dx, dy, *sums, jnp.sum(dx * rest[0], axis=0, keepdims=True), jnp.sum(dy, axis=0, keepdims=True))
    if not gate:
        return _rowwise(name, fn, [x, dh, dres], [nw, sc], [(D, F32)], [(1, D)] * 3, after=after)
    return _rowwise(name, fn, [x, dh, dres, gate[0]], [nw, sc, gate[1]], [(D, F32), (D, BF16)], [(1, D)] * 5,
                    tr=1024, after=after)


def _loss_head(x, tgt, fw, y, g):
    def fn(x, tgt, y, fw, g):
        rstd = lax.rsqrt(jnp.mean(x * x, axis=-1, keepdims=True) + EPS)
        xh = x * rstd
        err = xh * fw - tgt
        dout = err * (1.0 / D)
        dxh = dout * fw
        dx = rstd * (dxh - xh * jnp.mean(dxh * xh, axis=-1, keepdims=True))
        sq = jnp.sum(jnp.sum(err * err, axis=1, keepdims=True), axis=0, keepdims=True)
        return (dx, dx * g, sq, jnp.sum(dout * xh, axis=0, keepdims=True), jnp.sum(dx * y, axis=0, keepdims=True))
    return _rowwise("loss_head", fn, [x, tgt, y], [fw, g], [(D, F32), (D, BF16)], [(1, 1), (1, D), (1, D)], tr=1024)


def _ffn_fwd(tag, h, wg, wu, wd, x, g2, next_norm=None):
    def act(a, b):
        return a, b, a * _sigmoid(a) * b
    a, b, f = _mm(f"ffn_up_{tag}", [h, h], [wg, wu], "nt", [BF16, BF16, BF16], epi=act, groups=[0, 1], tn_cap=FFN)

    if next_norm is None:
        def res(y, x, g):
            return y, x + g * y
        y, xo = _mm(f"ffn_down_{tag}", [f], [wd], "nn", [F32, F32], epi=res, extras=[x], vecs=[g2])
        return a, b, f, y, xo, None

    def res_norm(y, x, g, nw, sc, sh):
        xo = x + g * y
        return y, xo, _norm_mod(xo, nw, sc, sh)
    assert wd.shape[1] == D
    y, xo, h_next = _mm(f"ffn_down_{tag}", [f], [wd], "nn", [F32, F32, BF16], epi=res_norm, extras=[x],
                        vecs=[g2, *next_norm], whole_rows=True)
    return a, b, f, y, xo, h_next


def _ffn_bwd(tag, dy, h, a, b, f, wg, wu, wd):
    def act_bwd(df, a, b):
        a, b = a.astype(F32), b.astype(F32)
        sg = _sigmoid(a)
        return df * b * (sg * (1.0 + a * (1.0 - sg))), df * (a * sg)
    da, db = _mm(f"ffn_dact_{tag}", [dy], [wd], "nt", [BF16, BF16], epi=act_bwd, extras=[a, b], tn_cap=FFN)
    dwd = _mm_tn(f"ffn_dwd_{tag}", f, dy)
    dwg = _mm_tn(f"ffn_dwg_{tag}", da, h)
    dwu = _mm_tn(f"ffn_dwu_{tag}", db, h)
    dh = _mm(f"ffn_dh_{tag}", [da, db], [wg, wu], "nn", [F32])[0]
    return dh, dwg, dwu, dwd


def _conv_fwd(xr, w, b, tb=512):
    S, C = xr.shape
    tb = min(tb, S)

    def body(x_ref, halo_ref, w_ref, b_ref, pre_ref, out_ref):
        i = pl.program_id(0)
        halo = jnp.where(i > 0, halo_ref[...], 0.0)
        xe = jnp.concatenate([halo, x_ref[...]], axis=0)
        pre = w_ref[3:4, :] * x_ref[...] + b_ref[...]
        for j in (1, 2, 3):
            pre = pre + w_ref[3 - j:4 - j, :] * pltpu.roll(xe, j, axis=0)[8:, :]
        pre_ref[...] = pre
        out_ref[...] = pre * _sigmoid(pre)

    return pl.pallas_call(
        body, name="conv_fwd", grid=(S // tb,),
        in_specs=[pl.BlockSpec((tb, C), lambda i: (i, 0)),
                  pl.BlockSpec((8, C), lambda i: (jnp.maximum(i * (tb // 8) - 1, 0), 0)),
                  pl.BlockSpec((4, C), lambda i: (0, 0)), pl.BlockSpec((1, C), lambda i: (0, 0))],
        out_specs=[pl.BlockSpec((tb, C), lambda i: (i, 0))] * 2,
        out_shape=[jax.ShapeDtypeStruct((S, C), F32)] * 2,
        compiler_params=_params(("parallel",)))(xr, xr, w, b)


def _conv_bwd(dxc, pre, xr, w, tb=512):
    S, C = xr.shape
    tb = min(tb, S)
    nblk = S // tb

    def body(d_ref, p_ref, dn_ref, pn_ref, x_ref, w_ref, dx_ref, dw_ref, db_ref):
        i = pl.program_id(0)

        @pl.when(i == 0)
        def _():
            dw_ref[...] = jnp.zeros_like(dw_ref)
            db_ref[...] = jnp.zeros_like(db_ref)

        dpre = d_ref[...] * _silu_grad(p_ref[...])
        dnext = jnp.where(i < nblk - 1, dn_ref[...] * _silu_grad(pn_ref[...]), 0.0)
        pe = jnp.concatenate([dpre, dnext], axis=0)
        xx = x_ref[...]
        dx = w_ref[3:4, :] * dpre
        dw_ref[3:4, :] += jnp.sum(dpre * xx, axis=0, keepdims=True)
        for j in (1, 2, 3):
            ahead = pltpu.roll(pe, tb + 8 - j, axis=0)[:tb, :]
            dx = dx + w_ref[3 - j:4 - j, :] * ahead
            dw_ref[3 - j:4 - j, :] += jnp.sum(ahead * xx, axis=0, keepdims=True)
        dx_ref[...] = dx.astype(dx_ref.dtype)
        db_ref[...] += jnp.sum(dpre, axis=0, keepdims=True)

    blk = pl.BlockSpec((tb, C), lambda i: (i, 0))
    nxt = pl.BlockSpec((8, C), lambda i: (jnp.minimum((i + 1) * (tb // 8), S // 8 - 1), 0))
    return pl.pallas_call(
        body, name="conv_bwd", grid=(nblk,),
        in_specs=[blk, blk, nxt, nxt, blk, pl.BlockSpec((4, C), lambda i: (0, 0))],
        out_specs=[blk, pl.BlockSpec((4, C), lambda i: (0, 0)), pl.BlockSpec((1, C), lambda i: (0, 0))],
        out_shape=[jax.ShapeDtypeStruct((S, C), BF16), jax.ShapeDtypeStruct((4, C), F32),
                   jax.ShapeDtypeStruct((1, C), F32)],
        compiler_params=_params(("arbitrary",)))(dxc, pre, dxc, pre, xr, w)


def _iota(shape, dim):
    return lax.broadcasted_iota(jnp.int32, shape, dim)


def _colsel(m, lane, h):
    return jnp.sum(jnp.where(lane == h, m, 0.0), axis=1, keepdims=True)


def _cumsum_rows(v):
    r = _iota(v.shape, 0)
    k = 1
    while k < v.shape[0]:
        v = v + jnp.where(r >= k, pltpu.roll(v, k, axis=0), 0.0)
        k *= 2
    return v


def _suffix_sum_rows(v):
    n = v.shape[0]
    r = _iota(v.shape, 0)
    k = 1
    while k < n:
        v = v + jnp.where(r < n - k, pltpu.roll(v, n - k, axis=0), 0.0)
        k *= 2
    return v


def _ssd_fwd(xc, dtr, z, dtb, alog, dskl, nw):
    S = xc.shape[0]
    nc = S // L

    def body(xc_ref, dtr_ref, z_ref, dtb_ref, alog_ref, dsk_ref, nw_ref, ya_ref, y_ref, prev_ref,
             st_ref, cum_ref, cumT_ref):
        i = pl.program_id(0)

        @pl.when(i == 0)
        def _():
            st_ref[...] = jnp.zeros_like(st_ref)

        lane = _iota((L, 128), 1)
        lane1 = _iota((1, 128), 1)
        lo = lane < 64
        lo1 = lane1 < 64
        tril = _iota((L, L), 0) >= _iota((L, L), 1)
        dt = _softplus(dtr_ref[...] + dtb_ref[...])
        a_neg = -jnp.exp(alog_ref[...])
        cum = _cumsum_rows(dt * a_neg)
        cum_ref[...] = cum
        cumT_ref[...] = cum.T
        last_all = cum_ref[L - 1:L, :]
        prev_t = st_ref[...]
        prev_ref[0] = prev_t
        for g in range(2):
            bg = xc_ref[:, 1024 + g * 128:1152 + g * 128]
            cg = xc_ref[:, 1280 + g * 128:1408 + g * 128]
            gmat = _dot(cg, bg, NT)
            yoff = _dot(cg, prev_t[:, g * 512:(g + 1) * 512])
            bg_t = bg.T
            for jp in range(4):
                j = g * 4 + jp
                sl = slice(j * 128, (j + 1) * 128)
                xp = xc_ref[:, sl]
                cc = [_colsel(cum, lane, 2 * j), _colsel(cum, lane, 2 * j + 1)]
                cum_l = jnp.where(lo, cc[0], cc[1])
                dt_l = jnp.where(lo, _colsel(dt, lane, 2 * j), _colsel(dt, lane, 2 * j + 1))
                last_l = jnp.where(lo1, _colsel(last_all, lane1, 2 * j), _colsel(last_all, lane1, 2 * j + 1))
                xd = xp * dt_l
                ys = []
                for hh in range(2):
                    seg = cc[hh] - cumT_ref[2 * j + hh:2 * j + hh + 1, :]
                    dm = jnp.where(tril, jnp.exp(seg), 0.0)
                    ys.append(_dot(gmat * dm, xd))
                y_ref[:, sl] = (jnp.where(lo, ys[0], ys[1]) + jnp.exp(cum_l) * yoff[:, jp * 128:(jp + 1) * 128]
                                + dsk_ref[:, sl] * xp)
                st_ref[:, sl] = prev_t[:, sl] * jnp.exp(last_l) + _dot(bg_t, xd * jnp.exp(last_l - cum_l))
        for g in range(2):
            sl = slice(g * 512, (g + 1) * 512)
            zz = z_ref[:, sl]
            yg = y_ref[:, sl] * (zz * _sigmoid(zz))
            rstd = lax.rsqrt(jnp.mean(yg * yg, axis=-1, keepdims=True) + EPS)
            ya_ref[:, sl] = (yg * rstd * nw_ref[:, sl]).astype(ya_ref.dtype)

    blk = lambda c: pl.BlockSpec((L, c), lambda i: (i, 0))
    vec = lambda c: pl.BlockSpec((1, c), lambda i: (0, 0))
    return pl.pallas_call(
        body, name="ssd_fwd", grid=(nc,),
        in_specs=[blk(1536), blk(128), blk(1024), vec(128), vec(128), vec(1024), vec(1024)],
        out_specs=[blk(1024), blk(1024), pl.BlockSpec((1, NSTATE, 1024), lambda i: (i, 0, 0))],
        out_shape=[jax.ShapeDtypeStruct((S, 1024), BF16), jax.ShapeDtypeStruct((S, 1024), F32),
                   jax.ShapeDtypeStruct((nc, NSTATE, 1024), F32)],
        scratch_shapes=[pltpu.VMEM((NSTATE, 1024), F32), pltpu.VMEM((L, 128), F32), pltpu.VMEM((L, 128), F32)],
        compiler_params=_params(("arbitrary",)))(xc, dtr, z, dtb, alog, dskl, nw)


def _ssd_bwd(dya, y, z, xc, dtr, prev, dtb, alog, dskl, nw):
    S = xc.shape[0]
    nc = S // L

    def body(dya_ref, y_ref, z_ref, xc_ref, dtr_ref, prev_ref, dtb_ref, alog_ref, dsk_ref, nw_ref,
             dz_ref, dxc_ref, ddtr_ref, dnw_ref, ddsk_ref, dalog_ref, ddtb_ref,
             dst_ref, cum_ref, cumT_ref, dy_ref, dskacc_ref):
        i = pl.program_id(0)

        @pl.when(i == 0)
        def _():
            dst_ref[...] = jnp.zeros_like(dst_ref)
            dskacc_ref[...] = jnp.zeros_like(dskacc_ref)
            dnw_ref[...] = jnp.zeros_like(dnw_ref)
            dalog_ref[...] = jnp.zeros_like(dalog_ref)
            ddtb_ref[...] = jnp.zeros_like(ddtb_ref)

        lane = _iota((L, 128), 1)
        lane1 = _iota((1, 128), 1)
        lo = lane < 64
        lo1 = lane1 < 64
        r2, c2 = _iota((L, L), 0), _iota((L, L), 1)
        tril = r2 >= c2
        triu = r2 <= c2
        is_last = _iota((L, 1), 0) == L - 1

        for g in range(2):
            sl = slice(g * 512, (g + 1) * 512)
            zz = z_ref[:, sl]
            sg = _sigmoid(zz)
            zg = zz * sg
            yv = y_ref[:, sl]
            yg = yv * zg
            rstd = lax.rsqrt(jnp.mean(yg * yg, axis=-1, keepdims=True) + EPS)
            xh = yg * rstd
            d_out = dya_ref[:, sl]
            dnw_ref[:, sl] += jnp.sum(d_out * xh, axis=0, keepdims=True)
            dyn = d_out * nw_ref[:, sl]
            dyg = rstd * (dyn - xh * jnp.mean(dyn * xh, axis=-1, keepdims=True))
            dy_ref[:, sl] = dyg * zg
            dz_ref[:, sl] = (dyg * yv * (sg * (1.0 + zz * (1.0 - sg)))).astype(dz_ref.dtype)

        dtin = dtr_ref[...] + dtb_ref[...]
        dt = _softplus(dtin)
        a_neg = -jnp.exp(alog_ref[...])
        cum = _cumsum_rows(dt * a_neg)
        cum_ref[...] = cum
        cumT_ref[...] = cum.T
        last_all = cum_ref[L - 1:L, :]
        prev_t = prev_ref[0]
        dn_t = dst_ref[...]
        dcum = jnp.zeros((L, 128), F32)
        ddt = jnp.zeros((L, 128), F32)
        for g in range(2):
            gsl = slice(g * 512, (g + 1) * 512)
            bg = xc_ref[:, 1024 + g * 128:1152 + g * 128]
            cg = xc_ref[:, 1280 + g * 128:1408 + g * 128]
            gmat = _dot(cg, bg, NT)
            gmat_t = _dot(bg, cg, NT)
            pg = prev_t[:, gsl]
            zmat = _dot(cg, pg)
            dgm = jnp.zeros((L, L), F32)
            dgm_t = jnp.zeros((L, L), F32)
            db_acc = jnp.zeros((L, NSTATE), F32)
            dz_parts, cd_parts = [], []
            for jp in range(4):
                j = g * 4 + jp
                sl = slice(j * 128, (j + 1) * 128)
                xp = xc_ref[:, sl]
                dyp = dy_ref[:, sl]
                cc = [_colsel(cum, lane, 2 * j), _colsel(cum, lane, 2 * j + 1)]
                lc = [_colsel(last_all, lane1, 2 * j), _colsel(last_all, lane1, 2 * j + 1)]
                cum_l = jnp.where(lo, cc[0], cc[1])
                dt_l = jnp.where(lo, _colsel(dt, lane, 2 * j), _colsel(dt, lane, 2 * j + 1))
                last_l = jnp.where(lo1, lc[0], lc[1])
                e_l = jnp.exp(cum_l)
                dte_l = jnp.exp(last_l - cum_l)
                cd_l = jnp.exp(last_l)
                cd_parts.append(cd_l)
                xd = xp * dt_l
                dskacc_ref[:, sl] += jnp.sum(dyp * xp, axis=0, keepdims=True)
                dxp = dsk_ref[:, sl] * dyp
                t = dyp * (e_l * zmat[:, jp * 128:(jp + 1) * 128])
                dcc = [jnp.sum(jnp.where(lo, t, 0.0), axis=1, keepdims=True),
                       jnp.sum(jnp.where(lo, 0.0, t), axis=1, keepdims=True)]
                dz_parts.append(e_l * dyp)
                dnp_ = dn_t[:, sl]
                t2 = jnp.sum(dnp_ * prev_t[:, sl], axis=0, keepdims=True)
                dcd = [jnp.sum(jnp.where(lo1, t2, 0.0), axis=1, keepdims=True),
                       jnp.sum(jnp.where(lo1, 0.0, t2), axis=1, keepdims=True)]
                wm = _dot(bg, dnp_)
                dxd = wm * dte_l
                t3 = wm * xd
                ddte = [jnp.sum(jnp.where(lo, t3, 0.0), axis=1, keepdims=True),
                        jnp.sum(jnp.where(lo, 0.0, t3), axis=1, keepdims=True)]
                db_acc = db_acc + _dot(xd * dte_l, dnp_, NT)
                for hh in range(2):
                    h = 2 * j + hh
                    half = lo if hh == 0 else jnp.logical_not(lo)
                    row = cumT_ref[h:h + 1, :]
                    dm = jnp.where(tril, jnp.exp(cc[hh] - row), 0.0)
                    dm_t = jnp.where(triu, jnp.exp(row - cc[hh]), 0.0)
                    dym = jnp.where(half, dyp, 0.0)
                    u = _dot(dym, xd, NT) * dm
                    u_t = _dot(xd, dym, NT) * dm_t
                    dxd = dxd + _dot(gmat_t * dm_t, dym)
                    dcc[hh] = dcc[hh] + jnp.sum(u * gmat, axis=1, keepdims=True) - jnp.sum(u_t * gmat_t, axis=1, keepdims=True)
                    dgm = dgm + u
                    dgm_t = dgm_t + u_t
                    dte_c = jnp.exp(lc[hh] - cc[hh])
                    dcc[hh] = dcc[hh] - ddte[hh] * dte_c
                    endc = dcd[hh] * jnp.exp(lc[hh]) + jnp.sum(ddte[hh] * dte_c, axis=0, keepdims=True)
                    dcc[hh] = dcc[hh] + jnp.where(is_last, endc, 0.0)
                    dcum = jnp.where(lane == h, dcc[hh], dcum)
                dxc_ref[:, sl] = dxp + dxd * dt_l
                t4 = dxd * xp
                ddt = jnp.where(lane == 2 * j, jnp.sum(jnp.where(lo, t4, 0.0), axis=1, keepdims=True), ddt)
                ddt = jnp.where(lane == 2 * j + 1, jnp.sum(jnp.where(lo, 0.0, t4), axis=1, keepdims=True), ddt)
            dzg = jnp.concatenate(dz_parts, axis=1)
            dst_ref[:, gsl] = dn_t[:, gsl] * jnp.concatenate(cd_parts, axis=1) + _dot(cg.T, dzg)
            dxc_ref[:, 1280 + g * 128:1408 + g * 128] = _dot(dgm, bg) + _dot(dzg, pg, NT)
            dxc_ref[:, 1024 + g * 128:1152 + g * 128] = _dot(dgm_t, cg) + db_acc
        dla = _suffix_sum_rows(dcum)
        ddt = ddt + dla * a_neg
        dalog_ref[...] += jnp.sum(dla * dt, axis=0, keepdims=True) * a_neg
        ddtr = jnp.where(lane < 16, ddt * _sigmoid(dtin), 0.0)
        ddtr_ref[...] = ddtr.astype(ddtr_ref.dtype)
        ddtb_ref[...] += jnp.sum(ddtr, axis=0, keepdims=True)

        @pl.when(i == nc - 1)
        def _():
            seg = (_iota((1024, 128), 0) // 64 == _iota((1024, 128), 1)).astype(F32)
            acc8 = jnp.broadcast_to(dskacc_ref[...], (8, 1024))
            ddsk_ref[...] = lax.dot_general(acc8, seg, NN, precision=lax.Precision.HIGHEST,
                                            preferred_element_type=F32)

    rev = lambda c: pl.BlockSpec((L, c), lambda i: (nc - 1 - i, 0))
    vec = lambda c: pl.BlockSpec((1, c), lambda i: (0, 0))
    return pl.pallas_call(
        body, name="ssd_bwd", grid=(nc,),
        in_specs=[rev(1024), rev(1024), rev(1024), rev(1536), rev(128),
                  pl.BlockSpec((1, NSTATE, 1024), lambda i: (nc - 1 - i, 0, 0)),
                  vec(128), vec(128), vec(1024), vec(1024)],
        out_specs=[rev(1024), rev(1536), rev(128), vec(1024), pl.BlockSpec((8, 128), lambda i: (0, 0)),
                   vec(128), vec(128)],
        out_shape=[jax.ShapeDtypeStruct((S, 1024), BF16), jax.ShapeDtypeStruct((S, 1536), F32),
                   jax.ShapeDtypeStruct((S, 128), BF16), jax.ShapeDtypeStruct((1, 1024), F32),
                   jax.ShapeDtypeStruct((8, 128), F32), jax.ShapeDtypeStruct((1, 128), F32),
                   jax.ShapeDtypeStruct((1, 128), F32)],
        scratch_shapes=[pltpu.VMEM((NSTATE, 1024), F32), pltpu.VMEM((L, 128), F32), pltpu.VMEM((L, 128), F32),
                        pltpu.VMEM((L, 1024), F32), pltpu.VMEM((1, 1024), F32)],
        compiler_params=_params(("arbitrary",)))(dya, y, z, xc, dtr, prev, dtb, alog, dskl, nw)


def _layer_norm_parts(vg):
    mu = jnp.mean(vg, axis=-1, keepdims=True)
    vc = vg - mu
    rstd = lax.rsqrt(jnp.mean(vc * vc, axis=-1, keepdims=True) + EPS)
    return vc * rstd, rstd


def _gmlp_fwd(u, v, lnw, lnb, ws, bse, tb=512):
    S = u.shape[0]
    tb = min(tb, S)

    def body(u_ref, v_ref, lnw_ref, lnb_ref, ws_ref, bse_ref, o_ref, vn_ref):
        tril = _iota((L, L), 0) >= _iota((L, L), 1)
        xh, _ = _layer_norm_parts(_gelu(v_ref[...]))
        vn_ref[...] = xh * lnw_ref[...] + lnb_ref[...]
        for g in range(8):
            w = jnp.where(tril, ws_ref[g], 0.0)
            gs = slice(g * 128, (g + 1) * 128)
            for ch in range(tb // L):
                rs = slice(ch * L, (ch + 1) * L)
                sv = _dot(w, vn_ref[rs, gs]) + bse_ref[g]
                o_ref[rs, gs] = (_gelu(u_ref[rs, gs]) * sv).astype(o_ref.dtype)

    blk = pl.BlockSpec((tb, 1024), lambda i: (i, 0))
    vec = pl.BlockSpec((1, 1024), lambda i: (0, 0))
    cube = pl.BlockSpec((8, L, 128), lambda i: (0, 0, 0))
    return pl.pallas_call(
        body, name="gmlp_fwd", grid=(S // tb,), in_specs=[blk, blk, vec, vec, cube, cube], out_specs=blk,
        out_shape=jax.ShapeDtypeStruct((S, 1024), BF16), scratch_shapes=[pltpu.VMEM((tb, 1024), F32)],
        compiler_params=_params(("parallel",)))(u, v, lnw, lnb, ws, bse)


def _gmlp_bwd(dyb, u, v, lnw, lnb, ws, bse, tb=512):
    S = u.shape[0]
    tb = min(tb, S)

    def body(d_ref, u_ref, v_ref, lnw_ref, lnb_ref, ws_ref, bse_ref,
             du_ref, dv_ref, dws_ref, dbse_ref, dlnw_ref, dlnb_ref, vn_ref, dvn_ref):
        @pl.when(pl.program_id(0) == 0)
        def _():
            dws_ref[...] = jnp.zeros_like(dws_ref)
            dbse_ref[...] = jnp.zeros_like(dbse_ref)
            dlnw_ref[...] = jnp.zeros_like(dlnw_ref)
            dlnb_ref[...] = jnp.zeros_like(dlnb_ref)

        tril = _iota((L, L), 0) >= _iota((L, L), 1)
        vv = v_ref[...]
        xh, rstd = _layer_norm_parts(_gelu(vv))
        vn_ref[...] = xh * lnw_ref[...] + lnb_ref[...]
        for g in range(8):
            w = jnp.where(tril, ws_ref[g], 0.0)
            w_t = w.T
            gs = slice(g * 128, (g + 1) * 128)
            dw = jnp.zeros((L, L), F32)
            dbs = jnp.zeros((L, 128), F32)
            for ch in range(tb // L):
                rs = slice(ch * L, (ch + 1) * L)
                vn = vn_ref[rs, gs]
                sv = _dot(w, vn) + bse_ref[g]
                uu = u_ref[rs, gs]
                dd = d_ref[rs, gs]
                du_ref[rs, gs] = (dd * sv * _gelu_grad(uu)).astype(du_ref.dtype)
                dsv = dd * _gelu(uu)
                dw = dw + _dot(dsv, vn, NT)
                dbs = dbs + dsv
                dvn_ref[rs, gs] = _dot(w_t, dsv)
            dws_ref[g] += jnp.where(tril, dw, 0.0)
            dbse_ref[g] += dbs
        dvn = dvn_ref[...]
        dlnw_ref[...] += jnp.sum(dvn * xh, axis=0, keepdims=True)
        dlnb_ref[...] += jnp.sum(dvn, axis=0, keepdims=True)
        dxh = dvn * lnw_ref[...]
        dvg = rstd * (dxh - jnp.mean(dxh, axis=-1, keepdims=True) - xh * jnp.mean(dxh * xh, axis=-1, keepdims=True))
        dv_ref[...] = (dvg * _gelu_grad(vv)).astype(dv_ref.dtype)

    blk = pl.BlockSpec((tb, 1024), lambda i: (i, 0))
    vec = pl.BlockSpec((1, 1024), lambda i: (0, 0))
    cube = pl.BlockSpec((8, L, 128), lambda i: (0, 0, 0))
    return pl.pallas_call(
        body, name="gmlp_bwd", grid=(S // tb,), in_specs=[blk, blk, blk, vec, vec, cube, cube],
        out_specs=[blk, blk, cube, cube, vec, vec],
        out_shape=[jax.ShapeDtypeStruct((S, 1024), BF16), jax.ShapeDtypeStruct((S, 1024), BF16),
                   jax.ShapeDtypeStruct((8, L, 128), F32), jax.ShapeDtypeStruct((8, L, 128), F32),
                   jax.ShapeDtypeStruct((1, 1024), F32), jax.ShapeDtypeStruct((1, 1024), F32)],
        scratch_shapes=[pltpu.VMEM((tb, 1024), F32), pltpu.VMEM((tb, 1024), F32)],
        compiler_params=_params(("arbitrary",)))(dyb, u, v, lnw, lnb, ws, bse)


def _lane_sum(name, a):
    def body(a_ref, o_ref):
        o_ref[...] = jnp.sum(a_ref[...], axis=1, keepdims=True)
    return pl.pallas_call(body, name=name, out_shape=jax.ShapeDtypeStruct((a.shape[0], 1), F32))(a)


def _bucket_onehot_t():
    qi = np.arange(L)[:, None]
    sj = np.arange(2 * L)[None, :]
    dist = np.maximum(qi + L - sj, 0)
    log_ratio = (np.log(np.maximum(dist, 1).astype(np.float32) / np.float32(16)) / np.float32(math.log(128 / 16)))
    large = 16 + (log_ratio.astype(np.float32) * np.float32(16)).astype(np.int32)
    bucket = np.where(dist < 16, dist, np.minimum(large, 31)).reshape(-1)
    return (np.arange(32)[:, None] == bucket[None, :]).astype(np.float32)


def _rel_bias(table_t, onehot_t):
    def body(t_ref, oh_ref, o_ref):
        o_ref[...] = lax.dot_general(t_ref[...], oh_ref[...], NN, precision=lax.Precision.HIGHEST,
                                     preferred_element_type=F32)
    return pl.pallas_call(body, name="rel_bias", out_shape=jax.ShapeDtypeStruct((16, L * 2 * L), F32),
                          compiler_params=_params())(table_t, onehot_t)


def _rel_bias_bwd(dbias, onehot_t):
    def body(d_ref, oh_ref, o_ref):
        o_ref[...] = lax.dot_general(d_ref[...], oh_ref[...], NT, precision=lax.Precision.HIGHEST,
                                     preferred_element_type=F32)
    return pl.pallas_call(body, name="rel_bias_bwd", out_shape=jax.ShapeDtypeStruct((16, 32), F32),
                          compiler_params=_params())(dbias, onehot_t)


def _band(kp, kc, lo):
    kk = jnp.concatenate([kp, kc], axis=0)
    kr = pltpu.roll(kk, 64, axis=1)
    return [jnp.where(lo, kk, kr), jnp.where(lo, kr, kk)]


def _attn_rows(ref, j, lo):
    parts = []
    for t in range(8):
        pair = ref[:, (4 * j + t // 2) * 128:(4 * j + t // 2 + 1) * 128]
        parts.append(jnp.where(lo if t % 2 == 0 else jnp.logical_not(lo), pair, 0.0))
    return jnp.concatenate(parts, axis=0)


def _attn_mask(i, rows):
    qi, sj = _iota((rows, 2 * L), 0) & (L - 1), _iota((rows, 2 * L), 1)
    rel = qi + L - sj
    return (rel >= 0) & (rel < L) & ((sj >= L) | (i > 0))


def _per_head_col(vals):
    return jnp.concatenate([jnp.broadcast_to(v, (L, 1)) for v in vals], axis=0)


SMEM = pl.BlockSpec(memory_space=pltpu.SMEM)


def _attn_fwd(qkv, bias, sinks):
    S = qkv.shape[0]
    nb = S // L
    scale = 64 ** -0.5

    def body(sink_ref, q_ref, kc_ref, vc_ref, kp_ref, vp_ref, bias_ref, o_ref, lse_ref):
        i = pl.program_id(0)
        lane = _iota((L, 128), 1)
        lo = lane < 64
        lo2 = _iota((2 * L, 128), 1) < 64
        mask = _attn_mask(i, L)
        kd = _band(kp_ref[...], kc_ref[...], lo2)
        vd = _band(vp_ref[...], vc_ref[...], lo2)
        lse = jnp.zeros((L, 128), F32)
        for pr in range(8):
            sl = slice(pr * 128, (pr + 1) * 128)
            qp = q_ref[:, sl]
            j = pr // 4
            outs = []
            for hh in range(2):
                h = 2 * pr + hh
                qm = jnp.where(lo if hh == 0 else jnp.logical_not(lo), qp, 0.0)
                lg = jnp.where(mask, _dot(qm, kd[j], NT) * scale + bias_ref[h], NEG_INF)
                s = sink_ref[h]
                m = jnp.maximum(jnp.max(lg, axis=1, keepdims=True), s)
                p = jnp.where(mask, jnp.exp(lg - m), 0.0)
                den = jnp.sum(p, axis=1, keepdims=True) + jnp.exp(s - m)
                outs.append(_dot(p * (1.0 / den), vd[j]))
                lse = jnp.where(lane == h, m + jnp.log(den), lse)
            o_ref[:, sl] = jnp.where(lo, outs[0], outs[1]).astype(o_ref.dtype)
        lse_ref[...] = lse

    prev = lambda col: pl.BlockSpec((L, 128), lambda i: (jnp.maximum(i - 1, 0), col))
    cur = lambda col: pl.BlockSpec((L, 128), lambda i: (i, col))
    return pl.pallas_call(
        body, name="attn_fwd", grid=(nb,),
        in_specs=[SMEM, pl.BlockSpec((L, 1024), lambda i: (i, 0)), cur(8), cur(9), prev(8), prev(9),
                  pl.BlockSpec((16, L, 2 * L), lambda i: (0, 0, 0))],
        out_specs=[pl.BlockSpec((L, 1024), lambda i: (i, 0)), pl.BlockSpec((L, 128), lambda i: (i, 0))],
        out_shape=[jax.ShapeDtypeStruct((S, 1024), BF16), jax.ShapeDtypeStruct((S, 128), F32)],
        compiler_params=_params(("parallel",)))(sinks, qkv, qkv, qkv, qkv, qkv, bias)


def _attn_bwd(qkv, d_o, lse, bias, sinks):
    S = qkv.shape[0]
    nb = S // L
    scale = 64 ** -0.5

    def body(sink_ref, q_ref, kc_ref, vc_ref, kp_ref, vp_ref, do_ref, lse_ref, bias_ref,
             dq_ref, dkv_ref, dbias_ref, dsink_ref, dbq_ref, dbkv_ref, carry_ref):
        i = pl.program_id(0)

        @pl.when(i == 0)
        def _():
            dbias_ref[...] = jnp.zeros_like(dbias_ref)
            dsink_ref[...] = jnp.zeros_like(dsink_ref)
            dbq_ref[...] = jnp.zeros_like(dbq_ref)
            dbkv_ref[...] = jnp.zeros_like(dbkv_ref)
            carry_ref[...] = jnp.zeros_like(carry_ref)

        @pl.when(i < nb)
        def _():
            lane = _iota((L, 128), 1)
            lane1 = _iota((1, 128), 1)
            lo = lane < 64
            lo2 = _iota((2 * L, 128), 1) < 64
            mask = _attn_mask(i, 8 * L)
            kd = _band(kp_ref[...], kc_ref[...], lo2)
            vd = _band(vp_ref[...], vc_ref[...], lo2)
            lse_all = lse_ref[...]
            dsink = jnp.zeros((1, 128), F32)
            tot_k, tot_v = [], []
            for j in range(2):
                q_all = _attn_rows(q_ref, j, lo)
                do_all = _attn_rows(do_ref, j, lo)
                lse_col = _per_head_col([_colsel(lse_all, lane, 8 * j + t) for t in range(8)])
                lg = _dot(q_all, kd[j], NT) * scale + bias_ref[8 * j:8 * j + 8].reshape(8 * L, 2 * L)
                p = jnp.where(mask, jnp.exp(jnp.where(mask, lg, NEG_INF) - lse_col), 0.0)
                dp = _dot(do_all, vd[j], NT)
                delta = jnp.sum(p * dp, axis=1, keepdims=True)
                ds = p * (dp - delta)
                dbias_ref[8 * j:8 * j + 8] += ds.reshape(8, L, 2 * L)
                s = _per_head_col([sink_ref[8 * j + t] for t in range(8)])
                sink_part = -jnp.exp(s - lse_col) * delta
                for t in range(8):
                    dsink = dsink + jnp.where(lane1 == 8 * j + t,
                                              jnp.sum(sink_part[t * L:(t + 1) * L], axis=0, keepdims=True), 0.0)
                dss = ds * scale
                dq_all = _dot(dss, kd[j])
                for t in range(0, 8, 2):
                    sl = slice((4 * j + t // 2) * 128, (4 * j + t // 2 + 1) * 128)
                    dq = jnp.where(lo, dq_all[t * L:(t + 1) * L], dq_all[(t + 1) * L:(t + 2) * L])
                    dq_ref[:, sl] = dq.astype(dq_ref.dtype)
                    dbq_ref[:, sl] += jnp.sum(dq, axis=0, keepdims=True)
                acc_k = _dot(dss, q_all, TN)
                acc_v = _dot(p, do_all, TN)
                tot_k.append(acc_k + pltpu.roll(acc_k, 64, axis=1))
                tot_v.append(acc_v + pltpu.roll(acc_v, 64, axis=1))
            dsink_ref[...] += dsink
            dkv = jnp.concatenate([jnp.where(lo2, tot_k[0], tot_k[1]), jnp.where(lo2, tot_v[0], tot_v[1])], axis=1)
            dbkv_ref[...] += jnp.sum(dkv, axis=0, keepdims=True)
            dkv_ref[...] = (carry_ref[...] + dkv[:L, :]).astype(dkv_ref.dtype)
            carry_ref[...] = dkv[L:, :]

        @pl.when(i == nb)
        def _():
            dkv_ref[...] = carry_ref[...].astype(dkv_ref.dtype)

    c = lambda i: jnp.minimum(i, nb - 1)
    prev = lambda col: pl.BlockSpec((L, 128), lambda i: (jnp.maximum(c(i) - 1, 0), col))
    cur = lambda col: pl.BlockSpec((L, 128), lambda i: (c(i), col))
    row = lambda w: pl.BlockSpec((L, w), lambda i: (c(i), 0))
    cube = pl.BlockSpec((16, L, 2 * L), lambda i: (0, 0, 0))
    vec = lambda w: pl.BlockSpec((1, w), lambda i: (0, 0))
    return pl.pallas_call(
        body, name="attn_bwd", grid=(nb + 1,),
        in_specs=[SMEM, row(1024), cur(8), cur(9), prev(8), prev(9), row(1024), row(128), cube],
        out_specs=[row(1024), pl.BlockSpec((L, 256), lambda i: (jnp.maximum(i - 1, 0), 0)), cube,
                   vec(128), vec(1024), vec(256)],
        out_shape=[jax.ShapeDtypeStruct((S, 1024), BF16), jax.ShapeDtypeStruct((S, 256), BF16),
                   jax.ShapeDtypeStruct((16, L, 2 * L), F32), jax.ShapeDtypeStruct((1, 128), F32),
                   jax.ShapeDtypeStruct((1, 1024), F32), jax.ShapeDtypeStruct((1, 256), F32)],
        scratch_shapes=[pltpu.VMEM((L, 256), F32)],
        compiler_params=_params(("arbitrary",)))(sinks, qkv, qkv, qkv, qkv, qkv, d_o, lse, bias)


def _pad_lanes(a, n=128):
    return jnp.pad(a, ((0, 0), (0, n - a.shape[1])))


def _local_step(x, tgt, mod, w_in, P, io):
    md = [[mod[l:l + 1, k * D:(k + 1) * D] for k in range(6)] for l in range(2)]
    G, g = {}, {}

    sh1, sc1, g1, sh2, sc2, g2 = md[0]
    nmw0, nfw0 = P["norm_mix_w"][0:1], P["norm_ffn_w"][0:1]
    h0 = _norm_mod_fwd("norm_mix_0", x, nmw0, sc1, sh1, after=io["start"])
    segs = {"z": w_in[0:1024], "xbc": w_in[1024:2560], "dt": jnp.pad(w_in[2560:2576], ((0, 112), (0, 0))),
            "u": w_in[2576:3600], "v": w_in[3600:4624]}
    proj = dict(zip(segs, _mm_shared_lhs("in_proj", h0, list(segs.values()))))
    conv_w, conv_b = P["conv_w"][0], P["conv_b"]
    pre, xc = _conv_fwd(proj["xbc"], conv_w, conv_b)
    dtb, alog = _pad_lanes(P["dt_bias"]), _pad_lanes(P["a_log"])
    dskl = jnp.repeat(P["d_skip"], 64, axis=1)
    ya, y_ssd, prev = _ssd_fwd(xc, proj["dt"], proj["z"], dtb, alog, dskl, P["ssm_norm_w"])
    ws = P["gmlp_ws"][0]
    bse = jnp.broadcast_to(P["gmlp_bs"][0][:, :, None], (8, L, 128))
    yb = _gmlp_fwd(proj["u"], proj["v"], P["gmlp_ln_w"], P["gmlp_ln_b"], ws, bse)
    W = dict(io["weights0"]((ya, yb)))
    w_oa, w_ob = W["out_w"][:1024], W["out_w"][1024:]

    def res(y, x, gate, nw, sc, sh):
        xo = x + gate * y
        return y, xo, _norm_mod(xo, nw, sc, sh)
    mix0, x1, h0f = _mm("out_proj_0", [ya, yb], [w_oa, w_ob], "nn", [F32, F32, BF16], epi=res, extras=[x],
                        vecs=[g1, nfw0, sc2, sh2], whole_rows=True)
    sh1b, sc1b, g1b, sh2b, sc2b, g2b = md[1]
    nmw1, nfw1 = P["norm_mix_w"][1:2], P["norm_ffn_w"][1:2]
    a0, b0, f0, y0, x2, h1 = _ffn_fwd("0", h0f, W["gate_wt0"], W["up_wt0"], W["down_w0"], x1, g2,
                                      next_norm=(nmw1, sc1b, sh1b))

    W.update(io["weights1"](x2))
    qkv = _mm("qkv_proj", [h1], [W["qkv_wt"]], "nt", [F32], epi=lambda acc, b: acc + b, vecs=[P["qkv_b"]])[0]
    onehot_t = jnp.asarray(_bucket_onehot_t())
    bias = _rel_bias(P["rel_table"].T, onehot_t).reshape(16, L, 2 * L)
    sinks = P["sinks"].reshape(16)
    att, lse = _attn_fwd(qkv, bias, sinks)

    def res_b(y, x, gate, b, nw, sc, sh):
        y = y + b
        xo = x + gate * y
        return y, xo, _norm_mod(xo, nw, sc, sh)
    mix1, x3, h1f = _mm("o_proj", [att], [W["o_w"]], "nn", [F32, F32, BF16], epi=res_b, extras=[x2],
                        vecs=[g1b, P["o_b"], nfw1, sc2b, sh2b], whole_rows=True)
    a1, b1, f1, y1, x4, _ = _ffn_fwd("1", h1f, W["gate_wt1"], W["up_wt1"], W["down_w1"], x3, g2b)

    dx, dy, sq, g["final_norm_w"], dg2b = _loss_head(x4, tgt, P["final_norm_w"], y1, g2b)

    dh, dwg1, dwu1, dwd1 = _ffn_bwd("1", dy, h1f, a1, b1, f1, W["gate_wt1"], W["up_wt1"], W["down_w1"])
    dx, dmix, dsh2b, dsc2b, dnfw1, dg1b, g["o_b"] = _norm_mod_bwd("norm_ffn_bwd_1", x3, dh, dx, nfw1, sc2b,
                                                                 gate=(mix1, g1b))
    G["o_w"] = _mm_tn("o_dw", att, dmix)
    d_att = _mm("o_dx", [dmix], [W["o_w"]], "nt", [F32])[0]
    dq, dkv, dbias, dsinks, dbq, dbkv = _attn_bwd(qkv, d_att, lse, bias, sinks)
    g["rel_table"] = _rel_bias_bwd(dbias.reshape(16, L * 2 * L), onehot_t).T
    g["sinks"] = dsinks[:, :16]
    g["qkv_b"] = jnp.concatenate([dbq, dbkv], axis=1)
    w_q, w_kv = W["qkv_wt"][:1024], W["qkv_wt"][1024:]
    G["qkv_wt"] = jnp.concatenate(_mm_tn_shared_rhs("qkv_dw", [dq, dkv], h1), axis=0)
    dh = _mm("qkv_dx", [dq, dkv], [w_q, w_kv], "nn", [F32])[0]
    behind = io["grads1"]({"qkv_wt": G.pop("qkv_wt"), "o_w": G.pop("o_w"), "gate_wt1": dwg1, "up_wt1": dwu1,
                           "down_w1": dwd1})
    dx, dy, dsh1b, dsc1b, dnmw1, dg2, _ = _norm_mod_bwd("norm_mix_bwd_1", x2, dh, dx, nmw1, sc1b, gate=(y0, g2),
                                                        after=behind)

    dh, dwg0, dwu0, dwd0 = _ffn_bwd("0", dy, h0f, a0, b0, f0, W["gate_wt0"], W["up_wt0"], W["down_w0"])
    behind = io["grads_ffn0"]({"gate_wt0": dwg0, "up_wt0": dwu0, "down_w0": dwd0})
    dx, dmix, dsh2, dsc2, dnfw0, dg1, _ = _norm_mod_bwd("norm_ffn_bwd_0", x1, dh, dx, nfw0, sc2, gate=(mix0, g1),
                                                        after=behind)
    G["out_w"] = jnp.concatenate(_mm_tn_shared_rhs("out_dw", [ya, yb], dmix), axis=0)
    dya, dyb = _mm_shared_lhs("out_dx", dmix, [w_oa, w_ob])
    du, dv, dws, dbse, g["gmlp_ln_w"], g["gmlp_ln_b"] = _gmlp_bwd(dyb, proj["u"], proj["v"], P["gmlp_ln_w"],
                                                                 P["gmlp_ln_b"], ws, bse)
    g["gmlp_ws"] = dws[None]
    g["gmlp_bs"] = _lane_sum("gmlp_dbs", dbse.reshape(8 * L, 128)).reshape(1, 8, L)
    dz, dxc, ddt, g["ssm_norm_w"], ddsk, dalog, ddtb = _ssd_bwd(dya, y_ssd, proj["z"], xc, proj["dt"], prev,
                                                                dtb, alog, dskl, P["ssm_norm_w"])
    g["d_skip"], g["a_log"], g["dt_bias"] = ddsk[0:1, :16], dalog[:, :16], ddtb[:, :16]
    dxr, dconv_w, g["conv_b"] = _conv_bwd(dxc, pre, proj["xbc"], conv_w)
    g["conv_w"] = dconv_w[None]
    dsegs = {"z": dz, "xbc": dxr, "dt": ddt, "u": du, "v": dv}
    dws_in = dict(zip(dsegs, _mm_tn_shared_rhs("in_dw", list(dsegs.values()), h0)))
    G["in_wt"] = jnp.concatenate([dws_in["z"], dws_in["xbc"], dws_in["dt"][:16], dws_in["u"], dws_in["v"]], axis=0)
    keys = ["z", "xbc", "dt", "u", "v"]
    dh = _mm("in_dx", [dsegs[k] for k in keys], [segs[k] for k in keys], "nn", [F32])[0]
    dx, dsh1, dsc1, dnmw0 = _norm_mod_bwd("norm_mix_bwd_0", x, dh, dx, nmw0, sc1)

    g["norm_mix_w"] = jnp.concatenate([dnmw0, dnmw1], axis=0)
    g["norm_ffn_w"] = jnp.concatenate([dnfw0, dnfw1], axis=0)
    dmod = jnp.concatenate([jnp.concatenate([dsh1, dsc1, dg1, dsh2, dsc2, dg2], axis=1),
                            jnp.concatenate([dsh1b, dsc1b, dg1b, dsh2b, dsc2b, dg2b], axis=1)], axis=0)
    return sq, dx, dmod, G, g


def _ada_fwd(c_all, ada_w, ada_b):
    n = ada_w.shape[2]
    tn = _col_tile(n, 512)

    def body(c_ref, w_ref, b_ref, o_ref):
        cc = c_ref[...]
        o_ref[...] = lax.dot_general(cc * _sigmoid(cc), w_ref[...], NN, precision=lax.Precision.HIGHEST,
                                     preferred_element_type=F32) + b_ref[...]

    return pl.pallas_call(
        body, name="ada_fwd", grid=(2, n // tn),
        in_specs=[pl.BlockSpec((8, D), lambda l, j: (0, 0)), pl.BlockSpec((None, D, tn), lambda l, j: (l, 0, j)),
                  pl.BlockSpec((None, 1, tn), lambda l, j: (l, 0, j))],
        out_specs=pl.BlockSpec((None, 8, tn), lambda l, j: (l, 0, j)),
        out_shape=jax.ShapeDtypeStruct((2, 8, n), F32), compiler_params=_params(("parallel", "parallel")))(
            c_all, ada_w, ada_b)


def _ada_bwd(c_all, dmod_cols, dmod_all):
    n = dmod_cols.shape[2]
    tn = _col_tile(n, 512)

    def body(c_ref, d_ref, o_ref):
        cc = c_ref[...]
        o_ref[...] = lax.dot_general(cc * _sigmoid(cc), d_ref[...], TN, precision=lax.Precision.HIGHEST,
                                     preferred_element_type=F32)

    dw = pl.pallas_call(
        body, name="ada_dw", grid=(2, n // tn),
        in_specs=[pl.BlockSpec((8, D), lambda l, j: (0, 0)), pl.BlockSpec((None, 8, tn), lambda l, j: (l, 0, j))],
        out_specs=pl.BlockSpec((None, D, tn), lambda l, j: (l, 0, j)),
        out_shape=jax.ShapeDtypeStruct((2, D, n), F32), compiler_params=_params(("parallel", "parallel")))(
            c_all, dmod_cols)

    def sum_body(d_ref, o_ref):
        o_ref[...] = jnp.sum(d_ref[...], axis=0, keepdims=True)

    db = pl.pallas_call(
        sum_body, name="ada_db", grid=(2,),
        in_specs=[pl.BlockSpec((None, 8, 6 * D), lambda l: (l, 0, 0))],
        out_specs=pl.BlockSpec((None, 1, 6 * D), lambda l: (l, 0, 0)),
        out_shape=jax.ShapeDtypeStruct((2, 1, 6 * D), F32), compiler_params=_params(("parallel",)))(dmod_all)
    return dw, db


def _row_tile(rows, cap=512, mult=8):
    best = rows
    for t in range(mult, min(rows, cap) + 1, mult):
        if rows % t == 0:
            best = t
    return best


def _adamw(name, w, g, m, v):
    def fn(w, g, m, v):
        m = ADAM_B1 * m + (1.0 - ADAM_B1) * g
        v = ADAM_B2 * v + (1.0 - ADAM_B2) * (g * g)
        m_hat = m / (1.0 - ADAM_B1 ** ADAM_STEP)
        v_hat = v / (1.0 - ADAM_B2 ** ADAM_STEP)
        return -ADAM_LR * (m_hat / (jnp.sqrt(v_hat) + ADAM_EPS) + ADAM_WD * w), m, v
    cols = w.shape[1]
    return _rowwise(name, fn, [w, g, m, v], [], [(cols, F32)] * 3, tr=_row_tile(w.shape[0]))


def _place():
    return lax.axis_index("x"), lax.axis_index("y"), lax.axis_index("c")


VMEM_SPEC = pl.BlockSpec(memory_space=pltpu.VMEM)


def _allreduce_small(name, buf, after=None):
    rows = buf.shape[0]
    deps = [] if after is None else [after]

    def body(x_ref, *rest):
        o_ref, stage, send_sems, recv_sems = rest[len(deps):]
        x, y, c = _place()
        me = 4 * x + 2 * y + c
        stage[me] = x_ref[...]
        copies = []
        for k in range(1, 8):
            peer = (1 - x if k & 4 else x, 1 - y if k & 2 else y, 1 - c if k & 1 else c)
            cp = pltpu.make_async_remote_copy(src_ref=x_ref, dst_ref=stage.at[me], send_sem=send_sems.at[k - 1],
                                              recv_sem=recv_sems.at[k - 1], device_id=peer, device_id_type=MESH)
            cp.start()
            copies.append(cp)
        for cp in copies:
            cp.wait()
        acc = stage[0]
        for d in range(1, 8):
            acc = acc + stage[d]
        o_ref[...] = acc

    return pl.pallas_call(
        body, name=name, in_specs=[VMEM_SPEC] + [ANY for _ in deps], out_specs=VMEM_SPEC,
        out_shape=jax.ShapeDtypeStruct((rows, 128), F32),
        scratch_shapes=[pltpu.VMEM((8, rows, 128), F32), pltpu.SemaphoreType.DMA((7,)), pltpu.SemaphoreType.DMA((7,))],
        compiler_params=pltpu.CompilerParams(vmem_limit_bytes=_VMEM_LIMIT))(buf, *deps)


def _sum_slots(name, own, land):
    def body(own_ref, land_ref, o_ref):
        x, y, c = _place()
        me = 4 * x + 2 * y + c
        acc = None
        for d in range(8):
            v = jnp.where(me == d, own_ref[...], land_ref[d])
            acc = v if acc is None else acc + v
        o_ref[...] = acc

    return pl.pallas_call(body, name=name, in_specs=[VMEM_SPEC, VMEM_SPEC], out_specs=VMEM_SPEC,
                          out_shape=jax.ShapeDtypeStruct(own.shape, F32),
                          compiler_params=pltpu.CompilerParams(vmem_limit_bytes=_VMEM_LIMIT))(own, land)


OTHER_CHIPS = ((1, 0), (0, 1), (1, 1))


SIBLING_COLLECTIVE_ID = 6


def _sibling_handshake():
    x, y, c = _place()
    barrier = pltpu.get_barrier_semaphore()
    pl.semaphore_signal(barrier, inc=1, device_id=(x, y, 1 - c), device_id_type=MESH)
    pl.semaphore_wait(barrier, 1)


def _sibling_swap(name, src, halves):
    half = src.shape[-2] // 2
    out_shape = (src.shape[0], half, 1024) if halves else src.shape

    def body(s_ref, o_ref, send_sem, recv_sem):
        x, y, c = _place()
        _sibling_handshake()
        part = s_ref.at[:, pl.ds(pl.multiple_of((1 - c) * half, 8), half)] if halves else s_ref
        cp = pltpu.make_async_remote_copy(src_ref=part, dst_ref=o_ref, send_sem=send_sem, recv_sem=recv_sem,
                                          device_id=(x, y, 1 - c), device_id_type=MESH)
        cp.start()
        cp.wait()

    return pl.pallas_call(
        body, name=name, in_specs=[ANY], out_specs=ANY, out_shape=jax.ShapeDtypeStruct(out_shape, src.dtype),
        scratch_shapes=[pltpu.SemaphoreType.DMA, pltpu.SemaphoreType.DMA],
        compiler_params=pltpu.CompilerParams(collective_id=SIBLING_COLLECTIVE_ID))(src)


HBM = pl.BlockSpec(memory_space=pltpu.HBM)
SEM = pl.BlockSpec(memory_space=pltpu.SEMAPHORE)


def _exchange_peers(mode):
    x, y, c = _place()
    if mode == "all":
        return [(1 - x if k & 4 else x, 1 - y if k & 2 else y, 1 - c if k & 1 else c) for k in range(1, 8)]
    return [(1 - x if fx else x, 1 - y if fy else y, c) for fx, fy in OTHER_CHIPS]


def _chip_copies(mode, src_ref, land_ref, send_sems, recv_sems):
    x, y, c = _place()
    k = 2 * x + y
    copies = []
    for j, peer in enumerate(_exchange_peers(mode)):
        if mode == "gather":
            half = src_ref.shape[0] // 2
            mine = pl.ds(pl.multiple_of(c * half, 16), half)
            src, dst = src_ref.at[mine], land_ref.at[k, mine]
        elif mode == "scatter":
            src, dst = src_ref.at[2 * peer[0] + peer[1]], land_ref.at[k]
        else:
            src, dst = src_ref, land_ref.at[4 * x + 2 * y + c]
        copies.append(pltpu.make_async_remote_copy(src_ref=src, dst_ref=dst, send_sem=send_sems.at[j],
                                                   recv_sem=recv_sems.at[j], device_id=peer, device_id_type=MESH))
    return copies


def _exchange_start(name, collective_id, mode, src, land, after=None):
    deps = [] if after is None else [after]
    npeers = 7 if mode == "all" else 3

    def body(s_ref, l_ref, *rest):
        send_sems, recv_sems, s_thru, l_thru, token = rest[len(deps):]
        barrier = pltpu.get_barrier_semaphore()
        for peer in _exchange_peers(mode):
            pl.semaphore_signal(barrier, inc=1, device_id=peer, device_id_type=MESH)
        pl.semaphore_wait(barrier, npeers)
        for cp in _chip_copies(mode, s_ref, l_ref, send_sems, recv_sems):
            cp.start()
        token[...] = jnp.zeros_like(token)

    return pl.pallas_call(
        body, name=name,
        out_shape=(pltpu.SemaphoreType.DMA((npeers,)), pltpu.SemaphoreType.DMA((npeers,)),
                   pltpu.HBM(src.shape, src.dtype),
                   pltpu.HBM(land.shape, land.dtype), jax.ShapeDtypeStruct((8, 128), F32)),
        in_specs=(HBM, HBM) + tuple(ANY for _ in deps), out_specs=(SEM, SEM, HBM, HBM, VMEM_SPEC),
        input_output_aliases={0: 2, 1: 3},
        compiler_params=pltpu.CompilerParams(has_side_effects=pltpu.SideEffectType.DATAFLOW_SIDE_EFFECTING,
                                             collective_id=collective_id))(
            pltpu.with_memory_space_constraint(src, pltpu.HBM), pltpu.with_memory_space_constraint(land, pltpu.HBM),
            *deps)


def _exchange_wait(name, mode, started, after):
    send_sems, recv_sems, s_thru, l_thru, _ = started
    deps = list(after) if isinstance(after, (tuple, list)) else [after]

    def body(s_ref, l_ref, send_sems, recv_sems, *rest):
        for cp in _chip_copies(mode, s_ref, l_ref, send_sems, recv_sems):
            cp.wait_send()
            cp.wait_recv()

    return pl.pallas_call(
        body, name=name, out_shape=(pltpu.HBM(s_thru.shape, s_thru.dtype), pltpu.HBM(l_thru.shape, l_thru.dtype)),
        in_specs=(HBM, HBM, SEM, SEM) + tuple(ANY for _ in deps), out_specs=(HBM, HBM),
        input_output_aliases={0: 0, 1: 1},
        compiler_params=pltpu.CompilerParams(has_side_effects=pltpu.SideEffectType.DATAFLOW_SIDE_EFFECTING))(
            s_thru, l_thru, send_sems, recv_sems, *deps)


def _allgather_finish(tag, land):
    half = land.shape[1] // 2

    def body(l_ref, o_ref, send_sem, recv_sem):
        x, y, c = _place()
        _sibling_handshake()
        mine = pl.ds(pl.multiple_of(c * half, 16), half)
        swap = pltpu.make_async_remote_copy(src_ref=o_ref.at[:, mine], dst_ref=o_ref.at[:, mine], send_sem=send_sem,
                                            recv_sem=recv_sem, device_id=(x, y, 1 - c), device_id_type=MESH)
        swap.start()
        swap.wait()

    return pl.pallas_call(
        body, name="allgather_finish_" + tag, in_specs=[ANY], out_specs=ANY, input_output_aliases={0: 0},
        out_shape=jax.ShapeDtypeStruct(land.shape, land.dtype),
        scratch_shapes=[pltpu.SemaphoreType.DMA, pltpu.SemaphoreType.DMA],
        compiler_params=pltpu.CompilerParams(collective_id=SIBLING_COLLECTIVE_ID))(land)


def _pair_sum(tag, g, r1, c):
    rows = g.shape[1]
    half = rows // 2
    th = _row_tile(half, 1408, 16)
    nblk = half // th

    def body(c_ref, g_ref, r_ref, o_ref, o2_ref):
        o_ref[...] = (g_ref[...].astype(F32) + r_ref[...].astype(F32)).astype(o_ref.dtype)
        o2_ref[...] = o_ref[...]

    spec = pl.BlockSpec((None, th, 1024), lambda k, i, c_ref: (k, i, 0))
    grid_spec = pltpu.PrefetchScalarGridSpec(
        num_scalar_prefetch=1, grid=(4, nblk),
        in_specs=[pl.BlockSpec((None, th, 1024), lambda k, i, c_ref: (k, c_ref[0] * nblk + i, 0)), spec],
        out_specs=[spec, spec])
    return pl.pallas_call(body, name="grad_pair_sum_" + tag, grid_spec=grid_spec,
                          out_shape=[jax.ShapeDtypeStruct((4, half, 1024), BF16)] * 2,
                          compiler_params=_params(("parallel", "parallel")))(c, g, r1)


def _chip_sum(tag, q, after=None):
    half = q.shape[1]
    th = _row_tile(half, 704, 16)
    deps = [] if after is None else [after]

    def body(a, b, c, d, *rest):
        rest[-1][...] = ((a[...].astype(F32) + b[...].astype(F32)) + c[...].astype(F32)) + d[...].astype(F32)

    specs = [pl.BlockSpec((None, th, 1024), functools.partial(lambda i, k: (k, i, 0), k=k)) for k in range(4)]
    return pl.pallas_call(body, name="grad_chip_sum_" + tag, grid=(half // th,), in_specs=specs + [ANY for _ in deps],
                          out_specs=pl.BlockSpec((th, 1024), lambda i: (i, 0)),
                          out_shape=jax.ShapeDtypeStruct((half, 1024), F32),
                          compiler_params=_params(("parallel",)))(q, q, q, q, *deps)


def _join_halves(tag, f, r, c):
    half = f.shape[0]
    th = _row_tile(half, 704)
    nblk = half // th

    def body(c_ref, f_ref, r_ref, o_ref):
        mine = (pl.program_id(0) == c_ref[0])
        o_ref[...] = jnp.where(mine, f_ref[...], r_ref[...])

    spec = pl.BlockSpec((th, 1024), lambda h, i, c_ref: (i, 0))
    grid_spec = pltpu.PrefetchScalarGridSpec(
        num_scalar_prefetch=1, grid=(2, nblk), in_specs=[spec, spec],
        out_specs=pl.BlockSpec((th, 1024), lambda h, i, c_ref: (h * nblk + i, 0)))
    return pl.pallas_call(body, name="grad_join_halves_" + tag, grid_spec=grid_spec,
                          out_shape=jax.ShapeDtypeStruct((2 * half, 1024), F32),
                          compiler_params=_params(("parallel", "parallel")))(c, f, r)


BIG_ARGS = ("in_w_even", "out_w_even", "qkv_w", "o_w", "ffn_gate_w", "ffn_up_w", "ffn_down_w")
def _ffn_pieces(layer):
    return tuple((f"{n}{layer}", 704, 704) for n in ("gate_wt", "up_wt", "down_w"))


IN_SLAB = (("in_wt", 1156, 1184),)
LAYER0_REST_SLAB = (("out_w", 512, 512),) + _ffn_pieces(0)
LAYER1_SLAB = (("qkv_wt", 320, 320), ("o_w", 256, 256)) + _ffn_pieces(1)
FFN0_SLAB = _ffn_pieces(0)
MIXER0_SLAB = (("in_wt", 1156, 1280), ("out_w", 512, 512))


def _slab(pieces, spec):
    parts = []
    for name, rows, room in spec:
        p = pieces[name]
        parts.append(jnp.pad(p, [(0, 0)] * (p.ndim - 2) + [(0, room - rows), (0, 0)]) if room > rows else p)
    return jnp.concatenate(parts, axis=-2) if len(parts) > 1 else parts[0]


def _unslab(slab, spec):
    out, off = {}, 0
    for name, rows, room in spec:
        out[name] = slab[..., off:off + rows, :]
        off += room
    return out


def _share_pieces(w):
    return {"in_wt": w["in_w_even"][0].T, "out_w": w["out_w_even"][0], "qkv_wt": w["qkv_w"][0].T, "o_w": w["o_w"][0],
            "gate_wt0": w["ffn_gate_w"][0].T, "gate_wt1": w["ffn_gate_w"][1].T,
            "up_wt0": w["ffn_up_w"][0].T, "up_wt1": w["ffn_up_w"][1].T,
            "down_w0": w["ffn_down_w"][0], "down_w1": w["ffn_down_w"][1]}


def _pieces_to_shares(p):
    return {"in_w_even": p["in_wt"].T[None], "out_w_even": p["out_w"][None], "qkv_w": p["qkv_wt"].T[None],
            "o_w": p["o_w"][None], "ffn_gate_w": jnp.stack([p["gate_wt0"].T, p["gate_wt1"].T]),
            "ffn_up_w": jnp.stack([p["up_wt0"].T, p["up_wt1"].T]),
            "ffn_down_w": jnp.stack([p["down_w0"], p["down_w1"]])}


def _chips_from_full(G, spec):
    return _slab({k: v.reshape(4, -1, D) for k, v in G.items()}, spec)


def _pack_small(parts):
    padded = []
    for p in parts:
        p = p.reshape(-1).astype(F32)
        padded.append(jnp.pad(p, (0, (-p.shape[0]) % 1024)))
    return jnp.concatenate(padded).reshape(-1, 128)


def _unpack_small(slab, shapes):
    flat, out, off = slab.reshape(-1), [], 0
    for shp in shapes:
        size = math.prod(shp)
        out.append(flat[off:off + size].reshape(shp))
        off += size + (-size) % 1024
    return out


SMALL = ("ada_b", "norm_mix_w", "norm_ffn_w", "conv_w", "conv_b", "dt_bias", "a_log", "d_skip", "ssm_norm_w",
         "gmlp_ln_w", "gmlp_ln_b", "gmlp_ws", "gmlp_bs", "qkv_b", "o_b", "sinks", "rel_table", "final_norm_w")
SMALL_SPLIT = {"conv_w": 1536, "qkv_b": 1280, "o_b": 1024}
WEIGHTS = ("ada_w", "ada_b", "norm_mix_w", "norm_ffn_w", "in_w_even", "conv_w", "conv_b", "dt_bias", "a_log", "d_skip",
           "ssm_norm_w", "gmlp_ln_w", "gmlp_ln_b", "gmlp_ws", "gmlp_bs", "out_w_even", "qkv_w", "qkv_b", "o_w", "o_b",
           "sinks", "rel_table", "ffn_gate_w", "ffn_up_w", "ffn_down_w", "final_norm_w")


def kernel(x, c, ada_w, ada_b, norm_mix_w, norm_ffn_w, in_w_even, conv_w, conv_b, dt_bias, a_log, d_skip, ssm_norm_w, gmlp_ln_w, gmlp_ln_b, gmlp_ws, gmlp_bs, out_w_even, qkv_w, qkv_b, o_w, o_b, sinks, rel_table, ffn_gate_w, ffn_up_w, ffn_down_w, final_norm_w, loss_target, m_ada_w, m_ada_b, m_norm_mix_w, m_norm_ffn_w, m_in_w_even, m_conv_w, m_conv_b, m_dt_bias, m_a_log, m_d_skip, m_ssm_norm_w, m_gmlp_ln_w, m_gmlp_ln_b, m_gmlp_ws, m_gmlp_bs, m_out_w_even, m_qkv_w, m_qkv_b, m_o_w, m_o_b, m_sinks, m_rel_table, m_ffn_gate_w, m_ffn_up_w, m_ffn_down_w, m_final_norm_w, v_ada_w, v_ada_b, v_norm_mix_w, v_norm_ffn_w, v_in_w_even, v_conv_w, v_conv_b, v_dt_bias, v_a_log, v_d_skip, v_ssm_norm_w, v_gmlp_ln_w, v_gmlp_ln_b, v_gmlp_ws, v_gmlp_bs, v_out_w_even, v_qkv_w, v_qkv_b, v_o_w, v_o_b, v_sinks, v_rel_table, v_ffn_gate_w, v_ffn_up_w, v_ffn_down_w, v_final_norm_w):
    args = dict(locals())
    w = {n: args[n] for n in WEIGHTS}
    m = {n: args["m_" + n] for n in WEIGHTS}
    v = {n: args["v_" + n] for n in WEIGHTS}
    ax, ay, ac = _place()
    me = 4 * ax + 2 * ay + ac
    chip = 2 * ax + ay
    south = (ac == 0).astype(F32)
    c_arr = jnp.reshape(ac, (1,)).astype(jnp.int32)

    c_all = _allreduce_small("gather_cond", lax.dynamic_update_slice(jnp.zeros((8, D), F32), c, (me, 0)).reshape(64, 128))
    c_all = c_all.reshape(8, D)
    n_ada = ada_w.shape[2]
    mod_cols = _ada_fwd(c_all, ada_w, lax.dynamic_slice(ada_b, (0, chip * n_ada), (2, n_ada)).reshape(2, 1, n_ada))
    pieces = [lax.dynamic_update_slice(jnp.zeros((2, 8, 6 * D), F32), mod_cols, (0, 0, chip * n_ada))]
    split_names = list(SMALL_SPLIT)
    for n in split_names:
        full = SMALL_SPLIT[n]
        local = w[n]
        idx = (0,) * (local.ndim - 1) + (chip * local.shape[-1],)
        pieces.append(lax.dynamic_update_slice(jnp.zeros(local.shape[:-1] + (full,), F32), local, idx))
    shapes = [p.shape for p in pieces]
    mod_own = _pack_small(pieces) * south
    mod_started = _exchange_start("gather_mod_start", 9, "all", mod_own, lax.empty((8,) + mod_own.shape, F32))

    pieces = _share_pieces(w)
    cast = {"in_wt": pieces["in_wt"].astype(_MXU)}

    def start_gather(tag, collective_id, share, after):
        return _exchange_start("allgather_start_" + tag, collective_id, "gather", share,
                               lax.empty((4,) + share.shape, share.dtype), after=after)

    def finish_gather(tag, started, spec, after):
        land = _exchange_wait("allgather_wait_" + tag, "gather", started, after)[1]
        out = {}
        for name, piece in _unslab(_allgather_finish(tag, land), spec).items():
            out[name] = lax.dynamic_update_slice(piece.reshape(-1, D), cast[name], (chip * piece.shape[1], 0))
        return out

    gather_in = start_gather("in", 7, _slab(cast, IN_SLAB), mod_started[4])
    zero = gather_in[4][0, 0]
    cast.update({k: (p + zero).astype(_MXU) for k, p in pieces.items() if k != "in_wt"})
    share0, share1 = _slab(cast, LAYER0_REST_SLAB), _slab(cast, LAYER1_SLAB)
    mod_own, mod_land = _exchange_wait("gather_mod_wait", "all", mod_started, (share0, share1))
    mod_slab = _sum_slots("gather_mod_sum", mod_own, mod_land)
    gathered = _unpack_small(mod_slab, shapes)
    mod = lax.dynamic_slice(gathered[0], (0, me, 0), (2, 1, 6 * D)).reshape(2, 6 * D)
    P = {n: w[n] for n in SMALL if n not in SMALL_SPLIT and n != "ada_b"}
    for n, full in zip(split_names, gathered[1:]):
        P[n] = full
    P["final_norm_w"] = final_norm_w.reshape(1, D)
    w_in = finish_gather("in", gather_in, IN_SLAB, mod_slab)["in_wt"]
    gather0 = start_gather("0", 1, share0, w_in)
    gather1 = start_gather("1", 2, share1, gather0[4])

    def start_reduce(tag, collective_id, G, spec, after=None):
        gp = _chips_from_full(G, spec).astype(BF16)
        p, q = _pair_sum(tag, gp, _sibling_swap("grad_pair_exchange_" + tag, gp, True), c_arr)
        return _exchange_start("grad_exchange_start_" + tag, collective_id, "scatter", p, q, after=after)

    def finish_reduce(tag, started, spec, after, behind=None):
        q = _exchange_wait("grad_exchange_wait_" + tag, "scatter", started, after)[1]
        fin = _chip_sum(tag, q, after=behind)
        total = _join_halves(tag, fin, _sibling_swap("grad_final_exchange_" + tag, fin, False), c_arr)
        return _unslab(total, spec)

    reduces = {}

    def grads1(G1):
        reduces["1"] = start_reduce("1", 3, G1, LAYER1_SLAB)
        return reduces["1"][4]

    def grads_ffn0(G):
        reduces["f"] = start_reduce("f", 4, G, FFN0_SLAB)
        return reduces["f"][4]

    io = {"start": gather1[4],
          "weights0": lambda after: finish_gather("0", gather0, LAYER0_REST_SLAB, after),
          "weights1": lambda after: finish_gather("1", gather1, LAYER1_SLAB, after),
          "grads1": grads1, "grads_ffn0": grads_ffn0}
    sq, grad_x, dmod, G0, g = _local_step(x[0], loss_target[0], mod, w_in, P, io)
    loss = lax.psum(0.5 * sq[0, 0] / D, ("x", "y", "c"))

    g["final_norm_w"] = g["final_norm_w"].reshape(D)
    small_names = [n for n in SMALL if n != "ada_b"]
    pieces = [lax.dynamic_update_slice(jnp.zeros((2, 8, 6 * D), F32), dmod.reshape(2, 1, 6 * D), (0, me, 0))]
    pieces += [g[n] for n in small_names]
    shapes = [p.shape for p in pieces]
    small_own = _pack_small(pieces)
    small_started = _exchange_start("small_grads_start", 8, "all", small_own, lax.empty((8,) + small_own.shape, F32))
    reduces["m"] = start_reduce("m", 5, G0, MIXER0_SLAB, after=small_started[4])
    shares = finish_reduce("1", reduces["1"], LAYER1_SLAB, grad_x, behind=reduces["m"][4])
    shares.update(finish_reduce("f", reduces["f"], FFN0_SLAB, grad_x, behind=reduces["m"][4]))
    small_own, small_land = _exchange_wait("small_grads_wait", "all", small_started, shares["down_w0"])
    reduced = _unpack_small(_sum_slots("small_grads_sum", small_own, small_land), shapes)
    dmod_all = reduced[0]
    grads = dict(zip(small_names, reduced[1:]))
    for n in split_names:
        full = grads[n]
        size = w[n].shape[-1]
        grads[n] = lax.dynamic_slice(full, (0,) * (full.ndim - 1) + (chip * size,), full.shape[:-1] + (size,))
    grads = {n: grads[n].reshape(w[n].shape) for n in small_names}
    dw_ada, db_ada = _ada_bwd(c_all, lax.dynamic_slice(dmod_all, (0, 0, chip * n_ada), (2, 8, n_ada)), dmod_all)
    grads["ada_w"], grads["ada_b"] = dw_ada, db_ada.reshape(2, 6 * D)

    delta, new_m, new_v = {}, {}, {}

    def update(n):
        cols = w[n].shape[-1]
        d_, m_, v_ = _adamw("adamw_" + n, w[n].reshape(-1, cols), grads[n].reshape(-1, cols), m[n].reshape(-1, cols),
                            v[n].reshape(-1, cols))
        delta[n], new_m[n], new_v[n] = d_.reshape(w[n].shape), m_.reshape(w[n].shape), v_.reshape(w[n].shape)

    update("ada_w")
    shapes = [w[n].shape for n in SMALL]
    packed = [_pack_small([t[n] for n in SMALL]) for t in (w, grads, m, v)]
    outs = _adamw("adamw_small", *packed)
    for dst, slab in zip((delta, new_m, new_v), outs):
        for n, t in zip(SMALL, _unpack_small(slab, shapes)):
            dst[n] = t
    shares.update(finish_reduce("m", reduces["m"], MIXER0_SLAB, outs[0]))
    grads.update(_pieces_to_shares(shares))
    for n in BIG_ARGS:
        update(n)
    return (loss, grad_x[None], *[grads[n] for n in WEIGHTS], *[delta[n] for n in WEIGHTS],
            *[new_m[n] for n in WEIGHTS], *[new_v[n] for n in WEIGHTS])
```

```python
import functools
import math

import numpy as np
import jax
import jax.numpy as jnp
from jax import lax
from jax.experimental import pallas as pl
from jax.experimental.pallas import tpu as pltpu

F32 = jnp.float32
BF16 = jnp.bfloat16
_MXU = jnp.bfloat16
_VMEM_LIMIT = 56 * 1024 * 1024
MXU_COLS = 256
D = 1024
L = 128
NSTATE = 128
EPS = 1e-6
NEG_INF = -1e30
FFN = 2816
ADAM_LR, ADAM_B1, ADAM_B2, ADAM_EPS, ADAM_WD, ADAM_STEP = 0.001, 0.9, 0.999, 1e-08, 0.01, 10
MESH = pl.DeviceIdType.MESH
ANY = pl.BlockSpec(memory_space=pl.ANY)

NN = (((1,), (0,)), ((), ()))
NT = (((1,), (1,)), ((), ()))
TN = (((0,), (0,)), ((), ()))


def _dot(a, b, dn=NN):
    return lax.dot_general(a.astype(_MXU), b.astype(_MXU), dn, preferred_element_type=F32)


def _params(sem=None):
    return pltpu.CompilerParams(dimension_semantics=sem, vmem_limit_bytes=_VMEM_LIMIT)


def _sigmoid(x):
    return 1.0 / (1.0 + jnp.exp(-x))


def _softplus(x):
    return jnp.maximum(x, 0.0) + jnp.log(1.0 + jnp.exp(-jnp.abs(x)))


def _gelu(x):
    return 0.5 * x * (1.0 + lax.erf(x * (2.0 ** -0.5)))


def _gelu_grad(x):
    return 0.5 * (1.0 + lax.erf(x * (2.0 ** -0.5))) + x * jnp.exp(-0.5 * x * x) * (1.0 / math.sqrt(2.0 * math.pi))


def _silu_grad(a):
    sg = _sigmoid(a)
    return sg * (1.0 + a * (1.0 - sg))


def _rowwise(name, fn, rows, vecs, out_rows, out_accs=(), tr=512, after=None):
    S = rows[0].shape[0]
    tr = min(tr, S)
    assert S % tr == 0
    nr, nv, no, na = len(rows), len(vecs), len(out_rows), len(out_accs)
    deps = [] if after is None else [after]

    def body(*refs):
        ins, outs = refs[:nr + nv], refs[nr + nv + len(deps):]
        res = fn(*[r[...] for r in ins])
        if not isinstance(res, (tuple, list)):
            res = (res,)
        for k in range(no):
            outs[k][...] = res[k].astype(outs[k].dtype)
        if na:
            @pl.when(pl.program_id(0) == 0)
            def _():
                for k in range(na):
                    outs[no + k][...] = jnp.zeros_like(outs[no + k])
            for k in range(na):
                outs[no + k][...] += res[no + k]

    in_specs = [pl.BlockSpec((tr, a.shape[1]), lambda i: (i, 0)) for a in rows]
    in_specs += [pl.BlockSpec(v.shape, lambda i: (0, 0)) for v in vecs] + [ANY for _ in deps]
    out_specs = [pl.BlockSpec((tr, c), lambda i: (i, 0)) for c, _ in out_rows]
    out_specs += [pl.BlockSpec(s, lambda i: (0, 0)) for s in out_accs]
    out_shape = [jax.ShapeDtypeStruct((S, c), dt) for c, dt in out_rows]
    out_shape += [jax.ShapeDtypeStruct(s, F32) for s in out_accs]
    return pl.pallas_call(body, name=name, grid=(S // tr,), in_specs=in_specs, out_specs=out_specs,
                          out_shape=out_shape, compiler_params=_params(("arbitrary",)))(*rows, *vecs, *deps)


def _col_tile(n, cap):
    if n <= cap or n % 128:
        return n
    best = 128
    for t in range(128, cap + 1, 128):
        if n % t == 0:
            best = t
    return best


def _mm(name, As, Bs, mode, outs, epi=None, groups=None, extras=(), vecs=(), tm=512, tn_cap=1536, whole_rows=False):
    M = As[0].shape[0]
    N = Bs[0].shape[1] if mode == "nn" else Bs[0].shape[0]
    tm = min(tm, M)
    tn = _col_tile(N, tn_cap)
    assert M % tm == 0 and N % tn == 0
    npair = len(As)
    groups = groups or [0] * npair
    ng = max(groups) + 1
    nx, nv = len(extras), len(vecs)
    dn = NN if mode == "nn" else NT

    def body(*refs):
        a_refs, b_refs = refs[:npair], refs[npair:2 * npair]
        x_refs = refs[2 * npair:2 * npair + nx]
        v_refs = refs[2 * npair + nx:2 * npair + nx + nv]
        o_refs = refs[2 * npair + nx + nv:]
        step = tn if (epi is None or whole_rows) else min(tn, MXU_COLS)
        for col in range(0, tn, step):
            sl = slice(col, min(col + step, tn))
            accs = [None] * ng
            for k in range(npair):
                b = b_refs[k][:, sl] if mode == "nn" else b_refs[k][sl, :]
                d = _dot(a_refs[k][...], b, dn)
                accs[groups[k]] = d if accs[groups[k]] is None else accs[groups[k]] + d
            args = accs + [x[:, sl] for x in x_refs] + [v[:, sl] for v in v_refs]
            res = epi(*args) if epi is not None else tuple(accs)
            if not isinstance(res, (tuple, list)):
                res = (res,)
            for o, r in zip(o_refs, res):
                o[:, sl] = r.astype(o.dtype)

    in_specs = [pl.BlockSpec((tm, a.shape[1]), lambda i, j: (i, 0)) for a in As]
    if mode == "nn":
        in_specs += [pl.BlockSpec((b.shape[0], tn), lambda i, j: (0, j)) for b in Bs]
    else:
        in_specs += [pl.BlockSpec((tn, b.shape[1]), lambda i, j: (j, 0)) for b in Bs]
    in_specs += [pl.BlockSpec((tm, tn), lambda i, j: (i, j)) for _ in extras]
    in_specs += [pl.BlockSpec((1, tn), lambda i, j: (0, j)) for _ in vecs]
    out_specs = [pl.BlockSpec((tm, tn), lambda i, j: (i, j)) for _ in outs]
    out_shape = [jax.ShapeDtypeStruct((M, N), dt) for dt in outs]
    return pl.pallas_call(body, name=name, grid=(M // tm, N // tn), in_specs=in_specs, out_specs=out_specs,
                          out_shape=out_shape, compiler_params=_params(("parallel", "parallel")))(
                              *As, *Bs, *extras, *vecs)


def _mm_shared_lhs(name, A, Bs, tm=512):
    M, K = A.shape
    tm = min(tm, M)
    assert M % tm == 0
    n = len(Bs)

    def body(a_ref, *refs):
        a = a_ref[...]
        for b_ref, o_ref in zip(refs[:n], refs[n:]):
            o_ref[...] = _dot(a, b_ref[...], NT)

    return pl.pallas_call(
        body, name=name, grid=(M // tm,),
        in_specs=[pl.BlockSpec((tm, K), lambda i: (i, 0))] + [pl.BlockSpec(b.shape, lambda i: (0, 0)) for b in Bs],
        out_specs=[pl.BlockSpec((tm, b.shape[0]), lambda i: (i, 0)) for b in Bs],
        out_shape=[jax.ShapeDtypeStruct((M, b.shape[0]), F32) for b in Bs],
        compiler_params=_params(("parallel",)))(A, *Bs)


def _mm_tn_shared_rhs(name, As, B, tk=256):
    S, N = B.shape
    tk = min(tk, S)
    assert S % tk == 0
    n, nk = len(As), S // tk

    def body(*refs):
        a_refs, b_ref, o_refs, acc_refs = refs[:n], refs[n], refs[n + 1:2 * n + 1], refs[2 * n + 1:]
        k = pl.program_id(0)

        @pl.when(k == 0)
        def _():
            for acc_ref in acc_refs:
                acc_ref[...] = jnp.zeros_like(acc_ref)
        b = b_ref[...]
        for a_ref, acc_ref in zip(a_refs, acc_refs):
            acc_ref[...] += _dot(a_ref[...], b, TN)

        @pl.when(k == nk - 1)
        def _():
            for o_ref, acc_ref in zip(o_refs, acc_refs):
                o_ref[...] = acc_ref[...].astype(o_ref.dtype)

    return pl.pallas_call(
        body, name=name, grid=(nk,),
        in_specs=[pl.BlockSpec((tk, a.shape[1]), lambda k: (k, 0)) for a in As] + [pl.BlockSpec((tk, N), lambda k: (k, 0))],
        out_specs=[pl.BlockSpec((a.shape[1], N), lambda k: (0, 0)) for a in As],
        out_shape=[jax.ShapeDtypeStruct((a.shape[1], N), BF16) for a in As],
        scratch_shapes=[pltpu.VMEM((a.shape[1], N), F32) for a in As],
        compiler_params=_params(("arbitrary",)))(*As, B)


def _mm_tn(name, A, B, tk=1024, t2_cap=1536):
    S, K1 = A.shape
    N2 = B.shape[1]
    tk = min(tk, S)
    t2 = _col_tile(N2, t2_cap)
    assert S % tk == 0 and N2 % t2 == 0
    nk = S // tk

    def body(a_ref, b_ref, o_ref, acc_ref):
        k = pl.program_id(1)

        @pl.when(k == 0)
        def _():
            acc_ref[...] = jnp.zeros_like(acc_ref)
        acc_ref[...] += _dot(a_ref[...], b_ref[...], TN)

        @pl.when(k == nk - 1)
        def _():
            o_ref[...] = acc_ref[...].astype(o_ref.dtype)

    return pl.pallas_call(
        body, name=name, grid=(N2 // t2, nk),
        in_specs=[pl.BlockSpec((tk, K1), lambda j, k: (k, 0)), pl.BlockSpec((tk, t2), lambda j, k: (k, j))],
        out_specs=pl.BlockSpec((K1, t2), lambda j, k: (0, j)),
        out_shape=jax.ShapeDtypeStruct((K1, N2), BF16), scratch_shapes=[pltpu.VMEM((K1, t2), F32)],
        compiler_params=_params(("parallel", "arbitrary")))(A, B)


def _norm_mod(x, nw, sc, sh):
    rstd = lax.rsqrt(jnp.mean(x * x, axis=-1, keepdims=True) + EPS)
    return (x * rstd * nw) * (1.0 + sc) + sh


def _norm_mod_fwd(name, x, nw, sc, sh, after=None):
    return _rowwise(name, _norm_mod, [x], [nw, sc, sh], [(D, BF16)], after=after)[0]


def _norm_mod_bwd(name, x, dh, dres, nw, sc, gate=None, after=None):
    def fn(x, dh, dres, *rest):
        nw, sc = rest[-3:-1] if gate else rest
        rstd = lax.rsqrt(jnp.mean(x * x, axis=-1, keepdims=True) + EPS)
        xh = x * rstd
        dn = dh * (1.0 + sc)
        dxh = dn * nw
        dx = dres + rstd * (dxh - xh * jnp.mean(dxh * xh, axis=-1, keepdims=True))
        sums = [jnp.sum(dh, axis=0, keepdims=True), jnp.sum(dh * (xh * nw), axis=0, keepdims=True),
                jnp.sum(dn * xh, axis=0, keepdims=True)]
        if not gate:
            return (dx, *sums)
        dy = dx * rest[-1]
        return (dx, dy, *sums, jnp.sum(dx * rest[0], axis=0, keepdims=True), jnp.sum(dy, axis=0, keepdims=True))
    if not gate:
        return _rowwise(name, fn, [x, dh, dres], [nw, sc], [(D, F32)], [(1, D)] * 3, after=after)
    return _rowwise(name, fn, [x, dh, dres, gate[0]], [nw, sc, gate[1]], [(D, F32), (D, BF16)], [(1, D)] * 5,
                    tr=1024, after=after)


def _loss_head(x, tgt, fw, y, g):
    def fn(x, tgt, y, fw, g):
        rstd = lax.rsqrt(jnp.mean(x * x, axis=-1, keepdims=True) + EPS)
        xh = x * rstd
        err = xh * fw - tgt
        dout = err * (1.0 / D)
        dxh = dout * fw
        dx = rstd * (dxh - xh * jnp.mean(dxh * xh, axis=-1, keepdims=True))
        sq = jnp.sum(jnp.sum(err * err, axis=1, keepdims=True), axis=0, keepdims=True)
        return (dx, dx * g, sq, jnp.sum(dout * xh, axis=0, keepdims=True), jnp.sum(dx * y, axis=0, keepdims=True))
    return _rowwise("loss_head", fn, [x, tgt, y], [fw, g], [(D, F32), (D, BF16)], [(1, 1), (1, D), (1, D)], tr=1024)


def _ffn_fwd(tag, h, wg, wu, wd, x, g2, next_norm=None):
    def act(a, b):
        return a, b, a * _sigmoid(a) * b
    a, b, f = _mm(f"ffn_up_{tag}", [h, h], [wg, wu], "nt", [BF16, BF16, BF16], epi=act, groups=[0, 1], tn_cap=FFN)

    if next_norm is None:
        def res(y, x, g):
            return y, x + g * y
        y, xo = _mm(f"ffn_down_{tag}", [f], [wd], "nn", [F32, F32], epi=res, extras=[x], vecs=[g2])
        return a, b, f, y, xo, None

    def res_norm(y, x, g, nw, sc, sh):
        xo = x + g * y
        return y, xo, _norm_mod(xo, nw, sc, sh)
    assert wd.shape[1] == D
    y, xo, h_next = _mm(f"ffn_down_{tag}", [f], [wd], "nn", [F32, F32, BF16], epi=res_norm, extras=[x],
                        vecs=[g2, *next_norm], whole_rows=True)
    return a, b, f, y, xo, h_next


def _ffn_bwd(tag, dy, h, a, b, f, wg, wu, wd):
    def act_bwd(df, a, b):
        a, b = a.astype(F32), b.astype(F32)
        sg = _sigmoid(a)
        return df * b * (sg * (1.0 + a * (1.0 - sg))), df * (a * sg)
    da, db = _mm(f"ffn_dact_{tag}", [dy], [wd], "nt", [BF16, BF16], epi=act_bwd, extras=[a, b], tn_cap=FFN)
    dwd = _mm_tn(f"ffn_dwd_{tag}", f, dy)
    dwg = _mm_tn(f"ffn_dwg_{tag}", da, h)
    dwu = _mm_tn(f"ffn_dwu_{tag}", db, h)
    dh = _mm(f"ffn_dh_{tag}", [da, db], [wg, wu], "nn", [F32])[0]
    return dh, dwg, dwu, dwd


def _conv_fwd(xr, w, b, tb=512):
    S, C = xr.shape
    tb = min(tb, S)

    def body(x_ref, halo_ref, w_ref, b_ref, pre_ref, out_ref):
        i = pl.program_id(0)
        halo = jnp.where(i > 0, halo_ref[...], 0.0)
        xe = jnp.concatenate([halo, x_ref[...]], axis=0)
        pre = w_ref[3:4, :] * x_ref[...] + b_ref[...]
        for j in (1, 2, 3):
            pre = pre + w_ref[3 - j:4 - j, :] * pltpu.roll(xe, j, axis=0)[8:, :]
        pre_ref[...] = pre
        out_ref[...] = pre * _sigmoid(pre)

    return pl.pallas_call(
        body, name="conv_fwd", grid=(S // tb,),
        in_specs=[pl.BlockSpec((tb, C), lambda i: (i, 0)),
                  pl.BlockSpec((8, C), lambda i: (jnp.maximum(i * (tb // 8) - 1, 0), 0)),
                  pl.BlockSpec((4, C), lambda i: (0, 0)), pl.BlockSpec((1, C), lambda i: (0, 0))],
        out_specs=[pl.BlockSpec((tb, C), lambda i: (i, 0))] * 2,
        out_shape=[jax.ShapeDtypeStruct((S, C), F32)] * 2,
        compiler_params=_params(("parallel",)))(xr, xr, w, b)


def _conv_bwd(dxc, pre, xr, w, tb=512):
    S, C = xr.shape
    tb = min(tb, S)
    nblk = S // tb

    def body(d_ref, p_ref, dn_ref, pn_ref, x_ref, w_ref, dx_ref, dw_ref, db_ref):
        i = pl.program_id(0)

        @pl.when(i == 0)
        def _():
            dw_ref[...] = jnp.zeros_like(dw_ref)
            db_ref[...] = jnp.zeros_like(db_ref)

        dpre = d_ref[...] * _silu_grad(p_ref[...])
        dnext = jnp.where(i < nblk - 1, dn_ref[...] * _silu_grad(pn_ref[...]), 0.0)
        pe = jnp.concatenate([dpre, dnext], axis=0)
        xx = x_ref[...]
        dx = w_ref[3:4, :] * dpre
        dw_ref[3:4, :] += jnp.sum(dpre * xx, axis=0, keepdims=True)
        for j in (1, 2, 3):
            ahead = pltpu.roll(pe, tb + 8 - j, axis=0)[:tb, :]
            dx = dx + w_ref[3 - j:4 - j, :] * ahead
            dw_ref[3 - j:4 - j, :] += jnp.sum(ahead * xx, axis=0, keepdims=True)
        dx_ref[...] = dx.astype(dx_ref.dtype)
        db_ref[...] += jnp.sum(dpre, axis=0, keepdims=True)

    blk = pl.BlockSpec((tb, C), lambda i: (i, 0))
    nxt = pl.BlockSpec((8, C), lambda i: (jnp.minimum((i + 1) * (tb // 8), S // 8 - 1), 0))
    return pl.pallas_call(
        body, name="conv_bwd", grid=(nblk,),
        in_specs=[blk, blk, nxt, nxt, blk, pl.BlockSpec((4, C), lambda i: (0, 0))],
        out_specs=[blk, pl.BlockSpec((4, C), lambda i: (0, 0)), pl.BlockSpec((1, C), lambda i: (0, 0))],
        out_shape=[jax.ShapeDtypeStruct((S, C), BF16), jax.ShapeDtypeStruct((4, C), F32),
                   jax.ShapeDtypeStruct((1, C), F32)],
        compiler_params=_params(("arbitrary",)))(dxc, pre, dxc, pre, xr, w)


def _iota(shape, dim):
    return lax.broadcasted_iota(jnp.int32, shape, dim)


def _colsel(m, lane, h):
    return jnp.sum(jnp.where(lane == h, m, 0.0), axis=1, keepdims=True)


def _cumsum_rows(v):
    r = _iota(v.shape, 0)
    k = 1
    while k < v.shape[0]:
        v = v + jnp.where(r >= k, pltpu.roll(v, k, axis=0), 0.0)
        k *= 2
    return v


def _suffix_sum_rows(v):
    n = v.shape[0]
    r = _iota(v.shape, 0)
    k = 1
    while k < n:
        v = v + jnp.where(r < n - k, pltpu.roll(v, n - k, axis=0), 0.0)
        k *= 2
    return v


def _ssd_fwd(xc, dtr, z, dtb, alog, dskl, nw):
    S = xc.shape[0]
    nc = S // L

    def body(xc_ref, dtr_ref, z_ref, dtb_ref, alog_ref, dsk_ref, nw_ref, ya_ref, y_ref, prev_ref,
             st_ref, cum_ref, cumT_ref):
        i = pl.program_id(0)

        @pl.when(i == 0)
        def _():
            st_ref[...] = jnp.zeros_like(st_ref)

        lane = _iota((L, 128), 1)
        lane1 = _iota((1, 128), 1)
        lo = lane < 64
        lo1 = lane1 < 64
        tril = _iota((L, L), 0) >= _iota((L, L), 1)
        dt = _softplus(dtr_ref[...] + dtb_ref[...])
        a_neg = -jnp.exp(alog_ref[...])
        cum = _cumsum_rows(dt * a_neg)
        cum_ref[...] = cum
        cumT_ref[...] = cum.T
        last_all = cum_ref[L - 1:L, :]
        prev_t = st_ref[...]
        prev_ref[0] = prev_t
        for g in range(2):
            bg = xc_ref[:, 1024 + g * 128:1152 + g * 128]
            cg = xc_ref[:, 1280 + g * 128:1408 + g * 128]
            gmat = _dot(cg, bg, NT)
            yoff = _dot(cg, prev_t[:, g * 512:(g + 1) * 512])
            bg_t = bg.T
            for jp in range(4):
                j = g * 4 + jp
                sl = slice(j * 128, (j + 1) * 128)
                xp = xc_ref[:, sl]
                cc = [_colsel(cum, lane, 2 * j), _colsel(cum, lane, 2 * j + 1)]
                cum_l = jnp.where(lo, cc[0], cc[1])
                dt_l = jnp.where(lo, _colsel(dt, lane, 2 * j), _colsel(dt, lane, 2 * j + 1))
                last_l = jnp.where(lo1, _colsel(last_all, lane1, 2 * j), _colsel(last_all, lane1, 2 * j + 1))
                xd = xp * dt_l
                ys = []
                for hh in range(2):
                    seg = cc[hh] - cumT_ref[2 * j + hh:2 * j + hh + 1, :]
                    dm = jnp.where(tril, jnp.exp(seg), 0.0)
                    ys.append(_dot(gmat * dm, xd))
                y_ref[:, sl] = (jnp.where(lo, ys[0], ys[1]) + jnp.exp(cum_l) * yoff[:, jp * 128:(jp + 1) * 128]
                                + dsk_ref[:, sl] * xp)
                st_ref[:, sl] = prev_t[:, sl] * jnp.exp(last_l) + _dot(bg_t, xd * jnp.exp(last_l - cum_l))
        for g in range(2):
            sl = slice(g * 512, (g + 1) * 512)
            zz = z_ref[:, sl]
            yg = y_ref[:, sl] * (zz * _sigmoid(zz))
            rstd = lax.rsqrt(jnp.mean(yg * yg, axis=-1, keepdims=True) + EPS)
            ya_ref[:, sl] = (yg * rstd * nw_ref[:, sl]).astype(ya_ref.dtype)

    blk = lambda c: pl.BlockSpec((L, c), lambda i: (i, 0))
    vec = lambda c: pl.BlockSpec((1, c), lambda i: (0, 0))
    return pl.pallas_call(
        body, name="ssd_fwd", grid=(nc,),
        in_specs=[blk(1536), blk(128), blk(1024), vec(128), vec(128), vec(1024), vec(1024)],
        out_specs=[blk(1024), blk(1024), pl.BlockSpec((1, NSTATE, 1024), lambda i: (i, 0, 0))],
        out_shape=[jax.ShapeDtypeStruct((S, 1024), BF16), jax.ShapeDtypeStruct((S, 1024), F32),
                   jax.ShapeDtypeStruct((nc, NSTATE, 1024), F32)],
        scratch_shapes=[pltpu.VMEM((NSTATE, 1024), F32), pltpu.VMEM((L, 128), F32), pltpu.VMEM((L, 128), F32)],
        compiler_params=_params(("arbitrary",)))(xc, dtr, z, dtb, alog, dskl, nw)


def _ssd_bwd(dya, y, z, xc, dtr, prev, dtb, alog, dskl, nw):
    S = xc.shape[0]
    nc = S // L

    def body(dya_ref, y_ref, z_ref, xc_ref, dtr_ref, prev_ref, dtb_ref, alog_ref, dsk_ref, nw_ref,
             dz_ref, dxc_ref, ddtr_ref, dnw_ref, ddsk_ref, dalog_ref, ddtb_ref,
             dst_ref, cum_ref, cumT_ref, dy_ref, dskacc_ref):
        i = pl.program_id(0)

        @pl.when(i == 0)
        def _():
            dst_ref[...] = jnp.zeros_like(dst_ref)
            dskacc_ref[...] = jnp.zeros_like(dskacc_ref)
            dnw_ref[...] = jnp.zeros_like(dnw_ref)
            dalog_ref[...] = jnp.zeros_like(dalog_ref)
            ddtb_ref[...] = jnp.zeros_like(ddtb_ref)

        lane = _iota((L, 128), 1)
        lane1 = _iota((1, 128), 1)
        lo = lane < 64
        lo1 = lane1 < 64
        r2, c2 = _iota((L, L), 0), _iota((L, L), 1)
        tril = r2 >= c2
        triu = r2 <= c2
        is_last = _iota((L, 1), 0) == L - 1

        for g in range(2):
            sl = slice(g * 512, (g + 1) * 512)
            zz = z_ref[:, sl]
            sg = _sigmoid(zz)
            zg = zz * sg
            yv = y_ref[:, sl]
            yg = yv * zg
            rstd = lax.rsqrt(jnp.mean(yg * yg, axis=-1, keepdims=True) + EPS)
            xh = yg * rstd
            d_out = dya_ref[:, sl]
            dnw_ref[:, sl] += jnp.sum(d_out * xh, axis=0, keepdims=True)
            dyn = d_out * nw_ref[:, sl]
            dyg = rstd * (dyn - xh * jnp.mean(dyn * xh, axis=-1, keepdims=True))
            dy_ref[:, sl] = dyg * zg
            dz_ref[:, sl] = (dyg * yv * (sg * (1.0 + zz * (1.0 - sg)))).astype(dz_ref.dtype)

        dtin = dtr_ref[...] + dtb_ref[...]
        dt = _softplus(dtin)
        a_neg = -jnp.exp(alog_ref[...])
        cum = _cumsum_rows(dt * a_neg)
        cum_ref[...] = cum
        cumT_ref[...] = cum.T
        last_all = cum_ref[L - 1:L, :]
        prev_t = prev_ref[0]
        dn_t = dst_ref[...]
        dcum = jnp.zeros((L, 128), F32)
        ddt = jnp.zeros((L, 128), F32)
        for g in range(2):
            gsl = slice(g * 512, (g + 1) * 512)
            bg = xc_ref[:, 1024 + g * 128:1152 + g * 128]
            cg = xc_ref[:, 1280 + g * 128:1408 + g * 128]
            gmat = _dot(cg, bg, NT)
            gmat_t = _dot(bg, cg, NT)
            pg = prev_t[:, gsl]
            zmat = _dot(cg, pg)
            dgm = jnp.zeros((L, L), F32)
            dgm_t = jnp.zeros((L, L), F32)
            db_acc = jnp.zeros((L, NSTATE), F32)
            dz_parts, cd_parts = [], []
            for jp in range(4):
                j = g * 4 + jp
                sl = slice(j * 128, (j + 1) * 128)
                xp = xc_ref[:, sl]
                dyp = dy_ref[:, sl]
                cc = [_colsel(cum, lane, 2 * j), _colsel(cum, lane, 2 * j + 1)]
                lc = [_colsel(last_all, lane1, 2 * j), _colsel(last_all, lane1, 2 * j + 1)]
                cum_l = jnp.where(lo, cc[0], cc[1])
                dt_l = jnp.where(lo, _colsel(dt, lane, 2 * j), _colsel(dt, lane, 2 * j + 1))
                last_l = jnp.where(lo1, lc[0], lc[1])
                e_l = jnp.exp(cum_l)
                dte_l = jnp.exp(last_l - cum_l)
                cd_l = jnp.exp(last_l)
                cd_parts.append(cd_l)
                xd = xp * dt_l
                dskacc_ref[:, sl] += jnp.sum(dyp * xp, axis=0, keepdims=True)
                dxp = dsk_ref[:, sl] * dyp
                t = dyp * (e_l * zmat[:, jp * 128:(jp + 1) * 128])
                dcc = [jnp.sum(jnp.where(lo, t, 0.0), axis=1, keepdims=True),
                       jnp.sum(jnp.where(lo, 0.0, t), axis=1, keepdims=True)]
                dz_parts.append(e_l * dyp)
                dnp_ = dn_t[:, sl]
                t2 = jnp.sum(dnp_ * prev_t[:, sl], axis=0, keepdims=True)
                dcd = [jnp.sum(jnp.where(lo1, t2, 0.0), axis=1, keepdims=True),
                       jnp.sum(jnp.where(lo1, 0.0, t2), axis=1, keepdims=True)]
                wm = _dot(bg, dnp_)
                dxd = wm * dte_l
                t3 = wm * xd
                ddte = [jnp.sum(jnp.where(lo, t3, 0.0), axis=1, keepdims=True),
                        jnp.sum(jnp.where(lo, 0.0, t3), axis=1, keepdims=True)]
                db_acc = db_acc + _dot(xd * dte_l, dnp_, NT)
                for hh in range(2):
                    h = 2 * j + hh
                    half = lo if hh == 0 else jnp.logical_not(lo)
                    row = cumT_ref[h:h + 1, :]
                    dm = jnp.where(tril, jnp.exp(cc[hh] - row), 0.0)
                    dm_t = jnp.where(triu, jnp.exp(row - cc[hh]), 0.0)
                    dym = jnp.where(half, dyp, 0.0)
                    u = _dot(dym, xd, NT) * dm
                    u_t = _dot(xd, dym, NT) * dm_t
                    dxd = dxd + _dot(gmat_t * dm_t, dym)
                    dcc[hh] = dcc[hh] + jnp.sum(u * gmat, axis=1, keepdims=True) - jnp.sum(u_t * gmat_t, axis=1, keepdims=True)
                    dgm = dgm + u
                    dgm_t = dgm_t + u_t
                    dte_c = jnp.exp(lc[hh] - cc[hh])
                    dcc[hh] = dcc[hh] - ddte[hh] * dte_c
                    endc = dcd[hh] * jnp.exp(lc[hh]) + jnp.sum(ddte[hh] * dte_c, axis=0, keepdims=True)
                    dcc[hh] = dcc[hh] + jnp.where(is_last, endc, 0.0)
                    dcum = jnp.where(lane == h, dcc[hh], dcum)
                dxc_ref[:, sl] = dxp + dxd * dt_l
                t4 = dxd * xp
                ddt = jnp.where(lane == 2 * j, jnp.sum(jnp.where(lo, t4, 0.0), axis=1, keepdims=True), ddt)
                ddt = jnp.where(lane == 2 * j + 1, jnp.sum(jnp.where(lo, 0.0, t4), axis=1, keepdims=True), ddt)
            dzg = jnp.concatenate(dz_parts, axis=1)
            dst_ref[:, gsl] = dn_t[:, gsl] * jnp.concatenate(cd_parts, axis=1) + _dot(cg.T, dzg)
            dxc_ref[:, 1280 + g * 128:1408 + g * 128] = _dot(dgm, bg) + _dot(dzg, pg, NT)
            dxc_ref[:, 1024 + g * 128:1152 + g * 128] = _dot(dgm_t, cg) + db_acc
        dla = _suffix_sum_rows(dcum)
        ddt = ddt + dla * a_neg
        dalog_ref[...] += jnp.sum(dla * dt, axis=0, keepdims=True) * a_neg
        ddtr = jnp.where(lane < 16, ddt * _sigmoid(dtin), 0.0)
        ddtr_ref[...] = ddtr.astype(ddtr_ref.dtype)
        ddtb_ref[...] += jnp.sum(ddtr, axis=0, keepdims=True)

        @pl.when(i == nc - 1)
        def _():
            seg = (_iota((1024, 128), 0) // 64 == _iota((1024, 128), 1)).astype(F32)
            acc8 = jnp.broadcast_to(dskacc_ref[...], (8, 1024))
            ddsk_ref[...] = lax.dot_general(acc8, seg, NN, precision=lax.Precision.HIGHEST,
                                            preferred_element_type=F32)

    rev = lambda c: pl.BlockSpec((L, c), lambda i: (nc - 1 - i, 0))
    vec = lambda c: pl.BlockSpec((1, c), lambda i: (0, 0))
    return pl.pallas_call(
        body, name="ssd_bwd", grid=(nc,),
        in_specs=[rev(1024), rev(1024), rev(1024), rev(1536), rev(128),
                  pl.BlockSpec((1, NSTATE, 1024), lambda i: (nc - 1 - i, 0, 0)),
                  vec(128), vec(128), vec(1024), vec(1024)],
        out_specs=[rev(1024), rev(1536), rev(128), vec(1024), pl.BlockSpec((8, 128), lambda i: (0, 0)),
                   vec(128), vec(128)],
        out_shape=[jax.ShapeDtypeStruct((S, 1024), BF16), jax.ShapeDtypeStruct((S, 1536), F32),
                   jax.ShapeDtypeStruct((S, 128), BF16), jax.ShapeDtypeStruct((1, 1024), F32),
                   jax.ShapeDtypeStruct((8, 128), F32), jax.ShapeDtypeStruct((1, 128), F32),
                   jax.ShapeDtypeStruct((1, 128), F32)],
        scratch_shapes=[pltpu.VMEM((NSTATE, 1024), F32), pltpu.VMEM((L, 128), F32), pltpu.VMEM((L, 128), F32),
                        pltpu.VMEM((L, 1024), F32), pltpu.VMEM((1, 1024), F32)],
        compiler_params=_params(("arbitrary",)))(dya, y, z, xc, dtr, prev, dtb, alog, dskl, nw)


def _layer_norm_parts(vg):
    mu = jnp.mean(vg, axis=-1, keepdims=True)
    vc = vg - mu
    rstd = lax.rsqrt(jnp.mean(vc * vc, axis=-1, keepdims=True) + EPS)
    return vc * rstd, rstd


def _gmlp_fwd(u, v, lnw, lnb, ws, bse, tb=512):
    S = u.shape[0]
    tb = min(tb, S)

    def body(u_ref, v_ref, lnw_ref, lnb_ref, ws_ref, bse_ref, o_ref, vn_ref):
        tril = _iota((L, L), 0) >= _iota((L, L), 1)
        xh, _ = _layer_norm_parts(_gelu(v_ref[...]))
        vn_ref[...] = xh * lnw_ref[...] + lnb_ref[...]
        for g in range(8):
            w = jnp.where(tril, ws_ref[g], 0.0)
            gs = slice(g * 128, (g + 1) * 128)
            for ch in range(tb // L):
                rs = slice(ch * L, (ch + 1) * L)
                sv = _dot(w, vn_ref[rs, gs]) + bse_ref[g]
                o_ref[rs, gs] = (_gelu(u_ref[rs, gs]) * sv).astype(o_ref.dtype)

    blk = pl.BlockSpec((tb, 1024), lambda i: (i, 0))
    vec = pl.BlockSpec((1, 1024), lambda i: (0, 0))
    cube = pl.BlockSpec((8, L, 128), lambda i: (0, 0, 0))
    return pl.pallas_call(
        body, name="gmlp_fwd", grid=(S // tb,), in_specs=[blk, blk, vec, vec, cube, cube], out_specs=blk,
        out_shape=jax.ShapeDtypeStruct((S, 1024), BF16), scratch_shapes=[pltpu.VMEM((tb, 1024), F32)],
        compiler_params=_params(("parallel",)))(u, v, lnw, lnb, ws, bse)


def _gmlp_bwd(dyb, u, v, lnw, lnb, ws, bse, tb=512):
    S = u.shape[0]
    tb = min(tb, S)

    def body(d_ref, u_ref, v_ref, lnw_ref, lnb_ref, ws_ref, bse_ref,
             du_ref, dv_ref, dws_ref, dbse_ref, dlnw_ref, dlnb_ref, vn_ref, dvn_ref):
        @pl.when(pl.program_id(0) == 0)
        def _():
            dws_ref[...] = jnp.zeros_like(dws_ref)
            dbse_ref[...] = jnp.zeros_like(dbse_ref)
            dlnw_ref[...] = jnp.zeros_like(dlnw_ref)
            dlnb_ref[...] = jnp.zeros_like(dlnb_ref)

        tril = _iota((L, L), 0) >= _iota((L, L), 1)
        vv = v_ref[...]
        xh, rstd = _layer_norm_parts(_gelu(vv))
        vn_ref[...] = xh * lnw_ref[...] + lnb_ref[...]
        for g in range(8):
            w = jnp.where(tril, ws_ref[g], 0.0)
            w_t = w.T
            gs = slice(g * 128, (g + 1) * 128)
            dw = jnp.zeros((L, L), F32)
            dbs = jnp.zeros((L, 128), F32)
            for ch in range(tb // L):
                rs = slice(ch * L, (ch + 1) * L)
                vn = vn_ref[rs, gs]
                sv = _dot(w, vn) + bse_ref[g]
                uu = u_ref[rs, gs]
                dd = d_ref[rs, gs]
                du_ref[rs, gs] = (dd * sv * _gelu_grad(uu)).astype(du_ref.dtype)
                dsv = dd * _gelu(uu)
                dw = dw + _dot(dsv, vn, NT)
                dbs = dbs + dsv
                dvn_ref[rs, gs] = _dot(w_t, dsv)
            dws_ref[g] += jnp.where(tril, dw, 0.0)
            dbse_ref[g] += dbs
        dvn = dvn_ref[...]
        dlnw_ref[...] += jnp.sum(dvn * xh, axis=0, keepdims=True)
        dlnb_ref[...] += jnp.sum(dvn, axis=0, keepdims=True)
        dxh = dvn * lnw_ref[...]
        dvg = rstd * (dxh - jnp.mean(dxh, axis=-1, keepdims=True) - xh * jnp.mean(dxh * xh, axis=-1, keepdims=True))
        dv_ref[...] = (dvg * _gelu_grad(vv)).astype(dv_ref.dtype)

    blk = pl.BlockSpec((tb, 1024), lambda i: (i, 0))
    vec = pl.BlockSpec((1, 1024), lambda i: (0, 0))
    cube = pl.BlockSpec((8, L, 128), lambda i: (0, 0, 0))
    return pl.pallas_call(
        body, name="gmlp_bwd", grid=(S // tb,), in_specs=[blk, blk, blk, vec, vec, cube, cube],
        out_specs=[blk, blk, cube, cube, vec, vec],
        out_shape=[jax.ShapeDtypeStruct((S, 1024), BF16), jax.ShapeDtypeStruct((S, 1024), BF16),
                   jax.ShapeDtypeStruct((8, L, 128), F32), jax.ShapeDtypeStruct((8, L, 128), F32),
                   jax.ShapeDtypeStruct((1, 1024), F32), jax.ShapeDtypeStruct((1, 1024), F32)],
        scratch_shapes=[pltpu.VMEM((tb, 1024), F32), pltpu.VMEM((tb, 1024), F32)],
        compiler_params=_params(("arbitrary",)))(dyb, u, v, lnw, lnb, ws, bse)


def _lane_sum(name, a):
    def body(a_ref, o_ref):
        o_ref[...] = jnp.sum(a_ref[...], axis=1, keepdims=True)
    return pl.pallas_call(body, name=name, out_shape=jax.ShapeDtypeStruct((a.shape[0], 1), F32))(a)


def _bucket_onehot_t():
    qi = np.arange(L)[:, None]
    sj = np.arange(2 * L)[None, :]
    dist = np.maximum(qi + L - sj, 0)
    log_ratio = (np.log(np.maximum(dist, 1).astype(np.float32) / np.float32(16)) / np.float32(math.log(128 / 16)))
    large = 16 + (log_ratio.astype(np.float32) * np.float32(16)).astype(np.int32)
    bucket = np.where(dist < 16, dist, np.minimum(large, 31)).reshape(-1)
    return (np.arange(32)[:, None] == bucket[None, :]).astype(np.float32)


def _rel_bias(table_t, onehot_t):
    def body(t_ref, oh_ref, o_ref):
        o_ref[...] = lax.dot_general(t_ref[...], oh_ref[...], NN, precision=lax.Precision.HIGHEST,
                                     preferred_element_type=F32)
    return pl.pallas_call(body, name="rel_bias", out_shape=jax.ShapeDtypeStruct((16, L * 2 * L), F32),
                          compiler_params=_params())(table_t, onehot_t)


def _rel_bias_bwd(dbias, onehot_t):
    def body(d_ref, oh_ref, o_ref):
        o_ref[...] = lax.dot_general(d_ref[...], oh_ref[...], NT, precision=lax.Precision.HIGHEST,
                                     preferred_element_type=F32)
    return pl.pallas_call(body, name="rel_bias_bwd", out_shape=jax.ShapeDtypeStruct((16, 32), F32),
                          compiler_params=_params())(dbias, onehot_t)


def _band(kp, kc, lo):
    kk = jnp.concatenate([kp, kc], axis=0)
    kr = pltpu.roll(kk, 64, axis=1)
    return [jnp.where(lo, kk, kr), jnp.where(lo, kr, kk)]


def _attn_rows(ref, j, lo):
    parts = []
    for t in range(8):
        pair = ref[:, (4 * j + t // 2) * 128:(4 * j + t // 2 + 1) * 128]
        parts.append(jnp.where(lo if t % 2 == 0 else jnp.logical_not(lo), pair, 0.0))
    return jnp.concatenate(parts, axis=0)


def _attn_mask(i, rows):
    qi, sj = _iota((rows, 2 * L), 0) & (L - 1), _iota((rows, 2 * L), 1)
    rel = qi + L - sj
    return (rel >= 0) & (rel < L) & ((sj >= L) | (i > 0))


def _per_head_col(vals):
    return jnp.concatenate([jnp.broadcast_to(v, (L, 1)) for v in vals], axis=0)


SMEM = pl.BlockSpec(memory_space=pltpu.SMEM)


def _attn_fwd(qkv, bias, sinks):
    S = qkv.shape[0]
    nb = S // L
    scale = 64 ** -0.5

    def body(sink_ref, q_ref, kc_ref, vc_ref, kp_ref, vp_ref, bias_ref, o_ref, lse_ref):
        i = pl.program_id(0)
        lane = _iota((L, 128), 1)
        lo = lane < 64
        lo2 = _iota((2 * L, 128), 1) < 64
        mask = _attn_mask(i, L)
        kd = _band(kp_ref[...], kc_ref[...], lo2)
        vd = _band(vp_ref[...], vc_ref[...], lo2)
        lse = jnp.zeros((L, 128), F32)
        for pr in range(8):
            sl = slice(pr * 128, (pr + 1) * 128)
            qp = q_ref[:, sl]
            j = pr // 4
            outs = []
            for hh in range(2):
                h = 2 * pr + hh
                qm = jnp.where(lo if hh == 0 else jnp.logical_not(lo), qp, 0.0)
                lg = jnp.where(mask, _dot(qm, kd[j], NT) * scale + bias_ref[h], NEG_INF)
                s = sink_ref[h]
                m = jnp.maximum(jnp.max(lg, axis=1, keepdims=True), s)
                p = jnp.where(mask, jnp.exp(lg - m), 0.0)
                den = jnp.sum(p, axis=1, keepdims=True) + jnp.exp(s - m)
                outs.append(_dot(p * (1.0 / den), vd[j]))
                lse = jnp.where(lane == h, m + jnp.log(den), lse)
            o_ref[:, sl] = jnp.where(lo, outs[0], outs[1]).astype(o_ref.dtype)
        lse_ref[...] = lse

    prev = lambda col: pl.BlockSpec((L, 128), lambda i: (jnp.maximum(i - 1, 0), col))
    cur = lambda col: pl.BlockSpec((L, 128), lambda i: (i, col))
    return pl.pallas_call(
        body, name="attn_fwd", grid=(nb,),
        in_specs=[SMEM, pl.BlockSpec((L, 1024), lambda i: (i, 0)), cur(8), cur(9), prev(8), prev(9),
                  pl.BlockSpec((16, L, 2 * L), lambda i: (0, 0, 0))],
        out_specs=[pl.BlockSpec((L, 1024), lambda i: (i, 0)), pl.BlockSpec((L, 128), lambda i: (i, 0))],
        out_shape=[jax.ShapeDtypeStruct((S, 1024), BF16), jax.ShapeDtypeStruct((S, 128), F32)],
        compiler_params=_params(("parallel",)))(sinks, qkv, qkv, qkv, qkv, qkv, bias)


def _attn_bwd(qkv, d_o, lse, bias, sinks):
    S = qkv.shape[0]
    nb = S // L
    scale = 64 ** -0.5

    def body(sink_ref, q_ref, kc_ref, vc_ref, kp_ref, vp_ref, do_ref, lse_ref, bias_ref,
             dq_ref, dkv_ref, dbias_ref, dsink_ref, dbq_ref, dbkv_ref, carry_ref):
        i = pl.program_id(0)

        @pl.when(i == 0)
        def _():
            dbias_ref[...] = jnp.zeros_like(dbias_ref)
            dsink_ref[...] = jnp.zeros_like(dsink_ref)
            dbq_ref[...] = jnp.zeros_like(dbq_ref)
            dbkv_ref[...] = jnp.zeros_like(dbkv_ref)
            carry_ref[...] = jnp.zeros_like(carry_ref)

        @pl.when(i < nb)
        def _():
            lane = _iota((L, 128), 1)
            lane1 = _iota((1, 128), 1)
            lo = lane < 64
            lo2 = _iota((2 * L, 128), 1) < 64
            mask = _attn_mask(i, 8 * L)
            kd = _band(kp_ref[...], kc_ref[...], lo2)
            vd = _band(vp_ref[...], vc_ref[...], lo2)
            lse_all = lse_ref[...]
            dsink = jnp.zeros((1, 128), F32)
            tot_k, tot_v = [], []
            for j in range(2):
                q_all = _attn_rows(q_ref, j, lo)
                do_all = _attn_rows(do_ref, j, lo)
                lse_col = _per_head_col([_colsel(lse_all, lane, 8 * j + t) for t in range(8)])
                lg = _dot(q_all, kd[j], NT) * scale + bias_ref[8 * j:8 * j + 8].reshape(8 * L, 2 * L)
                p = jnp.where(mask, jnp.exp(jnp.where(mask, lg, NEG_INF) - lse_col), 0.0)
                dp = _dot(do_all, vd[j], NT)
                delta = jnp.sum(p * dp, axis=1, keepdims=True)
                ds = p * (dp - delta)
                dbias_ref[8 * j:8 * j + 8] += ds.reshape(8, L, 2 * L)
                s = _per_head_col([sink_ref[8 * j + t] for t in range(8)])
                sink_part = -jnp.exp(s - lse_col) * delta
                for t in range(8):
                    dsink = dsink + jnp.where(lane1 == 8 * j + t,
                                              jnp.sum(sink_part[t * L:(t + 1) * L], axis=0, keepdims=True), 0.0)
                dss = ds * scale
                dq_all = _dot(dss, kd[j])
                for t in range(0, 8, 2):
                    sl = slice((4 * j + t // 2) * 128, (4 * j + t // 2 + 1) * 128)
                    dq = jnp.where(lo, dq_all[t * L:(t + 1) * L], dq_all[(t + 1) * L:(t + 2) * L])
                    dq_ref[:, sl] = dq.astype(dq_ref.dtype)
                    dbq_ref[:, sl] += jnp.sum(dq, axis=0, keepdims=True)
                acc_k = _dot(dss, q_all, TN)
                acc_v = _dot(p, do_all, TN)
                tot_k.append(acc_k + pltpu.roll(acc_k, 64, axis=1))
                tot_v.append(acc_v + pltpu.roll(acc_v, 64, axis=1))
            dsink_ref[...] += dsink
            dkv = jnp.concatenate([jnp.where(lo2, tot_k[0], tot_k[1]), jnp.where(lo2, tot_v[0], tot_v[1])], axis=1)
            dbkv_ref[...] += jnp.sum(dkv, axis=0, keepdims=True)
            dkv_ref[...] = (carry_ref[...] + dkv[:L, :]).astype(dkv_ref.dtype)
            carry_ref[...] = dkv[L:, :]

        @pl.when(i == nb)
        def _():
            dkv_ref[...] = carry_ref[...].astype(dkv_ref.dtype)

    c = lambda i: jnp.minimum(i, nb - 1)
    prev = lambda col: pl.BlockSpec((L, 128), lambda i: (jnp.maximum(c(i) - 1, 0), col))
    cur = lambda col: pl.BlockSpec((L, 128), lambda i: (c(i), col))
    row = lambda w: pl.BlockSpec((L, w), lambda i: (c(i), 0))
    cube = pl.BlockSpec((16, L, 2 * L), lambda i: (0, 0, 0))
    vec = lambda w: pl.BlockSpec((1, w), lambda i: (0, 0))
    return pl.pallas_call(
        body, name="attn_bwd", grid=(nb + 1,),
        in_specs=[SMEM, row(1024), cur(8), cur(9), prev(8), prev(9), row(1024), row(128), cube],
        out_specs=[row(1024), pl.BlockSpec((L, 256), lambda i: (jnp.maximum(i - 1, 0), 0)), cube,
                   vec(128), vec(1024), vec(256)],
        out_shape=[jax.ShapeDtypeStruct((S, 1024), BF16), jax.ShapeDtypeStruct((S, 256), BF16),
                   jax.ShapeDtypeStruct((16, L, 2 * L), F32), jax.ShapeDtypeStruct((1, 128), F32),
                   jax.ShapeDtypeStruct((1, 1024), F32), jax.ShapeDtypeStruct((1, 256), F32)],
        scratch_shapes=[pltpu.VMEM((L, 256), F32)],
        compiler_params=_params(("arbitrary",)))(sinks, qkv, qkv, qkv, qkv, qkv, d_o, lse, bias)


def _pad_lanes(a, n=128):
    return jnp.pad(a, ((0, 0), (0, n - a.shape[1])))


def _local_step(x, tgt, mod, w_in, P, io):
    md = [[mod[l:l + 1, k * D:(k + 1) * D] for k in range(6)] for l in range(2)]
    G, g = {}, {}

    sh1, sc1, g1, sh2, sc2, g2 = md[0]
    nmw0, nfw0 = P["norm_mix_w"][0:1], P["norm_ffn_w"][0:1]
    h0 = _norm_mod_fwd("norm_mix_0", x, nmw0, sc1, sh1, after=io["start"])
    segs = {"z": w_in[0:1024], "xbc": w_in[1024:2560], "dt": jnp.pad(w_in[2560:2576], ((0, 112), (0, 0))),
            "u": w_in[2576:3600], "v": w_in[3600:4624]}
    proj = dict(zip(segs, _mm_shared_lhs("in_proj", h0, list(segs.values()))))
    conv_w, conv_b = P["conv_w"][0], P["conv_b"]
    pre, xc = _conv_fwd(proj["xbc"], conv_w, conv_b)
    dtb, alog = _pad_lanes(P["dt_bias"]), _pad_lanes(P["a_log"])
    dskl = jnp.repeat(P["d_skip"], 64, axis=1)
    ya, y_ssd, prev = _ssd_fwd(xc, proj["dt"], proj["z"], dtb, alog, dskl, P["ssm_norm_w"])
    ws = P["gmlp_ws"][0]
    bse = jnp.broadcast_to(P["gmlp_bs"][0][:, :, None], (8, L, 128))
    yb = _gmlp_fwd(proj["u"], proj["v"], P["gmlp_ln_w"], P["gmlp_ln_b"], ws, bse)
    W = dict(io["weights0"]((ya, yb)))
    w_oa, w_ob = W["out_w"][:1024], W["out_w"][1024:]

    def res(y, x, gate, nw, sc, sh):
        xo = x + gate * y
        return y, xo, _norm_mod(xo, nw, sc, sh)
    mix0, x1, h0f = _mm("out_proj_0", [ya, yb], [w_oa, w_ob], "nn", [F32, F32, BF16], epi=res, extras=[x],
                        vecs=[g1, nfw0, sc2, sh2], whole_rows=True)
    sh1b, sc1b, g1b, sh2b, sc2b, g2b = md[1]
    nmw1, nfw1 = P["norm_mix_w"][1:2], P["norm_ffn_w"][1:2]
    a0, b0, f0, y0, x2, h1 = _ffn_fwd("0", h0f, W["gate_wt0"], W["up_wt0"], W["down_w0"], x1, g2,
                                      next_norm=(nmw1, sc1b, sh1b))

    W.update(io["weights1"](x2))
    qkv = _mm("qkv_proj", [h1], [W["qkv_wt"]], "nt", [F32], epi=lambda acc, b: acc + b, vecs=[P["qkv_b"]])[0]
    onehot_t = jnp.asarray(_bucket_onehot_t())
    bias = _rel_bias(P["rel_table"].T, onehot_t).reshape(16, L, 2 * L)
    sinks = P["sinks"].reshape(16)
    att, lse = _attn_fwd(qkv, bias, sinks)

    def res_b(y, x, gate, b, nw, sc, sh):
        y = y + b
        xo = x + gate * y
        return y, xo, _norm_mod(xo, nw, sc, sh)
    mix1, x3, h1f = _mm("o_proj", [att], [W["o_w"]], "nn", [F32, F32, BF16], epi=res_b, extras=[x2],
                        vecs=[g1b, P["o_b"], nfw1, sc2b, sh2b], whole_rows=True)
    a1, b1, f1, y1, x4, _ = _ffn_fwd("1", h1f, W["gate_wt1"], W["up_wt1"], W["down_w1"], x3, g2b)

    dx, dy, sq, g["final_norm_w"], dg2b = _loss_head(x4, tgt, P["final_norm_w"], y1, g2b)

    dh, dwg1, dwu1, dwd1 = _ffn_bwd("1", dy, h1f, a1, b1, f1, W["gate_wt1"], W["up_wt1"], W["down_w1"])
    dx, dmix, dsh2b, dsc2b, dnfw1, dg1b, g["o_b"] = _norm_mod_bwd("norm_ffn_bwd_1", x3, dh, dx, nfw1, sc2b,
                                                                 gate=(mix1, g1b))
    G["o_w"] = _mm_tn("o_dw", att, dmix)
    d_att = _mm("o_dx", [dmix], [W["o_w"]], "nt", [F32])[0]
    dq, dkv, dbias, dsinks, dbq, dbkv = _attn_bwd(qkv, d_att, lse, bias, sinks)
    g["rel_table"] = _rel_bias_bwd(dbias.reshape(16, L * 2 * L), onehot_t).T
    g["sinks"] = dsinks[:, :16]
    g["qkv_b"] = jnp.concatenate([dbq, dbkv], axis=1)
    w_q, w_kv = W["qkv_wt"][:1024], W["qkv_wt"][1024:]
    G["qkv_wt"] = jnp.concatenate(_mm_tn_shared_rhs("qkv_dw", [dq, dkv], h1), axis=0)
    dh = _mm("qkv_dx", [dq, dkv], [w_q, w_kv], "nn", [F32])[0]
    behind = io["grads1"]({"qkv_wt": G.pop("qkv_wt"), "o_w": G.pop("o_w"), "gate_wt1": dwg1, "up_wt1": dwu1,
                           "down_w1": dwd1})
    dx, dy, dsh1b, dsc1b, dnmw1, dg2, _ = _norm_mod_bwd("norm_mix_bwd_1", x2, dh, dx, nmw1, sc1b, gate=(y0, g2),
                                                        after=behind)

    dh, dwg0, dwu0, dwd0 = _ffn_bwd("0", dy, h0f, a0, b0, f0, W["gate_wt0"], W["up_wt0"], W["down_w0"])
    behind = io["grads_ffn0"]({"gate_wt0": dwg0, "up_wt0": dwu0, "down_w0": dwd0})
    dx, dmix, dsh2, dsc2, dnfw0, dg1, _ = _norm_mod_bwd("norm_ffn_bwd_0", x1, dh, dx, nfw0, sc2, gate=(mix0, g1),
                                                        after=behind)
    G["out_w"] = jnp.concatenate(_mm_tn_shared_rhs("out_dw", [ya, yb], dmix), axis=0)
    dya, dyb = _mm_shared_lhs("out_dx", dmix, [w_oa, w_ob])
    du, dv, dws, dbse, g["gmlp_ln_w"], g["gmlp_ln_b"] = _gmlp_bwd(dyb, proj["u"], proj["v"], P["gmlp_ln_w"],
                                                                 P["gmlp_ln_b"], ws, bse)
    g["gmlp_ws"] = dws[None]
    g["gmlp_bs"] = _lane_sum("gmlp_dbs", dbse.reshape(8 * L, 128)).reshape(1, 8, L)
    dz, dxc, ddt, g["ssm_norm_w"], ddsk, dalog, ddtb = _ssd_bwd(dya, y_ssd, proj["z"], xc, proj["dt"], prev,
                                                                dtb, alog, dskl, P["ssm_norm_w"])
    g["d_skip"], g["a_log"], g["dt_bias"] = ddsk[0:1, :16], dalog[:, :16], ddtb[:, :16]
    dxr, dconv_w, g["conv_b"] = _conv_bwd(dxc, pre, proj["xbc"], conv_w)
    g["conv_w"] = dconv_w[None]
    dsegs = {"z": dz, "xbc": dxr, "dt": ddt, "u": du, "v": dv}
    dws_in = dict(zip(dsegs, _mm_tn_shared_rhs("in_dw", list(dsegs.values()), h0)))
    G["in_wt"] = jnp.concatenate([dws_in["z"], dws_in["xbc"], dws_in["dt"][:16], dws_in["u"], dws_in["v"]], axis=0)
    keys = ["z", "xbc", "dt", "u", "v"]
    dh = _mm("in_dx", [dsegs[k] for k in keys], [segs[k] for k in keys], "nn", [F32])[0]
    dx, dsh1, dsc1, dnmw0 = _norm_mod_bwd("norm_mix_bwd_0", x, dh, dx, nmw0, sc1)

    g["norm_mix_w"] = jnp.concatenate([dnmw0, dnmw1], axis=0)
    g["norm_ffn_w"] = jnp.concatenate([dnfw0, dnfw1], axis=0)
    dmod = jnp.concatenate([jnp.concatenate([dsh1, dsc1, dg1, dsh2, dsc2, dg2], axis=1),
                            jnp.concatenate([dsh1b, dsc1b, dg1b, dsh2b, dsc2b, dg2b], axis=1)], axis=0)
    return sq, dx, dmod, G, g


def _ada_fwd(c_all, ada_w, ada_b):
    n = ada_w.shape[2]
    tn = _col_tile(n, 512)

    def body(c_ref, w_ref, b_ref, o_ref):
        cc = c_ref[...]
        o_ref[...] = lax.dot_general(cc * _sigmoid(cc), w_ref[...], NN, precision=lax.Precision.HIGHEST,
                                     preferred_element_type=F32) + b_ref[...]

    return pl.pallas_call(
        body, name="ada_fwd", grid=(2, n // tn),
        in_specs=[pl.BlockSpec((8, D), lambda l, j: (0, 0)), pl.BlockSpec((None, D, tn), lambda l, j: (l, 0, j)),
                  pl.BlockSpec((None, 1, tn), lambda l, j: (l, 0, j))],
        out_specs=pl.BlockSpec((None, 8, tn), lambda l, j: (l, 0, j)),
        out_shape=jax.ShapeDtypeStruct((2, 8, n), F32), compiler_params=_params(("parallel", "parallel")))(
            c_all, ada_w, ada_b)


def _ada_bwd(c_all, dmod_cols, dmod_all):
    n = dmod_cols.shape[2]
    tn = _col_tile(n, 512)

    def body(c_ref, d_ref, o_ref):
        cc = c_ref[...]
        o_ref[...] = lax.dot_general(cc * _sigmoid(cc), d_ref[...], TN, precision=lax.Precision.HIGHEST,
                                     preferred_element_type=F32)

    dw = pl.pallas_call(
        body, name="ada_dw", grid=(2, n // tn),
        in_specs=[pl.BlockSpec((8, D), lambda l, j: (0, 0)), pl.BlockSpec((None, 8, tn), lambda l, j: (l, 0, j))],
        out_specs=pl.BlockSpec((None, D, tn), lambda l, j: (l, 0, j)),
        out_shape=jax.ShapeDtypeStruct((2, D, n), F32), compiler_params=_params(("parallel", "parallel")))(
            c_all, dmod_cols)

    def sum_body(d_ref, o_ref):
        o_ref[...] = jnp.sum(d_ref[...], axis=0, keepdims=True)

    db = pl.pallas_call(
        sum_body, name="ada_db", grid=(2,),
        in_specs=[pl.BlockSpec((None, 8, 6 * D), lambda l: (l, 0, 0))],
        out_specs=pl.BlockSpec((None, 1, 6 * D), lambda l: (l, 0, 0)),
        out_shape=jax.ShapeDtypeStruct((2, 1, 6 * D), F32), compiler_params=_params(("parallel",)))(dmod_all)
    return dw, db


def _row_tile(rows, cap=512, mult=8):
    best = rows
    for t in range(mult, min(rows, cap) + 1, mult):
        if rows % t == 0:
            best = t
    return best


def _adamw(name, w, g, m, v):
    def fn(w, g, m, v):
        m = ADAM_B1 * m + (1.0 - ADAM_B1) * g
        v = ADAM_B2 * v + (1.0 - ADAM_B2) * (g * g)
        m_hat = m / (1.0 - ADAM_B1 ** ADAM_STEP)
        v_hat = v / (1.0 - ADAM_B2 ** ADAM_STEP)
        return -ADAM_LR * (m_hat / (jnp.sqrt(v_hat) + ADAM_EPS) + ADAM_WD * w), m, v
    cols = w.shape[1]
    return _rowwise(name, fn, [w, g, m, v], [], [(cols, F32)] * 3, tr=_row_tile(w.shape[0]))


def _place():
    return lax.axis_index("x"), lax.axis_index("y"), lax.axis_index("c")


VMEM_SPEC = pl.BlockSpec(memory_space=pltpu.VMEM)


def _allreduce_small(name, buf, after=None):
    rows = buf.shape[0]
    deps = [] if after is None else [after]

    def body(x_ref, *rest):
        o_ref, stage, send_sems, recv_sems = rest[len(deps):]
        x, y, c = _place()
        me = 4 * x + 2 * y + c
        stage[me] = x_ref[...]
        copies = []
        for k in range(1, 8):
            peer = (1 - x if k & 4 else x, 1 - y if k & 2 else y, 1 - c if k & 1 else c)
            cp = pltpu.make_async_remote_copy(src_ref=x_ref, dst_ref=stage.at[me], send_sem=send_sems.at[k - 1],
                                              recv_sem=recv_sems.at[k - 1], device_id=peer, device_id_type=MESH)
            cp.start()
            copies.append(cp)
        for cp in copies:
            cp.wait()
        acc = stage[0]
        for d in range(1, 8):
            acc = acc + stage[d]
        o_ref[...] = acc

    return pl.pallas_call(
        body, name=name, in_specs=[VMEM_SPEC] + [ANY for _ in deps], out_specs=VMEM_SPEC,
        out_shape=jax.ShapeDtypeStruct((rows, 128), F32),
        scratch_shapes=[pltpu.VMEM((8, rows, 128), F32), pltpu.SemaphoreType.DMA((7,)), pltpu.SemaphoreType.DMA((7,))],
        compiler_params=pltpu.CompilerParams(vmem_limit_bytes=_VMEM_LIMIT))(buf, *deps)


def _sum_slots(name, own, land):
    def body(own_ref, land_ref, o_ref):
        x, y, c = _place()
        me = 4 * x + 2 * y + c
        acc = None
        for d in range(8):
            v = jnp.where(me == d, own_ref[...], land_ref[d])
            acc = v if acc is None else acc + v
        o_ref[...] = acc

    return pl.pallas_call(body, name=name, in_specs=[VMEM_SPEC, VMEM_SPEC], out_specs=VMEM_SPEC,
                          out_shape=jax.ShapeDtypeStruct(own.shape, F32),
                          compiler_params=pltpu.CompilerParams(vmem_limit_bytes=_VMEM_LIMIT))(own, land)


OTHER_CHIPS = ((1, 0), (0, 1), (1, 1))


SIBLING_COLLECTIVE_ID = 6


def _sibling_handshake():
    x, y, c = _place()
    barrier = pltpu.get_barrier_semaphore()
    pl.semaphore_signal(barrier, inc=1, device_id=(x, y, 1 - c), device_id_type=MESH)
    pl.semaphore_wait(barrier, 1)


def _sibling_swap(name, src, halves):
    half = src.shape[-2] // 2
    out_shape = (src.shape[0], half, 1024) if halves else src.shape

    def body(s_ref, o_ref, send_sem, recv_sem):
        x, y, c = _place()
        _sibling_handshake()
        part = s_ref.at[:, pl.ds(pl.multiple_of((1 - c) * half, 8), half)] if halves else s_ref
        cp = pltpu.make_async_remote_copy(src_ref=part, dst_ref=o_ref, send_sem=send_sem, recv_sem=recv_sem,
                                          device_id=(x, y, 1 - c), device_id_type=MESH)
        cp.start()
        cp.wait()

    return pl.pallas_call(
        body, name=name, in_specs=[ANY], out_specs=ANY, out_shape=jax.ShapeDtypeStruct(out_shape, src.dtype),
        scratch_shapes=[pltpu.SemaphoreType.DMA, pltpu.SemaphoreType.DMA],
        compiler_params=pltpu.CompilerParams(collective_id=SIBLING_COLLECTIVE_ID))(src)


HBM = pl.BlockSpec(memory_space=pltpu.HBM)
SEM = pl.BlockSpec(memory_space=pltpu.SEMAPHORE)


def _exchange_peers(mode):
    x, y, c = _place()
    if mode == "all":
        return [(1 - x if k & 4 else x, 1 - y if k & 2 else y, 1 - c if k & 1 else c) for k in range(1, 8)]
    return [(1 - x if fx else x, 1 - y if fy else y, c) for fx, fy in OTHER_CHIPS]


def _chip_copies(mode, src_ref, land_ref, send_sems, recv_sems):
    x, y, c = _place()
    k = 2 * x + y
    copies = []
    for j, peer in enumerate(_exchange_peers(mode)):
        if mode == "gather":
            half = src_ref.shape[0] // 2
            mine = pl.ds(pl.multiple_of(c * half, 16), half)
            src, dst = src_ref.at[mine], land_ref.at[k, mine]
        elif mode == "scatter":
            src, dst = src_ref.at[2 * peer[0] + peer[1]], land_ref.at[k]
        else:
            src, dst = src_ref, land_ref.at[4 * x + 2 * y + c]
        copies.append(pltpu.make_async_remote_copy(src_ref=src, dst_ref=dst, send_sem=send_sems.at[j],
                                                   recv_sem=recv_sems.at[j], device_id=peer, device_id_type=MESH))
    return copies


def _exchange_start(name, collective_id, mode, src, land, after=None):
    deps = [] if after is None else [after]
    npeers = 7 if mode == "all" else 3

    def body(s_ref, l_ref, *rest):
        send_sems, recv_sems, s_thru, l_thru, token = rest[len(deps):]
        barrier = pltpu.get_barrier_semaphore()
        for peer in _exchange_peers(mode):
            pl.semaphore_signal(barrier, inc=1, device_id=peer, device_id_type=MESH)
        pl.semaphore_wait(barrier, npeers)
        for cp in _chip_copies(mode, s_ref, l_ref, send_sems, recv_sems):
            cp.start()
        token[...] = jnp.zeros_like(token)

    return pl.pallas_call(
        body, name=name,
        out_shape=(pltpu.SemaphoreType.DMA((npeers,)), pltpu.SemaphoreType.DMA((npeers,)),
                   pltpu.HBM(src.shape, src.dtype),
                   pltpu.HBM(land.shape, land.dtype), jax.ShapeDtypeStruct((8, 128), F32)),
        in_specs=(HBM, HBM) + tuple(ANY for _ in deps), out_specs=(SEM, SEM, HBM, HBM, VMEM_SPEC),
        input_output_aliases={0: 2, 1: 3},
        compiler_params=pltpu.CompilerParams(has_side_effects=pltpu.SideEffectType.DATAFLOW_SIDE_EFFECTING,
                                             collective_id=collective_id))(
            pltpu.with_memory_space_constraint(src, pltpu.HBM), pltpu.with_memory_space_constraint(land, pltpu.HBM),
            *deps)


def _exchange_wait(name, mode, started, after):
    send_sems, recv_sems, s_thru, l_thru, _ = started
    deps = list(after) if isinstance(after, (tuple, list)) else [after]

    def body(s_ref, l_ref, send_sems, recv_sems, *rest):
        for cp in _chip_copies(mode, s_ref, l_ref, send_sems, recv_sems):
            cp.wait_send()
            cp.wait_recv()

    return pl.pallas_call(
        body, name=name, out_shape=(pltpu.HBM(s_thru.shape, s_thru.dtype), pltpu.HBM(l_thru.shape, l_thru.dtype)),
        in_specs=(HBM, HBM, SEM, SEM) + tuple(ANY for _ in deps), out_specs=(HBM, HBM),
        input_output_aliases={0: 0, 1: 1},
        compiler_params=pltpu.CompilerParams(has_side_effects=pltpu.SideEffectType.DATAFLOW_SIDE_EFFECTING))(
            s_thru, l_thru, send_sems, recv_sems, *deps)


def _allgather_finish(tag, land):
    half = land.shape[1] // 2

    def body(l_ref, o_ref, send_sem, recv_sem):
        x, y, c = _place()
        _sibling_handshake()
        mine = pl.ds(pl.multiple_of(c * half, 16), half)
        swap = pltpu.make_async_remote_copy(src_ref=o_ref.at[:, mine], dst_ref=o_ref.at[:, mine], send_sem=send_sem,
                                            recv_sem=recv_sem, device_id=(x, y, 1 - c), device_id_type=MESH)
        swap.start()
        swap.wait()

    return pl.pallas_call(
        body, name="allgather_finish_" + tag, in_specs=[ANY], out_specs=ANY, input_output_aliases={0: 0},
        out_shape=jax.ShapeDtypeStruct(land.shape, land.dtype),
        scratch_shapes=[pltpu.SemaphoreType.DMA, pltpu.SemaphoreType.DMA],
        compiler_params=pltpu.CompilerParams(collective_id=SIBLING_COLLECTIVE_ID))(land)


def _pair_sum(tag, g, r1, c):
    rows = g.shape[1]
    half = rows // 2
    th = _row_tile(half, 1408, 16)
    nblk = half // th

    def body(c_ref, g_ref, r_ref, o_ref, o2_ref):
        o_ref[...] = (g_ref[...].astype(F32) + r_ref[...].astype(F32)).astype(o_ref.dtype)
        o2_ref[...] = o_ref[...]

    spec = pl.BlockSpec((None, th, 1024), lambda k, i, c_ref: (k, i, 0))
    grid_spec = pltpu.PrefetchScalarGridSpec(
        num_scalar_prefetch=1, grid=(4, nblk),
        in_specs=[pl.BlockSpec((None, th, 1024), lambda k, i, c_ref: (k, c_ref[0] * nblk + i, 0)), spec],
        out_specs=[spec, spec])
    return pl.pallas_call(body, name="grad_pair_sum_" + tag, grid_spec=grid_spec,
                          out_shape=[jax.ShapeDtypeStruct((4, half, 1024), BF16)] * 2,
                          compiler_params=_params(("parallel", "parallel")))(c, g, r1)


def _chip_sum(tag, q, after=None):
    half = q.shape[1]
    th = _row_tile(half, 704, 16)
    deps = [] if after is None else [after]

    def body(a, b, c, d, *rest):
        rest[-1][...] = ((a[...].astype(F32) + b[...].astype(F32)) + c[...].astype(F32)) + d[...].astype(F32)

    specs = [pl.BlockSpec((None, th, 1024), functools.partial(lambda i, k: (k, i, 0), k=k)) for k in range(4)]
    return pl.pallas_call(body, name="grad_chip_sum_" + tag, grid=(half // th,), in_specs=specs + [ANY for _ in deps],
                          out_specs=pl.BlockSpec((th, 1024), lambda i: (i, 0)),
                          out_shape=jax.ShapeDtypeStruct((half, 1024), F32),
                          compiler_params=_params(("parallel",)))(q, q, q, q, *deps)


def _join_halves(tag, f, r, c):
    half = f.shape[0]
    th = _row_tile(half, 704)
    nblk = half // th

    def body(c_ref, f_ref, r_ref, o_ref):
        mine = (pl.program_id(0) == c_ref[0])
        o_ref[...] = jnp.where(mine, f_ref[...], r_ref[...])

    spec = pl.BlockSpec((th, 1024), lambda h, i, c_ref: (i, 0))
    grid_spec = pltpu.PrefetchScalarGridSpec(
        num_scalar_prefetch=1, grid=(2, nblk), in_specs=[spec, spec],
        out_specs=pl.BlockSpec((th, 1024), lambda h, i, c_ref: (h * nblk + i, 0)))
    return pl.pallas_call(body, name="grad_join_halves_" + tag, grid_spec=grid_spec,
                          out_shape=jax.ShapeDtypeStruct((2 * half, 1024), F32),
                          compiler_params=_params(("parallel", "parallel")))(c, f, r)


BIG_ARGS = ("in_w_even", "out_w_even", "qkv_w", "o_w", "ffn_gate_w", "ffn_up_w", "ffn_down_w")
def _ffn_pieces(layer):
    return tuple((f"{n}{layer}", 704, 704) for n in ("gate_wt", "up_wt", "down_w"))


IN_SLAB = (("in_wt", 1156, 1184),)
LAYER0_REST_SLAB = (("out_w", 512, 512),) + _ffn_pieces(0)
LAYER1_SLAB = (("qkv_wt", 320, 320), ("o_w", 256, 256)) + _ffn_pieces(1)
FFN0_SLAB = _ffn_pieces(0)
MIXER0_SLAB = (("in_wt", 1156, 1280), ("out_w", 512, 512))


def _slab(pieces, spec):
    parts = []
    for name, rows, room in spec:
        p = pieces[name]
        parts.append(jnp.pad(p, [(0, 0)] * (p.ndim - 2) + [(0, room - rows), (0, 0)]) if room > rows else p)
    return jnp.concatenate(parts, axis=-2) if len(parts) > 1 else parts[0]


def _unslab(slab, spec):
    out, off = {}, 0
    for name, rows, room in spec:
        out[name] = slab[..., off:off + rows, :]
        off += room
    return out


def _share_pieces(w):
    return {"in_wt": w["in_w_even"][0].T, "out_w": w["out_w_even"][0], "qkv_wt": w["qkv_w"][0].T, "o_w": w["o_w"][0],
            "gate_wt0": w["ffn_gate_w"][0].T, "gate_wt1": w["ffn_gate_w"][1].T,
            "up_wt0": w["ffn_up_w"][0].T, "up_wt1": w["ffn_up_w"][1].T,
            "down_w0": w["ffn_down_w"][0], "down_w1": w["ffn_down_w"][1]}


def _pieces_to_shares(p):
    return {"in_w_even": p["in_wt"].T[None], "out_w_even": p["out_w"][None], "qkv_w": p["qkv_wt"].T[None],
            "o_w": p["o_w"][None], "ffn_gate_w": jnp.stack([p["gate_wt0"].T, p["gate_wt1"].T]),
            "ffn_up_w": jnp.stack([p["up_wt0"].T, p["up_wt1"].T]),
            "ffn_down_w": jnp.stack([p["down_w0"], p["down_w1"]])}


def _chips_from_full(G, spec):
    return _slab({k: v.reshape(4, -1, D) for k, v in G.items()}, spec)


def _pack_small(parts):
    padded = []
    for p in parts:
        p = p.reshape(-1).astype(F32)
        padded.append(jnp.pad(p, (0, (-p.shape[0]) % 1024)))
    return jnp.concatenate(padded).reshape(-1, 128)


def _unpack_small(slab, shapes):
    flat, out, off = slab.reshape(-1), [], 0
    for shp in shapes:
        size = math.prod(shp)
        out.append(flat[off:off + size].reshape(shp))
        off += size + (-size) % 1024
    return out


SMALL = ("ada_b", "norm_mix_w", "norm_ffn_w", "conv_w", "conv_b", "dt_bias", "a_log", "d_skip", "ssm_norm_w",
         "gmlp_ln_w", "gmlp_ln_b", "gmlp_ws", "gmlp_bs", "qkv_b", "o_b", "sinks", "rel_table", "final_norm_w")
SMALL_SPLIT = {"conv_w": 1536, "qkv_b": 1280, "o_b": 1024}
WEIGHTS = ("ada_w", "ada_b", "norm_mix_w", "norm_ffn_w", "in_w_even", "conv_w", "conv_b", "dt_bias", "a_log", "d_skip",
           "ssm_norm_w", "gmlp_ln_w", "gmlp_ln_b", "gmlp_ws", "gmlp_bs", "out_w_even", "qkv_w", "qkv_b", "o_w", "o_b",
           "sinks", "rel_table", "ffn_gate_w", "ffn_up_w", "ffn_down_w", "final_norm_w")


def kernel(x, c, ada_w, ada_b, norm_mix_w, norm_ffn_w, in_w_even, conv_w, conv_b, dt_bias, a_log, d_skip, ssm_norm_w, gmlp_ln_w, gmlp_ln_b, gmlp_ws, gmlp_bs, out_w_even, qkv_w, qkv_b, o_w, o_b, sinks, rel_table, ffn_gate_w, ffn_up_w, ffn_down_w, final_norm_w, loss_target, m_ada_w, m_ada_b, m_norm_mix_w, m_norm_ffn_w, m_in_w_even, m_conv_w, m_conv_b, m_dt_bias, m_a_log, m_d_skip, m_ssm_norm_w, m_gmlp_ln_w, m_gmlp_ln_b, m_gmlp_ws, m_gmlp_bs, m_out_w_even, m_qkv_w, m_qkv_b, m_o_w, m_o_b, m_sinks, m_rel_table, m_ffn_gate_w, m_ffn_up_w, m_ffn_down_w, m_final_norm_w, v_ada_w, v_ada_b, v_norm_mix_w, v_norm_ffn_w, v_in_w_even, v_conv_w, v_conv_b, v_dt_bias, v_a_log, v_d_skip, v_ssm_norm_w, v_gmlp_ln_w, v_gmlp_ln_b, v_gmlp_ws, v_gmlp_bs, v_out_w_even, v_qkv_w, v_qkv_b, v_o_w, v_o_b, v_sinks, v_rel_table, v_ffn_gate_w, v_ffn_up_w, v_ffn_down_w, v_final_norm_w):
    args = dict(locals())
    w = {n: args[n] for n in WEIGHTS}
    m = {n: args["m_" + n] for n in WEIGHTS}
    v = {n: args["v_" + n] for n in WEIGHTS}
    ax, ay, ac = _place()
    me = 4 * ax + 2 * ay + ac
    chip = 2 * ax + ay
    south = (ac == 0).astype(F32)
    c_arr = jnp.reshape(ac, (1,)).astype(jnp.int32)

    c_all = _allreduce_small("gather_cond", lax.dynamic_update_slice(jnp.zeros((8, D), F32), c, (me, 0)).reshape(64, 128))
    c_all = c_all.reshape(8, D)
    n_ada = ada_w.shape[2]
    mod_cols = _ada_fwd(c_all, ada_w, lax.dynamic_slice(ada_b, (0, chip * n_ada), (2, n_ada)).reshape(2, 1, n_ada))
    pieces = [lax.dynamic_update_slice(jnp.zeros((2, 8, 6 * D), F32), mod_cols, (0, 0, chip * n_ada))]
    split_names = list(SMALL_SPLIT)
    for n in split_names:
        full = SMALL_SPLIT[n]
        local = w[n]
        idx = (0,) * (local.ndim - 1) + (chip * local.shape[-1],)
        pieces.append(lax.dynamic_update_slice(jnp.zeros(local.shape[:-1] + (full,), F32), local, idx))
    shapes = [p.shape for p in pieces]
    mod_own = _pack_small(pieces) * south
    mod_started = _exchange_start("gather_mod_start", 9, "all", mod_own, lax.empty((8,) + mod_own.shape, F32))

    pieces = _share_pieces(w)
    cast = {"in_wt": pieces["in_wt"].astype(_MXU)}

    def start_gather(tag, collective_id, share, after):
        return _exchange_start("allgather_start_" + tag, collective_id, "gather", share,
                               lax.empty((4,) + share.shape, share.dtype), after=after)

    def finish_gather(tag, started, spec, after):
        land = _exchange_wait("allgather_wait_" + tag, "gather", started, after)[1]
        out = {}
        for name, piece in _unslab(_allgather_finish(tag, land), spec).items():
            out[name] = lax.dynamic_update_slice(piece.reshape(-1, D), cast[name], (chip * piece.shape[1], 0))
        return out

    gather_in = start_gather("in", 7, _slab(cast, IN_SLAB), mod_started[4])
    zero = gather_in[4][0, 0]
    cast.update({k: (p + zero).astype(_MXU) for k, p in pieces.items() if k != "in_wt"})
    share0, share1 = _slab(cast, LAYER0_REST_SLAB), _slab(cast, LAYER1_SLAB)
    mod_own, mod_land = _exchange_wait("gather_mod_wait", "all", mod_started, (share0, share1))
    mod_slab = _sum_slots("gather_mod_sum", mod_own, mod_land)
    gathered = _unpack_small(mod_slab, shapes)
    mod = lax.dynamic_slice(gathered[0], (0, me, 0), (2, 1, 6 * D)).reshape(2, 6 * D)
    P = {n: w[n] for n in SMALL if n not in SMALL_SPLIT and n != "ada_b"}
    for n, full in zip(split_names, gathered[1:]):
        P[n] = full
    P["final_norm_w"] = final_norm_w.reshape(1, D)
    w_in = finish_gather("in", gather_in, IN_SLAB, mod_slab)["in_wt"]
    gather0 = start_gather("0", 1, share0, w_in)
    gather1 = start_gather("1", 2, share1, gather0[4])

    def start_reduce(tag, collective_id, G, spec, after=None):
        gp = _chips_from_full(G, spec).astype(BF16)
        p, q = _pair_sum(tag, gp, _sibling_swap("grad_pair_exchange_" + tag, gp, True), c_arr)
        return _exchange_start("grad_exchange_start_" + tag, collective_id, "scatter", p, q, after=after)

    def finish_reduce(tag, started, spec, after, behind=None):
        q = _exchange_wait("grad_exchange_wait_" + tag, "scatter", started, after)[1]
        fin = _chip_sum(tag, q, after=behind)
        total = _join_halves(tag, fin, _sibling_swap("grad_final_exchange_" + tag, fin, False), c_arr)
        return _unslab(total, spec)

    reduces = {}

    def grads1(G1):
        reduces["1"] = start_reduce("1", 3, G1, LAYER1_SLAB)
        return reduces["1"][4]

    def grads_ffn0(G):
        reduces["f"] = start_reduce("f", 4, G, FFN0_SLAB)
        return reduces["f"][4]

    io = {"start": gather1[4],
          "weights0": lambda after: finish_gather("0", gather0, LAYER0_REST_SLAB, after),
          "weights1": lambda after: finish_gather("1", gather1, LAYER1_SLAB, after),
          "grads1": grads1, "grads_ffn0": grads_ffn0}
    sq, grad_x, dmod, G0, g = _local_step(x[0], loss_target[0], mod, w_in, P, io)
    loss = lax.psum(0.5 * sq[0, 0] / D, ("x", "y", "c"))

    g["final_norm_w"] = g["final_norm_w"].reshape(D)
    small_names = [n for n in SMALL if n != "ada_b"]
    pieces = [lax.dynamic_update_slice(jnp.zeros((2, 8, 6 * D), F32), dmod.reshape(2, 1, 6 * D), (0, me, 0))]
    pieces += [g[n] for n in small_names]
    shapes = [p.shape for p in pieces]
    small_own = _pack_small(pieces)
    small_started = _exchange_start("small_grads_start", 8, "all", small_own, lax.empty((8,) + small_own.shape, F32))
    reduces["m"] = start_reduce("m", 5, G0, MIXER0_SLAB, after=small_started[4])
    shares = finish_reduce("1", reduces["1"], LAYER1_SLAB, grad_x, behind=reduces["m"][4])
    shares.update(finish_reduce("f", reduces["f"], FFN0_SLAB, grad_x, behind=reduces["m"][4]))
    small_own, small_land = _exchange_wait("small_grads_wait", "all", small_started, shares["down_w0"])
    reduced = _unpack_small(_sum_slots("small_grads_sum", small_own, small_land), shapes)
    dmod_all = reduced[0]
    grads = dict(zip(small_names, reduced[1:]))
    for n in split_names:
        full = grads[n]
        size = w[n].shape[-1]
        grads[n] = lax.dynamic_slice(full, (0,) * (full.ndim - 1) + (chip * size,), full.shape[:-1] + (size,))
    grads = {n: grads[n].reshape(w[n].shape) for n in small_names}
    dw_ada, db_ada = _ada_bwd(c_all, lax.dynamic_slice(dmod_all, (0, 0, chip * n_ada), (2, 8, n_ada)), dmod_all)
    grads["ada_w"], grads["ada_b"] = dw_ada, db_ada.reshape(2, 6 * D)

    delta, new_m, new_v = {}, {}, {}

    def update(n):
        cols = w[n].shape[-1]
        d_, m_, v_ = _adamw("adamw_" + n, w[n].reshape(-1, cols), grads[n].reshape(-1, cols), m[n].reshape(-1, cols),
                            v[n].reshape(-1, cols))
        delta[n], new_m[n], new_v[n] = d_.reshape(w[n].shape), m_.reshape(w[n].shape), v_.reshape(w[n].shape)

    update("ada_w")
    shapes = [w[n].shape for n in SMALL]
    packed = [_pack_small([t[n] for n in SMALL]) for t in (w, grads, m, v)]
    outs = _adamw("adamw_small", *packed)
    for dst, slab in zip((delta, new_m, new_v), outs):
        for n, t in zip(SMALL, _unpack_small(slab, shapes)):
            dst[n] = t
    shares.update(finish_reduce("m", reduces["m"], MIXER0_SLAB, outs[0]))
    grads.update(_pieces_to_shares(shares))
    for n in BIG_ARGS:
        update(n)
    return (loss, grad_x[None], *[grads[n] for n in WEIGHTS], *[delta[n] for n in WEIGHTS],
            *[new_m[n] for n in WEIGHTS], *[new_v[n] for n in WEIGHTS])
```

```python
import functools
import math

import numpy as np
import jax
import jax.numpy as jnp
from jax import lax
from jax.experimental import pallas as pl
from jax.experimental.pallas import tpu as pltpu

F32 = jnp.float32
BF16 = jnp.bfloat16
_MXU = jnp.bfloat16
_VMEM_LIMIT = 56 * 1024 * 1024
MXU_COLS = 256
D = 1024
L = 128
NSTATE = 128
EPS = 1e-6
NEG_INF = -1e30
FFN = 2816
ADAM_LR, ADAM_B1, ADAM_B2, ADAM_EPS, ADAM_WD, ADAM_STEP = 0.001, 0.9, 0.999, 1e-08, 0.01, 10
MESH = pl.DeviceIdType.MESH
ANY = pl.BlockSpec(memory_space=pl.ANY)

NN = (((1,), (0,)), ((), ()))
NT = (((1,), (1,)), ((), ()))
TN = (((0,), (0,)), ((), ()))


def _dot(a, b, dn=NN):
    return lax.dot_general(a.astype(_MXU), b.astype(_MXU), dn, preferred_element_type=F32)


def _params(sem=None):
    return pltpu.CompilerParams(dimension_semantics=sem, vmem_limit_bytes=_VMEM_LIMIT)


def _sigmoid(x):
    return 1.0 / (1.0 + jnp.exp(-x))


def _softplus(x):
    return jnp.maximum(x, 0.0) + jnp.log(1.0 + jnp.exp(-jnp.abs(x)))


def _gelu(x):
    return 0.5 * x * (1.0 + lax.erf(x * (2.0 ** -0.5)))


def _gelu_grad(x):
    return 0.5 * (1.0 + lax.erf(x * (2.0 ** -0.5))) + x * jnp.exp(-0.5 * x * x) * (1.0 / math.sqrt(2.0 * math.pi))


def _silu_grad(a):
    sg = _sigmoid(a)
    return sg * (1.0 + a * (1.0 - sg))


def _rowwise(name, fn, rows, vecs, out_rows, out_accs=(), tr=512, after=None):
    S = rows[0].shape[0]
    tr = min(tr, S)
    assert S % tr == 0
    nr, nv, no, na = len(rows), len(vecs), len(out_rows), len(out_accs)
    deps = [] if after is None else [after]

    def body(*refs):
        ins, outs = refs[:nr + nv], refs[nr + nv + len(deps):]
        res = fn(*[r[...] for r in ins])
        if not isinstance(res, (tuple, list)):
            res = (res,)
        for k in range(no):
            outs[k][...] = res[k].astype(outs[k].dtype)
        if na:
            @pl.when(pl.program_id(0) == 0)
            def _():
                for k in range(na):
                    outs[no + k][...] = jnp.zeros_like(outs[no + k])
            for k in range(na):
                outs[no + k][...] += res[no + k]

    in_specs = [pl.BlockSpec((tr, a.shape[1]), lambda i: (i, 0)) for a in rows]
    in_specs += [pl.BlockSpec(v.shape, lambda i: (0, 0)) for v in vecs] + [ANY for _ in deps]
    out_specs = [pl.BlockSpec((tr, c), lambda i: (i, 0)) for c, _ in out_rows]
    out_specs += [pl.BlockSpec(s, lambda i: (0, 0)) for s in out_accs]
    out_shape = [jax.ShapeDtypeStruct((S, c), dt) for c, dt in out_rows]
    out_shape += [jax.ShapeDtypeStruct(s, F32) for s in out_accs]
    return pl.pallas_call(body, name=name, grid=(S // tr,), in_specs=in_specs, out_specs=out_specs,
                          out_shape=out_shape, compiler_params=_params(("arbitrary",)))(*rows, *vecs, *deps)


def _col_tile(n, cap):
    if n <= cap or n % 128:
        return n
    best = 128
    for t in range(128, cap + 1, 128):
        if n % t == 0:
            best = t
    return best


def _mm(name, As, Bs, mode, outs, epi=None, groups=None, extras=(), vecs=(), tm=512, tn_cap=1536, whole_rows=False):
    M = As[0].shape[0]
    N = Bs[0].shape[1] if mode == "nn" else Bs[0].shape[0]
    tm = min(tm, M)
    tn = _col_tile(N, tn_cap)
    assert M % tm == 0 and N % tn == 0
    npair = len(As)
    groups = groups or [0] * npair
    ng = max(groups) + 1
    nx, nv = len(extras), len(vecs)
    dn = NN if mode == "nn" else NT

    def body(*refs):
        a_refs, b_refs = refs[:npair], refs[npair:2 * npair]
        x_refs = refs[2 * npair:2 * npair + nx]
        v_refs = refs[2 * npair + nx:2 * npair + nx + nv]
        o_refs = refs[2 * npair + nx + nv:]
        step = tn if (epi is None or whole_rows) else min(tn, MXU_COLS)
        for col in range(0, tn, step):
            sl = slice(col, min(col + step, tn))
            accs = [None] * ng
            for k in range(npair):
                b = b_refs[k][:, sl] if mode == "nn" else b_refs[k][sl, :]
                d = _dot(a_refs[k][...], b, dn)
                accs[groups[k]] = d if accs[groups[k]] is None else accs[groups[k]] + d
            args = accs + [x[:, sl] for x in x_refs] + [v[:, sl] for v in v_refs]
            res = epi(*args) if epi is not None else tuple(accs)
            if not isinstance(res, (tuple, list)):
                res = (res,)
            for o, r in zip(o_refs, res):
                o[:, sl] = r.astype(o.dtype)

    in_specs = [pl.BlockSpec((tm, a.shape[1]), lambda i, j: (i, 0)) for a in As]
    if mode == "nn":
        in_specs += [pl.BlockSpec((b.shape[0], tn), lambda i, j: (0, j)) for b in Bs]
    else:
        in_specs += [pl.BlockSpec((tn, b.shape[1]), lambda i, j: (j, 0)) for b in Bs]
    in_specs += [pl.BlockSpec((tm, tn), lambda i, j: (i, j)) for _ in extras]
    in_specs += [pl.BlockSpec((1, tn), lambda i, j: (0, j)) for _ in vecs]
    out_specs = [pl.BlockSpec((tm, tn), lambda i, j: (i, j)) for _ in outs]
    out_shape = [jax.ShapeDtypeStruct((M, N), dt) for dt in outs]
    return pl.pallas_call(body, name=name, grid=(M // tm, N // tn), in_specs=in_specs, out_specs=out_specs,
                          out_shape=out_shape, compiler_params=_params(("parallel", "parallel")))(
                              *As, *Bs, *extras, *vecs)


def _mm_shared_lhs(name, A, Bs, tm=512):
    M, K = A.shape
    tm = min(tm, M)
    assert M % tm == 0
    n = len(Bs)

    def body(a_ref, *refs):
        a = a_ref[...]
        for b_ref, o_ref in zip(refs[:n], refs[n:]):
            o_ref[...] = _dot(a, b_ref[...], NT)

    return pl.pallas_call(
        body, name=name, grid=(M // tm,),
        in_specs=[pl.BlockSpec((tm, K), lambda i: (i, 0))] + [pl.BlockSpec(b.shape, lambda i: (0, 0)) for b in Bs],
        out_specs=[pl.BlockSpec((tm, b.shape[0]), lambda i: (i, 0)) for b in Bs],
        out_shape=[jax.ShapeDtypeStruct((M, b.shape[0]), F32) for b in Bs],
        compiler_params=_params(("parallel",)))(A, *Bs)


def _mm_tn_shared_rhs(name, As, B, tk=256):
    S, N = B.shape
    tk = min(tk, S)
    assert S % tk == 0
    n, nk = len(As), S // tk

    def body(*refs):
        a_refs, b_ref, o_refs, acc_refs = refs[:n], refs[n], refs[n + 1:2 * n + 1], refs[2 * n + 1:]
        k = pl.program_id(0)

        @pl.when(k == 0)
        def _():
            for acc_ref in acc_refs:
                acc_ref[...] = jnp.zeros_like(acc_ref)
        b = b_ref[...]
        for a_ref, acc_ref in zip(a_refs, acc_refs):
            acc_ref[...] += _dot(a_ref[...], b, TN)

        @pl.when(k == nk - 1)
        def _():
            for o_ref, acc_ref in zip(o_refs, acc_refs):
                o_ref[...] = acc_ref[...].astype(o_ref.dtype)

    return pl.pallas_call(
        body, name=name, grid=(nk,),
        in_specs=[pl.BlockSpec((tk, a.shape[1]), lambda k: (k, 0)) for a in As] + [pl.BlockSpec((tk, N), lambda k: (k, 0))],
        out_specs=[pl.BlockSpec((a.shape[1], N), lambda k: (0, 0)) for a in As],
        out_shape=[jax.ShapeDtypeStruct((a.shape[1], N), BF16) for a in As],
        scratch_shapes=[pltpu.VMEM((a.shape[1], N), F32) for a in As],
        compiler_params=_params(("arbitrary",)))(*As, B)


def _mm_tn(name, A, B, tk=1024, t2_cap=1536):
    S, K1 = A.shape
    N2 = B.shape[1]
    tk = min(tk, S)
    t2 = _col_tile(N2, t2_cap)
    assert S % tk == 0 and N2 % t2 == 0
    nk = S // tk

    def body(a_ref, b_ref, o_ref, acc_ref):
        k = pl.program_id(1)

        @pl.when(k == 0)
        def _():
            acc_ref[...] = jnp.zeros_like(acc_ref)
        acc_ref[...] += _dot(a_ref[...], b_ref[...], TN)

        @pl.when(k == nk - 1)
        def _():
            o_ref[...] = acc_ref[...].astype(o_ref.dtype)

    return pl.pallas_call(
        body, name=name, grid=(N2 // t2, nk),
        in_specs=[pl.BlockSpec((tk, K1), lambda j, k: (k, 0)), pl.BlockSpec((tk, t2), lambda j, k: (k, j))],
        out_specs=pl.BlockSpec((K1, t2), lambda j, k: (0, j)),
        out_shape=jax.ShapeDtypeStruct((K1, N2), BF16), scratch_shapes=[pltpu.VMEM((K1, t2), F32)],
        compiler_params=_params(("parallel", "arbitrary")))(A, B)


def _norm_mod(x, nw, sc, sh):
    rstd = lax.rsqrt(jnp.mean(x * x, axis=-1, keepdims=True) + EPS)
    return (x * rstd * nw) * (1.0 + sc) + sh


def _norm_mod_fwd(name, x, nw, sc, sh, after=None):
    return _rowwise(name, _norm_mod, [x], [nw, sc, sh], [(D, BF16)], after=after)[0]


def _norm_mod_bwd(name, x, dh, dres, nw, sc, gate=None, after=None):
    def fn(x, dh, dres, *rest):
        nw, sc = rest[-3:-1] if gate else rest
        rstd = lax.rsqrt(jnp.mean(x * x, axis=-1, keepdims=True) + EPS)
        xh = x * rstd
        dn = dh * (1.0 + sc)
        dxh = dn * nw
        dx = dres + rstd * (dxh - xh * jnp.mean(dxh * xh, axis=-1, keepdims=True))
        sums = [jnp.sum(dh, axis=0, keepdims=True), jnp.sum(dh * (xh * nw), axis=0, keepdims=True),
                jnp.sum(dn * xh, axis=0, keepdims=True)]
        if not gate:
            return (dx, *sums)
        dy = dx * rest[-1]
        return (dx, dy, *sums, jnp.sum(dx * rest[0], axis=0, keepdims=True), jnp.sum(dy, axis=0, keepdims=True))
    if not gate:
        return _rowwise(name, fn, [x, dh, dres], [nw, sc], [(D, F32)], [(1, D)] * 3, after=after)
    return _rowwise(name, fn, [x, dh, dres, gate[0]], [nw, sc, gate[1]], [(D, F32), (D, BF16)], [(1, D)] * 5,
                    tr=1024, after=after)


def _loss_head(x, tgt, fw, y, g):
    def fn(x, tgt, y, fw, g):
        rstd = lax.rsqrt(jnp.mean(x * x, axis=-1, keepdims=True) + EPS)
        xh = x * rstd
        err = xh * fw - tgt
        dout = err * (1.0 / D)
        dxh = dout * fw
        dx = rstd * (dxh - xh * jnp.mean(dxh * xh, axis=-1, keepdims=True))
        sq = jnp.sum(jnp.sum(err * err, axis=1, keepdims=True), axis=0, keepdims=True)
        return (dx, dx * g, sq, jnp.sum(dout * xh, axis=0, keepdims=True), jnp.sum(dx * y, axis=0, keepdims=True))
    return _rowwise("loss_head", fn, [x, tgt, y], [fw, g], [(D, F32), (D, BF16)], [(1, 1), (1, D), (1, D)], tr=1024)


def _ffn_fwd(tag, h, wg, wu, wd, x, g2, next_norm=None):
    def act(a, b):
        return a, b, a * _sigmoid(a) * b
    a, b, f = _mm(f"ffn_up_{tag}", [h, h], [wg, wu], "nt", [BF16, BF16, BF16], epi=act, groups=[0, 1], tn_cap=FFN)

    if next_norm is None:
        def res(y, x, g):
            return y, x + g * y
        y, xo = _mm(f"ffn_down_{tag}", [f], [wd], "nn", [F32, F32], epi=res, extras=[x], vecs=[g2])
        return a, b, f, y, xo, None

    def res_norm(y, x, g, nw, sc, sh):
        xo = x + g * y
        return y, xo, _norm_mod(xo, nw, sc, sh)
    assert wd.shape[1] == D
    y, xo, h_next = _mm(f"ffn_down_{tag}", [f], [wd], "nn", [F32, F32, BF16], epi=res_norm, extras=[x],
                        vecs=[g2, *next_norm], whole_rows=True)
    return a, b, f, y, xo, h_next


def _ffn_bwd(tag, dy, h, a, b, f, wg, wu, wd):
    def act_bwd(df, a, b):
        a, b = a.astype(F32), b.astype(F32)
        sg = _sigmoid(a)
        return df * b * (sg * (1.0 + a * (1.0 - sg))), df * (a * sg)
    da, db = _mm(f"ffn_dact_{tag}", [dy], [wd], "nt", [BF16, BF16], epi=act_bwd, extras=[a, b], tn_cap=FFN)
    dwd = _mm_tn(f"ffn_dwd_{tag}", f, dy)
    dwg = _mm_tn(f"ffn_dwg_{tag}", da, h)
    dwu = _mm_tn(f"ffn_dwu_{tag}", db, h)
    dh = _mm(f"ffn_dh_{tag}", [da, db], [wg, wu], "nn", [F32])[0]
    return dh, dwg, dwu, dwd


def _conv_fwd(xr, w, b, tb=512):
    S, C = xr.shape
    tb = min(tb, S)

    def body(x_ref, halo_ref, w_ref, b_ref, pre_ref, out_ref):
        i = pl.program_id(0)
        halo = jnp.where(i > 0, halo_ref[...], 0.0)
        xe = jnp.concatenate([halo, x_ref[...]], axis=0)
        pre = w_ref[3:4, :] * x_ref[...] + b_ref[...]
        for j in (1, 2, 3):
            pre = pre + w_ref[3 - j:4 - j, :] * pltpu.roll(xe, j, axis=0)[8:, :]
        pre_ref[...] = pre
        out_ref[...] = pre * _sigmoid(pre)

    return pl.pallas_call(
        body, name="conv_fwd", grid=(S // tb,),
        in_specs=[pl.BlockSpec((tb, C), lambda i: (i, 0)),
                  pl.BlockSpec((8, C), lambda i: (jnp.maximum(i * (tb // 8) - 1, 0), 0)),
                  pl.BlockSpec((4, C), lambda i: (0, 0)), pl.BlockSpec((1, C), lambda i: (0, 0))],
        out_specs=[pl.BlockSpec((tb, C), lambda i: (i, 0))] * 2,
        out_shape=[jax.ShapeDtypeStruct((S, C), F32)] * 2,
        compiler_params=_params(("parallel",)))(xr, xr, w, b)


def _conv_bwd(dxc, pre, xr, w, tb=512):
    S, C = xr.shape
    tb = min(tb, S)
    nblk = S // tb

    def body(d_ref, p_ref, dn_ref, pn_ref, x_ref, w_ref, dx_ref, dw_ref, db_ref):
        i = pl.program_id(0)

        @pl.when(i == 0)
        def _():
            dw_ref[...] = jnp.zeros_like(dw_ref)
            db_ref[...] = jnp.zeros_like(db_ref)

        dpre = d_ref[...] * _silu_grad(p_ref[...])
        dnext = jnp.where(i < nblk - 1, dn_ref[...] * _silu_grad(pn_ref[...]), 0.0)
        pe = jnp.concatenate([dpre, dnext], axis=0)
        xx = x_ref[...]
        dx = w_ref[3:4, :] * dpre
        dw_ref[3:4, :] += jnp.sum(dpre * xx, axis=0, keepdims=True)
        for j in (1, 2, 3):
            ahead = pltpu.roll(pe, tb + 8 - j, axis=0)[:tb, :]
            dx = dx + w_ref[3 - j:4 - j, :] * ahead
            dw_ref[3 - j:4 - j, :] += jnp.sum(ahead * xx, axis=0, keepdims=True)
        dx_ref[...] = dx.astype(dx_ref.dtype)
        db_ref[...] += jnp.sum(dpre, axis=0, keepdims=True)

    blk = pl.BlockSpec((tb, C), lambda i: (i, 0))
    nxt = pl.BlockSpec((8, C), lambda i: (jnp.minimum((i + 1) * (tb // 8), S // 8 - 1), 0))
    return pl.pallas_call(
        body, name="conv_bwd", grid=(nblk,),
        in_specs=[blk, blk, nxt, nxt, blk, pl.BlockSpec((4, C), lambda i: (0, 0))],
        out_specs=[blk, pl.BlockSpec((4, C), lambda i: (0, 0)), pl.BlockSpec((1, C), lambda i: (0, 0))],
        out_shape=[jax.ShapeDtypeStruct((S, C), BF16), jax.ShapeDtypeStruct((4, C), F32),
                   jax.ShapeDtypeStruct((1, C), F32)],
        compiler_params=_params(("arbitrary",)))(dxc, pre, dxc, pre, xr, w)


def _iota(shape, dim):
    return lax.broadcasted_iota(jnp.int32, shape, dim)


def _colsel(m, lane, h):
    return jnp.sum(jnp.where(lane == h, m, 0.0), axis=1, keepdims=True)


def _cumsum_rows(v):
    r = _iota(v.shape, 0)
    k = 1
    while k < v.shape[0]:
        v = v + jnp.where(r >= k, pltpu.roll(v, k, axis=0), 0.0)
        k *= 2
    return v


def _suffix_sum_rows(v):
    n = v.shape[0]
    r = _iota(v.shape, 0)
    k = 1
    while k < n:
        v = v + jnp.where(r < n - k, pltpu.roll(v, n - k, axis=0), 0.0)
        k *= 2
    return v


def _ssd_fwd(xc, dtr, z, dtb, alog, dskl, nw):
    S = xc.shape[0]
    nc = S // L

    def body(xc_ref, dtr_ref, z_ref, dtb_ref, alog_ref, dsk_ref, nw_ref, ya_ref, y_ref, prev_ref,
             st_ref, cum_ref, cumT_ref):
        i = pl.program_id(0)

        @pl.when(i == 0)
        def _():
            st_ref[...] = jnp.zeros_like(st_ref)

        lane = _iota((L, 128), 1)
        lane1 = _iota((1, 128), 1)
        lo = lane < 64
        lo1 = lane1 < 64
        tril = _iota((L, L), 0) >= _iota((L, L), 1)
        dt = _softplus(dtr_ref[...] + dtb_ref[...])
        a_neg = -jnp.exp(alog_ref[...])
        cum = _cumsum_rows(dt * a_neg)
        cum_ref[...] = cum
        cumT_ref[...] = cum.T
        last_all = cum_ref[L - 1:L, :]
        prev_t = st_ref[...]
        prev_ref[0] = prev_t
        for g in range(2):
            bg = xc_ref[:, 1024 + g * 128:1152 + g * 128]
            cg = xc_ref[:, 1280 + g * 128:1408 + g * 128]
            gmat = _dot(cg, bg, NT)
            yoff = _dot(cg, prev_t[:, g * 512:(g + 1) * 512])
            bg_t = bg.T
            for jp in range(4):
                j = g * 4 + jp
                sl = slice(j * 128, (j + 1) * 128)
                xp = xc_ref[:, sl]
                cc = [_colsel(cum, lane, 2 * j), _colsel(cum, lane, 2 * j + 1)]
                cum_l = jnp.where(lo, cc[0], cc[1])
                dt_l = jnp.where(lo, _colsel(dt, lane, 2 * j), _colsel(dt, lane, 2 * j + 1))
                last_l = jnp.where(lo1, _colsel(last_all, lane1, 2 * j), _colsel(last_all, lane1, 2 * j + 1))
                xd = xp * dt_l
                ys = []
                for hh in range(2):
                    seg = cc[hh] - cumT_ref[2 * j + hh:2 * j + hh + 1, :]
                    dm = jnp.where(tril, jnp.exp(seg), 0.0)
                    ys.append(_dot(gmat * dm, xd))
                y_ref[:, sl] = (jnp.where(lo, ys[0], ys[1]) + jnp.exp(cum_l) * yoff[:, jp * 128:(jp + 1) * 128]
                                + dsk_ref[:, sl] * xp)
                st_ref[:, sl] = prev_t[:, sl] * jnp.exp(last_l) + _dot(bg_t, xd * jnp.exp(last_l - cum_l))
        for g in range(2):
            sl = slice(g * 512, (g + 1) * 512)
            zz = z_ref[:, sl]
            yg = y_ref[:, sl] * (zz * _sigmoid(zz))
            rstd = lax.rsqrt(jnp.mean(yg * yg, axis=-1, keepdims=True) + EPS)
            ya_ref[:, sl] = (yg * rstd * nw_ref[:, sl]).astype(ya_ref.dtype)

    blk = lambda c: pl.BlockSpec((L, c), lambda i: (i, 0))
    vec = lambda c: pl.BlockSpec((1, c), lambda i: (0, 0))
    return pl.pallas_call(
        body, name="ssd_fwd", grid=(nc,),
        in_specs=[blk(1536), blk(128), blk(1024), vec(128), vec(128), vec(1024), vec(1024)],
        out_specs=[blk(1024), blk(1024), pl.BlockSpec((1, NSTATE, 1024), lambda i: (i, 0, 0))],
        out_shape=[jax.ShapeDtypeStruct((S, 1024), BF16), jax.ShapeDtypeStruct((S, 1024), F32),
                   jax.ShapeDtypeStruct((nc, NSTATE, 1024), F32)],
        scratch_shapes=[pltpu.VMEM((NSTATE, 1024), F32), pltpu.VMEM((L, 128), F32), pltpu.VMEM((L, 128), F32)],
        compiler_params=_params(("arbitrary",)))(xc, dtr, z, dtb, alog, dskl, nw)


def _ssd_bwd(dya, y, z, xc, dtr, prev, dtb, alog, dskl, nw):
    S = xc.shape[0]
    nc = S // L

    def body(dya_ref, y_ref, z_ref, xc_ref, dtr_ref, prev_ref, dtb_ref, alog_ref, dsk_ref, nw_ref,
             dz_ref, dxc_ref, ddtr_ref, dnw_ref, ddsk_ref, dalog_ref, ddtb_ref,
             dst_ref, cum_ref, cumT_ref, dy_ref, dskacc_ref):
        i = pl.program_id(0)

        @pl.when(i == 0)
        def _():
            dst_ref[...] = jnp.zeros_like(dst_ref)
            dskacc_ref[...] = jnp.zeros_like(dskacc_ref)
            dnw_ref[...] = jnp.zeros_like(dnw_ref)
            dalog_ref[...] = jnp.zeros_like(dalog_ref)
            ddtb_ref[...] = jnp.zeros_like(ddtb_ref)

        lane = _iota((L, 128), 1)
        lane1 = _iota((1, 128), 1)
        lo = lane < 64
        lo1 = lane1 < 64
        r2, c2 = _iota((L, L), 0), _iota((L, L), 1)
        tril = r2 >= c2
        triu = r2 <= c2
        is_last = _iota((L, 1), 0) == L - 1

        for g in range(2):
            sl = slice(g * 512, (g + 1) * 512)
            zz = z_ref[:, sl]
            sg = _sigmoid(zz)
            zg = zz * sg
            yv = y_ref[:, sl]
            yg = yv * zg
            rstd = lax.rsqrt(jnp.mean(yg * yg, axis=-1, keepdims=True) + EPS)
            xh = yg * rstd
            d_out = dya_ref[:, sl]
            dnw_ref[:, sl] += jnp.sum(d_out * xh, axis=0, keepdims=True)
            dyn = d_out * nw_ref[:, sl]
            dyg = rstd * (dyn - xh * jnp.mean(dyn * xh, axis=-1, keepdims=True))
            dy_ref[:, sl] = dyg * zg
            dz_ref[:, sl] = (dyg * yv * (sg * (1.0 + zz * (1.0 - sg)))).astype(dz_ref.dtype)

        dtin = dtr_ref[...] + dtb_ref[...]
        dt = _softplus(dtin)
        a_neg = -jnp.exp(alog_ref[...])
        cum = _cumsum_rows(dt * a_neg)
        cum_ref[...] = cum
        cumT_ref[...] = cum.T
        last_all = cum_ref[L - 1:L, :]
        prev_t = prev_ref[0]
        dn_t = dst_ref[...]
        dcum = jnp.zeros((L, 128), F32)
        ddt = jnp.zeros((L, 128), F32)
        for g in range(2):
            gsl = slice(g * 512, (g + 1) * 512)
            bg = xc_ref[:, 1024 + g * 128:1152 + g * 128]
            cg = xc_ref[:, 1280 + g * 128:1408 + g * 128]
            gmat = _dot(cg, bg, NT)
            gmat_t = _dot(bg, cg, NT)
            pg = prev_t[:, gsl]
            zmat = _dot(cg, pg)
            dgm = jnp.zeros((L, L), F32)
            dgm_t = jnp.zeros((L, L), F32)
            db_acc = jnp.zeros((L, NSTATE), F32)
            dz_parts, cd_parts = [], []
            for jp in range(4):
                j = g * 4 + jp
                sl = slice(j * 128, (j + 1) * 128)
                xp = xc_ref[:, sl]
                dyp = dy_ref[:, sl]
                cc = [_colsel(cum, lane, 2 * j), _colsel(cum, lane, 2 * j + 1)]
                lc = [_colsel(last_all, lane1, 2 * j), _colsel(last_all, lane1, 2 * j + 1)]
                cum_l = jnp.where(lo, cc[0], cc[1])
                dt_l = jnp.where(lo, _colsel(dt, lane, 2 * j), _colsel(dt, lane, 2 * j + 1))
                last_l = jnp.where(lo1, lc[0], lc[1])
                e_l = jnp.exp(cum_l)
                dte_l = jnp.exp(last_l - cum_l)
                cd_l = jnp.exp(last_l)
                cd_parts.append(cd_l)
                xd = xp * dt_l
                xdb = xd.astype(_MXU)
                dskacc_ref[:, sl] += jnp.sum(dyp * xp, axis=0, keepdims=True)
                dxp = dsk_ref[:, sl] * dyp
                t = dyp * (e_l * zmat[:, jp * 128:(jp + 1) * 128])
                dcc = [jnp.sum(jnp.where(lo, t, 0.0), axis=1, keepdims=True),
                       jnp.sum(jnp.where(lo, 0.0, t), axis=1, keepdims=True)]
                dz_parts.append(e_l * dyp)
                dnp_ = dn_t[:, sl]
                t2 = jnp.sum(dnp_ * prev_t[:, sl], axis=0, keepdims=True)
                dcd = [jnp.sum(jnp.where(lo1, t2, 0.0), axis=1, keepdims=True),
                       jnp.sum(jnp.where(lo1, 0.0, t2), axis=1, keepdims=True)]
                wm = _dot(bg, dnp_)
                dxd = wm * dte_l
                t3 = wm * xd
                ddte = [jnp.sum(jnp.where(lo, t3, 0.0), axis=1, keepdims=True),
                        jnp.sum(jnp.where(lo, 0.0, t3), axis=1, keepdims=True)]
                db_acc = db_acc + _dot(xd * dte_l, dnp_, NT)
                for hh in range(2):
                    h = 2 * j + hh
                    half = lo if hh == 0 else jnp.logical_not(lo)
                    row = cumT_ref[h:h + 1, :]
                    dm = jnp.where(tril, jnp.exp(cc[hh] - row), 0.0)
                    dm_t = jnp.where(triu, jnp.exp(row - cc[hh]), 0.0)
                    dym = jnp.where(half, dyp, 0.0).astype(_MXU)
                    u = _dot(dym, xdb, NT) * dm
                    u_t = _dot(xdb, dym, NT) * dm_t
                    dxd = dxd + _dot(gmat_t * dm_t, dym)
                    dcc[hh] = dcc[hh] + jnp.sum(u * gmat, axis=1, keepdims=True) - jnp.sum(u_t * gmat_t, axis=1, keepdims=True)
                    dgm = dgm + u
                    dgm_t = dgm_t + u_t
                    dte_c = jnp.exp(lc[hh] - cc[hh])
                    dcc[hh] = dcc[hh] - ddte[hh] * dte_c
                    endc = dcd[hh] * jnp.exp(lc[hh]) + jnp.sum(ddte[hh] * dte_c, axis=0, keepdims=True)
                    dcc[hh] = dcc[hh] + jnp.where(is_last, endc, 0.0)
                    dcum = jnp.where(lane == h, dcc[hh], dcum)
                dxc_ref[:, sl] = dxp + dxd * dt_l
                t4 = dxd * xp
                ddt = jnp.where(lane == 2 * j, jnp.sum(jnp.where(lo, t4, 0.0), axis=1, keepdims=True), ddt)
                ddt = jnp.where(lane == 2 * j + 1, jnp.sum(jnp.where(lo, 0.0, t4), axis=1, keepdims=True), ddt)
            dzg = jnp.concatenate(dz_parts, axis=1)
            dst_ref[:, gsl] = dn_t[:, gsl] * jnp.concatenate(cd_parts, axis=1) + _dot(cg.T, dzg)
            dxc_ref[:, 1280 + g * 128:1408 + g * 128] = _dot(dgm, bg) + _dot(dzg, pg, NT)
            dxc_ref[:, 1024 + g * 128:1152 + g * 128] = _dot(dgm_t, cg) + db_acc
        dla = _suffix_sum_rows(dcum)
        ddt = ddt + dla * a_neg
        dalog_ref[...] += jnp.sum(dla * dt, axis=0, keepdims=True) * a_neg
        ddtr = jnp.where(lane < 16, ddt * _sigmoid(dtin), 0.0)
        ddtr_ref[...] = ddtr.astype(ddtr_ref.dtype)
        ddtb_ref[...] += jnp.sum(ddtr, axis=0, keepdims=True)

        @pl.when(i == nc - 1)
        def _():
            seg = (_iota((1024, 128), 0) // 64 == _iota((1024, 128), 1)).astype(F32)
            acc8 = jnp.broadcast_to(dskacc_ref[...], (8, 1024))
            ddsk_ref[...] = lax.dot_general(acc8, seg, NN, precision=lax.Precision.HIGHEST,
                                            preferred_element_type=F32)

    rev = lambda c: pl.BlockSpec((L, c), lambda i: (nc - 1 - i, 0))
    vec = lambda c: pl.BlockSpec((1, c), lambda i: (0, 0))
    return pl.pallas_call(
        body, name="ssd_bwd", grid=(nc,),
        in_specs=[rev(1024), rev(1024), rev(1024), rev(1536), rev(128),
                  pl.BlockSpec((1, NSTATE, 1024), lambda i: (nc - 1 - i, 0, 0)),
                  vec(128), vec(128), vec(1024), vec(1024)],
        out_specs=[rev(1024), rev(1536), rev(128), vec(1024), pl.BlockSpec((8, 128), lambda i: (0, 0)),
                   vec(128), vec(128)],
        out_shape=[jax.ShapeDtypeStruct((S, 1024), BF16), jax.ShapeDtypeStruct((S, 1536), F32),
                   jax.ShapeDtypeStruct((S, 128), BF16), jax.ShapeDtypeStruct((1, 1024), F32),
                   jax.ShapeDtypeStruct((8, 128), F32), jax.ShapeDtypeStruct((1, 128), F32),
                   jax.ShapeDtypeStruct((1, 128), F32)],
        scratch_shapes=[pltpu.VMEM((NSTATE, 1024), F32), pltpu.VMEM((L, 128), F32), pltpu.VMEM((L, 128), F32),
                        pltpu.VMEM((L, 1024), F32), pltpu.VMEM((1, 1024), F32)],
        compiler_params=_params(("arbitrary",)))(dya, y, z, xc, dtr, prev, dtb, alog, dskl, nw)


def _layer_norm_parts(vg):
    mu = jnp.mean(vg, axis=-1, keepdims=True)
    vc = vg - mu
    rstd = lax.rsqrt(jnp.mean(vc * vc, axis=-1, keepdims=True) + EPS)
    return vc * rstd, rstd


def _gmlp_fwd(u, v, lnw, lnb, ws, bse, tb=512):
    S = u.shape[0]
    tb = min(tb, S)

    def body(u_ref, v_ref, lnw_ref, lnb_ref, ws_ref, bse_ref, o_ref, vn_ref):
        tril = _iota((L, L), 0) >= _iota((L, L), 1)
        xh, _ = _layer_norm_parts(_gelu(v_ref[...]))
        vn_ref[...] = xh * lnw_ref[...] + lnb_ref[...]
        for g in range(8):
            w = jnp.where(tril, ws_ref[g], 0.0)
            gs = slice(g * 128, (g + 1) * 128)
            for ch in range(tb // L):
                rs = slice(ch * L, (ch + 1) * L)
                sv = _dot(w, vn_ref[rs, gs]) + bse_ref[g]
                o_ref[rs, gs] = (_gelu(u_ref[rs, gs]) * sv).astype(o_ref.dtype)

    blk = pl.BlockSpec((tb, 1024), lambda i: (i, 0))
    vec = pl.BlockSpec((1, 1024), lambda i: (0, 0))
    cube = pl.BlockSpec((8, L, 128), lambda i: (0, 0, 0))
    return pl.pallas_call(
        body, name="gmlp_fwd", grid=(S // tb,), in_specs=[blk, blk, vec, vec, cube, cube], out_specs=blk,
        out_shape=jax.ShapeDtypeStruct((S, 1024), BF16), scratch_shapes=[pltpu.VMEM((tb, 1024), F32)],
        compiler_params=_params(("parallel",)))(u, v, lnw, lnb, ws, bse)


def _gmlp_bwd(dyb, u, v, lnw, lnb, ws, bse, tb=512):
    S = u.shape[0]
    tb = min(tb, S)

    def body(d_ref, u_ref, v_ref, lnw_ref, lnb_ref, ws_ref, bse_ref,
             du_ref, dv_ref, dws_ref, dbse_ref, dlnw_ref, dlnb_ref, vn_ref, dvn_ref):
        @pl.when(pl.program_id(0) == 0)
        def _():
            dws_ref[...] = jnp.zeros_like(dws_ref)
            dbse_ref[...] = jnp.zeros_like(dbse_ref)
            dlnw_ref[...] = jnp.zeros_like(dlnw_ref)
            dlnb_ref[...] = jnp.zeros_like(dlnb_ref)

        tril = _iota((L, L), 0) >= _iota((L, L), 1)
        vv = v_ref[...]
        xh, rstd = _layer_norm_parts(_gelu(vv))
        vn_ref[...] = xh * lnw_ref[...] + lnb_ref[...]
        for g in range(8):
            w = jnp.where(tril, ws_ref[g], 0.0)
            w_t = w.T
            gs = slice(g * 128, (g + 1) * 128)
            dw = jnp.zeros((L, L), F32)
            dbs = jnp.zeros((L, 128), F32)
            for ch in range(tb // L):
                rs = slice(ch * L, (ch + 1) * L)
                vn = vn_ref[rs, gs]
                sv = _dot(w, vn) + bse_ref[g]
                uu = u_ref[rs, gs]
                dd = d_ref[rs, gs]
                du_ref[rs, gs] = (dd * sv * _gelu_grad(uu)).astype(du_ref.dtype)
                dsv = dd * _gelu(uu)
                dw = dw + _dot(dsv, vn, NT)
                dbs = dbs + dsv
                dvn_ref[rs, gs] = _dot(w_t, dsv)
            dws_ref[g] += jnp.where(tril, dw, 0.0)
            dbse_ref[g] += dbs
        dvn = dvn_ref[...]
        dlnw_ref[...] += jnp.sum(dvn * xh, axis=0, keepdims=True)
        dlnb_ref[...] += jnp.sum(dvn, axis=0, keepdims=True)
        dxh = dvn * lnw_ref[...]
        dvg = rstd * (dxh - jnp.mean(dxh, axis=-1, keepdims=True) - xh * jnp.mean(dxh * xh, axis=-1, keepdims=True))
        dv_ref[...] = (dvg * _gelu_grad(vv)).astype(dv_ref.dtype)

    blk = pl.BlockSpec((tb, 1024), lambda i: (i, 0))
    vec = pl.BlockSpec((1, 1024), lambda i: (0, 0))
    cube = pl.BlockSpec((8, L, 128), lambda i: (0, 0, 0))
    return pl.pallas_call(
        body, name="gmlp_bwd", grid=(S // tb,), in_specs=[blk, blk, blk, vec, vec, cube, cube],
        out_specs=[blk, blk, cube, cube, vec, vec],
        out_shape=[jax.ShapeDtypeStruct((S, 1024), BF16), jax.ShapeDtypeStruct((S, 1024), BF16),
                   jax.ShapeDtypeStruct((8, L, 128), F32), jax.ShapeDtypeStruct((8, L, 128), F32),
                   jax.ShapeDtypeStruct((1, 1024), F32), jax.ShapeDtypeStruct((1, 1024), F32)],
        scratch_shapes=[pltpu.VMEM((tb, 1024), F32), pltpu.VMEM((tb, 1024), F32)],
        compiler_params=_params(("arbitrary",)))(dyb, u, v, lnw, lnb, ws, bse)


def _lane_sum(name, a):
    def body(a_ref, o_ref):
        o_ref[...] = jnp.sum(a_ref[...], axis=1, keepdims=True)
    return pl.pallas_call(body, name=name, out_shape=jax.ShapeDtypeStruct((a.shape[0], 1), F32))(a)


def _bucket_onehot_t():
    qi = np.arange(L)[:, None]
    sj = np.arange(2 * L)[None, :]
    dist = np.maximum(qi + L - sj, 0)
    log_ratio = (np.log(np.maximum(dist, 1).astype(np.float32) / np.float32(16)) / np.float32(math.log(128 / 16)))
    large = 16 + (log_ratio.astype(np.float32) * np.float32(16)).astype(np.int32)
    bucket = np.where(dist < 16, dist, np.minimum(large, 31)).reshape(-1)
    return (np.arange(32)[:, None] == bucket[None, :]).astype(np.float32)


def _rel_bias(table_t, onehot_t):
    def body(t_ref, oh_ref, o_ref):
        o_ref[...] = lax.dot_general(t_ref[...], oh_ref[...], NN, precision=lax.Precision.HIGHEST,
                                     preferred_element_type=F32)
    return pl.pallas_call(body, name="rel_bias", out_shape=jax.ShapeDtypeStruct((16, L * 2 * L), F32),
                          compiler_params=_params())(table_t, onehot_t)


def _rel_bias_bwd(dbias, onehot_t):
    def body(d_ref, oh_ref, o_ref):
        o_ref[...] = lax.dot_general(d_ref[...], oh_ref[...], NT, precision=lax.Precision.HIGHEST,
                                     preferred_element_type=F32)
    return pl.pallas_call(body, name="rel_bias_bwd", out_shape=jax.ShapeDtypeStruct((16, 32), F32),
                          compiler_params=_params())(dbias, onehot_t)


def _band(kp, kc, lo):
    kk = jnp.concatenate([kp, kc], axis=0)
    kr = pltpu.roll(kk, 64, axis=1)
    return [jnp.where(lo, kk, kr), jnp.where(lo, kr, kk)]


def _attn_rows(ref, j, lo):
    parts = []
    for t in range(8):
        pair = ref[:, (4 * j + t // 2) * 128:(4 * j + t // 2 + 1) * 128]
        parts.append(jnp.where(lo if t % 2 == 0 else jnp.logical_not(lo), pair, 0.0))
    return jnp.concatenate(parts, axis=0)


def _attn_mask(i, rows):
    qi, sj = _iota((rows, 2 * L), 0) & (L - 1), _iota((rows, 2 * L), 1)
    rel = qi + L - sj
    return (rel >= 0) & (rel < L) & ((sj >= L) | (i > 0))


def _per_head_col(vals):
    return jnp.concatenate([jnp.broadcast_to(v, (L, 1)) for v in vals], axis=0)


SMEM = pl.BlockSpec(memory_space=pltpu.SMEM)


def _attn_fwd(qkv, bias, sinks):
    S = qkv.shape[0]
    nb = S // L
    scale = 64 ** -0.5

    def body(sink_ref, q_ref, kc_ref, vc_ref, kp_ref, vp_ref, bias_ref, o_ref, lse_ref):
        i = pl.program_id(0)
        lane = _iota((L, 128), 1)
        lo = lane < 64
        lo2 = _iota((2 * L, 128), 1) < 64
        mask = _attn_mask(i, L)
        kd = _band(kp_ref[...], kc_ref[...], lo2)
        vd = _band(vp_ref[...], vc_ref[...], lo2)
        lse = jnp.zeros((L, 128), F32)
        for pr in range(8):
            sl = slice(pr * 128, (pr + 1) * 128)
            qp = q_ref[:, sl]
            j = pr // 4
            outs = []
            for hh in range(2):
                h = 2 * pr + hh
                qm = jnp.where(lo if hh == 0 else jnp.logical_not(lo), qp, 0.0)
                lg = jnp.where(mask, _dot(qm, kd[j], NT) * scale + bias_ref[h], NEG_INF)
                s = sink_ref[h]
                m = jnp.maximum(jnp.max(lg, axis=1, keepdims=True), s)
                p = jnp.where(mask, jnp.exp(lg - m), 0.0)
                den = jnp.sum(p, axis=1, keepdims=True) + jnp.exp(s - m)
                outs.append(_dot(p * (1.0 / den), vd[j]))
                lse = jnp.where(lane == h, m + jnp.log(den), lse)
            o_ref[:, sl] = jnp.where(lo, outs[0], outs[1]).astype(o_ref.dtype)
        lse_ref[...] = lse

    prev = lambda col: pl.BlockSpec((L, 128), lambda i: (jnp.maximum(i - 1, 0), col))
    cur = lambda col: pl.BlockSpec((L, 128), lambda i: (i, col))
    return pl.pallas_call(
        body, name="attn_fwd", grid=(nb,),
        in_specs=[SMEM, pl.BlockSpec((L, 1024), lambda i: (i, 0)), cur(8), cur(9), prev(8), prev(9),
                  pl.BlockSpec((16, L, 2 * L), lambda i: (0, 0, 0))],
        out_specs=[pl.BlockSpec((L, 1024), lambda i: (i, 0)), pl.BlockSpec((L, 128), lambda i: (i, 0))],
        out_shape=[jax.ShapeDtypeStruct((S, 1024), BF16), jax.ShapeDtypeStruct((S, 128), F32)],
        compiler_params=_params(("parallel",)))(sinks, qkv, qkv, qkv, qkv, qkv, bias)


def _attn_bwd(qkv, d_o, lse, bias, sinks):
    S = qkv.shape[0]
    nb = S // L
    scale = 64 ** -0.5

    def body(sink_ref, q_ref, kc_ref, vc_ref, kp_ref, vp_ref, do_ref, lse_ref, bias_ref,
             dq_ref, dkv_ref, dbias_ref, dsink_ref, dbq_ref, dbkv_ref, carry_ref):
        i = pl.program_id(0)

        @pl.when(i == 0)
        def _():
            dbias_ref[...] = jnp.zeros_like(dbias_ref)
            dsink_ref[...] = jnp.zeros_like(dsink_ref)
            dbq_ref[...] = jnp.zeros_like(dbq_ref)
            dbkv_ref[...] = jnp.zeros_like(dbkv_ref)
            carry_ref[...] = jnp.zeros_like(carry_ref)

        @pl.when(i < nb)
        def _():
            lane = _iota((L, 128), 1)
            lane1 = _iota((1, 128), 1)
            lo = lane < 64
            lo2 = _iota((2 * L, 128), 1) < 64
            mask = _attn_mask(i, 8 * L)
            kd = _band(kp_ref[...], kc_ref[...], lo2)
            vd = _band(vp_ref[...], vc_ref[...], lo2)
            lse_all = lse_ref[...]
            dsink = jnp.zeros((1, 128), F32)
            tot_k, tot_v = [], []
            for j in range(2):
                q_all = _attn_rows(q_ref, j, lo).astype(_MXU)
                do_all = _attn_rows(do_ref, j, lo).astype(_MXU)
                lse_col = _per_head_col([_colsel(lse_all, lane, 8 * j + t) for t in range(8)])
                lg = _dot(q_all, kd[j], NT) * scale + bias_ref[8 * j:8 * j + 8].reshape(8 * L, 2 * L)
                p = jnp.where(mask, jnp.exp(jnp.where(mask, lg, NEG_INF) - lse_col), 0.0)
                dp = _dot(do_all, vd[j], NT)
                delta = jnp.sum(p * dp, axis=1, keepdims=True)
                ds = p * (dp - delta)
                dbias_ref[8 * j:8 * j + 8] += ds.reshape(8, L, 2 * L)
                s = _per_head_col([sink_ref[8 * j + t] for t in range(8)])
                sink_part = -jnp.exp(s - lse_col) * delta
                for t in range(8):
                    dsink = dsink + jnp.where(lane1 == 8 * j + t,
                                              jnp.sum(sink_part[t * L:(t + 1) * L], axis=0, keepdims=True), 0.0)
                dss = ds * scale
                dq_all = _dot(dss, kd[j])
                for t in range(0, 8, 2):
                    sl = slice((4 * j + t // 2) * 128, (4 * j + t // 2 + 1) * 128)
                    dq = jnp.where(lo, dq_all[t * L:(t + 1) * L], dq_all[(t + 1) * L:(t + 2) * L])
                    dq_ref[:, sl] = dq.astype(dq_ref.dtype)
                    dbq_ref[:, sl] += jnp.sum(dq, axis=0, keepdims=True)
                acc_k = _dot(dss, q_all, TN)
                acc_v = _dot(p, do_all, TN)
                tot_k.append(acc_k + pltpu.roll(acc_k, 64, axis=1))
                tot_v.append(acc_v + pltpu.roll(acc_v, 64, axis=1))
            dsink_ref[...] += dsink
            dkv = jnp.concatenate([jnp.where(lo2, tot_k[0], tot_k[1]), jnp.where(lo2, tot_v[0], tot_v[1])], axis=1)
            dbkv_ref[...] += jnp.sum(dkv, axis=0, keepdims=True)
            dkv_ref[...] = (carry_ref[...] + dkv[:L, :]).astype(dkv_ref.dtype)
            carry_ref[...] = dkv[L:, :]

        @pl.when(i == nb)
        def _():
            dkv_ref[...] = carry_ref[...].astype(dkv_ref.dtype)

    c = lambda i: jnp.minimum(i, nb - 1)
    prev = lambda col: pl.BlockSpec((L, 128), lambda i: (jnp.maximum(c(i) - 1, 0), col))
    cur = lambda col: pl.BlockSpec((L, 128), lambda i: (c(i), col))
    row = lambda w: pl.BlockSpec((L, w), lambda i: (c(i), 0))
    cube = pl.BlockSpec((16, L, 2 * L), lambda i: (0, 0, 0))
    vec = lambda w: pl.BlockSpec((1, w), lambda i: (0, 0))
    return pl.pallas_call(
        body, name="attn_bwd", grid=(nb + 1,),
        in_specs=[SMEM, row(1024), cur(8), cur(9), prev(8), prev(9), row(1024), row(128), cube],
        out_specs=[row(1024), pl.BlockSpec((L, 256), lambda i: (jnp.maximum(i - 1, 0), 0)), cube,
                   vec(128), vec(1024), vec(256)],
        out_shape=[jax.ShapeDtypeStruct((S, 1024), BF16), jax.ShapeDtypeStruct((S, 256), BF16),
                   jax.ShapeDtypeStruct((16, L, 2 * L), F32), jax.ShapeDtypeStruct((1, 128), F32),
                   jax.ShapeDtypeStruct((1, 1024), F32), jax.ShapeDtypeStruct((1, 256), F32)],
        scratch_shapes=[pltpu.VMEM((L, 256), F32)],
        compiler_params=_params(("arbitrary",)))(sinks, qkv, qkv, qkv, qkv, qkv, d_o, lse, bias)


def _pad_lanes(a, n=128):
    return jnp.pad(a, ((0, 0), (0, n - a.shape[1])))


def _local_step(x, tgt, mod, w_in, P, io):
    md = [[mod[l:l + 1, k * D:(k + 1) * D] for k in range(6)] for l in range(2)]
    G, g = {}, {}

    sh1, sc1, g1, sh2, sc2, g2 = md[0]
    nmw0, nfw0 = P["norm_mix_w"][0:1], P["norm_ffn_w"][0:1]
    h0 = _norm_mod_fwd("norm_mix_0", x, nmw0, sc1, sh1, after=io["start"])
    segs = {"z": w_in[0:1024], "xbc": w_in[1024:2560], "dt": jnp.pad(w_in[2560:2576], ((0, 112), (0, 0))),
            "u": w_in[2576:3600], "v": w_in[3600:4624]}
    proj = dict(zip(segs, _mm_shared_lhs("in_proj", h0, list(segs.values()))))
    conv_w, conv_b = P["conv_w"][0], P["conv_b"]
    pre, xc = _conv_fwd(proj["xbc"], conv_w, conv_b)
    dtb, alog = _pad_lanes(P["dt_bias"]), _pad_lanes(P["a_log"])
    dskl = jnp.repeat(P["d_skip"], 64, axis=1)
    ya, y_ssd, prev = _ssd_fwd(xc, proj["dt"], proj["z"], dtb, alog, dskl, P["ssm_norm_w"])
    ws = P["gmlp_ws"][0]
    bse = jnp.broadcast_to(P["gmlp_bs"][0][:, :, None], (8, L, 128))
    yb = _gmlp_fwd(proj["u"], proj["v"], P["gmlp_ln_w"], P["gmlp_ln_b"], ws, bse)
    W = dict(io["weights0"]((ya, yb)))
    w_oa, w_ob = W["out_w"][:1024], W["out_w"][1024:]

    def res(y, x, gate, nw, sc, sh):
        xo = x + gate * y
        return y, xo, _norm_mod(xo, nw, sc, sh)
    mix0, x1, h0f = _mm("out_proj_0", [ya, yb], [w_oa, w_ob], "nn", [F32, F32, BF16], epi=res, extras=[x],
                        vecs=[g1, nfw0, sc2, sh2], whole_rows=True)
    sh1b, sc1b, g1b, sh2b, sc2b, g2b = md[1]
    nmw1, nfw1 = P["norm_mix_w"][1:2], P["norm_ffn_w"][1:2]
    a0, b0, f0, y0, x2, h1 = _ffn_fwd("0", h0f, W["gate_wt0"], W["up_wt0"], W["down_w0"], x1, g2,
                                      next_norm=(nmw1, sc1b, sh1b))

    W.update(io["weights1"](x2))
    qkv = _mm("qkv_proj", [h1], [W["qkv_wt"]], "nt", [F32], epi=lambda acc, b: acc + b, vecs=[P["qkv_b"]])[0]
    onehot_t = jnp.asarray(_bucket_onehot_t())
    bias = _rel_bias(P["rel_table"].T, onehot_t).reshape(16, L, 2 * L)
    sinks = P["sinks"].reshape(16)
    att, lse = _attn_fwd(qkv, bias, sinks)

    def res_b(y, x, gate, b, nw, sc, sh):
        y = y + b
        xo = x + gate * y
        return y, xo, _norm_mod(xo, nw, sc, sh)
    mix1, x3, h1f = _mm("o_proj", [att], [W["o_w"]], "nn", [F32, F32, BF16], epi=res_b, extras=[x2],
                        vecs=[g1b, P["o_b"], nfw1, sc2b, sh2b], whole_rows=True)
    a1, b1, f1, y1, x4, _ = _ffn_fwd("1", h1f, W["gate_wt1"], W["up_wt1"], W["down_w1"], x3, g2b)

    dx, dy, sq, g["final_norm_w"], dg2b = _loss_head(x4, tgt, P["final_norm_w"], y1, g2b)

    dh, dwg1, dwu1, dwd1 = _ffn_bwd("1", dy, h1f, a1, b1, f1, W["gate_wt1"], W["up_wt1"], W["down_w1"])
    dx, dmix, dsh2b, dsc2b, dnfw1, dg1b, g["o_b"] = _norm_mod_bwd("norm_ffn_bwd_1", x3, dh, dx, nfw1, sc2b,
                                                                 gate=(mix1, g1b))
    G["o_w"] = _mm_tn("o_dw", att, dmix)
    d_att = _mm("o_dx", [dmix], [W["o_w"]], "nt", [F32])[0]
    dq, dkv, dbias, dsinks, dbq, dbkv = _attn_bwd(qkv, d_att, lse, bias, sinks)
    g["rel_table"] = _rel_bias_bwd(dbias.reshape(16, L * 2 * L), onehot_t).T
    g["sinks"] = dsinks[:, :16]
    g["qkv_b"] = jnp.concatenate([dbq, dbkv], axis=1)
    w_q, w_kv = W["qkv_wt"][:1024], W["qkv_wt"][1024:]
    G["qkv_wt"] = jnp.concatenate(_mm_tn_shared_rhs("qkv_dw", [dq, dkv], h1), axis=0)
    dh = _mm("qkv_dx", [dq, dkv], [w_q, w_kv], "nn", [F32])[0]
    behind = io["grads1"]({"qkv_wt": G.pop("qkv_wt"), "o_w": G.pop("o_w"), "gate_wt1": dwg1, "up_wt1": dwu1,
                           "down_w1": dwd1})
    dx, dy, dsh1b, dsc1b, dnmw1, dg2, _ = _norm_mod_bwd("norm_mix_bwd_1", x2, dh, dx, nmw1, sc1b, gate=(y0, g2),
                                                        after=behind)

    dh, dwg0, dwu0, dwd0 = _ffn_bwd("0", dy, h0f, a0, b0, f0, W["gate_wt0"], W["up_wt0"], W["down_w0"])
    behind = io["grads_ffn0"]({"gate_wt0": dwg0, "up_wt0": dwu0, "down_w0": dwd0})
    dx, dmix, dsh2, dsc2, dnfw0, dg1, _ = _norm_mod_bwd("norm_ffn_bwd_0", x1, dh, dx, nfw0, sc2, gate=(mix0, g1),
                                                        after=behind)
    G["out_w"] = jnp.concatenate(_mm_tn_shared_rhs("out_dw", [ya, yb], dmix), axis=0)
    dya, dyb = _mm_shared_lhs("out_dx", dmix, [w_oa, w_ob])
    du, dv, dws, dbse, g["gmlp_ln_w"], g["gmlp_ln_b"] = _gmlp_bwd(dyb, proj["u"], proj["v"], P["gmlp_ln_w"],
                                                                 P["gmlp_ln_b"], ws, bse)
    g["gmlp_ws"] = dws[None]
    g["gmlp_bs"] = _lane_sum("gmlp_dbs", dbse.reshape(8 * L, 128)).reshape(1, 8, L)
    dz, dxc, ddt, g["ssm_norm_w"], ddsk, dalog, ddtb = _ssd_bwd(dya, y_ssd, proj["z"], xc, proj["dt"], prev,
                                                                dtb, alog, dskl, P["ssm_norm_w"])
    g["d_skip"], g["a_log"], g["dt_bias"] = ddsk[0:1, :16], dalog[:, :16], ddtb[:, :16]
    dxr, dconv_w, g["conv_b"] = _conv_bwd(dxc, pre, proj["xbc"], conv_w)
    g["conv_w"] = dconv_w[None]
    dsegs = {"z": dz, "xbc": dxr, "dt": ddt, "u": du, "v": dv}
    dws_in = dict(zip(dsegs, _mm_tn_shared_rhs("in_dw", list(dsegs.values()), h0)))
    G["in_wt"] = jnp.concatenate([dws_in["z"], dws_in["xbc"], dws_in["dt"][:16], dws_in["u"], dws_in["v"]], axis=0)
    keys = ["z", "xbc", "dt", "u", "v"]
    dh = _mm("in_dx", [dsegs[k] for k in keys], [segs[k] for k in keys], "nn", [F32])[0]
    dx, dsh1, dsc1, dnmw0 = _norm_mod_bwd("norm_mix_bwd_0", x, dh, dx, nmw0, sc1)

    g["norm_mix_w"] = jnp.concatenate([dnmw0, dnmw1], axis=0)
    g["norm_ffn_w"] = jnp.concatenate([dnfw0, dnfw1], axis=0)
    dmod = jnp.concatenate([jnp.concatenate([dsh1, dsc1, dg1, dsh2, dsc2, dg2], axis=1),
                            jnp.concatenate([dsh1b, dsc1b, dg1b, dsh2b, dsc2b, dg2b], axis=1)], axis=0)
    return sq, dx, dmod, G, g


def _ada_fwd(c_all, ada_w, ada_b):
    n = ada_w.shape[2]
    tn = _col_tile(n, 512)

    def body(c_ref, w_ref, b_ref, o_ref):
        cc = c_ref[...]
        o_ref[...] = lax.dot_general(cc * _sigmoid(cc), w_ref[...], NN, precision=lax.Precision.HIGHEST,
                                     preferred_element_type=F32) + b_ref[...]

    return pl.pallas_call(
        body, name="ada_fwd", grid=(2, n // tn),
        in_specs=[pl.BlockSpec((8, D), lambda l, j: (0, 0)), pl.BlockSpec((None, D, tn), lambda l, j: (l, 0, j)),
                  pl.BlockSpec((None, 1, tn), lambda l, j: (l, 0, j))],
        out_specs=pl.BlockSpec((None, 8, tn), lambda l, j: (l, 0, j)),
        out_shape=jax.ShapeDtypeStruct((2, 8, n), F32), compiler_params=_params(("parallel", "parallel")))(
            c_all, ada_w, ada_b)


def _ada_bwd(c_all, dmod_cols, dmod_all):
    n = dmod_cols.shape[2]
    tn = _col_tile(n, 512)

    def body(c_ref, d_ref, o_ref):
        cc = c_ref[...]
        o_ref[...] = lax.dot_general(cc * _sigmoid(cc), d_ref[...], TN, precision=lax.Precision.HIGHEST,
                                     preferred_element_type=F32)

    dw = pl.pallas_call(
        body, name="ada_dw", grid=(2, n // tn),
        in_specs=[pl.BlockSpec((8, D), lambda l, j: (0, 0)), pl.BlockSpec((None, 8, tn), lambda l, j: (l, 0, j))],
        out_specs=pl.BlockSpec((None, D, tn), lambda l, j: (l, 0, j)),
        out_shape=jax.ShapeDtypeStruct((2, D, n), F32), compiler_params=_params(("parallel", "parallel")))(
            c_all, dmod_cols)

    def sum_body(d_ref, o_ref):
        o_ref[...] = jnp.sum(d_ref[...], axis=0, keepdims=True)

    db = pl.pallas_call(
        sum_body, name="ada_db", grid=(2,),
        in_specs=[pl.BlockSpec((None, 8, 6 * D), lambda l: (l, 0, 0))],
        out_specs=pl.BlockSpec((None, 1, 6 * D), lambda l: (l, 0, 0)),
        out_shape=jax.ShapeDtypeStruct((2, 1, 6 * D), F32), compiler_params=_params(("parallel",)))(dmod_all)
    return dw, db


def _row_tile(rows, cap=512, mult=8):
    best = rows
    for t in range(mult, min(rows, cap) + 1, mult):
        if rows % t == 0:
            best = t
    return best


def _adamw(name, w, g, m, v):
    def fn(w, g, m, v):
        m = ADAM_B1 * m + (1.0 - ADAM_B1) * g
        v = ADAM_B2 * v + (1.0 - ADAM_B2) * (g * g)
        m_hat = m / (1.0 - ADAM_B1 ** ADAM_STEP)
        v_hat = v / (1.0 - ADAM_B2 ** ADAM_STEP)
        return -ADAM_LR * (m_hat / (jnp.sqrt(v_hat) + ADAM_EPS) + ADAM_WD * w), m, v
    cols = w.shape[1]
    return _rowwise(name, fn, [w, g, m, v], [], [(cols, F32)] * 3, tr=_row_tile(w.shape[0]))


def _place():
    return lax.axis_index("x"), lax.axis_index("y"), lax.axis_index("c")


VMEM_SPEC = pl.BlockSpec(memory_space=pltpu.VMEM)


def _allreduce_small(name, buf, after=None):
    rows = buf.shape[0]
    deps = [] if after is None else [after]

    def body(x_ref, *rest):
        o_ref, stage, send_sems, recv_sems = rest[len(deps):]
        x, y, c = _place()
        me = 4 * x + 2 * y + c
        stage[me] = x_ref[...]
        copies = []
        for k in range(1, 8):
            peer = (1 - x if k & 4 else x, 1 - y if k & 2 else y, 1 - c if k & 1 else c)
            cp = pltpu.make_async_remote_copy(src_ref=x_ref, dst_ref=stage.at[me], send_sem=send_sems.at[k - 1],
                                              recv_sem=recv_sems.at[k - 1], device_id=peer, device_id_type=MESH)
            cp.start()
            copies.append(cp)
        for cp in copies:
            cp.wait()
        acc = stage[0]
        for d in range(1, 8):
            acc = acc + stage[d]
        o_ref[...] = acc

    return pl.pallas_call(
        body, name=name, in_specs=[VMEM_SPEC] + [ANY for _ in deps], out_specs=VMEM_SPEC,
        out_shape=jax.ShapeDtypeStruct((rows, 128), F32),
        scratch_shapes=[pltpu.VMEM((8, rows, 128), F32), pltpu.SemaphoreType.DMA((7,)), pltpu.SemaphoreType.DMA((7,))],
        compiler_params=pltpu.CompilerParams(vmem_limit_bytes=_VMEM_LIMIT))(buf, *deps)


def _sum_slots(name, own, land):
    def body(own_ref, land_ref, o_ref):
        x, y, c = _place()
        me = 4 * x + 2 * y + c
        acc = None
        for d in range(8):
            v = jnp.where(me == d, own_ref[...], land_ref[d])
            acc = v if acc is None else acc + v
        o_ref[...] = acc

    return pl.pallas_call(body, name=name, in_specs=[VMEM_SPEC, VMEM_SPEC], out_specs=VMEM_SPEC,
                          out_shape=jax.ShapeDtypeStruct(own.shape, F32),
                          compiler_params=pltpu.CompilerParams(vmem_limit_bytes=_VMEM_LIMIT))(own, land)


OTHER_CHIPS = ((1, 0), (0, 1), (1, 1))


SIBLING_COLLECTIVE_ID = 6


def _sibling_handshake():
    x, y, c = _place()
    barrier = pltpu.get_barrier_semaphore()
    pl.semaphore_signal(barrier, inc=1, device_id=(x, y, 1 - c), device_id_type=MESH)
    pl.semaphore_wait(barrier, 1)


def _sibling_swap(name, src, halves):
    half = src.shape[-2] // 2
    out_shape = (src.shape[0], half, 1024) if halves else src.shape

    def body(s_ref, o_ref, send_sem, recv_sem):
        x, y, c = _place()
        _sibling_handshake()
        part = s_ref.at[:, pl.ds(pl.multiple_of((1 - c) * half, 8), half)] if halves else s_ref
        cp = pltpu.make_async_remote_copy(src_ref=part, dst_ref=o_ref, send_sem=send_sem, recv_sem=recv_sem,
                                          device_id=(x, y, 1 - c), device_id_type=MESH)
        cp.start()
        cp.wait()

    return pl.pallas_call(
        body, name=name, in_specs=[ANY], out_specs=ANY, out_shape=jax.ShapeDtypeStruct(out_shape, src.dtype),
        scratch_shapes=[pltpu.SemaphoreType.DMA, pltpu.SemaphoreType.DMA],
        compiler_params=pltpu.CompilerParams(collective_id=SIBLING_COLLECTIVE_ID))(src)


HBM = pl.BlockSpec(memory_space=pltpu.HBM)
SEM = pl.BlockSpec(memory_space=pltpu.SEMAPHORE)


def _exchange_peers(mode):
    x, y, c = _place()
    if mode == "all":
        return [(1 - x if k & 4 else x, 1 - y if k & 2 else y, 1 - c if k & 1 else c) for k in range(1, 8)]
    return [(1 - x if fx else x, 1 - y if fy else y, c) for fx, fy in OTHER_CHIPS]


def _chip_copies(mode, src_ref, land_ref, send_sems, recv_sems):
    x, y, c = _place()
    k = 2 * x + y
    copies = []
    for j, peer in enumerate(_exchange_peers(mode)):
        if mode == "gather":
            half = src_ref.shape[0] // 2
            mine = pl.ds(pl.multiple_of(c * half, 16), half)
            src, dst = src_ref.at[mine], land_ref.at[k, mine]
        elif mode == "scatter":
            src, dst = src_ref.at[2 * peer[0] + peer[1]], land_ref.at[k]
        else:
            src, dst = src_ref, land_ref.at[4 * x + 2 * y + c]
        copies.append(pltpu.make_async_remote_copy(src_ref=src, dst_ref=dst, send_sem=send_sems.at[j],
                                                   recv_sem=recv_sems.at[j], device_id=peer, device_id_type=MESH))
    return copies


def _exchange_start(name, collective_id, mode, src, land, after=None):
    deps = [] if after is None else [after]
    npeers = 7 if mode == "all" else 3

    def body(s_ref, l_ref, *rest):
        send_sems, recv_sems, s_thru, l_thru, token = rest[len(deps):]
        barrier = pltpu.get_barrier_semaphore()
        for peer in _exchange_peers(mode):
            pl.semaphore_signal(barrier, inc=1, device_id=peer, device_id_type=MESH)
        pl.semaphore_wait(barrier, npeers)
        for cp in _chip_copies(mode, s_ref, l_ref, send_sems, recv_sems):
            cp.start()
        token[...] = jnp.zeros_like(token)

    return pl.pallas_call(
        body, name=name,
        out_shape=(pltpu.SemaphoreType.DMA((npeers,)), pltpu.SemaphoreType.DMA((npeers,)),
                   pltpu.HBM(src.shape, src.dtype),
                   pltpu.HBM(land.shape, land.dtype), jax.ShapeDtypeStruct((8, 128), F32)),
        in_specs=(HBM, HBM) + tuple(ANY for _ in deps), out_specs=(SEM, SEM, HBM, HBM, VMEM_SPEC),
        input_output_aliases={0: 2, 1: 3},
        compiler_params=pltpu.CompilerParams(has_side_effects=pltpu.SideEffectType.DATAFLOW_SIDE_EFFECTING,
                                             collective_id=collective_id))(
            pltpu.with_memory_space_constraint(src, pltpu.HBM), pltpu.with_memory_space_constraint(land, pltpu.HBM),
            *deps)


def _exchange_wait(name, mode, started, after):
    send_sems, recv_sems, s_thru, l_thru, _ = started
    deps = list(after) if isinstance(after, (tuple, list)) else [after]

    def body(s_ref, l_ref, send_sems, recv_sems, *rest):
        for cp in _chip_copies(mode, s_ref, l_ref, send_sems, recv_sems):
            cp.wait_send()
            cp.wait_recv()

    return pl.pallas_call(
        body, name=name, out_shape=(pltpu.HBM(s_thru.shape, s_thru.dtype), pltpu.HBM(l_thru.shape, l_thru.dtype)),
        in_specs=(HBM, HBM, SEM, SEM) + tuple(ANY for _ in deps), out_specs=(HBM, HBM),
        input_output_aliases={0: 0, 1: 1},
        compiler_params=pltpu.CompilerParams(has_side_effects=pltpu.SideEffectType.DATAFLOW_SIDE_EFFECTING))(
            s_thru, l_thru, send_sems, recv_sems, *deps)


def _allgather_finish(tag, land):
    half = land.shape[1] // 2

    def body(l_ref, o_ref, send_sem, recv_sem):
        x, y, c = _place()
        _sibling_handshake()
        mine = pl.ds(pl.multiple_of(c * half, 16), half)
        swap = pltpu.make_async_remote_copy(src_ref=o_ref.at[:, mine], dst_ref=o_ref.at[:, mine], send_sem=send_sem,
                                            recv_sem=recv_sem, device_id=(x, y, 1 - c), device_id_type=MESH)
        swap.start()
        swap.wait()

    return pl.pallas_call(
        body, name="allgather_finish_" + tag, in_specs=[ANY], out_specs=ANY, input_output_aliases={0: 0},
        out_shape=jax.ShapeDtypeStruct(land.shape, land.dtype),
        scratch_shapes=[pltpu.SemaphoreType.DMA, pltpu.SemaphoreType.DMA],
        compiler_params=pltpu.CompilerParams(collective_id=SIBLING_COLLECTIVE_ID))(land)


def _pair_sum(tag, g, r1, c):
    rows = g.shape[1]
    half = rows // 2
    th = _row_tile(half, 1408, 16)
    nblk = half // th

    def body(c_ref, g_ref, r_ref, o_ref, o2_ref):
        o_ref[...] = (g_ref[...].astype(F32) + r_ref[...].astype(F32)).astype(o_ref.dtype)
        o2_ref[...] = o_ref[...]

    spec = pl.BlockSpec((None, th, 1024), lambda k, i, c_ref: (k, i, 0))
    grid_spec = pltpu.PrefetchScalarGridSpec(
        num_scalar_prefetch=1, grid=(4, nblk),
        in_specs=[pl.BlockSpec((None, th, 1024), lambda k, i, c_ref: (k, c_ref[0] * nblk + i, 0)), spec],
        out_specs=[spec, spec])
    return pl.pallas_call(body, name="grad_pair_sum_" + tag, grid_spec=grid_spec,
                          out_shape=[jax.ShapeDtypeStruct((4, half, 1024), BF16)] * 2,
                          compiler_params=_params(("parallel", "parallel")))(c, g, r1)


def _chip_sum(tag, q, after=None):
    half = q.shape[1]
    th = _row_tile(half, 704, 16)
    deps = [] if after is None else [after]

    def body(a, b, c, d, *rest):
        rest[-1][...] = ((a[...].astype(F32) + b[...].astype(F32)) + c[...].astype(F32)) + d[...].astype(F32)

    specs = [pl.BlockSpec((None, th, 1024), functools.partial(lambda i, k: (k, i, 0), k=k)) for k in range(4)]
    return pl.pallas_call(body, name="grad_chip_sum_" + tag, grid=(half // th,), in_specs=specs + [ANY for _ in deps],
                          out_specs=pl.BlockSpec((th, 1024), lambda i: (i, 0)),
                          out_shape=jax.ShapeDtypeStruct((half, 1024), F32),
                          compiler_params=_params(("parallel",)))(q, q, q, q, *deps)


def _join_halves(tag, f, r, c):
    half = f.shape[0]
    th = _row_tile(half, 704)
    nblk = half // th

    def body(c_ref, f_ref, r_ref, o_ref):
        mine = (pl.program_id(0) == c_ref[0])
        o_ref[...] = jnp.where(mine, f_ref[...], r_ref[...])

    spec = pl.BlockSpec((th, 1024), lambda h, i, c_ref: (i, 0))
    grid_spec = pltpu.PrefetchScalarGridSpec(
        num_scalar_prefetch=1, grid=(2, nblk), in_specs=[spec, spec],
        out_specs=pl.BlockSpec((th, 1024), lambda h, i, c_ref: (h * nblk + i, 0)))
    return pl.pallas_call(body, name="grad_join_halves_" + tag, grid_spec=grid_spec,
                          out_shape=jax.ShapeDtypeStruct((2 * half, 1024), F32),
                          compiler_params=_params(("parallel", "parallel")))(c, f, r)


BIG_ARGS = ("in_w_even", "out_w_even", "qkv_w", "o_w", "ffn_gate_w", "ffn_up_w", "ffn_down_w")
def _ffn_pieces(layer):
    return tuple((f"{n}{layer}", 704, 704) for n in ("gate_wt", "up_wt", "down_w"))


IN_SLAB = (("in_wt", 1156, 1184),)
LAYER0_REST_SLAB = (("out_w", 512, 512),) + _ffn_pieces(0)
LAYER1_SLAB = (("qkv_wt", 320, 320), ("o_w", 256, 256)) + _ffn_pieces(1)
FFN0_SLAB = _ffn_pieces(0)
MIXER0_SLAB = (("in_wt", 1156, 1280), ("out_w", 512, 512))


def _slab(pieces, spec):
    parts = []
    for name, rows, room in spec:
        p = pieces[name]
        parts.append(jnp.pad(p, [(0, 0)] * (p.ndim - 2) + [(0, room - rows), (0, 0)]) if room > rows else p)
    return jnp.concatenate(parts, axis=-2) if len(parts) > 1 else parts[0]


def _unslab(slab, spec):
    out, off = {}, 0
    for name, rows, room in spec:
        out[name] = slab[..., off:off + rows, :]
        off += room
    return out


def _share_pieces(w):
    return {"in_wt": w["in_w_even"][0].T, "out_w": w["out_w_even"][0], "qkv_wt": w["qkv_w"][0].T, "o_w": w["o_w"][0],
            "gate_wt0": w["ffn_gate_w"][0].T, "gate_wt1": w["ffn_gate_w"][1].T,
            "up_wt0": w["ffn_up_w"][0].T, "up_wt1": w["ffn_up_w"][1].T,
            "down_w0": w["ffn_down_w"][0], "down_w1": w["ffn_down_w"][1]}


def _pieces_to_shares(p):
    return {"in_w_even": p["in_wt"].T[None], "out_w_even": p["out_w"][None], "qkv_w": p["qkv_wt"].T[None],
            "o_w": p["o_w"][None], "ffn_gate_w": jnp.stack([p["gate_wt0"].T, p["gate_wt1"].T]),
            "ffn_up_w": jnp.stack([p["up_wt0"].T, p["up_wt1"].T]),
            "ffn_down_w": jnp.stack([p["down_w0"], p["down_w1"]])}


def _chips_from_full(G, spec):
    return _slab({k: v.reshape(4, -1, D) for k, v in G.items()}, spec)


def _pack_small(parts):
    padded = []
    for p in parts:
        p = p.reshape(-1).astype(F32)
        padded.append(jnp.pad(p, (0, (-p.shape[0]) % 1024)))
    return jnp.concatenate(padded).reshape(-1, 128)


def _unpack_small(slab, shapes):
    flat, out, off = slab.reshape(-1), [], 0
    for shp in shapes:
        size = math.prod(shp)
        out.append(flat[off:off + size].reshape(shp))
        off += size + (-size) % 1024
    return out


SMALL = ("ada_b", "norm_mix_w", "norm_ffn_w", "conv_w", "conv_b", "dt_bias", "a_log", "d_skip", "ssm_norm_w",
         "gmlp_ln_w", "gmlp_ln_b", "gmlp_ws", "gmlp_bs", "qkv_b", "o_b", "sinks", "rel_table", "final_norm_w")
SMALL_SPLIT = {"conv_w": 1536, "qkv_b": 1280, "o_b": 1024}
WEIGHTS = ("ada_w", "ada_b", "norm_mix_w", "norm_ffn_w", "in_w_even", "conv_w", "conv_b", "dt_bias", "a_log", "d_skip",
           "ssm_norm_w", "gmlp_ln_w", "gmlp_ln_b", "gmlp_ws", "gmlp_bs", "out_w_even", "qkv_w", "qkv_b", "o_w", "o_b",
           "sinks", "rel_table", "ffn_gate_w", "ffn_up_w", "ffn_down_w", "final_norm_w")


def kernel(x, c, ada_w, ada_b, norm_mix_w, norm_ffn_w, in_w_even, conv_w, conv_b, dt_bias, a_log, d_skip, ssm_norm_w, gmlp_ln_w, gmlp_ln_b, gmlp_ws, gmlp_bs, out_w_even, qkv_w, qkv_b, o_w, o_b, sinks, rel_table, ffn_gate_w, ffn_up_w, ffn_down_w, final_norm_w, loss_target, m_ada_w, m_ada_b, m_norm_mix_w, m_norm_ffn_w, m_in_w_even, m_conv_w, m_conv_b, m_dt_bias, m_a_log, m_d_skip, m_ssm_norm_w, m_gmlp_ln_w, m_gmlp_ln_b, m_gmlp_ws, m_gmlp_bs, m_out_w_even, m_qkv_w, m_qkv_b, m_o_w, m_o_b, m_sinks, m_rel_table, m_ffn_gate_w, m_ffn_up_w, m_ffn_down_w, m_final_norm_w, v_ada_w, v_ada_b, v_norm_mix_w, v_norm_ffn_w, v_in_w_even, v_conv_w, v_conv_b, v_dt_bias, v_a_log, v_d_skip, v_ssm_norm_w, v_gmlp_ln_w, v_gmlp_ln_b, v_gmlp_ws, v_gmlp_bs, v_out_w_even, v_qkv_w, v_qkv_b, v_o_w, v_o_b, v_sinks, v_rel_table, v_ffn_gate_w, v_ffn_up_w, v_ffn_down_w, v_final_norm_w):
    args = dict(locals())
    w = {n: args[n] for n in WEIGHTS}
    m = {n: args["m_" + n] for n in WEIGHTS}
    v = {n: args["v_" + n] for n in WEIGHTS}
    ax, ay, ac = _place()
    me = 4 * ax + 2 * ay + ac
    chip = 2 * ax + ay
    south = (ac == 0).astype(F32)
    c_arr = jnp.reshape(ac, (1,)).astype(jnp.int32)

    c_all = _allreduce_small("gather_cond", lax.dynamic_update_slice(jnp.zeros((8, D), F32), c, (me, 0)).reshape(64, 128))
    c_all = c_all.reshape(8, D)
    n_ada = ada_w.shape[2]
    mod_cols = _ada_fwd(c_all, ada_w, lax.dynamic_slice(ada_b, (0, chip * n_ada), (2, n_ada)).reshape(2, 1, n_ada))
    pieces = [lax.dynamic_update_slice(jnp.zeros((2, 8, 6 * D), F32), mod_cols, (0, 0, chip * n_ada))]
    split_names = list(SMALL_SPLIT)
    for n in split_names:
        full = SMALL_SPLIT[n]
        local = w[n]
        idx = (0,) * (local.ndim - 1) + (chip * local.shape[-1],)
        pieces.append(lax.dynamic_update_slice(jnp.zeros(local.shape[:-1] + (full,), F32), local, idx))
    shapes = [p.shape for p in pieces]
    mod_own = _pack_small(pieces) * south
    mod_started = _exchange_start("gather_mod_start", 9, "all", mod_own, lax.empty((8,) + mod_own.shape, F32))

    pieces = _share_pieces(w)
    cast = {"in_wt": pieces["in_wt"].astype(_MXU)}

    def start_gather(tag, collective_id, share, after):
        return _exchange_start("allgather_start_" + tag, collective_id, "gather", share,
                               lax.empty((4,) + share.shape, share.dtype), after=after)

    def finish_gather(tag, started, spec, after):
        land = _exchange_wait("allgather_wait_" + tag, "gather", started, after)[1]
        out = {}
        for name, piece in _unslab(_allgather_finish(tag, land), spec).items():
            out[name] = lax.dynamic_update_slice(piece.reshape(-1, D), cast[name], (chip * piece.shape[1], 0))
        return out

    gather_in = start_gather("in", 7, _slab(cast, IN_SLAB), mod_started[4])
    zero = gather_in[4][0, 0]
    cast.update({k: (p + zero).astype(_MXU) for k, p in pieces.items() if k != "in_wt"})
    share0, share1 = _slab(cast, LAYER0_REST_SLAB), _slab(cast, LAYER1_SLAB)
    mod_own, mod_land = _exchange_wait("gather_mod_wait", "all", mod_started, (share0, share1))
    mod_slab = _sum_slots("gather_mod_sum", mod_own, mod_land)
    gathered = _unpack_small(mod_slab, shapes)
    mod = lax.dynamic_slice(gathered[0], (0, me, 0), (2, 1, 6 * D)).reshape(2, 6 * D)
    P = {n: w[n] for n in SMALL if n not in SMALL_SPLIT and n != "ada_b"}
    for n, full in zip(split_names, gathered[1:]):
        P[n] = full
    P["final_norm_w"] = final_norm_w.reshape(1, D)
    w_in = finish_gather("in", gather_in, IN_SLAB, mod_slab)["in_wt"]
    gather0 = start_gather("0", 1, share0, w_in)
    gather1 = start_gather("1", 2, share1, gather0[4])

    def start_reduce(tag, collective_id, G, spec, after=None):
        gp = _chips_from_full(G, spec).astype(BF16)
        p, q = _pair_sum(tag, gp, _sibling_swap("grad_pair_exchange_" + tag, gp, True), c_arr)
        return _exchange_start("grad_exchange_start_" + tag, collective_id, "scatter", p, q, after=after)

    def finish_reduce(tag, started, spec, after, behind=None):
        q = _exchange_wait("grad_exchange_wait_" + tag, "scatter", started, after)[1]
        fin = _chip_sum(tag, q, after=behind)
        total = _join_halves(tag, fin, _sibling_swap("grad_final_exchange_" + tag, fin, False), c_arr)
        return _unslab(total, spec)

    reduces = {}

    def grads1(G1):
        reduces["1"] = start_reduce("1", 3, G1, LAYER1_SLAB)
        return reduces["1"][4]

    def grads_ffn0(G):
        reduces["f"] = start_reduce("f", 4, G, FFN0_SLAB)
        return reduces["f"][4]

    io = {"start": gather1[4],
          "weights0": lambda after: finish_gather("0", gather0, LAYER0_REST_SLAB, after),
          "weights1": lambda after: finish_gather("1", gather1, LAYER1_SLAB, after),
          "grads1": grads1, "grads_ffn0": grads_ffn0}
    sq, grad_x, dmod, G0, g = _local_step(x[0], loss_target[0], mod, w_in, P, io)
    loss = lax.psum(0.5 * sq[0, 0] / D, ("x", "y", "c"))

    g["final_norm_w"] = g["final_norm_w"].reshape(D)
    small_names = [n for n in SMALL if n != "ada_b"]
    pieces = [lax.dynamic_update_slice(jnp.zeros((2, 8, 6 * D), F32), dmod.reshape(2, 1, 6 * D), (0, me, 0))]
    pieces += [g[n] for n in small_names]
    shapes = [p.shape for p in pieces]
    small_own = _pack_small(pieces)
    small_started = _exchange_start("small_grads_start", 8, "all", small_own, lax.empty((8,) + small_own.shape, F32))
    reduces["m"] = start_reduce("m", 5, G0, MIXER0_SLAB, after=small_started[4])
    shares = finish_reduce("1", reduces["1"], LAYER1_SLAB, grad_x, behind=reduces["m"][4])
    shares.update(finish_reduce("f", reduces["f"], FFN0_SLAB, grad_x, behind=reduces["m"][4]))
    small_own, small_land = _exchange_wait("small_grads_wait", "all", small_started, shares["down_w0"])
    reduced = _unpack_small(_sum_slots("small_grads_sum", small_own, small_land), shapes)
    dmod_all = reduced[0]
    grads = dict(zip(small_names, reduced[1:]))
    for n in split_names:
        full = grads[n]
        size = w[n].shape[-1]
        grads[n] = lax.dynamic_slice(full, (0,) * (full.ndim - 1) + (chip * size,), full.shape[:-1] + (size,))
    grads = {n: grads[n].reshape(w[n].shape) for n in small_names}
    dw_ada, db_ada = _ada_bwd(c_all, lax.dynamic_slice(dmod_all, (0, 0, chip * n_ada), (2, 8, n_ada)), dmod_all)
    grads["ada_w"], grads["ada_b"] = dw_ada, db_ada.reshape(2, 6 * D)

    delta, new_m, new_v = {}, {}, {}

    def update(n):
        cols = w[n].shape[-1]
        d_, m_, v_ = _adamw("adamw_" + n, w[n].reshape(-1, cols), grads[n].reshape(-1, cols), m[n].reshape(-1, cols),
                            v[n].reshape(-1, cols))
        delta[n], new_m[n], new_v[n] = d_.reshape(w[n].shape), m_.reshape(w[n].shape), v_.reshape(w[n].shape)

    update("ada_w")
    shapes = [w[n].shape for n in SMALL]
    packed = [_pack_small([t[n] for n in SMALL]) for t in (w, grads, m, v)]
    outs = _adamw("adamw_small", *packed)
    for dst, slab in zip((delta, new_m, new_v), outs):
        for n, t in zip(SMALL, _unpack_small(slab, shapes)):
            dst[n] = t
    shares.update(finish_reduce("m", reduces["m"], MIXER0_SLAB, outs[0]))
    grads.update(_pieces_to_shares(shares))
    for n in BIG_ARGS:
        update(n)
    return (loss, grad_x[None], *[grads[n] for n in WEIGHTS], *[delta[n] for n in WEIGHTS],
            *[new_m[n] for n in WEIGHTS], *[new_v[n] for n in WEIGHTS])
```
